```python
import math
import jax, jax.numpy as jnp
from jax import lax
import numpy as np

D_MODEL = 1024
BATCH = 8
SEQ = 2048
DEPTH = 4

CHUNK = 64
Q_BLOCK = 128
N_EVEN = (DEPTH + 1) // 2
N_ODD = DEPTH // 2

POOL_WINDOWS = (2, 4, 8, 16)
POOL_GROUPS = len(POOL_WINDOWS)
POOL_GROUP_WIDTH = D_MODEL // 8
POOL_WIDTH = POOL_GROUPS * POOL_GROUP_WIDTH
LRU_WIDTH = D_MODEL
LRU_HEADS = 8
LRU_HEAD_DIM = LRU_WIDTH // LRU_HEADS
CONV_WIDTH = 4
LRU_C = 8.0
EVEN_IN_WIDTH = POOL_WIDTH + 2 * LRU_WIDTH
EVEN_MIX_WIDTH = POOL_WIDTH + LRU_WIDTH
MLA_HEADS = 8
QK_NOPE_DIM = 128
QK_ROPE_DIM = 64
V_HEAD_DIM = 128
Q_LORA_RANK = 384
KV_LORA_RANK = 256
ODD_IN_WIDTH = Q_LORA_RANK + KV_LORA_RANK + QK_ROPE_DIM
ROPE_THETA = 10000.0
D_FF = 4 * D_MODEL
DEEPNORM_ALPHA = (2 * DEPTH) ** 0.25
DEEPNORM_BETA = (8 * DEPTH) ** -0.25
LN_EPS = 1e-5
RMS_EPS = 1e-6

kernel_name = "pool_rglru_mla_deepnorm_hybrid"


def layer_norm(x, g, b):
    xf = x.astype(jnp.float32)
    mu = jnp.mean(xf, axis=-1, keepdims=True)
    var = jnp.mean(jnp.square(xf - mu), axis=-1, keepdims=True)
    y = (xf - mu) * lax.rsqrt(var + LN_EPS) * g.astype(jnp.float32) + b.astype(jnp.float32)
    return y.astype(x.dtype)


def rms_norm(x, g):
    xf = x.astype(jnp.float32)
    y = xf * lax.rsqrt(jnp.mean(jnp.square(xf), axis=-1, keepdims=True) + RMS_EPS) * g.astype(jnp.float32)
    return y.astype(x.dtype)


def rope_tables(positions):
    inv_freq = ROPE_THETA ** (-jnp.arange(0, QK_ROPE_DIM, 2, dtype=jnp.float32) / QK_ROPE_DIM)
    ang = positions.astype(jnp.float32)[..., None] * inv_freq
    return jnp.cos(ang), jnp.sin(ang)


def apply_rope(x, cos, sin):
    xf = x.astype(jnp.float32)
    x1, x2 = jnp.split(xf, 2, axis=-1)
    return jnp.concatenate([x1 * cos - x2 * sin, x1 * sin + x2 * cos], axis=-1).astype(x.dtype)


def multiscale_pool(u, w_pool, pool_scale):
    b, s, _ = u.shape
    ug = u.astype(jnp.float32).reshape(b, s, POOL_GROUPS, POOL_GROUP_WIDTH)
    csum = jnp.cumsum(ug, axis=1)
    t = jnp.arange(s)
    diffs = []
    for g, w in enumerate(POOL_WINDOWS):
        cg = csum[:, :, g]
        lagged = jnp.pad(cg, ((0, 0), (w, 0), (0, 0)))[:, :s]
        count = jnp.minimum(t + 1, w).astype(jnp.float32)[None, :, None]
        diffs.append((cg - lagged) / count - ug[:, :, g])
    d = jnp.stack(diffs, axis=2)
    y = jnp.einsum('bsgc,gcd->bsgd', d, w_pool.astype(jnp.float32))
    y = y.reshape(b, s, POOL_WIDTH) * pool_scale.astype(jnp.float32)
    return y.astype(u.dtype)


def causal_depthwise_conv(u, w, bias):
    c = u.shape[-1]
    y = lax.conv_general_dilated(u, w[:, None, :].astype(u.dtype), window_strides=(1,),
                                 padding=[(CONV_WIDTH - 1, 0)],
                                 dimension_numbers=('NWC', 'WIO', 'NWC'),
                                 feature_group_count=c)
    return y + bias.astype(u.dtype)


def rg_lru(u, w_a, b_a, w_x, b_x, lam):
    b, s, _ = u.shape
    uf = u.astype(jnp.float32)
    uh = uf.reshape(b, s, LRU_HEADS, LRU_HEAD_DIM)
    r = jax.nn.sigmoid(jnp.einsum('bshc,hcd->bshd', uh, w_a.astype(jnp.float32)).reshape(b, s, LRU_WIDTH)
                       + b_a.astype(jnp.float32))
    i = jax.nn.sigmoid(jnp.einsum('bshc,hcd->bshd', uh, w_x.astype(jnp.float32)).reshape(b, s, LRU_WIDTH)
                       + b_x.astype(jnp.float32))
    log_a = -LRU_C * r * jax.nn.softplus(-lam.astype(jnp.float32))
    a = jnp.exp(log_a)
    mult = jnp.sqrt(-jnp.expm1(2.0 * log_a))
    xin = mult * (i * uf)

    def combine(lhs, rhs):
        a1, b1 = lhs
        a2, b2 = rhs
        return a1 * a2, a2 * b1 + b2

    _, h = lax.associative_scan(combine, (a, xin), axis=1)
    return h


def pool_lru_mixer(x, w_in, w_pool, pool_scale, conv_w, conv_b, w_a, b_a, w_x, b_x, lam, w_out):
    proj = x @ w_in
    u_pool = proj[..., :POOL_WIDTH]
    u_lru = proj[..., POOL_WIDTH:POOL_WIDTH + LRU_WIDTH]
    u_gate = proj[..., POOL_WIDTH + LRU_WIDTH:]
    y_pool = multiscale_pool(u_pool, w_pool, pool_scale)
    h = rg_lru(causal_depthwise_conv(u_lru, conv_w, conv_b), w_a, b_a, w_x, b_x, lam)
    y_lru = (h * jax.nn.gelu(u_gate.astype(jnp.float32))).astype(x.dtype)
    return jnp.concatenate([y_pool, y_lru], axis=-1) @ w_out


def mla_mixer(x, cos, sin, w_down, q_norm_g, kv_norm_g, w_qb, w_kvb, w_o):
    b, s, _ = x.shape
    down = x @ w_down
    cq = rms_norm(down[..., :Q_LORA_RANK], q_norm_g)
    ckv = rms_norm(down[..., Q_LORA_RANK:Q_LORA_RANK + KV_LORA_RANK], kv_norm_g)
    k_pe = apply_rope(down[..., Q_LORA_RANK + KV_LORA_RANK:], cos, sin)
    q = (cq @ w_qb).reshape(b, s, MLA_HEADS, QK_NOPE_DIM + QK_ROPE_DIM)
    q_nope = q[..., :QK_NOPE_DIM]
    q_pe = apply_rope(q[..., QK_NOPE_DIM:], cos[:, :, None, :], sin[:, :, None, :])
    kv = (ckv @ w_kvb).reshape(b, s, MLA_HEADS, QK_NOPE_DIM + V_HEAD_DIM)
    k_nope = kv[..., :QK_NOPE_DIM]
    v = kv[..., QK_NOPE_DIM:]
    scale = (QK_NOPE_DIM + QK_ROPE_DIM) ** -0.5
    chunk_id = jnp.arange(s) // CHUNK
    neg = jnp.finfo(jnp.float32).min
    outs = []
    for qs in range(0, s, Q_BLOCK):
        ke = qs + Q_BLOCK
        sc = (jnp.einsum('bqhd,bkhd->bhqk', q_nope[:, qs:ke], k_nope[:, :ke],
                         preferred_element_type=jnp.float32)
              + jnp.einsum('bqhr,bkr->bhqk', q_pe[:, qs:ke], k_pe[:, :ke],
                           preferred_element_type=jnp.float32)) * scale
        mask = chunk_id[:ke][None, :] <= chunk_id[qs:ke][:, None]
        p = jax.nn.softmax(jnp.where(mask, sc, neg), axis=-1).astype(v.dtype)
        outs.append(jnp.einsum('bhqk,bkhv->bqhv', p, v[:, :ke]))
    o = jnp.concatenate(outs, axis=1).reshape(b, s, MLA_HEADS * V_HEAD_DIM)
    return o @ w_o


def squared_relu_mlp(x, w1, w2):
    return jnp.square(jax.nn.relu(x @ w1)) @ w2


def _fwd_setup_inputs(seed: int = 0) -> dict:
    key = jax.random.key(seed)
    ks = jax.random.split(key, 28)
    f32 = jnp.float32
    nrm = lambda k, shp, sc: jax.random.normal(k, shp, f32) * sc
    x = jax.random.normal(ks[0], (BATCH, SEQ, D_MODEL), f32)
    offset = jax.random.randint(ks[1], (BATCH, 1), 0, 4096, dtype=jnp.int32)
    positions = (offset + jnp.arange(SEQ, dtype=jnp.int32)[None, :]).astype(jnp.int32)
    ln_mix_g = 1.0 + nrm(ks[2], (DEPTH, D_MODEL), 0.1)
    ln_mix_b = nrm(ks[3], (DEPTH, D_MODEL), 0.02)
    ln_ffn_g = 1.0 + nrm(ks[4], (DEPTH, D_MODEL), 0.1)
    ln_ffn_b = nrm(ks[5], (DEPTH, D_MODEL), 0.02)
    even_w_in = nrm(ks[6], (N_EVEN, D_MODEL, EVEN_IN_WIDTH), D_MODEL ** -0.5)
    pool_w = nrm(ks[7], (N_EVEN, POOL_GROUPS, POOL_GROUP_WIDTH, POOL_GROUP_WIDTH), POOL_GROUP_WIDTH ** -0.5)
    pool_scale = 1.0 + nrm(ks[8], (N_EVEN, POOL_WIDTH), 0.1)
    lru_conv_w = nrm(ks[9], (N_EVEN, CONV_WIDTH, LRU_WIDTH), CONV_WIDTH ** -0.5)
    lru_conv_b = nrm(ks[10], (N_EVEN, LRU_WIDTH), 0.02)
    lru_w_a = nrm(ks[11], (N_EVEN, LRU_HEADS, LRU_HEAD_DIM, LRU_HEAD_DIM), LRU_HEAD_DIM ** -0.5)
    lru_b_a = nrm(ks[12], (N_EVEN, LRU_WIDTH), 0.02)
    lru_w_x = nrm(ks[13], (N_EVEN, LRU_HEADS, LRU_HEAD_DIM, LRU_HEAD_DIM), LRU_HEAD_DIM ** -0.5)
    lru_b_x = nrm(ks[14], (N_EVEN, LRU_WIDTH), 0.02)
    a_c = jax.random.uniform(ks[15], (N_EVEN, LRU_WIDTH), f32, 0.9, 0.999)
    s_a = a_c ** (1.0 / LRU_C)
    lru_lambda = jnp.log(s_a) - jnp.log1p(-s_a)
    even_w_out = nrm(ks[16], (N_EVEN, EVEN_MIX_WIDTH, D_MODEL), EVEN_MIX_WIDTH ** -0.5 * DEEPNORM_BETA)
    mla_w_down = nrm(ks[17], (N_ODD, D_MODEL, ODD_IN_WIDTH), D_MODEL ** -0.5)
    mla_q_norm_g = 1.0 + nrm(ks[18], (N_ODD, Q_LORA_RANK), 0.1)
    mla_kv_norm_g = 1.0 + nrm(ks[19], (N_ODD, KV_LORA_RANK), 0.1)
    mla_w_qb = nrm(ks[20], (N_ODD, Q_LORA_RANK, MLA_HEADS * (QK_NOPE_DIM + QK_ROPE_DIM)), Q_LORA_RANK ** -0.5)
    mla_w_kvb = nrm(ks[21], (N_ODD, KV_LORA_RANK, MLA_HEADS * (QK_NOPE_DIM + V_HEAD_DIM)), KV_LORA_RANK ** -0.5)
    mla_w_o = nrm(ks[22], (N_ODD, MLA_HEADS * V_HEAD_DIM, D_MODEL), (MLA_HEADS * V_HEAD_DIM) ** -0.5 * DEEPNORM_BETA)
    mlp_w1 = nrm(ks[23], (DEPTH, D_MODEL, D_FF), D_MODEL ** -0.5)
    mlp_w2 = nrm(ks[24], (DEPTH, D_FF, D_MODEL), D_FF ** -0.5 * DEEPNORM_BETA)
    return {"x": x, "positions": positions,
            "ln_mix_g": ln_mix_g, "ln_mix_b": ln_mix_b, "ln_ffn_g": ln_ffn_g, "ln_ffn_b": ln_ffn_b,
            "even_w_in": even_w_in, "pool_w": pool_w, "pool_scale": pool_scale,
            "lru_conv_w": lru_conv_w, "lru_conv_b": lru_conv_b,
            "lru_w_a": lru_w_a, "lru_b_a": lru_b_a, "lru_w_x": lru_w_x, "lru_b_x": lru_b_x,
            "lru_lambda": lru_lambda, "even_w_out": even_w_out,
            "mla_w_down": mla_w_down, "mla_q_norm_g": mla_q_norm_g, "mla_kv_norm_g": mla_kv_norm_g,
            "mla_w_qb": mla_w_qb, "mla_w_kvb": mla_w_kvb, "mla_w_o": mla_w_o,
            "mlp_w1": mlp_w1, "mlp_w2": mlp_w2}


def _fwd_reference(x, positions, ln_mix_g, ln_mix_b, ln_ffn_g, ln_ffn_b,
              even_w_in, pool_w, pool_scale, lru_conv_w, lru_conv_b,
              lru_w_a, lru_b_a, lru_w_x, lru_b_x, lru_lambda, even_w_out,
              mla_w_down, mla_q_norm_g, mla_kv_norm_g, mla_w_qb, mla_w_kvb, mla_w_o,
              mlp_w1, mlp_w2):
    cos, sin = rope_tables(positions)
    for layer in range(DEPTH):
        j = layer // 2
        if layer % 2 == 0:
            mix = pool_lru_mixer(x, even_w_in[j], pool_w[j], pool_scale[j],
                                 lru_conv_w[j], lru_conv_b[j], lru_w_a[j], lru_b_a[j],
                                 lru_w_x[j], lru_b_x[j], lru_lambda[j], even_w_out[j])
        else:
            mix = mla_mixer(x, cos, sin, mla_w_down[j], mla_q_norm_g[j], mla_kv_norm_g[j],
                            mla_w_qb[j], mla_w_kvb[j], mla_w_o[j])
        x = layer_norm(DEEPNORM_ALPHA * x + mix, ln_mix_g[layer], ln_mix_b[layer])
        x = layer_norm(DEEPNORM_ALPHA * x + squared_relu_mlp(x, mlp_w1[layer], mlp_w2[layer]),
                       ln_ffn_g[layer], ln_ffn_b[layer])
    return x


import jax as _jax
import jax.numpy as _jnp

TWIN_FORMAT = 'train_step'
FWD_PARAMS = ['x', 'positions', 'ln_mix_g', 'ln_mix_b', 'ln_ffn_g', 'ln_ffn_b', 'even_w_in', 'pool_w', 'pool_scale', 'lru_conv_w', 'lru_conv_b', 'lru_w_a', 'lru_b_a', 'lru_w_x', 'lru_b_x', 'lru_lambda', 'even_w_out', 'mla_w_down', 'mla_q_norm_g', 'mla_kv_norm_g', 'mla_w_qb', 'mla_w_kvb', 'mla_w_o', 'mlp_w1', 'mlp_w2']
TWIN_WEIGHTS = ['ln_mix_g', 'ln_mix_b', 'ln_ffn_g', 'ln_ffn_b', 'even_w_in', 'pool_w', 'pool_scale', 'lru_conv_w', 'lru_conv_b', 'lru_w_a', 'lru_b_a', 'lru_w_x', 'lru_b_x', 'lru_lambda', 'even_w_out', 'mla_w_down', 'mla_q_norm_g', 'mla_kv_norm_g', 'mla_w_qb', 'mla_w_kvb', 'mla_w_o', 'mlp_w1', 'mlp_w2']
TWIN_DIFF_INPUT = 'x'
TWIN_INPUTS = ['x', 'positions', 'ln_mix_g', 'ln_mix_b', 'ln_ffn_g', 'ln_ffn_b', 'even_w_in', 'pool_w', 'pool_scale', 'lru_conv_w', 'lru_conv_b', 'lru_w_a', 'lru_b_a', 'lru_w_x', 'lru_b_x', 'lru_lambda', 'even_w_out', 'mla_w_down', 'mla_q_norm_g', 'mla_kv_norm_g', 'mla_w_qb', 'mla_w_kvb', 'mla_w_o', 'mlp_w1', 'mlp_w2', 'loss_target', 'm_ln_mix_g', 'm_ln_mix_b', 'm_ln_ffn_g', 'm_ln_ffn_b', 'm_even_w_in', 'm_pool_w', 'm_pool_scale', 'm_lru_conv_w', 'm_lru_conv_b', 'm_lru_w_a', 'm_lru_b_a', 'm_lru_w_x', 'm_lru_b_x', 'm_lru_lambda', 'm_even_w_out', 'm_mla_w_down', 'm_mla_q_norm_g', 'm_mla_kv_norm_g', 'm_mla_w_qb', 'm_mla_w_kvb', 'm_mla_w_o', 'm_mlp_w1', 'm_mlp_w2', 'v_ln_mix_g', 'v_ln_mix_b', 'v_ln_ffn_g', 'v_ln_ffn_b', 'v_even_w_in', 'v_pool_w', 'v_pool_scale', 'v_lru_conv_w', 'v_lru_conv_b', 'v_lru_w_a', 'v_lru_b_a', 'v_lru_w_x', 'v_lru_b_x', 'v_lru_lambda', 'v_even_w_out', 'v_mla_w_down', 'v_mla_q_norm_g', 'v_mla_kv_norm_g', 'v_mla_w_qb', 'v_mla_w_kvb', 'v_mla_w_o', 'v_mlp_w1', 'v_mlp_w2']
TWIN_OUTPUTS = ['loss', 'grad_x', 'grad_ln_mix_g', 'grad_ln_mix_b', 'grad_ln_ffn_g', 'grad_ln_ffn_b', 'grad_even_w_in', 'grad_pool_w', 'grad_pool_scale', 'grad_lru_conv_w', 'grad_lru_conv_b', 'grad_lru_w_a', 'grad_lru_b_a', 'grad_lru_w_x', 'grad_lru_b_x', 'grad_lru_lambda', 'grad_even_w_out', 'grad_mla_w_down', 'grad_mla_q_norm_g', 'grad_mla_kv_norm_g', 'grad_mla_w_qb', 'grad_mla_w_kvb', 'grad_mla_w_o', 'grad_mlp_w1', 'grad_mlp_w2', 'delta_ln_mix_g', 'delta_ln_mix_b', 'delta_ln_ffn_g', 'delta_ln_ffn_b', 'delta_even_w_in', 'delta_pool_w', 'delta_pool_scale', 'delta_lru_conv_w', 'delta_lru_conv_b', 'delta_lru_w_a', 'delta_lru_b_a', 'delta_lru_w_x', 'delta_lru_b_x', 'delta_lru_lambda', 'delta_even_w_out', 'delta_mla_w_down', 'delta_mla_q_norm_g', 'delta_mla_kv_norm_g', 'delta_mla_w_qb', 'delta_mla_w_kvb', 'delta_mla_w_o', 'delta_mlp_w1', 'delta_mlp_w2', 'new_m_ln_mix_g', 'new_m_ln_mix_b', 'new_m_ln_ffn_g', 'new_m_ln_ffn_b', 'new_m_even_w_in', 'new_m_pool_w', 'new_m_pool_scale', 'new_m_lru_conv_w', 'new_m_lru_conv_b', 'new_m_lru_w_a', 'new_m_lru_b_a', 'new_m_lru_w_x', 'new_m_lru_b_x', 'new_m_lru_lambda', 'new_m_even_w_out', 'new_m_mla_w_down', 'new_m_mla_q_norm_g', 'new_m_mla_kv_norm_g', 'new_m_mla_w_qb', 'new_m_mla_w_kvb', 'new_m_mla_w_o', 'new_m_mlp_w1', 'new_m_mlp_w2', 'new_v_ln_mix_g', 'new_v_ln_mix_b', 'new_v_ln_ffn_g', 'new_v_ln_ffn_b', 'new_v_even_w_in', 'new_v_pool_w', 'new_v_pool_scale', 'new_v_lru_conv_w', 'new_v_lru_conv_b', 'new_v_lru_w_a', 'new_v_lru_b_a', 'new_v_lru_w_x', 'new_v_lru_b_x', 'new_v_lru_lambda', 'new_v_even_w_out', 'new_v_mla_w_down', 'new_v_mla_q_norm_g', 'new_v_mla_kv_norm_g', 'new_v_mla_w_qb', 'new_v_mla_w_kvb', 'new_v_mla_w_o', 'new_v_mlp_w1', 'new_v_mlp_w2']
TWIN_LEAF_KINDS = {'loss': 'loss', 'grad_x': 'grad_x', 'grad_ln_mix_g': 'grad_w', 'grad_ln_mix_b': 'grad_w', 'grad_ln_ffn_g': 'grad_w', 'grad_ln_ffn_b': 'grad_w', 'grad_even_w_in': 'grad_w', 'grad_pool_w': 'grad_w', 'grad_pool_scale': 'grad_w', 'grad_lru_conv_w': 'grad_w', 'grad_lru_conv_b': 'grad_w', 'grad_lru_w_a': 'grad_w', 'grad_lru_b_a': 'grad_w', 'grad_lru_w_x': 'grad_w', 'grad_lru_b_x': 'grad_w', 'grad_lru_lambda': 'grad_w', 'grad_even_w_out': 'grad_w', 'grad_mla_w_down': 'grad_w', 'grad_mla_q_norm_g': 'grad_w', 'grad_mla_kv_norm_g': 'grad_w', 'grad_mla_w_qb': 'grad_w', 'grad_mla_w_kvb': 'grad_w', 'grad_mla_w_o': 'grad_w', 'grad_mlp_w1': 'grad_w', 'grad_mlp_w2': 'grad_w', 'delta_ln_mix_g': 'delta_w', 'delta_ln_mix_b': 'delta_w', 'delta_ln_ffn_g': 'delta_w', 'delta_ln_ffn_b': 'delta_w', 'delta_even_w_in': 'delta_w', 'delta_pool_w': 'delta_w', 'delta_pool_scale': 'delta_w', 'delta_lru_conv_w': 'delta_w', 'delta_lru_conv_b': 'delta_w', 'delta_lru_w_a': 'delta_w', 'delta_lru_b_a': 'delta_w', 'delta_lru_w_x': 'delta_w', 'delta_lru_b_x': 'delta_w', 'delta_lru_lambda': 'delta_w', 'delta_even_w_out': 'delta_w', 'delta_mla_w_down': 'delta_w', 'delta_mla_q_norm_g': 'delta_w', 'delta_mla_kv_norm_g': 'delta_w', 'delta_mla_w_qb': 'delta_w', 'delta_mla_w_kvb': 'delta_w', 'delta_mla_w_o': 'delta_w', 'delta_mlp_w1': 'delta_w', 'delta_mlp_w2': 'delta_w', 'new_m_ln_mix_g': 'new_m', 'new_m_ln_mix_b': 'new_m', 'new_m_ln_ffn_g': 'new_m', 'new_m_ln_ffn_b': 'new_m', 'new_m_even_w_in': 'new_m', 'new_m_pool_w': 'new_m', 'new_m_pool_scale': 'new_m', 'new_m_lru_conv_w': 'new_m', 'new_m_lru_conv_b': 'new_m', 'new_m_lru_w_a': 'new_m', 'new_m_lru_b_a': 'new_m', 'new_m_lru_w_x': 'new_m', 'new_m_lru_b_x': 'new_m', 'new_m_lru_lambda': 'new_m', 'new_m_even_w_out': 'new_m', 'new_m_mla_w_down': 'new_m', 'new_m_mla_q_norm_g': 'new_m', 'new_m_mla_kv_norm_g': 'new_m', 'new_m_mla_w_qb': 'new_m', 'new_m_mla_w_kvb': 'new_m', 'new_m_mla_w_o': 'new_m', 'new_m_mlp_w1': 'new_m', 'new_m_mlp_w2': 'new_m', 'new_v_ln_mix_g': 'new_v', 'new_v_ln_mix_b': 'new_v', 'new_v_ln_ffn_g': 'new_v', 'new_v_ln_ffn_b': 'new_v', 'new_v_even_w_in': 'new_v', 'new_v_pool_w': 'new_v', 'new_v_pool_scale': 'new_v', 'new_v_lru_conv_w': 'new_v', 'new_v_lru_conv_b': 'new_v', 'new_v_lru_w_a': 'new_v', 'new_v_lru_b_a': 'new_v', 'new_v_lru_w_x': 'new_v', 'new_v_lru_b_x': 'new_v', 'new_v_lru_lambda': 'new_v', 'new_v_even_w_out': 'new_v', 'new_v_mla_w_down': 'new_v', 'new_v_mla_q_norm_g': 'new_v', 'new_v_mla_kv_norm_g': 'new_v', 'new_v_mla_w_qb': 'new_v', 'new_v_mla_w_kvb': 'new_v', 'new_v_mla_w_o': 'new_v', 'new_v_mlp_w1': 'new_v', 'new_v_mlp_w2': 'new_v'}


def _forward(args):
    return _fwd_reference(*[args[k] for k in FWD_PARAMS])


def _output_shape():
    out = _jax.eval_shape(lambda: _forward(_fwd_setup_inputs(0)))
    return out.shape, out.dtype

N_MICROBATCH = 1
ADAM_LR = 0.001
ADAM_B1 = 0.9
ADAM_B2 = 0.999
ADAM_EPS = 1e-08
ADAM_WD = 0.01
ADAM_STEP = 10
PER_EXAMPLE_BATCH_AXIS = {'x': 0, 'positions': 0, 'loss_target': 0}
SHARED_INPUTS = []
_WEIGHT_DTYPES = {'ln_mix_g': _jnp.float32, 'ln_mix_b': _jnp.float32, 'ln_ffn_g': _jnp.float32, 'ln_ffn_b': _jnp.float32, 'even_w_in': _jnp.float32, 'pool_w': _jnp.float32, 'pool_scale': _jnp.float32, 'lru_conv_w': _jnp.float32, 'lru_conv_b': _jnp.float32, 'lru_w_a': _jnp.float32, 'lru_b_a': _jnp.float32, 'lru_w_x': _jnp.float32, 'lru_b_x': _jnp.float32, 'lru_lambda': _jnp.float32, 'even_w_out': _jnp.float32, 'mla_w_down': _jnp.float32, 'mla_q_norm_g': _jnp.float32, 'mla_kv_norm_g': _jnp.float32, 'mla_w_qb': _jnp.float32, 'mla_w_kvb': _jnp.float32, 'mla_w_o': _jnp.float32, 'mlp_w1': _jnp.float32, 'mlp_w2': _jnp.float32}
MOMENT_SCALE = {'ln_mix_g': 2.728605e+00, 'ln_mix_b': 8.624348e-01, 'ln_ffn_g': 9.278641e+00, 'ln_ffn_b': 2.221760e+00, 'even_w_in': 2.794577e-02, 'pool_w': 2.568885e-02, 'pool_scale': 2.473979e-02, 'lru_conv_w': 4.110270e-02, 'lru_conv_b': 3.263623e-01, 'lru_w_a': 7.528236e-03, 'lru_b_a': 6.607589e-03, 'lru_w_x': 1.411396e-02, 'lru_b_x': 1.338821e-02, 'lru_lambda': 1.434608e-02, 'even_w_out': 8.824414e-02, 'mla_w_down': 4.209347e-02, 'mla_q_norm_g': 7.912932e-03, 'mla_kv_norm_g': 7.219671e-02, 'mla_w_qb': 3.949295e-03, 'mla_w_kvb': 2.472227e-02, 'mla_w_o': 8.122484e-02, 'mlp_w1': 2.558656e-02, 'mlp_w2': 2.474472e-01}


def _to_microbatches(a, axis):
    t = _jnp.moveaxis(a, axis, 0)
    t = t.reshape((N_MICROBATCH, t.shape[0] // N_MICROBATCH) + t.shape[1:])
    return _jnp.moveaxis(t, 1, axis + 1)


def setup_inputs(seed: int = 0) -> dict:
    inp = _fwd_setup_inputs(seed)
    key = _jax.random.fold_in(_jax.random.key(seed), 7919)
    shape, _ = _output_shape()
    out = dict(inp)
    out["loss_target"] = _jax.random.normal(_jax.random.fold_in(key, 0), shape, _jnp.float32)
    for i, name in enumerate(TWIN_WEIGHTS):
        w = inp[name].astype(_jnp.float32)
        if MOMENT_SCALE is None:
            s = _jnp.sqrt(_jnp.mean(_jnp.square(w)) + 1e-30)
        else:
            s = MOMENT_SCALE[name]
        km, kv = _jax.random.split(_jax.random.fold_in(key, i + 1))
        out[name] = w
        out["m_" + name] = s * _jax.random.normal(km, w.shape, _jnp.float32)
        out["v_" + name] = (s * s) * _jax.random.uniform(kv, w.shape, _jnp.float32, 0.5, 1.5)
    if N_MICROBATCH > 1:
        for name, axis in PER_EXAMPLE_BATCH_AXIS.items():
            out[name] = _to_microbatches(out[name], axis)
    return {'x': out['x'], 'positions': out['positions'], 'ln_mix_g': out['ln_mix_g'], 'ln_mix_b': out['ln_mix_b'], 'ln_ffn_g': out['ln_ffn_g'], 'ln_ffn_b': out['ln_ffn_b'], 'even_w_in': out['even_w_in'], 'pool_w': out['pool_w'], 'pool_scale': out['pool_scale'], 'lru_conv_w': out['lru_conv_w'], 'lru_conv_b': out['lru_conv_b'], 'lru_w_a': out['lru_w_a'], 'lru_b_a': out['lru_b_a'], 'lru_w_x': out['lru_w_x'], 'lru_b_x': out['lru_b_x'], 'lru_lambda': out['lru_lambda'], 'even_w_out': out['even_w_out'], 'mla_w_down': out['mla_w_down'], 'mla_q_norm_g': out['mla_q_norm_g'], 'mla_kv_norm_g': out['mla_kv_norm_g'], 'mla_w_qb': out['mla_w_qb'], 'mla_w_kvb': out['mla_w_kvb'], 'mla_w_o': out['mla_w_o'], 'mlp_w1': out['mlp_w1'], 'mlp_w2': out['mlp_w2'], 'loss_target': out['loss_target'], 'm_ln_mix_g': out['m_ln_mix_g'], 'm_ln_mix_b': out['m_ln_mix_b'], 'm_ln_ffn_g': out['m_ln_ffn_g'], 'm_ln_ffn_b': out['m_ln_ffn_b'], 'm_even_w_in': out['m_even_w_in'], 'm_pool_w': out['m_pool_w'], 'm_pool_scale': out['m_pool_scale'], 'm_lru_conv_w': out['m_lru_conv_w'], 'm_lru_conv_b': out['m_lru_conv_b'], 'm_lru_w_a': out['m_lru_w_a'], 'm_lru_b_a': out['m_lru_b_a'], 'm_lru_w_x': out['m_lru_w_x'], 'm_lru_b_x': out['m_lru_b_x'], 'm_lru_lambda': out['m_lru_lambda'], 'm_even_w_out': out['m_even_w_out'], 'm_mla_w_down': out['m_mla_w_down'], 'm_mla_q_norm_g': out['m_mla_q_norm_g'], 'm_mla_kv_norm_g': out['m_mla_kv_norm_g'], 'm_mla_w_qb': out['m_mla_w_qb'], 'm_mla_w_kvb': out['m_mla_w_kvb'], 'm_mla_w_o': out['m_mla_w_o'], 'm_mlp_w1': out['m_mlp_w1'], 'm_mlp_w2': out['m_mlp_w2'], 'v_ln_mix_g': out['v_ln_mix_g'], 'v_ln_mix_b': out['v_ln_mix_b'], 'v_ln_ffn_g': out['v_ln_ffn_g'], 'v_ln_ffn_b': out['v_ln_ffn_b'], 'v_even_w_in': out['v_even_w_in'], 'v_pool_w': out['v_pool_w'], 'v_pool_scale': out['v_pool_scale'], 'v_lru_conv_w': out['v_lru_conv_w'], 'v_lru_conv_b': out['v_lru_conv_b'], 'v_lru_w_a': out['v_lru_w_a'], 'v_lru_b_a': out['v_lru_b_a'], 'v_lru_w_x': out['v_lru_w_x'], 'v_lru_b_x': out['v_lru_b_x'], 'v_lru_lambda': out['v_lru_lambda'], 'v_even_w_out': out['v_even_w_out'], 'v_mla_w_down': out['v_mla_w_down'], 'v_mla_q_norm_g': out['v_mla_q_norm_g'], 'v_mla_kv_norm_g': out['v_mla_kv_norm_g'], 'v_mla_w_qb': out['v_mla_w_qb'], 'v_mla_w_kvb': out['v_mla_w_kvb'], 'v_mla_w_o': out['v_mla_w_o'], 'v_mlp_w1': out['v_mlp_w1'], 'v_mlp_w2': out['v_mlp_w2']}


def _loss(weights, diff, rest, loss_target):
    with _jax.named_scope("forward"):
        args = {**rest, TWIN_DIFF_INPUT: diff, **{k: w.astype(_WEIGHT_DTYPES[k]) for k, w in weights.items()}}
        y = _forward(args)
    with _jax.named_scope("loss_head"):
        err = _jnp.square(y.astype(_jnp.float32) - loss_target)
        return 0.5 * _jnp.sum(_jnp.mean(err, axis=-1)) if err.ndim else 0.5 * err


def _adamw(w, g, m, v):
    m = ADAM_B1 * m + (1.0 - ADAM_B1) * g
    v = ADAM_B2 * v + (1.0 - ADAM_B2) * _jnp.square(g)
    m_hat = m / (1.0 - ADAM_B1 ** ADAM_STEP)
    v_hat = v / (1.0 - ADAM_B2 ** ADAM_STEP)
    delta = -ADAM_LR * (m_hat / (_jnp.sqrt(v_hat) + ADAM_EPS) + ADAM_WD * w)
    return delta, m, v


def reference(x, positions, ln_mix_g, ln_mix_b, ln_ffn_g, ln_ffn_b, even_w_in, pool_w, pool_scale, lru_conv_w, lru_conv_b, lru_w_a, lru_b_a, lru_w_x, lru_b_x, lru_lambda, even_w_out, mla_w_down, mla_q_norm_g, mla_kv_norm_g, mla_w_qb, mla_w_kvb, mla_w_o, mlp_w1, mlp_w2, loss_target, m_ln_mix_g, m_ln_mix_b, m_ln_ffn_g, m_ln_ffn_b, m_even_w_in, m_pool_w, m_pool_scale, m_lru_conv_w, m_lru_conv_b, m_lru_w_a, m_lru_b_a, m_lru_w_x, m_lru_b_x, m_lru_lambda, m_even_w_out, m_mla_w_down, m_mla_q_norm_g, m_mla_kv_norm_g, m_mla_w_qb, m_mla_w_kvb, m_mla_w_o, m_mlp_w1, m_mlp_w2, v_ln_mix_g, v_ln_mix_b, v_ln_ffn_g, v_ln_ffn_b, v_even_w_in, v_pool_w, v_pool_scale, v_lru_conv_w, v_lru_conv_b, v_lru_w_a, v_lru_b_a, v_lru_w_x, v_lru_b_x, v_lru_lambda, v_even_w_out, v_mla_w_down, v_mla_q_norm_g, v_mla_kv_norm_g, v_mla_w_qb, v_mla_w_kvb, v_mla_w_o, v_mlp_w1, v_mlp_w2):
    given = dict(x=x, positions=positions, ln_mix_g=ln_mix_g, ln_mix_b=ln_mix_b, ln_ffn_g=ln_ffn_g, ln_ffn_b=ln_ffn_b, even_w_in=even_w_in, pool_w=pool_w, pool_scale=pool_scale, lru_conv_w=lru_conv_w, lru_conv_b=lru_conv_b, lru_w_a=lru_w_a, lru_b_a=lru_b_a, lru_w_x=lru_w_x, lru_b_x=lru_b_x, lru_lambda=lru_lambda, even_w_out=even_w_out, mla_w_down=mla_w_down, mla_q_norm_g=mla_q_norm_g, mla_kv_norm_g=mla_kv_norm_g, mla_w_qb=mla_w_qb, mla_w_kvb=mla_w_kvb, mla_w_o=mla_w_o, mlp_w1=mlp_w1, mlp_w2=mlp_w2, loss_target=loss_target, m_ln_mix_g=m_ln_mix_g, m_ln_mix_b=m_ln_mix_b, m_ln_ffn_g=m_ln_ffn_g, m_ln_ffn_b=m_ln_ffn_b, m_even_w_in=m_even_w_in, m_pool_w=m_pool_w, m_pool_scale=m_pool_scale, m_lru_conv_w=m_lru_conv_w, m_lru_conv_b=m_lru_conv_b, m_lru_w_a=m_lru_w_a, m_lru_b_a=m_lru_b_a, m_lru_w_x=m_lru_w_x, m_lru_b_x=m_lru_b_x, m_lru_lambda=m_lru_lambda, m_even_w_out=m_even_w_out, m_mla_w_down=m_mla_w_down, m_mla_q_norm_g=m_mla_q_norm_g, m_mla_kv_norm_g=m_mla_kv_norm_g, m_mla_w_qb=m_mla_w_qb, m_mla_w_kvb=m_mla_w_kvb, m_mla_w_o=m_mla_w_o, m_mlp_w1=m_mlp_w1, m_mlp_w2=m_mlp_w2, v_ln_mix_g=v_ln_mix_g, v_ln_mix_b=v_ln_mix_b, v_ln_ffn_g=v_ln_ffn_g, v_ln_ffn_b=v_ln_ffn_b, v_even_w_in=v_even_w_in, v_pool_w=v_pool_w, v_pool_scale=v_pool_scale, v_lru_conv_w=v_lru_conv_w, v_lru_conv_b=v_lru_conv_b, v_lru_w_a=v_lru_w_a, v_lru_b_a=v_lru_b_a, v_lru_w_x=v_lru_w_x, v_lru_b_x=v_lru_b_x, v_lru_lambda=v_lru_lambda, v_even_w_out=v_even_w_out, v_mla_w_down=v_mla_w_down, v_mla_q_norm_g=v_mla_q_norm_g, v_mla_kv_norm_g=v_mla_kv_norm_g, v_mla_w_qb=v_mla_w_qb, v_mla_w_kvb=v_mla_w_kvb, v_mla_w_o=v_mla_w_o, v_mlp_w1=v_mlp_w1, v_mlp_w2=v_mlp_w2)
    weights = {n: given[n] for n in TWIN_WEIGHTS}
    shared = {n: given[n] for n in SHARED_INPUTS}
    per_example = {n: given[n] for n in ['x', 'positions']}
    grad_fn = _jax.value_and_grad(_loss, argnums=(0, 1))

    def one_microbatch(ex, loss_target):
        ex = dict(ex)
        diff = ex.pop(TWIN_DIFF_INPUT)
        return grad_fn(weights, diff, {**shared, **ex}, loss_target)

    if N_MICROBATCH == 1:
        loss, (grad_w, grad_x) = one_microbatch(per_example, given["loss_target"])
    else:
        def body(carry, xs):
            loss_sum, grad_sum = carry
            l_k, (gw_k, gx_k) = one_microbatch(xs[0], xs[1])
            with _jax.named_scope("update"):
                return (loss_sum + l_k, _jax.tree.map(_jnp.add, grad_sum, gw_k)), gx_k

        init = (_jnp.zeros((), _jnp.float32), _jax.tree.map(_jnp.zeros_like, weights))
        (loss, grad_w), grad_x = _jax.lax.scan(body, init, (per_example, given["loss_target"]))
    with _jax.named_scope("update"):
        delta_w, new_m, new_v = {}, {}, {}
        for n in TWIN_WEIGHTS:
            delta_w[n], new_m[n], new_v[n] = _adamw(weights[n], grad_w[n], given["m_" + n], given["v_" + n])
    return (loss, grad_x, *[grad_w[n] for n in TWIN_WEIGHTS], *[delta_w[n] for n in TWIN_WEIGHTS],
            *[new_m[n] for n in TWIN_WEIGHTS], *[new_v[n] for n in TWIN_WEIGHTS])
```

```python
import functools

import jax
import jax.numpy as jnp
from jax import lax
from jax.experimental import pallas as pl
from jax.experimental.pallas import tpu as pltpu

F32 = jnp.float32
BF16 = jnp.bfloat16
S = jax.ShapeDtypeStruct

D = 1024
DEPTH = 4
N_DEV = 8
CHUNK_SHIFT = 6
POOL_WINDOWS = (2, 4, 8, 16)
POOL_W = 512
LRU_W = 1024
LRU_HEADS = 8
HEAD = 128
LRU_C = 8.0
EVEN_IN = 2560
EVEN_MIX = 1536
MLA_HEADS = 8
NOPE = 128
ROPE = 64
VDIM = 128
Q_RANK = 384
KV_RANK = 256
ODD_IN = 704
D_FF = 4096
FF_BLK = D_FF // N_DEV
ROPE_THETA = 10000.0
ALPHA = (2 * DEPTH) ** 0.25
LN_EPS = 1e-5
RMS_EPS = 1e-6
ATT_SCALE = (NOPE + ROPE) ** -0.5
NEG = float(jnp.finfo(jnp.float32).min)
ADAM_LR = 0.001
ADAM_B1 = 0.9
ADAM_B2 = 0.999
ADAM_EPS = 1e-08
ADAM_WD = 0.01
ADAM_STEP = 10
V7X_VMEM_BYTES = 64 * 1024 * 1024
VMEM_LIMIT = V7X_VMEM_BYTES - 8 * 1024 * 1024
MESH = pl.DeviceIdType.MESH


def _cp(*sem):
    return pltpu.CompilerParams(dimension_semantics=sem if sem else None, vmem_limit_bytes=VMEM_LIMIT)


def _dot(a, b):
    return jnp.dot(a, b, preferred_element_type=F32)


def _dot_nt(a, b):
    return lax.dot_general(a, b, (((1,), (1,)), ((), ())), preferred_element_type=F32)


def _dot_tn(a, b):
    return lax.dot_general(a, b, (((0,), (0,)), ((), ())), preferred_element_type=F32)


def _full(shape):
    return pl.BlockSpec(shape, lambda *_: (0,) * len(shape))


def _mm(a, b, *, mode, grid, a_spec, b_spec, out_shape, out_spec, name, add=None, add_spec=None, add_scale=1.0):
    dot = {"nn": _dot, "nt": _dot_nt, "tn": _dot_tn}[mode]

    def body(*refs):
        if add is None:
            a_ref, b_ref, o_ref = refs
            acc = dot(a_ref[...].astype(BF16), b_ref[...].astype(BF16))
        else:
            a_ref, b_ref, add_ref, o_ref = refs
            acc = dot(a_ref[...].astype(BF16), b_ref[...].astype(BF16)) + add_scale * add_ref[...]
        o_ref[...] = acc.astype(o_ref.dtype)

    ops = (a, b) if add is None else (a, b, add)
    specs = [a_spec, b_spec] if add is None else [a_spec, b_spec, add_spec]
    return pl.pallas_call(body, grid=grid, in_specs=specs, out_specs=out_spec, out_shape=out_shape,
                          compiler_params=_cp(*(("parallel",) * len(grid))), name=name)(*ops)


def _ln_stats(z):
    mu = jnp.mean(z, axis=-1, keepdims=True)
    zc = z - mu
    var = jnp.mean(zc * zc, axis=-1, keepdims=True)
    rstd = lax.rsqrt(var + LN_EPS)
    return zc * rstd, rstd


def _row_tile(t):
    return min(512, t)


def _resid_ln(x, mix, g3, b3, l, name):
    t = x.shape[0]
    bm = _row_tile(t)

    def body(x_ref, m_ref, g_ref, b_ref, z_ref, y_ref, yb_ref):
        z = ALPHA * x_ref[...] + m_ref[...]
        xh, _ = _ln_stats(z)
        y = xh * g_ref[...] + b_ref[...]
        z_ref[...] = z
        y_ref[...] = y
        yb_ref[...] = y.astype(BF16)

    row = pl.BlockSpec((bm, D), lambda i: (i, 0))
    vec = pl.BlockSpec((None, 1, D), lambda i: (l, 0, 0))
    return pl.pallas_call(body, grid=(t // bm,), in_specs=[row, row, vec, vec], out_specs=[row, row, row],
                          out_shape=[S((t, D), F32), S((t, D), F32), S((t, D), BF16)],
                          compiler_params=_cp("parallel"), name=name)(x, mix, g3, b3)


def _ln_bwd(d, z, g3, l, name, r=None):
    t = z.shape[0]
    bm = _row_tile(t)

    def body(*refs):
        if r is None:
            d_ref, z_ref, g_ref, dz_ref, dzb_ref, dg_ref, db_ref = refs
            dy = d_ref[...]
        else:
            d_ref, r_ref, z_ref, g_ref, dz_ref, dzb_ref, dg_ref, db_ref = refs
            dy = d_ref[...] + ALPHA * r_ref[...]
        xh, rstd = _ln_stats(z_ref[...])
        dyg = dy * g_ref[...]
        m1 = jnp.mean(dyg, axis=-1, keepdims=True)
        m2 = jnp.mean(dyg * xh, axis=-1, keepdims=True)
        dz = rstd * (dyg - m1 - xh * m2)
        dz_ref[...] = dz
        dzb_ref[...] = dz.astype(BF16)

        @pl.when(pl.program_id(0) == 0)
        def _():
            dg_ref[...] = jnp.zeros_like(dg_ref)
            db_ref[...] = jnp.zeros_like(db_ref)

        dg_ref[...] += jnp.sum(dy * xh, axis=0, keepdims=True)
        db_ref[...] += jnp.sum(dy, axis=0, keepdims=True)

    row = pl.BlockSpec((bm, D), lambda i: (i, 0))
    vec = pl.BlockSpec((None, 1, D), lambda i: (l, 0, 0))
    acc = pl.BlockSpec((1, D), lambda i: (0, 0))
    ops = (d, z, g3) if r is None else (d, r, z, g3)
    specs = [row, row, vec] if r is None else [row, row, row, vec]
    return pl.pallas_call(body, grid=(t // bm,), in_specs=specs, out_specs=[row, row, acc, acc],
                          out_shape=[S((t, D), F32), S((t, D), BF16), S((1, D), F32), S((1, D), F32)],
                          compiler_params=_cp("arbitrary"), name=name)(*ops)


def _loss_grad(y, tgt):
    t = y.shape[0]
    bm = _row_tile(t)

    def body(y_ref, t_ref, dy_ref, loss_ref, acc_ref):
        i = pl.program_id(0)
        e = y_ref[...] - t_ref[...]
        dy_ref[...] = e * (1.0 / D)

        @pl.when(i == 0)
        def _():
            acc_ref[...] = jnp.zeros_like(acc_ref)

        acc_ref[...] += jnp.sum(e * e, axis=0, keepdims=True)

        @pl.when(i == pl.num_programs(0) - 1)
        def _():
            loss_ref[...] = jnp.full(loss_ref.shape, (0.5 / D) * jnp.sum(acc_ref[...]), F32)

    row = pl.BlockSpec((bm, D), lambda i: (i, 0))
    return pl.pallas_call(body, grid=(t // bm,), in_specs=[row, row],
                          out_specs=[row, pl.BlockSpec((1, 128), lambda i: (0, 0))],
                          out_shape=[S((t, D), F32), S((1, 128), F32)],
                          scratch_shapes=[pltpu.VMEM((1, D), F32)],
                          compiler_params=_cp("arbitrary"), name="loss_grad")(y, tgt)


def _mlp_fwd(yb, w1g, w2g, l):
    t = yb.shape[0]
    bm = _row_tile(t)

    def body(y_ref, w1_ref, w2_ref, o_ref):
        j = pl.program_id(1)
        h = jnp.maximum(_dot(y_ref[...], w1_ref[...]), 0.0)
        c = _dot((h * h).astype(BF16), w2_ref[...])

        @pl.when(j == 0)
        def _():
            o_ref[...] = c

        @pl.when(j > 0)
        def _():
            o_ref[...] += c

    return pl.pallas_call(
        body, grid=(t // bm, N_DEV),
        in_specs=[pl.BlockSpec((bm, D), lambda i, j: (i, 0)),
                  pl.BlockSpec((None, None, D, FF_BLK), lambda i, j: (j, l, 0, 0)),
                  pl.BlockSpec((None, None, FF_BLK, D), lambda i, j: (j, l, 0, 0))],
        out_specs=pl.BlockSpec((bm, D), lambda i, j: (i, 0)),
        out_shape=S((t, D), F32), compiler_params=_cp("parallel", "arbitrary"), name="mlp_fwd")(yb, w1g, w2g)


def _mlp_bwd_dh(yb, dzb, w1g, w2g, l):
    t = yb.shape[0]
    bm = _row_tile(t)

    def body(y_ref, dz_ref, w1_ref, w2_ref, a_ref, dh_ref, acc_ref):
        j = pl.program_id(1)
        r = jnp.maximum(_dot(y_ref[...], w1_ref[...]), 0.0)
        a_ref[...] = (r * r).astype(BF16)
        da = _dot_nt(dz_ref[...], w2_ref[...])
        dh = (da * (2.0 * r)).astype(BF16)
        dh_ref[...] = dh
        c = _dot_nt(dh, w1_ref[...])

        @pl.when(j == 0)
        def _():
            acc_ref[...] = c

        @pl.when(j > 0)
        def _():
            acc_ref[...] += c

    row = pl.BlockSpec((bm, D), lambda i, j: (i, 0))
    hid = pl.BlockSpec((bm, FF_BLK), lambda i, j: (i, j))
    return pl.pallas_call(
        body, grid=(t // bm, N_DEV),
        in_specs=[row, row,
                  pl.BlockSpec((None, None, D, FF_BLK), lambda i, j: (j, l, 0, 0)),
                  pl.BlockSpec((None, None, FF_BLK, D), lambda i, j: (j, l, 0, 0))],
        out_specs=[hid, hid, row],
        out_shape=[S((t, D_FF), BF16), S((t, D_FF), BF16), S((t, D), F32)],
        compiler_params=_cp("parallel", "arbitrary"), name="mlp_bwd_dh")(yb, dzb, w1g, w2g)


def _shift_dn(x, k, rows, fill=0.0):
    return jnp.where(rows >= k, pltpu.roll(x, k, 0), fill)


def _shift_up(x, k, rows, fill=0.0):
    t = x.shape[0]
    return jnp.where(rows < t - k, pltpu.roll(x, t - k, 0), fill)


def _scan_dn(a, b, rows):
    k = 1
    t = a.shape[0]
    while k < t:
        b = a * _shift_dn(b, k, rows) + b
        if 2 * k < t:
            a = a * _shift_dn(a, k, rows, 1.0)
        k *= 2
    return b


def _scan_up(a, b, rows):
    k = 1
    t = a.shape[0]
    while k < t:
        b = a * _shift_up(b, k, rows) + b
        if 2 * k < t:
            a = a * _shift_up(a, k, rows, 1.0)
        k *= 2
    return b


def _window_sum_dn(x, w, rows):
    k = 1
    while k < w:
        x = x + _shift_dn(x, k, rows)
        k *= 2
    return x


def _window_sum_up(x, w, rows):
    k = 1
    while k < w:
        x = x + _shift_up(x, k, rows)
        k *= 2
    return x


def _pool_diff(u, w, rows):
    inv_count = 1.0 / jnp.minimum(rows + 1, w).astype(F32)
    return _window_sum_dn(u, w, rows) * inv_count - u, inv_count


def _pool_fwd(proj, pool_w, pool_scale3, j):
    t = proj.shape[0]

    def body(u_ref, w_ref, s_ref, y_ref):
        rows = lax.broadcasted_iota(jnp.int32, (t, HEAD), 0)
        for g, w in enumerate(POOL_WINDOWS):
            cols = slice(g * HEAD, (g + 1) * HEAD)
            d, _ = _pool_diff(u_ref[:, cols], w, rows)
            y = _dot(d.astype(BF16), w_ref[g].astype(BF16)) * s_ref[:, cols]
            y_ref[:, cols] = y.astype(BF16)

    return pl.pallas_call(
        body, grid=(1,),
        in_specs=[pl.BlockSpec((t, POOL_W), lambda i: (0, 0)),
                  pl.BlockSpec((None, 4, HEAD, HEAD), lambda i: (j, 0, 0, 0)),
                  pl.BlockSpec((None, 1, POOL_W), lambda i: (j, 0, 0))],
        out_specs=pl.BlockSpec((t, POOL_W), lambda i: (0, 0)),
        out_shape=S((t, POOL_W), BF16), compiler_params=_cp("arbitrary"), name="pool_fwd")(proj, pool_w, pool_scale3)


def _pool_bwd(proj, dycat, pool_w, pool_scale3, j):
    t = proj.shape[0]

    def body(u_ref, dy_ref, w_ref, s_ref, du_ref, dw_ref, ds_ref):
        rows = lax.broadcasted_iota(jnp.int32, (t, HEAD), 0)
        for g, w in enumerate(POOL_WINDOWS):
            cols = slice(g * HEAD, (g + 1) * HEAD)
            d, inv_count = _pool_diff(u_ref[:, cols], w, rows)
            db = d.astype(BF16)
            wg = w_ref[g].astype(BF16)
            dy = dy_ref[:, cols]
            ds_ref[:, cols] = jnp.sum(dy * _dot(db, wg), axis=0, keepdims=True)
            dzz = (dy * s_ref[:, cols]).astype(BF16)
            dw_ref[g] = _dot_tn(db, dzz)
            dd = _dot_nt(dzz, wg)
            du_ref[:, cols] = (_window_sum_up(dd * inv_count, w, rows) - dd).astype(BF16)

    return pl.pallas_call(
        body, grid=(1,),
        in_specs=[pl.BlockSpec((t, POOL_W), lambda i: (0, 0)),
                  pl.BlockSpec((t, POOL_W), lambda i: (0, 0)),
                  pl.BlockSpec((None, 4, HEAD, HEAD), lambda i: (j, 0, 0, 0)),
                  pl.BlockSpec((None, 1, POOL_W), lambda i: (j, 0, 0))],
        out_specs=[pl.BlockSpec((t, POOL_W), lambda i: (0, 0)), _full((4, HEAD, HEAD)), _full((1, POOL_W))],
        out_shape=[S((t, POOL_W), BF16), S((4, HEAD, HEAD), F32), S((1, POOL_W), F32)],
        compiler_params=_cp("arbitrary"), name="pool_bwd")(proj, dycat, pool_w, pool_scale3)


GELU_C = 0.7978845608028654
GELU_K = 0.044715


def _gelu(x):
    th = jnp.tanh(GELU_C * (x + GELU_K * x * x * x))
    return 0.5 * x * (1.0 + th), th


def _lru_forward(u, gate, cw, cb, wa, ba, wx, bx, lam, rows):
    v = cw[3:4] * u + cw[2:3] * _shift_dn(u, 1, rows) + cw[1:2] * _shift_dn(u, 2, rows) \
        + cw[0:1] * _shift_dn(u, 3, rows) + cb
    vb = v.astype(BF16)
    r = jax.nn.sigmoid(_dot(vb, wa) + ba)
    i = jax.nn.sigmoid(_dot(vb, wx) + bx)
    sp = jnp.maximum(-lam, 0.0) + jnp.log1p(jnp.exp(-jnp.abs(lam)))
    log_a = (-LRU_C) * r * sp
    a = jnp.exp(log_a)
    one_m_a2 = -jnp.tanh(log_a) * (a * a + 1.0)
    mult = jnp.sqrt(one_m_a2)
    h = _scan_dn(a, mult * (i * v), rows)
    gl, th = _gelu(gate)
    return dict(v=v, vb=vb, r=r, i=i, sp=sp, a=a, mult=mult, h=h, gl=gl, th=th)


def _lru_specs(t, j, col0_u, col0_g):
    blk = lambda c0: pl.BlockSpec((t, HEAD), lambda h: (0, c0 + h))
    vec = pl.BlockSpec((None, 1, HEAD), lambda h: (j, 0, h))
    return [blk(col0_u), blk(col0_g),
            pl.BlockSpec((None, 4, HEAD), lambda h: (j, 0, h)), vec,
            pl.BlockSpec((None, None, HEAD, HEAD), lambda h: (j, h, 0, 0)), vec,
            pl.BlockSpec((None, None, HEAD, HEAD), lambda h: (j, h, 0, 0)), vec, vec]


def _lru_fwd(proj, p, j):
    t = proj.shape[0]

    def body(u_ref, g_ref, cw_ref, cb_ref, wa_ref, ba_ref, wx_ref, bx_ref, lam_ref, y_ref):
        rows = lax.broadcasted_iota(jnp.int32, (t, HEAD), 0)
        f = _lru_forward(u_ref[...], g_ref[...], cw_ref[...], cb_ref[...], wa_ref[...].astype(BF16), ba_ref[...],
                         wx_ref[...].astype(BF16), bx_ref[...], lam_ref[...], rows)
        y_ref[...] = (f["h"] * f["gl"]).astype(BF16)

    return pl.pallas_call(
        body, grid=(LRU_HEADS,), in_specs=_lru_specs(t, j, POOL_W // HEAD, (POOL_W + LRU_W) // HEAD),
        out_specs=pl.BlockSpec((t, HEAD), lambda h: (0, h)), out_shape=S((t, LRU_W), BF16),
        compiler_params=_cp("parallel"), name="lru_fwd")(
            proj, proj, p["conv_w"], p["conv_b"], p["w_a"], p["b_a"], p["w_x"], p["b_x"], p["lam"])


def _lru_bwd(proj, dycat, p, j):
    t = proj.shape[0]

    def body(u_ref, g_ref, cw_ref, cb_ref, wa_ref, ba_ref, wx_ref, bx_ref, lam_ref, dy_ref,
             du_ref, dgate_ref, dcw_ref, dcb_ref, dwa_ref, dba_ref, dwx_ref, dbx_ref, dlam_ref):
        rows = lax.broadcasted_iota(jnp.int32, (t, HEAD), 0)
        u = u_ref[...]
        gate = g_ref[...]
        cw = cw_ref[...]
        wa = wa_ref[...].astype(BF16)
        wx = wx_ref[...].astype(BF16)
        lam = lam_ref[...]
        f = _lru_forward(u, gate, cw, cb_ref[...], wa, ba_ref[...], wx, bx_ref[...], lam, rows)
        v, r, i, a, mult, h, th = f["v"], f["r"], f["i"], f["a"], f["mult"], f["h"], f["th"]
        dy = dy_ref[...]
        dgl = 0.5 * (1.0 + th) + 0.5 * gate * (1.0 - th * th) * GELU_C * (1.0 + 3.0 * GELU_K * gate * gate)
        dgate_ref[...] = (dy * h * dgl).astype(BF16)
        g = _scan_up(_shift_up(a, 1, rows), dy * f["gl"], rows)
        da = g * _shift_dn(h, 1, rows)
        iv = i * v
        dmult = g * iv
        di = g * mult * v
        dv = g * mult * i
        dlog_a = da * a - dmult * (a * a) / mult
        dr = dlog_a * (-LRU_C) * f["sp"]
        dsp = jnp.sum(dlog_a * (-LRU_C) * r, axis=0, keepdims=True)
        dlam_ref[...] = -dsp * jax.nn.sigmoid(-lam)
        dpa = dr * r * (1.0 - r)
        dpx = di * i * (1.0 - i)
        dpab = dpa.astype(BF16)
        dpxb = dpx.astype(BF16)
        dwa_ref[...] = _dot_tn(f["vb"], dpab)
        dwx_ref[...] = _dot_tn(f["vb"], dpxb)
        dba_ref[...] = jnp.sum(dpa, axis=0, keepdims=True)
        dbx_ref[...] = jnp.sum(dpx, axis=0, keepdims=True)
        dv = dv + _dot_nt(dpab, wa) + _dot_nt(dpxb, wx)
        dcb_ref[...] = jnp.sum(dv, axis=0, keepdims=True)
        du = cw[3:4] * dv
        dcw_ref[3:4, :] = jnp.sum(dv * u, axis=0, keepdims=True)
        for k in (1, 2, 3):
            du = du + cw[3 - k:4 - k] * _shift_up(dv, k, rows)
            dcw_ref[3 - k:4 - k, :] = jnp.sum(dv * _shift_dn(u, k, rows), axis=0, keepdims=True)
        du_ref[...] = du.astype(BF16)

    blk = pl.BlockSpec((t, HEAD), lambda h: (0, h))
    vec = pl.BlockSpec((1, HEAD), lambda h: (0, h))
    mat = pl.BlockSpec((None, HEAD, HEAD), lambda h: (h, 0, 0))
    return pl.pallas_call(
        body, grid=(LRU_HEADS,),
        in_specs=_lru_specs(t, j, POOL_W // HEAD, (POOL_W + LRU_W) // HEAD)
        + [pl.BlockSpec((t, HEAD), lambda h: (0, POOL_W // HEAD + h))],
        out_specs=[blk, blk, pl.BlockSpec((4, HEAD), lambda h: (0, h)), vec, mat, vec, mat, vec, vec],
        out_shape=[S((t, LRU_W), BF16), S((t, LRU_W), BF16), S((4, LRU_W), F32), S((1, LRU_W), F32),
                   S((LRU_HEADS, HEAD, HEAD), F32), S((1, LRU_W), F32),
                   S((LRU_HEADS, HEAD, HEAD), F32), S((1, LRU_W), F32), S((1, LRU_W), F32)],
        compiler_params=_cp("parallel"), name="lru_bwd")(
            proj, proj, p["conv_w"], p["conv_b"], p["w_a"], p["b_a"], p["w_x"], p["b_x"], p["lam"], dycat)


def _rope(x, c, s):
    x1 = x[:, :ROPE // 2]
    x2 = x[:, ROPE // 2:]
    return jnp.concatenate([x1 * c - x2 * s, x1 * s + x2 * c], axis=-1)


def _rope_t(d, c, s):
    d1 = d[:, :ROPE // 2]
    d2 = d[:, ROPE // 2:]
    return jnp.concatenate([d1 * c + d2 * s, d2 * c - d1 * s], axis=-1)


def _rope_tables(pos2, inv_freq):
    t = pos2.shape[0]

    def body(p_ref, f_ref, c_ref, s_ref):
        ang = p_ref[...].astype(F32) * f_ref[...]
        c_ref[...] = jnp.cos(ang)
        s_ref[...] = jnp.sin(ang)

    return pl.pallas_call(body, out_shape=[S((t, ROPE // 2), F32), S((t, ROPE // 2), F32)],
                          name="rope_tables")(pos2, inv_freq)


def _down_norm(xb, wdown_g, gq3, gkv3, cos, sin, j):
    t = xb.shape[0]
    bm = _row_tile(t)

    def body(x_ref, w_ref, gq_ref, gkv_ref, c_ref, s_ref, down_ref, cq_ref, ckv_ref, kpe_ref):
        w = w_ref[...].reshape(D, ODD_IN)
        down = _dot(x_ref[...], w)
        down_ref[...] = down
        q = down[:, :Q_RANK]
        cq_ref[...] = (q * lax.rsqrt(jnp.mean(q * q, axis=-1, keepdims=True) + RMS_EPS) * gq_ref[...]).astype(BF16)
        kv = down[:, Q_RANK:Q_RANK + KV_RANK]
        ckv_ref[...] = (kv * lax.rsqrt(jnp.mean(kv * kv, axis=-1, keepdims=True) + RMS_EPS)
                        * gkv_ref[...]).astype(BF16)
        kpe_ref[...] = _rope(down[:, Q_RANK + KV_RANK:], c_ref[...], s_ref[...])

    row = lambda n: pl.BlockSpec((bm, n), lambda i: (i, 0))
    return pl.pallas_call(
        body, grid=(t // bm,),
        in_specs=[row(D), pl.BlockSpec((N_DEV, None, D // N_DEV, ODD_IN), lambda i: (0, j, 0, 0)),
                  pl.BlockSpec((None, 1, Q_RANK), lambda i: (j, 0, 0)),
                  pl.BlockSpec((None, 1, KV_RANK), lambda i: (j, 0, 0)), row(ROPE // 2), row(ROPE // 2)],
        out_specs=[row(ODD_IN), row(Q_RANK), row(KV_RANK), row(ROPE)],
        out_shape=[S((t, ODD_IN), F32), S((t, Q_RANK), BF16), S((t, KV_RANK), BF16), S((t, ROPE), F32)],
        compiler_params=_cp("parallel"), name="down_norm")(xb, wdown_g, gq3, gkv3, cos, sin)


def _q_tile(t):
    return min(256, t // 2)


def _attn_probs(qn, qp, kn, kp, qs):
    s = (_dot_nt(qn, kn) + _dot_nt(qp, kp)) * ATT_SCALE
    rows = qs + lax.broadcasted_iota(jnp.int32, s.shape, 0)
    cols = lax.broadcasted_iota(jnp.int32, s.shape, 1)
    s = jnp.where(jnp.right_shift(cols, CHUNK_SHIFT) <= jnp.right_shift(rows, CHUNK_SHIFT), s, NEG)
    e = jnp.exp(s - jnp.max(s, axis=-1, keepdims=True))
    return e / jnp.sum(e, axis=-1, keepdims=True)


def _head_qkv(cq, ckv, kpe, c, s, wq_ref, wkv_ref):
    qn = _dot(cq, wq_ref[:, :NOPE]).astype(BF16)
    qp = _rope(_dot(cq, wq_ref[:, NOPE:]), c, s).astype(BF16)
    kn = _dot(ckv, wkv_ref[:, :NOPE]).astype(BF16)
    vv = _dot(ckv, wkv_ref[:, NOPE:]).astype(BF16)
    return qn, qp, kn, kpe.astype(BF16), vv


def _attn_in_specs(t, j):
    return [_full((t, Q_RANK)), _full((t, KV_RANK)), _full((t, ROPE)), _full((t, ROPE // 2)), _full((t, ROPE // 2)),
            pl.BlockSpec((None, None, Q_RANK, NOPE + ROPE), lambda h: (h, j, 0, 0)),
            pl.BlockSpec((None, None, KV_RANK, NOPE + VDIM), lambda h: (h, j, 0, 0)),
            pl.BlockSpec((None, None, VDIM, D), lambda h: (h, j, 0, 0))]


def _attn_fwd(cq, ckv, kpe, cos, sin, wqb_g, wkvb_g, wo_g, j):
    t = cq.shape[0]
    tq = _q_tile(t)

    def body(cq_ref, ckv_ref, kpe_ref, c_ref, s_ref, wq_ref, wkv_ref, wo_ref, o_ref, mix_ref):
        qn, qp, kn, kp, vv = _head_qkv(cq_ref[...], ckv_ref[...], kpe_ref[...], c_ref[...], s_ref[...],
                                       wq_ref, wkv_ref)
        for qs in range(0, t, tq):
            ke = qs + tq
            p = _attn_probs(qn[qs:ke], qp[qs:ke], kn[:ke], kp[:ke], qs)
            o_ref[qs:ke, :] = _dot(p.astype(BF16), vv[:ke]).astype(BF16)
        c = _dot(o_ref[...], wo_ref[...])

        @pl.when(pl.program_id(0) == 0)
        def _():
            mix_ref[...] = c

        @pl.when(pl.program_id(0) > 0)
        def _():
            mix_ref[...] += c

    return pl.pallas_call(
        body, grid=(MLA_HEADS,), in_specs=_attn_in_specs(t, j),
        out_specs=[pl.BlockSpec((None, t, VDIM), lambda h: (h, 0, 0)), _full((t, D))],
        out_shape=[S((MLA_HEADS, t, VDIM), BF16), S((t, D), F32)],
        compiler_params=_cp("arbitrary"), name="attn_fwd")(cq, ckv, kpe, cos, sin, wqb_g, wkvb_g, wo_g)


def _attn_bwd(cq, ckv, kpe, cos, sin, wqb_g, wkvb_g, wo_g, o, dzb, j):
    t = cq.shape[0]
    tq = _q_tile(t)

    def body(cq_ref, ckv_ref, kpe_ref, c_ref, s_ref, wq_ref, wkv_ref, wo_ref, o_ref, dz_ref,
             dwo_ref, dwq_ref, dwkv_ref, dcq_ref, dckv_ref, dkpe_ref, dkn_s, dkp_s, dv_s, dqn_s, dqp_s):
        cqv = cq_ref[...]
        ckvv = ckv_ref[...]
        c = c_ref[...]
        s = s_ref[...]
        qn, qp, kn, kp, vv = _head_qkv(cqv, ckvv, kpe_ref[...], c, s, wq_ref, wkv_ref)
        dzv = dz_ref[...]
        dwo_ref[...] = _dot_tn(o_ref[...], dzv).astype(BF16)
        do = _dot_nt(dzv, wo_ref[...]).astype(BF16)
        dkn_s[...] = jnp.zeros_like(dkn_s)
        dkp_s[...] = jnp.zeros_like(dkp_s)
        dv_s[...] = jnp.zeros_like(dv_s)
        for qs in range(0, t, tq):
            ke = qs + tq
            p = _attn_probs(qn[qs:ke], qp[qs:ke], kn[:ke], kp[:ke], qs)
            dp = _dot_nt(do[qs:ke], vv[:ke])
            ds = (p * (dp - jnp.sum(p * dp, axis=-1, keepdims=True)) * ATT_SCALE).astype(BF16)
            dqn_s[qs:ke, :] = _dot(ds, kn[:ke])
            dqp_s[qs:ke, :] = _dot(ds, kp[:ke])
            dkn_s[0:ke, :] += _dot_tn(ds, qn[qs:ke])
            dkp_s[0:ke, :] += _dot_tn(ds, qp[qs:ke])
            dv_s[0:ke, :] += _dot_tn(p.astype(BF16), do[qs:ke])
        dqn = dqn_s[...].astype(BF16)
        dqp = _rope_t(dqp_s[...], c, s).astype(BF16)
        dkn = dkn_s[...].astype(BF16)
        dvv = dv_s[...].astype(BF16)
        dwq_ref[:, :NOPE] = _dot_tn(cqv, dqn).astype(BF16)
        dwq_ref[:, NOPE:] = _dot_tn(cqv, dqp).astype(BF16)
        dwkv_ref[:, :NOPE] = _dot_tn(ckvv, dkn).astype(BF16)
        dwkv_ref[:, NOPE:] = _dot_tn(ckvv, dvv).astype(BF16)
        dcq = _dot_nt(dqn, wq_ref[:, :NOPE]) + _dot_nt(dqp, wq_ref[:, NOPE:])
        dckv = _dot_nt(dkn, wkv_ref[:, :NOPE]) + _dot_nt(dvv, wkv_ref[:, NOPE:])

        @pl.when(pl.program_id(0) == 0)
        def _():
            dcq_ref[...] = dcq
            dckv_ref[...] = dckv
            dkpe_ref[...] = dkp_s[...]

        @pl.when(pl.program_id(0) > 0)
        def _():
            dcq_ref[...] += dcq
            dckv_ref[...] += dckv
            dkpe_ref[...] += dkp_s[...]

    per_head = lambda a, b: pl.BlockSpec((None, a, b), lambda h: (h, 0, 0))
    return pl.pallas_call(
        body, grid=(MLA_HEADS,),
        in_specs=_attn_in_specs(t, j) + [per_head(t, VDIM), _full((t, D))],
        out_specs=[per_head(VDIM, D), per_head(Q_RANK, NOPE + ROPE), per_head(KV_RANK, NOPE + VDIM),
                   _full((t, Q_RANK)), _full((t, KV_RANK)), _full((t, ROPE))],
        out_shape=[S((MLA_HEADS, VDIM, D), BF16), S((MLA_HEADS, Q_RANK, NOPE + ROPE), BF16),
                   S((MLA_HEADS, KV_RANK, NOPE + VDIM), BF16),
                   S((t, Q_RANK), F32), S((t, KV_RANK), F32), S((t, ROPE), F32)],
        scratch_shapes=[pltpu.VMEM((t, NOPE), F32), pltpu.VMEM((t, ROPE), F32), pltpu.VMEM((t, VDIM), F32),
                        pltpu.VMEM((t, NOPE), F32), pltpu.VMEM((t, ROPE), F32)],
        compiler_params=_cp("arbitrary"), name="attn_bwd")(cq, ckv, kpe, cos, sin, wqb_g, wkvb_g, wo_g, o, dzb)


def _rms_bwd(down, dcq, dckv, dkpe, cos, sin, gq3, gkv3, j):
    t = down.shape[0]
    bm = _row_tile(t)

    def body(down_ref, dcq_ref, dckv_ref, dkpe_ref, c_ref, s_ref, gq_ref, gkv_ref, dd_ref, dgq_ref, dgkv_ref):
        @pl.when(pl.program_id(0) == 0)
        def _():
            dgq_ref[...] = jnp.zeros_like(dgq_ref)
            dgkv_ref[...] = jnp.zeros_like(dgkv_ref)

        def rms_b(x, dy, g):
            rstd = lax.rsqrt(jnp.mean(x * x, axis=-1, keepdims=True) + RMS_EPS)
            xh = x * rstd
            dyg = dy * g
            return rstd * (dyg - xh * jnp.mean(dyg * xh, axis=-1, keepdims=True)), jnp.sum(dy * xh, axis=0, keepdims=True)

        dq, dgq = rms_b(down_ref[:, :Q_RANK], dcq_ref[...], gq_ref[...])
        dkv, dgkv = rms_b(down_ref[:, Q_RANK:Q_RANK + KV_RANK], dckv_ref[...], gkv_ref[...])
        dgq_ref[...] += dgq
        dgkv_ref[...] += dgkv
        dd_ref[:, :Q_RANK] = dq.astype(BF16)
        dd_ref[:, Q_RANK:Q_RANK + KV_RANK] = dkv.astype(BF16)
        dd_ref[:, Q_RANK + KV_RANK:] = _rope_t(dkpe_ref[...], c_ref[...], s_ref[...]).astype(BF16)

    row = lambda n: pl.BlockSpec((bm, n), lambda i: (i, 0))
    return pl.pallas_call(
        body, grid=(t // bm,),
        in_specs=[row(ODD_IN), row(Q_RANK), row(KV_RANK), row(ROPE), row(ROPE // 2), row(ROPE // 2),
                  pl.BlockSpec((None, 1, Q_RANK), lambda i: (j, 0, 0)),
                  pl.BlockSpec((None, 1, KV_RANK), lambda i: (j, 0, 0))],
        out_specs=[row(ODD_IN), _full((1, Q_RANK)), _full((1, KV_RANK))],
        out_shape=[S((t, ODD_IN), BF16), S((1, Q_RANK), F32), S((1, KV_RANK), F32)],
        compiler_params=_cp("arbitrary"), name="rms_bwd")(down, dcq, dckv, dkpe, cos, sin, gq3, gkv3)


def _col_blocks(t, n, bn):
    return pl.BlockSpec((t, bn), lambda i: (0, i))


def _row_blocks(n, bm):
    return pl.BlockSpec((bm, n), lambda i: (i, 0))


def _local_step(x, pos2, tgt, big, small):
    t = x.shape[0]
    bm = _row_tile(t)
    inv_freq = (ROPE_THETA ** (-jnp.arange(0, ROPE, 2, dtype=F32) / ROPE)).reshape(1, ROPE // 2)
    cos, sin = _rope_tables(pos2, inv_freq)
    lru_p = {k: small[k] for k in ("conv_w", "conv_b", "w_a", "b_a", "w_x", "b_x", "lam")}

    saved = []
    y, yb = x, x.astype(BF16)
    for l in range(DEPTH):
        j = l // 2
        sv = dict(xb=yb)
        if l % 2 == 0:
            proj = _mm(yb, big["win2d"][j], mode="nn", grid=(EVEN_IN // 512,), a_spec=_full((t, D)),
                       b_spec=_col_blocks(D, EVEN_IN, 512), out_shape=S((t, EVEN_IN), F32),
                       out_spec=_col_blocks(t, EVEN_IN, 512), name="even_proj")
            ycat = jnp.concatenate([_pool_fwd(proj, small["pool_w"], small["pool_scale"], j),
                                    _lru_fwd(proj, lru_p, j)], axis=1)
            mix = _mm(ycat, big["wout2d"][j], mode="nn", grid=(D // 512,), a_spec=_full((t, EVEN_MIX)),
                      b_spec=_col_blocks(EVEN_MIX, D, 512), out_shape=S((t, D), F32),
                      out_spec=_col_blocks(t, D, 512), name="even_out")
            sv.update(proj=proj, ycat=ycat)
        else:
            down, cq, ckv, kpe = _down_norm(yb, big["wdown"], small["gq"], small["gkv"], cos, sin, j)
            o, mix = _attn_fwd(cq, ckv, kpe, cos, sin, big["wqb"], big["wkvb"], big["wo"], j)
            sv.update(down=down, cq=cq, ckv=ckv, kpe=kpe, o=o)
        z1, y1, y1b = _resid_ln(y, mix, small["ln_mix_g"], small["ln_mix_b"], l, "resid_ln")
        ff = _mlp_fwd(y1b, big["w1"], big["w2"], l)
        z2, y, yb = _resid_ln(y1, ff, small["ln_ffn_g"], small["ln_ffn_b"], l, "resid_ln")
        sv.update(z1=z1, y1b=y1b, z2=z2)
        saved.append(sv)

    dy, loss_tile = _loss_grad(y, tgt)

    g = {k: [None] * n for k, n in (("ln_mix_g", 4), ("ln_mix_b", 4), ("ln_ffn_g", 4), ("ln_ffn_b", 4),
                                    ("win", 2), ("pool_w", 2), ("pool_scale", 2), ("conv_w", 2), ("conv_b", 2),
                                    ("w_a", 2), ("b_a", 2), ("w_x", 2), ("b_x", 2), ("lam", 2), ("wout", 2),
                                    ("wdown", 2), ("gq", 2), ("gkv", 2), ("wqb", 2), ("wkvb", 2), ("wo", 2),
                                    ("w1", 4), ("w2", 4))}
    for l in reversed(range(DEPTH)):
        j = l // 2
        sv = saved[l]
        dz2, dz2b, g["ln_ffn_g"][l], g["ln_ffn_b"][l] = _ln_bwd(dy, sv["z2"], small["ln_ffn_g"], l, "ln_bwd")
        act, dh, dff = _mlp_bwd_dh(sv["y1b"], dz2b, big["w1"], big["w2"], l)
        g["w1"][l] = _mm(sv["y1b"], dh, mode="tn", grid=(N_DEV,), a_spec=_full((t, D)),
                         b_spec=_col_blocks(t, D_FF, FF_BLK), out_shape=S((N_DEV, D, FF_BLK), BF16),
                         out_spec=pl.BlockSpec((None, D, FF_BLK), lambda i: (i, 0, 0)), name="mlp_dw1")
        g["w2"][l] = _mm(act, dz2b, mode="tn", grid=(N_DEV,), a_spec=_col_blocks(t, D_FF, FF_BLK),
                         b_spec=_full((t, D)), out_shape=S((N_DEV, FF_BLK, D), BF16),
                         out_spec=pl.BlockSpec((None, FF_BLK, D), lambda i: (i, 0, 0)), name="mlp_dw2")
        dz1, dz1b, g["ln_mix_g"][l], g["ln_mix_b"][l] = _ln_bwd(dff, sv["z1"], small["ln_mix_g"], l, "ln_bwd_res",
                                                                 r=dz2)
        if l % 2 == 0:
            wout = big["wout2d"][j]
            dycat = _mm(dz1b, wout, mode="nt", grid=(EVEN_MIX // 512,), a_spec=_full((t, D)),
                        b_spec=_row_blocks(D, 512), out_shape=S((t, EVEN_MIX), F32),
                        out_spec=_col_blocks(t, EVEN_MIX, 512), name="even_dycat")
            dwout = _mm(sv["ycat"], dz1b, mode="tn", grid=(EVEN_MIX // 512,), a_spec=_col_blocks(t, EVEN_MIX, 512),
                        b_spec=_full((t, D)), out_shape=S((EVEN_MIX, D), BF16), out_spec=_row_blocks(D, 512),
                        name="even_dwout")
            g["wout"][j] = dwout.reshape(N_DEV, EVEN_MIX // N_DEV, D)
            du_pool, g["pool_w"][j], g["pool_scale"][j] = _pool_bwd(sv["proj"], dycat, small["pool_w"],
                                                                   small["pool_scale"], j)
            (du_lru, du_gate, g["conv_w"][j], g["conv_b"][j], g["w_a"][j], g["b_a"][j], g["w_x"][j], g["b_x"][j],
             g["lam"][j]) = _lru_bwd(sv["proj"], dycat, lru_p, j)
            dproj = jnp.concatenate([du_pool, du_lru, du_gate], axis=1)
            dwin = _mm(sv["xb"], dproj, mode="tn", grid=(EVEN_IN // 512,), a_spec=_full((t, D)),
                       b_spec=_col_blocks(t, EVEN_IN, 512), out_shape=S((D, EVEN_IN), BF16),
                       out_spec=_col_blocks(D, EVEN_IN, 512), name="even_dwin")
            g["win"][j] = dwin.reshape(D, N_DEV, EVEN_IN // N_DEV).transpose(1, 0, 2)
            dy = _mm(dproj, big["win2d"][j], mode="nt", grid=(t // bm,), a_spec=_row_blocks(EVEN_IN, bm),
                     b_spec=_full((D, EVEN_IN)), out_shape=S((t, D), F32), out_spec=_row_blocks(D, bm),
                     add=dz1, add_spec=_row_blocks(D, bm), add_scale=ALPHA, name="even_dx")
        else:
            g["wo"][j], g["wqb"][j], g["wkvb"][j], dcq, dckv, dkpe = _attn_bwd(
                sv["cq"], sv["ckv"], sv["kpe"], cos, sin, big["wqb"], big["wkvb"], big["wo"], sv["o"], dz1b, j)
            ddown, g["gq"][j], g["gkv"][j] = _rms_bwd(sv["down"], dcq, dckv, dkpe, cos, sin, small["gq"],
                                                     small["gkv"], j)
            g["wdown"][j] = _mm(sv["xb"], ddown, mode="tn", grid=(N_DEV,), a_spec=_col_blocks(t, D, D // N_DEV),
                                b_spec=_full((t, ODD_IN)), out_shape=S((N_DEV, D // N_DEV, ODD_IN), BF16),
                                out_spec=pl.BlockSpec((None, D // N_DEV, ODD_IN), lambda i: (i, 0, 0)),
                                name="odd_dwdown")
            dy = _mm(ddown, big["wdown2d"][j], mode="nt", grid=(t // bm,), a_spec=_row_blocks(ODD_IN, bm),
                     b_spec=_full((D, ODD_IN)), out_shape=S((t, D), F32), out_spec=_row_blocks(D, bm),
                     add=dz1, add_spec=_row_blocks(D, bm), add_scale=ALPHA, name="odd_dx")
    return loss_tile[0, 0], dy, g


def _mesh_place():
    x, y, c = lax.axis_index("x"), lax.axis_index("y"), lax.axis_index("c")
    return x, y, c


def _peer(place, k):
    x, y, c = place
    return (1 - x if k & 4 else x, 1 - y if k & 2 else y, 1 - c if k & 1 else c)


def _index(place):
    x, y, c = place
    return 4 * x + 2 * y + c


ANY = pl.BlockSpec(memory_space=pl.ANY)


def _all_gather_big(shards):
    n = len(shards)

    def body(*refs):
        ins, outs = refs[:n], refs[n:2 * n]
        send, recv, local = refs[2 * n:]
        place = _mesh_place()
        me = _index(place)
        copies = []
        for w in range(n):
            cp = pltpu.make_async_copy(ins[w], outs[w].at[me], local.at[w])
            cp.start()
            copies.append(cp)
        for k in range(1, N_DEV):
            peer = _peer(place, k)
            for w in range(n):
                cp = pltpu.make_async_remote_copy(src_ref=ins[w], dst_ref=outs[w].at[me], send_sem=send.at[w, k - 1],
                                                  recv_sem=recv.at[w, k - 1], device_id=peer, device_id_type=MESH)
                cp.start()
                copies.append(cp)
        for cp in copies:
            cp.wait()

    return pl.pallas_call(
        body, in_specs=[ANY] * n, out_specs=[ANY] * n,
        out_shape=[S((N_DEV,) + s.shape, s.dtype) for s in shards],
        scratch_shapes=[pltpu.SemaphoreType.DMA((n, N_DEV - 1)), pltpu.SemaphoreType.DMA((n, N_DEV - 1)),
                        pltpu.SemaphoreType.DMA((n,))],
        compiler_params=pltpu.CompilerParams(has_side_effects=True), name="all_gather_big")(*shards)


def _reduce_scatter_big(grads):
    n = len(grads)

    def body(*refs):
        ins, outs = refs[:n], refs[n:2 * n]
        send, recv, local = refs[2 * n:]
        place = _mesh_place()
        me = _index(place)
        copies = []
        for w in range(n):
            cp = pltpu.make_async_copy(ins[w].at[me], outs[w].at[me], local.at[w])
            cp.start()
            copies.append(cp)
        for k in range(1, N_DEV):
            peer = _peer(place, k)
            for w in range(n):
                cp = pltpu.make_async_remote_copy(src_ref=ins[w].at[_index(peer)], dst_ref=outs[w].at[me],
                                                  send_sem=send.at[w, k - 1], recv_sem=recv.at[w, k - 1],
                                                  device_id=peer, device_id_type=MESH)
                cp.start()
                copies.append(cp)
        for cp in copies:
            cp.wait()

    return pl.pallas_call(
        body, in_specs=[ANY] * n, out_specs=[ANY] * n,
        out_shape=[S(gr.shape, gr.dtype) for gr in grads],
        scratch_shapes=[pltpu.SemaphoreType.DMA((n, N_DEV - 1)), pltpu.SemaphoreType.DMA((n, N_DEV - 1)),
                        pltpu.SemaphoreType.DMA((n,))],
        compiler_params=pltpu.CompilerParams(has_side_effects=True), name="reduce_scatter_big")(*grads)


def _all_reduce_small(part, name):
    r = part.shape[1]

    def body(p_ref, o_ref, rbuf, send1, recv1, send2, recv2):
        place = _mesh_place()
        me = _index(place)
        rbuf[pl.ds(me, 1)] = p_ref[pl.ds(me, 1)]
        first = [pltpu.make_async_remote_copy(src_ref=p_ref.at[_index(_peer(place, k))], dst_ref=rbuf.at[me],
                                              send_sem=send1.at[k - 1], recv_sem=recv1.at[k - 1],
                                              device_id=_peer(place, k), device_id_type=MESH)
                 for k in range(1, N_DEV)]
        for cp in first:
            cp.start()
        for cp in first:
            cp.wait()
        acc = rbuf[0]
        for d in range(1, N_DEV):
            acc = acc + rbuf[d]
        o_ref[pl.ds(me, 1)] = acc[None]
        second = [pltpu.make_async_remote_copy(src_ref=o_ref.at[me], dst_ref=o_ref.at[me], send_sem=send2.at[k - 1],
                                               recv_sem=recv2.at[k - 1], device_id=_peer(place, k),
                                               device_id_type=MESH)
                  for k in range(1, N_DEV)]
        for cp in second:
            cp.start()
        for cp in second:
            cp.wait()

    vm = pl.BlockSpec(memory_space=pltpu.VMEM)
    return pl.pallas_call(
        body, in_specs=[vm], out_specs=vm, out_shape=S(part.shape, F32),
        scratch_shapes=[pltpu.VMEM(part.shape, F32)] + [pltpu.SemaphoreType.DMA((N_DEV - 1,))] * 4,
        compiler_params=pltpu.CompilerParams(has_side_effects=True, vmem_limit_bytes=VMEM_LIMIT), name=name)(part)


def _adamw(w, g, m, v):
    m = ADAM_B1 * m + (1.0 - ADAM_B1) * g
    v = ADAM_B2 * v + (1.0 - ADAM_B2) * (g * g)
    m_hat = m / (1.0 - ADAM_B1 ** ADAM_STEP)
    v_hat = v / (1.0 - ADAM_B2 ** ADAM_STEP)
    return -ADAM_LR * (m_hat / (jnp.sqrt(v_hat) + ADAM_EPS) + ADAM_WD * w), m, v


def _adam_big(parts, w, m, v, name):
    nl, a, b = w.shape
    ta = max(d for d in range(16, 257, 16) if a % d == 0)

    def body(p_ref, w_ref, m_ref, v_ref, g_ref, d_ref, mo_ref, vo_ref):
        g = p_ref[0].astype(F32)
        for s in range(1, N_DEV):
            g = g + p_ref[s].astype(F32)
        g_ref[...] = g
        d_ref[...], mo_ref[...], vo_ref[...] = _adamw(w_ref[...], g, m_ref[...], v_ref[...])

    blk = pl.BlockSpec((None, ta, b), lambda l, i: (l, i, 0))
    return pl.pallas_call(
        body, grid=(nl, a // ta), in_specs=[pl.BlockSpec((N_DEV, None, ta, b), lambda l, i: (0, l, i, 0)), blk, blk, blk],
        out_specs=[blk] * 4, out_shape=[S(w.shape, F32)] * 4, compiler_params=_cp("parallel", "parallel"),
        name=name)(parts, w, m, v)


def _adam_small(g, w, m, v):
    rows = g.shape[0]
    tr = 256

    def body(g_ref, w_ref, m_ref, v_ref, d_ref, mo_ref, vo_ref):
        d_ref[...], mo_ref[...], vo_ref[...] = _adamw(w_ref[...], g_ref[...], m_ref[...], v_ref[...])

    blk = pl.BlockSpec((tr, 128), lambda i: (i, 0))
    return pl.pallas_call(body, grid=(rows // tr,), in_specs=[blk] * 4, out_specs=[blk] * 3,
                          out_shape=[S(g.shape, F32)] * 3, compiler_params=_cp("parallel"), name="adam_small")(g, w, m, v)


BIG = ("even_w_in", "even_w_out", "mla_w_down", "mla_w_qb", "mla_w_kvb", "mla_w_o", "mlp_w1", "mlp_w2")
BIG_KEY = dict(even_w_in="win", even_w_out="wout", mla_w_down="wdown", mla_w_qb="wqb", mla_w_kvb="wkvb",
               mla_w_o="wo", mlp_w1="w1", mlp_w2="w2")
SMALL = (("ln_mix_g", "ln_mix_g", None), ("ln_mix_b", "ln_mix_b", None), ("ln_ffn_g", "ln_ffn_g", None),
         ("ln_ffn_b", "ln_ffn_b", None), ("pool_w", "pool_w", None), ("pool_scale", "pool_scale", None),
         ("lru_conv_w", "conv_w", 2), ("lru_conv_b", "conv_b", None), ("lru_w_a", "w_a", None),
         ("lru_b_a", "b_a", None), ("lru_w_x", "w_x", None), ("lru_b_x", "b_x", None), ("lru_lambda", "lam", None),
         ("mla_q_norm_g", "gq", 1), ("mla_kv_norm_g", "gkv", 1))
WEIGHTS = ("ln_mix_g", "ln_mix_b", "ln_ffn_g", "ln_ffn_b", "even_w_in", "pool_w", "pool_scale", "lru_conv_w",
           "lru_conv_b", "lru_w_a", "lru_b_a", "lru_w_x", "lru_b_x", "lru_lambda", "even_w_out", "mla_w_down",
           "mla_q_norm_g", "mla_kv_norm_g", "mla_w_qb", "mla_w_kvb", "mla_w_o", "mlp_w1", "mlp_w2")
ALL_AXES = ("x", "y", "c")


def _pack(arrays, multiple):
    flat = jnp.concatenate([a.reshape(-1) for a in arrays])
    pad = (-flat.shape[0]) % multiple
    return jnp.pad(flat, (0, pad))


def _unpack(flat, shapes):
    out, at = [], 0
    for shp in shapes:
        n = 1
        for s in shp:
            n *= s
        out.append(flat[at:at + n].reshape(shp))
        at += n
    return out


def _global_shape(local_shape, axis):
    if axis is None:
        return tuple(local_shape)
    return tuple(s * N_DEV if i == axis else s for i, s in enumerate(local_shape))


def _step(x, positions, tgt, w, m, v):
    t = x.shape[1]
    me = _index(_mesh_place())

    sharded = [(name, axis) for name, _, axis in SMALL if axis is not None]
    zeros_with_mine = [lax.dynamic_update_slice_in_dim(jnp.zeros(_global_shape(w[name].shape, axis), F32), w[name],
                                                       me * w[name].shape[axis], axis) for name, axis in sharded]
    chunk = N_DEV * 8 * 128
    gathered = _all_reduce_small(_pack(zeros_with_mine, chunk).reshape(N_DEV, -1, 128), "gather_small")
    full = dict(zip([name for name, _ in sharded],
                    _unpack(gathered.reshape(-1), [_global_shape(w[name].shape, axis) for name, axis in sharded])))

    gathered_big = _all_gather_big([w[name].astype(BF16) for name in BIG])
    big = {BIG_KEY[name]: a for name, a in zip(BIG, gathered_big)}
    big["win2d"] = [big["win"][:, j].transpose(1, 0, 2).reshape(D, EVEN_IN) for j in range(2)]
    big["wout2d"] = [big["wout"][:, j].reshape(EVEN_MIX, D) for j in range(2)]
    big["wdown2d"] = [big["wdown"][:, j].reshape(D, ODD_IN) for j in range(2)]

    row3 = lambda a: a.reshape(a.shape[0], 1, a.shape[1])
    small = dict(ln_mix_g=row3(w["ln_mix_g"]), ln_mix_b=row3(w["ln_mix_b"]), ln_ffn_g=row3(w["ln_ffn_g"]),
                 ln_ffn_b=row3(w["ln_ffn_b"]), pool_w=w["pool_w"], pool_scale=row3(w["pool_scale"]),
                 conv_w=full["lru_conv_w"], conv_b=row3(w["lru_conv_b"]), w_a=w["lru_w_a"], b_a=row3(w["lru_b_a"]),
                 w_x=w["lru_w_x"], b_x=row3(w["lru_b_x"]), lam=row3(w["lru_lambda"]),
                 gq=row3(full["mla_q_norm_g"]), gkv=row3(full["mla_kv_norm_g"]))

    loss_part, grad_x, g = _local_step(x[0], positions.reshape(t, 1), tgt[0], big, small)
    loss = lax.psum(loss_part, ALL_AXES)

    parts = _reduce_scatter_big([jnp.stack(g[BIG_KEY[name]], axis=1) for name in BIG])
    out = {}
    for name, p in zip(BIG, parts):
        out[name] = _adam_big(p, w[name], m[name], v[name], "adam_" + name)

    local_g = [jnp.stack(g[key]).reshape(_global_shape(w[name].shape, axis)) for name, key, axis in SMALL]
    reduced = _all_reduce_small(_pack(local_g, chunk).reshape(N_DEV, -1, 128), "all_reduce_small")
    reduced = _unpack(reduced.reshape(-1), [a.shape for a in local_g])
    mine = [a if axis is None else lax.dynamic_slice_in_dim(a, me * w[name].shape[axis], w[name].shape[axis], axis)
            for a, (name, _, axis) in zip(reduced, SMALL)]
    tile = 256 * 128
    packed = [_pack(arrs, tile).reshape(-1, 128)
              for arrs in (mine, [w[n] for n, _, _ in SMALL], [m[n] for n, _, _ in SMALL], [v[n] for n, _, _ in SMALL])]
    shapes = [w[n].shape for n, _, _ in SMALL]
    d_s, m_s, v_s = (_unpack(a.reshape(-1), shapes) for a in _adam_small(*packed))
    for i, (name, _, _) in enumerate(SMALL):
        out[name] = (mine[i], d_s[i], m_s[i], v_s[i])

    return (loss, grad_x[None]) + tuple(out[name][i] for i in range(4) for name in WEIGHTS)


def kernel(x, positions, ln_mix_g, ln_mix_b, ln_ffn_g, ln_ffn_b, even_w_in, pool_w, pool_scale, lru_conv_w, lru_conv_b, lru_w_a, lru_b_a, lru_w_x, lru_b_x, lru_lambda, even_w_out, mla_w_down, mla_q_norm_g, mla_kv_norm_g, mla_w_qb, mla_w_kvb, mla_w_o, mlp_w1, mlp_w2, loss_target, m_ln_mix_g, m_ln_mix_b, m_ln_ffn_g, m_ln_ffn_b, m_even_w_in, m_pool_w, m_pool_scale, m_lru_conv_w, m_lru_conv_b, m_lru_w_a, m_lru_b_a, m_lru_w_x, m_lru_b_x, m_lru_lambda, m_even_w_out, m_mla_w_down, m_mla_q_norm_g, m_mla_kv_norm_g, m_mla_w_qb, m_mla_w_kvb, m_mla_w_o, m_mlp_w1, m_mlp_w2, v_ln_mix_g, v_ln_mix_b, v_ln_ffn_g, v_ln_ffn_b, v_even_w_in, v_pool_w, v_pool_scale, v_lru_conv_w, v_lru_conv_b, v_lru_w_a, v_lru_b_a, v_lru_w_x, v_lru_b_x, v_lru_lambda, v_even_w_out, v_mla_w_down, v_mla_q_norm_g, v_mla_kv_norm_g, v_mla_w_qb, v_mla_w_kvb, v_mla_w_o, v_mlp_w1, v_mlp_w2):
    w = dict(zip(WEIGHTS, (ln_mix_g, ln_mix_b, ln_ffn_g, ln_ffn_b, even_w_in, pool_w, pool_scale, lru_conv_w,
                           lru_conv_b, lru_w_a, lru_b_a, lru_w_x, lru_b_x, lru_lambda, even_w_out, mla_w_down,
                           mla_q_norm_g, mla_kv_norm_g, mla_w_qb, mla_w_kvb, mla_w_o, mlp_w1, mlp_w2)))
    m = dict(zip(WEIGHTS, (m_ln_mix_g, m_ln_mix_b, m_ln_ffn_g, m_ln_ffn_b, m_even_w_in, m_pool_w, m_pool_scale,
                           m_lru_conv_w, m_lru_conv_b, m_lru_w_a, m_lru_b_a, m_lru_w_x, m_lru_b_x, m_lru_lambda,
                           m_even_w_out, m_mla_w_down, m_mla_q_norm_g, m_mla_kv_norm_g, m_mla_w_qb, m_mla_w_kvb,
                           m_mla_w_o, m_mlp_w1, m_mlp_w2)))
    v = dict(zip(WEIGHTS, (v_ln_mix_g, v_ln_mix_b, v_ln_ffn_g, v_ln_ffn_b, v_even_w_in, v_pool_w, v_pool_scale,
                           v_lru_conv_w, v_lru_conv_b, v_lru_w_a, v_lru_b_a, v_lru_w_x, v_lru_b_x, v_lru_lambda,
                           v_even_w_out, v_mla_w_down, v_mla_q_norm_g, v_mla_kv_norm_g, v_mla_w_qb, v_mla_w_kvb,
                           v_mla_w_o, v_mlp_w1, v_mlp_w2)))
    return _step(x, positions, loss_target, w, m, v)
```

```python
import functools

import jax
import jax.numpy as jnp
from jax import lax
from jax.experimental import pallas as pl
from jax.experimental.pallas import tpu as pltpu

F32 = jnp.float32
BF16 = jnp.bfloat16
S = jax.ShapeDtypeStruct

D = 1024
DEPTH = 4
N_DEV = 8
CHUNK_SHIFT = 6
POOL_WINDOWS = (2, 4, 8, 16)
POOL_W = 512
LRU_W = 1024
LRU_HEADS = 8
HEAD = 128
LRU_C = 8.0
EVEN_IN = 2560
EVEN_MIX = 1536
MLA_HEADS = 8
NOPE = 128
ROPE = 64
VDIM = 128
Q_RANK = 384
KV_RANK = 256
ODD_IN = 704
D_FF = 4096
FF_BLK = D_FF // N_DEV
ROPE_THETA = 10000.0
ALPHA = (2 * DEPTH) ** 0.25
LN_EPS = 1e-5
RMS_EPS = 1e-6
ATT_SCALE = (NOPE + ROPE) ** -0.5
NEG = float(jnp.finfo(jnp.float32).min)
ADAM_LR = 0.001
ADAM_B1 = 0.9
ADAM_B2 = 0.999
ADAM_EPS = 1e-08
ADAM_WD = 0.01
ADAM_STEP = 10
V7X_VMEM_BYTES = 64 * 1024 * 1024
VMEM_LIMIT = V7X_VMEM_BYTES - 8 * 1024 * 1024
MESH = pl.DeviceIdType.MESH


def _cp(*sem):
    return pltpu.CompilerParams(dimension_semantics=sem if sem else None, vmem_limit_bytes=VMEM_LIMIT)


def _dot(a, b):
    return jnp.dot(a, b, preferred_element_type=F32)


def _dot_nt(a, b):
    return lax.dot_general(a, b, (((1,), (1,)), ((), ())), preferred_element_type=F32)


def _dot_tn(a, b):
    return lax.dot_general(a, b, (((0,), (0,)), ((), ())), preferred_element_type=F32)


def _full(shape):
    return pl.BlockSpec(shape, lambda *_: (0,) * len(shape))


def _mm(a, b, *, mode, grid, a_spec, b_spec, out_shape, out_spec, name, add=None, add_spec=None, add_scale=1.0):
    dot = {"nn": _dot, "nt": _dot_nt, "tn": _dot_tn}[mode]

    def body(*refs):
        if add is None:
            a_ref, b_ref, o_ref = refs
            acc = dot(a_ref[...].astype(BF16), b_ref[...].astype(BF16))
        else:
            a_ref, b_ref, add_ref, o_ref = refs
            acc = dot(a_ref[...].astype(BF16), b_ref[...].astype(BF16)) + add_scale * add_ref[...]
        o_ref[...] = acc.astype(o_ref.dtype)

    ops = (a, b) if add is None else (a, b, add)
    specs = [a_spec, b_spec] if add is None else [a_spec, b_spec, add_spec]
    return pl.pallas_call(body, grid=grid, in_specs=specs, out_specs=out_spec, out_shape=out_shape,
                          compiler_params=_cp(*(("parallel",) * len(grid))), name=name)(*ops)


def _ln_stats(z):
    mu = jnp.mean(z, axis=-1, keepdims=True)
    zc = z - mu
    var = jnp.mean(zc * zc, axis=-1, keepdims=True)
    rstd = lax.rsqrt(var + LN_EPS)
    return zc * rstd, rstd


def _row_tile(t):
    return min(512, t)


def _resid_ln(x, mix, g3, b3, l, name):
    t = x.shape[0]
    bm = _row_tile(t)

    def body(x_ref, m_ref, g_ref, b_ref, z_ref, y_ref, yb_ref):
        z = ALPHA * x_ref[...] + m_ref[...]
        xh, _ = _ln_stats(z)
        y = xh * g_ref[...] + b_ref[...]
        z_ref[...] = z
        y_ref[...] = y
        yb_ref[...] = y.astype(BF16)

    row = pl.BlockSpec((bm, D), lambda i: (i, 0))
    vec = pl.BlockSpec((None, 1, D), lambda i: (l, 0, 0))
    return pl.pallas_call(body, grid=(t // bm,), in_specs=[row, row, vec, vec], out_specs=[row, row, row],
                          out_shape=[S((t, D), F32), S((t, D), F32), S((t, D), BF16)],
                          compiler_params=_cp("parallel"), name=name)(x, mix, g3, b3)


def _ln_bwd(d, z, g3, l, name, r=None):
    t = z.shape[0]
    bm = _row_tile(t)

    def body(*refs):
        if r is None:
            d_ref, z_ref, g_ref, dz_ref, dzb_ref, dg_ref, db_ref = refs
            dy = d_ref[...]
        else:
            d_ref, r_ref, z_ref, g_ref, dz_ref, dzb_ref, dg_ref, db_ref = refs
            dy = d_ref[...] + ALPHA * r_ref[...]
        xh, rstd = _ln_stats(z_ref[...])
        dyg = dy * g_ref[...]
        m1 = jnp.mean(dyg, axis=-1, keepdims=True)
        m2 = jnp.mean(dyg * xh, axis=-1, keepdims=True)
        dz = rstd * (dyg - m1 - xh * m2)
        dz_ref[...] = dz
        dzb_ref[...] = dz.astype(BF16)

        @pl.when(pl.program_id(0) == 0)
        def _():
            dg_ref[...] = jnp.zeros_like(dg_ref)
            db_ref[...] = jnp.zeros_like(db_ref)

        dg_ref[...] += jnp.sum(dy * xh, axis=0, keepdims=True)
        db_ref[...] += jnp.sum(dy, axis=0, keepdims=True)

    row = pl.BlockSpec((bm, D), lambda i: (i, 0))
    vec = pl.BlockSpec((None, 1, D), lambda i: (l, 0, 0))
    acc = pl.BlockSpec((1, D), lambda i: (0, 0))
    ops = (d, z, g3) if r is None else (d, r, z, g3)
    specs = [row, row, vec] if r is None else [row, row, row, vec]
    return pl.pallas_call(body, grid=(t // bm,), in_specs=specs, out_specs=[row, row, acc, acc],
                          out_shape=[S((t, D), F32), S((t, D), BF16), S((1, D), F32), S((1, D), F32)],
                          compiler_params=_cp("arbitrary"), name=name)(*ops)


def _loss_grad(y, tgt):
    t = y.shape[0]
    bm = _row_tile(t)

    def body(y_ref, t_ref, dy_ref, loss_ref, acc_ref):
        i = pl.program_id(0)
        e = y_ref[...] - t_ref[...]
        dy_ref[...] = e * (1.0 / D)

        @pl.when(i == 0)
        def _():
            acc_ref[...] = jnp.zeros_like(acc_ref)

        acc_ref[...] += jnp.sum(e * e, axis=0, keepdims=True)

        @pl.when(i == pl.num_programs(0) - 1)
        def _():
            loss_ref[...] = jnp.full(loss_ref.shape, (0.5 / D) * jnp.sum(acc_ref[...]), F32)

    row = pl.BlockSpec((bm, D), lambda i: (i, 0))
    return pl.pallas_call(body, grid=(t // bm,), in_specs=[row, row],
                          out_specs=[row, pl.BlockSpec((1, 128), lambda i: (0, 0))],
                          out_shape=[S((t, D), F32), S((1, 128), F32)],
                          scratch_shapes=[pltpu.VMEM((1, D), F32)],
                          compiler_params=_cp("arbitrary"), name="loss_grad")(y, tgt)


def _mlp_fwd(yb, w1g, w2g, l):
    t = yb.shape[0]
    bm = _row_tile(t)

    def body(y_ref, w1_ref, w2_ref, o_ref):
        j = pl.program_id(1)
        h = jnp.maximum(_dot(y_ref[...], w1_ref[...]), 0.0)
        c = _dot((h * h).astype(BF16), w2_ref[...])

        @pl.when(j == 0)
        def _():
            o_ref[...] = c

        @pl.when(j > 0)
        def _():
            o_ref[...] += c

    return pl.pallas_call(
        body, grid=(t // bm, N_DEV),
        in_specs=[pl.BlockSpec((bm, D), lambda i, j: (i, 0)),
                  pl.BlockSpec((None, None, D, FF_BLK), lambda i, j: (j, l, 0, 0)),
                  pl.BlockSpec((None, None, FF_BLK, D), lambda i, j: (j, l, 0, 0))],
        out_specs=pl.BlockSpec((bm, D), lambda i, j: (i, 0)),
        out_shape=S((t, D), F32), compiler_params=_cp("parallel", "arbitrary"), name="mlp_fwd")(yb, w1g, w2g)


def _mlp_bwd_dh(yb, dzb, w1g, w2g, l):
    t = yb.shape[0]
    bm = _row_tile(t)

    def body(y_ref, dz_ref, w1_ref, w2_ref, a_ref, dh_ref, acc_ref):
        j = pl.program_id(1)
        r = jnp.maximum(_dot(y_ref[...], w1_ref[...]), 0.0)
        a_ref[...] = (r * r).astype(BF16)
        da = _dot_nt(dz_ref[...], w2_ref[...])
        dh = (da * (2.0 * r)).astype(BF16)
        dh_ref[...] = dh
        c = _dot_nt(dh, w1_ref[...])

        @pl.when(j == 0)
        def _():
            acc_ref[...] = c

        @pl.when(j > 0)
        def _():
            acc_ref[...] += c

    row = pl.BlockSpec((bm, D), lambda i, j: (i, 0))
    hid = pl.BlockSpec((bm, FF_BLK), lambda i, j: (i, j))
    return pl.pallas_call(
        body, grid=(t // bm, N_DEV),
        in_specs=[row, row,
                  pl.BlockSpec((None, None, D, FF_BLK), lambda i, j: (j, l, 0, 0)),
                  pl.BlockSpec((None, None, FF_BLK, D), lambda i, j: (j, l, 0, 0))],
        out_specs=[hid, hid, row],
        out_shape=[S((t, D_FF), BF16), S((t, D_FF), BF16), S((t, D), F32)],
        compiler_params=_cp("parallel", "arbitrary"), name="mlp_bwd_dh")(yb, dzb, w1g, w2g)


def _shift_dn(x, k, rows, fill=0.0):
    return jnp.where(rows >= k, pltpu.roll(x, k, 0), fill)


def _shift_up(x, k, rows, fill=0.0):
    t = x.shape[0]
    return jnp.where(rows < t - k, pltpu.roll(x, t - k, 0), fill)


def _scan_dn(a, b, rows):
    k = 1
    t = a.shape[0]
    while k < t:
        b = a * _shift_dn(b, k, rows) + b
        if 2 * k < t:
            a = a * _shift_dn(a, k, rows, 1.0)
        k *= 2
    return b


def _scan_up(a, b, rows):
    k = 1
    t = a.shape[0]
    while k < t:
        b = a * _shift_up(b, k, rows) + b
        if 2 * k < t:
            a = a * _shift_up(a, k, rows, 1.0)
        k *= 2
    return b


def _window_sum_dn(x, w, rows):
    k = 1
    while k < w:
        x = x + _shift_dn(x, k, rows)
        k *= 2
    return x


def _window_sum_up(x, w, rows):
    k = 1
    while k < w:
        x = x + _shift_up(x, k, rows)
        k *= 2
    return x


def _pool_diff(u, w, rows):
    inv_count = 1.0 / jnp.minimum(rows + 1, w).astype(F32)
    return _window_sum_dn(u, w, rows) * inv_count - u, inv_count


def _pool_fwd(proj, pool_w, pool_scale3, j):
    t = proj.shape[0]

    def body(u_ref, w_ref, s_ref, y_ref):
        rows = lax.broadcasted_iota(jnp.int32, (t, HEAD), 0)
        for g, w in enumerate(POOL_WINDOWS):
            cols = slice(g * HEAD, (g + 1) * HEAD)
            d, _ = _pool_diff(u_ref[:, cols], w, rows)
            y = _dot(d.astype(BF16), w_ref[g].astype(BF16)) * s_ref[:, cols]
            y_ref[:, cols] = y.astype(BF16)

    return pl.pallas_call(
        body, grid=(1,),
        in_specs=[pl.BlockSpec((t, POOL_W), lambda i: (0, 0)),
                  pl.BlockSpec((None, 4, HEAD, HEAD), lambda i: (j, 0, 0, 0)),
                  pl.BlockSpec((None, 1, POOL_W), lambda i: (j, 0, 0))],
        out_specs=pl.BlockSpec((t, POOL_W), lambda i: (0, 0)),
        out_shape=S((t, POOL_W), BF16), compiler_params=_cp("arbitrary"), name="pool_fwd")(proj, pool_w, pool_scale3)


def _pool_bwd(proj, dycat, pool_w, pool_scale3, j):
    t = proj.shape[0]

    def body(u_ref, dy_ref, w_ref, s_ref, du_ref, dw_ref, ds_ref):
        rows = lax.broadcasted_iota(jnp.int32, (t, HEAD), 0)
        for g, w in enumerate(POOL_WINDOWS):
            cols = slice(g * HEAD, (g + 1) * HEAD)
            d, inv_count = _pool_diff(u_ref[:, cols], w, rows)
            db = d.astype(BF16)
            wg = w_ref[g].astype(BF16)
            dy = dy_ref[:, cols]
            ds_ref[:, cols] = jnp.sum(dy * _dot(db, wg), axis=0, keepdims=True)
            dzz = (dy * s_ref[:, cols]).astype(BF16)
            dw_ref[g] = _dot_tn(db, dzz)
            dd = _dot_nt(dzz, wg)
            du_ref[:, cols] = (_window_sum_up(dd * inv_count, w, rows) - dd).astype(BF16)

    return pl.pallas_call(
        body, grid=(1,),
        in_specs=[pl.BlockSpec((t, POOL_W), lambda i: (0, 0)),
                  pl.BlockSpec((t, POOL_W), lambda i: (0, 0)),
                  pl.BlockSpec((None, 4, HEAD, HEAD), lambda i: (j, 0, 0, 0)),
                  pl.BlockSpec((None, 1, POOL_W), lambda i: (j, 0, 0))],
        out_specs=[pl.BlockSpec((t, POOL_W), lambda i: (0, 0)), _full((4, HEAD, HEAD)), _full((1, POOL_W))],
        out_shape=[S((t, POOL_W), BF16), S((4, HEAD, HEAD), F32), S((1, POOL_W), F32)],
        compiler_params=_cp("arbitrary"), name="pool_bwd")(proj, dycat, pool_w, pool_scale3)


GELU_C = 0.7978845608028654
GELU_K = 0.044715


def _gelu(x):
    th = jnp.tanh(GELU_C * (x + GELU_K * x * x * x))
    return 0.5 * x * (1.0 + th), th


def _lru_forward(u, gate, cw, cb, wa, ba, wx, bx, lam, rows):
    v = cw[3:4] * u + cw[2:3] * _shift_dn(u, 1, rows) + cw[1:2] * _shift_dn(u, 2, rows) \
        + cw[0:1] * _shift_dn(u, 3, rows) + cb
    vb = v.astype(BF16)
    r = jax.nn.sigmoid(_dot(vb, wa) + ba)
    i = jax.nn.sigmoid(_dot(vb, wx) + bx)
    sp = jnp.maximum(-lam, 0.0) + jnp.log1p(jnp.exp(-jnp.abs(lam)))
    log_a = (-LRU_C) * r * sp
    a = jnp.exp(log_a)
    one_m_a2 = -jnp.tanh(log_a) * (a * a + 1.0)
    mult = jnp.sqrt(one_m_a2)
    h = _scan_dn(a, mult * (i * v), rows)
    gl, th = _gelu(gate)
    return dict(v=v, vb=vb, r=r, i=i, sp=sp, a=a, mult=mult, h=h, gl=gl, th=th)


def _lru_specs(t, j, col0_u, col0_g):
    blk = lambda c0: pl.BlockSpec((t, HEAD), lambda h: (0, c0 + h))
    vec = pl.BlockSpec((None, 1, HEAD), lambda h: (j, 0, h))
    return [blk(col0_u), blk(col0_g),
            pl.BlockSpec((None, 4, HEAD), lambda h: (j, 0, h)), vec,
            pl.BlockSpec((None, None, HEAD, HEAD), lambda h: (j, h, 0, 0)), vec,
            pl.BlockSpec((None, None, HEAD, HEAD), lambda h: (j, h, 0, 0)), vec, vec]


def _lru_fwd(proj, p, j):
    t = proj.shape[0]

    def body(u_ref, g_ref, cw_ref, cb_ref, wa_ref, ba_ref, wx_ref, bx_ref, lam_ref, y_ref):
        rows = lax.broadcasted_iota(jnp.int32, (t, HEAD), 0)
        f = _lru_forward(u_ref[...], g_ref[...], cw_ref[...], cb_ref[...], wa_ref[...].astype(BF16), ba_ref[...],
                         wx_ref[...].astype(BF16), bx_ref[...], lam_ref[...], rows)
        y_ref[...] = (f["h"] * f["gl"]).astype(BF16)

    return pl.pallas_call(
        body, grid=(LRU_HEADS,), in_specs=_lru_specs(t, j, POOL_W // HEAD, (POOL_W + LRU_W) // HEAD),
        out_specs=pl.BlockSpec((t, HEAD), lambda h: (0, h)), out_shape=S((t, LRU_W), BF16),
        compiler_params=_cp("parallel"), name="lru_fwd")(
            proj, proj, p["conv_w"], p["conv_b"], p["w_a"], p["b_a"], p["w_x"], p["b_x"], p["lam"])


def _lru_bwd(proj, dycat, p, j):
    t = proj.shape[0]

    def body(u_ref, g_ref, cw_ref, cb_ref, wa_ref, ba_ref, wx_ref, bx_ref, lam_ref, dy_ref,
             du_ref, dgate_ref, dcw_ref, dcb_ref, dwa_ref, dba_ref, dwx_ref, dbx_ref, dlam_ref):
        rows = lax.broadcasted_iota(jnp.int32, (t, HEAD), 0)
        u = u_ref[...]
        gate = g_ref[...]
        cw = cw_ref[...]
        wa = wa_ref[...].astype(BF16)
        wx = wx_ref[...].astype(BF16)
        lam = lam_ref[...]
        f = _lru_forward(u, gate, cw, cb_ref[...], wa, ba_ref[...], wx, bx_ref[...], lam, rows)
        v, r, i, a, mult, h, th = f["v"], f["r"], f["i"], f["a"], f["mult"], f["h"], f["th"]
        dy = dy_ref[...]
        dgl = 0.5 * (1.0 + th) + 0.5 * gate * (1.0 - th * th) * GELU_C * (1.0 + 3.0 * GELU_K * gate * gate)
        dgate_ref[...] = (dy * h * dgl).astype(BF16)
        g = _scan_up(_shift_up(a, 1, rows), dy * f["gl"], rows)
        da = g * _shift_dn(h, 1, rows)
        iv = i * v
        dmult = g * iv
        di = g * mult * v
        dv = g * mult * i
        dlog_a = da * a - dmult * (a * a) / mult
        dr = dlog_a * (-LRU_C) * f["sp"]
        dsp = jnp.sum(dlog_a * (-LRU_C) * r, axis=0, keepdims=True)
        dlam_ref[...] = -dsp * jax.nn.sigmoid(-lam)
        dpa = dr * r * (1.0 - r)
        dpx = di * i * (1.0 - i)
        dpab = dpa.astype(BF16)
        dpxb = dpx.astype(BF16)
        dwa_ref[...] = _dot_tn(f["vb"], dpab)
        dwx_ref[...] = _dot_tn(f["vb"], dpxb)
        dba_ref[...] = jnp.sum(dpa, axis=0, keepdims=True)
        dbx_ref[...] = jnp.sum(dpx, axis=0, keepdims=True)
        dv = dv + _dot_nt(dpab, wa) + _dot_nt(dpxb, wx)
        dcb_ref[...] = jnp.sum(dv, axis=0, keepdims=True)
        du = cw[3:4] * dv
        dcw_ref[3:4, :] = jnp.sum(dv * u, axis=0, keepdims=True)
        for k in (1, 2, 3):
            du = du + cw[3 - k:4 - k] * _shift_up(dv, k, rows)
            dcw_ref[3 - k:4 - k, :] = jnp.sum(dv * _shift_dn(u, k, rows), axis=0, keepdims=True)
        du_ref[...] = du.astype(BF16)

    blk = pl.BlockSpec((t, HEAD), lambda h: (0, h))
    vec = pl.BlockSpec((1, HEAD), lambda h: (0, h))
    mat = pl.BlockSpec((None, HEAD, HEAD), lambda h: (h, 0, 0))
    return pl.pallas_call(
        body, grid=(LRU_HEADS,),
        in_specs=_lru_specs(t, j, POOL_W // HEAD, (POOL_W + LRU_W) // HEAD)
        + [pl.BlockSpec((t, HEAD), lambda h: (0, POOL_W // HEAD + h))],
        out_specs=[blk, blk, pl.BlockSpec((4, HEAD), lambda h: (0, h)), vec, mat, vec, mat, vec, vec],
        out_shape=[S((t, LRU_W), BF16), S((t, LRU_W), BF16), S((4, LRU_W), F32), S((1, LRU_W), F32),
                   S((LRU_HEADS, HEAD, HEAD), F32), S((1, LRU_W), F32),
                   S((LRU_HEADS, HEAD, HEAD), F32), S((1, LRU_W), F32), S((1, LRU_W), F32)],
        compiler_params=_cp("parallel"), name="lru_bwd")(
            proj, proj, p["conv_w"], p["conv_b"], p["w_a"], p["b_a"], p["w_x"], p["b_x"], p["lam"], dycat)


def _rope(x, c, s):
    x1 = x[:, :ROPE // 2]
    x2 = x[:, ROPE // 2:]
    return jnp.concatenate([x1 * c - x2 * s, x1 * s + x2 * c], axis=-1)


def _rope_t(d, c, s):
    d1 = d[:, :ROPE // 2]
    d2 = d[:, ROPE // 2:]
    return jnp.concatenate([d1 * c + d2 * s, d2 * c - d1 * s], axis=-1)


def _rope_tables(pos2, inv_freq):
    t = pos2.shape[0]

    def body(p_ref, f_ref, c_ref, s_ref):
        ang = p_ref[...].astype(F32) * f_ref[...]
        c_ref[...] = jnp.cos(ang)
        s_ref[...] = jnp.sin(ang)

    return pl.pallas_call(body, out_shape=[S((t, ROPE // 2), F32), S((t, ROPE // 2), F32)],
                          name="rope_tables")(pos2, inv_freq)


def _down_norm(xb, wdown_g, gq3, gkv3, cos, sin, j):
    t = xb.shape[0]
    bm = _row_tile(t)

    def body(x_ref, w_ref, gq_ref, gkv_ref, c_ref, s_ref, down_ref, cq_ref, ckv_ref, kpe_ref):
        w = w_ref[...].reshape(D, ODD_IN)
        down = _dot(x_ref[...], w)
        down_ref[...] = down
        q = down[:, :Q_RANK]
        cq_ref[...] = (q * lax.rsqrt(jnp.mean(q * q, axis=-1, keepdims=True) + RMS_EPS) * gq_ref[...]).astype(BF16)
        kv = down[:, Q_RANK:Q_RANK + KV_RANK]
        ckv_ref[...] = (kv * lax.rsqrt(jnp.mean(kv * kv, axis=-1, keepdims=True) + RMS_EPS)
                        * gkv_ref[...]).astype(BF16)
        kpe_ref[...] = _rope(down[:, Q_RANK + KV_RANK:], c_ref[...], s_ref[...])

    row = lambda n: pl.BlockSpec((bm, n), lambda i: (i, 0))
    return pl.pallas_call(
        body, grid=(t // bm,),
        in_specs=[row(D), pl.BlockSpec((N_DEV, None, D // N_DEV, ODD_IN), lambda i: (0, j, 0, 0)),
                  pl.BlockSpec((None, 1, Q_RANK), lambda i: (j, 0, 0)),
                  pl.BlockSpec((None, 1, KV_RANK), lambda i: (j, 0, 0)), row(ROPE // 2), row(ROPE // 2)],
        out_specs=[row(ODD_IN), row(Q_RANK), row(KV_RANK), row(ROPE)],
        out_shape=[S((t, ODD_IN), F32), S((t, Q_RANK), BF16), S((t, KV_RANK), BF16), S((t, ROPE), F32)],
        compiler_params=_cp("parallel"), name="down_norm")(xb, wdown_g, gq3, gkv3, cos, sin)


def _q_tile(t):
    return min(256, t // 2)


def _attn_probs(qn, qp, kn, kp, qs):
    s = (_dot_nt(qn, kn) + _dot_nt(qp, kp)) * ATT_SCALE
    rows = qs + lax.broadcasted_iota(jnp.int32, s.shape, 0)
    cols = lax.broadcasted_iota(jnp.int32, s.shape, 1)
    s = jnp.where(jnp.right_shift(cols, CHUNK_SHIFT) <= jnp.right_shift(rows, CHUNK_SHIFT), s, NEG)
    e = jnp.exp(s - jnp.max(s, axis=-1, keepdims=True))
    return e / jnp.sum(e, axis=-1, keepdims=True)


def _head_qkv(cq, ckv, kpe, c, s, wq_ref, wkv_ref):
    qn = _dot(cq, wq_ref[:, :NOPE]).astype(BF16)
    qp = _rope(_dot(cq, wq_ref[:, NOPE:]), c, s).astype(BF16)
    kn = _dot(ckv, wkv_ref[:, :NOPE]).astype(BF16)
    vv = _dot(ckv, wkv_ref[:, NOPE:]).astype(BF16)
    return qn, qp, kn, kpe.astype(BF16), vv


def _attn_in_specs(t, j):
    return [_full((t, Q_RANK)), _full((t, KV_RANK)), _full((t, ROPE)), _full((t, ROPE // 2)), _full((t, ROPE // 2)),
            pl.BlockSpec((None, None, Q_RANK, NOPE + ROPE), lambda h: (h, j, 0, 0)),
            pl.BlockSpec((None, None, KV_RANK, NOPE + VDIM), lambda h: (h, j, 0, 0)),
            pl.BlockSpec((None, None, VDIM, D), lambda h: (h, j, 0, 0))]


def _attn_fwd(cq, ckv, kpe, cos, sin, wqb_g, wkvb_g, wo_g, j):
    t = cq.shape[0]
    tq = _q_tile(t)

    def body(cq_ref, ckv_ref, kpe_ref, c_ref, s_ref, wq_ref, wkv_ref, wo_ref, o_ref, mix_ref):
        qn, qp, kn, kp, vv = _head_qkv(cq_ref[...], ckv_ref[...], kpe_ref[...], c_ref[...], s_ref[...],
                                       wq_ref, wkv_ref)
        for qs in range(0, t, tq):
            ke = qs + tq
            p = _attn_probs(qn[qs:ke], qp[qs:ke], kn[:ke], kp[:ke], qs)
            o_ref[qs:ke, :] = _dot(p.astype(BF16), vv[:ke]).astype(BF16)
        c = _dot(o_ref[...], wo_ref[...])

        @pl.when(pl.program_id(0) == 0)
        def _():
            mix_ref[...] = c

        @pl.when(pl.program_id(0) > 0)
        def _():
            mix_ref[...] += c

    return pl.pallas_call(
        body, grid=(MLA_HEADS,), in_specs=_attn_in_specs(t, j),
        out_specs=[pl.BlockSpec((None, t, VDIM), lambda h: (h, 0, 0)), _full((t, D))],
        out_shape=[S((MLA_HEADS, t, VDIM), BF16), S((t, D), F32)],
        compiler_params=_cp("arbitrary"), name="attn_fwd")(cq, ckv, kpe, cos, sin, wqb_g, wkvb_g, wo_g)


def _attn_bwd(cq, ckv, kpe, cos, sin, wqb_g, wkvb_g, wo_g, o, dzb, j):
    t = cq.shape[0]
    tq = _q_tile(t)

    def body(cq_ref, ckv_ref, kpe_ref, c_ref, s_ref, wq_ref, wkv_ref, wo_ref, o_ref, dz_ref,
             dwo_ref, dwq_ref, dwkv_ref, dcq_ref, dckv_ref, dkpe_ref, dkn_s, dkp_s, dv_s, dqn_s, dqp_s):
        cqv = cq_ref[...]
        ckvv = ckv_ref[...]
        c = c_ref[...]
        s = s_ref[...]
        qn, qp, kn, kp, vv = _head_qkv(cqv, ckvv, kpe_ref[...], c, s, wq_ref, wkv_ref)
        dzv = dz_ref[...]
        dwo_ref[...] = _dot_tn(o_ref[...], dzv).astype(BF16)
        do = _dot_nt(dzv, wo_ref[...]).astype(BF16)
        dkn_s[...] = jnp.zeros_like(dkn_s)
        dkp_s[...] = jnp.zeros_like(dkp_s)
        dv_s[...] = jnp.zeros_like(dv_s)
        for qs in range(0, t, tq):
            ke = qs + tq
            p = _attn_probs(qn[qs:ke], qp[qs:ke], kn[:ke], kp[:ke], qs)
            dp = _dot_nt(do[qs:ke], vv[:ke])
            ds = (p * (dp - jnp.sum(p * dp, axis=-1, keepdims=True)) * ATT_SCALE).astype(BF16)
            dqn_s[qs:ke, :] = _dot(ds, kn[:ke])
            dqp_s[qs:ke, :] = _dot(ds, kp[:ke])
            dkn_s[0:ke, :] += _dot_tn(ds, qn[qs:ke])
            dkp_s[0:ke, :] += _dot_tn(ds, qp[qs:ke])
            dv_s[0:ke, :] += _dot_tn(p.astype(BF16), do[qs:ke])
        dqn = dqn_s[...].astype(BF16)
        dqp = _rope_t(dqp_s[...], c, s).astype(BF16)
        dkn = dkn_s[...].astype(BF16)
        dvv = dv_s[...].astype(BF16)
        dwq_ref[:, :NOPE] = _dot_tn(cqv, dqn).astype(BF16)
        dwq_ref[:, NOPE:] = _dot_tn(cqv, dqp).astype(BF16)
        dwkv_ref[:, :NOPE] = _dot_tn(ckvv, dkn).astype(BF16)
        dwkv_ref[:, NOPE:] = _dot_tn(ckvv, dvv).astype(BF16)
        dcq = _dot_nt(dqn, wq_ref[:, :NOPE]) + _dot_nt(dqp, wq_ref[:, NOPE:])
        dckv = _dot_nt(dkn, wkv_ref[:, :NOPE]) + _dot_nt(dvv, wkv_ref[:, NOPE:])

        @pl.when(pl.program_id(0) == 0)
        def _():
            dcq_ref[...] = dcq
            dckv_ref[...] = dckv
            dkpe_ref[...] = dkp_s[...]

        @pl.when(pl.program_id(0) > 0)
        def _():
            dcq_ref[...] += dcq
            dckv_ref[...] += dckv
            dkpe_ref[...] += dkp_s[...]

    per_head = lambda a, b: pl.BlockSpec((None, a, b), lambda h: (h, 0, 0))
    return pl.pallas_call(
        body, grid=(MLA_HEADS,),
        in_specs=_attn_in_specs(t, j) + [per_head(t, VDIM), _full((t, D))],
        out_specs=[per_head(VDIM, D), per_head(Q_RANK, NOPE + ROPE), per_head(KV_RANK, NOPE + VDIM),
                   _full((t, Q_RANK)), _full((t, KV_RANK)), _full((t, ROPE))],
        out_shape=[S((MLA_HEADS, VDIM, D), BF16), S((MLA_HEADS, Q_RANK, NOPE + ROPE), BF16),
                   S((MLA_HEADS, KV_RANK, NOPE + VDIM), BF16),
                   S((t, Q_RANK), F32), S((t, KV_RANK), F32), S((t, ROPE), F32)],
        scratch_shapes=[pltpu.VMEM((t, NOPE), F32), pltpu.VMEM((t, ROPE), F32), pltpu.VMEM((t, VDIM), F32),
                        pltpu.VMEM((t, NOPE), F32), pltpu.VMEM((t, ROPE), F32)],
        compiler_params=_cp("arbitrary"), name="attn_bwd")(cq, ckv, kpe, cos, sin, wqb_g, wkvb_g, wo_g, o, dzb)


def _rms_bwd(down, dcq, dckv, dkpe, cos, sin, gq3, gkv3, j):
    t = down.shape[0]
    bm = _row_tile(t)

    def body(down_ref, dcq_ref, dckv_ref, dkpe_ref, c_ref, s_ref, gq_ref, gkv_ref, dd_ref, dgq_ref, dgkv_ref):
        @pl.when(pl.program_id(0) == 0)
        def _():
            dgq_ref[...] = jnp.zeros_like(dgq_ref)
            dgkv_ref[...] = jnp.zeros_like(dgkv_ref)

        def rms_b(x, dy, g):
            rstd = lax.rsqrt(jnp.mean(x * x, axis=-1, keepdims=True) + RMS_EPS)
            xh = x * rstd
            dyg = dy * g
            return rstd * (dyg - xh * jnp.mean(dyg * xh, axis=-1, keepdims=True)), jnp.sum(dy * xh, axis=0, keepdims=True)

        dq, dgq = rms_b(down_ref[:, :Q_RANK], dcq_ref[...], gq_ref[...])
        dkv, dgkv = rms_b(down_ref[:, Q_RANK:Q_RANK + KV_RANK], dckv_ref[...], gkv_ref[...])
        dgq_ref[...] += dgq
        dgkv_ref[...] += dgkv
        dd_ref[:, :Q_RANK] = dq.astype(BF16)
        dd_ref[:, Q_RANK:Q_RANK + KV_RANK] = dkv.astype(BF16)
        dd_ref[:, Q_RANK + KV_RANK:] = _rope_t(dkpe_ref[...], c_ref[...], s_ref[...]).astype(BF16)

    row = lambda n: pl.BlockSpec((bm, n), lambda i: (i, 0))
    return pl.pallas_call(
        body, grid=(t // bm,),
        in_specs=[row(ODD_IN), row(Q_RANK), row(KV_RANK), row(ROPE), row(ROPE // 2), row(ROPE // 2),
                  pl.BlockSpec((None, 1, Q_RANK), lambda i: (j, 0, 0)),
                  pl.BlockSpec((None, 1, KV_RANK), lambda i: (j, 0, 0))],
        out_specs=[row(ODD_IN), _full((1, Q_RANK)), _full((1, KV_RANK))],
        out_shape=[S((t, ODD_IN), BF16), S((1, Q_RANK), F32), S((1, KV_RANK), F32)],
        compiler_params=_cp("arbitrary"), name="rms_bwd")(down, dcq, dckv, dkpe, cos, sin, gq3, gkv3)


def _col_blocks(t, n, bn):
    return pl.BlockSpec((t, bn), lambda i: (0, i))


def _row_blocks(n, bm):
    return pl.BlockSpec((bm, n), lambda i: (i, 0))


def _local_step(x, pos2, tgt, big, small):
    t = x.shape[0]
    bm = _row_tile(t)
    inv_freq = (ROPE_THETA ** (-jnp.arange(0, ROPE, 2, dtype=F32) / ROPE)).reshape(1, ROPE // 2)
    cos, sin = _rope_tables(pos2, inv_freq)
    lru_p = {k: small[k] for k in ("conv_w", "conv_b", "w_a", "b_a", "w_x", "b_x", "lam")}

    saved = []
    y, yb = x, x.astype(BF16)
    for l in range(DEPTH):
        j = l // 2
        sv = dict(xb=yb)
        if l % 2 == 0:
            proj = _mm(yb, big["win2d"][j], mode="nn", grid=(EVEN_IN // 512,), a_spec=_full((t, D)),
                       b_spec=_col_blocks(D, EVEN_IN, 512), out_shape=S((t, EVEN_IN), F32),
                       out_spec=_col_blocks(t, EVEN_IN, 512), name="even_proj")
            ycat = jnp.concatenate([_pool_fwd(proj, small["pool_w"], small["pool_scale"], j),
                                    _lru_fwd(proj, lru_p, j)], axis=1)
            mix = _mm(ycat, big["wout2d"][j], mode="nn", grid=(D // 512,), a_spec=_full((t, EVEN_MIX)),
                      b_spec=_col_blocks(EVEN_MIX, D, 512), out_shape=S((t, D), F32),
                      out_spec=_col_blocks(t, D, 512), name="even_out")
            sv.update(proj=proj, ycat=ycat)
        else:
            down, cq, ckv, kpe = _down_norm(yb, big["wdown"], small["gq"], small["gkv"], cos, sin, j)
            o, mix = _attn_fwd(cq, ckv, kpe, cos, sin, big["wqb"], big["wkvb"], big["wo"], j)
            sv.update(down=down, cq=cq, ckv=ckv, kpe=kpe, o=o)
        z1, y1, y1b = _resid_ln(y, mix, small["ln_mix_g"], small["ln_mix_b"], l, "resid_ln")
        ff = _mlp_fwd(y1b, big["w1"], big["w2"], l)
        z2, y, yb = _resid_ln(y1, ff, small["ln_ffn_g"], small["ln_ffn_b"], l, "resid_ln")
        sv.update(z1=z1, y1b=y1b, z2=z2)
        saved.append(sv)

    dy, loss_tile = _loss_grad(y, tgt)

    g = {k: [None] * n for k, n in (("ln_mix_g", 4), ("ln_mix_b", 4), ("ln_ffn_g", 4), ("ln_ffn_b", 4),
                                    ("win", 2), ("pool_w", 2), ("pool_scale", 2), ("conv_w", 2), ("conv_b", 2),
                                    ("w_a", 2), ("b_a", 2), ("w_x", 2), ("b_x", 2), ("lam", 2), ("wout", 2),
                                    ("wdown", 2), ("gq", 2), ("gkv", 2), ("wqb", 2), ("wkvb", 2), ("wo", 2),
                                    ("w1", 4), ("w2", 4))}
    for l in reversed(range(DEPTH)):
        j = l // 2
        sv = saved[l]
        dz2, dz2b, g["ln_ffn_g"][l], g["ln_ffn_b"][l] = _ln_bwd(dy, sv["z2"], small["ln_ffn_g"], l, "ln_bwd")
        act, dh, dff = _mlp_bwd_dh(sv["y1b"], dz2b, big["w1"], big["w2"], l)
        g["w1"][l] = _mm(sv["y1b"], dh, mode="tn", grid=(N_DEV,), a_spec=_full((t, D)),
                         b_spec=_col_blocks(t, D_FF, FF_BLK), out_shape=S((N_DEV, D, FF_BLK), BF16),
                         out_spec=pl.BlockSpec((None, D, FF_BLK), lambda i: (i, 0, 0)), name="mlp_dw1")
        g["w2"][l] = _mm(act, dz2b, mode="tn", grid=(N_DEV,), a_spec=_col_blocks(t, D_FF, FF_BLK),
                         b_spec=_full((t, D)), out_shape=S((N_DEV, FF_BLK, D), BF16),
                         out_spec=pl.BlockSpec((None, FF_BLK, D), lambda i: (i, 0, 0)), name="mlp_dw2")
        dz1, dz1b, g["ln_mix_g"][l], g["ln_mix_b"][l] = _ln_bwd(dff, sv["z1"], small["ln_mix_g"], l, "ln_bwd_res",
                                                                 r=dz2)
        if l % 2 == 0:
            wout = big["wout2d"][j]
            dycat = _mm(dz1b, wout, mode="nt", grid=(EVEN_MIX // 512,), a_spec=_full((t, D)),
                        b_spec=_row_blocks(D, 512), out_shape=S((t, EVEN_MIX), F32),
                        out_spec=_col_blocks(t, EVEN_MIX, 512), name="even_dycat")
            dwout = _mm(sv["ycat"], dz1b, mode="tn", grid=(EVEN_MIX // 512,), a_spec=_col_blocks(t, EVEN_MIX, 512),
                        b_spec=_full((t, D)), out_shape=S((EVEN_MIX, D), BF16), out_spec=_row_blocks(D, 512),
                        name="even_dwout")
            g["wout"][j] = dwout.reshape(N_DEV, EVEN_MIX // N_DEV, D)
            du_pool, g["pool_w"][j], g["pool_scale"][j] = _pool_bwd(sv["proj"], dycat, small["pool_w"],
                                                                   small["pool_scale"], j)
            (du_lru, du_gate, g["conv_w"][j], g["conv_b"][j], g["w_a"][j], g["b_a"][j], g["w_x"][j], g["b_x"][j],
             g["lam"][j]) = _lru_bwd(sv["proj"], dycat, lru_p, j)
            dproj = jnp.concatenate([du_pool, du_lru, du_gate], axis=1)
            dwin = _mm(sv["xb"], dproj, mode="tn", grid=(EVEN_IN // 512,), a_spec=_full((t, D)),
                       b_spec=_col_blocks(t, EVEN_IN, 512), out_shape=S((D, EVEN_IN), BF16),
                       out_spec=_col_blocks(D, EVEN_IN, 512), name="even_dwin")
            g["win"][j] = dwin.reshape(D, N_DEV, EVEN_IN // N_DEV).transpose(1, 0, 2)
            dy = _mm(dproj, big["win2d"][j], mode="nt", grid=(t // bm,), a_spec=_row_blocks(EVEN_IN, bm),
                     b_spec=_full((D, EVEN_IN)), out_shape=S((t, D), F32), out_spec=_row_blocks(D, bm),
                     add=dz1, add_spec=_row_blocks(D, bm), add_scale=ALPHA, name="even_dx")
        else:
            g["wo"][j], g["wqb"][j], g["wkvb"][j], dcq, dckv, dkpe = _attn_bwd(
                sv["cq"], sv["ckv"], sv["kpe"], cos, sin, big["wqb"], big["wkvb"], big["wo"], sv["o"], dz1b, j)
            ddown, g["gq"][j], g["gkv"][j] = _rms_bwd(sv["down"], dcq, dckv, dkpe, cos, sin, small["gq"],
                                                     small["gkv"], j)
            g["wdown"][j] = _mm(sv["xb"], ddown, mode="tn", grid=(N_DEV,), a_spec=_col_blocks(t, D, D // N_DEV),
                                b_spec=_full((t, ODD_IN)), out_shape=S((N_DEV, D // N_DEV, ODD_IN), BF16),
                                out_spec=pl.BlockSpec((None, D // N_DEV, ODD_IN), lambda i: (i, 0, 0)),
                                name="odd_dwdown")
            dy = _mm(ddown, big["wdown2d"][j], mode="nt", grid=(t // bm,), a_spec=_row_blocks(ODD_IN, bm),
                     b_spec=_full((D, ODD_IN)), out_shape=S((t, D), F32), out_spec=_row_blocks(D, bm),
                     add=dz1, add_spec=_row_blocks(D, bm), add_scale=ALPHA, name="odd_dx")
    return loss_tile[0, 0], dy, g


def _mesh_place():
    x, y, c = lax.axis_index("x"), lax.axis_index("y"), lax.axis_index("c")
    return x, y, c


def _peer(place, k):
    x, y, c = place
    return (1 - x if k & 4 else x, 1 - y if k & 2 else y, 1 - c if k & 1 else c)


def _index(place):
    x, y, c = place
    return 4 * x + 2 * y + c


ANY = pl.BlockSpec(memory_space=pl.ANY)


def _all_gather_big(shards):
    n = len(shards)

    def body(*refs):
        ins, outs = refs[:n], refs[n:2 * n]
        send, recv, local = refs[2 * n:]
        x, y, c = _mesh_place()
        me, sibling = (x, y, c), (x, y, 1 - c)
        chips = [(1 - x, y), (x, 1 - y), (1 - x, 1 - y)]

        def copy(w, k, block, to, src=None):
            dst = outs[w].at[_index(block)]
            return pltpu.make_async_remote_copy(src_ref=dst if src is None else src, dst_ref=dst, send_sem=send.at[w, k],
                                                recv_sem=recv.at[w, k], device_id=to, device_id_type=MESH)

        mine = [pltpu.make_async_copy(ins[w], outs[w].at[_index(me)], local.at[w]) for w in range(n)]
        for cp in mine:
            cp.start()
        first = []
        for w in range(n):
            first.append(copy(w, 0, me, sibling, src=ins[w]))
            first += [copy(w, 1 + j, me, (*chip, c), src=ins[w]) for j, chip in enumerate(chips)]
        for cp in first:
            cp.start()
        passed = []
        for w in range(n):
            for j, chip in enumerate(chips):
                copy(w, 1 + j, (*chip, c), me).wait_recv()
                cp = copy(w, 4 + j, (*chip, c), sibling)
                cp.start()
                passed.append(cp)
        for w in range(n):
            copy(w, 0, sibling, me).wait_recv()
            for j, chip in enumerate(chips):
                copy(w, 4 + j, (*chip, 1 - c), me).wait_recv()
        for cp in first + passed:
            cp.wait_send()
        for cp in mine:
            cp.wait()

    return pl.pallas_call(
        body, in_specs=[ANY] * n, out_specs=[ANY] * n,
        out_shape=[S((N_DEV,) + s.shape, s.dtype) for s in shards],
        scratch_shapes=[pltpu.SemaphoreType.DMA((n, N_DEV - 1)), pltpu.SemaphoreType.DMA((n, N_DEV - 1)),
                        pltpu.SemaphoreType.DMA((n,))],
        compiler_params=pltpu.CompilerParams(has_side_effects=True), name="all_gather_big")(*shards)


N_CHIP = 4


def _rs_sibling(grads):
    n = len(grads)

    def body(*refs):
        ins, outs = refs[:n], refs[n:2 * n]
        send, recv = refs[2 * n:]
        x, y, c = _mesh_place()
        copies = [pltpu.make_async_remote_copy(src_ref=ins[w].at[pl.ds(0, N_CHIP), 1 - c], dst_ref=outs[w],
                                               send_sem=send.at[w], recv_sem=recv.at[w], device_id=(x, y, 1 - c),
                                               device_id_type=MESH) for w in range(n)]
        for cp in copies:
            cp.start()
        for cp in copies:
            cp.wait()

    return pl.pallas_call(
        body, in_specs=[ANY] * n, out_specs=[ANY] * n,
        out_shape=[S((N_CHIP,) + gr.shape[2:], gr.dtype) for gr in grads],
        scratch_shapes=[pltpu.SemaphoreType.DMA((n,)), pltpu.SemaphoreType.DMA((n,))],
        compiler_params=pltpu.CompilerParams(has_side_effects=True), name="rs_sibling")(*grads)


def _shard_rows_tile(a):
    return max(d for d in range(16, 257, 16) if a % d == 0)


def _rs_add(grad, other, core, name):
    _, _, nl, a, b = grad.shape
    ta = _shard_rows_tile(a)

    def body(c_ref, g_ref, h_ref, o_ref):
        o_ref[...] = (g_ref[...].astype(F32) + h_ref[...].astype(F32)).astype(BF16)

    blk = pl.BlockSpec((None, None, ta, b), lambda q, l, i, c_ref: (q, l, i, 0))
    grid_spec = pltpu.PrefetchScalarGridSpec(
        num_scalar_prefetch=1, grid=(N_CHIP, nl, a // ta),
        in_specs=[pl.BlockSpec((None, None, None, ta, b), lambda q, l, i, c_ref: (q, c_ref[0], l, i, 0)), blk],
        out_specs=blk)
    return pl.pallas_call(body, grid_spec=grid_spec, out_shape=S(other.shape, BF16),
                          compiler_params=_cp("parallel", "parallel", "parallel"), name=name)(core, grad, other)


def _rs_chips(parts):
    n = len(parts)

    def body(*refs):
        ins, outs = refs[:n], refs[n:2 * n]
        send, recv, local = refs[2 * n:]
        x, y, c = _mesh_place()
        my_chip = 2 * x + y
        chips = [(1 - x, y), (x, 1 - y), (1 - x, 1 - y)]
        copies = [pltpu.make_async_copy(ins[w].at[my_chip], outs[w].at[my_chip], local.at[w]) for w in range(n)]
        for w in range(n):
            copies += [pltpu.make_async_remote_copy(src_ref=ins[w].at[2 * px + py], dst_ref=outs[w].at[my_chip],
                                                    send_sem=send.at[w, j], recv_sem=recv.at[w, j],
                                                    device_id=(px, py, c), device_id_type=MESH)
                       for j, (px, py) in enumerate(chips)]
        for cp in copies:
            cp.start()
        for cp in copies:
            cp.wait()

    return pl.pallas_call(
        body, in_specs=[ANY] * n, out_specs=[ANY] * n, out_shape=[S(p.shape, p.dtype) for p in parts],
        scratch_shapes=[pltpu.SemaphoreType.DMA((n, N_CHIP - 1)), pltpu.SemaphoreType.DMA((n, N_CHIP - 1)),
                        pltpu.SemaphoreType.DMA((n,))],
        compiler_params=pltpu.CompilerParams(has_side_effects=True), name="rs_chips")(*parts)


def _all_reduce_small(part, name):
    r = part.shape[1]

    def body(p_ref, o_ref, rbuf, send1, recv1, send2, recv2):
        place = _mesh_place()
        me = _index(place)
        rbuf[pl.ds(me, 1)] = p_ref[pl.ds(me, 1)]
        first = [pltpu.make_async_remote_copy(src_ref=p_ref.at[_index(_peer(place, k))], dst_ref=rbuf.at[me],
                                              send_sem=send1.at[k - 1], recv_sem=recv1.at[k - 1],
                                              device_id=_peer(place, k), device_id_type=MESH)
                 for k in range(1, N_DEV)]
        for cp in first:
            cp.start()
        for cp in first:
            cp.wait()
        acc = rbuf[0]
        for d in range(1, N_DEV):
            acc = acc + rbuf[d]
        o_ref[pl.ds(me, 1)] = acc[None]
        second = [pltpu.make_async_remote_copy(src_ref=o_ref.at[me], dst_ref=o_ref.at[me], send_sem=send2.at[k - 1],
                                               recv_sem=recv2.at[k - 1], device_id=_peer(place, k),
                                               device_id_type=MESH)
                  for k in range(1, N_DEV)]
        for cp in second:
            cp.start()
        for cp in second:
            cp.wait()

    vm = pl.BlockSpec(memory_space=pltpu.VMEM)
    return pl.pallas_call(
        body, in_specs=[vm], out_specs=vm, out_shape=S(part.shape, F32),
        scratch_shapes=[pltpu.VMEM(part.shape, F32)] + [pltpu.SemaphoreType.DMA((N_DEV - 1,))] * 4,
        compiler_params=pltpu.CompilerParams(has_side_effects=True, vmem_limit_bytes=VMEM_LIMIT), name=name)(part)


def _adamw(w, g, m, v):
    m = ADAM_B1 * m + (1.0 - ADAM_B1) * g
    v = ADAM_B2 * v + (1.0 - ADAM_B2) * (g * g)
    m_hat = m / (1.0 - ADAM_B1 ** ADAM_STEP)
    v_hat = v / (1.0 - ADAM_B2 ** ADAM_STEP)
    return -ADAM_LR * (m_hat / (jnp.sqrt(v_hat) + ADAM_EPS) + ADAM_WD * w), m, v


def _adam_big(parts, w, m, v, name):
    nl, a, b = w.shape
    ta = _shard_rows_tile(a)

    def body(p_ref, w_ref, m_ref, v_ref, g_ref, d_ref, mo_ref, vo_ref):
        g = p_ref[0].astype(F32)
        for s in range(1, N_CHIP):
            g = g + p_ref[s].astype(F32)
        g_ref[...] = g
        d_ref[...], mo_ref[...], vo_ref[...] = _adamw(w_ref[...], g, m_ref[...], v_ref[...])

    blk = pl.BlockSpec((None, ta, b), lambda l, i: (l, i, 0))
    return pl.pallas_call(
        body, grid=(nl, a // ta), in_specs=[pl.BlockSpec((N_CHIP, None, ta, b), lambda l, i: (0, l, i, 0)), blk, blk, blk],
        out_specs=[blk] * 4, out_shape=[S(w.shape, F32)] * 4, compiler_params=_cp("parallel", "parallel"),
        name=name)(parts, w, m, v)


def _adam_small(g, w, m, v):
    rows = g.shape[0]
    tr = 256

    def body(g_ref, w_ref, m_ref, v_ref, d_ref, mo_ref, vo_ref):
        d_ref[...], mo_ref[...], vo_ref[...] = _adamw(w_ref[...], g_ref[...], m_ref[...], v_ref[...])

    blk = pl.BlockSpec((tr, 128), lambda i: (i, 0))
    return pl.pallas_call(body, grid=(rows // tr,), in_specs=[blk] * 4, out_specs=[blk] * 3,
                          out_shape=[S(g.shape, F32)] * 3, compiler_params=_cp("parallel"), name="adam_small")(g, w, m, v)


BIG = ("even_w_in", "even_w_out", "mla_w_down", "mla_w_qb", "mla_w_kvb", "mla_w_o", "mlp_w1", "mlp_w2")
BIG_KEY = dict(even_w_in="win", even_w_out="wout", mla_w_down="wdown", mla_w_qb="wqb", mla_w_kvb="wkvb",
               mla_w_o="wo", mlp_w1="w1", mlp_w2="w2")
SMALL = (("ln_mix_g", "ln_mix_g", None), ("ln_mix_b", "ln_mix_b", None), ("ln_ffn_g", "ln_ffn_g", None),
         ("ln_ffn_b", "ln_ffn_b", None), ("pool_w", "pool_w", None), ("pool_scale", "pool_scale", None),
         ("lru_conv_w", "conv_w", 2), ("lru_conv_b", "conv_b", None), ("lru_w_a", "w_a", None),
         ("lru_b_a", "b_a", None), ("lru_w_x", "w_x", None), ("lru_b_x", "b_x", None), ("lru_lambda", "lam", None),
         ("mla_q_norm_g", "gq", 1), ("mla_kv_norm_g", "gkv", 1))
WEIGHTS = ("ln_mix_g", "ln_mix_b", "ln_ffn_g", "ln_ffn_b", "even_w_in", "pool_w", "pool_scale", "lru_conv_w",
           "lru_conv_b", "lru_w_a", "lru_b_a", "lru_w_x", "lru_b_x", "lru_lambda", "even_w_out", "mla_w_down",
           "mla_q_norm_g", "mla_kv_norm_g", "mla_w_qb", "mla_w_kvb", "mla_w_o", "mlp_w1", "mlp_w2")
ALL_AXES = ("x", "y", "c")


def _pack(arrays, multiple):
    flat = jnp.concatenate([a.reshape(-1) for a in arrays])
    pad = (-flat.shape[0]) % multiple
    return jnp.pad(flat, (0, pad))


def _unpack(flat, shapes):
    out, at = [], 0
    for shp in shapes:
        n = 1
        for s in shp:
            n *= s
        out.append(flat[at:at + n].reshape(shp))
        at += n
    return out


def _global_shape(local_shape, axis):
    if axis is None:
        return tuple(local_shape)
    return tuple(s * N_DEV if i == axis else s for i, s in enumerate(local_shape))


def _step(x, positions, tgt, w, m, v):
    t = x.shape[1]
    me = _index(_mesh_place())

    sharded = [(name, axis) for name, _, axis in SMALL if axis is not None]
    zeros_with_mine = [lax.dynamic_update_slice_in_dim(jnp.zeros(_global_shape(w[name].shape, axis), F32), w[name],
                                                       me * w[name].shape[axis], axis) for name, axis in sharded]
    chunk = N_DEV * 8 * 128
    gathered = _all_reduce_small(_pack(zeros_with_mine, chunk).reshape(N_DEV, -1, 128), "gather_small")
    full = dict(zip([name for name, _ in sharded],
                    _unpack(gathered.reshape(-1), [_global_shape(w[name].shape, axis) for name, axis in sharded])))

    gathered_big = _all_gather_big([w[name].astype(BF16) for name in BIG])
    big = {BIG_KEY[name]: a for name, a in zip(BIG, gathered_big)}
    big["win2d"] = [big["win"][:, j].transpose(1, 0, 2).reshape(D, EVEN_IN) for j in range(2)]
    big["wout2d"] = [big["wout"][:, j].reshape(EVEN_MIX, D) for j in range(2)]
    big["wdown2d"] = [big["wdown"][:, j].reshape(D, ODD_IN) for j in range(2)]

    row3 = lambda a: a.reshape(a.shape[0], 1, a.shape[1])
    small = dict(ln_mix_g=row3(w["ln_mix_g"]), ln_mix_b=row3(w["ln_mix_b"]), ln_ffn_g=row3(w["ln_ffn_g"]),
                 ln_ffn_b=row3(w["ln_ffn_b"]), pool_w=w["pool_w"], pool_scale=row3(w["pool_scale"]),
                 conv_w=full["lru_conv_w"], conv_b=row3(w["lru_conv_b"]), w_a=w["lru_w_a"], b_a=row3(w["lru_b_a"]),
                 w_x=w["lru_w_x"], b_x=row3(w["lru_b_x"]), lam=row3(w["lru_lambda"]),
                 gq=row3(full["mla_q_norm_g"]), gkv=row3(full["mla_kv_norm_g"]))

    loss_part, grad_x, g = _local_step(x[0], positions.reshape(t, 1), tgt[0], big, small)
    loss = lax.psum(loss_part, ALL_AXES)

    stacked = [jnp.stack(g[BIG_KEY[name]], axis=1) for name in BIG]
    stacked = [a.reshape((N_CHIP, 2) + a.shape[1:]) for a in stacked]
    core = lax.axis_index("c").astype(jnp.int32).reshape(1)
    chip_sums = [_rs_add(a, h, core, "rs_add_" + name) for a, h, name in zip(stacked, _rs_sibling(stacked), BIG)]
    parts = _rs_chips(chip_sums)
    out = {}
    for name, p in zip(BIG, parts):
        out[name] = _adam_big(p, w[name], m[name], v[name], "adam_" + name)

    local_g = [jnp.stack(g[key]).reshape(_global_shape(w[name].shape, axis)) for name, key, axis in SMALL]
    reduced = _all_reduce_small(_pack(local_g, chunk).reshape(N_DEV, -1, 128), "all_reduce_small")
    reduced = _unpack(reduced.reshape(-1), [a.shape for a in local_g])
    mine = [a if axis is None else lax.dynamic_slice_in_dim(a, me * w[name].shape[axis], w[name].shape[axis], axis)
            for a, (name, _, axis) in zip(reduced, SMALL)]
    tile = 256 * 128
    packed = [_pack(arrs, tile).reshape(-1, 128)
              for arrs in (mine, [w[n] for n, _, _ in SMALL], [m[n] for n, _, _ in SMALL], [v[n] for n, _, _ in SMALL])]
    shapes = [w[n].shape for n, _, _ in SMALL]
    d_s, m_s, v_s = (_unpack(a.reshape(-1), shapes) for a in _adam_small(*packed))
    for i, (name, _, _) in enumerate(SMALL):
        out[name] = (mine[i], d_s[i], m_s[i], v_s[i])

    return (loss, grad_x[None]) + tuple(out[name][i] for i in range(4) for name in WEIGHTS)


def kernel(x, positions, ln_mix_g, ln_mix_b, ln_ffn_g, ln_ffn_b, even_w_in, pool_w, pool_scale, lru_conv_w, lru_conv_b, lru_w_a, lru_b_a, lru_w_x, lru_b_x, lru_lambda, even_w_out, mla_w_down, mla_q_norm_g, mla_kv_norm_g, mla_w_qb, mla_w_kvb, mla_w_o, mlp_w1, mlp_w2, loss_target, m_ln_mix_g, m_ln_mix_b, m_ln_ffn_g, m_ln_ffn_b, m_even_w_in, m_pool_w, m_pool_scale, m_lru_conv_w, m_lru_conv_b, m_lru_w_a, m_lru_b_a, m_lru_w_x, m_lru_b_x, m_lru_lambda, m_even_w_out, m_mla_w_down, m_mla_q_norm_g, m_mla_kv_norm_g, m_mla_w_qb, m_mla_w_kvb, m_mla_w_o, m_mlp_w1, m_mlp_w2, v_ln_mix_g, v_ln_mix_b, v_ln_ffn_g, v_ln_ffn_b, v_even_w_in, v_pool_w, v_pool_scale, v_lru_conv_w, v_lru_conv_b, v_lru_w_a, v_lru_b_a, v_lru_w_x, v_lru_b_x, v_lru_lambda, v_even_w_out, v_mla_w_down, v_mla_q_norm_g, v_mla_kv_norm_g, v_mla_w_qb, v_mla_w_kvb, v_mla_w_o, v_mlp_w1, v_mlp_w2):
    w = dict(zip(WEIGHTS, (ln_mix_g, ln_mix_b, ln_ffn_g, ln_ffn_b, even_w_in, pool_w, pool_scale, lru_conv_w,
                           lru_conv_b, lru_w_a, lru_b_a, lru_w_x, lru_b_x, lru_lambda, even_w_out, mla_w_down,
                           mla_q_norm_g, mla_kv_norm_g, mla_w_qb, mla_w_kvb, mla_w_o, mlp_w1, mlp_w2)))
    m = dict(zip(WEIGHTS, (m_ln_mix_g, m_ln_mix_b, m_ln_ffn_g, m_ln_ffn_b, m_even_w_in, m_pool_w, m_pool_scale,
                           m_lru_conv_w, m_lru_conv_b, m_lru_w_a, m_lru_b_a, m_lru_w_x, m_lru_b_x, m_lru_lambda,
                           m_even_w_out, m_mla_w_down, m_mla_q_norm_g, m_mla_kv_norm_g, m_mla_w_qb, m_mla_w_kvb,
                           m_mla_w_o, m_mlp_w1, m_mlp_w2)))
    v = dict(zip(WEIGHTS, (v_ln_mix_g, v_ln_mix_b, v_ln_ffn_g, v_ln_ffn_b, v_even_w_in, v_pool_w, v_pool_scale,
                           v_lru_conv_w, v_lru_conv_b, v_lru_w_a, v_lru_b_a, v_lru_w_x, v_lru_b_x, v_lru_lambda,
                           v_even_w_out, v_mla_w_down, v_mla_q_norm_g, v_mla_kv_norm_g, v_mla_w_qb, v_mla_w_kvb,
                           v_mla_w_o, v_mlp_w1, v_mlp_w2)))
    return _step(x, positions, loss_target, w, m, v)
```

```python
import functools

import jax
import jax.numpy as jnp
from jax import lax
from jax.experimental import pallas as pl
from jax.experimental.pallas import tpu as pltpu

F32 = jnp.float32
BF16 = jnp.bfloat16
S = jax.ShapeDtypeStruct

D = 1024
DEPTH = 4
N_DEV = 8
CHUNK_SHIFT = 6
POOL_WINDOWS = (2, 4, 8, 16)
POOL_W = 512
LRU_W = 1024
LRU_HEADS = 8
HEAD = 128
LRU_C = 8.0
EVEN_IN = 2560
EVEN_MIX = 1536
MLA_HEADS = 8
NOPE = 128
ROPE = 64
VDIM = 128
Q_RANK = 384
KV_RANK = 256
ODD_IN = 704
D_FF = 4096
FF_BLK = D_FF // N_DEV
ROPE_THETA = 10000.0
ALPHA = (2 * DEPTH) ** 0.25
LN_EPS = 1e-5
RMS_EPS = 1e-6
ATT_SCALE = (NOPE + ROPE) ** -0.5
NEG = float(jnp.finfo(jnp.float32).min)
ADAM_LR = 0.001
ADAM_B1 = 0.9
ADAM_B2 = 0.999
ADAM_EPS = 1e-08
ADAM_WD = 0.01
ADAM_STEP = 10
V7X_VMEM_BYTES = 64 * 1024 * 1024
VMEM_LIMIT = V7X_VMEM_BYTES - 8 * 1024 * 1024
MESH = pl.DeviceIdType.MESH


def _cp(*sem):
    return pltpu.CompilerParams(dimension_semantics=sem if sem else None, vmem_limit_bytes=VMEM_LIMIT)


def _dot(a, b):
    return jnp.dot(a, b, preferred_element_type=F32)


def _dot_nt(a, b):
    return lax.dot_general(a, b, (((1,), (1,)), ((), ())), preferred_element_type=F32)


def _dot_tn(a, b):
    return lax.dot_general(a, b, (((0,), (0,)), ((), ())), preferred_element_type=F32)


def _full(shape):
    return pl.BlockSpec(shape, lambda *_: (0,) * len(shape))


def _mm(a, b, *, mode, grid, a_spec, b_spec, out_shape, out_spec, name, add=None, add_spec=None, add_scale=1.0):
    dot = {"nn": _dot, "nt": _dot_nt, "tn": _dot_tn}[mode]

    def body(*refs):
        if add is None:
            a_ref, b_ref, o_ref = refs
            acc = dot(a_ref[...].astype(BF16), b_ref[...].astype(BF16))
        else:
            a_ref, b_ref, add_ref, o_ref = refs
            acc = dot(a_ref[...].astype(BF16), b_ref[...].astype(BF16)) + add_scale * add_ref[...]
        o_ref[...] = acc.astype(o_ref.dtype)

    ops = (a, b) if add is None else (a, b, add)
    specs = [a_spec, b_spec] if add is None else [a_spec, b_spec, add_spec]
    return pl.pallas_call(body, grid=grid, in_specs=specs, out_specs=out_spec, out_shape=out_shape,
                          compiler_params=_cp(*(("parallel",) * len(grid))), name=name)(*ops)


def _ln_stats(z):
    mu = jnp.mean(z, axis=-1, keepdims=True)
    zc = z - mu
    var = jnp.mean(zc * zc, axis=-1, keepdims=True)
    rstd = lax.rsqrt(var + LN_EPS)
    return zc * rstd, rstd


def _row_tile(t):
    return min(512, t)


def _resid_ln(x, mix, g3, b3, l, name):
    t = x.shape[0]
    bm = _row_tile(t)

    def body(x_ref, m_ref, g_ref, b_ref, z_ref, y_ref, yb_ref):
        z = ALPHA * x_ref[...] + m_ref[...]
        xh, _ = _ln_stats(z)
        y = xh * g_ref[...] + b_ref[...]
        z_ref[...] = z
        y_ref[...] = y
        yb_ref[...] = y.astype(BF16)

    row = pl.BlockSpec((bm, D), lambda i: (i, 0))
    vec = pl.BlockSpec((None, 1, D), lambda i: (l, 0, 0))
    return pl.pallas_call(body, grid=(t // bm,), in_specs=[row, row, vec, vec], out_specs=[row, row, row],
                          out_shape=[S((t, D), F32), S((t, D), F32), S((t, D), BF16)],
                          compiler_params=_cp("parallel"), name=name)(x, mix, g3, b3)


def _ln_bwd(d, z, g3, l, name, r=None, dep=None):
    t = z.shape[0]
    bm = _row_tile(t)

    def body(*refs):
        refs = list(refs)
        d_ref = refs.pop(0)
        dy = d_ref[...]
        if r is not None:
            dy = dy + ALPHA * refs.pop(0)[...]
        z_ref, g_ref = refs.pop(0), refs.pop(0)
        if dep is not None:
            refs.pop(0)
        dz_ref, dzb_ref, dg_ref, db_ref = refs
        xh, rstd = _ln_stats(z_ref[...])
        dyg = dy * g_ref[...]
        m1 = jnp.mean(dyg, axis=-1, keepdims=True)
        m2 = jnp.mean(dyg * xh, axis=-1, keepdims=True)
        dz = rstd * (dyg - m1 - xh * m2)
        dz_ref[...] = dz
        dzb_ref[...] = dz.astype(BF16)

        @pl.when(pl.program_id(0) == 0)
        def _():
            dg_ref[...] = jnp.zeros_like(dg_ref)
            db_ref[...] = jnp.zeros_like(db_ref)

        dg_ref[...] += jnp.sum(dy * xh, axis=0, keepdims=True)
        db_ref[...] += jnp.sum(dy, axis=0, keepdims=True)

    row = pl.BlockSpec((bm, D), lambda i: (i, 0))
    vec = pl.BlockSpec((None, 1, D), lambda i: (l, 0, 0))
    acc = pl.BlockSpec((1, D), lambda i: (0, 0))
    ops = [d, z, g3] if r is None else [d, r, z, g3]
    specs = [row, row, vec] if r is None else [row, row, row, vec]
    if dep is not None:
        ops.append(dep)
        specs.append(_full(dep.shape))
    return pl.pallas_call(body, grid=(t // bm,), in_specs=specs, out_specs=[row, row, acc, acc],
                          out_shape=[S((t, D), F32), S((t, D), BF16), S((1, D), F32), S((1, D), F32)],
                          compiler_params=_cp("arbitrary"), name=name)(*ops)


def _loss_grad(y, tgt):
    t = y.shape[0]
    bm = _row_tile(t)

    def body(y_ref, t_ref, dy_ref, loss_ref, acc_ref):
        i = pl.program_id(0)
        e = y_ref[...] - t_ref[...]
        dy_ref[...] = e * (1.0 / D)

        @pl.when(i == 0)
        def _():
            acc_ref[...] = jnp.zeros_like(acc_ref)

        acc_ref[...] += jnp.sum(e * e, axis=0, keepdims=True)

        @pl.when(i == pl.num_programs(0) - 1)
        def _():
            loss_ref[...] = jnp.full(loss_ref.shape, (0.5 / D) * jnp.sum(acc_ref[...]), F32)

    row = pl.BlockSpec((bm, D), lambda i: (i, 0))
    return pl.pallas_call(body, grid=(t // bm,), in_specs=[row, row],
                          out_specs=[row, pl.BlockSpec((1, 128), lambda i: (0, 0))],
                          out_shape=[S((t, D), F32), S((1, 128), F32)],
                          scratch_shapes=[pltpu.VMEM((1, D), F32)],
                          compiler_params=_cp("arbitrary"), name="loss_grad")(y, tgt)


def _mlp_fwd(yb, w1g, w2g):
    t = yb.shape[0]
    bm = _row_tile(t)

    def body(y_ref, w1_ref, w2_ref, o_ref):
        j = pl.program_id(1)
        h = jnp.maximum(_dot(y_ref[...], w1_ref[...]), 0.0)
        c = _dot((h * h).astype(BF16), w2_ref[...])

        @pl.when(j == 0)
        def _():
            o_ref[...] = c

        @pl.when(j > 0)
        def _():
            o_ref[...] += c

    return pl.pallas_call(
        body, grid=(t // bm, N_DEV),
        in_specs=[pl.BlockSpec((bm, D), lambda i, j: (i, 0)),
                  pl.BlockSpec((None, D, FF_BLK), lambda i, j: (j, 0, 0)),
                  pl.BlockSpec((None, FF_BLK, D), lambda i, j: (j, 0, 0))],
        out_specs=pl.BlockSpec((bm, D), lambda i, j: (i, 0)),
        out_shape=S((t, D), F32), compiler_params=_cp("parallel", "arbitrary"), name="mlp_fwd")(yb, w1g, w2g)


def _mlp_bwd_dh(yb, dzb, w1g, w2g):
    t = yb.shape[0]
    bm = _row_tile(t)

    def body(y_ref, dz_ref, w1_ref, w2_ref, a_ref, dh_ref, acc_ref):
        j = pl.program_id(1)
        r = jnp.maximum(_dot(y_ref[...], w1_ref[...]), 0.0)
        a_ref[...] = (r * r).astype(BF16)
        da = _dot_nt(dz_ref[...], w2_ref[...])
        dh = (da * (2.0 * r)).astype(BF16)
        dh_ref[...] = dh
        c = _dot_nt(dh, w1_ref[...])

        @pl.when(j == 0)
        def _():
            acc_ref[...] = c

        @pl.when(j > 0)
        def _():
            acc_ref[...] += c

    row = pl.BlockSpec((bm, D), lambda i, j: (i, 0))
    hid = pl.BlockSpec((bm, FF_BLK), lambda i, j: (i, j))
    return pl.pallas_call(
        body, grid=(t // bm, N_DEV),
        in_specs=[row, row,
                  pl.BlockSpec((None, D, FF_BLK), lambda i, j: (j, 0, 0)),
                  pl.BlockSpec((None, FF_BLK, D), lambda i, j: (j, 0, 0))],
        out_specs=[hid, hid, row],
        out_shape=[S((t, D_FF), BF16), S((t, D_FF), BF16), S((t, D), F32)],
        compiler_params=_cp("parallel", "arbitrary"), name="mlp_bwd_dh")(yb, dzb, w1g, w2g)


def _shift_dn(x, k, rows, fill=0.0):
    return jnp.where(rows >= k, pltpu.roll(x, k, 0), fill)


def _shift_up(x, k, rows, fill=0.0):
    t = x.shape[0]
    return jnp.where(rows < t - k, pltpu.roll(x, t - k, 0), fill)


def _scan_dn(a, b, rows):
    k = 1
    t = a.shape[0]
    while k < t:
        b = a * _shift_dn(b, k, rows) + b
        if 2 * k < t:
            a = a * _shift_dn(a, k, rows, 1.0)
        k *= 2
    return b


def _scan_up(a, b, rows):
    k = 1
    t = a.shape[0]
    while k < t:
        b = a * _shift_up(b, k, rows) + b
        if 2 * k < t:
            a = a * _shift_up(a, k, rows, 1.0)
        k *= 2
    return b


def _window_sum_dn(x, w, rows):
    k = 1
    while k < w:
        x = x + _shift_dn(x, k, rows)
        k *= 2
    return x


def _window_sum_up(x, w, rows):
    k = 1
    while k < w:
        x = x + _shift_up(x, k, rows)
        k *= 2
    return x


def _pool_diff(u, w, rows):
    inv_count = 1.0 / jnp.minimum(rows + 1, w).astype(F32)
    return _window_sum_dn(u, w, rows) * inv_count - u, inv_count


def _pool_fwd(proj, pool_w, pool_scale3, j):
    t = proj.shape[0]

    def body(u_ref, w_ref, s_ref, y_ref):
        rows = lax.broadcasted_iota(jnp.int32, (t, HEAD), 0)
        for g, w in enumerate(POOL_WINDOWS):
            cols = slice(g * HEAD, (g + 1) * HEAD)
            d, _ = _pool_diff(u_ref[:, cols], w, rows)
            y = _dot(d.astype(BF16), w_ref[g].astype(BF16)) * s_ref[:, cols]
            y_ref[:, cols] = y.astype(BF16)

    return pl.pallas_call(
        body, grid=(1,),
        in_specs=[pl.BlockSpec((t, POOL_W), lambda i: (0, 0)),
                  pl.BlockSpec((None, 4, HEAD, HEAD), lambda i: (j, 0, 0, 0)),
                  pl.BlockSpec((None, 1, POOL_W), lambda i: (j, 0, 0))],
        out_specs=pl.BlockSpec((t, POOL_W), lambda i: (0, 0)),
        out_shape=S((t, POOL_W), BF16), compiler_params=_cp("arbitrary"), name="pool_fwd")(proj, pool_w, pool_scale3)


def _pool_bwd(proj, dycat, pool_w, pool_scale3, j):
    t = proj.shape[0]

    def body(u_ref, dy_ref, w_ref, s_ref, du_ref, dw_ref, ds_ref):
        rows = lax.broadcasted_iota(jnp.int32, (t, HEAD), 0)
        for g, w in enumerate(POOL_WINDOWS):
            cols = slice(g * HEAD, (g + 1) * HEAD)
            d, inv_count = _pool_diff(u_ref[:, cols], w, rows)
            db = d.astype(BF16)
            wg = w_ref[g].astype(BF16)
            dy = dy_ref[:, cols]
            ds_ref[:, cols] = jnp.sum(dy * _dot(db, wg), axis=0, keepdims=True)
            dzz = (dy * s_ref[:, cols]).astype(BF16)
            dw_ref[g] = _dot_tn(db, dzz)
            dd = _dot_nt(dzz, wg)
            du_ref[:, cols] = (_window_sum_up(dd * inv_count, w, rows) - dd).astype(BF16)

    return pl.pallas_call(
        body, grid=(1,),
        in_specs=[pl.BlockSpec((t, POOL_W), lambda i: (0, 0)),
                  pl.BlockSpec((t, POOL_W), lambda i: (0, 0)),
                  pl.BlockSpec((None, 4, HEAD, HEAD), lambda i: (j, 0, 0, 0)),
                  pl.BlockSpec((None, 1, POOL_W), lambda i: (j, 0, 0))],
        out_specs=[pl.BlockSpec((t, POOL_W), lambda i: (0, 0)), _full((4, HEAD, HEAD)), _full((1, POOL_W))],
        out_shape=[S((t, POOL_W), BF16), S((4, HEAD, HEAD), F32), S((1, POOL_W), F32)],
        compiler_params=_cp("arbitrary"), name="pool_bwd")(proj, dycat, pool_w, pool_scale3)


GELU_C = 0.7978845608028654
GELU_K = 0.044715


def _gelu(x):
    th = jnp.tanh(GELU_C * (x + GELU_K * x * x * x))
    return 0.5 * x * (1.0 + th), th


def _lru_forward(u, gate, cw, cb, wa, ba, wx, bx, lam, rows):
    v = cw[3:4] * u + cw[2:3] * _shift_dn(u, 1, rows) + cw[1:2] * _shift_dn(u, 2, rows) \
        + cw[0:1] * _shift_dn(u, 3, rows) + cb
    vb = v.astype(BF16)
    r = jax.nn.sigmoid(_dot(vb, wa) + ba)
    i = jax.nn.sigmoid(_dot(vb, wx) + bx)
    sp = jnp.maximum(-lam, 0.0) + jnp.log1p(jnp.exp(-jnp.abs(lam)))
    log_a = (-LRU_C) * r * sp
    a = jnp.exp(log_a)
    one_m_a2 = -jnp.tanh(log_a) * (a * a + 1.0)
    mult = jnp.sqrt(one_m_a2)
    h = _scan_dn(a, mult * (i * v), rows)
    gl, th = _gelu(gate)
    return dict(v=v, vb=vb, r=r, i=i, sp=sp, a=a, mult=mult, h=h, gl=gl, th=th)


def _lru_specs(t, j, col0_u, col0_g):
    blk = lambda c0: pl.BlockSpec((t, HEAD), lambda h: (0, c0 + h))
    vec = pl.BlockSpec((None, 1, HEAD), lambda h: (j, 0, h))
    return [blk(col0_u), blk(col0_g),
            pl.BlockSpec((None, 4, HEAD), lambda h: (j, 0, h)), vec,
            pl.BlockSpec((None, None, HEAD, HEAD), lambda h: (j, h, 0, 0)), vec,
            pl.BlockSpec((None, None, HEAD, HEAD), lambda h: (j, h, 0, 0)), vec, vec]


def _lru_fwd(proj, p, j):
    t = proj.shape[0]

    def body(u_ref, g_ref, cw_ref, cb_ref, wa_ref, ba_ref, wx_ref, bx_ref, lam_ref, y_ref):
        rows = lax.broadcasted_iota(jnp.int32, (t, HEAD), 0)
        f = _lru_forward(u_ref[...], g_ref[...], cw_ref[...], cb_ref[...], wa_ref[...].astype(BF16), ba_ref[...],
                         wx_ref[...].astype(BF16), bx_ref[...], lam_ref[...], rows)
        y_ref[...] = (f["h"] * f["gl"]).astype(BF16)

    return pl.pallas_call(
        body, grid=(LRU_HEADS,), in_specs=_lru_specs(t, j, POOL_W // HEAD, (POOL_W + LRU_W) // HEAD),
        out_specs=pl.BlockSpec((t, HEAD), lambda h: (0, h)), out_shape=S((t, LRU_W), BF16),
        compiler_params=_cp("parallel"), name="lru_fwd")(
            proj, proj, p["conv_w"], p["conv_b"], p["w_a"], p["b_a"], p["w_x"], p["b_x"], p["lam"])


def _lru_bwd(proj, dycat, p, j):
    t = proj.shape[0]

    def body(u_ref, g_ref, cw_ref, cb_ref, wa_ref, ba_ref, wx_ref, bx_ref, lam_ref, dy_ref,
             du_ref, dgate_ref, dcw_ref, dcb_ref, dwa_ref, dba_ref, dwx_ref, dbx_ref, dlam_ref):
        rows = lax.broadcasted_iota(jnp.int32, (t, HEAD), 0)
        u = u_ref[...]
        gate = g_ref[...]
        cw = cw_ref[...]
        wa = wa_ref[...].astype(BF16)
        wx = wx_ref[...].astype(BF16)
        lam = lam_ref[...]
        f = _lru_forward(u, gate, cw, cb_ref[...], wa, ba_ref[...], wx, bx_ref[...], lam, rows)
        v, r, i, a, mult, h, th = f["v"], f["r"], f["i"], f["a"], f["mult"], f["h"], f["th"]
        dy = dy_ref[...]
        dgl = 0.5 * (1.0 + th) + 0.5 * gate * (1.0 - th * th) * GELU_C * (1.0 + 3.0 * GELU_K * gate * gate)
        dgate_ref[...] = (dy * h * dgl).astype(BF16)
        g = _scan_up(_shift_up(a, 1, rows), dy * f["gl"], rows)
        da = g * _shift_dn(h, 1, rows)
        iv = i * v
        dmult = g * iv
        di = g * mult * v
        dv = g * mult * i
        dlog_a = da * a - dmult * (a * a) / mult
        dr = dlog_a * (-LRU_C) * f["sp"]
        dsp = jnp.sum(dlog_a * (-LRU_C) * r, axis=0, keepdims=True)
        dlam_ref[...] = -dsp * jax.nn.sigmoid(-lam)
        dpa = dr * r * (1.0 - r)
        dpx = di * i * (1.0 - i)
        dpab = dpa.astype(BF16)
        dpxb = dpx.astype(BF16)
        dwa_ref[...] = _dot_tn(f["vb"], dpab)
        dwx_ref[...] = _dot_tn(f["vb"], dpxb)
        dba_ref[...] = jnp.sum(dpa, axis=0, keepdims=True)
        dbx_ref[...] = jnp.sum(dpx, axis=0, keepdims=True)
        dv = dv + _dot_nt(dpab, wa) + _dot_nt(dpxb, wx)
        dcb_ref[...] = jnp.sum(dv, axis=0, keepdims=True)
        du = cw[3:4] * dv
        dcw_ref[3:4, :] = jnp.sum(dv * u, axis=0, keepdims=True)
        for k in (1, 2, 3):
            du = du + cw[3 - k:4 - k] * _shift_up(dv, k, rows)
            dcw_ref[3 - k:4 - k, :] = jnp.sum(dv * _shift_dn(u, k, rows), axis=0, keepdims=True)
        du_ref[...] = du.astype(BF16)

    blk = pl.BlockSpec((t, HEAD), lambda h: (0, h))
    vec = pl.BlockSpec((1, HEAD), lambda h: (0, h))
    mat = pl.BlockSpec((None, HEAD, HEAD), lambda h: (h, 0, 0))
    return pl.pallas_call(
        body, grid=(LRU_HEADS,),
        in_specs=_lru_specs(t, j, POOL_W // HEAD, (POOL_W + LRU_W) // HEAD)
        + [pl.BlockSpec((t, HEAD), lambda h: (0, POOL_W // HEAD + h))],
        out_specs=[blk, blk, pl.BlockSpec((4, HEAD), lambda h: (0, h)), vec, mat, vec, mat, vec, vec],
        out_shape=[S((t, LRU_W), BF16), S((t, LRU_W), BF16), S((4, LRU_W), F32), S((1, LRU_W), F32),
                   S((LRU_HEADS, HEAD, HEAD), F32), S((1, LRU_W), F32),
                   S((LRU_HEADS, HEAD, HEAD), F32), S((1, LRU_W), F32), S((1, LRU_W), F32)],
        compiler_params=_cp("parallel"), name="lru_bwd")(
            proj, proj, p["conv_w"], p["conv_b"], p["w_a"], p["b_a"], p["w_x"], p["b_x"], p["lam"], dycat)


def _rope(x, c, s):
    x1 = x[:, :ROPE // 2]
    x2 = x[:, ROPE // 2:]
    return jnp.concatenate([x1 * c - x2 * s, x1 * s + x2 * c], axis=-1)


def _rope_t(d, c, s):
    d1 = d[:, :ROPE // 2]
    d2 = d[:, ROPE // 2:]
    return jnp.concatenate([d1 * c + d2 * s, d2 * c - d1 * s], axis=-1)


def _rope_tables(pos2, inv_freq):
    t = pos2.shape[0]

    def body(p_ref, f_ref, c_ref, s_ref):
        ang = p_ref[...].astype(F32) * f_ref[...]
        c_ref[...] = jnp.cos(ang)
        s_ref[...] = jnp.sin(ang)

    return pl.pallas_call(body, out_shape=[S((t, ROPE // 2), F32), S((t, ROPE // 2), F32)],
                          name="rope_tables")(pos2, inv_freq)


def _down_norm(xb, wdown_g, gq3, gkv3, cos, sin, j):
    t = xb.shape[0]
    bm = _row_tile(t)

    def body(x_ref, w_ref, gq_ref, gkv_ref, c_ref, s_ref, down_ref, cq_ref, ckv_ref, kpe_ref):
        w = w_ref[...].reshape(D, ODD_IN)
        down = _dot(x_ref[...], w)
        down_ref[...] = down
        q = down[:, :Q_RANK]
        cq_ref[...] = (q * lax.rsqrt(jnp.mean(q * q, axis=-1, keepdims=True) + RMS_EPS) * gq_ref[...]).astype(BF16)
        kv = down[:, Q_RANK:Q_RANK + KV_RANK]
        ckv_ref[...] = (kv * lax.rsqrt(jnp.mean(kv * kv, axis=-1, keepdims=True) + RMS_EPS)
                        * gkv_ref[...]).astype(BF16)
        kpe_ref[...] = _rope(down[:, Q_RANK + KV_RANK:], c_ref[...], s_ref[...])

    row = lambda n: pl.BlockSpec((bm, n), lambda i: (i, 0))
    return pl.pallas_call(
        body, grid=(t // bm,),
        in_specs=[row(D), _full((N_DEV, D // N_DEV, ODD_IN)),
                  pl.BlockSpec((None, 1, Q_RANK), lambda i: (j, 0, 0)),
                  pl.BlockSpec((None, 1, KV_RANK), lambda i: (j, 0, 0)), row(ROPE // 2), row(ROPE // 2)],
        out_specs=[row(ODD_IN), row(Q_RANK), row(KV_RANK), row(ROPE)],
        out_shape=[S((t, ODD_IN), F32), S((t, Q_RANK), BF16), S((t, KV_RANK), BF16), S((t, ROPE), F32)],
        compiler_params=_cp("parallel"), name="down_norm")(xb, wdown_g, gq3, gkv3, cos, sin)


def _q_tile(t):
    return min(256, t // 2)


def _attn_probs(qn, qp, kn, kp, qs):
    s = (_dot_nt(qn, kn) + _dot_nt(qp, kp)) * ATT_SCALE
    rows = qs + lax.broadcasted_iota(jnp.int32, s.shape, 0)
    cols = lax.broadcasted_iota(jnp.int32, s.shape, 1)
    s = jnp.where(jnp.right_shift(cols, CHUNK_SHIFT) <= jnp.right_shift(rows, CHUNK_SHIFT), s, NEG)
    e = jnp.exp(s - jnp.max(s, axis=-1, keepdims=True))
    return e / jnp.sum(e, axis=-1, keepdims=True)


def _head_qkv(cq, ckv, kpe, c, s, wq_ref, wkv_ref):
    qn = _dot(cq, wq_ref[:, :NOPE]).astype(BF16)
    qp = _rope(_dot(cq, wq_ref[:, NOPE:]), c, s).astype(BF16)
    kn = _dot(ckv, wkv_ref[:, :NOPE]).astype(BF16)
    vv = _dot(ckv, wkv_ref[:, NOPE:]).astype(BF16)
    return qn, qp, kn, kpe.astype(BF16), vv


def _attn_in_specs(t):
    return [_full((t, Q_RANK)), _full((t, KV_RANK)), _full((t, ROPE)), _full((t, ROPE // 2)), _full((t, ROPE // 2)),
            pl.BlockSpec((None, Q_RANK, NOPE + ROPE), lambda h: (h, 0, 0)),
            pl.BlockSpec((None, KV_RANK, NOPE + VDIM), lambda h: (h, 0, 0)),
            pl.BlockSpec((None, VDIM, D), lambda h: (h, 0, 0))]


def _attn_fwd(cq, ckv, kpe, cos, sin, wqb_g, wkvb_g, wo_g):
    t = cq.shape[0]
    tq = _q_tile(t)

    def body(cq_ref, ckv_ref, kpe_ref, c_ref, s_ref, wq_ref, wkv_ref, wo_ref, o_ref, mix_ref):
        qn, qp, kn, kp, vv = _head_qkv(cq_ref[...], ckv_ref[...], kpe_ref[...], c_ref[...], s_ref[...],
                                       wq_ref, wkv_ref)
        for qs in range(0, t, tq):
            ke = qs + tq
            p = _attn_probs(qn[qs:ke], qp[qs:ke], kn[:ke], kp[:ke], qs)
            o_ref[qs:ke, :] = _dot(p.astype(BF16), vv[:ke]).astype(BF16)
        c = _dot(o_ref[...], wo_ref[...])

        @pl.when(pl.program_id(0) == 0)
        def _():
            mix_ref[...] = c

        @pl.when(pl.program_id(0) > 0)
        def _():
            mix_ref[...] += c

    return pl.pallas_call(
        body, grid=(MLA_HEADS,), in_specs=_attn_in_specs(t),
        out_specs=[pl.BlockSpec((None, t, VDIM), lambda h: (h, 0, 0)), _full((t, D))],
        out_shape=[S((MLA_HEADS, t, VDIM), BF16), S((t, D), F32)],
        compiler_params=_cp("arbitrary"), name="attn_fwd")(cq, ckv, kpe, cos, sin, wqb_g, wkvb_g, wo_g)


def _attn_bwd(cq, ckv, kpe, cos, sin, wqb_g, wkvb_g, wo_g, o, dzb):
    t = cq.shape[0]
    tq = _q_tile(t)

    def body(cq_ref, ckv_ref, kpe_ref, c_ref, s_ref, wq_ref, wkv_ref, wo_ref, o_ref, dz_ref,
             dwo_ref, dwq_ref, dwkv_ref, dcq_ref, dckv_ref, dkpe_ref, dkn_s, dkp_s, dv_s, dqn_s, dqp_s):
        cqv = cq_ref[...]
        ckvv = ckv_ref[...]
        c = c_ref[...]
        s = s_ref[...]
        qn, qp, kn, kp, vv = _head_qkv(cqv, ckvv, kpe_ref[...], c, s, wq_ref, wkv_ref)
        dzv = dz_ref[...]
        dwo_ref[...] = _dot_tn(o_ref[...], dzv).astype(BF16)
        do = _dot_nt(dzv, wo_ref[...]).astype(BF16)
        dkn_s[...] = jnp.zeros_like(dkn_s)
        dkp_s[...] = jnp.zeros_like(dkp_s)
        dv_s[...] = jnp.zeros_like(dv_s)
        for qs in range(0, t, tq):
            ke = qs + tq
            p = _attn_probs(qn[qs:ke], qp[qs:ke], kn[:ke], kp[:ke], qs)
            dp = _dot_nt(do[qs:ke], vv[:ke])
            ds = (p * (dp - jnp.sum(p * dp, axis=-1, keepdims=True)) * ATT_SCALE).astype(BF16)
            dqn_s[qs:ke, :] = _dot(ds, kn[:ke])
            dqp_s[qs:ke, :] = _dot(ds, kp[:ke])
            dkn_s[0:ke, :] += _dot_tn(ds, qn[qs:ke])
            dkp_s[0:ke, :] += _dot_tn(ds, qp[qs:ke])
            dv_s[0:ke, :] += _dot_tn(p.astype(BF16), do[qs:ke])
        dqn = dqn_s[...].astype(BF16)
        dqp = _rope_t(dqp_s[...], c, s).astype(BF16)
        dkn = dkn_s[...].astype(BF16)
        dvv = dv_s[...].astype(BF16)
        dwq_ref[:, :NOPE] = _dot_tn(cqv, dqn).astype(BF16)
        dwq_ref[:, NOPE:] = _dot_tn(cqv, dqp).astype(BF16)
        dwkv_ref[:, :NOPE] = _dot_tn(ckvv, dkn).astype(BF16)
        dwkv_ref[:, NOPE:] = _dot_tn(ckvv, dvv).astype(BF16)
        dcq = _dot_nt(dqn, wq_ref[:, :NOPE]) + _dot_nt(dqp, wq_ref[:, NOPE:])
        dckv = _dot_nt(dkn, wkv_ref[:, :NOPE]) + _dot_nt(dvv, wkv_ref[:, NOPE:])

        @pl.when(pl.program_id(0) == 0)
        def _():
            dcq_ref[...] = dcq
            dckv_ref[...] = dckv
            dkpe_ref[...] = dkp_s[...]

        @pl.when(pl.program_id(0) > 0)
        def _():
            dcq_ref[...] += dcq
            dckv_ref[...] += dckv
            dkpe_ref[...] += dkp_s[...]

    per_head = lambda a, b: pl.BlockSpec((None, a, b), lambda h: (h, 0, 0))
    return pl.pallas_call(
        body, grid=(MLA_HEADS,),
        in_specs=_attn_in_specs(t) + [per_head(t, VDIM), _full((t, D))],
        out_specs=[per_head(VDIM, D), per_head(Q_RANK, NOPE + ROPE), per_head(KV_RANK, NOPE + VDIM),
                   _full((t, Q_RANK)), _full((t, KV_RANK)), _full((t, ROPE))],
        out_shape=[S((MLA_HEADS, VDIM, D), BF16), S((MLA_HEADS, Q_RANK, NOPE + ROPE), BF16),
                   S((MLA_HEADS, KV_RANK, NOPE + VDIM), BF16),
                   S((t, Q_RANK), F32), S((t, KV_RANK), F32), S((t, ROPE), F32)],
        scratch_shapes=[pltpu.VMEM((t, NOPE), F32), pltpu.VMEM((t, ROPE), F32), pltpu.VMEM((t, VDIM), F32),
                        pltpu.VMEM((t, NOPE), F32), pltpu.VMEM((t, ROPE), F32)],
        compiler_params=_cp("arbitrary"), name="attn_bwd")(cq, ckv, kpe, cos, sin, wqb_g, wkvb_g, wo_g, o, dzb)


def _rms_bwd(down, dcq, dckv, dkpe, cos, sin, gq3, gkv3, j):
    t = down.shape[0]
    bm = _row_tile(t)

    def body(down_ref, dcq_ref, dckv_ref, dkpe_ref, c_ref, s_ref, gq_ref, gkv_ref, dd_ref, dgq_ref, dgkv_ref):
        @pl.when(pl.program_id(0) == 0)
        def _():
            dgq_ref[...] = jnp.zeros_like(dgq_ref)
            dgkv_ref[...] = jnp.zeros_like(dgkv_ref)

        def rms_b(x, dy, g):
            rstd = lax.rsqrt(jnp.mean(x * x, axis=-1, keepdims=True) + RMS_EPS)
            xh = x * rstd
            dyg = dy * g
            return rstd * (dyg - xh * jnp.mean(dyg * xh, axis=-1, keepdims=True)), jnp.sum(dy * xh, axis=0, keepdims=True)

        dq, dgq = rms_b(down_ref[:, :Q_RANK], dcq_ref[...], gq_ref[...])
        dkv, dgkv = rms_b(down_ref[:, Q_RANK:Q_RANK + KV_RANK], dckv_ref[...], gkv_ref[...])
        dgq_ref[...] += dgq
        dgkv_ref[...] += dgkv
        dd_ref[:, :Q_RANK] = dq.astype(BF16)
        dd_ref[:, Q_RANK:Q_RANK + KV_RANK] = dkv.astype(BF16)
        dd_ref[:, Q_RANK + KV_RANK:] = _rope_t(dkpe_ref[...], c_ref[...], s_ref[...]).astype(BF16)

    row = lambda n: pl.BlockSpec((bm, n), lambda i: (i, 0))
    return pl.pallas_call(
        body, grid=(t // bm,),
        in_specs=[row(ODD_IN), row(Q_RANK), row(KV_RANK), row(ROPE), row(ROPE // 2), row(ROPE // 2),
                  pl.BlockSpec((None, 1, Q_RANK), lambda i: (j, 0, 0)),
                  pl.BlockSpec((None, 1, KV_RANK), lambda i: (j, 0, 0))],
        out_specs=[row(ODD_IN), _full((1, Q_RANK)), _full((1, KV_RANK))],
        out_shape=[S((t, ODD_IN), BF16), S((1, Q_RANK), F32), S((1, KV_RANK), F32)],
        compiler_params=_cp("arbitrary"), name="rms_bwd")(down, dcq, dckv, dkpe, cos, sin, gq3, gkv3)


def _col_blocks(t, n, bn):
    return pl.BlockSpec((t, bn), lambda i: (0, i))


def _row_blocks(n, bm):
    return pl.BlockSpec((bm, n), lambda i: (i, 0))


def _local_step(x, pos2, tgt, small, weights_of, grads_done):
    t = x.shape[0]
    bm = _row_tile(t)
    inv_freq = (ROPE_THETA ** (-jnp.arange(0, ROPE, 2, dtype=F32) / ROPE)).reshape(1, ROPE // 2)
    cos, sin = _rope_tables(pos2, inv_freq)
    lru_p = {k: small[k] for k in ("conv_w", "conv_b", "w_a", "b_a", "w_x", "b_x", "lam")}

    saved = []
    y, yb = x, x.astype(BF16)
    for l in range(DEPTH):
        j = l // 2
        big = weights_of(l, y)
        sv = dict(xb=yb, big=big)
        if l % 2 == 0:
            proj = _mm(yb, big["win2d"], mode="nn", grid=(EVEN_IN // 512,), a_spec=_full((t, D)),
                       b_spec=_col_blocks(D, EVEN_IN, 512), out_shape=S((t, EVEN_IN), F32),
                       out_spec=_col_blocks(t, EVEN_IN, 512), name="even_proj")
            ycat = jnp.concatenate([_pool_fwd(proj, small["pool_w"], small["pool_scale"], j),
                                    _lru_fwd(proj, lru_p, j)], axis=1)
            mix = _mm(ycat, big["wout2d"], mode="nn", grid=(D // 512,), a_spec=_full((t, EVEN_MIX)),
                      b_spec=_col_blocks(EVEN_MIX, D, 512), out_shape=S((t, D), F32),
                      out_spec=_col_blocks(t, D, 512), name="even_out")
            sv.update(proj=proj, ycat=ycat)
        else:
            down, cq, ckv, kpe = _down_norm(yb, big["wdown"], small["gq"], small["gkv"], cos, sin, j)
            o, mix = _attn_fwd(cq, ckv, kpe, cos, sin, big["wqb"], big["wkvb"], big["wo"])
            sv.update(down=down, cq=cq, ckv=ckv, kpe=kpe, o=o)
        z1, y1, y1b = _resid_ln(y, mix, small["ln_mix_g"], small["ln_mix_b"], l, "resid_ln")
        ff = _mlp_fwd(y1b, big["w1"], big["w2"])
        z2, y, yb = _resid_ln(y1, ff, small["ln_ffn_g"], small["ln_ffn_b"], l, "resid_ln")
        sv.update(z1=z1, y1b=y1b, z2=z2)
        saved.append(sv)

    dy, loss_tile = _loss_grad(y, tgt)

    g = {k: [None] * n for k, n in (("ln_mix_g", 4), ("ln_mix_b", 4), ("ln_ffn_g", 4), ("ln_ffn_b", 4),
                                    ("pool_w", 2), ("pool_scale", 2), ("conv_w", 2), ("conv_b", 2),
                                    ("w_a", 2), ("b_a", 2), ("w_x", 2), ("b_x", 2), ("lam", 2),
                                    ("gq", 2), ("gkv", 2))}
    dep = None
    for l in reversed(range(DEPTH)):
        j = l // 2
        sv = saved[l]
        big = sv["big"]
        dz2, dz2b, g["ln_ffn_g"][l], g["ln_ffn_b"][l] = _ln_bwd(dy, sv["z2"], small["ln_ffn_g"], l, "ln_bwd", dep=dep)
        act, dh, dff = _mlp_bwd_dh(sv["y1b"], dz2b, big["w1"], big["w2"])
        dw1 = _mm(sv["y1b"], dh, mode="tn", grid=(N_DEV,), a_spec=_full((t, D)),
                  b_spec=_col_blocks(t, D_FF, FF_BLK), out_shape=S((N_DEV, D, FF_BLK), BF16),
                  out_spec=pl.BlockSpec((None, D, FF_BLK), lambda i: (i, 0, 0)), name="mlp_dw1")
        dw2 = _mm(act, dz2b, mode="tn", grid=(N_DEV,), a_spec=_col_blocks(t, D_FF, FF_BLK),
                  b_spec=_full((t, D)), out_shape=S((N_DEV, FF_BLK, D), BF16),
                  out_spec=pl.BlockSpec((None, FF_BLK, D), lambda i: (i, 0, 0)), name="mlp_dw2")
        dep = grads_done(l, dict(w1=dw1, w2=dw2))
        dz1, dz1b, g["ln_mix_g"][l], g["ln_mix_b"][l] = _ln_bwd(dff, sv["z1"], small["ln_mix_g"], l, "ln_bwd_res",
                                                                 r=dz2, dep=dep)
        if l % 2 == 0:
            wout = big["wout2d"]
            dycat = _mm(dz1b, wout, mode="nt", grid=(EVEN_MIX // 512,), a_spec=_full((t, D)),
                        b_spec=_row_blocks(D, 512), out_shape=S((t, EVEN_MIX), F32),
                        out_spec=_col_blocks(t, EVEN_MIX, 512), name="even_dycat")
            dwout = _mm(sv["ycat"], dz1b, mode="tn", grid=(EVEN_MIX // 512,), a_spec=_col_blocks(t, EVEN_MIX, 512),
                        b_spec=_full((t, D)), out_shape=S((EVEN_MIX, D), BF16), out_spec=_row_blocks(D, 512),
                        name="even_dwout")
            du_pool, g["pool_w"][j], g["pool_scale"][j] = _pool_bwd(sv["proj"], dycat, small["pool_w"],
                                                                   small["pool_scale"], j)
            (du_lru, du_gate, g["conv_w"][j], g["conv_b"][j], g["w_a"][j], g["b_a"][j], g["w_x"][j], g["b_x"][j],
             g["lam"][j]) = _lru_bwd(sv["proj"], dycat, lru_p, j)
            dproj = jnp.concatenate([du_pool, du_lru, du_gate], axis=1)
            dwin = _mm(sv["xb"], dproj, mode="tn", grid=(EVEN_IN // 512,), a_spec=_full((t, D)),
                       b_spec=_col_blocks(t, EVEN_IN, 512), out_shape=S((D, EVEN_IN), BF16),
                       out_spec=_col_blocks(D, EVEN_IN, 512), name="even_dwin")
            dep = grads_done(l, dict(win=dwin.reshape(D, N_DEV, EVEN_IN // N_DEV).transpose(1, 0, 2),
                                     wout=dwout.reshape(N_DEV, EVEN_MIX // N_DEV, D)))
            dy = _mm(dproj, big["win2d"], mode="nt", grid=(t // bm,), a_spec=_row_blocks(EVEN_IN, bm),
                     b_spec=_full((D, EVEN_IN)), out_shape=S((t, D), F32), out_spec=_row_blocks(D, bm),
                     add=dz1, add_spec=_row_blocks(D, bm), add_scale=ALPHA, name="even_dx")
        else:
            dwo, dwqb, dwkvb, dcq, dckv, dkpe = _attn_bwd(
                sv["cq"], sv["ckv"], sv["kpe"], cos, sin, big["wqb"], big["wkvb"], big["wo"], sv["o"], dz1b)
            ddown, g["gq"][j], g["gkv"][j] = _rms_bwd(sv["down"], dcq, dckv, dkpe, cos, sin, small["gq"],
                                                     small["gkv"], j)
            dwdown = _mm(sv["xb"], ddown, mode="tn", grid=(N_DEV,), a_spec=_col_blocks(t, D, D // N_DEV),
                         b_spec=_full((t, ODD_IN)), out_shape=S((N_DEV, D // N_DEV, ODD_IN), BF16),
                         out_spec=pl.BlockSpec((None, D // N_DEV, ODD_IN), lambda i: (i, 0, 0)),
                         name="odd_dwdown")
            dep = grads_done(l, dict(wdown=dwdown, wqb=dwqb, wkvb=dwkvb, wo=dwo))
            dy = _mm(ddown, big["wdown2d"], mode="nt", grid=(t // bm,), a_spec=_row_blocks(ODD_IN, bm),
                     b_spec=_full((D, ODD_IN)), out_shape=S((t, D), F32), out_spec=_row_blocks(D, bm),
                     add=dz1, add_spec=_row_blocks(D, bm), add_scale=ALPHA, name="odd_dx")
    return loss_tile[0, 0], dy, g


def _mesh_place():
    x, y, c = lax.axis_index("x"), lax.axis_index("y"), lax.axis_index("c")
    return x, y, c


def _peer(place, k):
    x, y, c = place
    return (1 - x if k & 4 else x, 1 - y if k & 2 else y, 1 - c if k & 1 else c)


def _index(place):
    x, y, c = place
    return 4 * x + 2 * y + c


ANY = pl.BlockSpec(memory_space=pl.ANY)


def _all_gather_big(shards, later):
    n = len(shards)
    n_all = n + len(later)

    def body(*refs):
        ins, outs = refs[:n_all], refs[n_all:2 * n_all]
        send, recv, local = refs[2 * n_all:]
        x, y, c = _mesh_place()
        me, sibling = (x, y, c), (x, y, 1 - c)
        chips = [(1 - x, y), (x, 1 - y), (1 - x, 1 - y)]

        def copy(w, k, block, to, src=None):
            dst = outs[w].at[_index(block)]
            return pltpu.make_async_remote_copy(src_ref=dst if src is None else src, dst_ref=dst, send_sem=send.at[w, k],
                                                recv_sem=recv.at[w, k], device_id=to, device_id_type=MESH)

        mine = [pltpu.make_async_copy(ins[w], outs[w].at[_index(me)], local.at[w]) for w in range(n_all)]
        for cp in mine:
            cp.start()
        first = []
        for w in range(n):
            first.append(copy(w, 0, me, sibling, src=ins[w]))
            first += [copy(w, 1 + j, me, (*chip, c), src=ins[w]) for j, chip in enumerate(chips)]
        for cp in first:
            cp.start()
        passed = []
        for w in range(n):
            for j, chip in enumerate(chips):
                copy(w, 1 + j, (*chip, c), me).wait_recv()
                cp = copy(w, 4 + j, (*chip, c), sibling)
                cp.start()
                passed.append(cp)
        for w in range(n):
            copy(w, 0, sibling, me).wait_recv()
            for j, chip in enumerate(chips):
                copy(w, 4 + j, (*chip, 1 - c), me).wait_recv()
        for cp in first + passed:
            cp.wait_send()
        for cp in mine:
            cp.wait()

    out = pl.pallas_call(
        body, in_specs=[ANY] * n_all, out_specs=[ANY] * n_all,
        out_shape=[S((N_DEV,) + s.shape, s.dtype) for s in list(shards) + list(later)],
        scratch_shapes=[pltpu.SemaphoreType.DMA((n, N_DEV - 1)), pltpu.SemaphoreType.DMA((n, N_DEV - 1)),
                        pltpu.SemaphoreType.DMA((n_all,))],
        compiler_params=pltpu.CompilerParams(has_side_effects=True), name="all_gather_big")(*shards, *later)
    return out[:n], out[n:]


def _shard_rows_tile(a):
    return max(d for d in range(16, 257, 16) if a % d == 0)


HBM = pl.BlockSpec(memory_space=pltpu.HBM)
SEM = pl.BlockSpec(memory_space=pltpu.SEMAPHORE)
DATAFLOW = pltpu.SideEffectType.DATAFLOW_SIDE_EFFECTING


def _in_hbm(a):
    return pltpu.with_memory_space_constraint(a, pltpu.HBM)


def _pair(w, k):
    return w * (N_DEV - 1) + k - 1


def _gather_blocks():
    return (lambda ref, peer, w: ref), (lambda ref, me, w: ref.at[me])


def _scatter_blocks(layers):
    return (lambda ref, peer, w: ref.at[peer]), (lambda ref, me, w: ref.at[me, layers[w]])


def _exchange_start(srcs, lands, src_block, land_block, name, after=None):
    n = len(srcs)
    n_in = 2 * n + (0 if after is None else 1)

    def body(*refs):
        ins, land = refs[:n], refs[n:2 * n]
        send, recv = refs[n_in], refs[n_in + 1]
        token = refs[-1]
        place = _mesh_place()
        me = _index(place)
        for k in range(1, N_DEV):
            peer = _peer(place, k)
            for w in range(n):
                pltpu.make_async_remote_copy(src_ref=src_block(ins[w], _index(peer), w), dst_ref=land_block(land[w], me, w),
                                             send_sem=send.at[_pair(w, k)], recv_sem=recv.at[_pair(w, k)],
                                             device_id=peer, device_id_type=MESH).start()
        token[...] = jnp.zeros_like(token)

    sems = pltpu.SemaphoreType.DMA((n * (N_DEV - 1),))
    thru = [pltpu.HBM(a.shape, a.dtype) for a in list(srcs) + list(lands)]
    out = pl.pallas_call(
        body, name=name, in_specs=[HBM] * (2 * n) + ([] if after is None else [ANY]),
        out_shape=(sems, sems, *thru, S((8, 128), F32)),
        out_specs=(SEM, SEM, *([HBM] * (2 * n)), pl.BlockSpec(memory_space=pltpu.VMEM)),
        input_output_aliases={i: 2 + i for i in range(2 * n)},
        compiler_params=pltpu.CompilerParams(has_side_effects=DATAFLOW),
    )(*[_in_hbm(a) for a in list(srcs) + list(lands)], *([] if after is None else [after]))
    return out[0], out[1], list(out[2:2 + n]), list(out[2 + n:2 + 2 * n]), out[-1]


def _exchange_wait(send, recv, srcs, lands, src_block, land_block, after, name):
    n = len(srcs)

    def body(*refs):
        ins, land = refs[:n], refs[n:2 * n]
        send_ref, recv_ref = refs[2 * n], refs[2 * n + 1]
        place = _mesh_place()
        me = _index(place)
        for k in range(1, N_DEV):
            peer = _peer(place, k)
            for w in range(n):
                cp = pltpu.make_async_remote_copy(src_ref=src_block(ins[w], _index(peer), w),
                                                  dst_ref=land_block(land[w], me, w), send_sem=send_ref.at[_pair(w, k)],
                                                  recv_sem=recv_ref.at[_pair(w, k)], device_id=peer, device_id_type=MESH)
                cp.wait_send()
                cp.wait_recv()

    thru = [pltpu.HBM(a.shape, a.dtype) for a in list(srcs) + list(lands)]
    out = pl.pallas_call(
        body, name=name, in_specs=[HBM] * (2 * n) + [SEM, SEM, ANY],
        out_shape=tuple(thru), out_specs=tuple([HBM] * (2 * n)),
        input_output_aliases={i: i for i in range(2 * n)},
        compiler_params=pltpu.CompilerParams(has_side_effects=DATAFLOW),
    )(*srcs, *lands, send, recv, after)
    return list(out[n:])


def _copy_own_blocks(srcs, lands, land_of, layer_of, name):
    n, nl = len(srcs), len(lands)

    def body(*refs):
        ins, outs, sem = refs[:n], refs[n + nl:n + 2 * nl], refs[-1]
        me = _index(_mesh_place())
        copies = [pltpu.make_async_copy(ins[i].at[me], outs[land_of[i]].at[me, layer_of[i]], sem.at[i])
                  for i in range(n)]
        for cp in copies:
            cp.start()
        for cp in copies:
            cp.wait()

    return pl.pallas_call(
        body, name=name, in_specs=[ANY] * (n + nl), out_specs=[ANY] * nl,
        out_shape=[S(a.shape, a.dtype) for a in lands], input_output_aliases={n + i: i for i in range(nl)},
        scratch_shapes=[pltpu.SemaphoreType.DMA((n,))],
        compiler_params=pltpu.CompilerParams(has_side_effects=True))(*srcs, *lands)


def _all_reduce_small(part, name):
    r = part.shape[1]

    def body(p_ref, o_ref, rbuf, send1, recv1, send2, recv2):
        place = _mesh_place()
        me = _index(place)
        rbuf[pl.ds(me, 1)] = p_ref[pl.ds(me, 1)]
        first = [pltpu.make_async_remote_copy(src_ref=p_ref.at[_index(_peer(place, k))], dst_ref=rbuf.at[me],
                                              send_sem=send1.at[k - 1], recv_sem=recv1.at[k - 1],
                                              device_id=_peer(place, k), device_id_type=MESH)
                 for k in range(1, N_DEV)]
        for cp in first:
            cp.start()
        for cp in first:
            cp.wait()
        acc = rbuf[0]
        for d in range(1, N_DEV):
            acc = acc + rbuf[d]
        o_ref[pl.ds(me, 1)] = acc[None]
        second = [pltpu.make_async_remote_copy(src_ref=o_ref.at[me], dst_ref=o_ref.at[me], send_sem=send2.at[k - 1],
                                               recv_sem=recv2.at[k - 1], device_id=_peer(place, k),
                                               device_id_type=MESH)
                  for k in range(1, N_DEV)]
        for cp in second:
            cp.start()
        for cp in second:
            cp.wait()

    vm = pl.BlockSpec(memory_space=pltpu.VMEM)
    return pl.pallas_call(
        body, in_specs=[vm], out_specs=vm, out_shape=S(part.shape, F32),
        scratch_shapes=[pltpu.VMEM(part.shape, F32)] + [pltpu.SemaphoreType.DMA((N_DEV - 1,))] * 4,
        compiler_params=pltpu.CompilerParams(has_side_effects=True, vmem_limit_bytes=VMEM_LIMIT), name=name)(part)


def _adamw(w, g, m, v):
    m = ADAM_B1 * m + (1.0 - ADAM_B1) * g
    v = ADAM_B2 * v + (1.0 - ADAM_B2) * (g * g)
    m_hat = m / (1.0 - ADAM_B1 ** ADAM_STEP)
    v_hat = v / (1.0 - ADAM_B2 ** ADAM_STEP)
    return -ADAM_LR * (m_hat / (jnp.sqrt(v_hat) + ADAM_EPS) + ADAM_WD * w), m, v


def _adam_big(parts, w, m, v, name):
    nl, a, b = w.shape
    ta = _shard_rows_tile(a)

    def body(p_ref, w_ref, m_ref, v_ref, g_ref, d_ref, mo_ref, vo_ref):
        g = p_ref[0].astype(F32)
        for s in range(1, N_DEV):
            g = g + p_ref[s].astype(F32)
        g_ref[...] = g
        d_ref[...], mo_ref[...], vo_ref[...] = _adamw(w_ref[...], g, m_ref[...], v_ref[...])

    blk = pl.BlockSpec((None, ta, b), lambda l, i: (l, i, 0))
    return pl.pallas_call(
        body, grid=(nl, a // ta), in_specs=[pl.BlockSpec((N_DEV, None, ta, b), lambda l, i: (0, l, i, 0)), blk, blk, blk],
        out_specs=[blk] * 4, out_shape=[S(w.shape, F32)] * 4, compiler_params=_cp("parallel", "parallel"),
        name=name)(parts, w, m, v)


def _adam_small(g, w, m, v):
    rows = g.shape[0]
    tr = 256

    def body(g_ref, w_ref, m_ref, v_ref, d_ref, mo_ref, vo_ref):
        d_ref[...], mo_ref[...], vo_ref[...] = _adamw(w_ref[...], g_ref[...], m_ref[...], v_ref[...])

    blk = pl.BlockSpec((tr, 128), lambda i: (i, 0))
    return pl.pallas_call(body, grid=(rows // tr,), in_specs=[blk] * 4, out_specs=[blk] * 3,
                          out_shape=[S(g.shape, F32)] * 3, compiler_params=_cp("parallel"), name="adam_small")(g, w, m, v)


BIG = ("even_w_in", "even_w_out", "mla_w_down", "mla_w_qb", "mla_w_kvb", "mla_w_o", "mlp_w1", "mlp_w2")
BIG_KEY = dict(even_w_in="win", even_w_out="wout", mla_w_down="wdown", mla_w_qb="wqb", mla_w_kvb="wkvb",
               mla_w_o="wo", mlp_w1="w1", mlp_w2="w2")
SMALL = (("ln_mix_g", "ln_mix_g", None), ("ln_mix_b", "ln_mix_b", None), ("ln_ffn_g", "ln_ffn_g", None),
         ("ln_ffn_b", "ln_ffn_b", None), ("pool_w", "pool_w", None), ("pool_scale", "pool_scale", None),
         ("lru_conv_w", "conv_w", 2), ("lru_conv_b", "conv_b", None), ("lru_w_a", "w_a", None),
         ("lru_b_a", "b_a", None), ("lru_w_x", "w_x", None), ("lru_b_x", "b_x", None), ("lru_lambda", "lam", None),
         ("mla_q_norm_g", "gq", 1), ("mla_kv_norm_g", "gkv", 1))
WEIGHTS = ("ln_mix_g", "ln_mix_b", "ln_ffn_g", "ln_ffn_b", "even_w_in", "pool_w", "pool_scale", "lru_conv_w",
           "lru_conv_b", "lru_w_a", "lru_b_a", "lru_w_x", "lru_b_x", "lru_lambda", "even_w_out", "mla_w_down",
           "mla_q_norm_g", "mla_kv_norm_g", "mla_w_qb", "mla_w_kvb", "mla_w_o", "mlp_w1", "mlp_w2")
ALL_AXES = ("x", "y", "c")


def _layer_weights(l):
    j = l // 2
    if l % 2 == 0:
        mixer = [("win", "even_w_in", j), ("wout", "even_w_out", j)]
    else:
        mixer = [("wdown", "mla_w_down", j), ("wqb", "mla_w_qb", j), ("wkvb", "mla_w_kvb", j), ("wo", "mla_w_o", j)]
    return mixer + [("w1", "mlp_w1", l), ("w2", "mlp_w2", l)]


def _pack(arrays, multiple):
    flat = jnp.concatenate([a.reshape(-1) for a in arrays])
    pad = (-flat.shape[0]) % multiple
    return jnp.pad(flat, (0, pad))


def _unpack(flat, shapes):
    out, at = [], 0
    for shp in shapes:
        n = 1
        for s in shp:
            n *= s
        out.append(flat[at:at + n].reshape(shp))
        at += n
    return out


def _global_shape(local_shape, axis):
    if axis is None:
        return tuple(local_shape)
    return tuple(s * N_DEV if i == axis else s for i, s in enumerate(local_shape))


def _step(x, positions, tgt, w, m, v):
    t = x.shape[1]
    me = _index(_mesh_place())

    sharded = [(name, axis) for name, _, axis in SMALL if axis is not None]
    zeros_with_mine = [lax.dynamic_update_slice_in_dim(jnp.zeros(_global_shape(w[name].shape, axis), F32), w[name],
                                                       me * w[name].shape[axis], axis) for name, axis in sharded]
    chunk = N_DEV * 8 * 128
    gathered = _all_reduce_small(_pack(zeros_with_mine, chunk).reshape(N_DEV, -1, 128), "gather_small")
    full = dict(zip([name for name, _ in sharded],
                    _unpack(gathered.reshape(-1), [_global_shape(w[name].shape, axis) for name, axis in sharded])))

    shards = [[w[name][i].astype(BF16) for _, name, i in _layer_weights(l)] for l in range(DEPTH)]
    first, later = _all_gather_big(shards[0], [s for l in range(1, DEPTH) for s in shards[l]])
    gather_src, gather_land = _gather_blocks()
    flights, token, at = {}, first[0], 0
    for l in range(1, DEPTH):
        lands = later[at:at + len(shards[l])]
        at += len(shards[l])
        send, recv, srcs, lands, token = _exchange_start(shards[l], lands, gather_src, gather_land,
                                                         "gather_start_%d" % l, after=token)
        flights[l] = (send, recv, srcs, lands)

    def weights_of(l, after):
        arrays = first if l == 0 else _exchange_wait(*flights[l], gather_src, gather_land, after, "gather_wait_%d" % l)
        big = {key: a for (key, _, _), a in zip(_layer_weights(l), arrays)}
        if l % 2 == 0:
            big["win2d"] = big["win"].transpose(1, 0, 2).reshape(D, EVEN_IN)
            big["wout2d"] = big["wout"].reshape(EVEN_MIX, D)
        else:
            big["wdown2d"] = big["wdown"].reshape(D, ODD_IN)
        return big

    zone = {name: lax.empty((N_DEV,) + w[name].shape, BF16) for name in BIG}
    name_of = {key: name for name, key in BIG_KEY.items()}
    sent = []

    def grads_done(l, grads):
        keys = list(grads)
        index = {key: i for key, _, i in _layer_weights(l)}
        layers = [index[key] for key in keys]
        src_b, land_b = _scatter_blocks(layers)
        send, recv, srcs, lands, tok = _exchange_start([grads[k] for k in keys], [zone[name_of[k]] for k in keys],
                                                       src_b, land_b, "scatter_start_%d_%s" % (l, keys[0]))
        for k, land in zip(keys, lands):
            zone[name_of[k]] = land
        sent.append((send, recv, srcs, keys, layers))
        return tok

    row3 = lambda a: a.reshape(a.shape[0], 1, a.shape[1])
    small = dict(ln_mix_g=row3(w["ln_mix_g"]), ln_mix_b=row3(w["ln_mix_b"]), ln_ffn_g=row3(w["ln_ffn_g"]),
                 ln_ffn_b=row3(w["ln_ffn_b"]), pool_w=w["pool_w"], pool_scale=row3(w["pool_scale"]),
                 conv_w=full["lru_conv_w"], conv_b=row3(w["lru_conv_b"]), w_a=w["lru_w_a"], b_a=row3(w["lru_b_a"]),
                 w_x=w["lru_w_x"], b_x=row3(w["lru_b_x"]), lam=row3(w["lru_lambda"]),
                 gq=row3(full["mla_q_norm_g"]), gkv=row3(full["mla_kv_norm_g"]))

    loss_part, grad_x, g = _local_step(x[0] + token[0, 0], positions.reshape(t, 1), tgt[0], small, weights_of,
                                       grads_done)
    loss = lax.psum(loss_part, ALL_AXES)

    names = list(BIG)
    own = [(src, names.index(name_of[k]), layer) for _, _, srcs, keys, layers in sent
           for src, k, layer in zip(srcs, keys, layers)]
    lands = _copy_own_blocks([o[0] for o in own], [zone[name] for name in names], [o[1] for o in own],
                             [o[2] for o in own], "scatter_own")
    zone = dict(zip(names, lands))
    after = grad_x
    for n_flight, (send, recv, srcs, keys, layers) in enumerate(sent):
        src_b, land_b = _scatter_blocks(layers)
        lands = _exchange_wait(send, recv, srcs, [zone[name_of[k]] for k in keys], src_b, land_b, after,
                               "scatter_wait_%d" % n_flight)
        for k, land in zip(keys, lands):
            zone[name_of[k]] = land
        after = lands[0]
    out = {}
    for name in BIG:
        out[name] = _adam_big(zone[name], w[name], m[name], v[name], "adam_" + name)

    local_g = [jnp.stack(g[key]).reshape(_global_shape(w[name].shape, axis)) for name, key, axis in SMALL]
    reduced = _all_reduce_small(_pack(local_g, chunk).reshape(N_DEV, -1, 128), "all_reduce_small")
    reduced = _unpack(reduced.reshape(-1), [a.shape for a in local_g])
    mine = [a if axis is None else lax.dynamic_slice_in_dim(a, me * w[name].shape[axis], w[name].shape[axis], axis)
            for a, (name, _, axis) in zip(reduced, SMALL)]
    tile = 256 * 128
    packed = [_pack(arrs, tile).reshape(-1, 128)
              for arrs in (mine, [w[n] for n, _, _ in SMALL], [m[n] for n, _, _ in SMALL], [v[n] for n, _, _ in SMALL])]
    shapes = [w[n].shape for n, _, _ in SMALL]
    d_s, m_s, v_s = (_unpack(a.reshape(-1), shapes) for a in _adam_small(*packed))
    for i, (name, _, _) in enumerate(SMALL):
        out[name] = (mine[i], d_s[i], m_s[i], v_s[i])

    return (loss, grad_x[None]) + tuple(out[name][i] for i in range(4) for name in WEIGHTS)


def kernel(x, positions, ln_mix_g, ln_mix_b, ln_ffn_g, ln_ffn_b, even_w_in, pool_w, pool_scale, lru_conv_w, lru_conv_b, lru_w_a, lru_b_a, lru_w_x, lru_b_x, lru_lambda, even_w_out, mla_w_down, mla_q_norm_g, mla_kv_norm_g, mla_w_qb, mla_w_kvb, mla_w_o, mlp_w1, mlp_w2, loss_target, m_ln_mix_g, m_ln_mix_b, m_ln_ffn_g, m_ln_ffn_b, m_even_w_in, m_pool_w, m_pool_scale, m_lru_conv_w, m_lru_conv_b, m_lru_w_a, m_lru_b_a, m_lru_w_x, m_lru_b_x, m_lru_lambda, m_even_w_out, m_mla_w_down, m_mla_q_norm_g, m_mla_kv_norm_g, m_mla_w_qb, m_mla_w_kvb, m_mla_w_o, m_mlp_w1, m_mlp_w2, v_ln_mix_g, v_ln_mix_b, v_ln_ffn_g, v_ln_ffn_b, v_even_w_in, v_pool_w, v_pool_scale, v_lru_conv_w, v_lru_conv_b, v_lru_w_a, v_lru_b_a, v_lru_w_x, v_lru_b_x, v_lru_lambda, v_even_w_out, v_mla_w_down, v_mla_q_norm_g, v_mla_kv_norm_g, v_mla_w_qb, v_mla_w_kvb, v_mla_w_o, v_mlp_w1, v_mlp_w2):
    w = dict(zip(WEIGHTS, (ln_mix_g, ln_mix_b, ln_ffn_g, ln_ffn_b, even_w_in, pool_w, pool_scale, lru_conv_w,
                           lru_conv_b, lru_w_a, lru_b_a, lru_w_x, lru_b_x, lru_lambda, even_w_out, mla_w_down,
                           mla_q_norm_g, mla_kv_norm_g, mla_w_qb, mla_w_kvb, mla_w_o, mlp_w1, mlp_w2)))
    m = dict(zip(WEIGHTS, (m_ln_mix_g, m_ln_mix_b, m_ln_ffn_g, m_ln_ffn_b, m_even_w_in, m_pool_w, m_pool_scale,
                           m_lru_conv_w, m_lru_conv_b, m_lru_w_a, m_lru_b_a, m_lru_w_x, m_lru_b_x, m_lru_lambda,
                           m_even_w_out, m_mla_w_down, m_mla_q_norm_g, m_mla_kv_norm_g, m_mla_w_qb, m_mla_w_kvb,
                           m_mla_w_o, m_mlp_w1, m_mlp_w2)))
    v = dict(zip(WEIGHTS, (v_ln_mix_g, v_ln_mix_b, v_ln_ffn_g, v_ln_ffn_b, v_even_w_in, v_pool_w, v_pool_scale,
                           v_lru_conv_w, v_lru_conv_b, v_lru_w_a, v_lru_b_a, v_lru_w_x, v_lru_b_x, v_lru_lambda,
                           v_even_w_out, v_mla_w_down, v_mla_q_norm_g, v_mla_kv_norm_g, v_mla_w_qb, v_mla_w_kvb,
                           v_mla_w_o, v_mlp_w1, v_mlp_w2)))
    return _step(x, positions, loss_target, w, m, v)
```

```python
import functools

import jax
import jax.numpy as jnp
from jax import lax
from jax.experimental import pallas as pl
from jax.experimental.pallas import tpu as pltpu

F32 = jnp.float32
BF16 = jnp.bfloat16
S = jax.ShapeDtypeStruct

D = 1024
DEPTH = 4
N_DEV = 8
CHUNK_SHIFT = 6
POOL_WINDOWS = (2, 4, 8, 16)
POOL_W = 512
LRU_W = 1024
LRU_HEADS = 8
HEAD = 128
LRU_C = 8.0
EVEN_IN = 2560
EVEN_MIX = 1536
MLA_HEADS = 8
NOPE = 128
ROPE = 64
VDIM = 128
Q_RANK = 384
KV_RANK = 256
ODD_IN = 704
D_FF = 4096
FF_BLK = D_FF // N_DEV
ROPE_THETA = 10000.0
ALPHA = (2 * DEPTH) ** 0.25
LN_EPS = 1e-5
RMS_EPS = 1e-6
ATT_SCALE = (NOPE + ROPE) ** -0.5
NEG = float(jnp.finfo(jnp.float32).min)
ADAM_LR = 0.001
ADAM_B1 = 0.9
ADAM_B2 = 0.999
ADAM_EPS = 1e-08
ADAM_WD = 0.01
ADAM_STEP = 10
V7X_VMEM_BYTES = 64 * 1024 * 1024
VMEM_LIMIT = V7X_VMEM_BYTES - 8 * 1024 * 1024
MESH = pl.DeviceIdType.MESH


def _cp(*sem):
    return pltpu.CompilerParams(dimension_semantics=sem if sem else None, vmem_limit_bytes=VMEM_LIMIT)


def _dot(a, b):
    return jnp.dot(a, b, preferred_element_type=F32)


def _dot_nt(a, b):
    return lax.dot_general(a, b, (((1,), (1,)), ((), ())), preferred_element_type=F32)


def _dot_tn(a, b):
    return lax.dot_general(a, b, (((0,), (0,)), ((), ())), preferred_element_type=F32)


def _full(shape):
    return pl.BlockSpec(shape, lambda *_: (0,) * len(shape))


def _mm(a, b, *, mode, grid, a_spec, b_spec, out_shape, out_spec, name, add=None, add_spec=None, add_scale=1.0):
    dot = {"nn": _dot, "nt": _dot_nt, "tn": _dot_tn}[mode]

    def body(*refs):
        if add is None:
            a_ref, b_ref, o_ref = refs
            acc = dot(a_ref[...].astype(BF16), b_ref[...].astype(BF16))
        else:
            a_ref, b_ref, add_ref, o_ref = refs
            acc = dot(a_ref[...].astype(BF16), b_ref[...].astype(BF16)) + add_scale * add_ref[...]
        o_ref[...] = acc.astype(o_ref.dtype)

    ops = (a, b) if add is None else (a, b, add)
    specs = [a_spec, b_spec] if add is None else [a_spec, b_spec, add_spec]
    return pl.pallas_call(body, grid=grid, in_specs=specs, out_specs=out_spec, out_shape=out_shape,
                          compiler_params=_cp(*(("parallel",) * len(grid))), name=name)(*ops)


def _ln_stats(z):
    mu = jnp.mean(z, axis=-1, keepdims=True)
    zc = z - mu
    var = jnp.mean(zc * zc, axis=-1, keepdims=True)
    rstd = lax.rsqrt(var + LN_EPS)
    return zc * rstd, rstd


def _row_tile(t):
    return min(512, t)


def _resid_ln(x, mix, g3, b3, l, name):
    t = x.shape[0]
    bm = _row_tile(t)

    def body(x_ref, m_ref, g_ref, b_ref, z_ref, y_ref, yb_ref):
        z = ALPHA * x_ref[...] + m_ref[...]
        xh, _ = _ln_stats(z)
        y = xh * g_ref[...] + b_ref[...]
        z_ref[...] = z
        y_ref[...] = y
        yb_ref[...] = y.astype(BF16)

    row = pl.BlockSpec((bm, D), lambda i: (i, 0))
    vec = pl.BlockSpec((None, 1, D), lambda i: (l, 0, 0))
    return pl.pallas_call(body, grid=(t // bm,), in_specs=[row, row, vec, vec], out_specs=[row, row, row],
                          out_shape=[S((t, D), F32), S((t, D), F32), S((t, D), BF16)],
                          compiler_params=_cp("parallel"), name=name)(x, mix, g3, b3)


def _ln_bwd(d, z, g3, l, name, r=None, dep=None):
    t = z.shape[0]
    bm = _row_tile(t)

    def body(*refs):
        refs = list(refs)
        d_ref = refs.pop(0)
        dy = d_ref[...]
        if r is not None:
            dy = dy + ALPHA * refs.pop(0)[...]
        z_ref, g_ref = refs.pop(0), refs.pop(0)
        if dep is not None:
            refs.pop(0)
        dz_ref, dzb_ref, dg_ref, db_ref = refs
        xh, rstd = _ln_stats(z_ref[...])
        dyg = dy * g_ref[...]
        m1 = jnp.mean(dyg, axis=-1, keepdims=True)
        m2 = jnp.mean(dyg * xh, axis=-1, keepdims=True)
        dz = rstd * (dyg - m1 - xh * m2)
        dz_ref[...] = dz
        dzb_ref[...] = dz.astype(BF16)

        @pl.when(pl.program_id(0) == 0)
        def _():
            dg_ref[...] = jnp.zeros_like(dg_ref)
            db_ref[...] = jnp.zeros_like(db_ref)

        dg_ref[...] += jnp.sum(dy * xh, axis=0, keepdims=True)
        db_ref[...] += jnp.sum(dy, axis=0, keepdims=True)

    row = pl.BlockSpec((bm, D), lambda i: (i, 0))
    vec = pl.BlockSpec((None, 1, D), lambda i: (l, 0, 0))
    acc = pl.BlockSpec((1, D), lambda i: (0, 0))
    ops = [d, z, g3] if r is None else [d, r, z, g3]
    specs = [row, row, vec] if r is None else [row, row, row, vec]
    if dep is not None:
        ops.append(dep)
        specs.append(_full(dep.shape))
    return pl.pallas_call(body, grid=(t // bm,), in_specs=specs, out_specs=[row, row, acc, acc],
                          out_shape=[S((t, D), F32), S((t, D), BF16), S((1, D), F32), S((1, D), F32)],
                          compiler_params=_cp("arbitrary"), name=name)(*ops)


def _loss_grad(y, tgt):
    t = y.shape[0]
    bm = _row_tile(t)

    def body(y_ref, t_ref, dy_ref, loss_ref, acc_ref):
        i = pl.program_id(0)
        e = y_ref[...] - t_ref[...]
        dy_ref[...] = e * (1.0 / D)

        @pl.when(i == 0)
        def _():
            acc_ref[...] = jnp.zeros_like(acc_ref)

        acc_ref[...] += jnp.sum(e * e, axis=0, keepdims=True)

        @pl.when(i == pl.num_programs(0) - 1)
        def _():
            loss_ref[...] = jnp.full(loss_ref.shape, (0.5 / D) * jnp.sum(acc_ref[...]), F32)

    row = pl.BlockSpec((bm, D), lambda i: (i, 0))
    return pl.pallas_call(body, grid=(t // bm,), in_specs=[row, row],
                          out_specs=[row, pl.BlockSpec((1, 128), lambda i: (0, 0))],
                          out_shape=[S((t, D), F32), S((1, 128), F32)],
                          scratch_shapes=[pltpu.VMEM((1, D), F32)],
                          compiler_params=_cp("arbitrary"), name="loss_grad")(y, tgt)


def _mlp_fwd(yb, w1g, w2g):
    t = yb.shape[0]
    bm = _row_tile(t)

    def body(y_ref, w1_ref, w2_ref, o_ref):
        j = pl.program_id(1)
        h = jnp.maximum(_dot(y_ref[...], w1_ref[...]), 0.0)
        c = _dot((h * h).astype(BF16), w2_ref[...])

        @pl.when(j == 0)
        def _():
            o_ref[...] = c

        @pl.when(j > 0)
        def _():
            o_ref[...] += c

    return pl.pallas_call(
        body, grid=(t // bm, N_DEV),
        in_specs=[pl.BlockSpec((bm, D), lambda i, j: (i, 0)),
                  pl.BlockSpec((None, D, FF_BLK), lambda i, j: (j, 0, 0)),
                  pl.BlockSpec((None, FF_BLK, D), lambda i, j: (j, 0, 0))],
        out_specs=pl.BlockSpec((bm, D), lambda i, j: (i, 0)),
        out_shape=S((t, D), F32), compiler_params=_cp("parallel", "arbitrary"), name="mlp_fwd")(yb, w1g, w2g)


def _mlp_bwd_dh(yb, dzb, w1g, w2g):
    t = yb.shape[0]
    bm = _row_tile(t)

    def body(y_ref, dz_ref, w1_ref, w2_ref, a_ref, dh_ref, acc_ref):
        j = pl.program_id(1)
        r = jnp.maximum(_dot(y_ref[...], w1_ref[...]), 0.0)
        a_ref[...] = (r * r).astype(BF16)
        da = _dot_nt(dz_ref[...], w2_ref[...])
        dh = (da * (2.0 * r)).astype(BF16)
        dh_ref[...] = dh
        c = _dot_nt(dh, w1_ref[...])

        @pl.when(j == 0)
        def _():
            acc_ref[...] = c

        @pl.when(j > 0)
        def _():
            acc_ref[...] += c

    row = pl.BlockSpec((bm, D), lambda i, j: (i, 0))
    hid = pl.BlockSpec((bm, FF_BLK), lambda i, j: (i, j))
    return pl.pallas_call(
        body, grid=(t // bm, N_DEV),
        in_specs=[row, row,
                  pl.BlockSpec((None, D, FF_BLK), lambda i, j: (j, 0, 0)),
                  pl.BlockSpec((None, FF_BLK, D), lambda i, j: (j, 0, 0))],
        out_specs=[hid, hid, row],
        out_shape=[S((t, D_FF), BF16), S((t, D_FF), BF16), S((t, D), F32)],
        compiler_params=_cp("parallel", "arbitrary"), name="mlp_bwd_dh")(yb, dzb, w1g, w2g)


def _shift_dn(x, k, rows, fill=0.0):
    return jnp.where(rows >= k, pltpu.roll(x, k, 0), fill)


def _shift_up(x, k, rows, fill=0.0):
    t = x.shape[0]
    return jnp.where(rows < t - k, pltpu.roll(x, t - k, 0), fill)


def _scan_dn(a, b, rows):
    k = 1
    t = a.shape[0]
    while k < t:
        b = a * _shift_dn(b, k, rows) + b
        if 2 * k < t:
            a = a * _shift_dn(a, k, rows, 1.0)
        k *= 2
    return b


def _scan_up(a, b, rows):
    k = 1
    t = a.shape[0]
    while k < t:
        b = a * _shift_up(b, k, rows) + b
        if 2 * k < t:
            a = a * _shift_up(a, k, rows, 1.0)
        k *= 2
    return b


def _window_sum_dn(x, w, rows):
    k = 1
    while k < w:
        x = x + _shift_dn(x, k, rows)
        k *= 2
    return x


def _window_sum_up(x, w, rows):
    k = 1
    while k < w:
        x = x + _shift_up(x, k, rows)
        k *= 2
    return x


def _pool_diff(u, w, rows):
    inv_count = 1.0 / jnp.minimum(rows + 1, w).astype(F32)
    return _window_sum_dn(u, w, rows) * inv_count - u, inv_count


def _pool_fwd(proj, pool_w, pool_scale3, j):
    t = proj.shape[0]

    def body(u_ref, w_ref, s_ref, y_ref):
        rows = lax.broadcasted_iota(jnp.int32, (t, HEAD), 0)
        for g, w in enumerate(POOL_WINDOWS):
            cols = slice(g * HEAD, (g + 1) * HEAD)
            d, _ = _pool_diff(u_ref[:, cols], w, rows)
            y = _dot(d.astype(BF16), w_ref[g].astype(BF16)) * s_ref[:, cols]
            y_ref[:, cols] = y.astype(BF16)

    return pl.pallas_call(
        body, grid=(1,),
        in_specs=[pl.BlockSpec((t, POOL_W), lambda i: (0, 0)),
                  pl.BlockSpec((None, 4, HEAD, HEAD), lambda i: (j, 0, 0, 0)),
                  pl.BlockSpec((None, 1, POOL_W), lambda i: (j, 0, 0))],
        out_specs=pl.BlockSpec((t, POOL_W), lambda i: (0, 0)),
        out_shape=S((t, POOL_W), BF16), compiler_params=_cp("arbitrary"), name="pool_fwd")(proj, pool_w, pool_scale3)


def _pool_bwd(proj, dycat, pool_w, pool_scale3, j):
    t = proj.shape[0]

    def body(u_ref, dy_ref, w_ref, s_ref, du_ref, dw_ref, ds_ref):
        rows = lax.broadcasted_iota(jnp.int32, (t, HEAD), 0)
        for g, w in enumerate(POOL_WINDOWS):
            cols = slice(g * HEAD, (g + 1) * HEAD)
            d, inv_count = _pool_diff(u_ref[:, cols], w, rows)
            db = d.astype(BF16)
            wg = w_ref[g].astype(BF16)
            dy = dy_ref[:, cols]
            ds_ref[:, cols] = jnp.sum(dy * _dot(db, wg), axis=0, keepdims=True)
            dzz = (dy * s_ref[:, cols]).astype(BF16)
            dw_ref[g] = _dot_tn(db, dzz)
            dd = _dot_nt(dzz, wg)
            du_ref[:, cols] = (_window_sum_up(dd * inv_count, w, rows) - dd).astype(BF16)

    return pl.pallas_call(
        body, grid=(1,),
        in_specs=[pl.BlockSpec((t, POOL_W), lambda i: (0, 0)),
                  pl.BlockSpec((t, POOL_W), lambda i: (0, 0)),
                  pl.BlockSpec((None, 4, HEAD, HEAD), lambda i: (j, 0, 0, 0)),
                  pl.BlockSpec((None, 1, POOL_W), lambda i: (j, 0, 0))],
        out_specs=[pl.BlockSpec((t, POOL_W), lambda i: (0, 0)), _full((4, HEAD, HEAD)), _full((1, POOL_W))],
        out_shape=[S((t, POOL_W), BF16), S((4, HEAD, HEAD), F32), S((1, POOL_W), F32)],
        compiler_params=_cp("arbitrary"), name="pool_bwd")(proj, dycat, pool_w, pool_scale3)


GELU_C = 0.7978845608028654
GELU_K = 0.044715


def _gelu(x):
    th = jnp.tanh(GELU_C * (x + GELU_K * x * x * x))
    return 0.5 * x * (1.0 + th), th


def _lru_forward(u, gate, cw, cb, wa, ba, wx, bx, lam, rows):
    v = cw[3:4] * u + cw[2:3] * _shift_dn(u, 1, rows) + cw[1:2] * _shift_dn(u, 2, rows) \
        + cw[0:1] * _shift_dn(u, 3, rows) + cb
    vb = v.astype(BF16)
    r = jax.nn.sigmoid(_dot(vb, wa) + ba)
    i = jax.nn.sigmoid(_dot(vb, wx) + bx)
    sp = jnp.maximum(-lam, 0.0) + jnp.log1p(jnp.exp(-jnp.abs(lam)))
    log_a = (-LRU_C) * r * sp
    a = jnp.exp(log_a)
    one_m_a2 = -jnp.tanh(log_a) * (a * a + 1.0)
    mult = jnp.sqrt(one_m_a2)
    h = _scan_dn(a, mult * (i * v), rows)
    gl, th = _gelu(gate)
    return dict(v=v, vb=vb, r=r, i=i, sp=sp, a=a, mult=mult, h=h, gl=gl, th=th)


def _lru_specs(t, j, col0_u, col0_g):
    blk = lambda c0: pl.BlockSpec((t, HEAD), lambda h: (0, c0 + h))
    vec = pl.BlockSpec((None, 1, HEAD), lambda h: (j, 0, h))
    return [blk(col0_u), blk(col0_g),
            pl.BlockSpec((None, 4, HEAD), lambda h: (j, 0, h)), vec,
            pl.BlockSpec((None, None, HEAD, HEAD), lambda h: (j, h, 0, 0)), vec,
            pl.BlockSpec((None, None, HEAD, HEAD), lambda h: (j, h, 0, 0)), vec, vec]


def _lru_fwd(proj, p, j):
    t = proj.shape[0]

    def body(u_ref, g_ref, cw_ref, cb_ref, wa_ref, ba_ref, wx_ref, bx_ref, lam_ref, y_ref):
        rows = lax.broadcasted_iota(jnp.int32, (t, HEAD), 0)
        f = _lru_forward(u_ref[...], g_ref[...], cw_ref[...], cb_ref[...], wa_ref[...].astype(BF16), ba_ref[...],
                         wx_ref[...].astype(BF16), bx_ref[...], lam_ref[...], rows)
        y_ref[...] = (f["h"] * f["gl"]).astype(BF16)

    return pl.pallas_call(
        body, grid=(LRU_HEADS,), in_specs=_lru_specs(t, j, POOL_W // HEAD, (POOL_W + LRU_W) // HEAD),
        out_specs=pl.BlockSpec((t, HEAD), lambda h: (0, h)), out_shape=S((t, LRU_W), BF16),
        compiler_params=_cp("parallel"), name="lru_fwd")(
            proj, proj, p["conv_w"], p["conv_b"], p["w_a"], p["b_a"], p["w_x"], p["b_x"], p["lam"])


def _lru_bwd(proj, dycat, p, j):
    t = proj.shape[0]

    def body(u_ref, g_ref, cw_ref, cb_ref, wa_ref, ba_ref, wx_ref, bx_ref, lam_ref, dy_ref,
             du_ref, dgate_ref, dcw_ref, dcb_ref, dwa_ref, dba_ref, dwx_ref, dbx_ref, dlam_ref):
        rows = lax.broadcasted_iota(jnp.int32, (t, HEAD), 0)
        u = u_ref[...]
        gate = g_ref[...]
        cw = cw_ref[...]
        wa = wa_ref[...].astype(BF16)
        wx = wx_ref[...].astype(BF16)
        lam = lam_ref[...]
        f = _lru_forward(u, gate, cw, cb_ref[...], wa, ba_ref[...], wx, bx_ref[...], lam, rows)
        v, r, i, a, mult, h, th = f["v"], f["r"], f["i"], f["a"], f["mult"], f["h"], f["th"]
        dy = dy_ref[...]
        dgl = 0.5 * (1.0 + th) + 0.5 * gate * (1.0 - th * th) * GELU_C * (1.0 + 3.0 * GELU_K * gate * gate)
        dgate_ref[...] = (dy * h * dgl).astype(BF16)
        g = _scan_up(_shift_up(a, 1, rows), dy * f["gl"], rows)
        da = g * _shift_dn(h, 1, rows)
        iv = i * v
        dmult = g * iv
        di = g * mult * v
        dv = g * mult * i
        dlog_a = da * a - dmult * (a * a) / mult
        dr = dlog_a * (-LRU_C) * f["sp"]
        dsp = jnp.sum(dlog_a * (-LRU_C) * r, axis=0, keepdims=True)
        dlam_ref[...] = -dsp * jax.nn.sigmoid(-lam)
        dpa = dr * r * (1.0 - r)
        dpx = di * i * (1.0 - i)
        dpab = dpa.astype(BF16)
        dpxb = dpx.astype(BF16)
        dwa_ref[...] = _dot_tn(f["vb"], dpab)
        dwx_ref[...] = _dot_tn(f["vb"], dpxb)
        dba_ref[...] = jnp.sum(dpa, axis=0, keepdims=True)
        dbx_ref[...] = jnp.sum(dpx, axis=0, keepdims=True)
        dv = dv + _dot_nt(dpab, wa) + _dot_nt(dpxb, wx)
        dcb_ref[...] = jnp.sum(dv, axis=0, keepdims=True)
        du = cw[3:4] * dv
        dcw_ref[3:4, :] = jnp.sum(dv * u, axis=0, keepdims=True)
        for k in (1, 2, 3):
            du = du + cw[3 - k:4 - k] * _shift_up(dv, k, rows)
            dcw_ref[3 - k:4 - k, :] = jnp.sum(dv * _shift_dn(u, k, rows), axis=0, keepdims=True)
        du_ref[...] = du.astype(BF16)

    blk = pl.BlockSpec((t, HEAD), lambda h: (0, h))
    vec = pl.BlockSpec((1, HEAD), lambda h: (0, h))
    mat = pl.BlockSpec((None, HEAD, HEAD), lambda h: (h, 0, 0))
    return pl.pallas_call(
        body, grid=(LRU_HEADS,),
        in_specs=_lru_specs(t, j, POOL_W // HEAD, (POOL_W + LRU_W) // HEAD)
        + [pl.BlockSpec((t, HEAD), lambda h: (0, POOL_W // HEAD + h))],
        out_specs=[blk, blk, pl.BlockSpec((4, HEAD), lambda h: (0, h)), vec, mat, vec, mat, vec, vec],
        out_shape=[S((t, LRU_W), BF16), S((t, LRU_W), BF16), S((4, LRU_W), F32), S((1, LRU_W), F32),
                   S((LRU_HEADS, HEAD, HEAD), F32), S((1, LRU_W), F32),
                   S((LRU_HEADS, HEAD, HEAD), F32), S((1, LRU_W), F32), S((1, LRU_W), F32)],
        compiler_params=_cp("parallel"), name="lru_bwd")(
            proj, proj, p["conv_w"], p["conv_b"], p["w_a"], p["b_a"], p["w_x"], p["b_x"], p["lam"], dycat)


def _rope(x, c, s):
    x1 = x[:, :ROPE // 2]
    x2 = x[:, ROPE // 2:]
    return jnp.concatenate([x1 * c - x2 * s, x1 * s + x2 * c], axis=-1)


def _rope_t(d, c, s):
    d1 = d[:, :ROPE // 2]
    d2 = d[:, ROPE // 2:]
    return jnp.concatenate([d1 * c + d2 * s, d2 * c - d1 * s], axis=-1)


def _rope_tables(pos2, inv_freq):
    t = pos2.shape[0]

    def body(p_ref, f_ref, c_ref, s_ref):
        ang = p_ref[...].astype(F32) * f_ref[...]
        c_ref[...] = jnp.cos(ang)
        s_ref[...] = jnp.sin(ang)

    return pl.pallas_call(body, out_shape=[S((t, ROPE // 2), F32), S((t, ROPE // 2), F32)],
                          name="rope_tables")(pos2, inv_freq)


def _down_norm(xb, wdown_g, gq3, gkv3, cos, sin, j):
    t = xb.shape[0]
    bm = _row_tile(t)

    def body(x_ref, w_ref, gq_ref, gkv_ref, c_ref, s_ref, down_ref, cq_ref, ckv_ref, kpe_ref):
        w = w_ref[...].reshape(D, ODD_IN)
        down = _dot(x_ref[...], w)
        down_ref[...] = down
        q = down[:, :Q_RANK]
        cq_ref[...] = (q * lax.rsqrt(jnp.mean(q * q, axis=-1, keepdims=True) + RMS_EPS) * gq_ref[...]).astype(BF16)
        kv = down[:, Q_RANK:Q_RANK + KV_RANK]
        ckv_ref[...] = (kv * lax.rsqrt(jnp.mean(kv * kv, axis=-1, keepdims=True) + RMS_EPS)
                        * gkv_ref[...]).astype(BF16)
        kpe_ref[...] = _rope(down[:, Q_RANK + KV_RANK:], c_ref[...], s_ref[...])

    row = lambda n: pl.BlockSpec((bm, n), lambda i: (i, 0))
    return pl.pallas_call(
        body, grid=(t // bm,),
        in_specs=[row(D), _full((N_DEV, D // N_DEV, ODD_IN)),
                  pl.BlockSpec((None, 1, Q_RANK), lambda i: (j, 0, 0)),
                  pl.BlockSpec((None, 1, KV_RANK), lambda i: (j, 0, 0)), row(ROPE // 2), row(ROPE // 2)],
        out_specs=[row(ODD_IN), row(Q_RANK), row(KV_RANK), row(ROPE)],
        out_shape=[S((t, ODD_IN), F32), S((t, Q_RANK), BF16), S((t, KV_RANK), BF16), S((t, ROPE), F32)],
        compiler_params=_cp("parallel"), name="down_norm")(xb, wdown_g, gq3, gkv3, cos, sin)


def _q_tile(t):
    return min(256, t // 2)


def _attn_probs(qn, qp, kn, kp, qs):
    s = (_dot_nt(qn, kn) + _dot_nt(qp, kp)) * ATT_SCALE
    rows = qs + lax.broadcasted_iota(jnp.int32, s.shape, 0)
    cols = lax.broadcasted_iota(jnp.int32, s.shape, 1)
    s = jnp.where(jnp.right_shift(cols, CHUNK_SHIFT) <= jnp.right_shift(rows, CHUNK_SHIFT), s, NEG)
    e = jnp.exp(s - jnp.max(s, axis=-1, keepdims=True))
    return e / jnp.sum(e, axis=-1, keepdims=True)


def _head_qkv(cq, ckv, kpe, c, s, wq_ref, wkv_ref):
    qn = _dot(cq, wq_ref[:, :NOPE]).astype(BF16)
    qp = _rope(_dot(cq, wq_ref[:, NOPE:]), c, s).astype(BF16)
    kn = _dot(ckv, wkv_ref[:, :NOPE]).astype(BF16)
    vv = _dot(ckv, wkv_ref[:, NOPE:]).astype(BF16)
    return qn, qp, kn, kpe.astype(BF16), vv


def _attn_in_specs(t):
    return [_full((t, Q_RANK)), _full((t, KV_RANK)), _full((t, ROPE)), _full((t, ROPE // 2)), _full((t, ROPE // 2)),
            pl.BlockSpec((None, Q_RANK, NOPE + ROPE), lambda h: (h, 0, 0)),
            pl.BlockSpec((None, KV_RANK, NOPE + VDIM), lambda h: (h, 0, 0)),
            pl.BlockSpec((None, VDIM, D), lambda h: (h, 0, 0))]


def _attn_fwd(cq, ckv, kpe, cos, sin, wqb_g, wkvb_g, wo_g):
    t = cq.shape[0]
    tq = _q_tile(t)

    def body(cq_ref, ckv_ref, kpe_ref, c_ref, s_ref, wq_ref, wkv_ref, wo_ref, o_ref, mix_ref):
        qn, qp, kn, kp, vv = _head_qkv(cq_ref[...], ckv_ref[...], kpe_ref[...], c_ref[...], s_ref[...],
                                       wq_ref, wkv_ref)
        for qs in range(0, t, tq):
            ke = qs + tq
            p = _attn_probs(qn[qs:ke], qp[qs:ke], kn[:ke], kp[:ke], qs)
            o_ref[qs:ke, :] = _dot(p.astype(BF16), vv[:ke]).astype(BF16)
        c = _dot(o_ref[...], wo_ref[...])

        @pl.when(pl.program_id(0) == 0)
        def _():
            mix_ref[...] = c

        @pl.when(pl.program_id(0) > 0)
        def _():
            mix_ref[...] += c

    return pl.pallas_call(
        body, grid=(MLA_HEADS,), in_specs=_attn_in_specs(t),
        out_specs=[pl.BlockSpec((None, t, VDIM), lambda h: (h, 0, 0)), _full((t, D))],
        out_shape=[S((MLA_HEADS, t, VDIM), BF16), S((t, D), F32)],
        compiler_params=_cp("arbitrary"), name="attn_fwd")(cq, ckv, kpe, cos, sin, wqb_g, wkvb_g, wo_g)


def _attn_bwd(cq, ckv, kpe, cos, sin, wqb_g, wkvb_g, wo_g, o, dzb):
    t = cq.shape[0]
    tq = _q_tile(t)

    def body(cq_ref, ckv_ref, kpe_ref, c_ref, s_ref, wq_ref, wkv_ref, wo_ref, o_ref, dz_ref,
             dwo_ref, dwq_ref, dwkv_ref, dcq_ref, dckv_ref, dkpe_ref, dkn_s, dkp_s, dv_s, dqn_s, dqp_s):
        cqv = cq_ref[...]
        ckvv = ckv_ref[...]
        c = c_ref[...]
        s = s_ref[...]
        qn, qp, kn, kp, vv = _head_qkv(cqv, ckvv, kpe_ref[...], c, s, wq_ref, wkv_ref)
        dzv = dz_ref[...]
        dwo_ref[...] = _dot_tn(o_ref[...], dzv).astype(BF16)
        do = _dot_nt(dzv, wo_ref[...]).astype(BF16)
        dkn_s[...] = jnp.zeros_like(dkn_s)
        dkp_s[...] = jnp.zeros_like(dkp_s)
        dv_s[...] = jnp.zeros_like(dv_s)
        for qs in range(0, t, tq):
            ke = qs + tq
            p = _attn_probs(qn[qs:ke], qp[qs:ke], kn[:ke], kp[:ke], qs)
            dp = _dot_nt(do[qs:ke], vv[:ke])
            ds = (p * (dp - jnp.sum(p * dp, axis=-1, keepdims=True)) * ATT_SCALE).astype(BF16)
            dqn_s[qs:ke, :] = _dot(ds, kn[:ke])
            dqp_s[qs:ke, :] = _dot(ds, kp[:ke])
            dkn_s[0:ke, :] += _dot_tn(ds, qn[qs:ke])
            dkp_s[0:ke, :] += _dot_tn(ds, qp[qs:ke])
            dv_s[0:ke, :] += _dot_tn(p.astype(BF16), do[qs:ke])
        dqn = dqn_s[...].astype(BF16)
        dqp = _rope_t(dqp_s[...], c, s).astype(BF16)
        dkn = dkn_s[...].astype(BF16)
        dvv = dv_s[...].astype(BF16)
        dwq_ref[:, :NOPE] = _dot_tn(cqv, dqn).astype(BF16)
        dwq_ref[:, NOPE:] = _dot_tn(cqv, dqp).astype(BF16)
        dwkv_ref[:, :NOPE] = _dot_tn(ckvv, dkn).astype(BF16)
        dwkv_ref[:, NOPE:] = _dot_tn(ckvv, dvv).astype(BF16)
        dcq = _dot_nt(dqn, wq_ref[:, :NOPE]) + _dot_nt(dqp, wq_ref[:, NOPE:])
        dckv = _dot_nt(dkn, wkv_ref[:, :NOPE]) + _dot_nt(dvv, wkv_ref[:, NOPE:])

        @pl.when(pl.program_id(0) == 0)
        def _():
            dcq_ref[...] = dcq
            dckv_ref[...] = dckv
            dkpe_ref[...] = dkp_s[...]

        @pl.when(pl.program_id(0) > 0)
        def _():
            dcq_ref[...] += dcq
            dckv_ref[...] += dckv
            dkpe_ref[...] += dkp_s[...]

    per_head = lambda a, b: pl.BlockSpec((None, a, b), lambda h: (h, 0, 0))
    return pl.pallas_call(
        body, grid=(MLA_HEADS,),
        in_specs=_attn_in_specs(t) + [per_head(t, VDIM), _full((t, D))],
        out_specs=[per_head(VDIM, D), per_head(Q_RANK, NOPE + ROPE), per_head(KV_RANK, NOPE + VDIM),
                   _full((t, Q_RANK)), _full((t, KV_RANK)), _full((t, ROPE))],
        out_shape=[S((MLA_HEADS, VDIM, D), BF16), S((MLA_HEADS, Q_RANK, NOPE + ROPE), BF16),
                   S((MLA_HEADS, KV_RANK, NOPE + VDIM), BF16),
                   S((t, Q_RANK), F32), S((t, KV_RANK), F32), S((t, ROPE), F32)],
        scratch_shapes=[pltpu.VMEM((t, NOPE), F32), pltpu.VMEM((t, ROPE), F32), pltpu.VMEM((t, VDIM), F32),
                        pltpu.VMEM((t, NOPE), F32), pltpu.VMEM((t, ROPE), F32)],
        compiler_params=_cp("arbitrary"), name="attn_bwd")(cq, ckv, kpe, cos, sin, wqb_g, wkvb_g, wo_g, o, dzb)


def _rms_bwd(down, dcq, dckv, dkpe, cos, sin, gq3, gkv3, j):
    t = down.shape[0]
    bm = _row_tile(t)

    def body(down_ref, dcq_ref, dckv_ref, dkpe_ref, c_ref, s_ref, gq_ref, gkv_ref, dd_ref, dgq_ref, dgkv_ref):
        @pl.when(pl.program_id(0) == 0)
        def _():
            dgq_ref[...] = jnp.zeros_like(dgq_ref)
            dgkv_ref[...] = jnp.zeros_like(dgkv_ref)

        def rms_b(x, dy, g):
            rstd = lax.rsqrt(jnp.mean(x * x, axis=-1, keepdims=True) + RMS_EPS)
            xh = x * rstd
            dyg = dy * g
            return rstd * (dyg - xh * jnp.mean(dyg * xh, axis=-1, keepdims=True)), jnp.sum(dy * xh, axis=0, keepdims=True)

        dq, dgq = rms_b(down_ref[:, :Q_RANK], dcq_ref[...], gq_ref[...])
        dkv, dgkv = rms_b(down_ref[:, Q_RANK:Q_RANK + KV_RANK], dckv_ref[...], gkv_ref[...])
        dgq_ref[...] += dgq
        dgkv_ref[...] += dgkv
        dd_ref[:, :Q_RANK] = dq.astype(BF16)
        dd_ref[:, Q_RANK:Q_RANK + KV_RANK] = dkv.astype(BF16)
        dd_ref[:, Q_RANK + KV_RANK:] = _rope_t(dkpe_ref[...], c_ref[...], s_ref[...]).astype(BF16)

    row = lambda n: pl.BlockSpec((bm, n), lambda i: (i, 0))
    return pl.pallas_call(
        body, grid=(t // bm,),
        in_specs=[row(ODD_IN), row(Q_RANK), row(KV_RANK), row(ROPE), row(ROPE // 2), row(ROPE // 2),
                  pl.BlockSpec((None, 1, Q_RANK), lambda i: (j, 0, 0)),
                  pl.BlockSpec((None, 1, KV_RANK), lambda i: (j, 0, 0))],
        out_specs=[row(ODD_IN), _full((1, Q_RANK)), _full((1, KV_RANK))],
        out_shape=[S((t, ODD_IN), BF16), S((1, Q_RANK), F32), S((1, KV_RANK), F32)],
        compiler_params=_cp("arbitrary"), name="rms_bwd")(down, dcq, dckv, dkpe, cos, sin, gq3, gkv3)


def _col_blocks(t, n, bn):
    return pl.BlockSpec((t, bn), lambda i: (0, i))


def _row_blocks(n, bm):
    return pl.BlockSpec((bm, n), lambda i: (i, 0))


def _local_step(x, pos2, tgt, small, weights_of, grads_done):
    t = x.shape[0]
    bm = _row_tile(t)
    inv_freq = (ROPE_THETA ** (-jnp.arange(0, ROPE, 2, dtype=F32) / ROPE)).reshape(1, ROPE // 2)
    cos, sin = _rope_tables(pos2, inv_freq)
    lru_p = {k: small[k] for k in ("conv_w", "conv_b", "w_a", "b_a", "w_x", "b_x", "lam")}

    saved = []
    y, yb = x, x.astype(BF16)
    for l in range(DEPTH):
        j = l // 2
        big = weights_of(l, y)
        sv = dict(xb=yb, big=big)
        if l % 2 == 0:
            proj = _mm(yb, big["win2d"], mode="nn", grid=(EVEN_IN // 512,), a_spec=_full((t, D)),
                       b_spec=_col_blocks(D, EVEN_IN, 512), out_shape=S((t, EVEN_IN), F32),
                       out_spec=_col_blocks(t, EVEN_IN, 512), name="even_proj")
            ycat = jnp.concatenate([_pool_fwd(proj, small["pool_w"], small["pool_scale"], j),
                                    _lru_fwd(proj, lru_p, j)], axis=1)
            mix = _mm(ycat, big["wout2d"], mode="nn", grid=(D // 512,), a_spec=_full((t, EVEN_MIX)),
                      b_spec=_col_blocks(EVEN_MIX, D, 512), out_shape=S((t, D), F32),
                      out_spec=_col_blocks(t, D, 512), name="even_out")
            sv.update(proj=proj, ycat=ycat)
        else:
            down, cq, ckv, kpe = _down_norm(yb, big["wdown"], small["gq"], small["gkv"], cos, sin, j)
            o, mix = _attn_fwd(cq, ckv, kpe, cos, sin, big["wqb"], big["wkvb"], big["wo"])
            sv.update(down=down, cq=cq, ckv=ckv, kpe=kpe, o=o)
        z1, y1, y1b = _resid_ln(y, mix, small["ln_mix_g"], small["ln_mix_b"], l, "resid_ln")
        ff = _mlp_fwd(y1b, big["w1"], big["w2"])
        z2, y, yb = _resid_ln(y1, ff, small["ln_ffn_g"], small["ln_ffn_b"], l, "resid_ln")
        sv.update(z1=z1, y1b=y1b, z2=z2)
        saved.append(sv)

    dy, loss_tile = _loss_grad(y, tgt)

    g = {k: [None] * n for k, n in (("ln_mix_g", 4), ("ln_mix_b", 4), ("ln_ffn_g", 4), ("ln_ffn_b", 4),
                                    ("pool_w", 2), ("pool_scale", 2), ("conv_w", 2), ("conv_b", 2),
                                    ("w_a", 2), ("b_a", 2), ("w_x", 2), ("b_x", 2), ("lam", 2),
                                    ("gq", 2), ("gkv", 2))}
    dep = None
    for l in reversed(range(DEPTH)):
        j = l // 2
        sv = saved[l]
        big = sv["big"]
        dz2, dz2b, g["ln_ffn_g"][l], g["ln_ffn_b"][l] = _ln_bwd(dy, sv["z2"], small["ln_ffn_g"], l, "ln_bwd", dep=dep)
        act, dh, dff = _mlp_bwd_dh(sv["y1b"], dz2b, big["w1"], big["w2"])
        dw1 = _mm(sv["y1b"], dh, mode="tn", grid=(N_DEV,), a_spec=_full((t, D)),
                  b_spec=_col_blocks(t, D_FF, FF_BLK), out_shape=S((N_DEV, D, FF_BLK), BF16),
                  out_spec=pl.BlockSpec((None, D, FF_BLK), lambda i: (i, 0, 0)), name="mlp_dw1")
        dw2 = _mm(act, dz2b, mode="tn", grid=(N_DEV,), a_spec=_col_blocks(t, D_FF, FF_BLK),
                  b_spec=_full((t, D)), out_shape=S((N_DEV, FF_BLK, D), BF16),
                  out_spec=pl.BlockSpec((None, FF_BLK, D), lambda i: (i, 0, 0)), name="mlp_dw2")
        dep = grads_done(l, dict(w1=dw1, w2=dw2))
        dz1, dz1b, g["ln_mix_g"][l], g["ln_mix_b"][l] = _ln_bwd(dff, sv["z1"], small["ln_mix_g"], l, "ln_bwd_res",
                                                                 r=dz2, dep=dep)
        if l % 2 == 0:
            wout = big["wout2d"]
            dycat = _mm(dz1b, wout, mode="nt", grid=(EVEN_MIX // 512,), a_spec=_full((t, D)),
                        b_spec=_row_blocks(D, 512), out_shape=S((t, EVEN_MIX), F32),
                        out_spec=_col_blocks(t, EVEN_MIX, 512), name="even_dycat")
            dwout = _mm(sv["ycat"], dz1b, mode="tn", grid=(EVEN_MIX // 512,), a_spec=_col_blocks(t, EVEN_MIX, 512),
                        b_spec=_full((t, D)), out_shape=S((EVEN_MIX, D), BF16), out_spec=_row_blocks(D, 512),
                        name="even_dwout")
            du_pool, g["pool_w"][j], g["pool_scale"][j] = _pool_bwd(sv["proj"], dycat, small["pool_w"],
                                                                   small["pool_scale"], j)
            (du_lru, du_gate, g["conv_w"][j], g["conv_b"][j], g["w_a"][j], g["b_a"][j], g["w_x"][j], g["b_x"][j],
             g["lam"][j]) = _lru_bwd(sv["proj"], dycat, lru_p, j)
            dproj = jnp.concatenate([du_pool, du_lru, du_gate], axis=1)
            dwin = _mm(sv["xb"], dproj, mode="tn", grid=(EVEN_IN // 512,), a_spec=_full((t, D)),
                       b_spec=_col_blocks(t, EVEN_IN, 512), out_shape=S((D, EVEN_IN), BF16),
                       out_spec=_col_blocks(D, EVEN_IN, 512), name="even_dwin")
            dep = grads_done(l, dict(win=dwin.reshape(D, N_DEV, EVEN_IN // N_DEV).transpose(1, 0, 2),
                                     wout=dwout.reshape(N_DEV, EVEN_MIX // N_DEV, D)))
            dy = _mm(dproj, big["win2d"], mode="nt", grid=(t // bm,), a_spec=_row_blocks(EVEN_IN, bm),
                     b_spec=_full((D, EVEN_IN)), out_shape=S((t, D), F32), out_spec=_row_blocks(D, bm),
                     add=dz1, add_spec=_row_blocks(D, bm), add_scale=ALPHA, name="even_dx")
        else:
            dwo, dwqb, dwkvb, dcq, dckv, dkpe = _attn_bwd(
                sv["cq"], sv["ckv"], sv["kpe"], cos, sin, big["wqb"], big["wkvb"], big["wo"], sv["o"], dz1b)
            ddown, g["gq"][j], g["gkv"][j] = _rms_bwd(sv["down"], dcq, dckv, dkpe, cos, sin, small["gq"],
                                                     small["gkv"], j)
            dwdown = _mm(sv["xb"], ddown, mode="tn", grid=(N_DEV,), a_spec=_col_blocks(t, D, D // N_DEV),
                         b_spec=_full((t, ODD_IN)), out_shape=S((N_DEV, D // N_DEV, ODD_IN), BF16),
                         out_spec=pl.BlockSpec((None, D // N_DEV, ODD_IN), lambda i: (i, 0, 0)),
                         name="odd_dwdown")
            dep = grads_done(l, dict(wdown=dwdown, wqb=dwqb, wkvb=dwkvb, wo=dwo))
            dy = _mm(ddown, big["wdown2d"], mode="nt", grid=(t // bm,), a_spec=_row_blocks(ODD_IN, bm),
                     b_spec=_full((D, ODD_IN)), out_shape=S((t, D), F32), out_spec=_row_blocks(D, bm),
                     add=dz1, add_spec=_row_blocks(D, bm), add_scale=ALPHA, name="odd_dx")
    return loss_tile[0, 0], dy, g


def _mesh_place():
    x, y, c = lax.axis_index("x"), lax.axis_index("y"), lax.axis_index("c")
    return x, y, c


def _peer(place, k):
    x, y, c = place
    return (1 - x if k & 4 else x, 1 - y if k & 2 else y, 1 - c if k & 1 else c)


def _index(place):
    x, y, c = place
    return 4 * x + 2 * y + c


ANY = pl.BlockSpec(memory_space=pl.ANY)


def _all_gather_big(zones):
    n = len(zones)

    def body(*refs):
        outs = refs[n:2 * n]
        send, recv = refs[2 * n:]
        x, y, c = _mesh_place()
        me, sibling = (x, y, c), (x, y, 1 - c)
        chips = [(1 - x, y), (x, 1 - y), (1 - x, 1 - y)]

        def copy(w, k, block, to):
            blk = outs[w].at[_index(block)]
            return pltpu.make_async_remote_copy(src_ref=blk, dst_ref=blk, send_sem=send.at[w, k], recv_sem=recv.at[w, k],
                                                device_id=to, device_id_type=MESH)

        first = []
        for w in range(n):
            first.append(copy(w, 0, me, sibling))
            first += [copy(w, 1 + j, me, (*chip, c)) for j, chip in enumerate(chips)]
        for cp in first:
            cp.start()
        passed = []
        for w in range(n):
            for j, chip in enumerate(chips):
                copy(w, 1 + j, (*chip, c), me).wait_recv()
                cp = copy(w, 4 + j, (*chip, c), sibling)
                cp.start()
                passed.append(cp)
        for w in range(n):
            copy(w, 0, sibling, me).wait_recv()
            for j, chip in enumerate(chips):
                copy(w, 4 + j, (*chip, 1 - c), me).wait_recv()
        for cp in first + passed:
            cp.wait_send()

    return pl.pallas_call(
        body, in_specs=[ANY] * n, out_specs=[ANY] * n, out_shape=[S(z.shape, z.dtype) for z in zones],
        input_output_aliases={i: i for i in range(n)},
        scratch_shapes=[pltpu.SemaphoreType.DMA((n, N_DEV - 1)), pltpu.SemaphoreType.DMA((n, N_DEV - 1))],
        compiler_params=pltpu.CompilerParams(has_side_effects=True), name="all_gather_big")(*zones)


def _shard_rows_tile(a):
    return max(d for d in range(16, 257, 16) if a % d == 0)


HBM = pl.BlockSpec(memory_space=pltpu.HBM)
SEM = pl.BlockSpec(memory_space=pltpu.SEMAPHORE)
DATAFLOW = pltpu.SideEffectType.DATAFLOW_SIDE_EFFECTING


def _in_hbm(a):
    return pltpu.with_memory_space_constraint(a, pltpu.HBM)


def _pair(w, k):
    return w * (N_DEV - 1) + k - 1


def _gather_blocks():
    return (lambda ref, me, peer, w: ref.at[me]), (lambda ref, me, w: ref.at[me])


def _scatter_blocks(layers):
    return (lambda ref, me, peer, w: ref.at[peer]), (lambda ref, me, w: ref.at[me, layers[w]])


def _exchange_start(srcs, lands, src_block, land_block, name, after=()):
    ns, n = len(srcs), len(lands)
    n_in = ns + n + len(after)

    def body(*refs):
        ins, land = refs[:ns], refs[ns:ns + n]
        send, recv = refs[n_in], refs[n_in + 1]
        token = refs[-1]
        place = _mesh_place()
        me = _index(place)
        for k in range(1, N_DEV):
            peer = _peer(place, k)
            for w in range(n):
                pltpu.make_async_remote_copy(src_ref=src_block(ins[w] if ns else land[w], me, _index(peer), w),
                                             dst_ref=land_block(land[w], me, w), send_sem=send.at[_pair(w, k)],
                                             recv_sem=recv.at[_pair(w, k)], device_id=peer, device_id_type=MESH).start()
        token[...] = jnp.zeros_like(token)

    sems = pltpu.SemaphoreType.DMA((n * (N_DEV - 1),))
    thru = [pltpu.HBM(a.shape, a.dtype) for a in list(srcs) + list(lands)]
    out = pl.pallas_call(
        body, name=name, in_specs=[HBM] * (ns + n) + [ANY] * len(after),
        out_shape=(sems, sems, *thru, S((8, 128), F32)),
        out_specs=(SEM, SEM, *([HBM] * (ns + n)), pl.BlockSpec(memory_space=pltpu.VMEM)),
        input_output_aliases={i: 2 + i for i in range(ns + n)},
        compiler_params=pltpu.CompilerParams(has_side_effects=DATAFLOW),
    )(*[_in_hbm(a) for a in list(srcs) + list(lands)], *after)
    return out[0], out[1], list(out[2:2 + ns]), list(out[2 + ns:2 + ns + n]), out[-1]


def _exchange_wait(send, recv, srcs, lands, src_block, land_block, after, name):
    ns, n = len(srcs), len(lands)

    def body(*refs):
        ins, land = refs[:ns], refs[ns:ns + n]
        send_ref, recv_ref = refs[ns + n], refs[ns + n + 1]
        place = _mesh_place()
        me = _index(place)
        for k in range(1, N_DEV):
            peer = _peer(place, k)
            for w in range(n):
                cp = pltpu.make_async_remote_copy(src_ref=src_block(ins[w] if ns else land[w], me, _index(peer), w),
                                                  dst_ref=land_block(land[w], me, w), send_sem=send_ref.at[_pair(w, k)],
                                                  recv_sem=recv_ref.at[_pair(w, k)], device_id=peer, device_id_type=MESH)
                cp.wait_send()
                cp.wait_recv()

    thru = [pltpu.HBM(a.shape, a.dtype) for a in list(srcs) + list(lands)]
    out = pl.pallas_call(
        body, name=name, in_specs=[HBM] * (ns + n) + [SEM, SEM, ANY],
        out_shape=tuple(thru), out_specs=tuple([HBM] * (ns + n)),
        input_output_aliases={i: i for i in range(ns + n)},
        compiler_params=pltpu.CompilerParams(has_side_effects=DATAFLOW),
    )(*srcs, *lands, send, recv, after)
    return list(out[:ns]), list(out[ns:])


def _all_reduce_small(part, name):
    r = part.shape[1]

    def body(p_ref, o_ref, rbuf, send1, recv1, send2, recv2):
        place = _mesh_place()
        me = _index(place)
        rbuf[pl.ds(me, 1)] = p_ref[pl.ds(me, 1)]
        first = [pltpu.make_async_remote_copy(src_ref=p_ref.at[_index(_peer(place, k))], dst_ref=rbuf.at[me],
                                              send_sem=send1.at[k - 1], recv_sem=recv1.at[k - 1],
                                              device_id=_peer(place, k), device_id_type=MESH)
                 for k in range(1, N_DEV)]
        for cp in first:
            cp.start()
        for cp in first:
            cp.wait()
        acc = rbuf[0]
        for d in range(1, N_DEV):
            acc = acc + rbuf[d]
        o_ref[pl.ds(me, 1)] = acc[None]
        second = [pltpu.make_async_remote_copy(src_ref=o_ref.at[me], dst_ref=o_ref.at[me], send_sem=send2.at[k - 1],
                                               recv_sem=recv2.at[k - 1], device_id=_peer(place, k),
                                               device_id_type=MESH)
                  for k in range(1, N_DEV)]
        for cp in second:
            cp.start()
        for cp in second:
            cp.wait()

    vm = pl.BlockSpec(memory_space=pltpu.VMEM)
    return pl.pallas_call(
        body, in_specs=[vm], out_specs=vm, out_shape=S(part.shape, F32),
        scratch_shapes=[pltpu.VMEM(part.shape, F32)] + [pltpu.SemaphoreType.DMA((N_DEV - 1,))] * 4,
        compiler_params=pltpu.CompilerParams(has_side_effects=True, vmem_limit_bytes=VMEM_LIMIT), name=name)(part)


def _adamw(w, g, m, v):
    m = ADAM_B1 * m + (1.0 - ADAM_B1) * g
    v = ADAM_B2 * v + (1.0 - ADAM_B2) * (g * g)
    m_hat = m / (1.0 - ADAM_B1 ** ADAM_STEP)
    v_hat = v / (1.0 - ADAM_B2 ** ADAM_STEP)
    return -ADAM_LR * (m_hat / (jnp.sqrt(v_hat) + ADAM_EPS) + ADAM_WD * w), m, v


def _adam_big(parts, own, me, w, m, v, name):
    nl, a, b = w.shape
    ta = _shard_rows_tile(a)

    def body(me_ref, p_ref, *refs):
        own_refs, (w_ref, m_ref, v_ref, g_ref, d_ref, mo_ref, vo_ref) = refs[:nl], refs[nl:]
        layer = pl.program_id(0)
        mine = own_refs[0][...]
        for k in range(1, nl):
            mine = jnp.where(layer == k, own_refs[k][...], mine)
        g = None
        for s in range(N_DEV):
            term = jnp.where(me_ref[0] == s, mine, p_ref[s]).astype(F32)
            g = term if g is None else g + term
        g_ref[...] = g
        d_ref[...], mo_ref[...], vo_ref[...] = _adamw(w_ref[...], g, m_ref[...], v_ref[...])

    blk = pl.BlockSpec((None, ta, b), lambda l, i, me_ref: (l, i, 0))

    def own_spec(k):
        return pl.BlockSpec((None, ta, b), lambda l, i, me_ref: (me_ref[0], jnp.where(l == k, i, 0), 0))

    grid_spec = pltpu.PrefetchScalarGridSpec(
        num_scalar_prefetch=1, grid=(nl, a // ta),
        in_specs=[pl.BlockSpec((N_DEV, None, ta, b), lambda l, i, me_ref: (0, l, i, 0))]
        + [own_spec(k) for k in range(nl)] + [blk, blk, blk],
        out_specs=[blk] * 4)
    return pl.pallas_call(body, grid_spec=grid_spec, out_shape=[S(w.shape, F32)] * 4,
                          compiler_params=_cp("arbitrary", "arbitrary"), name=name)(me, parts, *own, w, m, v)


def _adam_small(g, w, m, v):
    rows = g.shape[0]
    tr = 256

    def body(g_ref, w_ref, m_ref, v_ref, d_ref, mo_ref, vo_ref):
        d_ref[...], mo_ref[...], vo_ref[...] = _adamw(w_ref[...], g_ref[...], m_ref[...], v_ref[...])

    blk = pl.BlockSpec((tr, 128), lambda i: (i, 0))
    return pl.pallas_call(body, grid=(rows // tr,), in_specs=[blk] * 4, out_specs=[blk] * 3,
                          out_shape=[S(g.shape, F32)] * 3, compiler_params=_cp("parallel"), name="adam_small")(g, w, m, v)


BIG = ("even_w_in", "even_w_out", "mla_w_down", "mla_w_qb", "mla_w_kvb", "mla_w_o", "mlp_w1", "mlp_w2")
BIG_KEY = dict(even_w_in="win", even_w_out="wout", mla_w_down="wdown", mla_w_qb="wqb", mla_w_kvb="wkvb",
               mla_w_o="wo", mlp_w1="w1", mlp_w2="w2")
SMALL = (("ln_mix_g", "ln_mix_g", None), ("ln_mix_b", "ln_mix_b", None), ("ln_ffn_g", "ln_ffn_g", None),
         ("ln_ffn_b", "ln_ffn_b", None), ("pool_w", "pool_w", None), ("pool_scale", "pool_scale", None),
         ("lru_conv_w", "conv_w", 2), ("lru_conv_b", "conv_b", None), ("lru_w_a", "w_a", None),
         ("lru_b_a", "b_a", None), ("lru_w_x", "w_x", None), ("lru_b_x", "b_x", None), ("lru_lambda", "lam", None),
         ("mla_q_norm_g", "gq", 1), ("mla_kv_norm_g", "gkv", 1))
WEIGHTS = ("ln_mix_g", "ln_mix_b", "ln_ffn_g", "ln_ffn_b", "even_w_in", "pool_w", "pool_scale", "lru_conv_w",
           "lru_conv_b", "lru_w_a", "lru_b_a", "lru_w_x", "lru_b_x", "lru_lambda", "even_w_out", "mla_w_down",
           "mla_q_norm_g", "mla_kv_norm_g", "mla_w_qb", "mla_w_kvb", "mla_w_o", "mlp_w1", "mlp_w2")
ALL_AXES = ("x", "y", "c")


def _layer_weights(l):
    j = l // 2
    if l % 2 == 0:
        mixer = [("win", "even_w_in", j), ("wout", "even_w_out", j)]
    else:
        mixer = [("wdown", "mla_w_down", j), ("wqb", "mla_w_qb", j), ("wkvb", "mla_w_kvb", j), ("wo", "mla_w_o", j)]
    return mixer + [("w1", "mlp_w1", l), ("w2", "mlp_w2", l)]


def _pack(arrays, multiple):
    flat = jnp.concatenate([a.reshape(-1) for a in arrays])
    pad = (-flat.shape[0]) % multiple
    return jnp.pad(flat, (0, pad))


def _unpack(flat, shapes):
    out, at = [], 0
    for shp in shapes:
        n = 1
        for s in shp:
            n *= s
        out.append(flat[at:at + n].reshape(shp))
        at += n
    return out


def _global_shape(local_shape, axis):
    if axis is None:
        return tuple(local_shape)
    return tuple(s * N_DEV if i == axis else s for i, s in enumerate(local_shape))


def _step(x, positions, tgt, w, m, v):
    t = x.shape[1]
    me = _index(_mesh_place())

    sharded = [(name, axis) for name, _, axis in SMALL if axis is not None]
    zeros_with_mine = [lax.dynamic_update_slice_in_dim(jnp.zeros(_global_shape(w[name].shape, axis), F32), w[name],
                                                       me * w[name].shape[axis], axis) for name, axis in sharded]
    chunk = N_DEV * 8 * 128
    gathered = _all_reduce_small(_pack(zeros_with_mine, chunk).reshape(N_DEV, -1, 128), "gather_small")
    full = dict(zip([name for name, _ in sharded],
                    _unpack(gathered.reshape(-1), [_global_shape(w[name].shape, axis) for name, axis in sharded])))

    def zone_of(shard):
        return lax.dynamic_update_slice_in_dim(lax.empty((N_DEV,) + shard.shape, BF16), shard.astype(BF16)[None], me, 0)

    zones = [[zone_of(w[name][i]) for _, name, i in _layer_weights(l)] for l in range(DEPTH)]
    first = _all_gather_big(zones[0])
    gather_src, gather_land = _gather_blocks()
    flights, after = {}, (first[0], gathered)
    for l in range(1, DEPTH):
        send, recv, _, lands, token = _exchange_start([], zones[l], gather_src, gather_land, "gather_start_%d" % l,
                                                      after=after)
        flights[l] = (send, recv, [], lands)
        after = (token,)

    def weights_of(l, after):
        arrays = first if l == 0 else _exchange_wait(*flights[l], gather_src, gather_land, after,
                                                     "gather_wait_%d" % l)[1]
        big = {key: a for (key, _, _), a in zip(_layer_weights(l), arrays)}
        if l % 2 == 0:
            big["win2d"] = big["win"].transpose(1, 0, 2).reshape(D, EVEN_IN)
            big["wout2d"] = big["wout"].reshape(EVEN_MIX, D)
        else:
            big["wdown2d"] = big["wdown"].reshape(D, ODD_IN)
        return big

    zone = {name: lax.empty((N_DEV,) + w[name].shape, BF16) for name in BIG}
    name_of = {key: name for name, key in BIG_KEY.items()}
    sent = []

    def grads_done(l, grads):
        keys = list(grads)
        index = {key: i for key, _, i in _layer_weights(l)}
        layers = [index[key] for key in keys]
        src_b, land_b = _scatter_blocks(layers)
        send, recv, srcs, lands, tok = _exchange_start([grads[k] for k in keys], [zone[name_of[k]] for k in keys],
                                                       src_b, land_b, "scatter_start_%d_%s" % (l, keys[0]))
        for k, land in zip(keys, lands):
            zone[name_of[k]] = land
        sent.append((send, recv, srcs, keys, layers))
        return tok

    row3 = lambda a: a.reshape(a.shape[0], 1, a.shape[1])
    small = dict(ln_mix_g=row3(w["ln_mix_g"]), ln_mix_b=row3(w["ln_mix_b"]), ln_ffn_g=row3(w["ln_ffn_g"]),
                 ln_ffn_b=row3(w["ln_ffn_b"]), pool_w=w["pool_w"], pool_scale=row3(w["pool_scale"]),
                 conv_w=full["lru_conv_w"], conv_b=row3(w["lru_conv_b"]), w_a=w["lru_w_a"], b_a=row3(w["lru_b_a"]),
                 w_x=w["lru_w_x"], b_x=row3(w["lru_b_x"]), lam=row3(w["lru_lambda"]),
                 gq=row3(full["mla_q_norm_g"]), gkv=row3(full["mla_kv_norm_g"]))

    loss_part, grad_x, g = _local_step(x[0] + token[0, 0], positions.reshape(t, 1), tgt[0], small, weights_of,
                                       grads_done)
    loss = lax.psum(loss_part, ALL_AXES)

    after = grad_x
    own = {name: [None] * w[name].shape[0] for name in BIG}
    for n_flight, (send, recv, srcs, keys, layers) in enumerate(sent):
        src_b, land_b = _scatter_blocks(layers)
        srcs, lands = _exchange_wait(send, recv, srcs, [zone[name_of[k]] for k in keys], src_b, land_b, after,
                                     "scatter_wait_%d" % n_flight)
        for k, land, src, layer in zip(keys, lands, srcs, layers):
            zone[name_of[k]] = land
            own[name_of[k]][layer] = src
        after = lands[0]
    me_arr = me.astype(jnp.int32).reshape(1)
    out = {}
    for name in BIG:
        out[name] = _adam_big(zone[name], own[name], me_arr, w[name], m[name], v[name], "adam_" + name)

    local_g = [jnp.stack(g[key]).reshape(_global_shape(w[name].shape, axis)) for name, key, axis in SMALL]
    reduced = _all_reduce_small(_pack(local_g, chunk).reshape(N_DEV, -1, 128), "all_reduce_small")
    reduced = _unpack(reduced.reshape(-1), [a.shape for a in local_g])
    mine = [a if axis is None else lax.dynamic_slice_in_dim(a, me * w[name].shape[axis], w[name].shape[axis], axis)
            for a, (name, _, axis) in zip(reduced, SMALL)]
    tile = 256 * 128
    packed = [_pack(arrs, tile).reshape(-1, 128)
              for arrs in (mine, [w[n] for n, _, _ in SMALL], [m[n] for n, _, _ in SMALL], [v[n] for n, _, _ in SMALL])]
    shapes = [w[n].shape for n, _, _ in SMALL]
    d_s, m_s, v_s = (_unpack(a.reshape(-1), shapes) for a in _adam_small(*packed))
    for i, (name, _, _) in enumerate(SMALL):
        out[name] = (mine[i], d_s[i], m_s[i], v_s[i])

    return (loss, grad_x[None]) + tuple(out[name][i] for i in range(4) for name in WEIGHTS)


def kernel(x, positions, ln_mix_g, ln_mix_b, ln_ffn_g, ln_ffn_b, even_w_in, pool_w, pool_scale, lru_conv_w, lru_conv_b, lru_w_a, lru_b_a, lru_w_x, lru_b_x, lru_lambda, even_w_out, mla_w_down, mla_q_norm_g, mla_kv_norm_g, mla_w_qb, mla_w_kvb, mla_w_o, mlp_w1, mlp_w2, loss_target, m_ln_mix_g, m_ln_mix_b, m_ln_ffn_g, m_ln_ffn_b, m_even_w_in, m_pool_w, m_pool_scale, m_lru_conv_w, m_lru_conv_b, m_lru_w_a, m_lru_b_a, m_lru_w_x, m_lru_b_x, m_lru_lambda, m_even_w_out, m_mla_w_down, m_mla_q_norm_g, m_mla_kv_norm_g, m_mla_w_qb, m_mla_w_kvb, m_mla_w_o, m_mlp_w1, m_mlp_w2, v_ln_mix_g, v_ln_mix_b, v_ln_ffn_g, v_ln_ffn_b, v_even_w_in, v_pool_w, v_pool_scale, v_lru_conv_w, v_lru_conv_b, v_lru_w_a, v_lru_b_a, v_lru_w_x, v_lru_b_x, v_lru_lambda, v_even_w_out, v_mla_w_down, v_mla_q_norm_g, v_mla_kv_norm_g, v_mla_w_qb, v_mla_w_kvb, v_mla_w_o, v_mlp_w1, v_mlp_w2):
    w = dict(zip(WEIGHTS, (ln_mix_g, ln_mix_b, ln_ffn_g, ln_ffn_b, even_w_in, pool_w, pool_scale, lru_conv_w,
                           lru_conv_b, lru_w_a, lru_b_a, lru_w_x, lru_b_x, lru_lambda, even_w_out, mla_w_down,
                           mla_q_norm_g, mla_kv_norm_g, mla_w_qb, mla_w_kvb, mla_w_o, mlp_w1, mlp_w2)))
    m = dict(zip(WEIGHTS, (m_ln_mix_g, m_ln_mix_b, m_ln_ffn_g, m_ln_ffn_b, m_even_w_in, m_pool_w, m_pool_scale,
                           m_lru_conv_w, m_lru_conv_b, m_lru_w_a, m_lru_b_a, m_lru_w_x, m_lru_b_x, m_lru_lambda,
                           m_even_w_out, m_mla_w_down, m_mla_q_norm_g, m_mla_kv_norm_g, m_mla_w_qb, m_mla_w_kvb,
                           m_mla_w_o, m_mlp_w1, m_mlp_w2)))
    v = dict(zip(WEIGHTS, (v_ln_mix_g, v_ln_mix_b, v_ln_ffn_g, v_ln_ffn_b, v_even_w_in, v_pool_w, v_pool_scale,
                           v_lru_conv_w, v_lru_conv_b, v_lru_w_a, v_lru_b_a, v_lru_w_x, v_lru_b_x, v_lru_lambda,
                           v_even_w_out, v_mla_w_down, v_mla_q_norm_g, v_mla_kv_norm_g, v_mla_w_qb, v_mla_w_kvb,
                           v_mla_w_o, v_mlp_w1, v_mlp_w2)))
    return _step(x, positions, loss_target, w, m, v)
```

```python
import functools

import jax
import jax.numpy as jnp
from jax import lax
from jax.experimental import pallas as pl
from jax.experimental.pallas import tpu as pltpu

F32 = jnp.float32
BF16 = jnp.bfloat16
S = jax.ShapeDtypeStruct

D = 1024
DEPTH = 4
N_DEV = 8
CHUNK_SHIFT = 6
POOL_WINDOWS = (2, 4, 8, 16)
POOL_W = 512
LRU_W = 1024
LRU_HEADS = 8
HEAD = 128
LRU_C = 8.0
EVEN_IN = 2560
EVEN_MIX = 1536
MLA_HEADS = 8
NOPE = 128
ROPE = 64
VDIM = 128
Q_RANK = 384
KV_RANK = 256
ODD_IN = 704
D_FF = 4096
FF_BLK = D_FF // N_DEV
ROPE_THETA = 10000.0
ALPHA = (2 * DEPTH) ** 0.25
LN_EPS = 1e-5
RMS_EPS = 1e-6
ATT_SCALE = (NOPE + ROPE) ** -0.5
NEG = float(jnp.finfo(jnp.float32).min)
ADAM_LR = 0.001
ADAM_B1 = 0.9
ADAM_B2 = 0.999
ADAM_EPS = 1e-08
ADAM_WD = 0.01
ADAM_STEP = 10
V7X_VMEM_BYTES = 64 * 1024 * 1024
VMEM_LIMIT = V7X_VMEM_BYTES - 8 * 1024 * 1024
MESH = pl.DeviceIdType.MESH


def _cp(*sem):
    return pltpu.CompilerParams(dimension_semantics=sem if sem else None, vmem_limit_bytes=VMEM_LIMIT)


def _dot(a, b):
    return jnp.dot(a, b, preferred_element_type=F32)


def _dot_nt(a, b):
    return lax.dot_general(a, b, (((1,), (1,)), ((), ())), preferred_element_type=F32)


def _dot_tn(a, b):
    return lax.dot_general(a, b, (((0,), (0,)), ((), ())), preferred_element_type=F32)


def _full(shape):
    return pl.BlockSpec(shape, lambda *_: (0,) * len(shape))


def _mm(a, b, *, mode, grid, a_spec, b_spec, out_shape, out_spec, name, add=None, add_spec=None, add_scale=1.0):
    dot = {"nn": _dot, "nt": _dot_nt, "tn": _dot_tn}[mode]

    def body(*refs):
        if add is None:
            a_ref, b_ref, o_ref = refs
            acc = dot(a_ref[...].astype(BF16), b_ref[...].astype(BF16))
        else:
            a_ref, b_ref, add_ref, o_ref = refs
            acc = dot(a_ref[...].astype(BF16), b_ref[...].astype(BF16)) + add_scale * add_ref[...]
        o_ref[...] = acc.astype(o_ref.dtype)

    ops = (a, b) if add is None else (a, b, add)
    specs = [a_spec, b_spec] if add is None else [a_spec, b_spec, add_spec]
    return pl.pallas_call(body, grid=grid, in_specs=specs, out_specs=out_spec, out_shape=out_shape,
                          compiler_params=_cp(*(("parallel",) * len(grid))), name=name)(*ops)


def _ln_stats(z):
    mu = jnp.mean(z, axis=-1, keepdims=True)
    zc = z - mu
    var = jnp.mean(zc * zc, axis=-1, keepdims=True)
    rstd = lax.rsqrt(var + LN_EPS)
    return zc * rstd, rstd


def _row_tile(t):
    return min(512, t)


def _resid_ln(x, mix, g3, b3, l, name):
    t = x.shape[0]
    bm = _row_tile(t)

    def body(x_ref, m_ref, g_ref, b_ref, z_ref, y_ref, yb_ref):
        z = ALPHA * x_ref[...] + m_ref[...]
        xh, _ = _ln_stats(z)
        y = xh * g_ref[...] + b_ref[...]
        z_ref[...] = z
        y_ref[...] = y
        yb_ref[...] = y.astype(BF16)

    row = pl.BlockSpec((bm, D), lambda i: (i, 0))
    vec = pl.BlockSpec((None, 1, D), lambda i: (l, 0, 0))
    return pl.pallas_call(body, grid=(t // bm,), in_specs=[row, row, vec, vec], out_specs=[row, row, row],
                          out_shape=[S((t, D), F32), S((t, D), F32), S((t, D), BF16)],
                          compiler_params=_cp("parallel"), name=name)(x, mix, g3, b3)


def _ln_bwd(d, z, g3, l, name, r=None, dep=None):
    t = z.shape[0]
    bm = _row_tile(t)

    def body(*refs):
        refs = list(refs)
        d_ref = refs.pop(0)
        dy = d_ref[...]
        if r is not None:
            dy = dy + ALPHA * refs.pop(0)[...]
        z_ref, g_ref = refs.pop(0), refs.pop(0)
        if dep is not None:
            refs.pop(0)
        dz_ref, dzb_ref, dg_ref, db_ref = refs
        xh, rstd = _ln_stats(z_ref[...])
        dyg = dy * g_ref[...]
        m1 = jnp.mean(dyg, axis=-1, keepdims=True)
        m2 = jnp.mean(dyg * xh, axis=-1, keepdims=True)
        dz = rstd * (dyg - m1 - xh * m2)
        dz_ref[...] = dz
        dzb_ref[...] = dz.astype(BF16)

        @pl.when(pl.program_id(0) == 0)
        def _():
            dg_ref[...] = jnp.zeros_like(dg_ref)
            db_ref[...] = jnp.zeros_like(db_ref)

        dg_ref[...] += jnp.sum(dy * xh, axis=0, keepdims=True)
        db_ref[...] += jnp.sum(dy, axis=0, keepdims=True)

    row = pl.BlockSpec((bm, D), lambda i: (i, 0))
    vec = pl.BlockSpec((None, 1, D), lambda i: (l, 0, 0))
    acc = pl.BlockSpec((1, D), lambda i: (0, 0))
    ops = [d, z, g3] if r is None else [d, r, z, g3]
    specs = [row, row, vec] if r is None else [row, row, row, vec]
    if dep is not None:
        ops.append(dep)
        specs.append(_full(dep.shape))
    return pl.pallas_call(body, grid=(t // bm,), in_specs=specs, out_specs=[row, row, acc, acc],
                          out_shape=[S((t, D), F32), S((t, D), BF16), S((1, D), F32), S((1, D), F32)],
                          compiler_params=_cp("arbitrary"), name=name)(*ops)


def _loss_grad(y, tgt):
    t = y.shape[0]
    bm = _row_tile(t)

    def body(y_ref, t_ref, dy_ref, loss_ref, acc_ref):
        i = pl.program_id(0)
        e = y_ref[...] - t_ref[...]
        dy_ref[...] = e * (1.0 / D)

        @pl.when(i == 0)
        def _():
            acc_ref[...] = jnp.zeros_like(acc_ref)

        acc_ref[...] += jnp.sum(e * e, axis=0, keepdims=True)

        @pl.when(i == pl.num_programs(0) - 1)
        def _():
            loss_ref[...] = jnp.full(loss_ref.shape, (0.5 / D) * jnp.sum(acc_ref[...]), F32)

    row = pl.BlockSpec((bm, D), lambda i: (i, 0))
    return pl.pallas_call(body, grid=(t // bm,), in_specs=[row, row],
                          out_specs=[row, pl.BlockSpec((1, 128), lambda i: (0, 0))],
                          out_shape=[S((t, D), F32), S((1, 128), F32)],
                          scratch_shapes=[pltpu.VMEM((1, D), F32)],
                          compiler_params=_cp("arbitrary"), name="loss_grad")(y, tgt)


def _mlp_row_tile(t):
    return min(1024, t)


def _mlp_fwd(yb, w1g, w2g):
    t = yb.shape[0]
    bm = _mlp_row_tile(t)

    def body(y_ref, w1_ref, w2_ref, o_ref):
        j = pl.program_id(1)
        h = jnp.maximum(_dot(y_ref[...], w1_ref[...]), 0.0)
        c = _dot((h * h).astype(BF16), w2_ref[...])

        @pl.when(j == 0)
        def _():
            o_ref[...] = c

        @pl.when(j > 0)
        def _():
            o_ref[...] += c

    return pl.pallas_call(
        body, grid=(t // bm, N_DEV),
        in_specs=[pl.BlockSpec((bm, D), lambda i, j: (i, 0)),
                  pl.BlockSpec((None, D, FF_BLK), lambda i, j: (j, 0, 0)),
                  pl.BlockSpec((None, FF_BLK, D), lambda i, j: (j, 0, 0))],
        out_specs=pl.BlockSpec((bm, D), lambda i, j: (i, 0)),
        out_shape=S((t, D), F32), compiler_params=_cp("parallel", "arbitrary"), name="mlp_fwd")(yb, w1g, w2g)


def _mlp_bwd_dh(yb, dzb, w1g, w2g):
    t = yb.shape[0]
    bm = _mlp_row_tile(t)

    def body(y_ref, dz_ref, w1_ref, w2_ref, a_ref, dh_ref, acc_ref):
        j = pl.program_id(1)
        r = jnp.maximum(_dot(y_ref[...], w1_ref[...]), 0.0)
        a_ref[...] = (r * r).astype(BF16)
        da = _dot_nt(dz_ref[...], w2_ref[...])
        dh = (da * (2.0 * r)).astype(BF16)
        dh_ref[...] = dh
        c = _dot_nt(dh, w1_ref[...])

        @pl.when(j == 0)
        def _():
            acc_ref[...] = c

        @pl.when(j > 0)
        def _():
            acc_ref[...] += c

    row = pl.BlockSpec((bm, D), lambda i, j: (i, 0))
    hid = pl.BlockSpec((bm, FF_BLK), lambda i, j: (i, j))
    return pl.pallas_call(
        body, grid=(t // bm, N_DEV),
        in_specs=[row, row,
                  pl.BlockSpec((None, D, FF_BLK), lambda i, j: (j, 0, 0)),
                  pl.BlockSpec((None, FF_BLK, D), lambda i, j: (j, 0, 0))],
        out_specs=[hid, hid, row],
        out_shape=[S((t, D_FF), BF16), S((t, D_FF), BF16), S((t, D), F32)],
        compiler_params=_cp("parallel", "arbitrary"), name="mlp_bwd_dh")(yb, dzb, w1g, w2g)


def _shift_dn(x, k, rows, fill=0.0):
    return jnp.where(rows >= k, pltpu.roll(x, k, 0), fill)


def _shift_up(x, k, rows, fill=0.0):
    t = x.shape[0]
    return jnp.where(rows < t - k, pltpu.roll(x, t - k, 0), fill)


def _scan_dn(a, b, rows):
    k = 1
    t = a.shape[0]
    while k < t:
        b = a * _shift_dn(b, k, rows) + b
        if 2 * k < t:
            a = a * _shift_dn(a, k, rows, 1.0)
        k *= 2
    return b


def _scan_up(a, b, rows):
    k = 1
    t = a.shape[0]
    while k < t:
        b = a * _shift_up(b, k, rows) + b
        if 2 * k < t:
            a = a * _shift_up(a, k, rows, 1.0)
        k *= 2
    return b


def _window_sum_dn(x, w, rows):
    k = 1
    while k < w:
        x = x + _shift_dn(x, k, rows)
        k *= 2
    return x


def _window_sum_up(x, w, rows):
    k = 1
    while k < w:
        x = x + _shift_up(x, k, rows)
        k *= 2
    return x


def _pool_diff(u, w, rows):
    inv_count = 1.0 / jnp.minimum(rows + 1, w).astype(F32)
    return _window_sum_dn(u, w, rows) * inv_count - u, inv_count


def _pool_fwd(proj, pool_w, pool_scale3, j):
    t = proj.shape[0]

    def body(u_ref, w_ref, s_ref, y_ref):
        rows = lax.broadcasted_iota(jnp.int32, (t, HEAD), 0)
        for g, w in enumerate(POOL_WINDOWS):
            cols = slice(g * HEAD, (g + 1) * HEAD)
            d, _ = _pool_diff(u_ref[:, cols], w, rows)
            y = _dot(d.astype(BF16), w_ref[g].astype(BF16)) * s_ref[:, cols]
            y_ref[:, cols] = y.astype(BF16)

    return pl.pallas_call(
        body, grid=(1,),
        in_specs=[pl.BlockSpec((t, POOL_W), lambda i: (0, 0)),
                  pl.BlockSpec((None, 4, HEAD, HEAD), lambda i: (j, 0, 0, 0)),
                  pl.BlockSpec((None, 1, POOL_W), lambda i: (j, 0, 0))],
        out_specs=pl.BlockSpec((t, POOL_W), lambda i: (0, 0)),
        out_shape=S((t, POOL_W), BF16), compiler_params=_cp("arbitrary"), name="pool_fwd")(proj, pool_w, pool_scale3)


def _pool_bwd(proj, dycat, pool_w, pool_scale3, j):
    t = proj.shape[0]

    def body(u_ref, dy_ref, w_ref, s_ref, du_ref, dw_ref, ds_ref):
        rows = lax.broadcasted_iota(jnp.int32, (t, HEAD), 0)
        for g, w in enumerate(POOL_WINDOWS):
            cols = slice(g * HEAD, (g + 1) * HEAD)
            d, inv_count = _pool_diff(u_ref[:, cols], w, rows)
            db = d.astype(BF16)
            wg = w_ref[g].astype(BF16)
            dy = dy_ref[:, cols]
            ds_ref[:, cols] = jnp.sum(dy * _dot(db, wg), axis=0, keepdims=True)
            dzz = (dy * s_ref[:, cols]).astype(BF16)
            dw_ref[g] = _dot_tn(db, dzz)
            dd = _dot_nt(dzz, wg)
            du_ref[:, cols] = (_window_sum_up(dd * inv_count, w, rows) - dd).astype(BF16)

    return pl.pallas_call(
        body, grid=(1,),
        in_specs=[pl.BlockSpec((t, POOL_W), lambda i: (0, 0)),
                  pl.BlockSpec((t, POOL_W), lambda i: (0, 0)),
                  pl.BlockSpec((None, 4, HEAD, HEAD), lambda i: (j, 0, 0, 0)),
                  pl.BlockSpec((None, 1, POOL_W), lambda i: (j, 0, 0))],
        out_specs=[pl.BlockSpec((t, POOL_W), lambda i: (0, 0)), _full((4, HEAD, HEAD)), _full((1, POOL_W))],
        out_shape=[S((t, POOL_W), BF16), S((4, HEAD, HEAD), F32), S((1, POOL_W), F32)],
        compiler_params=_cp("arbitrary"), name="pool_bwd")(proj, dycat, pool_w, pool_scale3)


GELU_C = 0.7978845608028654
GELU_K = 0.044715


def _gelu(x):
    th = jnp.tanh(GELU_C * (x + GELU_K * x * x * x))
    return 0.5 * x * (1.0 + th), th


def _lru_forward(u, gate, cw, cb, wa, ba, wx, bx, lam, rows):
    v = cw[3:4] * u + cw[2:3] * _shift_dn(u, 1, rows) + cw[1:2] * _shift_dn(u, 2, rows) \
        + cw[0:1] * _shift_dn(u, 3, rows) + cb
    vb = v.astype(BF16)
    r = jax.nn.sigmoid(_dot(vb, wa) + ba)
    i = jax.nn.sigmoid(_dot(vb, wx) + bx)
    sp = jnp.maximum(-lam, 0.0) + jnp.log1p(jnp.exp(-jnp.abs(lam)))
    log_a = (-LRU_C) * r * sp
    a = jnp.exp(log_a)
    one_m_a2 = -jnp.tanh(log_a) * (a * a + 1.0)
    mult = jnp.sqrt(one_m_a2)
    h = _scan_dn(a, mult * (i * v), rows)
    gl, th = _gelu(gate)
    return dict(v=v, vb=vb, r=r, i=i, sp=sp, a=a, mult=mult, h=h, gl=gl, th=th)


def _lru_specs(t, j, col0_u, col0_g):
    blk = lambda c0: pl.BlockSpec((t, HEAD), lambda h: (0, c0 + h))
    vec = pl.BlockSpec((None, 1, HEAD), lambda h: (j, 0, h))
    return [blk(col0_u), blk(col0_g),
            pl.BlockSpec((None, 4, HEAD), lambda h: (j, 0, h)), vec,
            pl.BlockSpec((None, None, HEAD, HEAD), lambda h: (j, h, 0, 0)), vec,
            pl.BlockSpec((None, None, HEAD, HEAD), lambda h: (j, h, 0, 0)), vec, vec]


def _lru_fwd(proj, p, j):
    t = proj.shape[0]

    def body(u_ref, g_ref, cw_ref, cb_ref, wa_ref, ba_ref, wx_ref, bx_ref, lam_ref, y_ref):
        rows = lax.broadcasted_iota(jnp.int32, (t, HEAD), 0)
        f = _lru_forward(u_ref[...], g_ref[...], cw_ref[...], cb_ref[...], wa_ref[...].astype(BF16), ba_ref[...],
                         wx_ref[...].astype(BF16), bx_ref[...], lam_ref[...], rows)
        y_ref[...] = (f["h"] * f["gl"]).astype(BF16)

    return pl.pallas_call(
        body, grid=(LRU_HEADS,), in_specs=_lru_specs(t, j, POOL_W // HEAD, (POOL_W + LRU_W) // HEAD),
        out_specs=pl.BlockSpec((t, HEAD), lambda h: (0, h)), out_shape=S((t, LRU_W), BF16),
        compiler_params=_cp("parallel"), name="lru_fwd")(
            proj, proj, p["conv_w"], p["conv_b"], p["w_a"], p["b_a"], p["w_x"], p["b_x"], p["lam"])


def _lru_bwd(proj, dycat, p, j):
    t = proj.shape[0]

    def body(u_ref, g_ref, cw_ref, cb_ref, wa_ref, ba_ref, wx_ref, bx_ref, lam_ref, dy_ref,
             du_ref, dgate_ref, dcw_ref, dcb_ref, dwa_ref, dba_ref, dwx_ref, dbx_ref, dlam_ref):
        rows = lax.broadcasted_iota(jnp.int32, (t, HEAD), 0)
        u = u_ref[...]
        gate = g_ref[...]
        cw = cw_ref[...]
        wa = wa_ref[...].astype(BF16)
        wx = wx_ref[...].astype(BF16)
        lam = lam_ref[...]
        f = _lru_forward(u, gate, cw, cb_ref[...], wa, ba_ref[...], wx, bx_ref[...], lam, rows)
        v, r, i, a, mult, h, th = f["v"], f["r"], f["i"], f["a"], f["mult"], f["h"], f["th"]
        dy = dy_ref[...]
        dgl = 0.5 * (1.0 + th) + 0.5 * gate * (1.0 - th * th) * GELU_C * (1.0 + 3.0 * GELU_K * gate * gate)
        dgate_ref[...] = (dy * h * dgl).astype(BF16)
        g = _scan_up(_shift_up(a, 1, rows), dy * f["gl"], rows)
        da = g * _shift_dn(h, 1, rows)
        iv = i * v
        dmult = g * iv
        di = g * mult * v
        dv = g * mult * i
        dlog_a = da * a - dmult * (a * a) / mult
        dr = dlog_a * (-LRU_C) * f["sp"]
        dsp = jnp.sum(dlog_a * (-LRU_C) * r, axis=0, keepdims=True)
        dlam_ref[...] = -dsp * jax.nn.sigmoid(-lam)
        dpa = dr * r * (1.0 - r)
        dpx = di * i * (1.0 - i)
        dpab = dpa.astype(BF16)
        dpxb = dpx.astype(BF16)
        dwa_ref[...] = _dot_tn(f["vb"], dpab)
        dwx_ref[...] = _dot_tn(f["vb"], dpxb)
        dba_ref[...] = jnp.sum(dpa, axis=0, keepdims=True)
        dbx_ref[...] = jnp.sum(dpx, axis=0, keepdims=True)
        dv = dv + _dot_nt(dpab, wa) + _dot_nt(dpxb, wx)
        dcb_ref[...] = jnp.sum(dv, axis=0, keepdims=True)
        du = cw[3:4] * dv
        dcw_ref[3:4, :] = jnp.sum(dv * u, axis=0, keepdims=True)
        for k in (1, 2, 3):
            du = du + cw[3 - k:4 - k] * _shift_up(dv, k, rows)
            dcw_ref[3 - k:4 - k, :] = jnp.sum(dv * _shift_dn(u, k, rows), axis=0, keepdims=True)
        du_ref[...] = du.astype(BF16)

    blk = pl.BlockSpec((t, HEAD), lambda h: (0, h))
    vec = pl.BlockSpec((1, HEAD), lambda h: (0, h))
    mat = pl.BlockSpec((None, HEAD, HEAD), lambda h: (h, 0, 0))
    return pl.pallas_call(
        body, grid=(LRU_HEADS,),
        in_specs=_lru_specs(t, j, POOL_W // HEAD, (POOL_W + LRU_W) // HEAD)
        + [pl.BlockSpec((t, HEAD), lambda h: (0, POOL_W // HEAD + h))],
        out_specs=[blk, blk, pl.BlockSpec((4, HEAD), lambda h: (0, h)), vec, mat, vec, mat, vec, vec],
        out_shape=[S((t, LRU_W), BF16), S((t, LRU_W), BF16), S((4, LRU_W), F32), S((1, LRU_W), F32),
                   S((LRU_HEADS, HEAD, HEAD), F32), S((1, LRU_W), F32),
                   S((LRU_HEADS, HEAD, HEAD), F32), S((1, LRU_W), F32), S((1, LRU_W), F32)],
        compiler_params=_cp("parallel"), name="lru_bwd")(
            proj, proj, p["conv_w"], p["conv_b"], p["w_a"], p["b_a"], p["w_x"], p["b_x"], p["lam"], dycat)


def _rope(x, c, s):
    x1 = x[:, :ROPE // 2]
    x2 = x[:, ROPE // 2:]
    return jnp.concatenate([x1 * c - x2 * s, x1 * s + x2 * c], axis=-1)


def _rope_t(d, c, s):
    d1 = d[:, :ROPE // 2]
    d2 = d[:, ROPE // 2:]
    return jnp.concatenate([d1 * c + d2 * s, d2 * c - d1 * s], axis=-1)


def _rope_tables(pos2, inv_freq):
    t = pos2.shape[0]

    def body(p_ref, f_ref, c_ref, s_ref):
        ang = p_ref[...].astype(F32) * f_ref[...]
        c_ref[...] = jnp.cos(ang)
        s_ref[...] = jnp.sin(ang)

    return pl.pallas_call(body, out_shape=[S((t, ROPE // 2), F32), S((t, ROPE // 2), F32)],
                          name="rope_tables")(pos2, inv_freq)


def _down_norm(xb, wdown_g, gq3, gkv3, cos, sin, j):
    t = xb.shape[0]
    bm = _row_tile(t)

    def body(x_ref, w_ref, gq_ref, gkv_ref, c_ref, s_ref, down_ref, cq_ref, ckv_ref, kpe_ref):
        w = w_ref[...].reshape(D, ODD_IN)
        down = _dot(x_ref[...], w)
        down_ref[...] = down
        q = down[:, :Q_RANK]
        cq_ref[...] = (q * lax.rsqrt(jnp.mean(q * q, axis=-1, keepdims=True) + RMS_EPS) * gq_ref[...]).astype(BF16)
        kv = down[:, Q_RANK:Q_RANK + KV_RANK]
        ckv_ref[...] = (kv * lax.rsqrt(jnp.mean(kv * kv, axis=-1, keepdims=True) + RMS_EPS)
                        * gkv_ref[...]).astype(BF16)
        kpe_ref[...] = _rope(down[:, Q_RANK + KV_RANK:], c_ref[...], s_ref[...])

    row = lambda n: pl.BlockSpec((bm, n), lambda i: (i, 0))
    return pl.pallas_call(
        body, grid=(t // bm,),
        in_specs=[row(D), _full((N_DEV, D // N_DEV, ODD_IN)),
                  pl.BlockSpec((None, 1, Q_RANK), lambda i: (j, 0, 0)),
                  pl.BlockSpec((None, 1, KV_RANK), lambda i: (j, 0, 0)), row(ROPE // 2), row(ROPE // 2)],
        out_specs=[row(ODD_IN), row(Q_RANK), row(KV_RANK), row(ROPE)],
        out_shape=[S((t, ODD_IN), F32), S((t, Q_RANK), BF16), S((t, KV_RANK), BF16), S((t, ROPE), F32)],
        compiler_params=_cp("parallel"), name="down_norm")(xb, wdown_g, gq3, gkv3, cos, sin)


def _q_tile(t):
    return min(256, t // 2)


def _attn_probs(qn, qp, kn, kp, qs):
    s = (_dot_nt(qn, kn) + _dot_nt(qp, kp)) * ATT_SCALE
    rows = qs + lax.broadcasted_iota(jnp.int32, s.shape, 0)
    cols = lax.broadcasted_iota(jnp.int32, s.shape, 1)
    s = jnp.where(jnp.right_shift(cols, CHUNK_SHIFT) <= jnp.right_shift(rows, CHUNK_SHIFT), s, NEG)
    e = jnp.exp(s - jnp.max(s, axis=-1, keepdims=True))
    return e / jnp.sum(e, axis=-1, keepdims=True)


def _head_qkv(cq, ckv, kpe, c, s, wq_ref, wkv_ref):
    qn = _dot(cq, wq_ref[:, :NOPE]).astype(BF16)
    qp = _rope(_dot(cq, wq_ref[:, NOPE:]), c, s).astype(BF16)
    kn = _dot(ckv, wkv_ref[:, :NOPE]).astype(BF16)
    vv = _dot(ckv, wkv_ref[:, NOPE:]).astype(BF16)
    return qn, qp, kn, kpe.astype(BF16), vv


def _attn_in_specs(t):
    return [_full((t, Q_RANK)), _full((t, KV_RANK)), _full((t, ROPE)), _full((t, ROPE // 2)), _full((t, ROPE // 2)),
            pl.BlockSpec((None, Q_RANK, NOPE + ROPE), lambda h: (h, 0, 0)),
            pl.BlockSpec((None, KV_RANK, NOPE + VDIM), lambda h: (h, 0, 0)),
            pl.BlockSpec((None, VDIM, D), lambda h: (h, 0, 0))]


def _attn_fwd(cq, ckv, kpe, cos, sin, wqb_g, wkvb_g, wo_g):
    t = cq.shape[0]
    tq = _q_tile(t)

    def body(cq_ref, ckv_ref, kpe_ref, c_ref, s_ref, wq_ref, wkv_ref, wo_ref, o_ref, mix_ref):
        qn, qp, kn, kp, vv = _head_qkv(cq_ref[...], ckv_ref[...], kpe_ref[...], c_ref[...], s_ref[...],
                                       wq_ref, wkv_ref)
        for qs in range(0, t, tq):
            ke = qs + tq
            p = _attn_probs(qn[qs:ke], qp[qs:ke], kn[:ke], kp[:ke], qs)
            o_ref[qs:ke, :] = _dot(p.astype(BF16), vv[:ke]).astype(BF16)
        c = _dot(o_ref[...], wo_ref[...])

        @pl.when(pl.program_id(0) == 0)
        def _():
            mix_ref[...] = c

        @pl.when(pl.program_id(0) > 0)
        def _():
            mix_ref[...] += c

    return pl.pallas_call(
        body, grid=(MLA_HEADS,), in_specs=_attn_in_specs(t),
        out_specs=[pl.BlockSpec((None, t, VDIM), lambda h: (h, 0, 0)), _full((t, D))],
        out_shape=[S((MLA_HEADS, t, VDIM), BF16), S((t, D), F32)],
        compiler_params=_cp("arbitrary"), name="attn_fwd")(cq, ckv, kpe, cos, sin, wqb_g, wkvb_g, wo_g)


def _attn_bwd(cq, ckv, kpe, cos, sin, wqb_g, wkvb_g, wo_g, o, dzb):
    t = cq.shape[0]
    tq = _q_tile(t)

    def body(cq_ref, ckv_ref, kpe_ref, c_ref, s_ref, wq_ref, wkv_ref, wo_ref, o_ref, dz_ref,
             dwo_ref, dwq_ref, dwkv_ref, dcq_ref, dckv_ref, dkpe_ref, dkn_s, dkp_s, dv_s, dqn_s, dqp_s):
        cqv = cq_ref[...]
        ckvv = ckv_ref[...]
        c = c_ref[...]
        s = s_ref[...]
        qn, qp, kn, kp, vv = _head_qkv(cqv, ckvv, kpe_ref[...], c, s, wq_ref, wkv_ref)
        dzv = dz_ref[...]
        dwo_ref[...] = _dot_tn(o_ref[...], dzv).astype(BF16)
        do = _dot_nt(dzv, wo_ref[...]).astype(BF16)
        dkn_s[...] = jnp.zeros_like(dkn_s)
        dkp_s[...] = jnp.zeros_like(dkp_s)
        dv_s[...] = jnp.zeros_like(dv_s)
        for qs in range(0, t, tq):
            ke = qs + tq
            p = _attn_probs(qn[qs:ke], qp[qs:ke], kn[:ke], kp[:ke], qs)
            dp = _dot_nt(do[qs:ke], vv[:ke])
            ds = (p * (dp - jnp.sum(p * dp, axis=-1, keepdims=True)) * ATT_SCALE).astype(BF16)
            dqn_s[qs:ke, :] = _dot(ds, kn[:ke])
            dqp_s[qs:ke, :] = _dot(ds, kp[:ke])
            dkn_s[0:ke, :] += _dot_tn(ds, qn[qs:ke])
            dkp_s[0:ke, :] += _dot_tn(ds, qp[qs:ke])
            dv_s[0:ke, :] += _dot_tn(p.astype(BF16), do[qs:ke])
        dqn = dqn_s[...].astype(BF16)
        dqp = _rope_t(dqp_s[...], c, s).astype(BF16)
        dkn = dkn_s[...].astype(BF16)
        dvv = dv_s[...].astype(BF16)
        dwq_ref[:, :NOPE] = _dot_tn(cqv, dqn).astype(BF16)
        dwq_ref[:, NOPE:] = _dot_tn(cqv, dqp).astype(BF16)
        dwkv_ref[:, :NOPE] = _dot_tn(ckvv, dkn).astype(BF16)
        dwkv_ref[:, NOPE:] = _dot_tn(ckvv, dvv).astype(BF16)
        dcq = _dot_nt(dqn, wq_ref[:, :NOPE]) + _dot_nt(dqp, wq_ref[:, NOPE:])
        dckv = _dot_nt(dkn, wkv_ref[:, :NOPE]) + _dot_nt(dvv, wkv_ref[:, NOPE:])

        @pl.when(pl.program_id(0) == 0)
        def _():
            dcq_ref[...] = dcq
            dckv_ref[...] = dckv
            dkpe_ref[...] = dkp_s[...]

        @pl.when(pl.program_id(0) > 0)
        def _():
            dcq_ref[...] += dcq
            dckv_ref[...] += dckv
            dkpe_ref[...] += dkp_s[...]

    per_head = lambda a, b: pl.BlockSpec((None, a, b), lambda h: (h, 0, 0))
    return pl.pallas_call(
        body, grid=(MLA_HEADS,),
        in_specs=_attn_in_specs(t) + [per_head(t, VDIM), _full((t, D))],
        out_specs=[per_head(VDIM, D), per_head(Q_RANK, NOPE + ROPE), per_head(KV_RANK, NOPE + VDIM),
                   _full((t, Q_RANK)), _full((t, KV_RANK)), _full((t, ROPE))],
        out_shape=[S((MLA_HEADS, VDIM, D), BF16), S((MLA_HEADS, Q_RANK, NOPE + ROPE), BF16),
                   S((MLA_HEADS, KV_RANK, NOPE + VDIM), BF16),
                   S((t, Q_RANK), F32), S((t, KV_RANK), F32), S((t, ROPE), F32)],
        scratch_shapes=[pltpu.VMEM((t, NOPE), F32), pltpu.VMEM((t, ROPE), F32), pltpu.VMEM((t, VDIM), F32),
                        pltpu.VMEM((t, NOPE), F32), pltpu.VMEM((t, ROPE), F32)],
        compiler_params=_cp("arbitrary"), name="attn_bwd")(cq, ckv, kpe, cos, sin, wqb_g, wkvb_g, wo_g, o, dzb)


def _rms_bwd(down, dcq, dckv, dkpe, cos, sin, gq3, gkv3, j):
    t = down.shape[0]
    bm = _row_tile(t)

    def body(down_ref, dcq_ref, dckv_ref, dkpe_ref, c_ref, s_ref, gq_ref, gkv_ref, dd_ref, dgq_ref, dgkv_ref):
        @pl.when(pl.program_id(0) == 0)
        def _():
            dgq_ref[...] = jnp.zeros_like(dgq_ref)
            dgkv_ref[...] = jnp.zeros_like(dgkv_ref)

        def rms_b(x, dy, g):
            rstd = lax.rsqrt(jnp.mean(x * x, axis=-1, keepdims=True) + RMS_EPS)
            xh = x * rstd
            dyg = dy * g
            return rstd * (dyg - xh * jnp.mean(dyg * xh, axis=-1, keepdims=True)), jnp.sum(dy * xh, axis=0, keepdims=True)

        dq, dgq = rms_b(down_ref[:, :Q_RANK], dcq_ref[...], gq_ref[...])
        dkv, dgkv = rms_b(down_ref[:, Q_RANK:Q_RANK + KV_RANK], dckv_ref[...], gkv_ref[...])
        dgq_ref[...] += dgq
        dgkv_ref[...] += dgkv
        dd_ref[:, :Q_RANK] = dq.astype(BF16)
        dd_ref[:, Q_RANK:Q_RANK + KV_RANK] = dkv.astype(BF16)
        dd_ref[:, Q_RANK + KV_RANK:] = _rope_t(dkpe_ref[...], c_ref[...], s_ref[...]).astype(BF16)

    row = lambda n: pl.BlockSpec((bm, n), lambda i: (i, 0))
    return pl.pallas_call(
        body, grid=(t // bm,),
        in_specs=[row(ODD_IN), row(Q_RANK), row(KV_RANK), row(ROPE), row(ROPE // 2), row(ROPE // 2),
                  pl.BlockSpec((None, 1, Q_RANK), lambda i: (j, 0, 0)),
                  pl.BlockSpec((None, 1, KV_RANK), lambda i: (j, 0, 0))],
        out_specs=[row(ODD_IN), _full((1, Q_RANK)), _full((1, KV_RANK))],
        out_shape=[S((t, ODD_IN), BF16), S((1, Q_RANK), F32), S((1, KV_RANK), F32)],
        compiler_params=_cp("arbitrary"), name="rms_bwd")(down, dcq, dckv, dkpe, cos, sin, gq3, gkv3)


def _col_blocks(t, n, bn):
    return pl.BlockSpec((t, bn), lambda i: (0, i))


def _row_blocks(n, bm):
    return pl.BlockSpec((bm, n), lambda i: (i, 0))


def _local_step(x, pos2, tgt, small, weights_of, grads_done):
    t = x.shape[0]
    bm = _row_tile(t)
    inv_freq = (ROPE_THETA ** (-jnp.arange(0, ROPE, 2, dtype=F32) / ROPE)).reshape(1, ROPE // 2)
    cos, sin = _rope_tables(pos2, inv_freq)
    lru_p = {k: small[k] for k in ("conv_w", "conv_b", "w_a", "b_a", "w_x", "b_x", "lam")}

    saved = []
    y, yb = x, x.astype(BF16)
    for l in range(DEPTH):
        j = l // 2
        big = weights_of(l, y)
        sv = dict(xb=yb, big=big)
        if l % 2 == 0:
            proj = _mm(yb, big["win2d"], mode="nn", grid=(EVEN_IN // 512,), a_spec=_full((t, D)),
                       b_spec=_col_blocks(D, EVEN_IN, 512), out_shape=S((t, EVEN_IN), F32),
                       out_spec=_col_blocks(t, EVEN_IN, 512), name="even_proj")
            ycat = jnp.concatenate([_pool_fwd(proj, small["pool_w"], small["pool_scale"], j),
                                    _lru_fwd(proj, lru_p, j)], axis=1)
            mix = _mm(ycat, big["wout2d"], mode="nn", grid=(D // 512,), a_spec=_full((t, EVEN_MIX)),
                      b_spec=_col_blocks(EVEN_MIX, D, 512), out_shape=S((t, D), F32),
                      out_spec=_col_blocks(t, D, 512), name="even_out")
            sv.update(proj=proj, ycat=ycat)
        else:
            down, cq, ckv, kpe = _down_norm(yb, big["wdown"], small["gq"], small["gkv"], cos, sin, j)
            o, mix = _attn_fwd(cq, ckv, kpe, cos, sin, big["wqb"], big["wkvb"], big["wo"])
            sv.update(down=down, cq=cq, ckv=ckv, kpe=kpe, o=o)
        z1, y1, y1b = _resid_ln(y, mix, small["ln_mix_g"], small["ln_mix_b"], l, "resid_ln")
        ff = _mlp_fwd(y1b, big["w1"], big["w2"])
        z2, y, yb = _resid_ln(y1, ff, small["ln_ffn_g"], small["ln_ffn_b"], l, "resid_ln")
        sv.update(z1=z1, y1b=y1b, z2=z2)
        saved.append(sv)

    dy, loss_tile = _loss_grad(y, tgt)

    g = {k: [None] * n for k, n in (("ln_mix_g", 4), ("ln_mix_b", 4), ("ln_ffn_g", 4), ("ln_ffn_b", 4),
                                    ("pool_w", 2), ("pool_scale", 2), ("conv_w", 2), ("conv_b", 2),
                                    ("w_a", 2), ("b_a", 2), ("w_x", 2), ("b_x", 2), ("lam", 2),
                                    ("gq", 2), ("gkv", 2))}
    dep = None
    for l in reversed(range(DEPTH)):
        j = l // 2
        sv = saved[l]
        big = sv["big"]
        dz2, dz2b, g["ln_ffn_g"][l], g["ln_ffn_b"][l] = _ln_bwd(dy, sv["z2"], small["ln_ffn_g"], l, "ln_bwd", dep=dep)
        act, dh, dff = _mlp_bwd_dh(sv["y1b"], dz2b, big["w1"], big["w2"])
        dw1 = _mm(sv["y1b"], dh, mode="tn", grid=(N_DEV,), a_spec=_full((t, D)),
                  b_spec=_col_blocks(t, D_FF, FF_BLK), out_shape=S((N_DEV, D, FF_BLK), BF16),
                  out_spec=pl.BlockSpec((None, D, FF_BLK), lambda i: (i, 0, 0)), name="mlp_dw1")
        dw2 = _mm(act, dz2b, mode="tn", grid=(N_DEV,), a_spec=_col_blocks(t, D_FF, FF_BLK),
                  b_spec=_full((t, D)), out_shape=S((N_DEV, FF_BLK, D), BF16),
                  out_spec=pl.BlockSpec((None, FF_BLK, D), lambda i: (i, 0, 0)), name="mlp_dw2")
        dep = grads_done(l, dict(w1=dw1, w2=dw2))
        dz1, dz1b, g["ln_mix_g"][l], g["ln_mix_b"][l] = _ln_bwd(dff, sv["z1"], small["ln_mix_g"], l, "ln_bwd_res",
                                                                 r=dz2, dep=dep)
        if l % 2 == 0:
            wout = big["wout2d"]
            dycat = _mm(dz1b, wout, mode="nt", grid=(EVEN_MIX // 512,), a_spec=_full((t, D)),
                        b_spec=_row_blocks(D, 512), out_shape=S((t, EVEN_MIX), F32),
                        out_spec=_col_blocks(t, EVEN_MIX, 512), name="even_dycat")
            dwout = _mm(sv["ycat"], dz1b, mode="tn", grid=(EVEN_MIX // 512,), a_spec=_col_blocks(t, EVEN_MIX, 512),
                        b_spec=_full((t, D)), out_shape=S((EVEN_MIX, D), BF16), out_spec=_row_blocks(D, 512),
                        name="even_dwout")
            du_pool, g["pool_w"][j], g["pool_scale"][j] = _pool_bwd(sv["proj"], dycat, small["pool_w"],
                                                                   small["pool_scale"], j)
            (du_lru, du_gate, g["conv_w"][j], g["conv_b"][j], g["w_a"][j], g["b_a"][j], g["w_x"][j], g["b_x"][j],
             g["lam"][j]) = _lru_bwd(sv["proj"], dycat, lru_p, j)
            dproj = jnp.concatenate([du_pool, du_lru, du_gate], axis=1)
            dwin = _mm(sv["xb"], dproj, mode="tn", grid=(EVEN_IN // 512,), a_spec=_full((t, D)),
                       b_spec=_col_blocks(t, EVEN_IN, 512), out_shape=S((D, EVEN_IN), BF16),
                       out_spec=_col_blocks(D, EVEN_IN, 512), name="even_dwin")
            dep = grads_done(l, dict(win=dwin.reshape(D, N_DEV, EVEN_IN // N_DEV).transpose(1, 0, 2),
                                     wout=dwout.reshape(N_DEV, EVEN_MIX // N_DEV, D)))
            dy = _mm(dproj, big["win2d"], mode="nt", grid=(t // bm,), a_spec=_row_blocks(EVEN_IN, bm),
                     b_spec=_full((D, EVEN_IN)), out_shape=S((t, D), F32), out_spec=_row_blocks(D, bm),
                     add=dz1, add_spec=_row_blocks(D, bm), add_scale=ALPHA, name="even_dx")
        else:
            dwo, dwqb, dwkvb, dcq, dckv, dkpe = _attn_bwd(
                sv["cq"], sv["ckv"], sv["kpe"], cos, sin, big["wqb"], big["wkvb"], big["wo"], sv["o"], dz1b)
            ddown, g["gq"][j], g["gkv"][j] = _rms_bwd(sv["down"], dcq, dckv, dkpe, cos, sin, small["gq"],
                                                     small["gkv"], j)
            dwdown = _mm(sv["xb"], ddown, mode="tn", grid=(N_DEV,), a_spec=_col_blocks(t, D, D // N_DEV),
                         b_spec=_full((t, ODD_IN)), out_shape=S((N_DEV, D // N_DEV, ODD_IN), BF16),
                         out_spec=pl.BlockSpec((None, D // N_DEV, ODD_IN), lambda i: (i, 0, 0)),
                         name="odd_dwdown")
            dep = grads_done(l, dict(wdown=dwdown, wqb=dwqb, wkvb=dwkvb, wo=dwo))
            dy = _mm(ddown, big["wdown2d"], mode="nt", grid=(t // bm,), a_spec=_row_blocks(ODD_IN, bm),
                     b_spec=_full((D, ODD_IN)), out_shape=S((t, D), F32), out_spec=_row_blocks(D, bm),
                     add=dz1, add_spec=_row_blocks(D, bm), add_scale=ALPHA, name="odd_dx")
    return loss_tile[0, 0], dy, g


def _mesh_place():
    x, y, c = lax.axis_index("x"), lax.axis_index("y"), lax.axis_index("c")
    return x, y, c


def _peer(place, k):
    x, y, c = place
    return (1 - x if k & 4 else x, 1 - y if k & 2 else y, 1 - c if k & 1 else c)


def _index(place):
    x, y, c = place
    return 4 * x + 2 * y + c


ANY = pl.BlockSpec(memory_space=pl.ANY)


def _all_gather_big(zones):
    n = len(zones)

    def body(*refs):
        outs = refs[n:2 * n]
        send, recv = refs[2 * n:]
        x, y, c = _mesh_place()
        me, sibling = (x, y, c), (x, y, 1 - c)
        chips = [(1 - x, y), (x, 1 - y), (1 - x, 1 - y)]

        def copy(w, k, block, to):
            blk = outs[w].at[_index(block)]
            return pltpu.make_async_remote_copy(src_ref=blk, dst_ref=blk, send_sem=send.at[w, k], recv_sem=recv.at[w, k],
                                                device_id=to, device_id_type=MESH)

        first = []
        for w in range(n):
            first.append(copy(w, 0, me, sibling))
            first += [copy(w, 1 + j, me, (*chip, c)) for j, chip in enumerate(chips)]
        for cp in first:
            cp.start()
        passed = []
        for w in range(n):
            for j, chip in enumerate(chips):
                copy(w, 1 + j, (*chip, c), me).wait_recv()
                cp = copy(w, 4 + j, (*chip, c), sibling)
                cp.start()
                passed.append(cp)
        for w in range(n):
            copy(w, 0, sibling, me).wait_recv()
            for j, chip in enumerate(chips):
                copy(w, 4 + j, (*chip, 1 - c), me).wait_recv()
        for cp in first + passed:
            cp.wait_send()

    return pl.pallas_call(
        body, in_specs=[ANY] * n, out_specs=[ANY] * n, out_shape=[S(z.shape, z.dtype) for z in zones],
        input_output_aliases={i: i for i in range(n)},
        scratch_shapes=[pltpu.SemaphoreType.DMA((n, N_DEV - 1)), pltpu.SemaphoreType.DMA((n, N_DEV - 1))],
        compiler_params=pltpu.CompilerParams(has_side_effects=True), name="all_gather_big")(*zones)


def _shard_rows_tile(a):
    return max(d for d in range(16, 257, 16) if a % d == 0)


HBM = pl.BlockSpec(memory_space=pltpu.HBM)
SEM = pl.BlockSpec(memory_space=pltpu.SEMAPHORE)
DATAFLOW = pltpu.SideEffectType.DATAFLOW_SIDE_EFFECTING


def _in_hbm(a):
    return pltpu.with_memory_space_constraint(a, pltpu.HBM)


def _gather_ici_copies(place, src, land, w):
    me = _index(place)
    return [(_peer(place, k), land.at[me], land.at[me]) for k in (1, 2, 4, 6)]


def _gather_d2d_copies(place, src, land, w):
    blocks = [_index(_peer(place, k)) for k in (2, 4, 6)]
    return [(_peer(place, 1), land.at[b], land.at[b]) for b in blocks]


GATHER_ICI = (4, _gather_ici_copies)
GATHER_D2D = (3, _gather_d2d_copies)


def _scatter_plan(layers):
    def copies(place, src, land, w):
        me = _index(place)
        return [(_peer(place, k), src.at[_index(_peer(place, k))], land.at[me, layers[w]]) for k in range(1, N_DEV)]
    return (N_DEV - 1, copies)


def _exchange_start(srcs, lands, plan, name, after=()):
    ns, n = len(srcs), len(lands)
    n_in = ns + n + len(after)
    per, copies = plan

    def body(*refs):
        ins, land = refs[:ns], refs[ns:ns + n]
        send, recv = refs[n_in], refs[n_in + 1]
        token = refs[-1]
        place = _mesh_place()
        for i in range(per):
            for w in range(n):
                target, src, dst = copies(place, ins[w] if ns else None, land[w], w)[i]
                pltpu.make_async_remote_copy(src_ref=src, dst_ref=dst, send_sem=send.at[w * per + i],
                                             recv_sem=recv.at[w * per + i], device_id=target, device_id_type=MESH).start()
        token[...] = jnp.zeros_like(token)

    sems = pltpu.SemaphoreType.DMA((n * per,))
    thru = [pltpu.HBM(a.shape, a.dtype) for a in list(srcs) + list(lands)]
    out = pl.pallas_call(
        body, name=name, in_specs=[HBM] * (ns + n) + [ANY] * len(after),
        out_shape=(sems, sems, *thru, S((8, 128), F32)),
        out_specs=(SEM, SEM, *([HBM] * (ns + n)), pl.BlockSpec(memory_space=pltpu.VMEM)),
        input_output_aliases={i: 2 + i for i in range(ns + n)},
        compiler_params=pltpu.CompilerParams(has_side_effects=DATAFLOW),
    )(*[_in_hbm(a) for a in list(srcs) + list(lands)], *after)
    return out[0], out[1], list(out[2:2 + ns]), list(out[2 + ns:2 + ns + n]), out[-1]


def _exchange_wait(send, recv, srcs, lands, plan, after, name):
    ns, n = len(srcs), len(lands)
    per, copies = plan

    def body(*refs):
        ins, land = refs[:ns], refs[ns:ns + n]
        send_ref, recv_ref = refs[ns + n], refs[ns + n + 1]
        place = _mesh_place()
        for i in range(per):
            for w in range(n):
                target, src, dst = copies(place, ins[w] if ns else None, land[w], w)[i]
                cp = pltpu.make_async_remote_copy(src_ref=src, dst_ref=dst, send_sem=send_ref.at[w * per + i],
                                                  recv_sem=recv_ref.at[w * per + i], device_id=target,
                                                  device_id_type=MESH)
                cp.wait_send()
                cp.wait_recv()

    thru = [pltpu.HBM(a.shape, a.dtype) for a in list(srcs) + list(lands)]
    out = pl.pallas_call(
        body, name=name, in_specs=[HBM] * (ns + n) + [SEM, SEM, ANY],
        out_shape=tuple(thru), out_specs=tuple([HBM] * (ns + n)),
        input_output_aliases={i: i for i in range(ns + n)},
        compiler_params=pltpu.CompilerParams(has_side_effects=DATAFLOW),
    )(*srcs, *lands, send, recv, after)
    return list(out[:ns]), list(out[ns:])


def _all_reduce_small(part, name):
    r = part.shape[1]

    def body(p_ref, o_ref, rbuf, send1, recv1, send2, recv2):
        place = _mesh_place()
        me = _index(place)
        rbuf[pl.ds(me, 1)] = p_ref[pl.ds(me, 1)]
        first = [pltpu.make_async_remote_copy(src_ref=p_ref.at[_index(_peer(place, k))], dst_ref=rbuf.at[me],
                                              send_sem=send1.at[k - 1], recv_sem=recv1.at[k - 1],
                                              device_id=_peer(place, k), device_id_type=MESH)
                 for k in range(1, N_DEV)]
        for cp in first:
            cp.start()
        for cp in first:
            cp.wait()
        acc = rbuf[0]
        for d in range(1, N_DEV):
            acc = acc + rbuf[d]
        o_ref[pl.ds(me, 1)] = acc[None]
        second = [pltpu.make_async_remote_copy(src_ref=o_ref.at[me], dst_ref=o_ref.at[me], send_sem=send2.at[k - 1],
                                               recv_sem=recv2.at[k - 1], device_id=_peer(place, k),
                                               device_id_type=MESH)
                  for k in range(1, N_DEV)]
        for cp in second:
            cp.start()
        for cp in second:
            cp.wait()

    vm = pl.BlockSpec(memory_space=pltpu.VMEM)
    return pl.pallas_call(
        body, in_specs=[vm], out_specs=vm, out_shape=S(part.shape, F32),
        scratch_shapes=[pltpu.VMEM(part.shape, F32)] + [pltpu.SemaphoreType.DMA((N_DEV - 1,))] * 4,
        compiler_params=pltpu.CompilerParams(has_side_effects=True, vmem_limit_bytes=VMEM_LIMIT), name=name)(part)


def _adamw(w, g, m, v):
    m = ADAM_B1 * m + (1.0 - ADAM_B1) * g
    v = ADAM_B2 * v + (1.0 - ADAM_B2) * (g * g)
    m_hat = m / (1.0 - ADAM_B1 ** ADAM_STEP)
    v_hat = v / (1.0 - ADAM_B2 ** ADAM_STEP)
    return -ADAM_LR * (m_hat / (jnp.sqrt(v_hat) + ADAM_EPS) + ADAM_WD * w), m, v


def _adam_big(parts, own, me, w, m, v, name):
    nl, a, b = w.shape
    ta = _shard_rows_tile(a)

    def body(me_ref, p_ref, *refs):
        own_refs, (w_ref, m_ref, v_ref, g_ref, d_ref, mo_ref, vo_ref) = refs[:nl], refs[nl:]
        layer = pl.program_id(0)
        mine = own_refs[0][...]
        for k in range(1, nl):
            mine = jnp.where(layer == k, own_refs[k][...], mine)
        g = None
        for s in range(N_DEV):
            term = jnp.where(me_ref[0] == s, mine, p_ref[s]).astype(F32)
            g = term if g is None else g + term
        g_ref[...] = g
        d_ref[...], mo_ref[...], vo_ref[...] = _adamw(w_ref[...], g, m_ref[...], v_ref[...])

    blk = pl.BlockSpec((None, ta, b), lambda l, i, me_ref: (l, i, 0))

    def own_spec(k):
        return pl.BlockSpec((None, ta, b), lambda l, i, me_ref: (me_ref[0], jnp.where(l == k, i, 0), 0))

    grid_spec = pltpu.PrefetchScalarGridSpec(
        num_scalar_prefetch=1, grid=(nl, a // ta),
        in_specs=[pl.BlockSpec((N_DEV, None, ta, b), lambda l, i, me_ref: (0, l, i, 0))]
        + [own_spec(k) for k in range(nl)] + [blk, blk, blk],
        out_specs=[blk] * 4)
    return pl.pallas_call(body, grid_spec=grid_spec, out_shape=[S(w.shape, F32)] * 4,
                          compiler_params=_cp("arbitrary", "arbitrary"), name=name)(me, parts, *own, w, m, v)


def _adam_small(g, w, m, v, name):
    def body(g_ref, w_ref, m_ref, v_ref, d_ref, mo_ref, vo_ref):
        d_ref[...], mo_ref[...], vo_ref[...] = _adamw(w_ref[...], g_ref[...], m_ref[...], v_ref[...])

    return pl.pallas_call(body, out_shape=[S(g.shape, F32)] * 3, compiler_params=_cp(), name=name)(g, w, m, v)


BIG = ("even_w_in", "even_w_out", "mla_w_down", "mla_w_qb", "mla_w_kvb", "mla_w_o", "mlp_w1", "mlp_w2")
BIG_KEY = dict(even_w_in="win", even_w_out="wout", mla_w_down="wdown", mla_w_qb="wqb", mla_w_kvb="wkvb",
               mla_w_o="wo", mlp_w1="w1", mlp_w2="w2")
SMALL = (("ln_mix_g", "ln_mix_g", None), ("ln_mix_b", "ln_mix_b", None), ("ln_ffn_g", "ln_ffn_g", None),
         ("ln_ffn_b", "ln_ffn_b", None), ("pool_w", "pool_w", None), ("pool_scale", "pool_scale", None),
         ("lru_conv_w", "conv_w", 2), ("lru_conv_b", "conv_b", None), ("lru_w_a", "w_a", None),
         ("lru_b_a", "b_a", None), ("lru_w_x", "w_x", None), ("lru_b_x", "b_x", None), ("lru_lambda", "lam", None),
         ("mla_q_norm_g", "gq", 1), ("mla_kv_norm_g", "gkv", 1))
WEIGHTS = ("ln_mix_g", "ln_mix_b", "ln_ffn_g", "ln_ffn_b", "even_w_in", "pool_w", "pool_scale", "lru_conv_w",
           "lru_conv_b", "lru_w_a", "lru_b_a", "lru_w_x", "lru_b_x", "lru_lambda", "even_w_out", "mla_w_down",
           "mla_q_norm_g", "mla_kv_norm_g", "mla_w_qb", "mla_w_kvb", "mla_w_o", "mlp_w1", "mlp_w2")
ALL_AXES = ("x", "y", "c")


def _layer_weights(l):
    j = l // 2
    if l % 2 == 0:
        mixer = [("win", "even_w_in", j), ("wout", "even_w_out", j)]
    else:
        mixer = [("wdown", "mla_w_down", j), ("wqb", "mla_w_qb", j), ("wkvb", "mla_w_kvb", j), ("wo", "mla_w_o", j)]
    return mixer + [("w1", "mlp_w1", l), ("w2", "mlp_w2", l)]


def _pack(arrays, multiple):
    flat = jnp.concatenate([a.reshape(-1) for a in arrays])
    pad = (-flat.shape[0]) % multiple
    return jnp.pad(flat, (0, pad))


def _unpack(flat, shapes):
    out, at = [], 0
    for shp in shapes:
        n = 1
        for s in shp:
            n *= s
        out.append(flat[at:at + n].reshape(shp))
        at += n
    return out


def _global_shape(local_shape, axis):
    if axis is None:
        return tuple(local_shape)
    return tuple(s * N_DEV if i == axis else s for i, s in enumerate(local_shape))


def _step(x, positions, tgt, w, m, v):
    t = x.shape[1]
    me = _index(_mesh_place())

    sharded = [(name, axis) for name, _, axis in SMALL if axis is not None]
    zeros_with_mine = [lax.dynamic_update_slice_in_dim(jnp.zeros(_global_shape(w[name].shape, axis), F32), w[name],
                                                       me * w[name].shape[axis], axis) for name, axis in sharded]
    chunk = N_DEV * 8 * 128
    gathered = _all_reduce_small(_pack(zeros_with_mine, chunk).reshape(N_DEV, -1, 128), "gather_small")
    full = dict(zip([name for name, _ in sharded],
                    _unpack(gathered.reshape(-1), [_global_shape(w[name].shape, axis) for name, axis in sharded])))

    def zone_of(shard):
        return lax.dynamic_update_slice_in_dim(lax.empty((N_DEV,) + shard.shape, BF16), shard.astype(BF16)[None], me, 0)

    zones = [[zone_of(w[name][i]) for _, name, i in _layer_weights(l)] for l in range(DEPTH)]
    first = _all_gather_big(zones[0])
    flights, after = {}, (first[0], gathered)
    for l in range(1, DEPTH):
        send, recv, _, lands, token = _exchange_start([], zones[l], GATHER_ICI, "gather_start_%d" % l, after=after)
        flights[l] = (send, recv, [], lands)
        after = (token,)

    def weights_of(l, after):
        if l == 0:
            arrays = first
        else:
            _, lands = _exchange_wait(*flights[l], GATHER_ICI, after, "gather_wait_%d" % l)
            send, recv, _, lands, _ = _exchange_start([], lands, GATHER_D2D, "gather_pass_%d" % l)
            _, arrays = _exchange_wait(send, recv, [], lands, GATHER_D2D, after, "gather_pass_wait_%d" % l)
        big = {key: a for (key, _, _), a in zip(_layer_weights(l), arrays)}
        if l % 2 == 0:
            big["win2d"] = big["win"].transpose(1, 0, 2).reshape(D, EVEN_IN)
            big["wout2d"] = big["wout"].reshape(EVEN_MIX, D)
        else:
            big["wdown2d"] = big["wdown"].reshape(D, ODD_IN)
        return big

    zone = {name: lax.empty((N_DEV,) + w[name].shape, BF16) for name in BIG}
    name_of = {key: name for name, key in BIG_KEY.items()}
    sent = []

    def grads_done(l, grads):
        keys = list(grads)
        index = {key: i for key, _, i in _layer_weights(l)}
        layers = [index[key] for key in keys]
        send, recv, srcs, lands, tok = _exchange_start([grads[k] for k in keys], [zone[name_of[k]] for k in keys],
                                                       _scatter_plan(layers), "scatter_start_%d_%s" % (l, keys[0]))
        for k, land in zip(keys, lands):
            zone[name_of[k]] = land
        sent.append((send, recv, srcs, keys, layers))
        return tok

    row3 = lambda a: a.reshape(a.shape[0], 1, a.shape[1])
    small = dict(ln_mix_g=row3(w["ln_mix_g"]), ln_mix_b=row3(w["ln_mix_b"]), ln_ffn_g=row3(w["ln_ffn_g"]),
                 ln_ffn_b=row3(w["ln_ffn_b"]), pool_w=w["pool_w"], pool_scale=row3(w["pool_scale"]),
                 conv_w=full["lru_conv_w"], conv_b=row3(w["lru_conv_b"]), w_a=w["lru_w_a"], b_a=row3(w["lru_b_a"]),
                 w_x=w["lru_w_x"], b_x=row3(w["lru_b_x"]), lam=row3(w["lru_lambda"]),
                 gq=row3(full["mla_q_norm_g"]), gkv=row3(full["mla_kv_norm_g"]))

    loss_part, grad_x, g = _local_step(x[0] + token[0, 0], positions.reshape(t, 1), tgt[0], small, weights_of,
                                       grads_done)
    loss = lax.psum(loss_part, ALL_AXES)

    after = grad_x
    own = {name: [None] * w[name].shape[0] for name in BIG}
    for n_flight, (send, recv, srcs, keys, layers) in enumerate(sent):
        srcs, lands = _exchange_wait(send, recv, srcs, [zone[name_of[k]] for k in keys], _scatter_plan(layers),
                                     after, "scatter_wait_%d" % n_flight)
        for k, land, src, layer in zip(keys, lands, srcs, layers):
            zone[name_of[k]] = land
            own[name_of[k]][layer] = src
        after = lands[0]
    me_arr = me.astype(jnp.int32).reshape(1)
    out = {}
    for name in BIG:
        out[name] = _adam_big(zone[name], own[name], me_arr, w[name], m[name], v[name], "adam_" + name)

    local_g = [jnp.stack(g[key]).reshape(_global_shape(w[name].shape, axis)) for name, key, axis in SMALL]
    reduced = _all_reduce_small(_pack(local_g, chunk).reshape(N_DEV, -1, 128), "all_reduce_small")
    reduced = _unpack(reduced.reshape(-1), [a.shape for a in local_g])
    mine = [a if axis is None else lax.dynamic_slice_in_dim(a, me * w[name].shape[axis], w[name].shape[axis], axis)
            for a, (name, _, axis) in zip(reduced, SMALL)]
    for grad, (name, _, _) in zip(mine, SMALL):
        shape = w[name].shape
        as_2d = lambda a: a.reshape(-1, shape[-1])
        new = _adam_small(as_2d(grad), as_2d(w[name]), as_2d(m[name]), as_2d(v[name]), "adam_" + name)
        out[name] = (grad,) + tuple(a.reshape(shape) for a in new)

    return (loss, grad_x[None]) + tuple(out[name][i] for i in range(4) for name in WEIGHTS)


def kernel(x, positions, ln_mix_g, ln_mix_b, ln_ffn_g, ln_ffn_b, even_w_in, pool_w, pool_scale, lru_conv_w, lru_conv_b, lru_w_a, lru_b_a, lru_w_x, lru_b_x, lru_lambda, even_w_out, mla_w_down, mla_q_norm_g, mla_kv_norm_g, mla_w_qb, mla_w_kvb, mla_w_o, mlp_w1, mlp_w2, loss_target, m_ln_mix_g, m_ln_mix_b, m_ln_ffn_g, m_ln_ffn_b, m_even_w_in, m_pool_w, m_pool_scale, m_lru_conv_w, m_lru_conv_b, m_lru_w_a, m_lru_b_a, m_lru_w_x, m_lru_b_x, m_lru_lambda, m_even_w_out, m_mla_w_down, m_mla_q_norm_g, m_mla_kv_norm_g, m_mla_w_qb, m_mla_w_kvb, m_mla_w_o, m_mlp_w1, m_mlp_w2, v_ln_mix_g, v_ln_mix_b, v_ln_ffn_g, v_ln_ffn_b, v_even_w_in, v_pool_w, v_pool_scale, v_lru_conv_w, v_lru_conv_b, v_lru_w_a, v_lru_b_a, v_lru_w_x, v_lru_b_x, v_lru_lambda, v_even_w_out, v_mla_w_down, v_mla_q_norm_g, v_mla_kv_norm_g, v_mla_w_qb, v_mla_w_kvb, v_mla_w_o, v_mlp_w1, v_mlp_w2):
    w = dict(zip(WEIGHTS, (ln_mix_g, ln_mix_b, ln_ffn_g, ln_ffn_b, even_w_in, pool_w, pool_scale, lru_conv_w,
                           lru_conv_b, lru_w_a, lru_b_a, lru_w_x, lru_b_x, lru_lambda, even_w_out, mla_w_down,
                           mla_q_norm_g, mla_kv_norm_g, mla_w_qb, mla_w_kvb, mla_w_o, mlp_w1, mlp_w2)))
    m = dict(zip(WEIGHTS, (m_ln_mix_g, m_ln_mix_b, m_ln_ffn_g, m_ln_ffn_b, m_even_w_in, m_pool_w, m_pool_scale,
                           m_lru_conv_w, m_lru_conv_b, m_lru_w_a, m_lru_b_a, m_lru_w_x, m_lru_b_x, m_lru_lambda,
                           m_even_w_out, m_mla_w_down, m_mla_q_norm_g, m_mla_kv_norm_g, m_mla_w_qb, m_mla_w_kvb,
                           m_mla_w_o, m_mlp_w1, m_mlp_w2)))
    v = dict(zip(WEIGHTS, (v_ln_mix_g, v_ln_mix_b, v_ln_ffn_g, v_ln_ffn_b, v_even_w_in, v_pool_w, v_pool_scale,
                           v_lru_conv_w, v_lru_conv_b, v_lru_w_a, v_lru_b_a, v_lru_w_x, v_lru_b_x, v_lru_lambda,
                           v_even_w_out, v_mla_w_down, v_mla_q_norm_g, v_mla_kv_norm_g, v_mla_w_qb, v_mla_w_kvb,
                           v_mla_w_o, v_mlp_w1, v_mlp_w2)))
    return _step(x, positions, loss_target, w, m, v)
```

```python
import functools

import jax
import jax.numpy as jnp
from jax import lax
from jax.experimental import pallas as pl
from jax.experimental.pallas import tpu as pltpu

F32 = jnp.float32
BF16 = jnp.bfloat16
S = jax.ShapeDtypeStruct

D = 1024
DEPTH = 4
N_DEV = 8
CHUNK_SHIFT = 6
POOL_WINDOWS = (2, 4, 8, 16)
POOL_W = 512
LRU_W = 1024
LRU_HEADS = 8
HEAD = 128
LRU_C = 8.0
EVEN_IN = 2560
EVEN_MIX = 1536
MLA_HEADS = 8
NOPE = 128
ROPE = 64
VDIM = 128
Q_RANK = 384
KV_RANK = 256
ODD_IN = 704
D_FF = 4096
FF_BLK = D_FF // N_DEV
ROPE_THETA = 10000.0
ALPHA = (2 * DEPTH) ** 0.25
LN_EPS = 1e-5
RMS_EPS = 1e-6
ATT_SCALE = (NOPE + ROPE) ** -0.5
NEG = float(jnp.finfo(jnp.float32).min)
ADAM_LR = 0.001
ADAM_B1 = 0.9
ADAM_B2 = 0.999
ADAM_EPS = 1e-08
ADAM_WD = 0.01
ADAM_STEP = 10
V7X_VMEM_BYTES = 64 * 1024 * 1024
VMEM_LIMIT = V7X_VMEM_BYTES - 8 * 1024 * 1024
MESH = pl.DeviceIdType.MESH


def _cp(*sem):
    return pltpu.CompilerParams(dimension_semantics=sem if sem else None, vmem_limit_bytes=VMEM_LIMIT)


def _dot(a, b):
    return jnp.dot(a, b, preferred_element_type=F32)


def _dot_nt(a, b):
    return lax.dot_general(a, b, (((1,), (1,)), ((), ())), preferred_element_type=F32)


def _dot_tn(a, b):
    return lax.dot_general(a, b, (((0,), (0,)), ((), ())), preferred_element_type=F32)


def _full(shape):
    return pl.BlockSpec(shape, lambda *_: (0,) * len(shape))


def _mm(a, b, *, mode, grid, a_spec, b_spec, out_shape, out_spec, name, add=None, add_spec=None, add_scale=1.0):
    dot = {"nn": _dot, "nt": _dot_nt, "tn": _dot_tn}[mode]

    def body(*refs):
        if add is None:
            a_ref, b_ref, o_ref = refs
            acc = dot(a_ref[...].astype(BF16), b_ref[...].astype(BF16))
        else:
            a_ref, b_ref, add_ref, o_ref = refs
            acc = dot(a_ref[...].astype(BF16), b_ref[...].astype(BF16)) + add_scale * add_ref[...]
        o_ref[...] = acc.astype(o_ref.dtype)

    ops = (a, b) if add is None else (a, b, add)
    specs = [a_spec, b_spec] if add is None else [a_spec, b_spec, add_spec]
    return pl.pallas_call(body, grid=grid, in_specs=specs, out_specs=out_spec, out_shape=out_shape,
                          compiler_params=_cp(*(("parallel",) * len(grid))), name=name)(*ops)


def _ln_stats(z):
    mu = jnp.mean(z, axis=-1, keepdims=True)
    zc = z - mu
    var = jnp.mean(zc * zc, axis=-1, keepdims=True)
    rstd = lax.rsqrt(var + LN_EPS)
    return zc * rstd, rstd


def _row_tile(t):
    return min(512, t)


def _resid_ln(x, mix, g3, b3, l, name):
    t = x.shape[0]
    bm = _row_tile(t)

    def body(x_ref, m_ref, g_ref, b_ref, z_ref, y_ref, yb_ref):
        z = ALPHA * x_ref[...] + m_ref[...]
        xh, _ = _ln_stats(z)
        y = xh * g_ref[...] + b_ref[...]
        z_ref[...] = z
        y_ref[...] = y
        yb_ref[...] = y.astype(BF16)

    row = pl.BlockSpec((bm, D), lambda i: (i, 0))
    vec = pl.BlockSpec((None, 1, D), lambda i: (l, 0, 0))
    return pl.pallas_call(body, grid=(t // bm,), in_specs=[row, row, vec, vec], out_specs=[row, row, row],
                          out_shape=[S((t, D), F32), S((t, D), F32), S((t, D), BF16)],
                          compiler_params=_cp("parallel"), name=name)(x, mix, g3, b3)


def _ln_bwd(d, z, g3, l, name, r=None, dep=None):
    t = z.shape[0]
    bm = _row_tile(t)

    def body(*refs):
        refs = list(refs)
        d_ref = refs.pop(0)
        dy = d_ref[...]
        if r is not None:
            dy = dy + ALPHA * refs.pop(0)[...]
        z_ref, g_ref = refs.pop(0), refs.pop(0)
        if dep is not None:
            refs.pop(0)
        dz_ref, dzb_ref, dg_ref, db_ref = refs
        xh, rstd = _ln_stats(z_ref[...])
        dyg = dy * g_ref[...]
        m1 = jnp.mean(dyg, axis=-1, keepdims=True)
        m2 = jnp.mean(dyg * xh, axis=-1, keepdims=True)
        dz = rstd * (dyg - m1 - xh * m2)
        dz_ref[...] = dz
        dzb_ref[...] = dz.astype(BF16)

        @pl.when(pl.program_id(0) == 0)
        def _():
            dg_ref[...] = jnp.zeros_like(dg_ref)
            db_ref[...] = jnp.zeros_like(db_ref)

        dg_ref[...] += jnp.sum(dy * xh, axis=0, keepdims=True)
        db_ref[...] += jnp.sum(dy, axis=0, keepdims=True)

    row = pl.BlockSpec((bm, D), lambda i: (i, 0))
    vec = pl.BlockSpec((None, 1, D), lambda i: (l, 0, 0))
    acc = pl.BlockSpec((1, D), lambda i: (0, 0))
    ops = [d, z, g3] if r is None else [d, r, z, g3]
    specs = [row, row, vec] if r is None else [row, row, row, vec]
    if dep is not None:
        ops.append(dep)
        specs.append(_full(dep.shape))
    return pl.pallas_call(body, grid=(t // bm,), in_specs=specs, out_specs=[row, row, acc, acc],
                          out_shape=[S((t, D), F32), S((t, D), BF16), S((1, D), F32), S((1, D), F32)],
                          compiler_params=_cp("arbitrary"), name=name)(*ops)


def _loss_grad(y, tgt):
    t = y.shape[0]
    bm = _row_tile(t)

    def body(y_ref, t_ref, dy_ref, loss_ref, acc_ref):
        i = pl.program_id(0)
        e = y_ref[...] - t_ref[...]
        dy_ref[...] = e * (1.0 / D)

        @pl.when(i == 0)
        def _():
            acc_ref[...] = jnp.zeros_like(acc_ref)

        acc_ref[...] += jnp.sum(e * e, axis=0, keepdims=True)

        @pl.when(i == pl.num_programs(0) - 1)
        def _():
            loss_ref[...] = jnp.full(loss_ref.shape, (0.5 / D) * jnp.sum(acc_ref[...]), F32)

    row = pl.BlockSpec((bm, D), lambda i: (i, 0))
    return pl.pallas_call(body, grid=(t // bm,), in_specs=[row, row],
                          out_specs=[row, pl.BlockSpec((1, 128), lambda i: (0, 0))],
                          out_shape=[S((t, D), F32), S((1, 128), F32)],
                          scratch_shapes=[pltpu.VMEM((1, D), F32)],
                          compiler_params=_cp("arbitrary"), name="loss_grad")(y, tgt)


def _mlp_row_tile(t):
    return min(1024, t)


def _mlp_fwd(yb, w1g, w2g):
    t = yb.shape[0]
    bm = _mlp_row_tile(t)

    def body(y_ref, w1_ref, w2_ref, o_ref):
        j = pl.program_id(1)
        h = jnp.maximum(_dot(y_ref[...], w1_ref[...]), 0.0)
        c = _dot((h * h).astype(BF16), w2_ref[...])

        @pl.when(j == 0)
        def _():
            o_ref[...] = c

        @pl.when(j > 0)
        def _():
            o_ref[...] += c

    return pl.pallas_call(
        body, grid=(t // bm, N_DEV),
        in_specs=[pl.BlockSpec((bm, D), lambda i, j: (i, 0)),
                  pl.BlockSpec((None, D, FF_BLK), lambda i, j: (j, 0, 0)),
                  pl.BlockSpec((None, FF_BLK, D), lambda i, j: (j, 0, 0))],
        out_specs=pl.BlockSpec((bm, D), lambda i, j: (i, 0)),
        out_shape=S((t, D), F32), compiler_params=_cp("parallel", "arbitrary"), name="mlp_fwd")(yb, w1g, w2g)


def _mlp_bwd_dh(yb, dzb, w1g, w2g):
    t = yb.shape[0]
    bm = _mlp_row_tile(t)

    def body(y_ref, dz_ref, w1_ref, w2_ref, a_ref, dh_ref, acc_ref):
        j = pl.program_id(1)
        r = jnp.maximum(_dot(y_ref[...], w1_ref[...]), 0.0)
        a_ref[...] = (r * r).astype(BF16)
        da = _dot_nt(dz_ref[...], w2_ref[...])
        dh = (da * (2.0 * r)).astype(BF16)
        dh_ref[...] = dh
        c = _dot_nt(dh, w1_ref[...])

        @pl.when(j == 0)
        def _():
            acc_ref[...] = c

        @pl.when(j > 0)
        def _():
            acc_ref[...] += c

    row = pl.BlockSpec((bm, D), lambda i, j: (i, 0))
    hid = pl.BlockSpec((bm, FF_BLK), lambda i, j: (i, j))
    return pl.pallas_call(
        body, grid=(t // bm, N_DEV),
        in_specs=[row, row,
                  pl.BlockSpec((None, D, FF_BLK), lambda i, j: (j, 0, 0)),
                  pl.BlockSpec((None, FF_BLK, D), lambda i, j: (j, 0, 0))],
        out_specs=[hid, hid, row],
        out_shape=[S((t, D_FF), BF16), S((t, D_FF), BF16), S((t, D), F32)],
        compiler_params=_cp("parallel", "arbitrary"), name="mlp_bwd_dh")(yb, dzb, w1g, w2g)


def _shift_dn(x, k, rows, fill=0.0):
    return jnp.where(rows >= k, pltpu.roll(x, k, 0), fill)


def _shift_up(x, k, rows, fill=0.0):
    t = x.shape[0]
    return jnp.where(rows < t - k, pltpu.roll(x, t - k, 0), fill)


def _scan_dn(a, b, rows):
    k = 1
    t = a.shape[0]
    while k < t:
        b = a * _shift_dn(b, k, rows) + b
        if 2 * k < t:
            a = a * _shift_dn(a, k, rows, 1.0)
        k *= 2
    return b


def _scan_up(a, b, rows):
    k = 1
    t = a.shape[0]
    while k < t:
        b = a * _shift_up(b, k, rows) + b
        if 2 * k < t:
            a = a * _shift_up(a, k, rows, 1.0)
        k *= 2
    return b


def _window_sum_dn(x, w, rows):
    k = 1
    while k < w:
        x = x + _shift_dn(x, k, rows)
        k *= 2
    return x


def _window_sum_up(x, w, rows):
    k = 1
    while k < w:
        x = x + _shift_up(x, k, rows)
        k *= 2
    return x


def _pool_diff(u, w, rows):
    inv_count = 1.0 / jnp.minimum(rows + 1, w).astype(F32)
    return _window_sum_dn(u, w, rows) * inv_count - u, inv_count


def _pool_fwd(proj, pool_w, pool_scale3, j):
    t = proj.shape[0]

    def body(u_ref, w_ref, s_ref, y_ref):
        rows = lax.broadcasted_iota(jnp.int32, (t, HEAD), 0)
        for g, w in enumerate(POOL_WINDOWS):
            cols = slice(g * HEAD, (g + 1) * HEAD)
            d, _ = _pool_diff(u_ref[:, cols], w, rows)
            y = _dot(d.astype(BF16), w_ref[g].astype(BF16)) * s_ref[:, cols]
            y_ref[:, cols] = y.astype(BF16)

    return pl.pallas_call(
        body, grid=(1,),
        in_specs=[pl.BlockSpec((t, POOL_W), lambda i: (0, 0)),
                  pl.BlockSpec((None, 4, HEAD, HEAD), lambda i: (j, 0, 0, 0)),
                  pl.BlockSpec((None, 1, POOL_W), lambda i: (j, 0, 0))],
        out_specs=pl.BlockSpec((t, POOL_W), lambda i: (0, 0)),
        out_shape=S((t, POOL_W), BF16), compiler_params=_cp("arbitrary"), name="pool_fwd")(proj, pool_w, pool_scale3)


def _pool_bwd(proj, dycat, pool_w, pool_scale3, j):
    t = proj.shape[0]

    def body(u_ref, dy_ref, w_ref, s_ref, du_ref, dw_ref, ds_ref):
        rows = lax.broadcasted_iota(jnp.int32, (t, HEAD), 0)
        for g, w in enumerate(POOL_WINDOWS):
            cols = slice(g * HEAD, (g + 1) * HEAD)
            d, inv_count = _pool_diff(u_ref[:, cols], w, rows)
            db = d.astype(BF16)
            wg = w_ref[g].astype(BF16)
            dy = dy_ref[:, cols]
            ds_ref[:, cols] = jnp.sum(dy * _dot(db, wg), axis=0, keepdims=True)
            dzz = (dy * s_ref[:, cols]).astype(BF16)
            dw_ref[g] = _dot_tn(db, dzz)
            dd = _dot_nt(dzz, wg)
            du_ref[:, cols] = (_window_sum_up(dd * inv_count, w, rows) - dd).astype(BF16)

    return pl.pallas_call(
        body, grid=(1,),
        in_specs=[pl.BlockSpec((t, POOL_W), lambda i: (0, 0)),
                  pl.BlockSpec((t, POOL_W), lambda i: (0, 0)),
                  pl.BlockSpec((None, 4, HEAD, HEAD), lambda i: (j, 0, 0, 0)),
                  pl.BlockSpec((None, 1, POOL_W), lambda i: (j, 0, 0))],
        out_specs=[pl.BlockSpec((t, POOL_W), lambda i: (0, 0)), _full((4, HEAD, HEAD)), _full((1, POOL_W))],
        out_shape=[S((t, POOL_W), BF16), S((4, HEAD, HEAD), F32), S((1, POOL_W), F32)],
        compiler_params=_cp("arbitrary"), name="pool_bwd")(proj, dycat, pool_w, pool_scale3)


GELU_C = 0.7978845608028654
GELU_K = 0.044715


def _gelu(x):
    th = jnp.tanh(GELU_C * (x + GELU_K * x * x * x))
    return 0.5 * x * (1.0 + th), th


def _lru_forward(u, gate, cw, cb, wa, ba, wx, bx, lam, rows):
    v = cw[3:4] * u + cw[2:3] * _shift_dn(u, 1, rows) + cw[1:2] * _shift_dn(u, 2, rows) \
        + cw[0:1] * _shift_dn(u, 3, rows) + cb
    vb = v.astype(BF16)
    r = jax.nn.sigmoid(_dot(vb, wa) + ba)
    i = jax.nn.sigmoid(_dot(vb, wx) + bx)
    sp = jnp.maximum(-lam, 0.0) + jnp.log1p(jnp.exp(-jnp.abs(lam)))
    log_a = (-LRU_C) * r * sp
    a = jnp.exp(log_a)
    one_m_a2 = -jnp.tanh(log_a) * (a * a + 1.0)
    mult = jnp.sqrt(one_m_a2)
    h = _scan_dn(a, mult * (i * v), rows)
    gl, th = _gelu(gate)
    return dict(v=v, vb=vb, r=r, i=i, sp=sp, a=a, mult=mult, h=h, gl=gl, th=th)


def _lru_specs(t, j, col0_u, col0_g):
    blk = lambda c0: pl.BlockSpec((t, HEAD), lambda h: (0, c0 + h))
    vec = pl.BlockSpec((None, 1, HEAD), lambda h: (j, 0, h))
    return [blk(col0_u), blk(col0_g),
            pl.BlockSpec((None, 4, HEAD), lambda h: (j, 0, h)), vec,
            pl.BlockSpec((None, None, HEAD, HEAD), lambda h: (j, h, 0, 0)), vec,
            pl.BlockSpec((None, None, HEAD, HEAD), lambda h: (j, h, 0, 0)), vec, vec]


def _lru_fwd(proj, p, j):
    t = proj.shape[0]

    def body(u_ref, g_ref, cw_ref, cb_ref, wa_ref, ba_ref, wx_ref, bx_ref, lam_ref, y_ref):
        rows = lax.broadcasted_iota(jnp.int32, (t, HEAD), 0)
        f = _lru_forward(u_ref[...], g_ref[...], cw_ref[...], cb_ref[...], wa_ref[...].astype(BF16), ba_ref[...],
                         wx_ref[...].astype(BF16), bx_ref[...], lam_ref[...], rows)
        y_ref[...] = (f["h"] * f["gl"]).astype(BF16)

    return pl.pallas_call(
        body, grid=(LRU_HEADS,), in_specs=_lru_specs(t, j, POOL_W // HEAD, (POOL_W + LRU_W) // HEAD),
        out_specs=pl.BlockSpec((t, HEAD), lambda h: (0, h)), out_shape=S((t, LRU_W), BF16),
        compiler_params=_cp("parallel"), name="lru_fwd")(
            proj, proj, p["conv_w"], p["conv_b"], p["w_a"], p["b_a"], p["w_x"], p["b_x"], p["lam"])


def _lru_bwd(proj, dycat, p, j):
    t = proj.shape[0]

    def body(u_ref, g_ref, cw_ref, cb_ref, wa_ref, ba_ref, wx_ref, bx_ref, lam_ref, dy_ref,
             du_ref, dgate_ref, dcw_ref, dcb_ref, dwa_ref, dba_ref, dwx_ref, dbx_ref, dlam_ref):
        rows = lax.broadcasted_iota(jnp.int32, (t, HEAD), 0)
        u = u_ref[...]
        gate = g_ref[...]
        cw = cw_ref[...]
        wa = wa_ref[...].astype(BF16)
        wx = wx_ref[...].astype(BF16)
        lam = lam_ref[...]
        f = _lru_forward(u, gate, cw, cb_ref[...], wa, ba_ref[...], wx, bx_ref[...], lam, rows)
        v, r, i, a, mult, h, th = f["v"], f["r"], f["i"], f["a"], f["mult"], f["h"], f["th"]
        dy = dy_ref[...]
        dgl = 0.5 * (1.0 + th) + 0.5 * gate * (1.0 - th * th) * GELU_C * (1.0 + 3.0 * GELU_K * gate * gate)
        dgate_ref[...] = (dy * h * dgl).astype(BF16)
        g = _scan_up(_shift_up(a, 1, rows), dy * f["gl"], rows)
        da = g * _shift_dn(h, 1, rows)
        iv = i * v
        dmult = g * iv
        di = g * mult * v
        dv = g * mult * i
        dlog_a = da * a - dmult * (a * a) / mult
        dr = dlog_a * (-LRU_C) * f["sp"]
        dsp = jnp.sum(dlog_a * (-LRU_C) * r, axis=0, keepdims=True)
        dlam_ref[...] = -dsp * jax.nn.sigmoid(-lam)
        dpa = dr * r * (1.0 - r)
        dpx = di * i * (1.0 - i)
        dpab = dpa.astype(BF16)
        dpxb = dpx.astype(BF16)
        dwa_ref[...] = _dot_tn(f["vb"], dpab)
        dwx_ref[...] = _dot_tn(f["vb"], dpxb)
        dba_ref[...] = jnp.sum(dpa, axis=0, keepdims=True)
        dbx_ref[...] = jnp.sum(dpx, axis=0, keepdims=True)
        dv = dv + _dot_nt(dpab, wa) + _dot_nt(dpxb, wx)
        dcb_ref[...] = jnp.sum(dv, axis=0, keepdims=True)
        du = cw[3:4] * dv
        dcw_ref[3:4, :] = jnp.sum(dv * u, axis=0, keepdims=True)
        for k in (1, 2, 3):
            du = du + cw[3 - k:4 - k] * _shift_up(dv, k, rows)
            dcw_ref[3 - k:4 - k, :] = jnp.sum(dv * _shift_dn(u, k, rows), axis=0, keepdims=True)
        du_ref[...] = du.astype(BF16)

    blk = pl.BlockSpec((t, HEAD), lambda h: (0, h))
    vec = pl.BlockSpec((1, HEAD), lambda h: (0, h))
    mat = pl.BlockSpec((None, HEAD, HEAD), lambda h: (h, 0, 0))
    return pl.pallas_call(
        body, grid=(LRU_HEADS,),
        in_specs=_lru_specs(t, j, POOL_W // HEAD, (POOL_W + LRU_W) // HEAD)
        + [pl.BlockSpec((t, HEAD), lambda h: (0, POOL_W // HEAD + h))],
        out_specs=[blk, blk, pl.BlockSpec((4, HEAD), lambda h: (0, h)), vec, mat, vec, mat, vec, vec],
        out_shape=[S((t, LRU_W), BF16), S((t, LRU_W), BF16), S((4, LRU_W), F32), S((1, LRU_W), F32),
                   S((LRU_HEADS, HEAD, HEAD), F32), S((1, LRU_W), F32),
                   S((LRU_HEADS, HEAD, HEAD), F32), S((1, LRU_W), F32), S((1, LRU_W), F32)],
        compiler_params=_cp("parallel"), name="lru_bwd")(
            proj, proj, p["conv_w"], p["conv_b"], p["w_a"], p["b_a"], p["w_x"], p["b_x"], p["lam"], dycat)


def _rope(x, c, s):
    x1 = x[:, :ROPE // 2]
    x2 = x[:, ROPE // 2:]
    return jnp.concatenate([x1 * c - x2 * s, x1 * s + x2 * c], axis=-1)


def _rope_t(d, c, s):
    d1 = d[:, :ROPE // 2]
    d2 = d[:, ROPE // 2:]
    return jnp.concatenate([d1 * c + d2 * s, d2 * c - d1 * s], axis=-1)


def _rope_tables(pos2, inv_freq):
    t = pos2.shape[0]

    def body(p_ref, f_ref, c_ref, s_ref):
        ang = p_ref[...].astype(F32) * f_ref[...]
        c_ref[...] = jnp.cos(ang)
        s_ref[...] = jnp.sin(ang)

    return pl.pallas_call(body, out_shape=[S((t, ROPE // 2), F32), S((t, ROPE // 2), F32)],
                          name="rope_tables")(pos2, inv_freq)


def _down_norm(xb, wdown_g, gq3, gkv3, cos, sin, j):
    t = xb.shape[0]
    bm = _row_tile(t)

    def body(x_ref, w_ref, gq_ref, gkv_ref, c_ref, s_ref, down_ref, cq_ref, ckv_ref, kpe_ref):
        w = w_ref[...].reshape(D, ODD_IN)
        down = _dot(x_ref[...], w)
        down_ref[...] = down
        q = down[:, :Q_RANK]
        cq_ref[...] = (q * lax.rsqrt(jnp.mean(q * q, axis=-1, keepdims=True) + RMS_EPS) * gq_ref[...]).astype(BF16)
        kv = down[:, Q_RANK:Q_RANK + KV_RANK]
        ckv_ref[...] = (kv * lax.rsqrt(jnp.mean(kv * kv, axis=-1, keepdims=True) + RMS_EPS)
                        * gkv_ref[...]).astype(BF16)
        kpe_ref[...] = _rope(down[:, Q_RANK + KV_RANK:], c_ref[...], s_ref[...])

    row = lambda n: pl.BlockSpec((bm, n), lambda i: (i, 0))
    return pl.pallas_call(
        body, grid=(t // bm,),
        in_specs=[row(D), _full((N_DEV, D // N_DEV, ODD_IN)),
                  pl.BlockSpec((None, 1, Q_RANK), lambda i: (j, 0, 0)),
                  pl.BlockSpec((None, 1, KV_RANK), lambda i: (j, 0, 0)), row(ROPE // 2), row(ROPE // 2)],
        out_specs=[row(ODD_IN), row(Q_RANK), row(KV_RANK), row(ROPE)],
        out_shape=[S((t, ODD_IN), F32), S((t, Q_RANK), BF16), S((t, KV_RANK), BF16), S((t, ROPE), F32)],
        compiler_params=_cp("parallel"), name="down_norm")(xb, wdown_g, gq3, gkv3, cos, sin)


def _q_tile(t):
    return min(256, t // 2)


def _attn_probs(qn, qp, kn, kp, qs):
    s = (_dot_nt(qn, kn) + _dot_nt(qp, kp)) * ATT_SCALE
    rows = qs + lax.broadcasted_iota(jnp.int32, s.shape, 0)
    cols = lax.broadcasted_iota(jnp.int32, s.shape, 1)
    s = jnp.where(jnp.right_shift(cols, CHUNK_SHIFT) <= jnp.right_shift(rows, CHUNK_SHIFT), s, NEG)
    e = jnp.exp(s - jnp.max(s, axis=-1, keepdims=True))
    return e / jnp.sum(e, axis=-1, keepdims=True)


def _head_qkv(cq, ckv, kpe, c, s, wq_ref, wkv_ref):
    qn = _dot(cq, wq_ref[:, :NOPE]).astype(BF16)
    qp = _rope(_dot(cq, wq_ref[:, NOPE:]), c, s).astype(BF16)
    kn = _dot(ckv, wkv_ref[:, :NOPE]).astype(BF16)
    vv = _dot(ckv, wkv_ref[:, NOPE:]).astype(BF16)
    return qn, qp, kn, kpe.astype(BF16), vv


def _attn_in_specs(t):
    return [_full((t, Q_RANK)), _full((t, KV_RANK)), _full((t, ROPE)), _full((t, ROPE // 2)), _full((t, ROPE // 2)),
            pl.BlockSpec((None, Q_RANK, NOPE + ROPE), lambda h: (h, 0, 0)),
            pl.BlockSpec((None, KV_RANK, NOPE + VDIM), lambda h: (h, 0, 0)),
            pl.BlockSpec((None, VDIM, D), lambda h: (h, 0, 0))]


def _attn_fwd(cq, ckv, kpe, cos, sin, wqb_g, wkvb_g, wo_g):
    t = cq.shape[0]
    tq = _q_tile(t)

    def body(cq_ref, ckv_ref, kpe_ref, c_ref, s_ref, wq_ref, wkv_ref, wo_ref, o_ref, mix_ref):
        qn, qp, kn, kp, vv = _head_qkv(cq_ref[...], ckv_ref[...], kpe_ref[...], c_ref[...], s_ref[...],
                                       wq_ref, wkv_ref)
        for qs in range(0, t, tq):
            ke = qs + tq
            p = _attn_probs(qn[qs:ke], qp[qs:ke], kn[:ke], kp[:ke], qs)
            o_ref[qs:ke, :] = _dot(p.astype(BF16), vv[:ke]).astype(BF16)
        c = _dot(o_ref[...], wo_ref[...])

        @pl.when(pl.program_id(0) == 0)
        def _():
            mix_ref[...] = c

        @pl.when(pl.program_id(0) > 0)
        def _():
            mix_ref[...] += c

    return pl.pallas_call(
        body, grid=(MLA_HEADS,), in_specs=_attn_in_specs(t),
        out_specs=[pl.BlockSpec((None, t, VDIM), lambda h: (h, 0, 0)), _full((t, D))],
        out_shape=[S((MLA_HEADS, t, VDIM), BF16), S((t, D), F32)],
        compiler_params=_cp("arbitrary"), name="attn_fwd")(cq, ckv, kpe, cos, sin, wqb_g, wkvb_g, wo_g)


def _attn_bwd(cq, ckv, kpe, cos, sin, wqb_g, wkvb_g, wo_g, o, dzb):
    t = cq.shape[0]
    tq = _q_tile(t)

    def body(cq_ref, ckv_ref, kpe_ref, c_ref, s_ref, wq_ref, wkv_ref, wo_ref, o_ref, dz_ref,
             dwo_ref, dwq_ref, dwkv_ref, dcq_ref, dckv_ref, dkpe_ref, dkn_s, dkp_s, dv_s, dqn_s, dqp_s):
        cqv = cq_ref[...]
        ckvv = ckv_ref[...]
        c = c_ref[...]
        s = s_ref[...]
        qn, qp, kn, kp, vv = _head_qkv(cqv, ckvv, kpe_ref[...], c, s, wq_ref, wkv_ref)
        dzv = dz_ref[...]
        dwo_ref[...] = _dot_tn(o_ref[...], dzv).astype(BF16)
        do = _dot_nt(dzv, wo_ref[...]).astype(BF16)
        dkn_s[...] = jnp.zeros_like(dkn_s)
        dkp_s[...] = jnp.zeros_like(dkp_s)
        dv_s[...] = jnp.zeros_like(dv_s)
        for qs in range(0, t, tq):
            ke = qs + tq
            p = _attn_probs(qn[qs:ke], qp[qs:ke], kn[:ke], kp[:ke], qs)
            dp = _dot_nt(do[qs:ke], vv[:ke])
            ds = (p * (dp - jnp.sum(p * dp, axis=-1, keepdims=True)) * ATT_SCALE).astype(BF16)
            dqn_s[qs:ke, :] = _dot(ds, kn[:ke])
            dqp_s[qs:ke, :] = _dot(ds, kp[:ke])
            dkn_s[0:ke, :] += _dot_tn(ds, qn[qs:ke])
            dkp_s[0:ke, :] += _dot_tn(ds, qp[qs:ke])
            dv_s[0:ke, :] += _dot_tn(p.astype(BF16), do[qs:ke])
        dqn = dqn_s[...].astype(BF16)
        dqp = _rope_t(dqp_s[...], c, s).astype(BF16)
        dkn = dkn_s[...].astype(BF16)
        dvv = dv_s[...].astype(BF16)
        dwq_ref[:, :NOPE] = _dot_tn(cqv, dqn).astype(BF16)
        dwq_ref[:, NOPE:] = _dot_tn(cqv, dqp).astype(BF16)
        dwkv_ref[:, :NOPE] = _dot_tn(ckvv, dkn).astype(BF16)
        dwkv_ref[:, NOPE:] = _dot_tn(ckvv, dvv).astype(BF16)
        dcq = _dot_nt(dqn, wq_ref[:, :NOPE]) + _dot_nt(dqp, wq_ref[:, NOPE:])
        dckv = _dot_nt(dkn, wkv_ref[:, :NOPE]) + _dot_nt(dvv, wkv_ref[:, NOPE:])

        @pl.when(pl.program_id(0) == 0)
        def _():
            dcq_ref[...] = dcq
            dckv_ref[...] = dckv
            dkpe_ref[...] = dkp_s[...]

        @pl.when(pl.program_id(0) > 0)
        def _():
            dcq_ref[...] += dcq
            dckv_ref[...] += dckv
            dkpe_ref[...] += dkp_s[...]

    per_head = lambda a, b: pl.BlockSpec((None, a, b), lambda h: (h, 0, 0))
    return pl.pallas_call(
        body, grid=(MLA_HEADS,),
        in_specs=_attn_in_specs(t) + [per_head(t, VDIM), _full((t, D))],
        out_specs=[per_head(VDIM, D), per_head(Q_RANK, NOPE + ROPE), per_head(KV_RANK, NOPE + VDIM),
                   _full((t, Q_RANK)), _full((t, KV_RANK)), _full((t, ROPE))],
        out_shape=[S((MLA_HEADS, VDIM, D), BF16), S((MLA_HEADS, Q_RANK, NOPE + ROPE), BF16),
                   S((MLA_HEADS, KV_RANK, NOPE + VDIM), BF16),
                   S((t, Q_RANK), F32), S((t, KV_RANK), F32), S((t, ROPE), F32)],
        scratch_shapes=[pltpu.VMEM((t, NOPE), F32), pltpu.VMEM((t, ROPE), F32), pltpu.VMEM((t, VDIM), F32),
                        pltpu.VMEM((t, NOPE), F32), pltpu.VMEM((t, ROPE), F32)],
        compiler_params=_cp("arbitrary"), name="attn_bwd")(cq, ckv, kpe, cos, sin, wqb_g, wkvb_g, wo_g, o, dzb)


def _rms_bwd(down, dcq, dckv, dkpe, cos, sin, gq3, gkv3, j):
    t = down.shape[0]
    bm = _row_tile(t)

    def body(down_ref, dcq_ref, dckv_ref, dkpe_ref, c_ref, s_ref, gq_ref, gkv_ref, dd_ref, dgq_ref, dgkv_ref):
        @pl.when(pl.program_id(0) == 0)
        def _():
            dgq_ref[...] = jnp.zeros_like(dgq_ref)
            dgkv_ref[...] = jnp.zeros_like(dgkv_ref)

        def rms_b(x, dy, g):
            rstd = lax.rsqrt(jnp.mean(x * x, axis=-1, keepdims=True) + RMS_EPS)
            xh = x * rstd
            dyg = dy * g
            return rstd * (dyg - xh * jnp.mean(dyg * xh, axis=-1, keepdims=True)), jnp.sum(dy * xh, axis=0, keepdims=True)

        dq, dgq = rms_b(down_ref[:, :Q_RANK], dcq_ref[...], gq_ref[...])
        dkv, dgkv = rms_b(down_ref[:, Q_RANK:Q_RANK + KV_RANK], dckv_ref[...], gkv_ref[...])
        dgq_ref[...] += dgq
        dgkv_ref[...] += dgkv
        dd_ref[:, :Q_RANK] = dq.astype(BF16)
        dd_ref[:, Q_RANK:Q_RANK + KV_RANK] = dkv.astype(BF16)
        dd_ref[:, Q_RANK + KV_RANK:] = _rope_t(dkpe_ref[...], c_ref[...], s_ref[...]).astype(BF16)

    row = lambda n: pl.BlockSpec((bm, n), lambda i: (i, 0))
    return pl.pallas_call(
        body, grid=(t // bm,),
        in_specs=[row(ODD_IN), row(Q_RANK), row(KV_RANK), row(ROPE), row(ROPE // 2), row(ROPE // 2),
                  pl.BlockSpec((None, 1, Q_RANK), lambda i: (j, 0, 0)),
                  pl.BlockSpec((None, 1, KV_RANK), lambda i: (j, 0, 0))],
        out_specs=[row(ODD_IN), _full((1, Q_RANK)), _full((1, KV_RANK))],
        out_shape=[S((t, ODD_IN), BF16), S((1, Q_RANK), F32), S((1, KV_RANK), F32)],
        compiler_params=_cp("arbitrary"), name="rms_bwd")(down, dcq, dckv, dkpe, cos, sin, gq3, gkv3)


def _col_blocks(t, n, bn):
    return pl.BlockSpec((t, bn), lambda i: (0, i))


def _row_blocks(n, bm):
    return pl.BlockSpec((bm, n), lambda i: (i, 0))


def _local_step(x, pos2, tgt, small, weights_of, grads_done):
    t = x.shape[0]
    bm = _row_tile(t)
    inv_freq = (ROPE_THETA ** (-jnp.arange(0, ROPE, 2, dtype=F32) / ROPE)).reshape(1, ROPE // 2)
    cos, sin = _rope_tables(pos2, inv_freq)
    lru_p = {k: small[k] for k in ("conv_w", "conv_b", "w_a", "b_a", "w_x", "b_x", "lam")}

    saved = []
    y, yb = x, x.astype(BF16)
    for l in range(DEPTH):
        j = l // 2
        big = weights_of(l, 0, y)
        sv = dict(xb=yb, big=big)
        if l % 2 == 0:
            proj = _mm(yb, big["win2d"], mode="nn", grid=(EVEN_IN // 512,), a_spec=_full((t, D)),
                       b_spec=_col_blocks(D, EVEN_IN, 512), out_shape=S((t, EVEN_IN), F32),
                       out_spec=_col_blocks(t, EVEN_IN, 512), name="even_proj")
            ycat = jnp.concatenate([_pool_fwd(proj, small["pool_w"], small["pool_scale"], j),
                                    _lru_fwd(proj, lru_p, j)], axis=1)
            big.update(weights_of(l, 1, ycat))
            mix = _mm(ycat, big["wout2d"], mode="nn", grid=(D // 512,), a_spec=_full((t, EVEN_MIX)),
                      b_spec=_col_blocks(EVEN_MIX, D, 512), out_shape=S((t, D), F32),
                      out_spec=_col_blocks(t, D, 512), name="even_out")
            sv.update(proj=proj, ycat=ycat)
        else:
            down, cq, ckv, kpe = _down_norm(yb, big["wdown"], small["gq"], small["gkv"], cos, sin, j)
            o, mix = _attn_fwd(cq, ckv, kpe, cos, sin, big["wqb"], big["wkvb"], big["wo"])
            sv.update(down=down, cq=cq, ckv=ckv, kpe=kpe, o=o)
        z1, y1, y1b = _resid_ln(y, mix, small["ln_mix_g"], small["ln_mix_b"], l, "resid_ln")
        ff = _mlp_fwd(y1b, big["w1"], big["w2"])
        z2, y, yb = _resid_ln(y1, ff, small["ln_ffn_g"], small["ln_ffn_b"], l, "resid_ln")
        sv.update(z1=z1, y1b=y1b, z2=z2)
        saved.append(sv)

    dy, loss_tile = _loss_grad(y, tgt)

    g = {k: [None] * n for k, n in (("ln_mix_g", 4), ("ln_mix_b", 4), ("ln_ffn_g", 4), ("ln_ffn_b", 4),
                                    ("pool_w", 2), ("pool_scale", 2), ("conv_w", 2), ("conv_b", 2),
                                    ("w_a", 2), ("b_a", 2), ("w_x", 2), ("b_x", 2), ("lam", 2),
                                    ("gq", 2), ("gkv", 2))}
    dep = None
    for l in reversed(range(DEPTH)):
        j = l // 2
        sv = saved[l]
        big = sv["big"]
        dz2, dz2b, g["ln_ffn_g"][l], g["ln_ffn_b"][l] = _ln_bwd(dy, sv["z2"], small["ln_ffn_g"], l, "ln_bwd", dep=dep)
        act, dh, dff = _mlp_bwd_dh(sv["y1b"], dz2b, big["w1"], big["w2"])
        dw1 = _mm(sv["y1b"], dh, mode="tn", grid=(N_DEV,), a_spec=_full((t, D)),
                  b_spec=_col_blocks(t, D_FF, FF_BLK), out_shape=S((N_DEV, D, FF_BLK), BF16),
                  out_spec=pl.BlockSpec((None, D, FF_BLK), lambda i: (i, 0, 0)), name="mlp_dw1")
        dw2 = _mm(act, dz2b, mode="tn", grid=(N_DEV,), a_spec=_col_blocks(t, D_FF, FF_BLK),
                  b_spec=_full((t, D)), out_shape=S((N_DEV, FF_BLK, D), BF16),
                  out_spec=pl.BlockSpec((None, FF_BLK, D), lambda i: (i, 0, 0)), name="mlp_dw2")
        dep = grads_done(l, dict(w1=dw1, w2=dw2))
        dz1, dz1b, g["ln_mix_g"][l], g["ln_mix_b"][l] = _ln_bwd(dff, sv["z1"], small["ln_mix_g"], l, "ln_bwd_res",
                                                                 r=dz2, dep=dep)
        if l % 2 == 0:
            wout = big["wout2d"]
            dycat = _mm(dz1b, wout, mode="nt", grid=(EVEN_MIX // 512,), a_spec=_full((t, D)),
                        b_spec=_row_blocks(D, 512), out_shape=S((t, EVEN_MIX), F32),
                        out_spec=_col_blocks(t, EVEN_MIX, 512), name="even_dycat")
            dwout = _mm(sv["ycat"], dz1b, mode="tn", grid=(EVEN_MIX // 512,), a_spec=_col_blocks(t, EVEN_MIX, 512),
                        b_spec=_full((t, D)), out_shape=S((EVEN_MIX, D), BF16), out_spec=_row_blocks(D, 512),
                        name="even_dwout")
            du_pool, g["pool_w"][j], g["pool_scale"][j] = _pool_bwd(sv["proj"], dycat, small["pool_w"],
                                                                   small["pool_scale"], j)
            (du_lru, du_gate, g["conv_w"][j], g["conv_b"][j], g["w_a"][j], g["b_a"][j], g["w_x"][j], g["b_x"][j],
             g["lam"][j]) = _lru_bwd(sv["proj"], dycat, lru_p, j)
            dproj = jnp.concatenate([du_pool, du_lru, du_gate], axis=1)
            dwin = _mm(sv["xb"], dproj, mode="tn", grid=(EVEN_IN // 512,), a_spec=_full((t, D)),
                       b_spec=_col_blocks(t, EVEN_IN, 512), out_shape=S((D, EVEN_IN), BF16),
                       out_spec=_col_blocks(D, EVEN_IN, 512), name="even_dwin")
            dep = grads_done(l, dict(win=dwin.reshape(D, N_DEV, EVEN_IN // N_DEV).transpose(1, 0, 2),
                                     wout=dwout.reshape(N_DEV, EVEN_MIX // N_DEV, D)))
            dy = _mm(dproj, big["win2d"], mode="nt", grid=(t // bm,), a_spec=_row_blocks(EVEN_IN, bm),
                     b_spec=_full((D, EVEN_IN)), out_shape=S((t, D), F32), out_spec=_row_blocks(D, bm),
                     add=dz1, add_spec=_row_blocks(D, bm), add_scale=ALPHA, name="even_dx")
        else:
            dwo, dwqb, dwkvb, dcq, dckv, dkpe = _attn_bwd(
                sv["cq"], sv["ckv"], sv["kpe"], cos, sin, big["wqb"], big["wkvb"], big["wo"], sv["o"], dz1b)
            ddown, g["gq"][j], g["gkv"][j] = _rms_bwd(sv["down"], dcq, dckv, dkpe, cos, sin, small["gq"],
                                                     small["gkv"], j)
            dwdown = _mm(sv["xb"], ddown, mode="tn", grid=(N_DEV,), a_spec=_col_blocks(t, D, D // N_DEV),
                         b_spec=_full((t, ODD_IN)), out_shape=S((N_DEV, D // N_DEV, ODD_IN), BF16),
                         out_spec=pl.BlockSpec((None, D // N_DEV, ODD_IN), lambda i: (i, 0, 0)),
                         name="odd_dwdown")
            dep = grads_done(l, dict(wdown=dwdown, wqb=dwqb, wkvb=dwkvb, wo=dwo))
            dy = _mm(ddown, big["wdown2d"], mode="nt", grid=(t // bm,), a_spec=_row_blocks(ODD_IN, bm),
                     b_spec=_full((D, ODD_IN)), out_shape=S((t, D), F32), out_spec=_row_blocks(D, bm),
                     add=dz1, add_spec=_row_blocks(D, bm), add_scale=ALPHA, name="odd_dx")
    return loss_tile[0, 0], dy, g


def _mesh_place():
    x, y, c = lax.axis_index("x"), lax.axis_index("y"), lax.axis_index("c")
    return x, y, c


def _peer(place, k):
    x, y, c = place
    return (1 - x if k & 4 else x, 1 - y if k & 2 else y, 1 - c if k & 1 else c)


def _index(place):
    x, y, c = place
    return 4 * x + 2 * y + c


ANY = pl.BlockSpec(memory_space=pl.ANY)


def _all_gather_big(zones):
    n = len(zones)

    def body(*refs):
        outs = refs[n:2 * n]
        send, recv = refs[2 * n:]
        x, y, c = _mesh_place()
        me, sibling = (x, y, c), (x, y, 1 - c)
        chips = [(1 - x, y), (x, 1 - y), (1 - x, 1 - y)]

        def copy(w, k, block, to):
            blk = outs[w].at[_index(block)]
            return pltpu.make_async_remote_copy(src_ref=blk, dst_ref=blk, send_sem=send.at[w, k], recv_sem=recv.at[w, k],
                                                device_id=to, device_id_type=MESH)

        first = []
        for w in range(n):
            first.append(copy(w, 0, me, sibling))
            first += [copy(w, 1 + j, me, (*chip, c)) for j, chip in enumerate(chips)]
        for cp in first:
            cp.start()
        passed = []
        for w in range(n):
            for j, chip in enumerate(chips):
                copy(w, 1 + j, (*chip, c), me).wait_recv()
                cp = copy(w, 4 + j, (*chip, c), sibling)
                cp.start()
                passed.append(cp)
        for w in range(n):
            copy(w, 0, sibling, me).wait_recv()
            for j, chip in enumerate(chips):
                copy(w, 4 + j, (*chip, 1 - c), me).wait_recv()
        for cp in first + passed:
            cp.wait_send()

    return pl.pallas_call(
        body, in_specs=[ANY] * n, out_specs=[ANY] * n, out_shape=[S(z.shape, z.dtype) for z in zones],
        input_output_aliases={i: i for i in range(n)},
        scratch_shapes=[pltpu.SemaphoreType.DMA((n, N_DEV - 1)), pltpu.SemaphoreType.DMA((n, N_DEV - 1))],
        compiler_params=pltpu.CompilerParams(has_side_effects=True), name="all_gather_big")(*zones)


def _shard_rows_tile(a):
    return max(d for d in range(16, 257, 16) if a % d == 0)


HBM = pl.BlockSpec(memory_space=pltpu.HBM)
SEM = pl.BlockSpec(memory_space=pltpu.SEMAPHORE)
DATAFLOW = pltpu.SideEffectType.DATAFLOW_SIDE_EFFECTING


def _in_hbm(a):
    return pltpu.with_memory_space_constraint(a, pltpu.HBM)


def _gather_ici_copies(place, src, land, w):
    me = _index(place)
    return [(_peer(place, k), land.at[me], land.at[me]) for k in (1, 2, 4, 6)]


def _gather_d2d_copies(place, src, land, w):
    blocks = [_index(_peer(place, k)) for k in (2, 4, 6)]
    return [(_peer(place, 1), land.at[b], land.at[b]) for b in blocks]


GATHER_ICI = (4, _gather_ici_copies)
GATHER_D2D = (3, _gather_d2d_copies)


def _scatter_plan(layers):
    def copies(place, src, land, w):
        me = _index(place)
        return [(_peer(place, k), src.at[_index(_peer(place, k))], land.at[me, layers[w]]) for k in range(1, N_DEV)]
    return (N_DEV - 1, copies)


def _exchange_start(srcs, lands, plan, name, after=()):
    ns, n = len(srcs), len(lands)
    n_in = ns + n + len(after)
    per, copies = plan

    def body(*refs):
        ins, land = refs[:ns], refs[ns:ns + n]
        send, recv = refs[n_in], refs[n_in + 1]
        token = refs[-1]
        place = _mesh_place()
        for i in range(per):
            for w in range(n):
                target, src, dst = copies(place, ins[w] if ns else None, land[w], w)[i]
                pltpu.make_async_remote_copy(src_ref=src, dst_ref=dst, send_sem=send.at[w * per + i],
                                             recv_sem=recv.at[w * per + i], device_id=target, device_id_type=MESH).start()
        token[...] = jnp.zeros_like(token)

    sems = pltpu.SemaphoreType.DMA((n * per,))
    thru = [pltpu.HBM(a.shape, a.dtype) for a in list(srcs) + list(lands)]
    out = pl.pallas_call(
        body, name=name, in_specs=[HBM] * (ns + n) + [ANY] * len(after),
        out_shape=(sems, sems, *thru, S((8, 128), F32)),
        out_specs=(SEM, SEM, *([HBM] * (ns + n)), pl.BlockSpec(memory_space=pltpu.VMEM)),
        input_output_aliases={i: 2 + i for i in range(ns + n)},
        compiler_params=pltpu.CompilerParams(has_side_effects=DATAFLOW),
    )(*[_in_hbm(a) for a in list(srcs) + list(lands)], *after)
    return out[0], out[1], list(out[2:2 + ns]), list(out[2 + ns:2 + ns + n]), out[-1]


def _exchange_wait(send, recv, srcs, lands, plan, after, name):
    ns, n = len(srcs), len(lands)
    per, copies = plan

    def body(*refs):
        ins, land = refs[:ns], refs[ns:ns + n]
        send_ref, recv_ref = refs[ns + n], refs[ns + n + 1]
        place = _mesh_place()
        for i in range(per):
            for w in range(n):
                target, src, dst = copies(place, ins[w] if ns else None, land[w], w)[i]
                cp = pltpu.make_async_remote_copy(src_ref=src, dst_ref=dst, send_sem=send_ref.at[w * per + i],
                                                  recv_sem=recv_ref.at[w * per + i], device_id=target,
                                                  device_id_type=MESH)
                cp.wait_send()
                cp.wait_recv()

    thru = [pltpu.HBM(a.shape, a.dtype) for a in list(srcs) + list(lands)]
    out = pl.pallas_call(
        body, name=name, in_specs=[HBM] * (ns + n) + [SEM, SEM, ANY],
        out_shape=tuple(thru), out_specs=tuple([HBM] * (ns + n)),
        input_output_aliases={i: i for i in range(ns + n)},
        compiler_params=pltpu.CompilerParams(has_side_effects=DATAFLOW),
    )(*srcs, *lands, send, recv, after)
    return list(out[:ns]), list(out[ns:])


def _all_reduce_small(part, name, dep=None):
    def body(*refs):
        p_ref = refs[0]
        o_ref, rbuf, send1, recv1, send2, recv2 = refs[-6:]
        place = _mesh_place()
        me = _index(place)
        rbuf[pl.ds(me, 1)] = p_ref[pl.ds(me, 1)]
        first = [pltpu.make_async_remote_copy(src_ref=p_ref.at[_index(_peer(place, k))], dst_ref=rbuf.at[me],
                                              send_sem=send1.at[k - 1], recv_sem=recv1.at[k - 1],
                                              device_id=_peer(place, k), device_id_type=MESH)
                 for k in range(1, N_DEV)]
        for cp in first:
            cp.start()
        for cp in first:
            cp.wait()
        acc = rbuf[0]
        for d in range(1, N_DEV):
            acc = acc + rbuf[d]
        o_ref[pl.ds(me, 1)] = acc[None]
        second = [pltpu.make_async_remote_copy(src_ref=o_ref.at[me], dst_ref=o_ref.at[me], send_sem=send2.at[k - 1],
                                               recv_sem=recv2.at[k - 1], device_id=_peer(place, k),
                                               device_id_type=MESH)
                  for k in range(1, N_DEV)]
        for cp in second:
            cp.start()
        for cp in second:
            cp.wait()

    vm = pl.BlockSpec(memory_space=pltpu.VMEM)
    ops = [part] if dep is None else [part, dep]
    return pl.pallas_call(
        body, in_specs=[vm] if dep is None else [vm, ANY], out_specs=vm, out_shape=S(part.shape, F32),
        scratch_shapes=[pltpu.VMEM(part.shape, F32)] + [pltpu.SemaphoreType.DMA((N_DEV - 1,))] * 4,
        compiler_params=pltpu.CompilerParams(has_side_effects=True, vmem_limit_bytes=VMEM_LIMIT), name=name)(*ops)


def _adamw(w, g, m, v):
    m = ADAM_B1 * m + (1.0 - ADAM_B1) * g
    v = ADAM_B2 * v + (1.0 - ADAM_B2) * (g * g)
    m_hat = m / (1.0 - ADAM_B1 ** ADAM_STEP)
    v_hat = v / (1.0 - ADAM_B2 ** ADAM_STEP)
    return -ADAM_LR * (m_hat / (jnp.sqrt(v_hat) + ADAM_EPS) + ADAM_WD * w), m, v


def _adam_big(parts, own, me, w, m, v, name):
    nl, a, b = w.shape
    ta = _shard_rows_tile(a)

    def body(me_ref, p_ref, *refs):
        own_refs, (w_ref, m_ref, v_ref, g_ref, d_ref, mo_ref, vo_ref) = refs[:nl], refs[nl:]
        layer = pl.program_id(0)
        mine = own_refs[0][...]
        for k in range(1, nl):
            mine = jnp.where(layer == k, own_refs[k][...], mine)
        g = None
        for s in range(N_DEV):
            term = jnp.where(me_ref[0] == s, mine, p_ref[s]).astype(F32)
            g = term if g is None else g + term
        g_ref[...] = g
        d_ref[...], mo_ref[...], vo_ref[...] = _adamw(w_ref[...], g, m_ref[...], v_ref[...])

    blk = pl.BlockSpec((None, ta, b), lambda l, i, me_ref: (l, i, 0))

    def own_spec(k):
        return pl.BlockSpec((None, ta, b), lambda l, i, me_ref: (me_ref[0], jnp.where(l == k, i, 0), 0))

    grid_spec = pltpu.PrefetchScalarGridSpec(
        num_scalar_prefetch=1, grid=(nl, a // ta),
        in_specs=[pl.BlockSpec((N_DEV, None, ta, b), lambda l, i, me_ref: (0, l, i, 0))]
        + [own_spec(k) for k in range(nl)] + [blk, blk, blk],
        out_specs=[blk] * 4)
    return pl.pallas_call(body, grid_spec=grid_spec, out_shape=[S(w.shape, F32)] * 4,
                          compiler_params=_cp("arbitrary", "arbitrary"), name=name)(me, parts, *own, w, m, v)


def _adam_small(g, w, m, v, name):
    def body(g_ref, w_ref, m_ref, v_ref, d_ref, mo_ref, vo_ref):
        d_ref[...], mo_ref[...], vo_ref[...] = _adamw(w_ref[...], g_ref[...], m_ref[...], v_ref[...])

    return pl.pallas_call(body, out_shape=[S(g.shape, F32)] * 3, compiler_params=_cp(), name=name)(g, w, m, v)


BIG = ("even_w_in", "even_w_out", "mla_w_down", "mla_w_qb", "mla_w_kvb", "mla_w_o", "mlp_w1", "mlp_w2")
BIG_KEY = dict(even_w_in="win", even_w_out="wout", mla_w_down="wdown", mla_w_qb="wqb", mla_w_kvb="wkvb",
               mla_w_o="wo", mlp_w1="w1", mlp_w2="w2")
SMALL = (("ln_mix_g", "ln_mix_g", None), ("ln_mix_b", "ln_mix_b", None), ("ln_ffn_g", "ln_ffn_g", None),
         ("ln_ffn_b", "ln_ffn_b", None), ("pool_w", "pool_w", None), ("pool_scale", "pool_scale", None),
         ("lru_conv_w", "conv_w", 2), ("lru_conv_b", "conv_b", None), ("lru_w_a", "w_a", None),
         ("lru_b_a", "b_a", None), ("lru_w_x", "w_x", None), ("lru_b_x", "b_x", None), ("lru_lambda", "lam", None),
         ("mla_q_norm_g", "gq", 1), ("mla_kv_norm_g", "gkv", 1))
WEIGHTS = ("ln_mix_g", "ln_mix_b", "ln_ffn_g", "ln_ffn_b", "even_w_in", "pool_w", "pool_scale", "lru_conv_w",
           "lru_conv_b", "lru_w_a", "lru_b_a", "lru_w_x", "lru_b_x", "lru_lambda", "even_w_out", "mla_w_down",
           "mla_q_norm_g", "mla_kv_norm_g", "mla_w_qb", "mla_w_kvb", "mla_w_o", "mlp_w1", "mlp_w2")
ALL_AXES = ("x", "y", "c")


def _layer_weights(l):
    j = l // 2
    if l % 2 == 0:
        mixer = [("win", "even_w_in", j), ("wout", "even_w_out", j)]
    else:
        mixer = [("wdown", "mla_w_down", j), ("wqb", "mla_w_qb", j), ("wkvb", "mla_w_kvb", j), ("wo", "mla_w_o", j)]
    return mixer + [("w1", "mlp_w1", l), ("w2", "mlp_w2", l)]


def _pack(arrays, multiple):
    flat = jnp.concatenate([a.reshape(-1) for a in arrays])
    pad = (-flat.shape[0]) % multiple
    return jnp.pad(flat, (0, pad))


def _unpack(flat, shapes):
    out, at = [], 0
    for shp in shapes:
        n = 1
        for s in shp:
            n *= s
        out.append(flat[at:at + n].reshape(shp))
        at += n
    return out


def _global_shape(local_shape, axis):
    if axis is None:
        return tuple(local_shape)
    return tuple(s * N_DEV if i == axis else s for i, s in enumerate(local_shape))


def _step(x, positions, tgt, w, m, v):
    t = x.shape[1]
    me = _index(_mesh_place())

    sharded = [(name, axis) for name, _, axis in SMALL if axis is not None]
    zeros_with_mine = [lax.dynamic_update_slice_in_dim(jnp.zeros(_global_shape(w[name].shape, axis), F32), w[name],
                                                       me * w[name].shape[axis], axis) for name, axis in sharded]
    chunk = N_DEV * 8 * 128
    gathered = _all_reduce_small(_pack(zeros_with_mine, chunk).reshape(N_DEV, -1, 128), "gather_small")
    full = dict(zip([name for name, _ in sharded],
                    _unpack(gathered.reshape(-1), [_global_shape(w[name].shape, axis) for name, axis in sharded])))

    def zone_of(shard):
        return lax.dynamic_update_slice_in_dim(lax.empty((N_DEV,) + shard.shape, BF16), shard.astype(BF16)[None], me, 0)

    def keys_of(l, part):
        keys = [key for key, _, _ in _layer_weights(l)]
        if l == 0:
            return keys[:1] if part == 0 else keys[1:]
        return keys if part == 0 else []

    shard_of = {(l, key): w[name][i] for l in range(DEPTH) for key, name, i in _layer_weights(l)}
    first = _all_gather_big([zone_of(shard_of[0, key]) for key in keys_of(0, 0)])
    flights, after = {}, (first[0], gathered)
    for l in range(DEPTH):
        for part in (0, 1):
            if (l, part) != (0, 0) and keys_of(l, part):
                zones = [zone_of(shard_of[l, key]) for key in keys_of(l, part)]
                send, recv, _, lands, token = _exchange_start([], zones, GATHER_ICI, "gather_start_%d_%d" % (l, part),
                                                              after=after)
                flights[l, part] = (send, recv, [], lands)
                after = (token,)

    def weights_of(l, part, after):
        keys = keys_of(l, part)
        if (l, part) == (0, 0):
            arrays = first
        elif keys:
            tag = "%d_%d" % (l, part)
            _, lands = _exchange_wait(*flights[l, part], GATHER_ICI, after, "gather_wait_" + tag)
            send, recv, _, lands, _ = _exchange_start([], lands, GATHER_D2D, "gather_pass_" + tag)
            _, arrays = _exchange_wait(send, recv, [], lands, GATHER_D2D, after, "gather_pass_wait_" + tag)
        big = dict(zip(keys, arrays)) if keys else {}
        if "win" in big:
            big["win2d"] = big["win"].transpose(1, 0, 2).reshape(D, EVEN_IN)
        if "wout" in big:
            big["wout2d"] = big["wout"].reshape(EVEN_MIX, D)
        if "wdown" in big:
            big["wdown2d"] = big["wdown"].reshape(D, ODD_IN)
        return big

    zone = {name: lax.empty((N_DEV,) + w[name].shape, BF16) for name in BIG}
    name_of = {key: name for name, key in BIG_KEY.items()}
    sent, last_token = [], [None]

    def grads_done(l, grads):
        keys = list(grads)
        index = {key: i for key, _, i in _layer_weights(l)}
        layers = [index[key] for key in keys]
        send, recv, srcs, lands, tok = _exchange_start([grads[k] for k in keys], [zone[name_of[k]] for k in keys],
                                                       _scatter_plan(layers), "scatter_start_%d_%s" % (l, keys[0]))
        for k, land in zip(keys, lands):
            zone[name_of[k]] = land
        sent.append((send, recv, srcs, keys, layers))
        last_token[0] = tok
        return tok

    row3 = lambda a: a.reshape(a.shape[0], 1, a.shape[1])
    small = dict(ln_mix_g=row3(w["ln_mix_g"]), ln_mix_b=row3(w["ln_mix_b"]), ln_ffn_g=row3(w["ln_ffn_g"]),
                 ln_ffn_b=row3(w["ln_ffn_b"]), pool_w=w["pool_w"], pool_scale=row3(w["pool_scale"]),
                 conv_w=full["lru_conv_w"], conv_b=row3(w["lru_conv_b"]), w_a=w["lru_w_a"], b_a=row3(w["lru_b_a"]),
                 w_x=w["lru_w_x"], b_x=row3(w["lru_b_x"]), lam=row3(w["lru_lambda"]),
                 gq=row3(full["mla_q_norm_g"]), gkv=row3(full["mla_kv_norm_g"]))

    loss_part, grad_x, g = _local_step(x[0] + token[0, 0], positions.reshape(t, 1), tgt[0], small, weights_of,
                                       grads_done)
    loss = lax.psum(loss_part, ALL_AXES)

    local_g = [jnp.stack(g[key]).reshape(_global_shape(w[name].shape, axis)) for name, key, axis in SMALL]
    reduced = _all_reduce_small(_pack(local_g, chunk).reshape(N_DEV, -1, 128), "all_reduce_small", dep=last_token[0])
    after = reduced

    own = {name: [None] * w[name].shape[0] for name in BIG}
    me_arr = me.astype(jnp.int32).reshape(1)
    out = {}
    for n_flight, (send, recv, srcs, keys, layers) in enumerate(sent):
        if n_flight == len(sent) - 1:
            for name in BIG:
                if BIG_KEY[name] not in keys:
                    out[name] = _adam_big(zone[name], own[name], me_arr, w[name], m[name], v[name], "adam_" + name)
                    after = out[name][0]
        srcs, lands = _exchange_wait(send, recv, srcs, [zone[name_of[k]] for k in keys], _scatter_plan(layers),
                                     after, "scatter_wait_%d" % n_flight)
        for k, land, src, layer in zip(keys, lands, srcs, layers):
            zone[name_of[k]] = land
            own[name_of[k]][layer] = src
        after = lands[0]
    for name in BIG:
        if name not in out:
            out[name] = _adam_big(zone[name], own[name], me_arr, w[name], m[name], v[name], "adam_" + name)

    reduced = _unpack(reduced.reshape(-1), [a.shape for a in local_g])
    mine = [a if axis is None else lax.dynamic_slice_in_dim(a, me * w[name].shape[axis], w[name].shape[axis], axis)
            for a, (name, _, axis) in zip(reduced, SMALL)]
    for grad, (name, _, _) in zip(mine, SMALL):
        shape = w[name].shape
        as_2d = lambda a: a.reshape(-1, shape[-1])
        new = _adam_small(as_2d(grad), as_2d(w[name]), as_2d(m[name]), as_2d(v[name]), "adam_" + name)
        out[name] = (grad,) + tuple(a.reshape(shape) for a in new)

    return (loss, grad_x[None]) + tuple(out[name][i] for i in range(4) for name in WEIGHTS)


def kernel(x, positions, ln_mix_g, ln_mix_b, ln_ffn_g, ln_ffn_b, even_w_in, pool_w, pool_scale, lru_conv_w, lru_conv_b, lru_w_a, lru_b_a, lru_w_x, lru_b_x, lru_lambda, even_w_out, mla_w_down, mla_q_norm_g, mla_kv_norm_g, mla_w_qb, mla_w_kvb, mla_w_o, mlp_w1, mlp_w2, loss_target, m_ln_mix_g, m_ln_mix_b, m_ln_ffn_g, m_ln_ffn_b, m_even_w_in, m_pool_w, m_pool_scale, m_lru_conv_w, m_lru_conv_b, m_lru_w_a, m_lru_b_a, m_lru_w_x, m_lru_b_x, m_lru_lambda, m_even_w_out, m_mla_w_down, m_mla_q_norm_g, m_mla_kv_norm_g, m_mla_w_qb, m_mla_w_kvb, m_mla_w_o, m_mlp_w1, m_mlp_w2, v_ln_mix_g, v_ln_mix_b, v_ln_ffn_g, v_ln_ffn_b, v_even_w_in, v_pool_w, v_pool_scale, v_lru_conv_w, v_lru_conv_b, v_lru_w_a, v_lru_b_a, v_lru_w_x, v_lru_b_x, v_lru_lambda, v_even_w_out, v_mla_w_down, v_mla_q_norm_g, v_mla_kv_norm_g, v_mla_w_qb, v_mla_w_kvb, v_mla_w_o, v_mlp_w1, v_mlp_w2):
    w = dict(zip(WEIGHTS, (ln_mix_g, ln_mix_b, ln_ffn_g, ln_ffn_b, even_w_in, pool_w, pool_scale, lru_conv_w,
                           lru_conv_b, lru_w_a, lru_b_a, lru_w_x, lru_b_x, lru_lambda, even_w_out, mla_w_down,
                           mla_q_norm_g, mla_kv_norm_g, mla_w_qb, mla_w_kvb, mla_w_o, mlp_w1, mlp_w2)))
    m = dict(zip(WEIGHTS, (m_ln_mix_g, m_ln_mix_b, m_ln_ffn_g, m_ln_ffn_b, m_even_w_in, m_pool_w, m_pool_scale,
                           m_lru_conv_w, m_lru_conv_b, m_lru_w_a, m_lru_b_a, m_lru_w_x, m_lru_b_x, m_lru_lambda,
                           m_even_w_out, m_mla_w_down, m_mla_q_norm_g, m_mla_kv_norm_g, m_mla_w_qb, m_mla_w_kvb,
                           m_mla_w_o, m_mlp_w1, m_mlp_w2)))
    v = dict(zip(WEIGHTS, (v_ln_mix_g, v_ln_mix_b, v_ln_ffn_g, v_ln_ffn_b, v_even_w_in, v_pool_w, v_pool_scale,
                           v_lru_conv_w, v_lru_conv_b, v_lru_w_a, v_lru_b_a, v_lru_w_x, v_lru_b_x, v_lru_lambda,
                           v_even_w_out, v_mla_w_down, v_mla_q_norm_g, v_mla_kv_norm_g, v_mla_w_qb, v_mla_w_kvb,
                           v_mla_w_o, v_mlp_w1, v_mlp_w2)))
    return _step(x, positions, loss_target, w, m, v)
```

```python
import functools

import jax
import jax.numpy as jnp
from jax import lax
from jax.experimental import pallas as pl
from jax.experimental.pallas import tpu as pltpu

F32 = jnp.float32
BF16 = jnp.bfloat16
S = jax.ShapeDtypeStruct

D = 1024
DEPTH = 4
N_DEV = 8
CHUNK_SHIFT = 6
POOL_WINDOWS = (2, 4, 8, 16)
POOL_W = 512
LRU_W = 1024
LRU_HEADS = 8
HEAD = 128
LRU_C = 8.0
EVEN_IN = 2560
EVEN_MIX = 1536
MLA_HEADS = 8
NOPE = 128
ROPE = 64
VDIM = 128
Q_RANK = 384
KV_RANK = 256
ODD_IN = 704
D_FF = 4096
FF_BLK = D_FF // N_DEV
ROPE_THETA = 10000.0
ALPHA = (2 * DEPTH) ** 0.25
LN_EPS = 1e-5
RMS_EPS = 1e-6
ATT_SCALE = (NOPE + ROPE) ** -0.5
NEG = float(jnp.finfo(jnp.float32).min)
ADAM_LR = 0.001
ADAM_B1 = 0.9
ADAM_B2 = 0.999
ADAM_EPS = 1e-08
ADAM_WD = 0.01
ADAM_STEP = 10
V7X_VMEM_BYTES = 64 * 1024 * 1024
VMEM_LIMIT = V7X_VMEM_BYTES - 8 * 1024 * 1024
MESH = pl.DeviceIdType.MESH


def _cp(*sem):
    return pltpu.CompilerParams(dimension_semantics=sem if sem else None, vmem_limit_bytes=VMEM_LIMIT)


def _dot(a, b):
    return jnp.dot(a, b, preferred_element_type=F32)


def _dot_nt(a, b):
    return lax.dot_general(a, b, (((1,), (1,)), ((), ())), preferred_element_type=F32)


def _dot_tn(a, b):
    return lax.dot_general(a, b, (((0,), (0,)), ((), ())), preferred_element_type=F32)


def _full(shape):
    return pl.BlockSpec(shape, lambda *_: (0,) * len(shape))


def _mm(a, b, *, mode, grid, a_spec, b_spec, out_shape, out_spec, name, add=None, add_spec=None, add_scale=1.0):
    dot = {"nn": _dot, "nt": _dot_nt, "tn": _dot_tn}[mode]

    def body(*refs):
        if add is None:
            a_ref, b_ref, o_ref = refs
            acc = dot(a_ref[...].astype(BF16), b_ref[...].astype(BF16))
        else:
            a_ref, b_ref, add_ref, o_ref = refs
            acc = dot(a_ref[...].astype(BF16), b_ref[...].astype(BF16)) + add_scale * add_ref[...]
        o_ref[...] = acc.astype(o_ref.dtype)

    ops = (a, b) if add is None else (a, b, add)
    specs = [a_spec, b_spec] if add is None else [a_spec, b_spec, add_spec]
    return pl.pallas_call(body, grid=grid, in_specs=specs, out_specs=out_spec, out_shape=out_shape,
                          compiler_params=_cp(*(("parallel",) * len(grid))), name=name)(*ops)


def _ln_stats(z):
    mu = jnp.mean(z, axis=-1, keepdims=True)
    zc = z - mu
    var = jnp.mean(zc * zc, axis=-1, keepdims=True)
    rstd = lax.rsqrt(var + LN_EPS)
    return zc * rstd, rstd


def _row_tile(t):
    return min(512, t)


def _resid_ln(x, mix, g3, b3, l, name):
    t = x.shape[0]
    bm = _row_tile(t)

    def body(x_ref, m_ref, g_ref, b_ref, z_ref, y_ref, yb_ref):
        z = ALPHA * x_ref[...] + m_ref[...]
        xh, _ = _ln_stats(z)
        y = xh * g_ref[...] + b_ref[...]
        z_ref[...] = z
        y_ref[...] = y
        yb_ref[...] = y.astype(BF16)

    row = pl.BlockSpec((bm, D), lambda i: (i, 0))
    vec = pl.BlockSpec((None, 1, D), lambda i: (l, 0, 0))
    return pl.pallas_call(body, grid=(t // bm,), in_specs=[row, row, vec, vec], out_specs=[row, row, row],
                          out_shape=[S((t, D), F32), S((t, D), F32), S((t, D), BF16)],
                          compiler_params=_cp("parallel"), name=name)(x, mix, g3, b3)


def _ln_bwd(d, z, g3, l, name, r=None, dep=None):
    t = z.shape[0]
    bm = _row_tile(t)

    def body(*refs):
        refs = list(refs)
        d_ref = refs.pop(0)
        dy = d_ref[...]
        if r is not None:
            dy = dy + ALPHA * refs.pop(0)[...]
        z_ref, g_ref = refs.pop(0), refs.pop(0)
        if dep is not None:
            refs.pop(0)
        dz_ref, dzb_ref, dg_ref, db_ref = refs
        xh, rstd = _ln_stats(z_ref[...])
        dyg = dy * g_ref[...]
        m1 = jnp.mean(dyg, axis=-1, keepdims=True)
        m2 = jnp.mean(dyg * xh, axis=-1, keepdims=True)
        dz = rstd * (dyg - m1 - xh * m2)
        dz_ref[...] = dz
        dzb_ref[...] = dz.astype(BF16)

        @pl.when(pl.program_id(0) == 0)
        def _():
            dg_ref[...] = jnp.zeros_like(dg_ref)
            db_ref[...] = jnp.zeros_like(db_ref)

        dg_ref[...] += jnp.sum(dy * xh, axis=0, keepdims=True)
        db_ref[...] += jnp.sum(dy, axis=0, keepdims=True)

    row = pl.BlockSpec((bm, D), lambda i: (i, 0))
    vec = pl.BlockSpec((None, 1, D), lambda i: (l, 0, 0))
    acc = pl.BlockSpec((1, D), lambda i: (0, 0))
    ops = [d, z, g3] if r is None else [d, r, z, g3]
    specs = [row, row, vec] if r is None else [row, row, row, vec]
    if dep is not None:
        ops.append(dep)
        specs.append(_full(dep.shape))
    return pl.pallas_call(body, grid=(t // bm,), in_specs=specs, out_specs=[row, row, acc, acc],
                          out_shape=[S((t, D), F32), S((t, D), BF16), S((1, D), F32), S((1, D), F32)],
                          compiler_params=_cp("arbitrary"), name=name)(*ops)


def _loss_grad(y, tgt):
    t = y.shape[0]
    bm = _row_tile(t)

    def body(y_ref, t_ref, dy_ref, loss_ref, acc_ref):
        i = pl.program_id(0)
        e = y_ref[...] - t_ref[...]
        dy_ref[...] = e * (1.0 / D)

        @pl.when(i == 0)
        def _():
            acc_ref[...] = jnp.zeros_like(acc_ref)

        acc_ref[...] += jnp.sum(e * e, axis=0, keepdims=True)

        @pl.when(i == pl.num_programs(0) - 1)
        def _():
            loss_ref[...] = jnp.full(loss_ref.shape, (0.5 / D) * jnp.sum(acc_ref[...]), F32)

    row = pl.BlockSpec((bm, D), lambda i: (i, 0))
    return pl.pallas_call(body, grid=(t // bm,), in_specs=[row, row],
                          out_specs=[row, pl.BlockSpec((1, 128), lambda i: (0, 0))],
                          out_shape=[S((t, D), F32), S((1, 128), F32)],
                          scratch_shapes=[pltpu.VMEM((1, D), F32)],
                          compiler_params=_cp("arbitrary"), name="loss_grad")(y, tgt)


def _mlp_row_tile(t):
    return min(1024, t)


def _mlp_fwd(yb, w1g, w2g):
    t = yb.shape[0]
    bm = _mlp_row_tile(t)

    def body(y_ref, w1_ref, w2_ref, o_ref):
        j = pl.program_id(1)
        h = jnp.maximum(_dot(y_ref[...], w1_ref[...]), 0.0)
        c = _dot((h * h).astype(BF16), w2_ref[...])

        @pl.when(j == 0)
        def _():
            o_ref[...] = c

        @pl.when(j > 0)
        def _():
            o_ref[...] += c

    return pl.pallas_call(
        body, grid=(t // bm, N_DEV),
        in_specs=[pl.BlockSpec((bm, D), lambda i, j: (i, 0)),
                  pl.BlockSpec((None, D, FF_BLK), lambda i, j: (j, 0, 0)),
                  pl.BlockSpec((None, FF_BLK, D), lambda i, j: (j, 0, 0))],
        out_specs=pl.BlockSpec((bm, D), lambda i, j: (i, 0)),
        out_shape=S((t, D), F32), compiler_params=_cp("parallel", "arbitrary"), name="mlp_fwd")(yb, w1g, w2g)


def _mlp_bwd_dh(yb, dzb, w1g, w2g):
    t = yb.shape[0]
    bm = _mlp_row_tile(t)

    def body(y_ref, dz_ref, w1_ref, w2_ref, a_ref, dh_ref, acc_ref):
        j = pl.program_id(1)
        r = jnp.maximum(_dot(y_ref[...], w1_ref[...]), 0.0)
        a_ref[...] = (r * r).astype(BF16)
        da = _dot_nt(dz_ref[...], w2_ref[...])
        dh = (da * (2.0 * r)).astype(BF16)
        dh_ref[...] = dh
        c = _dot_nt(dh, w1_ref[...])

        @pl.when(j == 0)
        def _():
            acc_ref[...] = c

        @pl.when(j > 0)
        def _():
            acc_ref[...] += c

    row = pl.BlockSpec((bm, D), lambda i, j: (i, 0))
    hid = pl.BlockSpec((bm, FF_BLK), lambda i, j: (i, j))
    return pl.pallas_call(
        body, grid=(t // bm, N_DEV),
        in_specs=[row, row,
                  pl.BlockSpec((None, D, FF_BLK), lambda i, j: (j, 0, 0)),
                  pl.BlockSpec((None, FF_BLK, D), lambda i, j: (j, 0, 0))],
        out_specs=[hid, hid, row],
        out_shape=[S((t, D_FF), BF16), S((t, D_FF), BF16), S((t, D), F32)],
        compiler_params=_cp("parallel", "arbitrary"), name="mlp_bwd_dh")(yb, dzb, w1g, w2g)


def _shift_dn(x, k, rows, fill=0.0):
    return jnp.where(rows >= k, pltpu.roll(x, k, 0), fill)


def _shift_up(x, k, rows, fill=0.0):
    t = x.shape[0]
    return jnp.where(rows < t - k, pltpu.roll(x, t - k, 0), fill)


def _scan_dn(a, b, rows):
    k = 1
    t = a.shape[0]
    while k < t:
        b = a * _shift_dn(b, k, rows) + b
        if 2 * k < t:
            a = a * _shift_dn(a, k, rows, 1.0)
        k *= 2
    return b


def _scan_up(a, b, rows):
    k = 1
    t = a.shape[0]
    while k < t:
        b = a * _shift_up(b, k, rows) + b
        if 2 * k < t:
            a = a * _shift_up(a, k, rows, 1.0)
        k *= 2
    return b


def _window_sum_dn(x, w, rows):
    k = 1
    while k < w:
        x = x + _shift_dn(x, k, rows)
        k *= 2
    return x


def _window_sum_up(x, w, rows):
    k = 1
    while k < w:
        x = x + _shift_up(x, k, rows)
        k *= 2
    return x


def _pool_diff(u, w, rows):
    inv_count = 1.0 / jnp.minimum(rows + 1, w).astype(F32)
    return _window_sum_dn(u, w, rows) * inv_count - u, inv_count


def _pool_fwd(proj, pool_w, pool_scale3, j):
    t = proj.shape[0]

    def body(u_ref, w_ref, s_ref, y_ref):
        rows = lax.broadcasted_iota(jnp.int32, (t, HEAD), 0)
        for g, w in enumerate(POOL_WINDOWS):
            cols = slice(g * HEAD, (g + 1) * HEAD)
            d, _ = _pool_diff(u_ref[:, cols], w, rows)
            y = _dot(d.astype(BF16), w_ref[g].astype(BF16)) * s_ref[:, cols]
            y_ref[:, cols] = y.astype(BF16)

    return pl.pallas_call(
        body, grid=(1,),
        in_specs=[pl.BlockSpec((t, POOL_W), lambda i: (0, 0)),
                  pl.BlockSpec((None, 4, HEAD, HEAD), lambda i: (j, 0, 0, 0)),
                  pl.BlockSpec((None, 1, POOL_W), lambda i: (j, 0, 0))],
        out_specs=pl.BlockSpec((t, POOL_W), lambda i: (0, 0)),
        out_shape=S((t, POOL_W), BF16), compiler_params=_cp("arbitrary"), name="pool_fwd")(proj, pool_w, pool_scale3)


def _pool_bwd(proj, dycat, pool_w, pool_scale3, j):
    t = proj.shape[0]

    def body(u_ref, dy_ref, w_ref, s_ref, du_ref, dw_ref, ds_ref):
        rows = lax.broadcasted_iota(jnp.int32, (t, HEAD), 0)
        for g, w in enumerate(POOL_WINDOWS):
            cols = slice(g * HEAD, (g + 1) * HEAD)
            d, inv_count = _pool_diff(u_ref[:, cols], w, rows)
            db = d.astype(BF16)
            wg = w_ref[g].astype(BF16)
            dy = dy_ref[:, cols]
            ds_ref[:, cols] = jnp.sum(dy * _dot(db, wg), axis=0, keepdims=True)
            dzz = (dy * s_ref[:, cols]).astype(BF16)
            dw_ref[g] = _dot_tn(db, dzz)
            dd = _dot_nt(dzz, wg)
            du_ref[:, cols] = (_window_sum_up(dd * inv_count, w, rows) - dd).astype(BF16)

    return pl.pallas_call(
        body, grid=(1,),
        in_specs=[pl.BlockSpec((t, POOL_W), lambda i: (0, 0)),
                  pl.BlockSpec((t, POOL_W), lambda i: (0, 0)),
                  pl.BlockSpec((None, 4, HEAD, HEAD), lambda i: (j, 0, 0, 0)),
                  pl.BlockSpec((None, 1, POOL_W), lambda i: (j, 0, 0))],
        out_specs=[pl.BlockSpec((t, POOL_W), lambda i: (0, 0)), _full((4, HEAD, HEAD)), _full((1, POOL_W))],
        out_shape=[S((t, POOL_W), BF16), S((4, HEAD, HEAD), F32), S((1, POOL_W), F32)],
        compiler_params=_cp("arbitrary"), name="pool_bwd")(proj, dycat, pool_w, pool_scale3)


GELU_C = 0.7978845608028654
GELU_K = 0.044715


def _gelu(x):
    th = jnp.tanh(GELU_C * (x + GELU_K * x * x * x))
    return 0.5 * x * (1.0 + th), th


def _lru_forward(u, gate, cw, cb, wa, ba, wx, bx, lam, rows):
    v = cw[3:4] * u + cw[2:3] * _shift_dn(u, 1, rows) + cw[1:2] * _shift_dn(u, 2, rows) \
        + cw[0:1] * _shift_dn(u, 3, rows) + cb
    vb = v.astype(BF16)
    r = jax.nn.sigmoid(_dot(vb, wa) + ba)
    i = jax.nn.sigmoid(_dot(vb, wx) + bx)
    sp = jnp.maximum(-lam, 0.0) + jnp.log1p(jnp.exp(-jnp.abs(lam)))
    log_a = (-LRU_C) * r * sp
    a = jnp.exp(log_a)
    one_m_a2 = -jnp.tanh(log_a) * (a * a + 1.0)
    mult = jnp.sqrt(one_m_a2)
    h = _scan_dn(a, mult * (i * v), rows)
    gl, th = _gelu(gate)
    return dict(v=v, vb=vb, r=r, i=i, sp=sp, a=a, mult=mult, h=h, gl=gl, th=th)


def _lru_specs(t, j, col0_u, col0_g):
    blk = lambda c0: pl.BlockSpec((t, HEAD), lambda h: (0, c0 + h))
    vec = pl.BlockSpec((None, 1, HEAD), lambda h: (j, 0, h))
    return [blk(col0_u), blk(col0_g),
            pl.BlockSpec((None, 4, HEAD), lambda h: (j, 0, h)), vec,
            pl.BlockSpec((None, None, HEAD, HEAD), lambda h: (j, h, 0, 0)), vec,
            pl.BlockSpec((None, None, HEAD, HEAD), lambda h: (j, h, 0, 0)), vec, vec]


def _lru_fwd(proj, p, j):
    t = proj.shape[0]

    def body(u_ref, g_ref, cw_ref, cb_ref, wa_ref, ba_ref, wx_ref, bx_ref, lam_ref, y_ref):
        rows = lax.broadcasted_iota(jnp.int32, (t, HEAD), 0)
        f = _lru_forward(u_ref[...], g_ref[...], cw_ref[...], cb_ref[...], wa_ref[...].astype(BF16), ba_ref[...],
                         wx_ref[...].astype(BF16), bx_ref[...], lam_ref[...], rows)
        y_ref[...] = (f["h"] * f["gl"]).astype(BF16)

    return pl.pallas_call(
        body, grid=(LRU_HEADS,), in_specs=_lru_specs(t, j, POOL_W // HEAD, (POOL_W + LRU_W) // HEAD),
        out_specs=pl.BlockSpec((t, HEAD), lambda h: (0, h)), out_shape=S((t, LRU_W), BF16),
        compiler_params=_cp("parallel"), name="lru_fwd")(
            proj, proj, p["conv_w"], p["conv_b"], p["w_a"], p["b_a"], p["w_x"], p["b_x"], p["lam"])


def _lru_bwd(proj, dycat, p, j):
    t = proj.shape[0]

    def body(u_ref, g_ref, cw_ref, cb_ref, wa_ref, ba_ref, wx_ref, bx_ref, lam_ref, dy_ref,
             du_ref, dgate_ref, dcw_ref, dcb_ref, dwa_ref, dba_ref, dwx_ref, dbx_ref, dlam_ref):
        rows = lax.broadcasted_iota(jnp.int32, (t, HEAD), 0)
        u = u_ref[...]
        gate = g_ref[...]
        cw = cw_ref[...]
        wa = wa_ref[...].astype(BF16)
        wx = wx_ref[...].astype(BF16)
        lam = lam_ref[...]
        f = _lru_forward(u, gate, cw, cb_ref[...], wa, ba_ref[...], wx, bx_ref[...], lam, rows)
        v, r, i, a, mult, h, th = f["v"], f["r"], f["i"], f["a"], f["mult"], f["h"], f["th"]
        dy = dy_ref[...]
        dgl = 0.5 * (1.0 + th) + 0.5 * gate * (1.0 - th * th) * GELU_C * (1.0 + 3.0 * GELU_K * gate * gate)
        dgate_ref[...] = (dy * h * dgl).astype(BF16)
        g = _scan_up(_shift_up(a, 1, rows), dy * f["gl"], rows)
        da = g * _shift_dn(h, 1, rows)
        iv = i * v
        dmult = g * iv
        di = g * mult * v
        dv = g * mult * i
        dlog_a = da * a - dmult * (a * a) / mult
        dr = dlog_a * (-LRU_C) * f["sp"]
        dsp = jnp.sum(dlog_a * (-LRU_C) * r, axis=0, keepdims=True)
        dlam_ref[...] = -dsp * jax.nn.sigmoid(-lam)
        dpa = dr * r * (1.0 - r)
        dpx = di * i * (1.0 - i)
        dpab = dpa.astype(BF16)
        dpxb = dpx.astype(BF16)
        dwa_ref[...] = _dot_tn(f["vb"], dpab)
        dwx_ref[...] = _dot_tn(f["vb"], dpxb)
        dba_ref[...] = jnp.sum(dpa, axis=0, keepdims=True)
        dbx_ref[...] = jnp.sum(dpx, axis=0, keepdims=True)
        dv = dv + _dot_nt(dpab, wa) + _dot_nt(dpxb, wx)
        dcb_ref[...] = jnp.sum(dv, axis=0, keepdims=True)
        du = cw[3:4] * dv
        dcw_ref[3:4, :] = jnp.sum(dv * u, axis=0, keepdims=True)
        for k in (1, 2, 3):
            du = du + cw[3 - k:4 - k] * _shift_up(dv, k, rows)
            dcw_ref[3 - k:4 - k, :] = jnp.sum(dv * _shift_dn(u, k, rows), axis=0, keepdims=True)
        du_ref[...] = du.astype(BF16)

    blk = pl.BlockSpec((t, HEAD), lambda h: (0, h))
    vec = pl.BlockSpec((1, HEAD), lambda h: (0, h))
    mat = pl.BlockSpec((None, HEAD, HEAD), lambda h: (h, 0, 0))
    return pl.pallas_call(
        body, grid=(LRU_HEADS,),
        in_specs=_lru_specs(t, j, POOL_W // HEAD, (POOL_W + LRU_W) // HEAD)
        + [pl.BlockSpec((t, HEAD), lambda h: (0, POOL_W // HEAD + h))],
        out_specs=[blk, blk, pl.BlockSpec((4, HEAD), lambda h: (0, h)), vec, mat, vec, mat, vec, vec],
        out_shape=[S((t, LRU_W), BF16), S((t, LRU_W), BF16), S((4, LRU_W), F32), S((1, LRU_W), F32),
                   S((LRU_HEADS, HEAD, HEAD), F32), S((1, LRU_W), F32),
                   S((LRU_HEADS, HEAD, HEAD), F32), S((1, LRU_W), F32), S((1, LRU_W), F32)],
        compiler_params=_cp("parallel"), name="lru_bwd")(
            proj, proj, p["conv_w"], p["conv_b"], p["w_a"], p["b_a"], p["w_x"], p["b_x"], p["lam"], dycat)


def _rope(x, c, s):
    x1 = x[:, :ROPE // 2]
    x2 = x[:, ROPE // 2:]
    return jnp.concatenate([x1 * c - x2 * s, x1 * s + x2 * c], axis=-1)


def _rope_t(d, c, s):
    d1 = d[:, :ROPE // 2]
    d2 = d[:, ROPE // 2:]
    return jnp.concatenate([d1 * c + d2 * s, d2 * c - d1 * s], axis=-1)


def _rope_tables(pos2, inv_freq):
    t = pos2.shape[0]

    def body(p_ref, f_ref, c_ref, s_ref):
        ang = p_ref[...].astype(F32) * f_ref[...]
        c_ref[...] = jnp.cos(ang)
        s_ref[...] = jnp.sin(ang)

    return pl.pallas_call(body, out_shape=[S((t, ROPE // 2), F32), S((t, ROPE // 2), F32)],
                          name="rope_tables")(pos2, inv_freq)


def _down_norm(xb, wdown_g, gq3, gkv3, cos, sin, j):
    t = xb.shape[0]
    bm = _row_tile(t)

    def body(x_ref, w_ref, gq_ref, gkv_ref, c_ref, s_ref, down_ref, cq_ref, ckv_ref, kpe_ref):
        w = w_ref[...].reshape(D, ODD_IN)
        down = _dot(x_ref[...], w)
        down_ref[...] = down
        q = down[:, :Q_RANK]
        cq_ref[...] = (q * lax.rsqrt(jnp.mean(q * q, axis=-1, keepdims=True) + RMS_EPS) * gq_ref[...]).astype(BF16)
        kv = down[:, Q_RANK:Q_RANK + KV_RANK]
        ckv_ref[...] = (kv * lax.rsqrt(jnp.mean(kv * kv, axis=-1, keepdims=True) + RMS_EPS)
                        * gkv_ref[...]).astype(BF16)
        kpe_ref[...] = _rope(down[:, Q_RANK + KV_RANK:], c_ref[...], s_ref[...])

    row = lambda n: pl.BlockSpec((bm, n), lambda i: (i, 0))
    return pl.pallas_call(
        body, grid=(t // bm,),
        in_specs=[row(D), _full((N_DEV, D // N_DEV, ODD_IN)),
                  pl.BlockSpec((None, 1, Q_RANK), lambda i: (j, 0, 0)),
                  pl.BlockSpec((None, 1, KV_RANK), lambda i: (j, 0, 0)), row(ROPE // 2), row(ROPE // 2)],
        out_specs=[row(ODD_IN), row(Q_RANK), row(KV_RANK), row(ROPE)],
        out_shape=[S((t, ODD_IN), F32), S((t, Q_RANK), BF16), S((t, KV_RANK), BF16), S((t, ROPE), F32)],
        compiler_params=_cp("parallel"), name="down_norm")(xb, wdown_g, gq3, gkv3, cos, sin)


def _q_tile(t):
    return min(256, t // 2)


def _attn_probs(q, k, qs):
    s = _dot_nt(q, k) * ATT_SCALE
    tq = q.shape[0]
    rows = lax.broadcasted_iota(jnp.int32, (tq, tq), 0)
    cols = lax.broadcasted_iota(jnp.int32, (tq, tq), 1)
    last = jnp.where(jnp.right_shift(cols, CHUNK_SHIFT) <= jnp.right_shift(rows, CHUNK_SHIFT), s[:, qs:], NEG)
    s = last if qs == 0 else jnp.concatenate([s[:, :qs], last], axis=1)
    e = jnp.exp(s - jnp.max(s, axis=-1, keepdims=True))
    return e / jnp.sum(e, axis=-1, keepdims=True)


def _head_qkv(cq, ckv, kpe, c, s, wq_ref, wkv_ref):
    q = jnp.concatenate([_dot(cq, wq_ref[:, :NOPE]), _rope(_dot(cq, wq_ref[:, NOPE:]), c, s)], axis=1).astype(BF16)
    k = jnp.concatenate([_dot(ckv, wkv_ref[:, :NOPE]), kpe], axis=1).astype(BF16)
    vv = _dot(ckv, wkv_ref[:, NOPE:]).astype(BF16)
    return q, k, vv


def _attn_in_specs(t):
    return [_full((t, Q_RANK)), _full((t, KV_RANK)), _full((t, ROPE)), _full((t, ROPE // 2)), _full((t, ROPE // 2)),
            pl.BlockSpec((None, Q_RANK, NOPE + ROPE), lambda h: (h, 0, 0)),
            pl.BlockSpec((None, KV_RANK, NOPE + VDIM), lambda h: (h, 0, 0)),
            pl.BlockSpec((None, VDIM, D), lambda h: (h, 0, 0))]


def _attn_fwd(cq, ckv, kpe, cos, sin, wqb_g, wkvb_g, wo_g):
    t = cq.shape[0]
    tq = _q_tile(t)

    def body(cq_ref, ckv_ref, kpe_ref, c_ref, s_ref, wq_ref, wkv_ref, wo_ref, o_ref, mix_ref):
        q, k, vv = _head_qkv(cq_ref[...], ckv_ref[...], kpe_ref[...], c_ref[...], s_ref[...], wq_ref, wkv_ref)
        for qs in range(0, t, tq):
            ke = qs + tq
            p = _attn_probs(q[qs:ke], k[:ke], qs)
            o_ref[qs:ke, :] = _dot(p.astype(BF16), vv[:ke]).astype(BF16)
        c = _dot(o_ref[...], wo_ref[...])

        @pl.when(pl.program_id(0) == 0)
        def _():
            mix_ref[...] = c

        @pl.when(pl.program_id(0) > 0)
        def _():
            mix_ref[...] += c

    return pl.pallas_call(
        body, grid=(MLA_HEADS,), in_specs=_attn_in_specs(t),
        out_specs=[pl.BlockSpec((None, t, VDIM), lambda h: (h, 0, 0)), _full((t, D))],
        out_shape=[S((MLA_HEADS, t, VDIM), BF16), S((t, D), F32)],
        compiler_params=_cp("arbitrary"), name="attn_fwd")(cq, ckv, kpe, cos, sin, wqb_g, wkvb_g, wo_g)


def _attn_bwd(cq, ckv, kpe, cos, sin, wqb_g, wkvb_g, wo_g, o, dzb):
    t = cq.shape[0]
    tq = _q_tile(t)

    def body(cq_ref, ckv_ref, kpe_ref, c_ref, s_ref, wq_ref, wkv_ref, wo_ref, o_ref, dz_ref,
             dwo_ref, dwq_ref, dwkv_ref, dcq_ref, dckv_ref, dkpe_ref, dk_s, dv_s, dq_s):
        cqv = cq_ref[...]
        ckvv = ckv_ref[...]
        c = c_ref[...]
        s = s_ref[...]
        q, k, vv = _head_qkv(cqv, ckvv, kpe_ref[...], c, s, wq_ref, wkv_ref)
        dzv = dz_ref[...]
        dwo_ref[...] = _dot_tn(o_ref[...], dzv).astype(BF16)
        do = _dot_nt(dzv, wo_ref[...]).astype(BF16)
        dk_s[...] = jnp.zeros_like(dk_s)
        dv_s[...] = jnp.zeros_like(dv_s)
        for qs in range(0, t, tq):
            ke = qs + tq
            p = _attn_probs(q[qs:ke], k[:ke], qs)
            dp = _dot_nt(do[qs:ke], vv[:ke])
            ds = (p * (dp - jnp.sum(p * dp, axis=-1, keepdims=True)) * ATT_SCALE).astype(BF16)
            dq_s[qs:ke, :] = _dot(ds, k[:ke])
            dk_s[0:ke, :] += _dot_tn(ds, q[qs:ke])
            dv_s[0:ke, :] += _dot_tn(p.astype(BF16), do[qs:ke])
        dqn = dq_s[:, :NOPE].astype(BF16)
        dqp = _rope_t(dq_s[:, NOPE:], c, s).astype(BF16)
        dkn = dk_s[:, :NOPE].astype(BF16)
        dvv = dv_s[...].astype(BF16)
        dwq_ref[:, :NOPE] = _dot_tn(cqv, dqn).astype(BF16)
        dwq_ref[:, NOPE:] = _dot_tn(cqv, dqp).astype(BF16)
        dwkv_ref[:, :NOPE] = _dot_tn(ckvv, dkn).astype(BF16)
        dwkv_ref[:, NOPE:] = _dot_tn(ckvv, dvv).astype(BF16)
        dcq = _dot_nt(dqn, wq_ref[:, :NOPE]) + _dot_nt(dqp, wq_ref[:, NOPE:])
        dckv = _dot_nt(dkn, wkv_ref[:, :NOPE]) + _dot_nt(dvv, wkv_ref[:, NOPE:])

        @pl.when(pl.program_id(0) == 0)
        def _():
            dcq_ref[...] = dcq
            dckv_ref[...] = dckv
            dkpe_ref[...] = dk_s[:, NOPE:]

        @pl.when(pl.program_id(0) > 0)
        def _():
            dcq_ref[...] += dcq
            dckv_ref[...] += dckv
            dkpe_ref[...] += dk_s[:, NOPE:]

    per_head = lambda a, b: pl.BlockSpec((None, a, b), lambda h: (h, 0, 0))
    return pl.pallas_call(
        body, grid=(MLA_HEADS,),
        in_specs=_attn_in_specs(t) + [per_head(t, VDIM), _full((t, D))],
        out_specs=[per_head(VDIM, D), per_head(Q_RANK, NOPE + ROPE), per_head(KV_RANK, NOPE + VDIM),
                   _full((t, Q_RANK)), _full((t, KV_RANK)), _full((t, ROPE))],
        out_shape=[S((MLA_HEADS, VDIM, D), BF16), S((MLA_HEADS, Q_RANK, NOPE + ROPE), BF16),
                   S((MLA_HEADS, KV_RANK, NOPE + VDIM), BF16),
                   S((t, Q_RANK), F32), S((t, KV_RANK), F32), S((t, ROPE), F32)],
        scratch_shapes=[pltpu.VMEM((t, NOPE + ROPE), F32), pltpu.VMEM((t, VDIM), F32),
                        pltpu.VMEM((t, NOPE + ROPE), F32)],
        compiler_params=_cp("arbitrary"), name="attn_bwd")(cq, ckv, kpe, cos, sin, wqb_g, wkvb_g, wo_g, o, dzb)


def _rms_bwd(down, dcq, dckv, dkpe, cos, sin, gq3, gkv3, j):
    t = down.shape[0]
    bm = _row_tile(t)

    def body(down_ref, dcq_ref, dckv_ref, dkpe_ref, c_ref, s_ref, gq_ref, gkv_ref, dd_ref, dgq_ref, dgkv_ref):
        @pl.when(pl.program_id(0) == 0)
        def _():
            dgq_ref[...] = jnp.zeros_like(dgq_ref)
            dgkv_ref[...] = jnp.zeros_like(dgkv_ref)

        def rms_b(x, dy, g):
            rstd = lax.rsqrt(jnp.mean(x * x, axis=-1, keepdims=True) + RMS_EPS)
            xh = x * rstd
            dyg = dy * g
            return rstd * (dyg - xh * jnp.mean(dyg * xh, axis=-1, keepdims=True)), jnp.sum(dy * xh, axis=0, keepdims=True)

        dq, dgq = rms_b(down_ref[:, :Q_RANK], dcq_ref[...], gq_ref[...])
        dkv, dgkv = rms_b(down_ref[:, Q_RANK:Q_RANK + KV_RANK], dckv_ref[...], gkv_ref[...])
        dgq_ref[...] += dgq
        dgkv_ref[...] += dgkv
        dd_ref[:, :Q_RANK] = dq.astype(BF16)
        dd_ref[:, Q_RANK:Q_RANK + KV_RANK] = dkv.astype(BF16)
        dd_ref[:, Q_RANK + KV_RANK:] = _rope_t(dkpe_ref[...], c_ref[...], s_ref[...]).astype(BF16)

    row = lambda n: pl.BlockSpec((bm, n), lambda i: (i, 0))
    return pl.pallas_call(
        body, grid=(t // bm,),
        in_specs=[row(ODD_IN), row(Q_RANK), row(KV_RANK), row(ROPE), row(ROPE // 2), row(ROPE // 2),
                  pl.BlockSpec((None, 1, Q_RANK), lambda i: (j, 0, 0)),
                  pl.BlockSpec((None, 1, KV_RANK), lambda i: (j, 0, 0))],
        out_specs=[row(ODD_IN), _full((1, Q_RANK)), _full((1, KV_RANK))],
        out_shape=[S((t, ODD_IN), BF16), S((1, Q_RANK), F32), S((1, KV_RANK), F32)],
        compiler_params=_cp("arbitrary"), name="rms_bwd")(down, dcq, dckv, dkpe, cos, sin, gq3, gkv3)


def _col_blocks(t, n, bn):
    return pl.BlockSpec((t, bn), lambda i: (0, i))


def _row_blocks(n, bm):
    return pl.BlockSpec((bm, n), lambda i: (i, 0))


def _local_step(x, pos2, tgt, small, weights_of, grads_done):
    t = x.shape[0]
    bm = _row_tile(t)
    inv_freq = (ROPE_THETA ** (-jnp.arange(0, ROPE, 2, dtype=F32) / ROPE)).reshape(1, ROPE // 2)
    cos, sin = _rope_tables(pos2, inv_freq)
    lru_p = {k: small[k] for k in ("conv_w", "conv_b", "w_a", "b_a", "w_x", "b_x", "lam")}

    saved = []
    y, yb = x, x.astype(BF16)
    for l in range(DEPTH):
        j = l // 2
        big = weights_of(l, 0, y)
        sv = dict(xb=yb, big=big)
        if l % 2 == 0:
            proj = _mm(yb, big["win2d"], mode="nn", grid=(EVEN_IN // 512,), a_spec=_full((t, D)),
                       b_spec=_col_blocks(D, EVEN_IN, 512), out_shape=S((t, EVEN_IN), F32),
                       out_spec=_col_blocks(t, EVEN_IN, 512), name="even_proj")
            ycat = jnp.concatenate([_pool_fwd(proj, small["pool_w"], small["pool_scale"], j),
                                    _lru_fwd(proj, lru_p, j)], axis=1)
            big.update(weights_of(l, 1, ycat))
            mix = _mm(ycat, big["wout2d"], mode="nn", grid=(D // 512,), a_spec=_full((t, EVEN_MIX)),
                      b_spec=_col_blocks(EVEN_MIX, D, 512), out_shape=S((t, D), F32),
                      out_spec=_col_blocks(t, D, 512), name="even_out")
            sv.update(proj=proj, ycat=ycat)
        else:
            down, cq, ckv, kpe = _down_norm(yb, big["wdown"], small["gq"], small["gkv"], cos, sin, j)
            o, mix = _attn_fwd(cq, ckv, kpe, cos, sin, big["wqb"], big["wkvb"], big["wo"])
            sv.update(down=down, cq=cq, ckv=ckv, kpe=kpe, o=o)
        z1, y1, y1b = _resid_ln(y, mix, small["ln_mix_g"], small["ln_mix_b"], l, "resid_ln")
        ff = _mlp_fwd(y1b, big["w1"], big["w2"])
        z2, y, yb = _resid_ln(y1, ff, small["ln_ffn_g"], small["ln_ffn_b"], l, "resid_ln")
        sv.update(z1=z1, y1b=y1b, z2=z2)
        saved.append(sv)

    dy, loss_tile = _loss_grad(y, tgt)

    g = {k: [None] * n for k, n in (("ln_mix_g", 4), ("ln_mix_b", 4), ("ln_ffn_g", 4), ("ln_ffn_b", 4),
                                    ("pool_w", 2), ("pool_scale", 2), ("conv_w", 2), ("conv_b", 2),
                                    ("w_a", 2), ("b_a", 2), ("w_x", 2), ("b_x", 2), ("lam", 2),
                                    ("gq", 2), ("gkv", 2))}
    dep = None
    for l in reversed(range(DEPTH)):
        j = l // 2
        sv = saved[l]
        big = sv["big"]
        dz2, dz2b, g["ln_ffn_g"][l], g["ln_ffn_b"][l] = _ln_bwd(dy, sv["z2"], small["ln_ffn_g"], l, "ln_bwd", dep=dep)
        act, dh, dff = _mlp_bwd_dh(sv["y1b"], dz2b, big["w1"], big["w2"])
        dw1 = _mm(sv["y1b"], dh, mode="tn", grid=(N_DEV,), a_spec=_full((t, D)),
                  b_spec=_col_blocks(t, D_FF, FF_BLK), out_shape=S((N_DEV, D, FF_BLK), BF16),
                  out_spec=pl.BlockSpec((None, D, FF_BLK), lambda i: (i, 0, 0)), name="mlp_dw1")
        dw2 = _mm(act, dz2b, mode="tn", grid=(N_DEV,), a_spec=_col_blocks(t, D_FF, FF_BLK),
                  b_spec=_full((t, D)), out_shape=S((N_DEV, FF_BLK, D), BF16),
                  out_spec=pl.BlockSpec((None, FF_BLK, D), lambda i: (i, 0, 0)), name="mlp_dw2")
        dep = grads_done(l, dict(w1=dw1, w2=dw2))
        dz1, dz1b, g["ln_mix_g"][l], g["ln_mix_b"][l] = _ln_bwd(dff, sv["z1"], small["ln_mix_g"], l, "ln_bwd_res",
                                                                 r=dz2, dep=dep)
        if l % 2 == 0:
            wout = big["wout2d"]
            dycat = _mm(dz1b, wout, mode="nt", grid=(EVEN_MIX // 512,), a_spec=_full((t, D)),
                        b_spec=_row_blocks(D, 512), out_shape=S((t, EVEN_MIX), F32),
                        out_spec=_col_blocks(t, EVEN_MIX, 512), name="even_dycat")
            dwout = _mm(sv["ycat"], dz1b, mode="tn", grid=(EVEN_MIX // 512,), a_spec=_col_blocks(t, EVEN_MIX, 512),
                        b_spec=_full((t, D)), out_shape=S((EVEN_MIX, D), BF16), out_spec=_row_blocks(D, 512),
                        name="even_dwout")
            du_pool, g["pool_w"][j], g["pool_scale"][j] = _pool_bwd(sv["proj"], dycat, small["pool_w"],
                                                                   small["pool_scale"], j)
            (du_lru, du_gate, g["conv_w"][j], g["conv_b"][j], g["w_a"][j], g["b_a"][j], g["w_x"][j], g["b_x"][j],
             g["lam"][j]) = _lru_bwd(sv["proj"], dycat, lru_p, j)
            dproj = jnp.concatenate([du_pool, du_lru, du_gate], axis=1)
            dwin = _mm(sv["xb"], dproj, mode="tn", grid=(EVEN_IN // 512,), a_spec=_full((t, D)),
                       b_spec=_col_blocks(t, EVEN_IN, 512), out_shape=S((D, EVEN_IN), BF16),
                       out_spec=_col_blocks(D, EVEN_IN, 512), name="even_dwin")
            dep = grads_done(l, dict(win=dwin.reshape(D, N_DEV, EVEN_IN // N_DEV).transpose(1, 0, 2),
                                     wout=dwout.reshape(N_DEV, EVEN_MIX // N_DEV, D)))
            dy = _mm(dproj, big["win2d"], mode="nt", grid=(t // bm,), a_spec=_row_blocks(EVEN_IN, bm),
                     b_spec=_full((D, EVEN_IN)), out_shape=S((t, D), F32), out_spec=_row_blocks(D, bm),
                     add=dz1, add_spec=_row_blocks(D, bm), add_scale=ALPHA, name="even_dx")
        else:
            dwo, dwqb, dwkvb, dcq, dckv, dkpe = _attn_bwd(
                sv["cq"], sv["ckv"], sv["kpe"], cos, sin, big["wqb"], big["wkvb"], big["wo"], sv["o"], dz1b)
            ddown, g["gq"][j], g["gkv"][j] = _rms_bwd(sv["down"], dcq, dckv, dkpe, cos, sin, small["gq"],
                                                     small["gkv"], j)
            dwdown = _mm(sv["xb"], ddown, mode="tn", grid=(N_DEV,), a_spec=_col_blocks(t, D, D // N_DEV),
                         b_spec=_full((t, ODD_IN)), out_shape=S((N_DEV, D // N_DEV, ODD_IN), BF16),
                         out_spec=pl.BlockSpec((None, D // N_DEV, ODD_IN), lambda i: (i, 0, 0)),
                         name="odd_dwdown")
            dep = grads_done(l, dict(wdown=dwdown, wqb=dwqb, wkvb=dwkvb, wo=dwo))
            dy = _mm(ddown, big["wdown2d"], mode="nt", grid=(t // bm,), a_spec=_row_blocks(ODD_IN, bm),
                     b_spec=_full((D, ODD_IN)), out_shape=S((t, D), F32), out_spec=_row_blocks(D, bm),
                     add=dz1, add_spec=_row_blocks(D, bm), add_scale=ALPHA, name="odd_dx")
    return loss_tile[0, 0], dy, g


def _mesh_place():
    x, y, c = lax.axis_index("x"), lax.axis_index("y"), lax.axis_index("c")
    return x, y, c


def _peer(place, k):
    x, y, c = place
    return (1 - x if k & 4 else x, 1 - y if k & 2 else y, 1 - c if k & 1 else c)


def _index(place):
    x, y, c = place
    return 4 * x + 2 * y + c


ANY = pl.BlockSpec(memory_space=pl.ANY)


def _all_gather_big(zones):
    n = len(zones)

    def body(*refs):
        outs = refs[n:2 * n]
        send, recv = refs[2 * n:]
        x, y, c = _mesh_place()
        me, sibling = (x, y, c), (x, y, 1 - c)
        chips = [(1 - x, y), (x, 1 - y), (1 - x, 1 - y)]

        def copy(w, k, block, to):
            blk = outs[w].at[_index(block)]
            return pltpu.make_async_remote_copy(src_ref=blk, dst_ref=blk, send_sem=send.at[w, k], recv_sem=recv.at[w, k],
                                                device_id=to, device_id_type=MESH)

        first = []
        for w in range(n):
            first.append(copy(w, 0, me, sibling))
            first += [copy(w, 1 + j, me, (*chip, c)) for j, chip in enumerate(chips)]
        for cp in first:
            cp.start()
        passed = []
        for w in range(n):
            for j, chip in enumerate(chips):
                copy(w, 1 + j, (*chip, c), me).wait_recv()
                cp = copy(w, 4 + j, (*chip, c), sibling)
                cp.start()
                passed.append(cp)
        for w in range(n):
            copy(w, 0, sibling, me).wait_recv()
            for j, chip in enumerate(chips):
                copy(w, 4 + j, (*chip, 1 - c), me).wait_recv()
        for cp in first + passed:
            cp.wait_send()

    return pl.pallas_call(
        body, in_specs=[ANY] * n, out_specs=[ANY] * n, out_shape=[S(z.shape, z.dtype) for z in zones],
        input_output_aliases={i: i for i in range(n)},
        scratch_shapes=[pltpu.SemaphoreType.DMA((n, N_DEV - 1)), pltpu.SemaphoreType.DMA((n, N_DEV - 1))],
        compiler_params=pltpu.CompilerParams(has_side_effects=True), name="all_gather_big")(*zones)


def _shard_rows_tile(a):
    return max(d for d in range(16, 257, 16) if a % d == 0)


HBM = pl.BlockSpec(memory_space=pltpu.HBM)
SEM = pl.BlockSpec(memory_space=pltpu.SEMAPHORE)
DATAFLOW = pltpu.SideEffectType.DATAFLOW_SIDE_EFFECTING


def _in_hbm(a):
    return pltpu.with_memory_space_constraint(a, pltpu.HBM)


def _gather_ici_copies(place, src, land, w):
    me = _index(place)
    return [(_peer(place, k), land.at[me], land.at[me]) for k in (1, 2, 4, 6)]


def _gather_d2d_copies(place, src, land, w):
    blocks = [_index(_peer(place, k)) for k in (2, 4, 6)]
    return [(_peer(place, 1), land.at[b], land.at[b]) for b in blocks]


GATHER_ICI = (4, _gather_ici_copies)
GATHER_D2D = (3, _gather_d2d_copies)


def _scatter_plan(layers):
    def copies(place, src, land, w):
        me = _index(place)
        return [(_peer(place, k), src.at[_index(_peer(place, k))], land.at[me, layers[w]]) for k in range(1, N_DEV)]
    return (N_DEV - 1, copies)


def _exchange_start(srcs, lands, plan, name, after=()):
    ns, n = len(srcs), len(lands)
    n_in = ns + n + len(after)
    per, copies = plan

    def body(*refs):
        ins, land = refs[:ns], refs[ns:ns + n]
        send, recv = refs[n_in], refs[n_in + 1]
        token = refs[-1]
        place = _mesh_place()
        for i in range(per):
            for w in range(n):
                target, src, dst = copies(place, ins[w] if ns else None, land[w], w)[i]
                pltpu.make_async_remote_copy(src_ref=src, dst_ref=dst, send_sem=send.at[w * per + i],
                                             recv_sem=recv.at[w * per + i], device_id=target, device_id_type=MESH).start()
        token[...] = jnp.zeros_like(token)

    sems = pltpu.SemaphoreType.DMA((n * per,))
    thru = [pltpu.HBM(a.shape, a.dtype) for a in list(srcs) + list(lands)]
    out = pl.pallas_call(
        body, name=name, in_specs=[HBM] * (ns + n) + [ANY] * len(after),
        out_shape=(sems, sems, *thru, S((8, 128), F32)),
        out_specs=(SEM, SEM, *([HBM] * (ns + n)), pl.BlockSpec(memory_space=pltpu.VMEM)),
        input_output_aliases={i: 2 + i for i in range(ns + n)},
        compiler_params=pltpu.CompilerParams(has_side_effects=DATAFLOW),
    )(*[_in_hbm(a) for a in list(srcs) + list(lands)], *after)
    return out[0], out[1], list(out[2:2 + ns]), list(out[2 + ns:2 + ns + n]), out[-1]


def _exchange_wait(send, recv, srcs, lands, plan, after, name):
    ns, n = len(srcs), len(lands)
    per, copies = plan

    def body(*refs):
        ins, land = refs[:ns], refs[ns:ns + n]
        send_ref, recv_ref = refs[ns + n], refs[ns + n + 1]
        place = _mesh_place()
        for i in range(per):
            for w in range(n):
                target, src, dst = copies(place, ins[w] if ns else None, land[w], w)[i]
                cp = pltpu.make_async_remote_copy(src_ref=src, dst_ref=dst, send_sem=send_ref.at[w * per + i],
                                                  recv_sem=recv_ref.at[w * per + i], device_id=target,
                                                  device_id_type=MESH)
                cp.wait_send()
                cp.wait_recv()

    thru = [pltpu.HBM(a.shape, a.dtype) for a in list(srcs) + list(lands)]
    out = pl.pallas_call(
        body, name=name, in_specs=[HBM] * (ns + n) + [SEM, SEM, ANY],
        out_shape=tuple(thru), out_specs=tuple([HBM] * (ns + n)),
        input_output_aliases={i: i for i in range(ns + n)},
        compiler_params=pltpu.CompilerParams(has_side_effects=DATAFLOW),
    )(*srcs, *lands, send, recv, after)
    return list(out[:ns]), list(out[ns:])


def _all_reduce_small(part, name, dep=None):
    def body(*refs):
        p_ref = refs[0]
        o_ref, rbuf, send1, recv1, send2, recv2 = refs[-6:]
        place = _mesh_place()
        me = _index(place)
        rbuf[pl.ds(me, 1)] = p_ref[pl.ds(me, 1)]
        first = [pltpu.make_async_remote_copy(src_ref=p_ref.at[_index(_peer(place, k))], dst_ref=rbuf.at[me],
                                              send_sem=send1.at[k - 1], recv_sem=recv1.at[k - 1],
                                              device_id=_peer(place, k), device_id_type=MESH)
                 for k in range(1, N_DEV)]
        for cp in first:
            cp.start()
        for cp in first:
            cp.wait()
        acc = rbuf[0]
        for d in range(1, N_DEV):
            acc = acc + rbuf[d]
        o_ref[pl.ds(me, 1)] = acc[None]
        second = [pltpu.make_async_remote_copy(src_ref=o_ref.at[me], dst_ref=o_ref.at[me], send_sem=send2.at[k - 1],
                                               recv_sem=recv2.at[k - 1], device_id=_peer(place, k),
                                               device_id_type=MESH)
                  for k in range(1, N_DEV)]
        for cp in second:
            cp.start()
        for cp in second:
            cp.wait()

    vm = pl.BlockSpec(memory_space=pltpu.VMEM)
    ops = [part] if dep is None else [part, dep]
    return pl.pallas_call(
        body, in_specs=[vm] if dep is None else [vm, ANY], out_specs=vm, out_shape=S(part.shape, F32),
        scratch_shapes=[pltpu.VMEM(part.shape, F32)] + [pltpu.SemaphoreType.DMA((N_DEV - 1,))] * 4,
        compiler_params=pltpu.CompilerParams(has_side_effects=True, vmem_limit_bytes=VMEM_LIMIT), name=name)(*ops)


def _adamw(w, g, m, v):
    m = ADAM_B1 * m + (1.0 - ADAM_B1) * g
    v = ADAM_B2 * v + (1.0 - ADAM_B2) * (g * g)
    m_hat = m / (1.0 - ADAM_B1 ** ADAM_STEP)
    v_hat = v / (1.0 - ADAM_B2 ** ADAM_STEP)
    return -ADAM_LR * (m_hat / (jnp.sqrt(v_hat) + ADAM_EPS) + ADAM_WD * w), m, v


def _adam_big(parts, own, me, w, m, v, name):
    nl, a, b = w.shape
    ta = _shard_rows_tile(a)

    def body(me_ref, p_ref, *refs):
        own_refs, (w_ref, m_ref, v_ref, g_ref, d_ref, mo_ref, vo_ref) = refs[:nl], refs[nl:]
        layer = pl.program_id(0)
        mine = own_refs[0][...]
        for k in range(1, nl):
            mine = jnp.where(layer == k, own_refs[k][...], mine)
        g = None
        for s in range(N_DEV):
            term = jnp.where(me_ref[0] == s, mine, p_ref[s]).astype(F32)
            g = term if g is None else g + term
        g_ref[...] = g
        d_ref[...], mo_ref[...], vo_ref[...] = _adamw(w_ref[...], g, m_ref[...], v_ref[...])

    blk = pl.BlockSpec((None, ta, b), lambda l, i, me_ref: (l, i, 0))

    def own_spec(k):
        return pl.BlockSpec((None, ta, b), lambda l, i, me_ref: (me_ref[0], jnp.where(l == k, i, 0), 0))

    grid_spec = pltpu.PrefetchScalarGridSpec(
        num_scalar_prefetch=1, grid=(nl, a // ta),
        in_specs=[pl.BlockSpec((N_DEV, None, ta, b), lambda l, i, me_ref: (0, l, i, 0))]
        + [own_spec(k) for k in range(nl)] + [blk, blk, blk],
        out_specs=[blk] * 4)
    return pl.pallas_call(body, grid_spec=grid_spec, out_shape=[S(w.shape, F32)] * 4,
                          compiler_params=_cp("arbitrary", "arbitrary"), name=name)(me, parts, *own, w, m, v)


def _adam_small(g, w, m, v, name):
    def body(g_ref, w_ref, m_ref, v_ref, d_ref, mo_ref, vo_ref):
        d_ref[...], mo_ref[...], vo_ref[...] = _adamw(w_ref[...], g_ref[...], m_ref[...], v_ref[...])

    return pl.pallas_call(body, out_shape=[S(g.shape, F32)] * 3, compiler_params=_cp(), name=name)(g, w, m, v)


BIG = ("even_w_in", "even_w_out", "mla_w_down", "mla_w_qb", "mla_w_kvb", "mla_w_o", "mlp_w1", "mlp_w2")
BIG_KEY = dict(even_w_in="win", even_w_out="wout", mla_w_down="wdown", mla_w_qb="wqb", mla_w_kvb="wkvb",
               mla_w_o="wo", mlp_w1="w1", mlp_w2="w2")
SMALL = (("ln_mix_g", "ln_mix_g", None), ("ln_mix_b", "ln_mix_b", None), ("ln_ffn_g", "ln_ffn_g", None),
         ("ln_ffn_b", "ln_ffn_b", None), ("pool_w", "pool_w", None), ("pool_scale", "pool_scale", None),
         ("lru_conv_w", "conv_w", 2), ("lru_conv_b", "conv_b", None), ("lru_w_a", "w_a", None),
         ("lru_b_a", "b_a", None), ("lru_w_x", "w_x", None), ("lru_b_x", "b_x", None), ("lru_lambda", "lam", None),
         ("mla_q_norm_g", "gq", 1), ("mla_kv_norm_g", "gkv", 1))
WEIGHTS = ("ln_mix_g", "ln_mix_b", "ln_ffn_g", "ln_ffn_b", "even_w_in", "pool_w", "pool_scale", "lru_conv_w",
           "lru_conv_b", "lru_w_a", "lru_b_a", "lru_w_x", "lru_b_x", "lru_lambda", "even_w_out", "mla_w_down",
           "mla_q_norm_g", "mla_kv_norm_g", "mla_w_qb", "mla_w_kvb", "mla_w_o", "mlp_w1", "mlp_w2")
ALL_AXES = ("x", "y", "c")


def _layer_weights(l):
    j = l // 2
    if l % 2 == 0:
        mixer = [("win", "even_w_in", j), ("wout", "even_w_out", j)]
    else:
        mixer = [("wdown", "mla_w_down", j), ("wqb", "mla_w_qb", j), ("wkvb", "mla_w_kvb", j), ("wo", "mla_w_o", j)]
    return mixer + [("w1", "mlp_w1", l), ("w2", "mlp_w2", l)]


def _pack(arrays, multiple):
    flat = jnp.concatenate([a.reshape(-1) for a in arrays])
    pad = (-flat.shape[0]) % multiple
    return jnp.pad(flat, (0, pad))


def _unpack(flat, shapes):
    out, at = [], 0
    for shp in shapes:
        n = 1
        for s in shp:
            n *= s
        out.append(flat[at:at + n].reshape(shp))
        at += n
    return out


def _global_shape(local_shape, axis):
    if axis is None:
        return tuple(local_shape)
    return tuple(s * N_DEV if i == axis else s for i, s in enumerate(local_shape))


def _step(x, positions, tgt, w, m, v):
    t = x.shape[1]
    me = _index(_mesh_place())

    sharded = [(name, axis) for name, _, axis in SMALL if axis is not None]
    zeros_with_mine = [lax.dynamic_update_slice_in_dim(jnp.zeros(_global_shape(w[name].shape, axis), F32), w[name],
                                                       me * w[name].shape[axis], axis) for name, axis in sharded]
    chunk = N_DEV * 8 * 128
    gathered = _all_reduce_small(_pack(zeros_with_mine, chunk).reshape(N_DEV, -1, 128), "gather_small")
    full = dict(zip([name for name, _ in sharded],
                    _unpack(gathered.reshape(-1), [_global_shape(w[name].shape, axis) for name, axis in sharded])))

    def zone_of(shard):
        return lax.dynamic_update_slice_in_dim(lax.empty((N_DEV,) + shard.shape, BF16), shard.astype(BF16)[None], me, 0)

    def keys_of(l, part):
        keys = [key for key, _, _ in _layer_weights(l)]
        if l == 0:
            return keys[:1] if part == 0 else keys[1:]
        return keys if part == 0 else []

    shard_of = {(l, key): w[name][i] for l in range(DEPTH) for key, name, i in _layer_weights(l)}
    first = _all_gather_big([zone_of(shard_of[0, key]) for key in keys_of(0, 0)])
    flights, after = {}, (first[0], gathered)
    for l in range(DEPTH):
        for part in (0, 1):
            if (l, part) != (0, 0) and keys_of(l, part):
                zones = [zone_of(shard_of[l, key]) for key in keys_of(l, part)]
                send, recv, _, lands, token = _exchange_start([], zones, GATHER_ICI, "gather_start_%d_%d" % (l, part),
                                                              after=after)
                flights[l, part] = (send, recv, [], lands)
                after = (token,)

    def weights_of(l, part, after):
        keys = keys_of(l, part)
        if (l, part) == (0, 0):
            arrays = first
        elif keys:
            tag = "%d_%d" % (l, part)
            _, lands = _exchange_wait(*flights[l, part], GATHER_ICI, after, "gather_wait_" + tag)
            send, recv, _, lands, _ = _exchange_start([], lands, GATHER_D2D, "gather_pass_" + tag)
            _, arrays = _exchange_wait(send, recv, [], lands, GATHER_D2D, after, "gather_pass_wait_" + tag)
        big = dict(zip(keys, arrays)) if keys else {}
        if "win" in big:
            big["win2d"] = big["win"].transpose(1, 0, 2).reshape(D, EVEN_IN)
        if "wout" in big:
            big["wout2d"] = big["wout"].reshape(EVEN_MIX, D)
        if "wdown" in big:
            big["wdown2d"] = big["wdown"].reshape(D, ODD_IN)
        return big

    zone = {name: lax.empty((N_DEV,) + w[name].shape, BF16) for name in BIG}
    name_of = {key: name for name, key in BIG_KEY.items()}
    sent, last_token = [], [None]

    def grads_done(l, grads):
        keys = list(grads)
        index = {key: i for key, _, i in _layer_weights(l)}
        layers = [index[key] for key in keys]
        send, recv, srcs, lands, tok = _exchange_start([grads[k] for k in keys], [zone[name_of[k]] for k in keys],
                                                       _scatter_plan(layers), "scatter_start_%d_%s" % (l, keys[0]))
        for k, land in zip(keys, lands):
            zone[name_of[k]] = land
        sent.append((send, recv, srcs, keys, layers))
        last_token[0] = tok
        return tok

    row3 = lambda a: a.reshape(a.shape[0], 1, a.shape[1])
    small = dict(ln_mix_g=row3(w["ln_mix_g"]), ln_mix_b=row3(w["ln_mix_b"]), ln_ffn_g=row3(w["ln_ffn_g"]),
                 ln_ffn_b=row3(w["ln_ffn_b"]), pool_w=w["pool_w"], pool_scale=row3(w["pool_scale"]),
                 conv_w=full["lru_conv_w"], conv_b=row3(w["lru_conv_b"]), w_a=w["lru_w_a"], b_a=row3(w["lru_b_a"]),
                 w_x=w["lru_w_x"], b_x=row3(w["lru_b_x"]), lam=row3(w["lru_lambda"]),
                 gq=row3(full["mla_q_norm_g"]), gkv=row3(full["mla_kv_norm_g"]))

    loss_part, grad_x, g = _local_step(x[0] + token[0, 0], positions.reshape(t, 1), tgt[0], small, weights_of,
                                       grads_done)

    own = {name: [None] * w[name].shape[0] for name in BIG}
    me_arr = me.astype(jnp.int32).reshape(1)
    out = {}
    local_g = [jnp.stack(g[key]).reshape(_global_shape(w[name].shape, axis)) for name, key, axis in SMALL]
    local_g.append(loss_part.reshape(1))
    after = last_token[0]
    for n_flight, (send, recv, srcs, keys, layers) in enumerate(sent):
        if n_flight == len(sent) - 1:
            for name in BIG:
                if BIG_KEY[name] not in keys:
                    out[name] = _adam_big(zone[name], own[name], me_arr, w[name], m[name], v[name], "adam_" + name)
                    after = out[name][0]
            reduced = _all_reduce_small(_pack(local_g, chunk).reshape(N_DEV, -1, 128), "all_reduce_small", dep=after)
            after = reduced
        srcs, lands = _exchange_wait(send, recv, srcs, [zone[name_of[k]] for k in keys], _scatter_plan(layers),
                                     after, "scatter_wait_%d" % n_flight)
        for k, land, src, layer in zip(keys, lands, srcs, layers):
            zone[name_of[k]] = land
            own[name_of[k]][layer] = src
        after = lands[0]
    for name in BIG:
        if name not in out:
            out[name] = _adam_big(zone[name], own[name], me_arr, w[name], m[name], v[name], "adam_" + name)

    reduced = _unpack(reduced.reshape(-1), [a.shape for a in local_g])
    loss = reduced[-1][0]
    mine = [a if axis is None else lax.dynamic_slice_in_dim(a, me * w[name].shape[axis], w[name].shape[axis], axis)
            for a, (name, _, axis) in zip(reduced, SMALL)]
    for grad, (name, _, _) in zip(mine, SMALL):
        shape = w[name].shape
        as_2d = lambda a: a.reshape(-1, shape[-1])
        new = _adam_small(as_2d(grad), as_2d(w[name]), as_2d(m[name]), as_2d(v[name]), "adam_" + name)
        out[name] = (grad,) + tuple(a.reshape(shape) for a in new)

    return (loss, grad_x[None]) + tuple(out[name][i] for i in range(4) for name in WEIGHTS)


def kernel(x, positions, ln_mix_g, ln_mix_b, ln_ffn_g, ln_ffn_b, even_w_in, pool_w, pool_scale, lru_conv_w, lru_conv_b, lru_w_a, lru_b_a, lru_w_x, lru_b_x, lru_lambda, even_w_out, mla_w_down, mla_q_norm_g, mla_kv_norm_g, mla_w_qb, mla_w_kvb, mla_w_o, mlp_w1, mlp_w2, loss_target, m_ln_mix_g, m_ln_mix_b, m_ln_ffn_g, m_ln_ffn_b, m_even_w_in, m_pool_w, m_pool_scale, m_lru_conv_w, m_lru_conv_b, m_lru_w_a, m_lru_b_a, m_lru_w_x, m_lru_b_x, m_lru_lambda, m_even_w_out, m_mla_w_down, m_mla_q_norm_g, m_mla_kv_norm_g, m_mla_w_qb, m_mla_w_kvb, m_mla_w_o, m_mlp_w1, m_mlp_w2, v_ln_mix_g, v_ln_mix_b, v_ln_ffn_g, v_ln_ffn_b, v_even_w_in, v_pool_w, v_pool_scale, v_lru_conv_w, v_lru_conv_b, v_lru_w_a, v_lru_b_a, v_lru_w_x, v_lru_b_x, v_lru_lambda, v_even_w_out, v_mla_w_down, v_mla_q_norm_g, v_mla_kv_norm_g, v_mla_w_qb, v_mla_w_kvb, v_mla_w_o, v_mlp_w1, v_mlp_w2):
    w = dict(zip(WEIGHTS, (ln_mix_g, ln_mix_b, ln_ffn_g, ln_ffn_b, even_w_in, pool_w, pool_scale, lru_conv_w,
                           lru_conv_b, lru_w_a, lru_b_a, lru_w_x, lru_b_x, lru_lambda, even_w_out, mla_w_down,
                           mla_q_norm_g, mla_kv_norm_g, mla_w_qb, mla_w_kvb, mla_w_o, mlp_w1, mlp_w2)))
    m = dict(zip(WEIGHTS, (m_ln_mix_g, m_ln_mix_b, m_ln_ffn_g, m_ln_ffn_b, m_even_w_in, m_pool_w, m_pool_scale,
                           m_lru_conv_w, m_lru_conv_b, m_lru_w_a, m_lru_b_a, m_lru_w_x, m_lru_b_x, m_lru_lambda,
                           m_even_w_out, m_mla_w_down, m_mla_q_norm_g, m_mla_kv_norm_g, m_mla_w_qb, m_mla_w_kvb,
                           m_mla_w_o, m_mlp_w1, m_mlp_w2)))
    v = dict(zip(WEIGHTS, (v_ln_mix_g, v_ln_mix_b, v_ln_ffn_g, v_ln_ffn_b, v_even_w_in, v_pool_w, v_pool_scale,
                           v_lru_conv_w, v_lru_conv_b, v_lru_w_a, v_lru_b_a, v_lru_w_x, v_lru_b_x, v_lru_lambda,
                           v_even_w_out, v_mla_w_down, v_mla_q_norm_g, v_mla_kv_norm_g, v_mla_w_qb, v_mla_w_kvb,
                           v_mla_w_o, v_mlp_w1, v_mlp_w2)))
    return _step(x, positions, loss_target, w, m, v)
```

```python
import functools

import jax
import jax.numpy as jnp
from jax import lax
from jax.experimental import pallas as pl
from jax.experimental.pallas import tpu as pltpu

F32 = jnp.float32
BF16 = jnp.bfloat16
S = jax.ShapeDtypeStruct

D = 1024
DEPTH = 4
N_DEV = 8
CHUNK_SHIFT = 6
POOL_WINDOWS = (2, 4, 8, 16)
POOL_W = 512
LRU_W = 1024
LRU_HEADS = 8
HEAD = 128
LRU_C = 8.0
EVEN_IN = 2560
EVEN_MIX = 1536
MLA_HEADS = 8
NOPE = 128
ROPE = 64
VDIM = 128
Q_RANK = 384
KV_RANK = 256
ODD_IN = 704
D_FF = 4096
FF_BLK = D_FF // N_DEV
ROPE_THETA = 10000.0
ALPHA = (2 * DEPTH) ** 0.25
LN_EPS = 1e-5
RMS_EPS = 1e-6
ATT_SCALE = (NOPE + ROPE) ** -0.5
NEG = float(jnp.finfo(jnp.float32).min)
ADAM_LR = 0.001
ADAM_B1 = 0.9
ADAM_B2 = 0.999
ADAM_EPS = 1e-08
ADAM_WD = 0.01
ADAM_STEP = 10
V7X_VMEM_BYTES = 64 * 1024 * 1024
VMEM_LIMIT = V7X_VMEM_BYTES - 8 * 1024 * 1024
MESH = pl.DeviceIdType.MESH


def _cp(*sem):
    return pltpu.CompilerParams(dimension_semantics=sem if sem else None, vmem_limit_bytes=VMEM_LIMIT)


def _dot(a, b):
    return jnp.dot(a, b, preferred_element_type=F32)


def _dot_nt(a, b):
    return lax.dot_general(a, b, (((1,), (1,)), ((), ())), preferred_element_type=F32)


def _dot_tn(a, b):
    return lax.dot_general(a, b, (((0,), (0,)), ((), ())), preferred_element_type=F32)


def _full(shape):
    return pl.BlockSpec(shape, lambda *_: (0,) * len(shape))


def _mm(a, b, *, mode, grid, a_spec, b_spec, out_shape, out_spec, name, add=None, add_spec=None, add_scale=1.0,
        dep=None):
    dot = {"nn": _dot, "nt": _dot_nt, "tn": _dot_tn}[mode]

    def body(*refs):
        a_ref, b_ref, o_ref = refs[0], refs[1], refs[-1]
        acc = dot(a_ref[...].astype(BF16), b_ref[...].astype(BF16))
        if add is not None:
            acc = acc + add_scale * refs[2][...]
        o_ref[...] = acc.astype(o_ref.dtype)

    ops = [a, b] if add is None else [a, b, add]
    specs = [a_spec, b_spec] if add is None else [a_spec, b_spec, add_spec]
    if dep is not None:
        ops.append(dep)
        specs.append(pl.BlockSpec(memory_space=pl.ANY))
    return pl.pallas_call(body, grid=grid, in_specs=specs, out_specs=out_spec, out_shape=out_shape,
                          compiler_params=_cp(*(("parallel",) * len(grid))), name=name)(*ops)


def _ln_stats(z):
    mu = jnp.mean(z, axis=-1, keepdims=True)
    zc = z - mu
    var = jnp.mean(zc * zc, axis=-1, keepdims=True)
    rstd = lax.rsqrt(var + LN_EPS)
    return zc * rstd, rstd


def _row_tile(t):
    return min(512, t)


def _resid_ln(x, mix, g3, b3, l, name):
    t = x.shape[0]
    bm = _row_tile(t)

    def body(x_ref, m_ref, g_ref, b_ref, z_ref, y_ref, yb_ref):
        z = ALPHA * x_ref[...] + m_ref[...]
        xh, _ = _ln_stats(z)
        y = xh * g_ref[...] + b_ref[...]
        z_ref[...] = z
        y_ref[...] = y
        yb_ref[...] = y.astype(BF16)

    row = pl.BlockSpec((bm, D), lambda i: (i, 0))
    vec = pl.BlockSpec((None, 1, D), lambda i: (l, 0, 0))
    return pl.pallas_call(body, grid=(t // bm,), in_specs=[row, row, vec, vec], out_specs=[row, row, row],
                          out_shape=[S((t, D), F32), S((t, D), F32), S((t, D), BF16)],
                          compiler_params=_cp("parallel"), name=name)(x, mix, g3, b3)


def _proj_resid_ln(x, a, wmat, g3, b3, l, name):
    t, k = a.shape
    bm = _row_tile(t)

    def body(x_ref, a_ref, w_ref, g_ref, b_ref, z_ref, y_ref, yb_ref):
        z = ALPHA * x_ref[...] + _dot(a_ref[...], w_ref[...])
        xh, _ = _ln_stats(z)
        y = xh * g_ref[...] + b_ref[...]
        z_ref[...] = z
        y_ref[...] = y
        yb_ref[...] = y.astype(BF16)

    row = pl.BlockSpec((bm, D), lambda i: (i, 0))
    vec = pl.BlockSpec((None, 1, D), lambda i: (l, 0, 0))
    return pl.pallas_call(body, grid=(t // bm,),
                          in_specs=[row, pl.BlockSpec((bm, k), lambda i: (i, 0)), _full((k, D)), vec, vec],
                          out_specs=[row, row, row], out_shape=[S((t, D), F32), S((t, D), F32), S((t, D), BF16)],
                          compiler_params=_cp("parallel"), name=name)(x, a, wmat, g3, b3)


def _ln_bwd(d, z, g3, l, name, r=None, dep=None):
    t = z.shape[0]
    bm = _row_tile(t)

    def body(*refs):
        refs = list(refs)
        d_ref = refs.pop(0)
        dy = d_ref[...]
        if r is not None:
            dy = dy + ALPHA * refs.pop(0)[...]
        z_ref, g_ref = refs.pop(0), refs.pop(0)
        if dep is not None:
            refs.pop(0)
        dz_ref, dzb_ref, dg_ref, db_ref = refs
        xh, rstd = _ln_stats(z_ref[...])
        dyg = dy * g_ref[...]
        m1 = jnp.mean(dyg, axis=-1, keepdims=True)
        m2 = jnp.mean(dyg * xh, axis=-1, keepdims=True)
        dz = rstd * (dyg - m1 - xh * m2)
        dz_ref[...] = dz
        dzb_ref[...] = dz.astype(BF16)

        @pl.when(pl.program_id(0) == 0)
        def _():
            dg_ref[...] = jnp.zeros_like(dg_ref)
            db_ref[...] = jnp.zeros_like(db_ref)

        dg_ref[...] += jnp.sum(dy * xh, axis=0, keepdims=True)
        db_ref[...] += jnp.sum(dy, axis=0, keepdims=True)

    row = pl.BlockSpec((bm, D), lambda i: (i, 0))
    vec = pl.BlockSpec((None, 1, D), lambda i: (l, 0, 0))
    acc = pl.BlockSpec((1, D), lambda i: (0, 0))
    ops = [d, z, g3] if r is None else [d, r, z, g3]
    specs = [row, row, vec] if r is None else [row, row, row, vec]
    if dep is not None:
        ops.append(dep)
        specs.append(_full(dep.shape))
    return pl.pallas_call(body, grid=(t // bm,), in_specs=specs, out_specs=[row, row, acc, acc],
                          out_shape=[S((t, D), F32), S((t, D), BF16), S((1, D), F32), S((1, D), F32)],
                          compiler_params=_cp("arbitrary"), name=name)(*ops)


def _loss_grad(y, tgt):
    t = y.shape[0]
    bm = _row_tile(t)

    def body(y_ref, t_ref, dy_ref, loss_ref, acc_ref):
        i = pl.program_id(0)
        e = y_ref[...] - t_ref[...]
        dy_ref[...] = e * (1.0 / D)

        @pl.when(i == 0)
        def _():
            acc_ref[...] = jnp.zeros_like(acc_ref)

        acc_ref[...] += jnp.sum(e * e, axis=0, keepdims=True)

        @pl.when(i == pl.num_programs(0) - 1)
        def _():
            loss_ref[...] = jnp.full(loss_ref.shape, (0.5 / D) * jnp.sum(acc_ref[...]), F32)

    row = pl.BlockSpec((bm, D), lambda i: (i, 0))
    return pl.pallas_call(body, grid=(t // bm,), in_specs=[row, row],
                          out_specs=[row, pl.BlockSpec((1, 128), lambda i: (0, 0))],
                          out_shape=[S((t, D), F32), S((1, 128), F32)],
                          scratch_shapes=[pltpu.VMEM((1, D), F32)],
                          compiler_params=_cp("arbitrary"), name="loss_grad")(y, tgt)


def _mlp_row_tile(t):
    return min(1024, t)


def _mlp_fwd(y, yb, w1g, w2g, g3, b3, l):
    t = yb.shape[0]
    bm = _mlp_row_tile(t)

    def body(y_ref, yb_ref, w1_ref, w2_ref, g_ref, b_ref, z_ref, o_ref, ob_ref, acc_ref):
        j = pl.program_id(1)
        h = jnp.maximum(_dot(yb_ref[...], w1_ref[...]), 0.0)
        c = _dot((h * h).astype(BF16), w2_ref[...])

        @pl.when(j == 0)
        def _():
            acc_ref[...] = c

        @pl.when(j > 0)
        def _():
            acc_ref[...] += c

        @pl.when(j == N_DEV - 1)
        def _():
            z = ALPHA * y_ref[...] + acc_ref[...]
            xh, _ = _ln_stats(z)
            out = xh * g_ref[...] + b_ref[...]
            z_ref[...] = z
            o_ref[...] = out
            ob_ref[...] = out.astype(BF16)

    row = pl.BlockSpec((bm, D), lambda i, j: (i, 0))
    vec = pl.BlockSpec((None, 1, D), lambda i, j: (l, 0, 0))
    return pl.pallas_call(
        body, grid=(t // bm, N_DEV),
        in_specs=[row, row, pl.BlockSpec((None, D, FF_BLK), lambda i, j: (j, 0, 0)),
                  pl.BlockSpec((None, FF_BLK, D), lambda i, j: (j, 0, 0)), vec, vec],
        out_specs=[row, row, row], out_shape=[S((t, D), F32), S((t, D), F32), S((t, D), BF16)],
        scratch_shapes=[pltpu.VMEM((bm, D), F32)],
        compiler_params=_cp("parallel", "arbitrary"), name="mlp_fwd")(y, yb, w1g, w2g, g3, b3)


def _mlp_bwd_dh(yb, dzb, w1g, w2g):
    t = yb.shape[0]
    bm = _mlp_row_tile(t)

    def body(y_ref, dz_ref, w1_ref, w2_ref, a_ref, dh_ref, acc_ref):
        j = pl.program_id(1)
        r = jnp.maximum(_dot(y_ref[...], w1_ref[...]), 0.0)
        a_ref[...] = (r * r).astype(BF16)
        da = _dot_nt(dz_ref[...], w2_ref[...])
        dh = (da * (2.0 * r)).astype(BF16)
        dh_ref[...] = dh
        c = _dot_nt(dh, w1_ref[...])

        @pl.when(j == 0)
        def _():
            acc_ref[...] = c

        @pl.when(j > 0)
        def _():
            acc_ref[...] += c

    row = pl.BlockSpec((bm, D), lambda i, j: (i, 0))
    hid = pl.BlockSpec((bm, FF_BLK), lambda i, j: (i, j))
    return pl.pallas_call(
        body, grid=(t // bm, N_DEV),
        in_specs=[row, row,
                  pl.BlockSpec((None, D, FF_BLK), lambda i, j: (j, 0, 0)),
                  pl.BlockSpec((None, FF_BLK, D), lambda i, j: (j, 0, 0))],
        out_specs=[hid, hid, row],
        out_shape=[S((t, D_FF), BF16), S((t, D_FF), BF16), S((t, D), F32)],
        compiler_params=_cp("parallel", "arbitrary"), name="mlp_bwd_dh")(yb, dzb, w1g, w2g)


def _shift_dn(x, k, rows, fill=0.0):
    return jnp.where(rows >= k, pltpu.roll(x, k, 0), fill)


def _shift_up(x, k, rows, fill=0.0):
    t = x.shape[0]
    return jnp.where(rows < t - k, pltpu.roll(x, t - k, 0), fill)


def _scan_dn(a, b, rows):
    k = 1
    t = a.shape[0]
    while k < t:
        b = a * _shift_dn(b, k, rows) + b
        if 2 * k < t:
            a = a * _shift_dn(a, k, rows, 1.0)
        k *= 2
    return b


def _scan_up(a, b, rows):
    k = 1
    t = a.shape[0]
    while k < t:
        b = a * _shift_up(b, k, rows) + b
        if 2 * k < t:
            a = a * _shift_up(a, k, rows, 1.0)
        k *= 2
    return b


def _window_sum_dn(x, w, rows):
    k = 1
    while k < w:
        x = x + _shift_dn(x, k, rows)
        k *= 2
    return x


def _window_sum_up(x, w, rows):
    k = 1
    while k < w:
        x = x + _shift_up(x, k, rows)
        k *= 2
    return x


def _pool_diff(u, w, rows):
    inv_count = 1.0 / jnp.minimum(rows + 1, w).astype(F32)
    return _window_sum_dn(u, w, rows) * inv_count - u, inv_count


def _pool_fwd(proj, pool_w, pool_scale3, j):
    t = proj.shape[0]

    def body(u_ref, w_ref, s_ref, y_ref):
        rows = lax.broadcasted_iota(jnp.int32, (t, HEAD), 0)
        for g, w in enumerate(POOL_WINDOWS):
            cols = slice(g * HEAD, (g + 1) * HEAD)
            d, _ = _pool_diff(u_ref[:, cols], w, rows)
            y = _dot(d.astype(BF16), w_ref[g].astype(BF16)) * s_ref[:, cols]
            y_ref[:, cols] = y.astype(BF16)

    return pl.pallas_call(
        body, grid=(1,),
        in_specs=[pl.BlockSpec((t, POOL_W), lambda i: (0, 0)),
                  pl.BlockSpec((None, 4, HEAD, HEAD), lambda i: (j, 0, 0, 0)),
                  pl.BlockSpec((None, 1, POOL_W), lambda i: (j, 0, 0))],
        out_specs=pl.BlockSpec((t, POOL_W), lambda i: (0, 0)),
        out_shape=S((t, POOL_W), BF16), compiler_params=_cp("arbitrary"), name="pool_fwd")(proj, pool_w, pool_scale3)


def _pool_bwd(proj, dycat, pool_w, pool_scale3, j):
    t = proj.shape[0]

    def body(u_ref, dy_ref, w_ref, s_ref, du_ref, dw_ref, ds_ref):
        rows = lax.broadcasted_iota(jnp.int32, (t, HEAD), 0)
        for g, w in enumerate(POOL_WINDOWS):
            cols = slice(g * HEAD, (g + 1) * HEAD)
            d, inv_count = _pool_diff(u_ref[:, cols], w, rows)
            db = d.astype(BF16)
            wg = w_ref[g].astype(BF16)
            dy = dy_ref[:, cols]
            ds_ref[:, cols] = jnp.sum(dy * _dot(db, wg), axis=0, keepdims=True)
            dzz = (dy * s_ref[:, cols]).astype(BF16)
            dw_ref[g] = _dot_tn(db, dzz)
            dd = _dot_nt(dzz, wg)
            du_ref[:, cols] = (_window_sum_up(dd * inv_count, w, rows) - dd).astype(BF16)

    return pl.pallas_call(
        body, grid=(1,),
        in_specs=[pl.BlockSpec((t, POOL_W), lambda i: (0, 0)),
                  pl.BlockSpec((t, POOL_W), lambda i: (0, 0)),
                  pl.BlockSpec((None, 4, HEAD, HEAD), lambda i: (j, 0, 0, 0)),
                  pl.BlockSpec((None, 1, POOL_W), lambda i: (j, 0, 0))],
        out_specs=[pl.BlockSpec((t, POOL_W), lambda i: (0, 0)), _full((4, HEAD, HEAD)), _full((1, POOL_W))],
        out_shape=[S((t, POOL_W), BF16), S((4, HEAD, HEAD), F32), S((1, POOL_W), F32)],
        compiler_params=_cp("arbitrary"), name="pool_bwd")(proj, dycat, pool_w, pool_scale3)


GELU_C = 0.7978845608028654
GELU_K = 0.044715


def _gelu(x):
    th = jnp.tanh(GELU_C * (x + GELU_K * x * x * x))
    return 0.5 * x * (1.0 + th), th


def _lru_forward(u, gate, cw, cb, wa, ba, wx, bx, lam, rows):
    v = cw[3:4] * u + cw[2:3] * _shift_dn(u, 1, rows) + cw[1:2] * _shift_dn(u, 2, rows) \
        + cw[0:1] * _shift_dn(u, 3, rows) + cb
    vb = v.astype(BF16)
    r = jax.nn.sigmoid(_dot(vb, wa) + ba)
    i = jax.nn.sigmoid(_dot(vb, wx) + bx)
    sp = jnp.maximum(-lam, 0.0) + jnp.log1p(jnp.exp(-jnp.abs(lam)))
    log_a = (-LRU_C) * r * sp
    a = jnp.exp(log_a)
    one_m_a2 = -jnp.tanh(log_a) * (a * a + 1.0)
    mult = jnp.sqrt(one_m_a2)
    h = _scan_dn(a, mult * (i * v), rows)
    gl, th = _gelu(gate)
    return dict(v=v, vb=vb, r=r, i=i, sp=sp, a=a, mult=mult, h=h, gl=gl, th=th)


def _lru_specs(t, j, col0_u, col0_g):
    blk = lambda c0: pl.BlockSpec((t, HEAD), lambda h: (0, c0 + h))
    vec = pl.BlockSpec((None, 1, HEAD), lambda h: (j, 0, h))
    return [blk(col0_u), blk(col0_g),
            pl.BlockSpec((None, 4, HEAD), lambda h: (j, 0, h)), vec,
            pl.BlockSpec((None, None, HEAD, HEAD), lambda h: (j, h, 0, 0)), vec,
            pl.BlockSpec((None, None, HEAD, HEAD), lambda h: (j, h, 0, 0)), vec, vec]


def _lru_fwd(proj, p, j):
    t = proj.shape[0]

    def body(u_ref, g_ref, cw_ref, cb_ref, wa_ref, ba_ref, wx_ref, bx_ref, lam_ref, y_ref):
        rows = lax.broadcasted_iota(jnp.int32, (t, HEAD), 0)
        f = _lru_forward(u_ref[...], g_ref[...], cw_ref[...], cb_ref[...], wa_ref[...].astype(BF16), ba_ref[...],
                         wx_ref[...].astype(BF16), bx_ref[...], lam_ref[...], rows)
        y_ref[...] = (f["h"] * f["gl"]).astype(BF16)

    return pl.pallas_call(
        body, grid=(LRU_HEADS,), in_specs=_lru_specs(t, j, POOL_W // HEAD, (POOL_W + LRU_W) // HEAD),
        out_specs=pl.BlockSpec((t, HEAD), lambda h: (0, h)), out_shape=S((t, LRU_W), BF16),
        compiler_params=_cp("parallel"), name="lru_fwd")(
            proj, proj, p["conv_w"], p["conv_b"], p["w_a"], p["b_a"], p["w_x"], p["b_x"], p["lam"])


def _lru_bwd(proj, dycat, p, j):
    t = proj.shape[0]

    def body(u_ref, g_ref, cw_ref, cb_ref, wa_ref, ba_ref, wx_ref, bx_ref, lam_ref, dy_ref,
             du_ref, dgate_ref, dcw_ref, dcb_ref, dwa_ref, dba_ref, dwx_ref, dbx_ref, dlam_ref):
        rows = lax.broadcasted_iota(jnp.int32, (t, HEAD), 0)
        u = u_ref[...]
        gate = g_ref[...]
        cw = cw_ref[...]
        wa = wa_ref[...].astype(BF16)
        wx = wx_ref[...].astype(BF16)
        lam = lam_ref[...]
        f = _lru_forward(u, gate, cw, cb_ref[...], wa, ba_ref[...], wx, bx_ref[...], lam, rows)
        v, r, i, a, mult, h, th = f["v"], f["r"], f["i"], f["a"], f["mult"], f["h"], f["th"]
        dy = dy_ref[...]
        dgl = 0.5 * (1.0 + th) + 0.5 * gate * (1.0 - th * th) * GELU_C * (1.0 + 3.0 * GELU_K * gate * gate)
        dgate_ref[...] = (dy * h * dgl).astype(BF16)
        g = _scan_up(_shift_up(a, 1, rows), dy * f["gl"], rows)
        da = g * _shift_dn(h, 1, rows)
        iv = i * v
        dmult = g * iv
        di = g * mult * v
        dv = g * mult * i
        dlog_a = da * a - dmult * (a * a) / mult
        dr = dlog_a * (-LRU_C) * f["sp"]
        dsp = jnp.sum(dlog_a * (-LRU_C) * r, axis=0, keepdims=True)
        dlam_ref[...] = -dsp * jax.nn.sigmoid(-lam)
        dpa = dr * r * (1.0 - r)
        dpx = di * i * (1.0 - i)
        dpab = dpa.astype(BF16)
        dpxb = dpx.astype(BF16)
        dwa_ref[...] = _dot_tn(f["vb"], dpab)
        dwx_ref[...] = _dot_tn(f["vb"], dpxb)
        dba_ref[...] = jnp.sum(dpa, axis=0, keepdims=True)
        dbx_ref[...] = jnp.sum(dpx, axis=0, keepdims=True)
        dv = dv + _dot_nt(dpab, wa) + _dot_nt(dpxb, wx)
        dcb_ref[...] = jnp.sum(dv, axis=0, keepdims=True)
        du = cw[3:4] * dv
        dcw_ref[3:4, :] = jnp.sum(dv * u, axis=0, keepdims=True)
        for k in (1, 2, 3):
            du = du + cw[3 - k:4 - k] * _shift_up(dv, k, rows)
            dcw_ref[3 - k:4 - k, :] = jnp.sum(dv * _shift_dn(u, k, rows), axis=0, keepdims=True)
        du_ref[...] = du.astype(BF16)

    blk = pl.BlockSpec((t, HEAD), lambda h: (0, h))
    vec = pl.BlockSpec((1, HEAD), lambda h: (0, h))
    mat = pl.BlockSpec((None, HEAD, HEAD), lambda h: (h, 0, 0))
    return pl.pallas_call(
        body, grid=(LRU_HEADS,),
        in_specs=_lru_specs(t, j, POOL_W // HEAD, (POOL_W + LRU_W) // HEAD)
        + [pl.BlockSpec((t, HEAD), lambda h: (0, POOL_W // HEAD + h))],
        out_specs=[blk, blk, pl.BlockSpec((4, HEAD), lambda h: (0, h)), vec, mat, vec, mat, vec, vec],
        out_shape=[S((t, LRU_W), BF16), S((t, LRU_W), BF16), S((4, LRU_W), F32), S((1, LRU_W), F32),
                   S((LRU_HEADS, HEAD, HEAD), F32), S((1, LRU_W), F32),
                   S((LRU_HEADS, HEAD, HEAD), F32), S((1, LRU_W), F32), S((1, LRU_W), F32)],
        compiler_params=_cp("parallel"), name="lru_bwd")(
            proj, proj, p["conv_w"], p["conv_b"], p["w_a"], p["b_a"], p["w_x"], p["b_x"], p["lam"], dycat)


def _rope(x, c, s):
    x1 = x[:, :ROPE // 2]
    x2 = x[:, ROPE // 2:]
    return jnp.concatenate([x1 * c - x2 * s, x1 * s + x2 * c], axis=-1)


def _rope_t(d, c, s):
    d1 = d[:, :ROPE // 2]
    d2 = d[:, ROPE // 2:]
    return jnp.concatenate([d1 * c + d2 * s, d2 * c - d1 * s], axis=-1)


def _rope_tables(pos2, inv_freq):
    t = pos2.shape[0]

    def body(p_ref, f_ref, c_ref, s_ref):
        ang = p_ref[...].astype(F32) * f_ref[...]
        c_ref[...] = jnp.cos(ang)
        s_ref[...] = jnp.sin(ang)

    return pl.pallas_call(body, out_shape=[S((t, ROPE // 2), F32), S((t, ROPE // 2), F32)],
                          name="rope_tables")(pos2, inv_freq)


def _down_norm(xb, wdown_g, gq3, gkv3, cos, sin, j):
    t = xb.shape[0]
    bm = _row_tile(t)

    def body(x_ref, w_ref, gq_ref, gkv_ref, c_ref, s_ref, down_ref, cq_ref, ckv_ref, kpe_ref):
        w = w_ref[...].reshape(D, ODD_IN)
        down = _dot(x_ref[...], w)
        down_ref[...] = down
        q = down[:, :Q_RANK]
        cq_ref[...] = (q * lax.rsqrt(jnp.mean(q * q, axis=-1, keepdims=True) + RMS_EPS) * gq_ref[...]).astype(BF16)
        kv = down[:, Q_RANK:Q_RANK + KV_RANK]
        ckv_ref[...] = (kv * lax.rsqrt(jnp.mean(kv * kv, axis=-1, keepdims=True) + RMS_EPS)
                        * gkv_ref[...]).astype(BF16)
        kpe_ref[...] = _rope(down[:, Q_RANK + KV_RANK:], c_ref[...], s_ref[...])

    row = lambda n: pl.BlockSpec((bm, n), lambda i: (i, 0))
    return pl.pallas_call(
        body, grid=(t // bm,),
        in_specs=[row(D), _full((N_DEV, D // N_DEV, ODD_IN)),
                  pl.BlockSpec((None, 1, Q_RANK), lambda i: (j, 0, 0)),
                  pl.BlockSpec((None, 1, KV_RANK), lambda i: (j, 0, 0)), row(ROPE // 2), row(ROPE // 2)],
        out_specs=[row(ODD_IN), row(Q_RANK), row(KV_RANK), row(ROPE)],
        out_shape=[S((t, ODD_IN), F32), S((t, Q_RANK), BF16), S((t, KV_RANK), BF16), S((t, ROPE), F32)],
        compiler_params=_cp("parallel"), name="down_norm")(xb, wdown_g, gq3, gkv3, cos, sin)


def _q_tile(t):
    return min(512, t // 2)


def _attn_probs(q, k, qs):
    s = _dot_nt(q, k) * ATT_SCALE
    tq = q.shape[0]
    rows = lax.broadcasted_iota(jnp.int32, (tq, tq), 0)
    cols = lax.broadcasted_iota(jnp.int32, (tq, tq), 1)
    last = jnp.where(jnp.right_shift(cols, CHUNK_SHIFT) <= jnp.right_shift(rows, CHUNK_SHIFT), s[:, qs:], NEG)
    s = last if qs == 0 else jnp.concatenate([s[:, :qs], last], axis=1)
    e = jnp.exp(s - jnp.max(s, axis=-1, keepdims=True))
    return e / jnp.sum(e, axis=-1, keepdims=True)


def _head_qkv(cq, ckv, kpe, c, s, wq_ref, wkv_ref):
    q = jnp.concatenate([_dot(cq, wq_ref[:, :NOPE]), _rope(_dot(cq, wq_ref[:, NOPE:]), c, s)], axis=1).astype(BF16)
    k = jnp.concatenate([_dot(ckv, wkv_ref[:, :NOPE]), kpe], axis=1).astype(BF16)
    vv = _dot(ckv, wkv_ref[:, NOPE:]).astype(BF16)
    return q, k, vv


def _attn_in_specs(t):
    return [_full((t, Q_RANK)), _full((t, KV_RANK)), _full((t, ROPE)), _full((t, ROPE // 2)), _full((t, ROPE // 2)),
            pl.BlockSpec((None, Q_RANK, NOPE + ROPE), lambda h: (h, 0, 0)),
            pl.BlockSpec((None, KV_RANK, NOPE + VDIM), lambda h: (h, 0, 0)),
            pl.BlockSpec((None, VDIM, D), lambda h: (h, 0, 0))]


def _attn_fwd(cq, ckv, kpe, cos, sin, wqb_g, wkvb_g, wo_g):
    t = cq.shape[0]
    tq = _q_tile(t)

    def body(cq_ref, ckv_ref, kpe_ref, c_ref, s_ref, wq_ref, wkv_ref, wo_ref, o_ref, mix_ref):
        q, k, vv = _head_qkv(cq_ref[...], ckv_ref[...], kpe_ref[...], c_ref[...], s_ref[...], wq_ref, wkv_ref)
        for qs in range(0, t, tq):
            ke = qs + tq
            p = _attn_probs(q[qs:ke], k[:ke], qs)
            o_ref[qs:ke, :] = _dot(p.astype(BF16), vv[:ke]).astype(BF16)
        c = _dot(o_ref[...], wo_ref[...])

        @pl.when(pl.program_id(0) == 0)
        def _():
            mix_ref[...] = c

        @pl.when(pl.program_id(0) > 0)
        def _():
            mix_ref[...] += c

    return pl.pallas_call(
        body, grid=(MLA_HEADS,), in_specs=_attn_in_specs(t),
        out_specs=[pl.BlockSpec((None, t, VDIM), lambda h: (h, 0, 0)), _full((t, D))],
        out_shape=[S((MLA_HEADS, t, VDIM), BF16), S((t, D), F32)],
        compiler_params=_cp("arbitrary"), name="attn_fwd")(cq, ckv, kpe, cos, sin, wqb_g, wkvb_g, wo_g)


def _attn_bwd(cq, ckv, kpe, cos, sin, wqb_g, wkvb_g, wo_g, o, dzb):
    t = cq.shape[0]
    tq = _q_tile(t)

    def body(cq_ref, ckv_ref, kpe_ref, c_ref, s_ref, wq_ref, wkv_ref, wo_ref, o_ref, dz_ref,
             dwo_ref, dwq_ref, dwkv_ref, dcq_ref, dckv_ref, dkpe_ref, dk_s, dv_s, dq_s):
        cqv = cq_ref[...]
        ckvv = ckv_ref[...]
        c = c_ref[...]
        s = s_ref[...]
        q, k, vv = _head_qkv(cqv, ckvv, kpe_ref[...], c, s, wq_ref, wkv_ref)
        dzv = dz_ref[...]
        dwo_ref[...] = _dot_tn(o_ref[...], dzv).astype(BF16)
        do = _dot_nt(dzv, wo_ref[...]).astype(BF16)
        dk_s[...] = jnp.zeros_like(dk_s)
        dv_s[...] = jnp.zeros_like(dv_s)
        for qs in range(0, t, tq):
            ke = qs + tq
            p = _attn_probs(q[qs:ke], k[:ke], qs)
            dp = _dot_nt(do[qs:ke], vv[:ke])
            ds = (p * (dp - jnp.sum(p * dp, axis=-1, keepdims=True)) * ATT_SCALE).astype(BF16)
            dq_s[qs:ke, :] = _dot(ds, k[:ke])
            dk_s[0:ke, :] += _dot_tn(ds, q[qs:ke])
            dv_s[0:ke, :] += _dot_tn(p.astype(BF16), do[qs:ke])
        dqn = dq_s[:, :NOPE].astype(BF16)
        dqp = _rope_t(dq_s[:, NOPE:], c, s).astype(BF16)
        dkn = dk_s[:, :NOPE].astype(BF16)
        dvv = dv_s[...].astype(BF16)
        dwq_ref[:, :NOPE] = _dot_tn(cqv, dqn).astype(BF16)
        dwq_ref[:, NOPE:] = _dot_tn(cqv, dqp).astype(BF16)
        dwkv_ref[:, :NOPE] = _dot_tn(ckvv, dkn).astype(BF16)
        dwkv_ref[:, NOPE:] = _dot_tn(ckvv, dvv).astype(BF16)
        dcq = _dot_nt(dqn, wq_ref[:, :NOPE]) + _dot_nt(dqp, wq_ref[:, NOPE:])
        dckv = _dot_nt(dkn, wkv_ref[:, :NOPE]) + _dot_nt(dvv, wkv_ref[:, NOPE:])

        @pl.when(pl.program_id(0) == 0)
        def _():
            dcq_ref[...] = dcq
            dckv_ref[...] = dckv
            dkpe_ref[...] = dk_s[:, NOPE:]

        @pl.when(pl.program_id(0) > 0)
        def _():
            dcq_ref[...] += dcq
            dckv_ref[...] += dckv
            dkpe_ref[...] += dk_s[:, NOPE:]

    per_head = lambda a, b: pl.BlockSpec((None, a, b), lambda h: (h, 0, 0))
    return pl.pallas_call(
        body, grid=(MLA_HEADS,),
        in_specs=_attn_in_specs(t) + [per_head(t, VDIM), _full((t, D))],
        out_specs=[per_head(VDIM, D), per_head(Q_RANK, NOPE + ROPE), per_head(KV_RANK, NOPE + VDIM),
                   _full((t, Q_RANK)), _full((t, KV_RANK)), _full((t, ROPE))],
        out_shape=[S((MLA_HEADS, VDIM, D), BF16), S((MLA_HEADS, Q_RANK, NOPE + ROPE), BF16),
                   S((MLA_HEADS, KV_RANK, NOPE + VDIM), BF16),
                   S((t, Q_RANK), F32), S((t, KV_RANK), F32), S((t, ROPE), F32)],
        scratch_shapes=[pltpu.VMEM((t, NOPE + ROPE), F32), pltpu.VMEM((t, VDIM), F32),
                        pltpu.VMEM((t, NOPE + ROPE), F32)],
        compiler_params=_cp("arbitrary"), name="attn_bwd")(cq, ckv, kpe, cos, sin, wqb_g, wkvb_g, wo_g, o, dzb)


def _rms_bwd(down, dcq, dckv, dkpe, cos, sin, gq3, gkv3, j):
    t = down.shape[0]
    bm = _row_tile(t)

    def body(down_ref, dcq_ref, dckv_ref, dkpe_ref, c_ref, s_ref, gq_ref, gkv_ref, dd_ref, dgq_ref, dgkv_ref):
        @pl.when(pl.program_id(0) == 0)
        def _():
            dgq_ref[...] = jnp.zeros_like(dgq_ref)
            dgkv_ref[...] = jnp.zeros_like(dgkv_ref)

        def rms_b(x, dy, g):
            rstd = lax.rsqrt(jnp.mean(x * x, axis=-1, keepdims=True) + RMS_EPS)
            xh = x * rstd
            dyg = dy * g
            return rstd * (dyg - xh * jnp.mean(dyg * xh, axis=-1, keepdims=True)), jnp.sum(dy * xh, axis=0, keepdims=True)

        dq, dgq = rms_b(down_ref[:, :Q_RANK], dcq_ref[...], gq_ref[...])
        dkv, dgkv = rms_b(down_ref[:, Q_RANK:Q_RANK + KV_RANK], dckv_ref[...], gkv_ref[...])
        dgq_ref[...] += dgq
        dgkv_ref[...] += dgkv
        dd_ref[:, :Q_RANK] = dq.astype(BF16)
        dd_ref[:, Q_RANK:Q_RANK + KV_RANK] = dkv.astype(BF16)
        dd_ref[:, Q_RANK + KV_RANK:] = _rope_t(dkpe_ref[...], c_ref[...], s_ref[...]).astype(BF16)

    row = lambda n: pl.BlockSpec((bm, n), lambda i: (i, 0))
    return pl.pallas_call(
        body, grid=(t // bm,),
        in_specs=[row(ODD_IN), row(Q_RANK), row(KV_RANK), row(ROPE), row(ROPE // 2), row(ROPE // 2),
                  pl.BlockSpec((None, 1, Q_RANK), lambda i: (j, 0, 0)),
                  pl.BlockSpec((None, 1, KV_RANK), lambda i: (j, 0, 0))],
        out_specs=[row(ODD_IN), _full((1, Q_RANK)), _full((1, KV_RANK))],
        out_shape=[S((t, ODD_IN), BF16), S((1, Q_RANK), F32), S((1, KV_RANK), F32)],
        compiler_params=_cp("arbitrary"), name="rms_bwd")(down, dcq, dckv, dkpe, cos, sin, gq3, gkv3)


def _col_blocks(t, n, bn):
    return pl.BlockSpec((t, bn), lambda i: (0, i))


def _row_blocks(n, bm):
    return pl.BlockSpec((bm, n), lambda i: (i, 0))


def _local_step(x, pos2, tgt, small, weights_of, grads_done, start_dep=None):
    t = x.shape[0]
    bm = _row_tile(t)
    inv_freq = (ROPE_THETA ** (-jnp.arange(0, ROPE, 2, dtype=F32) / ROPE)).reshape(1, ROPE // 2)
    cos, sin = _rope_tables(pos2, inv_freq)
    lru_p = {k: small[k] for k in ("conv_w", "conv_b", "w_a", "b_a", "w_x", "b_x", "lam")}

    saved = []
    y, yb = x, x.astype(BF16)
    for l in range(DEPTH):
        j = l // 2
        big = weights_of(l, 0, y)
        sv = dict(xb=yb, big=big)
        if l % 2 == 0:
            proj = _mm(yb, big["win2d"], mode="nn", grid=(EVEN_IN // 512,), a_spec=_full((t, D)),
                       b_spec=_col_blocks(D, EVEN_IN, 512), out_shape=S((t, EVEN_IN), F32),
                       out_spec=_col_blocks(t, EVEN_IN, 512), name="even_proj", dep=start_dep if l == 0 else None)
            ycat = jnp.concatenate([_pool_fwd(proj, small["pool_w"], small["pool_scale"], j),
                                    _lru_fwd(proj, lru_p, j)], axis=1)
            big.update(weights_of(l, 1, ycat))
            z1, y1, y1b = _proj_resid_ln(y, ycat, big["wout2d"], small["ln_mix_g"], small["ln_mix_b"], l, "even_out")
            sv.update(proj=proj, ycat=ycat)
        else:
            down, cq, ckv, kpe = _down_norm(yb, big["wdown"], small["gq"], small["gkv"], cos, sin, j)
            o, mix = _attn_fwd(cq, ckv, kpe, cos, sin, big["wqb"], big["wkvb"], big["wo"])
            z1, y1, y1b = _resid_ln(y, mix, small["ln_mix_g"], small["ln_mix_b"], l, "resid_ln")
            sv.update(down=down, cq=cq, ckv=ckv, kpe=kpe, o=o)
        z2, y, yb = _mlp_fwd(y1, y1b, big["w1"], big["w2"], small["ln_ffn_g"], small["ln_ffn_b"], l)
        sv.update(z1=z1, y1b=y1b, z2=z2)
        saved.append(sv)

    dy, loss_tile = _loss_grad(y, tgt)

    g = {k: [None] * n for k, n in (("ln_mix_g", 4), ("ln_mix_b", 4), ("ln_ffn_g", 4), ("ln_ffn_b", 4),
                                    ("pool_w", 2), ("pool_scale", 2), ("conv_w", 2), ("conv_b", 2),
                                    ("w_a", 2), ("b_a", 2), ("w_x", 2), ("b_x", 2), ("lam", 2),
                                    ("gq", 2), ("gkv", 2))}
    dep = None
    for l in reversed(range(DEPTH)):
        j = l // 2
        sv = saved[l]
        big = sv["big"]
        dz2, dz2b, g["ln_ffn_g"][l], g["ln_ffn_b"][l] = _ln_bwd(dy, sv["z2"], small["ln_ffn_g"], l, "ln_bwd", dep=dep)
        act, dh, dff = _mlp_bwd_dh(sv["y1b"], dz2b, big["w1"], big["w2"])
        dw1 = _mm(sv["y1b"], dh, mode="tn", grid=(N_DEV,), a_spec=_full((t, D)),
                  b_spec=_col_blocks(t, D_FF, FF_BLK), out_shape=S((N_DEV, D, FF_BLK), BF16),
                  out_spec=pl.BlockSpec((None, D, FF_BLK), lambda i: (i, 0, 0)), name="mlp_dw1")
        dw2 = _mm(act, dz2b, mode="tn", grid=(N_DEV,), a_spec=_col_blocks(t, D_FF, FF_BLK),
                  b_spec=_full((t, D)), out_shape=S((N_DEV, FF_BLK, D), BF16),
                  out_spec=pl.BlockSpec((None, FF_BLK, D), lambda i: (i, 0, 0)), name="mlp_dw2")
        dep = grads_done(l, dict(w1=dw1, w2=dw2))
        dz1, dz1b, g["ln_mix_g"][l], g["ln_mix_b"][l] = _ln_bwd(dff, sv["z1"], small["ln_mix_g"], l, "ln_bwd_res",
                                                                 r=dz2, dep=dep)
        if l % 2 == 0:
            wout = big["wout2d"]
            dycat = _mm(dz1b, wout, mode="nt", grid=(EVEN_MIX // 512,), a_spec=_full((t, D)),
                        b_spec=_row_blocks(D, 512), out_shape=S((t, EVEN_MIX), F32),
                        out_spec=_col_blocks(t, EVEN_MIX, 512), name="even_dycat")
            dwout = _mm(sv["ycat"], dz1b, mode="tn", grid=(EVEN_MIX // 512,), a_spec=_col_blocks(t, EVEN_MIX, 512),
                        b_spec=_full((t, D)), out_shape=S((EVEN_MIX, D), BF16), out_spec=_row_blocks(D, 512),
                        name="even_dwout")
            du_pool, g["pool_w"][j], g["pool_scale"][j] = _pool_bwd(sv["proj"], dycat, small["pool_w"],
                                                                   small["pool_scale"], j)
            (du_lru, du_gate, g["conv_w"][j], g["conv_b"][j], g["w_a"][j], g["b_a"][j], g["w_x"][j], g["b_x"][j],
             g["lam"][j]) = _lru_bwd(sv["proj"], dycat, lru_p, j)
            dproj = jnp.concatenate([du_pool, du_lru, du_gate], axis=1)
            dwin = _mm(sv["xb"], dproj, mode="tn", grid=(EVEN_IN // 512,), a_spec=_full((t, D)),
                       b_spec=_col_blocks(t, EVEN_IN, 512), out_shape=S((D, EVEN_IN), BF16),
                       out_spec=_col_blocks(D, EVEN_IN, 512), name="even_dwin")
            dep = grads_done(l, dict(win=dwin.reshape(D, N_DEV, EVEN_IN // N_DEV).transpose(1, 0, 2),
                                     wout=dwout.reshape(N_DEV, EVEN_MIX // N_DEV, D)))
            dy = _mm(dproj, big["win2d"], mode="nt", grid=(t // bm,), a_spec=_row_blocks(EVEN_IN, bm),
                     b_spec=_full((D, EVEN_IN)), out_shape=S((t, D), F32), out_spec=_row_blocks(D, bm),
                     add=dz1, add_spec=_row_blocks(D, bm), add_scale=ALPHA, name="even_dx")
        else:
            dwo, dwqb, dwkvb, dcq, dckv, dkpe = _attn_bwd(
                sv["cq"], sv["ckv"], sv["kpe"], cos, sin, big["wqb"], big["wkvb"], big["wo"], sv["o"], dz1b)
            ddown, g["gq"][j], g["gkv"][j] = _rms_bwd(sv["down"], dcq, dckv, dkpe, cos, sin, small["gq"],
                                                     small["gkv"], j)
            dwdown = _mm(sv["xb"], ddown, mode="tn", grid=(N_DEV,), a_spec=_col_blocks(t, D, D // N_DEV),
                         b_spec=_full((t, ODD_IN)), out_shape=S((N_DEV, D // N_DEV, ODD_IN), BF16),
                         out_spec=pl.BlockSpec((None, D // N_DEV, ODD_IN), lambda i: (i, 0, 0)),
                         name="odd_dwdown")
            dep = grads_done(l, dict(wdown=dwdown, wqb=dwqb, wkvb=dwkvb, wo=dwo))
            dy = _mm(ddown, big["wdown2d"], mode="nt", grid=(t // bm,), a_spec=_row_blocks(ODD_IN, bm),
                     b_spec=_full((D, ODD_IN)), out_shape=S((t, D), F32), out_spec=_row_blocks(D, bm),
                     add=dz1, add_spec=_row_blocks(D, bm), add_scale=ALPHA, name="odd_dx")
    return loss_tile[0, 0], dy, g


def _mesh_place():
    x, y, c = lax.axis_index("x"), lax.axis_index("y"), lax.axis_index("c")
    return x, y, c


def _peer(place, k):
    x, y, c = place
    return (1 - x if k & 4 else x, 1 - y if k & 2 else y, 1 - c if k & 1 else c)


def _index(place):
    x, y, c = place
    return 4 * x + 2 * y + c


ANY = pl.BlockSpec(memory_space=pl.ANY)


def _all_gather_big(zones):
    n = len(zones)

    def body(*refs):
        outs = refs[n:2 * n]
        send, recv = refs[2 * n:]
        x, y, c = _mesh_place()
        me, sibling = (x, y, c), (x, y, 1 - c)
        chips = [(1 - x, y), (x, 1 - y), (1 - x, 1 - y)]

        def copy(w, k, block, to):
            blk = outs[w].at[_index(block)]
            return pltpu.make_async_remote_copy(src_ref=blk, dst_ref=blk, send_sem=send.at[w, k], recv_sem=recv.at[w, k],
                                                device_id=to, device_id_type=MESH)

        first = []
        for w in range(n):
            first.append(copy(w, 0, me, sibling))
            first += [copy(w, 1 + j, me, (*chip, c)) for j, chip in enumerate(chips)]
        for cp in first:
            cp.start()
        passed = []
        for w in range(n):
            for j, chip in enumerate(chips):
                copy(w, 1 + j, (*chip, c), me).wait_recv()
                cp = copy(w, 4 + j, (*chip, c), sibling)
                cp.start()
                passed.append(cp)
        for w in range(n):
            copy(w, 0, sibling, me).wait_recv()
            for j, chip in enumerate(chips):
                copy(w, 4 + j, (*chip, 1 - c), me).wait_recv()
        for cp in first + passed:
            cp.wait_send()

    return pl.pallas_call(
        body, in_specs=[ANY] * n, out_specs=[ANY] * n, out_shape=[S(z.shape, z.dtype) for z in zones],
        input_output_aliases={i: i for i in range(n)},
        scratch_shapes=[pltpu.SemaphoreType.DMA((n, N_DEV - 1)), pltpu.SemaphoreType.DMA((n, N_DEV - 1))],
        compiler_params=pltpu.CompilerParams(has_side_effects=True), name="all_gather_big")(*zones)


def _shard_rows_tile(a):
    return max(d for d in range(16, 257, 16) if a % d == 0)


HBM = pl.BlockSpec(memory_space=pltpu.HBM)
SEM = pl.BlockSpec(memory_space=pltpu.SEMAPHORE)
DATAFLOW = pltpu.SideEffectType.DATAFLOW_SIDE_EFFECTING


def _in_hbm(a):
    return pltpu.with_memory_space_constraint(a, pltpu.HBM)


def _gather_ici_copies(place, src, land, w):
    me = _index(place)
    return [(_peer(place, k), land.at[me], land.at[me]) for k in (1, 2, 4, 6)]


def _gather_d2d_copies(place, src, land, w):
    blocks = [_index(_peer(place, k)) for k in (2, 4, 6)]
    return [(_peer(place, 1), land.at[b], land.at[b]) for b in blocks]


GATHER_ICI = (4, _gather_ici_copies)
GATHER_D2D = (3, _gather_d2d_copies)


def _scatter_plan(layers):
    def copies(place, src, land, w):
        me = _index(place)
        return [(_peer(place, k), src.at[_index(_peer(place, k))], land.at[me, layers[w]]) for k in range(1, N_DEV)]
    return (N_DEV - 1, copies)


def _exchange_start(srcs, lands, plan, name, after=()):
    ns, n = len(srcs), len(lands)
    n_in = ns + n + len(after)
    per, copies = plan

    def body(*refs):
        ins, land = refs[:ns], refs[ns:ns + n]
        send, recv = refs[n_in], refs[n_in + 1]
        token = refs[-1]
        place = _mesh_place()
        for i in range(per):
            for w in range(n):
                target, src, dst = copies(place, ins[w] if ns else None, land[w], w)[i]
                pltpu.make_async_remote_copy(src_ref=src, dst_ref=dst, send_sem=send.at[w * per + i],
                                             recv_sem=recv.at[w * per + i], device_id=target, device_id_type=MESH).start()
        token[...] = jnp.zeros_like(token)

    sems = pltpu.SemaphoreType.DMA((n * per,))
    thru = [pltpu.HBM(a.shape, a.dtype) for a in list(srcs) + list(lands)]
    out = pl.pallas_call(
        body, name=name, in_specs=[HBM] * (ns + n) + [ANY] * len(after),
        out_shape=(sems, sems, *thru, S((8, 128), F32)),
        out_specs=(SEM, SEM, *([HBM] * (ns + n)), pl.BlockSpec(memory_space=pltpu.VMEM)),
        input_output_aliases={i: 2 + i for i in range(ns + n)},
        compiler_params=pltpu.CompilerParams(has_side_effects=DATAFLOW),
    )(*[_in_hbm(a) for a in list(srcs) + list(lands)], *after)
    return out[0], out[1], list(out[2:2 + ns]), list(out[2 + ns:2 + ns + n]), out[-1]


def _exchange_wait(send, recv, srcs, lands, plan, after, name):
    ns, n = len(srcs), len(lands)
    per, copies = plan

    def body(*refs):
        ins, land = refs[:ns], refs[ns:ns + n]
        send_ref, recv_ref = refs[ns + n], refs[ns + n + 1]
        place = _mesh_place()
        for i in range(per):
            for w in range(n):
                target, src, dst = copies(place, ins[w] if ns else None, land[w], w)[i]
                cp = pltpu.make_async_remote_copy(src_ref=src, dst_ref=dst, send_sem=send_ref.at[w * per + i],
                                                  recv_sem=recv_ref.at[w * per + i], device_id=target,
                                                  device_id_type=MESH)
                cp.wait_send()
                cp.wait_recv()

    thru = [pltpu.HBM(a.shape, a.dtype) for a in list(srcs) + list(lands)]
    out = pl.pallas_call(
        body, name=name, in_specs=[HBM] * (ns + n) + [SEM, SEM, ANY],
        out_shape=tuple(thru), out_specs=tuple([HBM] * (ns + n)),
        input_output_aliases={i: i for i in range(ns + n)},
        compiler_params=pltpu.CompilerParams(has_side_effects=DATAFLOW),
    )(*srcs, *lands, send, recv, after)
    return list(out[:ns]), list(out[ns:])


def _all_reduce_small(part, name, deps=()):
    def body(*refs):
        p_ref = refs[0]
        o_ref, rbuf, send1, recv1, send2, recv2 = refs[-6:]
        place = _mesh_place()
        me = _index(place)
        rbuf[pl.ds(me, 1)] = p_ref[pl.ds(me, 1)]
        first = [pltpu.make_async_remote_copy(src_ref=p_ref.at[_index(_peer(place, k))], dst_ref=rbuf.at[me],
                                              send_sem=send1.at[k - 1], recv_sem=recv1.at[k - 1],
                                              device_id=_peer(place, k), device_id_type=MESH)
                 for k in range(1, N_DEV)]
        for cp in first:
            cp.start()
        for cp in first:
            cp.wait()
        acc = rbuf[0]
        for d in range(1, N_DEV):
            acc = acc + rbuf[d]
        o_ref[pl.ds(me, 1)] = acc[None]
        second = [pltpu.make_async_remote_copy(src_ref=o_ref.at[me], dst_ref=o_ref.at[me], send_sem=send2.at[k - 1],
                                               recv_sem=recv2.at[k - 1], device_id=_peer(place, k),
                                               device_id_type=MESH)
                  for k in range(1, N_DEV)]
        for cp in second:
            cp.start()
        for cp in second:
            cp.wait()

    vm = pl.BlockSpec(memory_space=pltpu.VMEM)
    ops = [part, *deps]
    return pl.pallas_call(
        body, in_specs=[vm] + [ANY] * len(deps), out_specs=vm, out_shape=S(part.shape, F32),
        scratch_shapes=[pltpu.VMEM(part.shape, F32)] + [pltpu.SemaphoreType.DMA((N_DEV - 1,))] * 4,
        compiler_params=pltpu.CompilerParams(has_side_effects=True, vmem_limit_bytes=VMEM_LIMIT), name=name)(*ops)


def _adamw(w, g, m, v):
    m = ADAM_B1 * m + (1.0 - ADAM_B1) * g
    v = ADAM_B2 * v + (1.0 - ADAM_B2) * (g * g)
    m_hat = m / (1.0 - ADAM_B1 ** ADAM_STEP)
    v_hat = v / (1.0 - ADAM_B2 ** ADAM_STEP)
    return -ADAM_LR * (m_hat / (jnp.sqrt(v_hat) + ADAM_EPS) + ADAM_WD * w), m, v


def _adam_big(parts, own, me, w, m, v, name):
    nl, a, b = w.shape
    ta = _shard_rows_tile(a)

    def body(me_ref, p_ref, *refs):
        own_refs, (w_ref, m_ref, v_ref, g_ref, d_ref, mo_ref, vo_ref) = refs[:nl], refs[nl:]
        layer = pl.program_id(0)
        mine = own_refs[0][...]
        for k in range(1, nl):
            mine = jnp.where(layer == k, own_refs[k][...], mine)
        g = None
        for s in range(N_DEV):
            term = jnp.where(me_ref[0] == s, mine, p_ref[s]).astype(F32)
            g = term if g is None else g + term
        g_ref[...] = g
        d_ref[...], mo_ref[...], vo_ref[...] = _adamw(w_ref[...], g, m_ref[...], v_ref[...])

    blk = pl.BlockSpec((None, ta, b), lambda l, i, me_ref: (l, i, 0))

    def own_spec(k):
        return pl.BlockSpec((None, ta, b), lambda l, i, me_ref: (me_ref[0], jnp.where(l == k, i, 0), 0))

    grid_spec = pltpu.PrefetchScalarGridSpec(
        num_scalar_prefetch=1, grid=(nl, a // ta),
        in_specs=[pl.BlockSpec((N_DEV, None, ta, b), lambda l, i, me_ref: (0, l, i, 0))]
        + [own_spec(k) for k in range(nl)] + [blk, blk, blk],
        out_specs=[blk] * 4)
    return pl.pallas_call(body, grid_spec=grid_spec, out_shape=[S(w.shape, F32)] * 4,
                          compiler_params=_cp("arbitrary", "arbitrary"), name=name)(me, parts, *own, w, m, v)


def _adam_small(g, w, m, v, name):
    def body(g_ref, w_ref, m_ref, v_ref, d_ref, mo_ref, vo_ref):
        d_ref[...], mo_ref[...], vo_ref[...] = _adamw(w_ref[...], g_ref[...], m_ref[...], v_ref[...])

    return pl.pallas_call(body, out_shape=[S(g.shape, F32)] * 3, compiler_params=_cp(), name=name)(g, w, m, v)


BIG = ("even_w_in", "even_w_out", "mla_w_down", "mla_w_qb", "mla_w_kvb", "mla_w_o", "mlp_w1", "mlp_w2")
BIG_KEY = dict(even_w_in="win", even_w_out="wout", mla_w_down="wdown", mla_w_qb="wqb", mla_w_kvb="wkvb",
               mla_w_o="wo", mlp_w1="w1", mlp_w2="w2")
SMALL = (("ln_mix_g", "ln_mix_g", None), ("ln_mix_b", "ln_mix_b", None), ("ln_ffn_g", "ln_ffn_g", None),
         ("ln_ffn_b", "ln_ffn_b", None), ("pool_w", "pool_w", None), ("pool_scale", "pool_scale", None),
         ("lru_conv_w", "conv_w", 2), ("lru_conv_b", "conv_b", None), ("lru_w_a", "w_a", None),
         ("lru_b_a", "b_a", None), ("lru_w_x", "w_x", None), ("lru_b_x", "b_x", None), ("lru_lambda", "lam", None),
         ("mla_q_norm_g", "gq", 1), ("mla_kv_norm_g", "gkv", 1))
WEIGHTS = ("ln_mix_g", "ln_mix_b", "ln_ffn_g", "ln_ffn_b", "even_w_in", "pool_w", "pool_scale", "lru_conv_w",
           "lru_conv_b", "lru_w_a", "lru_b_a", "lru_w_x", "lru_b_x", "lru_lambda", "even_w_out", "mla_w_down",
           "mla_q_norm_g", "mla_kv_norm_g", "mla_w_qb", "mla_w_kvb", "mla_w_o", "mlp_w1", "mlp_w2")
ALL_AXES = ("x", "y", "c")


def _layer_weights(l):
    j = l // 2
    if l % 2 == 0:
        mixer = [("win", "even_w_in", j), ("wout", "even_w_out", j)]
    else:
        mixer = [("wdown", "mla_w_down", j), ("wqb", "mla_w_qb", j), ("wkvb", "mla_w_kvb", j), ("wo", "mla_w_o", j)]
    return mixer + [("w1", "mlp_w1", l), ("w2", "mlp_w2", l)]


def _pack(arrays, multiple):
    flat = jnp.concatenate([a.reshape(-1) for a in arrays])
    pad = (-flat.shape[0]) % multiple
    return jnp.pad(flat, (0, pad))


def _unpack(flat, shapes):
    out, at = [], 0
    for shp in shapes:
        n = 1
        for s in shp:
            n *= s
        out.append(flat[at:at + n].reshape(shp))
        at += n
    return out


def _global_shape(local_shape, axis):
    if axis is None:
        return tuple(local_shape)
    return tuple(s * N_DEV if i == axis else s for i, s in enumerate(local_shape))


def _step(x, positions, tgt, w, m, v):
    t = x.shape[1]
    me = _index(_mesh_place())

    sharded = [(name, axis) for name, _, axis in SMALL if axis is not None]
    zeros_with_mine = [lax.dynamic_update_slice_in_dim(jnp.zeros(_global_shape(w[name].shape, axis), F32), w[name],
                                                       me * w[name].shape[axis], axis) for name, axis in sharded]
    chunk = N_DEV * 8 * 128
    gathered = _all_reduce_small(_pack(zeros_with_mine, chunk).reshape(N_DEV, -1, 128), "gather_small")
    full = dict(zip([name for name, _ in sharded],
                    _unpack(gathered.reshape(-1), [_global_shape(w[name].shape, axis) for name, axis in sharded])))

    def zone_of(shard):
        return lax.dynamic_update_slice_in_dim(lax.empty((N_DEV,) + shard.shape, BF16), shard.astype(BF16)[None], me, 0)

    def keys_of(l, part):
        keys = [key for key, _, _ in _layer_weights(l)]
        if l == 0:
            return keys[:1] if part == 0 else keys[1:]
        return keys if part == 0 else []

    shard_of = {(l, key): w[name][i] for l in range(DEPTH) for key, name, i in _layer_weights(l)}
    first = _all_gather_big([zone_of(shard_of[0, key]) for key in keys_of(0, 0)])
    flights, after = {}, (first[0], gathered)
    for l in range(DEPTH):
        for part in (0, 1):
            if (l, part) != (0, 0) and keys_of(l, part):
                zones = [zone_of(shard_of[l, key]) for key in keys_of(l, part)]
                send, recv, _, lands, token = _exchange_start([], zones, GATHER_ICI, "gather_start_%d_%d" % (l, part),
                                                              after=after)
                flights[l, part] = (send, recv, [], lands)
                after = (token,)

    def weights_of(l, part, after):
        keys = keys_of(l, part)
        if (l, part) == (0, 0):
            arrays = first
        elif keys:
            tag = "%d_%d" % (l, part)
            _, lands = _exchange_wait(*flights[l, part], GATHER_ICI, after, "gather_wait_" + tag)
            send, recv, _, lands, _ = _exchange_start([], lands, GATHER_D2D, "gather_pass_" + tag)
            _, arrays = _exchange_wait(send, recv, [], lands, GATHER_D2D, after, "gather_pass_wait_" + tag)
        big = dict(zip(keys, arrays)) if keys else {}
        if "win" in big:
            big["win2d"] = big["win"].transpose(1, 0, 2).reshape(D, EVEN_IN)
        if "wout" in big:
            big["wout2d"] = big["wout"].reshape(EVEN_MIX, D)
        if "wdown" in big:
            big["wdown2d"] = big["wdown"].reshape(D, ODD_IN)
        return big

    zone = {name: lax.empty((N_DEV,) + w[name].shape, BF16) for name in BIG}
    name_of = {key: name for name, key in BIG_KEY.items()}
    sent, last_token = [], [None]

    def grads_done(l, grads):
        keys = list(grads)
        index = {key: i for key, _, i in _layer_weights(l)}
        layers = [index[key] for key in keys]
        send, recv, srcs, lands, tok = _exchange_start([grads[k] for k in keys], [zone[name_of[k]] for k in keys],
                                                       _scatter_plan(layers), "scatter_start_%d_%s" % (l, keys[0]))
        for k, land in zip(keys, lands):
            zone[name_of[k]] = land
        sent.append((send, recv, srcs, keys, layers))
        last_token[0] = tok
        return tok

    row3 = lambda a: a.reshape(a.shape[0], 1, a.shape[1])
    small = dict(ln_mix_g=row3(w["ln_mix_g"]), ln_mix_b=row3(w["ln_mix_b"]), ln_ffn_g=row3(w["ln_ffn_g"]),
                 ln_ffn_b=row3(w["ln_ffn_b"]), pool_w=w["pool_w"], pool_scale=row3(w["pool_scale"]),
                 conv_w=full["lru_conv_w"], conv_b=row3(w["lru_conv_b"]), w_a=w["lru_w_a"], b_a=row3(w["lru_b_a"]),
                 w_x=w["lru_w_x"], b_x=row3(w["lru_b_x"]), lam=row3(w["lru_lambda"]),
                 gq=row3(full["mla_q_norm_g"]), gkv=row3(full["mla_kv_norm_g"]))

    loss_part, grad_x, g = _local_step(x[0], positions.reshape(t, 1), tgt[0], small, weights_of, grads_done,
                                       start_dep=token)

    own = {name: [None] * w[name].shape[0] for name in BIG}
    me_arr = me.astype(jnp.int32).reshape(1)
    out = {}
    local_g = [jnp.stack(g[key]).reshape(_global_shape(w[name].shape, axis)) for name, key, axis in SMALL]
    local_g.append(loss_part.reshape(1))
    after = last_token[0]
    for n_flight, (send, recv, srcs, keys, layers) in enumerate(sent):
        if n_flight == len(sent) - 1:
            for name in BIG:
                if BIG_KEY[name] not in keys:
                    out[name] = _adam_big(zone[name], own[name], me_arr, w[name], m[name], v[name], "adam_" + name)
            reduced = _all_reduce_small(_pack(local_g, chunk).reshape(N_DEV, -1, 128), "all_reduce_small",
                                        deps=[o[0] for o in out.values()])
            after = reduced
        srcs, lands = _exchange_wait(send, recv, srcs, [zone[name_of[k]] for k in keys], _scatter_plan(layers),
                                     after, "scatter_wait_%d" % n_flight)
        for k, land, src, layer in zip(keys, lands, srcs, layers):
            zone[name_of[k]] = land
            own[name_of[k]][layer] = src
        after = lands[0]
    for name in BIG:
        if name not in out:
            out[name] = _adam_big(zone[name], own[name], me_arr, w[name], m[name], v[name], "adam_" + name)

    reduced = _unpack(reduced.reshape(-1), [a.shape for a in local_g])
    loss = reduced[-1][0]
    mine = [a if axis is None else lax.dynamic_slice_in_dim(a, me * w[name].shape[axis], w[name].shape[axis], axis)
            for a, (name, _, axis) in zip(reduced, SMALL)]
    for grad, (name, _, _) in zip(mine, SMALL):
        shape = w[name].shape
        as_2d = lambda a: a.reshape(-1, shape[-1])
        new = _adam_small(as_2d(grad), as_2d(w[name]), as_2d(m[name]), as_2d(v[name]), "adam_" + name)
        out[name] = (grad,) + tuple(a.reshape(shape) for a in new)

    return (loss, grad_x[None]) + tuple(out[name][i] for i in range(4) for name in WEIGHTS)


def kernel(x, positions, ln_mix_g, ln_mix_b, ln_ffn_g, ln_ffn_b, even_w_in, pool_w, pool_scale, lru_conv_w, lru_conv_b, lru_w_a, lru_b_a, lru_w_x, lru_b_x, lru_lambda, even_w_out, mla_w_down, mla_q_norm_g, mla_kv_norm_g, mla_w_qb, mla_w_kvb, mla_w_o, mlp_w1, mlp_w2, loss_target, m_ln_mix_g, m_ln_mix_b, m_ln_ffn_g, m_ln_ffn_b, m_even_w_in, m_pool_w, m_pool_scale, m_lru_conv_w, m_lru_conv_b, m_lru_w_a, m_lru_b_a, m_lru_w_x, m_lru_b_x, m_lru_lambda, m_even_w_out, m_mla_w_down, m_mla_q_norm_g, m_mla_kv_norm_g, m_mla_w_qb, m_mla_w_kvb, m_mla_w_o, m_mlp_w1, m_mlp_w2, v_ln_mix_g, v_ln_mix_b, v_ln_ffn_g, v_ln_ffn_b, v_even_w_in, v_pool_w, v_pool_scale, v_lru_conv_w, v_lru_conv_b, v_lru_w_a, v_lru_b_a, v_lru_w_x, v_lru_b_x, v_lru_lambda, v_even_w_out, v_mla_w_down, v_mla_q_norm_g, v_mla_kv_norm_g, v_mla_w_qb, v_mla_w_kvb, v_mla_w_o, v_mlp_w1, v_mlp_w2):
    w = dict(zip(WEIGHTS, (ln_mix_g, ln_mix_b, ln_ffn_g, ln_ffn_b, even_w_in, pool_w, pool_scale, lru_conv_w,
                           lru_conv_b, lru_w_a, lru_b_a, lru_w_x, lru_b_x, lru_lambda, even_w_out, mla_w_down,
                           mla_q_norm_g, mla_kv_norm_g, mla_w_qb, mla_w_kvb, mla_w_o, mlp_w1, mlp_w2)))
    m = dict(zip(WEIGHTS, (m_ln_mix_g, m_ln_mix_b, m_ln_ffn_g, m_ln_ffn_b, m_even_w_in, m_pool_w, m_pool_scale,
                           m_lru_conv_w, m_lru_conv_b, m_lru_w_a, m_lru_b_a, m_lru_w_x, m_lru_b_x, m_lru_lambda,
                           m_even_w_out, m_mla_w_down, m_mla_q_norm_g, m_mla_kv_norm_g, m_mla_w_qb, m_mla_w_kvb,
                           m_mla_w_o, m_mlp_w1, m_mlp_w2)))
    v = dict(zip(WEIGHTS, (v_ln_mix_g, v_ln_mix_b, v_ln_ffn_g, v_ln_ffn_b, v_even_w_in, v_pool_w, v_pool_scale,
                           v_lru_conv_w, v_lru_conv_b, v_lru_w_a, v_lru_b_a, v_lru_w_x, v_lru_b_x, v_lru_lambda,
                           v_even_w_out, v_mla_w_down, v_mla_q_norm_g, v_mla_kv_norm_g, v_mla_w_qb, v_mla_w_kvb,
                           v_mla_w_o, v_mlp_w1, v_mlp_w2)))
    return _step(x, positions, loss_target, w, m, v)
```

```python
import functools

import jax
import jax.numpy as jnp
from jax import lax
from jax.experimental import pallas as pl
from jax.experimental.pallas import tpu as pltpu

F32 = jnp.float32
BF16 = jnp.bfloat16
S = jax.ShapeDtypeStruct

D = 1024
DEPTH = 4
N_DEV = 8
CHUNK_SHIFT = 6
POOL_WINDOWS = (2, 4, 8, 16)
POOL_W = 512
LRU_W = 1024
LRU_HEADS = 8
HEAD = 128
LRU_C = 8.0
EVEN_IN = 2560
EVEN_MIX = 1536
MLA_HEADS = 8
NOPE = 128
ROPE = 64
VDIM = 128
Q_RANK = 384
KV_RANK = 256
ODD_IN = 704
D_FF = 4096
FF_BLK = D_FF // N_DEV
ROPE_THETA = 10000.0
ALPHA = (2 * DEPTH) ** 0.25
LN_EPS = 1e-5
RMS_EPS = 1e-6
ATT_SCALE = (NOPE + ROPE) ** -0.5
NEG = float(jnp.finfo(jnp.float32).min)
ADAM_LR = 0.001
ADAM_B1 = 0.9
ADAM_B2 = 0.999
ADAM_EPS = 1e-08
ADAM_WD = 0.01
ADAM_STEP = 10
V7X_VMEM_BYTES = 64 * 1024 * 1024
VMEM_LIMIT = V7X_VMEM_BYTES - 8 * 1024 * 1024
MESH = pl.DeviceIdType.MESH


def _cp(*sem):
    return pltpu.CompilerParams(dimension_semantics=sem if sem else None, vmem_limit_bytes=VMEM_LIMIT)


def _dot(a, b):
    return jnp.dot(a, b, preferred_element_type=F32)


def _dot_nt(a, b):
    return lax.dot_general(a, b, (((1,), (1,)), ((), ())), preferred_element_type=F32)


def _dot_tn(a, b):
    return lax.dot_general(a, b, (((0,), (0,)), ((), ())), preferred_element_type=F32)


def _full(shape):
    return pl.BlockSpec(shape, lambda *_: (0,) * len(shape))


def _mm(a, b, *, mode, grid, a_spec, b_spec, out_shape, out_spec, name, add=None, add_spec=None, add_scale=1.0,
        dep=None):
    dot = {"nn": _dot, "nt": _dot_nt, "tn": _dot_tn}[mode]

    def body(*refs):
        a_ref, b_ref, o_ref = refs[0], refs[1], refs[-1]
        acc = dot(a_ref[...].astype(BF16), b_ref[...].astype(BF16))
        if add is not None:
            acc = acc + add_scale * refs[2][...]
        o_ref[...] = acc.astype(o_ref.dtype)

    ops = [a, b] if add is None else [a, b, add]
    specs = [a_spec, b_spec] if add is None else [a_spec, b_spec, add_spec]
    if dep is not None:
        ops.append(dep)
        specs.append(pl.BlockSpec(memory_space=pl.ANY))
    return pl.pallas_call(body, grid=grid, in_specs=specs, out_specs=out_spec, out_shape=out_shape,
                          compiler_params=_cp(*(("parallel",) * len(grid))), name=name)(*ops)


def _ln_stats(z):
    mu = jnp.mean(z, axis=-1, keepdims=True)
    zc = z - mu
    var = jnp.mean(zc * zc, axis=-1, keepdims=True)
    rstd = lax.rsqrt(var + LN_EPS)
    return zc * rstd, rstd


def _row_tile(t):
    return min(512, t)


def _resid_ln(x, mix, g3, b3, l, name):
    t = x.shape[0]
    bm = _row_tile(t)

    def body(x_ref, m_ref, g_ref, b_ref, z_ref, y_ref, yb_ref):
        z = ALPHA * x_ref[...] + m_ref[...]
        xh, _ = _ln_stats(z)
        y = xh * g_ref[...] + b_ref[...]
        z_ref[...] = z
        y_ref[...] = y
        yb_ref[...] = y.astype(BF16)

    row = pl.BlockSpec((bm, D), lambda i: (i, 0))
    vec = pl.BlockSpec((None, 1, D), lambda i: (l, 0, 0))
    return pl.pallas_call(body, grid=(t // bm,), in_specs=[row, row, vec, vec], out_specs=[row, row, row],
                          out_shape=[S((t, D), F32), S((t, D), F32), S((t, D), BF16)],
                          compiler_params=_cp("parallel"), name=name)(x, mix, g3, b3)


def _proj_resid_ln(x, a, wmat, g3, b3, l, name):
    t, k = a.shape
    bm = _row_tile(t)

    def body(x_ref, a_ref, w_ref, g_ref, b_ref, z_ref, y_ref, yb_ref):
        z = ALPHA * x_ref[...] + _dot(a_ref[...], w_ref[...])
        xh, _ = _ln_stats(z)
        y = xh * g_ref[...] + b_ref[...]
        z_ref[...] = z
        y_ref[...] = y
        yb_ref[...] = y.astype(BF16)

    row = pl.BlockSpec((bm, D), lambda i: (i, 0))
    vec = pl.BlockSpec((None, 1, D), lambda i: (l, 0, 0))
    return pl.pallas_call(body, grid=(t // bm,),
                          in_specs=[row, pl.BlockSpec((bm, k), lambda i: (i, 0)), _full((k, D)), vec, vec],
                          out_specs=[row, row, row], out_shape=[S((t, D), F32), S((t, D), F32), S((t, D), BF16)],
                          compiler_params=_cp("parallel"), name=name)(x, a, wmat, g3, b3)


def _ln_bwd(d, z, g3, l, name, r=None, dep=None):
    t = z.shape[0]
    bm = _row_tile(t)

    def body(*refs):
        refs = list(refs)
        d_ref = refs.pop(0)
        dy = d_ref[...]
        if r is not None:
            dy = dy + ALPHA * refs.pop(0)[...]
        z_ref, g_ref = refs.pop(0), refs.pop(0)
        if dep is not None:
            refs.pop(0)
        dz_ref, dzb_ref, dg_ref, db_ref = refs
        xh, rstd = _ln_stats(z_ref[...])
        dyg = dy * g_ref[...]
        m1 = jnp.mean(dyg, axis=-1, keepdims=True)
        m2 = jnp.mean(dyg * xh, axis=-1, keepdims=True)
        dz = rstd * (dyg - m1 - xh * m2)
        dz_ref[...] = dz
        dzb_ref[...] = dz.astype(BF16)

        @pl.when(pl.program_id(0) == 0)
        def _():
            dg_ref[...] = jnp.zeros_like(dg_ref)
            db_ref[...] = jnp.zeros_like(db_ref)

        dg_ref[...] += jnp.sum(dy * xh, axis=0, keepdims=True)
        db_ref[...] += jnp.sum(dy, axis=0, keepdims=True)

    row = pl.BlockSpec((bm, D), lambda i: (i, 0))
    vec = pl.BlockSpec((None, 1, D), lambda i: (l, 0, 0))
    acc = pl.BlockSpec((1, D), lambda i: (0, 0))
    ops = [d, z, g3] if r is None else [d, r, z, g3]
    specs = [row, row, vec] if r is None else [row, row, row, vec]
    if dep is not None:
        ops.append(dep)
        specs.append(_full(dep.shape))
    return pl.pallas_call(body, grid=(t // bm,), in_specs=specs, out_specs=[row, row, acc, acc],
                          out_shape=[S((t, D), F32), S((t, D), BF16), S((1, D), F32), S((1, D), F32)],
                          compiler_params=_cp("arbitrary"), name=name)(*ops)


def _loss_grad(y, tgt):
    t = y.shape[0]
    bm = _row_tile(t)

    def body(y_ref, t_ref, dy_ref, loss_ref, acc_ref):
        i = pl.program_id(0)
        e = y_ref[...] - t_ref[...]
        dy_ref[...] = e * (1.0 / D)

        @pl.when(i == 0)
        def _():
            acc_ref[...] = jnp.zeros_like(acc_ref)

        acc_ref[...] += jnp.sum(e * e, axis=0, keepdims=True)

        @pl.when(i == pl.num_programs(0) - 1)
        def _():
            loss_ref[...] = jnp.full(loss_ref.shape, (0.5 / D) * jnp.sum(acc_ref[...]), F32)

    row = pl.BlockSpec((bm, D), lambda i: (i, 0))
    return pl.pallas_call(body, grid=(t // bm,), in_specs=[row, row],
                          out_specs=[row, pl.BlockSpec((1, 128), lambda i: (0, 0))],
                          out_shape=[S((t, D), F32), S((1, 128), F32)],
                          scratch_shapes=[pltpu.VMEM((1, D), F32)],
                          compiler_params=_cp("arbitrary"), name="loss_grad")(y, tgt)


def _mlp_row_tile(t):
    return min(1024, t)


def _mlp_fwd(y, yb, w1g, w2g, g3, b3, l):
    t = yb.shape[0]
    bm = _mlp_row_tile(t)

    def body(y_ref, yb_ref, w1_ref, w2_ref, g_ref, b_ref, z_ref, o_ref, ob_ref, acc_ref):
        j = pl.program_id(1)
        h = jnp.maximum(_dot(yb_ref[...], w1_ref[...]), 0.0)
        c = _dot((h * h).astype(BF16), w2_ref[...])

        @pl.when(j == 0)
        def _():
            acc_ref[...] = c

        @pl.when(j > 0)
        def _():
            acc_ref[...] += c

        @pl.when(j == N_DEV - 1)
        def _():
            z = ALPHA * y_ref[...] + acc_ref[...]
            xh, _ = _ln_stats(z)
            out = xh * g_ref[...] + b_ref[...]
            z_ref[...] = z
            o_ref[...] = out
            ob_ref[...] = out.astype(BF16)

    row = pl.BlockSpec((bm, D), lambda i, j: (i, 0))
    vec = pl.BlockSpec((None, 1, D), lambda i, j: (l, 0, 0))
    return pl.pallas_call(
        body, grid=(t // bm, N_DEV),
        in_specs=[row, row, pl.BlockSpec((None, D, FF_BLK), lambda i, j: (j, 0, 0)),
                  pl.BlockSpec((None, FF_BLK, D), lambda i, j: (j, 0, 0)), vec, vec],
        out_specs=[row, row, row], out_shape=[S((t, D), F32), S((t, D), F32), S((t, D), BF16)],
        scratch_shapes=[pltpu.VMEM((bm, D), F32)],
        compiler_params=_cp("parallel", "arbitrary"), name="mlp_fwd")(y, yb, w1g, w2g, g3, b3)


def _mlp_bwd_dh(yb, dzb, w1g, w2g):
    t = yb.shape[0]
    bm = _mlp_row_tile(t)

    def body(y_ref, dz_ref, w1_ref, w2_ref, a_ref, dh_ref, acc_ref):
        j = pl.program_id(1)
        r = jnp.maximum(_dot(y_ref[...], w1_ref[...]), 0.0)
        a_ref[...] = (r * r).astype(BF16)
        da = _dot_nt(dz_ref[...], w2_ref[...])
        dh = (da * (2.0 * r)).astype(BF16)
        dh_ref[...] = dh
        c = _dot_nt(dh, w1_ref[...])

        @pl.when(j == 0)
        def _():
            acc_ref[...] = c

        @pl.when(j > 0)
        def _():
            acc_ref[...] += c

    row = pl.BlockSpec((bm, D), lambda i, j: (i, 0))
    hid = pl.BlockSpec((bm, FF_BLK), lambda i, j: (i, j))
    return pl.pallas_call(
        body, grid=(t // bm, N_DEV),
        in_specs=[row, row,
                  pl.BlockSpec((None, D, FF_BLK), lambda i, j: (j, 0, 0)),
                  pl.BlockSpec((None, FF_BLK, D), lambda i, j: (j, 0, 0))],
        out_specs=[hid, hid, row],
        out_shape=[S((t, D_FF), BF16), S((t, D_FF), BF16), S((t, D), F32)],
        compiler_params=_cp("parallel", "arbitrary"), name="mlp_bwd_dh")(yb, dzb, w1g, w2g)


def _shift_dn(x, k, rows, fill=0.0):
    return jnp.where(rows >= k, pltpu.roll(x, k, 0), fill)


def _shift_up(x, k, rows, fill=0.0):
    t = x.shape[0]
    return jnp.where(rows < t - k, pltpu.roll(x, t - k, 0), fill)


def _scan_dn(a, b, rows):
    k = 1
    t = a.shape[0]
    while k < t:
        b = a * _shift_dn(b, k, rows) + b
        if 2 * k < t:
            a = a * _shift_dn(a, k, rows, 1.0)
        k *= 2
    return b


def _scan_up(a, b, rows):
    k = 1
    t = a.shape[0]
    while k < t:
        b = a * _shift_up(b, k, rows) + b
        if 2 * k < t:
            a = a * _shift_up(a, k, rows, 1.0)
        k *= 2
    return b


def _window_sum_dn(x, w, rows):
    k = 1
    while k < w:
        x = x + _shift_dn(x, k, rows)
        k *= 2
    return x


def _window_sum_up(x, w, rows):
    k = 1
    while k < w:
        x = x + _shift_up(x, k, rows)
        k *= 2
    return x


def _pool_diff(u, w, rows):
    inv_count = 1.0 / jnp.minimum(rows + 1, w).astype(F32)
    return _window_sum_dn(u, w, rows) * inv_count - u, inv_count


def _pool_fwd(proj, pool_w, pool_scale3, j):
    t = proj.shape[0]

    def body(u_ref, w_ref, s_ref, y_ref):
        rows = lax.broadcasted_iota(jnp.int32, (t, HEAD), 0)
        for g, w in enumerate(POOL_WINDOWS):
            cols = slice(g * HEAD, (g + 1) * HEAD)
            d, _ = _pool_diff(u_ref[:, cols], w, rows)
            y = _dot(d.astype(BF16), w_ref[g].astype(BF16)) * s_ref[:, cols]
            y_ref[:, cols] = y.astype(BF16)

    return pl.pallas_call(
        body, grid=(1,),
        in_specs=[pl.BlockSpec((t, POOL_W), lambda i: (0, 0)),
                  pl.BlockSpec((None, 4, HEAD, HEAD), lambda i: (j, 0, 0, 0)),
                  pl.BlockSpec((None, 1, POOL_W), lambda i: (j, 0, 0))],
        out_specs=pl.BlockSpec((t, POOL_W), lambda i: (0, 0)),
        out_shape=S((t, POOL_W), BF16), compiler_params=_cp("arbitrary"), name="pool_fwd")(proj, pool_w, pool_scale3)


def _pool_bwd(proj, dycat, pool_w, pool_scale3, j):
    t = proj.shape[0]

    def body(u_ref, dy_ref, w_ref, s_ref, du_ref, dw_ref, ds_ref):
        rows = lax.broadcasted_iota(jnp.int32, (t, HEAD), 0)
        for g, w in enumerate(POOL_WINDOWS):
            cols = slice(g * HEAD, (g + 1) * HEAD)
            d, inv_count = _pool_diff(u_ref[:, cols], w, rows)
            db = d.astype(BF16)
            wg = w_ref[g].astype(BF16)
            dy = dy_ref[:, cols]
            ds_ref[:, cols] = jnp.sum(dy * _dot(db, wg), axis=0, keepdims=True)
            dzz = (dy * s_ref[:, cols]).astype(BF16)
            dw_ref[g] = _dot_tn(db, dzz)
            dd = _dot_nt(dzz, wg)
            du_ref[:, cols] = (_window_sum_up(dd * inv_count, w, rows) - dd).astype(BF16)

    return pl.pallas_call(
        body, grid=(1,),
        in_specs=[pl.BlockSpec((t, POOL_W), lambda i: (0, 0)),
                  pl.BlockSpec((t, POOL_W), lambda i: (0, 0)),
                  pl.BlockSpec((None, 4, HEAD, HEAD), lambda i: (j, 0, 0, 0)),
                  pl.BlockSpec((None, 1, POOL_W), lambda i: (j, 0, 0))],
        out_specs=[pl.BlockSpec((t, POOL_W), lambda i: (0, 0)), _full((4, HEAD, HEAD)), _full((1, POOL_W))],
        out_shape=[S((t, POOL_W), BF16), S((4, HEAD, HEAD), F32), S((1, POOL_W), F32)],
        compiler_params=_cp("arbitrary"), name="pool_bwd")(proj, dycat, pool_w, pool_scale3)


GELU_C = 0.7978845608028654
GELU_K = 0.044715


def _gelu(x):
    th = jnp.tanh(GELU_C * (x + GELU_K * x * x * x))
    return 0.5 * x * (1.0 + th), th


def _lru_forward(u, gate, cw, cb, wa, ba, wx, bx, lam, rows):
    v = cw[3:4] * u + cw[2:3] * _shift_dn(u, 1, rows) + cw[1:2] * _shift_dn(u, 2, rows) \
        + cw[0:1] * _shift_dn(u, 3, rows) + cb
    vb = v.astype(BF16)
    r = jax.nn.sigmoid(_dot(vb, wa) + ba)
    i = jax.nn.sigmoid(_dot(vb, wx) + bx)
    sp = jnp.maximum(-lam, 0.0) + jnp.log1p(jnp.exp(-jnp.abs(lam)))
    log_a = (-LRU_C) * r * sp
    a = jnp.exp(log_a)
    one_m_a2 = -jnp.tanh(log_a) * (a * a + 1.0)
    mult = jnp.sqrt(one_m_a2)
    h = _scan_dn(a, mult * (i * v), rows)
    gl, th = _gelu(gate)
    return dict(v=v, vb=vb, r=r, i=i, sp=sp, a=a, mult=mult, h=h, gl=gl, th=th)


def _lru_specs(t, j, col0_u, col0_g):
    blk = lambda c0: pl.BlockSpec((t, HEAD), lambda h: (0, c0 + h))
    vec = pl.BlockSpec((None, 1, HEAD), lambda h: (j, 0, h))
    return [blk(col0_u), blk(col0_g),
            pl.BlockSpec((None, 4, HEAD), lambda h: (j, 0, h)), vec,
            pl.BlockSpec((None, None, HEAD, HEAD), lambda h: (j, h, 0, 0)), vec,
            pl.BlockSpec((None, None, HEAD, HEAD), lambda h: (j, h, 0, 0)), vec, vec]


def _lru_fwd(proj, p, j):
    t = proj.shape[0]

    def body(u_ref, g_ref, cw_ref, cb_ref, wa_ref, ba_ref, wx_ref, bx_ref, lam_ref, y_ref):
        rows = lax.broadcasted_iota(jnp.int32, (t, HEAD), 0)
        f = _lru_forward(u_ref[...], g_ref[...], cw_ref[...], cb_ref[...], wa_ref[...].astype(BF16), ba_ref[...],
                         wx_ref[...].astype(BF16), bx_ref[...], lam_ref[...], rows)
        y_ref[...] = (f["h"] * f["gl"]).astype(BF16)

    return pl.pallas_call(
        body, grid=(LRU_HEADS,), in_specs=_lru_specs(t, j, POOL_W // HEAD, (POOL_W + LRU_W) // HEAD),
        out_specs=pl.BlockSpec((t, HEAD), lambda h: (0, h)), out_shape=S((t, LRU_W), BF16),
        compiler_params=_cp("parallel"), name="lru_fwd")(
            proj, proj, p["conv_w"], p["conv_b"], p["w_a"], p["b_a"], p["w_x"], p["b_x"], p["lam"])


def _lru_bwd(proj, dycat, p, j):
    t = proj.shape[0]

    def body(u_ref, g_ref, cw_ref, cb_ref, wa_ref, ba_ref, wx_ref, bx_ref, lam_ref, dy_ref,
             du_ref, dgate_ref, dcw_ref, dcb_ref, dwa_ref, dba_ref, dwx_ref, dbx_ref, dlam_ref):
        rows = lax.broadcasted_iota(jnp.int32, (t, HEAD), 0)
        u = u_ref[...]
        gate = g_ref[...]
        cw = cw_ref[...]
        wa = wa_ref[...].astype(BF16)
        wx = wx_ref[...].astype(BF16)
        lam = lam_ref[...]
        f = _lru_forward(u, gate, cw, cb_ref[...], wa, ba_ref[...], wx, bx_ref[...], lam, rows)
        v, r, i, a, mult, h, th = f["v"], f["r"], f["i"], f["a"], f["mult"], f["h"], f["th"]
        dy = dy_ref[...]
        dgl = 0.5 * (1.0 + th) + 0.5 * gate * (1.0 - th * th) * GELU_C * (1.0 + 3.0 * GELU_K * gate * gate)
        dgate_ref[...] = (dy * h * dgl).astype(BF16)
        g = _scan_up(_shift_up(a, 1, rows), dy * f["gl"], rows)
        da = g * _shift_dn(h, 1, rows)
        iv = i * v
        dmult = g * iv
        di = g * mult * v
        dv = g * mult * i
        dlog_a = da * a - dmult * (a * a) / mult
        dr = dlog_a * (-LRU_C) * f["sp"]
        dsp = jnp.sum(dlog_a * (-LRU_C) * r, axis=0, keepdims=True)
        dlam_ref[...] = -dsp * jax.nn.sigmoid(-lam)
        dpa = dr * r * (1.0 - r)
        dpx = di * i * (1.0 - i)
        dpab = dpa.astype(BF16)
        dpxb = dpx.astype(BF16)
        dwa_ref[...] = _dot_tn(f["vb"], dpab)
        dwx_ref[...] = _dot_tn(f["vb"], dpxb)
        dba_ref[...] = jnp.sum(dpa, axis=0, keepdims=True)
        dbx_ref[...] = jnp.sum(dpx, axis=0, keepdims=True)
        dv = dv + _dot_nt(dpab, wa) + _dot_nt(dpxb, wx)
        dcb_ref[...] = jnp.sum(dv, axis=0, keepdims=True)
        du = cw[3:4] * dv
        dcw_ref[3:4, :] = jnp.sum(dv * u, axis=0, keepdims=True)
        for k in (1, 2, 3):
            du = du + cw[3 - k:4 - k] * _shift_up(dv, k, rows)
            dcw_ref[3 - k:4 - k, :] = jnp.sum(dv * _shift_dn(u, k, rows), axis=0, keepdims=True)
        du_ref[...] = du.astype(BF16)

    blk = pl.BlockSpec((t, HEAD), lambda h: (0, h))
    vec = pl.BlockSpec((1, HEAD), lambda h: (0, h))
    mat = pl.BlockSpec((None, HEAD, HEAD), lambda h: (h, 0, 0))
    return pl.pallas_call(
        body, grid=(LRU_HEADS,),
        in_specs=_lru_specs(t, j, POOL_W // HEAD, (POOL_W + LRU_W) // HEAD)
        + [pl.BlockSpec((t, HEAD), lambda h: (0, POOL_W // HEAD + h))],
        out_specs=[blk, blk, pl.BlockSpec((4, HEAD), lambda h: (0, h)), vec, mat, vec, mat, vec, vec],
        out_shape=[S((t, LRU_W), BF16), S((t, LRU_W), BF16), S((4, LRU_W), F32), S((1, LRU_W), F32),
                   S((LRU_HEADS, HEAD, HEAD), F32), S((1, LRU_W), F32),
                   S((LRU_HEADS, HEAD, HEAD), F32), S((1, LRU_W), F32), S((1, LRU_W), F32)],
        compiler_params=_cp("parallel"), name="lru_bwd")(
            proj, proj, p["conv_w"], p["conv_b"], p["w_a"], p["b_a"], p["w_x"], p["b_x"], p["lam"], dycat)


def _rope(x, c, s):
    x1 = x[:, :ROPE // 2]
    x2 = x[:, ROPE // 2:]
    return jnp.concatenate([x1 * c - x2 * s, x1 * s + x2 * c], axis=-1)


def _rope_t(d, c, s):
    d1 = d[:, :ROPE // 2]
    d2 = d[:, ROPE // 2:]
    return jnp.concatenate([d1 * c + d2 * s, d2 * c - d1 * s], axis=-1)


def _rope_tables(pos2, inv_freq):
    t = pos2.shape[0]

    def body(p_ref, f_ref, c_ref, s_ref):
        ang = p_ref[...].astype(F32) * f_ref[...]
        c_ref[...] = jnp.cos(ang)
        s_ref[...] = jnp.sin(ang)

    return pl.pallas_call(body, out_shape=[S((t, ROPE // 2), F32), S((t, ROPE // 2), F32)],
                          name="rope_tables")(pos2, inv_freq)


def _down_norm(xb, wdown_g, gq3, gkv3, cos, sin, j):
    t = xb.shape[0]
    bm = _row_tile(t)

    def body(x_ref, w_ref, gq_ref, gkv_ref, c_ref, s_ref, down_ref, cq_ref, ckv_ref, kpe_ref):
        w = w_ref[...].reshape(D, ODD_IN)
        down = _dot(x_ref[...], w)
        down_ref[...] = down
        q = down[:, :Q_RANK]
        cq_ref[...] = (q * lax.rsqrt(jnp.mean(q * q, axis=-1, keepdims=True) + RMS_EPS) * gq_ref[...]).astype(BF16)
        kv = down[:, Q_RANK:Q_RANK + KV_RANK]
        ckv_ref[...] = (kv * lax.rsqrt(jnp.mean(kv * kv, axis=-1, keepdims=True) + RMS_EPS)
                        * gkv_ref[...]).astype(BF16)
        kpe_ref[...] = _rope(down[:, Q_RANK + KV_RANK:], c_ref[...], s_ref[...])

    row = lambda n: pl.BlockSpec((bm, n), lambda i: (i, 0))
    return pl.pallas_call(
        body, grid=(t // bm,),
        in_specs=[row(D), _full((N_DEV, D // N_DEV, ODD_IN)),
                  pl.BlockSpec((None, 1, Q_RANK), lambda i: (j, 0, 0)),
                  pl.BlockSpec((None, 1, KV_RANK), lambda i: (j, 0, 0)), row(ROPE // 2), row(ROPE // 2)],
        out_specs=[row(ODD_IN), row(Q_RANK), row(KV_RANK), row(ROPE)],
        out_shape=[S((t, ODD_IN), F32), S((t, Q_RANK), BF16), S((t, KV_RANK), BF16), S((t, ROPE), F32)],
        compiler_params=_cp("parallel"), name="down_norm")(xb, wdown_g, gq3, gkv3, cos, sin)


def _q_tile(t, widest):
    return min(widest, t // 2)


def _attn_probs(q, k, qs):
    s = _dot_nt(q, k) * ATT_SCALE
    tq = q.shape[0]
    rows = lax.broadcasted_iota(jnp.int32, (tq, tq), 0)
    cols = lax.broadcasted_iota(jnp.int32, (tq, tq), 1)
    last = jnp.where(jnp.right_shift(cols, CHUNK_SHIFT) <= jnp.right_shift(rows, CHUNK_SHIFT), s[:, qs:], NEG)
    s = last if qs == 0 else jnp.concatenate([s[:, :qs], last], axis=1)
    e = jnp.exp(s - jnp.max(s, axis=-1, keepdims=True))
    return e / jnp.sum(e, axis=-1, keepdims=True)


def _head_qkv(cq, ckv, kpe, c, s, wq_ref, wkv_ref):
    q = jnp.concatenate([_dot(cq, wq_ref[:, :NOPE]), _rope(_dot(cq, wq_ref[:, NOPE:]), c, s)], axis=1).astype(BF16)
    k = jnp.concatenate([_dot(ckv, wkv_ref[:, :NOPE]), kpe], axis=1).astype(BF16)
    vv = _dot(ckv, wkv_ref[:, NOPE:]).astype(BF16)
    return q, k, vv


def _attn_in_specs(t):
    return [_full((t, Q_RANK)), _full((t, KV_RANK)), _full((t, ROPE)), _full((t, ROPE // 2)), _full((t, ROPE // 2)),
            pl.BlockSpec((None, Q_RANK, NOPE + ROPE), lambda h: (h, 0, 0)),
            pl.BlockSpec((None, KV_RANK, NOPE + VDIM), lambda h: (h, 0, 0)),
            pl.BlockSpec((None, VDIM, D), lambda h: (h, 0, 0))]


def _attn_fwd(cq, ckv, kpe, cos, sin, wqb_g, wkvb_g, wo_g):
    t = cq.shape[0]
    tq = _q_tile(t, 256)

    def body(cq_ref, ckv_ref, kpe_ref, c_ref, s_ref, wq_ref, wkv_ref, wo_ref, o_ref, mix_ref):
        q, k, vv = _head_qkv(cq_ref[...], ckv_ref[...], kpe_ref[...], c_ref[...], s_ref[...], wq_ref, wkv_ref)
        for qs in range(0, t, tq):
            ke = qs + tq
            p = _attn_probs(q[qs:ke], k[:ke], qs)
            o_ref[qs:ke, :] = _dot(p.astype(BF16), vv[:ke]).astype(BF16)
        c = _dot(o_ref[...], wo_ref[...])

        @pl.when(pl.program_id(0) == 0)
        def _():
            mix_ref[...] = c

        @pl.when(pl.program_id(0) > 0)
        def _():
            mix_ref[...] += c

    return pl.pallas_call(
        body, grid=(MLA_HEADS,), in_specs=_attn_in_specs(t),
        out_specs=[pl.BlockSpec((None, t, VDIM), lambda h: (h, 0, 0)), _full((t, D))],
        out_shape=[S((MLA_HEADS, t, VDIM), BF16), S((t, D), F32)],
        compiler_params=_cp("arbitrary"), name="attn_fwd")(cq, ckv, kpe, cos, sin, wqb_g, wkvb_g, wo_g)


def _attn_bwd(cq, ckv, kpe, cos, sin, wqb_g, wkvb_g, wo_g, o, dzb):
    t = cq.shape[0]
    tq = _q_tile(t, 512)

    def body(cq_ref, ckv_ref, kpe_ref, c_ref, s_ref, wq_ref, wkv_ref, wo_ref, o_ref, dz_ref,
             dwo_ref, dwq_ref, dwkv_ref, dcq_ref, dckv_ref, dkpe_ref, dk_s, dv_s, dq_s):
        cqv = cq_ref[...]
        ckvv = ckv_ref[...]
        c = c_ref[...]
        s = s_ref[...]
        q, k, vv = _head_qkv(cqv, ckvv, kpe_ref[...], c, s, wq_ref, wkv_ref)
        dzv = dz_ref[...]
        dwo_ref[...] = _dot_tn(o_ref[...], dzv).astype(BF16)
        do = _dot_nt(dzv, wo_ref[...]).astype(BF16)
        dk_s[...] = jnp.zeros_like(dk_s)
        dv_s[...] = jnp.zeros_like(dv_s)
        for qs in range(0, t, tq):
            ke = qs + tq
            p = _attn_probs(q[qs:ke], k[:ke], qs)
            dp = _dot_nt(do[qs:ke], vv[:ke])
            ds = (p * (dp - jnp.sum(p * dp, axis=-1, keepdims=True)) * ATT_SCALE).astype(BF16)
            dq_s[qs:ke, :] = _dot(ds, k[:ke])
            dk_s[0:ke, :] += _dot_tn(ds, q[qs:ke])
            dv_s[0:ke, :] += _dot_tn(p.astype(BF16), do[qs:ke])
        dqn = dq_s[:, :NOPE].astype(BF16)
        dqp = _rope_t(dq_s[:, NOPE:], c, s).astype(BF16)
        dkn = dk_s[:, :NOPE].astype(BF16)
        dvv = dv_s[...].astype(BF16)
        dwq_ref[:, :NOPE] = _dot_tn(cqv, dqn).astype(BF16)
        dwq_ref[:, NOPE:] = _dot_tn(cqv, dqp).astype(BF16)
        dwkv_ref[:, :NOPE] = _dot_tn(ckvv, dkn).astype(BF16)
        dwkv_ref[:, NOPE:] = _dot_tn(ckvv, dvv).astype(BF16)
        dcq = _dot_nt(dqn, wq_ref[:, :NOPE]) + _dot_nt(dqp, wq_ref[:, NOPE:])
        dckv = _dot_nt(dkn, wkv_ref[:, :NOPE]) + _dot_nt(dvv, wkv_ref[:, NOPE:])

        @pl.when(pl.program_id(0) == 0)
        def _():
            dcq_ref[...] = dcq
            dckv_ref[...] = dckv
            dkpe_ref[...] = dk_s[:, NOPE:]

        @pl.when(pl.program_id(0) > 0)
        def _():
            dcq_ref[...] += dcq
            dckv_ref[...] += dckv
            dkpe_ref[...] += dk_s[:, NOPE:]

    per_head = lambda a, b: pl.BlockSpec((None, a, b), lambda h: (h, 0, 0))
    return pl.pallas_call(
        body, grid=(MLA_HEADS,),
        in_specs=_attn_in_specs(t) + [per_head(t, VDIM), _full((t, D))],
        out_specs=[per_head(VDIM, D), per_head(Q_RANK, NOPE + ROPE), per_head(KV_RANK, NOPE + VDIM),
                   _full((t, Q_RANK)), _full((t, KV_RANK)), _full((t, ROPE))],
        out_shape=[S((MLA_HEADS, VDIM, D), BF16), S((MLA_HEADS, Q_RANK, NOPE + ROPE), BF16),
                   S((MLA_HEADS, KV_RANK, NOPE + VDIM), BF16),
                   S((t, Q_RANK), F32), S((t, KV_RANK), F32), S((t, ROPE), F32)],
        scratch_shapes=[pltpu.VMEM((t, NOPE + ROPE), F32), pltpu.VMEM((t, VDIM), F32),
                        pltpu.VMEM((t, NOPE + ROPE), F32)],
        compiler_params=_cp("arbitrary"), name="attn_bwd")(cq, ckv, kpe, cos, sin, wqb_g, wkvb_g, wo_g, o, dzb)


def _rms_bwd(down, dcq, dckv, dkpe, cos, sin, gq3, gkv3, j):
    t = down.shape[0]
    bm = _row_tile(t)

    def body(down_ref, dcq_ref, dckv_ref, dkpe_ref, c_ref, s_ref, gq_ref, gkv_ref, dd_ref, dgq_ref, dgkv_ref):
        @pl.when(pl.program_id(0) == 0)
        def _():
            dgq_ref[...] = jnp.zeros_like(dgq_ref)
            dgkv_ref[...] = jnp.zeros_like(dgkv_ref)

        def rms_b(x, dy, g):
            rstd = lax.rsqrt(jnp.mean(x * x, axis=-1, keepdims=True) + RMS_EPS)
            xh = x * rstd
            dyg = dy * g
            return rstd * (dyg - xh * jnp.mean(dyg * xh, axis=-1, keepdims=True)), jnp.sum(dy * xh, axis=0, keepdims=True)

        dq, dgq = rms_b(down_ref[:, :Q_RANK], dcq_ref[...], gq_ref[...])
        dkv, dgkv = rms_b(down_ref[:, Q_RANK:Q_RANK + KV_RANK], dckv_ref[...], gkv_ref[...])
        dgq_ref[...] += dgq
        dgkv_ref[...] += dgkv
        dd_ref[:, :Q_RANK] = dq.astype(BF16)
        dd_ref[:, Q_RANK:Q_RANK + KV_RANK] = dkv.astype(BF16)
        dd_ref[:, Q_RANK + KV_RANK:] = _rope_t(dkpe_ref[...], c_ref[...], s_ref[...]).astype(BF16)

    row = lambda n: pl.BlockSpec((bm, n), lambda i: (i, 0))
    return pl.pallas_call(
        body, grid=(t // bm,),
        in_specs=[row(ODD_IN), row(Q_RANK), row(KV_RANK), row(ROPE), row(ROPE // 2), row(ROPE // 2),
                  pl.BlockSpec((None, 1, Q_RANK), lambda i: (j, 0, 0)),
                  pl.BlockSpec((None, 1, KV_RANK), lambda i: (j, 0, 0))],
        out_specs=[row(ODD_IN), _full((1, Q_RANK)), _full((1, KV_RANK))],
        out_shape=[S((t, ODD_IN), BF16), S((1, Q_RANK), F32), S((1, KV_RANK), F32)],
        compiler_params=_cp("arbitrary"), name="rms_bwd")(down, dcq, dckv, dkpe, cos, sin, gq3, gkv3)


def _col_blocks(t, n, bn):
    return pl.BlockSpec((t, bn), lambda i: (0, i))


def _row_blocks(n, bm):
    return pl.BlockSpec((bm, n), lambda i: (i, 0))


def _local_step(x, pos2, tgt, small, weights_of, grads_done, start_dep=None, prefetch=None):
    t = x.shape[0]
    bm = _row_tile(t)
    inv_freq = (ROPE_THETA ** (-jnp.arange(0, ROPE, 2, dtype=F32) / ROPE)).reshape(1, ROPE // 2)
    cos, sin = _rope_tables(pos2, inv_freq)
    lru_p = {k: small[k] for k in ("conv_w", "conv_b", "w_a", "b_a", "w_x", "b_x", "lam")}

    saved = []
    y, yb = x, x.astype(BF16)
    for l in range(DEPTH):
        j = l // 2
        big = weights_of(l, 0, y)
        sv = dict(xb=yb, big=big)
        if l % 2 == 0:
            proj = _mm(yb, big["win2d"], mode="nn", grid=(EVEN_IN // 512,), a_spec=_full((t, D)),
                       b_spec=_col_blocks(D, EVEN_IN, 512), out_shape=S((t, EVEN_IN), F32),
                       out_spec=_col_blocks(t, EVEN_IN, 512), name="even_proj", dep=start_dep if l == 0 else None)
            ycat = jnp.concatenate([_pool_fwd(proj, small["pool_w"], small["pool_scale"], j),
                                    _lru_fwd(proj, lru_p, j)], axis=1)
            big.update(weights_of(l, 1, ycat))
            z1, y1, y1b = _proj_resid_ln(y, ycat, big["wout2d"], small["ln_mix_g"], small["ln_mix_b"], l, "even_out")
            sv.update(proj=proj, ycat=ycat)
        else:
            down, cq, ckv, kpe = _down_norm(yb, big["wdown"], small["gq"], small["gkv"], cos, sin, j)
            o, mix = _attn_fwd(cq, ckv, kpe, cos, sin, big["wqb"], big["wkvb"], big["wo"])
            z1, y1, y1b = _resid_ln(y, mix, small["ln_mix_g"], small["ln_mix_b"], l, "resid_ln")
            sv.update(down=down, cq=cq, ckv=ckv, kpe=kpe, o=o)
        if prefetch is not None and l + 1 < DEPTH:
            prefetch(l + 1, y1)
        z2, y, yb = _mlp_fwd(y1, y1b, big["w1"], big["w2"], small["ln_ffn_g"], small["ln_ffn_b"], l)
        sv.update(z1=z1, y1b=y1b, z2=z2)
        saved.append(sv)

    dy, loss_tile = _loss_grad(y, tgt)

    g = {k: [None] * n for k, n in (("ln_mix_g", 4), ("ln_mix_b", 4), ("ln_ffn_g", 4), ("ln_ffn_b", 4),
                                    ("pool_w", 2), ("pool_scale", 2), ("conv_w", 2), ("conv_b", 2),
                                    ("w_a", 2), ("b_a", 2), ("w_x", 2), ("b_x", 2), ("lam", 2),
                                    ("gq", 2), ("gkv", 2))}
    dep = None
    for l in reversed(range(DEPTH)):
        j = l // 2
        sv = saved[l]
        big = sv["big"]
        dz2, dz2b, g["ln_ffn_g"][l], g["ln_ffn_b"][l] = _ln_bwd(dy, sv["z2"], small["ln_ffn_g"], l, "ln_bwd", dep=dep)
        act, dh, dff = _mlp_bwd_dh(sv["y1b"], dz2b, big["w1"], big["w2"])
        dw1 = _mm(sv["y1b"], dh, mode="tn", grid=(N_DEV,), a_spec=_full((t, D)),
                  b_spec=_col_blocks(t, D_FF, FF_BLK), out_shape=S((N_DEV, D, FF_BLK), BF16),
                  out_spec=pl.BlockSpec((None, D, FF_BLK), lambda i: (i, 0, 0)), name="mlp_dw1")
        dw2 = _mm(act, dz2b, mode="tn", grid=(N_DEV,), a_spec=_col_blocks(t, D_FF, FF_BLK),
                  b_spec=_full((t, D)), out_shape=S((N_DEV, FF_BLK, D), BF16),
                  out_spec=pl.BlockSpec((None, FF_BLK, D), lambda i: (i, 0, 0)), name="mlp_dw2")
        dep = grads_done(l, dict(w1=dw1, w2=dw2))
        dz1, dz1b, g["ln_mix_g"][l], g["ln_mix_b"][l] = _ln_bwd(dff, sv["z1"], small["ln_mix_g"], l, "ln_bwd_res",
                                                                 r=dz2, dep=dep)
        if l % 2 == 0:
            wout = big["wout2d"]
            dycat = _mm(dz1b, wout, mode="nt", grid=(EVEN_MIX // 512,), a_spec=_full((t, D)),
                        b_spec=_row_blocks(D, 512), out_shape=S((t, EVEN_MIX), F32),
                        out_spec=_col_blocks(t, EVEN_MIX, 512), name="even_dycat")
            dwout = _mm(sv["ycat"], dz1b, mode="tn", grid=(EVEN_MIX // 512,), a_spec=_col_blocks(t, EVEN_MIX, 512),
                        b_spec=_full((t, D)), out_shape=S((EVEN_MIX, D), BF16), out_spec=_row_blocks(D, 512),
                        name="even_dwout")
            du_pool, g["pool_w"][j], g["pool_scale"][j] = _pool_bwd(sv["proj"], dycat, small["pool_w"],
                                                                   small["pool_scale"], j)
            (du_lru, du_gate, g["conv_w"][j], g["conv_b"][j], g["w_a"][j], g["b_a"][j], g["w_x"][j], g["b_x"][j],
             g["lam"][j]) = _lru_bwd(sv["proj"], dycat, lru_p, j)
            dproj = jnp.concatenate([du_pool, du_lru, du_gate], axis=1)
            dwin = _mm(sv["xb"], dproj, mode="tn", grid=(EVEN_IN // 512,), a_spec=_full((t, D)),
                       b_spec=_col_blocks(t, EVEN_IN, 512), out_shape=S((D, EVEN_IN), BF16),
                       out_spec=_col_blocks(D, EVEN_IN, 512), name="even_dwin")
            dep = grads_done(l, dict(win=dwin.reshape(D, N_DEV, EVEN_IN // N_DEV).transpose(1, 0, 2),
                                     wout=dwout.reshape(N_DEV, EVEN_MIX // N_DEV, D)))
            dy = _mm(dproj, big["win2d"], mode="nt", grid=(t // bm,), a_spec=_row_blocks(EVEN_IN, bm),
                     b_spec=_full((D, EVEN_IN)), out_shape=S((t, D), F32), out_spec=_row_blocks(D, bm),
                     add=dz1, add_spec=_row_blocks(D, bm), add_scale=ALPHA, name="even_dx")
        else:
            dwo, dwqb, dwkvb, dcq, dckv, dkpe = _attn_bwd(
                sv["cq"], sv["ckv"], sv["kpe"], cos, sin, big["wqb"], big["wkvb"], big["wo"], sv["o"], dz1b)
            ddown, g["gq"][j], g["gkv"][j] = _rms_bwd(sv["down"], dcq, dckv, dkpe, cos, sin, small["gq"],
                                                     small["gkv"], j)
            dwdown = _mm(sv["xb"], ddown, mode="tn", grid=(N_DEV,), a_spec=_col_blocks(t, D, D // N_DEV),
                         b_spec=_full((t, ODD_IN)), out_shape=S((N_DEV, D // N_DEV, ODD_IN), BF16),
                         out_spec=pl.BlockSpec((None, D // N_DEV, ODD_IN), lambda i: (i, 0, 0)),
                         name="odd_dwdown")
            dep = grads_done(l, dict(wdown=dwdown, wqb=dwqb, wkvb=dwkvb, wo=dwo))
            dy = _mm(ddown, big["wdown2d"], mode="nt", grid=(t // bm,), a_spec=_row_blocks(ODD_IN, bm),
                     b_spec=_full((D, ODD_IN)), out_shape=S((t, D), F32), out_spec=_row_blocks(D, bm),
                     add=dz1, add_spec=_row_blocks(D, bm), add_scale=ALPHA, name="odd_dx")
    return loss_tile[0, 0], dy, g


def _mesh_place():
    x, y, c = lax.axis_index("x"), lax.axis_index("y"), lax.axis_index("c")
    return x, y, c


def _peer(place, k):
    x, y, c = place
    return (1 - x if k & 4 else x, 1 - y if k & 2 else y, 1 - c if k & 1 else c)


def _index(place):
    x, y, c = place
    return 4 * x + 2 * y + c


ANY = pl.BlockSpec(memory_space=pl.ANY)


def _all_gather_big(zones):
    n = len(zones)

    def body(*refs):
        outs = refs[n:2 * n]
        send, recv = refs[2 * n:]
        x, y, c = _mesh_place()
        me, sibling = (x, y, c), (x, y, 1 - c)
        chips = [(1 - x, y), (x, 1 - y), (1 - x, 1 - y)]

        def copy(w, k, block, to):
            blk = outs[w].at[_index(block)]
            return pltpu.make_async_remote_copy(src_ref=blk, dst_ref=blk, send_sem=send.at[w, k], recv_sem=recv.at[w, k],
                                                device_id=to, device_id_type=MESH)

        first = []
        for w in range(n):
            first.append(copy(w, 0, me, sibling))
            first += [copy(w, 1 + j, me, (*chip, c)) for j, chip in enumerate(chips)]
        for cp in first:
            cp.start()
        passed = []
        for w in range(n):
            for j, chip in enumerate(chips):
                copy(w, 1 + j, (*chip, c), me).wait_recv()
                cp = copy(w, 4 + j, (*chip, c), sibling)
                cp.start()
                passed.append(cp)
        for w in range(n):
            copy(w, 0, sibling, me).wait_recv()
            for j, chip in enumerate(chips):
                copy(w, 4 + j, (*chip, 1 - c), me).wait_recv()
        for cp in first + passed:
            cp.wait_send()

    return pl.pallas_call(
        body, in_specs=[ANY] * n, out_specs=[ANY] * n, out_shape=[S(z.shape, z.dtype) for z in zones],
        input_output_aliases={i: i for i in range(n)},
        scratch_shapes=[pltpu.SemaphoreType.DMA((n, N_DEV - 1)), pltpu.SemaphoreType.DMA((n, N_DEV - 1))],
        compiler_params=pltpu.CompilerParams(has_side_effects=True), name="all_gather_big")(*zones)


def _shard_rows_tile(a):
    return max(d for d in range(16, 257, 16) if a % d == 0)


HBM = pl.BlockSpec(memory_space=pltpu.HBM)
SEM = pl.BlockSpec(memory_space=pltpu.SEMAPHORE)
DATAFLOW = pltpu.SideEffectType.DATAFLOW_SIDE_EFFECTING


def _in_hbm(a):
    return pltpu.with_memory_space_constraint(a, pltpu.HBM)


def _gather_ici_copies(place, src, land, w):
    me = _index(place)
    return [(_peer(place, k), land.at[me], land.at[me]) for k in (1, 2, 4, 6)]


def _gather_d2d_copies(place, src, land, w):
    blocks = [_index(_peer(place, k)) for k in (2, 4, 6)]
    return [(_peer(place, 1), land.at[b], land.at[b]) for b in blocks]


GATHER_ICI = (4, _gather_ici_copies)
GATHER_D2D = (3, _gather_d2d_copies)


def _scatter_plan(layers):
    def copies(place, src, land, w):
        me = _index(place)
        mine = land.at[me] if layers[w] is None else land.at[me, layers[w]]
        return [(_peer(place, k), src.at[_index(_peer(place, k))], mine) for k in range(1, N_DEV)]
    return (N_DEV - 1, copies)


def _gather_all_copies(place, src, land, w):
    me = _index(place)
    return [(_peer(place, k), land.at[me], land.at[me]) for k in range(1, N_DEV)]


GATHER_ALL = (N_DEV - 1, _gather_all_copies)


def _sum_blocks(zone, part, me):
    r = part.shape[1]

    def body(me_ref, z_ref, p_ref, o_ref):
        acc = None
        for s in range(N_DEV):
            term = jnp.where(me_ref[0] == s, p_ref[...], z_ref[s])
            acc = term if acc is None else acc + term
        o_ref[...] = acc

    grid_spec = pltpu.PrefetchScalarGridSpec(
        num_scalar_prefetch=1, grid=(1,),
        in_specs=[pl.BlockSpec((N_DEV, r, 128), lambda i, me_ref: (0, 0, 0)),
                  pl.BlockSpec((None, r, 128), lambda i, me_ref: (me_ref[0], 0, 0))],
        out_specs=pl.BlockSpec((r, 128), lambda i, me_ref: (0, 0)))
    return pl.pallas_call(body, grid_spec=grid_spec, out_shape=S((r, 128), F32),
                          compiler_params=_cp("arbitrary"), name="sum_small")(me, zone, part)


def _exchange_start(srcs, lands, plan, name, after=()):
    ns, n = len(srcs), len(lands)
    n_in = ns + n + len(after)
    per, copies = plan

    def body(*refs):
        ins, land = refs[:ns], refs[ns:ns + n]
        send, recv = refs[n_in], refs[n_in + 1]
        token = refs[-1]
        place = _mesh_place()
        for i in range(per):
            for w in range(n):
                target, src, dst = copies(place, ins[w] if ns else None, land[w], w)[i]
                pltpu.make_async_remote_copy(src_ref=src, dst_ref=dst, send_sem=send.at[w * per + i],
                                             recv_sem=recv.at[w * per + i], device_id=target, device_id_type=MESH).start()
        token[...] = jnp.zeros_like(token)

    sems = pltpu.SemaphoreType.DMA((n * per,))
    thru = [pltpu.HBM(a.shape, a.dtype) for a in list(srcs) + list(lands)]
    out = pl.pallas_call(
        body, name=name, in_specs=[HBM] * (ns + n) + [ANY] * len(after),
        out_shape=(sems, sems, *thru, S((8, 128), F32)),
        out_specs=(SEM, SEM, *([HBM] * (ns + n)), pl.BlockSpec(memory_space=pltpu.VMEM)),
        input_output_aliases={i: 2 + i for i in range(ns + n)},
        compiler_params=pltpu.CompilerParams(has_side_effects=DATAFLOW),
    )(*[_in_hbm(a) for a in list(srcs) + list(lands)], *after)
    return out[0], out[1], list(out[2:2 + ns]), list(out[2 + ns:2 + ns + n]), out[-1]


def _exchange_wait(send, recv, srcs, lands, plan, after, name):
    ns, n = len(srcs), len(lands)
    per, copies = plan

    def body(*refs):
        ins, land = refs[:ns], refs[ns:ns + n]
        send_ref, recv_ref = refs[ns + n], refs[ns + n + 1]
        place = _mesh_place()
        for i in range(per):
            for w in range(n):
                target, src, dst = copies(place, ins[w] if ns else None, land[w], w)[i]
                cp = pltpu.make_async_remote_copy(src_ref=src, dst_ref=dst, send_sem=send_ref.at[w * per + i],
                                                  recv_sem=recv_ref.at[w * per + i], device_id=target,
                                                  device_id_type=MESH)
                cp.wait_send()
                cp.wait_recv()

    thru = [pltpu.HBM(a.shape, a.dtype) for a in list(srcs) + list(lands)]
    out = pl.pallas_call(
        body, name=name, in_specs=[HBM] * (ns + n) + [SEM, SEM, ANY],
        out_shape=tuple(thru), out_specs=tuple([HBM] * (ns + n)),
        input_output_aliases={i: i for i in range(ns + n)},
        compiler_params=pltpu.CompilerParams(has_side_effects=DATAFLOW),
    )(*srcs, *lands, send, recv, after)
    return list(out[:ns]), list(out[ns:])


def _all_reduce_small(part, name, deps=()):
    def body(*refs):
        p_ref = refs[0]
        o_ref, rbuf, send1, recv1, send2, recv2 = refs[-6:]
        place = _mesh_place()
        me = _index(place)
        rbuf[pl.ds(me, 1)] = p_ref[pl.ds(me, 1)]
        first = [pltpu.make_async_remote_copy(src_ref=p_ref.at[_index(_peer(place, k))], dst_ref=rbuf.at[me],
                                              send_sem=send1.at[k - 1], recv_sem=recv1.at[k - 1],
                                              device_id=_peer(place, k), device_id_type=MESH)
                 for k in range(1, N_DEV)]
        for cp in first:
            cp.start()
        for cp in first:
            cp.wait()
        acc = rbuf[0]
        for d in range(1, N_DEV):
            acc = acc + rbuf[d]
        o_ref[pl.ds(me, 1)] = acc[None]
        second = [pltpu.make_async_remote_copy(src_ref=o_ref.at[me], dst_ref=o_ref.at[me], send_sem=send2.at[k - 1],
                                               recv_sem=recv2.at[k - 1], device_id=_peer(place, k),
                                               device_id_type=MESH)
                  for k in range(1, N_DEV)]
        for cp in second:
            cp.start()
        for cp in second:
            cp.wait()

    vm = pl.BlockSpec(memory_space=pltpu.VMEM)
    ops = [part, *deps]
    return pl.pallas_call(
        body, in_specs=[vm] + [ANY] * len(deps), out_specs=vm, out_shape=S(part.shape, F32),
        scratch_shapes=[pltpu.VMEM(part.shape, F32)] + [pltpu.SemaphoreType.DMA((N_DEV - 1,))] * 4,
        compiler_params=pltpu.CompilerParams(has_side_effects=True, vmem_limit_bytes=VMEM_LIMIT), name=name)(*ops)


def _adamw(w, g, m, v):
    m = ADAM_B1 * m + (1.0 - ADAM_B1) * g
    v = ADAM_B2 * v + (1.0 - ADAM_B2) * (g * g)
    m_hat = m / (1.0 - ADAM_B1 ** ADAM_STEP)
    v_hat = v / (1.0 - ADAM_B2 ** ADAM_STEP)
    return -ADAM_LR * (m_hat / (jnp.sqrt(v_hat) + ADAM_EPS) + ADAM_WD * w), m, v


def _adam_big(parts, own, me, w, m, v, name):
    nl, a, b = w.shape
    ta = _shard_rows_tile(a)

    def body(me_ref, p_ref, *refs):
        own_refs, (w_ref, m_ref, v_ref, g_ref, d_ref, mo_ref, vo_ref) = refs[:nl], refs[nl:]
        layer = pl.program_id(0)
        mine = own_refs[0][...]
        for k in range(1, nl):
            mine = jnp.where(layer == k, own_refs[k][...], mine)
        g = None
        for s in range(N_DEV):
            term = jnp.where(me_ref[0] == s, mine, p_ref[s]).astype(F32)
            g = term if g is None else g + term
        g_ref[...] = g
        d_ref[...], mo_ref[...], vo_ref[...] = _adamw(w_ref[...], g, m_ref[...], v_ref[...])

    blk = pl.BlockSpec((None, ta, b), lambda l, i, me_ref: (l, i, 0))

    def own_spec(k):
        return pl.BlockSpec((None, ta, b), lambda l, i, me_ref: (me_ref[0], jnp.where(l == k, i, 0), 0))

    grid_spec = pltpu.PrefetchScalarGridSpec(
        num_scalar_prefetch=1, grid=(nl, a // ta),
        in_specs=[pl.BlockSpec((N_DEV, None, ta, b), lambda l, i, me_ref: (0, l, i, 0))]
        + [own_spec(k) for k in range(nl)] + [blk, blk, blk],
        out_specs=[blk] * 4)
    return pl.pallas_call(body, grid_spec=grid_spec, out_shape=[S(w.shape, F32)] * 4,
                          compiler_params=_cp("arbitrary", "arbitrary"), name=name)(me, parts, *own, w, m, v)


def _adam_small(g, w, m, v, name):
    def body(g_ref, w_ref, m_ref, v_ref, d_ref, mo_ref, vo_ref):
        d_ref[...], mo_ref[...], vo_ref[...] = _adamw(w_ref[...], g_ref[...], m_ref[...], v_ref[...])

    return pl.pallas_call(body, out_shape=[S(g.shape, F32)] * 3, compiler_params=_cp(), name=name)(g, w, m, v)


BIG = ("even_w_in", "even_w_out", "mla_w_down", "mla_w_qb", "mla_w_kvb", "mla_w_o", "mlp_w1", "mlp_w2")
BIG_KEY = dict(even_w_in="win", even_w_out="wout", mla_w_down="wdown", mla_w_qb="wqb", mla_w_kvb="wkvb",
               mla_w_o="wo", mlp_w1="w1", mlp_w2="w2")
SMALL = (("ln_mix_g", "ln_mix_g", None), ("ln_mix_b", "ln_mix_b", None), ("ln_ffn_g", "ln_ffn_g", None),
         ("ln_ffn_b", "ln_ffn_b", None), ("pool_w", "pool_w", None), ("pool_scale", "pool_scale", None),
         ("lru_conv_w", "conv_w", 2), ("lru_conv_b", "conv_b", None), ("lru_w_a", "w_a", None),
         ("lru_b_a", "b_a", None), ("lru_w_x", "w_x", None), ("lru_b_x", "b_x", None), ("lru_lambda", "lam", None),
         ("mla_q_norm_g", "gq", 1), ("mla_kv_norm_g", "gkv", 1))
WEIGHTS = ("ln_mix_g", "ln_mix_b", "ln_ffn_g", "ln_ffn_b", "even_w_in", "pool_w", "pool_scale", "lru_conv_w",
           "lru_conv_b", "lru_w_a", "lru_b_a", "lru_w_x", "lru_b_x", "lru_lambda", "even_w_out", "mla_w_down",
           "mla_q_norm_g", "mla_kv_norm_g", "mla_w_qb", "mla_w_kvb", "mla_w_o", "mlp_w1", "mlp_w2")
ALL_AXES = ("x", "y", "c")


def _layer_weights(l):
    j = l // 2
    if l % 2 == 0:
        mixer = [("win", "even_w_in", j), ("wout", "even_w_out", j)]
    else:
        mixer = [("wdown", "mla_w_down", j), ("wqb", "mla_w_qb", j), ("wkvb", "mla_w_kvb", j), ("wo", "mla_w_o", j)]
    return mixer + [("w1", "mlp_w1", l), ("w2", "mlp_w2", l)]


def _pack(arrays, multiple):
    flat = jnp.concatenate([a.reshape(-1) for a in arrays])
    pad = (-flat.shape[0]) % multiple
    return jnp.pad(flat, (0, pad))


def _unpack(flat, shapes):
    out, at = [], 0
    for shp in shapes:
        n = 1
        for s in shp:
            n *= s
        out.append(flat[at:at + n].reshape(shp))
        at += n
    return out


def _global_shape(local_shape, axis):
    if axis is None:
        return tuple(local_shape)
    return tuple(s * N_DEV if i == axis else s for i, s in enumerate(local_shape))


def _step(x, positions, tgt, w, m, v):
    t = x.shape[1]
    me = _index(_mesh_place())

    sharded = [(name, axis) for name, _, axis in SMALL if axis is not None]
    zeros_with_mine = [lax.dynamic_update_slice_in_dim(jnp.zeros(_global_shape(w[name].shape, axis), F32), w[name],
                                                       me * w[name].shape[axis], axis) for name, axis in sharded]
    chunk = N_DEV * 8 * 128
    gathered = _all_reduce_small(_pack(zeros_with_mine, chunk).reshape(N_DEV, -1, 128), "gather_small")
    full = dict(zip([name for name, _ in sharded],
                    _unpack(gathered.reshape(-1), [_global_shape(w[name].shape, axis) for name, axis in sharded])))

    def zone_of(shard):
        return lax.dynamic_update_slice_in_dim(lax.empty((N_DEV,) + shard.shape, BF16), shard.astype(BF16)[None], me, 0)

    def keys_of(l, part):
        keys = [key for key, _, _ in _layer_weights(l)]
        if l == 0:
            return keys[:1] if part == 0 else keys[1:]
        return keys if part == 0 else []

    shard_of = {(l, key): w[name][i] for l in range(DEPTH) for key, name, i in _layer_weights(l)}
    first = _all_gather_big([zone_of(shard_of[0, key]) for key in keys_of(0, 0)])
    flights, after = {}, (first[0], gathered)
    for l in range(DEPTH):
        for part in (0, 1):
            if (l, part) != (0, 0) and keys_of(l, part):
                zones = [zone_of(shard_of[l, key]) for key in keys_of(l, part)]
                send, recv, _, lands, token = _exchange_start([], zones, GATHER_ICI, "gather_start_%d_%d" % (l, part),
                                                              after=after)
                flights[l, part] = (send, recv, [], lands)
                after = (token,)

    passing = {}

    def pass_on(l, part, after):
        tag = "%d_%d" % (l, part)
        _, lands = _exchange_wait(*flights[l, part], GATHER_ICI, after, "gather_wait_" + tag)
        send, recv, _, lands, _ = _exchange_start([], lands, GATHER_D2D, "gather_pass_" + tag)
        passing[l, part] = (send, recv, [], lands)

    def weights_of(l, part, after):
        keys = keys_of(l, part)
        if (l, part) == (0, 0):
            arrays = first
        elif keys:
            if (l, part) not in passing:
                pass_on(l, part, after)
            _, arrays = _exchange_wait(*passing[l, part], GATHER_D2D, after, "gather_pass_wait_%d_%d" % (l, part))
        big = dict(zip(keys, arrays)) if keys else {}
        if "win" in big:
            big["win2d"] = big["win"].transpose(1, 0, 2).reshape(D, EVEN_IN)
        if "wout" in big:
            big["wout2d"] = big["wout"].reshape(EVEN_MIX, D)
        if "wdown" in big:
            big["wdown2d"] = big["wdown"].reshape(D, ODD_IN)
        return big

    zone = {name: lax.empty((N_DEV,) + w[name].shape, BF16) for name in BIG}
    name_of = {key: name for name, key in BIG_KEY.items()}
    sent, last_token = [], [None]

    def grads_done(l, grads):
        keys = list(grads)
        index = {key: i for key, _, i in _layer_weights(l)}
        layers = [index[key] for key in keys]
        send, recv, srcs, lands, tok = _exchange_start([grads[k] for k in keys], [zone[name_of[k]] for k in keys],
                                                       _scatter_plan(layers), "scatter_start_%d_%s" % (l, keys[0]))
        for k, land in zip(keys, lands):
            zone[name_of[k]] = land
        sent.append((send, recv, srcs, keys, layers))
        last_token[0] = tok
        return tok

    row3 = lambda a: a.reshape(a.shape[0], 1, a.shape[1])
    small = dict(ln_mix_g=row3(w["ln_mix_g"]), ln_mix_b=row3(w["ln_mix_b"]), ln_ffn_g=row3(w["ln_ffn_g"]),
                 ln_ffn_b=row3(w["ln_ffn_b"]), pool_w=w["pool_w"], pool_scale=row3(w["pool_scale"]),
                 conv_w=full["lru_conv_w"], conv_b=row3(w["lru_conv_b"]), w_a=w["lru_w_a"], b_a=row3(w["lru_b_a"]),
                 w_x=w["lru_w_x"], b_x=row3(w["lru_b_x"]), lam=row3(w["lru_lambda"]),
                 gq=row3(full["mla_q_norm_g"]), gkv=row3(full["mla_kv_norm_g"]))

    loss_part, grad_x, g = _local_step(x[0], positions.reshape(t, 1), tgt[0], small, weights_of, grads_done,
                                       start_dep=token, prefetch=lambda l, after: pass_on(l, 0, after))

    own = {name: [None] * w[name].shape[0] for name in BIG}
    me_arr = me.astype(jnp.int32).reshape(1)
    out = {}
    local_g = [jnp.stack(g[key]).reshape(_global_shape(w[name].shape, axis)) for name, key, axis in SMALL]
    local_g.append(loss_part.reshape(1))
    part = _pack(local_g, chunk).reshape(N_DEV, -1, 128)
    small_plan = _scatter_plan([None])
    s_send, s_recv, s_src, s_land, after = _exchange_start([part], [lax.empty(part.shape, F32)], small_plan,
                                                           "small_scatter_start", after=(last_token[0],))
    for n_flight, (send, recv, srcs, keys, layers) in enumerate(sent):
        if n_flight == len(sent) - 1:
            for name in BIG:
                if BIG_KEY[name] not in keys:
                    out[name] = _adam_big(zone[name], own[name], me_arr, w[name], m[name], v[name], "adam_" + name)
                    after = out[name][0]
            s_src, s_land = _exchange_wait(s_send, s_recv, s_src, s_land, small_plan, after, "small_scatter_wait")
            chunk_sum = _sum_blocks(s_land[0], s_src[0], me_arr)
            r_zone = lax.dynamic_update_slice_in_dim(lax.empty(part.shape, F32), chunk_sum[None], me, 0)
            r_send, r_recv, _, r_land, after = _exchange_start([], [r_zone], GATHER_ALL, "small_gather_start")
        srcs, lands = _exchange_wait(send, recv, srcs, [zone[name_of[k]] for k in keys], _scatter_plan(layers),
                                     after, "scatter_wait_%d" % n_flight)
        for k, land, src, layer in zip(keys, lands, srcs, layers):
            zone[name_of[k]] = land
            own[name_of[k]][layer] = src
        after = lands[0]
    for name in BIG:
        if name not in out:
            out[name] = _adam_big(zone[name], own[name], me_arr, w[name], m[name], v[name], "adam_" + name)
            after = out[name][0]

    _, reduced = _exchange_wait(r_send, r_recv, [], r_land, GATHER_ALL, after, "small_gather_wait")
    reduced = _unpack(reduced[0].reshape(-1), [a.shape for a in local_g])
    loss = reduced[-1][0]
    mine = [a if axis is None else lax.dynamic_slice_in_dim(a, me * w[name].shape[axis], w[name].shape[axis], axis)
            for a, (name, _, axis) in zip(reduced, SMALL)]
    for grad, (name, _, _) in zip(mine, SMALL):
        shape = w[name].shape
        as_2d = lambda a: a.reshape(-1, shape[-1])
        new = _adam_small(as_2d(grad), as_2d(w[name]), as_2d(m[name]), as_2d(v[name]), "adam_" + name)
        out[name] = (grad,) + tuple(a.reshape(shape) for a in new)

    return (loss, grad_x[None]) + tuple(out[name][i] for i in range(4) for name in WEIGHTS)


def kernel(x, positions, ln_mix_g, ln_mix_b, ln_ffn_g, ln_ffn_b, even_w_in, pool_w, pool_scale, lru_conv_w, lru_conv_b, lru_w_a, lru_b_a, lru_w_x, lru_b_x, lru_lambda, even_w_out, mla_w_down, mla_q_norm_g, mla_kv_norm_g, mla_w_qb, mla_w_kvb, mla_w_o, mlp_w1, mlp_w2, loss_target, m_ln_mix_g, m_ln_mix_b, m_ln_ffn_g, m_ln_ffn_b, m_even_w_in, m_pool_w, m_pool_scale, m_lru_conv_w, m_lru_conv_b, m_lru_w_a, m_lru_b_a, m_lru_w_x, m_lru_b_x, m_lru_lambda, m_even_w_out, m_mla_w_down, m_mla_q_norm_g, m_mla_kv_norm_g, m_mla_w_qb, m_mla_w_kvb, m_mla_w_o, m_mlp_w1, m_mlp_w2, v_ln_mix_g, v_ln_mix_b, v_ln_ffn_g, v_ln_ffn_b, v_even_w_in, v_pool_w, v_pool_scale, v_lru_conv_w, v_lru_conv_b, v_lru_w_a, v_lru_b_a, v_lru_w_x, v_lru_b_x, v_lru_lambda, v_even_w_out, v_mla_w_down, v_mla_q_norm_g, v_mla_kv_norm_g, v_mla_w_qb, v_mla_w_kvb, v_mla_w_o, v_mlp_w1, v_mlp_w2):
    w = dict(zip(WEIGHTS, (ln_mix_g, ln_mix_b, ln_ffn_g, ln_ffn_b, even_w_in, pool_w, pool_scale, lru_conv_w,
                           lru_conv_b, lru_w_a, lru_b_a, lru_w_x, lru_b_x, lru_lambda, even_w_out, mla_w_down,
                           mla_q_norm_g, mla_kv_norm_g, mla_w_qb, mla_w_kvb, mla_w_o, mlp_w1, mlp_w2)))
    m = dict(zip(WEIGHTS, (m_ln_mix_g, m_ln_mix_b, m_ln_ffn_g, m_ln_ffn_b, m_even_w_in, m_pool_w, m_pool_scale,
                           m_lru_conv_w, m_lru_conv_b, m_lru_w_a, m_lru_b_a, m_lru_w_x, m_lru_b_x, m_lru_lambda,
                           m_even_w_out, m_mla_w_down, m_mla_q_norm_g, m_mla_kv_norm_g, m_mla_w_qb, m_mla_w_kvb,
                           m_mla_w_o, m_mlp_w1, m_mlp_w2)))
    v = dict(zip(WEIGHTS, (v_ln_mix_g, v_ln_mix_b, v_ln_ffn_g, v_ln_ffn_b, v_even_w_in, v_pool_w, v_pool_scale,
                           v_lru_conv_w, v_lru_conv_b, v_lru_w_a, v_lru_b_a, v_lru_w_x, v_lru_b_x, v_lru_lambda,
                           v_even_w_out, v_mla_w_down, v_mla_q_norm_g, v_mla_kv_norm_g, v_mla_w_qb, v_mla_w_kvb,
                           v_mla_w_o, v_mlp_w1, v_mlp_w2)))
    return _step(x, positions, loss_target, w, m, v)
```

```python
import functools

import jax
import jax.numpy as jnp
from jax import lax
from jax.experimental import pallas as pl
from jax.experimental.pallas import tpu as pltpu

F32 = jnp.float32
BF16 = jnp.bfloat16
S = jax.ShapeDtypeStruct

D = 1024
DEPTH = 4
N_DEV = 8
CHUNK_SHIFT = 6
POOL_WINDOWS = (2, 4, 8, 16)
POOL_W = 512
LRU_W = 1024
LRU_HEADS = 8
HEAD = 128
LRU_C = 8.0
EVEN_IN = 2560
EVEN_MIX = 1536
MLA_HEADS = 8
NOPE = 128
ROPE = 64
VDIM = 128
Q_RANK = 384
KV_RANK = 256
ODD_IN = 704
D_FF = 4096
FF_BLK = D_FF // N_DEV
ROPE_THETA = 10000.0
ALPHA = (2 * DEPTH) ** 0.25
LN_EPS = 1e-5
RMS_EPS = 1e-6
ATT_SCALE = (NOPE + ROPE) ** -0.5
NEG = float(jnp.finfo(jnp.float32).min)
ADAM_LR = 0.001
ADAM_B1 = 0.9
ADAM_B2 = 0.999
ADAM_EPS = 1e-08
ADAM_WD = 0.01
ADAM_STEP = 10
V7X_VMEM_BYTES = 64 * 1024 * 1024
VMEM_LIMIT = V7X_VMEM_BYTES - 8 * 1024 * 1024
MESH = pl.DeviceIdType.MESH


def _cp(*sem):
    return pltpu.CompilerParams(dimension_semantics=sem if sem else None, vmem_limit_bytes=VMEM_LIMIT)


def _dot(a, b):
    return jnp.dot(a, b, preferred_element_type=F32)


def _dot_nt(a, b):
    return lax.dot_general(a, b, (((1,), (1,)), ((), ())), preferred_element_type=F32)


def _dot_tn(a, b):
    return lax.dot_general(a, b, (((0,), (0,)), ((), ())), preferred_element_type=F32)


def _full(shape):
    return pl.BlockSpec(shape, lambda *_: (0,) * len(shape))


def _mm(a, b, *, mode, grid, a_spec, b_spec, out_shape, out_spec, name, add=None, add_spec=None, add_scale=1.0,
        dep=None):
    dot = {"nn": _dot, "nt": _dot_nt, "tn": _dot_tn}[mode]

    def body(*refs):
        a_ref, b_ref, o_ref = refs[0], refs[1], refs[-1]
        acc = dot(a_ref[...].astype(BF16), b_ref[...].astype(BF16))
        if add is not None:
            acc = acc + add_scale * refs[2][...]
        o_ref[...] = acc.astype(o_ref.dtype)

    ops = [a, b] if add is None else [a, b, add]
    specs = [a_spec, b_spec] if add is None else [a_spec, b_spec, add_spec]
    if dep is not None:
        ops.append(dep)
        specs.append(pl.BlockSpec(memory_space=pl.ANY))
    return pl.pallas_call(body, grid=grid, in_specs=specs, out_specs=out_spec, out_shape=out_shape,
                          compiler_params=_cp(*(("parallel",) * len(grid))), name=name)(*ops)


def _ln_stats(z):
    mu = jnp.mean(z, axis=-1, keepdims=True)
    zc = z - mu
    var = jnp.mean(zc * zc, axis=-1, keepdims=True)
    rstd = lax.rsqrt(var + LN_EPS)
    return zc * rstd, rstd


def _row_tile(t):
    return min(512, t)


def _resid_ln(x, mix, g3, b3, l, name):
    t = x.shape[0]
    bm = _row_tile(t)

    def body(x_ref, m_ref, g_ref, b_ref, z_ref, y_ref, yb_ref):
        z = ALPHA * x_ref[...] + m_ref[...]
        xh, _ = _ln_stats(z)
        y = xh * g_ref[...] + b_ref[...]
        z_ref[...] = z
        y_ref[...] = y
        yb_ref[...] = y.astype(BF16)

    row = pl.BlockSpec((bm, D), lambda i: (i, 0))
    vec = pl.BlockSpec((None, 1, D), lambda i: (l, 0, 0))
    return pl.pallas_call(body, grid=(t // bm,), in_specs=[row, row, vec, vec], out_specs=[row, row, row],
                          out_shape=[S((t, D), F32), S((t, D), F32), S((t, D), BF16)],
                          compiler_params=_cp("parallel"), name=name)(x, mix, g3, b3)


def _proj_resid_ln(x, a, wmat, g3, b3, l, name):
    t, k = a.shape
    bm = _row_tile(t)

    def body(x_ref, a_ref, w_ref, g_ref, b_ref, z_ref, y_ref, yb_ref):
        z = ALPHA * x_ref[...] + _dot(a_ref[...], w_ref[...])
        xh, _ = _ln_stats(z)
        y = xh * g_ref[...] + b_ref[...]
        z_ref[...] = z
        y_ref[...] = y
        yb_ref[...] = y.astype(BF16)

    row = pl.BlockSpec((bm, D), lambda i: (i, 0))
    vec = pl.BlockSpec((None, 1, D), lambda i: (l, 0, 0))
    return pl.pallas_call(body, grid=(t // bm,),
                          in_specs=[row, pl.BlockSpec((bm, k), lambda i: (i, 0)), _full((k, D)), vec, vec],
                          out_specs=[row, row, row], out_shape=[S((t, D), F32), S((t, D), F32), S((t, D), BF16)],
                          compiler_params=_cp("parallel"), name=name)(x, a, wmat, g3, b3)


def _ln_bwd(d, z, g3, l, name, r=None, dep=None):
    t = z.shape[0]
    bm = _row_tile(t)

    def body(*refs):
        refs = list(refs)
        d_ref = refs.pop(0)
        dy = d_ref[...]
        if r is not None:
            dy = dy + ALPHA * refs.pop(0)[...]
        z_ref, g_ref = refs.pop(0), refs.pop(0)
        if dep is not None:
            refs.pop(0)
        dz_ref, dzb_ref, dg_ref, db_ref = refs
        xh, rstd = _ln_stats(z_ref[...])
        dyg = dy * g_ref[...]
        m1 = jnp.mean(dyg, axis=-1, keepdims=True)
        m2 = jnp.mean(dyg * xh, axis=-1, keepdims=True)
        dz = rstd * (dyg - m1 - xh * m2)
        dz_ref[...] = dz
        dzb_ref[...] = dz.astype(BF16)

        @pl.when(pl.program_id(0) == 0)
        def _():
            dg_ref[...] = jnp.zeros_like(dg_ref)
            db_ref[...] = jnp.zeros_like(db_ref)

        dg_ref[...] += jnp.sum(dy * xh, axis=0, keepdims=True)
        db_ref[...] += jnp.sum(dy, axis=0, keepdims=True)

    row = pl.BlockSpec((bm, D), lambda i: (i, 0))
    vec = pl.BlockSpec((None, 1, D), lambda i: (l, 0, 0))
    acc = pl.BlockSpec((1, D), lambda i: (0, 0))
    ops = [d, z, g3] if r is None else [d, r, z, g3]
    specs = [row, row, vec] if r is None else [row, row, row, vec]
    if dep is not None:
        ops.append(dep)
        specs.append(_full(dep.shape))
    return pl.pallas_call(body, grid=(t // bm,), in_specs=specs, out_specs=[row, row, acc, acc],
                          out_shape=[S((t, D), F32), S((t, D), BF16), S((1, D), F32), S((1, D), F32)],
                          compiler_params=_cp("arbitrary"), name=name)(*ops)


def _loss_grad(y, tgt):
    t = y.shape[0]
    bm = _row_tile(t)

    def body(y_ref, t_ref, dy_ref, loss_ref, acc_ref):
        i = pl.program_id(0)
        e = y_ref[...] - t_ref[...]
        dy_ref[...] = e * (1.0 / D)

        @pl.when(i == 0)
        def _():
            acc_ref[...] = jnp.zeros_like(acc_ref)

        acc_ref[...] += jnp.sum(e * e, axis=0, keepdims=True)

        @pl.when(i == pl.num_programs(0) - 1)
        def _():
            loss_ref[...] = jnp.full(loss_ref.shape, (0.5 / D) * jnp.sum(acc_ref[...]), F32)

    row = pl.BlockSpec((bm, D), lambda i: (i, 0))
    return pl.pallas_call(body, grid=(t // bm,), in_specs=[row, row],
                          out_specs=[row, pl.BlockSpec((1, 128), lambda i: (0, 0))],
                          out_shape=[S((t, D), F32), S((1, 128), F32)],
                          scratch_shapes=[pltpu.VMEM((1, D), F32)],
                          compiler_params=_cp("arbitrary"), name="loss_grad")(y, tgt)


def _mlp_row_tile(t):
    return min(1024, t)


def _mlp_fwd(y, yb, w1g, w2g, g3, b3, l, dep=None):
    t = yb.shape[0]
    bm = _mlp_row_tile(t)

    def body(*refs):
        y_ref, yb_ref, w1_ref, w2_ref, g_ref, b_ref = refs[:6]
        z_ref, o_ref, ob_ref, acc_ref = refs[-4:]
        j = pl.program_id(1)
        h = jnp.maximum(_dot(yb_ref[...], w1_ref[...]), 0.0)
        c = _dot((h * h).astype(BF16), w2_ref[...])

        @pl.when(j == 0)
        def _():
            acc_ref[...] = c

        @pl.when(j > 0)
        def _():
            acc_ref[...] += c

        @pl.when(j == N_DEV - 1)
        def _():
            z = ALPHA * y_ref[...] + acc_ref[...]
            xh, _ = _ln_stats(z)
            out = xh * g_ref[...] + b_ref[...]
            z_ref[...] = z
            o_ref[...] = out
            ob_ref[...] = out.astype(BF16)

    row = pl.BlockSpec((bm, D), lambda i, j: (i, 0))
    vec = pl.BlockSpec((None, 1, D), lambda i, j: (l, 0, 0))
    deps = [] if dep is None else [dep]
    return pl.pallas_call(
        body, grid=(t // bm, N_DEV),
        in_specs=[row, row, pl.BlockSpec((None, D, FF_BLK), lambda i, j: (j, 0, 0)),
                  pl.BlockSpec((None, FF_BLK, D), lambda i, j: (j, 0, 0)), vec, vec] + [ANY] * len(deps),
        out_specs=[row, row, row], out_shape=[S((t, D), F32), S((t, D), F32), S((t, D), BF16)],
        scratch_shapes=[pltpu.VMEM((bm, D), F32)],
        compiler_params=_cp("parallel", "arbitrary"), name="mlp_fwd")(y, yb, w1g, w2g, g3, b3, *deps)


def _mlp_bwd_dh(yb, dzb, w1g, w2g):
    t = yb.shape[0]
    bm = _mlp_row_tile(t)

    def body(y_ref, dz_ref, w1_ref, w2_ref, a_ref, dh_ref, acc_ref):
        j = pl.program_id(1)
        r = jnp.maximum(_dot(y_ref[...], w1_ref[...]), 0.0)
        a_ref[...] = (r * r).astype(BF16)
        da = _dot_nt(dz_ref[...], w2_ref[...])
        dh = (da * (2.0 * r)).astype(BF16)
        dh_ref[...] = dh
        c = _dot_nt(dh, w1_ref[...])

        @pl.when(j == 0)
        def _():
            acc_ref[...] = c

        @pl.when(j > 0)
        def _():
            acc_ref[...] += c

    row = pl.BlockSpec((bm, D), lambda i, j: (i, 0))
    hid = pl.BlockSpec((bm, FF_BLK), lambda i, j: (i, j))
    return pl.pallas_call(
        body, grid=(t // bm, N_DEV),
        in_specs=[row, row,
                  pl.BlockSpec((None, D, FF_BLK), lambda i, j: (j, 0, 0)),
                  pl.BlockSpec((None, FF_BLK, D), lambda i, j: (j, 0, 0))],
        out_specs=[hid, hid, row],
        out_shape=[S((t, D_FF), BF16), S((t, D_FF), BF16), S((t, D), F32)],
        compiler_params=_cp("parallel", "arbitrary"), name="mlp_bwd_dh")(yb, dzb, w1g, w2g)


def _shift_dn(x, k, rows, fill=0.0):
    return jnp.where(rows >= k, pltpu.roll(x, k, 0), fill)


def _shift_up(x, k, rows, fill=0.0):
    t = x.shape[0]
    return jnp.where(rows < t - k, pltpu.roll(x, t - k, 0), fill)


def _scan_dn(a, b, rows):
    k = 1
    t = a.shape[0]
    while k < t:
        b = a * _shift_dn(b, k, rows) + b
        if 2 * k < t:
            a = a * _shift_dn(a, k, rows, 1.0)
        k *= 2
    return b


def _scan_up(a, b, rows):
    k = 1
    t = a.shape[0]
    while k < t:
        b = a * _shift_up(b, k, rows) + b
        if 2 * k < t:
            a = a * _shift_up(a, k, rows, 1.0)
        k *= 2
    return b


def _window_sum_dn(x, w, rows):
    k = 1
    while k < w:
        x = x + _shift_dn(x, k, rows)
        k *= 2
    return x


def _window_sum_up(x, w, rows):
    k = 1
    while k < w:
        x = x + _shift_up(x, k, rows)
        k *= 2
    return x


def _pool_diff(u, w, rows):
    inv_count = 1.0 / jnp.minimum(rows + 1, w).astype(F32)
    return _window_sum_dn(u, w, rows) * inv_count - u, inv_count


def _pool_fwd(proj, pool_w, pool_scale3, j):
    t = proj.shape[0]

    def body(u_ref, w_ref, s_ref, y_ref):
        rows = lax.broadcasted_iota(jnp.int32, (t, HEAD), 0)
        for g, w in enumerate(POOL_WINDOWS):
            cols = slice(g * HEAD, (g + 1) * HEAD)
            d, _ = _pool_diff(u_ref[:, cols], w, rows)
            y = _dot(d.astype(BF16), w_ref[g].astype(BF16)) * s_ref[:, cols]
            y_ref[:, cols] = y.astype(BF16)

    return pl.pallas_call(
        body, grid=(1,),
        in_specs=[pl.BlockSpec((t, POOL_W), lambda i: (0, 0)),
                  pl.BlockSpec((None, 4, HEAD, HEAD), lambda i: (j, 0, 0, 0)),
                  pl.BlockSpec((None, 1, POOL_W), lambda i: (j, 0, 0))],
        out_specs=pl.BlockSpec((t, POOL_W), lambda i: (0, 0)),
        out_shape=S((t, POOL_W), BF16), compiler_params=_cp("arbitrary"), name="pool_fwd")(proj, pool_w, pool_scale3)


def _pool_bwd(proj, dycat, pool_w, pool_scale3, j):
    t = proj.shape[0]

    def body(u_ref, dy_ref, w_ref, s_ref, du_ref, dw_ref, ds_ref):
        rows = lax.broadcasted_iota(jnp.int32, (t, HEAD), 0)
        for g, w in enumerate(POOL_WINDOWS):
            cols = slice(g * HEAD, (g + 1) * HEAD)
            d, inv_count = _pool_diff(u_ref[:, cols], w, rows)
            db = d.astype(BF16)
            wg = w_ref[g].astype(BF16)
            dy = dy_ref[:, cols]
            ds_ref[:, cols] = jnp.sum(dy * _dot(db, wg), axis=0, keepdims=True)
            dzz = (dy * s_ref[:, cols]).astype(BF16)
            dw_ref[g] = _dot_tn(db, dzz)
            dd = _dot_nt(dzz, wg)
            du_ref[:, cols] = (_window_sum_up(dd * inv_count, w, rows) - dd).astype(BF16)

    return pl.pallas_call(
        body, grid=(1,),
        in_specs=[pl.BlockSpec((t, POOL_W), lambda i: (0, 0)),
                  pl.BlockSpec((t, POOL_W), lambda i: (0, 0)),
                  pl.BlockSpec((None, 4, HEAD, HEAD), lambda i: (j, 0, 0, 0)),
                  pl.BlockSpec((None, 1, POOL_W), lambda i: (j, 0, 0))],
        out_specs=[pl.BlockSpec((t, POOL_W), lambda i: (0, 0)), _full((4, HEAD, HEAD)), _full((1, POOL_W))],
        out_shape=[S((t, POOL_W), BF16), S((4, HEAD, HEAD), F32), S((1, POOL_W), F32)],
        compiler_params=_cp("arbitrary"), name="pool_bwd")(proj, dycat, pool_w, pool_scale3)


GELU_C = 0.7978845608028654
GELU_K = 0.044715


def _gelu(x):
    th = jnp.tanh(GELU_C * (x + GELU_K * x * x * x))
    return 0.5 * x * (1.0 + th), th


def _lru_forward(u, gate, cw, cb, wa, ba, wx, bx, lam, rows):
    v = cw[3:4] * u + cw[2:3] * _shift_dn(u, 1, rows) + cw[1:2] * _shift_dn(u, 2, rows) \
        + cw[0:1] * _shift_dn(u, 3, rows) + cb
    vb = v.astype(BF16)
    r = jax.nn.sigmoid(_dot(vb, wa) + ba)
    i = jax.nn.sigmoid(_dot(vb, wx) + bx)
    sp = jnp.maximum(-lam, 0.0) + jnp.log1p(jnp.exp(-jnp.abs(lam)))
    log_a = (-LRU_C) * r * sp
    a = jnp.exp(log_a)
    one_m_a2 = -jnp.tanh(log_a) * (a * a + 1.0)
    mult = jnp.sqrt(one_m_a2)
    h = _scan_dn(a, mult * (i * v), rows)
    gl, th = _gelu(gate)
    return dict(v=v, vb=vb, r=r, i=i, sp=sp, a=a, mult=mult, h=h, gl=gl, th=th)


def _lru_specs(t, j, col0_u, col0_g):
    blk = lambda c0: pl.BlockSpec((t, HEAD), lambda h: (0, c0 + h))
    vec = pl.BlockSpec((None, 1, HEAD), lambda h: (j, 0, h))
    return [blk(col0_u), blk(col0_g),
            pl.BlockSpec((None, 4, HEAD), lambda h: (j, 0, h)), vec,
            pl.BlockSpec((None, None, HEAD, HEAD), lambda h: (j, h, 0, 0)), vec,
            pl.BlockSpec((None, None, HEAD, HEAD), lambda h: (j, h, 0, 0)), vec, vec]


def _lru_fwd(proj, p, j):
    t = proj.shape[0]

    def body(u_ref, g_ref, cw_ref, cb_ref, wa_ref, ba_ref, wx_ref, bx_ref, lam_ref, y_ref):
        rows = lax.broadcasted_iota(jnp.int32, (t, HEAD), 0)
        f = _lru_forward(u_ref[...], g_ref[...], cw_ref[...], cb_ref[...], wa_ref[...].astype(BF16), ba_ref[...],
                         wx_ref[...].astype(BF16), bx_ref[...], lam_ref[...], rows)
        y_ref[...] = (f["h"] * f["gl"]).astype(BF16)

    return pl.pallas_call(
        body, grid=(LRU_HEADS,), in_specs=_lru_specs(t, j, POOL_W // HEAD, (POOL_W + LRU_W) // HEAD),
        out_specs=pl.BlockSpec((t, HEAD), lambda h: (0, h)), out_shape=S((t, LRU_W), BF16),
        compiler_params=_cp("parallel"), name="lru_fwd")(
            proj, proj, p["conv_w"], p["conv_b"], p["w_a"], p["b_a"], p["w_x"], p["b_x"], p["lam"])


def _lru_bwd(proj, dycat, p, j):
    t = proj.shape[0]

    def body(u_ref, g_ref, cw_ref, cb_ref, wa_ref, ba_ref, wx_ref, bx_ref, lam_ref, dy_ref,
             du_ref, dgate_ref, dcw_ref, dcb_ref, dwa_ref, dba_ref, dwx_ref, dbx_ref, dlam_ref):
        rows = lax.broadcasted_iota(jnp.int32, (t, HEAD), 0)
        u = u_ref[...]
        gate = g_ref[...]
        cw = cw_ref[...]
        wa = wa_ref[...].astype(BF16)
        wx = wx_ref[...].astype(BF16)
        lam = lam_ref[...]
        f = _lru_forward(u, gate, cw, cb_ref[...], wa, ba_ref[...], wx, bx_ref[...], lam, rows)
        v, r, i, a, mult, h, th = f["v"], f["r"], f["i"], f["a"], f["mult"], f["h"], f["th"]
        dy = dy_ref[...]
        dgl = 0.5 * (1.0 + th) + 0.5 * gate * (1.0 - th * th) * GELU_C * (1.0 + 3.0 * GELU_K * gate * gate)
        dgate_ref[...] = (dy * h * dgl).astype(BF16)
        g = _scan_up(_shift_up(a, 1, rows), dy * f["gl"], rows)
        da = g * _shift_dn(h, 1, rows)
        iv = i * v
        dmult = g * iv
        di = g * mult * v
        dv = g * mult * i
        dlog_a = da * a - dmult * (a * a) / mult
        dr = dlog_a * (-LRU_C) * f["sp"]
        dsp = jnp.sum(dlog_a * (-LRU_C) * r, axis=0, keepdims=True)
        dlam_ref[...] = -dsp * jax.nn.sigmoid(-lam)
        dpa = dr * r * (1.0 - r)
        dpx = di * i * (1.0 - i)
        dpab = dpa.astype(BF16)
        dpxb = dpx.astype(BF16)
        dwa_ref[...] = _dot_tn(f["vb"], dpab)
        dwx_ref[...] = _dot_tn(f["vb"], dpxb)
        dba_ref[...] = jnp.sum(dpa, axis=0, keepdims=True)
        dbx_ref[...] = jnp.sum(dpx, axis=0, keepdims=True)
        dv = dv + _dot_nt(dpab, wa) + _dot_nt(dpxb, wx)
        dcb_ref[...] = jnp.sum(dv, axis=0, keepdims=True)
        du = cw[3:4] * dv
        dcw_ref[3:4, :] = jnp.sum(dv * u, axis=0, keepdims=True)
        for k in (1, 2, 3):
            du = du + cw[3 - k:4 - k] * _shift_up(dv, k, rows)
            dcw_ref[3 - k:4 - k, :] = jnp.sum(dv * _shift_dn(u, k, rows), axis=0, keepdims=True)
        du_ref[...] = du.astype(BF16)

    blk = pl.BlockSpec((t, HEAD), lambda h: (0, h))
    vec = pl.BlockSpec((1, HEAD), lambda h: (0, h))
    mat = pl.BlockSpec((None, HEAD, HEAD), lambda h: (h, 0, 0))
    return pl.pallas_call(
        body, grid=(LRU_HEADS,),
        in_specs=_lru_specs(t, j, POOL_W // HEAD, (POOL_W + LRU_W) // HEAD)
        + [pl.BlockSpec((t, HEAD), lambda h: (0, POOL_W // HEAD + h))],
        out_specs=[blk, blk, pl.BlockSpec((4, HEAD), lambda h: (0, h)), vec, mat, vec, mat, vec, vec],
        out_shape=[S((t, LRU_W), BF16), S((t, LRU_W), BF16), S((4, LRU_W), F32), S((1, LRU_W), F32),
                   S((LRU_HEADS, HEAD, HEAD), F32), S((1, LRU_W), F32),
                   S((LRU_HEADS, HEAD, HEAD), F32), S((1, LRU_W), F32), S((1, LRU_W), F32)],
        compiler_params=_cp("parallel"), name="lru_bwd")(
            proj, proj, p["conv_w"], p["conv_b"], p["w_a"], p["b_a"], p["w_x"], p["b_x"], p["lam"], dycat)


def _rope(x, c, s):
    x1 = x[:, :ROPE // 2]
    x2 = x[:, ROPE // 2:]
    return jnp.concatenate([x1 * c - x2 * s, x1 * s + x2 * c], axis=-1)


def _rope_t(d, c, s):
    d1 = d[:, :ROPE // 2]
    d2 = d[:, ROPE // 2:]
    return jnp.concatenate([d1 * c + d2 * s, d2 * c - d1 * s], axis=-1)


def _rope_tables(pos2, inv_freq):
    t = pos2.shape[0]

    def body(p_ref, f_ref, c_ref, s_ref):
        ang = p_ref[...].astype(F32) * f_ref[...]
        c_ref[...] = jnp.cos(ang)
        s_ref[...] = jnp.sin(ang)

    return pl.pallas_call(body, out_shape=[S((t, ROPE // 2), F32), S((t, ROPE // 2), F32)],
                          name="rope_tables")(pos2, inv_freq)


def _down_norm(xb, wdown_g, gq3, gkv3, cos, sin, j):
    t = xb.shape[0]
    bm = _row_tile(t)

    def body(x_ref, w_ref, gq_ref, gkv_ref, c_ref, s_ref, down_ref, cq_ref, ckv_ref, kpe_ref):
        w = w_ref[...].reshape(D, ODD_IN)
        down = _dot(x_ref[...], w)
        down_ref[...] = down
        q = down[:, :Q_RANK]
        cq_ref[...] = (q * lax.rsqrt(jnp.mean(q * q, axis=-1, keepdims=True) + RMS_EPS) * gq_ref[...]).astype(BF16)
        kv = down[:, Q_RANK:Q_RANK + KV_RANK]
        ckv_ref[...] = (kv * lax.rsqrt(jnp.mean(kv * kv, axis=-1, keepdims=True) + RMS_EPS)
                        * gkv_ref[...]).astype(BF16)
        kpe_ref[...] = _rope(down[:, Q_RANK + KV_RANK:], c_ref[...], s_ref[...])

    row = lambda n: pl.BlockSpec((bm, n), lambda i: (i, 0))
    return pl.pallas_call(
        body, grid=(t // bm,),
        in_specs=[row(D), _full((N_DEV, D // N_DEV, ODD_IN)),
                  pl.BlockSpec((None, 1, Q_RANK), lambda i: (j, 0, 0)),
                  pl.BlockSpec((None, 1, KV_RANK), lambda i: (j, 0, 0)), row(ROPE // 2), row(ROPE // 2)],
        out_specs=[row(ODD_IN), row(Q_RANK), row(KV_RANK), row(ROPE)],
        out_shape=[S((t, ODD_IN), F32), S((t, Q_RANK), BF16), S((t, KV_RANK), BF16), S((t, ROPE), F32)],
        compiler_params=_cp("parallel"), name="down_norm")(xb, wdown_g, gq3, gkv3, cos, sin)


def _q_tile(t, widest):
    return min(widest, t // 2)


def _attn_probs(q, k, qs):
    s = _dot_nt(q, k) * ATT_SCALE
    tq = q.shape[0]
    rows = lax.broadcasted_iota(jnp.int32, (tq, tq), 0)
    cols = lax.broadcasted_iota(jnp.int32, (tq, tq), 1)
    last = jnp.where(jnp.right_shift(cols, CHUNK_SHIFT) <= jnp.right_shift(rows, CHUNK_SHIFT), s[:, qs:], NEG)
    s = last if qs == 0 else jnp.concatenate([s[:, :qs], last], axis=1)
    e = jnp.exp(s - jnp.max(s, axis=-1, keepdims=True))
    return e / jnp.sum(e, axis=-1, keepdims=True)


def _head_qkv(cq, ckv, kpe, c, s, wq_ref, wkv_ref):
    q = jnp.concatenate([_dot(cq, wq_ref[:, :NOPE]), _rope(_dot(cq, wq_ref[:, NOPE:]), c, s)], axis=1).astype(BF16)
    k = jnp.concatenate([_dot(ckv, wkv_ref[:, :NOPE]), kpe], axis=1).astype(BF16)
    vv = _dot(ckv, wkv_ref[:, NOPE:]).astype(BF16)
    return q, k, vv


def _attn_in_specs(t):
    return [_full((t, Q_RANK)), _full((t, KV_RANK)), _full((t, ROPE)), _full((t, ROPE // 2)), _full((t, ROPE // 2)),
            pl.BlockSpec((None, Q_RANK, NOPE + ROPE), lambda h: (h, 0, 0)),
            pl.BlockSpec((None, KV_RANK, NOPE + VDIM), lambda h: (h, 0, 0)),
            pl.BlockSpec((None, VDIM, D), lambda h: (h, 0, 0))]


def _attn_fwd(cq, ckv, kpe, cos, sin, wqb_g, wkvb_g, wo_g):
    t = cq.shape[0]
    tq = _q_tile(t, 256)

    def body(cq_ref, ckv_ref, kpe_ref, c_ref, s_ref, wq_ref, wkv_ref, wo_ref, o_ref, mix_ref):
        q, k, vv = _head_qkv(cq_ref[...], ckv_ref[...], kpe_ref[...], c_ref[...], s_ref[...], wq_ref, wkv_ref)
        for qs in range(0, t, tq):
            ke = qs + tq
            p = _attn_probs(q[qs:ke], k[:ke], qs)
            o_ref[qs:ke, :] = _dot(p.astype(BF16), vv[:ke]).astype(BF16)
        c = _dot(o_ref[...], wo_ref[...])

        @pl.when(pl.program_id(0) == 0)
        def _():
            mix_ref[...] = c

        @pl.when(pl.program_id(0) > 0)
        def _():
            mix_ref[...] += c

    return pl.pallas_call(
        body, grid=(MLA_HEADS,), in_specs=_attn_in_specs(t),
        out_specs=[pl.BlockSpec((None, t, VDIM), lambda h: (h, 0, 0)), _full((t, D))],
        out_shape=[S((MLA_HEADS, t, VDIM), BF16), S((t, D), F32)],
        compiler_params=_cp("arbitrary"), name="attn_fwd")(cq, ckv, kpe, cos, sin, wqb_g, wkvb_g, wo_g)


def _attn_bwd(cq, ckv, kpe, cos, sin, wqb_g, wkvb_g, wo_g, o, dzb):
    t = cq.shape[0]
    tq = _q_tile(t, 512)

    def body(cq_ref, ckv_ref, kpe_ref, c_ref, s_ref, wq_ref, wkv_ref, wo_ref, o_ref, dz_ref,
             dwo_ref, dwq_ref, dwkv_ref, dcq_ref, dckv_ref, dkpe_ref, dk_s, dv_s, dq_s):
        cqv = cq_ref[...]
        ckvv = ckv_ref[...]
        c = c_ref[...]
        s = s_ref[...]
        q, k, vv = _head_qkv(cqv, ckvv, kpe_ref[...], c, s, wq_ref, wkv_ref)
        dzv = dz_ref[...]
        dwo_ref[...] = _dot_tn(o_ref[...], dzv).astype(BF16)
        do = _dot_nt(dzv, wo_ref[...]).astype(BF16)
        dk_s[...] = jnp.zeros_like(dk_s)
        dv_s[...] = jnp.zeros_like(dv_s)
        for qs in range(0, t, tq):
            ke = qs + tq
            p = _attn_probs(q[qs:ke], k[:ke], qs)
            dp = _dot_nt(do[qs:ke], vv[:ke])
            ds = (p * (dp - jnp.sum(p * dp, axis=-1, keepdims=True)) * ATT_SCALE).astype(BF16)
            dq_s[qs:ke, :] = _dot(ds, k[:ke])
            dk_s[0:ke, :] += _dot_tn(ds, q[qs:ke])
            dv_s[0:ke, :] += _dot_tn(p.astype(BF16), do[qs:ke])
        dqn = dq_s[:, :NOPE].astype(BF16)
        dqp = _rope_t(dq_s[:, NOPE:], c, s).astype(BF16)
        dkn = dk_s[:, :NOPE].astype(BF16)
        dvv = dv_s[...].astype(BF16)
        dwq_ref[:, :NOPE] = _dot_tn(cqv, dqn).astype(BF16)
        dwq_ref[:, NOPE:] = _dot_tn(cqv, dqp).astype(BF16)
        dwkv_ref[:, :NOPE] = _dot_tn(ckvv, dkn).astype(BF16)
        dwkv_ref[:, NOPE:] = _dot_tn(ckvv, dvv).astype(BF16)
        dcq = _dot_nt(dqn, wq_ref[:, :NOPE]) + _dot_nt(dqp, wq_ref[:, NOPE:])
        dckv = _dot_nt(dkn, wkv_ref[:, :NOPE]) + _dot_nt(dvv, wkv_ref[:, NOPE:])

        @pl.when(pl.program_id(0) == 0)
        def _():
            dcq_ref[...] = dcq
            dckv_ref[...] = dckv
            dkpe_ref[...] = dk_s[:, NOPE:]

        @pl.when(pl.program_id(0) > 0)
        def _():
            dcq_ref[...] += dcq
            dckv_ref[...] += dckv
            dkpe_ref[...] += dk_s[:, NOPE:]

    per_head = lambda a, b: pl.BlockSpec((None, a, b), lambda h: (h, 0, 0))
    return pl.pallas_call(
        body, grid=(MLA_HEADS,),
        in_specs=_attn_in_specs(t) + [per_head(t, VDIM), _full((t, D))],
        out_specs=[per_head(VDIM, D), per_head(Q_RANK, NOPE + ROPE), per_head(KV_RANK, NOPE + VDIM),
                   _full((t, Q_RANK)), _full((t, KV_RANK)), _full((t, ROPE))],
        out_shape=[S((MLA_HEADS, VDIM, D), BF16), S((MLA_HEADS, Q_RANK, NOPE + ROPE), BF16),
                   S((MLA_HEADS, KV_RANK, NOPE + VDIM), BF16),
                   S((t, Q_RANK), F32), S((t, KV_RANK), F32), S((t, ROPE), F32)],
        scratch_shapes=[pltpu.VMEM((t, NOPE + ROPE), F32), pltpu.VMEM((t, VDIM), F32),
                        pltpu.VMEM((t, NOPE + ROPE), F32)],
        compiler_params=_cp("arbitrary"), name="attn_bwd")(cq, ckv, kpe, cos, sin, wqb_g, wkvb_g, wo_g, o, dzb)


def _rms_bwd(down, dcq, dckv, dkpe, cos, sin, gq3, gkv3, j):
    t = down.shape[0]
    bm = _row_tile(t)

    def body(down_ref, dcq_ref, dckv_ref, dkpe_ref, c_ref, s_ref, gq_ref, gkv_ref, dd_ref, dgq_ref, dgkv_ref):
        @pl.when(pl.program_id(0) == 0)
        def _():
            dgq_ref[...] = jnp.zeros_like(dgq_ref)
            dgkv_ref[...] = jnp.zeros_like(dgkv_ref)

        def rms_b(x, dy, g):
            rstd = lax.rsqrt(jnp.mean(x * x, axis=-1, keepdims=True) + RMS_EPS)
            xh = x * rstd
            dyg = dy * g
            return rstd * (dyg - xh * jnp.mean(dyg * xh, axis=-1, keepdims=True)), jnp.sum(dy * xh, axis=0, keepdims=True)

        dq, dgq = rms_b(down_ref[:, :Q_RANK], dcq_ref[...], gq_ref[...])
        dkv, dgkv = rms_b(down_ref[:, Q_RANK:Q_RANK + KV_RANK], dckv_ref[...], gkv_ref[...])
        dgq_ref[...] += dgq
        dgkv_ref[...] += dgkv
        dd_ref[:, :Q_RANK] = dq.astype(BF16)
        dd_ref[:, Q_RANK:Q_RANK + KV_RANK] = dkv.astype(BF16)
        dd_ref[:, Q_RANK + KV_RANK:] = _rope_t(dkpe_ref[...], c_ref[...], s_ref[...]).astype(BF16)

    row = lambda n: pl.BlockSpec((bm, n), lambda i: (i, 0))
    return pl.pallas_call(
        body, grid=(t // bm,),
        in_specs=[row(ODD_IN), row(Q_RANK), row(KV_RANK), row(ROPE), row(ROPE // 2), row(ROPE // 2),
                  pl.BlockSpec((None, 1, Q_RANK), lambda i: (j, 0, 0)),
                  pl.BlockSpec((None, 1, KV_RANK), lambda i: (j, 0, 0))],
        out_specs=[row(ODD_IN), _full((1, Q_RANK)), _full((1, KV_RANK))],
        out_shape=[S((t, ODD_IN), BF16), S((1, Q_RANK), F32), S((1, KV_RANK), F32)],
        compiler_params=_cp("arbitrary"), name="rms_bwd")(down, dcq, dckv, dkpe, cos, sin, gq3, gkv3)


def _col_blocks(t, n, bn):
    return pl.BlockSpec((t, bn), lambda i: (0, i))


def _row_blocks(n, bm):
    return pl.BlockSpec((bm, n), lambda i: (i, 0))


def _local_step(x, pos2, tgt, small, weights_of, grads_done, start_dep=None, prefetch=None):
    t = x.shape[0]
    bm = _row_tile(t)
    inv_freq = (ROPE_THETA ** (-jnp.arange(0, ROPE, 2, dtype=F32) / ROPE)).reshape(1, ROPE // 2)
    cos, sin = _rope_tables(pos2, inv_freq)
    lru_p = {k: small[k] for k in ("conv_w", "conv_b", "w_a", "b_a", "w_x", "b_x", "lam")}

    saved = []
    y, yb = x, x.astype(BF16)
    for l in range(DEPTH):
        j = l // 2
        big = weights_of(l, 0, y)
        sv = dict(xb=yb, big=big)
        if l % 2 == 0:
            proj = _mm(yb, big["win2d"], mode="nn", grid=(EVEN_IN // 512,), a_spec=_full((t, D)),
                       b_spec=_col_blocks(D, EVEN_IN, 512), out_shape=S((t, EVEN_IN), F32),
                       out_spec=_col_blocks(t, EVEN_IN, 512), name="even_proj", dep=start_dep if l == 0 else None)
            ycat = jnp.concatenate([_pool_fwd(proj, small["pool_w"], small["pool_scale"], j),
                                    _lru_fwd(proj, lru_p, j)], axis=1)
            big.update(weights_of(l, 1, ycat))
            z1, y1, y1b = _proj_resid_ln(y, ycat, big["wout2d"], small["ln_mix_g"], small["ln_mix_b"], l, "even_out")
            sv.update(proj=proj, ycat=ycat)
        else:
            down, cq, ckv, kpe = _down_norm(yb, big["wdown"], small["gq"], small["gkv"], cos, sin, j)
            o, mix = _attn_fwd(cq, ckv, kpe, cos, sin, big["wqb"], big["wkvb"], big["wo"])
            z1, y1, y1b = _resid_ln(y, mix, small["ln_mix_g"], small["ln_mix_b"], l, "resid_ln")
            sv.update(down=down, cq=cq, ckv=ckv, kpe=kpe, o=o)
        fetched = prefetch(l + 1, y1) if prefetch is not None and l + 1 < DEPTH else None
        z2, y, yb = _mlp_fwd(y1, y1b, big["w1"], big["w2"], small["ln_ffn_g"], small["ln_ffn_b"], l, dep=fetched)
        sv.update(z1=z1, y1b=y1b, z2=z2)
        saved.append(sv)

    dy, loss_tile = _loss_grad(y, tgt)

    g = {k: [None] * n for k, n in (("ln_mix_g", 4), ("ln_mix_b", 4), ("ln_ffn_g", 4), ("ln_ffn_b", 4),
                                    ("pool_w", 2), ("pool_scale", 2), ("conv_w", 2), ("conv_b", 2),
                                    ("w_a", 2), ("b_a", 2), ("w_x", 2), ("b_x", 2), ("lam", 2),
                                    ("gq", 2), ("gkv", 2))}
    dep = None
    for l in reversed(range(DEPTH)):
        j = l // 2
        sv = saved[l]
        big = sv["big"]
        dz2, dz2b, g["ln_ffn_g"][l], g["ln_ffn_b"][l] = _ln_bwd(dy, sv["z2"], small["ln_ffn_g"], l, "ln_bwd", dep=dep)
        act, dh, dff = _mlp_bwd_dh(sv["y1b"], dz2b, big["w1"], big["w2"])
        dw1 = _mm(sv["y1b"], dh, mode="tn", grid=(N_DEV,), a_spec=_full((t, D)),
                  b_spec=_col_blocks(t, D_FF, FF_BLK), out_shape=S((N_DEV, D, FF_BLK), BF16),
                  out_spec=pl.BlockSpec((None, D, FF_BLK), lambda i: (i, 0, 0)), name="mlp_dw1")
        dw2 = _mm(act, dz2b, mode="tn", grid=(N_DEV,), a_spec=_col_blocks(t, D_FF, FF_BLK),
                  b_spec=_full((t, D)), out_shape=S((N_DEV, FF_BLK, D), BF16),
                  out_spec=pl.BlockSpec((None, FF_BLK, D), lambda i: (i, 0, 0)), name="mlp_dw2")
        dep = grads_done(l, dict(w1=dw1, w2=dw2))
        dz1, dz1b, g["ln_mix_g"][l], g["ln_mix_b"][l] = _ln_bwd(dff, sv["z1"], small["ln_mix_g"], l, "ln_bwd_res",
                                                                 r=dz2, dep=dep)
        if l % 2 == 0:
            wout = big["wout2d"]
            dycat = _mm(dz1b, wout, mode="nt", grid=(EVEN_MIX // 512,), a_spec=_full((t, D)),
                        b_spec=_row_blocks(D, 512), out_shape=S((t, EVEN_MIX), F32),
                        out_spec=_col_blocks(t, EVEN_MIX, 512), name="even_dycat")
            dwout = _mm(sv["ycat"], dz1b, mode="tn", grid=(EVEN_MIX // 512,), a_spec=_col_blocks(t, EVEN_MIX, 512),
                        b_spec=_full((t, D)), out_shape=S((EVEN_MIX, D), BF16), out_spec=_row_blocks(D, 512),
                        name="even_dwout")
            du_pool, g["pool_w"][j], g["pool_scale"][j] = _pool_bwd(sv["proj"], dycat, small["pool_w"],
                                                                   small["pool_scale"], j)
            (du_lru, du_gate, g["conv_w"][j], g["conv_b"][j], g["w_a"][j], g["b_a"][j], g["w_x"][j], g["b_x"][j],
             g["lam"][j]) = _lru_bwd(sv["proj"], dycat, lru_p, j)
            dproj = jnp.concatenate([du_pool, du_lru, du_gate], axis=1)
            dwin = _mm(sv["xb"], dproj, mode="tn", grid=(EVEN_IN // 512,), a_spec=_full((t, D)),
                       b_spec=_col_blocks(t, EVEN_IN, 512), out_shape=S((D, EVEN_IN), BF16),
                       out_spec=_col_blocks(D, EVEN_IN, 512), name="even_dwin")
            dep = grads_done(l, dict(win=dwin.reshape(D, N_DEV, EVEN_IN // N_DEV).transpose(1, 0, 2),
                                     wout=dwout.reshape(N_DEV, EVEN_MIX // N_DEV, D)))
            dy = _mm(dproj, big["win2d"], mode="nt", grid=(t // bm,), a_spec=_row_blocks(EVEN_IN, bm),
                     b_spec=_full((D, EVEN_IN)), out_shape=S((t, D), F32), out_spec=_row_blocks(D, bm),
                     add=dz1, add_spec=_row_blocks(D, bm), add_scale=ALPHA, name="even_dx")
        else:
            dwo, dwqb, dwkvb, dcq, dckv, dkpe = _attn_bwd(
                sv["cq"], sv["ckv"], sv["kpe"], cos, sin, big["wqb"], big["wkvb"], big["wo"], sv["o"], dz1b)
            ddown, g["gq"][j], g["gkv"][j] = _rms_bwd(sv["down"], dcq, dckv, dkpe, cos, sin, small["gq"],
                                                     small["gkv"], j)
            dwdown = _mm(sv["xb"], ddown, mode="tn", grid=(N_DEV,), a_spec=_col_blocks(t, D, D // N_DEV),
                         b_spec=_full((t, ODD_IN)), out_shape=S((N_DEV, D // N_DEV, ODD_IN), BF16),
                         out_spec=pl.BlockSpec((None, D // N_DEV, ODD_IN), lambda i: (i, 0, 0)),
                         name="odd_dwdown")
            dep = grads_done(l, dict(wdown=dwdown, wqb=dwqb, wkvb=dwkvb, wo=dwo))
            dy = _mm(ddown, big["wdown2d"], mode="nt", grid=(t // bm,), a_spec=_row_blocks(ODD_IN, bm),
                     b_spec=_full((D, ODD_IN)), out_shape=S((t, D), F32), out_spec=_row_blocks(D, bm),
                     add=dz1, add_spec=_row_blocks(D, bm), add_scale=ALPHA, name="odd_dx")
    return loss_tile[0, 0], dy, g


def _mesh_place():
    x, y, c = lax.axis_index("x"), lax.axis_index("y"), lax.axis_index("c")
    return x, y, c


def _peer(place, k):
    x, y, c = place
    return (1 - x if k & 4 else x, 1 - y if k & 2 else y, 1 - c if k & 1 else c)


def _index(place):
    x, y, c = place
    return 4 * x + 2 * y + c


ANY = pl.BlockSpec(memory_space=pl.ANY)


def _all_gather_big(zones):
    n = len(zones)

    def body(*refs):
        outs = refs[n:2 * n]
        send, recv = refs[2 * n:]
        x, y, c = _mesh_place()
        me, sibling = (x, y, c), (x, y, 1 - c)
        chips = [(1 - x, y), (x, 1 - y), (1 - x, 1 - y)]

        def copy(w, k, block, to):
            blk = outs[w].at[_index(block)]
            return pltpu.make_async_remote_copy(src_ref=blk, dst_ref=blk, send_sem=send.at[w, k], recv_sem=recv.at[w, k],
                                                device_id=to, device_id_type=MESH)

        first = []
        for w in range(n):
            first.append(copy(w, 0, me, sibling))
            first += [copy(w, 1 + j, me, (*chip, c)) for j, chip in enumerate(chips)]
        for cp in first:
            cp.start()
        passed = []
        for w in range(n):
            for j, chip in enumerate(chips):
                copy(w, 1 + j, (*chip, c), me).wait_recv()
                cp = copy(w, 4 + j, (*chip, c), sibling)
                cp.start()
                passed.append(cp)
        for w in range(n):
            copy(w, 0, sibling, me).wait_recv()
            for j, chip in enumerate(chips):
                copy(w, 4 + j, (*chip, 1 - c), me).wait_recv()
        for cp in first + passed:
            cp.wait_send()

    return pl.pallas_call(
        body, in_specs=[ANY] * n, out_specs=[ANY] * n, out_shape=[S(z.shape, z.dtype) for z in zones],
        input_output_aliases={i: i for i in range(n)},
        scratch_shapes=[pltpu.SemaphoreType.DMA((n, N_DEV - 1)), pltpu.SemaphoreType.DMA((n, N_DEV - 1))],
        compiler_params=pltpu.CompilerParams(has_side_effects=True), name="all_gather_big")(*zones)


def _shard_rows_tile(a):
    return max(d for d in range(16, 257, 16) if a % d == 0)


HBM = pl.BlockSpec(memory_space=pltpu.HBM)
SEM = pl.BlockSpec(memory_space=pltpu.SEMAPHORE)
DATAFLOW = pltpu.SideEffectType.DATAFLOW_SIDE_EFFECTING


def _in_hbm(a):
    return pltpu.with_memory_space_constraint(a, pltpu.HBM)


def _gather_ici_copies(place, src, land, w):
    me = _index(place)
    return [(_peer(place, k), land.at[me], land.at[me]) for k in (1, 2, 4, 6)]


def _gather_d2d_copies(place, src, land, w):
    blocks = [_index(_peer(place, k)) for k in (2, 4, 6)]
    return [(_peer(place, 1), land.at[b], land.at[b]) for b in blocks]


GATHER_ICI = (4, _gather_ici_copies)
GATHER_D2D = (3, _gather_d2d_copies)


def _scatter_plan(layers):
    def copies(place, src, land, w):
        me = _index(place)
        mine = land.at[me] if layers[w] is None else land.at[me, layers[w]]
        return [(_peer(place, k), src.at[_index(_peer(place, k))], mine) for k in range(1, N_DEV)]
    return (N_DEV - 1, copies)


def _gather_all_copies(place, src, land, w):
    me = _index(place)
    return [(_peer(place, k), land.at[me], land.at[me]) for k in range(1, N_DEV)]


GATHER_ALL = (N_DEV - 1, _gather_all_copies)


def _sum_blocks(zone, part, me):
    r = part.shape[1]

    def body(me_ref, z_ref, p_ref, o_ref):
        acc = None
        for s in range(N_DEV):
            term = jnp.where(me_ref[0] == s, p_ref[...], z_ref[s])
            acc = term if acc is None else acc + term
        o_ref[...] = acc

    grid_spec = pltpu.PrefetchScalarGridSpec(
        num_scalar_prefetch=1, grid=(1,),
        in_specs=[pl.BlockSpec((N_DEV, r, 128), lambda i, me_ref: (0, 0, 0)),
                  pl.BlockSpec((None, r, 128), lambda i, me_ref: (me_ref[0], 0, 0))],
        out_specs=pl.BlockSpec((r, 128), lambda i, me_ref: (0, 0)))
    return pl.pallas_call(body, grid_spec=grid_spec, out_shape=S((r, 128), F32),
                          compiler_params=_cp("arbitrary"), name="sum_small")(me, zone, part)


def _exchange_start(srcs, lands, plan, name, after=()):
    ns, n = len(srcs), len(lands)
    n_in = ns + n + len(after)
    per, copies = plan

    def body(*refs):
        ins, land = refs[:ns], refs[ns:ns + n]
        send, recv = refs[n_in], refs[n_in + 1]
        token = refs[-1]
        place = _mesh_place()
        for i in range(per):
            for w in range(n):
                target, src, dst = copies(place, ins[w] if ns else None, land[w], w)[i]
                pltpu.make_async_remote_copy(src_ref=src, dst_ref=dst, send_sem=send.at[w * per + i],
                                             recv_sem=recv.at[w * per + i], device_id=target, device_id_type=MESH).start()
        token[...] = jnp.zeros_like(token)

    sems = pltpu.SemaphoreType.DMA((n * per,))
    thru = [pltpu.HBM(a.shape, a.dtype) for a in list(srcs) + list(lands)]
    out = pl.pallas_call(
        body, name=name, in_specs=[HBM] * (ns + n) + [ANY] * len(after),
        out_shape=(sems, sems, *thru, S((8, 128), F32)),
        out_specs=(SEM, SEM, *([HBM] * (ns + n)), pl.BlockSpec(memory_space=pltpu.VMEM)),
        input_output_aliases={i: 2 + i for i in range(ns + n)},
        compiler_params=pltpu.CompilerParams(has_side_effects=DATAFLOW),
    )(*[_in_hbm(a) for a in list(srcs) + list(lands)], *after)
    return out[0], out[1], list(out[2:2 + ns]), list(out[2 + ns:2 + ns + n]), out[-1]


def _exchange_wait(send, recv, srcs, lands, plan, after, name):
    ns, n = len(srcs), len(lands)
    per, copies = plan
    afters = tuple(after) if isinstance(after, (tuple, list)) else (after,)

    def body(*refs):
        ins, land = refs[:ns], refs[ns:ns + n]
        send_ref, recv_ref = refs[ns + n], refs[ns + n + 1]
        place = _mesh_place()
        for i in range(per):
            for w in range(n):
                target, src, dst = copies(place, ins[w] if ns else None, land[w], w)[i]
                cp = pltpu.make_async_remote_copy(src_ref=src, dst_ref=dst, send_sem=send_ref.at[w * per + i],
                                                  recv_sem=recv_ref.at[w * per + i], device_id=target,
                                                  device_id_type=MESH)
                cp.wait_send()
                cp.wait_recv()

    thru = [pltpu.HBM(a.shape, a.dtype) for a in list(srcs) + list(lands)]
    out = pl.pallas_call(
        body, name=name, in_specs=[HBM] * (ns + n) + [SEM, SEM] + [ANY] * len(afters),
        out_shape=tuple(thru), out_specs=tuple([HBM] * (ns + n)),
        input_output_aliases={i: i for i in range(ns + n)},
        compiler_params=pltpu.CompilerParams(has_side_effects=DATAFLOW),
    )(*srcs, *lands, send, recv, *afters)
    return list(out[:ns]), list(out[ns:])


def _all_reduce_small(part, name, deps=()):
    def body(*refs):
        p_ref = refs[0]
        o_ref, rbuf, send1, recv1, send2, recv2 = refs[-6:]
        place = _mesh_place()
        me = _index(place)
        rbuf[pl.ds(me, 1)] = p_ref[pl.ds(me, 1)]
        first = [pltpu.make_async_remote_copy(src_ref=p_ref.at[_index(_peer(place, k))], dst_ref=rbuf.at[me],
                                              send_sem=send1.at[k - 1], recv_sem=recv1.at[k - 1],
                                              device_id=_peer(place, k), device_id_type=MESH)
                 for k in range(1, N_DEV)]
        for cp in first:
            cp.start()
        for cp in first:
            cp.wait()
        acc = rbuf[0]
        for d in range(1, N_DEV):
            acc = acc + rbuf[d]
        o_ref[pl.ds(me, 1)] = acc[None]
        second = [pltpu.make_async_remote_copy(src_ref=o_ref.at[me], dst_ref=o_ref.at[me], send_sem=send2.at[k - 1],
                                               recv_sem=recv2.at[k - 1], device_id=_peer(place, k),
                                               device_id_type=MESH)
                  for k in range(1, N_DEV)]
        for cp in second:
            cp.start()
        for cp in second:
            cp.wait()

    vm = pl.BlockSpec(memory_space=pltpu.VMEM)
    ops = [part, *deps]
    return pl.pallas_call(
        body, in_specs=[vm] + [ANY] * len(deps), out_specs=vm, out_shape=S(part.shape, F32),
        scratch_shapes=[pltpu.VMEM(part.shape, F32)] + [pltpu.SemaphoreType.DMA((N_DEV - 1,))] * 4,
        compiler_params=pltpu.CompilerParams(has_side_effects=True, vmem_limit_bytes=VMEM_LIMIT), name=name)(*ops)


def _adamw(w, g, m, v):
    m = ADAM_B1 * m + (1.0 - ADAM_B1) * g
    v = ADAM_B2 * v + (1.0 - ADAM_B2) * (g * g)
    m_hat = m / (1.0 - ADAM_B1 ** ADAM_STEP)
    v_hat = v / (1.0 - ADAM_B2 ** ADAM_STEP)
    return -ADAM_LR * (m_hat / (jnp.sqrt(v_hat) + ADAM_EPS) + ADAM_WD * w), m, v


def _adam_big(parts, own, me, w, m, v, name):
    nl, a, b = w.shape
    ta = _shard_rows_tile(a)

    def body(me_ref, p_ref, *refs):
        own_refs, (w_ref, m_ref, v_ref, g_ref, d_ref, mo_ref, vo_ref) = refs[:nl], refs[nl:]
        layer = pl.program_id(0)
        mine = own_refs[0][...]
        for k in range(1, nl):
            mine = jnp.where(layer == k, own_refs[k][...], mine)
        g = None
        for s in range(N_DEV):
            term = jnp.where(me_ref[0] == s, mine, p_ref[s]).astype(F32)
            g = term if g is None else g + term
        g_ref[...] = g
        d_ref[...], mo_ref[...], vo_ref[...] = _adamw(w_ref[...], g, m_ref[...], v_ref[...])

    blk = pl.BlockSpec((None, ta, b), lambda l, i, me_ref: (l, i, 0))

    def own_spec(k):
        return pl.BlockSpec((None, ta, b), lambda l, i, me_ref: (me_ref[0], jnp.where(l == k, i, 0), 0))

    grid_spec = pltpu.PrefetchScalarGridSpec(
        num_scalar_prefetch=1, grid=(nl, a // ta),
        in_specs=[pl.BlockSpec((N_DEV, None, ta, b), lambda l, i, me_ref: (0, l, i, 0))]
        + [own_spec(k) for k in range(nl)] + [blk, blk, blk],
        out_specs=[blk] * 4)
    return pl.pallas_call(body, grid_spec=grid_spec, out_shape=[S(w.shape, F32)] * 4,
                          compiler_params=_cp("arbitrary", "arbitrary"), name=name)(me, parts, *own, w, m, v)


def _adam_small(g, w, m, v, name):
    def body(g_ref, w_ref, m_ref, v_ref, d_ref, mo_ref, vo_ref):
        d_ref[...], mo_ref[...], vo_ref[...] = _adamw(w_ref[...], g_ref[...], m_ref[...], v_ref[...])

    return pl.pallas_call(body, out_shape=[S(g.shape, F32)] * 3, compiler_params=_cp(), name=name)(g, w, m, v)


BIG = ("even_w_in", "even_w_out", "mla_w_down", "mla_w_qb", "mla_w_kvb", "mla_w_o", "mlp_w1", "mlp_w2")
BIG_KEY = dict(even_w_in="win", even_w_out="wout", mla_w_down="wdown", mla_w_qb="wqb", mla_w_kvb="wkvb",
               mla_w_o="wo", mlp_w1="w1", mlp_w2="w2")
SMALL = (("ln_mix_g", "ln_mix_g", None), ("ln_mix_b", "ln_mix_b", None), ("ln_ffn_g", "ln_ffn_g", None),
         ("ln_ffn_b", "ln_ffn_b", None), ("pool_w", "pool_w", None), ("pool_scale", "pool_scale", None),
         ("lru_conv_w", "conv_w", 2), ("lru_conv_b", "conv_b", None), ("lru_w_a", "w_a", None),
         ("lru_b_a", "b_a", None), ("lru_w_x", "w_x", None), ("lru_b_x", "b_x", None), ("lru_lambda", "lam", None),
         ("mla_q_norm_g", "gq", 1), ("mla_kv_norm_g", "gkv", 1))
WEIGHTS = ("ln_mix_g", "ln_mix_b", "ln_ffn_g", "ln_ffn_b", "even_w_in", "pool_w", "pool_scale", "lru_conv_w",
           "lru_conv_b", "lru_w_a", "lru_b_a", "lru_w_x", "lru_b_x", "lru_lambda", "even_w_out", "mla_w_down",
           "mla_q_norm_g", "mla_kv_norm_g", "mla_w_qb", "mla_w_kvb", "mla_w_o", "mlp_w1", "mlp_w2")
ALL_AXES = ("x", "y", "c")


def _layer_weights(l):
    j = l // 2
    if l % 2 == 0:
        mixer = [("win", "even_w_in", j), ("wout", "even_w_out", j)]
    else:
        mixer = [("wdown", "mla_w_down", j), ("wqb", "mla_w_qb", j), ("wkvb", "mla_w_kvb", j), ("wo", "mla_w_o", j)]
    return mixer + [("w1", "mlp_w1", l), ("w2", "mlp_w2", l)]


def _pack(arrays, multiple):
    flat = jnp.concatenate([a.reshape(-1) for a in arrays])
    pad = (-flat.shape[0]) % multiple
    return jnp.pad(flat, (0, pad))


def _unpack(flat, shapes):
    out, at = [], 0
    for shp in shapes:
        n = 1
        for s in shp:
            n *= s
        out.append(flat[at:at + n].reshape(shp))
        at += n
    return out


def _global_shape(local_shape, axis):
    if axis is None:
        return tuple(local_shape)
    return tuple(s * N_DEV if i == axis else s for i, s in enumerate(local_shape))


def _step(x, positions, tgt, w, m, v):
    t = x.shape[1]
    me = _index(_mesh_place())

    sharded = [(name, axis) for name, _, axis in SMALL if axis is not None]
    zeros_with_mine = [lax.dynamic_update_slice_in_dim(jnp.zeros(_global_shape(w[name].shape, axis), F32), w[name],
                                                       me * w[name].shape[axis], axis) for name, axis in sharded]
    chunk = N_DEV * 8 * 128
    gathered = _all_reduce_small(_pack(zeros_with_mine, chunk).reshape(N_DEV, -1, 128), "gather_small")
    full = dict(zip([name for name, _ in sharded],
                    _unpack(gathered.reshape(-1), [_global_shape(w[name].shape, axis) for name, axis in sharded])))

    def zone_of(shard):
        return lax.dynamic_update_slice_in_dim(lax.empty((N_DEV,) + shard.shape, BF16), shard.astype(BF16)[None], me, 0)

    def keys_of(l, part):
        keys = [key for key, _, _ in _layer_weights(l)]
        if l == 0:
            return keys[:1] if part == 0 else keys[1:]
        return keys if part == 0 else []

    shard_of = {(l, key): w[name][i] for l in range(DEPTH) for key, name, i in _layer_weights(l)}
    first = _all_gather_big([zone_of(shard_of[0, key]) for key in keys_of(0, 0)])
    flights, after = {}, (first[0], gathered)
    for l in range(DEPTH):
        for part in (0, 1):
            if (l, part) != (0, 0) and keys_of(l, part):
                zones = [zone_of(shard_of[l, key]) for key in keys_of(l, part)]
                send, recv, _, lands, token = _exchange_start([], zones, GATHER_ICI, "gather_start_%d_%d" % (l, part),
                                                              after=after)
                flights[l, part] = (send, recv, [], lands)
                after = (token,)

    passing = {}

    def pass_on(l, part, after):
        tag = "%d_%d" % (l, part)
        _, lands = _exchange_wait(*flights[l, part], GATHER_ICI, after, "gather_wait_" + tag)
        send, recv, _, lands, token = _exchange_start([], lands, GATHER_D2D, "gather_pass_" + tag)
        passing[l, part] = (send, recv, [], lands)
        return token

    def early_pass(l, after):
        return pass_on(l, 0, after) if l >= 2 else None

    def weights_of(l, part, after):
        keys = keys_of(l, part)
        if (l, part) == (0, 0):
            arrays = first
        elif keys:
            if (l, part) not in passing:
                pass_on(l, part, after)
            _, arrays = _exchange_wait(*passing[l, part], GATHER_D2D, after, "gather_pass_wait_%d_%d" % (l, part))
        big = dict(zip(keys, arrays)) if keys else {}
        if "win" in big:
            big["win2d"] = big["win"].transpose(1, 0, 2).reshape(D, EVEN_IN)
        if "wout" in big:
            big["wout2d"] = big["wout"].reshape(EVEN_MIX, D)
        if "wdown" in big:
            big["wdown2d"] = big["wdown"].reshape(D, ODD_IN)
        return big

    zone = {name: lax.empty((N_DEV,) + w[name].shape, BF16) for name in BIG}
    name_of = {key: name for name, key in BIG_KEY.items()}
    sent, last_token = [], [None]

    def grads_done(l, grads):
        keys = list(grads)
        index = {key: i for key, _, i in _layer_weights(l)}
        layers = [index[key] for key in keys]
        send, recv, srcs, lands, tok = _exchange_start([grads[k] for k in keys], [zone[name_of[k]] for k in keys],
                                                       _scatter_plan(layers), "scatter_start_%d_%s" % (l, keys[0]))
        for k, land in zip(keys, lands):
            zone[name_of[k]] = land
        sent.append((send, recv, srcs, keys, layers))
        last_token[0] = tok
        return tok

    row3 = lambda a: a.reshape(a.shape[0], 1, a.shape[1])
    small = dict(ln_mix_g=row3(w["ln_mix_g"]), ln_mix_b=row3(w["ln_mix_b"]), ln_ffn_g=row3(w["ln_ffn_g"]),
                 ln_ffn_b=row3(w["ln_ffn_b"]), pool_w=w["pool_w"], pool_scale=row3(w["pool_scale"]),
                 conv_w=full["lru_conv_w"], conv_b=row3(w["lru_conv_b"]), w_a=w["lru_w_a"], b_a=row3(w["lru_b_a"]),
                 w_x=w["lru_w_x"], b_x=row3(w["lru_b_x"]), lam=row3(w["lru_lambda"]),
                 gq=row3(full["mla_q_norm_g"]), gkv=row3(full["mla_kv_norm_g"]))

    loss_part, grad_x, g = _local_step(x[0], positions.reshape(t, 1), tgt[0], small, weights_of, grads_done,
                                       start_dep=token, prefetch=early_pass)

    own = {name: [None] * w[name].shape[0] for name in BIG}
    me_arr = me.astype(jnp.int32).reshape(1)
    out = {}
    local_g = [jnp.stack(g[key]).reshape(_global_shape(w[name].shape, axis)) for name, key, axis in SMALL]
    local_g.append(loss_part.reshape(1))
    part = _pack(local_g, chunk).reshape(N_DEV, -1, 128)
    small_plan = _scatter_plan([None])
    s_send, s_recv, s_src, s_land, after = _exchange_start([part], [lax.empty(part.shape, F32)], small_plan,
                                                           "small_scatter_start", after=(last_token[0],))
    for n_flight, (send, recv, srcs, keys, layers) in enumerate(sent):
        if n_flight == len(sent) - 1:
            for name in BIG:
                if BIG_KEY[name] not in keys:
                    out[name] = _adam_big(zone[name], own[name], me_arr, w[name], m[name], v[name], "adam_" + name)
            s_src, s_land = _exchange_wait(s_send, s_recv, s_src, s_land, small_plan, [o[0] for o in out.values()],
                                           "small_scatter_wait")
            chunk_sum = _sum_blocks(s_land[0], s_src[0], me_arr)
            r_zone = lax.dynamic_update_slice_in_dim(lax.empty(part.shape, F32), chunk_sum[None], me, 0)
            r_send, r_recv, _, r_land, after = _exchange_start([], [r_zone], GATHER_ALL, "small_gather_start")
        srcs, lands = _exchange_wait(send, recv, srcs, [zone[name_of[k]] for k in keys], _scatter_plan(layers),
                                     after, "scatter_wait_%d" % n_flight)
        for k, land, src, layer in zip(keys, lands, srcs, layers):
            zone[name_of[k]] = land
            own[name_of[k]][layer] = src
        after = lands[0]
    for name in BIG:
        if name not in out:
            out[name] = _adam_big(zone[name], own[name], me_arr, w[name], m[name], v[name], "adam_" + name)

    _, reduced = _exchange_wait(r_send, r_recv, [], r_land, GATHER_ALL, [out[name][0] for name in BIG],
                                "small_gather_wait")
    reduced = _unpack(reduced[0].reshape(-1), [a.shape for a in local_g])
    loss = reduced[-1][0]
    mine = [a if axis is None else lax.dynamic_slice_in_dim(a, me * w[name].shape[axis], w[name].shape[axis], axis)
            for a, (name, _, axis) in zip(reduced, SMALL)]
    for grad, (name, _, _) in zip(mine, SMALL):
        shape = w[name].shape
        as_2d = lambda a: a.reshape(-1, shape[-1])
        new = _adam_small(as_2d(grad), as_2d(w[name]), as_2d(m[name]), as_2d(v[name]), "adam_" + name)
        out[name] = (grad,) + tuple(a.reshape(shape) for a in new)

    return (loss, grad_x[None]) + tuple(out[name][i] for i in range(4) for name in WEIGHTS)


def kernel(x, positions, ln_mix_g, ln_mix_b, ln_ffn_g, ln_ffn_b, even_w_in, pool_w, pool_scale, lru_conv_w, lru_conv_b, lru_w_a, lru_b_a, lru_w_x, lru_b_x, lru_lambda, even_w_out, mla_w_down, mla_q_norm_g, mla_kv_norm_g, mla_w_qb, mla_w_kvb, mla_w_o, mlp_w1, mlp_w2, loss_target, m_ln_mix_g, m_ln_mix_b, m_ln_ffn_g, m_ln_ffn_b, m_even_w_in, m_pool_w, m_pool_scale, m_lru_conv_w, m_lru_conv_b, m_lru_w_a, m_lru_b_a, m_lru_w_x, m_lru_b_x, m_lru_lambda, m_even_w_out, m_mla_w_down, m_mla_q_norm_g, m_mla_kv_norm_g, m_mla_w_qb, m_mla_w_kvb, m_mla_w_o, m_mlp_w1, m_mlp_w2, v_ln_mix_g, v_ln_mix_b, v_ln_ffn_g, v_ln_ffn_b, v_even_w_in, v_pool_w, v_pool_scale, v_lru_conv_w, v_lru_conv_b, v_lru_w_a, v_lru_b_a, v_lru_w_x, v_lru_b_x, v_lru_lambda, v_even_w_out, v_mla_w_down, v_mla_q_norm_g, v_mla_kv_norm_g, v_mla_w_qb, v_mla_w_kvb, v_mla_w_o, v_mlp_w1, v_mlp_w2):
    w = dict(zip(WEIGHTS, (ln_mix_g, ln_mix_b, ln_ffn_g, ln_ffn_b, even_w_in, pool_w, pool_scale, lru_conv_w,
                           lru_conv_b, lru_w_a, lru_b_a, lru_w_x, lru_b_x, lru_lambda, even_w_out, mla_w_down,
                           mla_q_norm_g, mla_kv_norm_g, mla_w_qb, mla_w_kvb, mla_w_o, mlp_w1, mlp_w2)))
    m = dict(zip(WEIGHTS, (m_ln_mix_g, m_ln_mix_b, m_ln_ffn_g, m_ln_ffn_b, m_even_w_in, m_pool_w, m_pool_scale,
                           m_lru_conv_w, m_lru_conv_b, m_lru_w_a, m_lru_b_a, m_lru_w_x, m_lru_b_x, m_lru_lambda,
                           m_even_w_out, m_mla_w_down, m_mla_q_norm_g, m_mla_kv_norm_g, m_mla_w_qb, m_mla_w_kvb,
                           m_mla_w_o, m_mlp_w1, m_mlp_w2)))
    v = dict(zip(WEIGHTS, (v_ln_mix_g, v_ln_mix_b, v_ln_ffn_g, v_ln_ffn_b, v_even_w_in, v_pool_w, v_pool_scale,
                           v_lru_conv_w, v_lru_conv_b, v_lru_w_a, v_lru_b_a, v_lru_w_x, v_lru_b_x, v_lru_lambda,
                           v_even_w_out, v_mla_w_down, v_mla_q_norm_g, v_mla_kv_norm_g, v_mla_w_qb, v_mla_w_kvb,
                           v_mla_w_o, v_mlp_w1, v_mlp_w2)))
    return _step(x, positions, loss_target, w, m, v)
```

```python
import functools

import jax
import jax.numpy as jnp
from jax import lax
from jax.experimental import pallas as pl
from jax.experimental.pallas import tpu as pltpu

F32 = jnp.float32
BF16 = jnp.bfloat16
S = jax.ShapeDtypeStruct

D = 1024
DEPTH = 4
N_DEV = 8
CHUNK_SHIFT = 6
POOL_WINDOWS = (2, 4, 8, 16)
POOL_W = 512
LRU_W = 1024
LRU_HEADS = 8
HEAD = 128
LRU_C = 8.0
EVEN_IN = 2560
EVEN_MIX = 1536
MLA_HEADS = 8
NOPE = 128
ROPE = 64
VDIM = 128
Q_RANK = 384
KV_RANK = 256
ODD_IN = 704
D_FF = 4096
FF_BLK = D_FF // N_DEV
ROPE_THETA = 10000.0
ALPHA = (2 * DEPTH) ** 0.25
LN_EPS = 1e-5
RMS_EPS = 1e-6
ATT_SCALE = (NOPE + ROPE) ** -0.5
NEG = float(jnp.finfo(jnp.float32).min)
ADAM_LR = 0.001
ADAM_B1 = 0.9
ADAM_B2 = 0.999
ADAM_EPS = 1e-08
ADAM_WD = 0.01
ADAM_STEP = 10
V7X_VMEM_BYTES = 64 * 1024 * 1024
VMEM_LIMIT = V7X_VMEM_BYTES - 8 * 1024 * 1024
MESH = pl.DeviceIdType.MESH


def _cp(*sem):
    return pltpu.CompilerParams(dimension_semantics=sem if sem else None, vmem_limit_bytes=VMEM_LIMIT)


def _dot(a, b):
    return jnp.dot(a, b, preferred_element_type=F32)


def _dot_nt(a, b):
    return lax.dot_general(a, b, (((1,), (1,)), ((), ())), preferred_element_type=F32)


def _dot_tn(a, b):
    return lax.dot_general(a, b, (((0,), (0,)), ((), ())), preferred_element_type=F32)


def _full(shape):
    return pl.BlockSpec(shape, lambda *_: (0,) * len(shape))


def _mm(a, b, *, mode, grid, a_spec, b_spec, out_shape, out_spec, name, add=None, add_spec=None, add_scale=1.0,
        dep=None):
    dot = {"nn": _dot, "nt": _dot_nt, "tn": _dot_tn}[mode]

    def body(*refs):
        a_ref, b_ref, o_ref = refs[0], refs[1], refs[-1]
        acc = dot(a_ref[...].astype(BF16), b_ref[...].astype(BF16))
        if add is not None:
            acc = acc + add_scale * refs[2][...]
        o_ref[...] = acc.astype(o_ref.dtype)

    ops = [a, b] if add is None else [a, b, add]
    specs = [a_spec, b_spec] if add is None else [a_spec, b_spec, add_spec]
    if dep is not None:
        ops.append(dep)
        specs.append(pl.BlockSpec(memory_space=pl.ANY))
    return pl.pallas_call(body, grid=grid, in_specs=specs, out_specs=out_spec, out_shape=out_shape,
                          compiler_params=_cp(*(("parallel",) * len(grid))), name=name)(*ops)


def _ln_stats(z):
    mu = jnp.mean(z, axis=-1, keepdims=True)
    zc = z - mu
    var = jnp.mean(zc * zc, axis=-1, keepdims=True)
    rstd = lax.rsqrt(var + LN_EPS)
    return zc * rstd, rstd


def _row_tile(t):
    return min(512, t)


def _resid_ln(x, mix, g3, b3, l, name):
    t = x.shape[0]
    bm = _row_tile(t)

    def body(x_ref, m_ref, g_ref, b_ref, z_ref, y_ref, yb_ref):
        z = ALPHA * x_ref[...] + m_ref[...]
        xh, _ = _ln_stats(z)
        y = xh * g_ref[...] + b_ref[...]
        z_ref[...] = z
        y_ref[...] = y
        yb_ref[...] = y.astype(BF16)

    row = pl.BlockSpec((bm, D), lambda i: (i, 0))
    vec = pl.BlockSpec((None, 1, D), lambda i: (l, 0, 0))
    return pl.pallas_call(body, grid=(t // bm,), in_specs=[row, row, vec, vec], out_specs=[row, row, row],
                          out_shape=[S((t, D), F32), S((t, D), F32), S((t, D), BF16)],
                          compiler_params=_cp("parallel"), name=name)(x, mix, g3, b3)


def _proj_resid_ln(x, a, wmat, g3, b3, l, name):
    t, k = a.shape
    bm = _row_tile(t)

    def body(x_ref, a_ref, w_ref, g_ref, b_ref, z_ref, y_ref, yb_ref):
        z = ALPHA * x_ref[...] + _dot(a_ref[...], w_ref[...])
        xh, _ = _ln_stats(z)
        y = xh * g_ref[...] + b_ref[...]
        z_ref[...] = z
        y_ref[...] = y
        yb_ref[...] = y.astype(BF16)

    row = pl.BlockSpec((bm, D), lambda i: (i, 0))
    vec = pl.BlockSpec((None, 1, D), lambda i: (l, 0, 0))
    return pl.pallas_call(body, grid=(t // bm,),
                          in_specs=[row, pl.BlockSpec((bm, k), lambda i: (i, 0)), _full((k, D)), vec, vec],
                          out_specs=[row, row, row], out_shape=[S((t, D), F32), S((t, D), F32), S((t, D), BF16)],
                          compiler_params=_cp("parallel"), name=name)(x, a, wmat, g3, b3)


def _ln_bwd(d, z, g3, l, name, r=None, dep=None):
    t = z.shape[0]
    bm = _row_tile(t)

    def body(*refs):
        refs = list(refs)
        d_ref = refs.pop(0)
        dy = d_ref[...]
        if r is not None:
            dy = dy + ALPHA * refs.pop(0)[...]
        z_ref, g_ref = refs.pop(0), refs.pop(0)
        if dep is not None:
            refs.pop(0)
        dz_ref, dzb_ref, dg_ref, db_ref = refs
        xh, rstd = _ln_stats(z_ref[...])
        dyg = dy * g_ref[...]
        m1 = jnp.mean(dyg, axis=-1, keepdims=True)
        m2 = jnp.mean(dyg * xh, axis=-1, keepdims=True)
        dz = rstd * (dyg - m1 - xh * m2)
        dz_ref[...] = dz
        dzb_ref[...] = dz.astype(BF16)

        @pl.when(pl.program_id(0) == 0)
        def _():
            dg_ref[...] = jnp.zeros_like(dg_ref)
            db_ref[...] = jnp.zeros_like(db_ref)

        dg_ref[...] += jnp.sum(dy * xh, axis=0, keepdims=True)
        db_ref[...] += jnp.sum(dy, axis=0, keepdims=True)

    row = pl.BlockSpec((bm, D), lambda i: (i, 0))
    vec = pl.BlockSpec((None, 1, D), lambda i: (l, 0, 0))
    acc = pl.BlockSpec((1, D), lambda i: (0, 0))
    ops = [d, z, g3] if r is None else [d, r, z, g3]
    specs = [row, row, vec] if r is None else [row, row, row, vec]
    if dep is not None:
        ops.append(dep)
        specs.append(_full(dep.shape))
    return pl.pallas_call(body, grid=(t // bm,), in_specs=specs, out_specs=[row, row, acc, acc],
                          out_shape=[S((t, D), F32), S((t, D), BF16), S((1, D), F32), S((1, D), F32)],
                          compiler_params=_cp("arbitrary"), name=name)(*ops)


def _loss_grad(y, tgt):
    t = y.shape[0]
    bm = _row_tile(t)

    def body(y_ref, t_ref, dy_ref, loss_ref, acc_ref):
        i = pl.program_id(0)
        e = y_ref[...] - t_ref[...]
        dy_ref[...] = e * (1.0 / D)

        @pl.when(i == 0)
        def _():
            acc_ref[...] = jnp.zeros_like(acc_ref)

        acc_ref[...] += jnp.sum(e * e, axis=0, keepdims=True)

        @pl.when(i == pl.num_programs(0) - 1)
        def _():
            loss_ref[...] = jnp.full(loss_ref.shape, (0.5 / D) * jnp.sum(acc_ref[...]), F32)

    row = pl.BlockSpec((bm, D), lambda i: (i, 0))
    return pl.pallas_call(body, grid=(t // bm,), in_specs=[row, row],
                          out_specs=[row, pl.BlockSpec((1, 128), lambda i: (0, 0))],
                          out_shape=[S((t, D), F32), S((1, 128), F32)],
                          scratch_shapes=[pltpu.VMEM((1, D), F32)],
                          compiler_params=_cp("arbitrary"), name="loss_grad")(y, tgt)


def _mlp_row_tile(t):
    return min(1024, t)


def _mlp_fwd(y, yb, w1g, w2g, g3, b3, l, dep=None):
    t = yb.shape[0]
    bm = _mlp_row_tile(t)

    def body(*refs):
        y_ref, yb_ref, w1_ref, w2_ref, g_ref, b_ref = refs[:6]
        z_ref, o_ref, ob_ref, acc_ref = refs[-4:]
        j = pl.program_id(1)
        h = jnp.maximum(_dot(yb_ref[...], w1_ref[...]), 0.0)
        c = _dot((h * h).astype(BF16), w2_ref[...])

        @pl.when(j == 0)
        def _():
            acc_ref[...] = c

        @pl.when(j > 0)
        def _():
            acc_ref[...] += c

        @pl.when(j == N_DEV - 1)
        def _():
            z = ALPHA * y_ref[...] + acc_ref[...]
            xh, _ = _ln_stats(z)
            out = xh * g_ref[...] + b_ref[...]
            z_ref[...] = z
            o_ref[...] = out
            ob_ref[...] = out.astype(BF16)

    row = pl.BlockSpec((bm, D), lambda i, j: (i, 0))
    vec = pl.BlockSpec((None, 1, D), lambda i, j: (l, 0, 0))
    deps = [] if dep is None else [dep]
    return pl.pallas_call(
        body, grid=(t // bm, N_DEV),
        in_specs=[row, row, pl.BlockSpec((None, D, FF_BLK), lambda i, j: (j, 0, 0)),
                  pl.BlockSpec((None, FF_BLK, D), lambda i, j: (j, 0, 0)), vec, vec] + [ANY] * len(deps),
        out_specs=[row, row, row], out_shape=[S((t, D), F32), S((t, D), F32), S((t, D), BF16)],
        scratch_shapes=[pltpu.VMEM((bm, D), F32)],
        compiler_params=_cp("parallel", "arbitrary"), name="mlp_fwd")(y, yb, w1g, w2g, g3, b3, *deps)


def _mlp_bwd_dh(yb, dzb, w1g, w2g):
    t = yb.shape[0]
    bm = _mlp_row_tile(t)

    def body(y_ref, dz_ref, w1_ref, w2_ref, a_ref, dh_ref, acc_ref):
        j = pl.program_id(1)
        r = jnp.maximum(_dot(y_ref[...], w1_ref[...]), 0.0)
        a_ref[...] = (r * r).astype(BF16)
        da = _dot_nt(dz_ref[...], w2_ref[...])
        dh = (da * (2.0 * r)).astype(BF16)
        dh_ref[...] = dh
        c = _dot_nt(dh, w1_ref[...])

        @pl.when(j == 0)
        def _():
            acc_ref[...] = c

        @pl.when(j > 0)
        def _():
            acc_ref[...] += c

    row = pl.BlockSpec((bm, D), lambda i, j: (i, 0))
    hid = pl.BlockSpec((bm, FF_BLK), lambda i, j: (i, j))
    return pl.pallas_call(
        body, grid=(t // bm, N_DEV),
        in_specs=[row, row,
                  pl.BlockSpec((None, D, FF_BLK), lambda i, j: (j, 0, 0)),
                  pl.BlockSpec((None, FF_BLK, D), lambda i, j: (j, 0, 0))],
        out_specs=[hid, hid, row],
        out_shape=[S((t, D_FF), BF16), S((t, D_FF), BF16), S((t, D), F32)],
        compiler_params=_cp("parallel", "arbitrary"), name="mlp_bwd_dh")(yb, dzb, w1g, w2g)


def _shift_dn(x, k, rows, fill=0.0):
    return jnp.where(rows >= k, pltpu.roll(x, k, 0), fill)


def _shift_up(x, k, rows, fill=0.0):
    t = x.shape[0]
    return jnp.where(rows < t - k, pltpu.roll(x, t - k, 0), fill)


def _scan_rows(a, b, shift):
    rows = lax.broadcasted_iota(jnp.int32, a.shape, 0)
    k = 1
    t = a.shape[0]
    while k < t:
        b = a * shift(b, k, rows) + b
        if 2 * k < t:
            a = a * shift(a, k, rows, 1.0)
        k *= 2
    return b


SUBLANES = 8


def _scan_two_level(a, b, down):
    t = a.shape[0]
    nb = t // SUBLANES
    a3 = a.reshape(nb, SUBLANES, HEAD)
    b3 = b.reshape(nb, SUBLANES, HEAD)
    sub = lax.broadcasted_iota(jnp.int32, a3.shape, 1)
    for k in (1, 2, 4):
        keep = sub >= k if down else sub < SUBLANES - k
        amount = k if down else SUBLANES - k
        b3 = a3 * jnp.where(keep, pltpu.roll(b3, amount, 1), 0.0) + b3
        a3 = a3 * jnp.where(keep, pltpu.roll(a3, amount, 1), 1.0)
    edge = SUBLANES - 1 if down else 0
    shift = _shift_dn if down else _shift_up
    totals = _scan_rows(a3[:, edge, :], b3[:, edge, :], shift)
    entering = shift(totals, 1, lax.broadcasted_iota(jnp.int32, totals.shape, 0))
    return (b3 + a3 * entering[:, None, :]).reshape(t, HEAD)


def _scan_dn(a, b):
    return _scan_two_level(a, b, True)


def _scan_up(a, b):
    return _scan_two_level(a, b, False)


def _window_sum_dn(x, w, rows):
    k = 1
    while k < w:
        x = x + _shift_dn(x, k, rows)
        k *= 2
    return x


def _window_sum_up(x, w, rows):
    k = 1
    while k < w:
        x = x + _shift_up(x, k, rows)
        k *= 2
    return x


def _pool_diff(u, w, rows):
    inv_count = 1.0 / jnp.minimum(rows + 1, w).astype(F32)
    return _window_sum_dn(u, w, rows) * inv_count - u, inv_count


def _pool_fwd(proj, pool_w, pool_scale3, j):
    t = proj.shape[0]

    def body(u_ref, w_ref, s_ref, y_ref):
        rows = lax.broadcasted_iota(jnp.int32, (t, HEAD), 0)
        for g, w in enumerate(POOL_WINDOWS):
            cols = slice(g * HEAD, (g + 1) * HEAD)
            d, _ = _pool_diff(u_ref[:, cols], w, rows)
            y = _dot(d.astype(BF16), w_ref[g].astype(BF16)) * s_ref[:, cols]
            y_ref[:, cols] = y.astype(BF16)

    return pl.pallas_call(
        body, grid=(1,),
        in_specs=[pl.BlockSpec((t, POOL_W), lambda i: (0, 0)),
                  pl.BlockSpec((None, 4, HEAD, HEAD), lambda i: (j, 0, 0, 0)),
                  pl.BlockSpec((None, 1, POOL_W), lambda i: (j, 0, 0))],
        out_specs=pl.BlockSpec((t, POOL_W), lambda i: (0, 0)),
        out_shape=S((t, POOL_W), BF16), compiler_params=_cp("arbitrary"), name="pool_fwd")(proj, pool_w, pool_scale3)


def _pool_bwd(proj, dycat, pool_w, pool_scale3, j):
    t = proj.shape[0]

    def body(u_ref, dy_ref, w_ref, s_ref, du_ref, dw_ref, ds_ref):
        rows = lax.broadcasted_iota(jnp.int32, (t, HEAD), 0)
        for g, w in enumerate(POOL_WINDOWS):
            cols = slice(g * HEAD, (g + 1) * HEAD)
            d, inv_count = _pool_diff(u_ref[:, cols], w, rows)
            db = d.astype(BF16)
            wg = w_ref[g].astype(BF16)
            dy = dy_ref[:, cols]
            ds_ref[:, cols] = jnp.sum(dy * _dot(db, wg), axis=0, keepdims=True)
            dzz = (dy * s_ref[:, cols]).astype(BF16)
            dw_ref[g] = _dot_tn(db, dzz)
            dd = _dot_nt(dzz, wg)
            du_ref[:, cols] = (_window_sum_up(dd * inv_count, w, rows) - dd).astype(BF16)

    return pl.pallas_call(
        body, grid=(1,),
        in_specs=[pl.BlockSpec((t, POOL_W), lambda i: (0, 0)),
                  pl.BlockSpec((t, POOL_W), lambda i: (0, 0)),
                  pl.BlockSpec((None, 4, HEAD, HEAD), lambda i: (j, 0, 0, 0)),
                  pl.BlockSpec((None, 1, POOL_W), lambda i: (j, 0, 0))],
        out_specs=[pl.BlockSpec((t, POOL_W), lambda i: (0, 0)), _full((4, HEAD, HEAD)), _full((1, POOL_W))],
        out_shape=[S((t, POOL_W), BF16), S((4, HEAD, HEAD), F32), S((1, POOL_W), F32)],
        compiler_params=_cp("arbitrary"), name="pool_bwd")(proj, dycat, pool_w, pool_scale3)


GELU_C = 0.7978845608028654
GELU_K = 0.044715


def _gelu(x):
    th = jnp.tanh(GELU_C * (x + GELU_K * x * x * x))
    return 0.5 * x * (1.0 + th), th


def _lru_forward(u, gate, cw, cb, wa, ba, wx, bx, lam, rows):
    v = cw[3:4] * u + cw[2:3] * _shift_dn(u, 1, rows) + cw[1:2] * _shift_dn(u, 2, rows) \
        + cw[0:1] * _shift_dn(u, 3, rows) + cb
    vb = v.astype(BF16)
    r = jax.nn.sigmoid(_dot(vb, wa) + ba)
    i = jax.nn.sigmoid(_dot(vb, wx) + bx)
    sp = jnp.maximum(-lam, 0.0) + jnp.log1p(jnp.exp(-jnp.abs(lam)))
    log_a = (-LRU_C) * r * sp
    a = jnp.exp(log_a)
    one_m_a2 = -jnp.tanh(log_a) * (a * a + 1.0)
    mult = jnp.sqrt(one_m_a2)
    h = _scan_dn(a, mult * (i * v))
    gl, th = _gelu(gate)
    return dict(v=v, vb=vb, r=r, i=i, sp=sp, a=a, mult=mult, h=h, gl=gl, th=th)


def _lru_specs(t, j, col0_u, col0_g):
    blk = lambda c0: pl.BlockSpec((t, HEAD), lambda h: (0, c0 + h))
    vec = pl.BlockSpec((None, 1, HEAD), lambda h: (j, 0, h))
    return [blk(col0_u), blk(col0_g),
            pl.BlockSpec((None, 4, HEAD), lambda h: (j, 0, h)), vec,
            pl.BlockSpec((None, None, HEAD, HEAD), lambda h: (j, h, 0, 0)), vec,
            pl.BlockSpec((None, None, HEAD, HEAD), lambda h: (j, h, 0, 0)), vec, vec]


def _lru_fwd(proj, p, j):
    t = proj.shape[0]

    def body(u_ref, g_ref, cw_ref, cb_ref, wa_ref, ba_ref, wx_ref, bx_ref, lam_ref, y_ref):
        rows = lax.broadcasted_iota(jnp.int32, (t, HEAD), 0)
        f = _lru_forward(u_ref[...], g_ref[...], cw_ref[...], cb_ref[...], wa_ref[...].astype(BF16), ba_ref[...],
                         wx_ref[...].astype(BF16), bx_ref[...], lam_ref[...], rows)
        y_ref[...] = (f["h"] * f["gl"]).astype(BF16)

    return pl.pallas_call(
        body, grid=(LRU_HEADS,), in_specs=_lru_specs(t, j, POOL_W // HEAD, (POOL_W + LRU_W) // HEAD),
        out_specs=pl.BlockSpec((t, HEAD), lambda h: (0, h)), out_shape=S((t, LRU_W), BF16),
        compiler_params=_cp("parallel"), name="lru_fwd")(
            proj, proj, p["conv_w"], p["conv_b"], p["w_a"], p["b_a"], p["w_x"], p["b_x"], p["lam"])


def _lru_bwd(proj, dycat, p, j):
    t = proj.shape[0]

    def body(u_ref, g_ref, cw_ref, cb_ref, wa_ref, ba_ref, wx_ref, bx_ref, lam_ref, dy_ref,
             du_ref, dgate_ref, dcw_ref, dcb_ref, dwa_ref, dba_ref, dwx_ref, dbx_ref, dlam_ref):
        rows = lax.broadcasted_iota(jnp.int32, (t, HEAD), 0)
        u = u_ref[...]
        gate = g_ref[...]
        cw = cw_ref[...]
        wa = wa_ref[...].astype(BF16)
        wx = wx_ref[...].astype(BF16)
        lam = lam_ref[...]
        f = _lru_forward(u, gate, cw, cb_ref[...], wa, ba_ref[...], wx, bx_ref[...], lam, rows)
        v, r, i, a, mult, h, th = f["v"], f["r"], f["i"], f["a"], f["mult"], f["h"], f["th"]
        dy = dy_ref[...]
        dgl = 0.5 * (1.0 + th) + 0.5 * gate * (1.0 - th * th) * GELU_C * (1.0 + 3.0 * GELU_K * gate * gate)
        dgate_ref[...] = (dy * h * dgl).astype(BF16)
        g = _scan_up(_shift_up(a, 1, rows), dy * f["gl"])
        da = g * _shift_dn(h, 1, rows)
        iv = i * v
        dmult = g * iv
        di = g * mult * v
        dv = g * mult * i
        dlog_a = da * a - dmult * (a * a) / mult
        dr = dlog_a * (-LRU_C) * f["sp"]
        dsp = jnp.sum(dlog_a * (-LRU_C) * r, axis=0, keepdims=True)
        dlam_ref[...] = -dsp * jax.nn.sigmoid(-lam)
        dpa = dr * r * (1.0 - r)
        dpx = di * i * (1.0 - i)
        dpab = dpa.astype(BF16)
        dpxb = dpx.astype(BF16)
        dwa_ref[...] = _dot_tn(f["vb"], dpab)
        dwx_ref[...] = _dot_tn(f["vb"], dpxb)
        dba_ref[...] = jnp.sum(dpa, axis=0, keepdims=True)
        dbx_ref[...] = jnp.sum(dpx, axis=0, keepdims=True)
        dv = dv + _dot_nt(dpab, wa) + _dot_nt(dpxb, wx)
        dcb_ref[...] = jnp.sum(dv, axis=0, keepdims=True)
        du = cw[3:4] * dv
        dcw_ref[3:4, :] = jnp.sum(dv * u, axis=0, keepdims=True)
        for k in (1, 2, 3):
            du = du + cw[3 - k:4 - k] * _shift_up(dv, k, rows)
            dcw_ref[3 - k:4 - k, :] = jnp.sum(dv * _shift_dn(u, k, rows), axis=0, keepdims=True)
        du_ref[...] = du.astype(BF16)

    blk = pl.BlockSpec((t, HEAD), lambda h: (0, h))
    vec = pl.BlockSpec((1, HEAD), lambda h: (0, h))
    mat = pl.BlockSpec((None, HEAD, HEAD), lambda h: (h, 0, 0))
    return pl.pallas_call(
        body, grid=(LRU_HEADS,),
        in_specs=_lru_specs(t, j, POOL_W // HEAD, (POOL_W + LRU_W) // HEAD)
        + [pl.BlockSpec((t, HEAD), lambda h: (0, POOL_W // HEAD + h))],
        out_specs=[blk, blk, pl.BlockSpec((4, HEAD), lambda h: (0, h)), vec, mat, vec, mat, vec, vec],
        out_shape=[S((t, LRU_W), BF16), S((t, LRU_W), BF16), S((4, LRU_W), F32), S((1, LRU_W), F32),
                   S((LRU_HEADS, HEAD, HEAD), F32), S((1, LRU_W), F32),
                   S((LRU_HEADS, HEAD, HEAD), F32), S((1, LRU_W), F32), S((1, LRU_W), F32)],
        compiler_params=_cp("parallel"), name="lru_bwd")(
            proj, proj, p["conv_w"], p["conv_b"], p["w_a"], p["b_a"], p["w_x"], p["b_x"], p["lam"], dycat)


def _rope(x, c, s):
    x1 = x[:, :ROPE // 2]
    x2 = x[:, ROPE // 2:]
    return jnp.concatenate([x1 * c - x2 * s, x1 * s + x2 * c], axis=-1)


def _rope_t(d, c, s):
    d1 = d[:, :ROPE // 2]
    d2 = d[:, ROPE // 2:]
    return jnp.concatenate([d1 * c + d2 * s, d2 * c - d1 * s], axis=-1)


def _rope_tables(pos2, inv_freq):
    t = pos2.shape[0]

    def body(p_ref, f_ref, c_ref, s_ref):
        ang = p_ref[...].astype(F32) * f_ref[...]
        c_ref[...] = jnp.cos(ang)
        s_ref[...] = jnp.sin(ang)

    return pl.pallas_call(body, out_shape=[S((t, ROPE // 2), F32), S((t, ROPE // 2), F32)],
                          name="rope_tables")(pos2, inv_freq)


def _down_norm(xb, wdown_g, gq3, gkv3, cos, sin, j):
    t = xb.shape[0]
    bm = _row_tile(t)

    def body(x_ref, w_ref, gq_ref, gkv_ref, c_ref, s_ref, down_ref, cq_ref, ckv_ref, kpe_ref):
        w = w_ref[...].reshape(D, ODD_IN)
        down = _dot(x_ref[...], w)
        down_ref[...] = down
        q = down[:, :Q_RANK]
        cq_ref[...] = (q * lax.rsqrt(jnp.mean(q * q, axis=-1, keepdims=True) + RMS_EPS) * gq_ref[...]).astype(BF16)
        kv = down[:, Q_RANK:Q_RANK + KV_RANK]
        ckv_ref[...] = (kv * lax.rsqrt(jnp.mean(kv * kv, axis=-1, keepdims=True) + RMS_EPS)
                        * gkv_ref[...]).astype(BF16)
        kpe_ref[...] = _rope(down[:, Q_RANK + KV_RANK:], c_ref[...], s_ref[...])

    row = lambda n: pl.BlockSpec((bm, n), lambda i: (i, 0))
    return pl.pallas_call(
        body, grid=(t // bm,),
        in_specs=[row(D), _full((N_DEV, D // N_DEV, ODD_IN)),
                  pl.BlockSpec((None, 1, Q_RANK), lambda i: (j, 0, 0)),
                  pl.BlockSpec((None, 1, KV_RANK), lambda i: (j, 0, 0)), row(ROPE // 2), row(ROPE // 2)],
        out_specs=[row(ODD_IN), row(Q_RANK), row(KV_RANK), row(ROPE)],
        out_shape=[S((t, ODD_IN), F32), S((t, Q_RANK), BF16), S((t, KV_RANK), BF16), S((t, ROPE), F32)],
        compiler_params=_cp("parallel"), name="down_norm")(xb, wdown_g, gq3, gkv3, cos, sin)


def _q_tile(t, widest):
    return min(widest, t // 2)


def _attn_probs(q, k, qs):
    s = _dot_nt(q, k) * ATT_SCALE
    tq = q.shape[0]
    rows = lax.broadcasted_iota(jnp.int32, (tq, tq), 0)
    cols = lax.broadcasted_iota(jnp.int32, (tq, tq), 1)
    last = jnp.where(jnp.right_shift(cols, CHUNK_SHIFT) <= jnp.right_shift(rows, CHUNK_SHIFT), s[:, qs:], NEG)
    s = last if qs == 0 else jnp.concatenate([s[:, :qs], last], axis=1)
    e = jnp.exp(s - jnp.max(s, axis=-1, keepdims=True))
    return e / jnp.sum(e, axis=-1, keepdims=True)


def _head_qkv(cq, ckv, kpe, c, s, wq_ref, wkv_ref):
    q = jnp.concatenate([_dot(cq, wq_ref[:, :NOPE]), _rope(_dot(cq, wq_ref[:, NOPE:]), c, s)], axis=1).astype(BF16)
    k = jnp.concatenate([_dot(ckv, wkv_ref[:, :NOPE]), kpe], axis=1).astype(BF16)
    vv = _dot(ckv, wkv_ref[:, NOPE:]).astype(BF16)
    return q, k, vv


def _attn_in_specs(t):
    return [_full((t, Q_RANK)), _full((t, KV_RANK)), _full((t, ROPE)), _full((t, ROPE // 2)), _full((t, ROPE // 2)),
            pl.BlockSpec((None, Q_RANK, NOPE + ROPE), lambda h: (h, 0, 0)),
            pl.BlockSpec((None, KV_RANK, NOPE + VDIM), lambda h: (h, 0, 0)),
            pl.BlockSpec((None, VDIM, D), lambda h: (h, 0, 0))]


def _attn_fwd(cq, ckv, kpe, cos, sin, wqb_g, wkvb_g, wo_g):
    t = cq.shape[0]
    tq = _q_tile(t, 256)

    def body(cq_ref, ckv_ref, kpe_ref, c_ref, s_ref, wq_ref, wkv_ref, wo_ref, o_ref, mix_ref):
        q, k, vv = _head_qkv(cq_ref[...], ckv_ref[...], kpe_ref[...], c_ref[...], s_ref[...], wq_ref, wkv_ref)
        for qs in range(0, t, tq):
            ke = qs + tq
            p = _attn_probs(q[qs:ke], k[:ke], qs)
            o_ref[qs:ke, :] = _dot(p.astype(BF16), vv[:ke]).astype(BF16)
        c = _dot(o_ref[...], wo_ref[...])

        @pl.when(pl.program_id(0) == 0)
        def _():
            mix_ref[...] = c

        @pl.when(pl.program_id(0) > 0)
        def _():
            mix_ref[...] += c

    return pl.pallas_call(
        body, grid=(MLA_HEADS,), in_specs=_attn_in_specs(t),
        out_specs=[pl.BlockSpec((None, t, VDIM), lambda h: (h, 0, 0)), _full((t, D))],
        out_shape=[S((MLA_HEADS, t, VDIM), BF16), S((t, D), F32)],
        compiler_params=_cp("arbitrary"), name="attn_fwd")(cq, ckv, kpe, cos, sin, wqb_g, wkvb_g, wo_g)


def _attn_bwd(cq, ckv, kpe, cos, sin, wqb_g, wkvb_g, wo_g, o, dzb):
    t = cq.shape[0]
    tq = _q_tile(t, 512)

    def body(cq_ref, ckv_ref, kpe_ref, c_ref, s_ref, wq_ref, wkv_ref, wo_ref, o_ref, dz_ref,
             dwo_ref, dwq_ref, dwkv_ref, dcq_ref, dckv_ref, dkpe_ref, dkt_s, dvt_s, dq_s):
        cqv = cq_ref[...]
        ckvv = ckv_ref[...]
        c = c_ref[...]
        s = s_ref[...]
        q, k, vv = _head_qkv(cqv, ckvv, kpe_ref[...], c, s, wq_ref, wkv_ref)
        dzv = dz_ref[...]
        dwo_ref[...] = _dot_tn(o_ref[...], dzv).astype(BF16)
        do = _dot_nt(dzv, wo_ref[...]).astype(BF16)
        dkt_s[...] = jnp.zeros_like(dkt_s)
        dvt_s[...] = jnp.zeros_like(dvt_s)
        for qs in range(0, t, tq):
            ke = qs + tq
            p = _attn_probs(q[qs:ke], k[:ke], qs)
            dp = _dot_nt(do[qs:ke], vv[:ke])
            ds = (p * (dp - jnp.sum(p * dp, axis=-1, keepdims=True)) * ATT_SCALE).astype(BF16)
            dq_s[qs:ke, :] = _dot(ds, k[:ke])
            dkt_s[0:NOPE + ROPE, 0:ke] += _dot_tn(q[qs:ke], ds)
            dvt_s[:, 0:ke] += _dot_tn(do[qs:ke], p.astype(BF16))
        dk = dkt_s[...].T
        dqn = dq_s[:, :NOPE].astype(BF16)
        dqp = _rope_t(dq_s[:, NOPE:], c, s).astype(BF16)
        dkn = dk[:, :NOPE].astype(BF16)
        dkp = dk[:, NOPE:NOPE + ROPE]
        dvv = dvt_s[...].T.astype(BF16)
        dwq_ref[:, :NOPE] = _dot_tn(cqv, dqn).astype(BF16)
        dwq_ref[:, NOPE:] = _dot_tn(cqv, dqp).astype(BF16)
        dwkv_ref[:, :NOPE] = _dot_tn(ckvv, dkn).astype(BF16)
        dwkv_ref[:, NOPE:] = _dot_tn(ckvv, dvv).astype(BF16)
        dcq = _dot_nt(dqn, wq_ref[:, :NOPE]) + _dot_nt(dqp, wq_ref[:, NOPE:])
        dckv = _dot_nt(dkn, wkv_ref[:, :NOPE]) + _dot_nt(dvv, wkv_ref[:, NOPE:])

        @pl.when(pl.program_id(0) == 0)
        def _():
            dcq_ref[...] = dcq
            dckv_ref[...] = dckv
            dkpe_ref[...] = dkp

        @pl.when(pl.program_id(0) > 0)
        def _():
            dcq_ref[...] += dcq
            dckv_ref[...] += dckv
            dkpe_ref[...] += dkp

    per_head = lambda a, b: pl.BlockSpec((None, a, b), lambda h: (h, 0, 0))
    return pl.pallas_call(
        body, grid=(MLA_HEADS,),
        in_specs=_attn_in_specs(t) + [per_head(t, VDIM), _full((t, D))],
        out_specs=[per_head(VDIM, D), per_head(Q_RANK, NOPE + ROPE), per_head(KV_RANK, NOPE + VDIM),
                   _full((t, Q_RANK)), _full((t, KV_RANK)), _full((t, ROPE))],
        out_shape=[S((MLA_HEADS, VDIM, D), BF16), S((MLA_HEADS, Q_RANK, NOPE + ROPE), BF16),
                   S((MLA_HEADS, KV_RANK, NOPE + VDIM), BF16),
                   S((t, Q_RANK), F32), S((t, KV_RANK), F32), S((t, ROPE), F32)],
        scratch_shapes=[pltpu.VMEM((2 * NOPE, t), F32), pltpu.VMEM((VDIM, t), F32),
                        pltpu.VMEM((t, NOPE + ROPE), F32)],
        compiler_params=_cp("arbitrary"), name="attn_bwd")(cq, ckv, kpe, cos, sin, wqb_g, wkvb_g, wo_g, o, dzb)


def _rms_bwd(down, dcq, dckv, dkpe, cos, sin, gq3, gkv3, j):
    t = down.shape[0]
    bm = _row_tile(t)

    def body(down_ref, dcq_ref, dckv_ref, dkpe_ref, c_ref, s_ref, gq_ref, gkv_ref, dd_ref, dgq_ref, dgkv_ref):
        @pl.when(pl.program_id(0) == 0)
        def _():
            dgq_ref[...] = jnp.zeros_like(dgq_ref)
            dgkv_ref[...] = jnp.zeros_like(dgkv_ref)

        def rms_b(x, dy, g):
            rstd = lax.rsqrt(jnp.mean(x * x, axis=-1, keepdims=True) + RMS_EPS)
            xh = x * rstd
            dyg = dy * g
            return rstd * (dyg - xh * jnp.mean(dyg * xh, axis=-1, keepdims=True)), jnp.sum(dy * xh, axis=0, keepdims=True)

        dq, dgq = rms_b(down_ref[:, :Q_RANK], dcq_ref[...], gq_ref[...])
        dkv, dgkv = rms_b(down_ref[:, Q_RANK:Q_RANK + KV_RANK], dckv_ref[...], gkv_ref[...])
        dgq_ref[...] += dgq
        dgkv_ref[...] += dgkv
        dd_ref[:, :Q_RANK] = dq.astype(BF16)
        dd_ref[:, Q_RANK:Q_RANK + KV_RANK] = dkv.astype(BF16)
        dd_ref[:, Q_RANK + KV_RANK:] = _rope_t(dkpe_ref[...], c_ref[...], s_ref[...]).astype(BF16)

    row = lambda n: pl.BlockSpec((bm, n), lambda i: (i, 0))
    return pl.pallas_call(
        body, grid=(t // bm,),
        in_specs=[row(ODD_IN), row(Q_RANK), row(KV_RANK), row(ROPE), row(ROPE // 2), row(ROPE // 2),
                  pl.BlockSpec((None, 1, Q_RANK), lambda i: (j, 0, 0)),
                  pl.BlockSpec((None, 1, KV_RANK), lambda i: (j, 0, 0))],
        out_specs=[row(ODD_IN), _full((1, Q_RANK)), _full((1, KV_RANK))],
        out_shape=[S((t, ODD_IN), BF16), S((1, Q_RANK), F32), S((1, KV_RANK), F32)],
        compiler_params=_cp("arbitrary"), name="rms_bwd")(down, dcq, dckv, dkpe, cos, sin, gq3, gkv3)


def _col_blocks(t, n, bn):
    return pl.BlockSpec((t, bn), lambda i: (0, i))


def _row_blocks(n, bm):
    return pl.BlockSpec((bm, n), lambda i: (i, 0))


def _local_step(x, pos2, tgt, small, weights_of, grads_done, start_dep=None, prefetch=None):
    t = x.shape[0]
    bm = _row_tile(t)
    inv_freq = (ROPE_THETA ** (-jnp.arange(0, ROPE, 2, dtype=F32) / ROPE)).reshape(1, ROPE // 2)
    cos, sin = _rope_tables(pos2, inv_freq)
    lru_p = {k: small[k] for k in ("conv_w", "conv_b", "w_a", "b_a", "w_x", "b_x", "lam")}

    saved = []
    y, yb = x, x.astype(BF16)
    for l in range(DEPTH):
        j = l // 2
        big = weights_of(l, 0, y)
        sv = dict(xb=yb, big=big)
        if l % 2 == 0:
            proj = _mm(yb, big["win2d"], mode="nn", grid=(EVEN_IN // 512,), a_spec=_full((t, D)),
                       b_spec=_col_blocks(D, EVEN_IN, 512), out_shape=S((t, EVEN_IN), F32),
                       out_spec=_col_blocks(t, EVEN_IN, 512), name="even_proj", dep=start_dep if l == 0 else None)
            ycat = jnp.concatenate([_pool_fwd(proj, small["pool_w"], small["pool_scale"], j),
                                    _lru_fwd(proj, lru_p, j)], axis=1)
            big.update(weights_of(l, 1, ycat))
            z1, y1, y1b = _proj_resid_ln(y, ycat, big["wout2d"], small["ln_mix_g"], small["ln_mix_b"], l, "even_out")
            sv.update(proj=proj, ycat=ycat)
        else:
            down, cq, ckv, kpe = _down_norm(yb, big["wdown"], small["gq"], small["gkv"], cos, sin, j)
            o, mix = _attn_fwd(cq, ckv, kpe, cos, sin, big["wqb"], big["wkvb"], big["wo"])
            z1, y1, y1b = _resid_ln(y, mix, small["ln_mix_g"], small["ln_mix_b"], l, "resid_ln")
            sv.update(down=down, cq=cq, ckv=ckv, kpe=kpe, o=o)
        fetched = prefetch(l + 1, y1) if prefetch is not None and l + 1 < DEPTH else None
        z2, y, yb = _mlp_fwd(y1, y1b, big["w1"], big["w2"], small["ln_ffn_g"], small["ln_ffn_b"], l, dep=fetched)
        sv.update(z1=z1, y1b=y1b, z2=z2)
        saved.append(sv)

    dy, loss_tile = _loss_grad(y, tgt)

    g = {k: [None] * n for k, n in (("ln_mix_g", 4), ("ln_mix_b", 4), ("ln_ffn_g", 4), ("ln_ffn_b", 4),
                                    ("pool_w", 2), ("pool_scale", 2), ("conv_w", 2), ("conv_b", 2),
                                    ("w_a", 2), ("b_a", 2), ("w_x", 2), ("b_x", 2), ("lam", 2),
                                    ("gq", 2), ("gkv", 2))}
    dep = None
    for l in reversed(range(DEPTH)):
        j = l // 2
        sv = saved[l]
        big = sv["big"]
        dz2, dz2b, g["ln_ffn_g"][l], g["ln_ffn_b"][l] = _ln_bwd(dy, sv["z2"], small["ln_ffn_g"], l, "ln_bwd", dep=dep)
        act, dh, dff = _mlp_bwd_dh(sv["y1b"], dz2b, big["w1"], big["w2"])
        dw1 = _mm(sv["y1b"], dh, mode="tn", grid=(N_DEV,), a_spec=_full((t, D)),
                  b_spec=_col_blocks(t, D_FF, FF_BLK), out_shape=S((N_DEV, D, FF_BLK), BF16),
                  out_spec=pl.BlockSpec((None, D, FF_BLK), lambda i: (i, 0, 0)), name="mlp_dw1")
        dw2 = _mm(act, dz2b, mode="tn", grid=(N_DEV,), a_spec=_col_blocks(t, D_FF, FF_BLK),
                  b_spec=_full((t, D)), out_shape=S((N_DEV, FF_BLK, D), BF16),
                  out_spec=pl.BlockSpec((None, FF_BLK, D), lambda i: (i, 0, 0)), name="mlp_dw2")
        dep = grads_done(l, dict(w1=dw1, w2=dw2))
        dz1, dz1b, g["ln_mix_g"][l], g["ln_mix_b"][l] = _ln_bwd(dff, sv["z1"], small["ln_mix_g"], l, "ln_bwd_res",
                                                                 r=dz2, dep=dep)
        if l % 2 == 0:
            wout = big["wout2d"]
            dycat = _mm(dz1b, wout, mode="nt", grid=(EVEN_MIX // 512,), a_spec=_full((t, D)),
                        b_spec=_row_blocks(D, 512), out_shape=S((t, EVEN_MIX), F32),
                        out_spec=_col_blocks(t, EVEN_MIX, 512), name="even_dycat")
            dwout = _mm(sv["ycat"], dz1b, mode="tn", grid=(EVEN_MIX // 512,), a_spec=_col_blocks(t, EVEN_MIX, 512),
                        b_spec=_full((t, D)), out_shape=S((EVEN_MIX, D), BF16), out_spec=_row_blocks(D, 512),
                        name="even_dwout")
            du_pool, g["pool_w"][j], g["pool_scale"][j] = _pool_bwd(sv["proj"], dycat, small["pool_w"],
                                                                   small["pool_scale"], j)
            (du_lru, du_gate, g["conv_w"][j], g["conv_b"][j], g["w_a"][j], g["b_a"][j], g["w_x"][j], g["b_x"][j],
             g["lam"][j]) = _lru_bwd(sv["proj"], dycat, lru_p, j)
            dproj = jnp.concatenate([du_pool, du_lru, du_gate], axis=1)
            dwin = _mm(sv["xb"], dproj, mode="tn", grid=(EVEN_IN // 512,), a_spec=_full((t, D)),
                       b_spec=_col_blocks(t, EVEN_IN, 512), out_shape=S((D, EVEN_IN), BF16),
                       out_spec=_col_blocks(D, EVEN_IN, 512), name="even_dwin")
            dep = grads_done(l, dict(win=dwin.reshape(D, N_DEV, EVEN_IN // N_DEV).transpose(1, 0, 2),
                                     wout=dwout.reshape(N_DEV, EVEN_MIX // N_DEV, D)))
            dy = _mm(dproj, big["win2d"], mode="nt", grid=(t // bm,), a_spec=_row_blocks(EVEN_IN, bm),
                     b_spec=_full((D, EVEN_IN)), out_shape=S((t, D), F32), out_spec=_row_blocks(D, bm),
                     add=dz1, add_spec=_row_blocks(D, bm), add_scale=ALPHA, name="even_dx")
        else:
            dwo, dwqb, dwkvb, dcq, dckv, dkpe = _attn_bwd(
                sv["cq"], sv["ckv"], sv["kpe"], cos, sin, big["wqb"], big["wkvb"], big["wo"], sv["o"], dz1b)
            ddown, g["gq"][j], g["gkv"][j] = _rms_bwd(sv["down"], dcq, dckv, dkpe, cos, sin, small["gq"],
                                                     small["gkv"], j)
            dwdown = _mm(sv["xb"], ddown, mode="tn", grid=(N_DEV,), a_spec=_col_blocks(t, D, D // N_DEV),
                         b_spec=_full((t, ODD_IN)), out_shape=S((N_DEV, D // N_DEV, ODD_IN), BF16),
                         out_spec=pl.BlockSpec((None, D // N_DEV, ODD_IN), lambda i: (i, 0, 0)),
                         name="odd_dwdown")
            dep = grads_done(l, dict(wdown=dwdown, wqb=dwqb, wkvb=dwkvb, wo=dwo))
            dy = _mm(ddown, big["wdown2d"], mode="nt", grid=(t // bm,), a_spec=_row_blocks(ODD_IN, bm),
                     b_spec=_full((D, ODD_IN)), out_shape=S((t, D), F32), out_spec=_row_blocks(D, bm),
                     add=dz1, add_spec=_row_blocks(D, bm), add_scale=ALPHA, name="odd_dx")
    return loss_tile[0, 0], dy, g


def _mesh_place():
    x, y, c = lax.axis_index("x"), lax.axis_index("y"), lax.axis_index("c")
    return x, y, c


def _peer(place, k):
    x, y, c = place
    return (1 - x if k & 4 else x, 1 - y if k & 2 else y, 1 - c if k & 1 else c)


def _index(place):
    x, y, c = place
    return 4 * x + 2 * y + c


ANY = pl.BlockSpec(memory_space=pl.ANY)


def _all_gather_big(zones):
    n = len(zones)

    def body(*refs):
        outs = refs[n:2 * n]
        send, recv = refs[2 * n:]
        x, y, c = _mesh_place()
        me, sibling = (x, y, c), (x, y, 1 - c)
        chips = [(1 - x, y), (x, 1 - y), (1 - x, 1 - y)]

        def copy(w, k, block, to):
            blk = outs[w].at[_index(block)]
            return pltpu.make_async_remote_copy(src_ref=blk, dst_ref=blk, send_sem=send.at[w, k], recv_sem=recv.at[w, k],
                                                device_id=to, device_id_type=MESH)

        first = []
        for w in range(n):
            first.append(copy(w, 0, me, sibling))
            first += [copy(w, 1 + j, me, (*chip, c)) for j, chip in enumerate(chips)]
        for cp in first:
            cp.start()
        passed = []
        for w in range(n):
            for j, chip in enumerate(chips):
                copy(w, 1 + j, (*chip, c), me).wait_recv()
                cp = copy(w, 4 + j, (*chip, c), sibling)
                cp.start()
                passed.append(cp)
        for w in range(n):
            copy(w, 0, sibling, me).wait_recv()
            for j, chip in enumerate(chips):
                copy(w, 4 + j, (*chip, 1 - c), me).wait_recv()
        for cp in first + passed:
            cp.wait_send()

    return pl.pallas_call(
        body, in_specs=[ANY] * n, out_specs=[ANY] * n, out_shape=[S(z.shape, z.dtype) for z in zones],
        input_output_aliases={i: i for i in range(n)},
        scratch_shapes=[pltpu.SemaphoreType.DMA((n, N_DEV - 1)), pltpu.SemaphoreType.DMA((n, N_DEV - 1))],
        compiler_params=pltpu.CompilerParams(has_side_effects=True), name="all_gather_big")(*zones)


def _shard_rows_tile(a):
    return max(d for d in range(16, 257, 16) if a % d == 0)


HBM = pl.BlockSpec(memory_space=pltpu.HBM)
SEM = pl.BlockSpec(memory_space=pltpu.SEMAPHORE)
DATAFLOW = pltpu.SideEffectType.DATAFLOW_SIDE_EFFECTING


def _in_hbm(a):
    return pltpu.with_memory_space_constraint(a, pltpu.HBM)


def _gather_ici_copies(place, src, land, w):
    me = _index(place)
    return [(_peer(place, k), land.at[me], land.at[me]) for k in (1, 2, 4, 6)]


def _gather_d2d_copies(place, src, land, w):
    blocks = [_index(_peer(place, k)) for k in (2, 4, 6)]
    return [(_peer(place, 1), land.at[b], land.at[b]) for b in blocks]


GATHER_ICI = (4, _gather_ici_copies)
GATHER_D2D = (3, _gather_d2d_copies)


def _scatter_plan(layers):
    def copies(place, src, land, w):
        me = _index(place)
        mine = land.at[me] if layers[w] is None else land.at[me, layers[w]]
        return [(_peer(place, k), src.at[_index(_peer(place, k))], mine) for k in range(1, N_DEV)]
    return (N_DEV - 1, copies)


def _gather_all_copies(place, src, land, w):
    me = _index(place)
    return [(_peer(place, k), land.at[me], land.at[me]) for k in range(1, N_DEV)]


GATHER_ALL = (N_DEV - 1, _gather_all_copies)


def _sum_blocks(zone, part, me):
    r = part.shape[1]

    def body(me_ref, z_ref, p_ref, o_ref):
        acc = None
        for s in range(N_DEV):
            term = jnp.where(me_ref[0] == s, p_ref[...], z_ref[s])
            acc = term if acc is None else acc + term
        o_ref[...] = acc

    grid_spec = pltpu.PrefetchScalarGridSpec(
        num_scalar_prefetch=1, grid=(1,),
        in_specs=[pl.BlockSpec((N_DEV, r, 128), lambda i, me_ref: (0, 0, 0)),
                  pl.BlockSpec((None, r, 128), lambda i, me_ref: (me_ref[0], 0, 0))],
        out_specs=pl.BlockSpec((r, 128), lambda i, me_ref: (0, 0)))
    return pl.pallas_call(body, grid_spec=grid_spec, out_shape=S((r, 128), F32),
                          compiler_params=_cp("arbitrary"), name="sum_small")(me, zone, part)


def _exchange_start(srcs, lands, plan, name, after=()):
    ns, n = len(srcs), len(lands)
    n_in = ns + n + len(after)
    per, copies = plan

    def body(*refs):
        ins, land = refs[:ns], refs[ns:ns + n]
        send, recv = refs[n_in], refs[n_in + 1]
        token = refs[-1]
        place = _mesh_place()
        for i in range(per):
            for w in range(n):
                target, src, dst = copies(place, ins[w] if ns else None, land[w], w)[i]
                pltpu.make_async_remote_copy(src_ref=src, dst_ref=dst, send_sem=send.at[w * per + i],
                                             recv_sem=recv.at[w * per + i], device_id=target, device_id_type=MESH).start()
        token[...] = jnp.zeros_like(token)

    sems = pltpu.SemaphoreType.DMA((n * per,))
    thru = [pltpu.HBM(a.shape, a.dtype) for a in list(srcs) + list(lands)]
    out = pl.pallas_call(
        body, name=name, in_specs=[HBM] * (ns + n) + [ANY] * len(after),
        out_shape=(sems, sems, *thru, S((8, 128), F32)),
        out_specs=(SEM, SEM, *([HBM] * (ns + n)), pl.BlockSpec(memory_space=pltpu.VMEM)),
        input_output_aliases={i: 2 + i for i in range(ns + n)},
        compiler_params=pltpu.CompilerParams(has_side_effects=DATAFLOW),
    )(*[_in_hbm(a) for a in list(srcs) + list(lands)], *after)
    return out[0], out[1], list(out[2:2 + ns]), list(out[2 + ns:2 + ns + n]), out[-1]


def _exchange_wait(send, recv, srcs, lands, plan, after, name):
    ns, n = len(srcs), len(lands)
    per, copies = plan
    afters = tuple(after) if isinstance(after, (tuple, list)) else (after,)

    def body(*refs):
        ins, land = refs[:ns], refs[ns:ns + n]
        send_ref, recv_ref = refs[ns + n], refs[ns + n + 1]
        place = _mesh_place()
        for i in range(per):
            for w in range(n):
                target, src, dst = copies(place, ins[w] if ns else None, land[w], w)[i]
                cp = pltpu.make_async_remote_copy(src_ref=src, dst_ref=dst, send_sem=send_ref.at[w * per + i],
                                                  recv_sem=recv_ref.at[w * per + i], device_id=target,
                                                  device_id_type=MESH)
                cp.wait_send()
                cp.wait_recv()

    thru = [pltpu.HBM(a.shape, a.dtype) for a in list(srcs) + list(lands)]
    out = pl.pallas_call(
        body, name=name, in_specs=[HBM] * (ns + n) + [SEM, SEM] + [ANY] * len(afters),
        out_shape=tuple(thru), out_specs=tuple([HBM] * (ns + n)),
        input_output_aliases={i: i for i in range(ns + n)},
        compiler_params=pltpu.CompilerParams(has_side_effects=DATAFLOW),
    )(*srcs, *lands, send, recv, *afters)
    return list(out[:ns]), list(out[ns:])


def _all_reduce_small(part, name, deps=()):
    def body(*refs):
        p_ref = refs[0]
        o_ref, rbuf, send1, recv1, send2, recv2 = refs[-6:]
        place = _mesh_place()
        me = _index(place)
        rbuf[pl.ds(me, 1)] = p_ref[pl.ds(me, 1)]
        first = [pltpu.make_async_remote_copy(src_ref=p_ref.at[_index(_peer(place, k))], dst_ref=rbuf.at[me],
                                              send_sem=send1.at[k - 1], recv_sem=recv1.at[k - 1],
                                              device_id=_peer(place, k), device_id_type=MESH)
                 for k in range(1, N_DEV)]
        for cp in first:
            cp.start()
        for cp in first:
            cp.wait()
        acc = rbuf[0]
        for d in range(1, N_DEV):
            acc = acc + rbuf[d]
        o_ref[pl.ds(me, 1)] = acc[None]
        second = [pltpu.make_async_remote_copy(src_ref=o_ref.at[me], dst_ref=o_ref.at[me], send_sem=send2.at[k - 1],
                                               recv_sem=recv2.at[k - 1], device_id=_peer(place, k),
                                               device_id_type=MESH)
                  for k in range(1, N_DEV)]
        for cp in second:
            cp.start()
        for cp in second:
            cp.wait()

    vm = pl.BlockSpec(memory_space=pltpu.VMEM)
    ops = [part, *deps]
    return pl.pallas_call(
        body, in_specs=[vm] + [ANY] * len(deps), out_specs=vm, out_shape=S(part.shape, F32),
        scratch_shapes=[pltpu.VMEM(part.shape, F32)] + [pltpu.SemaphoreType.DMA((N_DEV - 1,))] * 4,
        compiler_params=pltpu.CompilerParams(has_side_effects=True, vmem_limit_bytes=VMEM_LIMIT), name=name)(*ops)


def _adamw(w, g, m, v):
    m = ADAM_B1 * m + (1.0 - ADAM_B1) * g
    v = ADAM_B2 * v + (1.0 - ADAM_B2) * (g * g)
    m_hat = m / (1.0 - ADAM_B1 ** ADAM_STEP)
    v_hat = v / (1.0 - ADAM_B2 ** ADAM_STEP)
    return -ADAM_LR * (m_hat / (jnp.sqrt(v_hat) + ADAM_EPS) + ADAM_WD * w), m, v


def _adam_big(parts, own, me, w, m, v, name):
    nl, a, b = w.shape
    ta = _shard_rows_tile(a)

    def body(me_ref, p_ref, *refs):
        own_refs, (w_ref, m_ref, v_ref, g_ref, d_ref, mo_ref, vo_ref) = refs[:nl], refs[nl:]
        layer = pl.program_id(0)
        mine = own_refs[0][...]
        for k in range(1, nl):
            mine = jnp.where(layer == k, own_refs[k][...], mine)
        g = None
        for s in range(N_DEV):
            term = jnp.where(me_ref[0] == s, mine, p_ref[s]).astype(F32)
            g = term if g is None else g + term
        g_ref[...] = g
        d_ref[...], mo_ref[...], vo_ref[...] = _adamw(w_ref[...], g, m_ref[...], v_ref[...])

    blk = pl.BlockSpec((None, ta, b), lambda l, i, me_ref: (l, i, 0))

    def own_spec(k):
        return pl.BlockSpec((None, ta, b), lambda l, i, me_ref: (me_ref[0], jnp.where(l == k, i, 0), 0))

    grid_spec = pltpu.PrefetchScalarGridSpec(
        num_scalar_prefetch=1, grid=(nl, a // ta),
        in_specs=[pl.BlockSpec((N_DEV, None, ta, b), lambda l, i, me_ref: (0, l, i, 0))]
        + [own_spec(k) for k in range(nl)] + [blk, blk, blk],
        out_specs=[blk] * 4)
    return pl.pallas_call(body, grid_spec=grid_spec, out_shape=[S(w.shape, F32)] * 4,
                          compiler_params=_cp("arbitrary", "arbitrary"), name=name)(me, parts, *own, w, m, v)


def _adam_small(g, w, m, v, name):
    def body(g_ref, w_ref, m_ref, v_ref, d_ref, mo_ref, vo_ref):
        d_ref[...], mo_ref[...], vo_ref[...] = _adamw(w_ref[...], g_ref[...], m_ref[...], v_ref[...])

    return pl.pallas_call(body, out_shape=[S(g.shape, F32)] * 3, compiler_params=_cp(), name=name)(g, w, m, v)


BIG = ("even_w_in", "even_w_out", "mla_w_down", "mla_w_qb", "mla_w_kvb", "mla_w_o", "mlp_w1", "mlp_w2")
BIG_KEY = dict(even_w_in="win", even_w_out="wout", mla_w_down="wdown", mla_w_qb="wqb", mla_w_kvb="wkvb",
               mla_w_o="wo", mlp_w1="w1", mlp_w2="w2")
SMALL = (("ln_mix_g", "ln_mix_g", None), ("ln_mix_b", "ln_mix_b", None), ("ln_ffn_g", "ln_ffn_g", None),
         ("ln_ffn_b", "ln_ffn_b", None), ("pool_w", "pool_w", None), ("pool_scale", "pool_scale", None),
         ("lru_conv_w", "conv_w", 2), ("lru_conv_b", "conv_b", None), ("lru_w_a", "w_a", None),
         ("lru_b_a", "b_a", None), ("lru_w_x", "w_x", None), ("lru_b_x", "b_x", None), ("lru_lambda", "lam", None),
         ("mla_q_norm_g", "gq", 1), ("mla_kv_norm_g", "gkv", 1))
WEIGHTS = ("ln_mix_g", "ln_mix_b", "ln_ffn_g", "ln_ffn_b", "even_w_in", "pool_w", "pool_scale", "lru_conv_w",
           "lru_conv_b", "lru_w_a", "lru_b_a", "lru_w_x", "lru_b_x", "lru_lambda", "even_w_out", "mla_w_down",
           "mla_q_norm_g", "mla_kv_norm_g", "mla_w_qb", "mla_w_kvb", "mla_w_o", "mlp_w1", "mlp_w2")
ALL_AXES = ("x", "y", "c")


def _layer_weights(l):
    j = l // 2
    if l % 2 == 0:
        mixer = [("win", "even_w_in", j), ("wout", "even_w_out", j)]
    else:
        mixer = [("wdown", "mla_w_down", j), ("wqb", "mla_w_qb", j), ("wkvb", "mla_w_kvb", j), ("wo", "mla_w_o", j)]
    return mixer + [("w1", "mlp_w1", l), ("w2", "mlp_w2", l)]


def _pack(arrays, multiple):
    flat = jnp.concatenate([a.reshape(-1) for a in arrays])
    pad = (-flat.shape[0]) % multiple
    return jnp.pad(flat, (0, pad))


def _unpack(flat, shapes):
    out, at = [], 0
    for shp in shapes:
        n = 1
        for s in shp:
            n *= s
        out.append(flat[at:at + n].reshape(shp))
        at += n
    return out


def _global_shape(local_shape, axis):
    if axis is None:
        return tuple(local_shape)
    return tuple(s * N_DEV if i == axis else s for i, s in enumerate(local_shape))


def _step(x, positions, tgt, w, m, v):
    t = x.shape[1]
    me = _index(_mesh_place())

    sharded = [(name, axis) for name, _, axis in SMALL if axis is not None]
    zeros_with_mine = [lax.dynamic_update_slice_in_dim(jnp.zeros(_global_shape(w[name].shape, axis), F32), w[name],
                                                       me * w[name].shape[axis], axis) for name, axis in sharded]
    chunk = N_DEV * 8 * 128
    gathered = _all_reduce_small(_pack(zeros_with_mine, chunk).reshape(N_DEV, -1, 128), "gather_small")
    full = dict(zip([name for name, _ in sharded],
                    _unpack(gathered.reshape(-1), [_global_shape(w[name].shape, axis) for name, axis in sharded])))

    def zone_of(shard):
        return lax.dynamic_update_slice_in_dim(lax.empty((N_DEV,) + shard.shape, BF16), shard.astype(BF16)[None], me, 0)

    def keys_of(l, part):
        keys = [key for key, _, _ in _layer_weights(l)]
        if l == 0:
            return keys[:1] if part == 0 else keys[1:]
        return keys if part == 0 else []

    shard_of = {(l, key): w[name][i] for l in range(DEPTH) for key, name, i in _layer_weights(l)}
    first = _all_gather_big([zone_of(shard_of[0, key]) for key in keys_of(0, 0)])
    flights, after = {}, (first[0], gathered)
    for l in range(DEPTH):
        for part in (0, 1):
            if (l, part) != (0, 0) and keys_of(l, part):
                zones = [zone_of(shard_of[l, key]) for key in keys_of(l, part)]
                send, recv, _, lands, token = _exchange_start([], zones, GATHER_ICI, "gather_start_%d_%d" % (l, part),
                                                              after=after)
                flights[l, part] = (send, recv, [], lands)
                after = (token,)

    passing = {}

    def pass_on(l, part, after):
        tag = "%d_%d" % (l, part)
        _, lands = _exchange_wait(*flights[l, part], GATHER_ICI, after, "gather_wait_" + tag)
        send, recv, _, lands, token = _exchange_start([], lands, GATHER_D2D, "gather_pass_" + tag)
        passing[l, part] = (send, recv, [], lands)
        return token

    def early_pass(l, after):
        return pass_on(l, 0, after) if l >= 2 else None

    def weights_of(l, part, after):
        keys = keys_of(l, part)
        if (l, part) == (0, 0):
            arrays = first
        elif keys:
            if (l, part) not in passing:
                pass_on(l, part, after)
            _, arrays = _exchange_wait(*passing[l, part], GATHER_D2D, after, "gather_pass_wait_%d_%d" % (l, part))
        big = dict(zip(keys, arrays)) if keys else {}
        if "win" in big:
            big["win2d"] = big["win"].transpose(1, 0, 2).reshape(D, EVEN_IN)
        if "wout" in big:
            big["wout2d"] = big["wout"].reshape(EVEN_MIX, D)
        if "wdown" in big:
            big["wdown2d"] = big["wdown"].reshape(D, ODD_IN)
        return big

    zone = {name: lax.empty((N_DEV,) + w[name].shape, BF16) for name in BIG}
    name_of = {key: name for name, key in BIG_KEY.items()}
    sent, last_token = [], [None]

    def grads_done(l, grads):
        keys = list(grads)
        index = {key: i for key, _, i in _layer_weights(l)}
        layers = [index[key] for key in keys]
        send, recv, srcs, lands, tok = _exchange_start([grads[k] for k in keys], [zone[name_of[k]] for k in keys],
                                                       _scatter_plan(layers), "scatter_start_%d_%s" % (l, keys[0]))
        for k, land in zip(keys, lands):
            zone[name_of[k]] = land
        sent.append((send, recv, srcs, keys, layers))
        last_token[0] = tok
        return tok

    row3 = lambda a: a.reshape(a.shape[0], 1, a.shape[1])
    small = dict(ln_mix_g=row3(w["ln_mix_g"]), ln_mix_b=row3(w["ln_mix_b"]), ln_ffn_g=row3(w["ln_ffn_g"]),
                 ln_ffn_b=row3(w["ln_ffn_b"]), pool_w=w["pool_w"], pool_scale=row3(w["pool_scale"]),
                 conv_w=full["lru_conv_w"], conv_b=row3(w["lru_conv_b"]), w_a=w["lru_w_a"], b_a=row3(w["lru_b_a"]),
                 w_x=w["lru_w_x"], b_x=row3(w["lru_b_x"]), lam=row3(w["lru_lambda"]),
                 gq=row3(full["mla_q_norm_g"]), gkv=row3(full["mla_kv_norm_g"]))

    loss_part, grad_x, g = _local_step(x[0], positions.reshape(t, 1), tgt[0], small, weights_of, grads_done,
                                       start_dep=token, prefetch=early_pass)

    own = {name: [None] * w[name].shape[0] for name in BIG}
    me_arr = me.astype(jnp.int32).reshape(1)
    out = {}
    local_g = [jnp.stack(g[key]).reshape(_global_shape(w[name].shape, axis)) for name, key, axis in SMALL]
    local_g.append(loss_part.reshape(1))
    part = _pack(local_g, chunk).reshape(N_DEV, -1, 128)
    small_plan = _scatter_plan([None])
    s_send, s_recv, s_src, s_land, after = _exchange_start([part], [lax.empty(part.shape, F32)], small_plan,
                                                           "small_scatter_start", after=(last_token[0],))
    for n_flight, (send, recv, srcs, keys, layers) in enumerate(sent):
        if n_flight == len(sent) - 1:
            for name in BIG:
                if BIG_KEY[name] not in keys:
                    out[name] = _adam_big(zone[name], own[name], me_arr, w[name], m[name], v[name], "adam_" + name)
            s_src, s_land = _exchange_wait(s_send, s_recv, s_src, s_land, small_plan, [o[0] for o in out.values()],
                                           "small_scatter_wait")
            chunk_sum = _sum_blocks(s_land[0], s_src[0], me_arr)
            r_zone = lax.dynamic_update_slice_in_dim(lax.empty(part.shape, F32), chunk_sum[None], me, 0)
            r_send, r_recv, _, r_land, after = _exchange_start([], [r_zone], GATHER_ALL, "small_gather_start")
        srcs, lands = _exchange_wait(send, recv, srcs, [zone[name_of[k]] for k in keys], _scatter_plan(layers),
                                     after, "scatter_wait_%d" % n_flight)
        for k, land, src, layer in zip(keys, lands, srcs, layers):
            zone[name_of[k]] = land
            own[name_of[k]][layer] = src
        after = lands[0]
    for name in BIG:
        if name not in out:
            out[name] = _adam_big(zone[name], own[name], me_arr, w[name], m[name], v[name], "adam_" + name)

    _, reduced = _exchange_wait(r_send, r_recv, [], r_land, GATHER_ALL, [out[name][0] for name in BIG],
                                "small_gather_wait")
    reduced = _unpack(reduced[0].reshape(-1), [a.shape for a in local_g])
    loss = reduced[-1][0]
    mine = [a if axis is None else lax.dynamic_slice_in_dim(a, me * w[name].shape[axis], w[name].shape[axis], axis)
            for a, (name, _, axis) in zip(reduced, SMALL)]
    for grad, (name, _, _) in zip(mine, SMALL):
        shape = w[name].shape
        as_2d = lambda a: a.reshape(-1, shape[-1])
        new = _adam_small(as_2d(grad), as_2d(w[name]), as_2d(m[name]), as_2d(v[name]), "adam_" + name)
        out[name] = (grad,) + tuple(a.reshape(shape) for a in new)

    return (loss, grad_x[None]) + tuple(out[name][i] for i in range(4) for name in WEIGHTS)


def kernel(x, positions, ln_mix_g, ln_mix_b, ln_ffn_g, ln_ffn_b, even_w_in, pool_w, pool_scale, lru_conv_w, lru_conv_b, lru_w_a, lru_b_a, lru_w_x, lru_b_x, lru_lambda, even_w_out, mla_w_down, mla_q_norm_g, mla_kv_norm_g, mla_w_qb, mla_w_kvb, mla_w_o, mlp_w1, mlp_w2, loss_target, m_ln_mix_g, m_ln_mix_b, m_ln_ffn_g, m_ln_ffn_b, m_even_w_in, m_pool_w, m_pool_scale, m_lru_conv_w, m_lru_conv_b, m_lru_w_a, m_lru_b_a, m_lru_w_x, m_lru_b_x, m_lru_lambda, m_even_w_out, m_mla_w_down, m_mla_q_norm_g, m_mla_kv_norm_g, m_mla_w_qb, m_mla_w_kvb, m_mla_w_o, m_mlp_w1, m_mlp_w2, v_ln_mix_g, v_ln_mix_b, v_ln_ffn_g, v_ln_ffn_b, v_even_w_in, v_pool_w, v_pool_scale, v_lru_conv_w, v_lru_conv_b, v_lru_w_a, v_lru_b_a, v_lru_w_x, v_lru_b_x, v_lru_lambda, v_even_w_out, v_mla_w_down, v_mla_q_norm_g, v_mla_kv_norm_g, v_mla_w_qb, v_mla_w_kvb, v_mla_w_o, v_mlp_w1, v_mlp_w2):
    w = dict(zip(WEIGHTS, (ln_mix_g, ln_mix_b, ln_ffn_g, ln_ffn_b, even_w_in, pool_w, pool_scale, lru_conv_w,
                           lru_conv_b, lru_w_a, lru_b_a, lru_w_x, lru_b_x, lru_lambda, even_w_out, mla_w_down,
                           mla_q_norm_g, mla_kv_norm_g, mla_w_qb, mla_w_kvb, mla_w_o, mlp_w1, mlp_w2)))
    m = dict(zip(WEIGHTS, (m_ln_mix_g, m_ln_mix_b, m_ln_ffn_g, m_ln_ffn_b, m_even_w_in, m_pool_w, m_pool_scale,
                           m_lru_conv_w, m_lru_conv_b, m_lru_w_a, m_lru_b_a, m_lru_w_x, m_lru_b_x, m_lru_lambda,
                           m_even_w_out, m_mla_w_down, m_mla_q_norm_g, m_mla_kv_norm_g, m_mla_w_qb, m_mla_w_kvb,
                           m_mla_w_o, m_mlp_w1, m_mlp_w2)))
    v = dict(zip(WEIGHTS, (v_ln_mix_g, v_ln_mix_b, v_ln_ffn_g, v_ln_ffn_b, v_even_w_in, v_pool_w, v_pool_scale,
                           v_lru_conv_w, v_lru_conv_b, v_lru_w_a, v_lru_b_a, v_lru_w_x, v_lru_b_x, v_lru_lambda,
                           v_even_w_out, v_mla_w_down, v_mla_q_norm_g, v_mla_kv_norm_g, v_mla_w_qb, v_mla_w_kvb,
                           v_mla_w_o, v_mlp_w1, v_mlp_w2)))
    return _step(x, positions, loss_target, w, m, v)
```

```python
import functools

import jax
import jax.numpy as jnp
from jax import lax
from jax.experimental import pallas as pl
from jax.experimental.pallas import tpu as pltpu

F32 = jnp.float32
BF16 = jnp.bfloat16
S = jax.ShapeDtypeStruct

D = 1024
DEPTH = 4
N_DEV = 8
CHUNK_SHIFT = 6
POOL_WINDOWS = (2, 4, 8, 16)
POOL_W = 512
LRU_W = 1024
LRU_HEADS = 8
HEAD = 128
LRU_C = 8.0
EVEN_IN = 2560
EVEN_MIX = 1536
MLA_HEADS = 8
NOPE = 128
ROPE = 64
VDIM = 128
Q_RANK = 384
KV_RANK = 256
ODD_IN = 704
D_FF = 4096
FF_BLK = D_FF // N_DEV
ROPE_THETA = 10000.0
ALPHA = (2 * DEPTH) ** 0.25
LN_EPS = 1e-5
RMS_EPS = 1e-6
ATT_SCALE = (NOPE + ROPE) ** -0.5
NEG = float(jnp.finfo(jnp.float32).min)
ADAM_LR = 0.001
ADAM_B1 = 0.9
ADAM_B2 = 0.999
ADAM_EPS = 1e-08
ADAM_WD = 0.01
ADAM_STEP = 10
V7X_VMEM_BYTES = 64 * 1024 * 1024
VMEM_LIMIT = V7X_VMEM_BYTES - 8 * 1024 * 1024
MESH = pl.DeviceIdType.MESH


def _cp(*sem):
    return pltpu.CompilerParams(dimension_semantics=sem if sem else None, vmem_limit_bytes=VMEM_LIMIT)


def _dot(a, b):
    return jnp.dot(a, b, preferred_element_type=F32)


def _dot_nt(a, b):
    return lax.dot_general(a, b, (((1,), (1,)), ((), ())), preferred_element_type=F32)


def _dot_tn(a, b):
    return lax.dot_general(a, b, (((0,), (0,)), ((), ())), preferred_element_type=F32)


def _full(shape):
    return pl.BlockSpec(shape, lambda *_: (0,) * len(shape))


def _mm(a, b, *, mode, grid, a_spec, b_spec, out_shape, out_spec, name, add=None, add_spec=None, add_scale=1.0,
        dep=None):
    dot = {"nn": _dot, "nt": _dot_nt, "tn": _dot_tn}[mode]

    def body(*refs):
        a_ref, b_ref, o_ref = refs[0], refs[1], refs[-1]
        acc = dot(a_ref[...].astype(BF16), b_ref[...].astype(BF16))
        if add is not None:
            acc = acc + add_scale * refs[2][...]
        o_ref[...] = acc.astype(o_ref.dtype)

    ops = [a, b] if add is None else [a, b, add]
    specs = [a_spec, b_spec] if add is None else [a_spec, b_spec, add_spec]
    if dep is not None:
        ops.append(dep)
        specs.append(pl.BlockSpec(memory_space=pl.ANY))
    return pl.pallas_call(body, grid=grid, in_specs=specs, out_specs=out_spec, out_shape=out_shape,
                          compiler_params=_cp(*(("parallel",) * len(grid))), name=name)(*ops)


def _ln_stats(z):
    mu = jnp.mean(z, axis=-1, keepdims=True)
    zc = z - mu
    var = jnp.mean(zc * zc, axis=-1, keepdims=True)
    rstd = lax.rsqrt(var + LN_EPS)
    return zc * rstd, rstd


def _row_tile(t):
    return min(512, t)


def _resid_ln(x, mix, g3, b3, l, name):
    t = x.shape[0]
    bm = _row_tile(t)

    def body(x_ref, m_ref, g_ref, b_ref, z_ref, y_ref, yb_ref):
        z = ALPHA * x_ref[...] + m_ref[...]
        xh, _ = _ln_stats(z)
        y = xh * g_ref[...] + b_ref[...]
        z_ref[...] = z
        y_ref[...] = y
        yb_ref[...] = y.astype(BF16)

    row = pl.BlockSpec((bm, D), lambda i: (i, 0))
    vec = pl.BlockSpec((None, 1, D), lambda i: (l, 0, 0))
    return pl.pallas_call(body, grid=(t // bm,), in_specs=[row, row, vec, vec], out_specs=[row, row, row],
                          out_shape=[S((t, D), F32), S((t, D), F32), S((t, D), BF16)],
                          compiler_params=_cp("parallel"), name=name)(x, mix, g3, b3)


def _proj_resid_ln(x, a, wmat, g3, b3, l, name):
    t, k = a.shape
    bm = _row_tile(t)

    def body(x_ref, a_ref, w_ref, g_ref, b_ref, z_ref, y_ref, yb_ref):
        z = ALPHA * x_ref[...] + _dot(a_ref[...], w_ref[...])
        xh, _ = _ln_stats(z)
        y = xh * g_ref[...] + b_ref[...]
        z_ref[...] = z
        y_ref[...] = y
        yb_ref[...] = y.astype(BF16)

    row = pl.BlockSpec((bm, D), lambda i: (i, 0))
    vec = pl.BlockSpec((None, 1, D), lambda i: (l, 0, 0))
    return pl.pallas_call(body, grid=(t // bm,),
                          in_specs=[row, pl.BlockSpec((bm, k), lambda i: (i, 0)), _full((k, D)), vec, vec],
                          out_specs=[row, row, row], out_shape=[S((t, D), F32), S((t, D), F32), S((t, D), BF16)],
                          compiler_params=_cp("parallel"), name=name)(x, a, wmat, g3, b3)


def _ln_bwd(d, z, g3, l, name, r=None, dep=None):
    t = z.shape[0]
    bm = _row_tile(t)

    def body(*refs):
        refs = list(refs)
        d_ref = refs.pop(0)
        dy = d_ref[...]
        if r is not None:
            dy = dy + ALPHA * refs.pop(0)[...]
        z_ref, g_ref = refs.pop(0), refs.pop(0)
        if dep is not None:
            refs.pop(0)
        dz_ref, dzb_ref, dg_ref, db_ref = refs
        xh, rstd = _ln_stats(z_ref[...])
        dyg = dy * g_ref[...]
        m1 = jnp.mean(dyg, axis=-1, keepdims=True)
        m2 = jnp.mean(dyg * xh, axis=-1, keepdims=True)
        dz = rstd * (dyg - m1 - xh * m2)
        dz_ref[...] = dz
        dzb_ref[...] = dz.astype(BF16)

        @pl.when(pl.program_id(0) == 0)
        def _():
            dg_ref[...] = jnp.zeros_like(dg_ref)
            db_ref[...] = jnp.zeros_like(db_ref)

        dg_ref[...] += jnp.sum(dy * xh, axis=0, keepdims=True)
        db_ref[...] += jnp.sum(dy, axis=0, keepdims=True)

    row = pl.BlockSpec((bm, D), lambda i: (i, 0))
    vec = pl.BlockSpec((None, 1, D), lambda i: (l, 0, 0))
    acc = pl.BlockSpec((1, D), lambda i: (0, 0))
    ops = [d, z, g3] if r is None else [d, r, z, g3]
    specs = [row, row, vec] if r is None else [row, row, row, vec]
    if dep is not None:
        ops.append(dep)
        specs.append(_full(dep.shape))
    return pl.pallas_call(body, grid=(t // bm,), in_specs=specs, out_specs=[row, row, acc, acc],
                          out_shape=[S((t, D), F32), S((t, D), BF16), S((1, D), F32), S((1, D), F32)],
                          compiler_params=_cp("arbitrary"), name=name)(*ops)


def _loss_grad(y, tgt):
    t = y.shape[0]
    bm = _row_tile(t)

    def body(y_ref, t_ref, dy_ref, loss_ref, acc_ref):
        i = pl.program_id(0)
        e = y_ref[...] - t_ref[...]
        dy_ref[...] = e * (1.0 / D)

        @pl.when(i == 0)
        def _():
            acc_ref[...] = jnp.zeros_like(acc_ref)

        acc_ref[...] += jnp.sum(e * e, axis=0, keepdims=True)

        @pl.when(i == pl.num_programs(0) - 1)
        def _():
            loss_ref[...] = jnp.full(loss_ref.shape, (0.5 / D) * jnp.sum(acc_ref[...]), F32)

    row = pl.BlockSpec((bm, D), lambda i: (i, 0))
    return pl.pallas_call(body, grid=(t // bm,), in_specs=[row, row],
                          out_specs=[row, pl.BlockSpec((1, 128), lambda i: (0, 0))],
                          out_shape=[S((t, D), F32), S((1, 128), F32)],
                          scratch_shapes=[pltpu.VMEM((1, D), F32)],
                          compiler_params=_cp("arbitrary"), name="loss_grad")(y, tgt)


def _mlp_row_tile(t):
    return min(1024, t)


def _mlp_fwd(y, yb, w1g, w2g, g3, b3, l, dep=None):
    t = yb.shape[0]
    bm = _mlp_row_tile(t)

    def body(*refs):
        y_ref, yb_ref, w1_ref, w2_ref, g_ref, b_ref = refs[:6]
        z_ref, o_ref, ob_ref, act_ref, acc_ref = refs[-5:]
        j = pl.program_id(1)
        h = jnp.maximum(_dot(yb_ref[...], w1_ref[...]), 0.0)
        act = (h * h).astype(BF16)
        act_ref[...] = act
        c = _dot(act, w2_ref[...])

        @pl.when(j == 0)
        def _():
            acc_ref[...] = c

        @pl.when(j > 0)
        def _():
            acc_ref[...] += c

        @pl.when(j == N_DEV - 1)
        def _():
            z = ALPHA * y_ref[...] + acc_ref[...]
            xh, _ = _ln_stats(z)
            out = xh * g_ref[...] + b_ref[...]
            z_ref[...] = z
            o_ref[...] = out
            ob_ref[...] = out.astype(BF16)

    row = pl.BlockSpec((bm, D), lambda i, j: (i, 0))
    vec = pl.BlockSpec((None, 1, D), lambda i, j: (l, 0, 0))
    deps = [] if dep is None else [dep]
    return pl.pallas_call(
        body, grid=(t // bm, N_DEV),
        in_specs=[row, row, pl.BlockSpec((None, D, FF_BLK), lambda i, j: (j, 0, 0)),
                  pl.BlockSpec((None, FF_BLK, D), lambda i, j: (j, 0, 0)), vec, vec] + [ANY] * len(deps),
        out_specs=[row, row, row, pl.BlockSpec((bm, FF_BLK), lambda i, j: (i, j))],
        out_shape=[S((t, D), F32), S((t, D), F32), S((t, D), BF16), S((t, D_FF), BF16)],
        scratch_shapes=[pltpu.VMEM((bm, D), F32)],
        compiler_params=_cp("parallel", "arbitrary"), name="mlp_fwd")(y, yb, w1g, w2g, g3, b3, *deps)


def _mlp_bwd_dh(act, dzb, w1g, w2g):
    t = act.shape[0]
    bm = _mlp_row_tile(t)

    def body(a_ref, dz_ref, w1_ref, w2_ref, dh_ref, acc_ref):
        j = pl.program_id(1)
        r = jnp.sqrt(a_ref[...].astype(F32))
        da = _dot_nt(dz_ref[...], w2_ref[...])
        dh = (da * (2.0 * r)).astype(BF16)
        dh_ref[...] = dh
        c = _dot_nt(dh, w1_ref[...])

        @pl.when(j == 0)
        def _():
            acc_ref[...] = c

        @pl.when(j > 0)
        def _():
            acc_ref[...] += c

    row = pl.BlockSpec((bm, D), lambda i, j: (i, 0))
    hid = pl.BlockSpec((bm, FF_BLK), lambda i, j: (i, j))
    return pl.pallas_call(
        body, grid=(t // bm, N_DEV),
        in_specs=[hid, row,
                  pl.BlockSpec((None, D, FF_BLK), lambda i, j: (j, 0, 0)),
                  pl.BlockSpec((None, FF_BLK, D), lambda i, j: (j, 0, 0))],
        out_specs=[hid, row],
        out_shape=[S((t, D_FF), BF16), S((t, D), F32)],
        compiler_params=_cp("parallel", "arbitrary"), name="mlp_bwd_dh")(act, dzb, w1g, w2g)


def _shift_dn(x, k, rows, fill=0.0):
    return jnp.where(rows >= k, pltpu.roll(x, k, 0), fill)


def _shift_up(x, k, rows, fill=0.0):
    t = x.shape[0]
    return jnp.where(rows < t - k, pltpu.roll(x, t - k, 0), fill)


def _scan_rows(a, b, shift):
    rows = lax.broadcasted_iota(jnp.int32, a.shape, 0)
    k = 1
    t = a.shape[0]
    while k < t:
        b = a * shift(b, k, rows) + b
        if 2 * k < t:
            a = a * shift(a, k, rows, 1.0)
        k *= 2
    return b


def _scan_dn(a, b):
    return _scan_rows(a, b, _shift_dn)


def _scan_up(a, b):
    return _scan_rows(a, b, _shift_up)


def _window_sum_dn(x, w, rows):
    k = 1
    while k < w:
        x = x + _shift_dn(x, k, rows)
        k *= 2
    return x


def _window_sum_up(x, w, rows):
    k = 1
    while k < w:
        x = x + _shift_up(x, k, rows)
        k *= 2
    return x


def _pool_diff(u, w, rows):
    inv_count = 1.0 / jnp.minimum(rows + 1, w).astype(F32)
    return _window_sum_dn(u, w, rows) * inv_count - u, inv_count


def _pool_fwd(proj, pool_w, pool_scale3, j):
    t = proj.shape[0]

    def body(u_ref, w_ref, s_ref, y_ref):
        rows = lax.broadcasted_iota(jnp.int32, (t, HEAD), 0)
        for g, w in enumerate(POOL_WINDOWS):
            cols = slice(g * HEAD, (g + 1) * HEAD)
            d, _ = _pool_diff(u_ref[:, cols], w, rows)
            y = _dot(d.astype(BF16), w_ref[g].astype(BF16)) * s_ref[:, cols]
            y_ref[:, cols] = y.astype(BF16)

    return pl.pallas_call(
        body, grid=(1,),
        in_specs=[pl.BlockSpec((t, POOL_W), lambda i: (0, 0)),
                  pl.BlockSpec((None, 4, HEAD, HEAD), lambda i: (j, 0, 0, 0)),
                  pl.BlockSpec((None, 1, POOL_W), lambda i: (j, 0, 0))],
        out_specs=pl.BlockSpec((t, POOL_W), lambda i: (0, 0)),
        out_shape=S((t, POOL_W), BF16), compiler_params=_cp("arbitrary"), name="pool_fwd")(proj, pool_w, pool_scale3)


def _pool_bwd(proj, dycat, pool_w, pool_scale3, j):
    t = proj.shape[0]

    def body(u_ref, dy_ref, w_ref, s_ref, du_ref, dw_ref, ds_ref):
        rows = lax.broadcasted_iota(jnp.int32, (t, HEAD), 0)
        for g, w in enumerate(POOL_WINDOWS):
            cols = slice(g * HEAD, (g + 1) * HEAD)
            d, inv_count = _pool_diff(u_ref[:, cols], w, rows)
            db = d.astype(BF16)
            wg = w_ref[g].astype(BF16)
            dy = dy_ref[:, cols]
            ds_ref[:, cols] = jnp.sum(dy * _dot(db, wg), axis=0, keepdims=True)
            dzz = (dy * s_ref[:, cols]).astype(BF16)
            dw_ref[g] = _dot_tn(db, dzz)
            dd = _dot_nt(dzz, wg)
            du_ref[:, cols] = (_window_sum_up(dd * inv_count, w, rows) - dd).astype(BF16)

    return pl.pallas_call(
        body, grid=(1,),
        in_specs=[pl.BlockSpec((t, POOL_W), lambda i: (0, 0)),
                  pl.BlockSpec((t, POOL_W), lambda i: (0, 0)),
                  pl.BlockSpec((None, 4, HEAD, HEAD), lambda i: (j, 0, 0, 0)),
                  pl.BlockSpec((None, 1, POOL_W), lambda i: (j, 0, 0))],
        out_specs=[pl.BlockSpec((t, POOL_W), lambda i: (0, 0)), _full((4, HEAD, HEAD)), _full((1, POOL_W))],
        out_shape=[S((t, POOL_W), BF16), S((4, HEAD, HEAD), F32), S((1, POOL_W), F32)],
        compiler_params=_cp("arbitrary"), name="pool_bwd")(proj, dycat, pool_w, pool_scale3)


GELU_C = 0.7978845608028654
GELU_K = 0.044715


def _gelu(x):
    th = jnp.tanh(GELU_C * (x + GELU_K * x * x * x))
    return 0.5 * x * (1.0 + th), th


def _lru_forward(u, gate, cw, cb, wa, ba, wx, bx, lam, rows):
    v = cw[3:4] * u + cw[2:3] * _shift_dn(u, 1, rows) + cw[1:2] * _shift_dn(u, 2, rows) \
        + cw[0:1] * _shift_dn(u, 3, rows) + cb
    vb = v.astype(BF16)
    r = jax.nn.sigmoid(_dot(vb, wa) + ba)
    i = jax.nn.sigmoid(_dot(vb, wx) + bx)
    sp = jnp.maximum(-lam, 0.0) + jnp.log1p(jnp.exp(-jnp.abs(lam)))
    log_a = (-LRU_C) * r * sp
    a = jnp.exp(log_a)
    one_m_a2 = -jnp.tanh(log_a) * (a * a + 1.0)
    mult = jnp.sqrt(one_m_a2)
    h = _scan_dn(a, mult * (i * v))
    gl, th = _gelu(gate)
    return dict(v=v, vb=vb, r=r, i=i, sp=sp, a=a, mult=mult, h=h, gl=gl, th=th)


def _lru_specs(t, j, col0_u, col0_g):
    blk = lambda c0: pl.BlockSpec((t, HEAD), lambda h: (0, c0 + h))
    vec = pl.BlockSpec((None, 1, HEAD), lambda h: (j, 0, h))
    return [blk(col0_u), blk(col0_g),
            pl.BlockSpec((None, 4, HEAD), lambda h: (j, 0, h)), vec,
            pl.BlockSpec((None, None, HEAD, HEAD), lambda h: (j, h, 0, 0)), vec,
            pl.BlockSpec((None, None, HEAD, HEAD), lambda h: (j, h, 0, 0)), vec, vec]


def _lru_fwd(proj, p, j):
    t = proj.shape[0]

    def body(u_ref, g_ref, cw_ref, cb_ref, wa_ref, ba_ref, wx_ref, bx_ref, lam_ref, y_ref):
        rows = lax.broadcasted_iota(jnp.int32, (t, HEAD), 0)
        f = _lru_forward(u_ref[...], g_ref[...], cw_ref[...], cb_ref[...], wa_ref[...].astype(BF16), ba_ref[...],
                         wx_ref[...].astype(BF16), bx_ref[...], lam_ref[...], rows)
        y_ref[...] = (f["h"] * f["gl"]).astype(BF16)

    return pl.pallas_call(
        body, grid=(LRU_HEADS,), in_specs=_lru_specs(t, j, POOL_W // HEAD, (POOL_W + LRU_W) // HEAD),
        out_specs=pl.BlockSpec((t, HEAD), lambda h: (0, h)), out_shape=S((t, LRU_W), BF16),
        compiler_params=_cp("parallel"), name="lru_fwd")(
            proj, proj, p["conv_w"], p["conv_b"], p["w_a"], p["b_a"], p["w_x"], p["b_x"], p["lam"])


def _lru_bwd(proj, dycat, p, j):
    t = proj.shape[0]

    def body(u_ref, g_ref, cw_ref, cb_ref, wa_ref, ba_ref, wx_ref, bx_ref, lam_ref, dy_ref,
             du_ref, dgate_ref, dcw_ref, dcb_ref, dwa_ref, dba_ref, dwx_ref, dbx_ref, dlam_ref):
        rows = lax.broadcasted_iota(jnp.int32, (t, HEAD), 0)
        u = u_ref[...]
        gate = g_ref[...]
        cw = cw_ref[...]
        wa = wa_ref[...].astype(BF16)
        wx = wx_ref[...].astype(BF16)
        lam = lam_ref[...]
        f = _lru_forward(u, gate, cw, cb_ref[...], wa, ba_ref[...], wx, bx_ref[...], lam, rows)
        v, r, i, a, mult, h, th = f["v"], f["r"], f["i"], f["a"], f["mult"], f["h"], f["th"]
        dy = dy_ref[...]
        dgl = 0.5 * (1.0 + th) + 0.5 * gate * (1.0 - th * th) * GELU_C * (1.0 + 3.0 * GELU_K * gate * gate)
        dgate_ref[...] = (dy * h * dgl).astype(BF16)
        g = _scan_up(_shift_up(a, 1, rows), dy * f["gl"])
        da = g * _shift_dn(h, 1, rows)
        iv = i * v
        dmult = g * iv
        di = g * mult * v
        dv = g * mult * i
        dlog_a = da * a - dmult * (a * a) / mult
        dr = dlog_a * (-LRU_C) * f["sp"]
        dsp = jnp.sum(dlog_a * (-LRU_C) * r, axis=0, keepdims=True)
        dlam_ref[...] = -dsp * jax.nn.sigmoid(-lam)
        dpa = dr * r * (1.0 - r)
        dpx = di * i * (1.0 - i)
        dpab = dpa.astype(BF16)
        dpxb = dpx.astype(BF16)
        dwa_ref[...] = _dot_tn(f["vb"], dpab)
        dwx_ref[...] = _dot_tn(f["vb"], dpxb)
        dba_ref[...] = jnp.sum(dpa, axis=0, keepdims=True)
        dbx_ref[...] = jnp.sum(dpx, axis=0, keepdims=True)
        dv = dv + _dot_nt(dpab, wa) + _dot_nt(dpxb, wx)
        dcb_ref[...] = jnp.sum(dv, axis=0, keepdims=True)
        du = cw[3:4] * dv
        dcw_ref[3:4, :] = jnp.sum(dv * u, axis=0, keepdims=True)
        for k in (1, 2, 3):
            du = du + cw[3 - k:4 - k] * _shift_up(dv, k, rows)
            dcw_ref[3 - k:4 - k, :] = jnp.sum(dv * _shift_dn(u, k, rows), axis=0, keepdims=True)
        du_ref[...] = du.astype(BF16)

    blk = pl.BlockSpec((t, HEAD), lambda h: (0, h))
    vec = pl.BlockSpec((1, HEAD), lambda h: (0, h))
    mat = pl.BlockSpec((None, HEAD, HEAD), lambda h: (h, 0, 0))
    return pl.pallas_call(
        body, grid=(LRU_HEADS,),
        in_specs=_lru_specs(t, j, POOL_W // HEAD, (POOL_W + LRU_W) // HEAD)
        + [pl.BlockSpec((t, HEAD), lambda h: (0, POOL_W // HEAD + h))],
        out_specs=[blk, blk, pl.BlockSpec((4, HEAD), lambda h: (0, h)), vec, mat, vec, mat, vec, vec],
        out_shape=[S((t, LRU_W), BF16), S((t, LRU_W), BF16), S((4, LRU_W), F32), S((1, LRU_W), F32),
                   S((LRU_HEADS, HEAD, HEAD), F32), S((1, LRU_W), F32),
                   S((LRU_HEADS, HEAD, HEAD), F32), S((1, LRU_W), F32), S((1, LRU_W), F32)],
        compiler_params=_cp("parallel"), name="lru_bwd")(
            proj, proj, p["conv_w"], p["conv_b"], p["w_a"], p["b_a"], p["w_x"], p["b_x"], p["lam"], dycat)


def _rope(x, c, s):
    x1 = x[:, :ROPE // 2]
    x2 = x[:, ROPE // 2:]
    return jnp.concatenate([x1 * c - x2 * s, x1 * s + x2 * c], axis=-1)


def _rope_t(d, c, s):
    d1 = d[:, :ROPE // 2]
    d2 = d[:, ROPE // 2:]
    return jnp.concatenate([d1 * c + d2 * s, d2 * c - d1 * s], axis=-1)


def _rope_tables(pos2, inv_freq):
    t = pos2.shape[0]

    def body(p_ref, f_ref, c_ref, s_ref):
        ang = p_ref[...].astype(F32) * f_ref[...]
        c_ref[...] = jnp.cos(ang)
        s_ref[...] = jnp.sin(ang)

    return pl.pallas_call(body, out_shape=[S((t, ROPE // 2), F32), S((t, ROPE // 2), F32)],
                          name="rope_tables")(pos2, inv_freq)


def _down_norm(xb, wdown_g, gq3, gkv3, cos, sin, j):
    t = xb.shape[0]
    bm = _row_tile(t)

    def body(x_ref, w_ref, gq_ref, gkv_ref, c_ref, s_ref, down_ref, cq_ref, ckv_ref, kpe_ref):
        w = w_ref[...].reshape(D, ODD_IN)
        down = _dot(x_ref[...], w)
        down_ref[...] = down
        q = down[:, :Q_RANK]
        cq_ref[...] = (q * lax.rsqrt(jnp.mean(q * q, axis=-1, keepdims=True) + RMS_EPS) * gq_ref[...]).astype(BF16)
        kv = down[:, Q_RANK:Q_RANK + KV_RANK]
        ckv_ref[...] = (kv * lax.rsqrt(jnp.mean(kv * kv, axis=-1, keepdims=True) + RMS_EPS)
                        * gkv_ref[...]).astype(BF16)
        kpe_ref[...] = _rope(down[:, Q_RANK + KV_RANK:], c_ref[...], s_ref[...])

    row = lambda n: pl.BlockSpec((bm, n), lambda i: (i, 0))
    return pl.pallas_call(
        body, grid=(t // bm,),
        in_specs=[row(D), _full((N_DEV, D // N_DEV, ODD_IN)),
                  pl.BlockSpec((None, 1, Q_RANK), lambda i: (j, 0, 0)),
                  pl.BlockSpec((None, 1, KV_RANK), lambda i: (j, 0, 0)), row(ROPE // 2), row(ROPE // 2)],
        out_specs=[row(ODD_IN), row(Q_RANK), row(KV_RANK), row(ROPE)],
        out_shape=[S((t, ODD_IN), F32), S((t, Q_RANK), BF16), S((t, KV_RANK), BF16), S((t, ROPE), F32)],
        compiler_params=_cp("parallel"), name="down_norm")(xb, wdown_g, gq3, gkv3, cos, sin)


def _q_tile(t, widest):
    return min(widest, t // 2)


def _attn_probs(q, k, qs):
    s = _dot_nt(q, k) * ATT_SCALE
    tq = q.shape[0]
    rows = lax.broadcasted_iota(jnp.int32, (tq, tq), 0)
    cols = lax.broadcasted_iota(jnp.int32, (tq, tq), 1)
    last = jnp.where(jnp.right_shift(cols, CHUNK_SHIFT) <= jnp.right_shift(rows, CHUNK_SHIFT), s[:, qs:], NEG)
    s = last if qs == 0 else jnp.concatenate([s[:, :qs], last], axis=1)
    e = jnp.exp(s - jnp.max(s, axis=-1, keepdims=True))
    return e / jnp.sum(e, axis=-1, keepdims=True)


def _head_qkv(cq, ckv, kpe, c, s, wq_ref, wkv_ref):
    q = jnp.concatenate([_dot(cq, wq_ref[:, :NOPE]), _rope(_dot(cq, wq_ref[:, NOPE:]), c, s)], axis=1).astype(BF16)
    k = jnp.concatenate([_dot(ckv, wkv_ref[:, :NOPE]), kpe], axis=1).astype(BF16)
    vv = _dot(ckv, wkv_ref[:, NOPE:]).astype(BF16)
    return q, k, vv


def _attn_in_specs(t):
    return [_full((t, Q_RANK)), _full((t, KV_RANK)), _full((t, ROPE)), _full((t, ROPE // 2)), _full((t, ROPE // 2)),
            pl.BlockSpec((None, Q_RANK, NOPE + ROPE), lambda h: (h, 0, 0)),
            pl.BlockSpec((None, KV_RANK, NOPE + VDIM), lambda h: (h, 0, 0)),
            pl.BlockSpec((None, VDIM, D), lambda h: (h, 0, 0))]


def _attn_fwd(cq, ckv, kpe, cos, sin, wqb_g, wkvb_g, wo_g):
    t = cq.shape[0]
    tq = _q_tile(t, 256)

    def body(cq_ref, ckv_ref, kpe_ref, c_ref, s_ref, wq_ref, wkv_ref, wo_ref, o_ref, mix_ref):
        q, k, vv = _head_qkv(cq_ref[...], ckv_ref[...], kpe_ref[...], c_ref[...], s_ref[...], wq_ref, wkv_ref)
        for qs in range(0, t, tq):
            ke = qs + tq
            p = _attn_probs(q[qs:ke], k[:ke], qs)
            o_ref[qs:ke, :] = _dot(p.astype(BF16), vv[:ke]).astype(BF16)
        c = _dot(o_ref[...], wo_ref[...])

        @pl.when(pl.program_id(0) == 0)
        def _():
            mix_ref[...] = c

        @pl.when(pl.program_id(0) > 0)
        def _():
            mix_ref[...] += c

    return pl.pallas_call(
        body, grid=(MLA_HEADS,), in_specs=_attn_in_specs(t),
        out_specs=[pl.BlockSpec((None, t, VDIM), lambda h: (h, 0, 0)), _full((t, D))],
        out_shape=[S((MLA_HEADS, t, VDIM), BF16), S((t, D), F32)],
        compiler_params=_cp("arbitrary"), name="attn_fwd")(cq, ckv, kpe, cos, sin, wqb_g, wkvb_g, wo_g)


def _attn_bwd(cq, ckv, kpe, cos, sin, wqb_g, wkvb_g, wo_g, o, dzb):
    t = cq.shape[0]
    tq = _q_tile(t, 512)

    def body(cq_ref, ckv_ref, kpe_ref, c_ref, s_ref, wq_ref, wkv_ref, wo_ref, o_ref, dz_ref,
             dwo_ref, dwq_ref, dwkv_ref, dcq_ref, dckv_ref, dkpe_ref, dkt_s, dvt_s, dq_s):
        cqv = cq_ref[...]
        ckvv = ckv_ref[...]
        c = c_ref[...]
        s = s_ref[...]
        q, k, vv = _head_qkv(cqv, ckvv, kpe_ref[...], c, s, wq_ref, wkv_ref)
        dzv = dz_ref[...]
        dwo_ref[...] = _dot_tn(o_ref[...], dzv).astype(BF16)
        do = _dot_nt(dzv, wo_ref[...]).astype(BF16)
        dkt_s[...] = jnp.zeros_like(dkt_s)
        dvt_s[...] = jnp.zeros_like(dvt_s)
        for qs in range(0, t, tq):
            ke = qs + tq
            p = _attn_probs(q[qs:ke], k[:ke], qs)
            dp = _dot_nt(do[qs:ke], vv[:ke])
            ds = (p * (dp - jnp.sum(p * dp, axis=-1, keepdims=True)) * ATT_SCALE).astype(BF16)
            dq_s[qs:ke, :] = _dot(ds, k[:ke])
            dkt_s[0:NOPE + ROPE, 0:ke] += _dot_tn(q[qs:ke], ds)
            dvt_s[:, 0:ke] += _dot_tn(do[qs:ke], p.astype(BF16))
        dk = dkt_s[...].T
        dqn = dq_s[:, :NOPE].astype(BF16)
        dqp = _rope_t(dq_s[:, NOPE:], c, s).astype(BF16)
        dkn = dk[:, :NOPE].astype(BF16)
        dkp = dk[:, NOPE:NOPE + ROPE]
        dvv = dvt_s[...].T.astype(BF16)
        dwq_ref[:, :NOPE] = _dot_tn(cqv, dqn).astype(BF16)
        dwq_ref[:, NOPE:] = _dot_tn(cqv, dqp).astype(BF16)
        dwkv_ref[:, :NOPE] = _dot_tn(ckvv, dkn).astype(BF16)
        dwkv_ref[:, NOPE:] = _dot_tn(ckvv, dvv).astype(BF16)
        dcq = _dot_nt(dqn, wq_ref[:, :NOPE]) + _dot_nt(dqp, wq_ref[:, NOPE:])
        dckv = _dot_nt(dkn, wkv_ref[:, :NOPE]) + _dot_nt(dvv, wkv_ref[:, NOPE:])

        @pl.when(pl.program_id(0) == 0)
        def _():
            dcq_ref[...] = dcq
            dckv_ref[...] = dckv
            dkpe_ref[...] = dkp

        @pl.when(pl.program_id(0) > 0)
        def _():
            dcq_ref[...] += dcq
            dckv_ref[...] += dckv
            dkpe_ref[...] += dkp

    per_head = lambda a, b: pl.BlockSpec((None, a, b), lambda h: (h, 0, 0))
    return pl.pallas_call(
        body, grid=(MLA_HEADS,),
        in_specs=_attn_in_specs(t) + [per_head(t, VDIM), _full((t, D))],
        out_specs=[per_head(VDIM, D), per_head(Q_RANK, NOPE + ROPE), per_head(KV_RANK, NOPE + VDIM),
                   _full((t, Q_RANK)), _full((t, KV_RANK)), _full((t, ROPE))],
        out_shape=[S((MLA_HEADS, VDIM, D), BF16), S((MLA_HEADS, Q_RANK, NOPE + ROPE), BF16),
                   S((MLA_HEADS, KV_RANK, NOPE + VDIM), BF16),
                   S((t, Q_RANK), F32), S((t, KV_RANK), F32), S((t, ROPE), F32)],
        scratch_shapes=[pltpu.VMEM((2 * NOPE, t), F32), pltpu.VMEM((VDIM, t), F32),
                        pltpu.VMEM((t, NOPE + ROPE), F32)],
        compiler_params=_cp("arbitrary"), name="attn_bwd")(cq, ckv, kpe, cos, sin, wqb_g, wkvb_g, wo_g, o, dzb)


def _rms_bwd(down, dcq, dckv, dkpe, cos, sin, gq3, gkv3, j):
    t = down.shape[0]
    bm = _row_tile(t)

    def body(down_ref, dcq_ref, dckv_ref, dkpe_ref, c_ref, s_ref, gq_ref, gkv_ref, dd_ref, dgq_ref, dgkv_ref):
        @pl.when(pl.program_id(0) == 0)
        def _():
            dgq_ref[...] = jnp.zeros_like(dgq_ref)
            dgkv_ref[...] = jnp.zeros_like(dgkv_ref)

        def rms_b(x, dy, g):
            rstd = lax.rsqrt(jnp.mean(x * x, axis=-1, keepdims=True) + RMS_EPS)
            xh = x * rstd
            dyg = dy * g
            return rstd * (dyg - xh * jnp.mean(dyg * xh, axis=-1, keepdims=True)), jnp.sum(dy * xh, axis=0, keepdims=True)

        dq, dgq = rms_b(down_ref[:, :Q_RANK], dcq_ref[...], gq_ref[...])
        dkv, dgkv = rms_b(down_ref[:, Q_RANK:Q_RANK + KV_RANK], dckv_ref[...], gkv_ref[...])
        dgq_ref[...] += dgq
        dgkv_ref[...] += dgkv
        dd_ref[:, :Q_RANK] = dq.astype(BF16)
        dd_ref[:, Q_RANK:Q_RANK + KV_RANK] = dkv.astype(BF16)
        dd_ref[:, Q_RANK + KV_RANK:] = _rope_t(dkpe_ref[...], c_ref[...], s_ref[...]).astype(BF16)

    row = lambda n: pl.BlockSpec((bm, n), lambda i: (i, 0))
    return pl.pallas_call(
        body, grid=(t // bm,),
        in_specs=[row(ODD_IN), row(Q_RANK), row(KV_RANK), row(ROPE), row(ROPE // 2), row(ROPE // 2),
                  pl.BlockSpec((None, 1, Q_RANK), lambda i: (j, 0, 0)),
                  pl.BlockSpec((None, 1, KV_RANK), lambda i: (j, 0, 0))],
        out_specs=[row(ODD_IN), _full((1, Q_RANK)), _full((1, KV_RANK))],
        out_shape=[S((t, ODD_IN), BF16), S((1, Q_RANK), F32), S((1, KV_RANK), F32)],
        compiler_params=_cp("arbitrary"), name="rms_bwd")(down, dcq, dckv, dkpe, cos, sin, gq3, gkv3)


def _col_blocks(t, n, bn):
    return pl.BlockSpec((t, bn), lambda i: (0, i))


def _row_blocks(n, bm):
    return pl.BlockSpec((bm, n), lambda i: (i, 0))


def _local_step(x, pos2, tgt, small, weights_of, grads_done, start_dep=None, prefetch=None):
    t = x.shape[0]
    bm = _row_tile(t)
    inv_freq = (ROPE_THETA ** (-jnp.arange(0, ROPE, 2, dtype=F32) / ROPE)).reshape(1, ROPE // 2)
    cos, sin = _rope_tables(pos2, inv_freq)
    lru_p = {k: small[k] for k in ("conv_w", "conv_b", "w_a", "b_a", "w_x", "b_x", "lam")}

    saved = []
    y, yb = x, x.astype(BF16)
    for l in range(DEPTH):
        j = l // 2
        big = weights_of(l, 0, y)
        sv = dict(xb=yb, big=big)
        if l % 2 == 0:
            proj = _mm(yb, big["win2d"], mode="nn", grid=(EVEN_IN // 512,), a_spec=_full((t, D)),
                       b_spec=_col_blocks(D, EVEN_IN, 512), out_shape=S((t, EVEN_IN), F32),
                       out_spec=_col_blocks(t, EVEN_IN, 512), name="even_proj", dep=start_dep if l == 0 else None)
            ycat = jnp.concatenate([_pool_fwd(proj, small["pool_w"], small["pool_scale"], j),
                                    _lru_fwd(proj, lru_p, j)], axis=1)
            big.update(weights_of(l, 1, ycat))
            z1, y1, y1b = _proj_resid_ln(y, ycat, big["wout2d"], small["ln_mix_g"], small["ln_mix_b"], l, "even_out")
            sv.update(proj=proj, ycat=ycat)
        else:
            down, cq, ckv, kpe = _down_norm(yb, big["wdown"], small["gq"], small["gkv"], cos, sin, j)
            o, mix = _attn_fwd(cq, ckv, kpe, cos, sin, big["wqb"], big["wkvb"], big["wo"])
            z1, y1, y1b = _resid_ln(y, mix, small["ln_mix_g"], small["ln_mix_b"], l, "resid_ln")
            sv.update(down=down, cq=cq, ckv=ckv, kpe=kpe, o=o)
        fetched = prefetch(l + 1, y1) if prefetch is not None and l + 1 < DEPTH else None
        z2, y, yb, act = _mlp_fwd(y1, y1b, big["w1"], big["w2"], small["ln_ffn_g"], small["ln_ffn_b"], l,
                                  dep=fetched)
        sv.update(z1=z1, y1b=y1b, z2=z2, act=act)
        saved.append(sv)

    dy, loss_tile = _loss_grad(y, tgt)

    g = {k: [None] * n for k, n in (("ln_mix_g", 4), ("ln_mix_b", 4), ("ln_ffn_g", 4), ("ln_ffn_b", 4),
                                    ("pool_w", 2), ("pool_scale", 2), ("conv_w", 2), ("conv_b", 2),
                                    ("w_a", 2), ("b_a", 2), ("w_x", 2), ("b_x", 2), ("lam", 2),
                                    ("gq", 2), ("gkv", 2))}
    dep = None
    for l in reversed(range(DEPTH)):
        j = l // 2
        sv = saved[l]
        big = sv["big"]
        dz2, dz2b, g["ln_ffn_g"][l], g["ln_ffn_b"][l] = _ln_bwd(dy, sv["z2"], small["ln_ffn_g"], l, "ln_bwd", dep=dep)
        act = sv["act"]
        dh, dff = _mlp_bwd_dh(act, dz2b, big["w1"], big["w2"])
        dw1 = _mm(sv["y1b"], dh, mode="tn", grid=(N_DEV,), a_spec=_full((t, D)),
                  b_spec=_col_blocks(t, D_FF, FF_BLK), out_shape=S((N_DEV, D, FF_BLK), BF16),
                  out_spec=pl.BlockSpec((None, D, FF_BLK), lambda i: (i, 0, 0)), name="mlp_dw1")
        dw2 = _mm(act, dz2b, mode="tn", grid=(N_DEV,), a_spec=_col_blocks(t, D_FF, FF_BLK),
                  b_spec=_full((t, D)), out_shape=S((N_DEV, FF_BLK, D), BF16),
                  out_spec=pl.BlockSpec((None, FF_BLK, D), lambda i: (i, 0, 0)), name="mlp_dw2")
        dep = grads_done(l, dict(w1=dw1, w2=dw2))
        dz1, dz1b, g["ln_mix_g"][l], g["ln_mix_b"][l] = _ln_bwd(dff, sv["z1"], small["ln_mix_g"], l, "ln_bwd_res",
                                                                 r=dz2, dep=dep)
        if l % 2 == 0:
            wout = big["wout2d"]
            dycat = _mm(dz1b, wout, mode="nt", grid=(EVEN_MIX // 512,), a_spec=_full((t, D)),
                        b_spec=_row_blocks(D, 512), out_shape=S((t, EVEN_MIX), F32),
                        out_spec=_col_blocks(t, EVEN_MIX, 512), name="even_dycat")
            dwout = _mm(sv["ycat"], dz1b, mode="tn", grid=(EVEN_MIX // 512,), a_spec=_col_blocks(t, EVEN_MIX, 512),
                        b_spec=_full((t, D)), out_shape=S((EVEN_MIX, D), BF16), out_spec=_row_blocks(D, 512),
                        name="even_dwout")
            du_pool, g["pool_w"][j], g["pool_scale"][j] = _pool_bwd(sv["proj"], dycat, small["pool_w"],
                                                                   small["pool_scale"], j)
            (du_lru, du_gate, g["conv_w"][j], g["conv_b"][j], g["w_a"][j], g["b_a"][j], g["w_x"][j], g["b_x"][j],
             g["lam"][j]) = _lru_bwd(sv["proj"], dycat, lru_p, j)
            dproj = jnp.concatenate([du_pool, du_lru, du_gate], axis=1)
            dwin = _mm(sv["xb"], dproj, mode="tn", grid=(EVEN_IN // 512,), a_spec=_full((t, D)),
                       b_spec=_col_blocks(t, EVEN_IN, 512), out_shape=S((D, EVEN_IN), BF16),
                       out_spec=_col_blocks(D, EVEN_IN, 512), name="even_dwin")
            dep = grads_done(l, dict(win=dwin.reshape(D, N_DEV, EVEN_IN // N_DEV).transpose(1, 0, 2),
                                     wout=dwout.reshape(N_DEV, EVEN_MIX // N_DEV, D)))
            dy = _mm(dproj, big["win2d"], mode="nt", grid=(t // bm,), a_spec=_row_blocks(EVEN_IN, bm),
                     b_spec=_full((D, EVEN_IN)), out_shape=S((t, D), F32), out_spec=_row_blocks(D, bm),
                     add=dz1, add_spec=_row_blocks(D, bm), add_scale=ALPHA, name="even_dx")
        else:
            dwo, dwqb, dwkvb, dcq, dckv, dkpe = _attn_bwd(
                sv["cq"], sv["ckv"], sv["kpe"], cos, sin, big["wqb"], big["wkvb"], big["wo"], sv["o"], dz1b)
            ddown, g["gq"][j], g["gkv"][j] = _rms_bwd(sv["down"], dcq, dckv, dkpe, cos, sin, small["gq"],
                                                     small["gkv"], j)
            dwdown = _mm(sv["xb"], ddown, mode="tn", grid=(N_DEV,), a_spec=_col_blocks(t, D, D // N_DEV),
                         b_spec=_full((t, ODD_IN)), out_shape=S((N_DEV, D // N_DEV, ODD_IN), BF16),
                         out_spec=pl.BlockSpec((None, D // N_DEV, ODD_IN), lambda i: (i, 0, 0)),
                         name="odd_dwdown")
            dep = grads_done(l, dict(wdown=dwdown, wqb=dwqb, wkvb=dwkvb, wo=dwo))
            dy = _mm(ddown, big["wdown2d"], mode="nt", grid=(t // bm,), a_spec=_row_blocks(ODD_IN, bm),
                     b_spec=_full((D, ODD_IN)), out_shape=S((t, D), F32), out_spec=_row_blocks(D, bm),
                     add=dz1, add_spec=_row_blocks(D, bm), add_scale=ALPHA, name="odd_dx")
    return loss_tile[0, 0], dy, g


def _mesh_place():
    x, y, c = lax.axis_index("x"), lax.axis_index("y"), lax.axis_index("c")
    return x, y, c


def _peer(place, k):
    x, y, c = place
    return (1 - x if k & 4 else x, 1 - y if k & 2 else y, 1 - c if k & 1 else c)


def _index(place):
    x, y, c = place
    return 4 * x + 2 * y + c


ANY = pl.BlockSpec(memory_space=pl.ANY)


def _all_gather_big(zones):
    n = len(zones)

    def body(*refs):
        outs = refs[n:2 * n]
        send, recv = refs[2 * n:]
        x, y, c = _mesh_place()
        me, sibling = (x, y, c), (x, y, 1 - c)
        chips = [(1 - x, y), (x, 1 - y), (1 - x, 1 - y)]

        def copy(w, k, block, to):
            blk = outs[w].at[_index(block)]
            return pltpu.make_async_remote_copy(src_ref=blk, dst_ref=blk, send_sem=send.at[w, k], recv_sem=recv.at[w, k],
                                                device_id=to, device_id_type=MESH)

        first = []
        for w in range(n):
            first.append(copy(w, 0, me, sibling))
            first += [copy(w, 1 + j, me, (*chip, c)) for j, chip in enumerate(chips)]
        for cp in first:
            cp.start()
        passed = []
        for w in range(n):
            for j, chip in enumerate(chips):
                copy(w, 1 + j, (*chip, c), me).wait_recv()
                cp = copy(w, 4 + j, (*chip, c), sibling)
                cp.start()
                passed.append(cp)
        for w in range(n):
            copy(w, 0, sibling, me).wait_recv()
            for j, chip in enumerate(chips):
                copy(w, 4 + j, (*chip, 1 - c), me).wait_recv()
        for cp in first + passed:
            cp.wait_send()

    return pl.pallas_call(
        body, in_specs=[ANY] * n, out_specs=[ANY] * n, out_shape=[S(z.shape, z.dtype) for z in zones],
        input_output_aliases={i: i for i in range(n)},
        scratch_shapes=[pltpu.SemaphoreType.DMA((n, N_DEV - 1)), pltpu.SemaphoreType.DMA((n, N_DEV - 1))],
        compiler_params=pltpu.CompilerParams(has_side_effects=True), name="all_gather_big")(*zones)


def _shard_rows_tile(a):
    return max(d for d in range(16, 257, 16) if a % d == 0)


HBM = pl.BlockSpec(memory_space=pltpu.HBM)
SEM = pl.BlockSpec(memory_space=pltpu.SEMAPHORE)
DATAFLOW = pltpu.SideEffectType.DATAFLOW_SIDE_EFFECTING


def _in_hbm(a):
    return pltpu.with_memory_space_constraint(a, pltpu.HBM)


def _gather_ici_copies(place, src, land, w):
    me = _index(place)
    return [(_peer(place, k), land.at[me], land.at[me]) for k in (1, 2, 4, 6)]


def _gather_d2d_copies(place, src, land, w):
    blocks = [_index(_peer(place, k)) for k in (2, 4, 6)]
    return [(_peer(place, 1), land.at[b], land.at[b]) for b in blocks]


GATHER_ICI = (4, _gather_ici_copies)
GATHER_D2D = (3, _gather_d2d_copies)


def _scatter_plan(layers):
    def copies(place, src, land, w):
        me = _index(place)
        mine = land.at[me] if layers[w] is None else land.at[me, layers[w]]
        return [(_peer(place, k), src.at[_index(_peer(place, k))], mine) for k in range(1, N_DEV)]
    return (N_DEV - 1, copies)


def _gather_all_copies(place, src, land, w):
    me = _index(place)
    return [(_peer(place, k), land.at[me], land.at[me]) for k in range(1, N_DEV)]


GATHER_ALL = (N_DEV - 1, _gather_all_copies)


def _sum_blocks(zone, part, me):
    r = part.shape[1]

    def body(me_ref, z_ref, p_ref, o_ref):
        acc = None
        for s in range(N_DEV):
            term = jnp.where(me_ref[0] == s, p_ref[...], z_ref[s])
            acc = term if acc is None else acc + term
        o_ref[...] = acc

    grid_spec = pltpu.PrefetchScalarGridSpec(
        num_scalar_prefetch=1, grid=(1,),
        in_specs=[pl.BlockSpec((N_DEV, r, 128), lambda i, me_ref: (0, 0, 0)),
                  pl.BlockSpec((None, r, 128), lambda i, me_ref: (me_ref[0], 0, 0))],
        out_specs=pl.BlockSpec((r, 128), lambda i, me_ref: (0, 0)))
    return pl.pallas_call(body, grid_spec=grid_spec, out_shape=S((r, 128), F32),
                          compiler_params=_cp("arbitrary"), name="sum_small")(me, zone, part)


def _exchange_start(srcs, lands, plan, name, after=()):
    ns, n = len(srcs), len(lands)
    n_in = ns + n + len(after)
    per, copies = plan

    def body(*refs):
        ins, land = refs[:ns], refs[ns:ns + n]
        send, recv = refs[n_in], refs[n_in + 1]
        token = refs[-1]
        place = _mesh_place()
        for i in range(per):
            for w in range(n):
                target, src, dst = copies(place, ins[w] if ns else None, land[w], w)[i]
                pltpu.make_async_remote_copy(src_ref=src, dst_ref=dst, send_sem=send.at[w * per + i],
                                             recv_sem=recv.at[w * per + i], device_id=target, device_id_type=MESH).start()
        token[...] = jnp.zeros_like(token)

    sems = pltpu.SemaphoreType.DMA((n * per,))
    thru = [pltpu.HBM(a.shape, a.dtype) for a in list(srcs) + list(lands)]
    out = pl.pallas_call(
        body, name=name, in_specs=[HBM] * (ns + n) + [ANY] * len(after),
        out_shape=(sems, sems, *thru, S((8, 128), F32)),
        out_specs=(SEM, SEM, *([HBM] * (ns + n)), pl.BlockSpec(memory_space=pltpu.VMEM)),
        input_output_aliases={i: 2 + i for i in range(ns + n)},
        compiler_params=pltpu.CompilerParams(has_side_effects=DATAFLOW),
    )(*[_in_hbm(a) for a in list(srcs) + list(lands)], *after)
    return out[0], out[1], list(out[2:2 + ns]), list(out[2 + ns:2 + ns + n]), out[-1]


def _exchange_wait(send, recv, srcs, lands, plan, after, name):
    ns, n = len(srcs), len(lands)
    per, copies = plan
    afters = tuple(after) if isinstance(after, (tuple, list)) else (after,)

    def body(*refs):
        ins, land = refs[:ns], refs[ns:ns + n]
        send_ref, recv_ref = refs[ns + n], refs[ns + n + 1]
        place = _mesh_place()
        for i in range(per):
            for w in range(n):
                target, src, dst = copies(place, ins[w] if ns else None, land[w], w)[i]
                cp = pltpu.make_async_remote_copy(src_ref=src, dst_ref=dst, send_sem=send_ref.at[w * per + i],
                                                  recv_sem=recv_ref.at[w * per + i], device_id=target,
                                                  device_id_type=MESH)
                cp.wait_send()
                cp.wait_recv()

    thru = [pltpu.HBM(a.shape, a.dtype) for a in list(srcs) + list(lands)]
    out = pl.pallas_call(
        body, name=name, in_specs=[HBM] * (ns + n) + [SEM, SEM] + [ANY] * len(afters),
        out_shape=tuple(thru), out_specs=tuple([HBM] * (ns + n)),
        input_output_aliases={i: i for i in range(ns + n)},
        compiler_params=pltpu.CompilerParams(has_side_effects=DATAFLOW),
    )(*srcs, *lands, send, recv, *afters)
    return list(out[:ns]), list(out[ns:])


def _all_reduce_small(part, name, deps=()):
    def body(*refs):
        p_ref = refs[0]
        o_ref, rbuf, send1, recv1, send2, recv2 = refs[-6:]
        place = _mesh_place()
        me = _index(place)
        rbuf[pl.ds(me, 1)] = p_ref[pl.ds(me, 1)]
        first = [pltpu.make_async_remote_copy(src_ref=p_ref.at[_index(_peer(place, k))], dst_ref=rbuf.at[me],
                                              send_sem=send1.at[k - 1], recv_sem=recv1.at[k - 1],
                                              device_id=_peer(place, k), device_id_type=MESH)
                 for k in range(1, N_DEV)]
        for cp in first:
            cp.start()
        for cp in first:
            cp.wait()
        acc = rbuf[0]
        for d in range(1, N_DEV):
            acc = acc + rbuf[d]
        o_ref[pl.ds(me, 1)] = acc[None]
        second = [pltpu.make_async_remote_copy(src_ref=o_ref.at[me], dst_ref=o_ref.at[me], send_sem=send2.at[k - 1],
                                               recv_sem=recv2.at[k - 1], device_id=_peer(place, k),
                                               device_id_type=MESH)
                  for k in range(1, N_DEV)]
        for cp in second:
            cp.start()
        for cp in second:
            cp.wait()

    vm = pl.BlockSpec(memory_space=pltpu.VMEM)
    ops = [part, *deps]
    return pl.pallas_call(
        body, in_specs=[vm] + [ANY] * len(deps), out_specs=vm, out_shape=S(part.shape, F32),
        scratch_shapes=[pltpu.VMEM(part.shape, F32)] + [pltpu.SemaphoreType.DMA((N_DEV - 1,))] * 4,
        compiler_params=pltpu.CompilerParams(has_side_effects=True, vmem_limit_bytes=VMEM_LIMIT), name=name)(*ops)


def _adamw(w, g, m, v):
    m = ADAM_B1 * m + (1.0 - ADAM_B1) * g
    v = ADAM_B2 * v + (1.0 - ADAM_B2) * (g * g)
    m_hat = m / (1.0 - ADAM_B1 ** ADAM_STEP)
    v_hat = v / (1.0 - ADAM_B2 ** ADAM_STEP)
    return -ADAM_LR * (m_hat / (jnp.sqrt(v_hat) + ADAM_EPS) + ADAM_WD * w), m, v


def _adam_big(parts, own, me, w, m, v, name):
    nl, a, b = w.shape
    ta = _shard_rows_tile(a)

    def body(me_ref, p_ref, *refs):
        own_refs, (w_ref, m_ref, v_ref, g_ref, d_ref, mo_ref, vo_ref) = refs[:nl], refs[nl:]
        layer = pl.program_id(0)
        mine = own_refs[0][...]
        for k in range(1, nl):
            mine = jnp.where(layer == k, own_refs[k][...], mine)
        g = None
        for s in range(N_DEV):
            term = jnp.where(me_ref[0] == s, mine, p_ref[s]).astype(F32)
            g = term if g is None else g + term
        g_ref[...] = g
        d_ref[...], mo_ref[...], vo_ref[...] = _adamw(w_ref[...], g, m_ref[...], v_ref[...])

    blk = pl.BlockSpec((None, ta, b), lambda l, i, me_ref: (l, i, 0))

    def own_spec(k):
        return pl.BlockSpec((None, ta, b), lambda l, i, me_ref: (me_ref[0], jnp.where(l == k, i, 0), 0))

    grid_spec = pltpu.PrefetchScalarGridSpec(
        num_scalar_prefetch=1, grid=(nl, a // ta),
        in_specs=[pl.BlockSpec((N_DEV, None, ta, b), lambda l, i, me_ref: (0, l, i, 0))]
        + [own_spec(k) for k in range(nl)] + [blk, blk, blk],
        out_specs=[blk] * 4)
    return pl.pallas_call(body, grid_spec=grid_spec, out_shape=[S(w.shape, F32)] * 4,
                          compiler_params=_cp("arbitrary", "arbitrary"), name=name)(me, parts, *own, w, m, v)


def _adam_small(g, w, m, v, name):
    def body(g_ref, w_ref, m_ref, v_ref, d_ref, mo_ref, vo_ref):
        d_ref[...], mo_ref[...], vo_ref[...] = _adamw(w_ref[...], g_ref[...], m_ref[...], v_ref[...])

    return pl.pallas_call(body, out_shape=[S(g.shape, F32)] * 3, compiler_params=_cp(), name=name)(g, w, m, v)


BIG = ("even_w_in", "even_w_out", "mla_w_down", "mla_w_qb", "mla_w_kvb", "mla_w_o", "mlp_w1", "mlp_w2")
BIG_KEY = dict(even_w_in="win", even_w_out="wout", mla_w_down="wdown", mla_w_qb="wqb", mla_w_kvb="wkvb",
               mla_w_o="wo", mlp_w1="w1", mlp_w2="w2")
SMALL = (("ln_mix_g", "ln_mix_g", None), ("ln_mix_b", "ln_mix_b", None), ("ln_ffn_g", "ln_ffn_g", None),
         ("ln_ffn_b", "ln_ffn_b", None), ("pool_w", "pool_w", None), ("pool_scale", "pool_scale", None),
         ("lru_conv_w", "conv_w", 2), ("lru_conv_b", "conv_b", None), ("lru_w_a", "w_a", None),
         ("lru_b_a", "b_a", None), ("lru_w_x", "w_x", None), ("lru_b_x", "b_x", None), ("lru_lambda", "lam", None),
         ("mla_q_norm_g", "gq", 1), ("mla_kv_norm_g", "gkv", 1))
WEIGHTS = ("ln_mix_g", "ln_mix_b", "ln_ffn_g", "ln_ffn_b", "even_w_in", "pool_w", "pool_scale", "lru_conv_w",
           "lru_conv_b", "lru_w_a", "lru_b_a", "lru_w_x", "lru_b_x", "lru_lambda", "even_w_out", "mla_w_down",
           "mla_q_norm_g", "mla_kv_norm_g", "mla_w_qb", "mla_w_kvb", "mla_w_o", "mlp_w1", "mlp_w2")
ALL_AXES = ("x", "y", "c")


def _layer_weights(l):
    j = l // 2
    if l % 2 == 0:
        mixer = [("win", "even_w_in", j), ("wout", "even_w_out", j)]
    else:
        mixer = [("wdown", "mla_w_down", j), ("wqb", "mla_w_qb", j), ("wkvb", "mla_w_kvb", j), ("wo", "mla_w_o", j)]
    return mixer + [("w1", "mlp_w1", l), ("w2", "mlp_w2", l)]


def _pack(arrays, multiple):
    flat = jnp.concatenate([a.reshape(-1) for a in arrays])
    pad = (-flat.shape[0]) % multiple
    return jnp.pad(flat, (0, pad))


def _unpack(flat, shapes):
    out, at = [], 0
    for shp in shapes:
        n = 1
        for s in shp:
            n *= s
        out.append(flat[at:at + n].reshape(shp))
        at += n
    return out


def _global_shape(local_shape, axis):
    if axis is None:
        return tuple(local_shape)
    return tuple(s * N_DEV if i == axis else s for i, s in enumerate(local_shape))


def _step(x, positions, tgt, w, m, v):
    t = x.shape[1]
    me = _index(_mesh_place())

    sharded = [(name, axis) for name, _, axis in SMALL if axis is not None]
    zeros_with_mine = [lax.dynamic_update_slice_in_dim(jnp.zeros(_global_shape(w[name].shape, axis), F32), w[name],
                                                       me * w[name].shape[axis], axis) for name, axis in sharded]
    chunk = N_DEV * 8 * 128
    gathered = _all_reduce_small(_pack(zeros_with_mine, chunk).reshape(N_DEV, -1, 128), "gather_small")
    full = dict(zip([name for name, _ in sharded],
                    _unpack(gathered.reshape(-1), [_global_shape(w[name].shape, axis) for name, axis in sharded])))

    def zone_of(shard):
        return lax.dynamic_update_slice_in_dim(lax.empty((N_DEV,) + shard.shape, BF16), shard.astype(BF16)[None], me, 0)

    def keys_of(l, part):
        keys = [key for key, _, _ in _layer_weights(l)]
        if l == 0:
            return keys[:1] if part == 0 else keys[1:]
        return keys if part == 0 else []

    shard_of = {(l, key): w[name][i] for l in range(DEPTH) for key, name, i in _layer_weights(l)}
    first = _all_gather_big([zone_of(shard_of[0, key]) for key in keys_of(0, 0)])
    flights, after = {}, (first[0], gathered)
    for l in range(DEPTH):
        for part in (0, 1):
            if (l, part) != (0, 0) and keys_of(l, part):
                zones = [zone_of(shard_of[l, key]) for key in keys_of(l, part)]
                send, recv, _, lands, token = _exchange_start([], zones, GATHER_ICI, "gather_start_%d_%d" % (l, part),
                                                              after=after)
                flights[l, part] = (send, recv, [], lands)
                after = (token,)

    passing = {}

    def pass_on(l, part, after):
        tag = "%d_%d" % (l, part)
        _, lands = _exchange_wait(*flights[l, part], GATHER_ICI, after, "gather_wait_" + tag)
        send, recv, _, lands, token = _exchange_start([], lands, GATHER_D2D, "gather_pass_" + tag)
        passing[l, part] = (send, recv, [], lands)
        return token

    def early_pass(l, after):
        return pass_on(l, 0, after) if l >= 2 else None

    def weights_of(l, part, after):
        keys = keys_of(l, part)
        if (l, part) == (0, 0):
            arrays = first
        elif keys:
            if (l, part) not in passing:
                pass_on(l, part, after)
            _, arrays = _exchange_wait(*passing[l, part], GATHER_D2D, after, "gather_pass_wait_%d_%d" % (l, part))
        big = dict(zip(keys, arrays)) if keys else {}
        if "win" in big:
            big["win2d"] = big["win"].transpose(1, 0, 2).reshape(D, EVEN_IN)
        if "wout" in big:
            big["wout2d"] = big["wout"].reshape(EVEN_MIX, D)
        if "wdown" in big:
            big["wdown2d"] = big["wdown"].reshape(D, ODD_IN)
        return big

    zone = {name: lax.empty((N_DEV,) + w[name].shape, BF16) for name in BIG}
    name_of = {key: name for name, key in BIG_KEY.items()}
    sent, last_token = [], [None]

    def grads_done(l, grads):
        keys = list(grads)
        index = {key: i for key, _, i in _layer_weights(l)}
        layers = [index[key] for key in keys]
        send, recv, srcs, lands, tok = _exchange_start([grads[k] for k in keys], [zone[name_of[k]] for k in keys],
                                                       _scatter_plan(layers), "scatter_start_%d_%s" % (l, keys[0]))
        for k, land in zip(keys, lands):
            zone[name_of[k]] = land
        sent.append((send, recv, srcs, keys, layers))
        last_token[0] = tok
        return tok

    row3 = lambda a: a.reshape(a.shape[0], 1, a.shape[1])
    small = dict(ln_mix_g=row3(w["ln_mix_g"]), ln_mix_b=row3(w["ln_mix_b"]), ln_ffn_g=row3(w["ln_ffn_g"]),
                 ln_ffn_b=row3(w["ln_ffn_b"]), pool_w=w["pool_w"], pool_scale=row3(w["pool_scale"]),
                 conv_w=full["lru_conv_w"], conv_b=row3(w["lru_conv_b"]), w_a=w["lru_w_a"], b_a=row3(w["lru_b_a"]),
                 w_x=w["lru_w_x"], b_x=row3(w["lru_b_x"]), lam=row3(w["lru_lambda"]),
                 gq=row3(full["mla_q_norm_g"]), gkv=row3(full["mla_kv_norm_g"]))

    loss_part, grad_x, g = _local_step(x[0], positions.reshape(t, 1), tgt[0], small, weights_of, grads_done,
                                       start_dep=token, prefetch=early_pass)

    own = {name: [None] * w[name].shape[0] for name in BIG}
    me_arr = me.astype(jnp.int32).reshape(1)
    out = {}
    local_g = [jnp.stack(g[key]).reshape(_global_shape(w[name].shape, axis)) for name, key, axis in SMALL]
    local_g.append(loss_part.reshape(1))
    part = _pack(local_g, chunk).reshape(N_DEV, -1, 128)
    small_plan = _scatter_plan([None])
    s_send, s_recv, s_src, s_land, after = _exchange_start([part], [lax.empty(part.shape, F32)], small_plan,
                                                           "small_scatter_start", after=(last_token[0],))
    for n_flight, (send, recv, srcs, keys, layers) in enumerate(sent):
        if n_flight == len(sent) - 1:
            for name in BIG:
                if BIG_KEY[name] not in keys:
                    out[name] = _adam_big(zone[name], own[name], me_arr, w[name], m[name], v[name], "adam_" + name)
            s_src, s_land = _exchange_wait(s_send, s_recv, s_src, s_land, small_plan, [o[0] for o in out.values()],
                                           "small_scatter_wait")
            chunk_sum = _sum_blocks(s_land[0], s_src[0], me_arr)
            r_zone = lax.dynamic_update_slice_in_dim(lax.empty(part.shape, F32), chunk_sum[None], me, 0)
            r_send, r_recv, _, r_land, after = _exchange_start([], [r_zone], GATHER_ALL, "small_gather_start")
        srcs, lands = _exchange_wait(send, recv, srcs, [zone[name_of[k]] for k in keys], _scatter_plan(layers),
                                     after, "scatter_wait_%d" % n_flight)
        for k, land, src, layer in zip(keys, lands, srcs, layers):
            zone[name_of[k]] = land
            own[name_of[k]][layer] = src
        after = lands[0]
    for name in BIG:
        if name not in out:
            out[name] = _adam_big(zone[name], own[name], me_arr, w[name], m[name], v[name], "adam_" + name)

    _, reduced = _exchange_wait(r_send, r_recv, [], r_land, GATHER_ALL, [out[name][0] for name in BIG],
                                "small_gather_wait")
    reduced = _unpack(reduced[0].reshape(-1), [a.shape for a in local_g])
    loss = reduced[-1][0]
    mine = [a if axis is None else lax.dynamic_slice_in_dim(a, me * w[name].shape[axis], w[name].shape[axis], axis)
            for a, (name, _, axis) in zip(reduced, SMALL)]
    for grad, (name, _, _) in zip(mine, SMALL):
        shape = w[name].shape
        as_2d = lambda a: a.reshape(-1, shape[-1])
        new = _adam_small(as_2d(grad), as_2d(w[name]), as_2d(m[name]), as_2d(v[name]), "adam_" + name)
        out[name] = (grad,) + tuple(a.reshape(shape) for a in new)

    return (loss, grad_x[None]) + tuple(out[name][i] for i in range(4) for name in WEIGHTS)


def kernel(x, positions, ln_mix_g, ln_mix_b, ln_ffn_g, ln_ffn_b, even_w_in, pool_w, pool_scale, lru_conv_w, lru_conv_b, lru_w_a, lru_b_a, lru_w_x, lru_b_x, lru_lambda, even_w_out, mla_w_down, mla_q_norm_g, mla_kv_norm_g, mla_w_qb, mla_w_kvb, mla_w_o, mlp_w1, mlp_w2, loss_target, m_ln_mix_g, m_ln_mix_b, m_ln_ffn_g, m_ln_ffn_b, m_even_w_in, m_pool_w, m_pool_scale, m_lru_conv_w, m_lru_conv_b, m_lru_w_a, m_lru_b_a, m_lru_w_x, m_lru_b_x, m_lru_lambda, m_even_w_out, m_mla_w_down, m_mla_q_norm_g, m_mla_kv_norm_g, m_mla_w_qb, m_mla_w_kvb, m_mla_w_o, m_mlp_w1, m_mlp_w2, v_ln_mix_g, v_ln_mix_b, v_ln_ffn_g, v_ln_ffn_b, v_even_w_in, v_pool_w, v_pool_scale, v_lru_conv_w, v_lru_conv_b, v_lru_w_a, v_lru_b_a, v_lru_w_x, v_lru_b_x, v_lru_lambda, v_even_w_out, v_mla_w_down, v_mla_q_norm_g, v_mla_kv_norm_g, v_mla_w_qb, v_mla_w_kvb, v_mla_w_o, v_mlp_w1, v_mlp_w2):
    w = dict(zip(WEIGHTS, (ln_mix_g, ln_mix_b, ln_ffn_g, ln_ffn_b, even_w_in, pool_w, pool_scale, lru_conv_w,
                           lru_conv_b, lru_w_a, lru_b_a, lru_w_x, lru_b_x, lru_lambda, even_w_out, mla_w_down,
                           mla_q_norm_g, mla_kv_norm_g, mla_w_qb, mla_w_kvb, mla_w_o, mlp_w1, mlp_w2)))
    m = dict(zip(WEIGHTS, (m_ln_mix_g, m_ln_mix_b, m_ln_ffn_g, m_ln_ffn_b, m_even_w_in, m_pool_w, m_pool_scale,
                           m_lru_conv_w, m_lru_conv_b, m_lru_w_a, m_lru_b_a, m_lru_w_x, m_lru_b_x, m_lru_lambda,
                           m_even_w_out, m_mla_w_down, m_mla_q_norm_g, m_mla_kv_norm_g, m_mla_w_qb, m_mla_w_kvb,
                           m_mla_w_o, m_mlp_w1, m_mlp_w2)))
    v = dict(zip(WEIGHTS, (v_ln_mix_g, v_ln_mix_b, v_ln_ffn_g, v_ln_ffn_b, v_even_w_in, v_pool_w, v_pool_scale,
                           v_lru_conv_w, v_lru_conv_b, v_lru_w_a, v_lru_b_a, v_lru_w_x, v_lru_b_x, v_lru_lambda,
                           v_even_w_out, v_mla_w_down, v_mla_q_norm_g, v_mla_kv_norm_g, v_mla_w_qb, v_mla_w_kvb,
                           v_mla_w_o, v_mlp_w1, v_mlp_w2)))
    return _step(x, positions, loss_target, w, m, v)
```

```python
import functools

import jax
import jax.numpy as jnp
from jax import lax
from jax.experimental import pallas as pl
from jax.experimental.pallas import tpu as pltpu

F32 = jnp.float32
BF16 = jnp.bfloat16
S = jax.ShapeDtypeStruct

D = 1024
DEPTH = 4
N_DEV = 8
CHUNK_SHIFT = 6
POOL_WINDOWS = (2, 4, 8, 16)
POOL_W = 512
LRU_W = 1024
LRU_HEADS = 8
HEAD = 128
LRU_C = 8.0
EVEN_IN = 2560
EVEN_MIX = 1536
MLA_HEADS = 8
NOPE = 128
ROPE = 64
VDIM = 128
Q_RANK = 384
KV_RANK = 256
ODD_IN = 704
D_FF = 4096
FF_BLK = D_FF // N_DEV
ROPE_THETA = 10000.0
ALPHA = (2 * DEPTH) ** 0.25
LN_EPS = 1e-5
RMS_EPS = 1e-6
ATT_SCALE = (NOPE + ROPE) ** -0.5
NEG = float(jnp.finfo(jnp.float32).min)
ADAM_LR = 0.001
ADAM_B1 = 0.9
ADAM_B2 = 0.999
ADAM_EPS = 1e-08
ADAM_WD = 0.01
ADAM_STEP = 10
V7X_VMEM_BYTES = 64 * 1024 * 1024
VMEM_LIMIT = V7X_VMEM_BYTES - 8 * 1024 * 1024
MESH = pl.DeviceIdType.MESH


def _cp(*sem):
    return pltpu.CompilerParams(dimension_semantics=sem if sem else None, vmem_limit_bytes=VMEM_LIMIT)


def _dot(a, b):
    return jnp.dot(a, b, preferred_element_type=F32)


def _dot_nt(a, b):
    return lax.dot_general(a, b, (((1,), (1,)), ((), ())), preferred_element_type=F32)


def _dot_tn(a, b):
    return lax.dot_general(a, b, (((0,), (0,)), ((), ())), preferred_element_type=F32)


def _full(shape):
    return pl.BlockSpec(shape, lambda *_: (0,) * len(shape))


def _mm(a, b, *, mode, grid, a_spec, b_spec, out_shape, out_spec, name, add=None, add_spec=None, add_scale=1.0,
        dep=None):
    dot = {"nn": _dot, "nt": _dot_nt, "tn": _dot_tn}[mode]

    def body(*refs):
        a_ref, b_ref, o_ref = refs[0], refs[1], refs[-1]
        acc = dot(a_ref[...].astype(BF16), b_ref[...].astype(BF16))
        if add is not None:
            acc = acc + add_scale * refs[2][...]
        o_ref[...] = acc.astype(o_ref.dtype)

    ops = [a, b] if add is None else [a, b, add]
    specs = [a_spec, b_spec] if add is None else [a_spec, b_spec, add_spec]
    if dep is not None:
        ops.append(dep)
        specs.append(pl.BlockSpec(memory_space=pl.ANY))
    return pl.pallas_call(body, grid=grid, in_specs=specs, out_specs=out_spec, out_shape=out_shape,
                          compiler_params=_cp(*(("parallel",) * len(grid))), name=name)(*ops)


def _ln_stats(z):
    mu = jnp.mean(z, axis=-1, keepdims=True)
    zc = z - mu
    var = jnp.mean(zc * zc, axis=-1, keepdims=True)
    rstd = lax.rsqrt(var + LN_EPS)
    return zc * rstd, rstd


def _row_tile(t):
    return min(512, t)


def _resid_ln(x, mix, g3, b3, l, name):
    t = x.shape[0]
    bm = _row_tile(t)

    def body(x_ref, m_ref, g_ref, b_ref, z_ref, y_ref, yb_ref):
        z = ALPHA * x_ref[...] + m_ref[...]
        xh, _ = _ln_stats(z)
        y = xh * g_ref[...] + b_ref[...]
        z_ref[...] = z
        y_ref[...] = y
        yb_ref[...] = y.astype(BF16)

    row = pl.BlockSpec((bm, D), lambda i: (i, 0))
    vec = pl.BlockSpec((None, 1, D), lambda i: (l, 0, 0))
    return pl.pallas_call(body, grid=(t // bm,), in_specs=[row, row, vec, vec], out_specs=[row, row, row],
                          out_shape=[S((t, D), F32), S((t, D), F32), S((t, D), BF16)],
                          compiler_params=_cp("parallel"), name=name)(x, mix, g3, b3)


def _proj_resid_ln(x, a, wmat, g3, b3, l, name):
    t, k = a.shape
    bm = _row_tile(t)

    def body(x_ref, a_ref, w_ref, g_ref, b_ref, z_ref, y_ref, yb_ref):
        z = ALPHA * x_ref[...] + _dot(a_ref[...], w_ref[...])
        xh, _ = _ln_stats(z)
        y = xh * g_ref[...] + b_ref[...]
        z_ref[...] = z
        y_ref[...] = y
        yb_ref[...] = y.astype(BF16)

    row = pl.BlockSpec((bm, D), lambda i: (i, 0))
    vec = pl.BlockSpec((None, 1, D), lambda i: (l, 0, 0))
    return pl.pallas_call(body, grid=(t // bm,),
                          in_specs=[row, pl.BlockSpec((bm, k), lambda i: (i, 0)), _full((k, D)), vec, vec],
                          out_specs=[row, row, row], out_shape=[S((t, D), F32), S((t, D), F32), S((t, D), BF16)],
                          compiler_params=_cp("parallel"), name=name)(x, a, wmat, g3, b3)


def _ln_bwd(d, z, g3, l, name, r=None, dep=None):
    t = z.shape[0]
    bm = _row_tile(t)

    def body(*refs):
        refs = list(refs)
        d_ref = refs.pop(0)
        dy = d_ref[...]
        if r is not None:
            dy = dy + ALPHA * refs.pop(0)[...]
        z_ref, g_ref = refs.pop(0), refs.pop(0)
        if dep is not None:
            refs.pop(0)
        dz_ref, dzb_ref, dg_ref, db_ref = refs
        xh, rstd = _ln_stats(z_ref[...])
        dyg = dy * g_ref[...]
        m1 = jnp.mean(dyg, axis=-1, keepdims=True)
        m2 = jnp.mean(dyg * xh, axis=-1, keepdims=True)
        dz = rstd * (dyg - m1 - xh * m2)
        dz_ref[...] = dz
        dzb_ref[...] = dz.astype(BF16)

        @pl.when(pl.program_id(0) == 0)
        def _():
            dg_ref[...] = jnp.zeros_like(dg_ref)
            db_ref[...] = jnp.zeros_like(db_ref)

        dg_ref[...] += jnp.sum(dy * xh, axis=0, keepdims=True)
        db_ref[...] += jnp.sum(dy, axis=0, keepdims=True)

    row = pl.BlockSpec((bm, D), lambda i: (i, 0))
    vec = pl.BlockSpec((None, 1, D), lambda i: (l, 0, 0))
    acc = pl.BlockSpec((1, D), lambda i: (0, 0))
    ops = [d, z, g3] if r is None else [d, r, z, g3]
    specs = [row, row, vec] if r is None else [row, row, row, vec]
    if dep is not None:
        ops.append(dep)
        specs.append(_full(dep.shape))
    return pl.pallas_call(body, grid=(t // bm,), in_specs=specs, out_specs=[row, row, acc, acc],
                          out_shape=[S((t, D), F32), S((t, D), BF16), S((1, D), F32), S((1, D), F32)],
                          compiler_params=_cp("arbitrary"), name=name)(*ops)


def _loss_grad(y, tgt):
    t = y.shape[0]
    bm = _row_tile(t)

    def body(y_ref, t_ref, dy_ref, loss_ref, acc_ref):
        i = pl.program_id(0)
        e = y_ref[...] - t_ref[...]
        dy_ref[...] = e * (1.0 / D)

        @pl.when(i == 0)
        def _():
            acc_ref[...] = jnp.zeros_like(acc_ref)

        acc_ref[...] += jnp.sum(e * e, axis=0, keepdims=True)

        @pl.when(i == pl.num_programs(0) - 1)
        def _():
            loss_ref[...] = jnp.full(loss_ref.shape, (0.5 / D) * jnp.sum(acc_ref[...]), F32)

    row = pl.BlockSpec((bm, D), lambda i: (i, 0))
    return pl.pallas_call(body, grid=(t // bm,), in_specs=[row, row],
                          out_specs=[row, pl.BlockSpec((1, 128), lambda i: (0, 0))],
                          out_shape=[S((t, D), F32), S((1, 128), F32)],
                          scratch_shapes=[pltpu.VMEM((1, D), F32)],
                          compiler_params=_cp("arbitrary"), name="loss_grad")(y, tgt)


def _mlp_row_tile(t):
    return min(1024, t)


def _mlp_fwd(y, yb, w1g, w2g, g3, b3, l, dep=None):
    t = yb.shape[0]
    bm = _mlp_row_tile(t)

    def body(*refs):
        y_ref, yb_ref, w1_ref, w2_ref, g_ref, b_ref = refs[:6]
        z_ref, o_ref, ob_ref, act_ref, acc_ref = refs[-5:]
        j = pl.program_id(1)
        h = jnp.maximum(_dot(yb_ref[...], w1_ref[...]), 0.0)
        act = (h * h).astype(BF16)
        act_ref[...] = act
        c = _dot(act, w2_ref[...])

        @pl.when(j == 0)
        def _():
            acc_ref[...] = c

        @pl.when(j > 0)
        def _():
            acc_ref[...] += c

        @pl.when(j == N_DEV - 1)
        def _():
            z = ALPHA * y_ref[...] + acc_ref[...]
            xh, _ = _ln_stats(z)
            out = xh * g_ref[...] + b_ref[...]
            z_ref[...] = z
            o_ref[...] = out
            ob_ref[...] = out.astype(BF16)

    row = pl.BlockSpec((bm, D), lambda i, j: (i, 0))
    vec = pl.BlockSpec((None, 1, D), lambda i, j: (l, 0, 0))
    deps = [] if dep is None else [dep]
    return pl.pallas_call(
        body, grid=(t // bm, N_DEV),
        in_specs=[row, row, pl.BlockSpec((None, D, FF_BLK), lambda i, j: (j, 0, 0)),
                  pl.BlockSpec((None, FF_BLK, D), lambda i, j: (j, 0, 0)), vec, vec] + [ANY] * len(deps),
        out_specs=[row, row, row, pl.BlockSpec((bm, FF_BLK), lambda i, j: (i, j))],
        out_shape=[S((t, D), F32), S((t, D), F32), S((t, D), BF16), S((t, D_FF), BF16)],
        scratch_shapes=[pltpu.VMEM((bm, D), F32)],
        compiler_params=_cp("parallel", "arbitrary"), name="mlp_fwd")(y, yb, w1g, w2g, g3, b3, *deps)


def _mlp_bwd_dh(act, dzb, w1g, w2g):
    t = act.shape[0]
    bm = _mlp_row_tile(t)

    def body(a_ref, dz_ref, w1_ref, w2_ref, dh_ref, acc_ref):
        j = pl.program_id(1)
        r = jnp.sqrt(a_ref[...].astype(F32))
        da = _dot_nt(dz_ref[...], w2_ref[...])
        dh = (da * (2.0 * r)).astype(BF16)
        dh_ref[...] = dh
        c = _dot_nt(dh, w1_ref[...])

        @pl.when(j == 0)
        def _():
            acc_ref[...] = c

        @pl.when(j > 0)
        def _():
            acc_ref[...] += c

    row = pl.BlockSpec((bm, D), lambda i, j: (i, 0))
    hid = pl.BlockSpec((bm, FF_BLK), lambda i, j: (i, j))
    return pl.pallas_call(
        body, grid=(t // bm, N_DEV),
        in_specs=[hid, row,
                  pl.BlockSpec((None, D, FF_BLK), lambda i, j: (j, 0, 0)),
                  pl.BlockSpec((None, FF_BLK, D), lambda i, j: (j, 0, 0))],
        out_specs=[hid, row],
        out_shape=[S((t, D_FF), BF16), S((t, D), F32)],
        compiler_params=_cp("parallel", "arbitrary"), name="mlp_bwd_dh")(act, dzb, w1g, w2g)


def _shift_dn(x, k, rows, fill=0.0):
    return jnp.where(rows >= k, pltpu.roll(x, k, 0), fill)


def _shift_up(x, k, rows, fill=0.0):
    t = x.shape[0]
    return jnp.where(rows < t - k, pltpu.roll(x, t - k, 0), fill)


def _scan_rows(a, b, shift):
    rows = lax.broadcasted_iota(jnp.int32, a.shape, 0)
    k = 1
    t = a.shape[0]
    while k < t:
        b = a * shift(b, k, rows) + b
        if 2 * k < t:
            a = a * shift(a, k, rows, 1.0)
        k *= 2
    return b


def _scan_dn(a, b):
    return _scan_rows(a, b, _shift_dn)


def _scan_up(a, b):
    return _scan_rows(a, b, _shift_up)


def _window_sum_dn(x, w, rows):
    k = 1
    while k < w:
        x = x + _shift_dn(x, k, rows)
        k *= 2
    return x


def _window_sum_up(x, w, rows):
    k = 1
    while k < w:
        x = x + _shift_up(x, k, rows)
        k *= 2
    return x


def _pool_diff(u, w, rows):
    inv_count = 1.0 / jnp.minimum(rows + 1, w).astype(F32)
    return _window_sum_dn(u, w, rows) * inv_count - u, inv_count


def _pool_fwd(proj, pool_w, pool_scale3, j):
    t = proj.shape[0]

    def body(u_ref, w_ref, s_ref, y_ref):
        rows = lax.broadcasted_iota(jnp.int32, (t, HEAD), 0)
        for g, w in enumerate(POOL_WINDOWS):
            cols = slice(g * HEAD, (g + 1) * HEAD)
            d, _ = _pool_diff(u_ref[:, cols], w, rows)
            y = _dot(d.astype(BF16), w_ref[g].astype(BF16)) * s_ref[:, cols]
            y_ref[:, cols] = y.astype(BF16)

    return pl.pallas_call(
        body, grid=(1,),
        in_specs=[pl.BlockSpec((t, POOL_W), lambda i: (0, 0)),
                  pl.BlockSpec((None, 4, HEAD, HEAD), lambda i: (j, 0, 0, 0)),
                  pl.BlockSpec((None, 1, POOL_W), lambda i: (j, 0, 0))],
        out_specs=pl.BlockSpec((t, POOL_W), lambda i: (0, 0)),
        out_shape=S((t, POOL_W), BF16), compiler_params=_cp("arbitrary"), name="pool_fwd")(proj, pool_w, pool_scale3)


def _pool_bwd(proj, dycat, pool_w, pool_scale3, j):
    t = proj.shape[0]

    def body(u_ref, dy_ref, w_ref, s_ref, du_ref, dw_ref, ds_ref):
        rows = lax.broadcasted_iota(jnp.int32, (t, HEAD), 0)
        for g, w in enumerate(POOL_WINDOWS):
            cols = slice(g * HEAD, (g + 1) * HEAD)
            d, inv_count = _pool_diff(u_ref[:, cols], w, rows)
            db = d.astype(BF16)
            wg = w_ref[g].astype(BF16)
            dy = dy_ref[:, cols]
            ds_ref[:, cols] = jnp.sum(dy * _dot(db, wg), axis=0, keepdims=True)
            dzz = (dy * s_ref[:, cols]).astype(BF16)
            dw_ref[g] = _dot_tn(db, dzz)
            dd = _dot_nt(dzz, wg)
            du_ref[:, cols] = (_window_sum_up(dd * inv_count, w, rows) - dd).astype(BF16)

    return pl.pallas_call(
        body, grid=(1,),
        in_specs=[pl.BlockSpec((t, POOL_W), lambda i: (0, 0)),
                  pl.BlockSpec((t, POOL_W), lambda i: (0, 0)),
                  pl.BlockSpec((None, 4, HEAD, HEAD), lambda i: (j, 0, 0, 0)),
                  pl.BlockSpec((None, 1, POOL_W), lambda i: (j, 0, 0))],
        out_specs=[pl.BlockSpec((t, POOL_W), lambda i: (0, 0)), _full((4, HEAD, HEAD)), _full((1, POOL_W))],
        out_shape=[S((t, POOL_W), BF16), S((4, HEAD, HEAD), F32), S((1, POOL_W), F32)],
        compiler_params=_cp("arbitrary"), name="pool_bwd")(proj, dycat, pool_w, pool_scale3)


GELU_C = 0.7978845608028654
GELU_K = 0.044715


def _gelu(x):
    th = jnp.tanh(GELU_C * (x + GELU_K * x * x * x))
    return 0.5 * x * (1.0 + th), th


def _lru_forward(u, gate, cw, cb, wa, ba, wx, bx, lam, rows):
    v = cw[3:4] * u + cw[2:3] * _shift_dn(u, 1, rows) + cw[1:2] * _shift_dn(u, 2, rows) \
        + cw[0:1] * _shift_dn(u, 3, rows) + cb
    vb = v.astype(BF16)
    r = jax.nn.sigmoid(_dot(vb, wa) + ba)
    i = jax.nn.sigmoid(_dot(vb, wx) + bx)
    sp = jnp.maximum(-lam, 0.0) + jnp.log1p(jnp.exp(-jnp.abs(lam)))
    log_a = (-LRU_C) * r * sp
    a = jnp.exp(log_a)
    one_m_a2 = -jnp.tanh(log_a) * (a * a + 1.0)
    mult = jnp.sqrt(one_m_a2)
    h = _scan_dn(a, mult * (i * v))
    gl, th = _gelu(gate)
    return dict(v=v, vb=vb, r=r, i=i, sp=sp, a=a, mult=mult, h=h, gl=gl, th=th)


def _lru_specs(t, j, col0_u, col0_g):
    blk = lambda c0: pl.BlockSpec((t, HEAD), lambda h: (0, c0 + h))
    vec = pl.BlockSpec((None, 1, HEAD), lambda h: (j, 0, h))
    return [blk(col0_u), blk(col0_g),
            pl.BlockSpec((None, 4, HEAD), lambda h: (j, 0, h)), vec,
            pl.BlockSpec((None, None, HEAD, HEAD), lambda h: (j, h, 0, 0)), vec,
            pl.BlockSpec((None, None, HEAD, HEAD), lambda h: (j, h, 0, 0)), vec, vec]


def _lru_fwd(proj, p, j):
    t = proj.shape[0]

    def body(u_ref, g_ref, cw_ref, cb_ref, wa_ref, ba_ref, wx_ref, bx_ref, lam_ref, y_ref):
        rows = lax.broadcasted_iota(jnp.int32, (t, HEAD), 0)
        f = _lru_forward(u_ref[...], g_ref[...], cw_ref[...], cb_ref[...], wa_ref[...].astype(BF16), ba_ref[...],
                         wx_ref[...].astype(BF16), bx_ref[...], lam_ref[...], rows)
        y_ref[...] = (f["h"] * f["gl"]).astype(BF16)

    return pl.pallas_call(
        body, grid=(LRU_HEADS,), in_specs=_lru_specs(t, j, POOL_W // HEAD, (POOL_W + LRU_W) // HEAD),
        out_specs=pl.BlockSpec((t, HEAD), lambda h: (0, h)), out_shape=S((t, LRU_W), BF16),
        compiler_params=_cp("parallel"), name="lru_fwd")(
            proj, proj, p["conv_w"], p["conv_b"], p["w_a"], p["b_a"], p["w_x"], p["b_x"], p["lam"])


def _lru_bwd(proj, dycat, p, j):
    t = proj.shape[0]

    def body(u_ref, g_ref, cw_ref, cb_ref, wa_ref, ba_ref, wx_ref, bx_ref, lam_ref, dy_ref,
             du_ref, dgate_ref, dcw_ref, dcb_ref, dwa_ref, dba_ref, dwx_ref, dbx_ref, dlam_ref):
        rows = lax.broadcasted_iota(jnp.int32, (t, HEAD), 0)
        u = u_ref[...]
        gate = g_ref[...]
        cw = cw_ref[...]
        wa = wa_ref[...].astype(BF16)
        wx = wx_ref[...].astype(BF16)
        lam = lam_ref[...]
        f = _lru_forward(u, gate, cw, cb_ref[...], wa, ba_ref[...], wx, bx_ref[...], lam, rows)
        v, r, i, a, mult, h, th = f["v"], f["r"], f["i"], f["a"], f["mult"], f["h"], f["th"]
        dy = dy_ref[...]
        dgl = 0.5 * (1.0 + th) + 0.5 * gate * (1.0 - th * th) * GELU_C * (1.0 + 3.0 * GELU_K * gate * gate)
        dgate_ref[...] = (dy * h * dgl).astype(BF16)
        g = _scan_up(_shift_up(a, 1, rows), dy * f["gl"])
        da = g * _shift_dn(h, 1, rows)
        iv = i * v
        dmult = g * iv
        di = g * mult * v
        dv = g * mult * i
        dlog_a = da * a - dmult * (a * a) / mult
        dr = dlog_a * (-LRU_C) * f["sp"]
        dsp = jnp.sum(dlog_a * (-LRU_C) * r, axis=0, keepdims=True)
        dlam_ref[...] = -dsp * jax.nn.sigmoid(-lam)
        dpa = dr * r * (1.0 - r)
        dpx = di * i * (1.0 - i)
        dpab = dpa.astype(BF16)
        dpxb = dpx.astype(BF16)
        dwa_ref[...] = _dot_tn(f["vb"], dpab)
        dwx_ref[...] = _dot_tn(f["vb"], dpxb)
        dba_ref[...] = jnp.sum(dpa, axis=0, keepdims=True)
        dbx_ref[...] = jnp.sum(dpx, axis=0, keepdims=True)
        dv = dv + _dot_nt(dpab, wa) + _dot_nt(dpxb, wx)
        dcb_ref[...] = jnp.sum(dv, axis=0, keepdims=True)
        du = cw[3:4] * dv
        dcw_ref[3:4, :] = jnp.sum(dv * u, axis=0, keepdims=True)
        for k in (1, 2, 3):
            du = du + cw[3 - k:4 - k] * _shift_up(dv, k, rows)
            dcw_ref[3 - k:4 - k, :] = jnp.sum(dv * _shift_dn(u, k, rows), axis=0, keepdims=True)
        du_ref[...] = du.astype(BF16)

    blk = pl.BlockSpec((t, HEAD), lambda h: (0, h))
    vec = pl.BlockSpec((1, HEAD), lambda h: (0, h))
    mat = pl.BlockSpec((None, HEAD, HEAD), lambda h: (h, 0, 0))
    return pl.pallas_call(
        body, grid=(LRU_HEADS,),
        in_specs=_lru_specs(t, j, POOL_W // HEAD, (POOL_W + LRU_W) // HEAD)
        + [pl.BlockSpec((t, HEAD), lambda h: (0, POOL_W // HEAD + h))],
        out_specs=[blk, blk, pl.BlockSpec((4, HEAD), lambda h: (0, h)), vec, mat, vec, mat, vec, vec],
        out_shape=[S((t, LRU_W), BF16), S((t, LRU_W), BF16), S((4, LRU_W), F32), S((1, LRU_W), F32),
                   S((LRU_HEADS, HEAD, HEAD), F32), S((1, LRU_W), F32),
                   S((LRU_HEADS, HEAD, HEAD), F32), S((1, LRU_W), F32), S((1, LRU_W), F32)],
        compiler_params=_cp("parallel"), name="lru_bwd")(
            proj, proj, p["conv_w"], p["conv_b"], p["w_a"], p["b_a"], p["w_x"], p["b_x"], p["lam"], dycat)


def _rope(x, c, s):
    x1 = x[:, :ROPE // 2]
    x2 = x[:, ROPE // 2:]
    return jnp.concatenate([x1 * c - x2 * s, x1 * s + x2 * c], axis=-1)


def _rope_t(d, c, s):
    d1 = d[:, :ROPE // 2]
    d2 = d[:, ROPE // 2:]
    return jnp.concatenate([d1 * c + d2 * s, d2 * c - d1 * s], axis=-1)


def _rope_tables(pos2, inv_freq):
    t = pos2.shape[0]

    def body(p_ref, f_ref, c_ref, s_ref):
        ang = p_ref[...].astype(F32) * f_ref[...]
        c_ref[...] = jnp.cos(ang)
        s_ref[...] = jnp.sin(ang)

    return pl.pallas_call(body, out_shape=[S((t, ROPE // 2), F32), S((t, ROPE // 2), F32)],
                          name="rope_tables")(pos2, inv_freq)


def _down_norm(xb, wdown_g, gq3, gkv3, cos, sin, j):
    t = xb.shape[0]
    bm = _row_tile(t)

    def body(x_ref, w_ref, gq_ref, gkv_ref, c_ref, s_ref, down_ref, cq_ref, ckv_ref, kpe_ref):
        w = w_ref[...].reshape(D, ODD_IN)
        down = _dot(x_ref[...], w)
        down_ref[...] = down
        q = down[:, :Q_RANK]
        cq_ref[...] = (q * lax.rsqrt(jnp.mean(q * q, axis=-1, keepdims=True) + RMS_EPS) * gq_ref[...]).astype(BF16)
        kv = down[:, Q_RANK:Q_RANK + KV_RANK]
        ckv_ref[...] = (kv * lax.rsqrt(jnp.mean(kv * kv, axis=-1, keepdims=True) + RMS_EPS)
                        * gkv_ref[...]).astype(BF16)
        kpe_ref[...] = _rope(down[:, Q_RANK + KV_RANK:], c_ref[...], s_ref[...])

    row = lambda n: pl.BlockSpec((bm, n), lambda i: (i, 0))
    return pl.pallas_call(
        body, grid=(t // bm,),
        in_specs=[row(D), _full((N_DEV, D // N_DEV, ODD_IN)),
                  pl.BlockSpec((None, 1, Q_RANK), lambda i: (j, 0, 0)),
                  pl.BlockSpec((None, 1, KV_RANK), lambda i: (j, 0, 0)), row(ROPE // 2), row(ROPE // 2)],
        out_specs=[row(ODD_IN), row(Q_RANK), row(KV_RANK), row(ROPE)],
        out_shape=[S((t, ODD_IN), F32), S((t, Q_RANK), BF16), S((t, KV_RANK), BF16), S((t, ROPE), F32)],
        compiler_params=_cp("parallel"), name="down_norm")(xb, wdown_g, gq3, gkv3, cos, sin)


def _q_tile(t, widest):
    return min(widest, t // 2)


def _attn_probs(q, k, qs):
    s = _dot_nt(q, k) * ATT_SCALE
    tq = q.shape[0]
    rows = lax.broadcasted_iota(jnp.int32, (tq, tq), 0)
    cols = lax.broadcasted_iota(jnp.int32, (tq, tq), 1)
    last = jnp.where(jnp.right_shift(cols, CHUNK_SHIFT) <= jnp.right_shift(rows, CHUNK_SHIFT), s[:, qs:], NEG)
    s = last if qs == 0 else jnp.concatenate([s[:, :qs], last], axis=1)
    e = jnp.exp(s - jnp.max(s, axis=-1, keepdims=True))
    return e / jnp.sum(e, axis=-1, keepdims=True)


def _head_qkv(cq, ckv, kpe, c, s, wq_ref, wkv_ref):
    q = jnp.concatenate([_dot(cq, wq_ref[:, :NOPE]), _rope(_dot(cq, wq_ref[:, NOPE:]), c, s)], axis=1).astype(BF16)
    k = jnp.concatenate([_dot(ckv, wkv_ref[:, :NOPE]), kpe], axis=1).astype(BF16)
    vv = _dot(ckv, wkv_ref[:, NOPE:]).astype(BF16)
    return q, k, vv


def _attn_in_specs(t):
    return [_full((t, Q_RANK)), _full((t, KV_RANK)), _full((t, ROPE)), _full((t, ROPE // 2)), _full((t, ROPE // 2)),
            pl.BlockSpec((None, Q_RANK, NOPE + ROPE), lambda h: (h, 0, 0)),
            pl.BlockSpec((None, KV_RANK, NOPE + VDIM), lambda h: (h, 0, 0)),
            pl.BlockSpec((None, VDIM, D), lambda h: (h, 0, 0))]


def _attn_fwd(cq, ckv, kpe, cos, sin, wqb_g, wkvb_g, wo_g):
    t = cq.shape[0]
    tq = _q_tile(t, 256)

    def body(cq_ref, ckv_ref, kpe_ref, c_ref, s_ref, wq_ref, wkv_ref, wo_ref, o_ref, mix_ref):
        q, k, vv = _head_qkv(cq_ref[...], ckv_ref[...], kpe_ref[...], c_ref[...], s_ref[...], wq_ref, wkv_ref)
        for qs in range(0, t, tq):
            ke = qs + tq
            p = _attn_probs(q[qs:ke], k[:ke], qs)
            o_ref[qs:ke, :] = _dot(p.astype(BF16), vv[:ke]).astype(BF16)
        c = _dot(o_ref[...], wo_ref[...])

        @pl.when(pl.program_id(0) == 0)
        def _():
            mix_ref[...] = c

        @pl.when(pl.program_id(0) > 0)
        def _():
            mix_ref[...] += c

    return pl.pallas_call(
        body, grid=(MLA_HEADS,), in_specs=_attn_in_specs(t),
        out_specs=[pl.BlockSpec((None, t, VDIM), lambda h: (h, 0, 0)), _full((t, D))],
        out_shape=[S((MLA_HEADS, t, VDIM), BF16), S((t, D), F32)],
        compiler_params=_cp("arbitrary"), name="attn_fwd")(cq, ckv, kpe, cos, sin, wqb_g, wkvb_g, wo_g)


def _attn_bwd(cq, ckv, kpe, cos, sin, wqb_g, wkvb_g, wo_g, o, dzb):
    t = cq.shape[0]
    tq = _q_tile(t, 512)

    def body(cq_ref, ckv_ref, kpe_ref, c_ref, s_ref, wq_ref, wkv_ref, wo_ref, o_ref, dz_ref,
             dwo_ref, dwq_ref, dwkv_ref, dcq_ref, dckv_ref, dkpe_ref, dkt_s, dvt_s, dq_s):
        cqv = cq_ref[...]
        ckvv = ckv_ref[...]
        c = c_ref[...]
        s = s_ref[...]
        q, k, vv = _head_qkv(cqv, ckvv, kpe_ref[...], c, s, wq_ref, wkv_ref)
        dzv = dz_ref[...]
        dwo_ref[...] = _dot_tn(o_ref[...], dzv).astype(BF16)
        do = _dot_nt(dzv, wo_ref[...]).astype(BF16)
        dkt_s[...] = jnp.zeros_like(dkt_s)
        dvt_s[...] = jnp.zeros_like(dvt_s)
        for qs in range(0, t, tq):
            ke = qs + tq
            p = _attn_probs(q[qs:ke], k[:ke], qs)
            dp = _dot_nt(do[qs:ke], vv[:ke])
            ds = (p * (dp - jnp.sum(p * dp, axis=-1, keepdims=True)) * ATT_SCALE).astype(BF16)
            dq_s[qs:ke, :] = _dot(ds, k[:ke])
            dkt_s[0:NOPE + ROPE, 0:ke] += _dot_tn(q[qs:ke], ds)
            dvt_s[:, 0:ke] += _dot_tn(do[qs:ke], p.astype(BF16))
        dk = dkt_s[...].T
        dqn = dq_s[:, :NOPE].astype(BF16)
        dqp = _rope_t(dq_s[:, NOPE:], c, s).astype(BF16)
        dkn = dk[:, :NOPE].astype(BF16)
        dkp = dk[:, NOPE:NOPE + ROPE]
        dvv = dvt_s[...].T.astype(BF16)
        dwq_ref[:, :NOPE] = _dot_tn(cqv, dqn).astype(BF16)
        dwq_ref[:, NOPE:] = _dot_tn(cqv, dqp).astype(BF16)
        dwkv_ref[:, :NOPE] = _dot_tn(ckvv, dkn).astype(BF16)
        dwkv_ref[:, NOPE:] = _dot_tn(ckvv, dvv).astype(BF16)
        dcq = _dot_nt(dqn, wq_ref[:, :NOPE]) + _dot_nt(dqp, wq_ref[:, NOPE:])
        dckv = _dot_nt(dkn, wkv_ref[:, :NOPE]) + _dot_nt(dvv, wkv_ref[:, NOPE:])

        @pl.when(pl.program_id(0) == 0)
        def _():
            dcq_ref[...] = dcq
            dckv_ref[...] = dckv
            dkpe_ref[...] = dkp

        @pl.when(pl.program_id(0) > 0)
        def _():
            dcq_ref[...] += dcq
            dckv_ref[...] += dckv
            dkpe_ref[...] += dkp

    per_head = lambda a, b: pl.BlockSpec((None, a, b), lambda h: (h, 0, 0))
    return pl.pallas_call(
        body, grid=(MLA_HEADS,),
        in_specs=_attn_in_specs(t) + [per_head(t, VDIM), _full((t, D))],
        out_specs=[per_head(VDIM, D), per_head(Q_RANK, NOPE + ROPE), per_head(KV_RANK, NOPE + VDIM),
                   _full((t, Q_RANK)), _full((t, KV_RANK)), _full((t, ROPE))],
        out_shape=[S((MLA_HEADS, VDIM, D), BF16), S((MLA_HEADS, Q_RANK, NOPE + ROPE), BF16),
                   S((MLA_HEADS, KV_RANK, NOPE + VDIM), BF16),
                   S((t, Q_RANK), F32), S((t, KV_RANK), F32), S((t, ROPE), F32)],
        scratch_shapes=[pltpu.VMEM((2 * NOPE, t), F32), pltpu.VMEM((VDIM, t), F32),
                        pltpu.VMEM((t, NOPE + ROPE), F32)],
        compiler_params=_cp("arbitrary"), name="attn_bwd")(cq, ckv, kpe, cos, sin, wqb_g, wkvb_g, wo_g, o, dzb)


def _rms_bwd(down, dcq, dckv, dkpe, cos, sin, gq3, gkv3, j):
    t = down.shape[0]
    bm = _row_tile(t)

    def body(down_ref, dcq_ref, dckv_ref, dkpe_ref, c_ref, s_ref, gq_ref, gkv_ref, dd_ref, dgq_ref, dgkv_ref):
        @pl.when(pl.program_id(0) == 0)
        def _():
            dgq_ref[...] = jnp.zeros_like(dgq_ref)
            dgkv_ref[...] = jnp.zeros_like(dgkv_ref)

        def rms_b(x, dy, g):
            rstd = lax.rsqrt(jnp.mean(x * x, axis=-1, keepdims=True) + RMS_EPS)
            xh = x * rstd
            dyg = dy * g
            return rstd * (dyg - xh * jnp.mean(dyg * xh, axis=-1, keepdims=True)), jnp.sum(dy * xh, axis=0, keepdims=True)

        dq, dgq = rms_b(down_ref[:, :Q_RANK], dcq_ref[...], gq_ref[...])
        dkv, dgkv = rms_b(down_ref[:, Q_RANK:Q_RANK + KV_RANK], dckv_ref[...], gkv_ref[...])
        dgq_ref[...] += dgq
        dgkv_ref[...] += dgkv
        dd_ref[:, :Q_RANK] = dq.astype(BF16)
        dd_ref[:, Q_RANK:Q_RANK + KV_RANK] = dkv.astype(BF16)
        dd_ref[:, Q_RANK + KV_RANK:] = _rope_t(dkpe_ref[...], c_ref[...], s_ref[...]).astype(BF16)

    row = lambda n: pl.BlockSpec((bm, n), lambda i: (i, 0))
    return pl.pallas_call(
        body, grid=(t // bm,),
        in_specs=[row(ODD_IN), row(Q_RANK), row(KV_RANK), row(ROPE), row(ROPE // 2), row(ROPE // 2),
                  pl.BlockSpec((None, 1, Q_RANK), lambda i: (j, 0, 0)),
                  pl.BlockSpec((None, 1, KV_RANK), lambda i: (j, 0, 0))],
        out_specs=[row(ODD_IN), _full((1, Q_RANK)), _full((1, KV_RANK))],
        out_shape=[S((t, ODD_IN), BF16), S((1, Q_RANK), F32), S((1, KV_RANK), F32)],
        compiler_params=_cp("arbitrary"), name="rms_bwd")(down, dcq, dckv, dkpe, cos, sin, gq3, gkv3)


def _col_blocks(t, n, bn):
    return pl.BlockSpec((t, bn), lambda i: (0, i))


def _row_blocks(n, bm):
    return pl.BlockSpec((bm, n), lambda i: (i, 0))


def _local_step(x, pos2, tgt, small, weights_of, grads_done, start_dep=None, prefetch=None):
    t = x.shape[0]
    bm = _row_tile(t)
    inv_freq = (ROPE_THETA ** (-jnp.arange(0, ROPE, 2, dtype=F32) / ROPE)).reshape(1, ROPE // 2)
    cos, sin = _rope_tables(pos2, inv_freq)
    lru_p = {k: small[k] for k in ("conv_w", "conv_b", "w_a", "b_a", "w_x", "b_x", "lam")}

    saved = []
    y, yb = x, x.astype(BF16)
    for l in range(DEPTH):
        j = l // 2
        big = weights_of(l, 0, y)
        sv = dict(xb=yb, big=big)
        if l % 2 == 0:
            proj = _mm(yb, big["win_t"], mode="nt", grid=(EVEN_IN // 512,), a_spec=_full((t, D)),
                       b_spec=_row_blocks(D, 512), out_shape=S((t, EVEN_IN), F32),
                       out_spec=_col_blocks(t, EVEN_IN, 512), name="even_proj", dep=start_dep if l == 0 else None)
            ycat = jnp.concatenate([_pool_fwd(proj, small["pool_w"], small["pool_scale"], j),
                                    _lru_fwd(proj, lru_p, j)], axis=1)
            big.update(weights_of(l, 1, ycat))
            z1, y1, y1b = _proj_resid_ln(y, ycat, big["wout2d"], small["ln_mix_g"], small["ln_mix_b"], l, "even_out")
            sv.update(proj=proj, ycat=ycat)
        else:
            down, cq, ckv, kpe = _down_norm(yb, big["wdown"], small["gq"], small["gkv"], cos, sin, j)
            o, mix = _attn_fwd(cq, ckv, kpe, cos, sin, big["wqb"], big["wkvb"], big["wo"])
            z1, y1, y1b = _resid_ln(y, mix, small["ln_mix_g"], small["ln_mix_b"], l, "resid_ln")
            sv.update(down=down, cq=cq, ckv=ckv, kpe=kpe, o=o)
        fetched = prefetch(l + 1, y1) if prefetch is not None and l + 1 < DEPTH else None
        z2, y, yb, act = _mlp_fwd(y1, y1b, big["w1"], big["w2"], small["ln_ffn_g"], small["ln_ffn_b"], l,
                                  dep=fetched)
        sv.update(z1=z1, y1b=y1b, z2=z2, act=act)
        saved.append(sv)

    dy, loss_tile = _loss_grad(y, tgt)

    g = {k: [None] * n for k, n in (("ln_mix_g", 4), ("ln_mix_b", 4), ("ln_ffn_g", 4), ("ln_ffn_b", 4),
                                    ("pool_w", 2), ("pool_scale", 2), ("conv_w", 2), ("conv_b", 2),
                                    ("w_a", 2), ("b_a", 2), ("w_x", 2), ("b_x", 2), ("lam", 2),
                                    ("gq", 2), ("gkv", 2))}
    dep = None
    for l in reversed(range(DEPTH)):
        j = l // 2
        sv = saved[l]
        big = sv["big"]
        dz2, dz2b, g["ln_ffn_g"][l], g["ln_ffn_b"][l] = _ln_bwd(dy, sv["z2"], small["ln_ffn_g"], l, "ln_bwd", dep=dep)
        act = sv["act"]
        dh, dff = _mlp_bwd_dh(act, dz2b, big["w1"], big["w2"])
        dw1 = _mm(sv["y1b"], dh, mode="tn", grid=(N_DEV,), a_spec=_full((t, D)),
                  b_spec=_col_blocks(t, D_FF, FF_BLK), out_shape=S((N_DEV, D, FF_BLK), BF16),
                  out_spec=pl.BlockSpec((None, D, FF_BLK), lambda i: (i, 0, 0)), name="mlp_dw1")
        dw2 = _mm(act, dz2b, mode="tn", grid=(N_DEV,), a_spec=_col_blocks(t, D_FF, FF_BLK),
                  b_spec=_full((t, D)), out_shape=S((N_DEV, FF_BLK, D), BF16),
                  out_spec=pl.BlockSpec((None, FF_BLK, D), lambda i: (i, 0, 0)), name="mlp_dw2")
        dep = grads_done(l, dict(w1=dw1, w2=dw2))
        dz1, dz1b, g["ln_mix_g"][l], g["ln_mix_b"][l] = _ln_bwd(dff, sv["z1"], small["ln_mix_g"], l, "ln_bwd_res",
                                                                 r=dz2, dep=dep)
        if l % 2 == 0:
            wout = big["wout2d"]
            dycat = _mm(dz1b, wout, mode="nt", grid=(EVEN_MIX // 512,), a_spec=_full((t, D)),
                        b_spec=_row_blocks(D, 512), out_shape=S((t, EVEN_MIX), F32),
                        out_spec=_col_blocks(t, EVEN_MIX, 512), name="even_dycat")
            dwout = _mm(sv["ycat"], dz1b, mode="tn", grid=(EVEN_MIX // 512,), a_spec=_col_blocks(t, EVEN_MIX, 512),
                        b_spec=_full((t, D)), out_shape=S((EVEN_MIX, D), BF16), out_spec=_row_blocks(D, 512),
                        name="even_dwout")
            du_pool, g["pool_w"][j], g["pool_scale"][j] = _pool_bwd(sv["proj"], dycat, small["pool_w"],
                                                                   small["pool_scale"], j)
            (du_lru, du_gate, g["conv_w"][j], g["conv_b"][j], g["w_a"][j], g["b_a"][j], g["w_x"][j], g["b_x"][j],
             g["lam"][j]) = _lru_bwd(sv["proj"], dycat, lru_p, j)
            dproj = jnp.concatenate([du_pool, du_lru, du_gate], axis=1)
            dwin = _mm(sv["xb"], dproj, mode="tn", grid=(EVEN_IN // 512,), a_spec=_full((t, D)),
                       b_spec=_col_blocks(t, EVEN_IN, 512), out_shape=S((D, EVEN_IN), BF16),
                       out_spec=_col_blocks(D, EVEN_IN, 512), name="even_dwin")
            dep = grads_done(l, dict(win=dwin.reshape(D, N_DEV, EVEN_IN // N_DEV).transpose(1, 0, 2),
                                     wout=dwout.reshape(N_DEV, EVEN_MIX // N_DEV, D)))
            dy = _mm(dproj, big["win_t"], mode="nn", grid=(t // bm,), a_spec=_row_blocks(EVEN_IN, bm),
                     b_spec=_full((EVEN_IN, D)), out_shape=S((t, D), F32), out_spec=_row_blocks(D, bm),
                     add=dz1, add_spec=_row_blocks(D, bm), add_scale=ALPHA, name="even_dx")
        else:
            dwo, dwqb, dwkvb, dcq, dckv, dkpe = _attn_bwd(
                sv["cq"], sv["ckv"], sv["kpe"], cos, sin, big["wqb"], big["wkvb"], big["wo"], sv["o"], dz1b)
            ddown, g["gq"][j], g["gkv"][j] = _rms_bwd(sv["down"], dcq, dckv, dkpe, cos, sin, small["gq"],
                                                     small["gkv"], j)
            dwdown = _mm(sv["xb"], ddown, mode="tn", grid=(N_DEV,), a_spec=_col_blocks(t, D, D // N_DEV),
                         b_spec=_full((t, ODD_IN)), out_shape=S((N_DEV, D // N_DEV, ODD_IN), BF16),
                         out_spec=pl.BlockSpec((None, D // N_DEV, ODD_IN), lambda i: (i, 0, 0)),
                         name="odd_dwdown")
            dep = grads_done(l, dict(wdown=dwdown, wqb=dwqb, wkvb=dwkvb, wo=dwo))
            dy = _mm(ddown, big["wdown2d"], mode="nt", grid=(t // bm,), a_spec=_row_blocks(ODD_IN, bm),
                     b_spec=_full((D, ODD_IN)), out_shape=S((t, D), F32), out_spec=_row_blocks(D, bm),
                     add=dz1, add_spec=_row_blocks(D, bm), add_scale=ALPHA, name="odd_dx")
    return loss_tile[0, 0], dy, g


def _mesh_place():
    x, y, c = lax.axis_index("x"), lax.axis_index("y"), lax.axis_index("c")
    return x, y, c


def _peer(place, k):
    x, y, c = place
    return (1 - x if k & 4 else x, 1 - y if k & 2 else y, 1 - c if k & 1 else c)


def _index(place):
    x, y, c = place
    return 4 * x + 2 * y + c


ANY = pl.BlockSpec(memory_space=pl.ANY)


def _make_zones(shards, me, name):
    n = len(shards)

    def body(me_ref, *refs):
        for src, dst in zip(refs[:n], refs[n:]):
            dst[...] = src[...].astype(BF16)

    grid_spec = pltpu.PrefetchScalarGridSpec(
        num_scalar_prefetch=1, grid=(1,),
        in_specs=[pl.BlockSpec(s.shape, lambda i, me_ref: (0, 0)) for s in shards],
        out_specs=[pl.BlockSpec((None,) + s.shape, lambda i, me_ref: (me_ref[0], 0, 0)) for s in shards])
    return pl.pallas_call(body, grid_spec=grid_spec, out_shape=[S((N_DEV,) + s.shape, BF16) for s in shards],
                          compiler_params=_cp("arbitrary"), name=name)(me, *shards)


def _all_gather_big(zones):
    n = len(zones)

    def body(*refs):
        outs = refs[n:2 * n]
        send, recv = refs[2 * n:]
        x, y, c = _mesh_place()
        me, sibling = (x, y, c), (x, y, 1 - c)
        chips = [(1 - x, y), (x, 1 - y), (1 - x, 1 - y)]

        def copy(w, k, block, to):
            blk = outs[w].at[_index(block)]
            return pltpu.make_async_remote_copy(src_ref=blk, dst_ref=blk, send_sem=send.at[w, k], recv_sem=recv.at[w, k],
                                                device_id=to, device_id_type=MESH)

        first = []
        for w in range(n):
            first.append(copy(w, 0, me, sibling))
            first += [copy(w, 1 + j, me, (*chip, c)) for j, chip in enumerate(chips)]
        for cp in first:
            cp.start()
        passed = []
        for w in range(n):
            for j, chip in enumerate(chips):
                copy(w, 1 + j, (*chip, c), me).wait_recv()
                cp = copy(w, 4 + j, (*chip, c), sibling)
                cp.start()
                passed.append(cp)
        for w in range(n):
            copy(w, 0, sibling, me).wait_recv()
            for j, chip in enumerate(chips):
                copy(w, 4 + j, (*chip, 1 - c), me).wait_recv()
        for cp in first + passed:
            cp.wait_send()

    return pl.pallas_call(
        body, in_specs=[ANY] * n, out_specs=[ANY] * n, out_shape=[S(z.shape, z.dtype) for z in zones],
        input_output_aliases={i: i for i in range(n)},
        scratch_shapes=[pltpu.SemaphoreType.DMA((n, N_DEV - 1)), pltpu.SemaphoreType.DMA((n, N_DEV - 1))],
        compiler_params=pltpu.CompilerParams(has_side_effects=True), name="all_gather_big")(*zones)


def _shard_rows_tile(a):
    return max(d for d in range(16, 257, 16) if a % d == 0)


HBM = pl.BlockSpec(memory_space=pltpu.HBM)
SEM = pl.BlockSpec(memory_space=pltpu.SEMAPHORE)
DATAFLOW = pltpu.SideEffectType.DATAFLOW_SIDE_EFFECTING


def _in_hbm(a):
    return pltpu.with_memory_space_constraint(a, pltpu.HBM)


def _gather_ici_copies(place, src, land, w):
    me = _index(place)
    return [(_peer(place, k), land.at[me], land.at[me]) for k in (1, 2, 4, 6)]


def _gather_d2d_copies(place, src, land, w):
    blocks = [_index(_peer(place, k)) for k in (2, 4, 6)]
    return [(_peer(place, 1), land.at[b], land.at[b]) for b in blocks]


GATHER_ICI = (4, _gather_ici_copies)
GATHER_D2D = (3, _gather_d2d_copies)


def _scatter_plan(layers):
    def copies(place, src, land, w):
        me = _index(place)
        mine = land.at[me] if layers[w] is None else land.at[me, layers[w]]
        return [(_peer(place, k), src.at[_index(_peer(place, k))], mine) for k in range(1, N_DEV)]
    return (N_DEV - 1, copies)


def _gather_all_copies(place, src, land, w):
    me = _index(place)
    return [(_peer(place, k), land.at[me], land.at[me]) for k in range(1, N_DEV)]


GATHER_ALL = (N_DEV - 1, _gather_all_copies)


def _sum_blocks(zone, part, me):
    r = part.shape[1]

    def body(me_ref, z_ref, p_ref, o_ref):
        acc = None
        for s in range(N_DEV):
            term = jnp.where(me_ref[0] == s, p_ref[...], z_ref[s])
            acc = term if acc is None else acc + term
        o_ref[...] = acc

    grid_spec = pltpu.PrefetchScalarGridSpec(
        num_scalar_prefetch=1, grid=(1,),
        in_specs=[pl.BlockSpec((N_DEV, r, 128), lambda i, me_ref: (0, 0, 0)),
                  pl.BlockSpec((None, r, 128), lambda i, me_ref: (me_ref[0], 0, 0))],
        out_specs=pl.BlockSpec((r, 128), lambda i, me_ref: (0, 0)))
    return pl.pallas_call(body, grid_spec=grid_spec, out_shape=S((r, 128), F32),
                          compiler_params=_cp("arbitrary"), name="sum_small")(me, zone, part)


def _exchange_start(srcs, lands, plan, name, after=()):
    ns, n = len(srcs), len(lands)
    n_in = ns + n + len(after)
    per, copies = plan

    def body(*refs):
        ins, land = refs[:ns], refs[ns:ns + n]
        send, recv = refs[n_in], refs[n_in + 1]
        token = refs[-1]
        place = _mesh_place()
        for i in range(per):
            for w in range(n):
                target, src, dst = copies(place, ins[w] if ns else None, land[w], w)[i]
                pltpu.make_async_remote_copy(src_ref=src, dst_ref=dst, send_sem=send.at[w * per + i],
                                             recv_sem=recv.at[w * per + i], device_id=target, device_id_type=MESH).start()
        token[...] = jnp.zeros_like(token)

    sems = pltpu.SemaphoreType.DMA((n * per,))
    thru = [pltpu.HBM(a.shape, a.dtype) for a in list(srcs) + list(lands)]
    out = pl.pallas_call(
        body, name=name, in_specs=[HBM] * (ns + n) + [ANY] * len(after),
        out_shape=(sems, sems, *thru, S((8, 128), F32)),
        out_specs=(SEM, SEM, *([HBM] * (ns + n)), pl.BlockSpec(memory_space=pltpu.VMEM)),
        input_output_aliases={i: 2 + i for i in range(ns + n)},
        compiler_params=pltpu.CompilerParams(has_side_effects=DATAFLOW),
    )(*[_in_hbm(a) for a in list(srcs) + list(lands)], *after)
    return out[0], out[1], list(out[2:2 + ns]), list(out[2 + ns:2 + ns + n]), out[-1]


def _exchange_wait(send, recv, srcs, lands, plan, after, name):
    ns, n = len(srcs), len(lands)
    per, copies = plan
    afters = tuple(after) if isinstance(after, (tuple, list)) else (after,)

    def body(*refs):
        ins, land = refs[:ns], refs[ns:ns + n]
        send_ref, recv_ref = refs[ns + n], refs[ns + n + 1]
        place = _mesh_place()
        for i in range(per):
            for w in range(n):
                target, src, dst = copies(place, ins[w] if ns else None, land[w], w)[i]
                cp = pltpu.make_async_remote_copy(src_ref=src, dst_ref=dst, send_sem=send_ref.at[w * per + i],
                                                  recv_sem=recv_ref.at[w * per + i], device_id=target,
                                                  device_id_type=MESH)
                cp.wait_send()
                cp.wait_recv()

    thru = [pltpu.HBM(a.shape, a.dtype) for a in list(srcs) + list(lands)]
    out = pl.pallas_call(
        body, name=name, in_specs=[HBM] * (ns + n) + [SEM, SEM] + [ANY] * len(afters),
        out_shape=tuple(thru), out_specs=tuple([HBM] * (ns + n)),
        input_output_aliases={i: i for i in range(ns + n)},
        compiler_params=pltpu.CompilerParams(has_side_effects=DATAFLOW),
    )(*srcs, *lands, send, recv, *afters)
    return list(out[:ns]), list(out[ns:])


def _all_reduce_small(part, name, deps=()):
    def body(*refs):
        p_ref = refs[0]
        o_ref, rbuf, send1, recv1, send2, recv2 = refs[-6:]
        place = _mesh_place()
        me = _index(place)
        rbuf[pl.ds(me, 1)] = p_ref[pl.ds(me, 1)]
        first = [pltpu.make_async_remote_copy(src_ref=p_ref.at[_index(_peer(place, k))], dst_ref=rbuf.at[me],
                                              send_sem=send1.at[k - 1], recv_sem=recv1.at[k - 1],
                                              device_id=_peer(place, k), device_id_type=MESH)
                 for k in range(1, N_DEV)]
        for cp in first:
            cp.start()
        for cp in first:
            cp.wait()
        acc = rbuf[0]
        for d in range(1, N_DEV):
            acc = acc + rbuf[d]
        o_ref[pl.ds(me, 1)] = acc[None]
        second = [pltpu.make_async_remote_copy(src_ref=o_ref.at[me], dst_ref=o_ref.at[me], send_sem=send2.at[k - 1],
                                               recv_sem=recv2.at[k - 1], device_id=_peer(place, k),
                                               device_id_type=MESH)
                  for k in range(1, N_DEV)]
        for cp in second:
            cp.start()
        for cp in second:
            cp.wait()

    vm = pl.BlockSpec(memory_space=pltpu.VMEM)
    ops = [part, *deps]
    return pl.pallas_call(
        body, in_specs=[vm] + [ANY] * len(deps), out_specs=vm, out_shape=S(part.shape, F32),
        scratch_shapes=[pltpu.VMEM(part.shape, F32)] + [pltpu.SemaphoreType.DMA((N_DEV - 1,))] * 4,
        compiler_params=pltpu.CompilerParams(has_side_effects=True, vmem_limit_bytes=VMEM_LIMIT), name=name)(*ops)


def _adamw(w, g, m, v):
    m = ADAM_B1 * m + (1.0 - ADAM_B1) * g
    v = ADAM_B2 * v + (1.0 - ADAM_B2) * (g * g)
    m_hat = m / (1.0 - ADAM_B1 ** ADAM_STEP)
    v_hat = v / (1.0 - ADAM_B2 ** ADAM_STEP)
    return -ADAM_LR * (m_hat / (jnp.sqrt(v_hat) + ADAM_EPS) + ADAM_WD * w), m, v


def _adam_big(parts, own, me, w, m, v, name):
    nl, a, b = w.shape
    ta = _shard_rows_tile(a)

    def body(me_ref, p_ref, *refs):
        own_refs, (w_ref, m_ref, v_ref, g_ref, d_ref, mo_ref, vo_ref) = refs[:nl], refs[nl:]
        layer = pl.program_id(0)
        mine = own_refs[0][...]
        for k in range(1, nl):
            mine = jnp.where(layer == k, own_refs[k][...], mine)
        g = None
        for s in range(N_DEV):
            term = jnp.where(me_ref[0] == s, mine, p_ref[s]).astype(F32)
            g = term if g is None else g + term
        g_ref[...] = g
        d_ref[...], mo_ref[...], vo_ref[...] = _adamw(w_ref[...], g, m_ref[...], v_ref[...])

    blk = pl.BlockSpec((None, ta, b), lambda l, i, me_ref: (l, i, 0))

    def own_spec(k):
        return pl.BlockSpec((None, ta, b), lambda l, i, me_ref: (me_ref[0], jnp.where(l == k, i, 0), 0))

    grid_spec = pltpu.PrefetchScalarGridSpec(
        num_scalar_prefetch=1, grid=(nl, a // ta),
        in_specs=[pl.BlockSpec((N_DEV, None, ta, b), lambda l, i, me_ref: (0, l, i, 0))]
        + [own_spec(k) for k in range(nl)] + [blk, blk, blk],
        out_specs=[blk] * 4)
    return pl.pallas_call(body, grid_spec=grid_spec, out_shape=[S(w.shape, F32)] * 4,
                          compiler_params=_cp("arbitrary", "arbitrary"), name=name)(me, parts, *own, w, m, v)


def _adam_small(g, w, m, v, name):
    def body(g_ref, w_ref, m_ref, v_ref, d_ref, mo_ref, vo_ref):
        d_ref[...], mo_ref[...], vo_ref[...] = _adamw(w_ref[...], g_ref[...], m_ref[...], v_ref[...])

    return pl.pallas_call(body, out_shape=[S(g.shape, F32)] * 3, compiler_params=_cp(), name=name)(g, w, m, v)


BIG = ("even_w_in", "even_w_out", "mla_w_down", "mla_w_qb", "mla_w_kvb", "mla_w_o", "mlp_w1", "mlp_w2")
BIG_KEY = dict(even_w_in="win", even_w_out="wout", mla_w_down="wdown", mla_w_qb="wqb", mla_w_kvb="wkvb",
               mla_w_o="wo", mlp_w1="w1", mlp_w2="w2")
SMALL = (("ln_mix_g", "ln_mix_g", None), ("ln_mix_b", "ln_mix_b", None), ("ln_ffn_g", "ln_ffn_g", None),
         ("ln_ffn_b", "ln_ffn_b", None), ("pool_w", "pool_w", None), ("pool_scale", "pool_scale", None),
         ("lru_conv_w", "conv_w", 2), ("lru_conv_b", "conv_b", None), ("lru_w_a", "w_a", None),
         ("lru_b_a", "b_a", None), ("lru_w_x", "w_x", None), ("lru_b_x", "b_x", None), ("lru_lambda", "lam", None),
         ("mla_q_norm_g", "gq", 1), ("mla_kv_norm_g", "gkv", 1))
WEIGHTS = ("ln_mix_g", "ln_mix_b", "ln_ffn_g", "ln_ffn_b", "even_w_in", "pool_w", "pool_scale", "lru_conv_w",
           "lru_conv_b", "lru_w_a", "lru_b_a", "lru_w_x", "lru_b_x", "lru_lambda", "even_w_out", "mla_w_down",
           "mla_q_norm_g", "mla_kv_norm_g", "mla_w_qb", "mla_w_kvb", "mla_w_o", "mlp_w1", "mlp_w2")
ALL_AXES = ("x", "y", "c")


def _layer_weights(l):
    j = l // 2
    if l % 2 == 0:
        mixer = [("win", "even_w_in", j), ("wout", "even_w_out", j)]
    else:
        mixer = [("wdown", "mla_w_down", j), ("wqb", "mla_w_qb", j), ("wkvb", "mla_w_kvb", j), ("wo", "mla_w_o", j)]
    return mixer + [("w1", "mlp_w1", l), ("w2", "mlp_w2", l)]


def _pack(arrays, multiple):
    flat = jnp.concatenate([a.reshape(-1) for a in arrays])
    pad = (-flat.shape[0]) % multiple
    return jnp.pad(flat, (0, pad))


def _unpack(flat, shapes):
    out, at = [], 0
    for shp in shapes:
        n = 1
        for s in shp:
            n *= s
        out.append(flat[at:at + n].reshape(shp))
        at += n
    return out


def _global_shape(local_shape, axis):
    if axis is None:
        return tuple(local_shape)
    return tuple(s * N_DEV if i == axis else s for i, s in enumerate(local_shape))


def _step(x, positions, tgt, w, m, v):
    t = x.shape[1]
    me = _index(_mesh_place())

    sharded = [(name, axis) for name, _, axis in SMALL if axis is not None]
    zeros_with_mine = [lax.dynamic_update_slice_in_dim(jnp.zeros(_global_shape(w[name].shape, axis), F32), w[name],
                                                       me * w[name].shape[axis], axis) for name, axis in sharded]
    chunk = N_DEV * 8 * 128
    gathered = _all_reduce_small(_pack(zeros_with_mine, chunk).reshape(N_DEV, -1, 128), "gather_small")
    full = dict(zip([name for name, _ in sharded],
                    _unpack(gathered.reshape(-1), [_global_shape(w[name].shape, axis) for name, axis in sharded])))

    def keys_of(l, part):
        keys = [key for key, _, _ in _layer_weights(l)]
        if l == 0:
            return keys[:1] if part == 0 else keys[1:]
        return keys if part == 0 else []

    shard_of = {(l, key): (w[name][i].T if key == "win" else w[name][i])
                for l in range(DEPTH) for key, name, i in _layer_weights(l)}
    me_arr = me.astype(jnp.int32).reshape(1)
    first = _all_gather_big(_make_zones([shard_of[0, key] for key in keys_of(0, 0)], me_arr, "zones_0_0"))
    flights, after = {}, (first[0], gathered)
    for l in range(DEPTH):
        for part in (0, 1):
            if (l, part) != (0, 0) and keys_of(l, part):
                zones = _make_zones([shard_of[l, key] for key in keys_of(l, part)], me_arr, "zones_%d_%d" % (l, part))
                send, recv, _, lands, token = _exchange_start([], zones, GATHER_ICI, "gather_start_%d_%d" % (l, part),
                                                              after=after)
                flights[l, part] = (send, recv, [], lands)
                after = (token,)

    passing = {}

    def pass_on(l, part, after):
        tag = "%d_%d" % (l, part)
        _, lands = _exchange_wait(*flights[l, part], GATHER_ICI, after, "gather_wait_" + tag)
        send, recv, _, lands, token = _exchange_start([], lands, GATHER_D2D, "gather_pass_" + tag)
        passing[l, part] = (send, recv, [], lands)
        return token

    def early_pass(l, after):
        return pass_on(l, 0, after) if l >= 2 else None

    def weights_of(l, part, after):
        keys = keys_of(l, part)
        if (l, part) == (0, 0):
            arrays = first
        elif keys:
            if (l, part) not in passing:
                pass_on(l, part, after)
            _, arrays = _exchange_wait(*passing[l, part], GATHER_D2D, after, "gather_pass_wait_%d_%d" % (l, part))
        big = dict(zip(keys, arrays)) if keys else {}
        if "win" in big:
            big["win_t"] = big["win"].reshape(EVEN_IN, D)
        if "wout" in big:
            big["wout2d"] = big["wout"].reshape(EVEN_MIX, D)
        if "wdown" in big:
            big["wdown2d"] = big["wdown"].reshape(D, ODD_IN)
        return big

    zone = {name: lax.empty((N_DEV,) + w[name].shape, BF16) for name in BIG}
    name_of = {key: name for name, key in BIG_KEY.items()}
    sent, last_token = [], [None]

    def grads_done(l, grads):
        keys = list(grads)
        index = {key: i for key, _, i in _layer_weights(l)}
        layers = [index[key] for key in keys]
        send, recv, srcs, lands, tok = _exchange_start([grads[k] for k in keys], [zone[name_of[k]] for k in keys],
                                                       _scatter_plan(layers), "scatter_start_%d_%s" % (l, keys[0]))
        for k, land in zip(keys, lands):
            zone[name_of[k]] = land
        sent.append((send, recv, srcs, keys, layers))
        last_token[0] = tok
        return tok

    row3 = lambda a: a.reshape(a.shape[0], 1, a.shape[1])
    small = dict(ln_mix_g=row3(w["ln_mix_g"]), ln_mix_b=row3(w["ln_mix_b"]), ln_ffn_g=row3(w["ln_ffn_g"]),
                 ln_ffn_b=row3(w["ln_ffn_b"]), pool_w=w["pool_w"], pool_scale=row3(w["pool_scale"]),
                 conv_w=full["lru_conv_w"], conv_b=row3(w["lru_conv_b"]), w_a=w["lru_w_a"], b_a=row3(w["lru_b_a"]),
                 w_x=w["lru_w_x"], b_x=row3(w["lru_b_x"]), lam=row3(w["lru_lambda"]),
                 gq=row3(full["mla_q_norm_g"]), gkv=row3(full["mla_kv_norm_g"]))

    loss_part, grad_x, g = _local_step(x[0], positions.reshape(t, 1), tgt[0], small, weights_of, grads_done,
                                       start_dep=token, prefetch=early_pass)

    own = {name: [None] * w[name].shape[0] for name in BIG}
    me_arr = me.astype(jnp.int32).reshape(1)
    out = {}
    local_g = [jnp.stack(g[key]).reshape(_global_shape(w[name].shape, axis)) for name, key, axis in SMALL]
    local_g.append(loss_part.reshape(1))
    part = _pack(local_g, chunk).reshape(N_DEV, -1, 128)
    small_plan = _scatter_plan([None])
    s_send, s_recv, s_src, s_land, after = _exchange_start([part], [lax.empty(part.shape, F32)], small_plan,
                                                           "small_scatter_start", after=(last_token[0],))
    for n_flight, (send, recv, srcs, keys, layers) in enumerate(sent):
        if n_flight == len(sent) - 1:
            for name in BIG:
                if BIG_KEY[name] not in keys:
                    out[name] = _adam_big(zone[name], own[name], me_arr, w[name], m[name], v[name], "adam_" + name)
            s_src, s_land = _exchange_wait(s_send, s_recv, s_src, s_land, small_plan, [o[0] for o in out.values()],
                                           "small_scatter_wait")
            chunk_sum = _sum_blocks(s_land[0], s_src[0], me_arr)
            r_zone = lax.dynamic_update_slice_in_dim(lax.empty(part.shape, F32), chunk_sum[None], me, 0)
            r_send, r_recv, _, r_land, after = _exchange_start([], [r_zone], GATHER_ALL, "small_gather_start")
        srcs, lands = _exchange_wait(send, recv, srcs, [zone[name_of[k]] for k in keys], _scatter_plan(layers),
                                     after, "scatter_wait_%d" % n_flight)
        for k, land, src, layer in zip(keys, lands, srcs, layers):
            zone[name_of[k]] = land
            own[name_of[k]][layer] = src
        after = lands[0]
    for name in BIG:
        if name not in out:
            out[name] = _adam_big(zone[name], own[name], me_arr, w[name], m[name], v[name], "adam_" + name)

    _, reduced = _exchange_wait(r_send, r_recv, [], r_land, GATHER_ALL, [out[name][0] for name in BIG],
                                "small_gather_wait")
    reduced = _unpack(reduced[0].reshape(-1), [a.shape for a in local_g])
    loss = reduced[-1][0]
    mine = [a if axis is None else lax.dynamic_slice_in_dim(a, me * w[name].shape[axis], w[name].shape[axis], axis)
            for a, (name, _, axis) in zip(reduced, SMALL)]
    for grad, (name, _, _) in zip(mine, SMALL):
        shape = w[name].shape
        as_2d = lambda a: a.reshape(-1, shape[-1])
        new = _adam_small(as_2d(grad), as_2d(w[name]), as_2d(m[name]), as_2d(v[name]), "adam_" + name)
        out[name] = (grad,) + tuple(a.reshape(shape) for a in new)

    return (loss, grad_x[None]) + tuple(out[name][i] for i in range(4) for name in WEIGHTS)


def kernel(x, positions, ln_mix_g, ln_mix_b, ln_ffn_g, ln_ffn_b, even_w_in, pool_w, pool_scale, lru_conv_w, lru_conv_b, lru_w_a, lru_b_a, lru_w_x, lru_b_x, lru_lambda, even_w_out, mla_w_down, mla_q_norm_g, mla_kv_norm_g, mla_w_qb, mla_w_kvb, mla_w_o, mlp_w1, mlp_w2, loss_target, m_ln_mix_g, m_ln_mix_b, m_ln_ffn_g, m_ln_ffn_b, m_even_w_in, m_pool_w, m_pool_scale, m_lru_conv_w, m_lru_conv_b, m_lru_w_a, m_lru_b_a, m_lru_w_x, m_lru_b_x, m_lru_lambda, m_even_w_out, m_mla_w_down, m_mla_q_norm_g, m_mla_kv_norm_g, m_mla_w_qb, m_mla_w_kvb, m_mla_w_o, m_mlp_w1, m_mlp_w2, v_ln_mix_g, v_ln_mix_b, v_ln_ffn_g, v_ln_ffn_b, v_even_w_in, v_pool_w, v_pool_scale, v_lru_conv_w, v_lru_conv_b, v_lru_w_a, v_lru_b_a, v_lru_w_x, v_lru_b_x, v_lru_lambda, v_even_w_out, v_mla_w_down, v_mla_q_norm_g, v_mla_kv_norm_g, v_mla_w_qb, v_mla_w_kvb, v_mla_w_o, v_mlp_w1, v_mlp_w2):
    w = dict(zip(WEIGHTS, (ln_mix_g, ln_mix_b, ln_ffn_g, ln_ffn_b, even_w_in, pool_w, pool_scale, lru_conv_w,
                           lru_conv_b, lru_w_a, lru_b_a, lru_w_x, lru_b_x, lru_lambda, even_w_out, mla_w_down,
                           mla_q_norm_g, mla_kv_norm_g, mla_w_qb, mla_w_kvb, mla_w_o, mlp_w1, mlp_w2)))
    m = dict(zip(WEIGHTS, (m_ln_mix_g, m_ln_mix_b, m_ln_ffn_g, m_ln_ffn_b, m_even_w_in, m_pool_w, m_pool_scale,
                           m_lru_conv_w, m_lru_conv_b, m_lru_w_a, m_lru_b_a, m_lru_w_x, m_lru_b_x, m_lru_lambda,
                           m_even_w_out, m_mla_w_down, m_mla_q_norm_g, m_mla_kv_norm_g, m_mla_w_qb, m_mla_w_kvb,
                           m_mla_w_o, m_mlp_w1, m_mlp_w2)))
    v = dict(zip(WEIGHTS, (v_ln_mix_g, v_ln_mix_b, v_ln_ffn_g, v_ln_ffn_b, v_even_w_in, v_pool_w, v_pool_scale,
                           v_lru_conv_w, v_lru_conv_b, v_lru_w_a, v_lru_b_a, v_lru_w_x, v_lru_b_x, v_lru_lambda,
                           v_even_w_out, v_mla_w_down, v_mla_q_norm_g, v_mla_kv_norm_g, v_mla_w_qb, v_mla_w_kvb,
                           v_mla_w_o, v_mlp_w1, v_mlp_w2)))
    return _step(x, positions, loss_target, w, m, v)
```

```python
import functools

import jax
import jax.numpy as jnp
from jax import lax
from jax.experimental import pallas as pl
from jax.experimental.pallas import tpu as pltpu

F32 = jnp.float32
BF16 = jnp.bfloat16
S = jax.ShapeDtypeStruct

D = 1024
DEPTH = 4
N_DEV = 8
CHUNK_SHIFT = 6
POOL_WINDOWS = (2, 4, 8, 16)
POOL_W = 512
LRU_W = 1024
LRU_HEADS = 8
HEAD = 128
LRU_C = 8.0
EVEN_IN = 2560
EVEN_MIX = 1536
MLA_HEADS = 8
NOPE = 128
ROPE = 64
VDIM = 128
Q_RANK = 384
KV_RANK = 256
ODD_IN = 704
D_FF = 4096
FF_BLK = D_FF // N_DEV
ROPE_THETA = 10000.0
ALPHA = (2 * DEPTH) ** 0.25
LN_EPS = 1e-5
RMS_EPS = 1e-6
ATT_SCALE = (NOPE + ROPE) ** -0.5
NEG = float(jnp.finfo(jnp.float32).min)
ADAM_LR = 0.001
ADAM_B1 = 0.9
ADAM_B2 = 0.999
ADAM_EPS = 1e-08
ADAM_WD = 0.01
ADAM_STEP = 10
V7X_VMEM_BYTES = 64 * 1024 * 1024
VMEM_LIMIT = V7X_VMEM_BYTES - 8 * 1024 * 1024
MESH = pl.DeviceIdType.MESH


def _cp(*sem):
    return pltpu.CompilerParams(dimension_semantics=sem if sem else None, vmem_limit_bytes=VMEM_LIMIT)


def _dot(a, b):
    return jnp.dot(a, b, preferred_element_type=F32)


def _dot_nt(a, b):
    return lax.dot_general(a, b, (((1,), (1,)), ((), ())), preferred_element_type=F32)


def _dot_tn(a, b):
    return lax.dot_general(a, b, (((0,), (0,)), ((), ())), preferred_element_type=F32)


def _full(shape):
    return pl.BlockSpec(shape, lambda *_: (0,) * len(shape))


def _mm(a, b, *, mode, grid, a_spec, b_spec, out_shape, out_spec, name, add=None, add_spec=None, add_scale=1.0,
        dep=None):
    dot = {"nn": _dot, "nt": _dot_nt, "tn": _dot_tn}[mode]

    def body(*refs):
        a_ref, b_ref, o_ref = refs[0], refs[1], refs[-1]
        acc = dot(a_ref[...].astype(BF16), b_ref[...].astype(BF16))
        if add is not None:
            acc = acc + add_scale * refs[2][...]
        o_ref[...] = acc.astype(o_ref.dtype)

    ops = [a, b] if add is None else [a, b, add]
    specs = [a_spec, b_spec] if add is None else [a_spec, b_spec, add_spec]
    if dep is not None:
        ops.append(dep)
        specs.append(pl.BlockSpec(memory_space=pl.ANY))
    return pl.pallas_call(body, grid=grid, in_specs=specs, out_specs=out_spec, out_shape=out_shape,
                          compiler_params=_cp(*(("parallel",) * len(grid))), name=name)(*ops)


def _ln_stats(z):
    mu = jnp.mean(z, axis=-1, keepdims=True)
    zc = z - mu
    var = jnp.mean(zc * zc, axis=-1, keepdims=True)
    rstd = lax.rsqrt(var + LN_EPS)
    return zc * rstd, rstd


def _row_tile(t):
    return min(512, t)


def _resid_ln(x, mix, g3, b3, l, name):
    t = x.shape[0]
    bm = _row_tile(t)

    def body(x_ref, m_ref, g_ref, b_ref, z_ref, y_ref, yb_ref):
        z = ALPHA * x_ref[...] + m_ref[...]
        xh, _ = _ln_stats(z)
        y = xh * g_ref[...] + b_ref[...]
        z_ref[...] = z
        y_ref[...] = y
        yb_ref[...] = y.astype(BF16)

    row = pl.BlockSpec((bm, D), lambda i: (i, 0))
    vec = pl.BlockSpec((None, 1, D), lambda i: (l, 0, 0))
    return pl.pallas_call(body, grid=(t // bm,), in_specs=[row, row, vec, vec], out_specs=[row, row, row],
                          out_shape=[S((t, D), F32), S((t, D), F32), S((t, D), BF16)],
                          compiler_params=_cp("parallel"), name=name)(x, mix, g3, b3)


def _proj_resid_ln(x, a, wmat, g3, b3, l, name):
    t, k = a.shape
    bm = _row_tile(t)

    def body(x_ref, a_ref, w_ref, g_ref, b_ref, z_ref, y_ref, yb_ref):
        z = ALPHA * x_ref[...] + _dot(a_ref[...], w_ref[...])
        xh, _ = _ln_stats(z)
        y = xh * g_ref[...] + b_ref[...]
        z_ref[...] = z
        y_ref[...] = y
        yb_ref[...] = y.astype(BF16)

    row = pl.BlockSpec((bm, D), lambda i: (i, 0))
    vec = pl.BlockSpec((None, 1, D), lambda i: (l, 0, 0))
    return pl.pallas_call(body, grid=(t // bm,),
                          in_specs=[row, pl.BlockSpec((bm, k), lambda i: (i, 0)), _full((k, D)), vec, vec],
                          out_specs=[row, row, row], out_shape=[S((t, D), F32), S((t, D), F32), S((t, D), BF16)],
                          compiler_params=_cp("parallel"), name=name)(x, a, wmat, g3, b3)


def _ln_bwd(d, z, g3, l, name, r=None, dep=None):
    t = z.shape[0]
    bm = _row_tile(t)

    def body(*refs):
        refs = list(refs)
        d_ref = refs.pop(0)
        dy = d_ref[...]
        if r is not None:
            dy = dy + ALPHA * refs.pop(0)[...]
        z_ref, g_ref = refs.pop(0), refs.pop(0)
        if dep is not None:
            refs.pop(0)
        dz_ref, dzb_ref, dg_ref, db_ref = refs
        xh, rstd = _ln_stats(z_ref[...])
        dyg = dy * g_ref[...]
        m1 = jnp.mean(dyg, axis=-1, keepdims=True)
        m2 = jnp.mean(dyg * xh, axis=-1, keepdims=True)
        dz = rstd * (dyg - m1 - xh * m2)
        dz_ref[...] = dz
        dzb_ref[...] = dz.astype(BF16)

        @pl.when(pl.program_id(0) == 0)
        def _():
            dg_ref[...] = jnp.zeros_like(dg_ref)
            db_ref[...] = jnp.zeros_like(db_ref)

        dg_ref[...] += jnp.sum(dy * xh, axis=0, keepdims=True)
        db_ref[...] += jnp.sum(dy, axis=0, keepdims=True)

    row = pl.BlockSpec((bm, D), lambda i: (i, 0))
    vec = pl.BlockSpec((None, 1, D), lambda i: (l, 0, 0))
    acc = pl.BlockSpec((1, D), lambda i: (0, 0))
    ops = [d, z, g3] if r is None else [d, r, z, g3]
    specs = [row, row, vec] if r is None else [row, row, row, vec]
    if dep is not None:
        ops.append(dep)
        specs.append(_full(dep.shape))
    return pl.pallas_call(body, grid=(t // bm,), in_specs=specs, out_specs=[row, row, acc, acc],
                          out_shape=[S((t, D), F32), S((t, D), BF16), S((1, D), F32), S((1, D), F32)],
                          compiler_params=_cp("arbitrary"), name=name)(*ops)


def _loss_grad(y, tgt):
    t = y.shape[0]
    bm = _row_tile(t)

    def body(y_ref, t_ref, dy_ref, loss_ref, acc_ref):
        i = pl.program_id(0)
        e = y_ref[...] - t_ref[...]
        dy_ref[...] = e * (1.0 / D)

        @pl.when(i == 0)
        def _():
            acc_ref[...] = jnp.zeros_like(acc_ref)

        acc_ref[...] += jnp.sum(e * e, axis=0, keepdims=True)

        @pl.when(i == pl.num_programs(0) - 1)
        def _():
            loss_ref[...] = jnp.full(loss_ref.shape, (0.5 / D) * jnp.sum(acc_ref[...]), F32)

    row = pl.BlockSpec((bm, D), lambda i: (i, 0))
    return pl.pallas_call(body, grid=(t // bm,), in_specs=[row, row],
                          out_specs=[row, pl.BlockSpec((1, 128), lambda i: (0, 0))],
                          out_shape=[S((t, D), F32), S((1, 128), F32)],
                          scratch_shapes=[pltpu.VMEM((1, D), F32)],
                          compiler_params=_cp("arbitrary"), name="loss_grad")(y, tgt)


def _mlp_row_tile(t):
    return min(1024, t)


def _mlp_fwd(y, yb, w1g, w2g, g3, b3, l, dep=None):
    t = yb.shape[0]
    bm = _mlp_row_tile(t)

    def body(*refs):
        y_ref, yb_ref, w1_ref, w2_ref, g_ref, b_ref = refs[:6]
        z_ref, o_ref, ob_ref, act_ref, acc_ref = refs[-5:]
        j = pl.program_id(1)
        h = jnp.maximum(_dot(yb_ref[...], w1_ref[...]), 0.0)
        act = (h * h).astype(BF16)
        act_ref[...] = act
        c = _dot(act, w2_ref[...])

        @pl.when(j == 0)
        def _():
            acc_ref[...] = c

        @pl.when(j > 0)
        def _():
            acc_ref[...] += c

        @pl.when(j == N_DEV - 1)
        def _():
            z = ALPHA * y_ref[...] + acc_ref[...]
            xh, _ = _ln_stats(z)
            out = xh * g_ref[...] + b_ref[...]
            z_ref[...] = z
            o_ref[...] = out
            ob_ref[...] = out.astype(BF16)

    row = pl.BlockSpec((bm, D), lambda i, j: (i, 0))
    vec = pl.BlockSpec((None, 1, D), lambda i, j: (l, 0, 0))
    deps = [] if dep is None else [dep]
    return pl.pallas_call(
        body, grid=(t // bm, N_DEV),
        in_specs=[row, row, pl.BlockSpec((None, D, FF_BLK), lambda i, j: (j, 0, 0)),
                  pl.BlockSpec((None, FF_BLK, D), lambda i, j: (j, 0, 0)), vec, vec] + [ANY] * len(deps),
        out_specs=[row, row, row, pl.BlockSpec((bm, FF_BLK), lambda i, j: (i, j))],
        out_shape=[S((t, D), F32), S((t, D), F32), S((t, D), BF16), S((t, D_FF), BF16)],
        scratch_shapes=[pltpu.VMEM((bm, D), F32)],
        compiler_params=_cp("parallel", "arbitrary"), name="mlp_fwd")(y, yb, w1g, w2g, g3, b3, *deps)


def _mlp_bwd_dh(act, dzb, w1g, w2g):
    t = act.shape[0]
    bm = _mlp_row_tile(t)

    def body(a_ref, dz_ref, w1_ref, w2_ref, dh_ref, acc_ref):
        j = pl.program_id(1)
        r = jnp.sqrt(a_ref[...].astype(F32))
        da = _dot_nt(dz_ref[...], w2_ref[...])
        dh = (da * (2.0 * r)).astype(BF16)
        dh_ref[...] = dh
        c = _dot_nt(dh, w1_ref[...])

        @pl.when(j == 0)
        def _():
            acc_ref[...] = c

        @pl.when(j > 0)
        def _():
            acc_ref[...] += c

    row = pl.BlockSpec((bm, D), lambda i, j: (i, 0))
    hid = pl.BlockSpec((bm, FF_BLK), lambda i, j: (i, j))
    return pl.pallas_call(
        body, grid=(t // bm, N_DEV),
        in_specs=[hid, row,
                  pl.BlockSpec((None, D, FF_BLK), lambda i, j: (j, 0, 0)),
                  pl.BlockSpec((None, FF_BLK, D), lambda i, j: (j, 0, 0))],
        out_specs=[hid, row],
        out_shape=[S((t, D_FF), BF16), S((t, D), F32)],
        compiler_params=_cp("parallel", "arbitrary"), name="mlp_bwd_dh")(act, dzb, w1g, w2g)


def _shift_dn(x, k, rows, fill=0.0):
    return jnp.where(rows >= k, pltpu.roll(x, k, 0), fill)


def _shift_up(x, k, rows, fill=0.0):
    t = x.shape[0]
    return jnp.where(rows < t - k, pltpu.roll(x, t - k, 0), fill)


def _scan_rows(a, b, shift):
    rows = lax.broadcasted_iota(jnp.int32, a.shape, 0)
    k = 1
    t = a.shape[0]
    while k < t:
        b = a * shift(b, k, rows) + b
        if 2 * k < t:
            a = a * shift(a, k, rows, 1.0)
        k *= 2
    return b


def _scan_dn(a, b):
    return _scan_rows(a, b, _shift_dn)


def _scan_up(a, b):
    return _scan_rows(a, b, _shift_up)


def _window_sum_dn(x, w, rows):
    k = 1
    while k < w:
        x = x + _shift_dn(x, k, rows)
        k *= 2
    return x


def _window_sum_up(x, w, rows):
    k = 1
    while k < w:
        x = x + _shift_up(x, k, rows)
        k *= 2
    return x


def _pool_diff(u, w, rows):
    inv_count = 1.0 / jnp.minimum(rows + 1, w).astype(F32)
    return _window_sum_dn(u, w, rows) * inv_count - u, inv_count


def _pool_fwd(proj, pool_w, pool_scale3, j):
    t = proj.shape[0]

    def body(u_ref, w_ref, s_ref, y_ref):
        rows = lax.broadcasted_iota(jnp.int32, (t, HEAD), 0)
        for g, w in enumerate(POOL_WINDOWS):
            cols = slice(g * HEAD, (g + 1) * HEAD)
            d, _ = _pool_diff(u_ref[:, cols], w, rows)
            y = _dot(d.astype(BF16), w_ref[g].astype(BF16)) * s_ref[:, cols]
            y_ref[:, cols] = y.astype(BF16)

    return pl.pallas_call(
        body, grid=(1,),
        in_specs=[pl.BlockSpec((t, POOL_W), lambda i: (0, 0)),
                  pl.BlockSpec((None, 4, HEAD, HEAD), lambda i: (j, 0, 0, 0)),
                  pl.BlockSpec((None, 1, POOL_W), lambda i: (j, 0, 0))],
        out_specs=pl.BlockSpec((t, POOL_W), lambda i: (0, 0)),
        out_shape=S((t, POOL_W), BF16), compiler_params=_cp("arbitrary"), name="pool_fwd")(proj, pool_w, pool_scale3)


def _pool_bwd(proj, dycat, pool_w, pool_scale3, j):
    t = proj.shape[0]

    def body(u_ref, dy_ref, w_ref, s_ref, du_ref, dw_ref, ds_ref):
        rows = lax.broadcasted_iota(jnp.int32, (t, HEAD), 0)
        for g, w in enumerate(POOL_WINDOWS):
            cols = slice(g * HEAD, (g + 1) * HEAD)
            d, inv_count = _pool_diff(u_ref[:, cols], w, rows)
            db = d.astype(BF16)
            wg = w_ref[g].astype(BF16)
            dy = dy_ref[:, cols]
            ds_ref[:, cols] = jnp.sum(dy * _dot(db, wg), axis=0, keepdims=True)
            dzz = (dy * s_ref[:, cols]).astype(BF16)
            dw_ref[g] = _dot_tn(db, dzz)
            dd = _dot_nt(dzz, wg)
            du_ref[:, cols] = (_window_sum_up(dd * inv_count, w, rows) - dd).astype(BF16)

    return pl.pallas_call(
        body, grid=(1,),
        in_specs=[pl.BlockSpec((t, POOL_W), lambda i: (0, 0)),
                  pl.BlockSpec((t, POOL_W), lambda i: (0, 0)),
                  pl.BlockSpec((None, 4, HEAD, HEAD), lambda i: (j, 0, 0, 0)),
                  pl.BlockSpec((None, 1, POOL_W), lambda i: (j, 0, 0))],
        out_specs=[pl.BlockSpec((t, POOL_W), lambda i: (0, 0)), _full((4, HEAD, HEAD)), _full((1, POOL_W))],
        out_shape=[S((t, POOL_W), BF16), S((4, HEAD, HEAD), F32), S((1, POOL_W), F32)],
        compiler_params=_cp("arbitrary"), name="pool_bwd")(proj, dycat, pool_w, pool_scale3)


GELU_C = 0.7978845608028654
GELU_K = 0.044715


def _gelu(x):
    th = jnp.tanh(GELU_C * (x + GELU_K * x * x * x))
    return 0.5 * x * (1.0 + th), th


def _lru_forward(u, gate, cw, cb, wa, ba, wx, bx, lam, rows):
    v = cw[3:4] * u + cw[2:3] * _shift_dn(u, 1, rows) + cw[1:2] * _shift_dn(u, 2, rows) \
        + cw[0:1] * _shift_dn(u, 3, rows) + cb
    vb = v.astype(BF16)
    r = jax.nn.sigmoid(_dot(vb, wa) + ba)
    i = jax.nn.sigmoid(_dot(vb, wx) + bx)
    sp = jnp.maximum(-lam, 0.0) + jnp.log1p(jnp.exp(-jnp.abs(lam)))
    log_a = (-LRU_C) * r * sp
    a = jnp.exp(log_a)
    one_m_a2 = -jnp.tanh(log_a) * (a * a + 1.0)
    mult = jnp.sqrt(one_m_a2)
    h = _scan_dn(a, mult * (i * v))
    gl, th = _gelu(gate)
    return dict(v=v, vb=vb, r=r, i=i, sp=sp, a=a, mult=mult, h=h, gl=gl, th=th)


def _lru_specs(t, j, col0_u, col0_g):
    blk = lambda c0: pl.BlockSpec((t, HEAD), lambda h: (0, c0 + h))
    vec = pl.BlockSpec((None, 1, HEAD), lambda h: (j, 0, h))
    return [blk(col0_u), blk(col0_g),
            pl.BlockSpec((None, 4, HEAD), lambda h: (j, 0, h)), vec,
            pl.BlockSpec((None, None, HEAD, HEAD), lambda h: (j, h, 0, 0)), vec,
            pl.BlockSpec((None, None, HEAD, HEAD), lambda h: (j, h, 0, 0)), vec, vec]


def _lru_fwd(proj, p, j):
    t = proj.shape[0]

    def body(u_ref, g_ref, cw_ref, cb_ref, wa_ref, ba_ref, wx_ref, bx_ref, lam_ref, y_ref):
        rows = lax.broadcasted_iota(jnp.int32, (t, HEAD), 0)
        f = _lru_forward(u_ref[...], g_ref[...], cw_ref[...], cb_ref[...], wa_ref[...].astype(BF16), ba_ref[...],
                         wx_ref[...].astype(BF16), bx_ref[...], lam_ref[...], rows)
        y_ref[...] = (f["h"] * f["gl"]).astype(BF16)

    return pl.pallas_call(
        body, grid=(LRU_HEADS,), in_specs=_lru_specs(t, j, POOL_W // HEAD, (POOL_W + LRU_W) // HEAD),
        out_specs=pl.BlockSpec((t, HEAD), lambda h: (0, h)), out_shape=S((t, LRU_W), BF16),
        compiler_params=_cp("parallel"), name="lru_fwd")(
            proj, proj, p["conv_w"], p["conv_b"], p["w_a"], p["b_a"], p["w_x"], p["b_x"], p["lam"])


def _lru_bwd(proj, dycat, p, j):
    t = proj.shape[0]

    def body(u_ref, g_ref, cw_ref, cb_ref, wa_ref, ba_ref, wx_ref, bx_ref, lam_ref, dy_ref,
             du_ref, dgate_ref, dcw_ref, dcb_ref, dwa_ref, dba_ref, dwx_ref, dbx_ref, dlam_ref):
        rows = lax.broadcasted_iota(jnp.int32, (t, HEAD), 0)
        u = u_ref[...]
        gate = g_ref[...]
        cw = cw_ref[...]
        wa = wa_ref[...].astype(BF16)
        wx = wx_ref[...].astype(BF16)
        lam = lam_ref[...]
        f = _lru_forward(u, gate, cw, cb_ref[...], wa, ba_ref[...], wx, bx_ref[...], lam, rows)
        v, r, i, a, mult, h, th = f["v"], f["r"], f["i"], f["a"], f["mult"], f["h"], f["th"]
        dy = dy_ref[...]
        dgl = 0.5 * (1.0 + th) + 0.5 * gate * (1.0 - th * th) * GELU_C * (1.0 + 3.0 * GELU_K * gate * gate)
        dgate_ref[...] = (dy * h * dgl).astype(BF16)
        g = _scan_up(_shift_up(a, 1, rows), dy * f["gl"])
        da = g * _shift_dn(h, 1, rows)
        iv = i * v
        dmult = g * iv
        di = g * mult * v
        dv = g * mult * i
        dlog_a = da * a - dmult * (a * a) / mult
        dr = dlog_a * (-LRU_C) * f["sp"]
        dsp = jnp.sum(dlog_a * (-LRU_C) * r, axis=0, keepdims=True)
        dlam_ref[...] = -dsp * jax.nn.sigmoid(-lam)
        dpa = dr * r * (1.0 - r)
        dpx = di * i * (1.0 - i)
        dpab = dpa.astype(BF16)
        dpxb = dpx.astype(BF16)
        dwa_ref[...] = _dot_tn(f["vb"], dpab)
        dwx_ref[...] = _dot_tn(f["vb"], dpxb)
        dba_ref[...] = jnp.sum(dpa, axis=0, keepdims=True)
        dbx_ref[...] = jnp.sum(dpx, axis=0, keepdims=True)
        dv = dv + _dot_nt(dpab, wa) + _dot_nt(dpxb, wx)
        dcb_ref[...] = jnp.sum(dv, axis=0, keepdims=True)
        du = cw[3:4] * dv
        dcw_ref[3:4, :] = jnp.sum(dv * u, axis=0, keepdims=True)
        for k in (1, 2, 3):
            du = du + cw[3 - k:4 - k] * _shift_up(dv, k, rows)
            dcw_ref[3 - k:4 - k, :] = jnp.sum(dv * _shift_dn(u, k, rows), axis=0, keepdims=True)
        du_ref[...] = du.astype(BF16)

    blk = pl.BlockSpec((t, HEAD), lambda h: (0, h))
    vec = pl.BlockSpec((1, HEAD), lambda h: (0, h))
    mat = pl.BlockSpec((None, HEAD, HEAD), lambda h: (h, 0, 0))
    return pl.pallas_call(
        body, grid=(LRU_HEADS,),
        in_specs=_lru_specs(t, j, POOL_W // HEAD, (POOL_W + LRU_W) // HEAD)
        + [pl.BlockSpec((t, HEAD), lambda h: (0, POOL_W // HEAD + h))],
        out_specs=[blk, blk, pl.BlockSpec((4, HEAD), lambda h: (0, h)), vec, mat, vec, mat, vec, vec],
        out_shape=[S((t, LRU_W), BF16), S((t, LRU_W), BF16), S((4, LRU_W), F32), S((1, LRU_W), F32),
                   S((LRU_HEADS, HEAD, HEAD), F32), S((1, LRU_W), F32),
                   S((LRU_HEADS, HEAD, HEAD), F32), S((1, LRU_W), F32), S((1, LRU_W), F32)],
        compiler_params=_cp("parallel"), name="lru_bwd")(
            proj, proj, p["conv_w"], p["conv_b"], p["w_a"], p["b_a"], p["w_x"], p["b_x"], p["lam"], dycat)


def _rope(x, c, s):
    x1 = x[:, :ROPE // 2]
    x2 = x[:, ROPE // 2:]
    return jnp.concatenate([x1 * c - x2 * s, x1 * s + x2 * c], axis=-1)


def _rope_t(d, c, s):
    d1 = d[:, :ROPE // 2]
    d2 = d[:, ROPE // 2:]
    return jnp.concatenate([d1 * c + d2 * s, d2 * c - d1 * s], axis=-1)


def _rope_tables(pos2, inv_freq):
    t = pos2.shape[0]

    def body(p_ref, f_ref, c_ref, s_ref):
        ang = p_ref[...].astype(F32) * f_ref[...]
        c_ref[...] = jnp.cos(ang)
        s_ref[...] = jnp.sin(ang)

    return pl.pallas_call(body, out_shape=[S((t, ROPE // 2), F32), S((t, ROPE // 2), F32)],
                          name="rope_tables")(pos2, inv_freq)


def _down_norm(xb, wdown_g, gq3, gkv3, cos, sin, j):
    t = xb.shape[0]
    bm = _row_tile(t)

    def body(x_ref, w_ref, gq_ref, gkv_ref, c_ref, s_ref, down_ref, cq_ref, ckv_ref, kpe_ref):
        w = w_ref[...].reshape(D, ODD_IN)
        down = _dot(x_ref[...], w)
        down_ref[...] = down
        q = down[:, :Q_RANK]
        cq_ref[...] = (q * lax.rsqrt(jnp.mean(q * q, axis=-1, keepdims=True) + RMS_EPS) * gq_ref[...]).astype(BF16)
        kv = down[:, Q_RANK:Q_RANK + KV_RANK]
        ckv_ref[...] = (kv * lax.rsqrt(jnp.mean(kv * kv, axis=-1, keepdims=True) + RMS_EPS)
                        * gkv_ref[...]).astype(BF16)
        kpe_ref[...] = _rope(down[:, Q_RANK + KV_RANK:], c_ref[...], s_ref[...])

    row = lambda n: pl.BlockSpec((bm, n), lambda i: (i, 0))
    return pl.pallas_call(
        body, grid=(t // bm,),
        in_specs=[row(D), _full((N_DEV, D // N_DEV, ODD_IN)),
                  pl.BlockSpec((None, 1, Q_RANK), lambda i: (j, 0, 0)),
                  pl.BlockSpec((None, 1, KV_RANK), lambda i: (j, 0, 0)), row(ROPE // 2), row(ROPE // 2)],
        out_specs=[row(ODD_IN), row(Q_RANK), row(KV_RANK), row(ROPE)],
        out_shape=[S((t, ODD_IN), F32), S((t, Q_RANK), BF16), S((t, KV_RANK), BF16), S((t, ROPE), F32)],
        compiler_params=_cp("parallel"), name="down_norm")(xb, wdown_g, gq3, gkv3, cos, sin)


def _q_tile(t, widest):
    return min(widest, t // 2)


def _attn_probs(q, k, qs):
    s = _dot_nt(q, k) * ATT_SCALE
    tq = q.shape[0]
    rows = lax.broadcasted_iota(jnp.int32, (tq, tq), 0)
    cols = lax.broadcasted_iota(jnp.int32, (tq, tq), 1)
    last = jnp.where(jnp.right_shift(cols, CHUNK_SHIFT) <= jnp.right_shift(rows, CHUNK_SHIFT), s[:, qs:], NEG)
    s = last if qs == 0 else jnp.concatenate([s[:, :qs], last], axis=1)
    e = jnp.exp(s - jnp.max(s, axis=-1, keepdims=True))
    return e / jnp.sum(e, axis=-1, keepdims=True)


def _head_qkv(cq, ckv, kpe, c, s, wq_ref, wkv_ref):
    q = jnp.concatenate([_dot(cq, wq_ref[:, :NOPE]), _rope(_dot(cq, wq_ref[:, NOPE:]), c, s)], axis=1).astype(BF16)
    k = jnp.concatenate([_dot(ckv, wkv_ref[:, :NOPE]), kpe], axis=1).astype(BF16)
    vv = _dot(ckv, wkv_ref[:, NOPE:]).astype(BF16)
    return q, k, vv


def _attn_in_specs(t):
    return [_full((t, Q_RANK)), _full((t, KV_RANK)), _full((t, ROPE)), _full((t, ROPE // 2)), _full((t, ROPE // 2)),
            pl.BlockSpec((None, Q_RANK, NOPE + ROPE), lambda h: (h, 0, 0)),
            pl.BlockSpec((None, KV_RANK, NOPE + VDIM), lambda h: (h, 0, 0)),
            pl.BlockSpec((None, VDIM, D), lambda h: (h, 0, 0))]


def _attn_fwd(cq, ckv, kpe, cos, sin, wqb_g, wkvb_g, wo_g):
    t = cq.shape[0]
    tq = _q_tile(t, 256)

    def body(cq_ref, ckv_ref, kpe_ref, c_ref, s_ref, wq_ref, wkv_ref, wo_ref, o_ref, mix_ref):
        q, k, vv = _head_qkv(cq_ref[...], ckv_ref[...], kpe_ref[...], c_ref[...], s_ref[...], wq_ref, wkv_ref)
        for qs in range(0, t, tq):
            ke = qs + tq
            p = _attn_probs(q[qs:ke], k[:ke], qs)
            o_ref[qs:ke, :] = _dot(p.astype(BF16), vv[:ke]).astype(BF16)
        c = _dot(o_ref[...], wo_ref[...])

        @pl.when(pl.program_id(0) == 0)
        def _():
            mix_ref[...] = c

        @pl.when(pl.program_id(0) > 0)
        def _():
            mix_ref[...] += c

    return pl.pallas_call(
        body, grid=(MLA_HEADS,), in_specs=_attn_in_specs(t),
        out_specs=[pl.BlockSpec((None, t, VDIM), lambda h: (h, 0, 0)), _full((t, D))],
        out_shape=[S((MLA_HEADS, t, VDIM), BF16), S((t, D), F32)],
        compiler_params=_cp("arbitrary"), name="attn_fwd")(cq, ckv, kpe, cos, sin, wqb_g, wkvb_g, wo_g)


def _attn_bwd(cq, ckv, kpe, cos, sin, wqb_g, wkvb_g, wo_g, o, dzb):
    t = cq.shape[0]
    tq = _q_tile(t, 512)

    def body(cq_ref, ckv_ref, kpe_ref, c_ref, s_ref, wq_ref, wkv_ref, wo_ref, o_ref, dz_ref,
             dwo_ref, dwq_ref, dwkv_ref, dcq_ref, dckv_ref, dkpe_ref, dkt_s, dvt_s, dq_s):
        cqv = cq_ref[...]
        ckvv = ckv_ref[...]
        c = c_ref[...]
        s = s_ref[...]
        q, k, vv = _head_qkv(cqv, ckvv, kpe_ref[...], c, s, wq_ref, wkv_ref)
        dzv = dz_ref[...]
        dwo_ref[...] = _dot_tn(o_ref[...], dzv).astype(BF16)
        do = _dot_nt(dzv, wo_ref[...]).astype(BF16)
        dkt_s[...] = jnp.zeros_like(dkt_s)
        dvt_s[...] = jnp.zeros_like(dvt_s)
        for qs in range(0, t, tq):
            ke = qs + tq
            p = _attn_probs(q[qs:ke], k[:ke], qs)
            dp = _dot_nt(do[qs:ke], vv[:ke])
            ds = (p * (dp - jnp.sum(p * dp, axis=-1, keepdims=True)) * ATT_SCALE).astype(BF16)
            dq_s[qs:ke, :] = _dot(ds, k[:ke])
            dkt_s[0:NOPE + ROPE, 0:ke] += _dot_tn(q[qs:ke], ds)
            dvt_s[:, 0:ke] += _dot_tn(do[qs:ke], p.astype(BF16))
        dk = dkt_s[...].T
        dqn = dq_s[:, :NOPE].astype(BF16)
        dqp = _rope_t(dq_s[:, NOPE:], c, s).astype(BF16)
        dkn = dk[:, :NOPE].astype(BF16)
        dkp = dk[:, NOPE:NOPE + ROPE]
        dvv = dvt_s[...].T.astype(BF16)
        dwq_ref[:, :NOPE] = _dot_tn(cqv, dqn).astype(BF16)
        dwq_ref[:, NOPE:] = _dot_tn(cqv, dqp).astype(BF16)
        dwkv_ref[:, :NOPE] = _dot_tn(ckvv, dkn).astype(BF16)
        dwkv_ref[:, NOPE:] = _dot_tn(ckvv, dvv).astype(BF16)
        dcq = _dot_nt(dqn, wq_ref[:, :NOPE]) + _dot_nt(dqp, wq_ref[:, NOPE:])
        dckv = _dot_nt(dkn, wkv_ref[:, :NOPE]) + _dot_nt(dvv, wkv_ref[:, NOPE:])

        @pl.when(pl.program_id(0) == 0)
        def _():
            dcq_ref[...] = dcq
            dckv_ref[...] = dckv
            dkpe_ref[...] = dkp

        @pl.when(pl.program_id(0) > 0)
        def _():
            dcq_ref[...] += dcq
            dckv_ref[...] += dckv
            dkpe_ref[...] += dkp

    per_head = lambda a, b: pl.BlockSpec((None, a, b), lambda h: (h, 0, 0))
    return pl.pallas_call(
        body, grid=(MLA_HEADS,),
        in_specs=_attn_in_specs(t) + [per_head(t, VDIM), _full((t, D))],
        out_specs=[per_head(VDIM, D), per_head(Q_RANK, NOPE + ROPE), per_head(KV_RANK, NOPE + VDIM),
                   _full((t, Q_RANK)), _full((t, KV_RANK)), _full((t, ROPE))],
        out_shape=[S((MLA_HEADS, VDIM, D), BF16), S((MLA_HEADS, Q_RANK, NOPE + ROPE), BF16),
                   S((MLA_HEADS, KV_RANK, NOPE + VDIM), BF16),
                   S((t, Q_RANK), F32), S((t, KV_RANK), F32), S((t, ROPE), F32)],
        scratch_shapes=[pltpu.VMEM((2 * NOPE, t), F32), pltpu.VMEM((VDIM, t), F32),
                        pltpu.VMEM((t, NOPE + ROPE), F32)],
        compiler_params=_cp("arbitrary"), name="attn_bwd")(cq, ckv, kpe, cos, sin, wqb_g, wkvb_g, wo_g, o, dzb)


def _rms_bwd(down, dcq, dckv, dkpe, cos, sin, gq3, gkv3, j):
    t = down.shape[0]
    bm = _row_tile(t)

    def body(down_ref, dcq_ref, dckv_ref, dkpe_ref, c_ref, s_ref, gq_ref, gkv_ref, dd_ref, dgq_ref, dgkv_ref):
        @pl.when(pl.program_id(0) == 0)
        def _():
            dgq_ref[...] = jnp.zeros_like(dgq_ref)
            dgkv_ref[...] = jnp.zeros_like(dgkv_ref)

        def rms_b(x, dy, g):
            rstd = lax.rsqrt(jnp.mean(x * x, axis=-1, keepdims=True) + RMS_EPS)
            xh = x * rstd
            dyg = dy * g
            return rstd * (dyg - xh * jnp.mean(dyg * xh, axis=-1, keepdims=True)), jnp.sum(dy * xh, axis=0, keepdims=True)

        dq, dgq = rms_b(down_ref[:, :Q_RANK], dcq_ref[...], gq_ref[...])
        dkv, dgkv = rms_b(down_ref[:, Q_RANK:Q_RANK + KV_RANK], dckv_ref[...], gkv_ref[...])
        dgq_ref[...] += dgq
        dgkv_ref[...] += dgkv
        dd_ref[:, :Q_RANK] = dq.astype(BF16)
        dd_ref[:, Q_RANK:Q_RANK + KV_RANK] = dkv.astype(BF16)
        dd_ref[:, Q_RANK + KV_RANK:] = _rope_t(dkpe_ref[...], c_ref[...], s_ref[...]).astype(BF16)

    row = lambda n: pl.BlockSpec((bm, n), lambda i: (i, 0))
    return pl.pallas_call(
        body, grid=(t // bm,),
        in_specs=[row(ODD_IN), row(Q_RANK), row(KV_RANK), row(ROPE), row(ROPE // 2), row(ROPE // 2),
                  pl.BlockSpec((None, 1, Q_RANK), lambda i: (j, 0, 0)),
                  pl.BlockSpec((None, 1, KV_RANK), lambda i: (j, 0, 0))],
        out_specs=[row(ODD_IN), _full((1, Q_RANK)), _full((1, KV_RANK))],
        out_shape=[S((t, ODD_IN), BF16), S((1, Q_RANK), F32), S((1, KV_RANK), F32)],
        compiler_params=_cp("arbitrary"), name="rms_bwd")(down, dcq, dckv, dkpe, cos, sin, gq3, gkv3)


def _col_blocks(t, n, bn):
    return pl.BlockSpec((t, bn), lambda i: (0, i))


def _row_blocks(n, bm):
    return pl.BlockSpec((bm, n), lambda i: (i, 0))


def _local_step(x, pos2, tgt, small, weights_of, grads_done, start_dep=None, prefetch=None):
    t = x.shape[0]
    bm = _row_tile(t)
    inv_freq = (ROPE_THETA ** (-jnp.arange(0, ROPE, 2, dtype=F32) / ROPE)).reshape(1, ROPE // 2)
    cos, sin = _rope_tables(pos2, inv_freq)
    lru_p = {k: small[k] for k in ("conv_w", "conv_b", "w_a", "b_a", "w_x", "b_x", "lam")}

    saved = []
    y, yb = x, x.astype(BF16)
    for l in range(DEPTH):
        j = l // 2
        big = weights_of(l, 0, y)
        sv = dict(xb=yb, big=big)
        if l % 2 == 0:
            proj = _mm(yb, big["win_t"], mode="nt", grid=(EVEN_IN // 512,), a_spec=_full((t, D)),
                       b_spec=_row_blocks(D, 512), out_shape=S((t, EVEN_IN), F32),
                       out_spec=_col_blocks(t, EVEN_IN, 512), name="even_proj", dep=start_dep if l == 0 else None)
            ycat = jnp.concatenate([_pool_fwd(proj, small["pool_w"], small["pool_scale"], j),
                                    _lru_fwd(proj, lru_p, j)], axis=1)
            big.update(weights_of(l, 1, ycat))
            z1, y1, y1b = _proj_resid_ln(y, ycat, big["wout2d"], small["ln_mix_g"], small["ln_mix_b"], l, "even_out")
            sv.update(proj=proj, ycat=ycat)
        else:
            down, cq, ckv, kpe = _down_norm(yb, big["wdown"], small["gq"], small["gkv"], cos, sin, j)
            o, mix = _attn_fwd(cq, ckv, kpe, cos, sin, big["wqb"], big["wkvb"], big["wo"])
            z1, y1, y1b = _resid_ln(y, mix, small["ln_mix_g"], small["ln_mix_b"], l, "resid_ln")
            sv.update(down=down, cq=cq, ckv=ckv, kpe=kpe, o=o)
        fetched = prefetch(l + 1, y1) if prefetch is not None and l + 1 < DEPTH else None
        z2, y, yb, act = _mlp_fwd(y1, y1b, big["w1"], big["w2"], small["ln_ffn_g"], small["ln_ffn_b"], l,
                                  dep=fetched)
        sv.update(z1=z1, y1b=y1b, z2=z2, act=act)
        saved.append(sv)

    dy, loss_tile = _loss_grad(y, tgt)

    g = {k: [None] * n for k, n in (("ln_mix_g", 4), ("ln_mix_b", 4), ("ln_ffn_g", 4), ("ln_ffn_b", 4),
                                    ("pool_w", 2), ("pool_scale", 2), ("conv_w", 2), ("conv_b", 2),
                                    ("w_a", 2), ("b_a", 2), ("w_x", 2), ("b_x", 2), ("lam", 2),
                                    ("gq", 2), ("gkv", 2))}
    dep = None
    for l in reversed(range(DEPTH)):
        j = l // 2
        sv = saved[l]
        big = sv["big"]
        dz2, dz2b, g["ln_ffn_g"][l], g["ln_ffn_b"][l] = _ln_bwd(dy, sv["z2"], small["ln_ffn_g"], l, "ln_bwd", dep=dep)
        act = sv["act"]
        dh, dff = _mlp_bwd_dh(act, dz2b, big["w1"], big["w2"])
        dw1 = _mm(sv["y1b"], dh, mode="tn", grid=(N_DEV,), a_spec=_full((t, D)),
                  b_spec=_col_blocks(t, D_FF, FF_BLK), out_shape=S((N_DEV, D, FF_BLK), BF16),
                  out_spec=pl.BlockSpec((None, D, FF_BLK), lambda i: (i, 0, 0)), name="mlp_dw1")
        dw2 = _mm(act, dz2b, mode="tn", grid=(N_DEV,), a_spec=_col_blocks(t, D_FF, FF_BLK),
                  b_spec=_full((t, D)), out_shape=S((N_DEV, FF_BLK, D), BF16),
                  out_spec=pl.BlockSpec((None, FF_BLK, D), lambda i: (i, 0, 0)), name="mlp_dw2")
        dep = grads_done(l, dict(w1=dw1, w2=dw2))
        dz1, dz1b, g["ln_mix_g"][l], g["ln_mix_b"][l] = _ln_bwd(dff, sv["z1"], small["ln_mix_g"], l, "ln_bwd_res",
                                                                 r=dz2, dep=dep)
        if l % 2 == 0:
            wout = big["wout2d"]
            dycat = _mm(dz1b, wout, mode="nt", grid=(EVEN_MIX // 512,), a_spec=_full((t, D)),
                        b_spec=_row_blocks(D, 512), out_shape=S((t, EVEN_MIX), F32),
                        out_spec=_col_blocks(t, EVEN_MIX, 512), name="even_dycat")
            dwout = _mm(sv["ycat"], dz1b, mode="tn", grid=(EVEN_MIX // 512,), a_spec=_col_blocks(t, EVEN_MIX, 512),
                        b_spec=_full((t, D)), out_shape=S((EVEN_MIX, D), BF16), out_spec=_row_blocks(D, 512),
                        name="even_dwout")
            du_pool, g["pool_w"][j], g["pool_scale"][j] = _pool_bwd(sv["proj"], dycat, small["pool_w"],
                                                                   small["pool_scale"], j)
            (du_lru, du_gate, g["conv_w"][j], g["conv_b"][j], g["w_a"][j], g["b_a"][j], g["w_x"][j], g["b_x"][j],
             g["lam"][j]) = _lru_bwd(sv["proj"], dycat, lru_p, j)
            dproj = jnp.concatenate([du_pool, du_lru, du_gate], axis=1)
            dwin = _mm(sv["xb"], dproj, mode="tn", grid=(EVEN_IN // 512,), a_spec=_full((t, D)),
                       b_spec=_col_blocks(t, EVEN_IN, 512), out_shape=S((D, EVEN_IN), BF16),
                       out_spec=_col_blocks(D, EVEN_IN, 512), name="even_dwin")
            dep = grads_done(l, dict(win=dwin.reshape(D, N_DEV, EVEN_IN // N_DEV).transpose(1, 0, 2),
                                     wout=dwout.reshape(N_DEV, EVEN_MIX // N_DEV, D)))
            dy = _mm(dproj, big["win_t"], mode="nn", grid=(t // bm,), a_spec=_row_blocks(EVEN_IN, bm),
                     b_spec=_full((EVEN_IN, D)), out_shape=S((t, D), F32), out_spec=_row_blocks(D, bm),
                     add=dz1, add_spec=_row_blocks(D, bm), add_scale=ALPHA, name="even_dx", dep=dep)
        else:
            dwo, dwqb, dwkvb, dcq, dckv, dkpe = _attn_bwd(
                sv["cq"], sv["ckv"], sv["kpe"], cos, sin, big["wqb"], big["wkvb"], big["wo"], sv["o"], dz1b)
            ddown, g["gq"][j], g["gkv"][j] = _rms_bwd(sv["down"], dcq, dckv, dkpe, cos, sin, small["gq"],
                                                     small["gkv"], j)
            dwdown = _mm(sv["xb"], ddown, mode="tn", grid=(N_DEV,), a_spec=_col_blocks(t, D, D // N_DEV),
                         b_spec=_full((t, ODD_IN)), out_shape=S((N_DEV, D // N_DEV, ODD_IN), BF16),
                         out_spec=pl.BlockSpec((None, D // N_DEV, ODD_IN), lambda i: (i, 0, 0)),
                         name="odd_dwdown")
            dep = grads_done(l, dict(wdown=dwdown, wqb=dwqb, wkvb=dwkvb, wo=dwo))
            dy = _mm(ddown, big["wdown2d"], mode="nt", grid=(t // bm,), a_spec=_row_blocks(ODD_IN, bm),
                     b_spec=_full((D, ODD_IN)), out_shape=S((t, D), F32), out_spec=_row_blocks(D, bm),
                     add=dz1, add_spec=_row_blocks(D, bm), add_scale=ALPHA, name="odd_dx", dep=dep)
    return loss_tile[0, 0], dy, g


def _mesh_place():
    x, y, c = lax.axis_index("x"), lax.axis_index("y"), lax.axis_index("c")
    return x, y, c


def _peer(place, k):
    x, y, c = place
    return (1 - x if k & 4 else x, 1 - y if k & 2 else y, 1 - c if k & 1 else c)


def _index(place):
    x, y, c = place
    return 4 * x + 2 * y + c


ANY = pl.BlockSpec(memory_space=pl.ANY)


def _make_zones(shards, me, name):
    n = len(shards)

    def body(me_ref, *refs):
        for src, dst in zip(refs[:n], refs[n:]):
            dst[...] = src[...].astype(BF16)

    grid_spec = pltpu.PrefetchScalarGridSpec(
        num_scalar_prefetch=1, grid=(1,),
        in_specs=[pl.BlockSpec(s.shape, lambda i, me_ref: (0, 0)) for s in shards],
        out_specs=[pl.BlockSpec((None,) + s.shape, lambda i, me_ref: (me_ref[0], 0, 0)) for s in shards])
    return pl.pallas_call(body, grid_spec=grid_spec, out_shape=[S((N_DEV,) + s.shape, BF16) for s in shards],
                          compiler_params=_cp("arbitrary"), name=name)(me, *shards)


def _all_gather_big(zones):
    n = len(zones)

    def body(*refs):
        outs = refs[n:2 * n]
        send, recv = refs[2 * n:]
        x, y, c = _mesh_place()
        me, sibling = (x, y, c), (x, y, 1 - c)
        chips = [(1 - x, y), (x, 1 - y), (1 - x, 1 - y)]

        def copy(w, k, block, to):
            blk = outs[w].at[_index(block)]
            return pltpu.make_async_remote_copy(src_ref=blk, dst_ref=blk, send_sem=send.at[w, k], recv_sem=recv.at[w, k],
                                                device_id=to, device_id_type=MESH)

        first = []
        for w in range(n):
            first.append(copy(w, 0, me, sibling))
            first += [copy(w, 1 + j, me, (*chip, c)) for j, chip in enumerate(chips)]
        for cp in first:
            cp.start()
        passed = []
        for w in range(n):
            for j, chip in enumerate(chips):
                copy(w, 1 + j, (*chip, c), me).wait_recv()
                cp = copy(w, 4 + j, (*chip, c), sibling)
                cp.start()
                passed.append(cp)
        for w in range(n):
            copy(w, 0, sibling, me).wait_recv()
            for j, chip in enumerate(chips):
                copy(w, 4 + j, (*chip, 1 - c), me).wait_recv()
        for cp in first + passed:
            cp.wait_send()

    return pl.pallas_call(
        body, in_specs=[ANY] * n, out_specs=[ANY] * n, out_shape=[S(z.shape, z.dtype) for z in zones],
        input_output_aliases={i: i for i in range(n)},
        scratch_shapes=[pltpu.SemaphoreType.DMA((n, N_DEV - 1)), pltpu.SemaphoreType.DMA((n, N_DEV - 1))],
        compiler_params=pltpu.CompilerParams(has_side_effects=True), name="all_gather_big")(*zones)


def _shard_rows_tile(a):
    return max(d for d in range(16, 257, 16) if a % d == 0)


HBM = pl.BlockSpec(memory_space=pltpu.HBM)
SEM = pl.BlockSpec(memory_space=pltpu.SEMAPHORE)
DATAFLOW = pltpu.SideEffectType.DATAFLOW_SIDE_EFFECTING


def _in_hbm(a):
    return pltpu.with_memory_space_constraint(a, pltpu.HBM)


def _gather_ici_copies(place, src, land, w):
    me = _index(place)
    return [(_peer(place, k), land.at[me], land.at[me]) for k in (1, 2, 4, 6)]


def _gather_d2d_copies(place, src, land, w):
    blocks = [_index(_peer(place, k)) for k in (2, 4, 6)]
    return [(_peer(place, 1), land.at[b], land.at[b]) for b in blocks]


GATHER_ICI = (4, _gather_ici_copies)
GATHER_D2D = (3, _gather_d2d_copies)


def _scatter_plan(layers):
    def copies(place, src, land, w):
        me = _index(place)
        mine = land.at[me] if layers[w] is None else land.at[me, layers[w]]
        return [(_peer(place, k), src.at[_index(_peer(place, k))], mine) for k in range(1, N_DEV)]
    return (N_DEV - 1, copies)


def _gather_all_copies(place, src, land, w):
    me = _index(place)
    return [(_peer(place, k), land.at[me], land.at[me]) for k in range(1, N_DEV)]


GATHER_ALL = (N_DEV - 1, _gather_all_copies)


def _sum_blocks(zone, part, me):
    r = part.shape[1]

    def body(me_ref, z_ref, p_ref, o_ref):
        acc = None
        for s in range(N_DEV):
            term = jnp.where(me_ref[0] == s, p_ref[...], z_ref[s])
            acc = term if acc is None else acc + term
        o_ref[...] = acc

    grid_spec = pltpu.PrefetchScalarGridSpec(
        num_scalar_prefetch=1, grid=(1,),
        in_specs=[pl.BlockSpec((N_DEV, r, 128), lambda i, me_ref: (0, 0, 0)),
                  pl.BlockSpec((None, r, 128), lambda i, me_ref: (me_ref[0], 0, 0))],
        out_specs=pl.BlockSpec((r, 128), lambda i, me_ref: (0, 0)))
    return pl.pallas_call(body, grid_spec=grid_spec, out_shape=S((r, 128), F32),
                          compiler_params=_cp("arbitrary"), name="sum_small")(me, zone, part)


def _exchange_start(srcs, lands, plan, name, after=()):
    ns, n = len(srcs), len(lands)
    n_in = ns + n + len(after)
    per, copies = plan

    def body(*refs):
        ins, land = refs[:ns], refs[ns:ns + n]
        send, recv = refs[n_in], refs[n_in + 1]
        token = refs[-1]
        place = _mesh_place()
        for i in range(per):
            for w in range(n):
                target, src, dst = copies(place, ins[w] if ns else None, land[w], w)[i]
                pltpu.make_async_remote_copy(src_ref=src, dst_ref=dst, send_sem=send.at[w * per + i],
                                             recv_sem=recv.at[w * per + i], device_id=target, device_id_type=MESH).start()
        token[...] = jnp.zeros_like(token)

    sems = pltpu.SemaphoreType.DMA((n * per,))
    thru = [pltpu.HBM(a.shape, a.dtype) for a in list(srcs) + list(lands)]
    out = pl.pallas_call(
        body, name=name, in_specs=[HBM] * (ns + n) + [ANY] * len(after),
        out_shape=(sems, sems, *thru, S((8, 128), F32)),
        out_specs=(SEM, SEM, *([HBM] * (ns + n)), pl.BlockSpec(memory_space=pltpu.VMEM)),
        input_output_aliases={i: 2 + i for i in range(ns + n)},
        compiler_params=pltpu.CompilerParams(has_side_effects=DATAFLOW),
    )(*[_in_hbm(a) for a in list(srcs) + list(lands)], *after)
    return out[0], out[1], list(out[2:2 + ns]), list(out[2 + ns:2 + ns + n]), out[-1]


def _exchange_wait(send, recv, srcs, lands, plan, after, name):
    ns, n = len(srcs), len(lands)
    per, copies = plan
    afters = tuple(after) if isinstance(after, (tuple, list)) else (after,)

    def body(*refs):
        ins, land = refs[:ns], refs[ns:ns + n]
        send_ref, recv_ref = refs[ns + n], refs[ns + n + 1]
        place = _mesh_place()
        for i in range(per):
            for w in range(n):
                target, src, dst = copies(place, ins[w] if ns else None, land[w], w)[i]
                cp = pltpu.make_async_remote_copy(src_ref=src, dst_ref=dst, send_sem=send_ref.at[w * per + i],
                                                  recv_sem=recv_ref.at[w * per + i], device_id=target,
                                                  device_id_type=MESH)
                cp.wait_send()
                cp.wait_recv()

    thru = [pltpu.HBM(a.shape, a.dtype) for a in list(srcs) + list(lands)]
    out = pl.pallas_call(
        body, name=name, in_specs=[HBM] * (ns + n) + [SEM, SEM] + [ANY] * len(afters),
        out_shape=tuple(thru), out_specs=tuple([HBM] * (ns + n)),
        input_output_aliases={i: i for i in range(ns + n)},
        compiler_params=pltpu.CompilerParams(has_side_effects=DATAFLOW),
    )(*srcs, *lands, send, recv, *afters)
    return list(out[:ns]), list(out[ns:])


def _all_reduce_small(part, name, deps=()):
    def body(*refs):
        p_ref = refs[0]
        o_ref, rbuf, send1, recv1, send2, recv2 = refs[-6:]
        place = _mesh_place()
        me = _index(place)
        rbuf[pl.ds(me, 1)] = p_ref[pl.ds(me, 1)]
        first = [pltpu.make_async_remote_copy(src_ref=p_ref.at[_index(_peer(place, k))], dst_ref=rbuf.at[me],
                                              send_sem=send1.at[k - 1], recv_sem=recv1.at[k - 1],
                                              device_id=_peer(place, k), device_id_type=MESH)
                 for k in range(1, N_DEV)]
        for cp in first:
            cp.start()
        for cp in first:
            cp.wait()
        acc = rbuf[0]
        for d in range(1, N_DEV):
            acc = acc + rbuf[d]
        o_ref[pl.ds(me, 1)] = acc[None]
        second = [pltpu.make_async_remote_copy(src_ref=o_ref.at[me], dst_ref=o_ref.at[me], send_sem=send2.at[k - 1],
                                               recv_sem=recv2.at[k - 1], device_id=_peer(place, k),
                                               device_id_type=MESH)
                  for k in range(1, N_DEV)]
        for cp in second:
            cp.start()
        for cp in second:
            cp.wait()

    vm = pl.BlockSpec(memory_space=pltpu.VMEM)
    ops = [part, *deps]
    return pl.pallas_call(
        body, in_specs=[vm] + [ANY] * len(deps), out_specs=vm, out_shape=S(part.shape, F32),
        scratch_shapes=[pltpu.VMEM(part.shape, F32)] + [pltpu.SemaphoreType.DMA((N_DEV - 1,))] * 4,
        compiler_params=pltpu.CompilerParams(has_side_effects=True, vmem_limit_bytes=VMEM_LIMIT), name=name)(*ops)


def _adamw(w, g, m, v):
    m = ADAM_B1 * m + (1.0 - ADAM_B1) * g
    v = ADAM_B2 * v + (1.0 - ADAM_B2) * (g * g)
    m_hat = m / (1.0 - ADAM_B1 ** ADAM_STEP)
    v_hat = v / (1.0 - ADAM_B2 ** ADAM_STEP)
    return -ADAM_LR * (m_hat / (jnp.sqrt(v_hat) + ADAM_EPS) + ADAM_WD * w), m, v


def _adam_big(parts, own, me, w, m, v, name):
    nl, a, b = w.shape
    ta = _shard_rows_tile(a)

    def body(me_ref, p_ref, *refs):
        own_refs, (w_ref, m_ref, v_ref, g_ref, d_ref, mo_ref, vo_ref) = refs[:nl], refs[nl:]
        layer = pl.program_id(0)
        mine = own_refs[0][...]
        for k in range(1, nl):
            mine = jnp.where(layer == k, own_refs[k][...], mine)
        g = None
        for s in range(N_DEV):
            term = jnp.where(me_ref[0] == s, mine, p_ref[s]).astype(F32)
            g = term if g is None else g + term
        g_ref[...] = g
        d_ref[...], mo_ref[...], vo_ref[...] = _adamw(w_ref[...], g, m_ref[...], v_ref[...])

    blk = pl.BlockSpec((None, ta, b), lambda l, i, me_ref: (l, i, 0))

    def own_spec(k):
        return pl.BlockSpec((None, ta, b), lambda l, i, me_ref: (me_ref[0], jnp.where(l == k, i, 0), 0))

    grid_spec = pltpu.PrefetchScalarGridSpec(
        num_scalar_prefetch=1, grid=(nl, a // ta),
        in_specs=[pl.BlockSpec((N_DEV, None, ta, b), lambda l, i, me_ref: (0, l, i, 0))]
        + [own_spec(k) for k in range(nl)] + [blk, blk, blk],
        out_specs=[blk] * 4)
    return pl.pallas_call(body, grid_spec=grid_spec, out_shape=[S(w.shape, F32)] * 4,
                          compiler_params=_cp("arbitrary", "arbitrary"), name=name)(me, parts, *own, w, m, v)


def _adam_small(gs, ws, ms, vs):
    n = len(gs)

    def body(*refs):
        ins, outs = refs[:4 * n], refs[4 * n:]
        for i in range(n):
            g_ref, w_ref, m_ref, v_ref = (ins[k * n + i] for k in range(4))
            outs[i][...], outs[n + i][...], outs[2 * n + i][...] = _adamw(w_ref[...], g_ref[...], m_ref[...], v_ref[...])

    out = pl.pallas_call(body, out_shape=[S(g.shape, F32) for g in gs] * 3, compiler_params=_cp(),
                         name="adam_small")(*gs, *ws, *ms, *vs)
    return out[:n], out[n:2 * n], out[2 * n:]


BIG = ("even_w_in", "even_w_out", "mla_w_down", "mla_w_qb", "mla_w_kvb", "mla_w_o", "mlp_w1", "mlp_w2")
BIG_KEY = dict(even_w_in="win", even_w_out="wout", mla_w_down="wdown", mla_w_qb="wqb", mla_w_kvb="wkvb",
               mla_w_o="wo", mlp_w1="w1", mlp_w2="w2")
SMALL = (("ln_mix_g", "ln_mix_g", None), ("ln_mix_b", "ln_mix_b", None), ("ln_ffn_g", "ln_ffn_g", None),
         ("ln_ffn_b", "ln_ffn_b", None), ("pool_w", "pool_w", None), ("pool_scale", "pool_scale", None),
         ("lru_conv_w", "conv_w", 2), ("lru_conv_b", "conv_b", None), ("lru_w_a", "w_a", None),
         ("lru_b_a", "b_a", None), ("lru_w_x", "w_x", None), ("lru_b_x", "b_x", None), ("lru_lambda", "lam", None),
         ("mla_q_norm_g", "gq", 1), ("mla_kv_norm_g", "gkv", 1))
WEIGHTS = ("ln_mix_g", "ln_mix_b", "ln_ffn_g", "ln_ffn_b", "even_w_in", "pool_w", "pool_scale", "lru_conv_w",
           "lru_conv_b", "lru_w_a", "lru_b_a", "lru_w_x", "lru_b_x", "lru_lambda", "even_w_out", "mla_w_down",
           "mla_q_norm_g", "mla_kv_norm_g", "mla_w_qb", "mla_w_kvb", "mla_w_o", "mlp_w1", "mlp_w2")
ALL_AXES = ("x", "y", "c")


def _layer_weights(l):
    j = l // 2
    if l % 2 == 0:
        mixer = [("win", "even_w_in", j), ("wout", "even_w_out", j)]
    else:
        mixer = [("wdown", "mla_w_down", j), ("wqb", "mla_w_qb", j), ("wkvb", "mla_w_kvb", j), ("wo", "mla_w_o", j)]
    return mixer + [("w1", "mlp_w1", l), ("w2", "mlp_w2", l)]


def _pack(arrays, multiple):
    flat = jnp.concatenate([a.reshape(-1) for a in arrays])
    pad = (-flat.shape[0]) % multiple
    return jnp.pad(flat, (0, pad))


def _unpack(flat, shapes):
    out, at = [], 0
    for shp in shapes:
        n = 1
        for s in shp:
            n *= s
        out.append(flat[at:at + n].reshape(shp))
        at += n
    return out


def _global_shape(local_shape, axis):
    if axis is None:
        return tuple(local_shape)
    return tuple(s * N_DEV if i == axis else s for i, s in enumerate(local_shape))


def _step(x, positions, tgt, w, m, v):
    t = x.shape[1]
    me = _index(_mesh_place())

    sharded = [(name, axis) for name, _, axis in SMALL if axis is not None]
    zeros_with_mine = [lax.dynamic_update_slice_in_dim(jnp.zeros(_global_shape(w[name].shape, axis), F32), w[name],
                                                       me * w[name].shape[axis], axis) for name, axis in sharded]
    chunk = N_DEV * 8 * 128
    gathered = _all_reduce_small(_pack(zeros_with_mine, chunk).reshape(N_DEV, -1, 128), "gather_small")
    full = dict(zip([name for name, _ in sharded],
                    _unpack(gathered.reshape(-1), [_global_shape(w[name].shape, axis) for name, axis in sharded])))

    def keys_of(l, part):
        keys = [key for key, _, _ in _layer_weights(l)]
        if l == 0:
            return keys[:1] if part == 0 else keys[1:]
        return keys if part == 0 else []

    shard_of = {(l, key): (w[name][i].T if key == "win" else w[name][i])
                for l in range(DEPTH) for key, name, i in _layer_weights(l)}
    me_arr = me.astype(jnp.int32).reshape(1)
    first = _all_gather_big(_make_zones([shard_of[0, key] for key in keys_of(0, 0)], me_arr, "zones_0_0"))
    flights, after = {}, (first[0], gathered)
    for l in range(DEPTH):
        for part in (0, 1):
            if (l, part) != (0, 0) and keys_of(l, part):
                zones = _make_zones([shard_of[l, key] for key in keys_of(l, part)], me_arr, "zones_%d_%d" % (l, part))
                send, recv, _, lands, token = _exchange_start([], zones, GATHER_ICI, "gather_start_%d_%d" % (l, part),
                                                              after=after)
                flights[l, part] = (send, recv, [], lands)
                after = (token,)

    passing = {}

    def pass_on(l, part, after):
        tag = "%d_%d" % (l, part)
        _, lands = _exchange_wait(*flights[l, part], GATHER_ICI, after, "gather_wait_" + tag)
        send, recv, _, lands, token = _exchange_start([], lands, GATHER_D2D, "gather_pass_" + tag)
        passing[l, part] = (send, recv, [], lands)
        return token

    def early_pass(l, after):
        return pass_on(l, 0, after) if l >= 2 else None

    def weights_of(l, part, after):
        keys = keys_of(l, part)
        if (l, part) == (0, 0):
            arrays = first
        elif keys:
            if (l, part) not in passing:
                pass_on(l, part, after)
            _, arrays = _exchange_wait(*passing[l, part], GATHER_D2D, after, "gather_pass_wait_%d_%d" % (l, part))
        big = dict(zip(keys, arrays)) if keys else {}
        if "win" in big:
            big["win_t"] = big["win"].reshape(EVEN_IN, D)
        if "wout" in big:
            big["wout2d"] = big["wout"].reshape(EVEN_MIX, D)
        if "wdown" in big:
            big["wdown2d"] = big["wdown"].reshape(D, ODD_IN)
        return big

    zone = {name: lax.empty((N_DEV,) + w[name].shape, BF16) for name in BIG}
    name_of = {key: name for name, key in BIG_KEY.items()}
    sent, last_token = [], [None]

    def grads_done(l, grads):
        keys = list(grads)
        index = {key: i for key, _, i in _layer_weights(l)}
        layers = [index[key] for key in keys]
        send, recv, srcs, lands, tok = _exchange_start([grads[k] for k in keys], [zone[name_of[k]] for k in keys],
                                                       _scatter_plan(layers), "scatter_start_%d_%s" % (l, keys[0]))
        for k, land in zip(keys, lands):
            zone[name_of[k]] = land
        sent.append((send, recv, srcs, keys, layers))
        last_token[0] = tok
        return tok

    row3 = lambda a: a.reshape(a.shape[0], 1, a.shape[1])
    small = dict(ln_mix_g=row3(w["ln_mix_g"]), ln_mix_b=row3(w["ln_mix_b"]), ln_ffn_g=row3(w["ln_ffn_g"]),
                 ln_ffn_b=row3(w["ln_ffn_b"]), pool_w=w["pool_w"], pool_scale=row3(w["pool_scale"]),
                 conv_w=full["lru_conv_w"], conv_b=row3(w["lru_conv_b"]), w_a=w["lru_w_a"], b_a=row3(w["lru_b_a"]),
                 w_x=w["lru_w_x"], b_x=row3(w["lru_b_x"]), lam=row3(w["lru_lambda"]),
                 gq=row3(full["mla_q_norm_g"]), gkv=row3(full["mla_kv_norm_g"]))

    loss_part, grad_x, g = _local_step(x[0], positions.reshape(t, 1), tgt[0], small, weights_of, grads_done,
                                       start_dep=token, prefetch=early_pass)

    own = {name: [None] * w[name].shape[0] for name in BIG}
    me_arr = me.astype(jnp.int32).reshape(1)
    out = {}
    local_g = [jnp.stack(g[key]).reshape(_global_shape(w[name].shape, axis)) for name, key, axis in SMALL]
    local_g.append(loss_part.reshape(1))
    part = _pack(local_g, chunk).reshape(N_DEV, -1, 128)
    small_plan = _scatter_plan([None])
    s_send, s_recv, s_src, s_land, after = _exchange_start([part], [lax.empty(part.shape, F32)], small_plan,
                                                           "small_scatter_start", after=(last_token[0],))
    for n_flight, (send, recv, srcs, keys, layers) in enumerate(sent):
        if n_flight == len(sent) - 1:
            for name in BIG:
                if BIG_KEY[name] not in keys:
                    out[name] = _adam_big(zone[name], own[name], me_arr, w[name], m[name], v[name], "adam_" + name)
            s_src, s_land = _exchange_wait(s_send, s_recv, s_src, s_land, small_plan, [o[0] for o in out.values()],
                                           "small_scatter_wait")
            chunk_sum = _sum_blocks(s_land[0], s_src[0], me_arr)
            r_zone = lax.dynamic_update_slice_in_dim(lax.empty(part.shape, F32), chunk_sum[None], me, 0)
            r_send, r_recv, _, r_land, after = _exchange_start([], [r_zone], GATHER_ALL, "small_gather_start")
        srcs, lands = _exchange_wait(send, recv, srcs, [zone[name_of[k]] for k in keys], _scatter_plan(layers),
                                     after, "scatter_wait_%d" % n_flight)
        for k, land, src, layer in zip(keys, lands, srcs, layers):
            zone[name_of[k]] = land
            own[name_of[k]][layer] = src
        after = lands[0]
    for name in BIG:
        if name not in out:
            out[name] = _adam_big(zone[name], own[name], me_arr, w[name], m[name], v[name], "adam_" + name)

    _, reduced = _exchange_wait(r_send, r_recv, [], r_land, GATHER_ALL, [out[name][0] for name in BIG],
                                "small_gather_wait")
    reduced = _unpack(reduced[0].reshape(-1), [a.shape for a in local_g])
    loss = reduced[-1][0]
    mine = [a if axis is None else lax.dynamic_slice_in_dim(a, me * w[name].shape[axis], w[name].shape[axis], axis)
            for a, (name, _, axis) in zip(reduced, SMALL)]
    names = [name for name, _, _ in SMALL]
    as_2d = lambda a: a.reshape(-1, a.shape[-1])
    new = _adam_small([as_2d(a) for a in mine], *([as_2d(src[name]) for name in names] for src in (w, m, v)))
    for i, name in enumerate(names):
        out[name] = (mine[i],) + tuple(part[i].reshape(w[name].shape) for part in new)

    return (loss, grad_x[None]) + tuple(out[name][i] for i in range(4) for name in WEIGHTS)


def kernel(x, positions, ln_mix_g, ln_mix_b, ln_ffn_g, ln_ffn_b, even_w_in, pool_w, pool_scale, lru_conv_w, lru_conv_b, lru_w_a, lru_b_a, lru_w_x, lru_b_x, lru_lambda, even_w_out, mla_w_down, mla_q_norm_g, mla_kv_norm_g, mla_w_qb, mla_w_kvb, mla_w_o, mlp_w1, mlp_w2, loss_target, m_ln_mix_g, m_ln_mix_b, m_ln_ffn_g, m_ln_ffn_b, m_even_w_in, m_pool_w, m_pool_scale, m_lru_conv_w, m_lru_conv_b, m_lru_w_a, m_lru_b_a, m_lru_w_x, m_lru_b_x, m_lru_lambda, m_even_w_out, m_mla_w_down, m_mla_q_norm_g, m_mla_kv_norm_g, m_mla_w_qb, m_mla_w_kvb, m_mla_w_o, m_mlp_w1, m_mlp_w2, v_ln_mix_g, v_ln_mix_b, v_ln_ffn_g, v_ln_ffn_b, v_even_w_in, v_pool_w, v_pool_scale, v_lru_conv_w, v_lru_conv_b, v_lru_w_a, v_lru_b_a, v_lru_w_x, v_lru_b_x, v_lru_lambda, v_even_w_out, v_mla_w_down, v_mla_q_norm_g, v_mla_kv_norm_g, v_mla_w_qb, v_mla_w_kvb, v_mla_w_o, v_mlp_w1, v_mlp_w2):
    w = dict(zip(WEIGHTS, (ln_mix_g, ln_mix_b, ln_ffn_g, ln_ffn_b, even_w_in, pool_w, pool_scale, lru_conv_w,
                           lru_conv_b, lru_w_a, lru_b_a, lru_w_x, lru_b_x, lru_lambda, even_w_out, mla_w_down,
                           mla_q_norm_g, mla_kv_norm_g, mla_w_qb, mla_w_kvb, mla_w_o, mlp_w1, mlp_w2)))
    m = dict(zip(WEIGHTS, (m_ln_mix_g, m_ln_mix_b, m_ln_ffn_g, m_ln_ffn_b, m_even_w_in, m_pool_w, m_pool_scale,
                           m_lru_conv_w, m_lru_conv_b, m_lru_w_a, m_lru_b_a, m_lru_w_x, m_lru_b_x, m_lru_lambda,
                           m_even_w_out, m_mla_w_down, m_mla_q_norm_g, m_mla_kv_norm_g, m_mla_w_qb, m_mla_w_kvb,
                           m_mla_w_o, m_mlp_w1, m_mlp_w2)))
    v = dict(zip(WEIGHTS, (v_ln_mix_g, v_ln_mix_b, v_ln_ffn_g, v_ln_ffn_b, v_even_w_in, v_pool_w, v_pool_scale,
                           v_lru_conv_w, v_lru_conv_b, v_lru_w_a, v_lru_b_a, v_lru_w_x, v_lru_b_x, v_lru_lambda,
                           v_even_w_out, v_mla_w_down, v_mla_q_norm_g, v_mla_kv_norm_g, v_mla_w_qb, v_mla_w_kvb,
                           v_mla_w_o, v_mlp_w1, v_mlp_w2)))
    return _step(x, positions, loss_target, w, m, v)
```

```python
import functools

import jax
import jax.numpy as jnp
from jax import lax
from jax.experimental import pallas as pl
from jax.experimental.pallas import tpu as pltpu

F32 = jnp.float32
BF16 = jnp.bfloat16
S = jax.ShapeDtypeStruct

D = 1024
DEPTH = 4
N_DEV = 8
CHUNK_SHIFT = 6
POOL_WINDOWS = (2, 4, 8, 16)
POOL_W = 512
LRU_W = 1024
LRU_HEADS = 8
HEAD = 128
LRU_C = 8.0
EVEN_IN = 2560
EVEN_MIX = 1536
MLA_HEADS = 8
NOPE = 128
ROPE = 64
VDIM = 128
Q_RANK = 384
KV_RANK = 256
ODD_IN = 704
D_FF = 4096
FF_BLK = D_FF // N_DEV
ROPE_THETA = 10000.0
ALPHA = (2 * DEPTH) ** 0.25
LN_EPS = 1e-5
RMS_EPS = 1e-6
ATT_SCALE = (NOPE + ROPE) ** -0.5
NEG = float(jnp.finfo(jnp.float32).min)
ADAM_LR = 0.001
ADAM_B1 = 0.9
ADAM_B2 = 0.999
ADAM_EPS = 1e-08
ADAM_WD = 0.01
ADAM_STEP = 10
V7X_VMEM_BYTES = 64 * 1024 * 1024
VMEM_LIMIT = V7X_VMEM_BYTES - 8 * 1024 * 1024
MESH = pl.DeviceIdType.MESH


def _cp(*sem):
    return pltpu.CompilerParams(dimension_semantics=sem if sem else None, vmem_limit_bytes=VMEM_LIMIT)


def _dot(a, b):
    return jnp.dot(a, b, preferred_element_type=F32)


def _dot_nt(a, b):
    return lax.dot_general(a, b, (((1,), (1,)), ((), ())), preferred_element_type=F32)


def _dot_tn(a, b):
    return lax.dot_general(a, b, (((0,), (0,)), ((), ())), preferred_element_type=F32)


def _full(shape):
    return pl.BlockSpec(shape, lambda *_: (0,) * len(shape))


def _mm(a, b, *, mode, grid, a_spec, b_spec, out_shape, out_spec, name, add=None, add_spec=None, add_scale=1.0,
        dep=None):
    dot = {"nn": _dot, "nt": _dot_nt, "tn": _dot_tn}[mode]

    def body(*refs):
        a_ref, b_ref, o_ref = refs[0], refs[1], refs[-1]
        acc = dot(a_ref[...].astype(BF16), b_ref[...].astype(BF16))
        if add is not None:
            acc = acc + add_scale * refs[2][...]
        o_ref[...] = acc.astype(o_ref.dtype)

    ops = [a, b] if add is None else [a, b, add]
    specs = [a_spec, b_spec] if add is None else [a_spec, b_spec, add_spec]
    if dep is not None:
        ops.append(dep)
        specs.append(pl.BlockSpec(memory_space=pl.ANY))
    return pl.pallas_call(body, grid=grid, in_specs=specs, out_specs=out_spec, out_shape=out_shape,
                          compiler_params=_cp(*(("parallel",) * len(grid))), name=name)(*ops)


def _ln_stats(z):
    mu = jnp.mean(z, axis=-1, keepdims=True)
    zc = z - mu
    var = jnp.mean(zc * zc, axis=-1, keepdims=True)
    rstd = lax.rsqrt(var + LN_EPS)
    return zc * rstd, rstd


def _row_tile(t):
    return min(1024, t)


def _resid_ln(x, mix, g3, b3, l, name):
    t = x.shape[0]
    bm = _row_tile(t)

    def body(x_ref, m_ref, g_ref, b_ref, z_ref, y_ref, yb_ref):
        z = ALPHA * x_ref[...] + m_ref[...]
        xh, _ = _ln_stats(z)
        y = xh * g_ref[...] + b_ref[...]
        z_ref[...] = z
        y_ref[...] = y
        yb_ref[...] = y.astype(BF16)

    row = pl.BlockSpec((bm, D), lambda i: (i, 0))
    vec = pl.BlockSpec((None, 1, D), lambda i: (l, 0, 0))
    return pl.pallas_call(body, grid=(t // bm,), in_specs=[row, row, vec, vec], out_specs=[row, row, row],
                          out_shape=[S((t, D), F32), S((t, D), F32), S((t, D), BF16)],
                          compiler_params=_cp("parallel"), name=name)(x, mix, g3, b3)


def _proj_resid_ln(x, a, wmat, g3, b3, l, name):
    t, k = a.shape
    bm = _row_tile(t)

    def body(x_ref, a_ref, w_ref, g_ref, b_ref, z_ref, y_ref, yb_ref):
        z = ALPHA * x_ref[...] + _dot(a_ref[...], w_ref[...])
        xh, _ = _ln_stats(z)
        y = xh * g_ref[...] + b_ref[...]
        z_ref[...] = z
        y_ref[...] = y
        yb_ref[...] = y.astype(BF16)

    row = pl.BlockSpec((bm, D), lambda i: (i, 0))
    vec = pl.BlockSpec((None, 1, D), lambda i: (l, 0, 0))
    return pl.pallas_call(body, grid=(t // bm,),
                          in_specs=[row, pl.BlockSpec((bm, k), lambda i: (i, 0)), _full((k, D)), vec, vec],
                          out_specs=[row, row, row], out_shape=[S((t, D), F32), S((t, D), F32), S((t, D), BF16)],
                          compiler_params=_cp("parallel"), name=name)(x, a, wmat, g3, b3)


def _ln_bwd(d, z, g3, l, name, r=None, dep=None):
    t = z.shape[0]
    bm = _row_tile(t)

    def body(*refs):
        refs = list(refs)
        d_ref = refs.pop(0)
        dy = d_ref[...]
        if r is not None:
            dy = dy + ALPHA * refs.pop(0)[...]
        z_ref, g_ref = refs.pop(0), refs.pop(0)
        if dep is not None:
            refs.pop(0)
        dz_ref, dzb_ref, dg_ref, db_ref = refs
        xh, rstd = _ln_stats(z_ref[...])
        dyg = dy * g_ref[...]
        m1 = jnp.mean(dyg, axis=-1, keepdims=True)
        m2 = jnp.mean(dyg * xh, axis=-1, keepdims=True)
        dz = rstd * (dyg - m1 - xh * m2)
        dz_ref[...] = dz
        dzb_ref[...] = dz.astype(BF16)

        @pl.when(pl.program_id(0) == 0)
        def _():
            dg_ref[...] = jnp.zeros_like(dg_ref)
            db_ref[...] = jnp.zeros_like(db_ref)

        dg_ref[...] += jnp.sum(dy * xh, axis=0, keepdims=True)
        db_ref[...] += jnp.sum(dy, axis=0, keepdims=True)

    row = pl.BlockSpec((bm, D), lambda i: (i, 0))
    vec = pl.BlockSpec((None, 1, D), lambda i: (l, 0, 0))
    acc = pl.BlockSpec((1, D), lambda i: (0, 0))
    ops = [d, z, g3] if r is None else [d, r, z, g3]
    specs = [row, row, vec] if r is None else [row, row, row, vec]
    if dep is not None:
        ops.append(dep)
        specs.append(_full(dep.shape))
    return pl.pallas_call(body, grid=(t // bm,), in_specs=specs, out_specs=[row, row, acc, acc],
                          out_shape=[S((t, D), F32), S((t, D), BF16), S((1, D), F32), S((1, D), F32)],
                          compiler_params=_cp("arbitrary"), name=name)(*ops)


def _loss_grad(y, tgt):
    t = y.shape[0]
    bm = _row_tile(t)

    def body(y_ref, t_ref, dy_ref, loss_ref, acc_ref):
        i = pl.program_id(0)
        e = y_ref[...] - t_ref[...]
        dy_ref[...] = e * (1.0 / D)

        @pl.when(i == 0)
        def _():
            acc_ref[...] = jnp.zeros_like(acc_ref)

        acc_ref[...] += jnp.sum(e * e, axis=0, keepdims=True)

        @pl.when(i == pl.num_programs(0) - 1)
        def _():
            loss_ref[...] = jnp.full(loss_ref.shape, (0.5 / D) * jnp.sum(acc_ref[...]), F32)

    row = pl.BlockSpec((bm, D), lambda i: (i, 0))
    return pl.pallas_call(body, grid=(t // bm,), in_specs=[row, row],
                          out_specs=[row, pl.BlockSpec((1, 128), lambda i: (0, 0))],
                          out_shape=[S((t, D), F32), S((1, 128), F32)],
                          scratch_shapes=[pltpu.VMEM((1, D), F32)],
                          compiler_params=_cp("arbitrary"), name="loss_grad")(y, tgt)


def _mlp_row_tile(t):
    return min(1024, t)


def _mlp_fwd(y, yb, w1g, w2g, g3, b3, l, dep=None):
    t = yb.shape[0]
    bm = _mlp_row_tile(t)

    def body(*refs):
        y_ref, yb_ref, w1_ref, w2_ref, g_ref, b_ref = refs[:6]
        z_ref, o_ref, ob_ref, act_ref, acc_ref = refs[-5:]
        j = pl.program_id(1)
        h = jnp.maximum(_dot(yb_ref[...], w1_ref[...]), 0.0)
        act = (h * h).astype(BF16)
        act_ref[...] = act
        c = _dot(act, w2_ref[...])

        @pl.when(j == 0)
        def _():
            acc_ref[...] = c

        @pl.when(j > 0)
        def _():
            acc_ref[...] += c

        @pl.when(j == N_DEV - 1)
        def _():
            z = ALPHA * y_ref[...] + acc_ref[...]
            xh, _ = _ln_stats(z)
            out = xh * g_ref[...] + b_ref[...]
            z_ref[...] = z
            o_ref[...] = out
            ob_ref[...] = out.astype(BF16)

    row = pl.BlockSpec((bm, D), lambda i, j: (i, 0))
    vec = pl.BlockSpec((None, 1, D), lambda i, j: (l, 0, 0))
    deps = [] if dep is None else [dep]
    return pl.pallas_call(
        body, grid=(t // bm, N_DEV),
        in_specs=[row, row, pl.BlockSpec((None, D, FF_BLK), lambda i, j: (j, 0, 0)),
                  pl.BlockSpec((None, FF_BLK, D), lambda i, j: (j, 0, 0)), vec, vec] + [ANY] * len(deps),
        out_specs=[row, row, row, pl.BlockSpec((bm, FF_BLK), lambda i, j: (i, j))],
        out_shape=[S((t, D), F32), S((t, D), F32), S((t, D), BF16), S((t, D_FF), BF16)],
        scratch_shapes=[pltpu.VMEM((bm, D), F32)],
        compiler_params=_cp("parallel", "arbitrary"), name="mlp_fwd")(y, yb, w1g, w2g, g3, b3, *deps)


def _mlp_bwd_dh(act, dzb, w1g, w2g):
    t = act.shape[0]
    bm = t

    def body(a_ref, dz_ref, w1_ref, w2_ref, dh_ref, acc_ref):
        j = pl.program_id(1)
        r = jnp.sqrt(a_ref[...].astype(F32))
        da = _dot_nt(dz_ref[...], w2_ref[...])
        dh = (da * (2.0 * r)).astype(BF16)
        dh_ref[...] = dh
        c = _dot_nt(dh, w1_ref[...])

        @pl.when(j == 0)
        def _():
            acc_ref[...] = c

        @pl.when(j > 0)
        def _():
            acc_ref[...] += c

    row = pl.BlockSpec((bm, D), lambda i, j: (i, 0))
    hid = pl.BlockSpec((bm, FF_BLK), lambda i, j: (i, j))
    return pl.pallas_call(
        body, grid=(t // bm, N_DEV),
        in_specs=[hid, row,
                  pl.BlockSpec((None, D, FF_BLK), lambda i, j: (j, 0, 0)),
                  pl.BlockSpec((None, FF_BLK, D), lambda i, j: (j, 0, 0))],
        out_specs=[hid, row],
        out_shape=[S((t, D_FF), BF16), S((t, D), F32)],
        compiler_params=_cp("parallel", "arbitrary"), name="mlp_bwd_dh")(act, dzb, w1g, w2g)


def _shift_dn(x, k, rows, fill=0.0):
    return jnp.where(rows >= k, pltpu.roll(x, k, 0), fill)


def _shift_up(x, k, rows, fill=0.0):
    t = x.shape[0]
    return jnp.where(rows < t - k, pltpu.roll(x, t - k, 0), fill)


def _scan_rows(a, b, shift):
    rows = lax.broadcasted_iota(jnp.int32, a.shape, 0)
    k = 1
    t = a.shape[0]
    while k < t:
        b = a * shift(b, k, rows) + b
        if 2 * k < t:
            a = a * shift(a, k, rows, 1.0)
        k *= 2
    return b


def _scan_dn(a, b):
    return _scan_rows(a, b, _shift_dn)


def _scan_up(a, b):
    return _scan_rows(a, b, _shift_up)


def _window_sum_dn(x, w, rows):
    k = 1
    while k < w:
        x = x + _shift_dn(x, k, rows)
        k *= 2
    return x


def _window_sum_up(x, w, rows):
    k = 1
    while k < w:
        x = x + _shift_up(x, k, rows)
        k *= 2
    return x


def _pool_diff(u, w, rows):
    inv_count = 1.0 / jnp.minimum(rows + 1, w).astype(F32)
    return _window_sum_dn(u, w, rows) * inv_count - u, inv_count


def _pool_fwd(proj, pool_w, pool_scale3, j):
    t = proj.shape[0]

    def body(u_ref, w_ref, s_ref, y_ref):
        rows = lax.broadcasted_iota(jnp.int32, (t, HEAD), 0)
        for g, w in enumerate(POOL_WINDOWS):
            cols = slice(g * HEAD, (g + 1) * HEAD)
            d, _ = _pool_diff(u_ref[:, cols], w, rows)
            y = _dot(d.astype(BF16), w_ref[g].astype(BF16)) * s_ref[:, cols]
            y_ref[:, cols] = y.astype(BF16)

    return pl.pallas_call(
        body, grid=(1,),
        in_specs=[pl.BlockSpec((t, POOL_W), lambda i: (0, 0)),
                  pl.BlockSpec((None, 4, HEAD, HEAD), lambda i: (j, 0, 0, 0)),
                  pl.BlockSpec((None, 1, POOL_W), lambda i: (j, 0, 0))],
        out_specs=pl.BlockSpec((t, POOL_W), lambda i: (0, 0)),
        out_shape=S((t, POOL_W), BF16), compiler_params=_cp("arbitrary"), name="pool_fwd")(proj, pool_w, pool_scale3)


def _pool_bwd(proj, dycat, pool_w, pool_scale3, j):
    t = proj.shape[0]

    def body(u_ref, dy_ref, w_ref, s_ref, du_ref, dw_ref, ds_ref):
        rows = lax.broadcasted_iota(jnp.int32, (t, HEAD), 0)
        for g, w in enumerate(POOL_WINDOWS):
            cols = slice(g * HEAD, (g + 1) * HEAD)
            d, inv_count = _pool_diff(u_ref[:, cols], w, rows)
            db = d.astype(BF16)
            wg = w_ref[g].astype(BF16)
            dy = dy_ref[:, cols]
            ds_ref[:, cols] = jnp.sum(dy * _dot(db, wg), axis=0, keepdims=True)
            dzz = (dy * s_ref[:, cols]).astype(BF16)
            dw_ref[g] = _dot_tn(db, dzz)
            dd = _dot_nt(dzz, wg)
            du_ref[:, cols] = (_window_sum_up(dd * inv_count, w, rows) - dd).astype(BF16)

    return pl.pallas_call(
        body, grid=(1,),
        in_specs=[pl.BlockSpec((t, POOL_W), lambda i: (0, 0)),
                  pl.BlockSpec((t, POOL_W), lambda i: (0, 0)),
                  pl.BlockSpec((None, 4, HEAD, HEAD), lambda i: (j, 0, 0, 0)),
                  pl.BlockSpec((None, 1, POOL_W), lambda i: (j, 0, 0))],
        out_specs=[pl.BlockSpec((t, POOL_W), lambda i: (0, 0)), _full((4, HEAD, HEAD)), _full((1, POOL_W))],
        out_shape=[S((t, POOL_W), BF16), S((4, HEAD, HEAD), F32), S((1, POOL_W), F32)],
        compiler_params=_cp("arbitrary"), name="pool_bwd")(proj, dycat, pool_w, pool_scale3)


GELU_C = 0.7978845608028654
GELU_K = 0.044715


def _gelu(x):
    th = jnp.tanh(GELU_C * (x + GELU_K * x * x * x))
    return 0.5 * x * (1.0 + th), th


def _lru_forward(u, gate, cw, cb, wa, ba, wx, bx, lam, rows):
    v = cw[3:4] * u + cw[2:3] * _shift_dn(u, 1, rows) + cw[1:2] * _shift_dn(u, 2, rows) \
        + cw[0:1] * _shift_dn(u, 3, rows) + cb
    vb = v.astype(BF16)
    r = jax.nn.sigmoid(_dot(vb, wa) + ba)
    i = jax.nn.sigmoid(_dot(vb, wx) + bx)
    sp = jnp.maximum(-lam, 0.0) + jnp.log1p(jnp.exp(-jnp.abs(lam)))
    log_a = (-LRU_C) * r * sp
    a = jnp.exp(log_a)
    one_m_a2 = -jnp.tanh(log_a) * (a * a + 1.0)
    mult = jnp.sqrt(one_m_a2)
    h = _scan_dn(a, mult * (i * v))
    gl, th = _gelu(gate)
    return dict(v=v, vb=vb, r=r, i=i, sp=sp, a=a, mult=mult, h=h, gl=gl, th=th)


def _lru_specs(t, j, col0_u, col0_g):
    blk = lambda c0: pl.BlockSpec((t, HEAD), lambda h: (0, c0 + h))
    vec = pl.BlockSpec((None, 1, HEAD), lambda h: (j, 0, h))
    return [blk(col0_u), blk(col0_g),
            pl.BlockSpec((None, 4, HEAD), lambda h: (j, 0, h)), vec,
            pl.BlockSpec((None, None, HEAD, HEAD), lambda h: (j, h, 0, 0)), vec,
            pl.BlockSpec((None, None, HEAD, HEAD), lambda h: (j, h, 0, 0)), vec, vec]


def _lru_fwd(proj, p, j):
    t = proj.shape[0]

    def body(u_ref, g_ref, cw_ref, cb_ref, wa_ref, ba_ref, wx_ref, bx_ref, lam_ref, y_ref):
        rows = lax.broadcasted_iota(jnp.int32, (t, HEAD), 0)
        f = _lru_forward(u_ref[...], g_ref[...], cw_ref[...], cb_ref[...], wa_ref[...].astype(BF16), ba_ref[...],
                         wx_ref[...].astype(BF16), bx_ref[...], lam_ref[...], rows)
        y_ref[...] = (f["h"] * f["gl"]).astype(BF16)

    return pl.pallas_call(
        body, grid=(LRU_HEADS,), in_specs=_lru_specs(t, j, POOL_W // HEAD, (POOL_W + LRU_W) // HEAD),
        out_specs=pl.BlockSpec((t, HEAD), lambda h: (0, h)), out_shape=S((t, LRU_W), BF16),
        compiler_params=_cp("parallel"), name="lru_fwd")(
            proj, proj, p["conv_w"], p["conv_b"], p["w_a"], p["b_a"], p["w_x"], p["b_x"], p["lam"])


def _lru_bwd(proj, dycat, p, j):
    t = proj.shape[0]

    def body(u_ref, g_ref, cw_ref, cb_ref, wa_ref, ba_ref, wx_ref, bx_ref, lam_ref, dy_ref,
             du_ref, dgate_ref, dcw_ref, dcb_ref, dwa_ref, dba_ref, dwx_ref, dbx_ref, dlam_ref):
        rows = lax.broadcasted_iota(jnp.int32, (t, HEAD), 0)
        u = u_ref[...]
        gate = g_ref[...]
        cw = cw_ref[...]
        wa = wa_ref[...].astype(BF16)
        wx = wx_ref[...].astype(BF16)
        lam = lam_ref[...]
        f = _lru_forward(u, gate, cw, cb_ref[...], wa, ba_ref[...], wx, bx_ref[...], lam, rows)
        v, r, i, a, mult, h, th = f["v"], f["r"], f["i"], f["a"], f["mult"], f["h"], f["th"]
        dy = dy_ref[...]
        dgl = 0.5 * (1.0 + th) + 0.5 * gate * (1.0 - th * th) * GELU_C * (1.0 + 3.0 * GELU_K * gate * gate)
        dgate_ref[...] = (dy * h * dgl).astype(BF16)
        g = _scan_up(_shift_up(a, 1, rows), dy * f["gl"])
        da = g * _shift_dn(h, 1, rows)
        iv = i * v
        dmult = g * iv
        di = g * mult * v
        dv = g * mult * i
        dlog_a = da * a - dmult * (a * a) / mult
        dr = dlog_a * (-LRU_C) * f["sp"]
        dsp = jnp.sum(dlog_a * (-LRU_C) * r, axis=0, keepdims=True)
        dlam_ref[...] = -dsp * jax.nn.sigmoid(-lam)
        dpa = dr * r * (1.0 - r)
        dpx = di * i * (1.0 - i)
        dpab = dpa.astype(BF16)
        dpxb = dpx.astype(BF16)
        dwa_ref[...] = _dot_tn(f["vb"], dpab)
        dwx_ref[...] = _dot_tn(f["vb"], dpxb)
        dba_ref[...] = jnp.sum(dpa, axis=0, keepdims=True)
        dbx_ref[...] = jnp.sum(dpx, axis=0, keepdims=True)
        dv = dv + _dot_nt(dpab, wa) + _dot_nt(dpxb, wx)
        dcb_ref[...] = jnp.sum(dv, axis=0, keepdims=True)
        du = cw[3:4] * dv
        dcw_ref[3:4, :] = jnp.sum(dv * u, axis=0, keepdims=True)
        for k in (1, 2, 3):
            du = du + cw[3 - k:4 - k] * _shift_up(dv, k, rows)
            dcw_ref[3 - k:4 - k, :] = jnp.sum(dv * _shift_dn(u, k, rows), axis=0, keepdims=True)
        du_ref[...] = du.astype(BF16)

    blk = pl.BlockSpec((t, HEAD), lambda h: (0, h))
    vec = pl.BlockSpec((1, HEAD), lambda h: (0, h))
    mat = pl.BlockSpec((None, HEAD, HEAD), lambda h: (h, 0, 0))
    return pl.pallas_call(
        body, grid=(LRU_HEADS,),
        in_specs=_lru_specs(t, j, POOL_W // HEAD, (POOL_W + LRU_W) // HEAD)
        + [pl.BlockSpec((t, HEAD), lambda h: (0, POOL_W // HEAD + h))],
        out_specs=[blk, blk, pl.BlockSpec((4, HEAD), lambda h: (0, h)), vec, mat, vec, mat, vec, vec],
        out_shape=[S((t, LRU_W), BF16), S((t, LRU_W), BF16), S((4, LRU_W), F32), S((1, LRU_W), F32),
                   S((LRU_HEADS, HEAD, HEAD), F32), S((1, LRU_W), F32),
                   S((LRU_HEADS, HEAD, HEAD), F32), S((1, LRU_W), F32), S((1, LRU_W), F32)],
        compiler_params=_cp("parallel"), name="lru_bwd")(
            proj, proj, p["conv_w"], p["conv_b"], p["w_a"], p["b_a"], p["w_x"], p["b_x"], p["lam"], dycat)


def _rope(x, c, s):
    x1 = x[:, :ROPE // 2]
    x2 = x[:, ROPE // 2:]
    return jnp.concatenate([x1 * c - x2 * s, x1 * s + x2 * c], axis=-1)


def _rope_t(d, c, s):
    d1 = d[:, :ROPE // 2]
    d2 = d[:, ROPE // 2:]
    return jnp.concatenate([d1 * c + d2 * s, d2 * c - d1 * s], axis=-1)


def _rope_tables(pos2, inv_freq):
    t = pos2.shape[0]

    def body(p_ref, f_ref, c_ref, s_ref):
        ang = p_ref[...].astype(F32) * f_ref[...]
        c_ref[...] = jnp.cos(ang)
        s_ref[...] = jnp.sin(ang)

    return pl.pallas_call(body, out_shape=[S((t, ROPE // 2), F32), S((t, ROPE // 2), F32)],
                          name="rope_tables")(pos2, inv_freq)


def _down_norm(xb, wdown_g, gq3, gkv3, cos, sin, j):
    t = xb.shape[0]
    bm = _row_tile(t)

    def body(x_ref, w_ref, gq_ref, gkv_ref, c_ref, s_ref, down_ref, cq_ref, ckv_ref, kpe_ref):
        w = w_ref[...].reshape(D, ODD_IN)
        down = _dot(x_ref[...], w)
        down_ref[...] = down
        q = down[:, :Q_RANK]
        cq_ref[...] = (q * lax.rsqrt(jnp.mean(q * q, axis=-1, keepdims=True) + RMS_EPS) * gq_ref[...]).astype(BF16)
        kv = down[:, Q_RANK:Q_RANK + KV_RANK]
        ckv_ref[...] = (kv * lax.rsqrt(jnp.mean(kv * kv, axis=-1, keepdims=True) + RMS_EPS)
                        * gkv_ref[...]).astype(BF16)
        kpe_ref[...] = _rope(down[:, Q_RANK + KV_RANK:], c_ref[...], s_ref[...])

    row = lambda n: pl.BlockSpec((bm, n), lambda i: (i, 0))
    return pl.pallas_call(
        body, grid=(t // bm,),
        in_specs=[row(D), _full((N_DEV, D // N_DEV, ODD_IN)),
                  pl.BlockSpec((None, 1, Q_RANK), lambda i: (j, 0, 0)),
                  pl.BlockSpec((None, 1, KV_RANK), lambda i: (j, 0, 0)), row(ROPE // 2), row(ROPE // 2)],
        out_specs=[row(ODD_IN), row(Q_RANK), row(KV_RANK), row(ROPE)],
        out_shape=[S((t, ODD_IN), F32), S((t, Q_RANK), BF16), S((t, KV_RANK), BF16), S((t, ROPE), F32)],
        compiler_params=_cp("parallel"), name="down_norm")(xb, wdown_g, gq3, gkv3, cos, sin)


def _q_tile(t, widest):
    return min(widest, t // 2)


def _attn_probs(q, k, qs):
    s = _dot_nt(q, k) * ATT_SCALE
    tq = q.shape[0]
    rows = lax.broadcasted_iota(jnp.int32, (tq, tq), 0)
    cols = lax.broadcasted_iota(jnp.int32, (tq, tq), 1)
    last = jnp.where(jnp.right_shift(cols, CHUNK_SHIFT) <= jnp.right_shift(rows, CHUNK_SHIFT), s[:, qs:], NEG)
    s = last if qs == 0 else jnp.concatenate([s[:, :qs], last], axis=1)
    e = jnp.exp(s - jnp.max(s, axis=-1, keepdims=True))
    return e / jnp.sum(e, axis=-1, keepdims=True)


def _head_qkv(cq, ckv, kpe, c, s, wq_ref, wkv_ref):
    q = jnp.concatenate([_dot(cq, wq_ref[:, :NOPE]), _rope(_dot(cq, wq_ref[:, NOPE:]), c, s)], axis=1).astype(BF16)
    k = jnp.concatenate([_dot(ckv, wkv_ref[:, :NOPE]), kpe], axis=1).astype(BF16)
    vv = _dot(ckv, wkv_ref[:, NOPE:]).astype(BF16)
    return q, k, vv


def _attn_in_specs(t):
    return [_full((t, Q_RANK)), _full((t, KV_RANK)), _full((t, ROPE)), _full((t, ROPE // 2)), _full((t, ROPE // 2)),
            pl.BlockSpec((None, Q_RANK, NOPE + ROPE), lambda h: (h, 0, 0)),
            pl.BlockSpec((None, KV_RANK, NOPE + VDIM), lambda h: (h, 0, 0)),
            pl.BlockSpec((None, VDIM, D), lambda h: (h, 0, 0))]


def _attn_fwd(cq, ckv, kpe, cos, sin, wqb_g, wkvb_g, wo_g):
    t = cq.shape[0]
    tq = _q_tile(t, 256)

    def body(cq_ref, ckv_ref, kpe_ref, c_ref, s_ref, wq_ref, wkv_ref, wo_ref, o_ref, mix_ref):
        q, k, vv = _head_qkv(cq_ref[...], ckv_ref[...], kpe_ref[...], c_ref[...], s_ref[...], wq_ref, wkv_ref)
        for qs in range(0, t, tq):
            ke = qs + tq
            p = _attn_probs(q[qs:ke], k[:ke], qs)
            o_ref[qs:ke, :] = _dot(p.astype(BF16), vv[:ke]).astype(BF16)
        c = _dot(o_ref[...], wo_ref[...])

        @pl.when(pl.program_id(0) == 0)
        def _():
            mix_ref[...] = c

        @pl.when(pl.program_id(0) > 0)
        def _():
            mix_ref[...] += c

    return pl.pallas_call(
        body, grid=(MLA_HEADS,), in_specs=_attn_in_specs(t),
        out_specs=[pl.BlockSpec((None, t, VDIM), lambda h: (h, 0, 0)), _full((t, D))],
        out_shape=[S((MLA_HEADS, t, VDIM), BF16), S((t, D), F32)],
        compiler_params=_cp("arbitrary"), name="attn_fwd")(cq, ckv, kpe, cos, sin, wqb_g, wkvb_g, wo_g)


def _attn_bwd(cq, ckv, kpe, cos, sin, wqb_g, wkvb_g, wo_g, o, dzb):
    t = cq.shape[0]
    tq = _q_tile(t, 512)

    def body(cq_ref, ckv_ref, kpe_ref, c_ref, s_ref, wq_ref, wkv_ref, wo_ref, o_ref, dz_ref,
             dwo_ref, dwq_ref, dwkv_ref, dcq_ref, dckv_ref, dkpe_ref, dkt_s, dvt_s, dq_s):
        cqv = cq_ref[...]
        ckvv = ckv_ref[...]
        c = c_ref[...]
        s = s_ref[...]
        q, k, vv = _head_qkv(cqv, ckvv, kpe_ref[...], c, s, wq_ref, wkv_ref)
        dzv = dz_ref[...]
        dwo_ref[...] = _dot_tn(o_ref[...], dzv).astype(BF16)
        do = _dot_nt(dzv, wo_ref[...]).astype(BF16)
        dkt_s[...] = jnp.zeros_like(dkt_s)
        dvt_s[...] = jnp.zeros_like(dvt_s)
        for qs in range(0, t, tq):
            ke = qs + tq
            p = _attn_probs(q[qs:ke], k[:ke], qs)
            dp = _dot_nt(do[qs:ke], vv[:ke])
            ds = (p * (dp - jnp.sum(p * dp, axis=-1, keepdims=True)) * ATT_SCALE).astype(BF16)
            dq_s[qs:ke, :] = _dot(ds, k[:ke])
            dkt_s[0:NOPE + ROPE, 0:ke] += _dot_tn(q[qs:ke], ds)
            dvt_s[:, 0:ke] += _dot_tn(do[qs:ke], p.astype(BF16))
        dk = dkt_s[...].T
        dqn = dq_s[:, :NOPE].astype(BF16)
        dqp = _rope_t(dq_s[:, NOPE:], c, s).astype(BF16)
        dkn = dk[:, :NOPE].astype(BF16)
        dkp = dk[:, NOPE:NOPE + ROPE]
        dvv = dvt_s[...].T.astype(BF16)
        dwq_ref[:, :NOPE] = _dot_tn(cqv, dqn).astype(BF16)
        dwq_ref[:, NOPE:] = _dot_tn(cqv, dqp).astype(BF16)
        dwkv_ref[:, :NOPE] = _dot_tn(ckvv, dkn).astype(BF16)
        dwkv_ref[:, NOPE:] = _dot_tn(ckvv, dvv).astype(BF16)
        dcq = _dot_nt(dqn, wq_ref[:, :NOPE]) + _dot_nt(dqp, wq_ref[:, NOPE:])
        dckv = _dot_nt(dkn, wkv_ref[:, :NOPE]) + _dot_nt(dvv, wkv_ref[:, NOPE:])

        @pl.when(pl.program_id(0) == 0)
        def _():
            dcq_ref[...] = dcq
            dckv_ref[...] = dckv
            dkpe_ref[...] = dkp

        @pl.when(pl.program_id(0) > 0)
        def _():
            dcq_ref[...] += dcq
            dckv_ref[...] += dckv
            dkpe_ref[...] += dkp

    per_head = lambda a, b: pl.BlockSpec((None, a, b), lambda h: (h, 0, 0))
    return pl.pallas_call(
        body, grid=(MLA_HEADS,),
        in_specs=_attn_in_specs(t) + [per_head(t, VDIM), _full((t, D))],
        out_specs=[per_head(VDIM, D), per_head(Q_RANK, NOPE + ROPE), per_head(KV_RANK, NOPE + VDIM),
                   _full((t, Q_RANK)), _full((t, KV_RANK)), _full((t, ROPE))],
        out_shape=[S((MLA_HEADS, VDIM, D), BF16), S((MLA_HEADS, Q_RANK, NOPE + ROPE), BF16),
                   S((MLA_HEADS, KV_RANK, NOPE + VDIM), BF16),
                   S((t, Q_RANK), F32), S((t, KV_RANK), F32), S((t, ROPE), F32)],
        scratch_shapes=[pltpu.VMEM((2 * NOPE, t), F32), pltpu.VMEM((VDIM, t), F32),
                        pltpu.VMEM((t, NOPE + ROPE), F32)],
        compiler_params=_cp("arbitrary"), name="attn_bwd")(cq, ckv, kpe, cos, sin, wqb_g, wkvb_g, wo_g, o, dzb)


def _rms_bwd(down, dcq, dckv, dkpe, cos, sin, gq3, gkv3, j):
    t = down.shape[0]
    bm = _row_tile(t)

    def body(down_ref, dcq_ref, dckv_ref, dkpe_ref, c_ref, s_ref, gq_ref, gkv_ref, dd_ref, dgq_ref, dgkv_ref):
        @pl.when(pl.program_id(0) == 0)
        def _():
            dgq_ref[...] = jnp.zeros_like(dgq_ref)
            dgkv_ref[...] = jnp.zeros_like(dgkv_ref)

        def rms_b(x, dy, g):
            rstd = lax.rsqrt(jnp.mean(x * x, axis=-1, keepdims=True) + RMS_EPS)
            xh = x * rstd
            dyg = dy * g
            return rstd * (dyg - xh * jnp.mean(dyg * xh, axis=-1, keepdims=True)), jnp.sum(dy * xh, axis=0, keepdims=True)

        dq, dgq = rms_b(down_ref[:, :Q_RANK], dcq_ref[...], gq_ref[...])
        dkv, dgkv = rms_b(down_ref[:, Q_RANK:Q_RANK + KV_RANK], dckv_ref[...], gkv_ref[...])
        dgq_ref[...] += dgq
        dgkv_ref[...] += dgkv
        dd_ref[:, :Q_RANK] = dq.astype(BF16)
        dd_ref[:, Q_RANK:Q_RANK + KV_RANK] = dkv.astype(BF16)
        dd_ref[:, Q_RANK + KV_RANK:] = _rope_t(dkpe_ref[...], c_ref[...], s_ref[...]).astype(BF16)

    row = lambda n: pl.BlockSpec((bm, n), lambda i: (i, 0))
    return pl.pallas_call(
        body, grid=(t // bm,),
        in_specs=[row(ODD_IN), row(Q_RANK), row(KV_RANK), row(ROPE), row(ROPE // 2), row(ROPE // 2),
                  pl.BlockSpec((None, 1, Q_RANK), lambda i: (j, 0, 0)),
                  pl.BlockSpec((None, 1, KV_RANK), lambda i: (j, 0, 0))],
        out_specs=[row(ODD_IN), _full((1, Q_RANK)), _full((1, KV_RANK))],
        out_shape=[S((t, ODD_IN), BF16), S((1, Q_RANK), F32), S((1, KV_RANK), F32)],
        compiler_params=_cp("arbitrary"), name="rms_bwd")(down, dcq, dckv, dkpe, cos, sin, gq3, gkv3)


def _col_blocks(t, n, bn):
    return pl.BlockSpec((t, bn), lambda i: (0, i))


def _row_blocks(n, bm):
    return pl.BlockSpec((bm, n), lambda i: (i, 0))


def _local_step(x, pos2, tgt, small, weights_of, grads_done, start_dep=None, prefetch=None):
    t = x.shape[0]
    bm = _row_tile(t)
    inv_freq = (ROPE_THETA ** (-jnp.arange(0, ROPE, 2, dtype=F32) / ROPE)).reshape(1, ROPE // 2)
    cos, sin = _rope_tables(pos2, inv_freq)
    lru_p = {k: small[k] for k in ("conv_w", "conv_b", "w_a", "b_a", "w_x", "b_x", "lam")}

    saved = []
    y, yb = x, x.astype(BF16)
    for l in range(DEPTH):
        j = l // 2
        big = weights_of(l, 0, y)
        sv = dict(xb=yb, big=big)
        if l % 2 == 0:
            proj = _mm(yb, big["win_t"], mode="nt", grid=(EVEN_IN // 512,), a_spec=_full((t, D)),
                       b_spec=_row_blocks(D, 512), out_shape=S((t, EVEN_IN), F32),
                       out_spec=_col_blocks(t, EVEN_IN, 512), name="even_proj", dep=start_dep if l == 0 else None)
            ycat = jnp.concatenate([_pool_fwd(proj, small["pool_w"], small["pool_scale"], j),
                                    _lru_fwd(proj, lru_p, j)], axis=1)
            big.update(weights_of(l, 1, ycat))
            z1, y1, y1b = _proj_resid_ln(y, ycat, big["wout2d"], small["ln_mix_g"], small["ln_mix_b"], l, "even_out")
            sv.update(proj=proj, ycat=ycat)
        else:
            down, cq, ckv, kpe = _down_norm(yb, big["wdown"], small["gq"], small["gkv"], cos, sin, j)
            o, mix = _attn_fwd(cq, ckv, kpe, cos, sin, big["wqb"], big["wkvb"], big["wo"])
            z1, y1, y1b = _resid_ln(y, mix, small["ln_mix_g"], small["ln_mix_b"], l, "resid_ln")
            sv.update(down=down, cq=cq, ckv=ckv, kpe=kpe, o=o)
        fetched = prefetch(l + 1, y1) if prefetch is not None and l + 1 < DEPTH else None
        z2, y, yb, act = _mlp_fwd(y1, y1b, big["w1"], big["w2"], small["ln_ffn_g"], small["ln_ffn_b"], l,
                                  dep=fetched)
        sv.update(z1=z1, y1b=y1b, z2=z2, act=act)
        saved.append(sv)

    dy, loss_tile = _loss_grad(y, tgt)

    g = {k: [None] * n for k, n in (("ln_mix_g", 4), ("ln_mix_b", 4), ("ln_ffn_g", 4), ("ln_ffn_b", 4),
                                    ("pool_w", 2), ("pool_scale", 2), ("conv_w", 2), ("conv_b", 2),
                                    ("w_a", 2), ("b_a", 2), ("w_x", 2), ("b_x", 2), ("lam", 2),
                                    ("gq", 2), ("gkv", 2))}
    dep = None
    for l in reversed(range(DEPTH)):
        j = l // 2
        sv = saved[l]
        big = sv["big"]
        dz2, dz2b, g["ln_ffn_g"][l], g["ln_ffn_b"][l] = _ln_bwd(dy, sv["z2"], small["ln_ffn_g"], l, "ln_bwd", dep=dep)
        act = sv["act"]
        dh, dff = _mlp_bwd_dh(act, dz2b, big["w1"], big["w2"])
        dw1 = _mm(sv["y1b"], dh, mode="tn", grid=(N_DEV,), a_spec=_full((t, D)),
                  b_spec=_col_blocks(t, D_FF, FF_BLK), out_shape=S((N_DEV, D, FF_BLK), BF16),
                  out_spec=pl.BlockSpec((None, D, FF_BLK), lambda i: (i, 0, 0)), name="mlp_dw1")
        dw2 = _mm(act, dz2b, mode="tn", grid=(N_DEV,), a_spec=_col_blocks(t, D_FF, FF_BLK),
                  b_spec=_full((t, D)), out_shape=S((N_DEV, FF_BLK, D), BF16),
                  out_spec=pl.BlockSpec((None, FF_BLK, D), lambda i: (i, 0, 0)), name="mlp_dw2")
        dep = grads_done(l, dict(w1=dw1, w2=dw2))
        dz1, dz1b, g["ln_mix_g"][l], g["ln_mix_b"][l] = _ln_bwd(dff, sv["z1"], small["ln_mix_g"], l, "ln_bwd_res",
                                                                 r=dz2, dep=dep)
        if l % 2 == 0:
            wout = big["wout2d"]
            dycat = _mm(dz1b, wout, mode="nt", grid=(EVEN_MIX // 512,), a_spec=_full((t, D)),
                        b_spec=_row_blocks(D, 512), out_shape=S((t, EVEN_MIX), F32),
                        out_spec=_col_blocks(t, EVEN_MIX, 512), name="even_dycat")
            dwout = _mm(sv["ycat"], dz1b, mode="tn", grid=(EVEN_MIX // 512,), a_spec=_col_blocks(t, EVEN_MIX, 512),
                        b_spec=_full((t, D)), out_shape=S((EVEN_MIX, D), BF16), out_spec=_row_blocks(D, 512),
                        name="even_dwout")
            du_pool, g["pool_w"][j], g["pool_scale"][j] = _pool_bwd(sv["proj"], dycat, small["pool_w"],
                                                                   small["pool_scale"], j)
            (du_lru, du_gate, g["conv_w"][j], g["conv_b"][j], g["w_a"][j], g["b_a"][j], g["w_x"][j], g["b_x"][j],
             g["lam"][j]) = _lru_bwd(sv["proj"], dycat, lru_p, j)
            dproj = jnp.concatenate([du_pool, du_lru, du_gate], axis=1)
            dwin = _mm(sv["xb"], dproj, mode="tn", grid=(EVEN_IN // 512,), a_spec=_full((t, D)),
                       b_spec=_col_blocks(t, EVEN_IN, 512), out_shape=S((D, EVEN_IN), BF16),
                       out_spec=_col_blocks(D, EVEN_IN, 512), name="even_dwin")
            dep = grads_done(l, dict(win=dwin.reshape(D, N_DEV, EVEN_IN // N_DEV).transpose(1, 0, 2),
                                     wout=dwout.reshape(N_DEV, EVEN_MIX // N_DEV, D)))
            dy = _mm(dproj, big["win_t"], mode="nn", grid=(t // bm,), a_spec=_row_blocks(EVEN_IN, bm),
                     b_spec=_full((EVEN_IN, D)), out_shape=S((t, D), F32), out_spec=_row_blocks(D, bm),
                     add=dz1, add_spec=_row_blocks(D, bm), add_scale=ALPHA, name="even_dx", dep=dep)
        else:
            dwo, dwqb, dwkvb, dcq, dckv, dkpe = _attn_bwd(
                sv["cq"], sv["ckv"], sv["kpe"], cos, sin, big["wqb"], big["wkvb"], big["wo"], sv["o"], dz1b)
            ddown, g["gq"][j], g["gkv"][j] = _rms_bwd(sv["down"], dcq, dckv, dkpe, cos, sin, small["gq"],
                                                     small["gkv"], j)
            dwdown = _mm(sv["xb"], ddown, mode="tn", grid=(N_DEV,), a_spec=_col_blocks(t, D, D // N_DEV),
                         b_spec=_full((t, ODD_IN)), out_shape=S((N_DEV, D // N_DEV, ODD_IN), BF16),
                         out_spec=pl.BlockSpec((None, D // N_DEV, ODD_IN), lambda i: (i, 0, 0)),
                         name="odd_dwdown")
            dep = grads_done(l, dict(wdown=dwdown, wqb=dwqb, wkvb=dwkvb, wo=dwo))
            dy = _mm(ddown, big["wdown2d"], mode="nt", grid=(t // bm,), a_spec=_row_blocks(ODD_IN, bm),
                     b_spec=_full((D, ODD_IN)), out_shape=S((t, D), F32), out_spec=_row_blocks(D, bm),
                     add=dz1, add_spec=_row_blocks(D, bm), add_scale=ALPHA, name="odd_dx", dep=dep)
    return loss_tile[0, 0], dy, g


def _mesh_place():
    x, y, c = lax.axis_index("x"), lax.axis_index("y"), lax.axis_index("c")
    return x, y, c


def _peer(place, k):
    x, y, c = place
    return (1 - x if k & 4 else x, 1 - y if k & 2 else y, 1 - c if k & 1 else c)


def _index(place):
    x, y, c = place
    return 4 * x + 2 * y + c


ANY = pl.BlockSpec(memory_space=pl.ANY)


def _make_zones(shards, me, name):
    n = len(shards)

    def body(me_ref, *refs):
        for src, dst in zip(refs[:n], refs[n:]):
            dst[...] = src[...].astype(BF16)

    grid_spec = pltpu.PrefetchScalarGridSpec(
        num_scalar_prefetch=1, grid=(1,),
        in_specs=[pl.BlockSpec(s.shape, lambda i, me_ref: (0, 0)) for s in shards],
        out_specs=[pl.BlockSpec((None,) + s.shape, lambda i, me_ref: (me_ref[0], 0, 0)) for s in shards])
    return pl.pallas_call(body, grid_spec=grid_spec, out_shape=[S((N_DEV,) + s.shape, BF16) for s in shards],
                          compiler_params=_cp("arbitrary"), name=name)(me, *shards)


def _all_gather_big(zones):
    n = len(zones)

    def body(*refs):
        outs = refs[n:2 * n]
        send, recv = refs[2 * n:]
        x, y, c = _mesh_place()
        me, sibling = (x, y, c), (x, y, 1 - c)
        chips = [(1 - x, y), (x, 1 - y), (1 - x, 1 - y)]

        def copy(w, k, block, to):
            blk = outs[w].at[_index(block)]
            return pltpu.make_async_remote_copy(src_ref=blk, dst_ref=blk, send_sem=send.at[w, k], recv_sem=recv.at[w, k],
                                                device_id=to, device_id_type=MESH)

        first = []
        for w in range(n):
            first.append(copy(w, 0, me, sibling))
            first += [copy(w, 1 + j, me, (*chip, c)) for j, chip in enumerate(chips)]
        for cp in first:
            cp.start()
        passed = []
        for w in range(n):
            for j, chip in enumerate(chips):
                copy(w, 1 + j, (*chip, c), me).wait_recv()
                cp = copy(w, 4 + j, (*chip, c), sibling)
                cp.start()
                passed.append(cp)
        for w in range(n):
            copy(w, 0, sibling, me).wait_recv()
            for j, chip in enumerate(chips):
                copy(w, 4 + j, (*chip, 1 - c), me).wait_recv()
        for cp in first + passed:
            cp.wait_send()

    return pl.pallas_call(
        body, in_specs=[ANY] * n, out_specs=[ANY] * n, out_shape=[S(z.shape, z.dtype) for z in zones],
        input_output_aliases={i: i for i in range(n)},
        scratch_shapes=[pltpu.SemaphoreType.DMA((n, N_DEV - 1)), pltpu.SemaphoreType.DMA((n, N_DEV - 1))],
        compiler_params=pltpu.CompilerParams(has_side_effects=True), name="all_gather_big")(*zones)


def _shard_rows_tile(a):
    return max(d for d in range(16, 257, 16) if a % d == 0)


HBM = pl.BlockSpec(memory_space=pltpu.HBM)
SEM = pl.BlockSpec(memory_space=pltpu.SEMAPHORE)
DATAFLOW = pltpu.SideEffectType.DATAFLOW_SIDE_EFFECTING


def _in_hbm(a):
    return pltpu.with_memory_space_constraint(a, pltpu.HBM)


def _gather_ici_copies(place, src, land, w):
    me = _index(place)
    return [(_peer(place, k), land.at[me], land.at[me]) for k in (1, 2, 4, 6)]


def _gather_d2d_copies(place, src, land, w):
    blocks = [_index(_peer(place, k)) for k in (2, 4, 6)]
    return [(_peer(place, 1), land.at[b], land.at[b]) for b in blocks]


GATHER_ICI = (4, _gather_ici_copies)
GATHER_D2D = (3, _gather_d2d_copies)


def _scatter_plan(layers):
    def copies(place, src, land, w):
        me = _index(place)
        mine = land.at[me] if layers[w] is None else land.at[me, layers[w]]
        return [(_peer(place, k), src.at[_index(_peer(place, k))], mine) for k in range(1, N_DEV)]
    return (N_DEV - 1, copies)


def _gather_all_copies(place, src, land, w):
    me = _index(place)
    return [(_peer(place, k), land.at[me], land.at[me]) for k in range(1, N_DEV)]


GATHER_ALL = (N_DEV - 1, _gather_all_copies)


def _sum_blocks(zone, part, me):
    r = part.shape[1]

    def body(me_ref, z_ref, p_ref, o_ref):
        acc = None
        for s in range(N_DEV):
            term = jnp.where(me_ref[0] == s, p_ref[...], z_ref[s])
            acc = term if acc is None else acc + term
        o_ref[...] = acc

    grid_spec = pltpu.PrefetchScalarGridSpec(
        num_scalar_prefetch=1, grid=(1,),
        in_specs=[pl.BlockSpec((N_DEV, r, 128), lambda i, me_ref: (0, 0, 0)),
                  pl.BlockSpec((None, r, 128), lambda i, me_ref: (me_ref[0], 0, 0))],
        out_specs=pl.BlockSpec((r, 128), lambda i, me_ref: (0, 0)))
    return pl.pallas_call(body, grid_spec=grid_spec, out_shape=S((r, 128), F32),
                          compiler_params=_cp("arbitrary"), name="sum_small")(me, zone, part)


def _exchange_start(srcs, lands, plan, name, after=()):
    ns, n = len(srcs), len(lands)
    n_in = ns + n + len(after)
    per, copies = plan

    def body(*refs):
        ins, land = refs[:ns], refs[ns:ns + n]
        send, recv = refs[n_in], refs[n_in + 1]
        token = refs[-1]
        place = _mesh_place()
        for i in range(per):
            for w in range(n):
                target, src, dst = copies(place, ins[w] if ns else None, land[w], w)[i]
                pltpu.make_async_remote_copy(src_ref=src, dst_ref=dst, send_sem=send.at[w * per + i],
                                             recv_sem=recv.at[w * per + i], device_id=target, device_id_type=MESH).start()
        token[...] = jnp.zeros_like(token)

    sems = pltpu.SemaphoreType.DMA((n * per,))
    thru = [pltpu.HBM(a.shape, a.dtype) for a in list(srcs) + list(lands)]
    out = pl.pallas_call(
        body, name=name, in_specs=[HBM] * (ns + n) + [ANY] * len(after),
        out_shape=(sems, sems, *thru, S((8, 128), F32)),
        out_specs=(SEM, SEM, *([HBM] * (ns + n)), pl.BlockSpec(memory_space=pltpu.VMEM)),
        input_output_aliases={i: 2 + i for i in range(ns + n)},
        compiler_params=pltpu.CompilerParams(has_side_effects=DATAFLOW),
    )(*[_in_hbm(a) for a in list(srcs) + list(lands)], *after)
    return out[0], out[1], list(out[2:2 + ns]), list(out[2 + ns:2 + ns + n]), out[-1]


def _exchange_wait(send, recv, srcs, lands, plan, after, name):
    ns, n = len(srcs), len(lands)
    per, copies = plan
    afters = tuple(after) if isinstance(after, (tuple, list)) else (after,)

    def body(*refs):
        ins, land = refs[:ns], refs[ns:ns + n]
        send_ref, recv_ref = refs[ns + n], refs[ns + n + 1]
        place = _mesh_place()
        for i in range(per):
            for w in range(n):
                target, src, dst = copies(place, ins[w] if ns else None, land[w], w)[i]
                cp = pltpu.make_async_remote_copy(src_ref=src, dst_ref=dst, send_sem=send_ref.at[w * per + i],
                                                  recv_sem=recv_ref.at[w * per + i], device_id=target,
                                                  device_id_type=MESH)
                cp.wait_send()
                cp.wait_recv()

    thru = [pltpu.HBM(a.shape, a.dtype) for a in list(srcs) + list(lands)]
    out = pl.pallas_call(
        body, name=name, in_specs=[HBM] * (ns + n) + [SEM, SEM] + [ANY] * len(afters),
        out_shape=tuple(thru), out_specs=tuple([HBM] * (ns + n)),
        input_output_aliases={i: i for i in range(ns + n)},
        compiler_params=pltpu.CompilerParams(has_side_effects=DATAFLOW),
    )(*srcs, *lands, send, recv, *afters)
    return list(out[:ns]), list(out[ns:])


def _all_reduce_small(part, name, deps=()):
    def body(*refs):
        p_ref = refs[0]
        o_ref, rbuf, send1, recv1, send2, recv2 = refs[-6:]
        place = _mesh_place()
        me = _index(place)
        rbuf[pl.ds(me, 1)] = p_ref[pl.ds(me, 1)]
        first = [pltpu.make_async_remote_copy(src_ref=p_ref.at[_index(_peer(place, k))], dst_ref=rbuf.at[me],
                                              send_sem=send1.at[k - 1], recv_sem=recv1.at[k - 1],
                                              device_id=_peer(place, k), device_id_type=MESH)
                 for k in range(1, N_DEV)]
        for cp in first:
            cp.start()
        for cp in first:
            cp.wait()
        acc = rbuf[0]
        for d in range(1, N_DEV):
            acc = acc + rbuf[d]
        o_ref[pl.ds(me, 1)] = acc[None]
        second = [pltpu.make_async_remote_copy(src_ref=o_ref.at[me], dst_ref=o_ref.at[me], send_sem=send2.at[k - 1],
                                               recv_sem=recv2.at[k - 1], device_id=_peer(place, k),
                                               device_id_type=MESH)
                  for k in range(1, N_DEV)]
        for cp in second:
            cp.start()
        for cp in second:
            cp.wait()

    vm = pl.BlockSpec(memory_space=pltpu.VMEM)
    ops = [part, *deps]
    return pl.pallas_call(
        body, in_specs=[vm] + [ANY] * len(deps), out_specs=vm, out_shape=S(part.shape, F32),
        scratch_shapes=[pltpu.VMEM(part.shape, F32)] + [pltpu.SemaphoreType.DMA((N_DEV - 1,))] * 4,
        compiler_params=pltpu.CompilerParams(has_side_effects=True, vmem_limit_bytes=VMEM_LIMIT), name=name)(*ops)


def _adamw(w, g, m, v):
    m = ADAM_B1 * m + (1.0 - ADAM_B1) * g
    v = ADAM_B2 * v + (1.0 - ADAM_B2) * (g * g)
    m_hat = m / (1.0 - ADAM_B1 ** ADAM_STEP)
    v_hat = v / (1.0 - ADAM_B2 ** ADAM_STEP)
    return -ADAM_LR * (m_hat / (jnp.sqrt(v_hat) + ADAM_EPS) + ADAM_WD * w), m, v


def _adam_big(parts, own, me, w, m, v, name):
    nl, a, b = w.shape
    ta = _shard_rows_tile(a)

    def body(me_ref, p_ref, *refs):
        own_refs, (w_ref, m_ref, v_ref, g_ref, d_ref, mo_ref, vo_ref) = refs[:nl], refs[nl:]
        layer = pl.program_id(0)
        mine = own_refs[0][...]
        for k in range(1, nl):
            mine = jnp.where(layer == k, own_refs[k][...], mine)
        g = None
        for s in range(N_DEV):
            term = jnp.where(me_ref[0] == s, mine, p_ref[s]).astype(F32)
            g = term if g is None else g + term
        g_ref[...] = g
        d_ref[...], mo_ref[...], vo_ref[...] = _adamw(w_ref[...], g, m_ref[...], v_ref[...])

    blk = pl.BlockSpec((None, ta, b), lambda l, i, me_ref: (l, i, 0))

    def own_spec(k):
        return pl.BlockSpec((None, ta, b), lambda l, i, me_ref: (me_ref[0], jnp.where(l == k, i, 0), 0))

    grid_spec = pltpu.PrefetchScalarGridSpec(
        num_scalar_prefetch=1, grid=(nl, a // ta),
        in_specs=[pl.BlockSpec((N_DEV, None, ta, b), lambda l, i, me_ref: (0, l, i, 0))]
        + [own_spec(k) for k in range(nl)] + [blk, blk, blk],
        out_specs=[blk] * 4)
    return pl.pallas_call(body, grid_spec=grid_spec, out_shape=[S(w.shape, F32)] * 4,
                          compiler_params=_cp("arbitrary", "arbitrary"), name=name)(me, parts, *own, w, m, v)


def _adam_small(gs, ws, ms, vs):
    n = len(gs)

    def body(*refs):
        ins, outs = refs[:4 * n], refs[4 * n:]
        for i in range(n):
            g_ref, w_ref, m_ref, v_ref = (ins[k * n + i] for k in range(4))
            outs[i][...], outs[n + i][...], outs[2 * n + i][...] = _adamw(w_ref[...], g_ref[...], m_ref[...], v_ref[...])

    out = pl.pallas_call(body, out_shape=[S(g.shape, F32) for g in gs] * 3, compiler_params=_cp(),
                         name="adam_small")(*gs, *ws, *ms, *vs)
    return out[:n], out[n:2 * n], out[2 * n:]


BIG = ("even_w_in", "even_w_out", "mla_w_down", "mla_w_qb", "mla_w_kvb", "mla_w_o", "mlp_w1", "mlp_w2")
BIG_KEY = dict(even_w_in="win", even_w_out="wout", mla_w_down="wdown", mla_w_qb="wqb", mla_w_kvb="wkvb",
               mla_w_o="wo", mlp_w1="w1", mlp_w2="w2")
SMALL = (("ln_mix_g", "ln_mix_g", None), ("ln_mix_b", "ln_mix_b", None), ("ln_ffn_g", "ln_ffn_g", None),
         ("ln_ffn_b", "ln_ffn_b", None), ("pool_w", "pool_w", None), ("pool_scale", "pool_scale", None),
         ("lru_conv_w", "conv_w", 2), ("lru_conv_b", "conv_b", None), ("lru_w_a", "w_a", None),
         ("lru_b_a", "b_a", None), ("lru_w_x", "w_x", None), ("lru_b_x", "b_x", None), ("lru_lambda", "lam", None),
         ("mla_q_norm_g", "gq", 1), ("mla_kv_norm_g", "gkv", 1))
WEIGHTS = ("ln_mix_g", "ln_mix_b", "ln_ffn_g", "ln_ffn_b", "even_w_in", "pool_w", "pool_scale", "lru_conv_w",
           "lru_conv_b", "lru_w_a", "lru_b_a", "lru_w_x", "lru_b_x", "lru_lambda", "even_w_out", "mla_w_down",
           "mla_q_norm_g", "mla_kv_norm_g", "mla_w_qb", "mla_w_kvb", "mla_w_o", "mlp_w1", "mlp_w2")
ALL_AXES = ("x", "y", "c")


def _layer_weights(l):
    j = l // 2
    if l % 2 == 0:
        mixer = [("win", "even_w_in", j), ("wout", "even_w_out", j)]
    else:
        mixer = [("wdown", "mla_w_down", j), ("wqb", "mla_w_qb", j), ("wkvb", "mla_w_kvb", j), ("wo", "mla_w_o", j)]
    return mixer + [("w1", "mlp_w1", l), ("w2", "mlp_w2", l)]


def _pack(arrays, multiple):
    flat = jnp.concatenate([a.reshape(-1) for a in arrays])
    pad = (-flat.shape[0]) % multiple
    return jnp.pad(flat, (0, pad))


def _unpack(flat, shapes):
    out, at = [], 0
    for shp in shapes:
        n = 1
        for s in shp:
            n *= s
        out.append(flat[at:at + n].reshape(shp))
        at += n
    return out


def _global_shape(local_shape, axis):
    if axis is None:
        return tuple(local_shape)
    return tuple(s * N_DEV if i == axis else s for i, s in enumerate(local_shape))


def _step(x, positions, tgt, w, m, v):
    t = x.shape[1]
    me = _index(_mesh_place())

    sharded = [(name, axis) for name, _, axis in SMALL if axis is not None]
    zeros_with_mine = [lax.dynamic_update_slice_in_dim(jnp.zeros(_global_shape(w[name].shape, axis), F32), w[name],
                                                       me * w[name].shape[axis], axis) for name, axis in sharded]
    chunk = N_DEV * 8 * 128
    gathered = _all_reduce_small(_pack(zeros_with_mine, chunk).reshape(N_DEV, -1, 128), "gather_small")
    full = dict(zip([name for name, _ in sharded],
                    _unpack(gathered.reshape(-1), [_global_shape(w[name].shape, axis) for name, axis in sharded])))

    def keys_of(l, part):
        keys = [key for key, _, _ in _layer_weights(l)]
        if l == 0:
            return keys[:1] if part == 0 else keys[1:]
        return keys if part == 0 else []

    shard_of = {(l, key): (w[name][i].T if key == "win" else w[name][i])
                for l in range(DEPTH) for key, name, i in _layer_weights(l)}
    me_arr = me.astype(jnp.int32).reshape(1)
    first = _all_gather_big(_make_zones([shard_of[0, key] for key in keys_of(0, 0)], me_arr, "zones_0_0"))
    flights, after = {}, (first[0], gathered)
    for l in range(DEPTH):
        for part in (0, 1):
            if (l, part) != (0, 0) and keys_of(l, part):
                zones = _make_zones([shard_of[l, key] for key in keys_of(l, part)], me_arr, "zones_%d_%d" % (l, part))
                send, recv, _, lands, token = _exchange_start([], zones, GATHER_ICI, "gather_start_%d_%d" % (l, part),
                                                              after=after)
                flights[l, part] = (send, recv, [], lands)
                after = (token,)

    passing = {}

    def pass_on(l, part, after):
        tag = "%d_%d" % (l, part)
        _, lands = _exchange_wait(*flights[l, part], GATHER_ICI, after, "gather_wait_" + tag)
        send, recv, _, lands, token = _exchange_start([], lands, GATHER_D2D, "gather_pass_" + tag)
        passing[l, part] = (send, recv, [], lands)
        return token

    def early_pass(l, after):
        return pass_on(l, 0, after) if l >= 2 else None

    def weights_of(l, part, after):
        keys = keys_of(l, part)
        if (l, part) == (0, 0):
            arrays = first
        elif keys:
            if (l, part) not in passing:
                pass_on(l, part, after)
            _, arrays = _exchange_wait(*passing[l, part], GATHER_D2D, after, "gather_pass_wait_%d_%d" % (l, part))
        big = dict(zip(keys, arrays)) if keys else {}
        if "win" in big:
            big["win_t"] = big["win"].reshape(EVEN_IN, D)
        if "wout" in big:
            big["wout2d"] = big["wout"].reshape(EVEN_MIX, D)
        if "wdown" in big:
            big["wdown2d"] = big["wdown"].reshape(D, ODD_IN)
        return big

    zone = {name: lax.empty((N_DEV,) + w[name].shape, BF16) for name in BIG}
    name_of = {key: name for name, key in BIG_KEY.items()}
    sent, last_token = [], [None]

    def grads_done(l, grads):
        keys = list(grads)
        index = {key: i for key, _, i in _layer_weights(l)}
        layers = [index[key] for key in keys]
        send, recv, srcs, lands, tok = _exchange_start([grads[k] for k in keys], [zone[name_of[k]] for k in keys],
                                                       _scatter_plan(layers), "scatter_start_%d_%s" % (l, keys[0]))
        for k, land in zip(keys, lands):
            zone[name_of[k]] = land
        sent.append((send, recv, srcs, keys, layers))
        last_token[0] = tok
        return tok

    row3 = lambda a: a.reshape(a.shape[0], 1, a.shape[1])
    small = dict(ln_mix_g=row3(w["ln_mix_g"]), ln_mix_b=row3(w["ln_mix_b"]), ln_ffn_g=row3(w["ln_ffn_g"]),
                 ln_ffn_b=row3(w["ln_ffn_b"]), pool_w=w["pool_w"], pool_scale=row3(w["pool_scale"]),
                 conv_w=full["lru_conv_w"], conv_b=row3(w["lru_conv_b"]), w_a=w["lru_w_a"], b_a=row3(w["lru_b_a"]),
                 w_x=w["lru_w_x"], b_x=row3(w["lru_b_x"]), lam=row3(w["lru_lambda"]),
                 gq=row3(full["mla_q_norm_g"]), gkv=row3(full["mla_kv_norm_g"]))

    loss_part, grad_x, g = _local_step(x[0], positions.reshape(t, 1), tgt[0], small, weights_of, grads_done,
                                       start_dep=token, prefetch=early_pass)

    own = {name: [None] * w[name].shape[0] for name in BIG}
    me_arr = me.astype(jnp.int32).reshape(1)
    out = {}
    local_g = [jnp.stack(g[key]).reshape(_global_shape(w[name].shape, axis)) for name, key, axis in SMALL]
    local_g.append(loss_part.reshape(1))
    part = _pack(local_g, chunk).reshape(N_DEV, -1, 128)
    small_plan = _scatter_plan([None])
    s_send, s_recv, s_src, s_land, after = _exchange_start([part], [lax.empty(part.shape, F32)], small_plan,
                                                           "small_scatter_start", after=(last_token[0],))
    for n_flight, (send, recv, srcs, keys, layers) in enumerate(sent):
        if n_flight == len(sent) - 1:
            for name in BIG:
                if BIG_KEY[name] not in keys:
                    out[name] = _adam_big(zone[name], own[name], me_arr, w[name], m[name], v[name], "adam_" + name)
            s_src, s_land = _exchange_wait(s_send, s_recv, s_src, s_land, small_plan,
                                           [grad_x] + [o[0] for o in out.values()], "small_scatter_wait")
            chunk_sum = _sum_blocks(s_land[0], s_src[0], me_arr)
            r_zone = lax.dynamic_update_slice_in_dim(lax.empty(part.shape, F32), chunk_sum[None], me, 0)
            r_send, r_recv, _, r_land, after = _exchange_start([], [r_zone], GATHER_ALL, "small_gather_start")
        srcs, lands = _exchange_wait(send, recv, srcs, [zone[name_of[k]] for k in keys], _scatter_plan(layers),
                                     after, "scatter_wait_%d" % n_flight)
        for k, land, src, layer in zip(keys, lands, srcs, layers):
            zone[name_of[k]] = land
            own[name_of[k]][layer] = src
        after = lands[0]
    for name in BIG:
        if name not in out:
            out[name] = _adam_big(zone[name], own[name], me_arr, w[name], m[name], v[name], "adam_" + name)

    _, reduced = _exchange_wait(r_send, r_recv, [], r_land, GATHER_ALL, [out[name][0] for name in BIG],
                                "small_gather_wait")
    reduced = _unpack(reduced[0].reshape(-1), [a.shape for a in local_g])
    loss = reduced[-1][0]
    mine = [a if axis is None else lax.dynamic_slice_in_dim(a, me * w[name].shape[axis], w[name].shape[axis], axis)
            for a, (name, _, axis) in zip(reduced, SMALL)]
    names = [name for name, _, _ in SMALL]
    as_2d = lambda a: a.reshape(-1, a.shape[-1])
    new = _adam_small([as_2d(a) for a in mine], *([as_2d(src[name]) for name in names] for src in (w, m, v)))
    for i, name in enumerate(names):
        out[name] = (mine[i],) + tuple(part[i].reshape(w[name].shape) for part in new)

    return (loss, grad_x[None]) + tuple(out[name][i] for i in range(4) for name in WEIGHTS)


def kernel(x, positions, ln_mix_g, ln_mix_b, ln_ffn_g, ln_ffn_b, even_w_in, pool_w, pool_scale, lru_conv_w, lru_conv_b, lru_w_a, lru_b_a, lru_w_x, lru_b_x, lru_lambda, even_w_out, mla_w_down, mla_q_norm_g, mla_kv_norm_g, mla_w_qb, mla_w_kvb, mla_w_o, mlp_w1, mlp_w2, loss_target, m_ln_mix_g, m_ln_mix_b, m_ln_ffn_g, m_ln_ffn_b, m_even_w_in, m_pool_w, m_pool_scale, m_lru_conv_w, m_lru_conv_b, m_lru_w_a, m_lru_b_a, m_lru_w_x, m_lru_b_x, m_lru_lambda, m_even_w_out, m_mla_w_down, m_mla_q_norm_g, m_mla_kv_norm_g, m_mla_w_qb, m_mla_w_kvb, m_mla_w_o, m_mlp_w1, m_mlp_w2, v_ln_mix_g, v_ln_mix_b, v_ln_ffn_g, v_ln_ffn_b, v_even_w_in, v_pool_w, v_pool_scale, v_lru_conv_w, v_lru_conv_b, v_lru_w_a, v_lru_b_a, v_lru_w_x, v_lru_b_x, v_lru_lambda, v_even_w_out, v_mla_w_down, v_mla_q_norm_g, v_mla_kv_norm_g, v_mla_w_qb, v_mla_w_kvb, v_mla_w_o, v_mlp_w1, v_mlp_w2):
    w = dict(zip(WEIGHTS, (ln_mix_g, ln_mix_b, ln_ffn_g, ln_ffn_b, even_w_in, pool_w, pool_scale, lru_conv_w,
                           lru_conv_b, lru_w_a, lru_b_a, lru_w_x, lru_b_x, lru_lambda, even_w_out, mla_w_down,
                           mla_q_norm_g, mla_kv_norm_g, mla_w_qb, mla_w_kvb, mla_w_o, mlp_w1, mlp_w2)))
    m = dict(zip(WEIGHTS, (m_ln_mix_g, m_ln_mix_b, m_ln_ffn_g, m_ln_ffn_b, m_even_w_in, m_pool_w, m_pool_scale,
                           m_lru_conv_w, m_lru_conv_b, m_lru_w_a, m_lru_b_a, m_lru_w_x, m_lru_b_x, m_lru_lambda,
                           m_even_w_out, m_mla_w_down, m_mla_q_norm_g, m_mla_kv_norm_g, m_mla_w_qb, m_mla_w_kvb,
                           m_mla_w_o, m_mlp_w1, m_mlp_w2)))
    v = dict(zip(WEIGHTS, (v_ln_mix_g, v_ln_mix_b, v_ln_ffn_g, v_ln_ffn_b, v_even_w_in, v_pool_w, v_pool_scale,
                           v_lru_conv_w, v_lru_conv_b, v_lru_w_a, v_lru_b_a, v_lru_w_x, v_lru_b_x, v_lru_lambda,
                           v_even_w_out, v_mla_w_down, v_mla_q_norm_g, v_mla_kv_norm_g, v_mla_w_qb, v_mla_w_kvb,
                           v_mla_w_o, v_mlp_w1, v_mlp_w2)))
    return _step(x, positions, loss_target, w, m, v)
```

```python
import functools

import jax
import jax.numpy as jnp
from jax import lax
from jax.experimental import pallas as pl
from jax.experimental.pallas import tpu as pltpu

F32 = jnp.float32
BF16 = jnp.bfloat16
S = jax.ShapeDtypeStruct

D = 1024
DEPTH = 4
N_DEV = 8
CHUNK_SHIFT = 6
POOL_WINDOWS = (2, 4, 8, 16)
POOL_W = 512
LRU_W = 1024
LRU_HEADS = 8
HEAD = 128
LRU_C = 8.0
EVEN_IN = 2560
EVEN_MIX = 1536
MLA_HEADS = 8
NOPE = 128
ROPE = 64
VDIM = 128
Q_RANK = 384
KV_RANK = 256
ODD_IN = 704
D_FF = 4096
FF_BLK = D_FF // N_DEV
ROPE_THETA = 10000.0
ALPHA = (2 * DEPTH) ** 0.25
LN_EPS = 1e-5
RMS_EPS = 1e-6
ATT_SCALE = (NOPE + ROPE) ** -0.5
NEG = float(jnp.finfo(jnp.float32).min)
ADAM_LR = 0.001
ADAM_B1 = 0.9
ADAM_B2 = 0.999
ADAM_EPS = 1e-08
ADAM_WD = 0.01
ADAM_STEP = 10
V7X_VMEM_BYTES = 64 * 1024 * 1024
VMEM_LIMIT = V7X_VMEM_BYTES - 8 * 1024 * 1024
MESH = pl.DeviceIdType.MESH


def _cp(*sem):
    return pltpu.CompilerParams(dimension_semantics=sem if sem else None, vmem_limit_bytes=VMEM_LIMIT)


def _dot(a, b):
    return jnp.dot(a, b, preferred_element_type=F32)


def _dot_nt(a, b):
    return lax.dot_general(a, b, (((1,), (1,)), ((), ())), preferred_element_type=F32)


def _dot_tn(a, b):
    return lax.dot_general(a, b, (((0,), (0,)), ((), ())), preferred_element_type=F32)


def _full(shape):
    return pl.BlockSpec(shape, lambda *_: (0,) * len(shape))


def _mm(a, b, *, mode, grid, a_spec, b_spec, out_shape, out_spec, name, add=None, add_spec=None, add_scale=1.0,
        dep=None):
    dot = {"nn": _dot, "nt": _dot_nt, "tn": _dot_tn}[mode]

    def body(*refs):
        a_ref, b_ref, o_ref = refs[0], refs[1], refs[-1]
        acc = dot(a_ref[...].astype(BF16), b_ref[...].astype(BF16))
        if add is not None:
            acc = acc + add_scale * refs[2][...]
        o_ref[...] = acc.astype(o_ref.dtype)

    ops = [a, b] if add is None else [a, b, add]
    specs = [a_spec, b_spec] if add is None else [a_spec, b_spec, add_spec]
    if dep is not None:
        ops.append(dep)
        specs.append(pl.BlockSpec(memory_space=pl.ANY))
    return pl.pallas_call(body, grid=grid, in_specs=specs, out_specs=out_spec, out_shape=out_shape,
                          compiler_params=_cp(*(("parallel",) * len(grid))), name=name)(*ops)


def _ln_stats(z):
    mu = jnp.mean(z, axis=-1, keepdims=True)
    zc = z - mu
    var = jnp.mean(zc * zc, axis=-1, keepdims=True)
    rstd = lax.rsqrt(var + LN_EPS)
    return zc * rstd, rstd


def _row_tile(t):
    return min(1024, t)


def _resid_ln(x, mix, g3, b3, l, name):
    t = x.shape[0]
    bm = _row_tile(t)

    def body(x_ref, m_ref, g_ref, b_ref, z_ref, y_ref, yb_ref):
        z = ALPHA * x_ref[...] + m_ref[...]
        xh, _ = _ln_stats(z)
        y = xh * g_ref[...] + b_ref[...]
        z_ref[...] = z
        y_ref[...] = y
        yb_ref[...] = y.astype(BF16)

    row = pl.BlockSpec((bm, D), lambda i: (i, 0))
    vec = pl.BlockSpec((None, 1, D), lambda i: (l, 0, 0))
    return pl.pallas_call(body, grid=(t // bm,), in_specs=[row, row, vec, vec], out_specs=[row, row, row],
                          out_shape=[S((t, D), F32), S((t, D), F32), S((t, D), BF16)],
                          compiler_params=_cp("parallel"), name=name)(x, mix, g3, b3)


def _proj_resid_ln(x, a, wmat, g3, b3, l, name):
    t, k = a.shape
    bm = _row_tile(t)

    def body(x_ref, a_ref, w_ref, g_ref, b_ref, z_ref, y_ref, yb_ref):
        z = ALPHA * x_ref[...] + _dot(a_ref[...], w_ref[...])
        xh, _ = _ln_stats(z)
        y = xh * g_ref[...] + b_ref[...]
        z_ref[...] = z
        y_ref[...] = y
        yb_ref[...] = y.astype(BF16)

    row = pl.BlockSpec((bm, D), lambda i: (i, 0))
    vec = pl.BlockSpec((None, 1, D), lambda i: (l, 0, 0))
    return pl.pallas_call(body, grid=(t // bm,),
                          in_specs=[row, pl.BlockSpec((bm, k), lambda i: (i, 0)), _full((k, D)), vec, vec],
                          out_specs=[row, row, row], out_shape=[S((t, D), F32), S((t, D), F32), S((t, D), BF16)],
                          compiler_params=_cp("parallel"), name=name)(x, a, wmat, g3, b3)


def _ln_bwd(d, z, g3, l, name, r=None, dep=None):
    t = z.shape[0]
    bm = _row_tile(t)

    def body(*refs):
        refs = list(refs)
        d_ref = refs.pop(0)
        dy = d_ref[...]
        if r is not None:
            dy = dy + ALPHA * refs.pop(0)[...]
        z_ref, g_ref = refs.pop(0), refs.pop(0)
        if dep is not None:
            refs.pop(0)
        dz_ref, dzb_ref, dg_ref, db_ref = refs
        xh, rstd = _ln_stats(z_ref[...])
        dyg = dy * g_ref[...]
        m1 = jnp.mean(dyg, axis=-1, keepdims=True)
        m2 = jnp.mean(dyg * xh, axis=-1, keepdims=True)
        dz = rstd * (dyg - m1 - xh * m2)
        dz_ref[...] = dz
        dzb_ref[...] = dz.astype(BF16)

        @pl.when(pl.program_id(0) == 0)
        def _():
            dg_ref[...] = jnp.zeros_like(dg_ref)
            db_ref[...] = jnp.zeros_like(db_ref)

        dg_ref[...] += jnp.sum(dy * xh, axis=0, keepdims=True)
        db_ref[...] += jnp.sum(dy, axis=0, keepdims=True)

    row = pl.BlockSpec((bm, D), lambda i: (i, 0))
    vec = pl.BlockSpec((None, 1, D), lambda i: (l, 0, 0))
    acc = pl.BlockSpec((1, D), lambda i: (0, 0))
    ops = [d, z, g3] if r is None else [d, r, z, g3]
    specs = [row, row, vec] if r is None else [row, row, row, vec]
    if dep is not None:
        ops.append(dep)
        specs.append(_full(dep.shape))
    return pl.pallas_call(body, grid=(t // bm,), in_specs=specs, out_specs=[row, row, acc, acc],
                          out_shape=[S((t, D), F32), S((t, D), BF16), S((1, D), F32), S((1, D), F32)],
                          compiler_params=_cp("arbitrary"), name=name)(*ops)


def _loss_grad(y, tgt):
    t = y.shape[0]
    bm = _row_tile(t)

    def body(y_ref, t_ref, dy_ref, loss_ref, acc_ref):
        i = pl.program_id(0)
        e = y_ref[...] - t_ref[...]
        dy_ref[...] = e * (1.0 / D)

        @pl.when(i == 0)
        def _():
            acc_ref[...] = jnp.zeros_like(acc_ref)

        acc_ref[...] += jnp.sum(e * e, axis=0, keepdims=True)

        @pl.when(i == pl.num_programs(0) - 1)
        def _():
            loss_ref[...] = jnp.full(loss_ref.shape, (0.5 / D) * jnp.sum(acc_ref[...]), F32)

    row = pl.BlockSpec((bm, D), lambda i: (i, 0))
    return pl.pallas_call(body, grid=(t // bm,), in_specs=[row, row],
                          out_specs=[row, pl.BlockSpec((1, 128), lambda i: (0, 0))],
                          out_shape=[S((t, D), F32), S((1, 128), F32)],
                          scratch_shapes=[pltpu.VMEM((1, D), F32)],
                          compiler_params=_cp("arbitrary"), name="loss_grad")(y, tgt)


def _mlp_row_tile(t):
    return min(1024, t)


def _mlp_fwd(y, yb, w1g, w2g, g3, b3, l, dep=None):
    t = yb.shape[0]
    bm = _mlp_row_tile(t)

    def body(*refs):
        y_ref, yb_ref, w1_ref, w2_ref, g_ref, b_ref = refs[:6]
        z_ref, o_ref, ob_ref, act_ref, acc_ref = refs[-5:]
        j = pl.program_id(1)

        @pl.when(j == 0)
        def _():
            acc_ref[...] = jnp.zeros_like(acc_ref)

        for rows in (slice(0, bm // 2), slice(bm // 2, bm)):
            h = jnp.maximum(_dot(yb_ref[rows, :], w1_ref[...]), 0.0)
            act = (h * h).astype(BF16)
            act_ref[rows, :] = act
            acc_ref[rows, :] += _dot(act, w2_ref[...])

        @pl.when(j == N_DEV - 1)
        def _():
            z = ALPHA * y_ref[...] + acc_ref[...]
            xh, _ = _ln_stats(z)
            out = xh * g_ref[...] + b_ref[...]
            z_ref[...] = z
            o_ref[...] = out
            ob_ref[...] = out.astype(BF16)

    row = pl.BlockSpec((bm, D), lambda i, j: (i, 0))
    vec = pl.BlockSpec((None, 1, D), lambda i, j: (l, 0, 0))
    deps = [] if dep is None else [dep]
    return pl.pallas_call(
        body, grid=(t // bm, N_DEV),
        in_specs=[row, row, pl.BlockSpec((None, D, FF_BLK), lambda i, j: (j, 0, 0)),
                  pl.BlockSpec((None, FF_BLK, D), lambda i, j: (j, 0, 0)), vec, vec] + [ANY] * len(deps),
        out_specs=[row, row, row, pl.BlockSpec((bm, FF_BLK), lambda i, j: (i, j))],
        out_shape=[S((t, D), F32), S((t, D), F32), S((t, D), BF16), S((t, D_FF), BF16)],
        scratch_shapes=[pltpu.VMEM((bm, D), F32)],
        compiler_params=_cp("parallel", "arbitrary"), name="mlp_fwd")(y, yb, w1g, w2g, g3, b3, *deps)


def _mlp_bwd_dh(act, dzb, w1g, w2g):
    t = act.shape[0]
    bm = _mlp_row_tile(t)

    def body(a_ref, dz_ref, w1_ref, w2_ref, dh_ref, acc_ref):
        @pl.when(pl.program_id(1) == 0)
        def _():
            acc_ref[...] = jnp.zeros_like(acc_ref)

        for rows in (slice(0, bm // 2), slice(bm // 2, bm)):
            r = jnp.sqrt(a_ref[rows, :].astype(F32))
            dh = (_dot_nt(dz_ref[rows, :], w2_ref[...]) * (2.0 * r)).astype(BF16)
            dh_ref[rows, :] = dh
            acc_ref[rows, :] += _dot_nt(dh, w1_ref[...])

    row = pl.BlockSpec((bm, D), lambda i, j: (i, 0))
    hid = pl.BlockSpec((bm, FF_BLK), lambda i, j: (i, j))
    return pl.pallas_call(
        body, grid=(t // bm, N_DEV),
        in_specs=[hid, row,
                  pl.BlockSpec((None, D, FF_BLK), lambda i, j: (j, 0, 0)),
                  pl.BlockSpec((None, FF_BLK, D), lambda i, j: (j, 0, 0))],
        out_specs=[hid, row],
        out_shape=[S((t, D_FF), BF16), S((t, D), F32)],
        compiler_params=_cp("parallel", "arbitrary"), name="mlp_bwd_dh")(act, dzb, w1g, w2g)


F32_SUBLANES = 8


def _shift_dn(x, k, rows, fill=0.0):
    if k % F32_SUBLANES == 0:
        return jnp.concatenate([jnp.full((k,) + x.shape[1:], fill, x.dtype), x[:x.shape[0] - k]], axis=0)
    return jnp.where(rows >= k, pltpu.roll(x, k, 0), fill)


def _shift_up(x, k, rows, fill=0.0):
    t = x.shape[0]
    if k % F32_SUBLANES == 0:
        return jnp.concatenate([x[k:], jnp.full((k,) + x.shape[1:], fill, x.dtype)], axis=0)
    return jnp.where(rows < t - k, pltpu.roll(x, t - k, 0), fill)


def _scan_rows(a, b, shift):
    rows = lax.broadcasted_iota(jnp.int32, a.shape, 0)
    k = 1
    t = a.shape[0]
    while k < t:
        b = a * shift(b, k, rows) + b
        if 2 * k < t:
            a = a * shift(a, k, rows, 1.0)
        k *= 2
    return b


def _scan_dn(a, b):
    return _scan_rows(a, b, _shift_dn)


def _scan_up(a, b):
    return _scan_rows(a, b, _shift_up)


def _window_sum_dn(x, w, rows):
    k = 1
    while k < w:
        x = x + _shift_dn(x, k, rows)
        k *= 2
    return x


def _window_sum_up(x, w, rows):
    k = 1
    while k < w:
        x = x + _shift_up(x, k, rows)
        k *= 2
    return x


def _pool_diff(u, w, rows):
    inv_count = 1.0 / jnp.minimum(rows + 1, w).astype(F32)
    return _window_sum_dn(u, w, rows) * inv_count - u, inv_count


def _pool_fwd(proj, pool_w, pool_scale3, j):
    t = proj.shape[0]

    def body(u_ref, w_ref, s_ref, y_ref):
        rows = lax.broadcasted_iota(jnp.int32, (t, HEAD), 0)
        for g, w in enumerate(POOL_WINDOWS):
            cols = slice(g * HEAD, (g + 1) * HEAD)
            d, _ = _pool_diff(u_ref[:, cols], w, rows)
            y = _dot(d.astype(BF16), w_ref[g].astype(BF16)) * s_ref[:, cols]
            y_ref[:, cols] = y.astype(BF16)

    return pl.pallas_call(
        body, grid=(1,),
        in_specs=[pl.BlockSpec((t, POOL_W), lambda i: (0, 0)),
                  pl.BlockSpec((None, 4, HEAD, HEAD), lambda i: (j, 0, 0, 0)),
                  pl.BlockSpec((None, 1, POOL_W), lambda i: (j, 0, 0))],
        out_specs=pl.BlockSpec((t, POOL_W), lambda i: (0, 0)),
        out_shape=S((t, POOL_W), BF16), compiler_params=_cp("arbitrary"), name="pool_fwd")(proj, pool_w, pool_scale3)


def _pool_bwd(proj, dycat, pool_w, pool_scale3, j):
    t = proj.shape[0]

    def body(u_ref, dy_ref, w_ref, s_ref, du_ref, dw_ref, ds_ref):
        rows = lax.broadcasted_iota(jnp.int32, (t, HEAD), 0)
        for g, w in enumerate(POOL_WINDOWS):
            cols = slice(g * HEAD, (g + 1) * HEAD)
            d, inv_count = _pool_diff(u_ref[:, cols], w, rows)
            db = d.astype(BF16)
            wg = w_ref[g].astype(BF16)
            dy = dy_ref[:, cols]
            ds_ref[:, cols] = jnp.sum(dy * _dot(db, wg), axis=0, keepdims=True)
            dzz = (dy * s_ref[:, cols]).astype(BF16)
            dw_ref[g] = _dot_tn(db, dzz)
            dd = _dot_nt(dzz, wg)
            du_ref[:, cols] = (_window_sum_up(dd * inv_count, w, rows) - dd).astype(BF16)

    return pl.pallas_call(
        body, grid=(1,),
        in_specs=[pl.BlockSpec((t, POOL_W), lambda i: (0, 0)),
                  pl.BlockSpec((t, POOL_W), lambda i: (0, 0)),
                  pl.BlockSpec((None, 4, HEAD, HEAD), lambda i: (j, 0, 0, 0)),
                  pl.BlockSpec((None, 1, POOL_W), lambda i: (j, 0, 0))],
        out_specs=[pl.BlockSpec((t, POOL_W), lambda i: (0, 0)), _full((4, HEAD, HEAD)), _full((1, POOL_W))],
        out_shape=[S((t, POOL_W), BF16), S((4, HEAD, HEAD), F32), S((1, POOL_W), F32)],
        compiler_params=_cp("arbitrary"), name="pool_bwd")(proj, dycat, pool_w, pool_scale3)


GELU_C = 0.7978845608028654
GELU_K = 0.044715


def _gelu(x):
    th = jnp.tanh(GELU_C * (x + GELU_K * x * x * x))
    return 0.5 * x * (1.0 + th), th


def _lru_forward(u, gate, cw, cb, wa, ba, wx, bx, lam, rows):
    v = cw[3:4] * u + cw[2:3] * _shift_dn(u, 1, rows) + cw[1:2] * _shift_dn(u, 2, rows) \
        + cw[0:1] * _shift_dn(u, 3, rows) + cb
    vb = v.astype(BF16)
    r = jax.nn.sigmoid(_dot(vb, wa) + ba)
    i = jax.nn.sigmoid(_dot(vb, wx) + bx)
    sp = jnp.maximum(-lam, 0.0) + jnp.log1p(jnp.exp(-jnp.abs(lam)))
    log_a = (-LRU_C) * r * sp
    a = jnp.exp(log_a)
    one_m_a2 = -jnp.tanh(log_a) * (a * a + 1.0)
    mult = jnp.sqrt(one_m_a2)
    h = _scan_dn(a, mult * (i * v))
    gl, th = _gelu(gate)
    return dict(v=v, vb=vb, r=r, i=i, sp=sp, a=a, mult=mult, h=h, gl=gl, th=th)


def _lru_specs(t, j, col0_u, col0_g):
    blk = lambda c0: pl.BlockSpec((t, HEAD), lambda h: (0, c0 + h))
    vec = pl.BlockSpec((None, 1, HEAD), lambda h: (j, 0, h))
    return [blk(col0_u), blk(col0_g),
            pl.BlockSpec((None, 4, HEAD), lambda h: (j, 0, h)), vec,
            pl.BlockSpec((None, None, HEAD, HEAD), lambda h: (j, h, 0, 0)), vec,
            pl.BlockSpec((None, None, HEAD, HEAD), lambda h: (j, h, 0, 0)), vec, vec]


def _lru_fwd(proj, p, j):
    t = proj.shape[0]

    def body(u_ref, g_ref, cw_ref, cb_ref, wa_ref, ba_ref, wx_ref, bx_ref, lam_ref, y_ref):
        rows = lax.broadcasted_iota(jnp.int32, (t, HEAD), 0)
        f = _lru_forward(u_ref[...], g_ref[...], cw_ref[...], cb_ref[...], wa_ref[...].astype(BF16), ba_ref[...],
                         wx_ref[...].astype(BF16), bx_ref[...], lam_ref[...], rows)
        y_ref[...] = (f["h"] * f["gl"]).astype(BF16)

    return pl.pallas_call(
        body, grid=(LRU_HEADS,), in_specs=_lru_specs(t, j, POOL_W // HEAD, (POOL_W + LRU_W) // HEAD),
        out_specs=pl.BlockSpec((t, HEAD), lambda h: (0, h)), out_shape=S((t, LRU_W), BF16),
        compiler_params=_cp("parallel"), name="lru_fwd")(
            proj, proj, p["conv_w"], p["conv_b"], p["w_a"], p["b_a"], p["w_x"], p["b_x"], p["lam"])


def _lru_bwd(proj, dycat, p, j):
    t = proj.shape[0]

    def body(u_ref, g_ref, cw_ref, cb_ref, wa_ref, ba_ref, wx_ref, bx_ref, lam_ref, dy_ref,
             du_ref, dgate_ref, dcw_ref, dcb_ref, dwa_ref, dba_ref, dwx_ref, dbx_ref, dlam_ref):
        rows = lax.broadcasted_iota(jnp.int32, (t, HEAD), 0)
        u = u_ref[...]
        gate = g_ref[...]
        cw = cw_ref[...]
        wa = wa_ref[...].astype(BF16)
        wx = wx_ref[...].astype(BF16)
        lam = lam_ref[...]
        f = _lru_forward(u, gate, cw, cb_ref[...], wa, ba_ref[...], wx, bx_ref[...], lam, rows)
        v, r, i, a, mult, h, th = f["v"], f["r"], f["i"], f["a"], f["mult"], f["h"], f["th"]
        dy = dy_ref[...]
        dgl = 0.5 * (1.0 + th) + 0.5 * gate * (1.0 - th * th) * GELU_C * (1.0 + 3.0 * GELU_K * gate * gate)
        dgate_ref[...] = (dy * h * dgl).astype(BF16)
        g = _scan_up(_shift_up(a, 1, rows), dy * f["gl"])
        da = g * _shift_dn(h, 1, rows)
        iv = i * v
        dmult = g * iv
        di = g * mult * v
        dv = g * mult * i
        dlog_a = da * a - dmult * (a * a) / mult
        dr = dlog_a * (-LRU_C) * f["sp"]
        dsp = jnp.sum(dlog_a * (-LRU_C) * r, axis=0, keepdims=True)
        dlam_ref[...] = -dsp * jax.nn.sigmoid(-lam)
        dpa = dr * r * (1.0 - r)
        dpx = di * i * (1.0 - i)
        dpab = dpa.astype(BF16)
        dpxb = dpx.astype(BF16)
        dwa_ref[...] = _dot_tn(f["vb"], dpab)
        dwx_ref[...] = _dot_tn(f["vb"], dpxb)
        dba_ref[...] = jnp.sum(dpa, axis=0, keepdims=True)
        dbx_ref[...] = jnp.sum(dpx, axis=0, keepdims=True)
        dv = dv + _dot_nt(dpab, wa) + _dot_nt(dpxb, wx)
        dcb_ref[...] = jnp.sum(dv, axis=0, keepdims=True)
        du = cw[3:4] * dv
        dcw_ref[3:4, :] = jnp.sum(dv * u, axis=0, keepdims=True)
        for k in (1, 2, 3):
            du = du + cw[3 - k:4 - k] * _shift_up(dv, k, rows)
            dcw_ref[3 - k:4 - k, :] = jnp.sum(dv * _shift_dn(u, k, rows), axis=0, keepdims=True)
        du_ref[...] = du.astype(BF16)

    blk = pl.BlockSpec((t, HEAD), lambda h: (0, h))
    vec = pl.BlockSpec((1, HEAD), lambda h: (0, h))
    mat = pl.BlockSpec((None, HEAD, HEAD), lambda h: (h, 0, 0))
    return pl.pallas_call(
        body, grid=(LRU_HEADS,),
        in_specs=_lru_specs(t, j, POOL_W // HEAD, (POOL_W + LRU_W) // HEAD)
        + [pl.BlockSpec((t, HEAD), lambda h: (0, POOL_W // HEAD + h))],
        out_specs=[blk, blk, pl.BlockSpec((4, HEAD), lambda h: (0, h)), vec, mat, vec, mat, vec, vec],
        out_shape=[S((t, LRU_W), BF16), S((t, LRU_W), BF16), S((4, LRU_W), F32), S((1, LRU_W), F32),
                   S((LRU_HEADS, HEAD, HEAD), F32), S((1, LRU_W), F32),
                   S((LRU_HEADS, HEAD, HEAD), F32), S((1, LRU_W), F32), S((1, LRU_W), F32)],
        compiler_params=_cp("parallel"), name="lru_bwd")(
            proj, proj, p["conv_w"], p["conv_b"], p["w_a"], p["b_a"], p["w_x"], p["b_x"], p["lam"], dycat)


def _rope(x, c, s):
    x1 = x[:, :ROPE // 2]
    x2 = x[:, ROPE // 2:]
    return jnp.concatenate([x1 * c - x2 * s, x1 * s + x2 * c], axis=-1)


def _rope_t(d, c, s):
    d1 = d[:, :ROPE // 2]
    d2 = d[:, ROPE // 2:]
    return jnp.concatenate([d1 * c + d2 * s, d2 * c - d1 * s], axis=-1)


def _rope_tables(pos2, inv_freq):
    t = pos2.shape[0]

    def body(p_ref, f_ref, c_ref, s_ref):
        ang = p_ref[...].astype(F32) * f_ref[...]
        c_ref[...] = jnp.cos(ang)
        s_ref[...] = jnp.sin(ang)

    return pl.pallas_call(body, out_shape=[S((t, ROPE // 2), F32), S((t, ROPE // 2), F32)],
                          name="rope_tables")(pos2, inv_freq)


def _down_norm(xb, wdown_g, gq3, gkv3, cos, sin, j):
    t = xb.shape[0]
    bm = _row_tile(t)

    def body(x_ref, w_ref, gq_ref, gkv_ref, c_ref, s_ref, down_ref, cq_ref, ckv_ref, kpe_ref):
        w = w_ref[...].reshape(D, ODD_IN)
        down = _dot(x_ref[...], w)
        down_ref[...] = down
        q = down[:, :Q_RANK]
        cq_ref[...] = (q * lax.rsqrt(jnp.mean(q * q, axis=-1, keepdims=True) + RMS_EPS) * gq_ref[...]).astype(BF16)
        kv = down[:, Q_RANK:Q_RANK + KV_RANK]
        ckv_ref[...] = (kv * lax.rsqrt(jnp.mean(kv * kv, axis=-1, keepdims=True) + RMS_EPS)
                        * gkv_ref[...]).astype(BF16)
        kpe_ref[...] = _rope(down[:, Q_RANK + KV_RANK:], c_ref[...], s_ref[...])

    row = lambda n: pl.BlockSpec((bm, n), lambda i: (i, 0))
    return pl.pallas_call(
        body, grid=(t // bm,),
        in_specs=[row(D), _full((N_DEV, D // N_DEV, ODD_IN)),
                  pl.BlockSpec((None, 1, Q_RANK), lambda i: (j, 0, 0)),
                  pl.BlockSpec((None, 1, KV_RANK), lambda i: (j, 0, 0)), row(ROPE // 2), row(ROPE // 2)],
        out_specs=[row(ODD_IN), row(Q_RANK), row(KV_RANK), row(ROPE)],
        out_shape=[S((t, ODD_IN), F32), S((t, Q_RANK), BF16), S((t, KV_RANK), BF16), S((t, ROPE), F32)],
        compiler_params=_cp("parallel"), name="down_norm")(xb, wdown_g, gq3, gkv3, cos, sin)


def _q_tile(t, widest):
    return min(widest, t // 2)


def _attn_probs(q, k, qs):
    s = _dot_nt(q, k) * ATT_SCALE
    tq = q.shape[0]
    rows = lax.broadcasted_iota(jnp.int32, (tq, tq), 0)
    cols = lax.broadcasted_iota(jnp.int32, (tq, tq), 1)
    last = jnp.where(jnp.right_shift(cols, CHUNK_SHIFT) <= jnp.right_shift(rows, CHUNK_SHIFT), s[:, qs:], NEG)
    s = last if qs == 0 else jnp.concatenate([s[:, :qs], last], axis=1)
    e = jnp.exp(s - jnp.max(s, axis=-1, keepdims=True))
    return e / jnp.sum(e, axis=-1, keepdims=True)


def _head_qkv(cq, ckv, kpe, c, s, wq_ref, wkv_ref):
    q = jnp.concatenate([_dot(cq, wq_ref[:, :NOPE]), _rope(_dot(cq, wq_ref[:, NOPE:]), c, s)], axis=1).astype(BF16)
    k = jnp.concatenate([_dot(ckv, wkv_ref[:, :NOPE]), kpe], axis=1).astype(BF16)
    vv = _dot(ckv, wkv_ref[:, NOPE:]).astype(BF16)
    return q, k, vv


def _attn_in_specs(t):
    return [_full((t, Q_RANK)), _full((t, KV_RANK)), _full((t, ROPE)), _full((t, ROPE // 2)), _full((t, ROPE // 2)),
            pl.BlockSpec((None, Q_RANK, NOPE + ROPE), lambda h: (h, 0, 0)),
            pl.BlockSpec((None, KV_RANK, NOPE + VDIM), lambda h: (h, 0, 0)),
            pl.BlockSpec((None, VDIM, D), lambda h: (h, 0, 0))]


def _attn_fwd(cq, ckv, kpe, cos, sin, wqb_g, wkvb_g, wo_g):
    t = cq.shape[0]
    tq = _q_tile(t, 256)

    def body(cq_ref, ckv_ref, kpe_ref, c_ref, s_ref, wq_ref, wkv_ref, wo_ref, o_ref, mix_ref):
        q, k, vv = _head_qkv(cq_ref[...], ckv_ref[...], kpe_ref[...], c_ref[...], s_ref[...], wq_ref, wkv_ref)
        for qs in range(0, t, tq):
            ke = qs + tq
            p = _attn_probs(q[qs:ke], k[:ke], qs)
            o_ref[qs:ke, :] = _dot(p.astype(BF16), vv[:ke]).astype(BF16)
        c = _dot(o_ref[...], wo_ref[...])

        @pl.when(pl.program_id(0) == 0)
        def _():
            mix_ref[...] = c

        @pl.when(pl.program_id(0) > 0)
        def _():
            mix_ref[...] += c

    return pl.pallas_call(
        body, grid=(MLA_HEADS,), in_specs=_attn_in_specs(t),
        out_specs=[pl.BlockSpec((None, t, VDIM), lambda h: (h, 0, 0)), _full((t, D))],
        out_shape=[S((MLA_HEADS, t, VDIM), BF16), S((t, D), F32)],
        compiler_params=_cp("arbitrary"), name="attn_fwd")(cq, ckv, kpe, cos, sin, wqb_g, wkvb_g, wo_g)


def _attn_bwd(cq, ckv, kpe, cos, sin, wqb_g, wkvb_g, wo_g, o, dzb):
    t = cq.shape[0]
    tq = _q_tile(t, 512)

    def body(cq_ref, ckv_ref, kpe_ref, c_ref, s_ref, wq_ref, wkv_ref, wo_ref, o_ref, dz_ref,
             dwo_ref, dwq_ref, dwkv_ref, dcq_ref, dckv_ref, dkpe_ref, dkt_s, dvt_s, dq_s):
        cqv = cq_ref[...]
        ckvv = ckv_ref[...]
        c = c_ref[...]
        s = s_ref[...]
        q, k, vv = _head_qkv(cqv, ckvv, kpe_ref[...], c, s, wq_ref, wkv_ref)
        dzv = dz_ref[...]
        dwo_ref[...] = _dot_tn(o_ref[...], dzv).astype(BF16)
        do = _dot_nt(dzv, wo_ref[...]).astype(BF16)
        dkt_s[...] = jnp.zeros_like(dkt_s)
        dvt_s[...] = jnp.zeros_like(dvt_s)
        for qs in range(0, t, tq):
            ke = qs + tq
            p = _attn_probs(q[qs:ke], k[:ke], qs)
            dp = _dot_nt(do[qs:ke], vv[:ke])
            ds = (p * (dp - jnp.sum(p * dp, axis=-1, keepdims=True)) * ATT_SCALE).astype(BF16)
            dq_s[qs:ke, :] = _dot(ds, k[:ke])
            dkt_s[0:NOPE + ROPE, 0:ke] += _dot_tn(q[qs:ke], ds)
            dvt_s[:, 0:ke] += _dot_tn(do[qs:ke], p.astype(BF16))
        dk = dkt_s[...].T
        dqn = dq_s[:, :NOPE].astype(BF16)
        dqp = _rope_t(dq_s[:, NOPE:], c, s).astype(BF16)
        dkn = dk[:, :NOPE].astype(BF16)
        dkp = dk[:, NOPE:NOPE + ROPE]
        dvv = dvt_s[...].T.astype(BF16)
        dwq_ref[:, :NOPE] = _dot_tn(cqv, dqn).astype(BF16)
        dwq_ref[:, NOPE:] = _dot_tn(cqv, dqp).astype(BF16)
        dwkv_ref[:, :NOPE] = _dot_tn(ckvv, dkn).astype(BF16)
        dwkv_ref[:, NOPE:] = _dot_tn(ckvv, dvv).astype(BF16)
        dcq = _dot_nt(dqn, wq_ref[:, :NOPE]) + _dot_nt(dqp, wq_ref[:, NOPE:])
        dckv = _dot_nt(dkn, wkv_ref[:, :NOPE]) + _dot_nt(dvv, wkv_ref[:, NOPE:])

        @pl.when(pl.program_id(0) == 0)
        def _():
            dcq_ref[...] = dcq
            dckv_ref[...] = dckv
            dkpe_ref[...] = dkp

        @pl.when(pl.program_id(0) > 0)
        def _():
            dcq_ref[...] += dcq
            dckv_ref[...] += dckv
            dkpe_ref[...] += dkp

    per_head = lambda a, b: pl.BlockSpec((None, a, b), lambda h: (h, 0, 0))
    return pl.pallas_call(
        body, grid=(MLA_HEADS,),
        in_specs=_attn_in_specs(t) + [per_head(t, VDIM), _full((t, D))],
        out_specs=[per_head(VDIM, D), per_head(Q_RANK, NOPE + ROPE), per_head(KV_RANK, NOPE + VDIM),
                   _full((t, Q_RANK)), _full((t, KV_RANK)), _full((t, ROPE))],
        out_shape=[S((MLA_HEADS, VDIM, D), BF16), S((MLA_HEADS, Q_RANK, NOPE + ROPE), BF16),
                   S((MLA_HEADS, KV_RANK, NOPE + VDIM), BF16),
                   S((t, Q_RANK), F32), S((t, KV_RANK), F32), S((t, ROPE), F32)],
        scratch_shapes=[pltpu.VMEM((2 * NOPE, t), F32), pltpu.VMEM((VDIM, t), F32),
                        pltpu.VMEM((t, NOPE + ROPE), F32)],
        compiler_params=_cp("arbitrary"), name="attn_bwd")(cq, ckv, kpe, cos, sin, wqb_g, wkvb_g, wo_g, o, dzb)


def _rms_bwd(down, dcq, dckv, dkpe, cos, sin, gq3, gkv3, j):
    t = down.shape[0]
    bm = _row_tile(t)

    def body(down_ref, dcq_ref, dckv_ref, dkpe_ref, c_ref, s_ref, gq_ref, gkv_ref, dd_ref, dgq_ref, dgkv_ref):
        @pl.when(pl.program_id(0) == 0)
        def _():
            dgq_ref[...] = jnp.zeros_like(dgq_ref)
            dgkv_ref[...] = jnp.zeros_like(dgkv_ref)

        def rms_b(x, dy, g):
            rstd = lax.rsqrt(jnp.mean(x * x, axis=-1, keepdims=True) + RMS_EPS)
            xh = x * rstd
            dyg = dy * g
            return rstd * (dyg - xh * jnp.mean(dyg * xh, axis=-1, keepdims=True)), jnp.sum(dy * xh, axis=0, keepdims=True)

        dq, dgq = rms_b(down_ref[:, :Q_RANK], dcq_ref[...], gq_ref[...])
        dkv, dgkv = rms_b(down_ref[:, Q_RANK:Q_RANK + KV_RANK], dckv_ref[...], gkv_ref[...])
        dgq_ref[...] += dgq
        dgkv_ref[...] += dgkv
        dd_ref[:, :Q_RANK] = dq.astype(BF16)
        dd_ref[:, Q_RANK:Q_RANK + KV_RANK] = dkv.astype(BF16)
        dd_ref[:, Q_RANK + KV_RANK:] = _rope_t(dkpe_ref[...], c_ref[...], s_ref[...]).astype(BF16)

    row = lambda n: pl.BlockSpec((bm, n), lambda i: (i, 0))
    return pl.pallas_call(
        body, grid=(t // bm,),
        in_specs=[row(ODD_IN), row(Q_RANK), row(KV_RANK), row(ROPE), row(ROPE // 2), row(ROPE // 2),
                  pl.BlockSpec((None, 1, Q_RANK), lambda i: (j, 0, 0)),
                  pl.BlockSpec((None, 1, KV_RANK), lambda i: (j, 0, 0))],
        out_specs=[row(ODD_IN), _full((1, Q_RANK)), _full((1, KV_RANK))],
        out_shape=[S((t, ODD_IN), BF16), S((1, Q_RANK), F32), S((1, KV_RANK), F32)],
        compiler_params=_cp("arbitrary"), name="rms_bwd")(down, dcq, dckv, dkpe, cos, sin, gq3, gkv3)


def _col_blocks(t, n, bn):
    return pl.BlockSpec((t, bn), lambda i: (0, i))


def _row_blocks(n, bm):
    return pl.BlockSpec((bm, n), lambda i: (i, 0))


def _local_step(x, pos2, tgt, small, weights_of, grads_done, start_dep=None, prefetch=None):
    t = x.shape[0]
    bm = _row_tile(t)
    inv_freq = (ROPE_THETA ** (-jnp.arange(0, ROPE, 2, dtype=F32) / ROPE)).reshape(1, ROPE // 2)
    cos, sin = _rope_tables(pos2, inv_freq)
    lru_p = {k: small[k] for k in ("conv_w", "conv_b", "w_a", "b_a", "w_x", "b_x", "lam")}

    saved = []
    y, yb = x, x.astype(BF16)
    for l in range(DEPTH):
        j = l // 2
        big = weights_of(l, 0, y)
        sv = dict(xb=yb, big=big)
        if l % 2 == 0:
            proj = _mm(yb, big["win_t"], mode="nt", grid=(EVEN_IN // 512,), a_spec=_full((t, D)),
                       b_spec=_row_blocks(D, 512), out_shape=S((t, EVEN_IN), F32),
                       out_spec=_col_blocks(t, EVEN_IN, 512), name="even_proj", dep=start_dep if l == 0 else None)
            ycat = jnp.concatenate([_pool_fwd(proj, small["pool_w"], small["pool_scale"], j),
                                    _lru_fwd(proj, lru_p, j)], axis=1)
            big.update(weights_of(l, 1, ycat))
            z1, y1, y1b = _proj_resid_ln(y, ycat, big["wout2d"], small["ln_mix_g"], small["ln_mix_b"], l, "even_out")
            sv.update(proj=proj, ycat=ycat)
        else:
            down, cq, ckv, kpe = _down_norm(yb, big["wdown"], small["gq"], small["gkv"], cos, sin, j)
            o, mix = _attn_fwd(cq, ckv, kpe, cos, sin, big["wqb"], big["wkvb"], big["wo"])
            z1, y1, y1b = _resid_ln(y, mix, small["ln_mix_g"], small["ln_mix_b"], l, "resid_ln")
            sv.update(down=down, cq=cq, ckv=ckv, kpe=kpe, o=o)
        fetched = prefetch(l + 1, y1) if prefetch is not None and l + 1 < DEPTH else None
        z2, y, yb, act = _mlp_fwd(y1, y1b, big["w1"], big["w2"], small["ln_ffn_g"], small["ln_ffn_b"], l,
                                  dep=fetched)
        sv.update(z1=z1, y1b=y1b, z2=z2, act=act)
        saved.append(sv)

    dy, loss_tile = _loss_grad(y, tgt)

    g = {k: [None] * n for k, n in (("ln_mix_g", 4), ("ln_mix_b", 4), ("ln_ffn_g", 4), ("ln_ffn_b", 4),
                                    ("pool_w", 2), ("pool_scale", 2), ("conv_w", 2), ("conv_b", 2),
                                    ("w_a", 2), ("b_a", 2), ("w_x", 2), ("b_x", 2), ("lam", 2),
                                    ("gq", 2), ("gkv", 2))}
    dep = None
    for l in reversed(range(DEPTH)):
        j = l // 2
        sv = saved[l]
        big = sv["big"]
        dz2, dz2b, g["ln_ffn_g"][l], g["ln_ffn_b"][l] = _ln_bwd(dy, sv["z2"], small["ln_ffn_g"], l, "ln_bwd", dep=dep)
        act = sv["act"]
        dh, dff = _mlp_bwd_dh(act, dz2b, big["w1"], big["w2"])
        dw1 = _mm(sv["y1b"], dh, mode="tn", grid=(N_DEV,), a_spec=_full((t, D)),
                  b_spec=_col_blocks(t, D_FF, FF_BLK), out_shape=S((N_DEV, D, FF_BLK), BF16),
                  out_spec=pl.BlockSpec((None, D, FF_BLK), lambda i: (i, 0, 0)), name="mlp_dw1")
        dw2 = _mm(act, dz2b, mode="tn", grid=(N_DEV,), a_spec=_col_blocks(t, D_FF, FF_BLK),
                  b_spec=_full((t, D)), out_shape=S((N_DEV, FF_BLK, D), BF16),
                  out_spec=pl.BlockSpec((None, FF_BLK, D), lambda i: (i, 0, 0)), name="mlp_dw2")
        dep = grads_done(l, dict(w1=dw1, w2=dw2))
        dz1, dz1b, g["ln_mix_g"][l], g["ln_mix_b"][l] = _ln_bwd(dff, sv["z1"], small["ln_mix_g"], l, "ln_bwd_res",
                                                                 r=dz2, dep=dep)
        if l % 2 == 0:
            wout = big["wout2d"]
            dycat = _mm(dz1b, wout, mode="nt", grid=(EVEN_MIX // 512,), a_spec=_full((t, D)),
                        b_spec=_row_blocks(D, 512), out_shape=S((t, EVEN_MIX), F32),
                        out_spec=_col_blocks(t, EVEN_MIX, 512), name="even_dycat")
            dwout = _mm(sv["ycat"], dz1b, mode="tn", grid=(EVEN_MIX // 512,), a_spec=_col_blocks(t, EVEN_MIX, 512),
                        b_spec=_full((t, D)), out_shape=S((EVEN_MIX, D), BF16), out_spec=_row_blocks(D, 512),
                        name="even_dwout")
            du_pool, g["pool_w"][j], g["pool_scale"][j] = _pool_bwd(sv["proj"], dycat, small["pool_w"],
                                                                   small["pool_scale"], j)
            (du_lru, du_gate, g["conv_w"][j], g["conv_b"][j], g["w_a"][j], g["b_a"][j], g["w_x"][j], g["b_x"][j],
             g["lam"][j]) = _lru_bwd(sv["proj"], dycat, lru_p, j)
            dproj = jnp.concatenate([du_pool, du_lru, du_gate], axis=1)
            dwin = _mm(sv["xb"], dproj, mode="tn", grid=(EVEN_IN // 512,), a_spec=_full((t, D)),
                       b_spec=_col_blocks(t, EVEN_IN, 512), out_shape=S((D, EVEN_IN), BF16),
                       out_spec=_col_blocks(D, EVEN_IN, 512), name="even_dwin")
            dep = grads_done(l, dict(win=dwin.reshape(D, N_DEV, EVEN_IN // N_DEV).transpose(1, 0, 2),
                                     wout=dwout.reshape(N_DEV, EVEN_MIX // N_DEV, D)))
            dy = _mm(dproj, big["win_t"], mode="nn", grid=(t // bm,), a_spec=_row_blocks(EVEN_IN, bm),
                     b_spec=_full((EVEN_IN, D)), out_shape=S((t, D), F32), out_spec=_row_blocks(D, bm),
                     add=dz1, add_spec=_row_blocks(D, bm), add_scale=ALPHA, name="even_dx", dep=dep)
        else:
            dwo, dwqb, dwkvb, dcq, dckv, dkpe = _attn_bwd(
                sv["cq"], sv["ckv"], sv["kpe"], cos, sin, big["wqb"], big["wkvb"], big["wo"], sv["o"], dz1b)
            ddown, g["gq"][j], g["gkv"][j] = _rms_bwd(sv["down"], dcq, dckv, dkpe, cos, sin, small["gq"],
                                                     small["gkv"], j)
            dwdown = _mm(sv["xb"], ddown, mode="tn", grid=(N_DEV,), a_spec=_col_blocks(t, D, D // N_DEV),
                         b_spec=_full((t, ODD_IN)), out_shape=S((N_DEV, D // N_DEV, ODD_IN), BF16),
                         out_spec=pl.BlockSpec((None, D // N_DEV, ODD_IN), lambda i: (i, 0, 0)),
                         name="odd_dwdown")
            dep = grads_done(l, dict(wdown=dwdown, wqb=dwqb, wkvb=dwkvb, wo=dwo))
            dy = _mm(ddown, big["wdown2d"], mode="nt", grid=(t // bm,), a_spec=_row_blocks(ODD_IN, bm),
                     b_spec=_full((D, ODD_IN)), out_shape=S((t, D), F32), out_spec=_row_blocks(D, bm),
                     add=dz1, add_spec=_row_blocks(D, bm), add_scale=ALPHA, name="odd_dx", dep=dep)
    return loss_tile[0, 0], dy, g


def _mesh_place():
    x, y, c = lax.axis_index("x"), lax.axis_index("y"), lax.axis_index("c")
    return x, y, c


def _peer(place, k):
    x, y, c = place
    return (1 - x if k & 4 else x, 1 - y if k & 2 else y, 1 - c if k & 1 else c)


def _index(place):
    x, y, c = place
    return 4 * x + 2 * y + c


ANY = pl.BlockSpec(memory_space=pl.ANY)


def _make_zones(shards, me, name):
    n = len(shards)

    def body(me_ref, *refs):
        for src, dst in zip(refs[:n], refs[n:]):
            dst[...] = src[...].astype(BF16)

    grid_spec = pltpu.PrefetchScalarGridSpec(
        num_scalar_prefetch=1, grid=(1,),
        in_specs=[pl.BlockSpec(s.shape, lambda i, me_ref: (0, 0)) for s in shards],
        out_specs=[pl.BlockSpec((None,) + s.shape, lambda i, me_ref: (me_ref[0], 0, 0)) for s in shards])
    return pl.pallas_call(body, grid_spec=grid_spec, out_shape=[S((N_DEV,) + s.shape, BF16) for s in shards],
                          compiler_params=_cp("arbitrary"), name=name)(me, *shards)


def _all_gather_big(zones):
    n = len(zones)

    def body(*refs):
        outs = refs[n:2 * n]
        send, recv = refs[2 * n:]
        x, y, c = _mesh_place()
        me, sibling = (x, y, c), (x, y, 1 - c)
        chips = [(1 - x, y), (x, 1 - y), (1 - x, 1 - y)]

        def copy(w, k, block, to):
            blk = outs[w].at[_index(block)]
            return pltpu.make_async_remote_copy(src_ref=blk, dst_ref=blk, send_sem=send.at[w, k], recv_sem=recv.at[w, k],
                                                device_id=to, device_id_type=MESH)

        first = []
        for w in range(n):
            first.append(copy(w, 0, me, sibling))
            first += [copy(w, 1 + j, me, (*chip, c)) for j, chip in enumerate(chips)]
        for cp in first:
            cp.start()
        passed = []
        for w in range(n):
            for j, chip in enumerate(chips):
                copy(w, 1 + j, (*chip, c), me).wait_recv()
                cp = copy(w, 4 + j, (*chip, c), sibling)
                cp.start()
                passed.append(cp)
        for w in range(n):
            copy(w, 0, sibling, me).wait_recv()
            for j, chip in enumerate(chips):
                copy(w, 4 + j, (*chip, 1 - c), me).wait_recv()
        for cp in first + passed:
            cp.wait_send()

    return pl.pallas_call(
        body, in_specs=[ANY] * n, out_specs=[ANY] * n, out_shape=[S(z.shape, z.dtype) for z in zones],
        input_output_aliases={i: i for i in range(n)},
        scratch_shapes=[pltpu.SemaphoreType.DMA((n, N_DEV - 1)), pltpu.SemaphoreType.DMA((n, N_DEV - 1))],
        compiler_params=pltpu.CompilerParams(has_side_effects=True), name="all_gather_big")(*zones)


def _shard_rows_tile(a):
    return max(d for d in range(16, 257, 16) if a % d == 0)


HBM = pl.BlockSpec(memory_space=pltpu.HBM)
SEM = pl.BlockSpec(memory_space=pltpu.SEMAPHORE)
DATAFLOW = pltpu.SideEffectType.DATAFLOW_SIDE_EFFECTING


def _in_hbm(a):
    return pltpu.with_memory_space_constraint(a, pltpu.HBM)


def _gather_ici_copies(place, src, land, w):
    me = _index(place)
    return [(_peer(place, k), land.at[me], land.at[me]) for k in (1, 2, 4, 6)]


def _gather_d2d_copies(place, src, land, w):
    blocks = [_index(_peer(place, k)) for k in (2, 4, 6)]
    return [(_peer(place, 1), land.at[b], land.at[b]) for b in blocks]


GATHER_ICI = (4, _gather_ici_copies)
GATHER_D2D = (3, _gather_d2d_copies)


def _scatter_plan(layers):
    def copies(place, src, land, w):
        me = _index(place)
        mine = land.at[me] if layers[w] is None else land.at[me, layers[w]]
        return [(_peer(place, k), src.at[_index(_peer(place, k))], mine) for k in range(1, N_DEV)]
    return (N_DEV - 1, copies)


def _gather_all_copies(place, src, land, w):
    me = _index(place)
    return [(_peer(place, k), land.at[me], land.at[me]) for k in range(1, N_DEV)]


GATHER_ALL = (N_DEV - 1, _gather_all_copies)


def _sum_blocks(zone, part, me):
    r = part.shape[1]

    def body(me_ref, z_ref, p_ref, o_ref):
        acc = None
        for s in range(N_DEV):
            term = jnp.where(me_ref[0] == s, p_ref[...], z_ref[s])
            acc = term if acc is None else acc + term
        o_ref[...] = acc

    grid_spec = pltpu.PrefetchScalarGridSpec(
        num_scalar_prefetch=1, grid=(1,),
        in_specs=[pl.BlockSpec((N_DEV, r, 128), lambda i, me_ref: (0, 0, 0)),
                  pl.BlockSpec((None, r, 128), lambda i, me_ref: (me_ref[0], 0, 0))],
        out_specs=pl.BlockSpec((r, 128), lambda i, me_ref: (0, 0)))
    return pl.pallas_call(body, grid_spec=grid_spec, out_shape=S((r, 128), F32),
                          compiler_params=_cp("arbitrary"), name="sum_small")(me, zone, part)


def _exchange_start(srcs, lands, plan, name, after=()):
    ns, n = len(srcs), len(lands)
    n_in = ns + n + len(after)
    per, copies = plan

    def body(*refs):
        ins, land = refs[:ns], refs[ns:ns + n]
        send, recv = refs[n_in], refs[n_in + 1]
        token = refs[-1]
        place = _mesh_place()
        for i in range(per):
            for w in range(n):
                target, src, dst = copies(place, ins[w] if ns else None, land[w], w)[i]
                pltpu.make_async_remote_copy(src_ref=src, dst_ref=dst, send_sem=send.at[w * per + i],
                                             recv_sem=recv.at[w * per + i], device_id=target, device_id_type=MESH).start()
        token[...] = jnp.zeros_like(token)

    sems = pltpu.SemaphoreType.DMA((n * per,))
    thru = [pltpu.HBM(a.shape, a.dtype) for a in list(srcs) + list(lands)]
    out = pl.pallas_call(
        body, name=name, in_specs=[HBM] * (ns + n) + [ANY] * len(after),
        out_shape=(sems, sems, *thru, S((8, 128), F32)),
        out_specs=(SEM, SEM, *([HBM] * (ns + n)), pl.BlockSpec(memory_space=pltpu.VMEM)),
        input_output_aliases={i: 2 + i for i in range(ns + n)},
        compiler_params=pltpu.CompilerParams(has_side_effects=DATAFLOW),
    )(*[_in_hbm(a) for a in list(srcs) + list(lands)], *after)
    return out[0], out[1], list(out[2:2 + ns]), list(out[2 + ns:2 + ns + n]), out[-1]


def _exchange_wait(send, recv, srcs, lands, plan, after, name):
    ns, n = len(srcs), len(lands)
    per, copies = plan
    afters = tuple(after) if isinstance(after, (tuple, list)) else (after,)

    def body(*refs):
        ins, land = refs[:ns], refs[ns:ns + n]
        send_ref, recv_ref = refs[ns + n], refs[ns + n + 1]
        place = _mesh_place()
        for i in range(per):
            for w in range(n):
                target, src, dst = copies(place, ins[w] if ns else None, land[w], w)[i]
                cp = pltpu.make_async_remote_copy(src_ref=src, dst_ref=dst, send_sem=send_ref.at[w * per + i],
                                                  recv_sem=recv_ref.at[w * per + i], device_id=target,
                                                  device_id_type=MESH)
                cp.wait_send()
                cp.wait_recv()

    thru = [pltpu.HBM(a.shape, a.dtype) for a in list(srcs) + list(lands)]
    out = pl.pallas_call(
        body, name=name, in_specs=[HBM] * (ns + n) + [SEM, SEM] + [ANY] * len(afters),
        out_shape=tuple(thru), out_specs=tuple([HBM] * (ns + n)),
        input_output_aliases={i: i for i in range(ns + n)},
        compiler_params=pltpu.CompilerParams(has_side_effects=DATAFLOW),
    )(*srcs, *lands, send, recv, *afters)
    return list(out[:ns]), list(out[ns:])


def _all_reduce_small(part, name, deps=()):
    def body(*refs):
        p_ref = refs[0]
        o_ref, rbuf, send1, recv1, send2, recv2 = refs[-6:]
        place = _mesh_place()
        me = _index(place)
        rbuf[pl.ds(me, 1)] = p_ref[pl.ds(me, 1)]
        first = [pltpu.make_async_remote_copy(src_ref=p_ref.at[_index(_peer(place, k))], dst_ref=rbuf.at[me],
                                              send_sem=send1.at[k - 1], recv_sem=recv1.at[k - 1],
                                              device_id=_peer(place, k), device_id_type=MESH)
                 for k in range(1, N_DEV)]
        for cp in first:
            cp.start()
        for cp in first:
            cp.wait()
        acc = rbuf[0]
        for d in range(1, N_DEV):
            acc = acc + rbuf[d]
        o_ref[pl.ds(me, 1)] = acc[None]
        second = [pltpu.make_async_remote_copy(src_ref=o_ref.at[me], dst_ref=o_ref.at[me], send_sem=send2.at[k - 1],
                                               recv_sem=recv2.at[k - 1], device_id=_peer(place, k),
                                               device_id_type=MESH)
                  for k in range(1, N_DEV)]
        for cp in second:
            cp.start()
        for cp in second:
            cp.wait()

    vm = pl.BlockSpec(memory_space=pltpu.VMEM)
    ops = [part, *deps]
    return pl.pallas_call(
        body, in_specs=[vm] + [ANY] * len(deps), out_specs=vm, out_shape=S(part.shape, F32),
        scratch_shapes=[pltpu.VMEM(part.shape, F32)] + [pltpu.SemaphoreType.DMA((N_DEV - 1,))] * 4,
        compiler_params=pltpu.CompilerParams(has_side_effects=True, vmem_limit_bytes=VMEM_LIMIT), name=name)(*ops)


def _adamw(w, g, m, v):
    m = ADAM_B1 * m + (1.0 - ADAM_B1) * g
    v = ADAM_B2 * v + (1.0 - ADAM_B2) * (g * g)
    m_hat = m / (1.0 - ADAM_B1 ** ADAM_STEP)
    v_hat = v / (1.0 - ADAM_B2 ** ADAM_STEP)
    return -ADAM_LR * (m_hat / (jnp.sqrt(v_hat) + ADAM_EPS) + ADAM_WD * w), m, v


def _adam_big(parts, own, me, w, m, v, name):
    nl, a, b = w.shape
    ta = _shard_rows_tile(a)

    def body(me_ref, p_ref, *refs):
        own_refs, (w_ref, m_ref, v_ref, g_ref, d_ref, mo_ref, vo_ref) = refs[:nl], refs[nl:]
        layer = pl.program_id(0)
        mine = own_refs[0][...]
        for k in range(1, nl):
            mine = jnp.where(layer == k, own_refs[k][...], mine)
        g = None
        for s in range(N_DEV):
            term = jnp.where(me_ref[0] == s, mine, p_ref[s]).astype(F32)
            g = term if g is None else g + term
        g_ref[...] = g
        d_ref[...], mo_ref[...], vo_ref[...] = _adamw(w_ref[...], g, m_ref[...], v_ref[...])

    blk = pl.BlockSpec((None, ta, b), lambda l, i, me_ref: (l, i, 0))

    def own_spec(k):
        return pl.BlockSpec((None, ta, b), lambda l, i, me_ref: (me_ref[0], jnp.where(l == k, i, 0), 0))

    grid_spec = pltpu.PrefetchScalarGridSpec(
        num_scalar_prefetch=1, grid=(nl, a // ta),
        in_specs=[pl.BlockSpec((N_DEV, None, ta, b), lambda l, i, me_ref: (0, l, i, 0))]
        + [own_spec(k) for k in range(nl)] + [blk, blk, blk],
        out_specs=[blk] * 4)
    return pl.pallas_call(body, grid_spec=grid_spec, out_shape=[S(w.shape, F32)] * 4,
                          compiler_params=_cp("arbitrary", "arbitrary"), name=name)(me, parts, *own, w, m, v)


def _adam_small(gs, ws, ms, vs):
    n = len(gs)

    def body(*refs):
        ins, outs = refs[:4 * n], refs[4 * n:]
        for i in range(n):
            g_ref, w_ref, m_ref, v_ref = (ins[k * n + i] for k in range(4))
            outs[i][...], outs[n + i][...], outs[2 * n + i][...] = _adamw(w_ref[...], g_ref[...], m_ref[...], v_ref[...])

    out = pl.pallas_call(body, out_shape=[S(g.shape, F32) for g in gs] * 3, compiler_params=_cp(),
                         name="adam_small")(*gs, *ws, *ms, *vs)
    return out[:n], out[n:2 * n], out[2 * n:]


BIG = ("even_w_in", "even_w_out", "mla_w_down", "mla_w_qb", "mla_w_kvb", "mla_w_o", "mlp_w1", "mlp_w2")
BIG_KEY = dict(even_w_in="win", even_w_out="wout", mla_w_down="wdown", mla_w_qb="wqb", mla_w_kvb="wkvb",
               mla_w_o="wo", mlp_w1="w1", mlp_w2="w2")
SMALL = (("ln_mix_g", "ln_mix_g", None), ("ln_mix_b", "ln_mix_b", None), ("ln_ffn_g", "ln_ffn_g", None),
         ("ln_ffn_b", "ln_ffn_b", None), ("pool_w", "pool_w", None), ("pool_scale", "pool_scale", None),
         ("lru_conv_w", "conv_w", 2), ("lru_conv_b", "conv_b", None), ("lru_w_a", "w_a", None),
         ("lru_b_a", "b_a", None), ("lru_w_x", "w_x", None), ("lru_b_x", "b_x", None), ("lru_lambda", "lam", None),
         ("mla_q_norm_g", "gq", 1), ("mla_kv_norm_g", "gkv", 1))
WEIGHTS = ("ln_mix_g", "ln_mix_b", "ln_ffn_g", "ln_ffn_b", "even_w_in", "pool_w", "pool_scale", "lru_conv_w",
           "lru_conv_b", "lru_w_a", "lru_b_a", "lru_w_x", "lru_b_x", "lru_lambda", "even_w_out", "mla_w_down",
           "mla_q_norm_g", "mla_kv_norm_g", "mla_w_qb", "mla_w_kvb", "mla_w_o", "mlp_w1", "mlp_w2")
ALL_AXES = ("x", "y", "c")


def _layer_weights(l):
    j = l // 2
    if l % 2 == 0:
        mixer = [("win", "even_w_in", j), ("wout", "even_w_out", j)]
    else:
        mixer = [("wdown", "mla_w_down", j), ("wqb", "mla_w_qb", j), ("wkvb", "mla_w_kvb", j), ("wo", "mla_w_o", j)]
    return mixer + [("w1", "mlp_w1", l), ("w2", "mlp_w2", l)]


def _pack(arrays, multiple):
    flat = jnp.concatenate([a.reshape(-1) for a in arrays])
    pad = (-flat.shape[0]) % multiple
    return jnp.pad(flat, (0, pad))


def _unpack(flat, shapes):
    out, at = [], 0
    for shp in shapes:
        n = 1
        for s in shp:
            n *= s
        out.append(flat[at:at + n].reshape(shp))
        at += n
    return out


def _global_shape(local_shape, axis):
    if axis is None:
        return tuple(local_shape)
    return tuple(s * N_DEV if i == axis else s for i, s in enumerate(local_shape))


def _step(x, positions, tgt, w, m, v):
    t = x.shape[1]
    me = _index(_mesh_place())

    sharded = [(name, axis) for name, _, axis in SMALL if axis is not None]
    zeros_with_mine = [lax.dynamic_update_slice_in_dim(jnp.zeros(_global_shape(w[name].shape, axis), F32), w[name],
                                                       me * w[name].shape[axis], axis) for name, axis in sharded]
    chunk = N_DEV * 8 * 128
    gathered = _all_reduce_small(_pack(zeros_with_mine, chunk).reshape(N_DEV, -1, 128), "gather_small")
    full = dict(zip([name for name, _ in sharded],
                    _unpack(gathered.reshape(-1), [_global_shape(w[name].shape, axis) for name, axis in sharded])))

    def keys_of(l, part):
        keys = [key for key, _, _ in _layer_weights(l)]
        if l == 0:
            return keys[:1] if part == 0 else keys[1:]
        return keys if part == 0 else []

    shard_of = {(l, key): (w[name][i].T if key == "win" else w[name][i])
                for l in range(DEPTH) for key, name, i in _layer_weights(l)}
    me_arr = me.astype(jnp.int32).reshape(1)
    first = _all_gather_big(_make_zones([shard_of[0, key] for key in keys_of(0, 0)], me_arr, "zones_0_0"))
    flights, after = {}, (first[0], gathered)
    for l in range(DEPTH):
        for part in (0, 1):
            if (l, part) != (0, 0) and keys_of(l, part):
                zones = _make_zones([shard_of[l, key] for key in keys_of(l, part)], me_arr, "zones_%d_%d" % (l, part))
                send, recv, _, lands, token = _exchange_start([], zones, GATHER_ICI, "gather_start_%d_%d" % (l, part),
                                                              after=after)
                flights[l, part] = (send, recv, [], lands)
                after = (token,)

    passing = {}

    def pass_on(l, part, after):
        tag = "%d_%d" % (l, part)
        _, lands = _exchange_wait(*flights[l, part], GATHER_ICI, after, "gather_wait_" + tag)
        send, recv, _, lands, token = _exchange_start([], lands, GATHER_D2D, "gather_pass_" + tag)
        passing[l, part] = (send, recv, [], lands)
        return token

    def early_pass(l, after):
        return pass_on(l, 0, after) if l >= 2 else None

    def weights_of(l, part, after):
        keys = keys_of(l, part)
        if (l, part) == (0, 0):
            arrays = first
        elif keys:
            if (l, part) not in passing:
                pass_on(l, part, after)
            _, arrays = _exchange_wait(*passing[l, part], GATHER_D2D, after, "gather_pass_wait_%d_%d" % (l, part))
        big = dict(zip(keys, arrays)) if keys else {}
        if "win" in big:
            big["win_t"] = big["win"].reshape(EVEN_IN, D)
        if "wout" in big:
            big["wout2d"] = big["wout"].reshape(EVEN_MIX, D)
        if "wdown" in big:
            big["wdown2d"] = big["wdown"].reshape(D, ODD_IN)
        return big

    zone = {name: lax.empty((N_DEV,) + w[name].shape, BF16) for name in BIG}
    name_of = {key: name for name, key in BIG_KEY.items()}
    sent, last_token = [], [None]

    def grads_done(l, grads):
        keys = list(grads)
        index = {key: i for key, _, i in _layer_weights(l)}
        layers = [index[key] for key in keys]
        send, recv, srcs, lands, tok = _exchange_start([grads[k] for k in keys], [zone[name_of[k]] for k in keys],
                                                       _scatter_plan(layers), "scatter_start_%d_%s" % (l, keys[0]))
        for k, land in zip(keys, lands):
            zone[name_of[k]] = land
        sent.append((send, recv, srcs, keys, layers))
        last_token[0] = tok
        return tok

    row3 = lambda a: a.reshape(a.shape[0], 1, a.shape[1])
    small = dict(ln_mix_g=row3(w["ln_mix_g"]), ln_mix_b=row3(w["ln_mix_b"]), ln_ffn_g=row3(w["ln_ffn_g"]),
                 ln_ffn_b=row3(w["ln_ffn_b"]), pool_w=w["pool_w"], pool_scale=row3(w["pool_scale"]),
                 conv_w=full["lru_conv_w"], conv_b=row3(w["lru_conv_b"]), w_a=w["lru_w_a"], b_a=row3(w["lru_b_a"]),
                 w_x=w["lru_w_x"], b_x=row3(w["lru_b_x"]), lam=row3(w["lru_lambda"]),
                 gq=row3(full["mla_q_norm_g"]), gkv=row3(full["mla_kv_norm_g"]))

    loss_part, grad_x, g = _local_step(x[0], positions.reshape(t, 1), tgt[0], small, weights_of, grads_done,
                                       start_dep=token, prefetch=early_pass)

    own = {name: [None] * w[name].shape[0] for name in BIG}
    me_arr = me.astype(jnp.int32).reshape(1)
    out = {}
    local_g = [jnp.stack(g[key]).reshape(_global_shape(w[name].shape, axis)) for name, key, axis in SMALL]
    local_g.append(loss_part.reshape(1))
    part = _pack(local_g, chunk).reshape(N_DEV, -1, 128)
    small_plan = _scatter_plan([None])
    s_send, s_recv, s_src, s_land, after = _exchange_start([part], [lax.empty(part.shape, F32)], small_plan,
                                                           "small_scatter_start", after=(last_token[0],))
    for n_flight, (send, recv, srcs, keys, layers) in enumerate(sent):
        if n_flight == len(sent) - 1:
            for name in BIG:
                if BIG_KEY[name] not in keys:
                    out[name] = _adam_big(zone[name], own[name], me_arr, w[name], m[name], v[name], "adam_" + name)
            s_src, s_land = _exchange_wait(s_send, s_recv, s_src, s_land, small_plan,
                                           [grad_x] + [o[0] for o in out.values()], "small_scatter_wait")
            chunk_sum = _sum_blocks(s_land[0], s_src[0], me_arr)
            r_zone = lax.dynamic_update_slice_in_dim(lax.empty(part.shape, F32), chunk_sum[None], me, 0)
            r_send, r_recv, _, r_land, after = _exchange_start([], [r_zone], GATHER_ALL, "small_gather_start")
        srcs, lands = _exchange_wait(send, recv, srcs, [zone[name_of[k]] for k in keys], _scatter_plan(layers),
                                     after, "scatter_wait_%d" % n_flight)
        for k, land, src, layer in zip(keys, lands, srcs, layers):
            zone[name_of[k]] = land
            own[name_of[k]][layer] = src
        after = lands[0]
    for name in BIG:
        if name not in out:
            out[name] = _adam_big(zone[name], own[name], me_arr, w[name], m[name], v[name], "adam_" + name)

    _, reduced = _exchange_wait(r_send, r_recv, [], r_land, GATHER_ALL, [out[name][0] for name in BIG],
                                "small_gather_wait")
    reduced = _unpack(reduced[0].reshape(-1), [a.shape for a in local_g])
    loss = reduced[-1][0]
    mine = [a if axis is None else lax.dynamic_slice_in_dim(a, me * w[name].shape[axis], w[name].shape[axis], axis)
            for a, (name, _, axis) in zip(reduced, SMALL)]
    names = [name for name, _, _ in SMALL]
    as_2d = lambda a: a.reshape(-1, a.shape[-1])
    new = _adam_small([as_2d(a) for a in mine], *([as_2d(src[name]) for name in names] for src in (w, m, v)))
    for i, name in enumerate(names):
        out[name] = (mine[i],) + tuple(part[i].reshape(w[name].shape) for part in new)

    return (loss, grad_x[None]) + tuple(out[name][i] for i in range(4) for name in WEIGHTS)


def kernel(x, positions, ln_mix_g, ln_mix_b, ln_ffn_g, ln_ffn_b, even_w_in, pool_w, pool_scale, lru_conv_w, lru_conv_b, lru_w_a, lru_b_a, lru_w_x, lru_b_x, lru_lambda, even_w_out, mla_w_down, mla_q_norm_g, mla_kv_norm_g, mla_w_qb, mla_w_kvb, mla_w_o, mlp_w1, mlp_w2, loss_target, m_ln_mix_g, m_ln_mix_b, m_ln_ffn_g, m_ln_ffn_b, m_even_w_in, m_pool_w, m_pool_scale, m_lru_conv_w, m_lru_conv_b, m_lru_w_a, m_lru_b_a, m_lru_w_x, m_lru_b_x, m_lru_lambda, m_even_w_out, m_mla_w_down, m_mla_q_norm_g, m_mla_kv_norm_g, m_mla_w_qb, m_mla_w_kvb, m_mla_w_o, m_mlp_w1, m_mlp_w2, v_ln_mix_g, v_ln_mix_b, v_ln_ffn_g, v_ln_ffn_b, v_even_w_in, v_pool_w, v_pool_scale, v_lru_conv_w, v_lru_conv_b, v_lru_w_a, v_lru_b_a, v_lru_w_x, v_lru_b_x, v_lru_lambda, v_even_w_out, v_mla_w_down, v_mla_q_norm_g, v_mla_kv_norm_g, v_mla_w_qb, v_mla_w_kvb, v_mla_w_o, v_mlp_w1, v_mlp_w2):
    w = dict(zip(WEIGHTS, (ln_mix_g, ln_mix_b, ln_ffn_g, ln_ffn_b, even_w_in, pool_w, pool_scale, lru_conv_w,
                           lru_conv_b, lru_w_a, lru_b_a, lru_w_x, lru_b_x, lru_lambda, even_w_out, mla_w_down,
                           mla_q_norm_g, mla_kv_norm_g, mla_w_qb, mla_w_kvb, mla_w_o, mlp_w1, mlp_w2)))
    m = dict(zip(WEIGHTS, (m_ln_mix_g, m_ln_mix_b, m_ln_ffn_g, m_ln_ffn_b, m_even_w_in, m_pool_w, m_pool_scale,
                           m_lru_conv_w, m_lru_conv_b, m_lru_w_a, m_lru_b_a, m_lru_w_x, m_lru_b_x, m_lru_lambda,
                           m_even_w_out, m_mla_w_down, m_mla_q_norm_g, m_mla_kv_norm_g, m_mla_w_qb, m_mla_w_kvb,
                           m_mla_w_o, m_mlp_w1, m_mlp_w2)))
    v = dict(zip(WEIGHTS, (v_ln_mix_g, v_ln_mix_b, v_ln_ffn_g, v_ln_ffn_b, v_even_w_in, v_pool_w, v_pool_scale,
                           v_lru_conv_w, v_lru_conv_b, v_lru_w_a, v_lru_b_a, v_lru_w_x, v_lru_b_x, v_lru_lambda,
                           v_even_w_out, v_mla_w_down, v_mla_q_norm_g, v_mla_kv_norm_g, v_mla_w_qb, v_mla_w_kvb,
                           v_mla_w_o, v_mlp_w1, v_mlp_w2)))
    return _step(x, positions, loss_target, w, m, v)
```

```python
import functools

import jax
import jax.numpy as jnp
from jax import lax
from jax.experimental import pallas as pl
from jax.experimental.pallas import tpu as pltpu

F32 = jnp.float32
BF16 = jnp.bfloat16
S = jax.ShapeDtypeStruct

D = 1024
DEPTH = 4
N_DEV = 8
CHUNK_SHIFT = 6
POOL_WINDOWS = (2, 4, 8, 16)
POOL_W = 512
LRU_W = 1024
LRU_HEADS = 8
HEAD = 128
LRU_C = 8.0
EVEN_IN = 2560
EVEN_MIX = 1536
MLA_HEADS = 8
NOPE = 128
ROPE = 64
VDIM = 128
Q_RANK = 384
KV_RANK = 256
ODD_IN = 704
D_FF = 4096
FF_BLK = D_FF // N_DEV
ROPE_THETA = 10000.0
ALPHA = (2 * DEPTH) ** 0.25
LN_EPS = 1e-5
RMS_EPS = 1e-6
ATT_SCALE = (NOPE + ROPE) ** -0.5
NEG = float(jnp.finfo(jnp.float32).min)
ADAM_LR = 0.001
ADAM_B1 = 0.9
ADAM_B2 = 0.999
ADAM_EPS = 1e-08
ADAM_WD = 0.01
ADAM_STEP = 10
V7X_VMEM_BYTES = 64 * 1024 * 1024
VMEM_LIMIT = V7X_VMEM_BYTES - 8 * 1024 * 1024
MESH = pl.DeviceIdType.MESH


def _cp(*sem):
    return pltpu.CompilerParams(dimension_semantics=sem if sem else None, vmem_limit_bytes=VMEM_LIMIT)


def _dot(a, b):
    return jnp.dot(a, b, preferred_element_type=F32)


def _dot_nt(a, b):
    return lax.dot_general(a, b, (((1,), (1,)), ((), ())), preferred_element_type=F32)


def _dot_tn(a, b):
    return lax.dot_general(a, b, (((0,), (0,)), ((), ())), preferred_element_type=F32)


def _full(shape):
    return pl.BlockSpec(shape, lambda *_: (0,) * len(shape))


def _mm(a, b, *, mode, grid, a_spec, b_spec, out_shape, out_spec, name, add=None, add_spec=None, add_scale=1.0,
        dep=None):
    dot = {"nn": _dot, "nt": _dot_nt, "tn": _dot_tn}[mode]

    def body(*refs):
        a_ref, b_ref, o_ref = refs[0], refs[1], refs[-1]
        acc = dot(a_ref[...].astype(BF16), b_ref[...].astype(BF16))
        if add is not None:
            acc = acc + add_scale * refs[2][...]
        o_ref[...] = acc.astype(o_ref.dtype)

    ops = [a, b] if add is None else [a, b, add]
    specs = [a_spec, b_spec] if add is None else [a_spec, b_spec, add_spec]
    if dep is not None:
        ops.append(dep)
        specs.append(pl.BlockSpec(memory_space=pl.ANY))
    return pl.pallas_call(body, grid=grid, in_specs=specs, out_specs=out_spec, out_shape=out_shape,
                          compiler_params=_cp(*(("parallel",) * len(grid))), name=name)(*ops)


def _ln_stats(z):
    mu = jnp.mean(z, axis=-1, keepdims=True)
    zc = z - mu
    var = jnp.mean(zc * zc, axis=-1, keepdims=True)
    rstd = lax.rsqrt(var + LN_EPS)
    return zc * rstd, rstd


def _row_tile(t):
    return min(1024, t)


def _resid_ln(x, mix, g3, b3, l, name):
    t = x.shape[0]
    bm = _row_tile(t)

    def body(x_ref, m_ref, g_ref, b_ref, z_ref, y_ref, yb_ref):
        z = ALPHA * x_ref[...] + m_ref[...]
        xh, _ = _ln_stats(z)
        y = xh * g_ref[...] + b_ref[...]
        z_ref[...] = z
        y_ref[...] = y
        yb_ref[...] = y.astype(BF16)

    row = pl.BlockSpec((bm, D), lambda i: (i, 0))
    vec = pl.BlockSpec((None, 1, D), lambda i: (l, 0, 0))
    return pl.pallas_call(body, grid=(t // bm,), in_specs=[row, row, vec, vec], out_specs=[row, row, row],
                          out_shape=[S((t, D), F32), S((t, D), F32), S((t, D), BF16)],
                          compiler_params=_cp("parallel"), name=name)(x, mix, g3, b3)


def _proj_resid_ln(x, a, wmat, g3, b3, l, name):
    t, k = a.shape
    bm = _row_tile(t)

    def body(x_ref, a_ref, w_ref, g_ref, b_ref, z_ref, y_ref, yb_ref):
        z = ALPHA * x_ref[...] + _dot(a_ref[...], w_ref[...])
        xh, _ = _ln_stats(z)
        y = xh * g_ref[...] + b_ref[...]
        z_ref[...] = z
        y_ref[...] = y
        yb_ref[...] = y.astype(BF16)

    row = pl.BlockSpec((bm, D), lambda i: (i, 0))
    vec = pl.BlockSpec((None, 1, D), lambda i: (l, 0, 0))
    return pl.pallas_call(body, grid=(t // bm,),
                          in_specs=[row, pl.BlockSpec((bm, k), lambda i: (i, 0)), _full((k, D)), vec, vec],
                          out_specs=[row, row, row], out_shape=[S((t, D), F32), S((t, D), F32), S((t, D), BF16)],
                          compiler_params=_cp("parallel"), name=name)(x, a, wmat, g3, b3)


def _ln_bwd(d, z, g3, l, name, r=None, dep=None):
    t = z.shape[0]
    bm = _row_tile(t)

    def body(*refs):
        refs = list(refs)
        d_ref = refs.pop(0)
        dy = d_ref[...]
        if r is not None:
            dy = dy + ALPHA * refs.pop(0)[...]
        z_ref, g_ref = refs.pop(0), refs.pop(0)
        if dep is not None:
            refs.pop(0)
        dz_ref, dzb_ref, dg_ref, db_ref = refs
        xh, rstd = _ln_stats(z_ref[...])
        dyg = dy * g_ref[...]
        m1 = jnp.mean(dyg, axis=-1, keepdims=True)
        m2 = jnp.mean(dyg * xh, axis=-1, keepdims=True)
        dz = rstd * (dyg - m1 - xh * m2)
        dz_ref[...] = dz
        dzb_ref[...] = dz.astype(BF16)

        @pl.when(pl.program_id(0) == 0)
        def _():
            dg_ref[...] = jnp.zeros_like(dg_ref)
            db_ref[...] = jnp.zeros_like(db_ref)

        dg_ref[...] += jnp.sum(dy * xh, axis=0, keepdims=True)
        db_ref[...] += jnp.sum(dy, axis=0, keepdims=True)

    row = pl.BlockSpec((bm, D), lambda i: (i, 0))
    vec = pl.BlockSpec((None, 1, D), lambda i: (l, 0, 0))
    acc = pl.BlockSpec((1, D), lambda i: (0, 0))
    ops = [d, z, g3] if r is None else [d, r, z, g3]
    specs = [row, row, vec] if r is None else [row, row, row, vec]
    if dep is not None:
        ops.append(dep)
        specs.append(_full(dep.shape))
    return pl.pallas_call(body, grid=(t // bm,), in_specs=specs, out_specs=[row, row, acc, acc],
                          out_shape=[S((t, D), F32), S((t, D), BF16), S((1, D), F32), S((1, D), F32)],
                          compiler_params=_cp("arbitrary"), name=name)(*ops)


def _loss_grad(y, tgt):
    t = y.shape[0]
    bm = _row_tile(t)

    def body(y_ref, t_ref, dy_ref, loss_ref, acc_ref):
        i = pl.program_id(0)
        e = y_ref[...] - t_ref[...]
        dy_ref[...] = e * (1.0 / D)

        @pl.when(i == 0)
        def _():
            acc_ref[...] = jnp.zeros_like(acc_ref)

        acc_ref[...] += jnp.sum(e * e, axis=0, keepdims=True)

        @pl.when(i == pl.num_programs(0) - 1)
        def _():
            loss_ref[...] = jnp.full(loss_ref.shape, (0.5 / D) * jnp.sum(acc_ref[...]), F32)

    row = pl.BlockSpec((bm, D), lambda i: (i, 0))
    return pl.pallas_call(body, grid=(t // bm,), in_specs=[row, row],
                          out_specs=[row, pl.BlockSpec((1, 128), lambda i: (0, 0))],
                          out_shape=[S((t, D), F32), S((1, 128), F32)],
                          scratch_shapes=[pltpu.VMEM((1, D), F32)],
                          compiler_params=_cp("arbitrary"), name="loss_grad")(y, tgt)


def _mlp_row_tile(t):
    return min(1024, t)


MLP_ROW_PARTS = 4


def _row_parts(bm):
    step = bm // MLP_ROW_PARTS
    return [slice(k * step, (k + 1) * step) for k in range(MLP_ROW_PARTS)]


def _mlp_fwd(y, yb, w1g, w2g, g3, b3, l, dep=None):
    t = yb.shape[0]
    bm = _mlp_row_tile(t)

    def body(*refs):
        y_ref, yb_ref, w1_ref, w2_ref, g_ref, b_ref = refs[:6]
        z_ref, o_ref, ob_ref, act_ref, acc_ref = refs[-5:]
        j = pl.program_id(1)

        @pl.when(j == 0)
        def _():
            acc_ref[...] = jnp.zeros_like(acc_ref)

        for rows in _row_parts(bm):
            h = jnp.maximum(_dot(yb_ref[rows, :], w1_ref[...]), 0.0)
            act = (h * h).astype(BF16)
            act_ref[rows, :] = act
            acc_ref[rows, :] += _dot(act, w2_ref[...])

        @pl.when(j == N_DEV - 1)
        def _():
            z = ALPHA * y_ref[...] + acc_ref[...]
            xh, _ = _ln_stats(z)
            out = xh * g_ref[...] + b_ref[...]
            z_ref[...] = z
            o_ref[...] = out
            ob_ref[...] = out.astype(BF16)

    row = pl.BlockSpec((bm, D), lambda i, j: (i, 0))
    vec = pl.BlockSpec((None, 1, D), lambda i, j: (l, 0, 0))
    deps = [] if dep is None else [dep]
    return pl.pallas_call(
        body, grid=(t // bm, N_DEV),
        in_specs=[row, row, pl.BlockSpec((None, D, FF_BLK), lambda i, j: (j, 0, 0)),
                  pl.BlockSpec((None, FF_BLK, D), lambda i, j: (j, 0, 0)), vec, vec] + [ANY] * len(deps),
        out_specs=[row, row, row, pl.BlockSpec((bm, FF_BLK), lambda i, j: (i, j))],
        out_shape=[S((t, D), F32), S((t, D), F32), S((t, D), BF16), S((t, D_FF), BF16)],
        scratch_shapes=[pltpu.VMEM((bm, D), F32)],
        compiler_params=_cp("parallel", "arbitrary"), name="mlp_fwd")(y, yb, w1g, w2g, g3, b3, *deps)


def _mlp_bwd_dh(act, dzb, w1g, w2g):
    t = act.shape[0]
    bm = _mlp_row_tile(t)

    def body(a_ref, dz_ref, w1_ref, w2_ref, dh_ref, acc_ref):
        @pl.when(pl.program_id(1) == 0)
        def _():
            acc_ref[...] = jnp.zeros_like(acc_ref)

        for rows in _row_parts(bm):
            r = jnp.sqrt(a_ref[rows, :].astype(F32))
            dh = (_dot_nt(dz_ref[rows, :], w2_ref[...]) * (2.0 * r)).astype(BF16)
            dh_ref[rows, :] = dh
            acc_ref[rows, :] += _dot_nt(dh, w1_ref[...])

    row = pl.BlockSpec((bm, D), lambda i, j: (i, 0))
    hid = pl.BlockSpec((bm, FF_BLK), lambda i, j: (i, j))
    return pl.pallas_call(
        body, grid=(t // bm, N_DEV),
        in_specs=[hid, row,
                  pl.BlockSpec((None, D, FF_BLK), lambda i, j: (j, 0, 0)),
                  pl.BlockSpec((None, FF_BLK, D), lambda i, j: (j, 0, 0))],
        out_specs=[hid, row],
        out_shape=[S((t, D_FF), BF16), S((t, D), F32)],
        compiler_params=_cp("parallel", "arbitrary"), name="mlp_bwd_dh")(act, dzb, w1g, w2g)


F32_SUBLANES = 8


def _shift_dn(x, k, rows, fill=0.0):
    if k % F32_SUBLANES == 0:
        return jnp.concatenate([jnp.full((k,) + x.shape[1:], fill, x.dtype), x[:x.shape[0] - k]], axis=0)
    return jnp.where(rows >= k, pltpu.roll(x, k, 0), fill)


def _shift_up(x, k, rows, fill=0.0):
    t = x.shape[0]
    if k % F32_SUBLANES == 0:
        return jnp.concatenate([x[k:], jnp.full((k,) + x.shape[1:], fill, x.dtype)], axis=0)
    return jnp.where(rows < t - k, pltpu.roll(x, t - k, 0), fill)


def _scan_rows(a, b, shift):
    rows = lax.broadcasted_iota(jnp.int32, a.shape, 0)
    k = 1
    t = a.shape[0]
    while k < t:
        b = a * shift(b, k, rows) + b
        if 2 * k < t:
            a = a * shift(a, k, rows, 1.0)
        k *= 2
    return b


def _scan_dn(a, b):
    return _scan_rows(a, b, _shift_dn)


def _scan_up(a, b):
    return _scan_rows(a, b, _shift_up)


def _window_sum_dn(x, w, rows):
    k = 1
    while k < w:
        x = x + _shift_dn(x, k, rows)
        k *= 2
    return x


def _window_sum_up(x, w, rows):
    k = 1
    while k < w:
        x = x + _shift_up(x, k, rows)
        k *= 2
    return x


def _pool_diff(u, w, rows):
    inv_count = 1.0 / jnp.minimum(rows + 1, w).astype(F32)
    return _window_sum_dn(u, w, rows) * inv_count - u, inv_count


def _pool_fwd(proj, pool_w, pool_scale3, j):
    t = proj.shape[0]

    def body(u_ref, w_ref, s_ref, y_ref):
        rows = lax.broadcasted_iota(jnp.int32, (t, HEAD), 0)
        for g, w in enumerate(POOL_WINDOWS):
            cols = slice(g * HEAD, (g + 1) * HEAD)
            d, _ = _pool_diff(u_ref[:, cols], w, rows)
            y = _dot(d.astype(BF16), w_ref[g].astype(BF16)) * s_ref[:, cols]
            y_ref[:, cols] = y.astype(BF16)

    return pl.pallas_call(
        body, grid=(1,),
        in_specs=[pl.BlockSpec((t, POOL_W), lambda i: (0, 0)),
                  pl.BlockSpec((None, 4, HEAD, HEAD), lambda i: (j, 0, 0, 0)),
                  pl.BlockSpec((None, 1, POOL_W), lambda i: (j, 0, 0))],
        out_specs=pl.BlockSpec((t, POOL_W), lambda i: (0, 0)),
        out_shape=S((t, POOL_W), BF16), compiler_params=_cp("arbitrary"), name="pool_fwd")(proj, pool_w, pool_scale3)


def _pool_bwd(proj, dycat, pool_w, pool_scale3, j):
    t = proj.shape[0]

    def body(u_ref, dy_ref, w_ref, s_ref, du_ref, dw_ref, ds_ref):
        rows = lax.broadcasted_iota(jnp.int32, (t, HEAD), 0)
        for g, w in enumerate(POOL_WINDOWS):
            cols = slice(g * HEAD, (g + 1) * HEAD)
            d, inv_count = _pool_diff(u_ref[:, cols], w, rows)
            db = d.astype(BF16)
            wg = w_ref[g].astype(BF16)
            dy = dy_ref[:, cols]
            ds_ref[:, cols] = jnp.sum(dy * _dot(db, wg), axis=0, keepdims=True)
            dzz = (dy * s_ref[:, cols]).astype(BF16)
            dw_ref[g] = _dot_tn(db, dzz)
            dd = _dot_nt(dzz, wg)
            du_ref[:, cols] = (_window_sum_up(dd * inv_count, w, rows) - dd).astype(BF16)

    return pl.pallas_call(
        body, grid=(1,),
        in_specs=[pl.BlockSpec((t, POOL_W), lambda i: (0, 0)),
                  pl.BlockSpec((t, POOL_W), lambda i: (0, 0)),
                  pl.BlockSpec((None, 4, HEAD, HEAD), lambda i: (j, 0, 0, 0)),
                  pl.BlockSpec((None, 1, POOL_W), lambda i: (j, 0, 0))],
        out_specs=[pl.BlockSpec((t, POOL_W), lambda i: (0, 0)), _full((4, HEAD, HEAD)), _full((1, POOL_W))],
        out_shape=[S((t, POOL_W), BF16), S((4, HEAD, HEAD), F32), S((1, POOL_W), F32)],
        compiler_params=_cp("arbitrary"), name="pool_bwd")(proj, dycat, pool_w, pool_scale3)


GELU_C = 0.7978845608028654
GELU_K = 0.044715


def _gelu(x):
    th = jnp.tanh(GELU_C * (x + GELU_K * x * x * x))
    return 0.5 * x * (1.0 + th), th


def _lru_forward(u, gate, cw, cb, wa, ba, wx, bx, lam, rows):
    v = cw[3:4] * u + cw[2:3] * _shift_dn(u, 1, rows) + cw[1:2] * _shift_dn(u, 2, rows) \
        + cw[0:1] * _shift_dn(u, 3, rows) + cb
    vb = v.astype(BF16)
    r = jax.nn.sigmoid(_dot(vb, wa) + ba)
    i = jax.nn.sigmoid(_dot(vb, wx) + bx)
    sp = jnp.maximum(-lam, 0.0) + jnp.log1p(jnp.exp(-jnp.abs(lam)))
    log_a = (-LRU_C) * r * sp
    a = jnp.exp(log_a)
    one_m_a2 = -jnp.tanh(log_a) * (a * a + 1.0)
    mult = jnp.sqrt(one_m_a2)
    h = _scan_dn(a, mult * (i * v))
    gl, th = _gelu(gate)
    return dict(v=v, vb=vb, r=r, i=i, sp=sp, a=a, mult=mult, h=h, gl=gl, th=th)


def _lru_specs(t, j, col0_u, col0_g):
    blk = lambda c0: pl.BlockSpec((t, HEAD), lambda h: (0, c0 + h))
    vec = pl.BlockSpec((None, 1, HEAD), lambda h: (j, 0, h))
    return [blk(col0_u), blk(col0_g),
            pl.BlockSpec((None, 4, HEAD), lambda h: (j, 0, h)), vec,
            pl.BlockSpec((None, None, HEAD, HEAD), lambda h: (j, h, 0, 0)), vec,
            pl.BlockSpec((None, None, HEAD, HEAD), lambda h: (j, h, 0, 0)), vec, vec]


def _lru_fwd(proj, p, j):
    t = proj.shape[0]

    def body(u_ref, g_ref, cw_ref, cb_ref, wa_ref, ba_ref, wx_ref, bx_ref, lam_ref, y_ref):
        rows = lax.broadcasted_iota(jnp.int32, (t, HEAD), 0)
        f = _lru_forward(u_ref[...], g_ref[...], cw_ref[...], cb_ref[...], wa_ref[...].astype(BF16), ba_ref[...],
                         wx_ref[...].astype(BF16), bx_ref[...], lam_ref[...], rows)
        y_ref[...] = (f["h"] * f["gl"]).astype(BF16)

    return pl.pallas_call(
        body, grid=(LRU_HEADS,), in_specs=_lru_specs(t, j, POOL_W // HEAD, (POOL_W + LRU_W) // HEAD),
        out_specs=pl.BlockSpec((t, HEAD), lambda h: (0, h)), out_shape=S((t, LRU_W), BF16),
        compiler_params=_cp("parallel"), name="lru_fwd")(
            proj, proj, p["conv_w"], p["conv_b"], p["w_a"], p["b_a"], p["w_x"], p["b_x"], p["lam"])


def _lru_bwd(proj, dycat, p, j):
    t = proj.shape[0]

    def body(u_ref, g_ref, cw_ref, cb_ref, wa_ref, ba_ref, wx_ref, bx_ref, lam_ref, dy_ref,
             du_ref, dgate_ref, dcw_ref, dcb_ref, dwa_ref, dba_ref, dwx_ref, dbx_ref, dlam_ref):
        rows = lax.broadcasted_iota(jnp.int32, (t, HEAD), 0)
        u = u_ref[...]
        gate = g_ref[...]
        cw = cw_ref[...]
        wa = wa_ref[...].astype(BF16)
        wx = wx_ref[...].astype(BF16)
        lam = lam_ref[...]
        f = _lru_forward(u, gate, cw, cb_ref[...], wa, ba_ref[...], wx, bx_ref[...], lam, rows)
        v, r, i, a, mult, h, th = f["v"], f["r"], f["i"], f["a"], f["mult"], f["h"], f["th"]
        dy = dy_ref[...]
        dgl = 0.5 * (1.0 + th) + 0.5 * gate * (1.0 - th * th) * GELU_C * (1.0 + 3.0 * GELU_K * gate * gate)
        dgate_ref[...] = (dy * h * dgl).astype(BF16)
        g = _scan_up(_shift_up(a, 1, rows), dy * f["gl"])
        da = g * _shift_dn(h, 1, rows)
        iv = i * v
        dmult = g * iv
        di = g * mult * v
        dv = g * mult * i
        dlog_a = da * a - dmult * (a * a) / mult
        dr = dlog_a * (-LRU_C) * f["sp"]
        dsp = jnp.sum(dlog_a * (-LRU_C) * r, axis=0, keepdims=True)
        dlam_ref[...] = -dsp * jax.nn.sigmoid(-lam)
        dpa = dr * r * (1.0 - r)
        dpx = di * i * (1.0 - i)
        dpab = dpa.astype(BF16)
        dpxb = dpx.astype(BF16)
        dwa_ref[...] = _dot_tn(f["vb"], dpab)
        dwx_ref[...] = _dot_tn(f["vb"], dpxb)
        dba_ref[...] = jnp.sum(dpa, axis=0, keepdims=True)
        dbx_ref[...] = jnp.sum(dpx, axis=0, keepdims=True)
        dv = dv + _dot_nt(dpab, wa) + _dot_nt(dpxb, wx)
        dcb_ref[...] = jnp.sum(dv, axis=0, keepdims=True)
        du = cw[3:4] * dv
        dcw_ref[3:4, :] = jnp.sum(dv * u, axis=0, keepdims=True)
        for k in (1, 2, 3):
            du = du + cw[3 - k:4 - k] * _shift_up(dv, k, rows)
            dcw_ref[3 - k:4 - k, :] = jnp.sum(dv * _shift_dn(u, k, rows), axis=0, keepdims=True)
        du_ref[...] = du.astype(BF16)

    blk = pl.BlockSpec((t, HEAD), lambda h: (0, h))
    vec = pl.BlockSpec((1, HEAD), lambda h: (0, h))
    mat = pl.BlockSpec((None, HEAD, HEAD), lambda h: (h, 0, 0))
    return pl.pallas_call(
        body, grid=(LRU_HEADS,),
        in_specs=_lru_specs(t, j, POOL_W // HEAD, (POOL_W + LRU_W) // HEAD)
        + [pl.BlockSpec((t, HEAD), lambda h: (0, POOL_W // HEAD + h))],
        out_specs=[blk, blk, pl.BlockSpec((4, HEAD), lambda h: (0, h)), vec, mat, vec, mat, vec, vec],
        out_shape=[S((t, LRU_W), BF16), S((t, LRU_W), BF16), S((4, LRU_W), F32), S((1, LRU_W), F32),
                   S((LRU_HEADS, HEAD, HEAD), F32), S((1, LRU_W), F32),
                   S((LRU_HEADS, HEAD, HEAD), F32), S((1, LRU_W), F32), S((1, LRU_W), F32)],
        compiler_params=_cp("parallel"), name="lru_bwd")(
            proj, proj, p["conv_w"], p["conv_b"], p["w_a"], p["b_a"], p["w_x"], p["b_x"], p["lam"], dycat)


def _rope(x, c, s):
    x1 = x[:, :ROPE // 2]
    x2 = x[:, ROPE // 2:]
    return jnp.concatenate([x1 * c - x2 * s, x1 * s + x2 * c], axis=-1)


def _rope_t(d, c, s):
    d1 = d[:, :ROPE // 2]
    d2 = d[:, ROPE // 2:]
    return jnp.concatenate([d1 * c + d2 * s, d2 * c - d1 * s], axis=-1)


def _rope_tables(pos2, inv_freq):
    t = pos2.shape[0]

    def body(p_ref, f_ref, c_ref, s_ref):
        ang = p_ref[...].astype(F32) * f_ref[...]
        c_ref[...] = jnp.cos(ang)
        s_ref[...] = jnp.sin(ang)

    return pl.pallas_call(body, out_shape=[S((t, ROPE // 2), F32), S((t, ROPE // 2), F32)],
                          name="rope_tables")(pos2, inv_freq)


def _down_norm(xb, wdown_g, gq3, gkv3, cos, sin, j):
    t = xb.shape[0]
    bm = _row_tile(t)

    def body(x_ref, w_ref, gq_ref, gkv_ref, c_ref, s_ref, down_ref, cq_ref, ckv_ref, kpe_ref):
        w = w_ref[...].reshape(D, ODD_IN)
        down = _dot(x_ref[...], w)
        down_ref[...] = down
        q = down[:, :Q_RANK]
        cq_ref[...] = (q * lax.rsqrt(jnp.mean(q * q, axis=-1, keepdims=True) + RMS_EPS) * gq_ref[...]).astype(BF16)
        kv = down[:, Q_RANK:Q_RANK + KV_RANK]
        ckv_ref[...] = (kv * lax.rsqrt(jnp.mean(kv * kv, axis=-1, keepdims=True) + RMS_EPS)
                        * gkv_ref[...]).astype(BF16)
        kpe_ref[...] = _rope(down[:, Q_RANK + KV_RANK:], c_ref[...], s_ref[...])

    row = lambda n: pl.BlockSpec((bm, n), lambda i: (i, 0))
    return pl.pallas_call(
        body, grid=(t // bm,),
        in_specs=[row(D), _full((N_DEV, D // N_DEV, ODD_IN)),
                  pl.BlockSpec((None, 1, Q_RANK), lambda i: (j, 0, 0)),
                  pl.BlockSpec((None, 1, KV_RANK), lambda i: (j, 0, 0)), row(ROPE // 2), row(ROPE // 2)],
        out_specs=[row(ODD_IN), row(Q_RANK), row(KV_RANK), row(ROPE)],
        out_shape=[S((t, ODD_IN), F32), S((t, Q_RANK), BF16), S((t, KV_RANK), BF16), S((t, ROPE), F32)],
        compiler_params=_cp("parallel"), name="down_norm")(xb, wdown_g, gq3, gkv3, cos, sin)


def _q_tile(t, widest):
    return min(widest, t // 2)


def _attn_probs(q, k, qs):
    s = _dot_nt(q, k) * ATT_SCALE
    tq = q.shape[0]
    rows = lax.broadcasted_iota(jnp.int32, (tq, tq), 0)
    cols = lax.broadcasted_iota(jnp.int32, (tq, tq), 1)
    last = jnp.where(jnp.right_shift(cols, CHUNK_SHIFT) <= jnp.right_shift(rows, CHUNK_SHIFT), s[:, qs:], NEG)
    s = last if qs == 0 else jnp.concatenate([s[:, :qs], last], axis=1)
    e = jnp.exp(s - jnp.max(s, axis=-1, keepdims=True))
    return e / jnp.sum(e, axis=-1, keepdims=True)


def _head_qkv(cq, ckv, kpe, c, s, wq_ref, wkv_ref):
    q = jnp.concatenate([_dot(cq, wq_ref[:, :NOPE]), _rope(_dot(cq, wq_ref[:, NOPE:]), c, s)], axis=1).astype(BF16)
    k = jnp.concatenate([_dot(ckv, wkv_ref[:, :NOPE]), kpe], axis=1).astype(BF16)
    vv = _dot(ckv, wkv_ref[:, NOPE:]).astype(BF16)
    return q, k, vv


def _attn_in_specs(t):
    return [_full((t, Q_RANK)), _full((t, KV_RANK)), _full((t, ROPE)), _full((t, ROPE // 2)), _full((t, ROPE // 2)),
            pl.BlockSpec((None, Q_RANK, NOPE + ROPE), lambda h: (h, 0, 0)),
            pl.BlockSpec((None, KV_RANK, NOPE + VDIM), lambda h: (h, 0, 0)),
            pl.BlockSpec((None, VDIM, D), lambda h: (h, 0, 0))]


def _attn_fwd(cq, ckv, kpe, cos, sin, wqb_g, wkvb_g, wo_g):
    t = cq.shape[0]
    tq = _q_tile(t, 128)

    def body(cq_ref, ckv_ref, kpe_ref, c_ref, s_ref, wq_ref, wkv_ref, wo_ref, o_ref, mix_ref):
        q, k, vv = _head_qkv(cq_ref[...], ckv_ref[...], kpe_ref[...], c_ref[...], s_ref[...], wq_ref, wkv_ref)
        for qs in range(0, t, tq):
            ke = qs + tq
            p = _attn_probs(q[qs:ke], k[:ke], qs)
            o_ref[qs:ke, :] = _dot(p.astype(BF16), vv[:ke]).astype(BF16)
        c = _dot(o_ref[...], wo_ref[...])

        @pl.when(pl.program_id(0) == 0)
        def _():
            mix_ref[...] = c

        @pl.when(pl.program_id(0) > 0)
        def _():
            mix_ref[...] += c

    return pl.pallas_call(
        body, grid=(MLA_HEADS,), in_specs=_attn_in_specs(t),
        out_specs=[pl.BlockSpec((None, t, VDIM), lambda h: (h, 0, 0)), _full((t, D))],
        out_shape=[S((MLA_HEADS, t, VDIM), BF16), S((t, D), F32)],
        compiler_params=_cp("arbitrary"), name="attn_fwd")(cq, ckv, kpe, cos, sin, wqb_g, wkvb_g, wo_g)


def _attn_bwd(cq, ckv, kpe, cos, sin, wqb_g, wkvb_g, wo_g, o, dzb):
    t = cq.shape[0]
    tq = _q_tile(t, 512)

    def body(cq_ref, ckv_ref, kpe_ref, c_ref, s_ref, wq_ref, wkv_ref, wo_ref, o_ref, dz_ref,
             dwo_ref, dwq_ref, dwkv_ref, dcq_ref, dckv_ref, dkpe_ref, dkt_s, dvt_s, dq_s):
        cqv = cq_ref[...]
        ckvv = ckv_ref[...]
        c = c_ref[...]
        s = s_ref[...]
        q, k, vv = _head_qkv(cqv, ckvv, kpe_ref[...], c, s, wq_ref, wkv_ref)
        dzv = dz_ref[...]
        dwo_ref[...] = _dot_tn(o_ref[...], dzv).astype(BF16)
        do = _dot_nt(dzv, wo_ref[...]).astype(BF16)
        dkt_s[...] = jnp.zeros_like(dkt_s)
        dvt_s[...] = jnp.zeros_like(dvt_s)
        for qs in range(0, t, tq):
            ke = qs + tq
            p = _attn_probs(q[qs:ke], k[:ke], qs)
            dp = _dot_nt(do[qs:ke], vv[:ke])
            ds = (p * (dp - jnp.sum(p * dp, axis=-1, keepdims=True)) * ATT_SCALE).astype(BF16)
            dq_s[qs:ke, :] = _dot(ds, k[:ke])
            dkt_s[0:NOPE + ROPE, 0:ke] += _dot_tn(q[qs:ke], ds)
            dvt_s[:, 0:ke] += _dot_tn(do[qs:ke], p.astype(BF16))
        dk = dkt_s[...].T
        dqn = dq_s[:, :NOPE].astype(BF16)
        dqp = _rope_t(dq_s[:, NOPE:], c, s).astype(BF16)
        dkn = dk[:, :NOPE].astype(BF16)
        dkp = dk[:, NOPE:NOPE + ROPE]
        dvv = dvt_s[...].T.astype(BF16)
        dwq_ref[:, :NOPE] = _dot_tn(cqv, dqn).astype(BF16)
        dwq_ref[:, NOPE:] = _dot_tn(cqv, dqp).astype(BF16)
        dwkv_ref[:, :NOPE] = _dot_tn(ckvv, dkn).astype(BF16)
        dwkv_ref[:, NOPE:] = _dot_tn(ckvv, dvv).astype(BF16)
        dcq = _dot_nt(dqn, wq_ref[:, :NOPE]) + _dot_nt(dqp, wq_ref[:, NOPE:])
        dckv = _dot_nt(dkn, wkv_ref[:, :NOPE]) + _dot_nt(dvv, wkv_ref[:, NOPE:])

        @pl.when(pl.program_id(0) == 0)
        def _():
            dcq_ref[...] = dcq
            dckv_ref[...] = dckv
            dkpe_ref[...] = dkp

        @pl.when(pl.program_id(0) > 0)
        def _():
            dcq_ref[...] += dcq
            dckv_ref[...] += dckv
            dkpe_ref[...] += dkp

    per_head = lambda a, b: pl.BlockSpec((None, a, b), lambda h: (h, 0, 0))
    return pl.pallas_call(
        body, grid=(MLA_HEADS,),
        in_specs=_attn_in_specs(t) + [per_head(t, VDIM), _full((t, D))],
        out_specs=[per_head(VDIM, D), per_head(Q_RANK, NOPE + ROPE), per_head(KV_RANK, NOPE + VDIM),
                   _full((t, Q_RANK)), _full((t, KV_RANK)), _full((t, ROPE))],
        out_shape=[S((MLA_HEADS, VDIM, D), BF16), S((MLA_HEADS, Q_RANK, NOPE + ROPE), BF16),
                   S((MLA_HEADS, KV_RANK, NOPE + VDIM), BF16),
                   S((t, Q_RANK), F32), S((t, KV_RANK), F32), S((t, ROPE), F32)],
        scratch_shapes=[pltpu.VMEM((2 * NOPE, t), F32), pltpu.VMEM((VDIM, t), F32),
                        pltpu.VMEM((t, NOPE + ROPE), F32)],
        compiler_params=_cp("arbitrary"), name="attn_bwd")(cq, ckv, kpe, cos, sin, wqb_g, wkvb_g, wo_g, o, dzb)


def _rms_bwd(down, dcq, dckv, dkpe, cos, sin, gq3, gkv3, j):
    t = down.shape[0]
    bm = _row_tile(t)

    def body(down_ref, dcq_ref, dckv_ref, dkpe_ref, c_ref, s_ref, gq_ref, gkv_ref, dd_ref, dgq_ref, dgkv_ref):
        @pl.when(pl.program_id(0) == 0)
        def _():
            dgq_ref[...] = jnp.zeros_like(dgq_ref)
            dgkv_ref[...] = jnp.zeros_like(dgkv_ref)

        def rms_b(x, dy, g):
            rstd = lax.rsqrt(jnp.mean(x * x, axis=-1, keepdims=True) + RMS_EPS)
            xh = x * rstd
            dyg = dy * g
            return rstd * (dyg - xh * jnp.mean(dyg * xh, axis=-1, keepdims=True)), jnp.sum(dy * xh, axis=0, keepdims=True)

        dq, dgq = rms_b(down_ref[:, :Q_RANK], dcq_ref[...], gq_ref[...])
        dkv, dgkv = rms_b(down_ref[:, Q_RANK:Q_RANK + KV_RANK], dckv_ref[...], gkv_ref[...])
        dgq_ref[...] += dgq
        dgkv_ref[...] += dgkv
        dd_ref[:, :Q_RANK] = dq.astype(BF16)
        dd_ref[:, Q_RANK:Q_RANK + KV_RANK] = dkv.astype(BF16)
        dd_ref[:, Q_RANK + KV_RANK:] = _rope_t(dkpe_ref[...], c_ref[...], s_ref[...]).astype(BF16)

    row = lambda n: pl.BlockSpec((bm, n), lambda i: (i, 0))
    return pl.pallas_call(
        body, grid=(t // bm,),
        in_specs=[row(ODD_IN), row(Q_RANK), row(KV_RANK), row(ROPE), row(ROPE // 2), row(ROPE // 2),
                  pl.BlockSpec((None, 1, Q_RANK), lambda i: (j, 0, 0)),
                  pl.BlockSpec((None, 1, KV_RANK), lambda i: (j, 0, 0))],
        out_specs=[row(ODD_IN), _full((1, Q_RANK)), _full((1, KV_RANK))],
        out_shape=[S((t, ODD_IN), BF16), S((1, Q_RANK), F32), S((1, KV_RANK), F32)],
        compiler_params=_cp("arbitrary"), name="rms_bwd")(down, dcq, dckv, dkpe, cos, sin, gq3, gkv3)


def _col_blocks(t, n, bn):
    return pl.BlockSpec((t, bn), lambda i: (0, i))


def _row_blocks(n, bm):
    return pl.BlockSpec((bm, n), lambda i: (i, 0))


def _local_step(x, pos2, tgt, small, weights_of, grads_done, start_dep=None, prefetch=None):
    t = x.shape[0]
    bm = _row_tile(t)
    inv_freq = (ROPE_THETA ** (-jnp.arange(0, ROPE, 2, dtype=F32) / ROPE)).reshape(1, ROPE // 2)
    cos, sin = _rope_tables(pos2, inv_freq)
    lru_p = {k: small[k] for k in ("conv_w", "conv_b", "w_a", "b_a", "w_x", "b_x", "lam")}

    saved = []
    y, yb = x, x.astype(BF16)
    for l in range(DEPTH):
        j = l // 2
        big = weights_of(l, 0, y)
        sv = dict(xb=yb, big=big)
        if l % 2 == 0:
            proj = _mm(yb, big["win_t"], mode="nt", grid=(EVEN_IN // 512,), a_spec=_full((t, D)),
                       b_spec=_row_blocks(D, 512), out_shape=S((t, EVEN_IN), F32),
                       out_spec=_col_blocks(t, EVEN_IN, 512), name="even_proj", dep=start_dep if l == 0 else None)
            ycat = jnp.concatenate([_pool_fwd(proj, small["pool_w"], small["pool_scale"], j),
                                    _lru_fwd(proj, lru_p, j)], axis=1)
            big.update(weights_of(l, 1, ycat))
            z1, y1, y1b = _proj_resid_ln(y, ycat, big["wout2d"], small["ln_mix_g"], small["ln_mix_b"], l, "even_out")
            sv.update(proj=proj, ycat=ycat)
        else:
            down, cq, ckv, kpe = _down_norm(yb, big["wdown"], small["gq"], small["gkv"], cos, sin, j)
            o, mix = _attn_fwd(cq, ckv, kpe, cos, sin, big["wqb"], big["wkvb"], big["wo"])
            z1, y1, y1b = _resid_ln(y, mix, small["ln_mix_g"], small["ln_mix_b"], l, "resid_ln")
            sv.update(down=down, cq=cq, ckv=ckv, kpe=kpe, o=o)
        fetched = prefetch(l + 1, y1) if prefetch is not None and l + 1 < DEPTH else None
        z2, y, yb, act = _mlp_fwd(y1, y1b, big["w1"], big["w2"], small["ln_ffn_g"], small["ln_ffn_b"], l,
                                  dep=fetched)
        sv.update(z1=z1, y1b=y1b, z2=z2, act=act)
        saved.append(sv)

    dy, loss_tile = _loss_grad(y, tgt)

    g = {k: [None] * n for k, n in (("ln_mix_g", 4), ("ln_mix_b", 4), ("ln_ffn_g", 4), ("ln_ffn_b", 4),
                                    ("pool_w", 2), ("pool_scale", 2), ("conv_w", 2), ("conv_b", 2),
                                    ("w_a", 2), ("b_a", 2), ("w_x", 2), ("b_x", 2), ("lam", 2),
                                    ("gq", 2), ("gkv", 2))}
    dep = None
    for l in reversed(range(DEPTH)):
        j = l // 2
        sv = saved[l]
        big = sv["big"]
        dz2, dz2b, g["ln_ffn_g"][l], g["ln_ffn_b"][l] = _ln_bwd(dy, sv["z2"], small["ln_ffn_g"], l, "ln_bwd", dep=dep)
        act = sv["act"]
        dh, dff = _mlp_bwd_dh(act, dz2b, big["w1"], big["w2"])
        dw1 = _mm(sv["y1b"], dh, mode="tn", grid=(N_DEV,), a_spec=_full((t, D)),
                  b_spec=_col_blocks(t, D_FF, FF_BLK), out_shape=S((N_DEV, D, FF_BLK), BF16),
                  out_spec=pl.BlockSpec((None, D, FF_BLK), lambda i: (i, 0, 0)), name="mlp_dw1")
        dw2 = _mm(act, dz2b, mode="tn", grid=(N_DEV,), a_spec=_col_blocks(t, D_FF, FF_BLK),
                  b_spec=_full((t, D)), out_shape=S((N_DEV, FF_BLK, D), BF16),
                  out_spec=pl.BlockSpec((None, FF_BLK, D), lambda i: (i, 0, 0)), name="mlp_dw2")
        dep = grads_done(l, dict(w1=dw1, w2=dw2))
        dz1, dz1b, g["ln_mix_g"][l], g["ln_mix_b"][l] = _ln_bwd(dff, sv["z1"], small["ln_mix_g"], l, "ln_bwd_res",
                                                                 r=dz2, dep=dep)
        if l % 2 == 0:
            wout = big["wout2d"]
            dycat = _mm(dz1b, wout, mode="nt", grid=(EVEN_MIX // 512,), a_spec=_full((t, D)),
                        b_spec=_row_blocks(D, 512), out_shape=S((t, EVEN_MIX), F32),
                        out_spec=_col_blocks(t, EVEN_MIX, 512), name="even_dycat")
            dwout = _mm(sv["ycat"], dz1b, mode="tn", grid=(EVEN_MIX // 512,), a_spec=_col_blocks(t, EVEN_MIX, 512),
                        b_spec=_full((t, D)), out_shape=S((EVEN_MIX, D), BF16), out_spec=_row_blocks(D, 512),
                        name="even_dwout")
            du_pool, g["pool_w"][j], g["pool_scale"][j] = _pool_bwd(sv["proj"], dycat, small["pool_w"],
                                                                   small["pool_scale"], j)
            (du_lru, du_gate, g["conv_w"][j], g["conv_b"][j], g["w_a"][j], g["b_a"][j], g["w_x"][j], g["b_x"][j],
             g["lam"][j]) = _lru_bwd(sv["proj"], dycat, lru_p, j)
            dproj = jnp.concatenate([du_pool, du_lru, du_gate], axis=1)
            dwin = _mm(sv["xb"], dproj, mode="tn", grid=(EVEN_IN // 512,), a_spec=_full((t, D)),
                       b_spec=_col_blocks(t, EVEN_IN, 512), out_shape=S((D, EVEN_IN), BF16),
                       out_spec=_col_blocks(D, EVEN_IN, 512), name="even_dwin")
            dep = grads_done(l, dict(win=dwin.reshape(D, N_DEV, EVEN_IN // N_DEV).transpose(1, 0, 2),
                                     wout=dwout.reshape(N_DEV, EVEN_MIX // N_DEV, D)))
            dy = _mm(dproj, big["win_t"], mode="nn", grid=(t // bm,), a_spec=_row_blocks(EVEN_IN, bm),
                     b_spec=_full((EVEN_IN, D)), out_shape=S((t, D), F32), out_spec=_row_blocks(D, bm),
                     add=dz1, add_spec=_row_blocks(D, bm), add_scale=ALPHA, name="even_dx", dep=dep)
        else:
            dwo, dwqb, dwkvb, dcq, dckv, dkpe = _attn_bwd(
                sv["cq"], sv["ckv"], sv["kpe"], cos, sin, big["wqb"], big["wkvb"], big["wo"], sv["o"], dz1b)
            ddown, g["gq"][j], g["gkv"][j] = _rms_bwd(sv["down"], dcq, dckv, dkpe, cos, sin, small["gq"],
                                                     small["gkv"], j)
            dwdown = _mm(sv["xb"], ddown, mode="tn", grid=(N_DEV,), a_spec=_col_blocks(t, D, D // N_DEV),
                         b_spec=_full((t, ODD_IN)), out_shape=S((N_DEV, D // N_DEV, ODD_IN), BF16),
                         out_spec=pl.BlockSpec((None, D // N_DEV, ODD_IN), lambda i: (i, 0, 0)),
                         name="odd_dwdown")
            dep = grads_done(l, dict(wdown=dwdown, wqb=dwqb, wkvb=dwkvb, wo=dwo))
            dy = _mm(ddown, big["wdown2d"], mode="nt", grid=(t // bm,), a_spec=_row_blocks(ODD_IN, bm),
                     b_spec=_full((D, ODD_IN)), out_shape=S((t, D), F32), out_spec=_row_blocks(D, bm),
                     add=dz1, add_spec=_row_blocks(D, bm), add_scale=ALPHA, name="odd_dx", dep=dep)
    return loss_tile[0, 0], dy, g


def _mesh_place():
    x, y, c = lax.axis_index("x"), lax.axis_index("y"), lax.axis_index("c")
    return x, y, c


def _peer(place, k):
    x, y, c = place
    return (1 - x if k & 4 else x, 1 - y if k & 2 else y, 1 - c if k & 1 else c)


def _index(place):
    x, y, c = place
    return 4 * x + 2 * y + c


ANY = pl.BlockSpec(memory_space=pl.ANY)


def _make_zones(shards, me, name):
    n = len(shards)

    def body(me_ref, *refs):
        for src, dst in zip(refs[:n], refs[n:]):
            dst[...] = src[...].astype(BF16)

    grid_spec = pltpu.PrefetchScalarGridSpec(
        num_scalar_prefetch=1, grid=(1,),
        in_specs=[pl.BlockSpec(s.shape, lambda i, me_ref: (0, 0)) for s in shards],
        out_specs=[pl.BlockSpec((None,) + s.shape, lambda i, me_ref: (me_ref[0], 0, 0)) for s in shards])
    return pl.pallas_call(body, grid_spec=grid_spec, out_shape=[S((N_DEV,) + s.shape, BF16) for s in shards],
                          compiler_params=_cp("arbitrary"), name=name)(me, *shards)


def _all_gather_big(zones):
    n = len(zones)

    def body(*refs):
        outs = refs[n:2 * n]
        send, recv = refs[2 * n:]
        x, y, c = _mesh_place()
        me, sibling = (x, y, c), (x, y, 1 - c)
        chips = [(1 - x, y), (x, 1 - y), (1 - x, 1 - y)]

        def copy(w, k, block, to):
            blk = outs[w].at[_index(block)]
            return pltpu.make_async_remote_copy(src_ref=blk, dst_ref=blk, send_sem=send.at[w, k], recv_sem=recv.at[w, k],
                                                device_id=to, device_id_type=MESH)

        first = []
        for w in range(n):
            first.append(copy(w, 0, me, sibling))
            first += [copy(w, 1 + j, me, (*chip, c)) for j, chip in enumerate(chips)]
        for cp in first:
            cp.start()
        passed = []
        for w in range(n):
            for j, chip in enumerate(chips):
                copy(w, 1 + j, (*chip, c), me).wait_recv()
                cp = copy(w, 4 + j, (*chip, c), sibling)
                cp.start()
                passed.append(cp)
        for w in range(n):
            copy(w, 0, sibling, me).wait_recv()
            for j, chip in enumerate(chips):
                copy(w, 4 + j, (*chip, 1 - c), me).wait_recv()
        for cp in first + passed:
            cp.wait_send()

    return pl.pallas_call(
        body, in_specs=[ANY] * n, out_specs=[ANY] * n, out_shape=[S(z.shape, z.dtype) for z in zones],
        input_output_aliases={i: i for i in range(n)},
        scratch_shapes=[pltpu.SemaphoreType.DMA((n, N_DEV - 1)), pltpu.SemaphoreType.DMA((n, N_DEV - 1))],
        compiler_params=pltpu.CompilerParams(has_side_effects=True), name="all_gather_big")(*zones)


def _shard_rows_tile(a):
    return max(d for d in range(16, 257, 16) if a % d == 0)


HBM = pl.BlockSpec(memory_space=pltpu.HBM)
SEM = pl.BlockSpec(memory_space=pltpu.SEMAPHORE)
DATAFLOW = pltpu.SideEffectType.DATAFLOW_SIDE_EFFECTING


def _in_hbm(a):
    return pltpu.with_memory_space_constraint(a, pltpu.HBM)


def _gather_ici_copies(place, src, land, w):
    me = _index(place)
    return [(_peer(place, k), land.at[me], land.at[me]) for k in (1, 2, 4, 6)]


def _gather_d2d_copies(place, src, land, w):
    blocks = [_index(_peer(place, k)) for k in (2, 4, 6)]
    return [(_peer(place, 1), land.at[b], land.at[b]) for b in blocks]


GATHER_ICI = (4, _gather_ici_copies)
GATHER_D2D = (3, _gather_d2d_copies)


def _scatter_plan(layers):
    def copies(place, src, land, w):
        me = _index(place)
        mine = land.at[me] if layers[w] is None else land.at[me, layers[w]]
        return [(_peer(place, k), src.at[_index(_peer(place, k))], mine) for k in range(1, N_DEV)]
    return (N_DEV - 1, copies)


def _gather_all_copies(place, src, land, w):
    me = _index(place)
    return [(_peer(place, k), land.at[me], land.at[me]) for k in range(1, N_DEV)]


GATHER_ALL = (N_DEV - 1, _gather_all_copies)


def _sum_blocks(zone, part, me):
    r = part.shape[1]

    def body(me_ref, z_ref, p_ref, o_ref):
        acc = None
        for s in range(N_DEV):
            term = jnp.where(me_ref[0] == s, p_ref[...], z_ref[s])
            acc = term if acc is None else acc + term
        o_ref[...] = acc

    grid_spec = pltpu.PrefetchScalarGridSpec(
        num_scalar_prefetch=1, grid=(1,),
        in_specs=[pl.BlockSpec((N_DEV, r, 128), lambda i, me_ref: (0, 0, 0)),
                  pl.BlockSpec((None, r, 128), lambda i, me_ref: (me_ref[0], 0, 0))],
        out_specs=pl.BlockSpec((r, 128), lambda i, me_ref: (0, 0)))
    return pl.pallas_call(body, grid_spec=grid_spec, out_shape=S((r, 128), F32),
                          compiler_params=_cp("arbitrary"), name="sum_small")(me, zone, part)


def _exchange_start(srcs, lands, plan, name, after=()):
    ns, n = len(srcs), len(lands)
    n_in = ns + n + len(after)
    per, copies = plan

    def body(*refs):
        ins, land = refs[:ns], refs[ns:ns + n]
        send, recv = refs[n_in], refs[n_in + 1]
        token = refs[-1]
        place = _mesh_place()
        for i in range(per):
            for w in range(n):
                target, src, dst = copies(place, ins[w] if ns else None, land[w], w)[i]
                pltpu.make_async_remote_copy(src_ref=src, dst_ref=dst, send_sem=send.at[w * per + i],
                                             recv_sem=recv.at[w * per + i], device_id=target, device_id_type=MESH).start()
        token[...] = jnp.zeros_like(token)

    sems = pltpu.SemaphoreType.DMA((n * per,))
    thru = [pltpu.HBM(a.shape, a.dtype) for a in list(srcs) + list(lands)]
    out = pl.pallas_call(
        body, name=name, in_specs=[HBM] * (ns + n) + [ANY] * len(after),
        out_shape=(sems, sems, *thru, S((8, 128), F32)),
        out_specs=(SEM, SEM, *([HBM] * (ns + n)), pl.BlockSpec(memory_space=pltpu.VMEM)),
        input_output_aliases={i: 2 + i for i in range(ns + n)},
        compiler_params=pltpu.CompilerParams(has_side_effects=DATAFLOW),
    )(*[_in_hbm(a) for a in list(srcs) + list(lands)], *after)
    return out[0], out[1], list(out[2:2 + ns]), list(out[2 + ns:2 + ns + n]), out[-1]


def _exchange_wait(send, recv, srcs, lands, plan, after, name):
    ns, n = len(srcs), len(lands)
    per, copies = plan
    afters = tuple(after) if isinstance(after, (tuple, list)) else (after,)

    def body(*refs):
        ins, land = refs[:ns], refs[ns:ns + n]
        send_ref, recv_ref = refs[ns + n], refs[ns + n + 1]
        place = _mesh_place()
        for i in range(per):
            for w in range(n):
                target, src, dst = copies(place, ins[w] if ns else None, land[w], w)[i]
                cp = pltpu.make_async_remote_copy(src_ref=src, dst_ref=dst, send_sem=send_ref.at[w * per + i],
                                                  recv_sem=recv_ref.at[w * per + i], device_id=target,
                                                  device_id_type=MESH)
                cp.wait_send()
                cp.wait_recv()

    thru = [pltpu.HBM(a.shape, a.dtype) for a in list(srcs) + list(lands)]
    out = pl.pallas_call(
        body, name=name, in_specs=[HBM] * (ns + n) + [SEM, SEM] + [ANY] * len(afters),
        out_shape=tuple(thru), out_specs=tuple([HBM] * (ns + n)),
        input_output_aliases={i: i for i in range(ns + n)},
        compiler_params=pltpu.CompilerParams(has_side_effects=DATAFLOW),
    )(*srcs, *lands, send, recv, *afters)
    return list(out[:ns]), list(out[ns:])


def _all_reduce_small(part, name, deps=()):
    def body(*refs):
        p_ref = refs[0]
        o_ref, rbuf, send1, recv1, send2, recv2 = refs[-6:]
        place = _mesh_place()
        me = _index(place)
        rbuf[pl.ds(me, 1)] = p_ref[pl.ds(me, 1)]
        first = [pltpu.make_async_remote_copy(src_ref=p_ref.at[_index(_peer(place, k))], dst_ref=rbuf.at[me],
                                              send_sem=send1.at[k - 1], recv_sem=recv1.at[k - 1],
                                              device_id=_peer(place, k), device_id_type=MESH)
                 for k in range(1, N_DEV)]
        for cp in first:
            cp.start()
        for cp in first:
            cp.wait()
        acc = rbuf[0]
        for d in range(1, N_DEV):
            acc = acc + rbuf[d]
        o_ref[pl.ds(me, 1)] = acc[None]
        second = [pltpu.make_async_remote_copy(src_ref=o_ref.at[me], dst_ref=o_ref.at[me], send_sem=send2.at[k - 1],
                                               recv_sem=recv2.at[k - 1], device_id=_peer(place, k),
                                               device_id_type=MESH)
                  for k in range(1, N_DEV)]
        for cp in second:
            cp.start()
        for cp in second:
            cp.wait()

    vm = pl.BlockSpec(memory_space=pltpu.VMEM)
    ops = [part, *deps]
    return pl.pallas_call(
        body, in_specs=[vm] + [ANY] * len(deps), out_specs=vm, out_shape=S(part.shape, F32),
        scratch_shapes=[pltpu.VMEM(part.shape, F32)] + [pltpu.SemaphoreType.DMA((N_DEV - 1,))] * 4,
        compiler_params=pltpu.CompilerParams(has_side_effects=True, vmem_limit_bytes=VMEM_LIMIT), name=name)(*ops)


def _adamw(w, g, m, v):
    m = ADAM_B1 * m + (1.0 - ADAM_B1) * g
    v = ADAM_B2 * v + (1.0 - ADAM_B2) * (g * g)
    m_hat = m / (1.0 - ADAM_B1 ** ADAM_STEP)
    v_hat = v / (1.0 - ADAM_B2 ** ADAM_STEP)
    return -ADAM_LR * (m_hat / (jnp.sqrt(v_hat) + ADAM_EPS) + ADAM_WD * w), m, v


def _adam_big(parts, own, me, w, m, v, name):
    nl, a, b = w.shape
    ta = _shard_rows_tile(a)

    def body(me_ref, p_ref, *refs):
        own_refs, (w_ref, m_ref, v_ref, g_ref, d_ref, mo_ref, vo_ref) = refs[:nl], refs[nl:]
        layer = pl.program_id(0)
        mine = own_refs[0][...]
        for k in range(1, nl):
            mine = jnp.where(layer == k, own_refs[k][...], mine)
        g = None
        for s in range(N_DEV):
            term = jnp.where(me_ref[0] == s, mine, p_ref[s]).astype(F32)
            g = term if g is None else g + term
        g_ref[...] = g
        d_ref[...], mo_ref[...], vo_ref[...] = _adamw(w_ref[...], g, m_ref[...], v_ref[...])

    blk = pl.BlockSpec((None, ta, b), lambda l, i, me_ref: (l, i, 0))

    def own_spec(k):
        return pl.BlockSpec((None, ta, b), lambda l, i, me_ref: (me_ref[0], jnp.where(l == k, i, 0), 0))

    grid_spec = pltpu.PrefetchScalarGridSpec(
        num_scalar_prefetch=1, grid=(nl, a // ta),
        in_specs=[pl.BlockSpec((N_DEV, None, ta, b), lambda l, i, me_ref: (0, l, i, 0))]
        + [own_spec(k) for k in range(nl)] + [blk, blk, blk],
        out_specs=[blk] * 4)
    return pl.pallas_call(body, grid_spec=grid_spec, out_shape=[S(w.shape, F32)] * 4,
                          compiler_params=_cp("arbitrary", "arbitrary"), name=name)(me, parts, *own, w, m, v)


def _adam_small(gs, ws, ms, vs):
    n = len(gs)

    def body(*refs):
        ins, outs = refs[:4 * n], refs[4 * n:]
        for i in range(n):
            g_ref, w_ref, m_ref, v_ref = (ins[k * n + i] for k in range(4))
            outs[i][...], outs[n + i][...], outs[2 * n + i][...] = _adamw(w_ref[...], g_ref[...], m_ref[...], v_ref[...])

    out = pl.pallas_call(body, out_shape=[S(g.shape, F32) for g in gs] * 3, compiler_params=_cp(),
                         name="adam_small")(*gs, *ws, *ms, *vs)
    return out[:n], out[n:2 * n], out[2 * n:]


BIG = ("even_w_in", "even_w_out", "mla_w_down", "mla_w_qb", "mla_w_kvb", "mla_w_o", "mlp_w1", "mlp_w2")
BIG_KEY = dict(even_w_in="win", even_w_out="wout", mla_w_down="wdown", mla_w_qb="wqb", mla_w_kvb="wkvb",
               mla_w_o="wo", mlp_w1="w1", mlp_w2="w2")
SMALL = (("ln_mix_g", "ln_mix_g", None), ("ln_mix_b", "ln_mix_b", None), ("ln_ffn_g", "ln_ffn_g", None),
         ("ln_ffn_b", "ln_ffn_b", None), ("pool_w", "pool_w", None), ("pool_scale", "pool_scale", None),
         ("lru_conv_w", "conv_w", 2), ("lru_conv_b", "conv_b", None), ("lru_w_a", "w_a", None),
         ("lru_b_a", "b_a", None), ("lru_w_x", "w_x", None), ("lru_b_x", "b_x", None), ("lru_lambda", "lam", None),
         ("mla_q_norm_g", "gq", 1), ("mla_kv_norm_g", "gkv", 1))
WEIGHTS = ("ln_mix_g", "ln_mix_b", "ln_ffn_g", "ln_ffn_b", "even_w_in", "pool_w", "pool_scale", "lru_conv_w",
           "lru_conv_b", "lru_w_a", "lru_b_a", "lru_w_x", "lru_b_x", "lru_lambda", "even_w_out", "mla_w_down",
           "mla_q_norm_g", "mla_kv_norm_g", "mla_w_qb", "mla_w_kvb", "mla_w_o", "mlp_w1", "mlp_w2")
ALL_AXES = ("x", "y", "c")


def _layer_weights(l):
    j = l // 2
    if l % 2 == 0:
        mixer = [("win", "even_w_in", j), ("wout", "even_w_out", j)]
    else:
        mixer = [("wdown", "mla_w_down", j), ("wqb", "mla_w_qb", j), ("wkvb", "mla_w_kvb", j), ("wo", "mla_w_o", j)]
    return mixer + [("w1", "mlp_w1", l), ("w2", "mlp_w2", l)]


def _pack(arrays, multiple):
    flat = jnp.concatenate([a.reshape(-1) for a in arrays])
    pad = (-flat.shape[0]) % multiple
    return jnp.pad(flat, (0, pad))


def _unpack(flat, shapes):
    out, at = [], 0
    for shp in shapes:
        n = 1
        for s in shp:
            n *= s
        out.append(flat[at:at + n].reshape(shp))
        at += n
    return out


def _global_shape(local_shape, axis):
    if axis is None:
        return tuple(local_shape)
    return tuple(s * N_DEV if i == axis else s for i, s in enumerate(local_shape))


def _step(x, positions, tgt, w, m, v):
    t = x.shape[1]
    me = _index(_mesh_place())

    sharded = [(name, axis) for name, _, axis in SMALL if axis is not None]
    zeros_with_mine = [lax.dynamic_update_slice_in_dim(jnp.zeros(_global_shape(w[name].shape, axis), F32), w[name],
                                                       me * w[name].shape[axis], axis) for name, axis in sharded]
    chunk = N_DEV * 8 * 128
    gathered = _all_reduce_small(_pack(zeros_with_mine, chunk).reshape(N_DEV, -1, 128), "gather_small")
    full = dict(zip([name for name, _ in sharded],
                    _unpack(gathered.reshape(-1), [_global_shape(w[name].shape, axis) for name, axis in sharded])))

    def keys_of(l, part):
        keys = [key for key, _, _ in _layer_weights(l)]
        if l == 0:
            return keys[:1] if part == 0 else keys[1:]
        return keys if part == 0 else []

    shard_of = {(l, key): (w[name][i].T if key == "win" else w[name][i])
                for l in range(DEPTH) for key, name, i in _layer_weights(l)}
    me_arr = me.astype(jnp.int32).reshape(1)
    first = _all_gather_big(_make_zones([shard_of[0, key] for key in keys_of(0, 0)], me_arr, "zones_0_0"))
    flights, after = {}, (first[0], gathered)
    for l in range(DEPTH):
        for part in (0, 1):
            if (l, part) != (0, 0) and keys_of(l, part):
                zones = _make_zones([shard_of[l, key] for key in keys_of(l, part)], me_arr, "zones_%d_%d" % (l, part))
                send, recv, _, lands, token = _exchange_start([], zones, GATHER_ICI, "gather_start_%d_%d" % (l, part),
                                                              after=after)
                flights[l, part] = (send, recv, [], lands)
                after = (token,)

    passing = {}

    def pass_on(l, part, after):
        tag = "%d_%d" % (l, part)
        _, lands = _exchange_wait(*flights[l, part], GATHER_ICI, after, "gather_wait_" + tag)
        send, recv, _, lands, token = _exchange_start([], lands, GATHER_D2D, "gather_pass_" + tag)
        passing[l, part] = (send, recv, [], lands)
        return token

    def early_pass(l, after):
        return pass_on(l, 0, after) if l >= 2 else None

    def weights_of(l, part, after):
        keys = keys_of(l, part)
        if (l, part) == (0, 0):
            arrays = first
        elif keys:
            if (l, part) not in passing:
                pass_on(l, part, after)
            _, arrays = _exchange_wait(*passing[l, part], GATHER_D2D, after, "gather_pass_wait_%d_%d" % (l, part))
        big = dict(zip(keys, arrays)) if keys else {}
        if "win" in big:
            big["win_t"] = big["win"].reshape(EVEN_IN, D)
        if "wout" in big:
            big["wout2d"] = big["wout"].reshape(EVEN_MIX, D)
        if "wdown" in big:
            big["wdown2d"] = big["wdown"].reshape(D, ODD_IN)
        return big

    zone = {name: lax.empty((N_DEV,) + w[name].shape, BF16) for name in BIG}
    name_of = {key: name for name, key in BIG_KEY.items()}
    sent, last_token = [], [None]

    def grads_done(l, grads):
        keys = list(grads)
        index = {key: i for key, _, i in _layer_weights(l)}
        layers = [index[key] for key in keys]
        send, recv, srcs, lands, tok = _exchange_start([grads[k] for k in keys], [zone[name_of[k]] for k in keys],
                                                       _scatter_plan(layers), "scatter_start_%d_%s" % (l, keys[0]))
        for k, land in zip(keys, lands):
            zone[name_of[k]] = land
        sent.append((send, recv, srcs, keys, layers))
        last_token[0] = tok
        return tok

    row3 = lambda a: a.reshape(a.shape[0], 1, a.shape[1])
    small = dict(ln_mix_g=row3(w["ln_mix_g"]), ln_mix_b=row3(w["ln_mix_b"]), ln_ffn_g=row3(w["ln_ffn_g"]),
                 ln_ffn_b=row3(w["ln_ffn_b"]), pool_w=w["pool_w"], pool_scale=row3(w["pool_scale"]),
                 conv_w=full["lru_conv_w"], conv_b=row3(w["lru_conv_b"]), w_a=w["lru_w_a"], b_a=row3(w["lru_b_a"]),
                 w_x=w["lru_w_x"], b_x=row3(w["lru_b_x"]), lam=row3(w["lru_lambda"]),
                 gq=row3(full["mla_q_norm_g"]), gkv=row3(full["mla_kv_norm_g"]))

    loss_part, grad_x, g = _local_step(x[0], positions.reshape(t, 1), tgt[0], small, weights_of, grads_done,
                                       start_dep=token, prefetch=early_pass)

    own = {name: [None] * w[name].shape[0] for name in BIG}
    me_arr = me.astype(jnp.int32).reshape(1)
    out = {}
    local_g = [jnp.stack(g[key]).reshape(_global_shape(w[name].shape, axis)) for name, key, axis in SMALL]
    local_g.append(loss_part.reshape(1))
    part = _pack(local_g, chunk).reshape(N_DEV, -1, 128)
    small_plan = _scatter_plan([None])
    s_send, s_recv, s_src, s_land, after = _exchange_start([part], [lax.empty(part.shape, F32)], small_plan,
                                                           "small_scatter_start", after=(last_token[0],))
    for n_flight, (send, recv, srcs, keys, layers) in enumerate(sent):
        if n_flight == len(sent) - 1:
            for name in BIG:
                if BIG_KEY[name] not in keys:
                    out[name] = _adam_big(zone[name], own[name], me_arr, w[name], m[name], v[name], "adam_" + name)
            s_src, s_land = _exchange_wait(s_send, s_recv, s_src, s_land, small_plan,
                                           [grad_x] + [o[0] for o in out.values()], "small_scatter_wait")
            chunk_sum = _sum_blocks(s_land[0], s_src[0], me_arr)
            r_zone = lax.dynamic_update_slice_in_dim(lax.empty(part.shape, F32), chunk_sum[None], me, 0)
            r_send, r_recv, _, r_land, after = _exchange_start([], [r_zone], GATHER_ALL, "small_gather_start")
        srcs, lands = _exchange_wait(send, recv, srcs, [zone[name_of[k]] for k in keys], _scatter_plan(layers),
                                     after, "scatter_wait_%d" % n_flight)
        for k, land, src, layer in zip(keys, lands, srcs, layers):
            zone[name_of[k]] = land
            own[name_of[k]][layer] = src
        after = lands[0]
    for name in BIG:
        if name not in out:
            out[name] = _adam_big(zone[name], own[name], me_arr, w[name], m[name], v[name], "adam_" + name)

    _, reduced = _exchange_wait(r_send, r_recv, [], r_land, GATHER_ALL, [out[name][0] for name in BIG],
                                "small_gather_wait")
    reduced = _unpack(reduced[0].reshape(-1), [a.shape for a in local_g])
    loss = reduced[-1][0]
    mine = [a if axis is None else lax.dynamic_slice_in_dim(a, me * w[name].shape[axis], w[name].shape[axis], axis)
            for a, (name, _, axis) in zip(reduced, SMALL)]
    names = [name for name, _, _ in SMALL]
    as_2d = lambda a: a.reshape(-1, a.shape[-1])
    new = _adam_small([as_2d(a) for a in mine], *([as_2d(src[name]) for name in names] for src in (w, m, v)))
    for i, name in enumerate(names):
        out[name] = (mine[i],) + tuple(part[i].reshape(w[name].shape) for part in new)

    return (loss, grad_x[None]) + tuple(out[name][i] for i in range(4) for name in WEIGHTS)


def kernel(x, positions, ln_mix_g, ln_mix_b, ln_ffn_g, ln_ffn_b, even_w_in, pool_w, pool_scale, lru_conv_w, lru_conv_b, lru_w_a, lru_b_a, lru_w_x, lru_b_x, lru_lambda, even_w_out, mla_w_down, mla_q_norm_g, mla_kv_norm_g, mla_w_qb, mla_w_kvb, mla_w_o, mlp_w1, mlp_w2, loss_target, m_ln_mix_g, m_ln_mix_b, m_ln_ffn_g, m_ln_ffn_b, m_even_w_in, m_pool_w, m_pool_scale, m_lru_conv_w, m_lru_conv_b, m_lru_w_a, m_lru_b_a, m_lru_w_x, m_lru_b_x, m_lru_lambda, m_even_w_out, m_mla_w_down, m_mla_q_norm_g, m_mla_kv_norm_g, m_mla_w_qb, m_mla_w_kvb, m_mla_w_o, m_mlp_w1, m_mlp_w2, v_ln_mix_g, v_ln_mix_b, v_ln_ffn_g, v_ln_ffn_b, v_even_w_in, v_pool_w, v_pool_scale, v_lru_conv_w, v_lru_conv_b, v_lru_w_a, v_lru_b_a, v_lru_w_x, v_lru_b_x, v_lru_lambda, v_even_w_out, v_mla_w_down, v_mla_q_norm_g, v_mla_kv_norm_g, v_mla_w_qb, v_mla_w_kvb, v_mla_w_o, v_mlp_w1, v_mlp_w2):
    w = dict(zip(WEIGHTS, (ln_mix_g, ln_mix_b, ln_ffn_g, ln_ffn_b, even_w_in, pool_w, pool_scale, lru_conv_w,
                           lru_conv_b, lru_w_a, lru_b_a, lru_w_x, lru_b_x, lru_lambda, even_w_out, mla_w_down,
                           mla_q_norm_g, mla_kv_norm_g, mla_w_qb, mla_w_kvb, mla_w_o, mlp_w1, mlp_w2)))
    m = dict(zip(WEIGHTS, (m_ln_mix_g, m_ln_mix_b, m_ln_ffn_g, m_ln_ffn_b, m_even_w_in, m_pool_w, m_pool_scale,
                           m_lru_conv_w, m_lru_conv_b, m_lru_w_a, m_lru_b_a, m_lru_w_x, m_lru_b_x, m_lru_lambda,
                           m_even_w_out, m_mla_w_down, m_mla_q_norm_g, m_mla_kv_norm_g, m_mla_w_qb, m_mla_w_kvb,
                           m_mla_w_o, m_mlp_w1, m_mlp_w2)))
    v = dict(zip(WEIGHTS, (v_ln_mix_g, v_ln_mix_b, v_ln_ffn_g, v_ln_ffn_b, v_even_w_in, v_pool_w, v_pool_scale,
                           v_lru_conv_w, v_lru_conv_b, v_lru_w_a, v_lru_b_a, v_lru_w_x, v_lru_b_x, v_lru_lambda,
                           v_even_w_out, v_mla_w_down, v_mla_q_norm_g, v_mla_kv_norm_g, v_mla_w_qb, v_mla_w_kvb,
                           v_mla_w_o, v_mlp_w1, v_mlp_w2)))
    return _step(x, positions, loss_target, w, m, v)
```

```python
import functools

import jax
import jax.numpy as jnp
from jax import lax
from jax.experimental import pallas as pl
from jax.experimental.pallas import tpu as pltpu

F32 = jnp.float32
BF16 = jnp.bfloat16
S = jax.ShapeDtypeStruct

D = 1024
DEPTH = 4
N_DEV = 8
CHUNK_SHIFT = 6
POOL_WINDOWS = (2, 4, 8, 16)
POOL_W = 512
LRU_W = 1024
LRU_HEADS = 8
HEAD = 128
LRU_C = 8.0
EVEN_IN = 2560
EVEN_MIX = 1536
MLA_HEADS = 8
NOPE = 128
ROPE = 64
VDIM = 128
Q_RANK = 384
KV_RANK = 256
ODD_IN = 704
D_FF = 4096
FF_BLK = D_FF // N_DEV
ROPE_THETA = 10000.0
ALPHA = (2 * DEPTH) ** 0.25
LN_EPS = 1e-5
RMS_EPS = 1e-6
ATT_SCALE = (NOPE + ROPE) ** -0.5
NEG = float(jnp.finfo(jnp.float32).min)
ADAM_LR = 0.001
ADAM_B1 = 0.9
ADAM_B2 = 0.999
ADAM_EPS = 1e-08
ADAM_WD = 0.01
ADAM_STEP = 10
V7X_VMEM_BYTES = 64 * 1024 * 1024
VMEM_LIMIT = V7X_VMEM_BYTES - 8 * 1024 * 1024
MESH = pl.DeviceIdType.MESH


def _cp(*sem):
    return pltpu.CompilerParams(dimension_semantics=sem if sem else None, vmem_limit_bytes=VMEM_LIMIT)


def _dot(a, b):
    return jnp.dot(a, b, preferred_element_type=F32)


def _dot_nt(a, b):
    return lax.dot_general(a, b, (((1,), (1,)), ((), ())), preferred_element_type=F32)


def _dot_tn(a, b):
    return lax.dot_general(a, b, (((0,), (0,)), ((), ())), preferred_element_type=F32)


def _full(shape):
    return pl.BlockSpec(shape, lambda *_: (0,) * len(shape))


def _mm(a, b, *, mode, grid, a_spec, b_spec, out_shape, out_spec, name, add=None, add_spec=None, add_scale=1.0,
        dep=None):
    dot = {"nn": _dot, "nt": _dot_nt, "tn": _dot_tn}[mode]

    def body(*refs):
        a_ref, b_ref, o_ref = refs[0], refs[1], refs[-1]
        m = o_ref.shape[0]
        halves = [slice(0, m // 2), slice(m // 2, m)] if m % 256 == 0 else [slice(0, m)]
        for rows in halves:
            a_part = a_ref[:, rows] if mode == "tn" else a_ref[rows, :]
            acc = dot(a_part.astype(BF16), b_ref[...].astype(BF16))
            if add is not None:
                acc = acc + add_scale * refs[2][rows, :]
            o_ref[rows, :] = acc.astype(o_ref.dtype)

    ops = [a, b] if add is None else [a, b, add]
    specs = [a_spec, b_spec] if add is None else [a_spec, b_spec, add_spec]
    if dep is not None:
        ops.append(dep)
        specs.append(pl.BlockSpec(memory_space=pl.ANY))
    return pl.pallas_call(body, grid=grid, in_specs=specs, out_specs=out_spec, out_shape=out_shape,
                          compiler_params=_cp(*(("parallel",) * len(grid))), name=name)(*ops)


def _ln_stats(z):
    mu = jnp.mean(z, axis=-1, keepdims=True)
    zc = z - mu
    var = jnp.mean(zc * zc, axis=-1, keepdims=True)
    rstd = lax.rsqrt(var + LN_EPS)
    return zc * rstd, rstd


def _row_tile(t):
    return min(1024, t)


def _resid_ln(x, mix, g3, b3, l, name):
    t = x.shape[0]
    bm = _row_tile(t)

    def body(x_ref, m_ref, g_ref, b_ref, z_ref, y_ref, yb_ref):
        z = ALPHA * x_ref[...] + m_ref[...]
        xh, _ = _ln_stats(z)
        y = xh * g_ref[...] + b_ref[...]
        z_ref[...] = z
        y_ref[...] = y
        yb_ref[...] = y.astype(BF16)

    row = pl.BlockSpec((bm, D), lambda i: (i, 0))
    vec = pl.BlockSpec((None, 1, D), lambda i: (l, 0, 0))
    return pl.pallas_call(body, grid=(t // bm,), in_specs=[row, row, vec, vec], out_specs=[row, row, row],
                          out_shape=[S((t, D), F32), S((t, D), F32), S((t, D), BF16)],
                          compiler_params=_cp("parallel"), name=name)(x, mix, g3, b3)


def _proj_resid_ln(x, a, wmat, g3, b3, l, name):
    t, k = a.shape
    bm = _row_tile(t)

    def body(x_ref, a_ref, w_ref, g_ref, b_ref, z_ref, y_ref, yb_ref):
        z = ALPHA * x_ref[...] + _dot(a_ref[...], w_ref[...])
        xh, _ = _ln_stats(z)
        y = xh * g_ref[...] + b_ref[...]
        z_ref[...] = z
        y_ref[...] = y
        yb_ref[...] = y.astype(BF16)

    row = pl.BlockSpec((bm, D), lambda i: (i, 0))
    vec = pl.BlockSpec((None, 1, D), lambda i: (l, 0, 0))
    return pl.pallas_call(body, grid=(t // bm,),
                          in_specs=[row, pl.BlockSpec((bm, k), lambda i: (i, 0)), _full((k, D)), vec, vec],
                          out_specs=[row, row, row], out_shape=[S((t, D), F32), S((t, D), F32), S((t, D), BF16)],
                          compiler_params=_cp("parallel"), name=name)(x, a, wmat, g3, b3)


def _ln_bwd(d, z, g3, l, name, r=None, dep=None):
    t = z.shape[0]
    bm = _row_tile(t)

    def body(*refs):
        refs = list(refs)
        d_ref = refs.pop(0)
        dy = d_ref[...]
        if r is not None:
            dy = dy + ALPHA * refs.pop(0)[...]
        z_ref, g_ref = refs.pop(0), refs.pop(0)
        if dep is not None:
            refs.pop(0)
        dz_ref, dzb_ref, dg_ref, db_ref = refs
        xh, rstd = _ln_stats(z_ref[...])
        dyg = dy * g_ref[...]
        m1 = jnp.mean(dyg, axis=-1, keepdims=True)
        m2 = jnp.mean(dyg * xh, axis=-1, keepdims=True)
        dz = rstd * (dyg - m1 - xh * m2)
        dz_ref[...] = dz
        dzb_ref[...] = dz.astype(BF16)

        @pl.when(pl.program_id(0) == 0)
        def _():
            dg_ref[...] = jnp.zeros_like(dg_ref)
            db_ref[...] = jnp.zeros_like(db_ref)

        dg_ref[...] += jnp.sum(dy * xh, axis=0, keepdims=True)
        db_ref[...] += jnp.sum(dy, axis=0, keepdims=True)

    row = pl.BlockSpec((bm, D), lambda i: (i, 0))
    vec = pl.BlockSpec((None, 1, D), lambda i: (l, 0, 0))
    acc = pl.BlockSpec((1, D), lambda i: (0, 0))
    ops = [d, z, g3] if r is None else [d, r, z, g3]
    specs = [row, row, vec] if r is None else [row, row, row, vec]
    if dep is not None:
        ops.append(dep)
        specs.append(_full(dep.shape))
    return pl.pallas_call(body, grid=(t // bm,), in_specs=specs, out_specs=[row, row, acc, acc],
                          out_shape=[S((t, D), F32), S((t, D), BF16), S((1, D), F32), S((1, D), F32)],
                          compiler_params=_cp("arbitrary"), name=name)(*ops)


def _loss_grad(y, tgt):
    t = y.shape[0]
    bm = _row_tile(t)

    def body(y_ref, t_ref, dy_ref, loss_ref, acc_ref):
        i = pl.program_id(0)
        e = y_ref[...] - t_ref[...]
        dy_ref[...] = e * (1.0 / D)

        @pl.when(i == 0)
        def _():
            acc_ref[...] = jnp.zeros_like(acc_ref)

        acc_ref[...] += jnp.sum(e * e, axis=0, keepdims=True)

        @pl.when(i == pl.num_programs(0) - 1)
        def _():
            loss_ref[...] = jnp.full(loss_ref.shape, (0.5 / D) * jnp.sum(acc_ref[...]), F32)

    row = pl.BlockSpec((bm, D), lambda i: (i, 0))
    return pl.pallas_call(body, grid=(t // bm,), in_specs=[row, row],
                          out_specs=[row, pl.BlockSpec((1, 128), lambda i: (0, 0))],
                          out_shape=[S((t, D), F32), S((1, 128), F32)],
                          scratch_shapes=[pltpu.VMEM((1, D), F32)],
                          compiler_params=_cp("arbitrary"), name="loss_grad")(y, tgt)


def _mlp_row_tile(t):
    return min(1024, t)


MLP_ROW_PARTS = 2


def _row_parts(bm):
    step = bm // MLP_ROW_PARTS
    return [slice(k * step, (k + 1) * step) for k in range(MLP_ROW_PARTS)]


def _mlp_fwd(y, yb, w1g, w2g, g3, b3, l, dep=None):
    t = yb.shape[0]
    bm = _mlp_row_tile(t)

    def body(*refs):
        y_ref, yb_ref, w1_ref, w2_ref, g_ref, b_ref = refs[:6]
        z_ref, o_ref, ob_ref, act_ref, acc_ref = refs[-5:]
        j = pl.program_id(1)

        @pl.when(j == 0)
        def _():
            acc_ref[...] = jnp.zeros_like(acc_ref)

        for rows in _row_parts(bm):
            h = jnp.maximum(_dot(yb_ref[rows, :], w1_ref[...]), 0.0)
            act = (h * h).astype(BF16)
            act_ref[rows, :] = act
            acc_ref[rows, :] += _dot(act, w2_ref[...])

        @pl.when(j == N_DEV - 1)
        def _():
            z = ALPHA * y_ref[...] + acc_ref[...]
            xh, _ = _ln_stats(z)
            out = xh * g_ref[...] + b_ref[...]
            z_ref[...] = z
            o_ref[...] = out
            ob_ref[...] = out.astype(BF16)

    row = pl.BlockSpec((bm, D), lambda i, j: (i, 0))
    vec = pl.BlockSpec((None, 1, D), lambda i, j: (l, 0, 0))
    deps = [] if dep is None else [dep]
    return pl.pallas_call(
        body, grid=(t // bm, N_DEV),
        in_specs=[row, row, pl.BlockSpec((None, D, FF_BLK), lambda i, j: (j, 0, 0)),
                  pl.BlockSpec((None, FF_BLK, D), lambda i, j: (j, 0, 0)), vec, vec] + [ANY] * len(deps),
        out_specs=[row, row, row, pl.BlockSpec((bm, FF_BLK), lambda i, j: (i, j))],
        out_shape=[S((t, D), F32), S((t, D), F32), S((t, D), BF16), S((t, D_FF), BF16)],
        scratch_shapes=[pltpu.VMEM((bm, D), F32)],
        compiler_params=_cp("parallel", "arbitrary"), name="mlp_fwd")(y, yb, w1g, w2g, g3, b3, *deps)


def _mlp_bwd_dh(act, dzb, w1g, w2g):
    t = act.shape[0]
    bm = _mlp_row_tile(t)

    def body(a_ref, dz_ref, w1_ref, w2_ref, dh_ref, acc_ref):
        @pl.when(pl.program_id(1) == 0)
        def _():
            acc_ref[...] = jnp.zeros_like(acc_ref)

        for rows in _row_parts(bm):
            r = jnp.sqrt(a_ref[rows, :].astype(F32))
            dh = (_dot_nt(dz_ref[rows, :], w2_ref[...]) * (2.0 * r)).astype(BF16)
            dh_ref[rows, :] = dh
            acc_ref[rows, :] += _dot_nt(dh, w1_ref[...])

    row = pl.BlockSpec((bm, D), lambda i, j: (i, 0))
    hid = pl.BlockSpec((bm, FF_BLK), lambda i, j: (i, j))
    return pl.pallas_call(
        body, grid=(t // bm, N_DEV),
        in_specs=[hid, row,
                  pl.BlockSpec((None, D, FF_BLK), lambda i, j: (j, 0, 0)),
                  pl.BlockSpec((None, FF_BLK, D), lambda i, j: (j, 0, 0))],
        out_specs=[hid, row],
        out_shape=[S((t, D_FF), BF16), S((t, D), F32)],
        compiler_params=_cp("parallel", "arbitrary"), name="mlp_bwd_dh")(act, dzb, w1g, w2g)


F32_SUBLANES = 8


def _shift_dn(x, k, rows, fill=0.0):
    if k % F32_SUBLANES == 0:
        return jnp.concatenate([jnp.full((k,) + x.shape[1:], fill, x.dtype), x[:x.shape[0] - k]], axis=0)
    return jnp.where(rows >= k, pltpu.roll(x, k, 0), fill)


def _shift_up(x, k, rows, fill=0.0):
    t = x.shape[0]
    if k % F32_SUBLANES == 0:
        return jnp.concatenate([x[k:], jnp.full((k,) + x.shape[1:], fill, x.dtype)], axis=0)
    return jnp.where(rows < t - k, pltpu.roll(x, t - k, 0), fill)


def _scan_rows(a, b, shift):
    rows = lax.broadcasted_iota(jnp.int32, a.shape, 0)
    k = 1
    t = a.shape[0]
    while k < t:
        b = a * shift(b, k, rows) + b
        if 2 * k < t:
            a = a * shift(a, k, rows, 1.0)
        k *= 2
    return b


def _scan_dn(a, b):
    return _scan_rows(a, b, _shift_dn)


def _scan_up(a, b):
    return _scan_rows(a, b, _shift_up)


def _window_sum_dn(x, w, rows):
    k = 1
    while k < w:
        x = x + _shift_dn(x, k, rows)
        k *= 2
    return x


def _window_sum_up(x, w, rows):
    k = 1
    while k < w:
        x = x + _shift_up(x, k, rows)
        k *= 2
    return x


def _pool_diff(u, w, rows):
    inv_count = 1.0 / jnp.minimum(rows + 1, w).astype(F32)
    return _window_sum_dn(u, w, rows) * inv_count - u, inv_count


def _pool_fwd(proj, pool_w, pool_scale3, j):
    t = proj.shape[0]

    def body(u_ref, w_ref, s_ref, y_ref):
        rows = lax.broadcasted_iota(jnp.int32, (t, HEAD), 0)
        for g, w in enumerate(POOL_WINDOWS):
            cols = slice(g * HEAD, (g + 1) * HEAD)
            d, _ = _pool_diff(u_ref[:, cols], w, rows)
            y = _dot(d.astype(BF16), w_ref[g].astype(BF16)) * s_ref[:, cols]
            y_ref[:, cols] = y.astype(BF16)

    return pl.pallas_call(
        body, grid=(1,),
        in_specs=[pl.BlockSpec((t, POOL_W), lambda i: (0, 0)),
                  pl.BlockSpec((None, 4, HEAD, HEAD), lambda i: (j, 0, 0, 0)),
                  pl.BlockSpec((None, 1, POOL_W), lambda i: (j, 0, 0))],
        out_specs=pl.BlockSpec((t, POOL_W), lambda i: (0, 0)),
        out_shape=S((t, POOL_W), BF16), compiler_params=_cp("arbitrary"), name="pool_fwd")(proj, pool_w, pool_scale3)


def _pool_bwd(proj, dycat, pool_w, pool_scale3, j):
    t = proj.shape[0]

    def body(u_ref, dy_ref, w_ref, s_ref, du_ref, dw_ref, ds_ref):
        rows = lax.broadcasted_iota(jnp.int32, (t, HEAD), 0)
        for g, w in enumerate(POOL_WINDOWS):
            cols = slice(g * HEAD, (g + 1) * HEAD)
            d, inv_count = _pool_diff(u_ref[:, cols], w, rows)
            db = d.astype(BF16)
            wg = w_ref[g].astype(BF16)
            dy = dy_ref[:, cols]
            ds_ref[:, cols] = jnp.sum(dy * _dot(db, wg), axis=0, keepdims=True)
            dzz = (dy * s_ref[:, cols]).astype(BF16)
            dw_ref[g] = _dot_tn(db, dzz)
            dd = _dot_nt(dzz, wg)
            du_ref[:, cols] = (_window_sum_up(dd * inv_count, w, rows) - dd).astype(BF16)

    return pl.pallas_call(
        body, grid=(1,),
        in_specs=[pl.BlockSpec((t, POOL_W), lambda i: (0, 0)),
                  pl.BlockSpec((t, POOL_W), lambda i: (0, 0)),
                  pl.BlockSpec((None, 4, HEAD, HEAD), lambda i: (j, 0, 0, 0)),
                  pl.BlockSpec((None, 1, POOL_W), lambda i: (j, 0, 0))],
        out_specs=[pl.BlockSpec((t, POOL_W), lambda i: (0, 0)), _full((4, HEAD, HEAD)), _full((1, POOL_W))],
        out_shape=[S((t, POOL_W), BF16), S((4, HEAD, HEAD), F32), S((1, POOL_W), F32)],
        compiler_params=_cp("arbitrary"), name="pool_bwd")(proj, dycat, pool_w, pool_scale3)


GELU_C = 0.7978845608028654
GELU_K = 0.044715


def _gelu(x):
    th = jnp.tanh(GELU_C * (x + GELU_K * x * x * x))
    return 0.5 * x * (1.0 + th), th


def _lru_forward(u, gate, cw, cb, wa, ba, wx, bx, lam, rows):
    v = cw[3:4] * u + cw[2:3] * _shift_dn(u, 1, rows) + cw[1:2] * _shift_dn(u, 2, rows) \
        + cw[0:1] * _shift_dn(u, 3, rows) + cb
    vb = v.astype(BF16)
    r = jax.nn.sigmoid(_dot(vb, wa) + ba)
    i = jax.nn.sigmoid(_dot(vb, wx) + bx)
    sp = jnp.maximum(-lam, 0.0) + jnp.log1p(jnp.exp(-jnp.abs(lam)))
    log_a = (-LRU_C) * r * sp
    a = jnp.exp(log_a)
    one_m_a2 = -jnp.tanh(log_a) * (a * a + 1.0)
    mult = jnp.sqrt(one_m_a2)
    h = _scan_dn(a, mult * (i * v))
    gl, th = _gelu(gate)
    return dict(v=v, vb=vb, r=r, i=i, sp=sp, a=a, mult=mult, h=h, gl=gl, th=th)


def _lru_specs(t, j, col0_u, col0_g):
    blk = lambda c0: pl.BlockSpec((t, HEAD), lambda h: (0, c0 + h))
    vec = pl.BlockSpec((None, 1, HEAD), lambda h: (j, 0, h))
    return [blk(col0_u), blk(col0_g),
            pl.BlockSpec((None, 4, HEAD), lambda h: (j, 0, h)), vec,
            pl.BlockSpec((None, None, HEAD, HEAD), lambda h: (j, h, 0, 0)), vec,
            pl.BlockSpec((None, None, HEAD, HEAD), lambda h: (j, h, 0, 0)), vec, vec]


def _lru_fwd(proj, p, j):
    t = proj.shape[0]

    def body(u_ref, g_ref, cw_ref, cb_ref, wa_ref, ba_ref, wx_ref, bx_ref, lam_ref, y_ref):
        rows = lax.broadcasted_iota(jnp.int32, (t, HEAD), 0)
        f = _lru_forward(u_ref[...], g_ref[...], cw_ref[...], cb_ref[...], wa_ref[...].astype(BF16), ba_ref[...],
                         wx_ref[...].astype(BF16), bx_ref[...], lam_ref[...], rows)
        y_ref[...] = (f["h"] * f["gl"]).astype(BF16)

    return pl.pallas_call(
        body, grid=(LRU_HEADS,), in_specs=_lru_specs(t, j, POOL_W // HEAD, (POOL_W + LRU_W) // HEAD),
        out_specs=pl.BlockSpec((t, HEAD), lambda h: (0, h)), out_shape=S((t, LRU_W), BF16),
        compiler_params=_cp("parallel"), name="lru_fwd")(
            proj, proj, p["conv_w"], p["conv_b"], p["w_a"], p["b_a"], p["w_x"], p["b_x"], p["lam"])


def _lru_bwd(proj, dycat, p, j):
    t = proj.shape[0]

    def body(u_ref, g_ref, cw_ref, cb_ref, wa_ref, ba_ref, wx_ref, bx_ref, lam_ref, dy_ref,
             du_ref, dgate_ref, dcw_ref, dcb_ref, dwa_ref, dba_ref, dwx_ref, dbx_ref, dlam_ref):
        rows = lax.broadcasted_iota(jnp.int32, (t, HEAD), 0)
        u = u_ref[...]
        gate = g_ref[...]
        cw = cw_ref[...]
        wa = wa_ref[...].astype(BF16)
        wx = wx_ref[...].astype(BF16)
        lam = lam_ref[...]
        f = _lru_forward(u, gate, cw, cb_ref[...], wa, ba_ref[...], wx, bx_ref[...], lam, rows)
        v, r, i, a, mult, h, th = f["v"], f["r"], f["i"], f["a"], f["mult"], f["h"], f["th"]
        dy = dy_ref[...]
        dgl = 0.5 * (1.0 + th) + 0.5 * gate * (1.0 - th * th) * GELU_C * (1.0 + 3.0 * GELU_K * gate * gate)
        dgate_ref[...] = (dy * h * dgl).astype(BF16)
        g = _scan_up(_shift_up(a, 1, rows), dy * f["gl"])
        da = g * _shift_dn(h, 1, rows)
        iv = i * v
        dmult = g * iv
        di = g * mult * v
        dv = g * mult * i
        dlog_a = da * a - dmult * (a * a) / mult
        dr = dlog_a * (-LRU_C) * f["sp"]
        dsp = jnp.sum(dlog_a * (-LRU_C) * r, axis=0, keepdims=True)
        dlam_ref[...] = -dsp * jax.nn.sigmoid(-lam)
        dpa = dr * r * (1.0 - r)
        dpx = di * i * (1.0 - i)
        dpab = dpa.astype(BF16)
        dpxb = dpx.astype(BF16)
        dwa_ref[...] = _dot_tn(f["vb"], dpab)
        dwx_ref[...] = _dot_tn(f["vb"], dpxb)
        dba_ref[...] = jnp.sum(dpa, axis=0, keepdims=True)
        dbx_ref[...] = jnp.sum(dpx, axis=0, keepdims=True)
        dv = dv + _dot_nt(dpab, wa) + _dot_nt(dpxb, wx)
        dcb_ref[...] = jnp.sum(dv, axis=0, keepdims=True)
        du = cw[3:4] * dv
        dcw_ref[3:4, :] = jnp.sum(dv * u, axis=0, keepdims=True)
        for k in (1, 2, 3):
            du = du + cw[3 - k:4 - k] * _shift_up(dv, k, rows)
            dcw_ref[3 - k:4 - k, :] = jnp.sum(dv * _shift_dn(u, k, rows), axis=0, keepdims=True)
        du_ref[...] = du.astype(BF16)

    blk = pl.BlockSpec((t, HEAD), lambda h: (0, h))
    vec = pl.BlockSpec((1, HEAD), lambda h: (0, h))
    mat = pl.BlockSpec((None, HEAD, HEAD), lambda h: (h, 0, 0))
    return pl.pallas_call(
        body, grid=(LRU_HEADS,),
        in_specs=_lru_specs(t, j, POOL_W // HEAD, (POOL_W + LRU_W) // HEAD)
        + [pl.BlockSpec((t, HEAD), lambda h: (0, POOL_W // HEAD + h))],
        out_specs=[blk, blk, pl.BlockSpec((4, HEAD), lambda h: (0, h)), vec, mat, vec, mat, vec, vec],
        out_shape=[S((t, LRU_W), BF16), S((t, LRU_W), BF16), S((4, LRU_W), F32), S((1, LRU_W), F32),
                   S((LRU_HEADS, HEAD, HEAD), F32), S((1, LRU_W), F32),
                   S((LRU_HEADS, HEAD, HEAD), F32), S((1, LRU_W), F32), S((1, LRU_W), F32)],
        compiler_params=_cp("parallel"), name="lru_bwd")(
            proj, proj, p["conv_w"], p["conv_b"], p["w_a"], p["b_a"], p["w_x"], p["b_x"], p["lam"], dycat)


def _rope(x, c, s):
    x1 = x[:, :ROPE // 2]
    x2 = x[:, ROPE // 2:]
    return jnp.concatenate([x1 * c - x2 * s, x1 * s + x2 * c], axis=-1)


def _rope_t(d, c, s):
    d1 = d[:, :ROPE // 2]
    d2 = d[:, ROPE // 2:]
    return jnp.concatenate([d1 * c + d2 * s, d2 * c - d1 * s], axis=-1)


def _rope_tables(pos2, inv_freq):
    t = pos2.shape[0]

    def body(p_ref, f_ref, c_ref, s_ref):
        ang = p_ref[...].astype(F32) * f_ref[...]
        c_ref[...] = jnp.cos(ang)
        s_ref[...] = jnp.sin(ang)

    return pl.pallas_call(body, out_shape=[S((t, ROPE // 2), F32), S((t, ROPE // 2), F32)],
                          name="rope_tables")(pos2, inv_freq)


def _down_norm(xb, wdown_g, gq3, gkv3, cos, sin, j):
    t = xb.shape[0]
    bm = _row_tile(t)

    def body(x_ref, w_ref, gq_ref, gkv_ref, c_ref, s_ref, down_ref, cq_ref, ckv_ref, kpe_ref):
        w = w_ref[...].reshape(D, ODD_IN)
        down = _dot(x_ref[...], w)
        down_ref[...] = down
        q = down[:, :Q_RANK]
        cq_ref[...] = (q * lax.rsqrt(jnp.mean(q * q, axis=-1, keepdims=True) + RMS_EPS) * gq_ref[...]).astype(BF16)
        kv = down[:, Q_RANK:Q_RANK + KV_RANK]
        ckv_ref[...] = (kv * lax.rsqrt(jnp.mean(kv * kv, axis=-1, keepdims=True) + RMS_EPS)
                        * gkv_ref[...]).astype(BF16)
        kpe_ref[...] = _rope(down[:, Q_RANK + KV_RANK:], c_ref[...], s_ref[...])

    row = lambda n: pl.BlockSpec((bm, n), lambda i: (i, 0))
    return pl.pallas_call(
        body, grid=(t // bm,),
        in_specs=[row(D), _full((N_DEV, D // N_DEV, ODD_IN)),
                  pl.BlockSpec((None, 1, Q_RANK), lambda i: (j, 0, 0)),
                  pl.BlockSpec((None, 1, KV_RANK), lambda i: (j, 0, 0)), row(ROPE // 2), row(ROPE // 2)],
        out_specs=[row(ODD_IN), row(Q_RANK), row(KV_RANK), row(ROPE)],
        out_shape=[S((t, ODD_IN), F32), S((t, Q_RANK), BF16), S((t, KV_RANK), BF16), S((t, ROPE), F32)],
        compiler_params=_cp("parallel"), name="down_norm")(xb, wdown_g, gq3, gkv3, cos, sin)


def _q_tile(t, widest):
    return min(widest, t // 2)


def _attn_probs(q, k, qs):
    s = _dot_nt(q, k) * ATT_SCALE
    tq = q.shape[0]
    rows = lax.broadcasted_iota(jnp.int32, (tq, tq), 0)
    cols = lax.broadcasted_iota(jnp.int32, (tq, tq), 1)
    last = jnp.where(jnp.right_shift(cols, CHUNK_SHIFT) <= jnp.right_shift(rows, CHUNK_SHIFT), s[:, qs:], NEG)
    s = last if qs == 0 else jnp.concatenate([s[:, :qs], last], axis=1)
    e = jnp.exp(s - jnp.max(s, axis=-1, keepdims=True))
    return e / jnp.sum(e, axis=-1, keepdims=True)


def _head_qkv(cq, ckv, kpe, c, s, wq_ref, wkv_ref):
    q = jnp.concatenate([_dot(cq, wq_ref[:, :NOPE]), _rope(_dot(cq, wq_ref[:, NOPE:]), c, s)], axis=1).astype(BF16)
    k = jnp.concatenate([_dot(ckv, wkv_ref[:, :NOPE]), kpe], axis=1).astype(BF16)
    vv = _dot(ckv, wkv_ref[:, NOPE:]).astype(BF16)
    return q, k, vv


def _attn_in_specs(t):
    return [_full((t, Q_RANK)), _full((t, KV_RANK)), _full((t, ROPE)), _full((t, ROPE // 2)), _full((t, ROPE // 2)),
            pl.BlockSpec((None, Q_RANK, NOPE + ROPE), lambda h: (h, 0, 0)),
            pl.BlockSpec((None, KV_RANK, NOPE + VDIM), lambda h: (h, 0, 0)),
            pl.BlockSpec((None, VDIM, D), lambda h: (h, 0, 0))]


def _attn_fwd(cq, ckv, kpe, cos, sin, wqb_g, wkvb_g, wo_g):
    t = cq.shape[0]
    tq = _q_tile(t, 256)

    def body(cq_ref, ckv_ref, kpe_ref, c_ref, s_ref, wq_ref, wkv_ref, wo_ref, o_ref, mix_ref):
        q, k, vv = _head_qkv(cq_ref[...], ckv_ref[...], kpe_ref[...], c_ref[...], s_ref[...], wq_ref, wkv_ref)
        for qs in range(0, t, tq):
            ke = qs + tq
            p = _attn_probs(q[qs:ke], k[:ke], qs)
            o_ref[qs:ke, :] = _dot(p.astype(BF16), vv[:ke]).astype(BF16)
        c = _dot(o_ref[...], wo_ref[...])

        @pl.when(pl.program_id(0) == 0)
        def _():
            mix_ref[...] = c

        @pl.when(pl.program_id(0) > 0)
        def _():
            mix_ref[...] += c

    return pl.pallas_call(
        body, grid=(MLA_HEADS,), in_specs=_attn_in_specs(t),
        out_specs=[pl.BlockSpec((None, t, VDIM), lambda h: (h, 0, 0)), _full((t, D))],
        out_shape=[S((MLA_HEADS, t, VDIM), BF16), S((t, D), F32)],
        compiler_params=_cp("arbitrary"), name="attn_fwd")(cq, ckv, kpe, cos, sin, wqb_g, wkvb_g, wo_g)


def _attn_bwd(cq, ckv, kpe, cos, sin, wqb_g, wkvb_g, wo_g, o, dzb):
    t = cq.shape[0]
    tq = _q_tile(t, 256)

    def body(cq_ref, ckv_ref, kpe_ref, c_ref, s_ref, wq_ref, wkv_ref, wo_ref, o_ref, dz_ref,
             dwo_ref, dwq_ref, dwkv_ref, dcq_ref, dckv_ref, dkpe_ref, dkt_s, dvt_s, dq_s):
        cqv = cq_ref[...]
        ckvv = ckv_ref[...]
        c = c_ref[...]
        s = s_ref[...]
        q, k, vv = _head_qkv(cqv, ckvv, kpe_ref[...], c, s, wq_ref, wkv_ref)
        dzv = dz_ref[...]
        dwo_ref[...] = _dot_tn(o_ref[...], dzv).astype(BF16)
        do = _dot_nt(dzv, wo_ref[...]).astype(BF16)
        dkt_s[...] = jnp.zeros_like(dkt_s)
        dvt_s[...] = jnp.zeros_like(dvt_s)
        for qs in range(0, t, tq):
            ke = qs + tq
            p = _attn_probs(q[qs:ke], k[:ke], qs)
            dp = _dot_nt(do[qs:ke], vv[:ke])
            ds = (p * (dp - jnp.sum(p * dp, axis=-1, keepdims=True)) * ATT_SCALE).astype(BF16)
            dq_s[qs:ke, :] = _dot(ds, k[:ke])
            dkt_s[0:NOPE + ROPE, 0:ke] += _dot_tn(q[qs:ke], ds)
            dvt_s[:, 0:ke] += _dot_tn(do[qs:ke], p.astype(BF16))
        dk = dkt_s[...].T
        dqn = dq_s[:, :NOPE].astype(BF16)
        dqp = _rope_t(dq_s[:, NOPE:], c, s).astype(BF16)
        dkn = dk[:, :NOPE].astype(BF16)
        dkp = dk[:, NOPE:NOPE + ROPE]
        dvv = dvt_s[...].T.astype(BF16)
        dwq_ref[:, :NOPE] = _dot_tn(cqv, dqn).astype(BF16)
        dwq_ref[:, NOPE:] = _dot_tn(cqv, dqp).astype(BF16)
        dwkv_ref[:, :NOPE] = _dot_tn(ckvv, dkn).astype(BF16)
        dwkv_ref[:, NOPE:] = _dot_tn(ckvv, dvv).astype(BF16)
        dcq = _dot_nt(dqn, wq_ref[:, :NOPE]) + _dot_nt(dqp, wq_ref[:, NOPE:])
        dckv = _dot_nt(dkn, wkv_ref[:, :NOPE]) + _dot_nt(dvv, wkv_ref[:, NOPE:])

        @pl.when(pl.program_id(0) == 0)
        def _():
            dcq_ref[...] = dcq
            dckv_ref[...] = dckv
            dkpe_ref[...] = dkp

        @pl.when(pl.program_id(0) > 0)
        def _():
            dcq_ref[...] += dcq
            dckv_ref[...] += dckv
            dkpe_ref[...] += dkp

    per_head = lambda a, b: pl.BlockSpec((None, a, b), lambda h: (h, 0, 0))
    return pl.pallas_call(
        body, grid=(MLA_HEADS,),
        in_specs=_attn_in_specs(t) + [per_head(t, VDIM), _full((t, D))],
        out_specs=[per_head(VDIM, D), per_head(Q_RANK, NOPE + ROPE), per_head(KV_RANK, NOPE + VDIM),
                   _full((t, Q_RANK)), _full((t, KV_RANK)), _full((t, ROPE))],
        out_shape=[S((MLA_HEADS, VDIM, D), BF16), S((MLA_HEADS, Q_RANK, NOPE + ROPE), BF16),
                   S((MLA_HEADS, KV_RANK, NOPE + VDIM), BF16),
                   S((t, Q_RANK), F32), S((t, KV_RANK), F32), S((t, ROPE), F32)],
        scratch_shapes=[pltpu.VMEM((2 * NOPE, t), F32), pltpu.VMEM((VDIM, t), F32),
                        pltpu.VMEM((t, NOPE + ROPE), F32)],
        compiler_params=_cp("arbitrary"), name="attn_bwd")(cq, ckv, kpe, cos, sin, wqb_g, wkvb_g, wo_g, o, dzb)


def _rms_bwd(down, dcq, dckv, dkpe, cos, sin, gq3, gkv3, j):
    t = down.shape[0]
    bm = _row_tile(t)

    def body(down_ref, dcq_ref, dckv_ref, dkpe_ref, c_ref, s_ref, gq_ref, gkv_ref, dd_ref, dgq_ref, dgkv_ref):
        @pl.when(pl.program_id(0) == 0)
        def _():
            dgq_ref[...] = jnp.zeros_like(dgq_ref)
            dgkv_ref[...] = jnp.zeros_like(dgkv_ref)

        def rms_b(x, dy, g):
            rstd = lax.rsqrt(jnp.mean(x * x, axis=-1, keepdims=True) + RMS_EPS)
            xh = x * rstd
            dyg = dy * g
            return rstd * (dyg - xh * jnp.mean(dyg * xh, axis=-1, keepdims=True)), jnp.sum(dy * xh, axis=0, keepdims=True)

        dq, dgq = rms_b(down_ref[:, :Q_RANK], dcq_ref[...], gq_ref[...])
        dkv, dgkv = rms_b(down_ref[:, Q_RANK:Q_RANK + KV_RANK], dckv_ref[...], gkv_ref[...])
        dgq_ref[...] += dgq
        dgkv_ref[...] += dgkv
        dd_ref[:, :Q_RANK] = dq.astype(BF16)
        dd_ref[:, Q_RANK:Q_RANK + KV_RANK] = dkv.astype(BF16)
        dd_ref[:, Q_RANK + KV_RANK:] = _rope_t(dkpe_ref[...], c_ref[...], s_ref[...]).astype(BF16)

    row = lambda n: pl.BlockSpec((bm, n), lambda i: (i, 0))
    return pl.pallas_call(
        body, grid=(t // bm,),
        in_specs=[row(ODD_IN), row(Q_RANK), row(KV_RANK), row(ROPE), row(ROPE // 2), row(ROPE // 2),
                  pl.BlockSpec((None, 1, Q_RANK), lambda i: (j, 0, 0)),
                  pl.BlockSpec((None, 1, KV_RANK), lambda i: (j, 0, 0))],
        out_specs=[row(ODD_IN), _full((1, Q_RANK)), _full((1, KV_RANK))],
        out_shape=[S((t, ODD_IN), BF16), S((1, Q_RANK), F32), S((1, KV_RANK), F32)],
        compiler_params=_cp("arbitrary"), name="rms_bwd")(down, dcq, dckv, dkpe, cos, sin, gq3, gkv3)


def _col_blocks(t, n, bn):
    return pl.BlockSpec((t, bn), lambda i: (0, i))


def _row_blocks(n, bm):
    return pl.BlockSpec((bm, n), lambda i: (i, 0))


def _local_step(x, pos2, tgt, small, weights_of, grads_done, start_dep=None, prefetch=None):
    t = x.shape[0]
    bm = min(512, t)
    inv_freq = (ROPE_THETA ** (-jnp.arange(0, ROPE, 2, dtype=F32) / ROPE)).reshape(1, ROPE // 2)
    cos, sin = _rope_tables(pos2, inv_freq)
    lru_p = {k: small[k] for k in ("conv_w", "conv_b", "w_a", "b_a", "w_x", "b_x", "lam")}

    saved = []
    y, yb = x, x.astype(BF16)
    for l in range(DEPTH):
        j = l // 2
        big = weights_of(l, 0, y)
        sv = dict(xb=yb, big=big)
        if l % 2 == 0:
            proj = _mm(yb, big["win_t"], mode="nt", grid=(EVEN_IN // 512,), a_spec=_full((t, D)),
                       b_spec=_row_blocks(D, 512), out_shape=S((t, EVEN_IN), F32),
                       out_spec=_col_blocks(t, EVEN_IN, 512), name="even_proj", dep=start_dep if l == 0 else None)
            ycat = jnp.concatenate([_pool_fwd(proj, small["pool_w"], small["pool_scale"], j),
                                    _lru_fwd(proj, lru_p, j)], axis=1)
            big.update(weights_of(l, 1, ycat))
            z1, y1, y1b = _proj_resid_ln(y, ycat, big["wout2d"], small["ln_mix_g"], small["ln_mix_b"], l, "even_out")
            sv.update(proj=proj, ycat=ycat)
        else:
            down, cq, ckv, kpe = _down_norm(yb, big["wdown"], small["gq"], small["gkv"], cos, sin, j)
            o, mix = _attn_fwd(cq, ckv, kpe, cos, sin, big["wqb"], big["wkvb"], big["wo"])
            z1, y1, y1b = _resid_ln(y, mix, small["ln_mix_g"], small["ln_mix_b"], l, "resid_ln")
            sv.update(down=down, cq=cq, ckv=ckv, kpe=kpe, o=o)
        fetched = prefetch(l + 1, y1) if prefetch is not None and l + 1 < DEPTH else None
        z2, y, yb, act = _mlp_fwd(y1, y1b, big["w1"], big["w2"], small["ln_ffn_g"], small["ln_ffn_b"], l,
                                  dep=fetched)
        sv.update(z1=z1, y1b=y1b, z2=z2, act=act)
        saved.append(sv)

    dy, loss_tile = _loss_grad(y, tgt)

    g = {k: [None] * n for k, n in (("ln_mix_g", 4), ("ln_mix_b", 4), ("ln_ffn_g", 4), ("ln_ffn_b", 4),
                                    ("pool_w", 2), ("pool_scale", 2), ("conv_w", 2), ("conv_b", 2),
                                    ("w_a", 2), ("b_a", 2), ("w_x", 2), ("b_x", 2), ("lam", 2),
                                    ("gq", 2), ("gkv", 2))}
    dep = None
    for l in reversed(range(DEPTH)):
        j = l // 2
        sv = saved[l]
        big = sv["big"]
        dz2, dz2b, g["ln_ffn_g"][l], g["ln_ffn_b"][l] = _ln_bwd(dy, sv["z2"], small["ln_ffn_g"], l, "ln_bwd", dep=dep)
        act = sv["act"]
        dh, dff = _mlp_bwd_dh(act, dz2b, big["w1"], big["w2"])
        dw1 = _mm(sv["y1b"], dh, mode="tn", grid=(N_DEV,), a_spec=_full((t, D)),
                  b_spec=_col_blocks(t, D_FF, FF_BLK), out_shape=S((N_DEV, D, FF_BLK), BF16),
                  out_spec=pl.BlockSpec((None, D, FF_BLK), lambda i: (i, 0, 0)), name="mlp_dw1")
        dw2 = _mm(act, dz2b, mode="tn", grid=(N_DEV,), a_spec=_col_blocks(t, D_FF, FF_BLK),
                  b_spec=_full((t, D)), out_shape=S((N_DEV, FF_BLK, D), BF16),
                  out_spec=pl.BlockSpec((None, FF_BLK, D), lambda i: (i, 0, 0)), name="mlp_dw2")
        dep = grads_done(l, dict(w1=dw1, w2=dw2))
        dz1, dz1b, g["ln_mix_g"][l], g["ln_mix_b"][l] = _ln_bwd(dff, sv["z1"], small["ln_mix_g"], l, "ln_bwd_res",
                                                                 r=dz2, dep=dep)
        if l % 2 == 0:
            wout = big["wout2d"]
            dycat = _mm(dz1b, wout, mode="nt", grid=(EVEN_MIX // 512,), a_spec=_full((t, D)),
                        b_spec=_row_blocks(D, 512), out_shape=S((t, EVEN_MIX), F32),
                        out_spec=_col_blocks(t, EVEN_MIX, 512), name="even_dycat")
            dwout = _mm(sv["ycat"], dz1b, mode="tn", grid=(EVEN_MIX // 512,), a_spec=_col_blocks(t, EVEN_MIX, 512),
                        b_spec=_full((t, D)), out_shape=S((EVEN_MIX, D), BF16), out_spec=_row_blocks(D, 512),
                        name="even_dwout")
            du_pool, g["pool_w"][j], g["pool_scale"][j] = _pool_bwd(sv["proj"], dycat, small["pool_w"],
                                                                   small["pool_scale"], j)
            (du_lru, du_gate, g["conv_w"][j], g["conv_b"][j], g["w_a"][j], g["b_a"][j], g["w_x"][j], g["b_x"][j],
             g["lam"][j]) = _lru_bwd(sv["proj"], dycat, lru_p, j)
            dproj = jnp.concatenate([du_pool, du_lru, du_gate], axis=1)
            dwin = _mm(sv["xb"], dproj, mode="tn", grid=(EVEN_IN // 512,), a_spec=_full((t, D)),
                       b_spec=_col_blocks(t, EVEN_IN, 512), out_shape=S((D, EVEN_IN), BF16),
                       out_spec=_col_blocks(D, EVEN_IN, 512), name="even_dwin")
            dep = grads_done(l, dict(win=dwin.reshape(D, N_DEV, EVEN_IN // N_DEV).transpose(1, 0, 2),
                                     wout=dwout.reshape(N_DEV, EVEN_MIX // N_DEV, D)))
            dy = _mm(dproj, big["win_t"], mode="nn", grid=(t // bm,), a_spec=_row_blocks(EVEN_IN, bm),
                     b_spec=_full((EVEN_IN, D)), out_shape=S((t, D), F32), out_spec=_row_blocks(D, bm),
                     add=dz1, add_spec=_row_blocks(D, bm), add_scale=ALPHA, name="even_dx", dep=dep)
        else:
            dwo, dwqb, dwkvb, dcq, dckv, dkpe = _attn_bwd(
                sv["cq"], sv["ckv"], sv["kpe"], cos, sin, big["wqb"], big["wkvb"], big["wo"], sv["o"], dz1b)
            ddown, g["gq"][j], g["gkv"][j] = _rms_bwd(sv["down"], dcq, dckv, dkpe, cos, sin, small["gq"],
                                                     small["gkv"], j)
            dwdown = _mm(sv["xb"], ddown, mode="tn", grid=(N_DEV,), a_spec=_col_blocks(t, D, D // N_DEV),
                         b_spec=_full((t, ODD_IN)), out_shape=S((N_DEV, D // N_DEV, ODD_IN), BF16),
                         out_spec=pl.BlockSpec((None, D // N_DEV, ODD_IN), lambda i: (i, 0, 0)),
                         name="odd_dwdown")
            dep = grads_done(l, dict(wdown=dwdown, wqb=dwqb, wkvb=dwkvb, wo=dwo))
            dy = _mm(ddown, big["wdown2d"], mode="nt", grid=(t // bm,), a_spec=_row_blocks(ODD_IN, bm),
                     b_spec=_full((D, ODD_IN)), out_shape=S((t, D), F32), out_spec=_row_blocks(D, bm),
                     add=dz1, add_spec=_row_blocks(D, bm), add_scale=ALPHA, name="odd_dx", dep=dep)
    return loss_tile[0, 0], dy, g


def _mesh_place():
    x, y, c = lax.axis_index("x"), lax.axis_index("y"), lax.axis_index("c")
    return x, y, c


def _peer(place, k):
    x, y, c = place
    return (1 - x if k & 4 else x, 1 - y if k & 2 else y, 1 - c if k & 1 else c)


def _index(place):
    x, y, c = place
    return 4 * x + 2 * y + c


ANY = pl.BlockSpec(memory_space=pl.ANY)


def _make_zones(shards, me, name):
    n = len(shards)

    def body(me_ref, *refs):
        for src, dst in zip(refs[:n], refs[n:]):
            dst[...] = src[...].astype(BF16)

    grid_spec = pltpu.PrefetchScalarGridSpec(
        num_scalar_prefetch=1, grid=(1,),
        in_specs=[pl.BlockSpec(s.shape, lambda i, me_ref: (0, 0)) for s in shards],
        out_specs=[pl.BlockSpec((None,) + s.shape, lambda i, me_ref: (me_ref[0], 0, 0)) for s in shards])
    return pl.pallas_call(body, grid_spec=grid_spec, out_shape=[S((N_DEV,) + s.shape, BF16) for s in shards],
                          compiler_params=_cp("arbitrary"), name=name)(me, *shards)


def _all_gather_big(zones):
    n = len(zones)

    def body(*refs):
        outs = refs[n:2 * n]
        send, recv = refs[2 * n:]
        x, y, c = _mesh_place()
        me, sibling = (x, y, c), (x, y, 1 - c)
        chips = [(1 - x, y), (x, 1 - y), (1 - x, 1 - y)]

        def copy(w, k, block, to):
            blk = outs[w].at[_index(block)]
            return pltpu.make_async_remote_copy(src_ref=blk, dst_ref=blk, send_sem=send.at[w, k], recv_sem=recv.at[w, k],
                                                device_id=to, device_id_type=MESH)

        first = []
        for w in range(n):
            first.append(copy(w, 0, me, sibling))
            first += [copy(w, 1 + j, me, (*chip, c)) for j, chip in enumerate(chips)]
        for cp in first:
            cp.start()
        passed = []
        for w in range(n):
            for j, chip in enumerate(chips):
                copy(w, 1 + j, (*chip, c), me).wait_recv()
                cp = copy(w, 4 + j, (*chip, c), sibling)
                cp.start()
                passed.append(cp)
        for w in range(n):
            copy(w, 0, sibling, me).wait_recv()
            for j, chip in enumerate(chips):
                copy(w, 4 + j, (*chip, 1 - c), me).wait_recv()
        for cp in first + passed:
            cp.wait_send()

    return pl.pallas_call(
        body, in_specs=[ANY] * n, out_specs=[ANY] * n, out_shape=[S(z.shape, z.dtype) for z in zones],
        input_output_aliases={i: i for i in range(n)},
        scratch_shapes=[pltpu.SemaphoreType.DMA((n, N_DEV - 1)), pltpu.SemaphoreType.DMA((n, N_DEV - 1))],
        compiler_params=pltpu.CompilerParams(has_side_effects=True), name="all_gather_big")(*zones)


def _shard_rows_tile(a):
    return max(d for d in range(16, 257, 16) if a % d == 0)


HBM = pl.BlockSpec(memory_space=pltpu.HBM)
SEM = pl.BlockSpec(memory_space=pltpu.SEMAPHORE)
DATAFLOW = pltpu.SideEffectType.DATAFLOW_SIDE_EFFECTING


def _in_hbm(a):
    return pltpu.with_memory_space_constraint(a, pltpu.HBM)


def _gather_ici_copies(place, src, land, w):
    me = _index(place)
    return [(_peer(place, k), land.at[me], land.at[me]) for k in (1, 2, 4, 6)]


def _gather_d2d_copies(place, src, land, w):
    blocks = [_index(_peer(place, k)) for k in (2, 4, 6)]
    return [(_peer(place, 1), land.at[b], land.at[b]) for b in blocks]


GATHER_ICI = (4, _gather_ici_copies)
GATHER_D2D = (3, _gather_d2d_copies)


def _scatter_plan(layers):
    def copies(place, src, land, w):
        me = _index(place)
        mine = land.at[me] if layers[w] is None else land.at[me, layers[w]]
        return [(_peer(place, k), src.at[_index(_peer(place, k))], mine) for k in range(1, N_DEV)]
    return (N_DEV - 1, copies)


def _gather_all_copies(place, src, land, w):
    me = _index(place)
    return [(_peer(place, k), land.at[me], land.at[me]) for k in range(1, N_DEV)]


GATHER_ALL = (N_DEV - 1, _gather_all_copies)


def _sum_blocks(zone, part, me):
    r = part.shape[1]

    def body(me_ref, z_ref, p_ref, o_ref):
        acc = None
        for s in range(N_DEV):
            term = jnp.where(me_ref[0] == s, p_ref[...], z_ref[s])
            acc = term if acc is None else acc + term
        o_ref[...] = acc

    grid_spec = pltpu.PrefetchScalarGridSpec(
        num_scalar_prefetch=1, grid=(1,),
        in_specs=[pl.BlockSpec((N_DEV, r, 128), lambda i, me_ref: (0, 0, 0)),
                  pl.BlockSpec((None, r, 128), lambda i, me_ref: (me_ref[0], 0, 0))],
        out_specs=pl.BlockSpec((r, 128), lambda i, me_ref: (0, 0)))
    return pl.pallas_call(body, grid_spec=grid_spec, out_shape=S((r, 128), F32),
                          compiler_params=_cp("arbitrary"), name="sum_small")(me, zone, part)


def _exchange_start(srcs, lands, plan, name, after=()):
    ns, n = len(srcs), len(lands)
    n_in = ns + n + len(after)
    per, copies = plan

    def body(*refs):
        ins, land = refs[:ns], refs[ns:ns + n]
        send, recv = refs[n_in], refs[n_in + 1]
        token = refs[-1]
        place = _mesh_place()
        for i in range(per):
            for w in range(n):
                target, src, dst = copies(place, ins[w] if ns else None, land[w], w)[i]
                pltpu.make_async_remote_copy(src_ref=src, dst_ref=dst, send_sem=send.at[w * per + i],
                                             recv_sem=recv.at[w * per + i], device_id=target, device_id_type=MESH).start()
        token[...] = jnp.zeros_like(token)

    sems = pltpu.SemaphoreType.DMA((n * per,))
    thru = [pltpu.HBM(a.shape, a.dtype) for a in list(srcs) + list(lands)]
    out = pl.pallas_call(
        body, name=name, in_specs=[HBM] * (ns + n) + [ANY] * len(after),
        out_shape=(sems, sems, *thru, S((8, 128), F32)),
        out_specs=(SEM, SEM, *([HBM] * (ns + n)), pl.BlockSpec(memory_space=pltpu.VMEM)),
        input_output_aliases={i: 2 + i for i in range(ns + n)},
        compiler_params=pltpu.CompilerParams(has_side_effects=DATAFLOW),
    )(*[_in_hbm(a) for a in list(srcs) + list(lands)], *after)
    return out[0], out[1], list(out[2:2 + ns]), list(out[2 + ns:2 + ns + n]), out[-1]


def _exchange_wait(send, recv, srcs, lands, plan, after, name):
    ns, n = len(srcs), len(lands)
    per, copies = plan
    afters = tuple(after) if isinstance(after, (tuple, list)) else (after,)

    def body(*refs):
        ins, land = refs[:ns], refs[ns:ns + n]
        send_ref, recv_ref = refs[ns + n], refs[ns + n + 1]
        place = _mesh_place()
        for i in range(per):
            for w in range(n):
                target, src, dst = copies(place, ins[w] if ns else None, land[w], w)[i]
                cp = pltpu.make_async_remote_copy(src_ref=src, dst_ref=dst, send_sem=send_ref.at[w * per + i],
                                                  recv_sem=recv_ref.at[w * per + i], device_id=target,
                                                  device_id_type=MESH)
                cp.wait_send()
                cp.wait_recv()

    thru = [pltpu.HBM(a.shape, a.dtype) for a in list(srcs) + list(lands)]
    out = pl.pallas_call(
        body, name=name, in_specs=[HBM] * (ns + n) + [SEM, SEM] + [ANY] * len(afters),
        out_shape=tuple(thru), out_specs=tuple([HBM] * (ns + n)),
        input_output_aliases={i: i for i in range(ns + n)},
        compiler_params=pltpu.CompilerParams(has_side_effects=DATAFLOW),
    )(*srcs, *lands, send, recv, *afters)
    return list(out[:ns]), list(out[ns:])


def _all_reduce_small(part, name, deps=()):
    def body(*refs):
        p_ref = refs[0]
        o_ref, rbuf, send1, recv1, send2, recv2 = refs[-6:]
        place = _mesh_place()
        me = _index(place)
        rbuf[pl.ds(me, 1)] = p_ref[pl.ds(me, 1)]
        first = [pltpu.make_async_remote_copy(src_ref=p_ref.at[_index(_peer(place, k))], dst_ref=rbuf.at[me],
                                              send_sem=send1.at[k - 1], recv_sem=recv1.at[k - 1],
                                              device_id=_peer(place, k), device_id_type=MESH)
                 for k in range(1, N_DEV)]
        for cp in first:
            cp.start()
        for cp in first:
            cp.wait()
        acc = rbuf[0]
        for d in range(1, N_DEV):
            acc = acc + rbuf[d]
        o_ref[pl.ds(me, 1)] = acc[None]
        second = [pltpu.make_async_remote_copy(src_ref=o_ref.at[me], dst_ref=o_ref.at[me], send_sem=send2.at[k - 1],
                                               recv_sem=recv2.at[k - 1], device_id=_peer(place, k),
                                               device_id_type=MESH)
                  for k in range(1, N_DEV)]
        for cp in second:
            cp.start()
        for cp in second:
            cp.wait()

    vm = pl.BlockSpec(memory_space=pltpu.VMEM)
    ops = [part, *deps]
    return pl.pallas_call(
        body, in_specs=[vm] + [ANY] * len(deps), out_specs=vm, out_shape=S(part.shape, F32),
        scratch_shapes=[pltpu.VMEM(part.shape, F32)] + [pltpu.SemaphoreType.DMA((N_DEV - 1,))] * 4,
        compiler_params=pltpu.CompilerParams(has_side_effects=True, vmem_limit_bytes=VMEM_LIMIT), name=name)(*ops)


def _adamw(w, g, m, v):
    m = ADAM_B1 * m + (1.0 - ADAM_B1) * g
    v = ADAM_B2 * v + (1.0 - ADAM_B2) * (g * g)
    m_hat = m / (1.0 - ADAM_B1 ** ADAM_STEP)
    v_hat = v / (1.0 - ADAM_B2 ** ADAM_STEP)
    return -ADAM_LR * (m_hat / (jnp.sqrt(v_hat) + ADAM_EPS) + ADAM_WD * w), m, v


def _adam_big(parts, own, me, w, m, v, name):
    nl, a, b = w.shape
    ta = _shard_rows_tile(a)

    def body(me_ref, p_ref, *refs):
        own_refs, (w_ref, m_ref, v_ref, g_ref, d_ref, mo_ref, vo_ref) = refs[:nl], refs[nl:]
        layer = pl.program_id(0)
        mine = own_refs[0][...]
        for k in range(1, nl):
            mine = jnp.where(layer == k, own_refs[k][...], mine)
        g = None
        for s in range(N_DEV):
            term = jnp.where(me_ref[0] == s, mine, p_ref[s]).astype(F32)
            g = term if g is None else g + term
        g_ref[...] = g
        d_ref[...], mo_ref[...], vo_ref[...] = _adamw(w_ref[...], g, m_ref[...], v_ref[...])

    blk = pl.BlockSpec((None, ta, b), lambda l, i, me_ref: (l, i, 0))

    def own_spec(k):
        return pl.BlockSpec((None, ta, b), lambda l, i, me_ref: (me_ref[0], jnp.where(l == k, i, 0), 0))

    grid_spec = pltpu.PrefetchScalarGridSpec(
        num_scalar_prefetch=1, grid=(nl, a // ta),
        in_specs=[pl.BlockSpec((N_DEV, None, ta, b), lambda l, i, me_ref: (0, l, i, 0))]
        + [own_spec(k) for k in range(nl)] + [blk, blk, blk],
        out_specs=[blk] * 4)
    return pl.pallas_call(body, grid_spec=grid_spec, out_shape=[S(w.shape, F32)] * 4,
                          compiler_params=_cp("arbitrary", "arbitrary"), name=name)(me, parts, *own, w, m, v)


def _adam_small(gs, ws, ms, vs):
    n = len(gs)

    def body(*refs):
        ins, outs = refs[:4 * n], refs[4 * n:]
        for i in range(n):
            g_ref, w_ref, m_ref, v_ref = (ins[k * n + i] for k in range(4))
            outs[i][...], outs[n + i][...], outs[2 * n + i][...] = _adamw(w_ref[...], g_ref[...], m_ref[...], v_ref[...])

    out = pl.pallas_call(body, out_shape=[S(g.shape, F32) for g in gs] * 3, compiler_params=_cp(),
                         name="adam_small")(*gs, *ws, *ms, *vs)
    return out[:n], out[n:2 * n], out[2 * n:]


BIG = ("even_w_in", "even_w_out", "mla_w_down", "mla_w_qb", "mla_w_kvb", "mla_w_o", "mlp_w1", "mlp_w2")
BIG_KEY = dict(even_w_in="win", even_w_out="wout", mla_w_down="wdown", mla_w_qb="wqb", mla_w_kvb="wkvb",
               mla_w_o="wo", mlp_w1="w1", mlp_w2="w2")
SMALL = (("ln_mix_g", "ln_mix_g", None), ("ln_mix_b", "ln_mix_b", None), ("ln_ffn_g", "ln_ffn_g", None),
         ("ln_ffn_b", "ln_ffn_b", None), ("pool_w", "pool_w", None), ("pool_scale", "pool_scale", None),
         ("lru_conv_w", "conv_w", 2), ("lru_conv_b", "conv_b", None), ("lru_w_a", "w_a", None),
         ("lru_b_a", "b_a", None), ("lru_w_x", "w_x", None), ("lru_b_x", "b_x", None), ("lru_lambda", "lam", None),
         ("mla_q_norm_g", "gq", 1), ("mla_kv_norm_g", "gkv", 1))
WEIGHTS = ("ln_mix_g", "ln_mix_b", "ln_ffn_g", "ln_ffn_b", "even_w_in", "pool_w", "pool_scale", "lru_conv_w",
           "lru_conv_b", "lru_w_a", "lru_b_a", "lru_w_x", "lru_b_x", "lru_lambda", "even_w_out", "mla_w_down",
           "mla_q_norm_g", "mla_kv_norm_g", "mla_w_qb", "mla_w_kvb", "mla_w_o", "mlp_w1", "mlp_w2")
ALL_AXES = ("x", "y", "c")


def _layer_weights(l):
    j = l // 2
    if l % 2 == 0:
        mixer = [("win", "even_w_in", j), ("wout", "even_w_out", j)]
    else:
        mixer = [("wdown", "mla_w_down", j), ("wqb", "mla_w_qb", j), ("wkvb", "mla_w_kvb", j), ("wo", "mla_w_o", j)]
    return mixer + [("w1", "mlp_w1", l), ("w2", "mlp_w2", l)]


def _pack(arrays, multiple):
    flat = jnp.concatenate([a.reshape(-1) for a in arrays])
    pad = (-flat.shape[0]) % multiple
    return jnp.pad(flat, (0, pad))


def _unpack(flat, shapes):
    out, at = [], 0
    for shp in shapes:
        n = 1
        for s in shp:
            n *= s
        out.append(flat[at:at + n].reshape(shp))
        at += n
    return out


def _global_shape(local_shape, axis):
    if axis is None:
        return tuple(local_shape)
    return tuple(s * N_DEV if i == axis else s for i, s in enumerate(local_shape))


def _step(x, positions, tgt, w, m, v):
    t = x.shape[1]
    me = _index(_mesh_place())

    sharded = [(name, axis) for name, _, axis in SMALL if axis is not None]
    zeros_with_mine = [lax.dynamic_update_slice_in_dim(jnp.zeros(_global_shape(w[name].shape, axis), F32), w[name],
                                                       me * w[name].shape[axis], axis) for name, axis in sharded]
    chunk = N_DEV * 8 * 128
    gathered = _all_reduce_small(_pack(zeros_with_mine, chunk).reshape(N_DEV, -1, 128), "gather_small")
    full = dict(zip([name for name, _ in sharded],
                    _unpack(gathered.reshape(-1), [_global_shape(w[name].shape, axis) for name, axis in sharded])))

    def keys_of(l, part):
        keys = [key for key, _, _ in _layer_weights(l)]
        if l == 0:
            return keys[:1] if part == 0 else keys[1:]
        return keys if part == 0 else []

    shard_of = {(l, key): (w[name][i].T if key == "win" else w[name][i])
                for l in range(DEPTH) for key, name, i in _layer_weights(l)}
    me_arr = me.astype(jnp.int32).reshape(1)
    first = _all_gather_big(_make_zones([shard_of[0, key] for key in keys_of(0, 0)], me_arr, "zones_0_0"))
    flights, after = {}, (first[0], gathered)
    for l in range(DEPTH):
        for part in (0, 1):
            if (l, part) != (0, 0) and keys_of(l, part):
                zones = _make_zones([shard_of[l, key] for key in keys_of(l, part)], me_arr, "zones_%d_%d" % (l, part))
                send, recv, _, lands, token = _exchange_start([], zones, GATHER_ICI, "gather_start_%d_%d" % (l, part),
                                                              after=after)
                flights[l, part] = (send, recv, [], lands)
                after = (token,)

    passing = {}

    def pass_on(l, part, after):
        tag = "%d_%d" % (l, part)
        _, lands = _exchange_wait(*flights[l, part], GATHER_ICI, after, "gather_wait_" + tag)
        send, recv, _, lands, token = _exchange_start([], lands, GATHER_D2D, "gather_pass_" + tag)
        passing[l, part] = (send, recv, [], lands)
        return token

    def early_pass(l, after):
        return pass_on(l, 0, after) if l >= 2 else None

    def weights_of(l, part, after):
        keys = keys_of(l, part)
        if (l, part) == (0, 0):
            arrays = first
        elif keys:
            if (l, part) not in passing:
                pass_on(l, part, after)
            _, arrays = _exchange_wait(*passing[l, part], GATHER_D2D, after, "gather_pass_wait_%d_%d" % (l, part))
        big = dict(zip(keys, arrays)) if keys else {}
        if "win" in big:
            big["win_t"] = big["win"].reshape(EVEN_IN, D)
        if "wout" in big:
            big["wout2d"] = big["wout"].reshape(EVEN_MIX, D)
        if "wdown" in big:
            big["wdown2d"] = big["wdown"].reshape(D, ODD_IN)
        return big

    zone = {name: lax.empty((N_DEV,) + w[name].shape, BF16) for name in BIG}
    name_of = {key: name for name, key in BIG_KEY.items()}
    sent, last_token = [], [None]

    def grads_done(l, grads):
        keys = list(grads)
        index = {key: i for key, _, i in _layer_weights(l)}
        layers = [index[key] for key in keys]
        send, recv, srcs, lands, tok = _exchange_start([grads[k] for k in keys], [zone[name_of[k]] for k in keys],
                                                       _scatter_plan(layers), "scatter_start_%d_%s" % (l, keys[0]))
        for k, land in zip(keys, lands):
            zone[name_of[k]] = land
        sent.append((send, recv, srcs, keys, layers))
        last_token[0] = tok
        return tok

    row3 = lambda a: a.reshape(a.shape[0], 1, a.shape[1])
    small = dict(ln_mix_g=row3(w["ln_mix_g"]), ln_mix_b=row3(w["ln_mix_b"]), ln_ffn_g=row3(w["ln_ffn_g"]),
                 ln_ffn_b=row3(w["ln_ffn_b"]), pool_w=w["pool_w"], pool_scale=row3(w["pool_scale"]),
                 conv_w=full["lru_conv_w"], conv_b=row3(w["lru_conv_b"]), w_a=w["lru_w_a"], b_a=row3(w["lru_b_a"]),
                 w_x=w["lru_w_x"], b_x=row3(w["lru_b_x"]), lam=row3(w["lru_lambda"]),
                 gq=row3(full["mla_q_norm_g"]), gkv=row3(full["mla_kv_norm_g"]))

    loss_part, grad_x, g = _local_step(x[0], positions.reshape(t, 1), tgt[0], small, weights_of, grads_done,
                                       start_dep=token, prefetch=early_pass)

    own = {name: [None] * w[name].shape[0] for name in BIG}
    me_arr = me.astype(jnp.int32).reshape(1)
    out = {}
    local_g = [jnp.stack(g[key]).reshape(_global_shape(w[name].shape, axis)) for name, key, axis in SMALL]
    local_g.append(loss_part.reshape(1))
    part = _pack(local_g, chunk).reshape(N_DEV, -1, 128)
    small_plan = _scatter_plan([None])
    s_send, s_recv, s_src, s_land, after = _exchange_start([part], [lax.empty(part.shape, F32)], small_plan,
                                                           "small_scatter_start", after=(last_token[0],))
    for n_flight, (send, recv, srcs, keys, layers) in enumerate(sent):
        if n_flight == len(sent) - 1:
            for name in BIG:
                if BIG_KEY[name] not in keys:
                    out[name] = _adam_big(zone[name], own[name], me_arr, w[name], m[name], v[name], "adam_" + name)
            s_src, s_land = _exchange_wait(s_send, s_recv, s_src, s_land, small_plan,
                                           [grad_x] + [o[0] for o in out.values()], "small_scatter_wait")
            chunk_sum = _sum_blocks(s_land[0], s_src[0], me_arr)
            r_zone = lax.dynamic_update_slice_in_dim(lax.empty(part.shape, F32), chunk_sum[None], me, 0)
            r_send, r_recv, _, r_land, after = _exchange_start([], [r_zone], GATHER_ALL, "small_gather_start")
        srcs, lands = _exchange_wait(send, recv, srcs, [zone[name_of[k]] for k in keys], _scatter_plan(layers),
                                     after, "scatter_wait_%d" % n_flight)
        for k, land, src, layer in zip(keys, lands, srcs, layers):
            zone[name_of[k]] = land
            own[name_of[k]][layer] = src
        after = lands[0]
    for name in BIG:
        if name not in out:
            out[name] = _adam_big(zone[name], own[name], me_arr, w[name], m[name], v[name], "adam_" + name)

    _, reduced = _exchange_wait(r_send, r_recv, [], r_land, GATHER_ALL, [out[name][0] for name in BIG],
                                "small_gather_wait")
    reduced = _unpack(reduced[0].reshape(-1), [a.shape for a in local_g])
    loss = reduced[-1][0]
    mine = [a if axis is None else lax.dynamic_slice_in_dim(a, me * w[name].shape[axis], w[name].shape[axis], axis)
            for a, (name, _, axis) in zip(reduced, SMALL)]
    names = [name for name, _, _ in SMALL]
    as_2d = lambda a: a.reshape(-1, a.shape[-1])
    new = _adam_small([as_2d(a) for a in mine], *([as_2d(src[name]) for name in names] for src in (w, m, v)))
    for i, name in enumerate(names):
        out[name] = (mine[i],) + tuple(part[i].reshape(w[name].shape) for part in new)

    return (loss, grad_x[None]) + tuple(out[name][i] for i in range(4) for name in WEIGHTS)


def kernel(x, positions, ln_mix_g, ln_mix_b, ln_ffn_g, ln_ffn_b, even_w_in, pool_w, pool_scale, lru_conv_w, lru_conv_b, lru_w_a, lru_b_a, lru_w_x, lru_b_x, lru_lambda, even_w_out, mla_w_down, mla_q_norm_g, mla_kv_norm_g, mla_w_qb, mla_w_kvb, mla_w_o, mlp_w1, mlp_w2, loss_target, m_ln_mix_g, m_ln_mix_b, m_ln_ffn_g, m_ln_ffn_b, m_even_w_in, m_pool_w, m_pool_scale, m_lru_conv_w, m_lru_conv_b, m_lru_w_a, m_lru_b_a, m_lru_w_x, m_lru_b_x, m_lru_lambda, m_even_w_out, m_mla_w_down, m_mla_q_norm_g, m_mla_kv_norm_g, m_mla_w_qb, m_mla_w_kvb, m_mla_w_o, m_mlp_w1, m_mlp_w2, v_ln_mix_g, v_ln_mix_b, v_ln_ffn_g, v_ln_ffn_b, v_even_w_in, v_pool_w, v_pool_scale, v_lru_conv_w, v_lru_conv_b, v_lru_w_a, v_lru_b_a, v_lru_w_x, v_lru_b_x, v_lru_lambda, v_even_w_out, v_mla_w_down, v_mla_q_norm_g, v_mla_kv_norm_g, v_mla_w_qb, v_mla_w_kvb, v_mla_w_o, v_mlp_w1, v_mlp_w2):
    w = dict(zip(WEIGHTS, (ln_mix_g, ln_mix_b, ln_ffn_g, ln_ffn_b, even_w_in, pool_w, pool_scale, lru_conv_w,
                           lru_conv_b, lru_w_a, lru_b_a, lru_w_x, lru_b_x, lru_lambda, even_w_out, mla_w_down,
                           mla_q_norm_g, mla_kv_norm_g, mla_w_qb, mla_w_kvb, mla_w_o, mlp_w1, mlp_w2)))
    m = dict(zip(WEIGHTS, (m_ln_mix_g, m_ln_mix_b, m_ln_ffn_g, m_ln_ffn_b, m_even_w_in, m_pool_w, m_pool_scale,
                           m_lru_conv_w, m_lru_conv_b, m_lru_w_a, m_lru_b_a, m_lru_w_x, m_lru_b_x, m_lru_lambda,
                           m_even_w_out, m_mla_w_down, m_mla_q_norm_g, m_mla_kv_norm_g, m_mla_w_qb, m_mla_w_kvb,
                           m_mla_w_o, m_mlp_w1, m_mlp_w2)))
    v = dict(zip(WEIGHTS, (v_ln_mix_g, v_ln_mix_b, v_ln_ffn_g, v_ln_ffn_b, v_even_w_in, v_pool_w, v_pool_scale,
                           v_lru_conv_w, v_lru_conv_b, v_lru_w_a, v_lru_b_a, v_lru_w_x, v_lru_b_x, v_lru_lambda,
                           v_even_w_out, v_mla_w_down, v_mla_q_norm_g, v_mla_kv_norm_g, v_mla_w_qb, v_mla_w_kvb,
                           v_mla_w_o, v_mlp_w1, v_mlp_w2)))
    return _step(x, positions, loss_target, w, m, v)
```

```python
import functools

import jax
import jax.numpy as jnp
from jax import lax
from jax.experimental import pallas as pl
from jax.experimental.pallas import tpu as pltpu

F32 = jnp.float32
BF16 = jnp.bfloat16
S = jax.ShapeDtypeStruct

D = 1024
DEPTH = 4
N_DEV = 8
CHUNK_SHIFT = 6
POOL_WINDOWS = (2, 4, 8, 16)
POOL_W = 512
LRU_W = 1024
LRU_HEADS = 8
HEAD = 128
LRU_C = 8.0
EVEN_IN = 2560
EVEN_MIX = 1536
MLA_HEADS = 8
NOPE = 128
ROPE = 64
VDIM = 128
Q_RANK = 384
KV_RANK = 256
ODD_IN = 704
D_FF = 4096
FF_BLK = D_FF // N_DEV
ROPE_THETA = 10000.0
ALPHA = (2 * DEPTH) ** 0.25
LN_EPS = 1e-5
RMS_EPS = 1e-6
ATT_SCALE = (NOPE + ROPE) ** -0.5
NEG = float(jnp.finfo(jnp.float32).min)
ADAM_LR = 0.001
ADAM_B1 = 0.9
ADAM_B2 = 0.999
ADAM_EPS = 1e-08
ADAM_WD = 0.01
ADAM_STEP = 10
V7X_VMEM_BYTES = 64 * 1024 * 1024
VMEM_LIMIT = V7X_VMEM_BYTES - 8 * 1024 * 1024
MESH = pl.DeviceIdType.MESH


def _cp(*sem):
    return pltpu.CompilerParams(dimension_semantics=sem if sem else None, vmem_limit_bytes=VMEM_LIMIT)


def _dot(a, b):
    return jnp.dot(a, b, preferred_element_type=F32)


def _dot_nt(a, b):
    return lax.dot_general(a, b, (((1,), (1,)), ((), ())), preferred_element_type=F32)


def _dot_tn(a, b):
    return lax.dot_general(a, b, (((0,), (0,)), ((), ())), preferred_element_type=F32)


def _full(shape):
    return pl.BlockSpec(shape, lambda *_: (0,) * len(shape))


def _mm(a, b, *, mode, grid, a_spec, b_spec, out_shape, out_spec, name, add=None, add_spec=None, add_scale=1.0,
        dep=None):
    dot = {"nn": _dot, "nt": _dot_nt, "tn": _dot_tn}[mode]

    def body(*refs):
        a_ref, b_ref, o_ref = refs[0], refs[1], refs[-1]
        acc = dot(a_ref[...].astype(BF16), b_ref[...].astype(BF16))
        if add is not None:
            acc = acc + add_scale * refs[2][...]
        o_ref[...] = acc.astype(o_ref.dtype)

    ops = [a, b] if add is None else [a, b, add]
    specs = [a_spec, b_spec] if add is None else [a_spec, b_spec, add_spec]
    if dep is not None:
        ops.append(dep)
        specs.append(pl.BlockSpec(memory_space=pl.ANY))
    return pl.pallas_call(body, grid=grid, in_specs=specs, out_specs=out_spec, out_shape=out_shape,
                          compiler_params=_cp(*(("parallel",) * len(grid))), name=name)(*ops)


def _ln_stats(z):
    mu = jnp.mean(z, axis=-1, keepdims=True)
    zc = z - mu
    var = jnp.mean(zc * zc, axis=-1, keepdims=True)
    rstd = lax.rsqrt(var + LN_EPS)
    return zc * rstd, rstd


def _row_tile(t):
    return min(1024, t)


def _resid_ln(x, mix, g3, b3, l, name):
    t = x.shape[0]
    bm = _row_tile(t)

    def body(x_ref, m_ref, g_ref, b_ref, z_ref, y_ref, yb_ref):
        z = ALPHA * x_ref[...] + m_ref[...]
        xh, _ = _ln_stats(z)
        y = xh * g_ref[...] + b_ref[...]
        z_ref[...] = z
        y_ref[...] = y
        yb_ref[...] = y.astype(BF16)

    row = pl.BlockSpec((bm, D), lambda i: (i, 0))
    vec = pl.BlockSpec((None, 1, D), lambda i: (l, 0, 0))
    return pl.pallas_call(body, grid=(t // bm,), in_specs=[row, row, vec, vec], out_specs=[row, row, row],
                          out_shape=[S((t, D), F32), S((t, D), F32), S((t, D), BF16)],
                          compiler_params=_cp("parallel"), name=name)(x, mix, g3, b3)


def _proj_resid_ln(x, a, wmat, g3, b3, l, name):
    t, k = a.shape
    bm = _row_tile(t)

    def body(x_ref, a_ref, w_ref, g_ref, b_ref, z_ref, y_ref, yb_ref):
        z = ALPHA * x_ref[...] + _dot(a_ref[...], w_ref[...])
        xh, _ = _ln_stats(z)
        y = xh * g_ref[...] + b_ref[...]
        z_ref[...] = z
        y_ref[...] = y
        yb_ref[...] = y.astype(BF16)

    row = pl.BlockSpec((bm, D), lambda i: (i, 0))
    vec = pl.BlockSpec((None, 1, D), lambda i: (l, 0, 0))
    return pl.pallas_call(body, grid=(t // bm,),
                          in_specs=[row, pl.BlockSpec((bm, k), lambda i: (i, 0)), _full((k, D)), vec, vec],
                          out_specs=[row, row, row], out_shape=[S((t, D), F32), S((t, D), F32), S((t, D), BF16)],
                          compiler_params=_cp("parallel"), name=name)(x, a, wmat, g3, b3)


def _ln_bwd(d, z, g3, l, name, r=None, dep=None):
    t = z.shape[0]
    bm = _row_tile(t)

    def body(*refs):
        refs = list(refs)
        d_ref = refs.pop(0)
        dy = d_ref[...]
        if r is not None:
            dy = dy + ALPHA * refs.pop(0)[...]
        z_ref, g_ref = refs.pop(0), refs.pop(0)
        if dep is not None:
            refs.pop(0)
        dz_ref, dzb_ref, dg_ref, db_ref = refs
        xh, rstd = _ln_stats(z_ref[...])
        dyg = dy * g_ref[...]
        m1 = jnp.mean(dyg, axis=-1, keepdims=True)
        m2 = jnp.mean(dyg * xh, axis=-1, keepdims=True)
        dz = rstd * (dyg - m1 - xh * m2)
        dz_ref[...] = dz
        dzb_ref[...] = dz.astype(BF16)

        @pl.when(pl.program_id(0) == 0)
        def _():
            dg_ref[...] = jnp.zeros_like(dg_ref)
            db_ref[...] = jnp.zeros_like(db_ref)

        dg_ref[...] += jnp.sum(dy * xh, axis=0, keepdims=True)
        db_ref[...] += jnp.sum(dy, axis=0, keepdims=True)

    row = pl.BlockSpec((bm, D), lambda i: (i, 0))
    vec = pl.BlockSpec((None, 1, D), lambda i: (l, 0, 0))
    acc = pl.BlockSpec((1, D), lambda i: (0, 0))
    ops = [d, z, g3] if r is None else [d, r, z, g3]
    specs = [row, row, vec] if r is None else [row, row, row, vec]
    if dep is not None:
        ops.append(dep)
        specs.append(_full(dep.shape))
    return pl.pallas_call(body, grid=(t // bm,), in_specs=specs, out_specs=[row, row, acc, acc],
                          out_shape=[S((t, D), F32), S((t, D), BF16), S((1, D), F32), S((1, D), F32)],
                          compiler_params=_cp("arbitrary"), name=name)(*ops)


def _loss_grad(y, tgt):
    t = y.shape[0]
    bm = _row_tile(t)

    def body(y_ref, t_ref, dy_ref, loss_ref, acc_ref):
        i = pl.program_id(0)
        e = y_ref[...] - t_ref[...]
        dy_ref[...] = e * (1.0 / D)

        @pl.when(i == 0)
        def _():
            acc_ref[...] = jnp.zeros_like(acc_ref)

        acc_ref[...] += jnp.sum(e * e, axis=0, keepdims=True)

        @pl.when(i == pl.num_programs(0) - 1)
        def _():
            loss_ref[...] = jnp.full(loss_ref.shape, (0.5 / D) * jnp.sum(acc_ref[...]), F32)

    row = pl.BlockSpec((bm, D), lambda i: (i, 0))
    return pl.pallas_call(body, grid=(t // bm,), in_specs=[row, row],
                          out_specs=[row, pl.BlockSpec((1, 128), lambda i: (0, 0))],
                          out_shape=[S((t, D), F32), S((1, 128), F32)],
                          scratch_shapes=[pltpu.VMEM((1, D), F32)],
                          compiler_params=_cp("arbitrary"), name="loss_grad")(y, tgt)


def _mlp_row_tile(t):
    return min(1024, t)


MLP_ROW_PARTS = 2


def _row_parts(bm):
    step = bm // MLP_ROW_PARTS
    return [slice(k * step, (k + 1) * step) for k in range(MLP_ROW_PARTS)]


def _mlp_fwd(y, yb, w1g, w2g, g3, b3, l, dep=None):
    t = yb.shape[0]
    bm = _mlp_row_tile(t)

    def body(*refs):
        y_ref, yb_ref, w1_ref, w2_ref, g_ref, b_ref = refs[:6]
        z_ref, o_ref, ob_ref, act_ref, acc_ref = refs[-5:]
        j = pl.program_id(1)

        @pl.when(j == 0)
        def _():
            acc_ref[...] = jnp.zeros_like(acc_ref)

        for rows in _row_parts(bm):
            h = jnp.maximum(_dot(yb_ref[rows, :], w1_ref[...]), 0.0)
            act = (h * h).astype(BF16)
            act_ref[rows, :] = act
            acc_ref[rows, :] += _dot(act, w2_ref[...])

        @pl.when(j == N_DEV - 1)
        def _():
            z = ALPHA * y_ref[...] + acc_ref[...]
            xh, _ = _ln_stats(z)
            out = xh * g_ref[...] + b_ref[...]
            z_ref[...] = z
            o_ref[...] = out
            ob_ref[...] = out.astype(BF16)

    row = pl.BlockSpec((bm, D), lambda i, j: (i, 0))
    vec = pl.BlockSpec((None, 1, D), lambda i, j: (l, 0, 0))
    deps = [] if dep is None else [dep]
    return pl.pallas_call(
        body, grid=(t // bm, N_DEV),
        in_specs=[row, row, pl.BlockSpec((None, D, FF_BLK), lambda i, j: (j, 0, 0)),
                  pl.BlockSpec((None, FF_BLK, D), lambda i, j: (j, 0, 0)), vec, vec] + [ANY] * len(deps),
        out_specs=[row, row, row, pl.BlockSpec((bm, FF_BLK), lambda i, j: (i, j))],
        out_shape=[S((t, D), F32), S((t, D), F32), S((t, D), BF16), S((t, D_FF), BF16)],
        scratch_shapes=[pltpu.VMEM((bm, D), F32)],
        compiler_params=_cp("parallel", "arbitrary"), name="mlp_fwd")(y, yb, w1g, w2g, g3, b3, *deps)


def _mlp_bwd_dh(act, dzb, w1g, w2g):
    t = act.shape[0]
    bm = _mlp_row_tile(t)

    def body(a_ref, dz_ref, w1_ref, w2_ref, dh_ref, acc_ref):
        @pl.when(pl.program_id(1) == 0)
        def _():
            acc_ref[...] = jnp.zeros_like(acc_ref)

        for rows in _row_parts(bm):
            r = jnp.sqrt(a_ref[rows, :].astype(F32))
            dh = (_dot_nt(dz_ref[rows, :], w2_ref[...]) * (2.0 * r)).astype(BF16)
            dh_ref[rows, :] = dh
            acc_ref[rows, :] += _dot_nt(dh, w1_ref[...])

    row = pl.BlockSpec((bm, D), lambda i, j: (i, 0))
    hid = pl.BlockSpec((bm, FF_BLK), lambda i, j: (i, j))
    return pl.pallas_call(
        body, grid=(t // bm, N_DEV),
        in_specs=[hid, row,
                  pl.BlockSpec((None, D, FF_BLK), lambda i, j: (j, 0, 0)),
                  pl.BlockSpec((None, FF_BLK, D), lambda i, j: (j, 0, 0))],
        out_specs=[hid, row],
        out_shape=[S((t, D_FF), BF16), S((t, D), F32)],
        compiler_params=_cp("parallel", "arbitrary"), name="mlp_bwd_dh")(act, dzb, w1g, w2g)


F32_SUBLANES = 8


def _shift_dn(x, k, rows, fill=0.0):
    if k % F32_SUBLANES == 0:
        return jnp.concatenate([jnp.full((k,) + x.shape[1:], fill, x.dtype), x[:x.shape[0] - k]], axis=0)
    return jnp.where(rows >= k, pltpu.roll(x, k, 0), fill)


def _shift_up(x, k, rows, fill=0.0):
    t = x.shape[0]
    if k % F32_SUBLANES == 0:
        return jnp.concatenate([x[k:], jnp.full((k,) + x.shape[1:], fill, x.dtype)], axis=0)
    return jnp.where(rows < t - k, pltpu.roll(x, t - k, 0), fill)


def _scan_rows(a, b, shift):
    rows = lax.broadcasted_iota(jnp.int32, a.shape, 0)
    k = 1
    t = a.shape[0]
    while k < t:
        b = a * shift(b, k, rows) + b
        if 2 * k < t:
            a = a * shift(a, k, rows, 1.0)
        k *= 2
    return b


def _scan_dn(a, b):
    return _scan_rows(a, b, _shift_dn)


def _scan_up(a, b):
    return _scan_rows(a, b, _shift_up)


def _window_sum_dn(x, w, rows):
    k = 1
    while k < w:
        x = x + _shift_dn(x, k, rows)
        k *= 2
    return x


def _window_sum_up(x, w, rows):
    k = 1
    while k < w:
        x = x + _shift_up(x, k, rows)
        k *= 2
    return x


def _pool_diff(u, w, rows):
    inv_count = 1.0 / jnp.minimum(rows + 1, w).astype(F32)
    return _window_sum_dn(u, w, rows) * inv_count - u, inv_count


def _pool_fwd(proj, pool_w, pool_scale3, j):
    t = proj.shape[0]

    def body(u_ref, w_ref, s_ref, y_ref):
        rows = lax.broadcasted_iota(jnp.int32, (t, HEAD), 0)
        for g, w in enumerate(POOL_WINDOWS):
            cols = slice(g * HEAD, (g + 1) * HEAD)
            d, _ = _pool_diff(u_ref[:, cols], w, rows)
            y = _dot(d.astype(BF16), w_ref[g].astype(BF16)) * s_ref[:, cols]
            y_ref[:, cols] = y.astype(BF16)

    return pl.pallas_call(
        body, grid=(1,),
        in_specs=[pl.BlockSpec((t, POOL_W), lambda i: (0, 0)),
                  pl.BlockSpec((None, 4, HEAD, HEAD), lambda i: (j, 0, 0, 0)),
                  pl.BlockSpec((None, 1, POOL_W), lambda i: (j, 0, 0))],
        out_specs=pl.BlockSpec((t, POOL_W), lambda i: (0, 0)),
        out_shape=S((t, POOL_W), BF16), compiler_params=_cp("arbitrary"), name="pool_fwd")(proj, pool_w, pool_scale3)


def _pool_bwd(proj, dycat, pool_w, pool_scale3, j):
    t = proj.shape[0]

    def body(u_ref, dy_ref, w_ref, s_ref, du_ref, dw_ref, ds_ref):
        rows = lax.broadcasted_iota(jnp.int32, (t, HEAD), 0)
        for g, w in enumerate(POOL_WINDOWS):
            cols = slice(g * HEAD, (g + 1) * HEAD)
            d, inv_count = _pool_diff(u_ref[:, cols], w, rows)
            db = d.astype(BF16)
            wg = w_ref[g].astype(BF16)
            dy = dy_ref[:, cols]
            ds_ref[:, cols] = jnp.sum(dy * _dot(db, wg), axis=0, keepdims=True)
            dzz = (dy * s_ref[:, cols]).astype(BF16)
            dw_ref[g] = _dot_tn(db, dzz)
            dd = _dot_nt(dzz, wg)
            du_ref[:, cols] = (_window_sum_up(dd * inv_count, w, rows) - dd).astype(BF16)

    return pl.pallas_call(
        body, grid=(1,),
        in_specs=[pl.BlockSpec((t, POOL_W), lambda i: (0, 0)),
                  pl.BlockSpec((t, POOL_W), lambda i: (0, 0)),
                  pl.BlockSpec((None, 4, HEAD, HEAD), lambda i: (j, 0, 0, 0)),
                  pl.BlockSpec((None, 1, POOL_W), lambda i: (j, 0, 0))],
        out_specs=[pl.BlockSpec((t, POOL_W), lambda i: (0, 0)), _full((4, HEAD, HEAD)), _full((1, POOL_W))],
        out_shape=[S((t, POOL_W), BF16), S((4, HEAD, HEAD), F32), S((1, POOL_W), F32)],
        compiler_params=_cp("arbitrary"), name="pool_bwd")(proj, dycat, pool_w, pool_scale3)


GELU_C = 0.7978845608028654
GELU_K = 0.044715


def _gelu(x):
    th = jnp.tanh(GELU_C * (x + GELU_K * x * x * x))
    return 0.5 * x * (1.0 + th), th


def _lru_forward(u, gate, cw, cb, wa, ba, wx, bx, lam, rows):
    v = cw[3:4] * u + cw[2:3] * _shift_dn(u, 1, rows) + cw[1:2] * _shift_dn(u, 2, rows) \
        + cw[0:1] * _shift_dn(u, 3, rows) + cb
    vb = v.astype(BF16)
    r = jax.nn.sigmoid(_dot(vb, wa) + ba)
    i = jax.nn.sigmoid(_dot(vb, wx) + bx)
    sp = jnp.maximum(-lam, 0.0) + jnp.log1p(jnp.exp(-jnp.abs(lam)))
    log_a = (-LRU_C) * r * sp
    a = jnp.exp(log_a)
    one_m_a2 = -jnp.tanh(log_a) * (a * a + 1.0)
    mult = jnp.sqrt(one_m_a2)
    h = _scan_dn(a, mult * (i * v))
    gl, th = _gelu(gate)
    return dict(v=v, vb=vb, r=r, i=i, sp=sp, a=a, mult=mult, h=h, gl=gl, th=th)


def _lru_specs(t, j, col0_u, col0_g):
    blk = lambda c0: pl.BlockSpec((t, HEAD), lambda h: (0, c0 + h))
    vec = pl.BlockSpec((None, 1, HEAD), lambda h: (j, 0, h))
    return [blk(col0_u), blk(col0_g),
            pl.BlockSpec((None, 4, HEAD), lambda h: (j, 0, h)), vec,
            pl.BlockSpec((None, None, HEAD, HEAD), lambda h: (j, h, 0, 0)), vec,
            pl.BlockSpec((None, None, HEAD, HEAD), lambda h: (j, h, 0, 0)), vec, vec]


def _lru_fwd(proj, p, j):
    t = proj.shape[0]

    def body(u_ref, g_ref, cw_ref, cb_ref, wa_ref, ba_ref, wx_ref, bx_ref, lam_ref, y_ref):
        rows = lax.broadcasted_iota(jnp.int32, (t, HEAD), 0)
        f = _lru_forward(u_ref[...], g_ref[...], cw_ref[...], cb_ref[...], wa_ref[...].astype(BF16), ba_ref[...],
                         wx_ref[...].astype(BF16), bx_ref[...], lam_ref[...], rows)
        y_ref[...] = (f["h"] * f["gl"]).astype(BF16)

    return pl.pallas_call(
        body, grid=(LRU_HEADS,), in_specs=_lru_specs(t, j, POOL_W // HEAD, (POOL_W + LRU_W) // HEAD),
        out_specs=pl.BlockSpec((t, HEAD), lambda h: (0, h)), out_shape=S((t, LRU_W), BF16),
        compiler_params=_cp("parallel"), name="lru_fwd")(
            proj, proj, p["conv_w"], p["conv_b"], p["w_a"], p["b_a"], p["w_x"], p["b_x"], p["lam"])


def _lru_bwd(proj, dycat, p, j):
    t = proj.shape[0]

    def body(u_ref, g_ref, cw_ref, cb_ref, wa_ref, ba_ref, wx_ref, bx_ref, lam_ref, dy_ref,
             du_ref, dgate_ref, dcw_ref, dcb_ref, dwa_ref, dba_ref, dwx_ref, dbx_ref, dlam_ref):
        rows = lax.broadcasted_iota(jnp.int32, (t, HEAD), 0)
        u = u_ref[...]
        gate = g_ref[...]
        cw = cw_ref[...]
        wa = wa_ref[...].astype(BF16)
        wx = wx_ref[...].astype(BF16)
        lam = lam_ref[...]
        f = _lru_forward(u, gate, cw, cb_ref[...], wa, ba_ref[...], wx, bx_ref[...], lam, rows)
        v, r, i, a, mult, h, th = f["v"], f["r"], f["i"], f["a"], f["mult"], f["h"], f["th"]
        dy = dy_ref[...]
        dgl = 0.5 * (1.0 + th) + 0.5 * gate * (1.0 - th * th) * GELU_C * (1.0 + 3.0 * GELU_K * gate * gate)
        dgate_ref[...] = (dy * h * dgl).astype(BF16)
        g = _scan_up(_shift_up(a, 1, rows), dy * f["gl"])
        da = g * _shift_dn(h, 1, rows)
        iv = i * v
        dmult = g * iv
        di = g * mult * v
        dv = g * mult * i
        dlog_a = da * a - dmult * (a * a) / mult
        dr = dlog_a * (-LRU_C) * f["sp"]
        dsp = jnp.sum(dlog_a * (-LRU_C) * r, axis=0, keepdims=True)
        dlam_ref[...] = -dsp * jax.nn.sigmoid(-lam)
        dpa = dr * r * (1.0 - r)
        dpx = di * i * (1.0 - i)
        dpab = dpa.astype(BF16)
        dpxb = dpx.astype(BF16)
        dwa_ref[...] = _dot_tn(f["vb"], dpab)
        dwx_ref[...] = _dot_tn(f["vb"], dpxb)
        dba_ref[...] = jnp.sum(dpa, axis=0, keepdims=True)
        dbx_ref[...] = jnp.sum(dpx, axis=0, keepdims=True)
        dv = dv + _dot_nt(dpab, wa) + _dot_nt(dpxb, wx)
        dcb_ref[...] = jnp.sum(dv, axis=0, keepdims=True)
        du = cw[3:4] * dv
        dcw_ref[3:4, :] = jnp.sum(dv * u, axis=0, keepdims=True)
        for k in (1, 2, 3):
            du = du + cw[3 - k:4 - k] * _shift_up(dv, k, rows)
            dcw_ref[3 - k:4 - k, :] = jnp.sum(dv * _shift_dn(u, k, rows), axis=0, keepdims=True)
        du_ref[...] = du.astype(BF16)

    blk = pl.BlockSpec((t, HEAD), lambda h: (0, h))
    vec = pl.BlockSpec((1, HEAD), lambda h: (0, h))
    mat = pl.BlockSpec((None, HEAD, HEAD), lambda h: (h, 0, 0))
    return pl.pallas_call(
        body, grid=(LRU_HEADS,),
        in_specs=_lru_specs(t, j, POOL_W // HEAD, (POOL_W + LRU_W) // HEAD)
        + [pl.BlockSpec((t, HEAD), lambda h: (0, POOL_W // HEAD + h))],
        out_specs=[blk, blk, pl.BlockSpec((4, HEAD), lambda h: (0, h)), vec, mat, vec, mat, vec, vec],
        out_shape=[S((t, LRU_W), BF16), S((t, LRU_W), BF16), S((4, LRU_W), F32), S((1, LRU_W), F32),
                   S((LRU_HEADS, HEAD, HEAD), F32), S((1, LRU_W), F32),
                   S((LRU_HEADS, HEAD, HEAD), F32), S((1, LRU_W), F32), S((1, LRU_W), F32)],
        compiler_params=_cp("parallel"), name="lru_bwd")(
            proj, proj, p["conv_w"], p["conv_b"], p["w_a"], p["b_a"], p["w_x"], p["b_x"], p["lam"], dycat)


def _rope(x, c, s):
    x1 = x[:, :ROPE // 2]
    x2 = x[:, ROPE // 2:]
    return jnp.concatenate([x1 * c - x2 * s, x1 * s + x2 * c], axis=-1)


def _rope_t(d, c, s):
    d1 = d[:, :ROPE // 2]
    d2 = d[:, ROPE // 2:]
    return jnp.concatenate([d1 * c + d2 * s, d2 * c - d1 * s], axis=-1)


def _rope_tables(pos2, inv_freq):
    t = pos2.shape[0]

    def body(p_ref, f_ref, c_ref, s_ref):
        ang = p_ref[...].astype(F32) * f_ref[...]
        c_ref[...] = jnp.cos(ang)
        s_ref[...] = jnp.sin(ang)

    return pl.pallas_call(body, out_shape=[S((t, ROPE // 2), F32), S((t, ROPE // 2), F32)],
                          name="rope_tables")(pos2, inv_freq)


def _down_norm(xb, wdown_g, gq3, gkv3, cos, sin, j):
    t = xb.shape[0]
    bm = _row_tile(t)

    def body(x_ref, w_ref, gq_ref, gkv_ref, c_ref, s_ref, down_ref, cq_ref, ckv_ref, kpe_ref):
        w = w_ref[...].reshape(D, ODD_IN)
        down = _dot(x_ref[...], w)
        down_ref[...] = down
        q = down[:, :Q_RANK]
        cq_ref[...] = (q * lax.rsqrt(jnp.mean(q * q, axis=-1, keepdims=True) + RMS_EPS) * gq_ref[...]).astype(BF16)
        kv = down[:, Q_RANK:Q_RANK + KV_RANK]
        ckv_ref[...] = (kv * lax.rsqrt(jnp.mean(kv * kv, axis=-1, keepdims=True) + RMS_EPS)
                        * gkv_ref[...]).astype(BF16)
        kpe_ref[...] = _rope(down[:, Q_RANK + KV_RANK:], c_ref[...], s_ref[...])

    row = lambda n: pl.BlockSpec((bm, n), lambda i: (i, 0))
    return pl.pallas_call(
        body, grid=(t // bm,),
        in_specs=[row(D), _full((N_DEV, D // N_DEV, ODD_IN)),
                  pl.BlockSpec((None, 1, Q_RANK), lambda i: (j, 0, 0)),
                  pl.BlockSpec((None, 1, KV_RANK), lambda i: (j, 0, 0)), row(ROPE // 2), row(ROPE // 2)],
        out_specs=[row(ODD_IN), row(Q_RANK), row(KV_RANK), row(ROPE)],
        out_shape=[S((t, ODD_IN), F32), S((t, Q_RANK), BF16), S((t, KV_RANK), BF16), S((t, ROPE), F32)],
        compiler_params=_cp("parallel"), name="down_norm")(xb, wdown_g, gq3, gkv3, cos, sin)


def _q_tile(t, widest):
    return min(widest, t // 2)


def _attn_probs(q, k, qs):
    s = _dot_nt(q, k) * ATT_SCALE
    tq = q.shape[0]
    rows = lax.broadcasted_iota(jnp.int32, (tq, tq), 0)
    cols = lax.broadcasted_iota(jnp.int32, (tq, tq), 1)
    last = jnp.where(jnp.right_shift(cols, CHUNK_SHIFT) <= jnp.right_shift(rows, CHUNK_SHIFT), s[:, qs:], NEG)
    s = last if qs == 0 else jnp.concatenate([s[:, :qs], last], axis=1)
    e = jnp.exp(s - jnp.max(s, axis=-1, keepdims=True))
    return e / jnp.sum(e, axis=-1, keepdims=True)


def _head_qkv(cq, ckv, kpe, c, s, wq_ref, wkv_ref):
    q = jnp.concatenate([_dot(cq, wq_ref[:, :NOPE]), _rope(_dot(cq, wq_ref[:, NOPE:]), c, s)], axis=1).astype(BF16)
    k = jnp.concatenate([_dot(ckv, wkv_ref[:, :NOPE]), kpe], axis=1).astype(BF16)
    vv = _dot(ckv, wkv_ref[:, NOPE:]).astype(BF16)
    return q, k, vv


def _attn_in_specs(t):
    return [_full((t, Q_RANK)), _full((t, KV_RANK)), _full((t, ROPE)), _full((t, ROPE // 2)), _full((t, ROPE // 2)),
            pl.BlockSpec((None, Q_RANK, NOPE + ROPE), lambda h: (h, 0, 0)),
            pl.BlockSpec((None, KV_RANK, NOPE + VDIM), lambda h: (h, 0, 0)),
            pl.BlockSpec((None, VDIM, D), lambda h: (h, 0, 0))]


def _attn_fwd(cq, ckv, kpe, cos, sin, wqb_g, wkvb_g, wo_g):
    t = cq.shape[0]
    tq = _q_tile(t, 256)

    def body(cq_ref, ckv_ref, kpe_ref, c_ref, s_ref, wq_ref, wkv_ref, wo_ref, o_ref, mix_ref):
        q, k, vv = _head_qkv(cq_ref[...], ckv_ref[...], kpe_ref[...], c_ref[...], s_ref[...], wq_ref, wkv_ref)
        for qs in range(0, t, tq):
            ke = qs + tq
            p = _attn_probs(q[qs:ke], k[:ke], qs)
            o_ref[qs:ke, :] = _dot(p.astype(BF16), vv[:ke]).astype(BF16)
        c = _dot(o_ref[...], wo_ref[...])

        @pl.when(pl.program_id(0) == 0)
        def _():
            mix_ref[...] = c

        @pl.when(pl.program_id(0) > 0)
        def _():
            mix_ref[...] += c

    return pl.pallas_call(
        body, grid=(MLA_HEADS,), in_specs=_attn_in_specs(t),
        out_specs=[pl.BlockSpec((None, t, VDIM), lambda h: (h, 0, 0)), _full((t, D))],
        out_shape=[S((MLA_HEADS, t, VDIM), BF16), S((t, D), F32)],
        compiler_params=_cp("arbitrary"), name="attn_fwd")(cq, ckv, kpe, cos, sin, wqb_g, wkvb_g, wo_g)


def _attn_bwd(cq, ckv, kpe, cos, sin, wqb_g, wkvb_g, wo_g, o, dzb):
    t = cq.shape[0]
    tq = _q_tile(t, 512)

    def body(cq_ref, ckv_ref, kpe_ref, c_ref, s_ref, wq_ref, wkv_ref, wo_ref, o_ref, dz_ref,
             dwo_ref, dwq_ref, dwkv_ref, dcq_ref, dckv_ref, dkpe_ref, dkt_s, dvt_s, dq_s):
        cqv = cq_ref[...]
        ckvv = ckv_ref[...]
        c = c_ref[...]
        s = s_ref[...]
        q, k, vv = _head_qkv(cqv, ckvv, kpe_ref[...], c, s, wq_ref, wkv_ref)
        dzv = dz_ref[...]
        dwo_ref[...] = _dot_tn(o_ref[...], dzv).astype(BF16)
        do = _dot_nt(dzv, wo_ref[...]).astype(BF16)
        dkt_s[...] = jnp.zeros_like(dkt_s)
        dvt_s[...] = jnp.zeros_like(dvt_s)
        for qs in range(0, t, tq):
            ke = qs + tq
            p = _attn_probs(q[qs:ke], k[:ke], qs)
            dp = _dot_nt(do[qs:ke], vv[:ke])
            ds = (p * (dp - jnp.sum(p * dp, axis=-1, keepdims=True)) * ATT_SCALE).astype(BF16)
            dq_s[qs:ke, :] = _dot(ds, k[:ke])
            dkt_s[0:NOPE + ROPE, 0:ke] += _dot_tn(q[qs:ke], ds)
            dvt_s[:, 0:ke] += _dot_tn(do[qs:ke], p.astype(BF16))
        dk = dkt_s[...].T
        dqn = dq_s[:, :NOPE].astype(BF16)
        dqp = _rope_t(dq_s[:, NOPE:], c, s).astype(BF16)
        dkn = dk[:, :NOPE].astype(BF16)
        dkp = dk[:, NOPE:NOPE + ROPE]
        dvv = dvt_s[...].T.astype(BF16)
        dwq_ref[:, :NOPE] = _dot_tn(cqv, dqn).astype(BF16)
        dwq_ref[:, NOPE:] = _dot_tn(cqv, dqp).astype(BF16)
        dwkv_ref[:, :NOPE] = _dot_tn(ckvv, dkn).astype(BF16)
        dwkv_ref[:, NOPE:] = _dot_tn(ckvv, dvv).astype(BF16)
        dcq = _dot_nt(dqn, wq_ref[:, :NOPE]) + _dot_nt(dqp, wq_ref[:, NOPE:])
        dckv = _dot_nt(dkn, wkv_ref[:, :NOPE]) + _dot_nt(dvv, wkv_ref[:, NOPE:])

        @pl.when(pl.program_id(0) == 0)
        def _():
            dcq_ref[...] = dcq
            dckv_ref[...] = dckv
            dkpe_ref[...] = dkp

        @pl.when(pl.program_id(0) > 0)
        def _():
            dcq_ref[...] += dcq
            dckv_ref[...] += dckv
            dkpe_ref[...] += dkp

    per_head = lambda a, b: pl.BlockSpec((None, a, b), lambda h: (h, 0, 0))
    return pl.pallas_call(
        body, grid=(MLA_HEADS,),
        in_specs=_attn_in_specs(t) + [per_head(t, VDIM), _full((t, D))],
        out_specs=[per_head(VDIM, D), per_head(Q_RANK, NOPE + ROPE), per_head(KV_RANK, NOPE + VDIM),
                   _full((t, Q_RANK)), _full((t, KV_RANK)), _full((t, ROPE))],
        out_shape=[S((MLA_HEADS, VDIM, D), BF16), S((MLA_HEADS, Q_RANK, NOPE + ROPE), BF16),
                   S((MLA_HEADS, KV_RANK, NOPE + VDIM), BF16),
                   S((t, Q_RANK), F32), S((t, KV_RANK), F32), S((t, ROPE), F32)],
        scratch_shapes=[pltpu.VMEM((2 * NOPE, t), F32), pltpu.VMEM((VDIM, t), F32),
                        pltpu.VMEM((t, NOPE + ROPE), F32)],
        compiler_params=_cp("arbitrary"), name="attn_bwd")(cq, ckv, kpe, cos, sin, wqb_g, wkvb_g, wo_g, o, dzb)


def _rms_bwd(down, dcq, dckv, dkpe, cos, sin, gq3, gkv3, j):
    t = down.shape[0]
    bm = _row_tile(t)

    def body(down_ref, dcq_ref, dckv_ref, dkpe_ref, c_ref, s_ref, gq_ref, gkv_ref, dd_ref, dgq_ref, dgkv_ref):
        @pl.when(pl.program_id(0) == 0)
        def _():
            dgq_ref[...] = jnp.zeros_like(dgq_ref)
            dgkv_ref[...] = jnp.zeros_like(dgkv_ref)

        def rms_b(x, dy, g):
            rstd = lax.rsqrt(jnp.mean(x * x, axis=-1, keepdims=True) + RMS_EPS)
            xh = x * rstd
            dyg = dy * g
            return rstd * (dyg - xh * jnp.mean(dyg * xh, axis=-1, keepdims=True)), jnp.sum(dy * xh, axis=0, keepdims=True)

        dq, dgq = rms_b(down_ref[:, :Q_RANK], dcq_ref[...], gq_ref[...])
        dkv, dgkv = rms_b(down_ref[:, Q_RANK:Q_RANK + KV_RANK], dckv_ref[...], gkv_ref[...])
        dgq_ref[...] += dgq
        dgkv_ref[...] += dgkv
        dd_ref[:, :Q_RANK] = dq.astype(BF16)
        dd_ref[:, Q_RANK:Q_RANK + KV_RANK] = dkv.astype(BF16)
        dd_ref[:, Q_RANK + KV_RANK:] = _rope_t(dkpe_ref[...], c_ref[...], s_ref[...]).astype(BF16)

    row = lambda n: pl.BlockSpec((bm, n), lambda i: (i, 0))
    return pl.pallas_call(
        body, grid=(t // bm,),
        in_specs=[row(ODD_IN), row(Q_RANK), row(KV_RANK), row(ROPE), row(ROPE // 2), row(ROPE // 2),
                  pl.BlockSpec((None, 1, Q_RANK), lambda i: (j, 0, 0)),
                  pl.BlockSpec((None, 1, KV_RANK), lambda i: (j, 0, 0))],
        out_specs=[row(ODD_IN), _full((1, Q_RANK)), _full((1, KV_RANK))],
        out_shape=[S((t, ODD_IN), BF16), S((1, Q_RANK), F32), S((1, KV_RANK), F32)],
        compiler_params=_cp("arbitrary"), name="rms_bwd")(down, dcq, dckv, dkpe, cos, sin, gq3, gkv3)


def _col_blocks(t, n, bn):
    return pl.BlockSpec((t, bn), lambda i: (0, i))


def _row_blocks(n, bm):
    return pl.BlockSpec((bm, n), lambda i: (i, 0))


def _local_step(x, pos2, tgt, small, weights_of, grads_done, start_dep=None, prefetch=None):
    t = x.shape[0]
    bm = min(512, t)
    inv_freq = (ROPE_THETA ** (-jnp.arange(0, ROPE, 2, dtype=F32) / ROPE)).reshape(1, ROPE // 2)
    cos, sin = _rope_tables(pos2, inv_freq)
    lru_p = {k: small[k] for k in ("conv_w", "conv_b", "w_a", "b_a", "w_x", "b_x", "lam")}

    saved = []
    y, yb = x, x.astype(BF16)
    for l in range(DEPTH):
        j = l // 2
        big = weights_of(l, 0, y)
        sv = dict(xb=yb, big=big)
        if l % 2 == 0:
            proj = _mm(yb, big["win_t"], mode="nt", grid=(EVEN_IN // 512,), a_spec=_full((t, D)),
                       b_spec=_row_blocks(D, 512), out_shape=S((t, EVEN_IN), F32),
                       out_spec=_col_blocks(t, EVEN_IN, 512), name="even_proj", dep=start_dep if l == 0 else None)
            ycat = jnp.concatenate([_pool_fwd(proj, small["pool_w"], small["pool_scale"], j),
                                    _lru_fwd(proj, lru_p, j)], axis=1)
            big.update(weights_of(l, 1, ycat))
            z1, y1, y1b = _proj_resid_ln(y, ycat, big["wout2d"], small["ln_mix_g"], small["ln_mix_b"], l, "even_out")
            sv.update(proj=proj, ycat=ycat)
        else:
            down, cq, ckv, kpe = _down_norm(yb, big["wdown"], small["gq"], small["gkv"], cos, sin, j)
            o, mix = _attn_fwd(cq, ckv, kpe, cos, sin, big["wqb"], big["wkvb"], big["wo"])
            z1, y1, y1b = _resid_ln(y, mix, small["ln_mix_g"], small["ln_mix_b"], l, "resid_ln")
            sv.update(down=down, cq=cq, ckv=ckv, kpe=kpe, o=o)
        fetched = prefetch(l + 1, y1) if prefetch is not None and l + 1 < DEPTH else None
        z2, y, yb, act = _mlp_fwd(y1, y1b, big["w1"], big["w2"], small["ln_ffn_g"], small["ln_ffn_b"], l,
                                  dep=fetched)
        sv.update(z1=z1, y1b=y1b, z2=z2, act=act)
        saved.append(sv)

    dy, loss_tile = _loss_grad(y, tgt)

    g = {k: [None] * n for k, n in (("ln_mix_g", 4), ("ln_mix_b", 4), ("ln_ffn_g", 4), ("ln_ffn_b", 4),
                                    ("pool_w", 2), ("pool_scale", 2), ("conv_w", 2), ("conv_b", 2),
                                    ("w_a", 2), ("b_a", 2), ("w_x", 2), ("b_x", 2), ("lam", 2),
                                    ("gq", 2), ("gkv", 2))}
    dep = None
    for l in reversed(range(DEPTH)):
        j = l // 2
        sv = saved[l]
        big = sv["big"]
        dz2, dz2b, g["ln_ffn_g"][l], g["ln_ffn_b"][l] = _ln_bwd(dy, sv["z2"], small["ln_ffn_g"], l, "ln_bwd", dep=dep)
        act = sv["act"]
        dh, dff = _mlp_bwd_dh(act, dz2b, big["w1"], big["w2"])
        dw1 = _mm(sv["y1b"], dh, mode="tn", grid=(N_DEV,), a_spec=_full((t, D)),
                  b_spec=_col_blocks(t, D_FF, FF_BLK), out_shape=S((N_DEV, D, FF_BLK), BF16),
                  out_spec=pl.BlockSpec((None, D, FF_BLK), lambda i: (i, 0, 0)), name="mlp_dw1")
        dw2 = _mm(act, dz2b, mode="tn", grid=(N_DEV,), a_spec=_col_blocks(t, D_FF, FF_BLK),
                  b_spec=_full((t, D)), out_shape=S((N_DEV, FF_BLK, D), BF16),
                  out_spec=pl.BlockSpec((None, FF_BLK, D), lambda i: (i, 0, 0)), name="mlp_dw2")
        dep = grads_done(l, dict(w1=dw1, w2=dw2))
        dz1, dz1b, g["ln_mix_g"][l], g["ln_mix_b"][l] = _ln_bwd(dff, sv["z1"], small["ln_mix_g"], l, "ln_bwd_res",
                                                                 r=dz2, dep=dep)
        if l % 2 == 0:
            wout = big["wout2d"]
            dycat = _mm(dz1b, wout, mode="nt", grid=(EVEN_MIX // 512,), a_spec=_full((t, D)),
                        b_spec=_row_blocks(D, 512), out_shape=S((t, EVEN_MIX), F32),
                        out_spec=_col_blocks(t, EVEN_MIX, 512), name="even_dycat")
            dwout = _mm(sv["ycat"], dz1b, mode="tn", grid=(EVEN_MIX // 512,), a_spec=_col_blocks(t, EVEN_MIX, 512),
                        b_spec=_full((t, D)), out_shape=S((EVEN_MIX, D), BF16), out_spec=_row_blocks(D, 512),
                        name="even_dwout")
            du_pool, g["pool_w"][j], g["pool_scale"][j] = _pool_bwd(sv["proj"], dycat, small["pool_w"],
                                                                   small["pool_scale"], j)
            (du_lru, du_gate, g["conv_w"][j], g["conv_b"][j], g["w_a"][j], g["b_a"][j], g["w_x"][j], g["b_x"][j],
             g["lam"][j]) = _lru_bwd(sv["proj"], dycat, lru_p, j)
            dproj = jnp.concatenate([du_pool, du_lru, du_gate], axis=1)
            dwin = _mm(sv["xb"], dproj, mode="tn", grid=(EVEN_IN // 512,), a_spec=_full((t, D)),
                       b_spec=_col_blocks(t, EVEN_IN, 512), out_shape=S((D, EVEN_IN), BF16),
                       out_spec=_col_blocks(D, EVEN_IN, 512), name="even_dwin")
            dep = grads_done(l, dict(win=dwin.reshape(D, N_DEV, EVEN_IN // N_DEV).transpose(1, 0, 2),
                                     wout=dwout.reshape(N_DEV, EVEN_MIX // N_DEV, D)))
            dy = _mm(dproj, big["win_t"], mode="nn", grid=(t // bm,), a_spec=_row_blocks(EVEN_IN, bm),
                     b_spec=_full((EVEN_IN, D)), out_shape=S((t, D), F32), out_spec=_row_blocks(D, bm),
                     add=dz1, add_spec=_row_blocks(D, bm), add_scale=ALPHA, name="even_dx", dep=dep)
        else:
            dwo, dwqb, dwkvb, dcq, dckv, dkpe = _attn_bwd(
                sv["cq"], sv["ckv"], sv["kpe"], cos, sin, big["wqb"], big["wkvb"], big["wo"], sv["o"], dz1b)
            ddown, g["gq"][j], g["gkv"][j] = _rms_bwd(sv["down"], dcq, dckv, dkpe, cos, sin, small["gq"],
                                                     small["gkv"], j)
            dwdown = _mm(sv["xb"], ddown, mode="tn", grid=(N_DEV,), a_spec=_col_blocks(t, D, D // N_DEV),
                         b_spec=_full((t, ODD_IN)), out_shape=S((N_DEV, D // N_DEV, ODD_IN), BF16),
                         out_spec=pl.BlockSpec((None, D // N_DEV, ODD_IN), lambda i: (i, 0, 0)),
                         name="odd_dwdown")
            dep = grads_done(l, dict(wdown=dwdown, wqb=dwqb, wkvb=dwkvb, wo=dwo))
            dy = _mm(ddown, big["wdown2d"], mode="nt", grid=(t // bm,), a_spec=_row_blocks(ODD_IN, bm),
                     b_spec=_full((D, ODD_IN)), out_shape=S((t, D), F32), out_spec=_row_blocks(D, bm),
                     add=dz1, add_spec=_row_blocks(D, bm), add_scale=ALPHA, name="odd_dx", dep=dep)
    return loss_tile[0, 0], dy, g


def _mesh_place():
    x, y, c = lax.axis_index("x"), lax.axis_index("y"), lax.axis_index("c")
    return x, y, c


def _peer(place, k):
    x, y, c = place
    return (1 - x if k & 4 else x, 1 - y if k & 2 else y, 1 - c if k & 1 else c)


def _index(place):
    x, y, c = place
    return 4 * x + 2 * y + c


ANY = pl.BlockSpec(memory_space=pl.ANY)


def _make_zones(shards, me, name, dtype=BF16):
    n = len(shards)

    def body(me_ref, *refs):
        for src, dst in zip(refs[:n], refs[n:]):
            dst[...] = src[...].astype(dtype)

    grid_spec = pltpu.PrefetchScalarGridSpec(
        num_scalar_prefetch=1, grid=(1,),
        in_specs=[pl.BlockSpec(s.shape, lambda i, me_ref: (0, 0)) for s in shards],
        out_specs=[pl.BlockSpec((None,) + s.shape, lambda i, me_ref: (me_ref[0], 0, 0)) for s in shards])
    return pl.pallas_call(body, grid_spec=grid_spec, out_shape=[S((N_DEV,) + s.shape, dtype) for s in shards],
                          compiler_params=_cp("arbitrary"), name=name)(me, *shards)


def _all_gather_big(zones):
    n = len(zones)

    def body(*refs):
        outs = refs[n:2 * n]
        send, recv = refs[2 * n:]
        x, y, c = _mesh_place()
        me, sibling = (x, y, c), (x, y, 1 - c)
        chips = [(1 - x, y), (x, 1 - y), (1 - x, 1 - y)]

        def copy(w, k, block, to):
            blk = outs[w].at[_index(block)]
            return pltpu.make_async_remote_copy(src_ref=blk, dst_ref=blk, send_sem=send.at[w, k], recv_sem=recv.at[w, k],
                                                device_id=to, device_id_type=MESH)

        first = []
        for w in range(n):
            first.append(copy(w, 0, me, sibling))
            first += [copy(w, 1 + j, me, (*chip, c)) for j, chip in enumerate(chips)]
        for cp in first:
            cp.start()
        passed = []
        for w in range(n):
            for j, chip in enumerate(chips):
                copy(w, 1 + j, (*chip, c), me).wait_recv()
                cp = copy(w, 4 + j, (*chip, c), sibling)
                cp.start()
                passed.append(cp)
        for w in range(n):
            copy(w, 0, sibling, me).wait_recv()
            for j, chip in enumerate(chips):
                copy(w, 4 + j, (*chip, 1 - c), me).wait_recv()
        for cp in first + passed:
            cp.wait_send()

    return pl.pallas_call(
        body, in_specs=[ANY] * n, out_specs=[ANY] * n, out_shape=[S(z.shape, z.dtype) for z in zones],
        input_output_aliases={i: i for i in range(n)},
        scratch_shapes=[pltpu.SemaphoreType.DMA((n, N_DEV - 1)), pltpu.SemaphoreType.DMA((n, N_DEV - 1))],
        compiler_params=pltpu.CompilerParams(has_side_effects=True), name="all_gather_big")(*zones)


def _shard_rows_tile(a):
    return max(d for d in range(16, 257, 16) if a % d == 0)


HBM = pl.BlockSpec(memory_space=pltpu.HBM)
SEM = pl.BlockSpec(memory_space=pltpu.SEMAPHORE)
DATAFLOW = pltpu.SideEffectType.DATAFLOW_SIDE_EFFECTING


def _in_hbm(a):
    return pltpu.with_memory_space_constraint(a, pltpu.HBM)


def _gather_ici_copies(place, src, land, w):
    me = _index(place)
    return [(_peer(place, k), land.at[me], land.at[me]) for k in (1, 2, 4, 6)]


def _gather_d2d_copies(place, src, land, w):
    blocks = [_index(_peer(place, k)) for k in (2, 4, 6)]
    return [(_peer(place, 1), land.at[b], land.at[b]) for b in blocks]


GATHER_ICI = (4, _gather_ici_copies)
GATHER_D2D = (3, _gather_d2d_copies)


def _scatter_plan(layers):
    def copies(place, src, land, w):
        me = _index(place)
        mine = land.at[me] if layers[w] is None else land.at[me, layers[w]]
        return [(_peer(place, k), src.at[_index(_peer(place, k))], mine) for k in range(1, N_DEV)]
    return (N_DEV - 1, copies)


def _gather_all_copies(place, src, land, w):
    me = _index(place)
    return [(_peer(place, k), land.at[me], land.at[me]) for k in range(1, N_DEV)]


GATHER_ALL = (N_DEV - 1, _gather_all_copies)


def _sum_blocks(zone, part, me):
    r = part.shape[1]

    def body(me_ref, z_ref, p_ref, o_ref):
        acc = None
        for s in range(N_DEV):
            term = jnp.where(me_ref[0] == s, p_ref[...], z_ref[s])
            acc = term if acc is None else acc + term
        o_ref[...] = acc

    grid_spec = pltpu.PrefetchScalarGridSpec(
        num_scalar_prefetch=1, grid=(1,),
        in_specs=[pl.BlockSpec((N_DEV, r, 128), lambda i, me_ref: (0, 0, 0)),
                  pl.BlockSpec((None, r, 128), lambda i, me_ref: (me_ref[0], 0, 0))],
        out_specs=pl.BlockSpec((r, 128), lambda i, me_ref: (0, 0)))
    return pl.pallas_call(body, grid_spec=grid_spec, out_shape=S((r, 128), F32),
                          compiler_params=_cp("arbitrary"), name="sum_small")(me, zone, part)


def _exchange_start(srcs, lands, plan, name, after=()):
    ns, n = len(srcs), len(lands)
    n_in = ns + n + len(after)
    per, copies = plan

    def body(*refs):
        ins, land = refs[:ns], refs[ns:ns + n]
        send, recv = refs[n_in], refs[n_in + 1]
        token = refs[-1]
        place = _mesh_place()
        for i in range(per):
            for w in range(n):
                target, src, dst = copies(place, ins[w] if ns else None, land[w], w)[i]
                pltpu.make_async_remote_copy(src_ref=src, dst_ref=dst, send_sem=send.at[w * per + i],
                                             recv_sem=recv.at[w * per + i], device_id=target, device_id_type=MESH).start()
        token[...] = jnp.zeros_like(token)

    sems = pltpu.SemaphoreType.DMA((n * per,))
    thru = [pltpu.HBM(a.shape, a.dtype) for a in list(srcs) + list(lands)]
    out = pl.pallas_call(
        body, name=name, in_specs=[HBM] * (ns + n) + [ANY] * len(after),
        out_shape=(sems, sems, *thru, S((8, 128), F32)),
        out_specs=(SEM, SEM, *([HBM] * (ns + n)), pl.BlockSpec(memory_space=pltpu.VMEM)),
        input_output_aliases={i: 2 + i for i in range(ns + n)},
        compiler_params=pltpu.CompilerParams(has_side_effects=DATAFLOW),
    )(*[_in_hbm(a) for a in list(srcs) + list(lands)], *after)
    return out[0], out[1], list(out[2:2 + ns]), list(out[2 + ns:2 + ns + n]), out[-1]


def _exchange_wait(send, recv, srcs, lands, plan, after, name):
    ns, n = len(srcs), len(lands)
    per, copies = plan
    afters = tuple(after) if isinstance(after, (tuple, list)) else (after,)

    def body(*refs):
        ins, land = refs[:ns], refs[ns:ns + n]
        send_ref, recv_ref = refs[ns + n], refs[ns + n + 1]
        place = _mesh_place()
        for i in range(per):
            for w in range(n):
                target, src, dst = copies(place, ins[w] if ns else None, land[w], w)[i]
                cp = pltpu.make_async_remote_copy(src_ref=src, dst_ref=dst, send_sem=send_ref.at[w * per + i],
                                                  recv_sem=recv_ref.at[w * per + i], device_id=target,
                                                  device_id_type=MESH)
                cp.wait_send()
                cp.wait_recv()

    thru = [pltpu.HBM(a.shape, a.dtype) for a in list(srcs) + list(lands)]
    out = pl.pallas_call(
        body, name=name, in_specs=[HBM] * (ns + n) + [SEM, SEM] + [ANY] * len(afters),
        out_shape=tuple(thru), out_specs=tuple([HBM] * (ns + n)),
        input_output_aliases={i: i for i in range(ns + n)},
        compiler_params=pltpu.CompilerParams(has_side_effects=DATAFLOW),
    )(*srcs, *lands, send, recv, *afters)
    return list(out[:ns]), list(out[ns:])


def _all_reduce_small(part, name, deps=()):
    def body(*refs):
        p_ref = refs[0]
        o_ref, rbuf, send1, recv1, send2, recv2 = refs[-6:]
        place = _mesh_place()
        me = _index(place)
        rbuf[pl.ds(me, 1)] = p_ref[pl.ds(me, 1)]
        first = [pltpu.make_async_remote_copy(src_ref=p_ref.at[_index(_peer(place, k))], dst_ref=rbuf.at[me],
                                              send_sem=send1.at[k - 1], recv_sem=recv1.at[k - 1],
                                              device_id=_peer(place, k), device_id_type=MESH)
                 for k in range(1, N_DEV)]
        for cp in first:
            cp.start()
        for cp in first:
            cp.wait()
        acc = rbuf[0]
        for d in range(1, N_DEV):
            acc = acc + rbuf[d]
        o_ref[pl.ds(me, 1)] = acc[None]
        second = [pltpu.make_async_remote_copy(src_ref=o_ref.at[me], dst_ref=o_ref.at[me], send_sem=send2.at[k - 1],
                                               recv_sem=recv2.at[k - 1], device_id=_peer(place, k),
                                               device_id_type=MESH)
                  for k in range(1, N_DEV)]
        for cp in second:
            cp.start()
        for cp in second:
            cp.wait()

    vm = pl.BlockSpec(memory_space=pltpu.VMEM)
    ops = [part, *deps]
    return pl.pallas_call(
        body, in_specs=[vm] + [ANY] * len(deps), out_specs=vm, out_shape=S(part.shape, F32),
        scratch_shapes=[pltpu.VMEM(part.shape, F32)] + [pltpu.SemaphoreType.DMA((N_DEV - 1,))] * 4,
        compiler_params=pltpu.CompilerParams(has_side_effects=True, vmem_limit_bytes=VMEM_LIMIT), name=name)(*ops)


def _adamw(w, g, m, v):
    m = ADAM_B1 * m + (1.0 - ADAM_B1) * g
    v = ADAM_B2 * v + (1.0 - ADAM_B2) * (g * g)
    m_hat = m / (1.0 - ADAM_B1 ** ADAM_STEP)
    v_hat = v / (1.0 - ADAM_B2 ** ADAM_STEP)
    return -ADAM_LR * (m_hat / (jnp.sqrt(v_hat) + ADAM_EPS) + ADAM_WD * w), m, v


def _adam_big(parts, own, me, w, m, v, name):
    nl, a, b = w.shape
    ta = _shard_rows_tile(a)

    def body(me_ref, p_ref, *refs):
        own_refs, (w_ref, m_ref, v_ref, g_ref, d_ref, mo_ref, vo_ref) = refs[:nl], refs[nl:]
        layer = pl.program_id(0)
        mine = own_refs[0][...]
        for k in range(1, nl):
            mine = jnp.where(layer == k, own_refs[k][...], mine)
        g = None
        for s in range(N_DEV):
            term = jnp.where(me_ref[0] == s, mine, p_ref[s]).astype(F32)
            g = term if g is None else g + term
        g_ref[...] = g
        d_ref[...], mo_ref[...], vo_ref[...] = _adamw(w_ref[...], g, m_ref[...], v_ref[...])

    blk = pl.BlockSpec((None, ta, b), lambda l, i, me_ref: (l, i, 0))

    def own_spec(k):
        return pl.BlockSpec((None, ta, b), lambda l, i, me_ref: (me_ref[0], jnp.where(l == k, i, 0), 0))

    grid_spec = pltpu.PrefetchScalarGridSpec(
        num_scalar_prefetch=1, grid=(nl, a // ta),
        in_specs=[pl.BlockSpec((N_DEV, None, ta, b), lambda l, i, me_ref: (0, l, i, 0))]
        + [own_spec(k) for k in range(nl)] + [blk, blk, blk],
        out_specs=[blk] * 4)
    return pl.pallas_call(body, grid_spec=grid_spec, out_shape=[S(w.shape, F32)] * 4,
                          compiler_params=_cp("arbitrary", "arbitrary"), name=name)(me, parts, *own, w, m, v)


def _adam_small(gs, ws, ms, vs):
    n = len(gs)

    def body(*refs):
        ins, outs = refs[:4 * n], refs[4 * n:]
        for i in range(n):
            g_ref, w_ref, m_ref, v_ref = (ins[k * n + i] for k in range(4))
            outs[i][...], outs[n + i][...], outs[2 * n + i][...] = _adamw(w_ref[...], g_ref[...], m_ref[...], v_ref[...])

    out = pl.pallas_call(body, out_shape=[S(g.shape, F32) for g in gs] * 3, compiler_params=_cp(),
                         name="adam_small")(*gs, *ws, *ms, *vs)
    return out[:n], out[n:2 * n], out[2 * n:]


BIG = ("even_w_in", "even_w_out", "mla_w_down", "mla_w_qb", "mla_w_kvb", "mla_w_o", "mlp_w1", "mlp_w2")
BIG_KEY = dict(even_w_in="win", even_w_out="wout", mla_w_down="wdown", mla_w_qb="wqb", mla_w_kvb="wkvb",
               mla_w_o="wo", mlp_w1="w1", mlp_w2="w2")
SMALL = (("ln_mix_g", "ln_mix_g", None), ("ln_mix_b", "ln_mix_b", None), ("ln_ffn_g", "ln_ffn_g", None),
         ("ln_ffn_b", "ln_ffn_b", None), ("pool_w", "pool_w", None), ("pool_scale", "pool_scale", None),
         ("lru_conv_w", "conv_w", 2), ("lru_conv_b", "conv_b", None), ("lru_w_a", "w_a", None),
         ("lru_b_a", "b_a", None), ("lru_w_x", "w_x", None), ("lru_b_x", "b_x", None), ("lru_lambda", "lam", None),
         ("mla_q_norm_g", "gq", 1), ("mla_kv_norm_g", "gkv", 1))
WEIGHTS = ("ln_mix_g", "ln_mix_b", "ln_ffn_g", "ln_ffn_b", "even_w_in", "pool_w", "pool_scale", "lru_conv_w",
           "lru_conv_b", "lru_w_a", "lru_b_a", "lru_w_x", "lru_b_x", "lru_lambda", "even_w_out", "mla_w_down",
           "mla_q_norm_g", "mla_kv_norm_g", "mla_w_qb", "mla_w_kvb", "mla_w_o", "mlp_w1", "mlp_w2")
ALL_AXES = ("x", "y", "c")


def _layer_weights(l):
    j = l // 2
    if l % 2 == 0:
        mixer = [("win", "even_w_in", j), ("wout", "even_w_out", j)]
    else:
        mixer = [("wdown", "mla_w_down", j), ("wqb", "mla_w_qb", j), ("wkvb", "mla_w_kvb", j), ("wo", "mla_w_o", j)]
    return mixer + [("w1", "mlp_w1", l), ("w2", "mlp_w2", l)]


def _pack(arrays, multiple):
    flat = jnp.concatenate([a.reshape(-1) for a in arrays])
    pad = (-flat.shape[0]) % multiple
    return jnp.pad(flat, (0, pad))


def _unpack(flat, shapes):
    out, at = [], 0
    for shp in shapes:
        n = 1
        for s in shp:
            n *= s
        out.append(flat[at:at + n].reshape(shp))
        at += n
    return out


def _global_shape(local_shape, axis):
    if axis is None:
        return tuple(local_shape)
    return tuple(s * N_DEV if i == axis else s for i, s in enumerate(local_shape))


def _step(x, positions, tgt, w, m, v):
    t = x.shape[1]
    me = _index(_mesh_place())

    chunk = N_DEV * 8 * 128
    me_arr = me.astype(jnp.int32).reshape(1)

    lanes = lambda a: jnp.pad(a, ((0, 0), (0, 128 - a.shape[1])))
    mine_packed = jnp.concatenate([w["lru_conv_w"].reshape(8, HEAD), lanes(w["mla_q_norm_g"]),
                                   lanes(w["mla_kv_norm_g"]), jnp.zeros((4, 128), F32)])
    g_send, g_recv, _, g_land, token = _exchange_start([], _make_zones([mine_packed], me_arr, "zones_small", F32),
                                                       GATHER_ALL, "small_params_start")

    def keys_of(l, part):
        keys = [key for key, _, _ in _layer_weights(l)]
        if l == 0:
            return keys[:1] if part == 0 else keys[1:]
        return keys if part == 0 else []

    shard_of = {(l, key): (w[name][i].T if key == "win" else w[name][i])
                for l in range(DEPTH) for key, name, i in _layer_weights(l)}
    flights, after = {}, (token,)
    for l in range(DEPTH):
        for part in (0, 1):
            if keys_of(l, part):
                zones = _make_zones([shard_of[l, key] for key in keys_of(l, part)], me_arr, "zones_%d_%d" % (l, part))
                send, recv, _, lands, token = _exchange_start([], zones, GATHER_ICI, "gather_start_%d_%d" % (l, part),
                                                              after=after)
                flights[l, part] = (send, recv, [], lands)
                after = (token,)

    _, g_land = _exchange_wait(g_send, g_recv, [], g_land, GATHER_ALL, token, "small_params_wait")
    rows_first = g_land[0].transpose(1, 0, 2)
    q_shard, kv_shard = w["mla_q_norm_g"].shape[1], w["mla_kv_norm_g"].shape[1]
    full = dict(lru_conv_w=rows_first[:8].reshape(2, 4, LRU_W),
                mla_q_norm_g=rows_first[8:10, :, :q_shard].reshape(2, Q_RANK),
                mla_kv_norm_g=rows_first[10:12, :, :kv_shard].reshape(2, KV_RANK))

    passing = {}

    def pass_on(l, part, after):
        tag = "%d_%d" % (l, part)
        _, lands = _exchange_wait(*flights[l, part], GATHER_ICI, after, "gather_wait_" + tag)
        send, recv, _, lands, token = _exchange_start([], lands, GATHER_D2D, "gather_pass_" + tag)
        passing[l, part] = (send, recv, [], lands)
        return token

    def early_pass(l, after):
        return pass_on(l, 0, after) if l >= 2 else None

    def weights_of(l, part, after):
        keys = keys_of(l, part)
        if keys:
            if (l, part) not in passing:
                pass_on(l, part, after)
            _, arrays = _exchange_wait(*passing[l, part], GATHER_D2D, after, "gather_pass_wait_%d_%d" % (l, part))
        big = dict(zip(keys, arrays)) if keys else {}
        if "win" in big:
            big["win_t"] = big["win"].reshape(EVEN_IN, D)
        if "wout" in big:
            big["wout2d"] = big["wout"].reshape(EVEN_MIX, D)
        if "wdown" in big:
            big["wdown2d"] = big["wdown"].reshape(D, ODD_IN)
        return big

    zone = {name: lax.empty((N_DEV,) + w[name].shape, BF16) for name in BIG}
    name_of = {key: name for name, key in BIG_KEY.items()}
    sent, last_token = [], [None]

    def grads_done(l, grads):
        keys = list(grads)
        index = {key: i for key, _, i in _layer_weights(l)}
        layers = [index[key] for key in keys]
        send, recv, srcs, lands, tok = _exchange_start([grads[k] for k in keys], [zone[name_of[k]] for k in keys],
                                                       _scatter_plan(layers), "scatter_start_%d_%s" % (l, keys[0]))
        for k, land in zip(keys, lands):
            zone[name_of[k]] = land
        sent.append((send, recv, srcs, keys, layers))
        last_token[0] = tok
        return tok

    row3 = lambda a: a.reshape(a.shape[0], 1, a.shape[1])
    small = dict(ln_mix_g=row3(w["ln_mix_g"]), ln_mix_b=row3(w["ln_mix_b"]), ln_ffn_g=row3(w["ln_ffn_g"]),
                 ln_ffn_b=row3(w["ln_ffn_b"]), pool_w=w["pool_w"], pool_scale=row3(w["pool_scale"]),
                 conv_w=full["lru_conv_w"], conv_b=row3(w["lru_conv_b"]), w_a=w["lru_w_a"], b_a=row3(w["lru_b_a"]),
                 w_x=w["lru_w_x"], b_x=row3(w["lru_b_x"]), lam=row3(w["lru_lambda"]),
                 gq=row3(full["mla_q_norm_g"]), gkv=row3(full["mla_kv_norm_g"]))

    loss_part, grad_x, g = _local_step(x[0], positions.reshape(t, 1), tgt[0], small, weights_of, grads_done,
                                       start_dep=token, prefetch=early_pass)

    own = {name: [None] * w[name].shape[0] for name in BIG}
    me_arr = me.astype(jnp.int32).reshape(1)
    out = {}
    local_g = [jnp.stack(g[key]).reshape(_global_shape(w[name].shape, axis)) for name, key, axis in SMALL]
    local_g.append(loss_part.reshape(1))
    part = _pack(local_g, chunk).reshape(N_DEV, -1, 128)
    small_plan = _scatter_plan([None])
    s_send, s_recv, s_src, s_land, after = _exchange_start([part], [lax.empty(part.shape, F32)], small_plan,
                                                           "small_scatter_start", after=(last_token[0],))
    for n_flight, (send, recv, srcs, keys, layers) in enumerate(sent):
        if n_flight == len(sent) - 1:
            for name in BIG:
                if BIG_KEY[name] not in keys:
                    out[name] = _adam_big(zone[name], own[name], me_arr, w[name], m[name], v[name], "adam_" + name)
            s_src, s_land = _exchange_wait(s_send, s_recv, s_src, s_land, small_plan,
                                           [grad_x] + [o[0] for o in out.values()], "small_scatter_wait")
            chunk_sum = _sum_blocks(s_land[0], s_src[0], me_arr)
            r_zone = lax.dynamic_update_slice_in_dim(lax.empty(part.shape, F32), chunk_sum[None], me, 0)
            r_send, r_recv, _, r_land, after = _exchange_start([], [r_zone], GATHER_ALL, "small_gather_start")
        srcs, lands = _exchange_wait(send, recv, srcs, [zone[name_of[k]] for k in keys], _scatter_plan(layers),
                                     after, "scatter_wait_%d" % n_flight)
        for k, land, src, layer in zip(keys, lands, srcs, layers):
            zone[name_of[k]] = land
            own[name_of[k]][layer] = src
        after = lands[0]
    for name in BIG:
        if name not in out:
            out[name] = _adam_big(zone[name], own[name], me_arr, w[name], m[name], v[name], "adam_" + name)

    _, reduced = _exchange_wait(r_send, r_recv, [], r_land, GATHER_ALL, [out[name][0] for name in BIG],
                                "small_gather_wait")
    reduced = _unpack(reduced[0].reshape(-1), [a.shape for a in local_g])
    loss = reduced[-1][0]
    mine = [a if axis is None else lax.dynamic_slice_in_dim(a, me * w[name].shape[axis], w[name].shape[axis], axis)
            for a, (name, _, axis) in zip(reduced, SMALL)]
    names = [name for name, _, _ in SMALL]
    as_2d = lambda a: a.reshape(-1, a.shape[-1])
    new = _adam_small([as_2d(a) for a in mine], *([as_2d(src[name]) for name in names] for src in (w, m, v)))
    for i, name in enumerate(names):
        out[name] = (mine[i],) + tuple(part[i].reshape(w[name].shape) for part in new)

    return (loss, grad_x[None]) + tuple(out[name][i] for i in range(4) for name in WEIGHTS)


def kernel(x, positions, ln_mix_g, ln_mix_b, ln_ffn_g, ln_ffn_b, even_w_in, pool_w, pool_scale, lru_conv_w, lru_conv_b, lru_w_a, lru_b_a, lru_w_x, lru_b_x, lru_lambda, even_w_out, mla_w_down, mla_q_norm_g, mla_kv_norm_g, mla_w_qb, mla_w_kvb, mla_w_o, mlp_w1, mlp_w2, loss_target, m_ln_mix_g, m_ln_mix_b, m_ln_ffn_g, m_ln_ffn_b, m_even_w_in, m_pool_w, m_pool_scale, m_lru_conv_w, m_lru_conv_b, m_lru_w_a, m_lru_b_a, m_lru_w_x, m_lru_b_x, m_lru_lambda, m_even_w_out, m_mla_w_down, m_mla_q_norm_g, m_mla_kv_norm_g, m_mla_w_qb, m_mla_w_kvb, m_mla_w_o, m_mlp_w1, m_mlp_w2, v_ln_mix_g, v_ln_mix_b, v_ln_ffn_g, v_ln_ffn_b, v_even_w_in, v_pool_w, v_pool_scale, v_lru_conv_w, v_lru_conv_b, v_lru_w_a, v_lru_b_a, v_lru_w_x, v_lru_b_x, v_lru_lambda, v_even_w_out, v_mla_w_down, v_mla_q_norm_g, v_mla_kv_norm_g, v_mla_w_qb, v_mla_w_kvb, v_mla_w_o, v_mlp_w1, v_mlp_w2):
    w = dict(zip(WEIGHTS, (ln_mix_g, ln_mix_b, ln_ffn_g, ln_ffn_b, even_w_in, pool_w, pool_scale, lru_conv_w,
                           lru_conv_b, lru_w_a, lru_b_a, lru_w_x, lru_b_x, lru_lambda, even_w_out, mla_w_down,
                           mla_q_norm_g, mla_kv_norm_g, mla_w_qb, mla_w_kvb, mla_w_o, mlp_w1, mlp_w2)))
    m = dict(zip(WEIGHTS, (m_ln_mix_g, m_ln_mix_b, m_ln_ffn_g, m_ln_ffn_b, m_even_w_in, m_pool_w, m_pool_scale,
                           m_lru_conv_w, m_lru_conv_b, m_lru_w_a, m_lru_b_a, m_lru_w_x, m_lru_b_x, m_lru_lambda,
                           m_even_w_out, m_mla_w_down, m_mla_q_norm_g, m_mla_kv_norm_g, m_mla_w_qb, m_mla_w_kvb,
                           m_mla_w_o, m_mlp_w1, m_mlp_w2)))
    v = dict(zip(WEIGHTS, (v_ln_mix_g, v_ln_mix_b, v_ln_ffn_g, v_ln_ffn_b, v_even_w_in, v_pool_w, v_pool_scale,
                           v_lru_conv_w, v_lru_conv_b, v_lru_w_a, v_lru_b_a, v_lru_w_x, v_lru_b_x, v_lru_lambda,
                           v_even_w_out, v_mla_w_down, v_mla_q_norm_g, v_mla_kv_norm_g, v_mla_w_qb, v_mla_w_kvb,
                           v_mla_w_o, v_mlp_w1, v_mlp_w2)))
    return _step(x, positions, loss_target, w, m, v)
```

```python
import jax
import jax.numpy as jnp
from jax import lax
from jax.experimental import pallas as pl
from jax.experimental.pallas import tpu as pltpu

F32 = jnp.float32
BF16 = jnp.bfloat16
S = jax.ShapeDtypeStruct

D = 1024
DEPTH = 4
N_DEV = 8
CHUNK_SHIFT = 6
POOL_WINDOWS = (2, 4, 8, 16)
POOL_W = 512
LRU_W = 1024
LRU_HEADS = 8
HEAD = 128
LRU_C = 8.0
EVEN_IN = 2560
EVEN_MIX = 1536
MLA_HEADS = 8
NOPE = 128
ROPE = 64
VDIM = 128
Q_RANK = 384
KV_RANK = 256
ODD_IN = 704
D_FF = 4096
FF_BLK = D_FF // N_DEV
ROPE_THETA = 10000.0
ALPHA = (2 * DEPTH) ** 0.25
LN_EPS = 1e-5
RMS_EPS = 1e-6
ATT_SCALE = (NOPE + ROPE) ** -0.5
NEG = float(jnp.finfo(jnp.float32).min)
ADAM_LR = 0.001
ADAM_B1 = 0.9
ADAM_B2 = 0.999
ADAM_EPS = 1e-08
ADAM_WD = 0.01
ADAM_STEP = 10
V7X_VMEM_BYTES = 64 * 1024 * 1024
VMEM_LIMIT = V7X_VMEM_BYTES - 8 * 1024 * 1024
MESH = pl.DeviceIdType.MESH


def _cp(*sem):
    return pltpu.CompilerParams(dimension_semantics=sem if sem else None, vmem_limit_bytes=VMEM_LIMIT)


def _dot(a, b):
    return jnp.dot(a, b, preferred_element_type=F32)


def _dot_nt(a, b):
    return lax.dot_general(a, b, (((1,), (1,)), ((), ())), preferred_element_type=F32)


def _dot_tn(a, b):
    return lax.dot_general(a, b, (((0,), (0,)), ((), ())), preferred_element_type=F32)


def _full(shape):
    return pl.BlockSpec(shape, lambda *_: (0,) * len(shape))


def _mm(a, b, *, mode, grid, a_spec, b_spec, out_shape, out_spec, name, add=None, add_spec=None, add_scale=1.0,
        dep=None):
    dot = {"nn": _dot, "nt": _dot_nt, "tn": _dot_tn}[mode]

    def body(*refs):
        a_ref, b_ref, o_ref = refs[0], refs[1], refs[-1]
        acc = dot(a_ref[...].astype(BF16), b_ref[...].astype(BF16))
        if add is not None:
            acc = acc + add_scale * refs[2][...]
        o_ref[...] = acc.astype(o_ref.dtype)

    ops = [a, b] if add is None else [a, b, add]
    specs = [a_spec, b_spec] if add is None else [a_spec, b_spec, add_spec]
    if dep is not None:
        ops.append(dep)
        specs.append(pl.BlockSpec(memory_space=pl.ANY))
    return pl.pallas_call(body, grid=grid, in_specs=specs, out_specs=out_spec, out_shape=out_shape,
                          compiler_params=_cp(*(("parallel",) * len(grid))), name=name)(*ops)


def _ln_stats(z):
    mu = jnp.mean(z, axis=-1, keepdims=True)
    zc = z - mu
    var = jnp.mean(zc * zc, axis=-1, keepdims=True)
    rstd = lax.rsqrt(var + LN_EPS)
    return zc * rstd, rstd


def _row_tile(t):
    return min(1024, t)


def _resid_ln(x, mix, g3, b3, l, name):
    t = x.shape[0]
    bm = _row_tile(t)

    def body(x_ref, m_ref, g_ref, b_ref, z_ref, y_ref, yb_ref):
        z = ALPHA * x_ref[...] + m_ref[...]
        xh, _ = _ln_stats(z)
        y = xh * g_ref[...] + b_ref[...]
        z_ref[...] = z
        y_ref[...] = y
        yb_ref[...] = y.astype(BF16)

    row = pl.BlockSpec((bm, D), lambda i: (i, 0))
    vec = pl.BlockSpec((None, 1, D), lambda i: (l, 0, 0))
    return pl.pallas_call(body, grid=(t // bm,), in_specs=[row, row, vec, vec], out_specs=[row, row, row],
                          out_shape=[S((t, D), F32), S((t, D), F32), S((t, D), BF16)],
                          compiler_params=_cp("parallel"), name=name)(x, mix, g3, b3)


def _proj_resid_ln(x, a, wmat, g3, b3, l, name):
    t, k = a.shape
    bm = _row_tile(t)

    def body(x_ref, a_ref, w_ref, g_ref, b_ref, z_ref, y_ref, yb_ref):
        z = ALPHA * x_ref[...] + _dot(a_ref[...], w_ref[...])
        xh, _ = _ln_stats(z)
        y = xh * g_ref[...] + b_ref[...]
        z_ref[...] = z
        y_ref[...] = y
        yb_ref[...] = y.astype(BF16)

    row = pl.BlockSpec((bm, D), lambda i: (i, 0))
    vec = pl.BlockSpec((None, 1, D), lambda i: (l, 0, 0))
    return pl.pallas_call(body, grid=(t // bm,),
                          in_specs=[row, pl.BlockSpec((bm, k), lambda i: (i, 0)), _full((k, D)), vec, vec],
                          out_specs=[row, row, row], out_shape=[S((t, D), F32), S((t, D), F32), S((t, D), BF16)],
                          compiler_params=_cp("parallel"), name=name)(x, a, wmat, g3, b3)


def _ln_bwd(d, z, g3, l, name, r=None, dep=None):
    t = z.shape[0]
    bm = _row_tile(t)

    def body(*refs):
        refs = list(refs)
        d_ref = refs.pop(0)
        dy = d_ref[...]
        if r is not None:
            dy = dy + ALPHA * refs.pop(0)[...]
        z_ref, g_ref = refs.pop(0), refs.pop(0)
        if dep is not None:
            refs.pop(0)
        dz_ref, dzb_ref, dg_ref, db_ref = refs
        xh, rstd = _ln_stats(z_ref[...])
        dyg = dy * g_ref[...]
        m1 = jnp.mean(dyg, axis=-1, keepdims=True)
        m2 = jnp.mean(dyg * xh, axis=-1, keepdims=True)
        dz = rstd * (dyg - m1 - xh * m2)
        dz_ref[...] = dz
        dzb_ref[...] = dz.astype(BF16)

        @pl.when(pl.program_id(0) == 0)
        def _():
            dg_ref[...] = jnp.zeros_like(dg_ref)
            db_ref[...] = jnp.zeros_like(db_ref)

        dg_ref[...] += jnp.sum(dy * xh, axis=0, keepdims=True)
        db_ref[...] += jnp.sum(dy, axis=0, keepdims=True)

    row = pl.BlockSpec((bm, D), lambda i: (i, 0))
    vec = pl.BlockSpec((None, 1, D), lambda i: (l, 0, 0))
    acc = pl.BlockSpec((1, D), lambda i: (0, 0))
    ops = [d, z, g3] if r is None else [d, r, z, g3]
    specs = [row, row, vec] if r is None else [row, row, row, vec]
    if dep is not None:
        ops.append(dep)
        specs.append(_full(dep.shape))
    return pl.pallas_call(body, grid=(t // bm,), in_specs=specs, out_specs=[row, row, acc, acc],
                          out_shape=[S((t, D), F32), S((t, D), BF16), S((1, D), F32), S((1, D), F32)],
                          compiler_params=_cp("arbitrary"), name=name)(*ops)


def _loss_grad(y, tgt):
    t = y.shape[0]
    bm = _row_tile(t)

    def body(y_ref, t_ref, dy_ref, loss_ref, acc_ref):
        i = pl.program_id(0)
        e = y_ref[...] - t_ref[...]
        dy_ref[...] = e * (1.0 / D)

        @pl.when(i == 0)
        def _():
            acc_ref[...] = jnp.zeros_like(acc_ref)

        acc_ref[...] += jnp.sum(e * e, axis=0, keepdims=True)

        @pl.when(i == pl.num_programs(0) - 1)
        def _():
            loss_ref[...] = jnp.full(loss_ref.shape, (0.5 / D) * jnp.sum(acc_ref[...]), F32)

    row = pl.BlockSpec((bm, D), lambda i: (i, 0))
    return pl.pallas_call(body, grid=(t // bm,), in_specs=[row, row],
                          out_specs=[row, pl.BlockSpec((1, 128), lambda i: (0, 0))],
                          out_shape=[S((t, D), F32), S((1, 128), F32)],
                          scratch_shapes=[pltpu.VMEM((1, D), F32)],
                          compiler_params=_cp("arbitrary"), name="loss_grad")(y, tgt)


def _mlp_row_tile(t):
    return min(1024, t)


MLP_ROW_PARTS = 2


def _row_parts(bm):
    step = bm // MLP_ROW_PARTS
    return [slice(k * step, (k + 1) * step) for k in range(MLP_ROW_PARTS)]


def _mlp_fwd(y, yb, w1g, w2g, g3, b3, l, dep=None):
    t = yb.shape[0]
    bm = _mlp_row_tile(t)

    def body(*refs):
        y_ref, yb_ref, w1_ref, w2_ref, g_ref, b_ref = refs[:6]
        z_ref, o_ref, ob_ref, act_ref, acc_ref = refs[-5:]
        j = pl.program_id(1)

        @pl.when(j == 0)
        def _():
            acc_ref[...] = jnp.zeros_like(acc_ref)

        for rows in _row_parts(bm):
            h = jnp.maximum(_dot(yb_ref[rows, :], w1_ref[...]), 0.0)
            act = (h * h).astype(BF16)
            act_ref[rows, :] = act
            acc_ref[rows, :] += _dot(act, w2_ref[...])

        @pl.when(j == N_DEV - 1)
        def _():
            z = ALPHA * y_ref[...] + acc_ref[...]
            xh, _ = _ln_stats(z)
            out = xh * g_ref[...] + b_ref[...]
            z_ref[...] = z
            o_ref[...] = out
            ob_ref[...] = out.astype(BF16)

    row = pl.BlockSpec((bm, D), lambda i, j: (i, 0))
    vec = pl.BlockSpec((None, 1, D), lambda i, j: (l, 0, 0))
    deps = [] if dep is None else [dep]
    return pl.pallas_call(
        body, grid=(t // bm, N_DEV),
        in_specs=[row, row, pl.BlockSpec((None, D, FF_BLK), lambda i, j: (j, 0, 0)),
                  pl.BlockSpec((None, FF_BLK, D), lambda i, j: (j, 0, 0)), vec, vec] + [ANY] * len(deps),
        out_specs=[row, row, row, pl.BlockSpec((bm, FF_BLK), lambda i, j: (i, j))],
        out_shape=[S((t, D), F32), S((t, D), F32), S((t, D), BF16), S((t, D_FF), BF16)],
        scratch_shapes=[pltpu.VMEM((bm, D), F32)],
        compiler_params=_cp("parallel", "arbitrary"), name="mlp_fwd")(y, yb, w1g, w2g, g3, b3, *deps)


def _mlp_bwd_dh(act, dzb, w1g, w2g):
    t = act.shape[0]
    bm = _mlp_row_tile(t)

    def body(a_ref, dz_ref, w1_ref, w2_ref, dh_ref, acc_ref):
        @pl.when(pl.program_id(1) == 0)
        def _():
            acc_ref[...] = jnp.zeros_like(acc_ref)

        for rows in _row_parts(bm):
            r = jnp.sqrt(a_ref[rows, :].astype(F32))
            dh = (_dot_nt(dz_ref[rows, :], w2_ref[...]) * (2.0 * r)).astype(BF16)
            dh_ref[rows, :] = dh
            acc_ref[rows, :] += _dot_nt(dh, w1_ref[...])

    row = pl.BlockSpec((bm, D), lambda i, j: (i, 0))
    hid = pl.BlockSpec((bm, FF_BLK), lambda i, j: (i, j))
    return pl.pallas_call(
        body, grid=(t // bm, N_DEV),
        in_specs=[hid, row,
                  pl.BlockSpec((None, D, FF_BLK), lambda i, j: (j, 0, 0)),
                  pl.BlockSpec((None, FF_BLK, D), lambda i, j: (j, 0, 0))],
        out_specs=[hid, row],
        out_shape=[S((t, D_FF), BF16), S((t, D), F32)],
        compiler_params=_cp("parallel", "arbitrary"), name="mlp_bwd_dh")(act, dzb, w1g, w2g)


F32_SUBLANES = 8


def _shift_dn(x, k, rows, fill=0.0):
    if k % F32_SUBLANES == 0:
        return jnp.concatenate([jnp.full((k,) + x.shape[1:], fill, x.dtype), x[:x.shape[0] - k]], axis=0)
    return jnp.where(rows >= k, pltpu.roll(x, k, 0), fill)


def _shift_up(x, k, rows, fill=0.0):
    t = x.shape[0]
    if k % F32_SUBLANES == 0:
        return jnp.concatenate([x[k:], jnp.full((k,) + x.shape[1:], fill, x.dtype)], axis=0)
    return jnp.where(rows < t - k, pltpu.roll(x, t - k, 0), fill)


def _scan_rows(a, b, shift):
    rows = lax.broadcasted_iota(jnp.int32, a.shape, 0)
    k = 1
    t = a.shape[0]
    while k < t:
        b = a * shift(b, k, rows) + b
        if 2 * k < t:
            a = a * shift(a, k, rows, 1.0)
        k *= 2
    return b


def _scan_dn(a, b):
    return _scan_rows(a, b, _shift_dn)


def _scan_up(a, b):
    return _scan_rows(a, b, _shift_up)


def _window_sum_dn(x, w, rows):
    k = 1
    while k < w:
        x = x + _shift_dn(x, k, rows)
        k *= 2
    return x


def _window_sum_up(x, w, rows):
    k = 1
    while k < w:
        x = x + _shift_up(x, k, rows)
        k *= 2
    return x


def _pool_diff(u, w, rows):
    inv_count = 1.0 / jnp.minimum(rows + 1, w).astype(F32)
    return _window_sum_dn(u, w, rows) * inv_count - u, inv_count


def _pool_fwd(proj, pool_w, pool_scale3, j):
    t = proj.shape[0]

    def body(u_ref, w_ref, s_ref, y_ref):
        rows = lax.broadcasted_iota(jnp.int32, (t, HEAD), 0)
        for g, w in enumerate(POOL_WINDOWS):
            cols = slice(g * HEAD, (g + 1) * HEAD)
            d, _ = _pool_diff(u_ref[:, cols], w, rows)
            y = _dot(d.astype(BF16), w_ref[g].astype(BF16)) * s_ref[:, cols]
            y_ref[:, cols] = y.astype(BF16)

    return pl.pallas_call(
        body, grid=(1,),
        in_specs=[pl.BlockSpec((t, POOL_W), lambda i: (0, 0)),
                  pl.BlockSpec((None, 4, HEAD, HEAD), lambda i: (j, 0, 0, 0)),
                  pl.BlockSpec((None, 1, POOL_W), lambda i: (j, 0, 0))],
        out_specs=pl.BlockSpec((t, POOL_W), lambda i: (0, 0)),
        out_shape=S((t, POOL_W), BF16), compiler_params=_cp("arbitrary"), name="pool_fwd")(proj, pool_w, pool_scale3)


def _pool_bwd(proj, dycat, pool_w, pool_scale3, j):
    t = proj.shape[0]

    def body(u_ref, dy_ref, w_ref, s_ref, du_ref, dw_ref, ds_ref):
        rows = lax.broadcasted_iota(jnp.int32, (t, HEAD), 0)
        for g, w in enumerate(POOL_WINDOWS):
            cols = slice(g * HEAD, (g + 1) * HEAD)
            d, inv_count = _pool_diff(u_ref[:, cols], w, rows)
            db = d.astype(BF16)
            wg = w_ref[g].astype(BF16)
            dy = dy_ref[:, cols]
            ds_ref[:, cols] = jnp.sum(dy * _dot(db, wg), axis=0, keepdims=True)
            dzz = (dy * s_ref[:, cols]).astype(BF16)
            dw_ref[g] = _dot_tn(db, dzz)
            dd = _dot_nt(dzz, wg)
            du_ref[:, cols] = (_window_sum_up(dd * inv_count, w, rows) - dd).astype(BF16)

    return pl.pallas_call(
        body, grid=(1,),
        in_specs=[pl.BlockSpec((t, POOL_W), lambda i: (0, 0)),
                  pl.BlockSpec((t, POOL_W), lambda i: (0, 0)),
                  pl.BlockSpec((None, 4, HEAD, HEAD), lambda i: (j, 0, 0, 0)),
                  pl.BlockSpec((None, 1, POOL_W), lambda i: (j, 0, 0))],
        out_specs=[pl.BlockSpec((t, POOL_W), lambda i: (0, 0)), _full((4, HEAD, HEAD)), _full((1, POOL_W))],
        out_shape=[S((t, POOL_W), BF16), S((4, HEAD, HEAD), F32), S((1, POOL_W), F32)],
        compiler_params=_cp("arbitrary"), name="pool_bwd")(proj, dycat, pool_w, pool_scale3)


GELU_C = 0.7978845608028654
GELU_K = 0.044715


def _gelu(x):
    th = jnp.tanh(GELU_C * (x + GELU_K * x * x * x))
    return 0.5 * x * (1.0 + th), th


def _lru_forward(u, gate, cw, cb, wa, ba, wx, bx, lam, rows):
    v = cw[3:4] * u + cw[2:3] * _shift_dn(u, 1, rows) + cw[1:2] * _shift_dn(u, 2, rows) \
        + cw[0:1] * _shift_dn(u, 3, rows) + cb
    vb = v.astype(BF16)
    r = jax.nn.sigmoid(_dot(vb, wa) + ba)
    i = jax.nn.sigmoid(_dot(vb, wx) + bx)
    sp = jnp.maximum(-lam, 0.0) + jnp.log1p(jnp.exp(-jnp.abs(lam)))
    log_a = (-LRU_C) * r * sp
    a = jnp.exp(log_a)
    one_m_a2 = -jnp.tanh(log_a) * (a * a + 1.0)
    mult = jnp.sqrt(one_m_a2)
    h = _scan_dn(a, mult * (i * v))
    gl, th = _gelu(gate)
    return dict(v=v, vb=vb, r=r, i=i, sp=sp, a=a, mult=mult, h=h, gl=gl, th=th)


def _lru_specs(t, j, col0_u, col0_g):
    blk = lambda c0: pl.BlockSpec((t, HEAD), lambda h: (0, c0 + h))
    vec = pl.BlockSpec((None, 1, HEAD), lambda h: (j, 0, h))
    return [blk(col0_u), blk(col0_g),
            pl.BlockSpec((None, 4, HEAD), lambda h: (j, 0, h)), vec,
            pl.BlockSpec((None, None, HEAD, HEAD), lambda h: (j, h, 0, 0)), vec,
            pl.BlockSpec((None, None, HEAD, HEAD), lambda h: (j, h, 0, 0)), vec, vec]


def _lru_fwd(proj, p, j):
    t = proj.shape[0]

    def body(u_ref, g_ref, cw_ref, cb_ref, wa_ref, ba_ref, wx_ref, bx_ref, lam_ref, y_ref):
        rows = lax.broadcasted_iota(jnp.int32, (t, HEAD), 0)
        f = _lru_forward(u_ref[...], g_ref[...], cw_ref[...], cb_ref[...], wa_ref[...].astype(BF16), ba_ref[...],
                         wx_ref[...].astype(BF16), bx_ref[...], lam_ref[...], rows)
        y_ref[...] = (f["h"] * f["gl"]).astype(BF16)

    return pl.pallas_call(
        body, grid=(LRU_HEADS,), in_specs=_lru_specs(t, j, POOL_W // HEAD, (POOL_W + LRU_W) // HEAD),
        out_specs=pl.BlockSpec((t, HEAD), lambda h: (0, h)), out_shape=S((t, LRU_W), BF16),
        compiler_params=_cp("parallel"), name="lru_fwd")(
            proj, proj, p["conv_w"], p["conv_b"], p["w_a"], p["b_a"], p["w_x"], p["b_x"], p["lam"])


def _lru_bwd(proj, dycat, p, j):
    t = proj.shape[0]

    def body(u_ref, g_ref, cw_ref, cb_ref, wa_ref, ba_ref, wx_ref, bx_ref, lam_ref, dy_ref,
             du_ref, dgate_ref, dcw_ref, dcb_ref, dwa_ref, dba_ref, dwx_ref, dbx_ref, dlam_ref):
        rows = lax.broadcasted_iota(jnp.int32, (t, HEAD), 0)
        u = u_ref[...]
        gate = g_ref[...]
        cw = cw_ref[...]
        wa = wa_ref[...].astype(BF16)
        wx = wx_ref[...].astype(BF16)
        lam = lam_ref[...]
        f = _lru_forward(u, gate, cw, cb_ref[...], wa, ba_ref[...], wx, bx_ref[...], lam, rows)
        v, r, i, a, mult, h, th = f["v"], f["r"], f["i"], f["a"], f["mult"], f["h"], f["th"]
        dy = dy_ref[...]
        dgl = 0.5 * (1.0 + th) + 0.5 * gate * (1.0 - th * th) * GELU_C * (1.0 + 3.0 * GELU_K * gate * gate)
        dgate_ref[...] = (dy * h * dgl).astype(BF16)
        g = _scan_up(_shift_up(a, 1, rows), dy * f["gl"])
        da = g * _shift_dn(h, 1, rows)
        iv = i * v
        dmult = g * iv
        di = g * mult * v
        dv = g * mult * i
        dlog_a = da * a - dmult * (a * a) / mult
        dr = dlog_a * (-LRU_C) * f["sp"]
        dsp = jnp.sum(dlog_a * (-LRU_C) * r, axis=0, keepdims=True)
        dlam_ref[...] = -dsp * jax.nn.sigmoid(-lam)
        dpa = dr * r * (1.0 - r)
        dpx = di * i * (1.0 - i)
        dpab = dpa.astype(BF16)
        dpxb = dpx.astype(BF16)
        dwa_ref[...] = _dot_tn(f["vb"], dpab)
        dwx_ref[...] = _dot_tn(f["vb"], dpxb)
        dba_ref[...] = jnp.sum(dpa, axis=0, keepdims=True)
        dbx_ref[...] = jnp.sum(dpx, axis=0, keepdims=True)
        dv = dv + _dot_nt(dpab, wa) + _dot_nt(dpxb, wx)
        dcb_ref[...] = jnp.sum(dv, axis=0, keepdims=True)
        du = cw[3:4] * dv
        dcw_ref[3:4, :] = jnp.sum(dv * u, axis=0, keepdims=True)
        for k in (1, 2, 3):
            du = du + cw[3 - k:4 - k] * _shift_up(dv, k, rows)
            dcw_ref[3 - k:4 - k, :] = jnp.sum(dv * _shift_dn(u, k, rows), axis=0, keepdims=True)
        du_ref[...] = du.astype(BF16)

    blk = pl.BlockSpec((t, HEAD), lambda h: (0, h))
    vec = pl.BlockSpec((1, HEAD), lambda h: (0, h))
    mat = pl.BlockSpec((None, HEAD, HEAD), lambda h: (h, 0, 0))
    return pl.pallas_call(
        body, grid=(LRU_HEADS,),
        in_specs=_lru_specs(t, j, POOL_W // HEAD, (POOL_W + LRU_W) // HEAD)
        + [pl.BlockSpec((t, HEAD), lambda h: (0, POOL_W // HEAD + h))],
        out_specs=[blk, blk, pl.BlockSpec((4, HEAD), lambda h: (0, h)), vec, mat, vec, mat, vec, vec],
        out_shape=[S((t, LRU_W), BF16), S((t, LRU_W), BF16), S((4, LRU_W), F32), S((1, LRU_W), F32),
                   S((LRU_HEADS, HEAD, HEAD), F32), S((1, LRU_W), F32),
                   S((LRU_HEADS, HEAD, HEAD), F32), S((1, LRU_W), F32), S((1, LRU_W), F32)],
        compiler_params=_cp("parallel"), name="lru_bwd")(
            proj, proj, p["conv_w"], p["conv_b"], p["w_a"], p["b_a"], p["w_x"], p["b_x"], p["lam"], dycat)


def _rope(x, c, s):
    x1 = x[:, :ROPE // 2]
    x2 = x[:, ROPE // 2:]
    return jnp.concatenate([x1 * c - x2 * s, x1 * s + x2 * c], axis=-1)


def _rope_t(d, c, s):
    d1 = d[:, :ROPE // 2]
    d2 = d[:, ROPE // 2:]
    return jnp.concatenate([d1 * c + d2 * s, d2 * c - d1 * s], axis=-1)


def _rope_tables(pos2, inv_freq):
    t = pos2.shape[0]

    def body(p_ref, f_ref, c_ref, s_ref):
        ang = p_ref[...].astype(F32) * f_ref[...]
        c_ref[...] = jnp.cos(ang)
        s_ref[...] = jnp.sin(ang)

    return pl.pallas_call(body, out_shape=[S((t, ROPE // 2), F32), S((t, ROPE // 2), F32)],
                          name="rope_tables")(pos2, inv_freq)


def _down_norm(xb, wdown_g, gq3, gkv3, cos, sin, j):
    t = xb.shape[0]
    bm = _row_tile(t)

    def body(x_ref, w_ref, gq_ref, gkv_ref, c_ref, s_ref, down_ref, cq_ref, ckv_ref, kpe_ref):
        w = w_ref[...].reshape(D, ODD_IN)
        down = _dot(x_ref[...], w)
        down_ref[...] = down
        q = down[:, :Q_RANK]
        cq_ref[...] = (q * lax.rsqrt(jnp.mean(q * q, axis=-1, keepdims=True) + RMS_EPS) * gq_ref[...]).astype(BF16)
        kv = down[:, Q_RANK:Q_RANK + KV_RANK]
        ckv_ref[...] = (kv * lax.rsqrt(jnp.mean(kv * kv, axis=-1, keepdims=True) + RMS_EPS)
                        * gkv_ref[...]).astype(BF16)
        kpe_ref[...] = _rope(down[:, Q_RANK + KV_RANK:], c_ref[...], s_ref[...])

    row = lambda n: pl.BlockSpec((bm, n), lambda i: (i, 0))
    return pl.pallas_call(
        body, grid=(t // bm,),
        in_specs=[row(D), _full((N_DEV, D // N_DEV, ODD_IN)),
                  pl.BlockSpec((None, 1, Q_RANK), lambda i: (j, 0, 0)),
                  pl.BlockSpec((None, 1, KV_RANK), lambda i: (j, 0, 0)), row(ROPE // 2), row(ROPE // 2)],
        out_specs=[row(ODD_IN), row(Q_RANK), row(KV_RANK), row(ROPE)],
        out_shape=[S((t, ODD_IN), F32), S((t, Q_RANK), BF16), S((t, KV_RANK), BF16), S((t, ROPE), F32)],
        compiler_params=_cp("parallel"), name="down_norm")(xb, wdown_g, gq3, gkv3, cos, sin)


def _q_tile(t, widest):
    return min(widest, t // 2)


def _attn_probs(q, k, qs):
    s = _dot_nt(q, k) * ATT_SCALE
    tq = q.shape[0]
    rows = lax.broadcasted_iota(jnp.int32, (tq, tq), 0)
    cols = lax.broadcasted_iota(jnp.int32, (tq, tq), 1)
    last = jnp.where(jnp.right_shift(cols, CHUNK_SHIFT) <= jnp.right_shift(rows, CHUNK_SHIFT), s[:, qs:], NEG)
    s = last if qs == 0 else jnp.concatenate([s[:, :qs], last], axis=1)
    e = jnp.exp(s - jnp.max(s, axis=-1, keepdims=True))
    return e / jnp.sum(e, axis=-1, keepdims=True)


def _head_qkv(cq, ckv, kpe, c, s, wq_ref, wkv_ref):
    q = jnp.concatenate([_dot(cq, wq_ref[:, :NOPE]), _rope(_dot(cq, wq_ref[:, NOPE:]), c, s)], axis=1).astype(BF16)
    k = jnp.concatenate([_dot(ckv, wkv_ref[:, :NOPE]), kpe], axis=1).astype(BF16)
    vv = _dot(ckv, wkv_ref[:, NOPE:]).astype(BF16)
    return q, k, vv


def _attn_in_specs(t):
    return [_full((t, Q_RANK)), _full((t, KV_RANK)), _full((t, ROPE)), _full((t, ROPE // 2)), _full((t, ROPE // 2)),
            pl.BlockSpec((None, Q_RANK, NOPE + ROPE), lambda h: (h, 0, 0)),
            pl.BlockSpec((None, KV_RANK, NOPE + VDIM), lambda h: (h, 0, 0)),
            pl.BlockSpec((None, VDIM, D), lambda h: (h, 0, 0))]


def _attn_fwd(cq, ckv, kpe, cos, sin, wqb_g, wkvb_g, wo_g):
    t = cq.shape[0]
    tq = _q_tile(t, 256)

    def body(cq_ref, ckv_ref, kpe_ref, c_ref, s_ref, wq_ref, wkv_ref, wo_ref, o_ref, mix_ref):
        q, k, vv = _head_qkv(cq_ref[...], ckv_ref[...], kpe_ref[...], c_ref[...], s_ref[...], wq_ref, wkv_ref)
        for qs in range(0, t, tq):
            ke = qs + tq
            p = _attn_probs(q[qs:ke], k[:ke], qs)
            o_ref[qs:ke, :] = _dot(p.astype(BF16), vv[:ke]).astype(BF16)
        c = _dot(o_ref[...], wo_ref[...])

        @pl.when(pl.program_id(0) == 0)
        def _():
            mix_ref[...] = c

        @pl.when(pl.program_id(0) > 0)
        def _():
            mix_ref[...] += c

    return pl.pallas_call(
        body, grid=(MLA_HEADS,), in_specs=_attn_in_specs(t),
        out_specs=[pl.BlockSpec((None, t, VDIM), lambda h: (h, 0, 0)), _full((t, D))],
        out_shape=[S((MLA_HEADS, t, VDIM), BF16), S((t, D), F32)],
        compiler_params=_cp("arbitrary"), name="attn_fwd")(cq, ckv, kpe, cos, sin, wqb_g, wkvb_g, wo_g)


def _attn_bwd(cq, ckv, kpe, cos, sin, wqb_g, wkvb_g, wo_g, o, dzb):
    t = cq.shape[0]
    tq = _q_tile(t, 512)

    def body(cq_ref, ckv_ref, kpe_ref, c_ref, s_ref, wq_ref, wkv_ref, wo_ref, o_ref, dz_ref,
             dwo_ref, dwq_ref, dwkv_ref, dcq_ref, dckv_ref, dkpe_ref, dkt_s, dvt_s, dq_s):
        cqv = cq_ref[...]
        ckvv = ckv_ref[...]
        c = c_ref[...]
        s = s_ref[...]
        q, k, vv = _head_qkv(cqv, ckvv, kpe_ref[...], c, s, wq_ref, wkv_ref)
        dzv = dz_ref[...]
        dwo_ref[...] = _dot_tn(o_ref[...], dzv).astype(BF16)
        do = _dot_nt(dzv, wo_ref[...]).astype(BF16)
        dkt_s[...] = jnp.zeros_like(dkt_s)
        dvt_s[...] = jnp.zeros_like(dvt_s)
        for qs in range(0, t, tq):
            ke = qs + tq
            p = _attn_probs(q[qs:ke], k[:ke], qs)
            dp = _dot_nt(do[qs:ke], vv[:ke])
            ds = (p * (dp - jnp.sum(p * dp, axis=-1, keepdims=True)) * ATT_SCALE).astype(BF16)
            dq_s[qs:ke, :] = _dot(ds, k[:ke])
            dkt_s[0:NOPE + ROPE, 0:ke] += _dot_tn(q[qs:ke], ds)
            dvt_s[:, 0:ke] += _dot_tn(do[qs:ke], p.astype(BF16))
        dk = dkt_s[...].T
        dqn = dq_s[:, :NOPE].astype(BF16)
        dqp = _rope_t(dq_s[:, NOPE:], c, s).astype(BF16)
        dkn = dk[:, :NOPE].astype(BF16)
        dkp = dk[:, NOPE:NOPE + ROPE]
        dvv = dvt_s[...].T.astype(BF16)
        dwq_ref[:, :NOPE] = _dot_tn(cqv, dqn).astype(BF16)
        dwq_ref[:, NOPE:] = _dot_tn(cqv, dqp).astype(BF16)
        dwkv_ref[:, :NOPE] = _dot_tn(ckvv, dkn).astype(BF16)
        dwkv_ref[:, NOPE:] = _dot_tn(ckvv, dvv).astype(BF16)
        dcq = _dot_nt(dqn, wq_ref[:, :NOPE]) + _dot_nt(dqp, wq_ref[:, NOPE:])
        dckv = _dot_nt(dkn, wkv_ref[:, :NOPE]) + _dot_nt(dvv, wkv_ref[:, NOPE:])

        @pl.when(pl.program_id(0) == 0)
        def _():
            dcq_ref[...] = dcq
            dckv_ref[...] = dckv
            dkpe_ref[...] = dkp

        @pl.when(pl.program_id(0) > 0)
        def _():
            dcq_ref[...] += dcq
            dckv_ref[...] += dckv
            dkpe_ref[...] += dkp

    per_head = lambda a, b: pl.BlockSpec((None, a, b), lambda h: (h, 0, 0))
    return pl.pallas_call(
        body, grid=(MLA_HEADS,),
        in_specs=_attn_in_specs(t) + [per_head(t, VDIM), _full((t, D))],
        out_specs=[per_head(VDIM, D), per_head(Q_RANK, NOPE + ROPE), per_head(KV_RANK, NOPE + VDIM),
                   _full((t, Q_RANK)), _full((t, KV_RANK)), _full((t, ROPE))],
        out_shape=[S((MLA_HEADS, VDIM, D), BF16), S((MLA_HEADS, Q_RANK, NOPE + ROPE), BF16),
                   S((MLA_HEADS, KV_RANK, NOPE + VDIM), BF16),
                   S((t, Q_RANK), F32), S((t, KV_RANK), F32), S((t, ROPE), F32)],
        scratch_shapes=[pltpu.VMEM((2 * NOPE, t), F32), pltpu.VMEM((VDIM, t), F32),
                        pltpu.VMEM((t, NOPE + ROPE), F32)],
        compiler_params=_cp("arbitrary"), name="attn_bwd")(cq, ckv, kpe, cos, sin, wqb_g, wkvb_g, wo_g, o, dzb)


def _rms_bwd(down, dcq, dckv, dkpe, cos, sin, gq3, gkv3, j):
    t = down.shape[0]
    bm = _row_tile(t)

    def body(down_ref, dcq_ref, dckv_ref, dkpe_ref, c_ref, s_ref, gq_ref, gkv_ref, dd_ref, dgq_ref, dgkv_ref):
        @pl.when(pl.program_id(0) == 0)
        def _():
            dgq_ref[...] = jnp.zeros_like(dgq_ref)
            dgkv_ref[...] = jnp.zeros_like(dgkv_ref)

        def rms_b(x, dy, g):
            rstd = lax.rsqrt(jnp.mean(x * x, axis=-1, keepdims=True) + RMS_EPS)
            xh = x * rstd
            dyg = dy * g
            return rstd * (dyg - xh * jnp.mean(dyg * xh, axis=-1, keepdims=True)), jnp.sum(dy * xh, axis=0, keepdims=True)

        dq, dgq = rms_b(down_ref[:, :Q_RANK], dcq_ref[...], gq_ref[...])
        dkv, dgkv = rms_b(down_ref[:, Q_RANK:Q_RANK + KV_RANK], dckv_ref[...], gkv_ref[...])
        dgq_ref[...] += dgq
        dgkv_ref[...] += dgkv
        dd_ref[:, :Q_RANK] = dq.astype(BF16)
        dd_ref[:, Q_RANK:Q_RANK + KV_RANK] = dkv.astype(BF16)
        dd_ref[:, Q_RANK + KV_RANK:] = _rope_t(dkpe_ref[...], c_ref[...], s_ref[...]).astype(BF16)

    row = lambda n: pl.BlockSpec((bm, n), lambda i: (i, 0))
    return pl.pallas_call(
        body, grid=(t // bm,),
        in_specs=[row(ODD_IN), row(Q_RANK), row(KV_RANK), row(ROPE), row(ROPE // 2), row(ROPE // 2),
                  pl.BlockSpec((None, 1, Q_RANK), lambda i: (j, 0, 0)),
                  pl.BlockSpec((None, 1, KV_RANK), lambda i: (j, 0, 0))],
        out_specs=[row(ODD_IN), _full((1, Q_RANK)), _full((1, KV_RANK))],
        out_shape=[S((t, ODD_IN), BF16), S((1, Q_RANK), F32), S((1, KV_RANK), F32)],
        compiler_params=_cp("arbitrary"), name="rms_bwd")(down, dcq, dckv, dkpe, cos, sin, gq3, gkv3)


def _col_blocks(t, n, bn):
    return pl.BlockSpec((t, bn), lambda i: (0, i))


def _row_blocks(n, bm):
    return pl.BlockSpec((bm, n), lambda i: (i, 0))


def _local_step(x, pos2, tgt, small, weights_of, grads_done, start_dep=None, prefetch=None):
    t = x.shape[0]
    bm = min(512, t)
    inv_freq = (ROPE_THETA ** (-jnp.arange(0, ROPE, 2, dtype=F32) / ROPE)).reshape(1, ROPE // 2)
    cos, sin = _rope_tables(pos2, inv_freq)
    lru_p = {k: small[k] for k in ("conv_w", "conv_b", "w_a", "b_a", "w_x", "b_x", "lam")}

    saved = []
    y, yb = x, x
    for l in range(DEPTH):
        j = l // 2
        big = weights_of(l, 0, y)
        sv = dict(xb=yb, big=big)
        if l % 2 == 0:
            proj = _mm(yb, big["win_t"], mode="nt", grid=(EVEN_IN // 512,), a_spec=_full((t, D)),
                       b_spec=_row_blocks(D, 512), out_shape=S((t, EVEN_IN), F32),
                       out_spec=_col_blocks(t, EVEN_IN, 512), name="even_proj", dep=start_dep if l == 0 else None)
            ycat = jnp.concatenate([_pool_fwd(proj, small["pool_w"], small["pool_scale"], j),
                                    _lru_fwd(proj, lru_p, j)], axis=1)
            big.update(weights_of(l, 1, ycat))
            z1, y1, y1b = _proj_resid_ln(y, ycat, big["wout2d"], small["ln_mix_g"], small["ln_mix_b"], l, "even_out")
            sv.update(proj=proj, ycat=ycat)
        else:
            down, cq, ckv, kpe = _down_norm(yb, big["wdown"], small["gq"], small["gkv"], cos, sin, j)
            o, mix = _attn_fwd(cq, ckv, kpe, cos, sin, big["wqb"], big["wkvb"], big["wo"])
            z1, y1, y1b = _resid_ln(y, mix, small["ln_mix_g"], small["ln_mix_b"], l, "resid_ln")
            sv.update(down=down, cq=cq, ckv=ckv, kpe=kpe, o=o)
        fetched = prefetch(l + 1, y1) if prefetch is not None and l + 1 < DEPTH else None
        z2, y, yb, act = _mlp_fwd(y1, y1b, big["w1"], big["w2"], small["ln_ffn_g"], small["ln_ffn_b"], l,
                                  dep=fetched)
        sv.update(z1=z1, y1b=y1b, z2=z2, act=act)
        saved.append(sv)

    dy, loss_tile = _loss_grad(y, tgt)

    g = {k: [None] * n for k, n in (("ln_mix_g", 4), ("ln_mix_b", 4), ("ln_ffn_g", 4), ("ln_ffn_b", 4),
                                    ("pool_w", 2), ("pool_scale", 2), ("conv_w", 2), ("conv_b", 2),
                                    ("w_a", 2), ("b_a", 2), ("w_x", 2), ("b_x", 2), ("lam", 2),
                                    ("gq", 2), ("gkv", 2))}
    dep = None
    for l in reversed(range(DEPTH)):
        j = l // 2
        sv = saved[l]
        big = sv["big"]
        dz2, dz2b, g["ln_ffn_g"][l], g["ln_ffn_b"][l] = _ln_bwd(dy, sv["z2"], small["ln_ffn_g"], l, "ln_bwd", dep=dep)
        act = sv["act"]
        dh, dff = _mlp_bwd_dh(act, dz2b, big["w1"], big["w2"])
        dw1 = _mm(sv["y1b"], dh, mode="tn", grid=(N_DEV,), a_spec=_full((t, D)),
                  b_spec=_col_blocks(t, D_FF, FF_BLK), out_shape=S((N_DEV, D, FF_BLK), BF16),
                  out_spec=pl.BlockSpec((None, D, FF_BLK), lambda i: (i, 0, 0)), name="mlp_dw1")
        dw2 = _mm(act, dz2b, mode="tn", grid=(N_DEV,), a_spec=_col_blocks(t, D_FF, FF_BLK),
                  b_spec=_full((t, D)), out_shape=S((N_DEV, FF_BLK, D), BF16),
                  out_spec=pl.BlockSpec((None, FF_BLK, D), lambda i: (i, 0, 0)), name="mlp_dw2")
        dep = grads_done(l, dict(w1=dw1, w2=dw2))
        dz1, dz1b, g["ln_mix_g"][l], g["ln_mix_b"][l] = _ln_bwd(dff, sv["z1"], small["ln_mix_g"], l, "ln_bwd_res",
                                                                 r=dz2, dep=dep)
        if l % 2 == 0:
            wout = big["wout2d"]
            dycat = _mm(dz1b, wout, mode="nt", grid=(EVEN_MIX // 512,), a_spec=_full((t, D)),
                        b_spec=_row_blocks(D, 512), out_shape=S((t, EVEN_MIX), F32),
                        out_spec=_col_blocks(t, EVEN_MIX, 512), name="even_dycat")
            dwout = _mm(sv["ycat"], dz1b, mode="tn", grid=(EVEN_MIX // 512,), a_spec=_col_blocks(t, EVEN_MIX, 512),
                        b_spec=_full((t, D)), out_shape=S((EVEN_MIX, D), BF16), out_spec=_row_blocks(D, 512),
                        name="even_dwout")
            du_pool, g["pool_w"][j], g["pool_scale"][j] = _pool_bwd(sv["proj"], dycat, small["pool_w"],
                                                                   small["pool_scale"], j)
            (du_lru, du_gate, g["conv_w"][j], g["conv_b"][j], g["w_a"][j], g["b_a"][j], g["w_x"][j], g["b_x"][j],
             g["lam"][j]) = _lru_bwd(sv["proj"], dycat, lru_p, j)
            dproj = jnp.concatenate([du_pool, du_lru, du_gate], axis=1)
            dwin = _mm(sv["xb"], dproj, mode="tn", grid=(EVEN_IN // 512,), a_spec=_full((t, D)),
                       b_spec=_col_blocks(t, EVEN_IN, 512), out_shape=S((D, EVEN_IN), BF16),
                       out_spec=_col_blocks(D, EVEN_IN, 512), name="even_dwin")
            dep = grads_done(l, dict(win=dwin.reshape(D, N_DEV, EVEN_IN // N_DEV).transpose(1, 0, 2),
                                     wout=dwout.reshape(N_DEV, EVEN_MIX // N_DEV, D)))
            dy = _mm(dproj, big["win_t"], mode="nn", grid=(t // bm,), a_spec=_row_blocks(EVEN_IN, bm),
                     b_spec=_full((EVEN_IN, D)), out_shape=S((t, D), F32), out_spec=_row_blocks(D, bm),
                     add=dz1, add_spec=_row_blocks(D, bm), add_scale=ALPHA, name="even_dx", dep=dep)
        else:
            dwo, dwqb, dwkvb, dcq, dckv, dkpe = _attn_bwd(
                sv["cq"], sv["ckv"], sv["kpe"], cos, sin, big["wqb"], big["wkvb"], big["wo"], sv["o"], dz1b)
            ddown, g["gq"][j], g["gkv"][j] = _rms_bwd(sv["down"], dcq, dckv, dkpe, cos, sin, small["gq"],
                                                     small["gkv"], j)
            dwdown = _mm(sv["xb"], ddown, mode="tn", grid=(N_DEV,), a_spec=_col_blocks(t, D, D // N_DEV),
                         b_spec=_full((t, ODD_IN)), out_shape=S((N_DEV, D // N_DEV, ODD_IN), BF16),
                         out_spec=pl.BlockSpec((None, D // N_DEV, ODD_IN), lambda i: (i, 0, 0)),
                         name="odd_dwdown")
            dep = grads_done(l, dict(wdown=dwdown, wqb=dwqb, wkvb=dwkvb, wo=dwo))
            dy = _mm(ddown, big["wdown2d"], mode="nt", grid=(t // bm,), a_spec=_row_blocks(ODD_IN, bm),
                     b_spec=_full((D, ODD_IN)), out_shape=S((t, D), F32), out_spec=_row_blocks(D, bm),
                     add=dz1, add_spec=_row_blocks(D, bm), add_scale=ALPHA, name="odd_dx", dep=dep)
    return loss_tile[0, 0], dy, g


def _mesh_place():
    x, y, c = lax.axis_index("x"), lax.axis_index("y"), lax.axis_index("c")
    return x, y, c


def _peer(place, k):
    x, y, c = place
    return (1 - x if k & 4 else x, 1 - y if k & 2 else y, 1 - c if k & 1 else c)


def _index(place):
    x, y, c = place
    return 4 * x + 2 * y + c


ANY = pl.BlockSpec(memory_space=pl.ANY)


def _make_zones(shards, me, name, dtype=BF16):
    n = len(shards)

    def body(me_ref, *refs):
        for src, dst in zip(refs[:n], refs[n:]):
            dst[...] = src[...].astype(dtype)

    grid_spec = pltpu.PrefetchScalarGridSpec(
        num_scalar_prefetch=1, grid=(1,),
        in_specs=[pl.BlockSpec(s.shape, lambda i, me_ref: (0, 0)) for s in shards],
        out_specs=[pl.BlockSpec((None,) + s.shape, lambda i, me_ref: (me_ref[0], 0, 0)) for s in shards])
    return pl.pallas_call(body, grid_spec=grid_spec, out_shape=[S((N_DEV,) + s.shape, dtype) for s in shards],
                          compiler_params=_cp("arbitrary"), name=name)(me, *shards)


def _shard_rows_tile(a):
    return max(d for d in range(16, 257, 16) if a % d == 0)


HBM = pl.BlockSpec(memory_space=pltpu.HBM)
SEM = pl.BlockSpec(memory_space=pltpu.SEMAPHORE)
DATAFLOW = pltpu.SideEffectType.DATAFLOW_SIDE_EFFECTING


def _in_hbm(a):
    return pltpu.with_memory_space_constraint(a, pltpu.HBM)


def _gather_ici_copies(place, src, land, w):
    me = _index(place)
    return [(_peer(place, k), land.at[me], land.at[me]) for k in (1, 2, 4, 6)]


def _gather_d2d_copies(place, src, land, w):
    blocks = [_index(_peer(place, k)) for k in (2, 4, 6)]
    return [(_peer(place, 1), land.at[b], land.at[b]) for b in blocks]


GATHER_ICI = (4, _gather_ici_copies)
GATHER_D2D = (3, _gather_d2d_copies)


def _scatter_plan(layers):
    def copies(place, src, land, w):
        me = _index(place)
        mine = land.at[me] if layers[w] is None else land.at[me, layers[w]]
        return [(_peer(place, k), src.at[_index(_peer(place, k))], mine) for k in range(1, N_DEV)]
    return (N_DEV - 1, copies)


def _gather_all_copies(place, src, land, w):
    me = _index(place)
    return [(_peer(place, k), land.at[me], land.at[me]) for k in range(1, N_DEV)]


GATHER_ALL = (N_DEV - 1, _gather_all_copies)


def _sum_blocks(zone, part, me):
    r = part.shape[1]

    def body(me_ref, z_ref, p_ref, o_ref):
        acc = None
        for s in range(N_DEV):
            term = jnp.where(me_ref[0] == s, p_ref[...], z_ref[s])
            acc = term if acc is None else acc + term
        o_ref[...] = acc

    grid_spec = pltpu.PrefetchScalarGridSpec(
        num_scalar_prefetch=1, grid=(1,),
        in_specs=[pl.BlockSpec((N_DEV, r, 128), lambda i, me_ref: (0, 0, 0)),
                  pl.BlockSpec((None, r, 128), lambda i, me_ref: (me_ref[0], 0, 0))],
        out_specs=pl.BlockSpec((r, 128), lambda i, me_ref: (0, 0)))
    return pl.pallas_call(body, grid_spec=grid_spec, out_shape=S((r, 128), F32),
                          compiler_params=_cp("arbitrary"), name="sum_small")(me, zone, part)


def _exchange_start(srcs, lands, plan, name, after=()):
    ns, n = len(srcs), len(lands)
    n_in = ns + n + len(after)
    per, copies = plan

    def body(*refs):
        ins, land = refs[:ns], refs[ns:ns + n]
        send, recv = refs[n_in], refs[n_in + 1]
        token = refs[-1]
        place = _mesh_place()
        for i in range(per):
            for w in range(n):
                target, src, dst = copies(place, ins[w] if ns else None, land[w], w)[i]
                pltpu.make_async_remote_copy(src_ref=src, dst_ref=dst, send_sem=send.at[w * per + i],
                                             recv_sem=recv.at[w * per + i], device_id=target, device_id_type=MESH).start()
        token[...] = jnp.zeros_like(token)

    sems = pltpu.SemaphoreType.DMA((n * per,))
    thru = [pltpu.HBM(a.shape, a.dtype) for a in list(srcs) + list(lands)]
    out = pl.pallas_call(
        body, name=name, in_specs=[HBM] * (ns + n) + [ANY] * len(after),
        out_shape=(sems, sems, *thru, S((8, 128), F32)),
        out_specs=(SEM, SEM, *([HBM] * (ns + n)), pl.BlockSpec(memory_space=pltpu.VMEM)),
        input_output_aliases={i: 2 + i for i in range(ns + n)},
        compiler_params=pltpu.CompilerParams(has_side_effects=DATAFLOW),
    )(*[_in_hbm(a) for a in list(srcs) + list(lands)], *after)
    return out[0], out[1], list(out[2:2 + ns]), list(out[2 + ns:2 + ns + n]), out[-1]


def _exchange_wait(send, recv, srcs, lands, plan, after, name):
    ns, n = len(srcs), len(lands)
    per, copies = plan
    afters = tuple(after) if isinstance(after, (tuple, list)) else (after,)

    def body(*refs):
        ins, land = refs[:ns], refs[ns:ns + n]
        send_ref, recv_ref = refs[ns + n], refs[ns + n + 1]
        place = _mesh_place()
        for i in range(per):
            for w in range(n):
                target, src, dst = copies(place, ins[w] if ns else None, land[w], w)[i]
                cp = pltpu.make_async_remote_copy(src_ref=src, dst_ref=dst, send_sem=send_ref.at[w * per + i],
                                                  recv_sem=recv_ref.at[w * per + i], device_id=target,
                                                  device_id_type=MESH)
                cp.wait_send()
                cp.wait_recv()

    thru = [pltpu.HBM(a.shape, a.dtype) for a in list(srcs) + list(lands)]
    out = pl.pallas_call(
        body, name=name, in_specs=[HBM] * (ns + n) + [SEM, SEM] + [ANY] * len(afters),
        out_shape=tuple(thru), out_specs=tuple([HBM] * (ns + n)),
        input_output_aliases={i: i for i in range(ns + n)},
        compiler_params=pltpu.CompilerParams(has_side_effects=DATAFLOW),
    )(*srcs, *lands, send, recv, *afters)
    return list(out[:ns]), list(out[ns:])


def _adamw(w, g, m, v):
    m = ADAM_B1 * m + (1.0 - ADAM_B1) * g
    v = ADAM_B2 * v + (1.0 - ADAM_B2) * (g * g)
    m_hat = m / (1.0 - ADAM_B1 ** ADAM_STEP)
    v_hat = v / (1.0 - ADAM_B2 ** ADAM_STEP)
    return -ADAM_LR * (m_hat / (jnp.sqrt(v_hat) + ADAM_EPS) + ADAM_WD * w), m, v


def _adam_big(parts, own, me, w, m, v, name):
    nl, a, b = w.shape
    ta = _shard_rows_tile(a)

    def body(me_ref, p_ref, *refs):
        own_refs, (w_ref, m_ref, v_ref, g_ref, d_ref, mo_ref, vo_ref) = refs[:nl], refs[nl:]
        layer = pl.program_id(0)
        mine = own_refs[0][...]
        for k in range(1, nl):
            mine = jnp.where(layer == k, own_refs[k][...], mine)
        g = None
        for s in range(N_DEV):
            term = jnp.where(me_ref[0] == s, mine, p_ref[s]).astype(F32)
            g = term if g is None else g + term
        g_ref[...] = g
        d_ref[...], mo_ref[...], vo_ref[...] = _adamw(w_ref[...], g, m_ref[...], v_ref[...])

    blk = pl.BlockSpec((None, ta, b), lambda l, i, me_ref: (l, i, 0))

    def own_spec(k):
        return pl.BlockSpec((None, ta, b), lambda l, i, me_ref: (me_ref[0], jnp.where(l == k, i, 0), 0))

    grid_spec = pltpu.PrefetchScalarGridSpec(
        num_scalar_prefetch=1, grid=(nl, a // ta),
        in_specs=[pl.BlockSpec((N_DEV, None, ta, b), lambda l, i, me_ref: (0, l, i, 0))]
        + [own_spec(k) for k in range(nl)] + [blk, blk, blk],
        out_specs=[blk] * 4)
    return pl.pallas_call(body, grid_spec=grid_spec, out_shape=[S(w.shape, F32)] * 4,
                          compiler_params=_cp("arbitrary", "arbitrary"), name=name)(me, parts, *own, w, m, v)


def _adam_small(gs, ws, ms, vs):
    n = len(gs)

    def body(*refs):
        ins, outs = refs[:4 * n], refs[4 * n:]
        for i in range(n):
            g_ref, w_ref, m_ref, v_ref = (ins[k * n + i] for k in range(4))
            outs[i][...], outs[n + i][...], outs[2 * n + i][...] = _adamw(w_ref[...], g_ref[...], m_ref[...], v_ref[...])

    out = pl.pallas_call(body, out_shape=[S(g.shape, F32) for g in gs] * 3, compiler_params=_cp(),
                         name="adam_small")(*gs, *ws, *ms, *vs)
    return out[:n], out[n:2 * n], out[2 * n:]


BIG = ("even_w_in", "even_w_out", "mla_w_down", "mla_w_qb", "mla_w_kvb", "mla_w_o", "mlp_w1", "mlp_w2")
BIG_KEY = dict(even_w_in="win", even_w_out="wout", mla_w_down="wdown", mla_w_qb="wqb", mla_w_kvb="wkvb",
               mla_w_o="wo", mlp_w1="w1", mlp_w2="w2")
SMALL = (("ln_mix_g", "ln_mix_g", None), ("ln_mix_b", "ln_mix_b", None), ("ln_ffn_g", "ln_ffn_g", None),
         ("ln_ffn_b", "ln_ffn_b", None), ("pool_w", "pool_w", None), ("pool_scale", "pool_scale", None),
         ("lru_conv_w", "conv_w", 2), ("lru_conv_b", "conv_b", None), ("lru_w_a", "w_a", None),
         ("lru_b_a", "b_a", None), ("lru_w_x", "w_x", None), ("lru_b_x", "b_x", None), ("lru_lambda", "lam", None),
         ("mla_q_norm_g", "gq", 1), ("mla_kv_norm_g", "gkv", 1))
WEIGHTS = ("ln_mix_g", "ln_mix_b", "ln_ffn_g", "ln_ffn_b", "even_w_in", "pool_w", "pool_scale", "lru_conv_w",
           "lru_conv_b", "lru_w_a", "lru_b_a", "lru_w_x", "lru_b_x", "lru_lambda", "even_w_out", "mla_w_down",
           "mla_q_norm_g", "mla_kv_norm_g", "mla_w_qb", "mla_w_kvb", "mla_w_o", "mlp_w1", "mlp_w2")


def _layer_weights(l):
    j = l // 2
    if l % 2 == 0:
        mixer = [("win", "even_w_in", j), ("wout", "even_w_out", j)]
    else:
        mixer = [("wdown", "mla_w_down", j), ("wqb", "mla_w_qb", j), ("wkvb", "mla_w_kvb", j), ("wo", "mla_w_o", j)]
    return mixer + [("w1", "mlp_w1", l), ("w2", "mlp_w2", l)]


def _pack(arrays, multiple):
    flat = jnp.concatenate([a.reshape(-1) for a in arrays])
    pad = (-flat.shape[0]) % multiple
    return jnp.pad(flat, (0, pad))


def _unpack(flat, shapes):
    out, at = [], 0
    for shp in shapes:
        n = 1
        for s in shp:
            n *= s
        out.append(flat[at:at + n].reshape(shp))
        at += n
    return out


def _global_shape(local_shape, axis):
    if axis is None:
        return tuple(local_shape)
    return tuple(s * N_DEV if i == axis else s for i, s in enumerate(local_shape))


def _step(x, positions, tgt, w, m, v):
    t = x.shape[1]
    me = _index(_mesh_place())

    chunk = N_DEV * 8 * 128
    me_arr = me.astype(jnp.int32).reshape(1)

    lanes = lambda a: jnp.pad(a, ((0, 0), (0, 128 - a.shape[1])))
    mine_packed = jnp.concatenate([w["lru_conv_w"].reshape(8, HEAD), lanes(w["mla_q_norm_g"]),
                                   lanes(w["mla_kv_norm_g"]), jnp.zeros((4, 128), F32)])
    g_send, g_recv, _, g_land, token = _exchange_start([], _make_zones([mine_packed], me_arr, "zones_small", F32),
                                                       GATHER_ALL, "small_params_start")

    def keys_of(l, part):
        keys = [key for key, _, _ in _layer_weights(l)]
        if l == 0:
            return keys[:1] if part == 0 else keys[1:]
        return keys if part == 0 else []

    shard_of = {(l, key): (w[name][i].T if key == "win" else w[name][i])
                for l in range(DEPTH) for key, name, i in _layer_weights(l)}
    flights, after = {}, (token,)
    for l in range(DEPTH):
        for part in (0, 1):
            if keys_of(l, part):
                zones = _make_zones([shard_of[l, key] for key in keys_of(l, part)], me_arr, "zones_%d_%d" % (l, part))
                send, recv, _, lands, token = _exchange_start([], zones, GATHER_ICI, "gather_start_%d_%d" % (l, part),
                                                              after=after)
                flights[l, part] = (send, recv, [], lands)
                after = (token,)

    _, g_land = _exchange_wait(g_send, g_recv, [], g_land, GATHER_ALL, token, "small_params_wait")
    rows_first = g_land[0].transpose(1, 0, 2)
    q_shard, kv_shard = w["mla_q_norm_g"].shape[1], w["mla_kv_norm_g"].shape[1]
    full = dict(lru_conv_w=rows_first[:8].reshape(2, 4, LRU_W),
                mla_q_norm_g=rows_first[8:10, :, :q_shard].reshape(2, Q_RANK),
                mla_kv_norm_g=rows_first[10:12, :, :kv_shard].reshape(2, KV_RANK))

    passing = {}

    def pass_on(l, part, after):
        tag = "%d_%d" % (l, part)
        _, lands = _exchange_wait(*flights[l, part], GATHER_ICI, after, "gather_wait_" + tag)
        send, recv, _, lands, token = _exchange_start([], lands, GATHER_D2D, "gather_pass_" + tag)
        passing[l, part] = (send, recv, [], lands)
        return token

    def early_pass(l, after):
        return pass_on(l, 0, after) if l >= 2 else None

    def weights_of(l, part, after):
        keys = keys_of(l, part)
        if keys:
            if (l, part) not in passing:
                pass_on(l, part, after)
            _, arrays = _exchange_wait(*passing[l, part], GATHER_D2D, after, "gather_pass_wait_%d_%d" % (l, part))
        big = dict(zip(keys, arrays)) if keys else {}
        if "win" in big:
            big["win_t"] = big["win"].reshape(EVEN_IN, D)
        if "wout" in big:
            big["wout2d"] = big["wout"].reshape(EVEN_MIX, D)
        if "wdown" in big:
            big["wdown2d"] = big["wdown"].reshape(D, ODD_IN)
        return big

    zone = {name: lax.empty((N_DEV,) + w[name].shape, BF16) for name in BIG}
    name_of = {key: name for name, key in BIG_KEY.items()}
    sent, last_token = [], [None]

    def grads_done(l, grads):
        keys = list(grads)
        index = {key: i for key, _, i in _layer_weights(l)}
        layers = [index[key] for key in keys]
        send, recv, srcs, lands, tok = _exchange_start([grads[k] for k in keys], [zone[name_of[k]] for k in keys],
                                                       _scatter_plan(layers), "scatter_start_%d_%s" % (l, keys[0]))
        for k, land in zip(keys, lands):
            zone[name_of[k]] = land
        sent.append((send, recv, srcs, keys, layers))
        last_token[0] = tok
        return tok

    row3 = lambda a: a.reshape(a.shape[0], 1, a.shape[1])
    small = dict(ln_mix_g=row3(w["ln_mix_g"]), ln_mix_b=row3(w["ln_mix_b"]), ln_ffn_g=row3(w["ln_ffn_g"]),
                 ln_ffn_b=row3(w["ln_ffn_b"]), pool_w=w["pool_w"], pool_scale=row3(w["pool_scale"]),
                 conv_w=full["lru_conv_w"], conv_b=row3(w["lru_conv_b"]), w_a=w["lru_w_a"], b_a=row3(w["lru_b_a"]),
                 w_x=w["lru_w_x"], b_x=row3(w["lru_b_x"]), lam=row3(w["lru_lambda"]),
                 gq=row3(full["mla_q_norm_g"]), gkv=row3(full["mla_kv_norm_g"]))

    loss_part, grad_x, g = _local_step(x[0], positions.reshape(t, 1), tgt[0], small, weights_of, grads_done,
                                       start_dep=token, prefetch=early_pass)

    own = {name: [None] * w[name].shape[0] for name in BIG}
    me_arr = me.astype(jnp.int32).reshape(1)
    out = {}
    local_g = [jnp.stack(g[key]).reshape(_global_shape(w[name].shape, axis)) for name, key, axis in SMALL]
    local_g.append(loss_part.reshape(1))
    part = _pack(local_g, chunk).reshape(N_DEV, -1, 128)
    small_plan = _scatter_plan([None])
    s_send, s_recv, s_src, s_land, after = _exchange_start([part], [lax.empty(part.shape, F32)], small_plan,
                                                           "small_scatter_start", after=(last_token[0],))
    for n_flight, (send, recv, srcs, keys, layers) in enumerate(sent):
        if n_flight == len(sent) - 1:
            for name in BIG:
                if BIG_KEY[name] not in keys:
                    out[name] = _adam_big(zone[name], own[name], me_arr, w[name], m[name], v[name], "adam_" + name)
            s_src, s_land = _exchange_wait(s_send, s_recv, s_src, s_land, small_plan,
                                           [grad_x] + [o[0] for o in out.values()], "small_scatter_wait")
            chunk_sum = _sum_blocks(s_land[0], s_src[0], me_arr)
            r_zone = lax.dynamic_update_slice_in_dim(lax.empty(part.shape, F32), chunk_sum[None], me, 0)
            r_send, r_recv, _, r_land, after = _exchange_start([], [r_zone], GATHER_ALL, "small_gather_start")
        srcs, lands = _exchange_wait(send, recv, srcs, [zone[name_of[k]] for k in keys], _scatter_plan(layers),
                                     after, "scatter_wait_%d" % n_flight)
        for k, land, src, layer in zip(keys, lands, srcs, layers):
            zone[name_of[k]] = land
            own[name_of[k]][layer] = src
        after = lands[0]
    for name in BIG:
        if name not in out:
            out[name] = _adam_big(zone[name], own[name], me_arr, w[name], m[name], v[name], "adam_" + name)

    _, reduced = _exchange_wait(r_send, r_recv, [], r_land, GATHER_ALL, [out[name][0] for name in BIG],
                                "small_gather_wait")
    reduced = _unpack(reduced[0].reshape(-1), [a.shape for a in local_g])
    loss = reduced[-1][0]
    mine = [a if axis is None else lax.dynamic_slice_in_dim(a, me * w[name].shape[axis], w[name].shape[axis], axis)
            for a, (name, _, axis) in zip(reduced, SMALL)]
    names = [name for name, _, _ in SMALL]
    as_2d = lambda a: a.reshape(-1, a.shape[-1])
    new = _adam_small([as_2d(a) for a in mine], *([as_2d(src[name]) for name in names] for src in (w, m, v)))
    for i, name in enumerate(names):
        out[name] = (mine[i],) + tuple(part[i].reshape(w[name].shape) for part in new)

    return (loss, grad_x[None]) + tuple(out[name][i] for i in range(4) for name in WEIGHTS)


def kernel(x, positions, ln_mix_g, ln_mix_b, ln_ffn_g, ln_ffn_b, even_w_in, pool_w, pool_scale, lru_conv_w, lru_conv_b, lru_w_a, lru_b_a, lru_w_x, lru_b_x, lru_lambda, even_w_out, mla_w_down, mla_q_norm_g, mla_kv_norm_g, mla_w_qb, mla_w_kvb, mla_w_o, mlp_w1, mlp_w2, loss_target, m_ln_mix_g, m_ln_mix_b, m_ln_ffn_g, m_ln_ffn_b, m_even_w_in, m_pool_w, m_pool_scale, m_lru_conv_w, m_lru_conv_b, m_lru_w_a, m_lru_b_a, m_lru_w_x, m_lru_b_x, m_lru_lambda, m_even_w_out, m_mla_w_down, m_mla_q_norm_g, m_mla_kv_norm_g, m_mla_w_qb, m_mla_w_kvb, m_mla_w_o, m_mlp_w1, m_mlp_w2, v_ln_mix_g, v_ln_mix_b, v_ln_ffn_g, v_ln_ffn_b, v_even_w_in, v_pool_w, v_pool_scale, v_lru_conv_w, v_lru_conv_b, v_lru_w_a, v_lru_b_a, v_lru_w_x, v_lru_b_x, v_lru_lambda, v_even_w_out, v_mla_w_down, v_mla_q_norm_g, v_mla_kv_norm_g, v_mla_w_qb, v_mla_w_kvb, v_mla_w_o, v_mlp_w1, v_mlp_w2):
    w = dict(zip(WEIGHTS, (ln_mix_g, ln_mix_b, ln_ffn_g, ln_ffn_b, even_w_in, pool_w, pool_scale, lru_conv_w,
                           lru_conv_b, lru_w_a, lru_b_a, lru_w_x, lru_b_x, lru_lambda, even_w_out, mla_w_down,
                           mla_q_norm_g, mla_kv_norm_g, mla_w_qb, mla_w_kvb, mla_w_o, mlp_w1, mlp_w2)))
    m = dict(zip(WEIGHTS, (m_ln_mix_g, m_ln_mix_b, m_ln_ffn_g, m_ln_ffn_b, m_even_w_in, m_pool_w, m_pool_scale,
                           m_lru_conv_w, m_lru_conv_b, m_lru_w_a, m_lru_b_a, m_lru_w_x, m_lru_b_x, m_lru_lambda,
                           m_even_w_out, m_mla_w_down, m_mla_q_norm_g, m_mla_kv_norm_g, m_mla_w_qb, m_mla_w_kvb,
                           m_mla_w_o, m_mlp_w1, m_mlp_w2)))
    v = dict(zip(WEIGHTS, (v_ln_mix_g, v_ln_mix_b, v_ln_ffn_g, v_ln_ffn_b, v_even_w_in, v_pool_w, v_pool_scale,
                           v_lru_conv_w, v_lru_conv_b, v_lru_w_a, v_lru_b_a, v_lru_w_x, v_lru_b_x, v_lru_lambda,
                           v_even_w_out, v_mla_w_down, v_mla_q_norm_g, v_mla_kv_norm_g, v_mla_w_qb, v_mla_w_kvb,
                           v_mla_w_o, v_mlp_w1, v_mlp_w2)))
    return _step(x, positions, loss_target, w, m, v)
```

```python
import jax
import jax.numpy as jnp
from jax import lax
from jax.experimental import pallas as pl
from jax.experimental.pallas import tpu as pltpu

F32 = jnp.float32
BF16 = jnp.bfloat16
S = jax.ShapeDtypeStruct

D = 1024
DEPTH = 4
N_DEV = 8
CHUNK_SHIFT = 6
POOL_WINDOWS = (2, 4, 8, 16)
POOL_W = 512
LRU_W = 1024
LRU_HEADS = 8
HEAD = 128
LRU_C = 8.0
EVEN_IN = 2560
EVEN_MIX = 1536
MLA_HEADS = 8
NOPE = 128
ROPE = 64
VDIM = 128
Q_RANK = 384
KV_RANK = 256
ODD_IN = 704
D_FF = 4096
FF_BLK = D_FF // N_DEV
ROPE_THETA = 10000.0
ALPHA = (2 * DEPTH) ** 0.25
LN_EPS = 1e-5
RMS_EPS = 1e-6
ATT_SCALE = (NOPE + ROPE) ** -0.5
NEG = float(jnp.finfo(jnp.float32).min)
ADAM_LR = 0.001
ADAM_B1 = 0.9
ADAM_B2 = 0.999
ADAM_EPS = 1e-08
ADAM_WD = 0.01
ADAM_STEP = 10
V7X_VMEM_BYTES = 64 * 1024 * 1024
VMEM_LIMIT = V7X_VMEM_BYTES - 8 * 1024 * 1024
MESH = pl.DeviceIdType.MESH


def _cp(*sem):
    return pltpu.CompilerParams(dimension_semantics=sem if sem else None, vmem_limit_bytes=VMEM_LIMIT)


def _dot(a, b):
    return jnp.dot(a, b, preferred_element_type=F32)


def _dot_nt(a, b):
    return lax.dot_general(a, b, (((1,), (1,)), ((), ())), preferred_element_type=F32)


def _dot_tn(a, b):
    return lax.dot_general(a, b, (((0,), (0,)), ((), ())), preferred_element_type=F32)


def _full(shape):
    return pl.BlockSpec(shape, lambda *_: (0,) * len(shape))


def _mm(a, b, *, mode, grid, a_spec, b_spec, out_shape, out_spec, name, add=None, add_spec=None, add_scale=1.0,
        dep=None):
    dot = {"nn": _dot, "nt": _dot_nt, "tn": _dot_tn}[mode]

    def body(*refs):
        a_ref, b_ref, o_ref = refs[0], refs[1], refs[-1]
        acc = dot(a_ref[...].astype(BF16), b_ref[...].astype(BF16))
        if add is not None:
            acc = acc + add_scale * refs[2][...]
        o_ref[...] = acc.astype(o_ref.dtype)

    ops = [a, b] if add is None else [a, b, add]
    specs = [a_spec, b_spec] if add is None else [a_spec, b_spec, add_spec]
    if dep is not None:
        ops.append(dep)
        specs.append(pl.BlockSpec(memory_space=pl.ANY))
    return pl.pallas_call(body, grid=grid, in_specs=specs, out_specs=out_spec, out_shape=out_shape,
                          compiler_params=_cp(*(("parallel",) * len(grid))), name=name)(*ops)


def _ln_stats(z):
    mu = jnp.mean(z, axis=-1, keepdims=True)
    zc = z - mu
    var = jnp.mean(zc * zc, axis=-1, keepdims=True)
    rstd = lax.rsqrt(var + LN_EPS)
    return zc * rstd, rstd


def _row_tile(t):
    return min(1024, t)


def _resid_ln(x, mix, g3, b3, l, name):
    t = x.shape[0]
    bm = _row_tile(t)

    def body(x_ref, m_ref, g_ref, b_ref, z_ref, y_ref, yb_ref):
        z = ALPHA * x_ref[...] + m_ref[...]
        xh, _ = _ln_stats(z)
        y = xh * g_ref[...] + b_ref[...]
        z_ref[...] = z
        y_ref[...] = y
        yb_ref[...] = y.astype(BF16)

    row = pl.BlockSpec((bm, D), lambda i: (i, 0))
    vec = pl.BlockSpec((None, 1, D), lambda i: (l, 0, 0))
    return pl.pallas_call(body, grid=(t // bm,), in_specs=[row, row, vec, vec], out_specs=[row, row, row],
                          out_shape=[S((t, D), F32), S((t, D), F32), S((t, D), BF16)],
                          compiler_params=_cp("parallel"), name=name)(x, mix, g3, b3)


def _proj_resid_ln(x, a, wmat, g3, b3, l, name):
    t, k = a.shape
    bm = _row_tile(t)

    def body(x_ref, a_ref, w_ref, g_ref, b_ref, z_ref, y_ref, yb_ref):
        z = ALPHA * x_ref[...] + _dot(a_ref[...], w_ref[...])
        xh, _ = _ln_stats(z)
        y = xh * g_ref[...] + b_ref[...]
        z_ref[...] = z
        y_ref[...] = y
        yb_ref[...] = y.astype(BF16)

    row = pl.BlockSpec((bm, D), lambda i: (i, 0))
    vec = pl.BlockSpec((None, 1, D), lambda i: (l, 0, 0))
    return pl.pallas_call(body, grid=(t // bm,),
                          in_specs=[row, pl.BlockSpec((bm, k), lambda i: (i, 0)), _full((k, D)), vec, vec],
                          out_specs=[row, row, row], out_shape=[S((t, D), F32), S((t, D), F32), S((t, D), BF16)],
                          compiler_params=_cp("parallel"), name=name)(x, a, wmat, g3, b3)


def _ln_bwd(d, z, g3, l, name, r=None, dep=None):
    t = z.shape[0]
    bm = _row_tile(t)

    def body(*refs):
        refs = list(refs)
        d_ref = refs.pop(0)
        dy = d_ref[...]
        if r is not None:
            dy = dy + ALPHA * refs.pop(0)[...]
        z_ref, g_ref = refs.pop(0), refs.pop(0)
        if dep is not None:
            refs.pop(0)
        dz_ref, dzb_ref, dg_ref, db_ref = refs
        xh, rstd = _ln_stats(z_ref[...])
        dyg = dy * g_ref[...]
        m1 = jnp.mean(dyg, axis=-1, keepdims=True)
        m2 = jnp.mean(dyg * xh, axis=-1, keepdims=True)
        dz = rstd * (dyg - m1 - xh * m2)
        dz_ref[...] = dz
        dzb_ref[...] = dz.astype(BF16)

        @pl.when(pl.program_id(0) == 0)
        def _():
            dg_ref[...] = jnp.zeros_like(dg_ref)
            db_ref[...] = jnp.zeros_like(db_ref)

        dg_ref[...] += jnp.sum(dy * xh, axis=0, keepdims=True)
        db_ref[...] += jnp.sum(dy, axis=0, keepdims=True)

    row = pl.BlockSpec((bm, D), lambda i: (i, 0))
    vec = pl.BlockSpec((None, 1, D), lambda i: (l, 0, 0))
    acc = pl.BlockSpec((1, D), lambda i: (0, 0))
    ops = [d, z, g3] if r is None else [d, r, z, g3]
    specs = [row, row, vec] if r is None else [row, row, row, vec]
    if dep is not None:
        ops.append(dep)
        specs.append(_full(dep.shape))
    return pl.pallas_call(body, grid=(t // bm,), in_specs=specs, out_specs=[row, row, acc, acc],
                          out_shape=[S((t, D), F32), S((t, D), BF16), S((1, D), F32), S((1, D), F32)],
                          compiler_params=_cp("arbitrary"), name=name)(*ops)


def _loss_grad(y, tgt):
    t = y.shape[0]
    bm = _row_tile(t)

    def body(y_ref, t_ref, dy_ref, loss_ref, acc_ref):
        i = pl.program_id(0)
        e = y_ref[...] - t_ref[...]
        dy_ref[...] = e * (1.0 / D)

        @pl.when(i == 0)
        def _():
            acc_ref[...] = jnp.zeros_like(acc_ref)

        acc_ref[...] += jnp.sum(e * e, axis=0, keepdims=True)

        @pl.when(i == pl.num_programs(0) - 1)
        def _():
            loss_ref[...] = jnp.full(loss_ref.shape, (0.5 / D) * jnp.sum(acc_ref[...]), F32)

    row = pl.BlockSpec((bm, D), lambda i: (i, 0))
    return pl.pallas_call(body, grid=(t // bm,), in_specs=[row, row],
                          out_specs=[row, pl.BlockSpec((1, 128), lambda i: (0, 0))],
                          out_shape=[S((t, D), F32), S((1, 128), F32)],
                          scratch_shapes=[pltpu.VMEM((1, D), F32)],
                          compiler_params=_cp("arbitrary"), name="loss_grad")(y, tgt)


def _mlp_row_tile(t):
    return min(1024, t)


MLP_ROW_PARTS = 2


def _row_parts(bm):
    step = bm // MLP_ROW_PARTS
    return [slice(k * step, (k + 1) * step) for k in range(MLP_ROW_PARTS)]


def _mlp_fwd(y, yb, w1g, w2g, g3, b3, l, dep=None):
    t = yb.shape[0]
    bm = _mlp_row_tile(t)

    def body(*refs):
        y_ref, yb_ref, w1_ref, w2_ref, g_ref, b_ref = refs[:6]
        z_ref, o_ref, ob_ref, act_ref, acc_ref = refs[-5:]
        j = pl.program_id(1)

        @pl.when(j == 0)
        def _():
            acc_ref[...] = jnp.zeros_like(acc_ref)

        for rows in _row_parts(bm):
            h = jnp.maximum(_dot(yb_ref[rows, :], w1_ref[...]), 0.0)
            act = (h * h).astype(BF16)
            act_ref[rows, :] = act
            acc_ref[rows, :] += _dot(act, w2_ref[...])

        @pl.when(j == N_DEV - 1)
        def _():
            z = ALPHA * y_ref[...] + acc_ref[...]
            xh, _ = _ln_stats(z)
            out = xh * g_ref[...] + b_ref[...]
            z_ref[...] = z
            o_ref[...] = out
            ob_ref[...] = out.astype(BF16)

    row = pl.BlockSpec((bm, D), lambda i, j: (i, 0))
    vec = pl.BlockSpec((None, 1, D), lambda i, j: (l, 0, 0))
    deps = [] if dep is None else [dep]
    return pl.pallas_call(
        body, grid=(t // bm, N_DEV),
        in_specs=[row, row, pl.BlockSpec((None, D, FF_BLK), lambda i, j: (j, 0, 0)),
                  pl.BlockSpec((None, FF_BLK, D), lambda i, j: (j, 0, 0)), vec, vec] + [ANY] * len(deps),
        out_specs=[row, row, row, pl.BlockSpec((bm, FF_BLK), lambda i, j: (i, j))],
        out_shape=[S((t, D), F32), S((t, D), F32), S((t, D), BF16), S((t, D_FF), BF16)],
        scratch_shapes=[pltpu.VMEM((bm, D), F32)],
        compiler_params=_cp("parallel", "arbitrary"), name="mlp_fwd")(y, yb, w1g, w2g, g3, b3, *deps)


def _mlp_bwd_dh(act, dzb, w1g, w2g):
    t = act.shape[0]
    bm = _mlp_row_tile(t)

    def body(a_ref, dz_ref, w1_ref, w2_ref, dh_ref, acc_ref):
        @pl.when(pl.program_id(1) == 0)
        def _():
            acc_ref[...] = jnp.zeros_like(acc_ref)

        for rows in _row_parts(bm):
            r = jnp.sqrt(a_ref[rows, :].astype(F32))
            dh = (_dot_nt(dz_ref[rows, :], w2_ref[...]) * (2.0 * r)).astype(BF16)
            dh_ref[rows, :] = dh
            acc_ref[rows, :] += _dot_nt(dh, w1_ref[...])

    row = pl.BlockSpec((bm, D), lambda i, j: (i, 0))
    hid = pl.BlockSpec((bm, FF_BLK), lambda i, j: (i, j))
    return pl.pallas_call(
        body, grid=(t // bm, N_DEV),
        in_specs=[hid, row,
                  pl.BlockSpec((None, D, FF_BLK), lambda i, j: (j, 0, 0)),
                  pl.BlockSpec((None, FF_BLK, D), lambda i, j: (j, 0, 0))],
        out_specs=[hid, row],
        out_shape=[S((t, D_FF), BF16), S((t, D), F32)],
        compiler_params=_cp("parallel", "arbitrary"), name="mlp_bwd_dh")(act, dzb, w1g, w2g)


F32_SUBLANES = 8


def _shift_dn(x, k, rows, fill=0.0):
    if k % F32_SUBLANES == 0:
        return jnp.concatenate([jnp.full((k,) + x.shape[1:], fill, x.dtype), x[:x.shape[0] - k]], axis=0)
    return jnp.where(rows >= k, pltpu.roll(x, k, 0), fill)


def _shift_up(x, k, rows, fill=0.0):
    t = x.shape[0]
    if k % F32_SUBLANES == 0:
        return jnp.concatenate([x[k:], jnp.full((k,) + x.shape[1:], fill, x.dtype)], axis=0)
    return jnp.where(rows < t - k, pltpu.roll(x, t - k, 0), fill)


def _scan_rows(a, b, shift):
    rows = lax.broadcasted_iota(jnp.int32, a.shape, 0)
    k = 1
    t = a.shape[0]
    while k < t:
        b = a * shift(b, k, rows) + b
        if 2 * k < t:
            a = a * shift(a, k, rows, 1.0)
        k *= 2
    return b


def _scan_dn(a, b):
    return _scan_rows(a, b, _shift_dn)


def _scan_up(a, b):
    return _scan_rows(a, b, _shift_up)


def _window_sum_dn(x, w, rows):
    k = 1
    while k < w:
        x = x + _shift_dn(x, k, rows)
        k *= 2
    return x


def _window_sum_up(x, w, rows):
    k = 1
    while k < w:
        x = x + _shift_up(x, k, rows)
        k *= 2
    return x


def _pool_diff(u, w, rows):
    inv_count = 1.0 / jnp.minimum(rows + 1, w).astype(F32)
    return _window_sum_dn(u, w, rows) * inv_count - u, inv_count


def _pool_fwd(proj, pool_w, pool_scale3, j):
    t = proj.shape[0]

    def body(u_ref, w_ref, s_ref, y_ref):
        rows = lax.broadcasted_iota(jnp.int32, (t, HEAD), 0)
        for g, w in enumerate(POOL_WINDOWS):
            cols = slice(g * HEAD, (g + 1) * HEAD)
            d, _ = _pool_diff(u_ref[:, cols], w, rows)
            y = _dot(d.astype(BF16), w_ref[g].astype(BF16)) * s_ref[:, cols]
            y_ref[:, cols] = y.astype(BF16)

    return pl.pallas_call(
        body, grid=(1,),
        in_specs=[pl.BlockSpec((t, POOL_W), lambda i: (0, 0)),
                  pl.BlockSpec((None, 4, HEAD, HEAD), lambda i: (j, 0, 0, 0)),
                  pl.BlockSpec((None, 1, POOL_W), lambda i: (j, 0, 0))],
        out_specs=pl.BlockSpec((t, POOL_W), lambda i: (0, 0)),
        out_shape=S((t, POOL_W), BF16), compiler_params=_cp("arbitrary"), name="pool_fwd")(proj, pool_w, pool_scale3)


def _pool_bwd(proj, dycat, pool_w, pool_scale3, j):
    t = proj.shape[0]

    def body(u_ref, dy_ref, w_ref, s_ref, du_ref, dw_ref, ds_ref):
        rows = lax.broadcasted_iota(jnp.int32, (t, HEAD), 0)
        for g, w in enumerate(POOL_WINDOWS):
            cols = slice(g * HEAD, (g + 1) * HEAD)
            d, inv_count = _pool_diff(u_ref[:, cols], w, rows)
            db = d.astype(BF16)
            wg = w_ref[g].astype(BF16)
            dy = dy_ref[:, cols]
            ds_ref[:, cols] = jnp.sum(dy * _dot(db, wg), axis=0, keepdims=True)
            dzz = (dy * s_ref[:, cols]).astype(BF16)
            dw_ref[g] = _dot_tn(db, dzz)
            dd = _dot_nt(dzz, wg)
            du_ref[:, cols] = (_window_sum_up(dd * inv_count, w, rows) - dd).astype(BF16)

    return pl.pallas_call(
        body, grid=(1,),
        in_specs=[pl.BlockSpec((t, POOL_W), lambda i: (0, 0)),
                  pl.BlockSpec((t, POOL_W), lambda i: (0, 0)),
                  pl.BlockSpec((None, 4, HEAD, HEAD), lambda i: (j, 0, 0, 0)),
                  pl.BlockSpec((None, 1, POOL_W), lambda i: (j, 0, 0))],
        out_specs=[pl.BlockSpec((t, POOL_W), lambda i: (0, 0)), _full((4, HEAD, HEAD)), _full((1, POOL_W))],
        out_shape=[S((t, POOL_W), BF16), S((4, HEAD, HEAD), F32), S((1, POOL_W), F32)],
        compiler_params=_cp("arbitrary"), name="pool_bwd")(proj, dycat, pool_w, pool_scale3)


GELU_C = 0.7978845608028654
GELU_K = 0.044715


def _gelu(x):
    th = jnp.tanh(GELU_C * (x + GELU_K * x * x * x))
    return 0.5 * x * (1.0 + th), th


def _lru_forward(u, gate, cw, cb, wa, ba, wx, bx, lam, rows):
    v = cw[3:4] * u + cw[2:3] * _shift_dn(u, 1, rows) + cw[1:2] * _shift_dn(u, 2, rows) \
        + cw[0:1] * _shift_dn(u, 3, rows) + cb
    vb = v.astype(BF16)
    r = jax.nn.sigmoid(_dot(vb, wa) + ba)
    i = jax.nn.sigmoid(_dot(vb, wx) + bx)
    sp = jnp.maximum(-lam, 0.0) + jnp.log1p(jnp.exp(-jnp.abs(lam)))
    log_a = (-LRU_C) * r * sp
    a = jnp.exp(log_a)
    one_m_a2 = -jnp.tanh(log_a) * (a * a + 1.0)
    mult = jnp.sqrt(one_m_a2)
    h = _scan_dn(a, mult * (i * v))
    gl, th = _gelu(gate)
    return dict(v=v, vb=vb, r=r, i=i, sp=sp, a=a, mult=mult, h=h, gl=gl, th=th)


def _lru_specs(t, j, col0_u, col0_g):
    blk = lambda c0: pl.BlockSpec((t, HEAD), lambda h: (0, c0 + h))
    vec = pl.BlockSpec((None, 1, HEAD), lambda h: (j, 0, h))
    return [blk(col0_u), blk(col0_g),
            pl.BlockSpec((None, 4, HEAD), lambda h: (j, 0, h)), vec,
            pl.BlockSpec((None, None, HEAD, HEAD), lambda h: (j, h, 0, 0)), vec,
            pl.BlockSpec((None, None, HEAD, HEAD), lambda h: (j, h, 0, 0)), vec, vec]


def _lru_fwd(proj, p, j):
    t = proj.shape[0]

    def body(u_ref, g_ref, cw_ref, cb_ref, wa_ref, ba_ref, wx_ref, bx_ref, lam_ref, y_ref):
        rows = lax.broadcasted_iota(jnp.int32, (t, HEAD), 0)
        f = _lru_forward(u_ref[...], g_ref[...], cw_ref[...], cb_ref[...], wa_ref[...].astype(BF16), ba_ref[...],
                         wx_ref[...].astype(BF16), bx_ref[...], lam_ref[...], rows)
        y_ref[...] = (f["h"] * f["gl"]).astype(BF16)

    return pl.pallas_call(
        body, grid=(LRU_HEADS,), in_specs=_lru_specs(t, j, POOL_W // HEAD, (POOL_W + LRU_W) // HEAD),
        out_specs=pl.BlockSpec((t, HEAD), lambda h: (0, h)), out_shape=S((t, LRU_W), BF16),
        compiler_params=_cp("parallel"), name="lru_fwd")(
            proj, proj, p["conv_w"], p["conv_b"], p["w_a"], p["b_a"], p["w_x"], p["b_x"], p["lam"])


def _lru_bwd(proj, dycat, p, j):
    t = proj.shape[0]

    def body(u_ref, g_ref, cw_ref, cb_ref, wa_ref, ba_ref, wx_ref, bx_ref, lam_ref, dy_ref,
             du_ref, dgate_ref, dcw_ref, dcb_ref, dwa_ref, dba_ref, dwx_ref, dbx_ref, dlam_ref):
        rows = lax.broadcasted_iota(jnp.int32, (t, HEAD), 0)
        u = u_ref[...]
        gate = g_ref[...]
        cw = cw_ref[...]
        wa = wa_ref[...].astype(BF16)
        wx = wx_ref[...].astype(BF16)
        lam = lam_ref[...]
        f = _lru_forward(u, gate, cw, cb_ref[...], wa, ba_ref[...], wx, bx_ref[...], lam, rows)
        v, r, i, a, mult, h, th = f["v"], f["r"], f["i"], f["a"], f["mult"], f["h"], f["th"]
        dy = dy_ref[...]
        dgl = 0.5 * (1.0 + th) + 0.5 * gate * (1.0 - th * th) * GELU_C * (1.0 + 3.0 * GELU_K * gate * gate)
        dgate_ref[...] = (dy * h * dgl).astype(BF16)
        g = _scan_up(_shift_up(a, 1, rows), dy * f["gl"])
        da = g * _shift_dn(h, 1, rows)
        iv = i * v
        dmult = g * iv
        di = g * mult * v
        dv = g * mult * i
        dlog_a = da * a - dmult * (a * a) / mult
        dr = dlog_a * (-LRU_C) * f["sp"]
        dsp = jnp.sum(dlog_a * (-LRU_C) * r, axis=0, keepdims=True)
        dlam_ref[...] = -dsp * jax.nn.sigmoid(-lam)
        dpa = dr * r * (1.0 - r)
        dpx = di * i * (1.0 - i)
        dpab = dpa.astype(BF16)
        dpxb = dpx.astype(BF16)
        dwa_ref[...] = _dot_tn(f["vb"], dpab)
        dwx_ref[...] = _dot_tn(f["vb"], dpxb)
        dba_ref[...] = jnp.sum(dpa, axis=0, keepdims=True)
        dbx_ref[...] = jnp.sum(dpx, axis=0, keepdims=True)
        dv = dv + _dot_nt(dpab, wa) + _dot_nt(dpxb, wx)
        dcb_ref[...] = jnp.sum(dv, axis=0, keepdims=True)
        du = cw[3:4] * dv
        dcw_ref[3:4, :] = jnp.sum(dv * u, axis=0, keepdims=True)
        for k in (1, 2, 3):
            du = du + cw[3 - k:4 - k] * _shift_up(dv, k, rows)
            dcw_ref[3 - k:4 - k, :] = jnp.sum(dv * _shift_dn(u, k, rows), axis=0, keepdims=True)
        du_ref[...] = du.astype(BF16)

    blk = pl.BlockSpec((t, HEAD), lambda h: (0, h))
    vec = pl.BlockSpec((1, HEAD), lambda h: (0, h))
    mat = pl.BlockSpec((None, HEAD, HEAD), lambda h: (h, 0, 0))
    return pl.pallas_call(
        body, grid=(LRU_HEADS,),
        in_specs=_lru_specs(t, j, POOL_W // HEAD, (POOL_W + LRU_W) // HEAD)
        + [pl.BlockSpec((t, HEAD), lambda h: (0, POOL_W // HEAD + h))],
        out_specs=[blk, blk, pl.BlockSpec((4, HEAD), lambda h: (0, h)), vec, mat, vec, mat, vec, vec],
        out_shape=[S((t, LRU_W), BF16), S((t, LRU_W), BF16), S((4, LRU_W), F32), S((1, LRU_W), F32),
                   S((LRU_HEADS, HEAD, HEAD), F32), S((1, LRU_W), F32),
                   S((LRU_HEADS, HEAD, HEAD), F32), S((1, LRU_W), F32), S((1, LRU_W), F32)],
        compiler_params=_cp("parallel"), name="lru_bwd")(
            proj, proj, p["conv_w"], p["conv_b"], p["w_a"], p["b_a"], p["w_x"], p["b_x"], p["lam"], dycat)


def _rope(x, c, s):
    x1 = x[:, :ROPE // 2]
    x2 = x[:, ROPE // 2:]
    return jnp.concatenate([x1 * c - x2 * s, x1 * s + x2 * c], axis=-1)


def _rope_t(d, c, s):
    d1 = d[:, :ROPE // 2]
    d2 = d[:, ROPE // 2:]
    return jnp.concatenate([d1 * c + d2 * s, d2 * c - d1 * s], axis=-1)


def _rope_tables(pos2, inv_freq):
    t = pos2.shape[0]

    def body(p_ref, f_ref, c_ref, s_ref):
        ang = p_ref[...].astype(F32) * f_ref[...]
        c_ref[...] = jnp.cos(ang)
        s_ref[...] = jnp.sin(ang)

    return pl.pallas_call(body, out_shape=[S((t, ROPE // 2), F32), S((t, ROPE // 2), F32)],
                          name="rope_tables")(pos2, inv_freq)


def _down_norm(xb, wdown_g, gq3, gkv3, cos, sin, j):
    t = xb.shape[0]
    bm = _row_tile(t)

    def body(x_ref, w_ref, gq_ref, gkv_ref, c_ref, s_ref, down_ref, cq_ref, ckv_ref, kpe_ref):
        w = w_ref[...].reshape(D, ODD_IN)
        down = _dot(x_ref[...], w)
        down_ref[...] = down
        q = down[:, :Q_RANK]
        cq_ref[...] = (q * lax.rsqrt(jnp.mean(q * q, axis=-1, keepdims=True) + RMS_EPS) * gq_ref[...]).astype(BF16)
        kv = down[:, Q_RANK:Q_RANK + KV_RANK]
        ckv_ref[...] = (kv * lax.rsqrt(jnp.mean(kv * kv, axis=-1, keepdims=True) + RMS_EPS)
                        * gkv_ref[...]).astype(BF16)
        kpe_ref[...] = _rope(down[:, Q_RANK + KV_RANK:], c_ref[...], s_ref[...])

    row = lambda n: pl.BlockSpec((bm, n), lambda i: (i, 0))
    return pl.pallas_call(
        body, grid=(t // bm,),
        in_specs=[row(D), _full((N_DEV, D // N_DEV, ODD_IN)),
                  pl.BlockSpec((None, 1, Q_RANK), lambda i: (j, 0, 0)),
                  pl.BlockSpec((None, 1, KV_RANK), lambda i: (j, 0, 0)), row(ROPE // 2), row(ROPE // 2)],
        out_specs=[row(ODD_IN), row(Q_RANK), row(KV_RANK), row(ROPE)],
        out_shape=[S((t, ODD_IN), F32), S((t, Q_RANK), BF16), S((t, KV_RANK), BF16), S((t, ROPE), F32)],
        compiler_params=_cp("parallel"), name="down_norm")(xb, wdown_g, gq3, gkv3, cos, sin)


def _q_tile(t, widest):
    return min(widest, t // 2)


def _attn_probs(q, k, qs):
    s = _dot_nt(q, k) * ATT_SCALE
    tq = q.shape[0]
    rows = lax.broadcasted_iota(jnp.int32, (tq, tq), 0)
    cols = lax.broadcasted_iota(jnp.int32, (tq, tq), 1)
    last = jnp.where(jnp.right_shift(cols, CHUNK_SHIFT) <= jnp.right_shift(rows, CHUNK_SHIFT), s[:, qs:], NEG)
    s = last if qs == 0 else jnp.concatenate([s[:, :qs], last], axis=1)
    e = jnp.exp(s - jnp.max(s, axis=-1, keepdims=True))
    return e / jnp.sum(e, axis=-1, keepdims=True)


def _head_qkv(cq, ckv, kpe, c, s, wq_ref, wkv_ref):
    q = jnp.concatenate([_dot(cq, wq_ref[:, :NOPE]), _rope(_dot(cq, wq_ref[:, NOPE:]), c, s)], axis=1).astype(BF16)
    k = jnp.concatenate([_dot(ckv, wkv_ref[:, :NOPE]), kpe], axis=1).astype(BF16)
    vv = _dot(ckv, wkv_ref[:, NOPE:]).astype(BF16)
    return q, k, vv


def _attn_in_specs(t):
    return [_full((t, Q_RANK)), _full((t, KV_RANK)), _full((t, ROPE)), _full((t, ROPE // 2)), _full((t, ROPE // 2)),
            pl.BlockSpec((None, Q_RANK, NOPE + ROPE), lambda h: (h, 0, 0)),
            pl.BlockSpec((None, KV_RANK, NOPE + VDIM), lambda h: (h, 0, 0)),
            pl.BlockSpec((None, VDIM, D), lambda h: (h, 0, 0))]


def _attn_fwd(cq, ckv, kpe, cos, sin, wqb_g, wkvb_g, wo_g):
    t = cq.shape[0]
    tq = _q_tile(t, 256)

    def body(cq_ref, ckv_ref, kpe_ref, c_ref, s_ref, wq_ref, wkv_ref, wo_ref, o_ref, mix_ref):
        q, k, vv = _head_qkv(cq_ref[...], ckv_ref[...], kpe_ref[...], c_ref[...], s_ref[...], wq_ref, wkv_ref)
        for qs in range(0, t, tq):
            ke = qs + tq
            p = _attn_probs(q[qs:ke], k[:ke], qs)
            o_ref[qs:ke, :] = _dot(p.astype(BF16), vv[:ke]).astype(BF16)
        c = _dot(o_ref[...], wo_ref[...])

        @pl.when(pl.program_id(0) == 0)
        def _():
            mix_ref[...] = c

        @pl.when(pl.program_id(0) > 0)
        def _():
            mix_ref[...] += c

    return pl.pallas_call(
        body, grid=(MLA_HEADS,), in_specs=_attn_in_specs(t),
        out_specs=[pl.BlockSpec((None, t, VDIM), lambda h: (h, 0, 0)), _full((t, D))],
        out_shape=[S((MLA_HEADS, t, VDIM), BF16), S((t, D), F32)],
        compiler_params=_cp("arbitrary"), name="attn_fwd")(cq, ckv, kpe, cos, sin, wqb_g, wkvb_g, wo_g)


def _attn_bwd(cq, ckv, kpe, cos, sin, wqb_g, wkvb_g, wo_g, o, dzb):
    t = cq.shape[0]
    tq = _q_tile(t, 512)

    def body(cq_ref, ckv_ref, kpe_ref, c_ref, s_ref, wq_ref, wkv_ref, wo_ref, o_ref, dz_ref,
             dwo_ref, dwq_ref, dwkv_ref, dcq_ref, dckv_ref, dkpe_ref, dkt_s, dvt_s, dq_s):
        cqv = cq_ref[...]
        ckvv = ckv_ref[...]
        c = c_ref[...]
        s = s_ref[...]
        q, k, vv = _head_qkv(cqv, ckvv, kpe_ref[...], c, s, wq_ref, wkv_ref)
        dzv = dz_ref[...]
        dwo_ref[...] = _dot_tn(o_ref[...], dzv).astype(BF16)
        do = _dot_nt(dzv, wo_ref[...]).astype(BF16)
        dkt_s[...] = jnp.zeros_like(dkt_s)
        dvt_s[...] = jnp.zeros_like(dvt_s)
        for qs in range(0, t, tq):
            ke = qs + tq
            p = _attn_probs(q[qs:ke], k[:ke], qs)
            dp = _dot_nt(do[qs:ke], vv[:ke])
            ds = (p * (dp - jnp.sum(p * dp, axis=-1, keepdims=True)) * ATT_SCALE).astype(BF16)
            dq_s[qs:ke, :] = _dot(ds, k[:ke])
            dkt_s[0:NOPE + ROPE, 0:ke] += _dot_tn(q[qs:ke], ds)
            dvt_s[:, 0:ke] += _dot_tn(do[qs:ke], p.astype(BF16))
        dk = dkt_s[...].T
        dqn = dq_s[:, :NOPE].astype(BF16)
        dqp = _rope_t(dq_s[:, NOPE:], c, s).astype(BF16)
        dkn = dk[:, :NOPE].astype(BF16)
        dkp = dk[:, NOPE:NOPE + ROPE]
        dvv = dvt_s[...].T.astype(BF16)
        dwq_ref[:, :NOPE] = _dot_tn(cqv, dqn).astype(BF16)
        dwq_ref[:, NOPE:] = _dot_tn(cqv, dqp).astype(BF16)
        dwkv_ref[:, :NOPE] = _dot_tn(ckvv, dkn).astype(BF16)
        dwkv_ref[:, NOPE:] = _dot_tn(ckvv, dvv).astype(BF16)
        dcq = _dot_nt(dqn, wq_ref[:, :NOPE]) + _dot_nt(dqp, wq_ref[:, NOPE:])
        dckv = _dot_nt(dkn, wkv_ref[:, :NOPE]) + _dot_nt(dvv, wkv_ref[:, NOPE:])

        @pl.when(pl.program_id(0) == 0)
        def _():
            dcq_ref[...] = dcq
            dckv_ref[...] = dckv
            dkpe_ref[...] = dkp

        @pl.when(pl.program_id(0) > 0)
        def _():
            dcq_ref[...] += dcq
            dckv_ref[...] += dckv
            dkpe_ref[...] += dkp

    per_head = lambda a, b: pl.BlockSpec((None, a, b), lambda h: (h, 0, 0))
    return pl.pallas_call(
        body, grid=(MLA_HEADS,),
        in_specs=_attn_in_specs(t) + [per_head(t, VDIM), _full((t, D))],
        out_specs=[per_head(VDIM, D), per_head(Q_RANK, NOPE + ROPE), per_head(KV_RANK, NOPE + VDIM),
                   _full((t, Q_RANK)), _full((t, KV_RANK)), _full((t, ROPE))],
        out_shape=[S((MLA_HEADS, VDIM, D), BF16), S((MLA_HEADS, Q_RANK, NOPE + ROPE), BF16),
                   S((MLA_HEADS, KV_RANK, NOPE + VDIM), BF16),
                   S((t, Q_RANK), F32), S((t, KV_RANK), F32), S((t, ROPE), F32)],
        scratch_shapes=[pltpu.VMEM((2 * NOPE, t), F32), pltpu.VMEM((VDIM, t), F32),
                        pltpu.VMEM((t, NOPE + ROPE), F32)],
        compiler_params=_cp("arbitrary"), name="attn_bwd")(cq, ckv, kpe, cos, sin, wqb_g, wkvb_g, wo_g, o, dzb)


def _rms_bwd(down, dcq, dckv, dkpe, cos, sin, gq3, gkv3, j):
    t = down.shape[0]
    bm = _row_tile(t)

    def body(down_ref, dcq_ref, dckv_ref, dkpe_ref, c_ref, s_ref, gq_ref, gkv_ref, dd_ref, dgq_ref, dgkv_ref):
        @pl.when(pl.program_id(0) == 0)
        def _():
            dgq_ref[...] = jnp.zeros_like(dgq_ref)
            dgkv_ref[...] = jnp.zeros_like(dgkv_ref)

        def rms_b(x, dy, g):
            rstd = lax.rsqrt(jnp.mean(x * x, axis=-1, keepdims=True) + RMS_EPS)
            xh = x * rstd
            dyg = dy * g
            return rstd * (dyg - xh * jnp.mean(dyg * xh, axis=-1, keepdims=True)), jnp.sum(dy * xh, axis=0, keepdims=True)

        dq, dgq = rms_b(down_ref[:, :Q_RANK], dcq_ref[...], gq_ref[...])
        dkv, dgkv = rms_b(down_ref[:, Q_RANK:Q_RANK + KV_RANK], dckv_ref[...], gkv_ref[...])
        dgq_ref[...] += dgq
        dgkv_ref[...] += dgkv
        dd_ref[:, :Q_RANK] = dq.astype(BF16)
        dd_ref[:, Q_RANK:Q_RANK + KV_RANK] = dkv.astype(BF16)
        dd_ref[:, Q_RANK + KV_RANK:] = _rope_t(dkpe_ref[...], c_ref[...], s_ref[...]).astype(BF16)

    row = lambda n: pl.BlockSpec((bm, n), lambda i: (i, 0))
    return pl.pallas_call(
        body, grid=(t // bm,),
        in_specs=[row(ODD_IN), row(Q_RANK), row(KV_RANK), row(ROPE), row(ROPE // 2), row(ROPE // 2),
                  pl.BlockSpec((None, 1, Q_RANK), lambda i: (j, 0, 0)),
                  pl.BlockSpec((None, 1, KV_RANK), lambda i: (j, 0, 0))],
        out_specs=[row(ODD_IN), _full((1, Q_RANK)), _full((1, KV_RANK))],
        out_shape=[S((t, ODD_IN), BF16), S((1, Q_RANK), F32), S((1, KV_RANK), F32)],
        compiler_params=_cp("arbitrary"), name="rms_bwd")(down, dcq, dckv, dkpe, cos, sin, gq3, gkv3)


def _col_blocks(t, n, bn):
    return pl.BlockSpec((t, bn), lambda i: (0, i))


def _row_blocks(n, bm):
    return pl.BlockSpec((bm, n), lambda i: (i, 0))


def _local_step(x, pos2, tgt, small, weights_of, grads_done, start_dep=None, prefetch=None):
    t = x.shape[0]
    bm = min(512, t)
    inv_freq = (ROPE_THETA ** (-jnp.arange(0, ROPE, 2, dtype=F32) / ROPE)).reshape(1, ROPE // 2)
    cos, sin = _rope_tables(pos2, inv_freq)
    lru_p = {k: small[k] for k in ("conv_w", "conv_b", "w_a", "b_a", "w_x", "b_x", "lam")}

    saved = []
    y, yb = x, x
    for l in range(DEPTH):
        j = l // 2
        big = weights_of(l, 0, y)
        sv = dict(xb=yb, big=big)
        if l % 2 == 0:
            proj = _mm(yb, big["win_t"], mode="nt", grid=(EVEN_IN // 512,), a_spec=_full((t, D)),
                       b_spec=_row_blocks(D, 512), out_shape=S((t, EVEN_IN), F32),
                       out_spec=_col_blocks(t, EVEN_IN, 512), name="even_proj", dep=start_dep if l == 0 else None)
            ycat = jnp.concatenate([_pool_fwd(proj, small["pool_w"], small["pool_scale"], j),
                                    _lru_fwd(proj, lru_p, j)], axis=1)
            big.update(weights_of(l, 1, ycat))
            z1, y1, y1b = _proj_resid_ln(y, ycat, big["wout2d"], small["ln_mix_g"], small["ln_mix_b"], l, "even_out")
            sv.update(proj=proj, ycat=ycat)
        else:
            down, cq, ckv, kpe = _down_norm(yb, big["wdown"], small["gq"], small["gkv"], cos, sin, j)
            o, mix = _attn_fwd(cq, ckv, kpe, cos, sin, big["wqb"], big["wkvb"], big["wo"])
            z1, y1, y1b = _resid_ln(y, mix, small["ln_mix_g"], small["ln_mix_b"], l, "resid_ln")
            sv.update(down=down, cq=cq, ckv=ckv, kpe=kpe, o=o)
        big.update(weights_of(l, 2, y1))
        fetched = prefetch(l + 1, y1) if prefetch is not None and l + 1 < DEPTH else None
        z2, y, yb, act = _mlp_fwd(y1, y1b, big["w1"], big["w2"], small["ln_ffn_g"], small["ln_ffn_b"], l,
                                  dep=fetched)
        sv.update(z1=z1, y1b=y1b, z2=z2, act=act)
        saved.append(sv)

    dy, loss_tile = _loss_grad(y, tgt)

    g = {k: [None] * n for k, n in (("ln_mix_g", 4), ("ln_mix_b", 4), ("ln_ffn_g", 4), ("ln_ffn_b", 4),
                                    ("pool_w", 2), ("pool_scale", 2), ("conv_w", 2), ("conv_b", 2),
                                    ("w_a", 2), ("b_a", 2), ("w_x", 2), ("b_x", 2), ("lam", 2),
                                    ("gq", 2), ("gkv", 2))}
    dep = None
    for l in reversed(range(DEPTH)):
        j = l // 2
        sv = saved[l]
        big = sv["big"]
        dz2, dz2b, g["ln_ffn_g"][l], g["ln_ffn_b"][l] = _ln_bwd(dy, sv["z2"], small["ln_ffn_g"], l, "ln_bwd", dep=dep)
        act = sv["act"]
        dh, dff = _mlp_bwd_dh(act, dz2b, big["w1"], big["w2"])
        dw1 = _mm(sv["y1b"], dh, mode="tn", grid=(N_DEV,), a_spec=_full((t, D)),
                  b_spec=_col_blocks(t, D_FF, FF_BLK), out_shape=S((N_DEV, D, FF_BLK), BF16),
                  out_spec=pl.BlockSpec((None, D, FF_BLK), lambda i: (i, 0, 0)), name="mlp_dw1")
        dw2 = _mm(act, dz2b, mode="tn", grid=(N_DEV,), a_spec=_col_blocks(t, D_FF, FF_BLK),
                  b_spec=_full((t, D)), out_shape=S((N_DEV, FF_BLK, D), BF16),
                  out_spec=pl.BlockSpec((None, FF_BLK, D), lambda i: (i, 0, 0)), name="mlp_dw2")
        dep = grads_done(l, dict(w1=dw1, w2=dw2))
        dz1, dz1b, g["ln_mix_g"][l], g["ln_mix_b"][l] = _ln_bwd(dff, sv["z1"], small["ln_mix_g"], l, "ln_bwd_res",
                                                                 r=dz2, dep=dep)
        if l % 2 == 0:
            wout = big["wout2d"]
            dycat = _mm(dz1b, wout, mode="nt", grid=(EVEN_MIX // 512,), a_spec=_full((t, D)),
                        b_spec=_row_blocks(D, 512), out_shape=S((t, EVEN_MIX), F32),
                        out_spec=_col_blocks(t, EVEN_MIX, 512), name="even_dycat")
            dwout = _mm(sv["ycat"], dz1b, mode="tn", grid=(EVEN_MIX // 512,), a_spec=_col_blocks(t, EVEN_MIX, 512),
                        b_spec=_full((t, D)), out_shape=S((EVEN_MIX, D), BF16), out_spec=_row_blocks(D, 512),
                        name="even_dwout")
            du_pool, g["pool_w"][j], g["pool_scale"][j] = _pool_bwd(sv["proj"], dycat, small["pool_w"],
                                                                   small["pool_scale"], j)
            (du_lru, du_gate, g["conv_w"][j], g["conv_b"][j], g["w_a"][j], g["b_a"][j], g["w_x"][j], g["b_x"][j],
             g["lam"][j]) = _lru_bwd(sv["proj"], dycat, lru_p, j)
            dproj = jnp.concatenate([du_pool, du_lru, du_gate], axis=1)
            dwin = _mm(sv["xb"], dproj, mode="tn", grid=(EVEN_IN // 512,), a_spec=_full((t, D)),
                       b_spec=_col_blocks(t, EVEN_IN, 512), out_shape=S((D, EVEN_IN), BF16),
                       out_spec=_col_blocks(D, EVEN_IN, 512), name="even_dwin")
            dep = grads_done(l, dict(win=dwin.reshape(D, N_DEV, EVEN_IN // N_DEV).transpose(1, 0, 2),
                                     wout=dwout.reshape(N_DEV, EVEN_MIX // N_DEV, D)))
            dy = _mm(dproj, big["win_t"], mode="nn", grid=(t // bm,), a_spec=_row_blocks(EVEN_IN, bm),
                     b_spec=_full((EVEN_IN, D)), out_shape=S((t, D), F32), out_spec=_row_blocks(D, bm),
                     add=dz1, add_spec=_row_blocks(D, bm), add_scale=ALPHA, name="even_dx", dep=dep)
        else:
            dwo, dwqb, dwkvb, dcq, dckv, dkpe = _attn_bwd(
                sv["cq"], sv["ckv"], sv["kpe"], cos, sin, big["wqb"], big["wkvb"], big["wo"], sv["o"], dz1b)
            ddown, g["gq"][j], g["gkv"][j] = _rms_bwd(sv["down"], dcq, dckv, dkpe, cos, sin, small["gq"],
                                                     small["gkv"], j)
            dwdown = _mm(sv["xb"], ddown, mode="tn", grid=(N_DEV,), a_spec=_col_blocks(t, D, D // N_DEV),
                         b_spec=_full((t, ODD_IN)), out_shape=S((N_DEV, D // N_DEV, ODD_IN), BF16),
                         out_spec=pl.BlockSpec((None, D // N_DEV, ODD_IN), lambda i: (i, 0, 0)),
                         name="odd_dwdown")
            dep = grads_done(l, dict(wdown=dwdown, wqb=dwqb, wkvb=dwkvb, wo=dwo))
            dy = _mm(ddown, big["wdown2d"], mode="nt", grid=(t // bm,), a_spec=_row_blocks(ODD_IN, bm),
                     b_spec=_full((D, ODD_IN)), out_shape=S((t, D), F32), out_spec=_row_blocks(D, bm),
                     add=dz1, add_spec=_row_blocks(D, bm), add_scale=ALPHA, name="odd_dx", dep=dep)
    return loss_tile[0, 0], dy, g


def _mesh_place():
    x, y, c = lax.axis_index("x"), lax.axis_index("y"), lax.axis_index("c")
    return x, y, c


def _peer(place, k):
    x, y, c = place
    return (1 - x if k & 4 else x, 1 - y if k & 2 else y, 1 - c if k & 1 else c)


def _index(place):
    x, y, c = place
    return 4 * x + 2 * y + c


ANY = pl.BlockSpec(memory_space=pl.ANY)


def _make_zones(shards, me, name, dtype=BF16):
    n = len(shards)

    def body(me_ref, *refs):
        for src, dst in zip(refs[:n], refs[n:]):
            dst[...] = src[...].astype(dtype)

    grid_spec = pltpu.PrefetchScalarGridSpec(
        num_scalar_prefetch=1, grid=(1,),
        in_specs=[pl.BlockSpec(s.shape, lambda i, me_ref: (0, 0)) for s in shards],
        out_specs=[pl.BlockSpec((None,) + s.shape, lambda i, me_ref: (me_ref[0], 0, 0)) for s in shards])
    return pl.pallas_call(body, grid_spec=grid_spec, out_shape=[S((N_DEV,) + s.shape, dtype) for s in shards],
                          compiler_params=_cp("arbitrary"), name=name)(me, *shards)


def _shard_rows_tile(a):
    return max(d for d in range(16, 257, 16) if a % d == 0)


HBM = pl.BlockSpec(memory_space=pltpu.HBM)
SEM = pl.BlockSpec(memory_space=pltpu.SEMAPHORE)
DATAFLOW = pltpu.SideEffectType.DATAFLOW_SIDE_EFFECTING


def _in_hbm(a):
    return pltpu.with_memory_space_constraint(a, pltpu.HBM)


def _gather_ici_copies(place, src, land, w):
    me = _index(place)
    return [(_peer(place, k), land.at[me], land.at[me]) for k in (1, 2, 4, 6)]


def _gather_d2d_copies(place, src, land, w):
    blocks = [_index(_peer(place, k)) for k in (2, 4, 6)]
    return [(_peer(place, 1), land.at[b], land.at[b]) for b in blocks]


GATHER_ICI = (4, _gather_ici_copies)
GATHER_D2D = (3, _gather_d2d_copies)


def _scatter_plan(layers):
    def copies(place, src, land, w):
        me = _index(place)
        mine = land.at[me] if layers[w] is None else land.at[me, layers[w]]
        return [(_peer(place, k), src.at[_index(_peer(place, k))], mine) for k in range(1, N_DEV)]
    return (N_DEV - 1, copies)


def _gather_all_copies(place, src, land, w):
    me = _index(place)
    return [(_peer(place, k), land.at[me], land.at[me]) for k in range(1, N_DEV)]


GATHER_ALL = (N_DEV - 1, _gather_all_copies)


def _sum_blocks(zone, part, me):
    r = part.shape[1]

    def body(me_ref, z_ref, p_ref, o_ref):
        acc = None
        for s in range(N_DEV):
            term = jnp.where(me_ref[0] == s, p_ref[...], z_ref[s])
            acc = term if acc is None else acc + term
        o_ref[...] = acc

    grid_spec = pltpu.PrefetchScalarGridSpec(
        num_scalar_prefetch=1, grid=(1,),
        in_specs=[pl.BlockSpec((N_DEV, r, 128), lambda i, me_ref: (0, 0, 0)),
                  pl.BlockSpec((None, r, 128), lambda i, me_ref: (me_ref[0], 0, 0))],
        out_specs=pl.BlockSpec((r, 128), lambda i, me_ref: (0, 0)))
    return pl.pallas_call(body, grid_spec=grid_spec, out_shape=S((r, 128), F32),
                          compiler_params=_cp("arbitrary"), name="sum_small")(me, zone, part)


def _exchange_start(srcs, lands, plan, name, after=()):
    ns, n = len(srcs), len(lands)
    n_in = ns + n + len(after)
    per, copies = plan

    def body(*refs):
        ins, land = refs[:ns], refs[ns:ns + n]
        send, recv = refs[n_in], refs[n_in + 1]
        token = refs[-1]
        place = _mesh_place()
        for i in range(per):
            for w in range(n):
                target, src, dst = copies(place, ins[w] if ns else None, land[w], w)[i]
                pltpu.make_async_remote_copy(src_ref=src, dst_ref=dst, send_sem=send.at[w * per + i],
                                             recv_sem=recv.at[w * per + i], device_id=target, device_id_type=MESH).start()
        token[...] = jnp.zeros_like(token)

    sems = pltpu.SemaphoreType.DMA((n * per,))
    thru = [pltpu.HBM(a.shape, a.dtype) for a in list(srcs) + list(lands)]
    out = pl.pallas_call(
        body, name=name, in_specs=[HBM] * (ns + n) + [ANY] * len(after),
        out_shape=(sems, sems, *thru, S((8, 128), F32)),
        out_specs=(SEM, SEM, *([HBM] * (ns + n)), pl.BlockSpec(memory_space=pltpu.VMEM)),
        input_output_aliases={i: 2 + i for i in range(ns + n)},
        compiler_params=pltpu.CompilerParams(has_side_effects=DATAFLOW),
    )(*[_in_hbm(a) for a in list(srcs) + list(lands)], *after)
    return out[0], out[1], list(out[2:2 + ns]), list(out[2 + ns:2 + ns + n]), out[-1]


def _exchange_wait(send, recv, srcs, lands, plan, after, name):
    ns, n = len(srcs), len(lands)
    per, copies = plan
    afters = tuple(after) if isinstance(after, (tuple, list)) else (after,)

    def body(*refs):
        ins, land = refs[:ns], refs[ns:ns + n]
        send_ref, recv_ref = refs[ns + n], refs[ns + n + 1]
        place = _mesh_place()
        for i in range(per):
            for w in range(n):
                target, src, dst = copies(place, ins[w] if ns else None, land[w], w)[i]
                cp = pltpu.make_async_remote_copy(src_ref=src, dst_ref=dst, send_sem=send_ref.at[w * per + i],
                                                  recv_sem=recv_ref.at[w * per + i], device_id=target,
                                                  device_id_type=MESH)
                cp.wait_send()
                cp.wait_recv()

    thru = [pltpu.HBM(a.shape, a.dtype) for a in list(srcs) + list(lands)]
    out = pl.pallas_call(
        body, name=name, in_specs=[HBM] * (ns + n) + [SEM, SEM] + [ANY] * len(afters),
        out_shape=tuple(thru), out_specs=tuple([HBM] * (ns + n)),
        input_output_aliases={i: i for i in range(ns + n)},
        compiler_params=pltpu.CompilerParams(has_side_effects=DATAFLOW),
    )(*srcs, *lands, send, recv, *afters)
    return list(out[:ns]), list(out[ns:])


def _adamw(w, g, m, v):
    m = ADAM_B1 * m + (1.0 - ADAM_B1) * g
    v = ADAM_B2 * v + (1.0 - ADAM_B2) * (g * g)
    m_hat = m / (1.0 - ADAM_B1 ** ADAM_STEP)
    v_hat = v / (1.0 - ADAM_B2 ** ADAM_STEP)
    return -ADAM_LR * (m_hat / (jnp.sqrt(v_hat) + ADAM_EPS) + ADAM_WD * w), m, v


def _adam_big(parts, own, me, w, m, v, name):
    nl, a, b = w.shape
    ta = _shard_rows_tile(a)

    def body(me_ref, p_ref, *refs):
        own_refs, (w_ref, m_ref, v_ref, g_ref, d_ref, mo_ref, vo_ref) = refs[:nl], refs[nl:]
        layer = pl.program_id(0)
        mine = own_refs[0][...]
        for k in range(1, nl):
            mine = jnp.where(layer == k, own_refs[k][...], mine)
        g = None
        for s in range(N_DEV):
            term = jnp.where(me_ref[0] == s, mine, p_ref[s]).astype(F32)
            g = term if g is None else g + term
        g_ref[...] = g
        d_ref[...], mo_ref[...], vo_ref[...] = _adamw(w_ref[...], g, m_ref[...], v_ref[...])

    blk = pl.BlockSpec((None, ta, b), lambda l, i, me_ref: (l, i, 0))

    def own_spec(k):
        return pl.BlockSpec((None, ta, b), lambda l, i, me_ref: (me_ref[0], jnp.where(l == k, i, 0), 0))

    grid_spec = pltpu.PrefetchScalarGridSpec(
        num_scalar_prefetch=1, grid=(nl, a // ta),
        in_specs=[pl.BlockSpec((N_DEV, None, ta, b), lambda l, i, me_ref: (0, l, i, 0))]
        + [own_spec(k) for k in range(nl)] + [blk, blk, blk],
        out_specs=[blk] * 4)
    return pl.pallas_call(body, grid_spec=grid_spec, out_shape=[S(w.shape, F32)] * 4,
                          compiler_params=_cp("arbitrary", "arbitrary"), name=name)(me, parts, *own, w, m, v)


def _adam_small(gs, ws, ms, vs):
    n = len(gs)

    def body(*refs):
        ins, outs = refs[:4 * n], refs[4 * n:]
        for i in range(n):
            g_ref, w_ref, m_ref, v_ref = (ins[k * n + i] for k in range(4))
            outs[i][...], outs[n + i][...], outs[2 * n + i][...] = _adamw(w_ref[...], g_ref[...], m_ref[...], v_ref[...])

    out = pl.pallas_call(body, out_shape=[S(g.shape, F32) for g in gs] * 3, compiler_params=_cp(),
                         name="adam_small")(*gs, *ws, *ms, *vs)
    return out[:n], out[n:2 * n], out[2 * n:]


BIG = ("even_w_in", "even_w_out", "mla_w_down", "mla_w_qb", "mla_w_kvb", "mla_w_o", "mlp_w1", "mlp_w2")
BIG_KEY = dict(even_w_in="win", even_w_out="wout", mla_w_down="wdown", mla_w_qb="wqb", mla_w_kvb="wkvb",
               mla_w_o="wo", mlp_w1="w1", mlp_w2="w2")
SMALL = (("ln_mix_g", "ln_mix_g", None), ("ln_mix_b", "ln_mix_b", None), ("ln_ffn_g", "ln_ffn_g", None),
         ("ln_ffn_b", "ln_ffn_b", None), ("pool_w", "pool_w", None), ("pool_scale", "pool_scale", None),
         ("lru_conv_w", "conv_w", 2), ("lru_conv_b", "conv_b", None), ("lru_w_a", "w_a", None),
         ("lru_b_a", "b_a", None), ("lru_w_x", "w_x", None), ("lru_b_x", "b_x", None), ("lru_lambda", "lam", None),
         ("mla_q_norm_g", "gq", 1), ("mla_kv_norm_g", "gkv", 1))
WEIGHTS = ("ln_mix_g", "ln_mix_b", "ln_ffn_g", "ln_ffn_b", "even_w_in", "pool_w", "pool_scale", "lru_conv_w",
           "lru_conv_b", "lru_w_a", "lru_b_a", "lru_w_x", "lru_b_x", "lru_lambda", "even_w_out", "mla_w_down",
           "mla_q_norm_g", "mla_kv_norm_g", "mla_w_qb", "mla_w_kvb", "mla_w_o", "mlp_w1", "mlp_w2")


def _layer_weights(l):
    j = l // 2
    if l % 2 == 0:
        mixer = [("win", "even_w_in", j), ("wout", "even_w_out", j)]
    else:
        mixer = [("wdown", "mla_w_down", j), ("wqb", "mla_w_qb", j), ("wkvb", "mla_w_kvb", j), ("wo", "mla_w_o", j)]
    return mixer + [("w1", "mlp_w1", l), ("w2", "mlp_w2", l)]


def _pack(arrays, multiple):
    flat = jnp.concatenate([a.reshape(-1) for a in arrays])
    pad = (-flat.shape[0]) % multiple
    return jnp.pad(flat, (0, pad))


def _unpack(flat, shapes):
    out, at = [], 0
    for shp in shapes:
        n = 1
        for s in shp:
            n *= s
        out.append(flat[at:at + n].reshape(shp))
        at += n
    return out


def _global_shape(local_shape, axis):
    if axis is None:
        return tuple(local_shape)
    return tuple(s * N_DEV if i == axis else s for i, s in enumerate(local_shape))


def _step(x, positions, tgt, w, m, v):
    t = x.shape[1]
    me = _index(_mesh_place())

    chunk = N_DEV * 8 * 128
    me_arr = me.astype(jnp.int32).reshape(1)

    lanes = lambda a: jnp.pad(a, ((0, 0), (0, 128 - a.shape[1])))
    mine_packed = jnp.concatenate([w["lru_conv_w"].reshape(8, HEAD), lanes(w["mla_q_norm_g"]),
                                   lanes(w["mla_kv_norm_g"]), jnp.zeros((4, 128), F32)])
    g_send, g_recv, _, g_land, token = _exchange_start([], _make_zones([mine_packed], me_arr, "zones_small", F32),
                                                       GATHER_ALL, "small_params_start")

    def keys_of(l, part):
        keys = [key for key, _, _ in _layer_weights(l)]
        if l == 0:
            return [keys[:1], keys[1:], []][part]
        if l == 1:
            return [keys[:-2], [], keys[-2:]][part]
        return keys if part == 0 else []

    shard_of = {(l, key): (w[name][i].T if key == "win" else w[name][i])
                for l in range(DEPTH) for key, name, i in _layer_weights(l)}
    flights, after = {}, (token,)
    for l in range(DEPTH):
        for part in (0, 1, 2):
            if keys_of(l, part):
                zones = _make_zones([shard_of[l, key] for key in keys_of(l, part)], me_arr, "zones_%d_%d" % (l, part))
                send, recv, _, lands, token = _exchange_start([], zones, GATHER_ICI, "gather_start_%d_%d" % (l, part),
                                                              after=after)
                flights[l, part] = (send, recv, [], lands)
                after = (token,)

    _, g_land = _exchange_wait(g_send, g_recv, [], g_land, GATHER_ALL, token, "small_params_wait")
    rows_first = g_land[0].transpose(1, 0, 2)
    q_shard, kv_shard = w["mla_q_norm_g"].shape[1], w["mla_kv_norm_g"].shape[1]
    full = dict(lru_conv_w=rows_first[:8].reshape(2, 4, LRU_W),
                mla_q_norm_g=rows_first[8:10, :, :q_shard].reshape(2, Q_RANK),
                mla_kv_norm_g=rows_first[10:12, :, :kv_shard].reshape(2, KV_RANK))

    passing = {}

    def pass_on(l, part, after):
        tag = "%d_%d" % (l, part)
        _, lands = _exchange_wait(*flights[l, part], GATHER_ICI, after, "gather_wait_" + tag)
        send, recv, _, lands, token = _exchange_start([], lands, GATHER_D2D, "gather_pass_" + tag)
        passing[l, part] = (send, recv, [], lands)
        return token

    def early_pass(l, after):
        return pass_on(l, 0, after) if l >= 2 else None

    def weights_of(l, part, after):
        keys = keys_of(l, part)
        if keys:
            if (l, part) not in passing:
                pass_on(l, part, after)
            _, arrays = _exchange_wait(*passing[l, part], GATHER_D2D, after, "gather_pass_wait_%d_%d" % (l, part))
        big = dict(zip(keys, arrays)) if keys else {}
        if "win" in big:
            big["win_t"] = big["win"].reshape(EVEN_IN, D)
        if "wout" in big:
            big["wout2d"] = big["wout"].reshape(EVEN_MIX, D)
        if "wdown" in big:
            big["wdown2d"] = big["wdown"].reshape(D, ODD_IN)
        return big

    zone = {name: lax.empty((N_DEV,) + w[name].shape, BF16) for name in BIG}
    name_of = {key: name for name, key in BIG_KEY.items()}
    sent, last_token = [], [None]

    def grads_done(l, grads):
        keys = list(grads)
        index = {key: i for key, _, i in _layer_weights(l)}
        layers = [index[key] for key in keys]
        send, recv, srcs, lands, tok = _exchange_start([grads[k] for k in keys], [zone[name_of[k]] for k in keys],
                                                       _scatter_plan(layers), "scatter_start_%d_%s" % (l, keys[0]))
        for k, land in zip(keys, lands):
            zone[name_of[k]] = land
        sent.append((send, recv, srcs, keys, layers))
        last_token[0] = tok
        return tok

    row3 = lambda a: a.reshape(a.shape[0], 1, a.shape[1])
    small = dict(ln_mix_g=row3(w["ln_mix_g"]), ln_mix_b=row3(w["ln_mix_b"]), ln_ffn_g=row3(w["ln_ffn_g"]),
                 ln_ffn_b=row3(w["ln_ffn_b"]), pool_w=w["pool_w"], pool_scale=row3(w["pool_scale"]),
                 conv_w=full["lru_conv_w"], conv_b=row3(w["lru_conv_b"]), w_a=w["lru_w_a"], b_a=row3(w["lru_b_a"]),
                 w_x=w["lru_w_x"], b_x=row3(w["lru_b_x"]), lam=row3(w["lru_lambda"]),
                 gq=row3(full["mla_q_norm_g"]), gkv=row3(full["mla_kv_norm_g"]))

    loss_part, grad_x, g = _local_step(x[0], positions.reshape(t, 1), tgt[0], small, weights_of, grads_done,
                                       start_dep=token, prefetch=early_pass)

    own = {name: [None] * w[name].shape[0] for name in BIG}
    me_arr = me.astype(jnp.int32).reshape(1)
    out = {}
    local_g = [jnp.stack(g[key]).reshape(_global_shape(w[name].shape, axis)) for name, key, axis in SMALL]
    local_g.append(loss_part.reshape(1))
    part = _pack(local_g, chunk).reshape(N_DEV, -1, 128)
    small_plan = _scatter_plan([None])
    s_send, s_recv, s_src, s_land, after = _exchange_start([part], [lax.empty(part.shape, F32)], small_plan,
                                                           "small_scatter_start", after=(last_token[0],))
    for n_flight, (send, recv, srcs, keys, layers) in enumerate(sent):
        if n_flight == len(sent) - 1:
            for name in BIG:
                if BIG_KEY[name] not in keys:
                    out[name] = _adam_big(zone[name], own[name], me_arr, w[name], m[name], v[name], "adam_" + name)
            s_src, s_land = _exchange_wait(s_send, s_recv, s_src, s_land, small_plan,
                                           [grad_x] + [o[0] for o in out.values()], "small_scatter_wait")
            chunk_sum = _sum_blocks(s_land[0], s_src[0], me_arr)
            r_zone = lax.dynamic_update_slice_in_dim(lax.empty(part.shape, F32), chunk_sum[None], me, 0)
            r_send, r_recv, _, r_land, after = _exchange_start([], [r_zone], GATHER_ALL, "small_gather_start")
        srcs, lands = _exchange_wait(send, recv, srcs, [zone[name_of[k]] for k in keys], _scatter_plan(layers),
                                     after, "scatter_wait_%d" % n_flight)
        for k, land, src, layer in zip(keys, lands, srcs, layers):
            zone[name_of[k]] = land
            own[name_of[k]][layer] = src
        after = lands[0]
    for name in BIG:
        if name not in out:
            out[name] = _adam_big(zone[name], own[name], me_arr, w[name], m[name], v[name], "adam_" + name)

    _, reduced = _exchange_wait(r_send, r_recv, [], r_land, GATHER_ALL, [out[name][0] for name in BIG],
                                "small_gather_wait")
    reduced = _unpack(reduced[0].reshape(-1), [a.shape for a in local_g])
    loss = reduced[-1][0]
    mine = [a if axis is None else lax.dynamic_slice_in_dim(a, me * w[name].shape[axis], w[name].shape[axis], axis)
            for a, (name, _, axis) in zip(reduced, SMALL)]
    names = [name for name, _, _ in SMALL]
    as_2d = lambda a: a.reshape(-1, a.shape[-1])
    new = _adam_small([as_2d(a) for a in mine], *([as_2d(src[name]) for name in names] for src in (w, m, v)))
    for i, name in enumerate(names):
        out[name] = (mine[i],) + tuple(part[i].reshape(w[name].shape) for part in new)

    return (loss, grad_x[None]) + tuple(out[name][i] for i in range(4) for name in WEIGHTS)


def kernel(x, positions, ln_mix_g, ln_mix_b, ln_ffn_g, ln_ffn_b, even_w_in, pool_w, pool_scale, lru_conv_w, lru_conv_b, lru_w_a, lru_b_a, lru_w_x, lru_b_x, lru_lambda, even_w_out, mla_w_down, mla_q_norm_g, mla_kv_norm_g, mla_w_qb, mla_w_kvb, mla_w_o, mlp_w1, mlp_w2, loss_target, m_ln_mix_g, m_ln_mix_b, m_ln_ffn_g, m_ln_ffn_b, m_even_w_in, m_pool_w, m_pool_scale, m_lru_conv_w, m_lru_conv_b, m_lru_w_a, m_lru_b_a, m_lru_w_x, m_lru_b_x, m_lru_lambda, m_even_w_out, m_mla_w_down, m_mla_q_norm_g, m_mla_kv_norm_g, m_mla_w_qb, m_mla_w_kvb, m_mla_w_o, m_mlp_w1, m_mlp_w2, v_ln_mix_g, v_ln_mix_b, v_ln_ffn_g, v_ln_ffn_b, v_even_w_in, v_pool_w, v_pool_scale, v_lru_conv_w, v_lru_conv_b, v_lru_w_a, v_lru_b_a, v_lru_w_x, v_lru_b_x, v_lru_lambda, v_even_w_out, v_mla_w_down, v_mla_q_norm_g, v_mla_kv_norm_g, v_mla_w_qb, v_mla_w_kvb, v_mla_w_o, v_mlp_w1, v_mlp_w2):
    w = dict(zip(WEIGHTS, (ln_mix_g, ln_mix_b, ln_ffn_g, ln_ffn_b, even_w_in, pool_w, pool_scale, lru_conv_w,
                           lru_conv_b, lru_w_a, lru_b_a, lru_w_x, lru_b_x, lru_lambda, even_w_out, mla_w_down,
                           mla_q_norm_g, mla_kv_norm_g, mla_w_qb, mla_w_kvb, mla_w_o, mlp_w1, mlp_w2)))
    m = dict(zip(WEIGHTS, (m_ln_mix_g, m_ln_mix_b, m_ln_ffn_g, m_ln_ffn_b, m_even_w_in, m_pool_w, m_pool_scale,
                           m_lru_conv_w, m_lru_conv_b, m_lru_w_a, m_lru_b_a, m_lru_w_x, m_lru_b_x, m_lru_lambda,
                           m_even_w_out, m_mla_w_down, m_mla_q_norm_g, m_mla_kv_norm_g, m_mla_w_qb, m_mla_w_kvb,
                           m_mla_w_o, m_mlp_w1, m_mlp_w2)))
    v = dict(zip(WEIGHTS, (v_ln_mix_g, v_ln_mix_b, v_ln_ffn_g, v_ln_ffn_b, v_even_w_in, v_pool_w, v_pool_scale,
                           v_lru_conv_w, v_lru_conv_b, v_lru_w_a, v_lru_b_a, v_lru_w_x, v_lru_b_x, v_lru_lambda,
                           v_even_w_out, v_mla_w_down, v_mla_q_norm_g, v_mla_kv_norm_g, v_mla_w_qb, v_mla_w_kvb,
                           v_mla_w_o, v_mlp_w1, v_mlp_w2)))
    return _step(x, positions, loss_target, w, m, v)
```

```python
import jax
import jax.numpy as jnp
from jax import lax
from jax.experimental import pallas as pl
from jax.experimental.pallas import tpu as pltpu

F32 = jnp.float32
BF16 = jnp.bfloat16
S = jax.ShapeDtypeStruct

D = 1024
DEPTH = 4
N_DEV = 8
CHUNK_SHIFT = 6
POOL_WINDOWS = (2, 4, 8, 16)
POOL_W = 512
LRU_W = 1024
LRU_HEADS = 8
HEAD = 128
LRU_C = 8.0
EVEN_IN = 2560
EVEN_MIX = 1536
MLA_HEADS = 8
NOPE = 128
ROPE = 64
VDIM = 128
Q_RANK = 384
KV_RANK = 256
ODD_IN = 704
D_FF = 4096
FF_BLK = D_FF // N_DEV
ROPE_THETA = 10000.0
ALPHA = (2 * DEPTH) ** 0.25
LN_EPS = 1e-5
RMS_EPS = 1e-6
ATT_SCALE = (NOPE + ROPE) ** -0.5
NEG = float(jnp.finfo(jnp.float32).min)
ADAM_LR = 0.001
ADAM_B1 = 0.9
ADAM_B2 = 0.999
ADAM_EPS = 1e-08
ADAM_WD = 0.01
ADAM_STEP = 10
V7X_VMEM_BYTES = 64 * 1024 * 1024
VMEM_LIMIT = V7X_VMEM_BYTES - 8 * 1024 * 1024
MESH = pl.DeviceIdType.MESH


def _cp(*sem):
    return pltpu.CompilerParams(dimension_semantics=sem if sem else None, vmem_limit_bytes=VMEM_LIMIT)


def _dot(a, b):
    return jnp.dot(a, b, preferred_element_type=F32)


def _dot_nt(a, b):
    return lax.dot_general(a, b, (((1,), (1,)), ((), ())), preferred_element_type=F32)


def _dot_tn(a, b):
    return lax.dot_general(a, b, (((0,), (0,)), ((), ())), preferred_element_type=F32)


def _full(shape):
    return pl.BlockSpec(shape, lambda *_: (0,) * len(shape))


def _mm(a, b, *, mode, grid, a_spec, b_spec, out_shape, out_spec, name, add=None, add_spec=None, add_scale=1.0,
        dep=None):
    dot = {"nn": _dot, "nt": _dot_nt, "tn": _dot_tn}[mode]

    def body(*refs):
        a_ref, b_ref, o_ref = refs[0], refs[1], refs[-1]
        acc = dot(a_ref[...].astype(BF16), b_ref[...].astype(BF16))
        if add is not None:
            acc = acc + add_scale * refs[2][...]
        o_ref[...] = acc.astype(o_ref.dtype)

    ops = [a, b] if add is None else [a, b, add]
    specs = [a_spec, b_spec] if add is None else [a_spec, b_spec, add_spec]
    if dep is not None:
        ops.append(dep)
        specs.append(pl.BlockSpec(memory_space=pl.ANY))
    return pl.pallas_call(body, grid=grid, in_specs=specs, out_specs=out_spec, out_shape=out_shape,
                          compiler_params=_cp(*(("parallel",) * len(grid))), name=name)(*ops)


def _even_dwin(xb, dproj):
    shard = EVEN_IN // N_DEV

    def body(x_ref, dp_ref, o_ref):
        xv = x_ref[...].astype(BF16)
        for d in range(N_DEV):
            o_ref[d] = _dot_tn(xv, dp_ref[:, d * shard:(d + 1) * shard]).astype(BF16)

    return pl.pallas_call(body, out_shape=S((N_DEV, D, shard), BF16), compiler_params=_cp(), name="even_dwin")(xb, dproj)


def _ln_stats(z):
    mu = jnp.mean(z, axis=-1, keepdims=True)
    zc = z - mu
    var = jnp.mean(zc * zc, axis=-1, keepdims=True)
    rstd = lax.rsqrt(var + LN_EPS)
    return zc * rstd, rstd


def _row_tile(t):
    return min(1024, t)


def _resid_ln(x, mix, g3, b3, l, name):
    t = x.shape[0]
    bm = _row_tile(t)

    def body(x_ref, m_ref, g_ref, b_ref, z_ref, y_ref, yb_ref):
        z = ALPHA * x_ref[...] + m_ref[...]
        xh, _ = _ln_stats(z)
        y = xh * g_ref[...] + b_ref[...]
        z_ref[...] = z
        y_ref[...] = y
        yb_ref[...] = y.astype(BF16)

    row = pl.BlockSpec((bm, D), lambda i: (i, 0))
    vec = pl.BlockSpec((None, 1, D), lambda i: (l, 0, 0))
    return pl.pallas_call(body, grid=(t // bm,), in_specs=[row, row, vec, vec], out_specs=[row, row, row],
                          out_shape=[S((t, D), F32), S((t, D), F32), S((t, D), BF16)],
                          compiler_params=_cp("parallel"), name=name)(x, mix, g3, b3)


def _proj_resid_ln(x, a, wmat, g3, b3, l, name):
    t, k = a.shape
    bm = _row_tile(t)

    def body(x_ref, a_ref, w_ref, g_ref, b_ref, z_ref, y_ref, yb_ref):
        z = ALPHA * x_ref[...] + _dot(a_ref[...], w_ref[...])
        xh, _ = _ln_stats(z)
        y = xh * g_ref[...] + b_ref[...]
        z_ref[...] = z
        y_ref[...] = y
        yb_ref[...] = y.astype(BF16)

    row = pl.BlockSpec((bm, D), lambda i: (i, 0))
    vec = pl.BlockSpec((None, 1, D), lambda i: (l, 0, 0))
    return pl.pallas_call(body, grid=(t // bm,),
                          in_specs=[row, pl.BlockSpec((bm, k), lambda i: (i, 0)), _full((k, D)), vec, vec],
                          out_specs=[row, row, row], out_shape=[S((t, D), F32), S((t, D), F32), S((t, D), BF16)],
                          compiler_params=_cp("parallel"), name=name)(x, a, wmat, g3, b3)


def _ln_bwd(d, z, g3, l, name, r=None, dep=None):
    t = z.shape[0]
    bm = _row_tile(t)

    def body(*refs):
        refs = list(refs)
        d_ref = refs.pop(0)
        dy = d_ref[...]
        if r is not None:
            dy = dy + ALPHA * refs.pop(0)[...]
        z_ref, g_ref = refs.pop(0), refs.pop(0)
        if dep is not None:
            refs.pop(0)
        dz_ref, dzb_ref, dg_ref, db_ref = refs
        xh, rstd = _ln_stats(z_ref[...])
        dyg = dy * g_ref[...]
        m1 = jnp.mean(dyg, axis=-1, keepdims=True)
        m2 = jnp.mean(dyg * xh, axis=-1, keepdims=True)
        dz = rstd * (dyg - m1 - xh * m2)
        dz_ref[...] = dz
        dzb_ref[...] = dz.astype(BF16)

        @pl.when(pl.program_id(0) == 0)
        def _():
            dg_ref[...] = jnp.zeros_like(dg_ref)
            db_ref[...] = jnp.zeros_like(db_ref)

        dg_ref[...] += jnp.sum(dy * xh, axis=0, keepdims=True)
        db_ref[...] += jnp.sum(dy, axis=0, keepdims=True)

    row = pl.BlockSpec((bm, D), lambda i: (i, 0))
    vec = pl.BlockSpec((None, 1, D), lambda i: (l, 0, 0))
    acc = pl.BlockSpec((1, D), lambda i: (0, 0))
    ops = [d, z, g3] if r is None else [d, r, z, g3]
    specs = [row, row, vec] if r is None else [row, row, row, vec]
    if dep is not None:
        ops.append(dep)
        specs.append(_full(dep.shape))
    return pl.pallas_call(body, grid=(t // bm,), in_specs=specs, out_specs=[row, row, acc, acc],
                          out_shape=[S((t, D), F32), S((t, D), BF16), S((1, D), F32), S((1, D), F32)],
                          compiler_params=_cp("arbitrary"), name=name)(*ops)


def _loss_grad(y, tgt):
    t = y.shape[0]
    bm = _row_tile(t)

    def body(y_ref, t_ref, dy_ref, loss_ref, acc_ref):
        i = pl.program_id(0)
        e = y_ref[...] - t_ref[...]
        dy_ref[...] = e * (1.0 / D)

        @pl.when(i == 0)
        def _():
            acc_ref[...] = jnp.zeros_like(acc_ref)

        acc_ref[...] += jnp.sum(e * e, axis=0, keepdims=True)

        @pl.when(i == pl.num_programs(0) - 1)
        def _():
            loss_ref[...] = jnp.full(loss_ref.shape, (0.5 / D) * jnp.sum(acc_ref[...]), F32)

    row = pl.BlockSpec((bm, D), lambda i: (i, 0))
    return pl.pallas_call(body, grid=(t // bm,), in_specs=[row, row],
                          out_specs=[row, pl.BlockSpec((1, 128), lambda i: (0, 0))],
                          out_shape=[S((t, D), F32), S((1, 128), F32)],
                          scratch_shapes=[pltpu.VMEM((1, D), F32)],
                          compiler_params=_cp("arbitrary"), name="loss_grad")(y, tgt)


def _mlp_row_tile(t):
    return min(1024, t)


MLP_ROW_PARTS = 2


def _row_parts(bm):
    step = bm // MLP_ROW_PARTS
    return [slice(k * step, (k + 1) * step) for k in range(MLP_ROW_PARTS)]


def _mlp_fwd(y, yb, w1g, w2g, g3, b3, l, dep=None):
    t = yb.shape[0]
    bm = _mlp_row_tile(t)

    def body(*refs):
        y_ref, yb_ref, w1_ref, w2_ref, g_ref, b_ref = refs[:6]
        z_ref, o_ref, ob_ref, act_ref, acc_ref = refs[-5:]
        j = pl.program_id(1)

        @pl.when(j == 0)
        def _():
            acc_ref[...] = jnp.zeros_like(acc_ref)

        for rows in _row_parts(bm):
            h = jnp.maximum(_dot(yb_ref[rows, :], w1_ref[...]), 0.0)
            act = (h * h).astype(BF16)
            act_ref[rows, :] = act
            acc_ref[rows, :] += _dot(act, w2_ref[...])

        @pl.when(j == N_DEV - 1)
        def _():
            z = ALPHA * y_ref[...] + acc_ref[...]
            xh, _ = _ln_stats(z)
            out = xh * g_ref[...] + b_ref[...]
            z_ref[...] = z
            o_ref[...] = out
            ob_ref[...] = out.astype(BF16)

    row = pl.BlockSpec((bm, D), lambda i, j: (i, 0))
    vec = pl.BlockSpec((None, 1, D), lambda i, j: (l, 0, 0))
    deps = [] if dep is None else [dep]
    return pl.pallas_call(
        body, grid=(t // bm, N_DEV),
        in_specs=[row, row, pl.BlockSpec((None, D, FF_BLK), lambda i, j: (j, 0, 0)),
                  pl.BlockSpec((None, FF_BLK, D), lambda i, j: (j, 0, 0)), vec, vec] + [ANY] * len(deps),
        out_specs=[row, row, row, pl.BlockSpec((bm, FF_BLK), lambda i, j: (i, j))],
        out_shape=[S((t, D), F32), S((t, D), F32), S((t, D), BF16), S((t, D_FF), BF16)],
        scratch_shapes=[pltpu.VMEM((bm, D), F32)],
        compiler_params=_cp("parallel", "arbitrary"), name="mlp_fwd")(y, yb, w1g, w2g, g3, b3, *deps)


def _mlp_bwd_dh(act, dzb, w1g, w2g):
    t = act.shape[0]
    bm = _mlp_row_tile(t)

    def body(a_ref, dz_ref, w1_ref, w2_ref, dh_ref, acc_ref):
        @pl.when(pl.program_id(1) == 0)
        def _():
            acc_ref[...] = jnp.zeros_like(acc_ref)

        for rows in _row_parts(bm):
            r = jnp.sqrt(a_ref[rows, :].astype(F32))
            dh = (_dot_nt(dz_ref[rows, :], w2_ref[...]) * (2.0 * r)).astype(BF16)
            dh_ref[rows, :] = dh
            acc_ref[rows, :] += _dot_nt(dh, w1_ref[...])

    row = pl.BlockSpec((bm, D), lambda i, j: (i, 0))
    hid = pl.BlockSpec((bm, FF_BLK), lambda i, j: (i, j))
    return pl.pallas_call(
        body, grid=(t // bm, N_DEV),
        in_specs=[hid, row,
                  pl.BlockSpec((None, D, FF_BLK), lambda i, j: (j, 0, 0)),
                  pl.BlockSpec((None, FF_BLK, D), lambda i, j: (j, 0, 0))],
        out_specs=[hid, row],
        out_shape=[S((t, D_FF), BF16), S((t, D), F32)],
        compiler_params=_cp("parallel", "arbitrary"), name="mlp_bwd_dh")(act, dzb, w1g, w2g)


F32_SUBLANES = 8


def _shift_dn(x, k, rows, fill=0.0):
    if k % F32_SUBLANES == 0:
        return jnp.concatenate([jnp.full((k,) + x.shape[1:], fill, x.dtype), x[:x.shape[0] - k]], axis=0)
    return jnp.where(rows >= k, pltpu.roll(x, k, 0), fill)


def _shift_up(x, k, rows, fill=0.0):
    t = x.shape[0]
    if k % F32_SUBLANES == 0:
        return jnp.concatenate([x[k:], jnp.full((k,) + x.shape[1:], fill, x.dtype)], axis=0)
    return jnp.where(rows < t - k, pltpu.roll(x, t - k, 0), fill)


def _scan_rows(a, b, shift):
    rows = lax.broadcasted_iota(jnp.int32, a.shape, 0)
    k = 1
    t = a.shape[0]
    while k < t:
        b = a * shift(b, k, rows) + b
        if 2 * k < t:
            a = a * shift(a, k, rows, 1.0)
        k *= 2
    return b


def _scan_dn(a, b):
    return _scan_rows(a, b, _shift_dn)


def _scan_up(a, b):
    return _scan_rows(a, b, _shift_up)


def _window_sum_dn(x, w, rows):
    k = 1
    while k < w:
        x = x + _shift_dn(x, k, rows)
        k *= 2
    return x


def _window_sum_up(x, w, rows):
    k = 1
    while k < w:
        x = x + _shift_up(x, k, rows)
        k *= 2
    return x


def _pool_diff(u, w, rows):
    inv_count = 1.0 / jnp.minimum(rows + 1, w).astype(F32)
    return _window_sum_dn(u, w, rows) * inv_count - u, inv_count


def _pool_fwd(proj, pool_w, pool_scale3, j):
    t = proj.shape[0]

    def body(u_ref, w_ref, s_ref, y_ref):
        rows = lax.broadcasted_iota(jnp.int32, (t, HEAD), 0)
        for g, w in enumerate(POOL_WINDOWS):
            cols = slice(g * HEAD, (g + 1) * HEAD)
            d, _ = _pool_diff(u_ref[:, cols], w, rows)
            y = _dot(d.astype(BF16), w_ref[g].astype(BF16)) * s_ref[:, cols]
            y_ref[:, cols] = y.astype(BF16)

    return pl.pallas_call(
        body, grid=(1,),
        in_specs=[pl.BlockSpec((t, POOL_W), lambda i: (0, 0)),
                  pl.BlockSpec((None, 4, HEAD, HEAD), lambda i: (j, 0, 0, 0)),
                  pl.BlockSpec((None, 1, POOL_W), lambda i: (j, 0, 0))],
        out_specs=pl.BlockSpec((t, POOL_W), lambda i: (0, 0)),
        out_shape=S((t, POOL_W), BF16), compiler_params=_cp("arbitrary"), name="pool_fwd")(proj, pool_w, pool_scale3)


def _pool_bwd(proj, dycat, pool_w, pool_scale3, j):
    t = proj.shape[0]

    def body(u_ref, dy_ref, w_ref, s_ref, du_ref, dw_ref, ds_ref):
        rows = lax.broadcasted_iota(jnp.int32, (t, HEAD), 0)
        for g, w in enumerate(POOL_WINDOWS):
            cols = slice(g * HEAD, (g + 1) * HEAD)
            d, inv_count = _pool_diff(u_ref[:, cols], w, rows)
            db = d.astype(BF16)
            wg = w_ref[g].astype(BF16)
            dy = dy_ref[:, cols]
            ds_ref[:, cols] = jnp.sum(dy * _dot(db, wg), axis=0, keepdims=True)
            dzz = (dy * s_ref[:, cols]).astype(BF16)
            dw_ref[g] = _dot_tn(db, dzz)
            dd = _dot_nt(dzz, wg)
            du_ref[:, cols] = (_window_sum_up(dd * inv_count, w, rows) - dd).astype(BF16)

    return pl.pallas_call(
        body, grid=(1,),
        in_specs=[pl.BlockSpec((t, POOL_W), lambda i: (0, 0)),
                  pl.BlockSpec((t, POOL_W), lambda i: (0, 0)),
                  pl.BlockSpec((None, 4, HEAD, HEAD), lambda i: (j, 0, 0, 0)),
                  pl.BlockSpec((None, 1, POOL_W), lambda i: (j, 0, 0))],
        out_specs=[pl.BlockSpec((t, POOL_W), lambda i: (0, 0)), _full((4, HEAD, HEAD)), _full((1, POOL_W))],
        out_shape=[S((t, POOL_W), BF16), S((4, HEAD, HEAD), F32), S((1, POOL_W), F32)],
        compiler_params=_cp("arbitrary"), name="pool_bwd")(proj, dycat, pool_w, pool_scale3)


GELU_C = 0.7978845608028654
GELU_K = 0.044715


def _gelu(x):
    th = jnp.tanh(GELU_C * (x + GELU_K * x * x * x))
    return 0.5 * x * (1.0 + th), th


def _lru_forward(u, gate, cw, cb, wa, ba, wx, bx, lam, rows):
    v = cw[3:4] * u + cw[2:3] * _shift_dn(u, 1, rows) + cw[1:2] * _shift_dn(u, 2, rows) \
        + cw[0:1] * _shift_dn(u, 3, rows) + cb
    vb = v.astype(BF16)
    r = jax.nn.sigmoid(_dot(vb, wa) + ba)
    i = jax.nn.sigmoid(_dot(vb, wx) + bx)
    sp = jnp.maximum(-lam, 0.0) + jnp.log1p(jnp.exp(-jnp.abs(lam)))
    log_a = (-LRU_C) * r * sp
    a = jnp.exp(log_a)
    one_m_a2 = -jnp.tanh(log_a) * (a * a + 1.0)
    mult = jnp.sqrt(one_m_a2)
    h = _scan_dn(a, mult * (i * v))
    gl, th = _gelu(gate)
    return dict(v=v, vb=vb, r=r, i=i, sp=sp, a=a, mult=mult, h=h, gl=gl, th=th)


def _lru_specs(t, j, col0_u, col0_g):
    blk = lambda c0: pl.BlockSpec((t, HEAD), lambda h: (0, c0 + h))
    vec = pl.BlockSpec((None, 1, HEAD), lambda h: (j, 0, h))
    return [blk(col0_u), blk(col0_g),
            pl.BlockSpec((None, 4, HEAD), lambda h: (j, 0, h)), vec,
            pl.BlockSpec((None, None, HEAD, HEAD), lambda h: (j, h, 0, 0)), vec,
            pl.BlockSpec((None, None, HEAD, HEAD), lambda h: (j, h, 0, 0)), vec, vec]


def _lru_fwd(proj, p, j):
    t = proj.shape[0]

    def body(u_ref, g_ref, cw_ref, cb_ref, wa_ref, ba_ref, wx_ref, bx_ref, lam_ref, y_ref):
        rows = lax.broadcasted_iota(jnp.int32, (t, HEAD), 0)
        f = _lru_forward(u_ref[...], g_ref[...], cw_ref[...], cb_ref[...], wa_ref[...].astype(BF16), ba_ref[...],
                         wx_ref[...].astype(BF16), bx_ref[...], lam_ref[...], rows)
        y_ref[...] = (f["h"] * f["gl"]).astype(BF16)

    return pl.pallas_call(
        body, grid=(LRU_HEADS,), in_specs=_lru_specs(t, j, POOL_W // HEAD, (POOL_W + LRU_W) // HEAD),
        out_specs=pl.BlockSpec((t, HEAD), lambda h: (0, h)), out_shape=S((t, LRU_W), BF16),
        compiler_params=_cp("parallel"), name="lru_fwd")(
            proj, proj, p["conv_w"], p["conv_b"], p["w_a"], p["b_a"], p["w_x"], p["b_x"], p["lam"])


def _lru_bwd(proj, dycat, p, j):
    t = proj.shape[0]

    def body(u_ref, g_ref, cw_ref, cb_ref, wa_ref, ba_ref, wx_ref, bx_ref, lam_ref, dy_ref,
             du_ref, dgate_ref, dcw_ref, dcb_ref, dwa_ref, dba_ref, dwx_ref, dbx_ref, dlam_ref):
        rows = lax.broadcasted_iota(jnp.int32, (t, HEAD), 0)
        u = u_ref[...]
        gate = g_ref[...]
        cw = cw_ref[...]
        wa = wa_ref[...].astype(BF16)
        wx = wx_ref[...].astype(BF16)
        lam = lam_ref[...]
        f = _lru_forward(u, gate, cw, cb_ref[...], wa, ba_ref[...], wx, bx_ref[...], lam, rows)
        v, r, i, a, mult, h, th = f["v"], f["r"], f["i"], f["a"], f["mult"], f["h"], f["th"]
        dy = dy_ref[...]
        dgl = 0.5 * (1.0 + th) + 0.5 * gate * (1.0 - th * th) * GELU_C * (1.0 + 3.0 * GELU_K * gate * gate)
        dgate_ref[...] = (dy * h * dgl).astype(BF16)
        g = _scan_up(_shift_up(a, 1, rows), dy * f["gl"])
        da = g * _shift_dn(h, 1, rows)
        iv = i * v
        dmult = g * iv
        di = g * mult * v
        dv = g * mult * i
        dlog_a = da * a - dmult * (a * a) / mult
        dr = dlog_a * (-LRU_C) * f["sp"]
        dsp = jnp.sum(dlog_a * (-LRU_C) * r, axis=0, keepdims=True)
        dlam_ref[...] = -dsp * jax.nn.sigmoid(-lam)
        dpa = dr * r * (1.0 - r)
        dpx = di * i * (1.0 - i)
        dpab = dpa.astype(BF16)
        dpxb = dpx.astype(BF16)
        dwa_ref[...] = _dot_tn(f["vb"], dpab)
        dwx_ref[...] = _dot_tn(f["vb"], dpxb)
        dba_ref[...] = jnp.sum(dpa, axis=0, keepdims=True)
        dbx_ref[...] = jnp.sum(dpx, axis=0, keepdims=True)
        dv = dv + _dot_nt(dpab, wa) + _dot_nt(dpxb, wx)
        dcb_ref[...] = jnp.sum(dv, axis=0, keepdims=True)
        du = cw[3:4] * dv
        dcw_ref[3:4, :] = jnp.sum(dv * u, axis=0, keepdims=True)
        for k in (1, 2, 3):
            du = du + cw[3 - k:4 - k] * _shift_up(dv, k, rows)
            dcw_ref[3 - k:4 - k, :] = jnp.sum(dv * _shift_dn(u, k, rows), axis=0, keepdims=True)
        du_ref[...] = du.astype(BF16)

    blk = pl.BlockSpec((t, HEAD), lambda h: (0, h))
    vec = pl.BlockSpec((1, HEAD), lambda h: (0, h))
    mat = pl.BlockSpec((None, HEAD, HEAD), lambda h: (h, 0, 0))
    return pl.pallas_call(
        body, grid=(LRU_HEADS,),
        in_specs=_lru_specs(t, j, POOL_W // HEAD, (POOL_W + LRU_W) // HEAD)
        + [pl.BlockSpec((t, HEAD), lambda h: (0, POOL_W // HEAD + h))],
        out_specs=[blk, blk, pl.BlockSpec((4, HEAD), lambda h: (0, h)), vec, mat, vec, mat, vec, vec],
        out_shape=[S((t, LRU_W), BF16), S((t, LRU_W), BF16), S((4, LRU_W), F32), S((1, LRU_W), F32),
                   S((LRU_HEADS, HEAD, HEAD), F32), S((1, LRU_W), F32),
                   S((LRU_HEADS, HEAD, HEAD), F32), S((1, LRU_W), F32), S((1, LRU_W), F32)],
        compiler_params=_cp("parallel"), name="lru_bwd")(
            proj, proj, p["conv_w"], p["conv_b"], p["w_a"], p["b_a"], p["w_x"], p["b_x"], p["lam"], dycat)


def _rope(x, c, s):
    x1 = x[:, :ROPE // 2]
    x2 = x[:, ROPE // 2:]
    return jnp.concatenate([x1 * c - x2 * s, x1 * s + x2 * c], axis=-1)


def _rope_t(d, c, s):
    d1 = d[:, :ROPE // 2]
    d2 = d[:, ROPE // 2:]
    return jnp.concatenate([d1 * c + d2 * s, d2 * c - d1 * s], axis=-1)


def _rope_tables(pos2, inv_freq):
    t = pos2.shape[0]

    def body(p_ref, f_ref, c_ref, s_ref):
        ang = p_ref[...].astype(F32) * f_ref[...]
        c_ref[...] = jnp.cos(ang)
        s_ref[...] = jnp.sin(ang)

    return pl.pallas_call(body, out_shape=[S((t, ROPE // 2), F32), S((t, ROPE // 2), F32)],
                          name="rope_tables")(pos2, inv_freq)


def _down_norm(xb, wdown_g, gq3, gkv3, cos, sin, j):
    t = xb.shape[0]
    bm = _row_tile(t)

    def body(x_ref, w_ref, gq_ref, gkv_ref, c_ref, s_ref, down_ref, cq_ref, ckv_ref, kpe_ref):
        w = w_ref[...].reshape(D, ODD_IN)
        down = _dot(x_ref[...], w)
        down_ref[...] = down
        q = down[:, :Q_RANK]
        cq_ref[...] = (q * lax.rsqrt(jnp.mean(q * q, axis=-1, keepdims=True) + RMS_EPS) * gq_ref[...]).astype(BF16)
        kv = down[:, Q_RANK:Q_RANK + KV_RANK]
        ckv_ref[...] = (kv * lax.rsqrt(jnp.mean(kv * kv, axis=-1, keepdims=True) + RMS_EPS)
                        * gkv_ref[...]).astype(BF16)
        kpe_ref[...] = _rope(down[:, Q_RANK + KV_RANK:], c_ref[...], s_ref[...])

    row = lambda n: pl.BlockSpec((bm, n), lambda i: (i, 0))
    return pl.pallas_call(
        body, grid=(t // bm,),
        in_specs=[row(D), _full((N_DEV, D // N_DEV, ODD_IN)),
                  pl.BlockSpec((None, 1, Q_RANK), lambda i: (j, 0, 0)),
                  pl.BlockSpec((None, 1, KV_RANK), lambda i: (j, 0, 0)), row(ROPE // 2), row(ROPE // 2)],
        out_specs=[row(ODD_IN), row(Q_RANK), row(KV_RANK), row(ROPE)],
        out_shape=[S((t, ODD_IN), F32), S((t, Q_RANK), BF16), S((t, KV_RANK), BF16), S((t, ROPE), F32)],
        compiler_params=_cp("parallel"), name="down_norm")(xb, wdown_g, gq3, gkv3, cos, sin)


def _q_tile(t, widest):
    return min(widest, t // 2)


def _attn_probs(q, k, qs):
    s = _dot_nt(q, k) * ATT_SCALE
    tq = q.shape[0]
    rows = lax.broadcasted_iota(jnp.int32, (tq, tq), 0)
    cols = lax.broadcasted_iota(jnp.int32, (tq, tq), 1)
    last = jnp.where(jnp.right_shift(cols, CHUNK_SHIFT) <= jnp.right_shift(rows, CHUNK_SHIFT), s[:, qs:], NEG)
    s = last if qs == 0 else jnp.concatenate([s[:, :qs], last], axis=1)
    e = jnp.exp(s - jnp.max(s, axis=-1, keepdims=True))
    return e / jnp.sum(e, axis=-1, keepdims=True)


def _head_qkv(cq, ckv, kpe, c, s, wq_ref, wkv_ref):
    q = jnp.concatenate([_dot(cq, wq_ref[:, :NOPE]), _rope(_dot(cq, wq_ref[:, NOPE:]), c, s)], axis=1).astype(BF16)
    k = jnp.concatenate([_dot(ckv, wkv_ref[:, :NOPE]), kpe], axis=1).astype(BF16)
    vv = _dot(ckv, wkv_ref[:, NOPE:]).astype(BF16)
    return q, k, vv


def _attn_in_specs(t):
    return [_full((t, Q_RANK)), _full((t, KV_RANK)), _full((t, ROPE)), _full((t, ROPE // 2)), _full((t, ROPE // 2)),
            pl.BlockSpec((None, Q_RANK, NOPE + ROPE), lambda h: (h, 0, 0)),
            pl.BlockSpec((None, KV_RANK, NOPE + VDIM), lambda h: (h, 0, 0)),
            pl.BlockSpec((None, VDIM, D), lambda h: (h, 0, 0))]


def _attn_fwd(cq, ckv, kpe, cos, sin, wqb_g, wkvb_g, wo_g):
    t = cq.shape[0]
    tq = _q_tile(t, 256)

    def body(cq_ref, ckv_ref, kpe_ref, c_ref, s_ref, wq_ref, wkv_ref, wo_ref, o_ref, mix_ref):
        q, k, vv = _head_qkv(cq_ref[...], ckv_ref[...], kpe_ref[...], c_ref[...], s_ref[...], wq_ref, wkv_ref)
        for qs in range(0, t, tq):
            ke = qs + tq
            p = _attn_probs(q[qs:ke], k[:ke], qs)
            o_ref[qs:ke, :] = _dot(p.astype(BF16), vv[:ke]).astype(BF16)
        c = _dot(o_ref[...], wo_ref[...])

        @pl.when(pl.program_id(0) == 0)
        def _():
            mix_ref[...] = c

        @pl.when(pl.program_id(0) > 0)
        def _():
            mix_ref[...] += c

    return pl.pallas_call(
        body, grid=(MLA_HEADS,), in_specs=_attn_in_specs(t),
        out_specs=[pl.BlockSpec((None, t, VDIM), lambda h: (h, 0, 0)), _full((t, D))],
        out_shape=[S((MLA_HEADS, t, VDIM), BF16), S((t, D), F32)],
        compiler_params=_cp("arbitrary"), name="attn_fwd")(cq, ckv, kpe, cos, sin, wqb_g, wkvb_g, wo_g)


def _attn_bwd(cq, ckv, kpe, cos, sin, wqb_g, wkvb_g, wo_g, o, dzb):
    t = cq.shape[0]
    tq = _q_tile(t, 512)

    def body(cq_ref, ckv_ref, kpe_ref, c_ref, s_ref, wq_ref, wkv_ref, wo_ref, o_ref, dz_ref,
             dwo_ref, dwq_ref, dwkv_ref, dcq_ref, dckv_ref, dkpe_ref, dkt_s, dvt_s, dq_s):
        cqv = cq_ref[...]
        ckvv = ckv_ref[...]
        c = c_ref[...]
        s = s_ref[...]
        q, k, vv = _head_qkv(cqv, ckvv, kpe_ref[...], c, s, wq_ref, wkv_ref)
        dzv = dz_ref[...]
        dwo_ref[...] = _dot_tn(o_ref[...], dzv).astype(BF16)
        do = _dot_nt(dzv, wo_ref[...]).astype(BF16)
        dkt_s[...] = jnp.zeros_like(dkt_s)
        dvt_s[...] = jnp.zeros_like(dvt_s)
        for qs in range(0, t, tq):
            ke = qs + tq
            p = _attn_probs(q[qs:ke], k[:ke], qs)
            dp = _dot_nt(do[qs:ke], vv[:ke])
            ds = (p * (dp - jnp.sum(p * dp, axis=-1, keepdims=True)) * ATT_SCALE).astype(BF16)
            dq_s[qs:ke, :] = _dot(ds, k[:ke])
            dkt_s[0:NOPE + ROPE, 0:ke] += _dot_tn(q[qs:ke], ds)
            dvt_s[:, 0:ke] += _dot_tn(do[qs:ke], p.astype(BF16))
        dk = dkt_s[...].T
        dqn = dq_s[:, :NOPE].astype(BF16)
        dqp = _rope_t(dq_s[:, NOPE:], c, s).astype(BF16)
        dkn = dk[:, :NOPE].astype(BF16)
        dkp = dk[:, NOPE:NOPE + ROPE]
        dvv = dvt_s[...].T.astype(BF16)
        dwq_ref[:, :NOPE] = _dot_tn(cqv, dqn).astype(BF16)
        dwq_ref[:, NOPE:] = _dot_tn(cqv, dqp).astype(BF16)
        dwkv_ref[:, :NOPE] = _dot_tn(ckvv, dkn).astype(BF16)
        dwkv_ref[:, NOPE:] = _dot_tn(ckvv, dvv).astype(BF16)
        dcq = _dot_nt(dqn, wq_ref[:, :NOPE]) + _dot_nt(dqp, wq_ref[:, NOPE:])
        dckv = _dot_nt(dkn, wkv_ref[:, :NOPE]) + _dot_nt(dvv, wkv_ref[:, NOPE:])

        @pl.when(pl.program_id(0) == 0)
        def _():
            dcq_ref[...] = dcq
            dckv_ref[...] = dckv
            dkpe_ref[...] = dkp

        @pl.when(pl.program_id(0) > 0)
        def _():
            dcq_ref[...] += dcq
            dckv_ref[...] += dckv
            dkpe_ref[...] += dkp

    per_head = lambda a, b: pl.BlockSpec((None, a, b), lambda h: (h, 0, 0))
    return pl.pallas_call(
        body, grid=(MLA_HEADS,),
        in_specs=_attn_in_specs(t) + [per_head(t, VDIM), _full((t, D))],
        out_specs=[per_head(VDIM, D), per_head(Q_RANK, NOPE + ROPE), per_head(KV_RANK, NOPE + VDIM),
                   _full((t, Q_RANK)), _full((t, KV_RANK)), _full((t, ROPE))],
        out_shape=[S((MLA_HEADS, VDIM, D), BF16), S((MLA_HEADS, Q_RANK, NOPE + ROPE), BF16),
                   S((MLA_HEADS, KV_RANK, NOPE + VDIM), BF16),
                   S((t, Q_RANK), F32), S((t, KV_RANK), F32), S((t, ROPE), F32)],
        scratch_shapes=[pltpu.VMEM((2 * NOPE, t), F32), pltpu.VMEM((VDIM, t), F32),
                        pltpu.VMEM((t, NOPE + ROPE), F32)],
        compiler_params=_cp("arbitrary"), name="attn_bwd")(cq, ckv, kpe, cos, sin, wqb_g, wkvb_g, wo_g, o, dzb)


def _rms_bwd(down, dcq, dckv, dkpe, cos, sin, gq3, gkv3, j):
    t = down.shape[0]
    bm = _row_tile(t)

    def body(down_ref, dcq_ref, dckv_ref, dkpe_ref, c_ref, s_ref, gq_ref, gkv_ref, dd_ref, dgq_ref, dgkv_ref):
        @pl.when(pl.program_id(0) == 0)
        def _():
            dgq_ref[...] = jnp.zeros_like(dgq_ref)
            dgkv_ref[...] = jnp.zeros_like(dgkv_ref)

        def rms_b(x, dy, g):
            rstd = lax.rsqrt(jnp.mean(x * x, axis=-1, keepdims=True) + RMS_EPS)
            xh = x * rstd
            dyg = dy * g
            return rstd * (dyg - xh * jnp.mean(dyg * xh, axis=-1, keepdims=True)), jnp.sum(dy * xh, axis=0, keepdims=True)

        dq, dgq = rms_b(down_ref[:, :Q_RANK], dcq_ref[...], gq_ref[...])
        dkv, dgkv = rms_b(down_ref[:, Q_RANK:Q_RANK + KV_RANK], dckv_ref[...], gkv_ref[...])
        dgq_ref[...] += dgq
        dgkv_ref[...] += dgkv
        dd_ref[:, :Q_RANK] = dq.astype(BF16)
        dd_ref[:, Q_RANK:Q_RANK + KV_RANK] = dkv.astype(BF16)
        dd_ref[:, Q_RANK + KV_RANK:] = _rope_t(dkpe_ref[...], c_ref[...], s_ref[...]).astype(BF16)

    row = lambda n: pl.BlockSpec((bm, n), lambda i: (i, 0))
    return pl.pallas_call(
        body, grid=(t // bm,),
        in_specs=[row(ODD_IN), row(Q_RANK), row(KV_RANK), row(ROPE), row(ROPE // 2), row(ROPE // 2),
                  pl.BlockSpec((None, 1, Q_RANK), lambda i: (j, 0, 0)),
                  pl.BlockSpec((None, 1, KV_RANK), lambda i: (j, 0, 0))],
        out_specs=[row(ODD_IN), _full((1, Q_RANK)), _full((1, KV_RANK))],
        out_shape=[S((t, ODD_IN), BF16), S((1, Q_RANK), F32), S((1, KV_RANK), F32)],
        compiler_params=_cp("arbitrary"), name="rms_bwd")(down, dcq, dckv, dkpe, cos, sin, gq3, gkv3)


def _col_blocks(t, n, bn):
    return pl.BlockSpec((t, bn), lambda i: (0, i))


def _row_blocks(n, bm):
    return pl.BlockSpec((bm, n), lambda i: (i, 0))


def _local_step(x, pos2, tgt, small, weights_of, grads_done, start_dep=None, prefetch=None):
    t = x.shape[0]
    bm = min(512, t)
    inv_freq = (ROPE_THETA ** (-jnp.arange(0, ROPE, 2, dtype=F32) / ROPE)).reshape(1, ROPE // 2)
    cos, sin = _rope_tables(pos2, inv_freq)
    lru_p = {k: small[k] for k in ("conv_w", "conv_b", "w_a", "b_a", "w_x", "b_x", "lam")}

    saved = []
    y, yb = x, x
    for l in range(DEPTH):
        j = l // 2
        big = weights_of(l, 0, y)
        sv = dict(xb=yb, big=big)
        if l % 2 == 0:
            proj = _mm(yb, big["win_t"], mode="nt", grid=(EVEN_IN // 512,), a_spec=_full((t, D)),
                       b_spec=_row_blocks(D, 512), out_shape=S((t, EVEN_IN), F32),
                       out_spec=_col_blocks(t, EVEN_IN, 512), name="even_proj", dep=start_dep if l == 0 else None)
            ycat = jnp.concatenate([_pool_fwd(proj, small["pool_w"], small["pool_scale"], j),
                                    _lru_fwd(proj, lru_p, j)], axis=1)
            big.update(weights_of(l, 1, ycat))
            z1, y1, y1b = _proj_resid_ln(y, ycat, big["wout2d"], small["ln_mix_g"], small["ln_mix_b"], l, "even_out")
            sv.update(proj=proj, ycat=ycat)
        else:
            down, cq, ckv, kpe = _down_norm(yb, big["wdown"], small["gq"], small["gkv"], cos, sin, j)
            o, mix = _attn_fwd(cq, ckv, kpe, cos, sin, big["wqb"], big["wkvb"], big["wo"])
            z1, y1, y1b = _resid_ln(y, mix, small["ln_mix_g"], small["ln_mix_b"], l, "resid_ln")
            sv.update(down=down, cq=cq, ckv=ckv, kpe=kpe, o=o)
        fetched = prefetch(l + 1, y1) if prefetch is not None and l + 1 < DEPTH else None
        z2, y, yb, act = _mlp_fwd(y1, y1b, big["w1"], big["w2"], small["ln_ffn_g"], small["ln_ffn_b"], l,
                                  dep=fetched)
        sv.update(z1=z1, y1b=y1b, z2=z2, act=act)
        saved.append(sv)

    dy, loss_tile = _loss_grad(y, tgt)

    g = {k: [None] * n for k, n in (("ln_mix_g", 4), ("ln_mix_b", 4), ("ln_ffn_g", 4), ("ln_ffn_b", 4),
                                    ("pool_w", 2), ("pool_scale", 2), ("conv_w", 2), ("conv_b", 2),
                                    ("w_a", 2), ("b_a", 2), ("w_x", 2), ("b_x", 2), ("lam", 2),
                                    ("gq", 2), ("gkv", 2))}
    dep = None
    for l in reversed(range(DEPTH)):
        j = l // 2
        sv = saved[l]
        big = sv["big"]
        dz2, dz2b, g["ln_ffn_g"][l], g["ln_ffn_b"][l] = _ln_bwd(dy, sv["z2"], small["ln_ffn_g"], l, "ln_bwd", dep=dep)
        act = sv["act"]
        dh, dff = _mlp_bwd_dh(act, dz2b, big["w1"], big["w2"])
        dw1 = _mm(sv["y1b"], dh, mode="tn", grid=(N_DEV,), a_spec=_full((t, D)),
                  b_spec=_col_blocks(t, D_FF, FF_BLK), out_shape=S((N_DEV, D, FF_BLK), BF16),
                  out_spec=pl.BlockSpec((None, D, FF_BLK), lambda i: (i, 0, 0)), name="mlp_dw1")
        dw2 = _mm(act, dz2b, mode="tn", grid=(N_DEV,), a_spec=_col_blocks(t, D_FF, FF_BLK),
                  b_spec=_full((t, D)), out_shape=S((N_DEV, FF_BLK, D), BF16),
                  out_spec=pl.BlockSpec((None, FF_BLK, D), lambda i: (i, 0, 0)), name="mlp_dw2")
        dep = grads_done(l, dict(w1=dw1, w2=dw2))
        dz1, dz1b, g["ln_mix_g"][l], g["ln_mix_b"][l] = _ln_bwd(dff, sv["z1"], small["ln_mix_g"], l, "ln_bwd_res",
                                                                 r=dz2, dep=dep)
        if l % 2 == 0:
            wout = big["wout2d"]
            dycat = _mm(dz1b, wout, mode="nt", grid=(EVEN_MIX // 512,), a_spec=_full((t, D)),
                        b_spec=_row_blocks(D, 512), out_shape=S((t, EVEN_MIX), F32),
                        out_spec=_col_blocks(t, EVEN_MIX, 512), name="even_dycat")
            dwout = _mm(sv["ycat"], dz1b, mode="tn", grid=(EVEN_MIX // 512,), a_spec=_col_blocks(t, EVEN_MIX, 512),
                        b_spec=_full((t, D)), out_shape=S((EVEN_MIX, D), BF16), out_spec=_row_blocks(D, 512),
                        name="even_dwout")
            du_pool, g["pool_w"][j], g["pool_scale"][j] = _pool_bwd(sv["proj"], dycat, small["pool_w"],
                                                                   small["pool_scale"], j)
            (du_lru, du_gate, g["conv_w"][j], g["conv_b"][j], g["w_a"][j], g["b_a"][j], g["w_x"][j], g["b_x"][j],
             g["lam"][j]) = _lru_bwd(sv["proj"], dycat, lru_p, j)
            dproj = jnp.concatenate([du_pool, du_lru, du_gate], axis=1)
            dep = grads_done(l, dict(win=_even_dwin(sv["xb"], dproj), wout=dwout.reshape(N_DEV, EVEN_MIX // N_DEV, D)))
            dy = _mm(dproj, big["win_t"], mode="nn", grid=(t // bm,), a_spec=_row_blocks(EVEN_IN, bm),
                     b_spec=_full((EVEN_IN, D)), out_shape=S((t, D), F32), out_spec=_row_blocks(D, bm),
                     add=dz1, add_spec=_row_blocks(D, bm), add_scale=ALPHA, name="even_dx", dep=dep)
        else:
            dwo, dwqb, dwkvb, dcq, dckv, dkpe = _attn_bwd(
                sv["cq"], sv["ckv"], sv["kpe"], cos, sin, big["wqb"], big["wkvb"], big["wo"], sv["o"], dz1b)
            ddown, g["gq"][j], g["gkv"][j] = _rms_bwd(sv["down"], dcq, dckv, dkpe, cos, sin, small["gq"],
                                                     small["gkv"], j)
            dwdown = _mm(sv["xb"], ddown, mode="tn", grid=(N_DEV,), a_spec=_col_blocks(t, D, D // N_DEV),
                         b_spec=_full((t, ODD_IN)), out_shape=S((N_DEV, D // N_DEV, ODD_IN), BF16),
                         out_spec=pl.BlockSpec((None, D // N_DEV, ODD_IN), lambda i: (i, 0, 0)),
                         name="odd_dwdown")
            dep = grads_done(l, dict(wdown=dwdown, wqb=dwqb, wkvb=dwkvb, wo=dwo))
            dy = _mm(ddown, big["wdown2d"], mode="nt", grid=(t // bm,), a_spec=_row_blocks(ODD_IN, bm),
                     b_spec=_full((D, ODD_IN)), out_shape=S((t, D), F32), out_spec=_row_blocks(D, bm),
                     add=dz1, add_spec=_row_blocks(D, bm), add_scale=ALPHA, name="odd_dx", dep=dep)
    return loss_tile[0, 0], dy, g


def _mesh_place():
    x, y, c = lax.axis_index("x"), lax.axis_index("y"), lax.axis_index("c")
    return x, y, c


def _peer(place, k):
    x, y, c = place
    return (1 - x if k & 4 else x, 1 - y if k & 2 else y, 1 - c if k & 1 else c)


def _index(place):
    x, y, c = place
    return 4 * x + 2 * y + c


ANY = pl.BlockSpec(memory_space=pl.ANY)


def _make_zones(shards, me, name, dtype=BF16):
    n = len(shards)

    def body(me_ref, *refs):
        for src, dst in zip(refs[:n], refs[n:]):
            dst[...] = src[...].astype(dtype)

    grid_spec = pltpu.PrefetchScalarGridSpec(
        num_scalar_prefetch=1, grid=(1,),
        in_specs=[pl.BlockSpec(s.shape, lambda i, me_ref: (0, 0)) for s in shards],
        out_specs=[pl.BlockSpec((None,) + s.shape, lambda i, me_ref: (me_ref[0], 0, 0)) for s in shards])
    return pl.pallas_call(body, grid_spec=grid_spec, out_shape=[S((N_DEV,) + s.shape, dtype) for s in shards],
                          compiler_params=_cp("arbitrary"), name=name)(me, *shards)


def _shard_rows_tile(a):
    return max(d for d in range(16, 257, 16) if a % d == 0)


HBM = pl.BlockSpec(memory_space=pltpu.HBM)
SEM = pl.BlockSpec(memory_space=pltpu.SEMAPHORE)
DATAFLOW = pltpu.SideEffectType.DATAFLOW_SIDE_EFFECTING


def _in_hbm(a):
    return pltpu.with_memory_space_constraint(a, pltpu.HBM)


def _gather_ici_copies(place, src, land, w):
    me = _index(place)
    return [(_peer(place, k), land.at[me], land.at[me]) for k in (1, 2, 4, 6)]


def _gather_d2d_copies(place, src, land, w):
    blocks = [_index(_peer(place, k)) for k in (2, 4, 6)]
    return [(_peer(place, 1), land.at[b], land.at[b]) for b in blocks]


GATHER_ICI = (4, _gather_ici_copies)
GATHER_D2D = (3, _gather_d2d_copies)


def _scatter_plan(layers):
    def copies(place, src, land, w):
        me = _index(place)
        mine = land.at[me] if layers[w] is None else land.at[me, layers[w]]
        return [(_peer(place, k), src.at[_index(_peer(place, k))], mine) for k in range(1, N_DEV)]
    return (N_DEV - 1, copies)


def _gather_all_copies(place, src, land, w):
    me = _index(place)
    return [(_peer(place, k), land.at[me], land.at[me]) for k in range(1, N_DEV)]


GATHER_ALL = (N_DEV - 1, _gather_all_copies)


def _sum_blocks(zone, part, me):
    r = part.shape[1]

    def body(me_ref, z_ref, p_ref, o_ref):
        acc = None
        for s in range(N_DEV):
            term = jnp.where(me_ref[0] == s, p_ref[...], z_ref[s])
            acc = term if acc is None else acc + term
        o_ref[...] = acc

    grid_spec = pltpu.PrefetchScalarGridSpec(
        num_scalar_prefetch=1, grid=(1,),
        in_specs=[pl.BlockSpec((N_DEV, r, 128), lambda i, me_ref: (0, 0, 0)),
                  pl.BlockSpec((None, r, 128), lambda i, me_ref: (me_ref[0], 0, 0))],
        out_specs=pl.BlockSpec((r, 128), lambda i, me_ref: (0, 0)))
    return pl.pallas_call(body, grid_spec=grid_spec, out_shape=S((r, 128), F32),
                          compiler_params=_cp("arbitrary"), name="sum_small")(me, zone, part)


def _exchange_start(srcs, lands, plan, name, after=()):
    ns, n = len(srcs), len(lands)
    n_in = ns + n + len(after)
    per, copies = plan

    def body(*refs):
        ins, land = refs[:ns], refs[ns:ns + n]
        send, recv = refs[n_in], refs[n_in + 1]
        token = refs[-1]
        place = _mesh_place()
        for i in range(per):
            for w in range(n):
                target, src, dst = copies(place, ins[w] if ns else None, land[w], w)[i]
                pltpu.make_async_remote_copy(src_ref=src, dst_ref=dst, send_sem=send.at[w * per + i],
                                             recv_sem=recv.at[w * per + i], device_id=target, device_id_type=MESH).start()
        token[...] = jnp.zeros_like(token)

    sems = pltpu.SemaphoreType.DMA((n * per,))
    thru = [pltpu.HBM(a.shape, a.dtype) for a in list(srcs) + list(lands)]
    out = pl.pallas_call(
        body, name=name, in_specs=[HBM] * (ns + n) + [ANY] * len(after),
        out_shape=(sems, sems, *thru, S((8, 128), F32)),
        out_specs=(SEM, SEM, *([HBM] * (ns + n)), pl.BlockSpec(memory_space=pltpu.VMEM)),
        input_output_aliases={i: 2 + i for i in range(ns + n)},
        compiler_params=pltpu.CompilerParams(has_side_effects=DATAFLOW),
    )(*[_in_hbm(a) for a in list(srcs) + list(lands)], *after)
    return out[0], out[1], list(out[2:2 + ns]), list(out[2 + ns:2 + ns + n]), out[-1]


def _exchange_wait(send, recv, srcs, lands, plan, after, name):
    ns, n = len(srcs), len(lands)
    per, copies = plan
    afters = tuple(after) if isinstance(after, (tuple, list)) else (after,)

    def body(*refs):
        ins, land = refs[:ns], refs[ns:ns + n]
        send_ref, recv_ref = refs[ns + n], refs[ns + n + 1]
        place = _mesh_place()
        for i in range(per):
            for w in range(n):
                target, src, dst = copies(place, ins[w] if ns else None, land[w], w)[i]
                cp = pltpu.make_async_remote_copy(src_ref=src, dst_ref=dst, send_sem=send_ref.at[w * per + i],
                                                  recv_sem=recv_ref.at[w * per + i], device_id=target,
                                                  device_id_type=MESH)
                cp.wait_send()
                cp.wait_recv()

    thru = [pltpu.HBM(a.shape, a.dtype) for a in list(srcs) + list(lands)]
    out = pl.pallas_call(
        body, name=name, in_specs=[HBM] * (ns + n) + [SEM, SEM] + [ANY] * len(afters),
        out_shape=tuple(thru), out_specs=tuple([HBM] * (ns + n)),
        input_output_aliases={i: i for i in range(ns + n)},
        compiler_params=pltpu.CompilerParams(has_side_effects=DATAFLOW),
    )(*srcs, *lands, send, recv, *afters)
    return list(out[:ns]), list(out[ns:])


def _adamw(w, g, m, v):
    m = ADAM_B1 * m + (1.0 - ADAM_B1) * g
    v = ADAM_B2 * v + (1.0 - ADAM_B2) * (g * g)
    m_hat = m / (1.0 - ADAM_B1 ** ADAM_STEP)
    v_hat = v / (1.0 - ADAM_B2 ** ADAM_STEP)
    return -ADAM_LR * (m_hat / (jnp.sqrt(v_hat) + ADAM_EPS) + ADAM_WD * w), m, v


def _adam_big(parts, own, me, w, m, v, name):
    nl, a, b = w.shape
    ta = _shard_rows_tile(a)

    def body(me_ref, p_ref, *refs):
        own_refs, (w_ref, m_ref, v_ref, g_ref, d_ref, mo_ref, vo_ref) = refs[:nl], refs[nl:]
        layer = pl.program_id(0)
        mine = own_refs[0][...]
        for k in range(1, nl):
            mine = jnp.where(layer == k, own_refs[k][...], mine)
        g = None
        for s in range(N_DEV):
            term = jnp.where(me_ref[0] == s, mine, p_ref[s]).astype(F32)
            g = term if g is None else g + term
        g_ref[...] = g
        d_ref[...], mo_ref[...], vo_ref[...] = _adamw(w_ref[...], g, m_ref[...], v_ref[...])

    blk = pl.BlockSpec((None, ta, b), lambda l, i, me_ref: (l, i, 0))

    def own_spec(k):
        return pl.BlockSpec((None, ta, b), lambda l, i, me_ref: (me_ref[0], jnp.where(l == k, i, 0), 0))

    grid_spec = pltpu.PrefetchScalarGridSpec(
        num_scalar_prefetch=1, grid=(nl, a // ta),
        in_specs=[pl.BlockSpec((N_DEV, None, ta, b), lambda l, i, me_ref: (0, l, i, 0))]
        + [own_spec(k) for k in range(nl)] + [blk, blk, blk],
        out_specs=[blk] * 4)
    return pl.pallas_call(body, grid_spec=grid_spec, out_shape=[S(w.shape, F32)] * 4,
                          compiler_params=_cp("arbitrary", "arbitrary"), name=name)(me, parts, *own, w, m, v)


def _adam_small(gs, ws, ms, vs):
    n = len(gs)

    def body(*refs):
        ins, outs = refs[:4 * n], refs[4 * n:]
        for i in range(n):
            g_ref, w_ref, m_ref, v_ref = (ins[k * n + i] for k in range(4))
            outs[i][...], outs[n + i][...], outs[2 * n + i][...] = _adamw(w_ref[...], g_ref[...], m_ref[...], v_ref[...])

    out = pl.pallas_call(body, out_shape=[S(g.shape, F32) for g in gs] * 3, compiler_params=_cp(),
                         name="adam_small")(*gs, *ws, *ms, *vs)
    return out[:n], out[n:2 * n], out[2 * n:]


BIG = ("even_w_in", "even_w_out", "mla_w_down", "mla_w_qb", "mla_w_kvb", "mla_w_o", "mlp_w1", "mlp_w2")
BIG_KEY = dict(even_w_in="win", even_w_out="wout", mla_w_down="wdown", mla_w_qb="wqb", mla_w_kvb="wkvb",
               mla_w_o="wo", mlp_w1="w1", mlp_w2="w2")
SMALL = (("ln_mix_g", "ln_mix_g", None), ("ln_mix_b", "ln_mix_b", None), ("ln_ffn_g", "ln_ffn_g", None),
         ("ln_ffn_b", "ln_ffn_b", None), ("pool_w", "pool_w", None), ("pool_scale", "pool_scale", None),
         ("lru_conv_w", "conv_w", 2), ("lru_conv_b", "conv_b", None), ("lru_w_a", "w_a", None),
         ("lru_b_a", "b_a", None), ("lru_w_x", "w_x", None), ("lru_b_x", "b_x", None), ("lru_lambda", "lam", None),
         ("mla_q_norm_g", "gq", 1), ("mla_kv_norm_g", "gkv", 1))
WEIGHTS = ("ln_mix_g", "ln_mix_b", "ln_ffn_g", "ln_ffn_b", "even_w_in", "pool_w", "pool_scale", "lru_conv_w",
           "lru_conv_b", "lru_w_a", "lru_b_a", "lru_w_x", "lru_b_x", "lru_lambda", "even_w_out", "mla_w_down",
           "mla_q_norm_g", "mla_kv_norm_g", "mla_w_qb", "mla_w_kvb", "mla_w_o", "mlp_w1", "mlp_w2")


def _layer_weights(l):
    j = l // 2
    if l % 2 == 0:
        mixer = [("win", "even_w_in", j), ("wout", "even_w_out", j)]
    else:
        mixer = [("wdown", "mla_w_down", j), ("wqb", "mla_w_qb", j), ("wkvb", "mla_w_kvb", j), ("wo", "mla_w_o", j)]
    return mixer + [("w1", "mlp_w1", l), ("w2", "mlp_w2", l)]


def _pack(arrays, multiple):
    flat = jnp.concatenate([a.reshape(-1) for a in arrays])
    pad = (-flat.shape[0]) % multiple
    return jnp.pad(flat, (0, pad))


def _unpack(flat, shapes):
    out, at = [], 0
    for shp in shapes:
        n = 1
        for s in shp:
            n *= s
        out.append(flat[at:at + n].reshape(shp))
        at += n
    return out


def _global_shape(local_shape, axis):
    if axis is None:
        return tuple(local_shape)
    return tuple(s * N_DEV if i == axis else s for i, s in enumerate(local_shape))


def _step(x, positions, tgt, w, m, v):
    t = x.shape[1]
    me = _index(_mesh_place())

    chunk = N_DEV * 8 * 128
    me_arr = me.astype(jnp.int32).reshape(1)

    lanes = lambda a: jnp.pad(a, ((0, 0), (0, 128 - a.shape[1])))
    mine_packed = jnp.concatenate([w["lru_conv_w"].reshape(8, HEAD), lanes(w["mla_q_norm_g"]),
                                   lanes(w["mla_kv_norm_g"]), jnp.zeros((4, 128), F32)])
    g_send, g_recv, _, g_land, token = _exchange_start([], _make_zones([mine_packed], me_arr, "zones_small", F32),
                                                       GATHER_ALL, "small_params_start")

    def keys_of(l, part):
        keys = [key for key, _, _ in _layer_weights(l)]
        if l == 0:
            return keys[:1] if part == 0 else keys[1:]
        return keys if part == 0 else []

    shard_of = {(l, key): (w[name][i].T if key == "win" else w[name][i])
                for l in range(DEPTH) for key, name, i in _layer_weights(l)}
    flights, after = {}, (token,)
    for l in range(DEPTH):
        for part in (0, 1):
            if keys_of(l, part):
                zones = _make_zones([shard_of[l, key] for key in keys_of(l, part)], me_arr, "zones_%d_%d" % (l, part))
                send, recv, _, lands, token = _exchange_start([], zones, GATHER_ICI, "gather_start_%d_%d" % (l, part),
                                                              after=after)
                flights[l, part] = (send, recv, [], lands)
                after = (token,)

    _, g_land = _exchange_wait(g_send, g_recv, [], g_land, GATHER_ALL, token, "small_params_wait")
    rows_first = g_land[0].transpose(1, 0, 2)
    q_shard, kv_shard = w["mla_q_norm_g"].shape[1], w["mla_kv_norm_g"].shape[1]
    full = dict(lru_conv_w=rows_first[:8].reshape(2, 4, LRU_W),
                mla_q_norm_g=rows_first[8:10, :, :q_shard].reshape(2, Q_RANK),
                mla_kv_norm_g=rows_first[10:12, :, :kv_shard].reshape(2, KV_RANK))

    passing = {}

    def pass_on(l, part, after):
        tag = "%d_%d" % (l, part)
        _, lands = _exchange_wait(*flights[l, part], GATHER_ICI, after, "gather_wait_" + tag)
        send, recv, _, lands, token = _exchange_start([], lands, GATHER_D2D, "gather_pass_" + tag)
        passing[l, part] = (send, recv, [], lands)
        return token

    def early_pass(l, after):
        return pass_on(l, 0, after) if l >= 2 else None

    def weights_of(l, part, after):
        keys = keys_of(l, part)
        if keys:
            if (l, part) not in passing:
                pass_on(l, part, after)
            _, arrays = _exchange_wait(*passing[l, part], GATHER_D2D, after, "gather_pass_wait_%d_%d" % (l, part))
        big = dict(zip(keys, arrays)) if keys else {}
        if "win" in big:
            big["win_t"] = big["win"].reshape(EVEN_IN, D)
        if "wout" in big:
            big["wout2d"] = big["wout"].reshape(EVEN_MIX, D)
        if "wdown" in big:
            big["wdown2d"] = big["wdown"].reshape(D, ODD_IN)
        return big

    zone = {name: lax.empty((N_DEV,) + w[name].shape, BF16) for name in BIG}
    name_of = {key: name for name, key in BIG_KEY.items()}
    sent, last_token = [], [None]

    def grads_done(l, grads):
        keys = list(grads)
        index = {key: i for key, _, i in _layer_weights(l)}
        layers = [index[key] for key in keys]
        send, recv, srcs, lands, tok = _exchange_start([grads[k] for k in keys], [zone[name_of[k]] for k in keys],
                                                       _scatter_plan(layers), "scatter_start_%d_%s" % (l, keys[0]))
        for k, land in zip(keys, lands):
            zone[name_of[k]] = land
        sent.append((send, recv, srcs, keys, layers))
        last_token[0] = tok
        return tok

    row3 = lambda a: a.reshape(a.shape[0], 1, a.shape[1])
    small = dict(ln_mix_g=row3(w["ln_mix_g"]), ln_mix_b=row3(w["ln_mix_b"]), ln_ffn_g=row3(w["ln_ffn_g"]),
                 ln_ffn_b=row3(w["ln_ffn_b"]), pool_w=w["pool_w"], pool_scale=row3(w["pool_scale"]),
                 conv_w=full["lru_conv_w"], conv_b=row3(w["lru_conv_b"]), w_a=w["lru_w_a"], b_a=row3(w["lru_b_a"]),
                 w_x=w["lru_w_x"], b_x=row3(w["lru_b_x"]), lam=row3(w["lru_lambda"]),
                 gq=row3(full["mla_q_norm_g"]), gkv=row3(full["mla_kv_norm_g"]))

    loss_part, grad_x, g = _local_step(x[0], positions.reshape(t, 1), tgt[0], small, weights_of, grads_done,
                                       start_dep=token, prefetch=early_pass)

    own = {name: [None] * w[name].shape[0] for name in BIG}
    me_arr = me.astype(jnp.int32).reshape(1)
    out = {}
    local_g = [jnp.stack(g[key]).reshape(_global_shape(w[name].shape, axis)) for name, key, axis in SMALL]
    local_g.append(loss_part.reshape(1))
    part = _pack(local_g, chunk).reshape(N_DEV, -1, 128)
    small_plan = _scatter_plan([None])
    s_send, s_recv, s_src, s_land, after = _exchange_start([part], [lax.empty(part.shape, F32)], small_plan,
                                                           "small_scatter_start", after=(last_token[0],))
    for n_flight, (send, recv, srcs, keys, layers) in enumerate(sent):
        if n_flight == len(sent) - 1:
            for name in BIG:
                if BIG_KEY[name] not in keys:
                    out[name] = _adam_big(zone[name], own[name], me_arr, w[name], m[name], v[name], "adam_" + name)
            s_src, s_land = _exchange_wait(s_send, s_recv, s_src, s_land, small_plan,
                                           [grad_x] + [o[0] for o in out.values()], "small_scatter_wait")
            chunk_sum = _sum_blocks(s_land[0], s_src[0], me_arr)
            r_zone = lax.dynamic_update_slice_in_dim(lax.empty(part.shape, F32), chunk_sum[None], me, 0)
            r_send, r_recv, _, r_land, after = _exchange_start([], [r_zone], GATHER_ALL, "small_gather_start")
        srcs, lands = _exchange_wait(send, recv, srcs, [zone[name_of[k]] for k in keys], _scatter_plan(layers),
                                     after, "scatter_wait_%d" % n_flight)
        for k, land, src, layer in zip(keys, lands, srcs, layers):
            zone[name_of[k]] = land
            own[name_of[k]][layer] = src
        after = lands[0]
    for name in BIG:
        if name not in out:
            out[name] = _adam_big(zone[name], own[name], me_arr, w[name], m[name], v[name], "adam_" + name)

    _, reduced = _exchange_wait(r_send, r_recv, [], r_land, GATHER_ALL, [out[name][0] for name in BIG],
                                "small_gather_wait")
    reduced = _unpack(reduced[0].reshape(-1), [a.shape for a in local_g])
    loss = reduced[-1][0]
    mine = [a if axis is None else lax.dynamic_slice_in_dim(a, me * w[name].shape[axis], w[name].shape[axis], axis)
            for a, (name, _, axis) in zip(reduced, SMALL)]
    names = [name for name, _, _ in SMALL]
    as_2d = lambda a: a.reshape(-1, a.shape[-1])
    new = _adam_small([as_2d(a) for a in mine], *([as_2d(src[name]) for name in names] for src in (w, m, v)))
    for i, name in enumerate(names):
        out[name] = (mine[i],) + tuple(part[i].reshape(w[name].shape) for part in new)

    return (loss, grad_x[None]) + tuple(out[name][i] for i in range(4) for name in WEIGHTS)


def kernel(x, positions, ln_mix_g, ln_mix_b, ln_ffn_g, ln_ffn_b, even_w_in, pool_w, pool_scale, lru_conv_w, lru_conv_b, lru_w_a, lru_b_a, lru_w_x, lru_b_x, lru_lambda, even_w_out, mla_w_down, mla_q_norm_g, mla_kv_norm_g, mla_w_qb, mla_w_kvb, mla_w_o, mlp_w1, mlp_w2, loss_target, m_ln_mix_g, m_ln_mix_b, m_ln_ffn_g, m_ln_ffn_b, m_even_w_in, m_pool_w, m_pool_scale, m_lru_conv_w, m_lru_conv_b, m_lru_w_a, m_lru_b_a, m_lru_w_x, m_lru_b_x, m_lru_lambda, m_even_w_out, m_mla_w_down, m_mla_q_norm_g, m_mla_kv_norm_g, m_mla_w_qb, m_mla_w_kvb, m_mla_w_o, m_mlp_w1, m_mlp_w2, v_ln_mix_g, v_ln_mix_b, v_ln_ffn_g, v_ln_ffn_b, v_even_w_in, v_pool_w, v_pool_scale, v_lru_conv_w, v_lru_conv_b, v_lru_w_a, v_lru_b_a, v_lru_w_x, v_lru_b_x, v_lru_lambda, v_even_w_out, v_mla_w_down, v_mla_q_norm_g, v_mla_kv_norm_g, v_mla_w_qb, v_mla_w_kvb, v_mla_w_o, v_mlp_w1, v_mlp_w2):
    w = dict(zip(WEIGHTS, (ln_mix_g, ln_mix_b, ln_ffn_g, ln_ffn_b, even_w_in, pool_w, pool_scale, lru_conv_w,
                           lru_conv_b, lru_w_a, lru_b_a, lru_w_x, lru_b_x, lru_lambda, even_w_out, mla_w_down,
                           mla_q_norm_g, mla_kv_norm_g, mla_w_qb, mla_w_kvb, mla_w_o, mlp_w1, mlp_w2)))
    m = dict(zip(WEIGHTS, (m_ln_mix_g, m_ln_mix_b, m_ln_ffn_g, m_ln_ffn_b, m_even_w_in, m_pool_w, m_pool_scale,
                           m_lru_conv_w, m_lru_conv_b, m_lru_w_a, m_lru_b_a, m_lru_w_x, m_lru_b_x, m_lru_lambda,
                           m_even_w_out, m_mla_w_down, m_mla_q_norm_g, m_mla_kv_norm_g, m_mla_w_qb, m_mla_w_kvb,
                           m_mla_w_o, m_mlp_w1, m_mlp_w2)))
    v = dict(zip(WEIGHTS, (v_ln_mix_g, v_ln_mix_b, v_ln_ffn_g, v_ln_ffn_b, v_even_w_in, v_pool_w, v_pool_scale,
                           v_lru_conv_w, v_lru_conv_b, v_lru_w_a, v_lru_b_a, v_lru_w_x, v_lru_b_x, v_lru_lambda,
                           v_even_w_out, v_mla_w_down, v_mla_q_norm_g, v_mla_kv_norm_g, v_mla_w_qb, v_mla_w_kvb,
                           v_mla_w_o, v_mlp_w1, v_mlp_w2)))
    return _step(x, positions, loss_target, w, m, v)
```

```python
import jax
import jax.numpy as jnp
from jax import lax
from jax.experimental import pallas as pl
from jax.experimental.pallas import tpu as pltpu

F32 = jnp.float32
BF16 = jnp.bfloat16
S = jax.ShapeDtypeStruct

D = 1024
DEPTH = 4
N_DEV = 8
CHUNK_SHIFT = 6
POOL_WINDOWS = (2, 4, 8, 16)
POOL_W = 512
LRU_W = 1024
LRU_HEADS = 8
HEAD = 128
LRU_C = 8.0
EVEN_IN = 2560
EVEN_MIX = 1536
MLA_HEADS = 8
NOPE = 128
ROPE = 64
VDIM = 128
Q_RANK = 384
KV_RANK = 256
ODD_IN = 704
D_FF = 4096
FF_BLK = D_FF // N_DEV
ROPE_THETA = 10000.0
ALPHA = (2 * DEPTH) ** 0.25
LN_EPS = 1e-5
RMS_EPS = 1e-6
ATT_SCALE = (NOPE + ROPE) ** -0.5
NEG = float(jnp.finfo(jnp.float32).min)
ADAM_LR = 0.001
ADAM_B1 = 0.9
ADAM_B2 = 0.999
ADAM_EPS = 1e-08
ADAM_WD = 0.01
ADAM_STEP = 10
V7X_VMEM_BYTES = 64 * 1024 * 1024
VMEM_LIMIT = V7X_VMEM_BYTES - 8 * 1024 * 1024
MESH = pl.DeviceIdType.MESH


def _cp(*sem):
    return pltpu.CompilerParams(dimension_semantics=sem if sem else None, vmem_limit_bytes=VMEM_LIMIT)


def _dot(a, b):
    return jnp.dot(a, b, preferred_element_type=F32)


def _dot_nt(a, b):
    return lax.dot_general(a, b, (((1,), (1,)), ((), ())), preferred_element_type=F32)


def _dot_tn(a, b):
    return lax.dot_general(a, b, (((0,), (0,)), ((), ())), preferred_element_type=F32)


def _full(shape):
    return pl.BlockSpec(shape, lambda *_: (0,) * len(shape))


def _mm(a, b, *, mode, grid, a_spec, b_spec, out_shape, out_spec, name, add=None, add_spec=None, add_scale=1.0,
        dep=None):
    dot = {"nn": _dot, "nt": _dot_nt, "tn": _dot_tn}[mode]

    def body(*refs):
        a_ref, b_ref, o_ref = refs[0], refs[1], refs[-1]
        acc = dot(a_ref[...].astype(BF16), b_ref[...].astype(BF16))
        if add is not None:
            acc = acc + add_scale * refs[2][...]
        o_ref[...] = acc.astype(o_ref.dtype)

    ops = [a, b] if add is None else [a, b, add]
    specs = [a_spec, b_spec] if add is None else [a_spec, b_spec, add_spec]
    if dep is not None:
        ops.append(dep)
        specs.append(pl.BlockSpec(memory_space=pl.ANY))
    return pl.pallas_call(body, grid=grid, in_specs=specs, out_specs=out_spec, out_shape=out_shape,
                          compiler_params=_cp(*(("parallel",) * len(grid))), name=name)(*ops)


def _even_dwin(xb, dproj):
    shard = EVEN_IN // N_DEV

    def body(x_ref, dp_ref, o_ref):
        xt = x_ref[...].astype(BF16).T
        for d in range(N_DEV):
            o_ref[d] = _dot(xt, dp_ref[:, d * shard:(d + 1) * shard]).astype(BF16)

    return pl.pallas_call(body, out_shape=S((N_DEV, D, shard), BF16), compiler_params=_cp(), name="even_dwin")(xb, dproj)


def _mlp_dw1(yb, dh):
    t = yb.shape[0]

    def body(y_ref, dh_ref, o_ref, yt_ref):
        @pl.when(pl.program_id(0) == 0)
        def _():
            yt_ref[...] = y_ref[...].T

        o_ref[...] = _dot(yt_ref[...], dh_ref[...]).astype(BF16)

    return pl.pallas_call(
        body, grid=(N_DEV,), in_specs=[_full((t, D)), pl.BlockSpec((t, FF_BLK), lambda i: (0, i))],
        out_specs=pl.BlockSpec((None, D, FF_BLK), lambda i: (i, 0, 0)), out_shape=S((N_DEV, D, FF_BLK), BF16),
        scratch_shapes=[pltpu.VMEM((D, t), BF16)], compiler_params=_cp("arbitrary"), name="mlp_dw1")(yb, dh)


def _ln_stats(z):
    mu = jnp.mean(z, axis=-1, keepdims=True)
    zc = z - mu
    var = jnp.mean(zc * zc, axis=-1, keepdims=True)
    rstd = lax.rsqrt(var + LN_EPS)
    return zc * rstd, rstd


def _row_tile(t):
    return min(1024, t)


def _resid_ln(x, mix, g3, b3, l, name):
    t = x.shape[0]
    bm = _row_tile(t)

    def body(x_ref, m_ref, g_ref, b_ref, z_ref, y_ref, yb_ref):
        z = ALPHA * x_ref[...] + m_ref[...]
        xh, _ = _ln_stats(z)
        y = xh * g_ref[...] + b_ref[...]
        z_ref[...] = z
        y_ref[...] = y
        yb_ref[...] = y.astype(BF16)

    row = pl.BlockSpec((bm, D), lambda i: (i, 0))
    vec = pl.BlockSpec((None, 1, D), lambda i: (l, 0, 0))
    return pl.pallas_call(body, grid=(t // bm,), in_specs=[row, row, vec, vec], out_specs=[row, row, row],
                          out_shape=[S((t, D), F32), S((t, D), F32), S((t, D), BF16)],
                          compiler_params=_cp("parallel"), name=name)(x, mix, g3, b3)


def _proj_resid_ln(x, a, wmat, g3, b3, l, name):
    t, k = a.shape
    bm = _row_tile(t)

    def body(x_ref, a_ref, w_ref, g_ref, b_ref, z_ref, y_ref, yb_ref):
        z = ALPHA * x_ref[...] + _dot(a_ref[...], w_ref[...])
        xh, _ = _ln_stats(z)
        y = xh * g_ref[...] + b_ref[...]
        z_ref[...] = z
        y_ref[...] = y
        yb_ref[...] = y.astype(BF16)

    row = pl.BlockSpec((bm, D), lambda i: (i, 0))
    vec = pl.BlockSpec((None, 1, D), lambda i: (l, 0, 0))
    return pl.pallas_call(body, grid=(t // bm,),
                          in_specs=[row, pl.BlockSpec((bm, k), lambda i: (i, 0)), _full((k, D)), vec, vec],
                          out_specs=[row, row, row], out_shape=[S((t, D), F32), S((t, D), F32), S((t, D), BF16)],
                          compiler_params=_cp("parallel"), name=name)(x, a, wmat, g3, b3)


def _ln_bwd(d, z, g3, l, name, r=None, dep=None):
    t = z.shape[0]
    bm = _row_tile(t)

    def body(*refs):
        refs = list(refs)
        d_ref = refs.pop(0)
        dy = d_ref[...]
        if r is not None:
            dy = dy + ALPHA * refs.pop(0)[...]
        z_ref, g_ref = refs.pop(0), refs.pop(0)
        if dep is not None:
            refs.pop(0)
        dz_ref, dzb_ref, dg_ref, db_ref = refs
        xh, rstd = _ln_stats(z_ref[...])
        dyg = dy * g_ref[...]
        m1 = jnp.mean(dyg, axis=-1, keepdims=True)
        m2 = jnp.mean(dyg * xh, axis=-1, keepdims=True)
        dz = rstd * (dyg - m1 - xh * m2)
        dz_ref[...] = dz
        dzb_ref[...] = dz.astype(BF16)

        @pl.when(pl.program_id(0) == 0)
        def _():
            dg_ref[...] = jnp.zeros_like(dg_ref)
            db_ref[...] = jnp.zeros_like(db_ref)

        dg_ref[...] += jnp.sum(dy * xh, axis=0, keepdims=True)
        db_ref[...] += jnp.sum(dy, axis=0, keepdims=True)

    row = pl.BlockSpec((bm, D), lambda i: (i, 0))
    vec = pl.BlockSpec((None, 1, D), lambda i: (l, 0, 0))
    acc = pl.BlockSpec((1, D), lambda i: (0, 0))
    ops = [d, z, g3] if r is None else [d, r, z, g3]
    specs = [row, row, vec] if r is None else [row, row, row, vec]
    if dep is not None:
        ops.append(dep)
        specs.append(_full(dep.shape))
    return pl.pallas_call(body, grid=(t // bm,), in_specs=specs, out_specs=[row, row, acc, acc],
                          out_shape=[S((t, D), F32), S((t, D), BF16), S((1, D), F32), S((1, D), F32)],
                          compiler_params=_cp("arbitrary"), name=name)(*ops)


def _loss_grad(y, tgt):
    t = y.shape[0]
    bm = _row_tile(t)

    def body(y_ref, t_ref, dy_ref, loss_ref, acc_ref):
        i = pl.program_id(0)
        e = y_ref[...] - t_ref[...]
        dy_ref[...] = e * (1.0 / D)

        @pl.when(i == 0)
        def _():
            acc_ref[...] = jnp.zeros_like(acc_ref)

        acc_ref[...] += jnp.sum(e * e, axis=0, keepdims=True)

        @pl.when(i == pl.num_programs(0) - 1)
        def _():
            loss_ref[...] = jnp.full(loss_ref.shape, (0.5 / D) * jnp.sum(acc_ref[...]), F32)

    row = pl.BlockSpec((bm, D), lambda i: (i, 0))
    return pl.pallas_call(body, grid=(t // bm,), in_specs=[row, row],
                          out_specs=[row, pl.BlockSpec((1, 128), lambda i: (0, 0))],
                          out_shape=[S((t, D), F32), S((1, 128), F32)],
                          scratch_shapes=[pltpu.VMEM((1, D), F32)],
                          compiler_params=_cp("arbitrary"), name="loss_grad")(y, tgt)


def _mlp_row_tile(t):
    return min(1024, t)


MLP_ROW_PARTS = 2


def _row_parts(bm):
    step = bm // MLP_ROW_PARTS
    return [slice(k * step, (k + 1) * step) for k in range(MLP_ROW_PARTS)]


def _mlp_fwd(y, yb, w1g, w2g, g3, b3, l, dep=None):
    t = yb.shape[0]
    bm = _mlp_row_tile(t)

    def body(*refs):
        y_ref, yb_ref, w1_ref, w2_ref, g_ref, b_ref = refs[:6]
        z_ref, o_ref, ob_ref, act_ref, acc_ref = refs[-5:]
        j = pl.program_id(1)

        @pl.when(j == 0)
        def _():
            acc_ref[...] = jnp.zeros_like(acc_ref)

        for rows in _row_parts(bm):
            h = jnp.maximum(_dot(yb_ref[rows, :], w1_ref[...]), 0.0)
            act = (h * h).astype(BF16)
            act_ref[rows, :] = act
            acc_ref[rows, :] += _dot(act, w2_ref[...])

        @pl.when(j == N_DEV - 1)
        def _():
            z = ALPHA * y_ref[...] + acc_ref[...]
            xh, _ = _ln_stats(z)
            out = xh * g_ref[...] + b_ref[...]
            z_ref[...] = z
            o_ref[...] = out
            ob_ref[...] = out.astype(BF16)

    row = pl.BlockSpec((bm, D), lambda i, j: (i, 0))
    vec = pl.BlockSpec((None, 1, D), lambda i, j: (l, 0, 0))
    deps = [] if dep is None else [dep]
    return pl.pallas_call(
        body, grid=(t // bm, N_DEV),
        in_specs=[row, row, pl.BlockSpec((None, D, FF_BLK), lambda i, j: (j, 0, 0)),
                  pl.BlockSpec((None, FF_BLK, D), lambda i, j: (j, 0, 0)), vec, vec] + [ANY] * len(deps),
        out_specs=[row, row, row, pl.BlockSpec((bm, FF_BLK), lambda i, j: (i, j))],
        out_shape=[S((t, D), F32), S((t, D), F32), S((t, D), BF16), S((t, D_FF), BF16)],
        scratch_shapes=[pltpu.VMEM((bm, D), F32)],
        compiler_params=_cp("parallel", "arbitrary"), name="mlp_fwd")(y, yb, w1g, w2g, g3, b3, *deps)


def _mlp_bwd_dh(act, dzb, w1g, w2g):
    t = act.shape[0]
    bm = _mlp_row_tile(t)

    def body(a_ref, dz_ref, w1_ref, w2_ref, dh_ref, acc_ref):
        @pl.when(pl.program_id(1) == 0)
        def _():
            acc_ref[...] = jnp.zeros_like(acc_ref)

        for rows in _row_parts(bm):
            r = jnp.sqrt(a_ref[rows, :].astype(F32))
            dh = (_dot_nt(dz_ref[rows, :], w2_ref[...]) * (2.0 * r)).astype(BF16)
            dh_ref[rows, :] = dh
            acc_ref[rows, :] += _dot_nt(dh, w1_ref[...])

    row = pl.BlockSpec((bm, D), lambda i, j: (i, 0))
    hid = pl.BlockSpec((bm, FF_BLK), lambda i, j: (i, j))
    return pl.pallas_call(
        body, grid=(t // bm, N_DEV),
        in_specs=[hid, row,
                  pl.BlockSpec((None, D, FF_BLK), lambda i, j: (j, 0, 0)),
                  pl.BlockSpec((None, FF_BLK, D), lambda i, j: (j, 0, 0))],
        out_specs=[hid, row],
        out_shape=[S((t, D_FF), BF16), S((t, D), F32)],
        compiler_params=_cp("parallel", "arbitrary"), name="mlp_bwd_dh")(act, dzb, w1g, w2g)


F32_SUBLANES = 8


def _shift_dn(x, k, rows, fill=0.0):
    if k % F32_SUBLANES == 0:
        return jnp.concatenate([jnp.full((k,) + x.shape[1:], fill, x.dtype), x[:x.shape[0] - k]], axis=0)
    return jnp.where(rows >= k, pltpu.roll(x, k, 0), fill)


def _shift_up(x, k, rows, fill=0.0):
    t = x.shape[0]
    if k % F32_SUBLANES == 0:
        return jnp.concatenate([x[k:], jnp.full((k,) + x.shape[1:], fill, x.dtype)], axis=0)
    return jnp.where(rows < t - k, pltpu.roll(x, t - k, 0), fill)


def _scan_rows(a, b, shift):
    rows = lax.broadcasted_iota(jnp.int32, a.shape, 0)
    k = 1
    t = a.shape[0]
    while k < t:
        b = a * shift(b, k, rows) + b
        if 2 * k < t:
            a = a * shift(a, k, rows, 1.0)
        k *= 2
    return b


def _scan_dn(a, b):
    return _scan_rows(a, b, _shift_dn)


def _scan_up(a, b):
    return _scan_rows(a, b, _shift_up)


def _window_sum_dn(x, w, rows):
    k = 1
    while k < w:
        x = x + _shift_dn(x, k, rows)
        k *= 2
    return x


def _window_sum_up(x, w, rows):
    k = 1
    while k < w:
        x = x + _shift_up(x, k, rows)
        k *= 2
    return x


def _pool_diff(u, w, rows):
    inv_count = 1.0 / jnp.minimum(rows + 1, w).astype(F32)
    return _window_sum_dn(u, w, rows) * inv_count - u, inv_count


def _pool_fwd(proj, pool_w, pool_scale3, j):
    t = proj.shape[0]

    def body(u_ref, w_ref, s_ref, y_ref):
        rows = lax.broadcasted_iota(jnp.int32, (t, HEAD), 0)
        for g, w in enumerate(POOL_WINDOWS):
            cols = slice(g * HEAD, (g + 1) * HEAD)
            d, _ = _pool_diff(u_ref[:, cols], w, rows)
            y = _dot(d.astype(BF16), w_ref[g].astype(BF16)) * s_ref[:, cols]
            y_ref[:, cols] = y.astype(BF16)

    return pl.pallas_call(
        body, grid=(1,),
        in_specs=[pl.BlockSpec((t, POOL_W), lambda i: (0, 0)),
                  pl.BlockSpec((None, 4, HEAD, HEAD), lambda i: (j, 0, 0, 0)),
                  pl.BlockSpec((None, 1, POOL_W), lambda i: (j, 0, 0))],
        out_specs=pl.BlockSpec((t, POOL_W), lambda i: (0, 0)),
        out_shape=S((t, POOL_W), BF16), compiler_params=_cp("arbitrary"), name="pool_fwd")(proj, pool_w, pool_scale3)


def _pool_bwd(proj, dycat, pool_w, pool_scale3, j):
    t = proj.shape[0]

    def body(u_ref, dy_ref, w_ref, s_ref, du_ref, dw_ref, ds_ref):
        rows = lax.broadcasted_iota(jnp.int32, (t, HEAD), 0)
        for g, w in enumerate(POOL_WINDOWS):
            cols = slice(g * HEAD, (g + 1) * HEAD)
            d, inv_count = _pool_diff(u_ref[:, cols], w, rows)
            db = d.astype(BF16)
            wg = w_ref[g].astype(BF16)
            dy = dy_ref[:, cols]
            ds_ref[:, cols] = jnp.sum(dy * _dot(db, wg), axis=0, keepdims=True)
            dzz = (dy * s_ref[:, cols]).astype(BF16)
            dw_ref[g] = _dot_tn(db, dzz)
            dd = _dot_nt(dzz, wg)
            du_ref[:, cols] = (_window_sum_up(dd * inv_count, w, rows) - dd).astype(BF16)

    return pl.pallas_call(
        body, grid=(1,),
        in_specs=[pl.BlockSpec((t, POOL_W), lambda i: (0, 0)),
                  pl.BlockSpec((t, POOL_W), lambda i: (0, 0)),
                  pl.BlockSpec((None, 4, HEAD, HEAD), lambda i: (j, 0, 0, 0)),
                  pl.BlockSpec((None, 1, POOL_W), lambda i: (j, 0, 0))],
        out_specs=[pl.BlockSpec((t, POOL_W), lambda i: (0, 0)), _full((4, HEAD, HEAD)), _full((1, POOL_W))],
        out_shape=[S((t, POOL_W), BF16), S((4, HEAD, HEAD), F32), S((1, POOL_W), F32)],
        compiler_params=_cp("arbitrary"), name="pool_bwd")(proj, dycat, pool_w, pool_scale3)


GELU_C = 0.7978845608028654
GELU_K = 0.044715


def _gelu(x):
    th = jnp.tanh(GELU_C * (x + GELU_K * x * x * x))
    return 0.5 * x * (1.0 + th), th


def _lru_forward(u, gate, cw, cb, wa, ba, wx, bx, lam, rows):
    v = cw[3:4] * u + cw[2:3] * _shift_dn(u, 1, rows) + cw[1:2] * _shift_dn(u, 2, rows) \
        + cw[0:1] * _shift_dn(u, 3, rows) + cb
    vb = v.astype(BF16)
    r = jax.nn.sigmoid(_dot(vb, wa) + ba)
    i = jax.nn.sigmoid(_dot(vb, wx) + bx)
    sp = jnp.maximum(-lam, 0.0) + jnp.log1p(jnp.exp(-jnp.abs(lam)))
    log_a = (-LRU_C) * r * sp
    a = jnp.exp(log_a)
    one_m_a2 = -jnp.tanh(log_a) * (a * a + 1.0)
    mult = jnp.sqrt(one_m_a2)
    h = _scan_dn(a, mult * (i * v))
    gl, th = _gelu(gate)
    return dict(v=v, vb=vb, r=r, i=i, sp=sp, a=a, mult=mult, h=h, gl=gl, th=th)


def _lru_specs(t, j, col0_u, col0_g):
    blk = lambda c0: pl.BlockSpec((t, HEAD), lambda h: (0, c0 + h))
    vec = pl.BlockSpec((None, 1, HEAD), lambda h: (j, 0, h))
    return [blk(col0_u), blk(col0_g),
            pl.BlockSpec((None, 4, HEAD), lambda h: (j, 0, h)), vec,
            pl.BlockSpec((None, None, HEAD, HEAD), lambda h: (j, h, 0, 0)), vec,
            pl.BlockSpec((None, None, HEAD, HEAD), lambda h: (j, h, 0, 0)), vec, vec]


def _lru_fwd(proj, p, j):
    t = proj.shape[0]

    def body(u_ref, g_ref, cw_ref, cb_ref, wa_ref, ba_ref, wx_ref, bx_ref, lam_ref, y_ref):
        rows = lax.broadcasted_iota(jnp.int32, (t, HEAD), 0)
        f = _lru_forward(u_ref[...], g_ref[...], cw_ref[...], cb_ref[...], wa_ref[...].astype(BF16), ba_ref[...],
                         wx_ref[...].astype(BF16), bx_ref[...], lam_ref[...], rows)
        y_ref[...] = (f["h"] * f["gl"]).astype(BF16)

    return pl.pallas_call(
        body, grid=(LRU_HEADS,), in_specs=_lru_specs(t, j, POOL_W // HEAD, (POOL_W + LRU_W) // HEAD),
        out_specs=pl.BlockSpec((t, HEAD), lambda h: (0, h)), out_shape=S((t, LRU_W), BF16),
        compiler_params=_cp("parallel"), name="lru_fwd")(
            proj, proj, p["conv_w"], p["conv_b"], p["w_a"], p["b_a"], p["w_x"], p["b_x"], p["lam"])


def _lru_bwd(proj, dycat, p, j):
    t = proj.shape[0]

    def body(u_ref, g_ref, cw_ref, cb_ref, wa_ref, ba_ref, wx_ref, bx_ref, lam_ref, dy_ref,
             du_ref, dgate_ref, dcw_ref, dcb_ref, dwa_ref, dba_ref, dwx_ref, dbx_ref, dlam_ref):
        rows = lax.broadcasted_iota(jnp.int32, (t, HEAD), 0)
        u = u_ref[...]
        gate = g_ref[...]
        cw = cw_ref[...]
        wa = wa_ref[...].astype(BF16)
        wx = wx_ref[...].astype(BF16)
        lam = lam_ref[...]
        f = _lru_forward(u, gate, cw, cb_ref[...], wa, ba_ref[...], wx, bx_ref[...], lam, rows)
        v, r, i, a, mult, h, th = f["v"], f["r"], f["i"], f["a"], f["mult"], f["h"], f["th"]
        dy = dy_ref[...]
        dgl = 0.5 * (1.0 + th) + 0.5 * gate * (1.0 - th * th) * GELU_C * (1.0 + 3.0 * GELU_K * gate * gate)
        dgate_ref[...] = (dy * h * dgl).astype(BF16)
        g = _scan_up(_shift_up(a, 1, rows), dy * f["gl"])
        da = g * _shift_dn(h, 1, rows)
        iv = i * v
        dmult = g * iv
        di = g * mult * v
        dv = g * mult * i
        dlog_a = da * a - dmult * (a * a) / mult
        dr = dlog_a * (-LRU_C) * f["sp"]
        dsp = jnp.sum(dlog_a * (-LRU_C) * r, axis=0, keepdims=True)
        dlam_ref[...] = -dsp * jax.nn.sigmoid(-lam)
        dpa = dr * r * (1.0 - r)
        dpx = di * i * (1.0 - i)
        dpab = dpa.astype(BF16)
        dpxb = dpx.astype(BF16)
        dwa_ref[...] = _dot_tn(f["vb"], dpab)
        dwx_ref[...] = _dot_tn(f["vb"], dpxb)
        dba_ref[...] = jnp.sum(dpa, axis=0, keepdims=True)
        dbx_ref[...] = jnp.sum(dpx, axis=0, keepdims=True)
        dv = dv + _dot_nt(dpab, wa) + _dot_nt(dpxb, wx)
        dcb_ref[...] = jnp.sum(dv, axis=0, keepdims=True)
        du = cw[3:4] * dv
        dcw_ref[3:4, :] = jnp.sum(dv * u, axis=0, keepdims=True)
        for k in (1, 2, 3):
            du = du + cw[3 - k:4 - k] * _shift_up(dv, k, rows)
            dcw_ref[3 - k:4 - k, :] = jnp.sum(dv * _shift_dn(u, k, rows), axis=0, keepdims=True)
        du_ref[...] = du.astype(BF16)

    blk = pl.BlockSpec((t, HEAD), lambda h: (0, h))
    vec = pl.BlockSpec((1, HEAD), lambda h: (0, h))
    mat = pl.BlockSpec((None, HEAD, HEAD), lambda h: (h, 0, 0))
    return pl.pallas_call(
        body, grid=(LRU_HEADS,),
        in_specs=_lru_specs(t, j, POOL_W // HEAD, (POOL_W + LRU_W) // HEAD)
        + [pl.BlockSpec((t, HEAD), lambda h: (0, POOL_W // HEAD + h))],
        out_specs=[blk, blk, pl.BlockSpec((4, HEAD), lambda h: (0, h)), vec, mat, vec, mat, vec, vec],
        out_shape=[S((t, LRU_W), BF16), S((t, LRU_W), BF16), S((4, LRU_W), F32), S((1, LRU_W), F32),
                   S((LRU_HEADS, HEAD, HEAD), F32), S((1, LRU_W), F32),
                   S((LRU_HEADS, HEAD, HEAD), F32), S((1, LRU_W), F32), S((1, LRU_W), F32)],
        compiler_params=_cp("parallel"), name="lru_bwd")(
            proj, proj, p["conv_w"], p["conv_b"], p["w_a"], p["b_a"], p["w_x"], p["b_x"], p["lam"], dycat)


def _rope(x, c, s):
    x1 = x[:, :ROPE // 2]
    x2 = x[:, ROPE // 2:]
    return jnp.concatenate([x1 * c - x2 * s, x1 * s + x2 * c], axis=-1)


def _rope_t(d, c, s):
    d1 = d[:, :ROPE // 2]
    d2 = d[:, ROPE // 2:]
    return jnp.concatenate([d1 * c + d2 * s, d2 * c - d1 * s], axis=-1)


def _rope_tables(pos2, inv_freq):
    t = pos2.shape[0]

    def body(p_ref, f_ref, c_ref, s_ref):
        ang = p_ref[...].astype(F32) * f_ref[...]
        c_ref[...] = jnp.cos(ang)
        s_ref[...] = jnp.sin(ang)

    return pl.pallas_call(body, out_shape=[S((t, ROPE // 2), F32), S((t, ROPE // 2), F32)],
                          name="rope_tables")(pos2, inv_freq)


def _down_norm(xb, wdown_g, gq3, gkv3, cos, sin, j):
    t = xb.shape[0]
    bm = _row_tile(t)

    def body(x_ref, w_ref, gq_ref, gkv_ref, c_ref, s_ref, down_ref, cq_ref, ckv_ref, kpe_ref):
        w = w_ref[...].reshape(D, ODD_IN)
        down = _dot(x_ref[...], w)
        down_ref[...] = down
        q = down[:, :Q_RANK]
        cq_ref[...] = (q * lax.rsqrt(jnp.mean(q * q, axis=-1, keepdims=True) + RMS_EPS) * gq_ref[...]).astype(BF16)
        kv = down[:, Q_RANK:Q_RANK + KV_RANK]
        ckv_ref[...] = (kv * lax.rsqrt(jnp.mean(kv * kv, axis=-1, keepdims=True) + RMS_EPS)
                        * gkv_ref[...]).astype(BF16)
        kpe_ref[...] = _rope(down[:, Q_RANK + KV_RANK:], c_ref[...], s_ref[...])

    row = lambda n: pl.BlockSpec((bm, n), lambda i: (i, 0))
    return pl.pallas_call(
        body, grid=(t // bm,),
        in_specs=[row(D), _full((N_DEV, D // N_DEV, ODD_IN)),
                  pl.BlockSpec((None, 1, Q_RANK), lambda i: (j, 0, 0)),
                  pl.BlockSpec((None, 1, KV_RANK), lambda i: (j, 0, 0)), row(ROPE // 2), row(ROPE // 2)],
        out_specs=[row(ODD_IN), row(Q_RANK), row(KV_RANK), row(ROPE)],
        out_shape=[S((t, ODD_IN), F32), S((t, Q_RANK), BF16), S((t, KV_RANK), BF16), S((t, ROPE), F32)],
        compiler_params=_cp("parallel"), name="down_norm")(xb, wdown_g, gq3, gkv3, cos, sin)


def _q_tile(t, widest):
    return min(widest, t // 2)


def _attn_probs(q, k, qs):
    s = _dot_nt(q, k) * ATT_SCALE
    tq = q.shape[0]
    rows = lax.broadcasted_iota(jnp.int32, (tq, tq), 0)
    cols = lax.broadcasted_iota(jnp.int32, (tq, tq), 1)
    last = jnp.where(jnp.right_shift(cols, CHUNK_SHIFT) <= jnp.right_shift(rows, CHUNK_SHIFT), s[:, qs:], NEG)
    s = last if qs == 0 else jnp.concatenate([s[:, :qs], last], axis=1)
    e = jnp.exp(s - jnp.max(s, axis=-1, keepdims=True))
    return e / jnp.sum(e, axis=-1, keepdims=True)


def _head_qkv(cq, ckv, kpe, c, s, wq_ref, wkv_ref):
    q = jnp.concatenate([_dot(cq, wq_ref[:, :NOPE]), _rope(_dot(cq, wq_ref[:, NOPE:]), c, s)], axis=1).astype(BF16)
    k = jnp.concatenate([_dot(ckv, wkv_ref[:, :NOPE]), kpe], axis=1).astype(BF16)
    vv = _dot(ckv, wkv_ref[:, NOPE:]).astype(BF16)
    return q, k, vv


def _attn_in_specs(t):
    return [_full((t, Q_RANK)), _full((t, KV_RANK)), _full((t, ROPE)), _full((t, ROPE // 2)), _full((t, ROPE // 2)),
            pl.BlockSpec((None, Q_RANK, NOPE + ROPE), lambda h: (h, 0, 0)),
            pl.BlockSpec((None, KV_RANK, NOPE + VDIM), lambda h: (h, 0, 0)),
            pl.BlockSpec((None, VDIM, D), lambda h: (h, 0, 0))]


def _attn_fwd(cq, ckv, kpe, cos, sin, wqb_g, wkvb_g, wo_g):
    t = cq.shape[0]
    tq = _q_tile(t, 256)

    def body(cq_ref, ckv_ref, kpe_ref, c_ref, s_ref, wq_ref, wkv_ref, wo_ref, o_ref, mix_ref):
        q, k, vv = _head_qkv(cq_ref[...], ckv_ref[...], kpe_ref[...], c_ref[...], s_ref[...], wq_ref, wkv_ref)
        for qs in range(0, t, tq):
            ke = qs + tq
            p = _attn_probs(q[qs:ke], k[:ke], qs)
            o_ref[qs:ke, :] = _dot(p.astype(BF16), vv[:ke]).astype(BF16)
        c = _dot(o_ref[...], wo_ref[...])

        @pl.when(pl.program_id(0) == 0)
        def _():
            mix_ref[...] = c

        @pl.when(pl.program_id(0) > 0)
        def _():
            mix_ref[...] += c

    return pl.pallas_call(
        body, grid=(MLA_HEADS,), in_specs=_attn_in_specs(t),
        out_specs=[pl.BlockSpec((None, t, VDIM), lambda h: (h, 0, 0)), _full((t, D))],
        out_shape=[S((MLA_HEADS, t, VDIM), BF16), S((t, D), F32)],
        compiler_params=_cp("arbitrary"), name="attn_fwd")(cq, ckv, kpe, cos, sin, wqb_g, wkvb_g, wo_g)


def _attn_bwd(cq, ckv, kpe, cos, sin, wqb_g, wkvb_g, wo_g, o, dzb):
    t = cq.shape[0]
    tq = _q_tile(t, 512)

    def body(cq_ref, ckv_ref, kpe_ref, c_ref, s_ref, wq_ref, wkv_ref, wo_ref, o_ref, dz_ref,
             dwo_ref, dwq_ref, dwkv_ref, dcq_ref, dckv_ref, dkpe_ref, dkt_s, dvt_s, dq_s):
        cqv = cq_ref[...]
        ckvv = ckv_ref[...]
        c = c_ref[...]
        s = s_ref[...]
        q, k, vv = _head_qkv(cqv, ckvv, kpe_ref[...], c, s, wq_ref, wkv_ref)
        dzv = dz_ref[...]
        dwo_ref[...] = _dot_tn(o_ref[...], dzv).astype(BF16)
        do = _dot_nt(dzv, wo_ref[...]).astype(BF16)
        dkt_s[...] = jnp.zeros_like(dkt_s)
        dvt_s[...] = jnp.zeros_like(dvt_s)
        for qs in range(0, t, tq):
            ke = qs + tq
            p = _attn_probs(q[qs:ke], k[:ke], qs)
            dp = _dot_nt(do[qs:ke], vv[:ke])
            ds = (p * (dp - jnp.sum(p * dp, axis=-1, keepdims=True)) * ATT_SCALE).astype(BF16)
            dq_s[qs:ke, :] = _dot(ds, k[:ke])
            dkt_s[0:NOPE + ROPE, 0:ke] += _dot_tn(q[qs:ke], ds)
            dvt_s[:, 0:ke] += _dot_tn(do[qs:ke], p.astype(BF16))
        dk = dkt_s[...].T
        dqn = dq_s[:, :NOPE].astype(BF16)
        dqp = _rope_t(dq_s[:, NOPE:], c, s).astype(BF16)
        dkn = dk[:, :NOPE].astype(BF16)
        dkp = dk[:, NOPE:NOPE + ROPE]
        dvv = dvt_s[...].T.astype(BF16)
        dwq_ref[:, :NOPE] = _dot_tn(cqv, dqn).astype(BF16)
        dwq_ref[:, NOPE:] = _dot_tn(cqv, dqp).astype(BF16)
        dwkv_ref[:, :NOPE] = _dot_tn(ckvv, dkn).astype(BF16)
        dwkv_ref[:, NOPE:] = _dot_tn(ckvv, dvv).astype(BF16)
        dcq = _dot_nt(dqn, wq_ref[:, :NOPE]) + _dot_nt(dqp, wq_ref[:, NOPE:])
        dckv = _dot_nt(dkn, wkv_ref[:, :NOPE]) + _dot_nt(dvv, wkv_ref[:, NOPE:])

        @pl.when(pl.program_id(0) == 0)
        def _():
            dcq_ref[...] = dcq
            dckv_ref[...] = dckv
            dkpe_ref[...] = dkp

        @pl.when(pl.program_id(0) > 0)
        def _():
            dcq_ref[...] += dcq
            dckv_ref[...] += dckv
            dkpe_ref[...] += dkp

    per_head = lambda a, b: pl.BlockSpec((None, a, b), lambda h: (h, 0, 0))
    return pl.pallas_call(
        body, grid=(MLA_HEADS,),
        in_specs=_attn_in_specs(t) + [per_head(t, VDIM), _full((t, D))],
        out_specs=[per_head(VDIM, D), per_head(Q_RANK, NOPE + ROPE), per_head(KV_RANK, NOPE + VDIM),
                   _full((t, Q_RANK)), _full((t, KV_RANK)), _full((t, ROPE))],
        out_shape=[S((MLA_HEADS, VDIM, D), BF16), S((MLA_HEADS, Q_RANK, NOPE + ROPE), BF16),
                   S((MLA_HEADS, KV_RANK, NOPE + VDIM), BF16),
                   S((t, Q_RANK), F32), S((t, KV_RANK), F32), S((t, ROPE), F32)],
        scratch_shapes=[pltpu.VMEM((2 * NOPE, t), F32), pltpu.VMEM((VDIM, t), F32),
                        pltpu.VMEM((t, NOPE + ROPE), F32)],
        compiler_params=_cp("arbitrary"), name="attn_bwd")(cq, ckv, kpe, cos, sin, wqb_g, wkvb_g, wo_g, o, dzb)


def _rms_bwd(down, dcq, dckv, dkpe, cos, sin, gq3, gkv3, j):
    t = down.shape[0]
    bm = _row_tile(t)

    def body(down_ref, dcq_ref, dckv_ref, dkpe_ref, c_ref, s_ref, gq_ref, gkv_ref, dd_ref, dgq_ref, dgkv_ref):
        @pl.when(pl.program_id(0) == 0)
        def _():
            dgq_ref[...] = jnp.zeros_like(dgq_ref)
            dgkv_ref[...] = jnp.zeros_like(dgkv_ref)

        def rms_b(x, dy, g):
            rstd = lax.rsqrt(jnp.mean(x * x, axis=-1, keepdims=True) + RMS_EPS)
            xh = x * rstd
            dyg = dy * g
            return rstd * (dyg - xh * jnp.mean(dyg * xh, axis=-1, keepdims=True)), jnp.sum(dy * xh, axis=0, keepdims=True)

        dq, dgq = rms_b(down_ref[:, :Q_RANK], dcq_ref[...], gq_ref[...])
        dkv, dgkv = rms_b(down_ref[:, Q_RANK:Q_RANK + KV_RANK], dckv_ref[...], gkv_ref[...])
        dgq_ref[...] += dgq
        dgkv_ref[...] += dgkv
        dd_ref[:, :Q_RANK] = dq.astype(BF16)
        dd_ref[:, Q_RANK:Q_RANK + KV_RANK] = dkv.astype(BF16)
        dd_ref[:, Q_RANK + KV_RANK:] = _rope_t(dkpe_ref[...], c_ref[...], s_ref[...]).astype(BF16)

    row = lambda n: pl.BlockSpec((bm, n), lambda i: (i, 0))
    return pl.pallas_call(
        body, grid=(t // bm,),
        in_specs=[row(ODD_IN), row(Q_RANK), row(KV_RANK), row(ROPE), row(ROPE // 2), row(ROPE // 2),
                  pl.BlockSpec((None, 1, Q_RANK), lambda i: (j, 0, 0)),
                  pl.BlockSpec((None, 1, KV_RANK), lambda i: (j, 0, 0))],
        out_specs=[row(ODD_IN), _full((1, Q_RANK)), _full((1, KV_RANK))],
        out_shape=[S((t, ODD_IN), BF16), S((1, Q_RANK), F32), S((1, KV_RANK), F32)],
        compiler_params=_cp("arbitrary"), name="rms_bwd")(down, dcq, dckv, dkpe, cos, sin, gq3, gkv3)


def _col_blocks(t, n, bn):
    return pl.BlockSpec((t, bn), lambda i: (0, i))


def _row_blocks(n, bm):
    return pl.BlockSpec((bm, n), lambda i: (i, 0))


def _local_step(x, pos2, tgt, small, weights_of, grads_done, start_dep=None, prefetch=None):
    t = x.shape[0]
    bm = min(512, t)
    inv_freq = (ROPE_THETA ** (-jnp.arange(0, ROPE, 2, dtype=F32) / ROPE)).reshape(1, ROPE // 2)
    cos, sin = _rope_tables(pos2, inv_freq)
    lru_p = {k: small[k] for k in ("conv_w", "conv_b", "w_a", "b_a", "w_x", "b_x", "lam")}

    saved = []
    y, yb = x, x
    for l in range(DEPTH):
        j = l // 2
        big = weights_of(l, 0, y)
        sv = dict(xb=yb, big=big)
        if l % 2 == 0:
            proj = _mm(yb, big["win_t"], mode="nt", grid=(EVEN_IN // 512,), a_spec=_full((t, D)),
                       b_spec=_row_blocks(D, 512), out_shape=S((t, EVEN_IN), F32),
                       out_spec=_col_blocks(t, EVEN_IN, 512), name="even_proj", dep=start_dep if l == 0 else None)
            ycat = jnp.concatenate([_pool_fwd(proj, small["pool_w"], small["pool_scale"], j),
                                    _lru_fwd(proj, lru_p, j)], axis=1)
            big.update(weights_of(l, 1, ycat))
            z1, y1, y1b = _proj_resid_ln(y, ycat, big["wout2d"], small["ln_mix_g"], small["ln_mix_b"], l, "even_out")
            sv.update(proj=proj, ycat=ycat)
        else:
            down, cq, ckv, kpe = _down_norm(yb, big["wdown"], small["gq"], small["gkv"], cos, sin, j)
            o, mix = _attn_fwd(cq, ckv, kpe, cos, sin, big["wqb"], big["wkvb"], big["wo"])
            z1, y1, y1b = _resid_ln(y, mix, small["ln_mix_g"], small["ln_mix_b"], l, "resid_ln")
            sv.update(down=down, cq=cq, ckv=ckv, kpe=kpe, o=o)
        fetched = prefetch(l + 1, y1) if prefetch is not None and l + 1 < DEPTH else None
        z2, y, yb, act = _mlp_fwd(y1, y1b, big["w1"], big["w2"], small["ln_ffn_g"], small["ln_ffn_b"], l,
                                  dep=fetched)
        sv.update(z1=z1, y1b=y1b, z2=z2, act=act)
        saved.append(sv)

    dy, loss_tile = _loss_grad(y, tgt)

    g = {k: [None] * n for k, n in (("ln_mix_g", 4), ("ln_mix_b", 4), ("ln_ffn_g", 4), ("ln_ffn_b", 4),
                                    ("pool_w", 2), ("pool_scale", 2), ("conv_w", 2), ("conv_b", 2),
                                    ("w_a", 2), ("b_a", 2), ("w_x", 2), ("b_x", 2), ("lam", 2),
                                    ("gq", 2), ("gkv", 2))}
    dep = None
    for l in reversed(range(DEPTH)):
        j = l // 2
        sv = saved[l]
        big = sv["big"]
        dz2, dz2b, g["ln_ffn_g"][l], g["ln_ffn_b"][l] = _ln_bwd(dy, sv["z2"], small["ln_ffn_g"], l, "ln_bwd", dep=dep)
        act = sv["act"]
        dh, dff = _mlp_bwd_dh(act, dz2b, big["w1"], big["w2"])
        dw1 = _mlp_dw1(sv["y1b"], dh)
        dw2 = _mm(act, dz2b, mode="tn", grid=(N_DEV,), a_spec=_col_blocks(t, D_FF, FF_BLK),
                  b_spec=_full((t, D)), out_shape=S((N_DEV, FF_BLK, D), BF16),
                  out_spec=pl.BlockSpec((None, FF_BLK, D), lambda i: (i, 0, 0)), name="mlp_dw2")
        dep = grads_done(l, dict(w1=dw1, w2=dw2))
        dz1, dz1b, g["ln_mix_g"][l], g["ln_mix_b"][l] = _ln_bwd(dff, sv["z1"], small["ln_mix_g"], l, "ln_bwd_res",
                                                                 r=dz2, dep=dep)
        if l % 2 == 0:
            wout = big["wout2d"]
            dycat = _mm(dz1b, wout, mode="nt", grid=(EVEN_MIX // 512,), a_spec=_full((t, D)),
                        b_spec=_row_blocks(D, 512), out_shape=S((t, EVEN_MIX), F32),
                        out_spec=_col_blocks(t, EVEN_MIX, 512), name="even_dycat")
            dwout = _mm(sv["ycat"], dz1b, mode="tn", grid=(EVEN_MIX // 512,), a_spec=_col_blocks(t, EVEN_MIX, 512),
                        b_spec=_full((t, D)), out_shape=S((EVEN_MIX, D), BF16), out_spec=_row_blocks(D, 512),
                        name="even_dwout")
            du_pool, g["pool_w"][j], g["pool_scale"][j] = _pool_bwd(sv["proj"], dycat, small["pool_w"],
                                                                   small["pool_scale"], j)
            (du_lru, du_gate, g["conv_w"][j], g["conv_b"][j], g["w_a"][j], g["b_a"][j], g["w_x"][j], g["b_x"][j],
             g["lam"][j]) = _lru_bwd(sv["proj"], dycat, lru_p, j)
            dproj = jnp.concatenate([du_pool, du_lru, du_gate], axis=1)
            dep = grads_done(l, dict(win=_even_dwin(sv["xb"], dproj), wout=dwout.reshape(N_DEV, EVEN_MIX // N_DEV, D)))
            dy = _mm(dproj, big["win_t"], mode="nn", grid=(t // bm,), a_spec=_row_blocks(EVEN_IN, bm),
                     b_spec=_full((EVEN_IN, D)), out_shape=S((t, D), F32), out_spec=_row_blocks(D, bm),
                     add=dz1, add_spec=_row_blocks(D, bm), add_scale=ALPHA, name="even_dx", dep=dep)
        else:
            dwo, dwqb, dwkvb, dcq, dckv, dkpe = _attn_bwd(
                sv["cq"], sv["ckv"], sv["kpe"], cos, sin, big["wqb"], big["wkvb"], big["wo"], sv["o"], dz1b)
            ddown, g["gq"][j], g["gkv"][j] = _rms_bwd(sv["down"], dcq, dckv, dkpe, cos, sin, small["gq"],
                                                     small["gkv"], j)
            dwdown = _mm(sv["xb"], ddown, mode="tn", grid=(N_DEV,), a_spec=_col_blocks(t, D, D // N_DEV),
                         b_spec=_full((t, ODD_IN)), out_shape=S((N_DEV, D // N_DEV, ODD_IN), BF16),
                         out_spec=pl.BlockSpec((None, D // N_DEV, ODD_IN), lambda i: (i, 0, 0)),
                         name="odd_dwdown")
            dep = grads_done(l, dict(wdown=dwdown, wqb=dwqb, wkvb=dwkvb, wo=dwo))
            dy = _mm(ddown, big["wdown2d"], mode="nt", grid=(t // bm,), a_spec=_row_blocks(ODD_IN, bm),
                     b_spec=_full((D, ODD_IN)), out_shape=S((t, D), F32), out_spec=_row_blocks(D, bm),
                     add=dz1, add_spec=_row_blocks(D, bm), add_scale=ALPHA, name="odd_dx", dep=dep)
    return loss_tile[0, 0], dy, g


def _mesh_place():
    x, y, c = lax.axis_index("x"), lax.axis_index("y"), lax.axis_index("c")
    return x, y, c


def _peer(place, k):
    x, y, c = place
    return (1 - x if k & 4 else x, 1 - y if k & 2 else y, 1 - c if k & 1 else c)


def _index(place):
    x, y, c = place
    return 4 * x + 2 * y + c


ANY = pl.BlockSpec(memory_space=pl.ANY)


def _make_zones(shards, me, name, dtype=BF16):
    n = len(shards)

    def body(me_ref, *refs):
        for src, dst in zip(refs[:n], refs[n:]):
            dst[...] = src[...].astype(dtype)

    grid_spec = pltpu.PrefetchScalarGridSpec(
        num_scalar_prefetch=1, grid=(1,),
        in_specs=[pl.BlockSpec(s.shape, lambda i, me_ref: (0, 0)) for s in shards],
        out_specs=[pl.BlockSpec((None,) + s.shape, lambda i, me_ref: (me_ref[0], 0, 0)) for s in shards])
    return pl.pallas_call(body, grid_spec=grid_spec, out_shape=[S((N_DEV,) + s.shape, dtype) for s in shards],
                          compiler_params=_cp("arbitrary"), name=name)(me, *shards)


def _shard_rows_tile(a):
    return max(d for d in range(16, 257, 16) if a % d == 0)


HBM = pl.BlockSpec(memory_space=pltpu.HBM)
SEM = pl.BlockSpec(memory_space=pltpu.SEMAPHORE)
DATAFLOW = pltpu.SideEffectType.DATAFLOW_SIDE_EFFECTING


def _in_hbm(a):
    return pltpu.with_memory_space_constraint(a, pltpu.HBM)


def _gather_ici_copies(place, src, land, w):
    me = _index(place)
    return [(_peer(place, k), land.at[me], land.at[me]) for k in (1, 2, 4, 6)]


def _gather_d2d_copies(place, src, land, w):
    blocks = [_index(_peer(place, k)) for k in (2, 4, 6)]
    return [(_peer(place, 1), land.at[b], land.at[b]) for b in blocks]


GATHER_ICI = (4, _gather_ici_copies)
GATHER_D2D = (3, _gather_d2d_copies)


def _scatter_plan(layers):
    def copies(place, src, land, w):
        me = _index(place)
        mine = land.at[me] if layers[w] is None else land.at[me, layers[w]]
        return [(_peer(place, k), src.at[_index(_peer(place, k))], mine) for k in range(1, N_DEV)]
    return (N_DEV - 1, copies)


def _gather_all_copies(place, src, land, w):
    me = _index(place)
    return [(_peer(place, k), land.at[me], land.at[me]) for k in range(1, N_DEV)]


GATHER_ALL = (N_DEV - 1, _gather_all_copies)


def _sum_blocks(zone, part, me):
    r = part.shape[1]

    def body(me_ref, z_ref, p_ref, o_ref):
        acc = None
        for s in range(N_DEV):
            term = jnp.where(me_ref[0] == s, p_ref[...], z_ref[s])
            acc = term if acc is None else acc + term
        o_ref[...] = acc

    grid_spec = pltpu.PrefetchScalarGridSpec(
        num_scalar_prefetch=1, grid=(1,),
        in_specs=[pl.BlockSpec((N_DEV, r, 128), lambda i, me_ref: (0, 0, 0)),
                  pl.BlockSpec((None, r, 128), lambda i, me_ref: (me_ref[0], 0, 0))],
        out_specs=pl.BlockSpec((r, 128), lambda i, me_ref: (0, 0)))
    return pl.pallas_call(body, grid_spec=grid_spec, out_shape=S((r, 128), F32),
                          compiler_params=_cp("arbitrary"), name="sum_small")(me, zone, part)


def _exchange_start(srcs, lands, plan, name, after=()):
    ns, n = len(srcs), len(lands)
    n_in = ns + n + len(after)
    per, copies = plan

    def body(*refs):
        ins, land = refs[:ns], refs[ns:ns + n]
        send, recv = refs[n_in], refs[n_in + 1]
        token = refs[-1]
        place = _mesh_place()
        for i in range(per):
            for w in range(n):
                target, src, dst = copies(place, ins[w] if ns else None, land[w], w)[i]
                pltpu.make_async_remote_copy(src_ref=src, dst_ref=dst, send_sem=send.at[w * per + i],
                                             recv_sem=recv.at[w * per + i], device_id=target, device_id_type=MESH).start()
        token[...] = jnp.zeros_like(token)

    sems = pltpu.SemaphoreType.DMA((n * per,))
    thru = [pltpu.HBM(a.shape, a.dtype) for a in list(srcs) + list(lands)]
    out = pl.pallas_call(
        body, name=name, in_specs=[HBM] * (ns + n) + [ANY] * len(after),
        out_shape=(sems, sems, *thru, S((8, 128), F32)),
        out_specs=(SEM, SEM, *([HBM] * (ns + n)), pl.BlockSpec(memory_space=pltpu.VMEM)),
        input_output_aliases={i: 2 + i for i in range(ns + n)},
        compiler_params=pltpu.CompilerParams(has_side_effects=DATAFLOW),
    )(*[_in_hbm(a) for a in list(srcs) + list(lands)], *after)
    return out[0], out[1], list(out[2:2 + ns]), list(out[2 + ns:2 + ns + n]), out[-1]


def _exchange_wait(send, recv, srcs, lands, plan, after, name):
    ns, n = len(srcs), len(lands)
    per, copies = plan
    afters = tuple(after) if isinstance(after, (tuple, list)) else (after,)

    def body(*refs):
        ins, land = refs[:ns], refs[ns:ns + n]
        send_ref, recv_ref = refs[ns + n], refs[ns + n + 1]
        place = _mesh_place()
        for i in range(per):
            for w in range(n):
                target, src, dst = copies(place, ins[w] if ns else None, land[w], w)[i]
                cp = pltpu.make_async_remote_copy(src_ref=src, dst_ref=dst, send_sem=send_ref.at[w * per + i],
                                                  recv_sem=recv_ref.at[w * per + i], device_id=target,
                                                  device_id_type=MESH)
                cp.wait_send()
                cp.wait_recv()

    thru = [pltpu.HBM(a.shape, a.dtype) for a in list(srcs) + list(lands)]
    out = pl.pallas_call(
        body, name=name, in_specs=[HBM] * (ns + n) + [SEM, SEM] + [ANY] * len(afters),
        out_shape=tuple(thru), out_specs=tuple([HBM] * (ns + n)),
        input_output_aliases={i: i for i in range(ns + n)},
        compiler_params=pltpu.CompilerParams(has_side_effects=DATAFLOW),
    )(*srcs, *lands, send, recv, *afters)
    return list(out[:ns]), list(out[ns:])


def _adamw(w, g, m, v):
    m = ADAM_B1 * m + (1.0 - ADAM_B1) * g
    v = ADAM_B2 * v + (1.0 - ADAM_B2) * (g * g)
    m_hat = m / (1.0 - ADAM_B1 ** ADAM_STEP)
    v_hat = v / (1.0 - ADAM_B2 ** ADAM_STEP)
    return -ADAM_LR * (m_hat / (jnp.sqrt(v_hat) + ADAM_EPS) + ADAM_WD * w), m, v


def _adam_big(parts, own, me, w, m, v, name):
    nl, a, b = w.shape
    ta = _shard_rows_tile(a)

    def body(me_ref, p_ref, *refs):
        own_refs, (w_ref, m_ref, v_ref, g_ref, d_ref, mo_ref, vo_ref) = refs[:nl], refs[nl:]
        layer = pl.program_id(0)
        mine = own_refs[0][...]
        for k in range(1, nl):
            mine = jnp.where(layer == k, own_refs[k][...], mine)
        g = None
        for s in range(N_DEV):
            term = jnp.where(me_ref[0] == s, mine, p_ref[s]).astype(F32)
            g = term if g is None else g + term
        g_ref[...] = g
        d_ref[...], mo_ref[...], vo_ref[...] = _adamw(w_ref[...], g, m_ref[...], v_ref[...])

    blk = pl.BlockSpec((None, ta, b), lambda l, i, me_ref: (l, i, 0))

    def own_spec(k):
        return pl.BlockSpec((None, ta, b), lambda l, i, me_ref: (me_ref[0], jnp.where(l == k, i, 0), 0))

    grid_spec = pltpu.PrefetchScalarGridSpec(
        num_scalar_prefetch=1, grid=(nl, a // ta),
        in_specs=[pl.BlockSpec((N_DEV, None, ta, b), lambda l, i, me_ref: (0, l, i, 0))]
        + [own_spec(k) for k in range(nl)] + [blk, blk, blk],
        out_specs=[blk] * 4)
    return pl.pallas_call(body, grid_spec=grid_spec, out_shape=[S(w.shape, F32)] * 4,
                          compiler_params=_cp("arbitrary", "arbitrary"), name=name)(me, parts, *own, w, m, v)


def _adam_small(gs, ws, ms, vs):
    n = len(gs)

    def body(*refs):
        ins, outs = refs[:4 * n], refs[4 * n:]
        for i in range(n):
            g_ref, w_ref, m_ref, v_ref = (ins[k * n + i] for k in range(4))
            outs[i][...], outs[n + i][...], outs[2 * n + i][...] = _adamw(w_ref[...], g_ref[...], m_ref[...], v_ref[...])

    out = pl.pallas_call(body, out_shape=[S(g.shape, F32) for g in gs] * 3, compiler_params=_cp(),
                         name="adam_small")(*gs, *ws, *ms, *vs)
    return out[:n], out[n:2 * n], out[2 * n:]


BIG = ("even_w_in", "even_w_out", "mla_w_down", "mla_w_qb", "mla_w_kvb", "mla_w_o", "mlp_w1", "mlp_w2")
BIG_KEY = dict(even_w_in="win", even_w_out="wout", mla_w_down="wdown", mla_w_qb="wqb", mla_w_kvb="wkvb",
               mla_w_o="wo", mlp_w1="w1", mlp_w2="w2")
SMALL = (("ln_mix_g", "ln_mix_g", None), ("ln_mix_b", "ln_mix_b", None), ("ln_ffn_g", "ln_ffn_g", None),
         ("ln_ffn_b", "ln_ffn_b", None), ("pool_w", "pool_w", None), ("pool_scale", "pool_scale", None),
         ("lru_conv_w", "conv_w", 2), ("lru_conv_b", "conv_b", None), ("lru_w_a", "w_a", None),
         ("lru_b_a", "b_a", None), ("lru_w_x", "w_x", None), ("lru_b_x", "b_x", None), ("lru_lambda", "lam", None),
         ("mla_q_norm_g", "gq", 1), ("mla_kv_norm_g", "gkv", 1))
WEIGHTS = ("ln_mix_g", "ln_mix_b", "ln_ffn_g", "ln_ffn_b", "even_w_in", "pool_w", "pool_scale", "lru_conv_w",
           "lru_conv_b", "lru_w_a", "lru_b_a", "lru_w_x", "lru_b_x", "lru_lambda", "even_w_out", "mla_w_down",
           "mla_q_norm_g", "mla_kv_norm_g", "mla_w_qb", "mla_w_kvb", "mla_w_o", "mlp_w1", "mlp_w2")


def _layer_weights(l):
    j = l // 2
    if l % 2 == 0:
        mixer = [("win", "even_w_in", j), ("wout", "even_w_out", j)]
    else:
        mixer = [("wdown", "mla_w_down", j), ("wqb", "mla_w_qb", j), ("wkvb", "mla_w_kvb", j), ("wo", "mla_w_o", j)]
    return mixer + [("w1", "mlp_w1", l), ("w2", "mlp_w2", l)]


def _pack(arrays, multiple):
    flat = jnp.concatenate([a.reshape(-1) for a in arrays])
    pad = (-flat.shape[0]) % multiple
    return jnp.pad(flat, (0, pad))


def _unpack(flat, shapes):
    out, at = [], 0
    for shp in shapes:
        n = 1
        for s in shp:
            n *= s
        out.append(flat[at:at + n].reshape(shp))
        at += n
    return out


def _global_shape(local_shape, axis):
    if axis is None:
        return tuple(local_shape)
    return tuple(s * N_DEV if i == axis else s for i, s in enumerate(local_shape))


def _step(x, positions, tgt, w, m, v):
    t = x.shape[1]
    me = _index(_mesh_place())

    chunk = N_DEV * 8 * 128
    me_arr = me.astype(jnp.int32).reshape(1)

    lanes = lambda a: jnp.pad(a, ((0, 0), (0, 128 - a.shape[1])))
    mine_packed = jnp.concatenate([w["lru_conv_w"].reshape(8, HEAD), lanes(w["mla_q_norm_g"]),
                                   lanes(w["mla_kv_norm_g"]), jnp.zeros((4, 128), F32)])
    g_send, g_recv, _, g_land, token = _exchange_start([], _make_zones([mine_packed], me_arr, "zones_small", F32),
                                                       GATHER_ALL, "small_params_start")

    def keys_of(l, part):
        keys = [key for key, _, _ in _layer_weights(l)]
        if l == 0:
            return keys[:1] if part == 0 else keys[1:]
        return keys if part == 0 else []

    shard_of = {(l, key): (w[name][i].T if key == "win" else w[name][i])
                for l in range(DEPTH) for key, name, i in _layer_weights(l)}
    flights, after = {}, (token,)
    for l in range(DEPTH):
        for part in (0, 1):
            if keys_of(l, part):
                zones = _make_zones([shard_of[l, key] for key in keys_of(l, part)], me_arr, "zones_%d_%d" % (l, part))
                send, recv, _, lands, token = _exchange_start([], zones, GATHER_ICI, "gather_start_%d_%d" % (l, part),
                                                              after=after)
                flights[l, part] = (send, recv, [], lands)
                after = (token,)

    _, g_land = _exchange_wait(g_send, g_recv, [], g_land, GATHER_ALL, token, "small_params_wait")
    rows_first = g_land[0].transpose(1, 0, 2)
    q_shard, kv_shard = w["mla_q_norm_g"].shape[1], w["mla_kv_norm_g"].shape[1]
    full = dict(lru_conv_w=rows_first[:8].reshape(2, 4, LRU_W),
                mla_q_norm_g=rows_first[8:10, :, :q_shard].reshape(2, Q_RANK),
                mla_kv_norm_g=rows_first[10:12, :, :kv_shard].reshape(2, KV_RANK))

    passing = {}

    def pass_on(l, part, after):
        tag = "%d_%d" % (l, part)
        _, lands = _exchange_wait(*flights[l, part], GATHER_ICI, after, "gather_wait_" + tag)
        send, recv, _, lands, token = _exchange_start([], lands, GATHER_D2D, "gather_pass_" + tag)
        passing[l, part] = (send, recv, [], lands)
        return token

    def early_pass(l, after):
        return pass_on(l, 0, after) if l >= 2 else None

    def weights_of(l, part, after):
        keys = keys_of(l, part)
        if keys:
            if (l, part) not in passing:
                pass_on(l, part, after)
            _, arrays = _exchange_wait(*passing[l, part], GATHER_D2D, after, "gather_pass_wait_%d_%d" % (l, part))
        big = dict(zip(keys, arrays)) if keys else {}
        if "win" in big:
            big["win_t"] = big["win"].reshape(EVEN_IN, D)
        if "wout" in big:
            big["wout2d"] = big["wout"].reshape(EVEN_MIX, D)
        if "wdown" in big:
            big["wdown2d"] = big["wdown"].reshape(D, ODD_IN)
        return big

    zone = {name: lax.empty((N_DEV,) + w[name].shape, BF16) for name in BIG}
    name_of = {key: name for name, key in BIG_KEY.items()}
    sent, last_token = [], [None]

    def grads_done(l, grads):
        keys = list(grads)
        index = {key: i for key, _, i in _layer_weights(l)}
        layers = [index[key] for key in keys]
        send, recv, srcs, lands, tok = _exchange_start([grads[k] for k in keys], [zone[name_of[k]] for k in keys],
                                                       _scatter_plan(layers), "scatter_start_%d_%s" % (l, keys[0]))
        for k, land in zip(keys, lands):
            zone[name_of[k]] = land
        sent.append((send, recv, srcs, keys, layers))
        last_token[0] = tok
        return tok

    row3 = lambda a: a.reshape(a.shape[0], 1, a.shape[1])
    small = dict(ln_mix_g=row3(w["ln_mix_g"]), ln_mix_b=row3(w["ln_mix_b"]), ln_ffn_g=row3(w["ln_ffn_g"]),
                 ln_ffn_b=row3(w["ln_ffn_b"]), pool_w=w["pool_w"], pool_scale=row3(w["pool_scale"]),
                 conv_w=full["lru_conv_w"], conv_b=row3(w["lru_conv_b"]), w_a=w["lru_w_a"], b_a=row3(w["lru_b_a"]),
                 w_x=w["lru_w_x"], b_x=row3(w["lru_b_x"]), lam=row3(w["lru_lambda"]),
                 gq=row3(full["mla_q_norm_g"]), gkv=row3(full["mla_kv_norm_g"]))

    loss_part, grad_x, g = _local_step(x[0], positions.reshape(t, 1), tgt[0], small, weights_of, grads_done,
                                       start_dep=token, prefetch=early_pass)

    own = {name: [None] * w[name].shape[0] for name in BIG}
    me_arr = me.astype(jnp.int32).reshape(1)
    out = {}
    local_g = [jnp.stack(g[key]).reshape(_global_shape(w[name].shape, axis)) for name, key, axis in SMALL]
    local_g.append(loss_part.reshape(1))
    part = _pack(local_g, chunk).reshape(N_DEV, -1, 128)
    small_plan = _scatter_plan([None])
    s_send, s_recv, s_src, s_land, after = _exchange_start([part], [lax.empty(part.shape, F32)], small_plan,
                                                           "small_scatter_start", after=(last_token[0],))
    for n_flight, (send, recv, srcs, keys, layers) in enumerate(sent):
        if n_flight == len(sent) - 1:
            for name in BIG:
                if BIG_KEY[name] not in keys:
                    out[name] = _adam_big(zone[name], own[name], me_arr, w[name], m[name], v[name], "adam_" + name)
            s_src, s_land = _exchange_wait(s_send, s_recv, s_src, s_land, small_plan,
                                           [grad_x] + [o[0] for o in out.values()], "small_scatter_wait")
            chunk_sum = _sum_blocks(s_land[0], s_src[0], me_arr)
            r_zone = lax.dynamic_update_slice_in_dim(lax.empty(part.shape, F32), chunk_sum[None], me, 0)
            r_send, r_recv, _, r_land, after = _exchange_start([], [r_zone], GATHER_ALL, "small_gather_start")
        srcs, lands = _exchange_wait(send, recv, srcs, [zone[name_of[k]] for k in keys], _scatter_plan(layers),
                                     after, "scatter_wait_%d" % n_flight)
        for k, land, src, layer in zip(keys, lands, srcs, layers):
            zone[name_of[k]] = land
            own[name_of[k]][layer] = src
        after = lands[0]
    for name in BIG:
        if name not in out:
            out[name] = _adam_big(zone[name], own[name], me_arr, w[name], m[name], v[name], "adam_" + name)

    _, reduced = _exchange_wait(r_send, r_recv, [], r_land, GATHER_ALL, [out[name][0] for name in BIG],
                                "small_gather_wait")
    reduced = _unpack(reduced[0].reshape(-1), [a.shape for a in local_g])
    loss = reduced[-1][0]
    mine = [a if axis is None else lax.dynamic_slice_in_dim(a, me * w[name].shape[axis], w[name].shape[axis], axis)
            for a, (name, _, axis) in zip(reduced, SMALL)]
    names = [name for name, _, _ in SMALL]
    as_2d = lambda a: a.reshape(-1, a.shape[-1])
    new = _adam_small([as_2d(a) for a in mine], *([as_2d(src[name]) for name in names] for src in (w, m, v)))
    for i, name in enumerate(names):
        out[name] = (mine[i],) + tuple(part[i].reshape(w[name].shape) for part in new)

    return (loss, grad_x[None]) + tuple(out[name][i] for i in range(4) for name in WEIGHTS)


def kernel(x, positions, ln_mix_g, ln_mix_b, ln_ffn_g, ln_ffn_b, even_w_in, pool_w, pool_scale, lru_conv_w, lru_conv_b, lru_w_a, lru_b_a, lru_w_x, lru_b_x, lru_lambda, even_w_out, mla_w_down, mla_q_norm_g, mla_kv_norm_g, mla_w_qb, mla_w_kvb, mla_w_o, mlp_w1, mlp_w2, loss_target, m_ln_mix_g, m_ln_mix_b, m_ln_ffn_g, m_ln_ffn_b, m_even_w_in, m_pool_w, m_pool_scale, m_lru_conv_w, m_lru_conv_b, m_lru_w_a, m_lru_b_a, m_lru_w_x, m_lru_b_x, m_lru_lambda, m_even_w_out, m_mla_w_down, m_mla_q_norm_g, m_mla_kv_norm_g, m_mla_w_qb, m_mla_w_kvb, m_mla_w_o, m_mlp_w1, m_mlp_w2, v_ln_mix_g, v_ln_mix_b, v_ln_ffn_g, v_ln_ffn_b, v_even_w_in, v_pool_w, v_pool_scale, v_lru_conv_w, v_lru_conv_b, v_lru_w_a, v_lru_b_a, v_lru_w_x, v_lru_b_x, v_lru_lambda, v_even_w_out, v_mla_w_down, v_mla_q_norm_g, v_mla_kv_norm_g, v_mla_w_qb, v_mla_w_kvb, v_mla_w_o, v_mlp_w1, v_mlp_w2):
    w = dict(zip(WEIGHTS, (ln_mix_g, ln_mix_b, ln_ffn_g, ln_ffn_b, even_w_in, pool_w, pool_scale, lru_conv_w,
                           lru_conv_b, lru_w_a, lru_b_a, lru_w_x, lru_b_x, lru_lambda, even_w_out, mla_w_down,
                           mla_q_norm_g, mla_kv_norm_g, mla_w_qb, mla_w_kvb, mla_w_o, mlp_w1, mlp_w2)))
    m = dict(zip(WEIGHTS, (m_ln_mix_g, m_ln_mix_b, m_ln_ffn_g, m_ln_ffn_b, m_even_w_in, m_pool_w, m_pool_scale,
                           m_lru_conv_w, m_lru_conv_b, m_lru_w_a, m_lru_b_a, m_lru_w_x, m_lru_b_x, m_lru_lambda,
                           m_even_w_out, m_mla_w_down, m_mla_q_norm_g, m_mla_kv_norm_g, m_mla_w_qb, m_mla_w_kvb,
                           m_mla_w_o, m_mlp_w1, m_mlp_w2)))
    v = dict(zip(WEIGHTS, (v_ln_mix_g, v_ln_mix_b, v_ln_ffn_g, v_ln_ffn_b, v_even_w_in, v_pool_w, v_pool_scale,
                           v_lru_conv_w, v_lru_conv_b, v_lru_w_a, v_lru_b_a, v_lru_w_x, v_lru_b_x, v_lru_lambda,
                           v_even_w_out, v_mla_w_down, v_mla_q_norm_g, v_mla_kv_norm_g, v_mla_w_qb, v_mla_w_kvb,
                           v_mla_w_o, v_mlp_w1, v_mlp_w2)))
    return _step(x, positions, loss_target, w, m, v)
```

```python
import jax
import jax.numpy as jnp
from jax import lax
from jax.experimental import pallas as pl
from jax.experimental.pallas import tpu as pltpu

F32 = jnp.float32
BF16 = jnp.bfloat16
S = jax.ShapeDtypeStruct

D = 1024
DEPTH = 4
N_DEV = 8
CHUNK_SHIFT = 6
POOL_WINDOWS = (2, 4, 8, 16)
POOL_W = 512
LRU_W = 1024
LRU_HEADS = 8
HEAD = 128
LRU_C = 8.0
EVEN_IN = 2560
EVEN_MIX = 1536
MLA_HEADS = 8
NOPE = 128
ROPE = 64
VDIM = 128
Q_RANK = 384
KV_RANK = 256
ODD_IN = 704
D_FF = 4096
FF_BLK = D_FF // N_DEV
ROPE_THETA = 10000.0
ALPHA = (2 * DEPTH) ** 0.25
LN_EPS = 1e-5
RMS_EPS = 1e-6
ATT_SCALE = (NOPE + ROPE) ** -0.5
NEG = float(jnp.finfo(jnp.float32).min)
ADAM_LR = 0.001
ADAM_B1 = 0.9
ADAM_B2 = 0.999
ADAM_EPS = 1e-08
ADAM_WD = 0.01
ADAM_STEP = 10
V7X_VMEM_BYTES = 64 * 1024 * 1024
VMEM_LIMIT = V7X_VMEM_BYTES - 8 * 1024 * 1024
MESH = pl.DeviceIdType.MESH


def _cp(*sem):
    return pltpu.CompilerParams(dimension_semantics=sem if sem else None, vmem_limit_bytes=VMEM_LIMIT)


def _dot(a, b):
    return jnp.dot(a, b, preferred_element_type=F32)


def _dot_nt(a, b):
    return lax.dot_general(a, b, (((1,), (1,)), ((), ())), preferred_element_type=F32)


def _dot_tn(a, b):
    return lax.dot_general(a, b, (((0,), (0,)), ((), ())), preferred_element_type=F32)


def _full(shape):
    return pl.BlockSpec(shape, lambda *_: (0,) * len(shape))


def _mm(a, b, *, mode, grid, a_spec, b_spec, out_shape, out_spec, name, add=None, add_spec=None, add_scale=1.0,
        dep=None):
    dot = {"nn": _dot, "nt": _dot_nt, "tn": _dot_tn}[mode]

    def body(*refs):
        a_ref, b_ref, o_ref = refs[0], refs[1], refs[-1]
        acc = dot(a_ref[...].astype(BF16), b_ref[...].astype(BF16))
        if add is not None:
            acc = acc + add_scale * refs[2][...]
        o_ref[...] = acc.astype(o_ref.dtype)

    ops = [a, b] if add is None else [a, b, add]
    specs = [a_spec, b_spec] if add is None else [a_spec, b_spec, add_spec]
    if dep is not None:
        ops.append(dep)
        specs.append(pl.BlockSpec(memory_space=pl.ANY))
    return pl.pallas_call(body, grid=grid, in_specs=specs, out_specs=out_spec, out_shape=out_shape,
                          compiler_params=_cp(*(("parallel",) * len(grid))), name=name)(*ops)


def _even_dwin(xb, dproj):
    shard = EVEN_IN // N_DEV

    def body(x_ref, dp_ref, o_ref):
        xv = x_ref[...].astype(BF16)
        for d in range(N_DEV):
            o_ref[d] = _dot_tn(xv, dp_ref[:, d * shard:(d + 1) * shard]).astype(BF16)

    return pl.pallas_call(body, out_shape=S((N_DEV, D, shard), BF16), compiler_params=_cp(), name="even_dwin")(xb, dproj)


def _ln_stats(z):
    mu = jnp.mean(z, axis=-1, keepdims=True)
    zc = z - mu
    var = jnp.mean(zc * zc, axis=-1, keepdims=True)
    rstd = lax.rsqrt(var + LN_EPS)
    return zc * rstd, rstd


def _row_tile(t):
    return min(1024, t)


def _resid_ln(x, mix, g3, b3, l, name):
    t = x.shape[0]
    bm = _row_tile(t)

    def body(x_ref, m_ref, g_ref, b_ref, z_ref, y_ref, yb_ref):
        z = ALPHA * x_ref[...] + m_ref[...]
        xh, _ = _ln_stats(z)
        y = xh * g_ref[...] + b_ref[...]
        z_ref[...] = z
        y_ref[...] = y
        yb_ref[...] = y.astype(BF16)

    row = pl.BlockSpec((bm, D), lambda i: (i, 0))
    vec = pl.BlockSpec((None, 1, D), lambda i: (l, 0, 0))
    return pl.pallas_call(body, grid=(t // bm,), in_specs=[row, row, vec, vec], out_specs=[row, row, row],
                          out_shape=[S((t, D), F32), S((t, D), F32), S((t, D), BF16)],
                          compiler_params=_cp("parallel"), name=name)(x, mix, g3, b3)


def _proj_resid_ln(x, a, wmat, g3, b3, l, name):
    t, k = a.shape
    bm = _row_tile(t)

    def body(x_ref, a_ref, w_ref, g_ref, b_ref, z_ref, y_ref, yb_ref):
        z = ALPHA * x_ref[...] + _dot(a_ref[...], w_ref[...])
        xh, _ = _ln_stats(z)
        y = xh * g_ref[...] + b_ref[...]
        z_ref[...] = z
        y_ref[...] = y
        yb_ref[...] = y.astype(BF16)

    row = pl.BlockSpec((bm, D), lambda i: (i, 0))
    vec = pl.BlockSpec((None, 1, D), lambda i: (l, 0, 0))
    return pl.pallas_call(body, grid=(t // bm,),
                          in_specs=[row, pl.BlockSpec((bm, k), lambda i: (i, 0)), _full((k, D)), vec, vec],
                          out_specs=[row, row, row], out_shape=[S((t, D), F32), S((t, D), F32), S((t, D), BF16)],
                          compiler_params=_cp("parallel"), name=name)(x, a, wmat, g3, b3)


def _ln_bwd(d, z, g3, l, name, r=None, dep=None):
    t = z.shape[0]
    bm = _row_tile(t)

    def body(*refs):
        refs = list(refs)
        d_ref = refs.pop(0)
        dy = d_ref[...]
        if r is not None:
            dy = dy + ALPHA * refs.pop(0)[...]
        z_ref, g_ref = refs.pop(0), refs.pop(0)
        if dep is not None:
            refs.pop(0)
        dz_ref, dzb_ref, dg_ref, db_ref = refs
        xh, rstd = _ln_stats(z_ref[...])
        dyg = dy * g_ref[...]
        m1 = jnp.mean(dyg, axis=-1, keepdims=True)
        m2 = jnp.mean(dyg * xh, axis=-1, keepdims=True)
        dz = rstd * (dyg - m1 - xh * m2)
        dz_ref[...] = dz
        dzb_ref[...] = dz.astype(BF16)

        @pl.when(pl.program_id(0) == 0)
        def _():
            dg_ref[...] = jnp.zeros_like(dg_ref)
            db_ref[...] = jnp.zeros_like(db_ref)

        dg_ref[...] += jnp.sum(dy * xh, axis=0, keepdims=True)
        db_ref[...] += jnp.sum(dy, axis=0, keepdims=True)

    row = pl.BlockSpec((bm, D), lambda i: (i, 0))
    vec = pl.BlockSpec((None, 1, D), lambda i: (l, 0, 0))
    acc = pl.BlockSpec((1, D), lambda i: (0, 0))
    ops = [d, z, g3] if r is None else [d, r, z, g3]
    specs = [row, row, vec] if r is None else [row, row, row, vec]
    if dep is not None:
        ops.append(dep)
        specs.append(_full(dep.shape))
    return pl.pallas_call(body, grid=(t // bm,), in_specs=specs, out_specs=[row, row, acc, acc],
                          out_shape=[S((t, D), F32), S((t, D), BF16), S((1, D), F32), S((1, D), F32)],
                          compiler_params=_cp("arbitrary"), name=name)(*ops)


def _loss_grad(y, tgt):
    t = y.shape[0]
    bm = _row_tile(t)

    def body(y_ref, t_ref, dy_ref, loss_ref, acc_ref):
        i = pl.program_id(0)
        e = y_ref[...] - t_ref[...]
        dy_ref[...] = e * (1.0 / D)

        @pl.when(i == 0)
        def _():
            acc_ref[...] = jnp.zeros_like(acc_ref)

        acc_ref[...] += jnp.sum(e * e, axis=0, keepdims=True)

        @pl.when(i == pl.num_programs(0) - 1)
        def _():
            loss_ref[...] = jnp.full(loss_ref.shape, (0.5 / D) * jnp.sum(acc_ref[...]), F32)

    row = pl.BlockSpec((bm, D), lambda i: (i, 0))
    return pl.pallas_call(body, grid=(t // bm,), in_specs=[row, row],
                          out_specs=[row, pl.BlockSpec((1, 128), lambda i: (0, 0))],
                          out_shape=[S((t, D), F32), S((1, 128), F32)],
                          scratch_shapes=[pltpu.VMEM((1, D), F32)],
                          compiler_params=_cp("arbitrary"), name="loss_grad")(y, tgt)


def _mlp_row_tile(t):
    return min(1024, t)


MLP_ROW_PARTS = 2


def _row_parts(bm):
    step = bm // MLP_ROW_PARTS
    return [slice(k * step, (k + 1) * step) for k in range(MLP_ROW_PARTS)]


def _mlp_fwd(y, yb, w1g, w2g, g3, b3, l, dep=None):
    t = yb.shape[0]
    bm = _mlp_row_tile(t)

    def body(*refs):
        y_ref, yb_ref, w1_ref, w2_ref, g_ref, b_ref = refs[:6]
        z_ref, o_ref, ob_ref, act_ref, acc_ref = refs[-5:]
        j = pl.program_id(1)

        @pl.when(j == 0)
        def _():
            acc_ref[...] = jnp.zeros_like(acc_ref)

        for rows in _row_parts(bm):
            h = jnp.maximum(_dot(yb_ref[rows, :], w1_ref[...]), 0.0)
            act = (h * h).astype(BF16)
            act_ref[rows, :] = act
            acc_ref[rows, :] += _dot(act, w2_ref[...])

        @pl.when(j == N_DEV - 1)
        def _():
            z = ALPHA * y_ref[...] + acc_ref[...]
            xh, _ = _ln_stats(z)
            out = xh * g_ref[...] + b_ref[...]
            z_ref[...] = z
            o_ref[...] = out
            ob_ref[...] = out.astype(BF16)

    row = pl.BlockSpec((bm, D), lambda i, j: (i, 0))
    vec = pl.BlockSpec((None, 1, D), lambda i, j: (l, 0, 0))
    deps = [] if dep is None else [dep]
    return pl.pallas_call(
        body, grid=(t // bm, N_DEV),
        in_specs=[row, row, pl.BlockSpec((None, D, FF_BLK), lambda i, j: (j, 0, 0)),
                  pl.BlockSpec((None, FF_BLK, D), lambda i, j: (j, 0, 0)), vec, vec] + [ANY] * len(deps),
        out_specs=[row, row, row, pl.BlockSpec((bm, FF_BLK), lambda i, j: (i, j))],
        out_shape=[S((t, D), F32), S((t, D), F32), S((t, D), BF16), S((t, D_FF), BF16)],
        scratch_shapes=[pltpu.VMEM((bm, D), F32)],
        compiler_params=_cp("parallel", "arbitrary"), name="mlp_fwd")(y, yb, w1g, w2g, g3, b3, *deps)


def _mlp_bwd_dh(act, dzb, w1g, w2g):
    t = act.shape[0]
    bm = _mlp_row_tile(t)

    def body(a_ref, dz_ref, w1_ref, w2_ref, dh_ref, acc_ref):
        @pl.when(pl.program_id(1) == 0)
        def _():
            acc_ref[...] = jnp.zeros_like(acc_ref)

        for rows in _row_parts(bm):
            r = jnp.sqrt(a_ref[rows, :].astype(F32))
            dh = (_dot_nt(dz_ref[rows, :], w2_ref[...]) * (2.0 * r)).astype(BF16)
            dh_ref[rows, :] = dh
            acc_ref[rows, :] += _dot_nt(dh, w1_ref[...])

    row = pl.BlockSpec((bm, D), lambda i, j: (i, 0))
    hid = pl.BlockSpec((bm, FF_BLK), lambda i, j: (i, j))
    return pl.pallas_call(
        body, grid=(t // bm, N_DEV),
        in_specs=[hid, row,
                  pl.BlockSpec((None, D, FF_BLK), lambda i, j: (j, 0, 0)),
                  pl.BlockSpec((None, FF_BLK, D), lambda i, j: (j, 0, 0))],
        out_specs=[hid, row],
        out_shape=[S((t, D_FF), BF16), S((t, D), F32)],
        compiler_params=_cp("parallel", "arbitrary"), name="mlp_bwd_dh")(act, dzb, w1g, w2g)


F32_SUBLANES = 8


def _shift_dn(x, k, rows, fill=0.0):
    if k % F32_SUBLANES == 0:
        return jnp.concatenate([jnp.full((k,) + x.shape[1:], fill, x.dtype), x[:x.shape[0] - k]], axis=0)
    return jnp.where(rows >= k, pltpu.roll(x, k, 0), fill)


def _shift_up(x, k, rows, fill=0.0):
    t = x.shape[0]
    if k % F32_SUBLANES == 0:
        return jnp.concatenate([x[k:], jnp.full((k,) + x.shape[1:], fill, x.dtype)], axis=0)
    return jnp.where(rows < t - k, pltpu.roll(x, t - k, 0), fill)


def _scan_rows(a, b, shift):
    rows = lax.broadcasted_iota(jnp.int32, a.shape, 0)
    k = 1
    t = a.shape[0]
    while k < t:
        b = a * shift(b, k, rows) + b
        if 2 * k < t:
            a = a * shift(a, k, rows, 1.0)
        k *= 2
    return b


def _scan_dn(a, b):
    return _scan_rows(a, b, _shift_dn)


def _scan_up(a, b):
    return _scan_rows(a, b, _shift_up)


def _window_sum_dn(x, w, rows):
    k = 1
    while k < w:
        x = x + _shift_dn(x, k, rows)
        k *= 2
    return x


def _window_sum_up(x, w, rows):
    k = 1
    while k < w:
        x = x + _shift_up(x, k, rows)
        k *= 2
    return x


def _pool_diff(u, w, rows):
    inv_count = 1.0 / jnp.minimum(rows + 1, w).astype(F32)
    return _window_sum_dn(u, w, rows) * inv_count - u, inv_count


def _pool_fwd(proj, pool_w, pool_scale3, j):
    t = proj.shape[0]

    def body(u_ref, w_ref, s_ref, y_ref):
        rows = lax.broadcasted_iota(jnp.int32, (t, HEAD), 0)
        for g, w in enumerate(POOL_WINDOWS):
            cols = slice(g * HEAD, (g + 1) * HEAD)
            d, _ = _pool_diff(u_ref[:, cols], w, rows)
            y = _dot(d.astype(BF16), w_ref[g].astype(BF16)) * s_ref[:, cols]
            y_ref[:, cols] = y.astype(BF16)

    return pl.pallas_call(
        body, grid=(1,),
        in_specs=[pl.BlockSpec((t, POOL_W), lambda i: (0, 0)),
                  pl.BlockSpec((None, 4, HEAD, HEAD), lambda i: (j, 0, 0, 0)),
                  pl.BlockSpec((None, 1, POOL_W), lambda i: (j, 0, 0))],
        out_specs=pl.BlockSpec((t, POOL_W), lambda i: (0, 0)),
        out_shape=S((t, POOL_W), BF16), compiler_params=_cp("arbitrary"), name="pool_fwd")(proj, pool_w, pool_scale3)


def _pool_bwd(proj, dycat, pool_w, pool_scale3, j):
    t = proj.shape[0]

    def body(u_ref, dy_ref, w_ref, s_ref, du_ref, dw_ref, ds_ref):
        rows = lax.broadcasted_iota(jnp.int32, (t, HEAD), 0)
        for g, w in enumerate(POOL_WINDOWS):
            cols = slice(g * HEAD, (g + 1) * HEAD)
            d, inv_count = _pool_diff(u_ref[:, cols], w, rows)
            db = d.astype(BF16)
            wg = w_ref[g].astype(BF16)
            dy = dy_ref[:, cols]
            ds_ref[:, cols] = jnp.sum(dy * _dot(db, wg), axis=0, keepdims=True)
            dzz = (dy * s_ref[:, cols]).astype(BF16)
            dw_ref[g] = _dot_tn(db, dzz)
            dd = _dot_nt(dzz, wg)
            du_ref[:, cols] = (_window_sum_up(dd * inv_count, w, rows) - dd).astype(BF16)

    return pl.pallas_call(
        body, grid=(1,),
        in_specs=[pl.BlockSpec((t, POOL_W), lambda i: (0, 0)),
                  pl.BlockSpec((t, POOL_W), lambda i: (0, 0)),
                  pl.BlockSpec((None, 4, HEAD, HEAD), lambda i: (j, 0, 0, 0)),
                  pl.BlockSpec((None, 1, POOL_W), lambda i: (j, 0, 0))],
        out_specs=[pl.BlockSpec((t, POOL_W), lambda i: (0, 0)), _full((4, HEAD, HEAD)), _full((1, POOL_W))],
        out_shape=[S((t, POOL_W), BF16), S((4, HEAD, HEAD), F32), S((1, POOL_W), F32)],
        compiler_params=_cp("arbitrary"), name="pool_bwd")(proj, dycat, pool_w, pool_scale3)


GELU_C = 0.7978845608028654
GELU_K = 0.044715


def _gelu(x):
    th = jnp.tanh(GELU_C * (x + GELU_K * x * x * x))
    return 0.5 * x * (1.0 + th), th


def _lru_forward(u, gate, cw, cb, wa, ba, wx, bx, lam, rows, h=None):
    v = cw[3:4] * u + cw[2:3] * _shift_dn(u, 1, rows) + cw[1:2] * _shift_dn(u, 2, rows) \
        + cw[0:1] * _shift_dn(u, 3, rows) + cb
    vb = v.astype(BF16)
    r = jax.nn.sigmoid(_dot(vb, wa) + ba)
    i = jax.nn.sigmoid(_dot(vb, wx) + bx)
    sp = jnp.maximum(-lam, 0.0) + jnp.log1p(jnp.exp(-jnp.abs(lam)))
    log_a = (-LRU_C) * r * sp
    a = jnp.exp(log_a)
    one_m_a2 = -jnp.tanh(log_a) * (a * a + 1.0)
    mult = jnp.sqrt(one_m_a2)
    if h is None:
        h = _scan_dn(a, mult * (i * v))
    gl, th = _gelu(gate)
    return dict(v=v, vb=vb, r=r, i=i, sp=sp, a=a, mult=mult, h=h, gl=gl, th=th)


def _lru_specs(t, j, col0_u, col0_g):
    blk = lambda c0: pl.BlockSpec((t, HEAD), lambda h: (0, c0 + h))
    vec = pl.BlockSpec((None, 1, HEAD), lambda h: (j, 0, h))
    return [blk(col0_u), blk(col0_g),
            pl.BlockSpec((None, 4, HEAD), lambda h: (j, 0, h)), vec,
            pl.BlockSpec((None, None, HEAD, HEAD), lambda h: (j, h, 0, 0)), vec,
            pl.BlockSpec((None, None, HEAD, HEAD), lambda h: (j, h, 0, 0)), vec, vec]


def _lru_fwd(proj, p, j):
    t = proj.shape[0]

    def body(u_ref, g_ref, cw_ref, cb_ref, wa_ref, ba_ref, wx_ref, bx_ref, lam_ref, y_ref, h_ref):
        rows = lax.broadcasted_iota(jnp.int32, (t, HEAD), 0)
        f = _lru_forward(u_ref[...], g_ref[...], cw_ref[...], cb_ref[...], wa_ref[...].astype(BF16), ba_ref[...],
                         wx_ref[...].astype(BF16), bx_ref[...], lam_ref[...], rows)
        y_ref[...] = (f["h"] * f["gl"]).astype(BF16)
        h_ref[...] = f["h"]

    blk = pl.BlockSpec((t, HEAD), lambda h: (0, h))
    return pl.pallas_call(
        body, grid=(LRU_HEADS,), in_specs=_lru_specs(t, j, POOL_W // HEAD, (POOL_W + LRU_W) // HEAD),
        out_specs=[blk, blk], out_shape=[S((t, LRU_W), BF16), S((t, LRU_W), F32)],
        compiler_params=_cp("parallel"), name="lru_fwd")(
            proj, proj, p["conv_w"], p["conv_b"], p["w_a"], p["b_a"], p["w_x"], p["b_x"], p["lam"])


def _lru_bwd(proj, dycat, hstate, p, j):
    t = proj.shape[0]

    def body(u_ref, g_ref, cw_ref, cb_ref, wa_ref, ba_ref, wx_ref, bx_ref, lam_ref, dy_ref, h_ref,
             du_ref, dgate_ref, dcw_ref, dcb_ref, dwa_ref, dba_ref, dwx_ref, dbx_ref, dlam_ref):
        rows = lax.broadcasted_iota(jnp.int32, (t, HEAD), 0)
        u = u_ref[...]
        gate = g_ref[...]
        cw = cw_ref[...]
        wa = wa_ref[...].astype(BF16)
        wx = wx_ref[...].astype(BF16)
        lam = lam_ref[...]
        f = _lru_forward(u, gate, cw, cb_ref[...], wa, ba_ref[...], wx, bx_ref[...], lam, rows, h=h_ref[...])
        v, r, i, a, mult, h, th = f["v"], f["r"], f["i"], f["a"], f["mult"], f["h"], f["th"]
        dy = dy_ref[...]
        dgl = 0.5 * (1.0 + th) + 0.5 * gate * (1.0 - th * th) * GELU_C * (1.0 + 3.0 * GELU_K * gate * gate)
        dgate_ref[...] = (dy * h * dgl).astype(BF16)
        g = _scan_up(_shift_up(a, 1, rows), dy * f["gl"])
        da = g * _shift_dn(h, 1, rows)
        iv = i * v
        dmult = g * iv
        di = g * mult * v
        dv = g * mult * i
        dlog_a = da * a - dmult * (a * a) / mult
        dr = dlog_a * (-LRU_C) * f["sp"]
        dsp = jnp.sum(dlog_a * (-LRU_C) * r, axis=0, keepdims=True)
        dlam_ref[...] = -dsp * jax.nn.sigmoid(-lam)
        dpa = dr * r * (1.0 - r)
        dpx = di * i * (1.0 - i)
        dpab = dpa.astype(BF16)
        dpxb = dpx.astype(BF16)
        dwa_ref[...] = _dot_tn(f["vb"], dpab)
        dwx_ref[...] = _dot_tn(f["vb"], dpxb)
        dba_ref[...] = jnp.sum(dpa, axis=0, keepdims=True)
        dbx_ref[...] = jnp.sum(dpx, axis=0, keepdims=True)
        dv = dv + _dot_nt(dpab, wa) + _dot_nt(dpxb, wx)
        dcb_ref[...] = jnp.sum(dv, axis=0, keepdims=True)
        du = cw[3:4] * dv
        dcw_ref[3:4, :] = jnp.sum(dv * u, axis=0, keepdims=True)
        for k in (1, 2, 3):
            du = du + cw[3 - k:4 - k] * _shift_up(dv, k, rows)
            dcw_ref[3 - k:4 - k, :] = jnp.sum(dv * _shift_dn(u, k, rows), axis=0, keepdims=True)
        du_ref[...] = du.astype(BF16)

    blk = pl.BlockSpec((t, HEAD), lambda h: (0, h))
    vec = pl.BlockSpec((1, HEAD), lambda h: (0, h))
    mat = pl.BlockSpec((None, HEAD, HEAD), lambda h: (h, 0, 0))
    return pl.pallas_call(
        body, grid=(LRU_HEADS,),
        in_specs=_lru_specs(t, j, POOL_W // HEAD, (POOL_W + LRU_W) // HEAD)
        + [pl.BlockSpec((t, HEAD), lambda h: (0, POOL_W // HEAD + h)), blk],
        out_specs=[blk, blk, pl.BlockSpec((4, HEAD), lambda h: (0, h)), vec, mat, vec, mat, vec, vec],
        out_shape=[S((t, LRU_W), BF16), S((t, LRU_W), BF16), S((4, LRU_W), F32), S((1, LRU_W), F32),
                   S((LRU_HEADS, HEAD, HEAD), F32), S((1, LRU_W), F32),
                   S((LRU_HEADS, HEAD, HEAD), F32), S((1, LRU_W), F32), S((1, LRU_W), F32)],
        compiler_params=_cp("parallel"), name="lru_bwd")(
            proj, proj, p["conv_w"], p["conv_b"], p["w_a"], p["b_a"], p["w_x"], p["b_x"], p["lam"], dycat, hstate)


def _rope(x, c, s):
    x1 = x[:, :ROPE // 2]
    x2 = x[:, ROPE // 2:]
    return jnp.concatenate([x1 * c - x2 * s, x1 * s + x2 * c], axis=-1)


def _rope_t(d, c, s):
    d1 = d[:, :ROPE // 2]
    d2 = d[:, ROPE // 2:]
    return jnp.concatenate([d1 * c + d2 * s, d2 * c - d1 * s], axis=-1)


def _rope_tables(pos2, inv_freq):
    t = pos2.shape[0]

    def body(p_ref, f_ref, c_ref, s_ref):
        ang = p_ref[...].astype(F32) * f_ref[...]
        c_ref[...] = jnp.cos(ang)
        s_ref[...] = jnp.sin(ang)

    return pl.pallas_call(body, out_shape=[S((t, ROPE // 2), F32), S((t, ROPE // 2), F32)],
                          name="rope_tables")(pos2, inv_freq)


def _down_norm(xb, wdown_g, gq3, gkv3, cos, sin, j):
    t = xb.shape[0]
    bm = _row_tile(t)

    def body(x_ref, w_ref, gq_ref, gkv_ref, c_ref, s_ref, down_ref, cq_ref, ckv_ref, kpe_ref):
        w = w_ref[...].reshape(D, ODD_IN)
        down = _dot(x_ref[...], w)
        down_ref[...] = down
        q = down[:, :Q_RANK]
        cq_ref[...] = (q * lax.rsqrt(jnp.mean(q * q, axis=-1, keepdims=True) + RMS_EPS) * gq_ref[...]).astype(BF16)
        kv = down[:, Q_RANK:Q_RANK + KV_RANK]
        ckv_ref[...] = (kv * lax.rsqrt(jnp.mean(kv * kv, axis=-1, keepdims=True) + RMS_EPS)
                        * gkv_ref[...]).astype(BF16)
        kpe_ref[...] = _rope(down[:, Q_RANK + KV_RANK:], c_ref[...], s_ref[...])

    row = lambda n: pl.BlockSpec((bm, n), lambda i: (i, 0))
    return pl.pallas_call(
        body, grid=(t // bm,),
        in_specs=[row(D), _full((N_DEV, D // N_DEV, ODD_IN)),
                  pl.BlockSpec((None, 1, Q_RANK), lambda i: (j, 0, 0)),
                  pl.BlockSpec((None, 1, KV_RANK), lambda i: (j, 0, 0)), row(ROPE // 2), row(ROPE // 2)],
        out_specs=[row(ODD_IN), row(Q_RANK), row(KV_RANK), row(ROPE)],
        out_shape=[S((t, ODD_IN), F32), S((t, Q_RANK), BF16), S((t, KV_RANK), BF16), S((t, ROPE), F32)],
        compiler_params=_cp("parallel"), name="down_norm")(xb, wdown_g, gq3, gkv3, cos, sin)


def _q_tile(t, widest):
    return min(widest, t // 2)


def _attn_probs(q, k, qs):
    s = _dot_nt(q, k) * ATT_SCALE
    tq = q.shape[0]
    rows = lax.broadcasted_iota(jnp.int32, (tq, tq), 0)
    cols = lax.broadcasted_iota(jnp.int32, (tq, tq), 1)
    last = jnp.where(jnp.right_shift(cols, CHUNK_SHIFT) <= jnp.right_shift(rows, CHUNK_SHIFT), s[:, qs:], NEG)
    s = last if qs == 0 else jnp.concatenate([s[:, :qs], last], axis=1)
    e = jnp.exp(s - jnp.max(s, axis=-1, keepdims=True))
    return e / jnp.sum(e, axis=-1, keepdims=True)


def _head_qkv(cq, ckv, kpe, c, s, wq_ref, wkv_ref):
    q = jnp.concatenate([_dot(cq, wq_ref[:, :NOPE]), _rope(_dot(cq, wq_ref[:, NOPE:]), c, s)], axis=1).astype(BF16)
    k = jnp.concatenate([_dot(ckv, wkv_ref[:, :NOPE]), kpe], axis=1).astype(BF16)
    vv = _dot(ckv, wkv_ref[:, NOPE:]).astype(BF16)
    return q, k, vv


def _attn_in_specs(t):
    return [_full((t, Q_RANK)), _full((t, KV_RANK)), _full((t, ROPE)), _full((t, ROPE // 2)), _full((t, ROPE // 2)),
            pl.BlockSpec((None, Q_RANK, NOPE + ROPE), lambda h: (h, 0, 0)),
            pl.BlockSpec((None, KV_RANK, NOPE + VDIM), lambda h: (h, 0, 0)),
            pl.BlockSpec((None, VDIM, D), lambda h: (h, 0, 0))]


def _attn_fwd(cq, ckv, kpe, cos, sin, wqb_g, wkvb_g, wo_g):
    t = cq.shape[0]
    tq = _q_tile(t, 256)

    def body(cq_ref, ckv_ref, kpe_ref, c_ref, s_ref, wq_ref, wkv_ref, wo_ref, o_ref, mix_ref):
        q, k, vv = _head_qkv(cq_ref[...], ckv_ref[...], kpe_ref[...], c_ref[...], s_ref[...], wq_ref, wkv_ref)
        for qs in range(0, t, tq):
            ke = qs + tq
            p = _attn_probs(q[qs:ke], k[:ke], qs)
            o_ref[qs:ke, :] = _dot(p.astype(BF16), vv[:ke]).astype(BF16)
        c = _dot(o_ref[...], wo_ref[...])

        @pl.when(pl.program_id(0) == 0)
        def _():
            mix_ref[...] = c

        @pl.when(pl.program_id(0) > 0)
        def _():
            mix_ref[...] += c

    return pl.pallas_call(
        body, grid=(MLA_HEADS,), in_specs=_attn_in_specs(t),
        out_specs=[pl.BlockSpec((None, t, VDIM), lambda h: (h, 0, 0)), _full((t, D))],
        out_shape=[S((MLA_HEADS, t, VDIM), BF16), S((t, D), F32)],
        compiler_params=_cp("arbitrary"), name="attn_fwd")(cq, ckv, kpe, cos, sin, wqb_g, wkvb_g, wo_g)


def _attn_bwd(cq, ckv, kpe, cos, sin, wqb_g, wkvb_g, wo_g, o, dzb):
    t = cq.shape[0]
    tq = _q_tile(t, 512)

    def body(cq_ref, ckv_ref, kpe_ref, c_ref, s_ref, wq_ref, wkv_ref, wo_ref, o_ref, dz_ref,
             dwo_ref, dwq_ref, dwkv_ref, dcq_ref, dckv_ref, dkpe_ref, dkt_s, dvt_s, dq_s):
        cqv = cq_ref[...]
        ckvv = ckv_ref[...]
        c = c_ref[...]
        s = s_ref[...]
        q, k, vv = _head_qkv(cqv, ckvv, kpe_ref[...], c, s, wq_ref, wkv_ref)
        dzv = dz_ref[...]
        dwo_ref[...] = _dot_tn(o_ref[...], dzv).astype(BF16)
        do = _dot_nt(dzv, wo_ref[...]).astype(BF16)
        dkt_s[...] = jnp.zeros_like(dkt_s)
        dvt_s[...] = jnp.zeros_like(dvt_s)
        for qs in range(0, t, tq):
            ke = qs + tq
            p = _attn_probs(q[qs:ke], k[:ke], qs)
            dp = _dot_nt(do[qs:ke], vv[:ke])
            ds = (p * (dp - jnp.sum(p * dp, axis=-1, keepdims=True)) * ATT_SCALE).astype(BF16)
            dq_s[qs:ke, :] = _dot(ds, k[:ke])
            dkt_s[0:NOPE + ROPE, 0:ke] += _dot_tn(q[qs:ke], ds)
            dvt_s[:, 0:ke] += _dot_tn(do[qs:ke], p.astype(BF16))
        dk = dkt_s[...].T
        dqn = dq_s[:, :NOPE].astype(BF16)
        dqp = _rope_t(dq_s[:, NOPE:], c, s).astype(BF16)
        dkn = dk[:, :NOPE].astype(BF16)
        dkp = dk[:, NOPE:NOPE + ROPE]
        dvv = dvt_s[...].T.astype(BF16)
        dwq_ref[:, :NOPE] = _dot_tn(cqv, dqn).astype(BF16)
        dwq_ref[:, NOPE:] = _dot_tn(cqv, dqp).astype(BF16)
        dwkv_ref[:, :NOPE] = _dot_tn(ckvv, dkn).astype(BF16)
        dwkv_ref[:, NOPE:] = _dot_tn(ckvv, dvv).astype(BF16)
        dcq = _dot_nt(dqn, wq_ref[:, :NOPE]) + _dot_nt(dqp, wq_ref[:, NOPE:])
        dckv = _dot_nt(dkn, wkv_ref[:, :NOPE]) + _dot_nt(dvv, wkv_ref[:, NOPE:])

        @pl.when(pl.program_id(0) == 0)
        def _():
            dcq_ref[...] = dcq
            dckv_ref[...] = dckv
            dkpe_ref[...] = dkp

        @pl.when(pl.program_id(0) > 0)
        def _():
            dcq_ref[...] += dcq
            dckv_ref[...] += dckv
            dkpe_ref[...] += dkp

    per_head = lambda a, b: pl.BlockSpec((None, a, b), lambda h: (h, 0, 0))
    return pl.pallas_call(
        body, grid=(MLA_HEADS,),
        in_specs=_attn_in_specs(t) + [per_head(t, VDIM), _full((t, D))],
        out_specs=[per_head(VDIM, D), per_head(Q_RANK, NOPE + ROPE), per_head(KV_RANK, NOPE + VDIM),
                   _full((t, Q_RANK)), _full((t, KV_RANK)), _full((t, ROPE))],
        out_shape=[S((MLA_HEADS, VDIM, D), BF16), S((MLA_HEADS, Q_RANK, NOPE + ROPE), BF16),
                   S((MLA_HEADS, KV_RANK, NOPE + VDIM), BF16),
                   S((t, Q_RANK), F32), S((t, KV_RANK), F32), S((t, ROPE), F32)],
        scratch_shapes=[pltpu.VMEM((2 * NOPE, t), F32), pltpu.VMEM((VDIM, t), F32),
                        pltpu.VMEM((t, NOPE + ROPE), F32)],
        compiler_params=_cp("arbitrary"), name="attn_bwd")(cq, ckv, kpe, cos, sin, wqb_g, wkvb_g, wo_g, o, dzb)


def _rms_bwd(down, dcq, dckv, dkpe, cos, sin, gq3, gkv3, j):
    t = down.shape[0]
    bm = _row_tile(t)

    def body(down_ref, dcq_ref, dckv_ref, dkpe_ref, c_ref, s_ref, gq_ref, gkv_ref, dd_ref, dgq_ref, dgkv_ref):
        @pl.when(pl.program_id(0) == 0)
        def _():
            dgq_ref[...] = jnp.zeros_like(dgq_ref)
            dgkv_ref[...] = jnp.zeros_like(dgkv_ref)

        def rms_b(x, dy, g):
            rstd = lax.rsqrt(jnp.mean(x * x, axis=-1, keepdims=True) + RMS_EPS)
            xh = x * rstd
            dyg = dy * g
            return rstd * (dyg - xh * jnp.mean(dyg * xh, axis=-1, keepdims=True)), jnp.sum(dy * xh, axis=0, keepdims=True)

        dq, dgq = rms_b(down_ref[:, :Q_RANK], dcq_ref[...], gq_ref[...])
        dkv, dgkv = rms_b(down_ref[:, Q_RANK:Q_RANK + KV_RANK], dckv_ref[...], gkv_ref[...])
        dgq_ref[...] += dgq
        dgkv_ref[...] += dgkv
        dd_ref[:, :Q_RANK] = dq.astype(BF16)
        dd_ref[:, Q_RANK:Q_RANK + KV_RANK] = dkv.astype(BF16)
        dd_ref[:, Q_RANK + KV_RANK:] = _rope_t(dkpe_ref[...], c_ref[...], s_ref[...]).astype(BF16)

    row = lambda n: pl.BlockSpec((bm, n), lambda i: (i, 0))
    return pl.pallas_call(
        body, grid=(t // bm,),
        in_specs=[row(ODD_IN), row(Q_RANK), row(KV_RANK), row(ROPE), row(ROPE // 2), row(ROPE // 2),
                  pl.BlockSpec((None, 1, Q_RANK), lambda i: (j, 0, 0)),
                  pl.BlockSpec((None, 1, KV_RANK), lambda i: (j, 0, 0))],
        out_specs=[row(ODD_IN), _full((1, Q_RANK)), _full((1, KV_RANK))],
        out_shape=[S((t, ODD_IN), BF16), S((1, Q_RANK), F32), S((1, KV_RANK), F32)],
        compiler_params=_cp("arbitrary"), name="rms_bwd")(down, dcq, dckv, dkpe, cos, sin, gq3, gkv3)


def _col_blocks(t, n, bn):
    return pl.BlockSpec((t, bn), lambda i: (0, i))


def _row_blocks(n, bm):
    return pl.BlockSpec((bm, n), lambda i: (i, 0))


def _local_step(x, pos2, tgt, small, weights_of, grads_done, start_dep=None, prefetch=None):
    t = x.shape[0]
    bm = min(512, t)
    inv_freq = (ROPE_THETA ** (-jnp.arange(0, ROPE, 2, dtype=F32) / ROPE)).reshape(1, ROPE // 2)
    cos, sin = _rope_tables(pos2, inv_freq)
    lru_p = {k: small[k] for k in ("conv_w", "conv_b", "w_a", "b_a", "w_x", "b_x", "lam")}

    saved = []
    y, yb = x, x
    for l in range(DEPTH):
        j = l // 2
        big = weights_of(l, 0, y)
        sv = dict(xb=yb, big=big)
        if l % 2 == 0:
            proj = _mm(yb, big["win_t"], mode="nt", grid=(EVEN_IN // 512,), a_spec=_full((t, D)),
                       b_spec=_row_blocks(D, 512), out_shape=S((t, EVEN_IN), F32),
                       out_spec=_col_blocks(t, EVEN_IN, 512), name="even_proj", dep=start_dep if l == 0 else None)
            y_lru, hstate = _lru_fwd(proj, lru_p, j)
            ycat = jnp.concatenate([_pool_fwd(proj, small["pool_w"], small["pool_scale"], j), y_lru], axis=1)
            big.update(weights_of(l, 1, ycat))
            z1, y1, y1b = _proj_resid_ln(y, ycat, big["wout2d"], small["ln_mix_g"], small["ln_mix_b"], l, "even_out")
            sv.update(proj=proj, ycat=ycat, hstate=hstate)
        else:
            down, cq, ckv, kpe = _down_norm(yb, big["wdown"], small["gq"], small["gkv"], cos, sin, j)
            o, mix = _attn_fwd(cq, ckv, kpe, cos, sin, big["wqb"], big["wkvb"], big["wo"])
            z1, y1, y1b = _resid_ln(y, mix, small["ln_mix_g"], small["ln_mix_b"], l, "resid_ln")
            sv.update(down=down, cq=cq, ckv=ckv, kpe=kpe, o=o)
        fetched = prefetch(l + 1, y1) if prefetch is not None and l + 1 < DEPTH else None
        z2, y, yb, act = _mlp_fwd(y1, y1b, big["w1"], big["w2"], small["ln_ffn_g"], small["ln_ffn_b"], l,
                                  dep=fetched)
        sv.update(z1=z1, y1b=y1b, z2=z2, act=act)
        saved.append(sv)

    dy, loss_tile = _loss_grad(y, tgt)

    g = {k: [None] * n for k, n in (("ln_mix_g", 4), ("ln_mix_b", 4), ("ln_ffn_g", 4), ("ln_ffn_b", 4),
                                    ("pool_w", 2), ("pool_scale", 2), ("conv_w", 2), ("conv_b", 2),
                                    ("w_a", 2), ("b_a", 2), ("w_x", 2), ("b_x", 2), ("lam", 2),
                                    ("gq", 2), ("gkv", 2))}
    dep = None
    for l in reversed(range(DEPTH)):
        j = l // 2
        sv = saved[l]
        big = sv["big"]
        dz2, dz2b, g["ln_ffn_g"][l], g["ln_ffn_b"][l] = _ln_bwd(dy, sv["z2"], small["ln_ffn_g"], l, "ln_bwd", dep=dep)
        act = sv["act"]
        dh, dff = _mlp_bwd_dh(act, dz2b, big["w1"], big["w2"])
        dw1 = _mm(sv["y1b"], dh, mode="tn", grid=(N_DEV,), a_spec=_full((t, D)),
                  b_spec=_col_blocks(t, D_FF, FF_BLK), out_shape=S((N_DEV, D, FF_BLK), BF16),
                  out_spec=pl.BlockSpec((None, D, FF_BLK), lambda i: (i, 0, 0)), name="mlp_dw1")
        dw2 = _mm(act, dz2b, mode="tn", grid=(N_DEV,), a_spec=_col_blocks(t, D_FF, FF_BLK),
                  b_spec=_full((t, D)), out_shape=S((N_DEV, FF_BLK, D), BF16),
                  out_spec=pl.BlockSpec((None, FF_BLK, D), lambda i: (i, 0, 0)), name="mlp_dw2")
        dep = grads_done(l, dict(w1=dw1, w2=dw2))
        dz1, dz1b, g["ln_mix_g"][l], g["ln_mix_b"][l] = _ln_bwd(dff, sv["z1"], small["ln_mix_g"], l, "ln_bwd_res",
                                                                 r=dz2, dep=dep)
        if l % 2 == 0:
            wout = big["wout2d"]
            dycat = _mm(dz1b, wout, mode="nt", grid=(EVEN_MIX // 512,), a_spec=_full((t, D)),
                        b_spec=_row_blocks(D, 512), out_shape=S((t, EVEN_MIX), F32),
                        out_spec=_col_blocks(t, EVEN_MIX, 512), name="even_dycat")
            dwout = _mm(sv["ycat"], dz1b, mode="tn", grid=(EVEN_MIX // 512,), a_spec=_col_blocks(t, EVEN_MIX, 512),
                        b_spec=_full((t, D)), out_shape=S((EVEN_MIX, D), BF16), out_spec=_row_blocks(D, 512),
                        name="even_dwout")
            du_pool, g["pool_w"][j], g["pool_scale"][j] = _pool_bwd(sv["proj"], dycat, small["pool_w"],
                                                                   small["pool_scale"], j)
            (du_lru, du_gate, g["conv_w"][j], g["conv_b"][j], g["w_a"][j], g["b_a"][j], g["w_x"][j], g["b_x"][j],
             g["lam"][j]) = _lru_bwd(sv["proj"], dycat, sv["hstate"], lru_p, j)
            dproj = jnp.concatenate([du_pool, du_lru, du_gate], axis=1)
            dep = grads_done(l, dict(win=_even_dwin(sv["xb"], dproj), wout=dwout.reshape(N_DEV, EVEN_MIX // N_DEV, D)))
            dy = _mm(dproj, big["win_t"], mode="nn", grid=(t // bm,), a_spec=_row_blocks(EVEN_IN, bm),
                     b_spec=_full((EVEN_IN, D)), out_shape=S((t, D), F32), out_spec=_row_blocks(D, bm),
                     add=dz1, add_spec=_row_blocks(D, bm), add_scale=ALPHA, name="even_dx", dep=dep)
        else:
            dwo, dwqb, dwkvb, dcq, dckv, dkpe = _attn_bwd(
                sv["cq"], sv["ckv"], sv["kpe"], cos, sin, big["wqb"], big["wkvb"], big["wo"], sv["o"], dz1b)
            ddown, g["gq"][j], g["gkv"][j] = _rms_bwd(sv["down"], dcq, dckv, dkpe, cos, sin, small["gq"],
                                                     small["gkv"], j)
            dwdown = _mm(sv["xb"], ddown, mode="tn", grid=(N_DEV,), a_spec=_col_blocks(t, D, D // N_DEV),
                         b_spec=_full((t, ODD_IN)), out_shape=S((N_DEV, D // N_DEV, ODD_IN), BF16),
                         out_spec=pl.BlockSpec((None, D // N_DEV, ODD_IN), lambda i: (i, 0, 0)),
                         name="odd_dwdown")
            dep = grads_done(l, dict(wdown=dwdown, wqb=dwqb, wkvb=dwkvb, wo=dwo))
            dy = _mm(ddown, big["wdown2d"], mode="nt", grid=(t // bm,), a_spec=_row_blocks(ODD_IN, bm),
                     b_spec=_full((D, ODD_IN)), out_shape=S((t, D), F32), out_spec=_row_blocks(D, bm),
                     add=dz1, add_spec=_row_blocks(D, bm), add_scale=ALPHA, name="odd_dx", dep=dep)
    return loss_tile[0, 0], dy, g


def _mesh_place():
    x, y, c = lax.axis_index("x"), lax.axis_index("y"), lax.axis_index("c")
    return x, y, c


def _peer(place, k):
    x, y, c = place
    return (1 - x if k & 4 else x, 1 - y if k & 2 else y, 1 - c if k & 1 else c)


def _index(place):
    x, y, c = place
    return 4 * x + 2 * y + c


ANY = pl.BlockSpec(memory_space=pl.ANY)


def _make_zones(shards, me, name, dtype=BF16):
    n = len(shards)

    def body(me_ref, *refs):
        for src, dst in zip(refs[:n], refs[n:]):
            dst[...] = src[...].astype(dtype)

    grid_spec = pltpu.PrefetchScalarGridSpec(
        num_scalar_prefetch=1, grid=(1,),
        in_specs=[pl.BlockSpec(s.shape, lambda i, me_ref: (0, 0)) for s in shards],
        out_specs=[pl.BlockSpec((None,) + s.shape, lambda i, me_ref: (me_ref[0], 0, 0)) for s in shards])
    return pl.pallas_call(body, grid_spec=grid_spec, out_shape=[S((N_DEV,) + s.shape, dtype) for s in shards],
                          compiler_params=_cp("arbitrary"), name=name)(me, *shards)


def _shard_rows_tile(a):
    return max(d for d in range(16, 257, 16) if a % d == 0)


HBM = pl.BlockSpec(memory_space=pltpu.HBM)
SEM = pl.BlockSpec(memory_space=pltpu.SEMAPHORE)
DATAFLOW = pltpu.SideEffectType.DATAFLOW_SIDE_EFFECTING


def _in_hbm(a):
    return pltpu.with_memory_space_constraint(a, pltpu.HBM)


def _gather_ici_copies(place, src, land, w):
    me = _index(place)
    return [(_peer(place, k), land.at[me], land.at[me]) for k in (1, 2, 4, 6)]


def _gather_d2d_copies(place, src, land, w):
    blocks = [_index(_peer(place, k)) for k in (2, 4, 6)]
    return [(_peer(place, 1), land.at[b], land.at[b]) for b in blocks]


GATHER_ICI = (4, _gather_ici_copies)
GATHER_D2D = (3, _gather_d2d_copies)


def _scatter_plan(layers):
    def copies(place, src, land, w):
        me = _index(place)
        mine = land.at[me] if layers[w] is None else land.at[me, layers[w]]
        return [(_peer(place, k), src.at[_index(_peer(place, k))], mine) for k in range(1, N_DEV)]
    return (N_DEV - 1, copies)


def _gather_all_copies(place, src, land, w):
    me = _index(place)
    return [(_peer(place, k), land.at[me], land.at[me]) for k in range(1, N_DEV)]


GATHER_ALL = (N_DEV - 1, _gather_all_copies)


def _sum_blocks(zone, part, me):
    r = part.shape[1]

    def body(me_ref, z_ref, p_ref, o_ref):
        acc = None
        for s in range(N_DEV):
            term = jnp.where(me_ref[0] == s, p_ref[...], z_ref[s])
            acc = term if acc is None else acc + term
        o_ref[...] = acc

    grid_spec = pltpu.PrefetchScalarGridSpec(
        num_scalar_prefetch=1, grid=(1,),
        in_specs=[pl.BlockSpec((N_DEV, r, 128), lambda i, me_ref: (0, 0, 0)),
                  pl.BlockSpec((None, r, 128), lambda i, me_ref: (me_ref[0], 0, 0))],
        out_specs=pl.BlockSpec((r, 128), lambda i, me_ref: (0, 0)))
    return pl.pallas_call(body, grid_spec=grid_spec, out_shape=S((r, 128), F32),
                          compiler_params=_cp("arbitrary"), name="sum_small")(me, zone, part)


def _exchange_start(srcs, lands, plan, name, after=()):
    ns, n = len(srcs), len(lands)
    n_in = ns + n + len(after)
    per, copies = plan

    def body(*refs):
        ins, land = refs[:ns], refs[ns:ns + n]
        send, recv = refs[n_in], refs[n_in + 1]
        token = refs[-1]
        place = _mesh_place()
        for i in range(per):
            for w in range(n):
                target, src, dst = copies(place, ins[w] if ns else None, land[w], w)[i]
                pltpu.make_async_remote_copy(src_ref=src, dst_ref=dst, send_sem=send.at[w * per + i],
                                             recv_sem=recv.at[w * per + i], device_id=target, device_id_type=MESH).start()
        token[...] = jnp.zeros_like(token)

    sems = pltpu.SemaphoreType.DMA((n * per,))
    thru = [pltpu.HBM(a.shape, a.dtype) for a in list(srcs) + list(lands)]
    out = pl.pallas_call(
        body, name=name, in_specs=[HBM] * (ns + n) + [ANY] * len(after),
        out_shape=(sems, sems, *thru, S((8, 128), F32)),
        out_specs=(SEM, SEM, *([HBM] * (ns + n)), pl.BlockSpec(memory_space=pltpu.VMEM)),
        input_output_aliases={i: 2 + i for i in range(ns + n)},
        compiler_params=pltpu.CompilerParams(has_side_effects=DATAFLOW),
    )(*[_in_hbm(a) for a in list(srcs) + list(lands)], *after)
    return out[0], out[1], list(out[2:2 + ns]), list(out[2 + ns:2 + ns + n]), out[-1]


def _exchange_wait(send, recv, srcs, lands, plan, after, name):
    ns, n = len(srcs), len(lands)
    per, copies = plan
    afters = tuple(after) if isinstance(after, (tuple, list)) else (after,)

    def body(*refs):
        ins, land = refs[:ns], refs[ns:ns + n]
        send_ref, recv_ref = refs[ns + n], refs[ns + n + 1]
        place = _mesh_place()
        for i in range(per):
            for w in range(n):
                target, src, dst = copies(place, ins[w] if ns else None, land[w], w)[i]
                cp = pltpu.make_async_remote_copy(src_ref=src, dst_ref=dst, send_sem=send_ref.at[w * per + i],
                                                  recv_sem=recv_ref.at[w * per + i], device_id=target,
                                                  device_id_type=MESH)
                cp.wait_send()
                cp.wait_recv()

    thru = [pltpu.HBM(a.shape, a.dtype) for a in list(srcs) + list(lands)]
    out = pl.pallas_call(
        body, name=name, in_specs=[HBM] * (ns + n) + [SEM, SEM] + [ANY] * len(afters),
        out_shape=tuple(thru), out_specs=tuple([HBM] * (ns + n)),
        input_output_aliases={i: i for i in range(ns + n)},
        compiler_params=pltpu.CompilerParams(has_side_effects=DATAFLOW),
    )(*srcs, *lands, send, recv, *afters)
    return list(out[:ns]), list(out[ns:])


def _adamw(w, g, m, v):
    m = ADAM_B1 * m + (1.0 - ADAM_B1) * g
    v = ADAM_B2 * v + (1.0 - ADAM_B2) * (g * g)
    m_hat = m / (1.0 - ADAM_B1 ** ADAM_STEP)
    v_hat = v / (1.0 - ADAM_B2 ** ADAM_STEP)
    return -ADAM_LR * (m_hat / (jnp.sqrt(v_hat) + ADAM_EPS) + ADAM_WD * w), m, v


def _adam_big(parts, own, me, w, m, v, name):
    nl, a, b = w.shape
    ta = _shard_rows_tile(a)

    def body(me_ref, p_ref, *refs):
        own_refs, (w_ref, m_ref, v_ref, g_ref, d_ref, mo_ref, vo_ref) = refs[:nl], refs[nl:]
        layer = pl.program_id(0)
        mine = own_refs[0][...]
        for k in range(1, nl):
            mine = jnp.where(layer == k, own_refs[k][...], mine)
        g = None
        for s in range(N_DEV):
            term = jnp.where(me_ref[0] == s, mine, p_ref[s]).astype(F32)
            g = term if g is None else g + term
        g_ref[...] = g
        d_ref[...], mo_ref[...], vo_ref[...] = _adamw(w_ref[...], g, m_ref[...], v_ref[...])

    blk = pl.BlockSpec((None, ta, b), lambda l, i, me_ref: (l, i, 0))

    def own_spec(k):
        return pl.BlockSpec((None, ta, b), lambda l, i, me_ref: (me_ref[0], jnp.where(l == k, i, 0), 0))

    grid_spec = pltpu.PrefetchScalarGridSpec(
        num_scalar_prefetch=1, grid=(nl, a // ta),
        in_specs=[pl.BlockSpec((N_DEV, None, ta, b), lambda l, i, me_ref: (0, l, i, 0))]
        + [own_spec(k) for k in range(nl)] + [blk, blk, blk],
        out_specs=[blk] * 4)
    return pl.pallas_call(body, grid_spec=grid_spec, out_shape=[S(w.shape, F32)] * 4,
                          compiler_params=_cp("arbitrary", "arbitrary"), name=name)(me, parts, *own, w, m, v)


def _adam_small(gs, ws, ms, vs):
    n = len(gs)

    def body(*refs):
        ins, outs = refs[:4 * n], refs[4 * n:]
        for i in range(n):
            g_ref, w_ref, m_ref, v_ref = (ins[k * n + i] for k in range(4))
            outs[i][...], outs[n + i][...], outs[2 * n + i][...] = _adamw(w_ref[...], g_ref[...], m_ref[...], v_ref[...])

    out = pl.pallas_call(body, out_shape=[S(g.shape, F32) for g in gs] * 3, compiler_params=_cp(),
                         name="adam_small")(*gs, *ws, *ms, *vs)
    return out[:n], out[n:2 * n], out[2 * n:]


BIG = ("even_w_in", "even_w_out", "mla_w_down", "mla_w_qb", "mla_w_kvb", "mla_w_o", "mlp_w1", "mlp_w2")
BIG_KEY = dict(even_w_in="win", even_w_out="wout", mla_w_down="wdown", mla_w_qb="wqb", mla_w_kvb="wkvb",
               mla_w_o="wo", mlp_w1="w1", mlp_w2="w2")
SMALL = (("ln_mix_g", "ln_mix_g", None), ("ln_mix_b", "ln_mix_b", None), ("ln_ffn_g", "ln_ffn_g", None),
         ("ln_ffn_b", "ln_ffn_b", None), ("pool_w", "pool_w", None), ("pool_scale", "pool_scale", None),
         ("lru_conv_w", "conv_w", 2), ("lru_conv_b", "conv_b", None), ("lru_w_a", "w_a", None),
         ("lru_b_a", "b_a", None), ("lru_w_x", "w_x", None), ("lru_b_x", "b_x", None), ("lru_lambda", "lam", None),
         ("mla_q_norm_g", "gq", 1), ("mla_kv_norm_g", "gkv", 1))
WEIGHTS = ("ln_mix_g", "ln_mix_b", "ln_ffn_g", "ln_ffn_b", "even_w_in", "pool_w", "pool_scale", "lru_conv_w",
           "lru_conv_b", "lru_w_a", "lru_b_a", "lru_w_x", "lru_b_x", "lru_lambda", "even_w_out", "mla_w_down",
           "mla_q_norm_g", "mla_kv_norm_g", "mla_w_qb", "mla_w_kvb", "mla_w_o", "mlp_w1", "mlp_w2")


def _layer_weights(l):
    j = l // 2
    if l % 2 == 0:
        mixer = [("win", "even_w_in", j), ("wout", "even_w_out", j)]
    else:
        mixer = [("wdown", "mla_w_down", j), ("wqb", "mla_w_qb", j), ("wkvb", "mla_w_kvb", j), ("wo", "mla_w_o", j)]
    return mixer + [("w1", "mlp_w1", l), ("w2", "mlp_w2", l)]


def _pack(arrays, multiple):
    flat = jnp.concatenate([a.reshape(-1) for a in arrays])
    pad = (-flat.shape[0]) % multiple
    return jnp.pad(flat, (0, pad))


def _unpack(flat, shapes):
    out, at = [], 0
    for shp in shapes:
        n = 1
        for s in shp:
            n *= s
        out.append(flat[at:at + n].reshape(shp))
        at += n
    return out


def _global_shape(local_shape, axis):
    if axis is None:
        return tuple(local_shape)
    return tuple(s * N_DEV if i == axis else s for i, s in enumerate(local_shape))


def _step(x, positions, tgt, w, m, v):
    t = x.shape[1]
    me = _index(_mesh_place())

    chunk = N_DEV * 8 * 128
    me_arr = me.astype(jnp.int32).reshape(1)

    lanes = lambda a: jnp.pad(a, ((0, 0), (0, 128 - a.shape[1])))
    mine_packed = jnp.concatenate([w["lru_conv_w"].reshape(8, HEAD), lanes(w["mla_q_norm_g"]),
                                   lanes(w["mla_kv_norm_g"]), jnp.zeros((4, 128), F32)])
    g_send, g_recv, _, g_land, token = _exchange_start([], _make_zones([mine_packed], me_arr, "zones_small", F32),
                                                       GATHER_ALL, "small_params_start")

    def keys_of(l, part):
        keys = [key for key, _, _ in _layer_weights(l)]
        if l == 0:
            return keys[:1] if part == 0 else keys[1:]
        return keys if part == 0 else []

    shard_of = {(l, key): (w[name][i].T if key == "win" else w[name][i])
                for l in range(DEPTH) for key, name, i in _layer_weights(l)}
    flights, after = {}, (token,)
    for l in range(DEPTH):
        for part in (0, 1):
            if keys_of(l, part):
                zones = _make_zones([shard_of[l, key] for key in keys_of(l, part)], me_arr, "zones_%d_%d" % (l, part))
                send, recv, _, lands, token = _exchange_start([], zones, GATHER_ICI, "gather_start_%d_%d" % (l, part),
                                                              after=after)
                flights[l, part] = (send, recv, [], lands)
                after = (token,)

    _, g_land = _exchange_wait(g_send, g_recv, [], g_land, GATHER_ALL, token, "small_params_wait")
    rows_first = g_land[0].transpose(1, 0, 2)
    q_shard, kv_shard = w["mla_q_norm_g"].shape[1], w["mla_kv_norm_g"].shape[1]
    full = dict(lru_conv_w=rows_first[:8].reshape(2, 4, LRU_W),
                mla_q_norm_g=rows_first[8:10, :, :q_shard].reshape(2, Q_RANK),
                mla_kv_norm_g=rows_first[10:12, :, :kv_shard].reshape(2, KV_RANK))

    passing = {}

    def pass_on(l, part, after):
        tag = "%d_%d" % (l, part)
        _, lands = _exchange_wait(*flights[l, part], GATHER_ICI, after, "gather_wait_" + tag)
        send, recv, _, lands, token = _exchange_start([], lands, GATHER_D2D, "gather_pass_" + tag)
        passing[l, part] = (send, recv, [], lands)
        return token

    def early_pass(l, after):
        return pass_on(l, 0, after) if l >= 2 else None

    def weights_of(l, part, after):
        keys = keys_of(l, part)
        if keys:
            if (l, part) not in passing:
                pass_on(l, part, after)
            _, arrays = _exchange_wait(*passing[l, part], GATHER_D2D, after, "gather_pass_wait_%d_%d" % (l, part))
        big = dict(zip(keys, arrays)) if keys else {}
        if "win" in big:
            big["win_t"] = big["win"].reshape(EVEN_IN, D)
        if "wout" in big:
            big["wout2d"] = big["wout"].reshape(EVEN_MIX, D)
        if "wdown" in big:
            big["wdown2d"] = big["wdown"].reshape(D, ODD_IN)
        return big

    zone = {name: lax.empty((N_DEV,) + w[name].shape, BF16) for name in BIG}
    name_of = {key: name for name, key in BIG_KEY.items()}
    sent, last_token = [], [None]

    def grads_done(l, grads):
        keys = list(grads)
        index = {key: i for key, _, i in _layer_weights(l)}
        layers = [index[key] for key in keys]
        send, recv, srcs, lands, tok = _exchange_start([grads[k] for k in keys], [zone[name_of[k]] for k in keys],
                                                       _scatter_plan(layers), "scatter_start_%d_%s" % (l, keys[0]))
        for k, land in zip(keys, lands):
            zone[name_of[k]] = land
        sent.append((send, recv, srcs, keys, layers))
        last_token[0] = tok
        return tok

    row3 = lambda a: a.reshape(a.shape[0], 1, a.shape[1])
    small = dict(ln_mix_g=row3(w["ln_mix_g"]), ln_mix_b=row3(w["ln_mix_b"]), ln_ffn_g=row3(w["ln_ffn_g"]),
                 ln_ffn_b=row3(w["ln_ffn_b"]), pool_w=w["pool_w"], pool_scale=row3(w["pool_scale"]),
                 conv_w=full["lru_conv_w"], conv_b=row3(w["lru_conv_b"]), w_a=w["lru_w_a"], b_a=row3(w["lru_b_a"]),
                 w_x=w["lru_w_x"], b_x=row3(w["lru_b_x"]), lam=row3(w["lru_lambda"]),
                 gq=row3(full["mla_q_norm_g"]), gkv=row3(full["mla_kv_norm_g"]))

    loss_part, grad_x, g = _local_step(x[0], positions.reshape(t, 1), tgt[0], small, weights_of, grads_done,
                                       start_dep=token, prefetch=early_pass)

    own = {name: [None] * w[name].shape[0] for name in BIG}
    me_arr = me.astype(jnp.int32).reshape(1)
    out = {}
    local_g = [jnp.stack(g[key]).reshape(_global_shape(w[name].shape, axis)) for name, key, axis in SMALL]
    local_g.append(loss_part.reshape(1))
    part = _pack(local_g, chunk).reshape(N_DEV, -1, 128)
    small_plan = _scatter_plan([None])
    s_send, s_recv, s_src, s_land, after = _exchange_start([part], [lax.empty(part.shape, F32)], small_plan,
                                                           "small_scatter_start", after=(last_token[0],))
    for n_flight, (send, recv, srcs, keys, layers) in enumerate(sent):
        if n_flight == len(sent) - 1:
            for name in BIG:
                if BIG_KEY[name] not in keys:
                    out[name] = _adam_big(zone[name], own[name], me_arr, w[name], m[name], v[name], "adam_" + name)
            s_src, s_land = _exchange_wait(s_send, s_recv, s_src, s_land, small_plan,
                                           [grad_x] + [o[0] for o in out.values()], "small_scatter_wait")
            chunk_sum = _sum_blocks(s_land[0], s_src[0], me_arr)
            r_zone = lax.dynamic_update_slice_in_dim(lax.empty(part.shape, F32), chunk_sum[None], me, 0)
            r_send, r_recv, _, r_land, after = _exchange_start([], [r_zone], GATHER_ALL, "small_gather_start")
        srcs, lands = _exchange_wait(send, recv, srcs, [zone[name_of[k]] for k in keys], _scatter_plan(layers),
                                     after, "scatter_wait_%d" % n_flight)
        for k, land, src, layer in zip(keys, lands, srcs, layers):
            zone[name_of[k]] = land
            own[name_of[k]][layer] = src
        after = lands[0]
    for name in BIG:
        if name not in out:
            out[name] = _adam_big(zone[name], own[name], me_arr, w[name], m[name], v[name], "adam_" + name)

    _, reduced = _exchange_wait(r_send, r_recv, [], r_land, GATHER_ALL, [out[name][0] for name in BIG],
                                "small_gather_wait")
    reduced = _unpack(reduced[0].reshape(-1), [a.shape for a in local_g])
    loss = reduced[-1][0]
    mine = [a if axis is None else lax.dynamic_slice_in_dim(a, me * w[name].shape[axis], w[name].shape[axis], axis)
            for a, (name, _, axis) in zip(reduced, SMALL)]
    names = [name for name, _, _ in SMALL]
    as_2d = lambda a: a.reshape(-1, a.shape[-1])
    new = _adam_small([as_2d(a) for a in mine], *([as_2d(src[name]) for name in names] for src in (w, m, v)))
    for i, name in enumerate(names):
        out[name] = (mine[i],) + tuple(part[i].reshape(w[name].shape) for part in new)

    return (loss, grad_x[None]) + tuple(out[name][i] for i in range(4) for name in WEIGHTS)


def kernel(x, positions, ln_mix_g, ln_mix_b, ln_ffn_g, ln_ffn_b, even_w_in, pool_w, pool_scale, lru_conv_w, lru_conv_b, lru_w_a, lru_b_a, lru_w_x, lru_b_x, lru_lambda, even_w_out, mla_w_down, mla_q_norm_g, mla_kv_norm_g, mla_w_qb, mla_w_kvb, mla_w_o, mlp_w1, mlp_w2, loss_target, m_ln_mix_g, m_ln_mix_b, m_ln_ffn_g, m_ln_ffn_b, m_even_w_in, m_pool_w, m_pool_scale, m_lru_conv_w, m_lru_conv_b, m_lru_w_a, m_lru_b_a, m_lru_w_x, m_lru_b_x, m_lru_lambda, m_even_w_out, m_mla_w_down, m_mla_q_norm_g, m_mla_kv_norm_g, m_mla_w_qb, m_mla_w_kvb, m_mla_w_o, m_mlp_w1, m_mlp_w2, v_ln_mix_g, v_ln_mix_b, v_ln_ffn_g, v_ln_ffn_b, v_even_w_in, v_pool_w, v_pool_scale, v_lru_conv_w, v_lru_conv_b, v_lru_w_a, v_lru_b_a, v_lru_w_x, v_lru_b_x, v_lru_lambda, v_even_w_out, v_mla_w_down, v_mla_q_norm_g, v_mla_kv_norm_g, v_mla_w_qb, v_mla_w_kvb, v_mla_w_o, v_mlp_w1, v_mlp_w2):
    w = dict(zip(WEIGHTS, (ln_mix_g, ln_mix_b, ln_ffn_g, ln_ffn_b, even_w_in, pool_w, pool_scale, lru_conv_w,
                           lru_conv_b, lru_w_a, lru_b_a, lru_w_x, lru_b_x, lru_lambda, even_w_out, mla_w_down,
                           mla_q_norm_g, mla_kv_norm_g, mla_w_qb, mla_w_kvb, mla_w_o, mlp_w1, mlp_w2)))
    m = dict(zip(WEIGHTS, (m_ln_mix_g, m_ln_mix_b, m_ln_ffn_g, m_ln_ffn_b, m_even_w_in, m_pool_w, m_pool_scale,
                           m_lru_conv_w, m_lru_conv_b, m_lru_w_a, m_lru_b_a, m_lru_w_x, m_lru_b_x, m_lru_lambda,
                           m_even_w_out, m_mla_w_down, m_mla_q_norm_g, m_mla_kv_norm_g, m_mla_w_qb, m_mla_w_kvb,
                           m_mla_w_o, m_mlp_w1, m_mlp_w2)))
    v = dict(zip(WEIGHTS, (v_ln_mix_g, v_ln_mix_b, v_ln_ffn_g, v_ln_ffn_b, v_even_w_in, v_pool_w, v_pool_scale,
                           v_lru_conv_w, v_lru_conv_b, v_lru_w_a, v_lru_b_a, v_lru_w_x, v_lru_b_x, v_lru_lambda,
                           v_even_w_out, v_mla_w_down, v_mla_q_norm_g, v_mla_kv_norm_g, v_mla_w_qb, v_mla_w_kvb,
                           v_mla_w_o, v_mlp_w1, v_mlp_w2)))
    return _step(x, positions, loss_target, w, m, v)
```

```python
import jax
import jax.numpy as jnp
from jax import lax
from jax.experimental import pallas as pl
from jax.experimental.pallas import tpu as pltpu

F32 = jnp.float32
BF16 = jnp.bfloat16
S = jax.ShapeDtypeStruct

D = 1024
DEPTH = 4
N_DEV = 8
CHUNK_SHIFT = 6
POOL_WINDOWS = (2, 4, 8, 16)
POOL_W = 512
LRU_W = 1024
LRU_HEADS = 8
HEAD = 128
LRU_C = 8.0
EVEN_IN = 2560
EVEN_MIX = 1536
MLA_HEADS = 8
NOPE = 128
ROPE = 64
VDIM = 128
Q_RANK = 384
KV_RANK = 256
ODD_IN = 704
D_FF = 4096
FF_BLK = D_FF // N_DEV
ROPE_THETA = 10000.0
ALPHA = (2 * DEPTH) ** 0.25
LN_EPS = 1e-5
RMS_EPS = 1e-6
ATT_SCALE = (NOPE + ROPE) ** -0.5
NEG = float(jnp.finfo(jnp.float32).min)
ADAM_LR = 0.001
ADAM_B1 = 0.9
ADAM_B2 = 0.999
ADAM_EPS = 1e-08
ADAM_WD = 0.01
ADAM_STEP = 10
V7X_VMEM_BYTES = 64 * 1024 * 1024
VMEM_LIMIT = V7X_VMEM_BYTES - 8 * 1024 * 1024
MESH = pl.DeviceIdType.MESH


def _cp(*sem):
    return pltpu.CompilerParams(dimension_semantics=sem if sem else None, vmem_limit_bytes=VMEM_LIMIT)


def _dot(a, b):
    return jnp.dot(a, b, preferred_element_type=F32)


def _dot_nt(a, b):
    return lax.dot_general(a, b, (((1,), (1,)), ((), ())), preferred_element_type=F32)


def _dot_tn(a, b):
    return lax.dot_general(a, b, (((0,), (0,)), ((), ())), preferred_element_type=F32)


def _full(shape):
    return pl.BlockSpec(shape, lambda *_: (0,) * len(shape))


def _mm(a, b, *, mode, grid, a_spec, b_spec, out_shape, out_spec, name, add=None, add_spec=None, add_scale=1.0,
        dep=None):
    dot = {"nn": _dot, "nt": _dot_nt, "tn": _dot_tn}[mode]

    def body(*refs):
        a_ref, b_ref, o_ref = refs[0], refs[1], refs[-1]
        acc = dot(a_ref[...].astype(BF16), b_ref[...].astype(BF16))
        if add is not None:
            acc = acc + add_scale * refs[2][...]
        o_ref[...] = acc.astype(o_ref.dtype)

    ops = [a, b] if add is None else [a, b, add]
    specs = [a_spec, b_spec] if add is None else [a_spec, b_spec, add_spec]
    if dep is not None:
        ops.append(dep)
        specs.append(pl.BlockSpec(memory_space=pl.ANY))
    return pl.pallas_call(body, grid=grid, in_specs=specs, out_specs=out_spec, out_shape=out_shape,
                          compiler_params=_cp(*(("parallel",) * len(grid))), name=name)(*ops)


def _even_dwin(xb, dproj):
    shard = EVEN_IN // N_DEV

    def body(x_ref, dp_ref, o_ref):
        xv = x_ref[...].astype(BF16)
        for d in range(N_DEV):
            o_ref[d] = _dot_tn(xv, dp_ref[:, d * shard:(d + 1) * shard]).astype(BF16)

    return pl.pallas_call(body, out_shape=S((N_DEV, D, shard), BF16), compiler_params=_cp(), name="even_dwin")(xb, dproj)


def _ln_stats(z):
    mu = jnp.mean(z, axis=-1, keepdims=True)
    zc = z - mu
    var = jnp.mean(zc * zc, axis=-1, keepdims=True)
    rstd = lax.rsqrt(var + LN_EPS)
    return zc * rstd, rstd


def _row_tile(t):
    return min(1024, t)


def _resid_ln(x, mix, g3, b3, l, name):
    t = x.shape[0]
    bm = _row_tile(t)

    def body(x_ref, m_ref, g_ref, b_ref, z_ref, y_ref, yb_ref):
        z = ALPHA * x_ref[...] + m_ref[...]
        xh, _ = _ln_stats(z)
        y = xh * g_ref[...] + b_ref[...]
        z_ref[...] = z
        y_ref[...] = y
        yb_ref[...] = y.astype(BF16)

    row = pl.BlockSpec((bm, D), lambda i: (i, 0))
    vec = pl.BlockSpec((None, 1, D), lambda i: (l, 0, 0))
    return pl.pallas_call(body, grid=(t // bm,), in_specs=[row, row, vec, vec], out_specs=[row, row, row],
                          out_shape=[S((t, D), F32), S((t, D), F32), S((t, D), BF16)],
                          compiler_params=_cp("parallel"), name=name)(x, mix, g3, b3)


def _proj_resid_ln(x, a, wmat, g3, b3, l, name):
    t, k = a.shape
    bm = _row_tile(t)

    def body(x_ref, a_ref, w_ref, g_ref, b_ref, z_ref, y_ref, yb_ref):
        z = ALPHA * x_ref[...] + _dot(a_ref[...], w_ref[...])
        xh, _ = _ln_stats(z)
        y = xh * g_ref[...] + b_ref[...]
        z_ref[...] = z
        y_ref[...] = y
        yb_ref[...] = y.astype(BF16)

    row = pl.BlockSpec((bm, D), lambda i: (i, 0))
    vec = pl.BlockSpec((None, 1, D), lambda i: (l, 0, 0))
    return pl.pallas_call(body, grid=(t // bm,),
                          in_specs=[row, pl.BlockSpec((bm, k), lambda i: (i, 0)), _full((k, D)), vec, vec],
                          out_specs=[row, row, row], out_shape=[S((t, D), F32), S((t, D), F32), S((t, D), BF16)],
                          compiler_params=_cp("parallel"), name=name)(x, a, wmat, g3, b3)


def _ln_bwd(d, z, g3, l, name, r=None, dep=None):
    t = z.shape[0]
    bm = _row_tile(t)

    def body(*refs):
        refs = list(refs)
        d_ref = refs.pop(0)
        dy = d_ref[...]
        if r is not None:
            dy = dy + ALPHA * refs.pop(0)[...]
        z_ref, g_ref = refs.pop(0), refs.pop(0)
        if dep is not None:
            refs.pop(0)
        dz_ref, dzb_ref, dg_ref, db_ref = refs
        xh, rstd = _ln_stats(z_ref[...])
        dyg = dy * g_ref[...]
        m1 = jnp.mean(dyg, axis=-1, keepdims=True)
        m2 = jnp.mean(dyg * xh, axis=-1, keepdims=True)
        dz = rstd * (dyg - m1 - xh * m2)
        dz_ref[...] = dz
        dzb_ref[...] = dz.astype(BF16)

        @pl.when(pl.program_id(0) == 0)
        def _():
            dg_ref[...] = jnp.zeros_like(dg_ref)
            db_ref[...] = jnp.zeros_like(db_ref)

        dg_ref[...] += jnp.sum(dy * xh, axis=0, keepdims=True)
        db_ref[...] += jnp.sum(dy, axis=0, keepdims=True)

    row = pl.BlockSpec((bm, D), lambda i: (i, 0))
    vec = pl.BlockSpec((None, 1, D), lambda i: (l, 0, 0))
    acc = pl.BlockSpec((1, D), lambda i: (0, 0))
    ops = [d, z, g3] if r is None else [d, r, z, g3]
    specs = [row, row, vec] if r is None else [row, row, row, vec]
    if dep is not None:
        ops.append(dep)
        specs.append(_full(dep.shape))
    return pl.pallas_call(body, grid=(t // bm,), in_specs=specs, out_specs=[row, row, acc, acc],
                          out_shape=[S((t, D), F32), S((t, D), BF16), S((1, D), F32), S((1, D), F32)],
                          compiler_params=_cp("arbitrary"), name=name)(*ops)


def _loss_grad(y, tgt):
    t = y.shape[0]
    bm = _row_tile(t)

    def body(y_ref, t_ref, dy_ref, loss_ref, acc_ref):
        i = pl.program_id(0)
        e = y_ref[...] - t_ref[...]
        dy_ref[...] = e * (1.0 / D)

        @pl.when(i == 0)
        def _():
            acc_ref[...] = jnp.zeros_like(acc_ref)

        acc_ref[...] += jnp.sum(e * e, axis=0, keepdims=True)

        @pl.when(i == pl.num_programs(0) - 1)
        def _():
            loss_ref[...] = jnp.full(loss_ref.shape, (0.5 / D) * jnp.sum(acc_ref[...]), F32)

    row = pl.BlockSpec((bm, D), lambda i: (i, 0))
    return pl.pallas_call(body, grid=(t // bm,), in_specs=[row, row],
                          out_specs=[row, pl.BlockSpec((1, 128), lambda i: (0, 0))],
                          out_shape=[S((t, D), F32), S((1, 128), F32)],
                          scratch_shapes=[pltpu.VMEM((1, D), F32)],
                          compiler_params=_cp("arbitrary"), name="loss_grad")(y, tgt)


def _mlp_row_tile(t):
    return min(1024, t)


MLP_ROW_PARTS = 2


def _row_parts(bm):
    step = bm // MLP_ROW_PARTS
    return [slice(k * step, (k + 1) * step) for k in range(MLP_ROW_PARTS)]


def _mlp_fwd(y, yb, w1g, w2g, g3, b3, l, dep=None):
    t = yb.shape[0]
    bm = _mlp_row_tile(t)

    def body(*refs):
        y_ref, yb_ref, w1_ref, w2_ref, g_ref, b_ref = refs[:6]
        z_ref, o_ref, ob_ref, act_ref, acc_ref = refs[-5:]
        j = pl.program_id(1)

        @pl.when(j == 0)
        def _():
            acc_ref[...] = jnp.zeros_like(acc_ref)

        for rows in _row_parts(bm):
            h = jnp.maximum(_dot(yb_ref[rows, :], w1_ref[...]), 0.0)
            act = (h * h).astype(BF16)
            act_ref[rows, :] = act
            acc_ref[rows, :] += _dot(act, w2_ref[...])

        @pl.when(j == N_DEV - 1)
        def _():
            z = ALPHA * y_ref[...] + acc_ref[...]
            xh, _ = _ln_stats(z)
            out = xh * g_ref[...] + b_ref[...]
            z_ref[...] = z
            o_ref[...] = out
            ob_ref[...] = out.astype(BF16)

    row = pl.BlockSpec((bm, D), lambda i, j: (i, 0))
    vec = pl.BlockSpec((None, 1, D), lambda i, j: (l, 0, 0))
    deps = [] if dep is None else [dep]
    return pl.pallas_call(
        body, grid=(t // bm, N_DEV),
        in_specs=[row, row, pl.BlockSpec((None, D, FF_BLK), lambda i, j: (j, 0, 0)),
                  pl.BlockSpec((None, FF_BLK, D), lambda i, j: (j, 0, 0)), vec, vec] + [ANY] * len(deps),
        out_specs=[row, row, row, pl.BlockSpec((bm, FF_BLK), lambda i, j: (i, j))],
        out_shape=[S((t, D), F32), S((t, D), F32), S((t, D), BF16), S((t, D_FF), BF16)],
        scratch_shapes=[pltpu.VMEM((bm, D), F32)],
        compiler_params=_cp("parallel", "arbitrary"), name="mlp_fwd")(y, yb, w1g, w2g, g3, b3, *deps)


def _mlp_bwd_dh(act, dzb, w1g, w2g):
    t = act.shape[0]
    bm = _mlp_row_tile(t)

    def body(a_ref, dz_ref, w1_ref, w2_ref, dh_ref, acc_ref):
        @pl.when(pl.program_id(1) == 0)
        def _():
            acc_ref[...] = jnp.zeros_like(acc_ref)

        for rows in _row_parts(bm):
            r = jnp.sqrt(a_ref[rows, :].astype(F32))
            dh = (_dot_nt(dz_ref[rows, :], w2_ref[...]) * (2.0 * r)).astype(BF16)
            dh_ref[rows, :] = dh
            acc_ref[rows, :] += _dot_nt(dh, w1_ref[...])

    row = pl.BlockSpec((bm, D), lambda i, j: (i, 0))
    hid = pl.BlockSpec((bm, FF_BLK), lambda i, j: (i, j))
    return pl.pallas_call(
        body, grid=(t // bm, N_DEV),
        in_specs=[hid, row,
                  pl.BlockSpec((None, D, FF_BLK), lambda i, j: (j, 0, 0)),
                  pl.BlockSpec((None, FF_BLK, D), lambda i, j: (j, 0, 0))],
        out_specs=[hid, row],
        out_shape=[S((t, D_FF), BF16), S((t, D), F32)],
        compiler_params=_cp("parallel", "arbitrary"), name="mlp_bwd_dh")(act, dzb, w1g, w2g)


F32_SUBLANES = 8


def _shift_dn(x, k, rows, fill=0.0):
    if k % F32_SUBLANES == 0:
        return jnp.concatenate([jnp.full((k,) + x.shape[1:], fill, x.dtype), x[:x.shape[0] - k]], axis=0)
    return jnp.where(rows >= k, pltpu.roll(x, k, 0), fill)


def _shift_up(x, k, rows, fill=0.0):
    t = x.shape[0]
    if k % F32_SUBLANES == 0:
        return jnp.concatenate([x[k:], jnp.full((k,) + x.shape[1:], fill, x.dtype)], axis=0)
    return jnp.where(rows < t - k, pltpu.roll(x, t - k, 0), fill)


def _scan_rows(a, b, shift):
    rows = lax.broadcasted_iota(jnp.int32, a.shape, 0)
    k = 1
    t = a.shape[0]
    while k < t:
        b = a * shift(b, k, rows) + b
        if 2 * k < t:
            a = a * shift(a, k, rows, 1.0)
        k *= 2
    return b


def _scan_dn(a, b):
    return _scan_rows(a, b, _shift_dn)


def _scan_up(a, b):
    return _scan_rows(a, b, _shift_up)


def _window_sum_dn(x, w, rows):
    k = 1
    while k < w:
        x = x + _shift_dn(x, k, rows)
        k *= 2
    return x


def _window_sum_up(x, w, rows):
    k = 1
    while k < w:
        x = x + _shift_up(x, k, rows)
        k *= 2
    return x


def _pool_diff(u, w, rows):
    inv_count = 1.0 / jnp.minimum(rows + 1, w).astype(F32)
    return _window_sum_dn(u, w, rows) * inv_count - u, inv_count


def _pool_fwd(proj, pool_w, pool_scale3, j):
    t = proj.shape[0]

    def body(u_ref, w_ref, s_ref, y_ref):
        rows = lax.broadcasted_iota(jnp.int32, (t, HEAD), 0)
        for g, w in enumerate(POOL_WINDOWS):
            cols = slice(g * HEAD, (g + 1) * HEAD)
            d, _ = _pool_diff(u_ref[:, cols], w, rows)
            y = _dot(d.astype(BF16), w_ref[g].astype(BF16)) * s_ref[:, cols]
            y_ref[:, cols] = y.astype(BF16)

    return pl.pallas_call(
        body, grid=(1,),
        in_specs=[pl.BlockSpec((t, POOL_W), lambda i: (0, 0)),
                  pl.BlockSpec((None, 4, HEAD, HEAD), lambda i: (j, 0, 0, 0)),
                  pl.BlockSpec((None, 1, POOL_W), lambda i: (j, 0, 0))],
        out_specs=pl.BlockSpec((t, POOL_W), lambda i: (0, 0)),
        out_shape=S((t, POOL_W), BF16), compiler_params=_cp("arbitrary"), name="pool_fwd")(proj, pool_w, pool_scale3)


def _pool_bwd(proj, dycat, pool_w, pool_scale3, j):
    t = proj.shape[0]

    def body(u_ref, dy_ref, w_ref, s_ref, du_ref, dw_ref, ds_ref):
        rows = lax.broadcasted_iota(jnp.int32, (t, HEAD), 0)
        for g, w in enumerate(POOL_WINDOWS):
            cols = slice(g * HEAD, (g + 1) * HEAD)
            d, inv_count = _pool_diff(u_ref[:, cols], w, rows)
            db = d.astype(BF16)
            wg = w_ref[g].astype(BF16)
            dy = dy_ref[:, cols]
            ds_ref[:, cols] = jnp.sum(dy * _dot(db, wg), axis=0, keepdims=True)
            dzz = (dy * s_ref[:, cols]).astype(BF16)
            dw_ref[g] = _dot_tn(db, dzz)
            dd = _dot_nt(dzz, wg)
            du_ref[:, cols] = (_window_sum_up(dd * inv_count, w, rows) - dd).astype(BF16)

    return pl.pallas_call(
        body, grid=(1,),
        in_specs=[pl.BlockSpec((t, POOL_W), lambda i: (0, 0)),
                  pl.BlockSpec((t, POOL_W), lambda i: (0, 0)),
                  pl.BlockSpec((None, 4, HEAD, HEAD), lambda i: (j, 0, 0, 0)),
                  pl.BlockSpec((None, 1, POOL_W), lambda i: (j, 0, 0))],
        out_specs=[pl.BlockSpec((t, POOL_W), lambda i: (0, 0)), _full((4, HEAD, HEAD)), _full((1, POOL_W))],
        out_shape=[S((t, POOL_W), BF16), S((4, HEAD, HEAD), F32), S((1, POOL_W), F32)],
        compiler_params=_cp("arbitrary"), name="pool_bwd")(proj, dycat, pool_w, pool_scale3)


GELU_C = 0.7978845608028654
GELU_K = 0.044715


def _gelu(x):
    th = jnp.tanh(GELU_C * (x + GELU_K * x * x * x))
    return 0.5 * x * (1.0 + th), th


def _lru_forward(u, gate, cw, cb, wa, ba, wx, bx, lam, rows, h=None):
    v = cw[3:4] * u + cw[2:3] * _shift_dn(u, 1, rows) + cw[1:2] * _shift_dn(u, 2, rows) \
        + cw[0:1] * _shift_dn(u, 3, rows) + cb
    vb = v.astype(BF16)
    r = jax.nn.sigmoid(_dot(vb, wa) + ba)
    i = jax.nn.sigmoid(_dot(vb, wx) + bx)
    sp = jnp.maximum(-lam, 0.0) + jnp.log1p(jnp.exp(-jnp.abs(lam)))
    log_a = (-LRU_C) * r * sp
    a = jnp.exp(log_a)
    one_m_a2 = -jnp.tanh(log_a) * (a * a + 1.0)
    mult = jnp.sqrt(one_m_a2)
    if h is None:
        h = _scan_dn(a, mult * (i * v))
    gl, th = _gelu(gate)
    return dict(v=v, vb=vb, r=r, i=i, sp=sp, a=a, mult=mult, h=h, gl=gl, th=th)


def _lru_specs(t, j, col0_u, col0_g):
    blk = lambda c0: pl.BlockSpec((t, HEAD), lambda h: (0, c0 + h))
    vec = pl.BlockSpec((None, 1, HEAD), lambda h: (j, 0, h))
    return [blk(col0_u), blk(col0_g),
            pl.BlockSpec((None, 4, HEAD), lambda h: (j, 0, h)), vec,
            pl.BlockSpec((None, None, HEAD, HEAD), lambda h: (j, h, 0, 0)), vec,
            pl.BlockSpec((None, None, HEAD, HEAD), lambda h: (j, h, 0, 0)), vec, vec]


def _lru_fwd(proj, p, j):
    t = proj.shape[0]

    def body(u_ref, g_ref, cw_ref, cb_ref, wa_ref, ba_ref, wx_ref, bx_ref, lam_ref, y_ref, h_ref):
        rows = lax.broadcasted_iota(jnp.int32, (t, HEAD), 0)
        f = _lru_forward(u_ref[...], g_ref[...], cw_ref[...], cb_ref[...], wa_ref[...].astype(BF16), ba_ref[...],
                         wx_ref[...].astype(BF16), bx_ref[...], lam_ref[...], rows)
        y_ref[...] = (f["h"] * f["gl"]).astype(BF16)
        h_ref[...] = f["h"]

    blk = pl.BlockSpec((t, HEAD), lambda h: (0, h))
    return pl.pallas_call(
        body, grid=(LRU_HEADS,), in_specs=_lru_specs(t, j, POOL_W // HEAD, (POOL_W + LRU_W) // HEAD),
        out_specs=[blk, blk], out_shape=[S((t, LRU_W), BF16), S((t, LRU_W), F32)],
        compiler_params=_cp("parallel"), name="lru_fwd")(
            proj, proj, p["conv_w"], p["conv_b"], p["w_a"], p["b_a"], p["w_x"], p["b_x"], p["lam"])


def _lru_bwd(proj, dycat, hstate, p, j):
    t = proj.shape[0]

    def body(u_ref, g_ref, cw_ref, cb_ref, wa_ref, ba_ref, wx_ref, bx_ref, lam_ref, dy_ref, h_ref,
             du_ref, dgate_ref, dcw_ref, dcb_ref, dwa_ref, dba_ref, dwx_ref, dbx_ref, dlam_ref):
        rows = lax.broadcasted_iota(jnp.int32, (t, HEAD), 0)
        u = u_ref[...]
        gate = g_ref[...]
        cw = cw_ref[...]
        wa = wa_ref[...].astype(BF16)
        wx = wx_ref[...].astype(BF16)
        lam = lam_ref[...]
        f = _lru_forward(u, gate, cw, cb_ref[...], wa, ba_ref[...], wx, bx_ref[...], lam, rows, h=h_ref[...])
        v, r, i, a, mult, h, th = f["v"], f["r"], f["i"], f["a"], f["mult"], f["h"], f["th"]
        dy = dy_ref[...]
        dgl = 0.5 * (1.0 + th) + 0.5 * gate * (1.0 - th * th) * GELU_C * (1.0 + 3.0 * GELU_K * gate * gate)
        dgate_ref[...] = (dy * h * dgl).astype(BF16)
        g = _scan_up(_shift_up(a, 1, rows), dy * f["gl"])
        da = g * _shift_dn(h, 1, rows)
        iv = i * v
        dmult = g * iv
        di = g * mult * v
        dv = g * mult * i
        dlog_a = da * a - dmult * (a * a) / mult
        dr = dlog_a * (-LRU_C) * f["sp"]
        dsp = jnp.sum(dlog_a * (-LRU_C) * r, axis=0, keepdims=True)
        dlam_ref[...] = -dsp * jax.nn.sigmoid(-lam)
        dpa = dr * r * (1.0 - r)
        dpx = di * i * (1.0 - i)
        dpab = dpa.astype(BF16)
        dpxb = dpx.astype(BF16)
        dwa_ref[...] = _dot_tn(f["vb"], dpab)
        dwx_ref[...] = _dot_tn(f["vb"], dpxb)
        dba_ref[...] = jnp.sum(dpa, axis=0, keepdims=True)
        dbx_ref[...] = jnp.sum(dpx, axis=0, keepdims=True)
        dv = dv + _dot_nt(dpab, wa) + _dot_nt(dpxb, wx)
        dcb_ref[...] = jnp.sum(dv, axis=0, keepdims=True)
        du = cw[3:4] * dv
        dcw_ref[3:4, :] = jnp.sum(dv * u, axis=0, keepdims=True)
        for k in (1, 2, 3):
            du = du + cw[3 - k:4 - k] * _shift_up(dv, k, rows)
            dcw_ref[3 - k:4 - k, :] = jnp.sum(dv * _shift_dn(u, k, rows), axis=0, keepdims=True)
        du_ref[...] = du.astype(BF16)

    blk = pl.BlockSpec((t, HEAD), lambda h: (0, h))
    vec = pl.BlockSpec((1, HEAD), lambda h: (0, h))
    mat = pl.BlockSpec((None, HEAD, HEAD), lambda h: (h, 0, 0))
    return pl.pallas_call(
        body, grid=(LRU_HEADS,),
        in_specs=_lru_specs(t, j, POOL_W // HEAD, (POOL_W + LRU_W) // HEAD)
        + [pl.BlockSpec((t, HEAD), lambda h: (0, POOL_W // HEAD + h)), blk],
        out_specs=[blk, blk, pl.BlockSpec((4, HEAD), lambda h: (0, h)), vec, mat, vec, mat, vec, vec],
        out_shape=[S((t, LRU_W), BF16), S((t, LRU_W), BF16), S((4, LRU_W), F32), S((1, LRU_W), F32),
                   S((LRU_HEADS, HEAD, HEAD), F32), S((1, LRU_W), F32),
                   S((LRU_HEADS, HEAD, HEAD), F32), S((1, LRU_W), F32), S((1, LRU_W), F32)],
        compiler_params=_cp("parallel"), name="lru_bwd")(
            proj, proj, p["conv_w"], p["conv_b"], p["w_a"], p["b_a"], p["w_x"], p["b_x"], p["lam"], dycat, hstate)


def _rope(x, c, s):
    x1 = x[:, :ROPE // 2]
    x2 = x[:, ROPE // 2:]
    return jnp.concatenate([x1 * c - x2 * s, x1 * s + x2 * c], axis=-1)


def _rope_t(d, c, s):
    d1 = d[:, :ROPE // 2]
    d2 = d[:, ROPE // 2:]
    return jnp.concatenate([d1 * c + d2 * s, d2 * c - d1 * s], axis=-1)


def _rope_tables(pos2, inv_freq):
    t = pos2.shape[0]

    def body(p_ref, f_ref, c_ref, s_ref):
        ang = p_ref[...].astype(F32) * f_ref[...]
        c_ref[...] = jnp.cos(ang)
        s_ref[...] = jnp.sin(ang)

    return pl.pallas_call(body, out_shape=[S((t, ROPE // 2), F32), S((t, ROPE // 2), F32)],
                          name="rope_tables")(pos2, inv_freq)


def _down_norm(xb, wdown_g, gq3, gkv3, cos, sin, j):
    t = xb.shape[0]
    bm = _row_tile(t)

    def body(x_ref, w_ref, gq_ref, gkv_ref, c_ref, s_ref, down_ref, cq_ref, ckv_ref, kpe_ref):
        w = w_ref[...].reshape(D, ODD_IN)
        down = _dot(x_ref[...], w)
        down_ref[...] = down
        q = down[:, :Q_RANK]
        cq_ref[...] = (q * lax.rsqrt(jnp.mean(q * q, axis=-1, keepdims=True) + RMS_EPS) * gq_ref[...]).astype(BF16)
        kv = down[:, Q_RANK:Q_RANK + KV_RANK]
        ckv_ref[...] = (kv * lax.rsqrt(jnp.mean(kv * kv, axis=-1, keepdims=True) + RMS_EPS)
                        * gkv_ref[...]).astype(BF16)
        kpe_ref[...] = _rope(down[:, Q_RANK + KV_RANK:], c_ref[...], s_ref[...])

    row = lambda n: pl.BlockSpec((bm, n), lambda i: (i, 0))
    return pl.pallas_call(
        body, grid=(t // bm,),
        in_specs=[row(D), _full((N_DEV, D // N_DEV, ODD_IN)),
                  pl.BlockSpec((None, 1, Q_RANK), lambda i: (j, 0, 0)),
                  pl.BlockSpec((None, 1, KV_RANK), lambda i: (j, 0, 0)), row(ROPE // 2), row(ROPE // 2)],
        out_specs=[row(ODD_IN), row(Q_RANK), row(KV_RANK), row(ROPE)],
        out_shape=[S((t, ODD_IN), F32), S((t, Q_RANK), BF16), S((t, KV_RANK), BF16), S((t, ROPE), F32)],
        compiler_params=_cp("parallel"), name="down_norm")(xb, wdown_g, gq3, gkv3, cos, sin)


def _q_tile(t, widest):
    return min(widest, t // 2)


def _attn_probs(q, k, qs):
    s = _dot_nt(q, k) * ATT_SCALE
    tq = q.shape[0]
    rows = lax.broadcasted_iota(jnp.int32, (tq, tq), 0)
    cols = lax.broadcasted_iota(jnp.int32, (tq, tq), 1)
    last = jnp.where(jnp.right_shift(cols, CHUNK_SHIFT) <= jnp.right_shift(rows, CHUNK_SHIFT), s[:, qs:], NEG)
    s = last if qs == 0 else jnp.concatenate([s[:, :qs], last], axis=1)
    e = jnp.exp(s - jnp.max(s, axis=-1, keepdims=True))
    return e / jnp.sum(e, axis=-1, keepdims=True)


def _head_qkv(cq, ckv, kpe, c, s, wq_ref, wkv_ref):
    q = jnp.concatenate([_dot(cq, wq_ref[:, :NOPE]), _rope(_dot(cq, wq_ref[:, NOPE:]), c, s)], axis=1).astype(BF16)
    k = jnp.concatenate([_dot(ckv, wkv_ref[:, :NOPE]), kpe], axis=1).astype(BF16)
    vv = _dot(ckv, wkv_ref[:, NOPE:]).astype(BF16)
    return q, k, vv


def _attn_in_specs(t):
    return [_full((t, Q_RANK)), _full((t, KV_RANK)), _full((t, ROPE)), _full((t, ROPE // 2)), _full((t, ROPE // 2)),
            pl.BlockSpec((None, Q_RANK, NOPE + ROPE), lambda h: (h, 0, 0)),
            pl.BlockSpec((None, KV_RANK, NOPE + VDIM), lambda h: (h, 0, 0)),
            pl.BlockSpec((None, VDIM, D), lambda h: (h, 0, 0))]


def _attn_fwd(cq, ckv, kpe, cos, sin, wqb_g, wkvb_g, wo_g):
    t = cq.shape[0]
    tq = _q_tile(t, 256)

    def body(cq_ref, ckv_ref, kpe_ref, c_ref, s_ref, wq_ref, wkv_ref, wo_ref, o_ref, mix_ref):
        q, k, vv = _head_qkv(cq_ref[...], ckv_ref[...], kpe_ref[...], c_ref[...], s_ref[...], wq_ref, wkv_ref)
        for qs in range(0, t, tq):
            ke = qs + tq
            p = _attn_probs(q[qs:ke], k[:ke], qs)
            o_ref[qs:ke, :] = _dot(p.astype(BF16), vv[:ke]).astype(BF16)
        c = _dot(o_ref[...], wo_ref[...])

        @pl.when(pl.program_id(0) == 0)
        def _():
            mix_ref[...] = c

        @pl.when(pl.program_id(0) > 0)
        def _():
            mix_ref[...] += c

    return pl.pallas_call(
        body, grid=(MLA_HEADS,), in_specs=_attn_in_specs(t),
        out_specs=[pl.BlockSpec((None, t, VDIM), lambda h: (h, 0, 0)), _full((t, D))],
        out_shape=[S((MLA_HEADS, t, VDIM), BF16), S((t, D), F32)],
        compiler_params=_cp("arbitrary"), name="attn_fwd")(cq, ckv, kpe, cos, sin, wqb_g, wkvb_g, wo_g)


def _attn_bwd(cq, ckv, kpe, cos, sin, wqb_g, wkvb_g, wo_g, o, dzb):
    t = cq.shape[0]
    tq = _q_tile(t, 512)

    def body(cq_ref, ckv_ref, kpe_ref, c_ref, s_ref, wq_ref, wkv_ref, wo_ref, o_ref, dz_ref,
             dwo_ref, dwq_ref, dwkv_ref, dcq_ref, dckv_ref, dkpe_ref, dkt_s, dvt_s, dq_s):
        cqv = cq_ref[...]
        ckvv = ckv_ref[...]
        c = c_ref[...]
        s = s_ref[...]
        q, k, vv = _head_qkv(cqv, ckvv, kpe_ref[...], c, s, wq_ref, wkv_ref)
        dzv = dz_ref[...]
        dwo_ref[...] = _dot_tn(o_ref[...], dzv).astype(BF16)
        do = _dot_nt(dzv, wo_ref[...]).astype(BF16)
        dkt_s[...] = jnp.zeros_like(dkt_s)
        dvt_s[...] = jnp.zeros_like(dvt_s)
        for qs in range(0, t, tq):
            ke = qs + tq
            p = _attn_probs(q[qs:ke], k[:ke], qs)
            dp = _dot_nt(do[qs:ke], vv[:ke])
            ds = (p * (dp - jnp.sum(p * dp, axis=-1, keepdims=True)) * ATT_SCALE).astype(BF16)
            dq_s[qs:ke, :] = _dot(ds, k[:ke])
            dkt_s[0:NOPE + ROPE, 0:ke] += _dot_tn(q[qs:ke], ds)
            dvt_s[:, 0:ke] += _dot_tn(do[qs:ke], p.astype(BF16))
        dk = dkt_s[...].T
        dqn = dq_s[:, :NOPE].astype(BF16)
        dqp = _rope_t(dq_s[:, NOPE:], c, s).astype(BF16)
        dkn = dk[:, :NOPE].astype(BF16)
        dkp = dk[:, NOPE:NOPE + ROPE]
        dvv = dvt_s[...].T.astype(BF16)
        dwq_ref[:, :NOPE] = _dot_tn(cqv, dqn).astype(BF16)
        dwq_ref[:, NOPE:] = _dot_tn(cqv, dqp).astype(BF16)
        dwkv_ref[:, :NOPE] = _dot_tn(ckvv, dkn).astype(BF16)
        dwkv_ref[:, NOPE:] = _dot_tn(ckvv, dvv).astype(BF16)
        dcq = _dot_nt(dqn, wq_ref[:, :NOPE]) + _dot_nt(dqp, wq_ref[:, NOPE:])
        dckv = _dot_nt(dkn, wkv_ref[:, :NOPE]) + _dot_nt(dvv, wkv_ref[:, NOPE:])

        @pl.when(pl.program_id(0) == 0)
        def _():
            dcq_ref[...] = dcq
            dckv_ref[...] = dckv
            dkpe_ref[...] = dkp

        @pl.when(pl.program_id(0) > 0)
        def _():
            dcq_ref[...] += dcq
            dckv_ref[...] += dckv
            dkpe_ref[...] += dkp

    per_head = lambda a, b: pl.BlockSpec((None, a, b), lambda h: (h, 0, 0))
    return pl.pallas_call(
        body, grid=(MLA_HEADS,),
        in_specs=_attn_in_specs(t) + [per_head(t, VDIM), _full((t, D))],
        out_specs=[per_head(VDIM, D), per_head(Q_RANK, NOPE + ROPE), per_head(KV_RANK, NOPE + VDIM),
                   _full((t, Q_RANK)), _full((t, KV_RANK)), _full((t, ROPE))],
        out_shape=[S((MLA_HEADS, VDIM, D), BF16), S((MLA_HEADS, Q_RANK, NOPE + ROPE), BF16),
                   S((MLA_HEADS, KV_RANK, NOPE + VDIM), BF16),
                   S((t, Q_RANK), F32), S((t, KV_RANK), F32), S((t, ROPE), F32)],
        scratch_shapes=[pltpu.VMEM((2 * NOPE, t), F32), pltpu.VMEM((VDIM, t), F32),
                        pltpu.VMEM((t, NOPE + ROPE), F32)],
        compiler_params=_cp("arbitrary"), name="attn_bwd")(cq, ckv, kpe, cos, sin, wqb_g, wkvb_g, wo_g, o, dzb)


def _rms_bwd(down, dcq, dckv, dkpe, cos, sin, gq3, gkv3, j):
    t = down.shape[0]
    bm = _row_tile(t)

    def body(down_ref, dcq_ref, dckv_ref, dkpe_ref, c_ref, s_ref, gq_ref, gkv_ref, dd_ref, dgq_ref, dgkv_ref):
        @pl.when(pl.program_id(0) == 0)
        def _():
            dgq_ref[...] = jnp.zeros_like(dgq_ref)
            dgkv_ref[...] = jnp.zeros_like(dgkv_ref)

        def rms_b(x, dy, g):
            rstd = lax.rsqrt(jnp.mean(x * x, axis=-1, keepdims=True) + RMS_EPS)
            xh = x * rstd
            dyg = dy * g
            return rstd * (dyg - xh * jnp.mean(dyg * xh, axis=-1, keepdims=True)), jnp.sum(dy * xh, axis=0, keepdims=True)

        dq, dgq = rms_b(down_ref[:, :Q_RANK], dcq_ref[...], gq_ref[...])
        dkv, dgkv = rms_b(down_ref[:, Q_RANK:Q_RANK + KV_RANK], dckv_ref[...], gkv_ref[...])
        dgq_ref[...] += dgq
        dgkv_ref[...] += dgkv
        dd_ref[:, :Q_RANK] = dq.astype(BF16)
        dd_ref[:, Q_RANK:Q_RANK + KV_RANK] = dkv.astype(BF16)
        dd_ref[:, Q_RANK + KV_RANK:] = _rope_t(dkpe_ref[...], c_ref[...], s_ref[...]).astype(BF16)

    row = lambda n: pl.BlockSpec((bm, n), lambda i: (i, 0))
    return pl.pallas_call(
        body, grid=(t // bm,),
        in_specs=[row(ODD_IN), row(Q_RANK), row(KV_RANK), row(ROPE), row(ROPE // 2), row(ROPE // 2),
                  pl.BlockSpec((None, 1, Q_RANK), lambda i: (j, 0, 0)),
                  pl.BlockSpec((None, 1, KV_RANK), lambda i: (j, 0, 0))],
        out_specs=[row(ODD_IN), _full((1, Q_RANK)), _full((1, KV_RANK))],
        out_shape=[S((t, ODD_IN), BF16), S((1, Q_RANK), F32), S((1, KV_RANK), F32)],
        compiler_params=_cp("arbitrary"), name="rms_bwd")(down, dcq, dckv, dkpe, cos, sin, gq3, gkv3)


def _col_blocks(t, n, bn):
    return pl.BlockSpec((t, bn), lambda i: (0, i))


def _row_blocks(n, bm):
    return pl.BlockSpec((bm, n), lambda i: (i, 0))


def _local_step(x, pos2, tgt, small, weights_of, grads_done, start_dep=None, prefetch=None):
    t = x.shape[0]
    bm = min(512, t)
    inv_freq = (ROPE_THETA ** (-jnp.arange(0, ROPE, 2, dtype=F32) / ROPE)).reshape(1, ROPE // 2)
    cos, sin = _rope_tables(pos2, inv_freq)
    lru_p = {k: small[k] for k in ("conv_w", "conv_b", "w_a", "b_a", "w_x", "b_x", "lam")}

    saved = []
    y, yb = x, x
    for l in range(DEPTH):
        j = l // 2
        big = weights_of(l, 0, y)
        sv = dict(xb=yb, big=big)
        if l % 2 == 0:
            proj = _mm(yb, big["win_t"], mode="nt", grid=(EVEN_IN // 512,), a_spec=_full((t, D)),
                       b_spec=_row_blocks(D, 512), out_shape=S((t, EVEN_IN), F32),
                       out_spec=_col_blocks(t, EVEN_IN, 512), name="even_proj", dep=start_dep if l == 0 else None)
            y_lru, hstate = _lru_fwd(proj, lru_p, j)
            ycat = jnp.concatenate([_pool_fwd(proj, small["pool_w"], small["pool_scale"], j), y_lru], axis=1)
            big.update(weights_of(l, 1, ycat))
            z1, y1, y1b = _proj_resid_ln(y, ycat, big["wout2d"], small["ln_mix_g"], small["ln_mix_b"], l, "even_out")
            sv.update(proj=proj, ycat=ycat, hstate=hstate)
        else:
            down, cq, ckv, kpe = _down_norm(yb, big["wdown"], small["gq"], small["gkv"], cos, sin, j)
            o, mix = _attn_fwd(cq, ckv, kpe, cos, sin, big["wqb"], big["wkvb"], big["wo"])
            z1, y1, y1b = _resid_ln(y, mix, small["ln_mix_g"], small["ln_mix_b"], l, "resid_ln")
            sv.update(down=down, cq=cq, ckv=ckv, kpe=kpe, o=o)
        fetched = prefetch(l + 1, y1) if prefetch is not None and l + 1 < DEPTH else None
        z2, y, yb, act = _mlp_fwd(y1, y1b, big["w1"], big["w2"], small["ln_ffn_g"], small["ln_ffn_b"], l,
                                  dep=fetched)
        sv.update(z1=z1, y1b=y1b, z2=z2, act=act)
        saved.append(sv)

    dy, loss_tile = _loss_grad(y, tgt)

    g = {k: [None] * n for k, n in (("ln_mix_g", 4), ("ln_mix_b", 4), ("ln_ffn_g", 4), ("ln_ffn_b", 4),
                                    ("pool_w", 2), ("pool_scale", 2), ("conv_w", 2), ("conv_b", 2),
                                    ("w_a", 2), ("b_a", 2), ("w_x", 2), ("b_x", 2), ("lam", 2),
                                    ("gq", 2), ("gkv", 2))}
    dep = None
    for l in reversed(range(DEPTH)):
        j = l // 2
        sv = saved[l]
        big = sv["big"]
        dz2, dz2b, g["ln_ffn_g"][l], g["ln_ffn_b"][l] = _ln_bwd(dy, sv["z2"], small["ln_ffn_g"], l, "ln_bwd", dep=dep)
        act = sv["act"]
        dh, dff = _mlp_bwd_dh(act, dz2b, big["w1"], big["w2"])
        dw1 = _mm(sv["y1b"], dh, mode="tn", grid=(N_DEV,), a_spec=_full((t, D)),
                  b_spec=_col_blocks(t, D_FF, FF_BLK), out_shape=S((N_DEV, D, FF_BLK), BF16),
                  out_spec=pl.BlockSpec((None, D, FF_BLK), lambda i: (i, 0, 0)), name="mlp_dw1")
        dw2 = _mm(act, dz2b, mode="tn", grid=(N_DEV,), a_spec=_col_blocks(t, D_FF, FF_BLK),
                  b_spec=_full((t, D)), out_shape=S((N_DEV, FF_BLK, D), BF16),
                  out_spec=pl.BlockSpec((None, FF_BLK, D), lambda i: (i, 0, 0)), name="mlp_dw2")
        dep = grads_done(l, dict(w1=dw1, w2=dw2))
        dz1, dz1b, g["ln_mix_g"][l], g["ln_mix_b"][l] = _ln_bwd(dff, sv["z1"], small["ln_mix_g"], l, "ln_bwd_res",
                                                                 r=dz2, dep=dep)
        if l % 2 == 0:
            wout = big["wout2d"]
            dycat = _mm(dz1b, wout, mode="nt", grid=(EVEN_MIX // 512,), a_spec=_full((t, D)),
                        b_spec=_row_blocks(D, 512), out_shape=S((t, EVEN_MIX), F32),
                        out_spec=_col_blocks(t, EVEN_MIX, 512), name="even_dycat")
            dwout = _mm(sv["ycat"], dz1b, mode="tn", grid=(EVEN_MIX // 512,), a_spec=_col_blocks(t, EVEN_MIX, 512),
                        b_spec=_full((t, D)), out_shape=S((EVEN_MIX, D), BF16), out_spec=_row_blocks(D, 512),
                        name="even_dwout")
            du_pool, g["pool_w"][j], g["pool_scale"][j] = _pool_bwd(sv["proj"], dycat, small["pool_w"],
                                                                   small["pool_scale"], j)
            (du_lru, du_gate, g["conv_w"][j], g["conv_b"][j], g["w_a"][j], g["b_a"][j], g["w_x"][j], g["b_x"][j],
             g["lam"][j]) = _lru_bwd(sv["proj"], dycat, sv["hstate"], lru_p, j)
            dproj = jnp.concatenate([du_pool, du_lru, du_gate], axis=1)
            dep = grads_done(l, dict(win=_even_dwin(sv["xb"], dproj), wout=dwout.reshape(N_DEV, EVEN_MIX // N_DEV, D)))
            dy = _mm(dproj, big["win_t"], mode="nn", grid=(t // bm,), a_spec=_row_blocks(EVEN_IN, bm),
                     b_spec=_full((EVEN_IN, D)), out_shape=S((t, D), F32), out_spec=_row_blocks(D, bm),
                     add=dz1, add_spec=_row_blocks(D, bm), add_scale=ALPHA, name="even_dx", dep=dep)
        else:
            dwo, dwqb, dwkvb, dcq, dckv, dkpe = _attn_bwd(
                sv["cq"], sv["ckv"], sv["kpe"], cos, sin, big["wqb"], big["wkvb"], big["wo"], sv["o"], dz1b)
            ddown, g["gq"][j], g["gkv"][j] = _rms_bwd(sv["down"], dcq, dckv, dkpe, cos, sin, small["gq"],
                                                     small["gkv"], j)
            dwdown = _mm(sv["xb"], ddown, mode="tn", grid=(N_DEV,), a_spec=_col_blocks(t, D, D // N_DEV),
                         b_spec=_full((t, ODD_IN)), out_shape=S((N_DEV, D // N_DEV, ODD_IN), BF16),
                         out_spec=pl.BlockSpec((None, D // N_DEV, ODD_IN), lambda i: (i, 0, 0)),
                         name="odd_dwdown")
            dep = grads_done(l, dict(wdown=dwdown, wqb=dwqb, wkvb=dwkvb, wo=dwo))
            dy = _mm(ddown, big["wdown2d"], mode="nt", grid=(t // bm,), a_spec=_row_blocks(ODD_IN, bm),
                     b_spec=_full((D, ODD_IN)), out_shape=S((t, D), F32), out_spec=_row_blocks(D, bm),
                     add=dz1, add_spec=_row_blocks(D, bm), add_scale=ALPHA, name="odd_dx", dep=dep)
    return loss_tile[0, 0], dy, g


def _mesh_place():
    x, y, c = lax.axis_index("x"), lax.axis_index("y"), lax.axis_index("c")
    return x, y, c


def _peer(place, k):
    x, y, c = place
    return (1 - x if k & 4 else x, 1 - y if k & 2 else y, 1 - c if k & 1 else c)


def _index(place):
    x, y, c = place
    return 4 * x + 2 * y + c


ANY = pl.BlockSpec(memory_space=pl.ANY)


def _make_zones(shards, me, name, dtype=BF16):
    n = len(shards)

    def body(me_ref, *refs):
        for src, dst in zip(refs[:n], refs[n:]):
            dst[...] = src[...].astype(dtype)

    grid_spec = pltpu.PrefetchScalarGridSpec(
        num_scalar_prefetch=1, grid=(1,),
        in_specs=[pl.BlockSpec(s.shape, lambda i, me_ref: (0, 0)) for s in shards],
        out_specs=[pl.BlockSpec((None,) + s.shape, lambda i, me_ref: (me_ref[0], 0, 0)) for s in shards])
    return pl.pallas_call(body, grid_spec=grid_spec, out_shape=[S((N_DEV,) + s.shape, dtype) for s in shards],
                          compiler_params=_cp("arbitrary"), name=name)(me, *shards)


def _shard_rows_tile(a):
    return max(d for d in range(16, 257, 16) if a % d == 0)


HBM = pl.BlockSpec(memory_space=pltpu.HBM)
SEM = pl.BlockSpec(memory_space=pltpu.SEMAPHORE)
DATAFLOW = pltpu.SideEffectType.DATAFLOW_SIDE_EFFECTING


def _in_hbm(a):
    return pltpu.with_memory_space_constraint(a, pltpu.HBM)


def _gather_ici_copies(place, src, land, w):
    me = _index(place)
    return [(_peer(place, k), land.at[me], land.at[me]) for k in (1, 2, 4, 6)]


def _gather_d2d_copies(place, src, land, w):
    blocks = [_index(_peer(place, k)) for k in (2, 4, 6)]
    return [(_peer(place, 1), land.at[b], land.at[b]) for b in blocks]


GATHER_ICI = (4, _gather_ici_copies)
GATHER_D2D = (3, _gather_d2d_copies)


def _scatter_plan(layers):
    def copies(place, src, land, w):
        me = _index(place)
        mine = land.at[me] if layers[w] is None else land.at[me, layers[w]]
        return [(_peer(place, k), src.at[_index(_peer(place, k))], mine) for k in range(1, N_DEV)]
    return (N_DEV - 1, copies)


def _gather_all_copies(place, src, land, w):
    me = _index(place)
    return [(_peer(place, k), land.at[me], land.at[me]) for k in range(1, N_DEV)]


GATHER_ALL = (N_DEV - 1, _gather_all_copies)


def _sum_blocks(zone, part, me):
    r = part.shape[1]

    def body(me_ref, z_ref, p_ref, o_ref):
        acc = None
        for s in range(N_DEV):
            term = jnp.where(me_ref[0] == s, p_ref[...], z_ref[s])
            acc = term if acc is None else acc + term
        o_ref[...] = acc

    grid_spec = pltpu.PrefetchScalarGridSpec(
        num_scalar_prefetch=1, grid=(1,),
        in_specs=[pl.BlockSpec((N_DEV, r, 128), lambda i, me_ref: (0, 0, 0)),
                  pl.BlockSpec((None, r, 128), lambda i, me_ref: (me_ref[0], 0, 0))],
        out_specs=pl.BlockSpec((r, 128), lambda i, me_ref: (0, 0)))
    return pl.pallas_call(body, grid_spec=grid_spec, out_shape=S((r, 128), F32),
                          compiler_params=_cp("arbitrary"), name="sum_small")(me, zone, part)


def _exchange_start(srcs, lands, plan, name, after=()):
    ns, n = len(srcs), len(lands)
    n_in = ns + n + len(after)
    per, copies = plan

    def body(*refs):
        ins, land = refs[:ns], refs[ns:ns + n]
        send, recv = refs[n_in], refs[n_in + 1]
        token = refs[-1]
        place = _mesh_place()
        for i in range(per):
            for w in range(n):
                target, src, dst = copies(place, ins[w] if ns else None, land[w], w)[i]
                pltpu.make_async_remote_copy(src_ref=src, dst_ref=dst, send_sem=send.at[w * per + i],
                                             recv_sem=recv.at[w * per + i], device_id=target, device_id_type=MESH).start()
        token[...] = jnp.zeros_like(token)

    sems = pltpu.SemaphoreType.DMA((n * per,))
    thru = [pltpu.HBM(a.shape, a.dtype) for a in list(srcs) + list(lands)]
    out = pl.pallas_call(
        body, name=name, in_specs=[HBM] * (ns + n) + [ANY] * len(after),
        out_shape=(sems, sems, *thru, S((8, 128), F32)),
        out_specs=(SEM, SEM, *([HBM] * (ns + n)), pl.BlockSpec(memory_space=pltpu.VMEM)),
        input_output_aliases={i: 2 + i for i in range(ns + n)},
        compiler_params=pltpu.CompilerParams(has_side_effects=DATAFLOW),
    )(*[_in_hbm(a) for a in list(srcs) + list(lands)], *after)
    return out[0], out[1], list(out[2:2 + ns]), list(out[2 + ns:2 + ns + n]), out[-1]


def _exchange_wait(send, recv, srcs, lands, plan, after, name):
    ns, n = len(srcs), len(lands)
    per, copies = plan
    afters = tuple(after) if isinstance(after, (tuple, list)) else (after,)

    def body(*refs):
        ins, land = refs[:ns], refs[ns:ns + n]
        send_ref, recv_ref = refs[ns + n], refs[ns + n + 1]
        place = _mesh_place()
        for i in range(per):
            for w in range(n):
                target, src, dst = copies(place, ins[w] if ns else None, land[w], w)[i]
                cp = pltpu.make_async_remote_copy(src_ref=src, dst_ref=dst, send_sem=send_ref.at[w * per + i],
                                                  recv_sem=recv_ref.at[w * per + i], device_id=target,
                                                  device_id_type=MESH)
                cp.wait_send()
                cp.wait_recv()

    thru = [pltpu.HBM(a.shape, a.dtype) for a in list(srcs) + list(lands)]
    out = pl.pallas_call(
        body, name=name, in_specs=[HBM] * (ns + n) + [SEM, SEM] + [ANY] * len(afters),
        out_shape=tuple(thru), out_specs=tuple([HBM] * (ns + n)),
        input_output_aliases={i: i for i in range(ns + n)},
        compiler_params=pltpu.CompilerParams(has_side_effects=DATAFLOW),
    )(*srcs, *lands, send, recv, *afters)
    return list(out[:ns]), list(out[ns:])


def _adamw(w, g, m, v):
    m = ADAM_B1 * m + (1.0 - ADAM_B1) * g
    v = ADAM_B2 * v + (1.0 - ADAM_B2) * (g * g)
    m_hat = m / (1.0 - ADAM_B1 ** ADAM_STEP)
    v_hat = v / (1.0 - ADAM_B2 ** ADAM_STEP)
    return -ADAM_LR * (m_hat / (jnp.sqrt(v_hat) + ADAM_EPS) + ADAM_WD * w), m, v


def _adam_big(parts, own, me, w, m, v, name):
    nl, a, b = w.shape
    ta = _shard_rows_tile(a)

    def body(me_ref, p_ref, *refs):
        own_refs, (w_ref, m_ref, v_ref, g_ref, d_ref, mo_ref, vo_ref) = refs[:nl], refs[nl:]
        layer = pl.program_id(0)
        mine = own_refs[0][...]
        for k in range(1, nl):
            mine = jnp.where(layer == k, own_refs[k][...], mine)
        g = None
        for s in range(N_DEV):
            term = jnp.where(me_ref[0] == s, mine, p_ref[s]).astype(F32)
            g = term if g is None else g + term
        g_ref[...] = g
        d_ref[...], mo_ref[...], vo_ref[...] = _adamw(w_ref[...], g, m_ref[...], v_ref[...])

    blk = pl.BlockSpec((None, ta, b), lambda l, i, me_ref: (l, i, 0))

    def own_spec(k):
        return pl.BlockSpec((None, ta, b), lambda l, i, me_ref: (me_ref[0], jnp.where(l == k, i, 0), 0))

    grid_spec = pltpu.PrefetchScalarGridSpec(
        num_scalar_prefetch=1, grid=(nl, a // ta),
        in_specs=[pl.BlockSpec((N_DEV, None, ta, b), lambda l, i, me_ref: (0, l, i, 0))]
        + [own_spec(k) for k in range(nl)] + [blk, blk, blk],
        out_specs=[blk] * 4)
    return pl.pallas_call(body, grid_spec=grid_spec, out_shape=[S(w.shape, F32)] * 4,
                          compiler_params=_cp("arbitrary", "arbitrary"), name=name)(me, parts, *own, w, m, v)


def _adam_small(gs, ws, ms, vs):
    n = len(gs)

    def body(*refs):
        ins, outs = refs[:4 * n], refs[4 * n:]
        for i in range(n):
            g_ref, w_ref, m_ref, v_ref = (ins[k * n + i] for k in range(4))
            outs[i][...], outs[n + i][...], outs[2 * n + i][...] = _adamw(w_ref[...], g_ref[...], m_ref[...], v_ref[...])

    out = pl.pallas_call(body, out_shape=[S(g.shape, F32) for g in gs] * 3, compiler_params=_cp(),
                         name="adam_small")(*gs, *ws, *ms, *vs)
    return out[:n], out[n:2 * n], out[2 * n:]


BIG = ("even_w_in", "even_w_out", "mla_w_down", "mla_w_qb", "mla_w_kvb", "mla_w_o", "mlp_w1", "mlp_w2")
BIG_KEY = dict(even_w_in="win", even_w_out="wout", mla_w_down="wdown", mla_w_qb="wqb", mla_w_kvb="wkvb",
               mla_w_o="wo", mlp_w1="w1", mlp_w2="w2")
SMALL = (("ln_mix_g", "ln_mix_g", None), ("ln_mix_b", "ln_mix_b", None), ("ln_ffn_g", "ln_ffn_g", None),
         ("ln_ffn_b", "ln_ffn_b", None), ("pool_w", "pool_w", None), ("pool_scale", "pool_scale", None),
         ("lru_conv_w", "conv_w", 2), ("lru_conv_b", "conv_b", None), ("lru_w_a", "w_a", None),
         ("lru_b_a", "b_a", None), ("lru_w_x", "w_x", None), ("lru_b_x", "b_x", None), ("lru_lambda", "lam", None),
         ("mla_q_norm_g", "gq", 1), ("mla_kv_norm_g", "gkv", 1))
WEIGHTS = ("ln_mix_g", "ln_mix_b", "ln_ffn_g", "ln_ffn_b", "even_w_in", "pool_w", "pool_scale", "lru_conv_w",
           "lru_conv_b", "lru_w_a", "lru_b_a", "lru_w_x", "lru_b_x", "lru_lambda", "even_w_out", "mla_w_down",
           "mla_q_norm_g", "mla_kv_norm_g", "mla_w_qb", "mla_w_kvb", "mla_w_o", "mlp_w1", "mlp_w2")


def _layer_weights(l):
    j = l // 2
    if l % 2 == 0:
        mixer = [("win", "even_w_in", j), ("wout", "even_w_out", j)]
    else:
        mixer = [("wdown", "mla_w_down", j), ("wqb", "mla_w_qb", j), ("wkvb", "mla_w_kvb", j), ("wo", "mla_w_o", j)]
    return mixer + [("w1", "mlp_w1", l), ("w2", "mlp_w2", l)]


def _pack(arrays, multiple):
    flat = jnp.concatenate([a.reshape(-1) for a in arrays])
    pad = (-flat.shape[0]) % multiple
    return jnp.pad(flat, (0, pad))


def _unpack(flat, shapes):
    out, at = [], 0
    for shp in shapes:
        n = 1
        for s in shp:
            n *= s
        out.append(flat[at:at + n].reshape(shp))
        at += n
    return out


def _global_shape(local_shape, axis):
    if axis is None:
        return tuple(local_shape)
    return tuple(s * N_DEV if i == axis else s for i, s in enumerate(local_shape))


def _step(x, positions, tgt, w, m, v):
    t = x.shape[1]
    me = _index(_mesh_place())

    chunk = N_DEV * 8 * 128
    me_arr = me.astype(jnp.int32).reshape(1)

    lanes = lambda a: jnp.pad(a, ((0, 0), (0, 128 - a.shape[1])))
    mine_packed = jnp.concatenate([w["lru_conv_w"].reshape(8, HEAD), lanes(w["mla_q_norm_g"]),
                                   lanes(w["mla_kv_norm_g"]), jnp.zeros((4, 128), F32)])
    g_send, g_recv, _, g_land, token = _exchange_start([], _make_zones([mine_packed], me_arr, "zones_small", F32),
                                                       GATHER_ALL, "small_params_start")

    def keys_of(l, part):
        keys = [key for key, _, _ in _layer_weights(l)]
        if l == 0:
            return keys[:1] if part == 0 else keys[1:]
        return keys if part == 0 else []

    shard_of = {(l, key): (w[name][i].T if key == "win" else w[name][i])
                for l in range(DEPTH) for key, name, i in _layer_weights(l)}
    flights, after = {}, (token,)
    for l in range(DEPTH):
        for part in (0, 1):
            if keys_of(l, part):
                zones = _make_zones([shard_of[l, key] for key in keys_of(l, part)], me_arr, "zones_%d_%d" % (l, part))
                send, recv, _, lands, token = _exchange_start([], zones, GATHER_ICI, "gather_start_%d_%d" % (l, part),
                                                              after=after)
                flights[l, part] = (send, recv, [], lands)
                after = (token,)

    _, g_land = _exchange_wait(g_send, g_recv, [], g_land, GATHER_ALL, token, "small_params_wait")
    rows_first = g_land[0].transpose(1, 0, 2)
    q_shard, kv_shard = w["mla_q_norm_g"].shape[1], w["mla_kv_norm_g"].shape[1]
    full = dict(lru_conv_w=rows_first[:8].reshape(2, 4, LRU_W),
                mla_q_norm_g=rows_first[8:10, :, :q_shard].reshape(2, Q_RANK),
                mla_kv_norm_g=rows_first[10:12, :, :kv_shard].reshape(2, KV_RANK))

    passing = {}

    def pass_on(l, part, after):
        tag = "%d_%d" % (l, part)
        _, lands = _exchange_wait(*flights[l, part], GATHER_ICI, after, "gather_wait_" + tag)
        send, recv, _, lands, token = _exchange_start([], lands, GATHER_D2D, "gather_pass_" + tag)
        passing[l, part] = (send, recv, [], lands)
        return token

    def early_pass(l, after):
        return pass_on(l, 0, after) if l >= 2 else None

    def weights_of(l, part, after):
        keys = keys_of(l, part)
        if keys:
            if (l, part) not in passing:
                pass_on(l, part, after)
            _, arrays = _exchange_wait(*passing[l, part], GATHER_D2D, after, "gather_pass_wait_%d_%d" % (l, part))
        big = dict(zip(keys, arrays)) if keys else {}
        if "win" in big:
            big["win_t"] = big["win"].reshape(EVEN_IN, D)
        if "wout" in big:
            big["wout2d"] = big["wout"].reshape(EVEN_MIX, D)
        if "wdown" in big:
            big["wdown2d"] = big["wdown"].reshape(D, ODD_IN)
        return big

    zone = {name: lax.empty((N_DEV,) + w[name].shape, BF16) for name in BIG}
    name_of = {key: name for name, key in BIG_KEY.items()}
    sent, last_token = [], [None]

    def grads_done(l, grads):
        keys = list(grads)
        index = {key: i for key, _, i in _layer_weights(l)}
        layers = [index[key] for key in keys]
        send, recv, srcs, lands, tok = _exchange_start([grads[k] for k in keys], [zone[name_of[k]] for k in keys],
                                                       _scatter_plan(layers), "scatter_start_%d_%s" % (l, keys[0]))
        for k, land in zip(keys, lands):
            zone[name_of[k]] = land
        sent.append((send, recv, srcs, keys, layers))
        last_token[0] = tok
        return tok

    row3 = lambda a: a.reshape(a.shape[0], 1, a.shape[1])
    small = dict(ln_mix_g=row3(w["ln_mix_g"]), ln_mix_b=row3(w["ln_mix_b"]), ln_ffn_g=row3(w["ln_ffn_g"]),
                 ln_ffn_b=row3(w["ln_ffn_b"]), pool_w=w["pool_w"], pool_scale=row3(w["pool_scale"]),
                 conv_w=full["lru_conv_w"], conv_b=row3(w["lru_conv_b"]), w_a=w["lru_w_a"], b_a=row3(w["lru_b_a"]),
                 w_x=w["lru_w_x"], b_x=row3(w["lru_b_x"]), lam=row3(w["lru_lambda"]),
                 gq=row3(full["mla_q_norm_g"]), gkv=row3(full["mla_kv_norm_g"]))

    loss_part, grad_x, g = _local_step(x[0], positions.reshape(t, 1), tgt[0], small, weights_of, grads_done,
                                       start_dep=token, prefetch=early_pass)

    own = {name: [None] * w[name].shape[0] for name in BIG}
    me_arr = me.astype(jnp.int32).reshape(1)
    out = {}
    local_g = [jnp.stack(g[key]).reshape(_global_shape(w[name].shape, axis)) for name, key, axis in SMALL]
    local_g.append(loss_part.reshape(1))
    part = _pack(local_g, chunk).reshape(N_DEV, -1, 128)
    small_plan = _scatter_plan([None])
    s_send, s_recv, s_src, s_land, after = _exchange_start([part], [lax.empty(part.shape, F32)], small_plan,
                                                           "small_scatter_start", after=(last_token[0],))
    after = [after]
    for n_flight, (send, recv, srcs, keys, layers) in enumerate(sent):
        if n_flight == len(sent) - 1:
            s_src, s_land = _exchange_wait(s_send, s_recv, s_src, s_land, small_plan, [grad_x] + after,
                                           "small_scatter_wait")
            chunk_sum = _sum_blocks(s_land[0], s_src[0], me_arr)
            r_zone = lax.dynamic_update_slice_in_dim(lax.empty(part.shape, F32), chunk_sum[None], me, 0)
            r_send, r_recv, _, r_land, token = _exchange_start([], [r_zone], GATHER_ALL, "small_gather_start")
            after = [token]
        srcs, lands = _exchange_wait(send, recv, srcs, [zone[name_of[k]] for k in keys], _scatter_plan(layers),
                                     after, "scatter_wait_%d" % n_flight)
        for k, land, src, layer in zip(keys, lands, srcs, layers):
            zone[name_of[k]] = land
            own[name_of[k]][layer] = src
        after = [lands[0]]
        for name in BIG:
            if name not in out and all(o is not None for o in own[name]):
                out[name] = _adam_big(zone[name], own[name], me_arr, w[name], m[name], v[name], "adam_" + name)
                after.append(out[name][0])

    _, reduced = _exchange_wait(r_send, r_recv, [], r_land, GATHER_ALL, [out[name][0] for name in BIG],
                                "small_gather_wait")
    reduced = _unpack(reduced[0].reshape(-1), [a.shape for a in local_g])
    loss = reduced[-1][0]
    mine = [a if axis is None else lax.dynamic_slice_in_dim(a, me * w[name].shape[axis], w[name].shape[axis], axis)
            for a, (name, _, axis) in zip(reduced, SMALL)]
    names = [name for name, _, _ in SMALL]
    as_2d = lambda a: a.reshape(-1, a.shape[-1])
    new = _adam_small([as_2d(a) for a in mine], *([as_2d(src[name]) for name in names] for src in (w, m, v)))
    for i, name in enumerate(names):
        out[name] = (mine[i],) + tuple(part[i].reshape(w[name].shape) for part in new)

    return (loss, grad_x[None]) + tuple(out[name][i] for i in range(4) for name in WEIGHTS)


def kernel(x, positions, ln_mix_g, ln_mix_b, ln_ffn_g, ln_ffn_b, even_w_in, pool_w, pool_scale, lru_conv_w, lru_conv_b, lru_w_a, lru_b_a, lru_w_x, lru_b_x, lru_lambda, even_w_out, mla_w_down, mla_q_norm_g, mla_kv_norm_g, mla_w_qb, mla_w_kvb, mla_w_o, mlp_w1, mlp_w2, loss_target, m_ln_mix_g, m_ln_mix_b, m_ln_ffn_g, m_ln_ffn_b, m_even_w_in, m_pool_w, m_pool_scale, m_lru_conv_w, m_lru_conv_b, m_lru_w_a, m_lru_b_a, m_lru_w_x, m_lru_b_x, m_lru_lambda, m_even_w_out, m_mla_w_down, m_mla_q_norm_g, m_mla_kv_norm_g, m_mla_w_qb, m_mla_w_kvb, m_mla_w_o, m_mlp_w1, m_mlp_w2, v_ln_mix_g, v_ln_mix_b, v_ln_ffn_g, v_ln_ffn_b, v_even_w_in, v_pool_w, v_pool_scale, v_lru_conv_w, v_lru_conv_b, v_lru_w_a, v_lru_b_a, v_lru_w_x, v_lru_b_x, v_lru_lambda, v_even_w_out, v_mla_w_down, v_mla_q_norm_g, v_mla_kv_norm_g, v_mla_w_qb, v_mla_w_kvb, v_mla_w_o, v_mlp_w1, v_mlp_w2):
    w = dict(zip(WEIGHTS, (ln_mix_g, ln_mix_b, ln_ffn_g, ln_ffn_b, even_w_in, pool_w, pool_scale, lru_conv_w,
                           lru_conv_b, lru_w_a, lru_b_a, lru_w_x, lru_b_x, lru_lambda, even_w_out, mla_w_down,
                           mla_q_norm_g, mla_kv_norm_g, mla_w_qb, mla_w_kvb, mla_w_o, mlp_w1, mlp_w2)))
    m = dict(zip(WEIGHTS, (m_ln_mix_g, m_ln_mix_b, m_ln_ffn_g, m_ln_ffn_b, m_even_w_in, m_pool_w, m_pool_scale,
                           m_lru_conv_w, m_lru_conv_b, m_lru_w_a, m_lru_b_a, m_lru_w_x, m_lru_b_x, m_lru_lambda,
                           m_even_w_out, m_mla_w_down, m_mla_q_norm_g, m_mla_kv_norm_g, m_mla_w_qb, m_mla_w_kvb,
                           m_mla_w_o, m_mlp_w1, m_mlp_w2)))
    v = dict(zip(WEIGHTS, (v_ln_mix_g, v_ln_mix_b, v_ln_ffn_g, v_ln_ffn_b, v_even_w_in, v_pool_w, v_pool_scale,
                           v_lru_conv_w, v_lru_conv_b, v_lru_w_a, v_lru_b_a, v_lru_w_x, v_lru_b_x, v_lru_lambda,
                           v_even_w_out, v_mla_w_down, v_mla_q_norm_g, v_mla_kv_norm_g, v_mla_w_qb, v_mla_w_kvb,
                           v_mla_w_o, v_mlp_w1, v_mlp_w2)))
    return _step(x, positions, loss_target, w, m, v)
```

```python
import jax
import jax.numpy as jnp
from jax import lax
from jax.experimental import pallas as pl
from jax.experimental.pallas import tpu as pltpu

F32 = jnp.float32
BF16 = jnp.bfloat16
S = jax.ShapeDtypeStruct

D = 1024
DEPTH = 4
N_DEV = 8
CHUNK_SHIFT = 6
POOL_WINDOWS = (2, 4, 8, 16)
POOL_W = 512
LRU_W = 1024
LRU_HEADS = 8
HEAD = 128
LRU_C = 8.0
EVEN_IN = 2560
EVEN_MIX = 1536
MLA_HEADS = 8
NOPE = 128
ROPE = 64
VDIM = 128
Q_RANK = 384
KV_RANK = 256
ODD_IN = 704
D_FF = 4096
FF_BLK = D_FF // N_DEV
ROPE_THETA = 10000.0
ALPHA = (2 * DEPTH) ** 0.25
LN_EPS = 1e-5
RMS_EPS = 1e-6
ATT_SCALE = (NOPE + ROPE) ** -0.5
NEG = float(jnp.finfo(jnp.float32).min)
ADAM_LR = 0.001
ADAM_B1 = 0.9
ADAM_B2 = 0.999
ADAM_EPS = 1e-08
ADAM_WD = 0.01
ADAM_STEP = 10
V7X_VMEM_BYTES = 64 * 1024 * 1024
VMEM_LIMIT = V7X_VMEM_BYTES - 8 * 1024 * 1024
MESH = pl.DeviceIdType.MESH


def _cp(*sem):
    return pltpu.CompilerParams(dimension_semantics=sem if sem else None, vmem_limit_bytes=VMEM_LIMIT)


def _dot(a, b):
    return jnp.dot(a, b, preferred_element_type=F32)


def _dot_nt(a, b):
    return lax.dot_general(a, b, (((1,), (1,)), ((), ())), preferred_element_type=F32)


def _dot_tn(a, b):
    return lax.dot_general(a, b, (((0,), (0,)), ((), ())), preferred_element_type=F32)


def _full(shape):
    return pl.BlockSpec(shape, lambda *_: (0,) * len(shape))


def _mm(a, b, *, mode, grid, a_spec, b_spec, out_shape, out_spec, name, add=None, add_spec=None, add_scale=1.0,
        dep=None):
    dot = {"nn": _dot, "nt": _dot_nt, "tn": _dot_tn}[mode]

    def body(*refs):
        a_ref, b_ref, o_ref = refs[0], refs[1], refs[-1]
        acc = dot(a_ref[...].astype(BF16), b_ref[...].astype(BF16))
        if add is not None:
            acc = acc + add_scale * refs[2][...]
        o_ref[...] = acc.astype(o_ref.dtype)

    ops = [a, b] if add is None else [a, b, add]
    specs = [a_spec, b_spec] if add is None else [a_spec, b_spec, add_spec]
    if dep is not None:
        ops.append(dep)
        specs.append(pl.BlockSpec(memory_space=pl.ANY))
    return pl.pallas_call(body, grid=grid, in_specs=specs, out_specs=out_spec, out_shape=out_shape,
                          compiler_params=_cp(*(("parallel",) * len(grid))), name=name)(*ops)


def _even_dwin(xb, dproj):
    shard = EVEN_IN // N_DEV

    def body(x_ref, dp_ref, o_ref):
        xv = x_ref[...].astype(BF16)
        for d in range(N_DEV):
            o_ref[d] = _dot_tn(xv, dp_ref[:, d * shard:(d + 1) * shard]).astype(BF16)

    return pl.pallas_call(body, out_shape=S((N_DEV, D, shard), BF16), compiler_params=_cp(), name="even_dwin")(xb, dproj)


def _ln_stats(z):
    mu = jnp.mean(z, axis=-1, keepdims=True)
    zc = z - mu
    var = jnp.mean(zc * zc, axis=-1, keepdims=True)
    rstd = lax.rsqrt(var + LN_EPS)
    return zc * rstd, rstd


def _row_tile(t):
    return min(1024, t)


def _resid_ln(x, mix, g3, b3, l, name):
    t = x.shape[0]
    bm = _row_tile(t)

    def body(x_ref, m_ref, g_ref, b_ref, z_ref, y_ref, yb_ref):
        z = ALPHA * x_ref[...] + m_ref[...]
        xh, _ = _ln_stats(z)
        y = xh * g_ref[...] + b_ref[...]
        z_ref[...] = z
        y_ref[...] = y
        yb_ref[...] = y.astype(BF16)

    row = pl.BlockSpec((bm, D), lambda i: (i, 0))
    vec = pl.BlockSpec((None, 1, D), lambda i: (l, 0, 0))
    return pl.pallas_call(body, grid=(t // bm,), in_specs=[row, row, vec, vec], out_specs=[row, row, row],
                          out_shape=[S((t, D), F32), S((t, D), F32), S((t, D), BF16)],
                          compiler_params=_cp("parallel"), name=name)(x, mix, g3, b3)


def _proj_resid_ln(x, a, wmat, g3, b3, l, name):
    t, k = a.shape
    bm = _row_tile(t)

    def body(x_ref, a_ref, w_ref, g_ref, b_ref, z_ref, y_ref, yb_ref):
        z = ALPHA * x_ref[...] + _dot(a_ref[...], w_ref[...])
        xh, _ = _ln_stats(z)
        y = xh * g_ref[...] + b_ref[...]
        z_ref[...] = z
        y_ref[...] = y
        yb_ref[...] = y.astype(BF16)

    row = pl.BlockSpec((bm, D), lambda i: (i, 0))
    vec = pl.BlockSpec((None, 1, D), lambda i: (l, 0, 0))
    return pl.pallas_call(body, grid=(t // bm,),
                          in_specs=[row, pl.BlockSpec((bm, k), lambda i: (i, 0)), _full((k, D)), vec, vec],
                          out_specs=[row, row, row], out_shape=[S((t, D), F32), S((t, D), F32), S((t, D), BF16)],
                          compiler_params=_cp("parallel"), name=name)(x, a, wmat, g3, b3)


def _ln_bwd(d, z, g3, l, name, r=None, dep=None):
    t = z.shape[0]
    bm = _row_tile(t)

    def body(*refs):
        refs = list(refs)
        d_ref = refs.pop(0)
        dy = d_ref[...]
        if r is not None:
            dy = dy + ALPHA * refs.pop(0)[...]
        z_ref, g_ref = refs.pop(0), refs.pop(0)
        if dep is not None:
            refs.pop(0)
        dz_ref, dzb_ref, dg_ref, db_ref = refs
        xh, rstd = _ln_stats(z_ref[...])
        dyg = dy * g_ref[...]
        m1 = jnp.mean(dyg, axis=-1, keepdims=True)
        m2 = jnp.mean(dyg * xh, axis=-1, keepdims=True)
        dz = rstd * (dyg - m1 - xh * m2)
        dz_ref[...] = dz
        dzb_ref[...] = dz.astype(BF16)

        @pl.when(pl.program_id(0) == 0)
        def _():
            dg_ref[...] = jnp.zeros_like(dg_ref)
            db_ref[...] = jnp.zeros_like(db_ref)

        dg_ref[...] += jnp.sum(dy * xh, axis=0, keepdims=True)
        db_ref[...] += jnp.sum(dy, axis=0, keepdims=True)

    row = pl.BlockSpec((bm, D), lambda i: (i, 0))
    vec = pl.BlockSpec((None, 1, D), lambda i: (l, 0, 0))
    acc = pl.BlockSpec((1, D), lambda i: (0, 0))
    ops = [d, z, g3] if r is None else [d, r, z, g3]
    specs = [row, row, vec] if r is None else [row, row, row, vec]
    if dep is not None:
        ops.append(dep)
        specs.append(_full(dep.shape))
    return pl.pallas_call(body, grid=(t // bm,), in_specs=specs, out_specs=[row, row, acc, acc],
                          out_shape=[S((t, D), F32), S((t, D), BF16), S((1, D), F32), S((1, D), F32)],
                          compiler_params=_cp("arbitrary"), name=name)(*ops)


def _loss_grad(y, tgt):
    t = y.shape[0]
    bm = _row_tile(t)

    def body(y_ref, t_ref, dy_ref, loss_ref, acc_ref):
        i = pl.program_id(0)
        e = y_ref[...] - t_ref[...]
        dy_ref[...] = e * (1.0 / D)

        @pl.when(i == 0)
        def _():
            acc_ref[...] = jnp.zeros_like(acc_ref)

        acc_ref[...] += jnp.sum(e * e, axis=0, keepdims=True)

        @pl.when(i == pl.num_programs(0) - 1)
        def _():
            loss_ref[...] = jnp.full(loss_ref.shape, (0.5 / D) * jnp.sum(acc_ref[...]), F32)

    row = pl.BlockSpec((bm, D), lambda i: (i, 0))
    return pl.pallas_call(body, grid=(t // bm,), in_specs=[row, row],
                          out_specs=[row, pl.BlockSpec((1, 128), lambda i: (0, 0))],
                          out_shape=[S((t, D), F32), S((1, 128), F32)],
                          scratch_shapes=[pltpu.VMEM((1, D), F32)],
                          compiler_params=_cp("arbitrary"), name="loss_grad")(y, tgt)


def _mlp_row_tile(t):
    return min(1024, t)


MLP_ROW_PARTS = 2


def _row_parts(bm):
    step = bm // MLP_ROW_PARTS
    return [slice(k * step, (k + 1) * step) for k in range(MLP_ROW_PARTS)]


def _mlp_fwd(y, yb, w1g, w2g, g3, b3, l, dep=None):
    t = yb.shape[0]
    bm = _mlp_row_tile(t)

    def body(*refs):
        y_ref, yb_ref, w1_ref, w2_ref, g_ref, b_ref = refs[:6]
        z_ref, o_ref, ob_ref, act_ref, acc_ref = refs[-5:]
        j = pl.program_id(1)

        @pl.when(j == 0)
        def _():
            acc_ref[...] = jnp.zeros_like(acc_ref)

        for rows in _row_parts(bm):
            h = jnp.maximum(_dot(yb_ref[rows, :], w1_ref[...]), 0.0)
            act = (h * h).astype(BF16)
            act_ref[rows, :] = act
            acc_ref[rows, :] += _dot(act, w2_ref[...])

        @pl.when(j == N_DEV - 1)
        def _():
            z = ALPHA * y_ref[...] + acc_ref[...]
            xh, _ = _ln_stats(z)
            out = xh * g_ref[...] + b_ref[...]
            z_ref[...] = z
            o_ref[...] = out
            ob_ref[...] = out.astype(BF16)

    row = pl.BlockSpec((bm, D), lambda i, j: (i, 0))
    vec = pl.BlockSpec((None, 1, D), lambda i, j: (l, 0, 0))
    deps = [] if dep is None else [dep]
    return pl.pallas_call(
        body, grid=(t // bm, N_DEV),
        in_specs=[row, row, pl.BlockSpec((None, D, FF_BLK), lambda i, j: (j, 0, 0)),
                  pl.BlockSpec((None, FF_BLK, D), lambda i, j: (j, 0, 0)), vec, vec] + [ANY] * len(deps),
        out_specs=[row, row, row, pl.BlockSpec((bm, FF_BLK), lambda i, j: (i, j))],
        out_shape=[S((t, D), F32), S((t, D), F32), S((t, D), BF16), S((t, D_FF), BF16)],
        scratch_shapes=[pltpu.VMEM((bm, D), F32)],
        compiler_params=_cp("parallel", "arbitrary"), name="mlp_fwd")(y, yb, w1g, w2g, g3, b3, *deps)


def _mlp_bwd_dh(act, dzb, w1g, w2g):
    t = act.shape[0]
    bm = _mlp_row_tile(t)

    def body(a_ref, dz_ref, w1_ref, w2_ref, dh_ref, acc_ref):
        @pl.when(pl.program_id(1) == 0)
        def _():
            acc_ref[...] = jnp.zeros_like(acc_ref)

        for rows in _row_parts(bm):
            r = jnp.sqrt(a_ref[rows, :].astype(F32))
            dh = (_dot_nt(dz_ref[rows, :], w2_ref[...]) * (2.0 * r)).astype(BF16)
            dh_ref[rows, :] = dh
            acc_ref[rows, :] += _dot_nt(dh, w1_ref[...])

    row = pl.BlockSpec((bm, D), lambda i, j: (i, 0))
    hid = pl.BlockSpec((bm, FF_BLK), lambda i, j: (i, j))
    return pl.pallas_call(
        body, grid=(t // bm, N_DEV),
        in_specs=[hid, row,
                  pl.BlockSpec((None, D, FF_BLK), lambda i, j: (j, 0, 0)),
                  pl.BlockSpec((None, FF_BLK, D), lambda i, j: (j, 0, 0))],
        out_specs=[hid, row],
        out_shape=[S((t, D_FF), BF16), S((t, D), F32)],
        compiler_params=_cp("parallel", "arbitrary"), name="mlp_bwd_dh")(act, dzb, w1g, w2g)


F32_SUBLANES = 8


def _shift_dn(x, k, rows, fill=0.0):
    if k % F32_SUBLANES == 0:
        return jnp.concatenate([jnp.full((k,) + x.shape[1:], fill, x.dtype), x[:x.shape[0] - k]], axis=0)
    return jnp.where(rows >= k, pltpu.roll(x, k, 0), fill)


def _shift_up(x, k, rows, fill=0.0):
    t = x.shape[0]
    if k % F32_SUBLANES == 0:
        return jnp.concatenate([x[k:], jnp.full((k,) + x.shape[1:], fill, x.dtype)], axis=0)
    return jnp.where(rows < t - k, pltpu.roll(x, t - k, 0), fill)


def _scan_rows(a, b, shift):
    rows = lax.broadcasted_iota(jnp.int32, a.shape, 0)
    k = 1
    t = a.shape[0]
    while k < t:
        b = a * shift(b, k, rows) + b
        if 2 * k < t:
            a = a * shift(a, k, rows, 1.0)
        k *= 2
    return b


def _scan_dn(a, b):
    return _scan_rows(a, b, _shift_dn)


def _scan_up(a, b):
    return _scan_rows(a, b, _shift_up)


def _window_sum_dn(x, w, rows):
    k = 1
    while k < w:
        x = x + _shift_dn(x, k, rows)
        k *= 2
    return x


def _window_sum_up(x, w, rows):
    k = 1
    while k < w:
        x = x + _shift_up(x, k, rows)
        k *= 2
    return x


def _pool_diff(u, w, rows):
    inv_count = 1.0 / jnp.minimum(rows + 1, w).astype(F32)
    return _window_sum_dn(u, w, rows) * inv_count - u, inv_count


def _pool_fwd(proj, pool_w, pool_scale3, j):
    t = proj.shape[0]

    def body(u_ref, w_ref, s_ref, y_ref):
        rows = lax.broadcasted_iota(jnp.int32, (t, HEAD), 0)
        for g, w in enumerate(POOL_WINDOWS):
            cols = slice(g * HEAD, (g + 1) * HEAD)
            d, _ = _pool_diff(u_ref[:, cols], w, rows)
            y = _dot(d.astype(BF16), w_ref[g].astype(BF16)) * s_ref[:, cols]
            y_ref[:, cols] = y.astype(BF16)

    return pl.pallas_call(
        body, grid=(1,),
        in_specs=[pl.BlockSpec((t, POOL_W), lambda i: (0, 0)),
                  pl.BlockSpec((None, 4, HEAD, HEAD), lambda i: (j, 0, 0, 0)),
                  pl.BlockSpec((None, 1, POOL_W), lambda i: (j, 0, 0))],
        out_specs=pl.BlockSpec((t, POOL_W), lambda i: (0, 0)),
        out_shape=S((t, POOL_W), BF16), compiler_params=_cp("arbitrary"), name="pool_fwd")(proj, pool_w, pool_scale3)


def _pool_bwd(proj, dycat, pool_w, pool_scale3, j):
    t = proj.shape[0]

    def body(u_ref, dy_ref, w_ref, s_ref, du_ref, dw_ref, ds_ref):
        rows = lax.broadcasted_iota(jnp.int32, (t, HEAD), 0)
        for g, w in enumerate(POOL_WINDOWS):
            cols = slice(g * HEAD, (g + 1) * HEAD)
            d, inv_count = _pool_diff(u_ref[:, cols], w, rows)
            db = d.astype(BF16)
            wg = w_ref[g].astype(BF16)
            dy = dy_ref[:, cols]
            ds_ref[:, cols] = jnp.sum(dy * _dot(db, wg), axis=0, keepdims=True)
            dzz = (dy * s_ref[:, cols]).astype(BF16)
            dw_ref[g] = _dot_tn(db, dzz)
            dd = _dot_nt(dzz, wg)
            du_ref[:, cols] = (_window_sum_up(dd * inv_count, w, rows) - dd).astype(BF16)

    return pl.pallas_call(
        body, grid=(1,),
        in_specs=[pl.BlockSpec((t, POOL_W), lambda i: (0, 0)),
                  pl.BlockSpec((t, POOL_W), lambda i: (0, 0)),
                  pl.BlockSpec((None, 4, HEAD, HEAD), lambda i: (j, 0, 0, 0)),
                  pl.BlockSpec((None, 1, POOL_W), lambda i: (j, 0, 0))],
        out_specs=[pl.BlockSpec((t, POOL_W), lambda i: (0, 0)), _full((4, HEAD, HEAD)), _full((1, POOL_W))],
        out_shape=[S((t, POOL_W), BF16), S((4, HEAD, HEAD), F32), S((1, POOL_W), F32)],
        compiler_params=_cp("arbitrary"), name="pool_bwd")(proj, dycat, pool_w, pool_scale3)


GELU_C = 0.7978845608028654
GELU_K = 0.044715


def _gelu(x):
    th = jnp.tanh(GELU_C * (x + GELU_K * x * x * x))
    return 0.5 * x * (1.0 + th), th


def _lru_forward(u, gate, cw, cb, wa, ba, wx, bx, lam, rows, h=None):
    v = cw[3:4] * u + cw[2:3] * _shift_dn(u, 1, rows) + cw[1:2] * _shift_dn(u, 2, rows) \
        + cw[0:1] * _shift_dn(u, 3, rows) + cb
    vb = v.astype(BF16)
    r = jax.nn.sigmoid(_dot(vb, wa) + ba)
    i = jax.nn.sigmoid(_dot(vb, wx) + bx)
    sp = jnp.maximum(-lam, 0.0) + jnp.log1p(jnp.exp(-jnp.abs(lam)))
    log_a = (-LRU_C) * r * sp
    a = jnp.exp(log_a)
    one_m_a2 = -jnp.tanh(log_a) * (a * a + 1.0)
    mult = jnp.sqrt(one_m_a2)
    if h is None:
        h = _scan_dn(a, mult * (i * v))
    gl, th = _gelu(gate)
    return dict(v=v, vb=vb, r=r, i=i, sp=sp, a=a, mult=mult, h=h, gl=gl, th=th)


def _lru_specs(t, j, col0_u, col0_g):
    blk = lambda c0: pl.BlockSpec((t, HEAD), lambda h: (0, c0 + h))
    vec = pl.BlockSpec((None, 1, HEAD), lambda h: (j, 0, h))
    return [blk(col0_u), blk(col0_g),
            pl.BlockSpec((None, 4, HEAD), lambda h: (j, 0, h)), vec,
            pl.BlockSpec((None, None, HEAD, HEAD), lambda h: (j, h, 0, 0)), vec,
            pl.BlockSpec((None, None, HEAD, HEAD), lambda h: (j, h, 0, 0)), vec, vec]


def _lru_fwd(proj, p, j):
    t = proj.shape[0]

    def body(u_ref, g_ref, cw_ref, cb_ref, wa_ref, ba_ref, wx_ref, bx_ref, lam_ref, y_ref, h_ref):
        rows = lax.broadcasted_iota(jnp.int32, (t, HEAD), 0)
        f = _lru_forward(u_ref[...], g_ref[...], cw_ref[...], cb_ref[...], wa_ref[...].astype(BF16), ba_ref[...],
                         wx_ref[...].astype(BF16), bx_ref[...], lam_ref[...], rows)
        y_ref[...] = (f["h"] * f["gl"]).astype(BF16)
        h_ref[...] = f["h"]

    blk = pl.BlockSpec((t, HEAD), lambda h: (0, h))
    return pl.pallas_call(
        body, grid=(LRU_HEADS,), in_specs=_lru_specs(t, j, POOL_W // HEAD, (POOL_W + LRU_W) // HEAD),
        out_specs=[blk, blk], out_shape=[S((t, LRU_W), BF16), S((t, LRU_W), F32)],
        compiler_params=_cp("parallel"), name="lru_fwd")(
            proj, proj, p["conv_w"], p["conv_b"], p["w_a"], p["b_a"], p["w_x"], p["b_x"], p["lam"])


def _lru_bwd(proj, dycat, hstate, p, j):
    t = proj.shape[0]

    def body(u_ref, g_ref, cw_ref, cb_ref, wa_ref, ba_ref, wx_ref, bx_ref, lam_ref, dy_ref, h_ref,
             du_ref, dgate_ref, dcw_ref, dcb_ref, dwa_ref, dba_ref, dwx_ref, dbx_ref, dlam_ref):
        rows = lax.broadcasted_iota(jnp.int32, (t, HEAD), 0)
        u = u_ref[...]
        gate = g_ref[...]
        cw = cw_ref[...]
        wa = wa_ref[...].astype(BF16)
        wx = wx_ref[...].astype(BF16)
        lam = lam_ref[...]
        f = _lru_forward(u, gate, cw, cb_ref[...], wa, ba_ref[...], wx, bx_ref[...], lam, rows, h=h_ref[...])
        v, r, i, a, mult, h, th = f["v"], f["r"], f["i"], f["a"], f["mult"], f["h"], f["th"]
        dy = dy_ref[...]
        dgl = 0.5 * (1.0 + th) + 0.5 * gate * (1.0 - th * th) * GELU_C * (1.0 + 3.0 * GELU_K * gate * gate)
        dgate_ref[...] = (dy * h * dgl).astype(BF16)
        g = _scan_up(_shift_up(a, 1, rows), dy * f["gl"])
        da = g * _shift_dn(h, 1, rows)
        iv = i * v
        dmult = g * iv
        di = g * mult * v
        dv = g * mult * i
        dlog_a = da * a - dmult * (a * a) / mult
        dr = dlog_a * (-LRU_C) * f["sp"]
        dsp = jnp.sum(dlog_a * (-LRU_C) * r, axis=0, keepdims=True)
        dlam_ref[...] = -dsp * jax.nn.sigmoid(-lam)
        dpa = dr * r * (1.0 - r)
        dpx = di * i * (1.0 - i)
        dpab = dpa.astype(BF16)
        dpxb = dpx.astype(BF16)
        dwa_ref[...] = _dot_tn(f["vb"], dpab)
        dwx_ref[...] = _dot_tn(f["vb"], dpxb)
        dba_ref[...] = jnp.sum(dpa, axis=0, keepdims=True)
        dbx_ref[...] = jnp.sum(dpx, axis=0, keepdims=True)
        dv = dv + _dot_nt(dpab, wa) + _dot_nt(dpxb, wx)
        dcb_ref[...] = jnp.sum(dv, axis=0, keepdims=True)
        du = cw[3:4] * dv
        dcw_ref[3:4, :] = jnp.sum(dv * u, axis=0, keepdims=True)
        for k in (1, 2, 3):
            du = du + cw[3 - k:4 - k] * _shift_up(dv, k, rows)
            dcw_ref[3 - k:4 - k, :] = jnp.sum(dv * _shift_dn(u, k, rows), axis=0, keepdims=True)
        du_ref[...] = du.astype(BF16)

    blk = pl.BlockSpec((t, HEAD), lambda h: (0, h))
    vec = pl.BlockSpec((1, HEAD), lambda h: (0, h))
    mat = pl.BlockSpec((None, HEAD, HEAD), lambda h: (h, 0, 0))
    return pl.pallas_call(
        body, grid=(LRU_HEADS,),
        in_specs=_lru_specs(t, j, POOL_W // HEAD, (POOL_W + LRU_W) // HEAD)
        + [pl.BlockSpec((t, HEAD), lambda h: (0, POOL_W // HEAD + h)), blk],
        out_specs=[blk, blk, pl.BlockSpec((4, HEAD), lambda h: (0, h)), vec, mat, vec, mat, vec, vec],
        out_shape=[S((t, LRU_W), BF16), S((t, LRU_W), BF16), S((4, LRU_W), F32), S((1, LRU_W), F32),
                   S((LRU_HEADS, HEAD, HEAD), F32), S((1, LRU_W), F32),
                   S((LRU_HEADS, HEAD, HEAD), F32), S((1, LRU_W), F32), S((1, LRU_W), F32)],
        compiler_params=_cp("parallel"), name="lru_bwd")(
            proj, proj, p["conv_w"], p["conv_b"], p["w_a"], p["b_a"], p["w_x"], p["b_x"], p["lam"], dycat, hstate)


def _rope(x, c, s):
    x1 = x[:, :ROPE // 2]
    x2 = x[:, ROPE // 2:]
    return jnp.concatenate([x1 * c - x2 * s, x1 * s + x2 * c], axis=-1)


def _rope_t(d, c, s):
    d1 = d[:, :ROPE // 2]
    d2 = d[:, ROPE // 2:]
    return jnp.concatenate([d1 * c + d2 * s, d2 * c - d1 * s], axis=-1)


def _rope_tables(pos2, inv_freq):
    t = pos2.shape[0]

    def body(p_ref, f_ref, c_ref, s_ref):
        ang = p_ref[...].astype(F32) * f_ref[...]
        c_ref[...] = jnp.cos(ang)
        s_ref[...] = jnp.sin(ang)

    return pl.pallas_call(body, out_shape=[S((t, ROPE // 2), F32), S((t, ROPE // 2), F32)],
                          name="rope_tables")(pos2, inv_freq)


def _down_norm(xb, wdown_g, gq3, gkv3, cos, sin, j):
    t = xb.shape[0]
    bm = _row_tile(t)

    def body(x_ref, w_ref, gq_ref, gkv_ref, c_ref, s_ref, down_ref, cq_ref, ckv_ref, kpe_ref):
        w = w_ref[...].reshape(D, ODD_IN)
        down = _dot(x_ref[...], w)
        down_ref[...] = down
        q = down[:, :Q_RANK]
        cq_ref[...] = (q * lax.rsqrt(jnp.mean(q * q, axis=-1, keepdims=True) + RMS_EPS) * gq_ref[...]).astype(BF16)
        kv = down[:, Q_RANK:Q_RANK + KV_RANK]
        ckv_ref[...] = (kv * lax.rsqrt(jnp.mean(kv * kv, axis=-1, keepdims=True) + RMS_EPS)
                        * gkv_ref[...]).astype(BF16)
        kpe_ref[...] = _rope(down[:, Q_RANK + KV_RANK:], c_ref[...], s_ref[...])

    row = lambda n: pl.BlockSpec((bm, n), lambda i: (i, 0))
    return pl.pallas_call(
        body, grid=(t // bm,),
        in_specs=[row(D), _full((N_DEV, D // N_DEV, ODD_IN)),
                  pl.BlockSpec((None, 1, Q_RANK), lambda i: (j, 0, 0)),
                  pl.BlockSpec((None, 1, KV_RANK), lambda i: (j, 0, 0)), row(ROPE // 2), row(ROPE // 2)],
        out_specs=[row(ODD_IN), row(Q_RANK), row(KV_RANK), row(ROPE)],
        out_shape=[S((t, ODD_IN), F32), S((t, Q_RANK), BF16), S((t, KV_RANK), BF16), S((t, ROPE), F32)],
        compiler_params=_cp("parallel"), name="down_norm")(xb, wdown_g, gq3, gkv3, cos, sin)


def _q_tile(t, widest):
    return min(widest, t // 2)


def _attn_probs(q, k, qs):
    s = _dot_nt(q, k) * ATT_SCALE
    tq = q.shape[0]
    rows = lax.broadcasted_iota(jnp.int32, (tq, tq), 0)
    cols = lax.broadcasted_iota(jnp.int32, (tq, tq), 1)
    last = jnp.where(jnp.right_shift(cols, CHUNK_SHIFT) <= jnp.right_shift(rows, CHUNK_SHIFT), s[:, qs:], NEG)
    s = last if qs == 0 else jnp.concatenate([s[:, :qs], last], axis=1)
    e = jnp.exp(s - jnp.max(s, axis=-1, keepdims=True))
    return e / jnp.sum(e, axis=-1, keepdims=True)


def _head_qkv(cq, ckv, kpe, c, s, wq_ref, wkv_ref):
    q = jnp.concatenate([_dot(cq, wq_ref[:, :NOPE]), _rope(_dot(cq, wq_ref[:, NOPE:]), c, s)], axis=1).astype(BF16)
    k = jnp.concatenate([_dot(ckv, wkv_ref[:, :NOPE]), kpe], axis=1).astype(BF16)
    vv = _dot(ckv, wkv_ref[:, NOPE:]).astype(BF16)
    return q, k, vv


def _attn_in_specs(t):
    return [_full((t, Q_RANK)), _full((t, KV_RANK)), _full((t, ROPE)), _full((t, ROPE // 2)), _full((t, ROPE // 2)),
            pl.BlockSpec((None, Q_RANK, NOPE + ROPE), lambda h: (h, 0, 0)),
            pl.BlockSpec((None, KV_RANK, NOPE + VDIM), lambda h: (h, 0, 0)),
            pl.BlockSpec((None, VDIM, D), lambda h: (h, 0, 0))]


def _attn_fwd(cq, ckv, kpe, cos, sin, wqb_g, wkvb_g, wo_g):
    t = cq.shape[0]
    tq = _q_tile(t, 256)

    def body(cq_ref, ckv_ref, kpe_ref, c_ref, s_ref, wq_ref, wkv_ref, wo_ref, o_ref, mix_ref):
        q, k, vv = _head_qkv(cq_ref[...], ckv_ref[...], kpe_ref[...], c_ref[...], s_ref[...], wq_ref, wkv_ref)
        for qs in range(0, t, tq):
            ke = qs + tq
            p = _attn_probs(q[qs:ke], k[:ke], qs)
            o_ref[qs:ke, :] = _dot(p.astype(BF16), vv[:ke]).astype(BF16)
        c = _dot(o_ref[...], wo_ref[...])

        @pl.when(pl.program_id(0) == 0)
        def _():
            mix_ref[...] = c

        @pl.when(pl.program_id(0) > 0)
        def _():
            mix_ref[...] += c

    return pl.pallas_call(
        body, grid=(MLA_HEADS,), in_specs=_attn_in_specs(t),
        out_specs=[pl.BlockSpec((None, t, VDIM), lambda h: (h, 0, 0)), _full((t, D))],
        out_shape=[S((MLA_HEADS, t, VDIM), BF16), S((t, D), F32)],
        compiler_params=_cp("arbitrary"), name="attn_fwd")(cq, ckv, kpe, cos, sin, wqb_g, wkvb_g, wo_g)


def _attn_bwd(cq, ckv, kpe, cos, sin, wqb_g, wkvb_g, wo_g, o, dzb):
    t = cq.shape[0]
    tq = _q_tile(t, 512)

    def body(cq_ref, ckv_ref, kpe_ref, c_ref, s_ref, wq_ref, wkv_ref, wo_ref, o_ref, dz_ref,
             dwo_ref, dwq_ref, dwkv_ref, dcq_ref, dckv_ref, dkpe_ref, dkt_s, dvt_s, dq_s):
        cqv = cq_ref[...]
        ckvv = ckv_ref[...]
        c = c_ref[...]
        s = s_ref[...]
        q, k, vv = _head_qkv(cqv, ckvv, kpe_ref[...], c, s, wq_ref, wkv_ref)
        dzv = dz_ref[...]
        dwo_ref[...] = _dot_tn(o_ref[...], dzv).astype(BF16)
        do = _dot_nt(dzv, wo_ref[...]).astype(BF16)
        dkt_s[...] = jnp.zeros_like(dkt_s)
        dvt_s[...] = jnp.zeros_like(dvt_s)
        for qs in range(0, t, tq):
            ke = qs + tq
            p = _attn_probs(q[qs:ke], k[:ke], qs)
            dp = _dot_nt(do[qs:ke], vv[:ke])
            ds = (p * (dp - jnp.sum(p * dp, axis=-1, keepdims=True)) * ATT_SCALE).astype(BF16)
            dq_s[qs:ke, :] = _dot(ds, k[:ke])
            dkt_s[0:NOPE + ROPE, 0:ke] += _dot_tn(q[qs:ke], ds)
            dvt_s[:, 0:ke] += _dot_tn(do[qs:ke], p.astype(BF16))
        dk = dkt_s[...].T
        dqn = dq_s[:, :NOPE].astype(BF16)
        dqp = _rope_t(dq_s[:, NOPE:], c, s).astype(BF16)
        dkn = dk[:, :NOPE].astype(BF16)
        dkp = dk[:, NOPE:NOPE + ROPE]
        dvv = dvt_s[...].T.astype(BF16)
        dwq_ref[:, :NOPE] = _dot_tn(cqv, dqn).astype(BF16)
        dwq_ref[:, NOPE:] = _dot_tn(cqv, dqp).astype(BF16)
        dwkv_ref[:, :NOPE] = _dot_tn(ckvv, dkn).astype(BF16)
        dwkv_ref[:, NOPE:] = _dot_tn(ckvv, dvv).astype(BF16)
        dcq = _dot_nt(dqn, wq_ref[:, :NOPE]) + _dot_nt(dqp, wq_ref[:, NOPE:])
        dckv = _dot_nt(dkn, wkv_ref[:, :NOPE]) + _dot_nt(dvv, wkv_ref[:, NOPE:])

        @pl.when(pl.program_id(0) == 0)
        def _():
            dcq_ref[...] = dcq
            dckv_ref[...] = dckv
            dkpe_ref[...] = dkp

        @pl.when(pl.program_id(0) > 0)
        def _():
            dcq_ref[...] += dcq
            dckv_ref[...] += dckv
            dkpe_ref[...] += dkp

    per_head = lambda a, b: pl.BlockSpec((None, a, b), lambda h: (h, 0, 0))
    return pl.pallas_call(
        body, grid=(MLA_HEADS,),
        in_specs=_attn_in_specs(t) + [per_head(t, VDIM), _full((t, D))],
        out_specs=[per_head(VDIM, D), per_head(Q_RANK, NOPE + ROPE), per_head(KV_RANK, NOPE + VDIM),
                   _full((t, Q_RANK)), _full((t, KV_RANK)), _full((t, ROPE))],
        out_shape=[S((MLA_HEADS, VDIM, D), BF16), S((MLA_HEADS, Q_RANK, NOPE + ROPE), BF16),
                   S((MLA_HEADS, KV_RANK, NOPE + VDIM), BF16),
                   S((t, Q_RANK), F32), S((t, KV_RANK), F32), S((t, ROPE), F32)],
        scratch_shapes=[pltpu.VMEM((2 * NOPE, t), F32), pltpu.VMEM((VDIM, t), F32),
                        pltpu.VMEM((t, NOPE + ROPE), F32)],
        compiler_params=_cp("arbitrary"), name="attn_bwd")(cq, ckv, kpe, cos, sin, wqb_g, wkvb_g, wo_g, o, dzb)


def _rms_bwd(down, dcq, dckv, dkpe, cos, sin, gq3, gkv3, j):
    t = down.shape[0]
    bm = _row_tile(t)

    def body(down_ref, dcq_ref, dckv_ref, dkpe_ref, c_ref, s_ref, gq_ref, gkv_ref, dd_ref, dgq_ref, dgkv_ref):
        @pl.when(pl.program_id(0) == 0)
        def _():
            dgq_ref[...] = jnp.zeros_like(dgq_ref)
            dgkv_ref[...] = jnp.zeros_like(dgkv_ref)

        def rms_b(x, dy, g):
            rstd = lax.rsqrt(jnp.mean(x * x, axis=-1, keepdims=True) + RMS_EPS)
            xh = x * rstd
            dyg = dy * g
            return rstd * (dyg - xh * jnp.mean(dyg * xh, axis=-1, keepdims=True)), jnp.sum(dy * xh, axis=0, keepdims=True)

        dq, dgq = rms_b(down_ref[:, :Q_RANK], dcq_ref[...], gq_ref[...])
        dkv, dgkv = rms_b(down_ref[:, Q_RANK:Q_RANK + KV_RANK], dckv_ref[...], gkv_ref[...])
        dgq_ref[...] += dgq
        dgkv_ref[...] += dgkv
        dd_ref[:, :Q_RANK] = dq.astype(BF16)
        dd_ref[:, Q_RANK:Q_RANK + KV_RANK] = dkv.astype(BF16)
        dd_ref[:, Q_RANK + KV_RANK:] = _rope_t(dkpe_ref[...], c_ref[...], s_ref[...]).astype(BF16)

    row = lambda n: pl.BlockSpec((bm, n), lambda i: (i, 0))
    return pl.pallas_call(
        body, grid=(t // bm,),
        in_specs=[row(ODD_IN), row(Q_RANK), row(KV_RANK), row(ROPE), row(ROPE // 2), row(ROPE // 2),
                  pl.BlockSpec((None, 1, Q_RANK), lambda i: (j, 0, 0)),
                  pl.BlockSpec((None, 1, KV_RANK), lambda i: (j, 0, 0))],
        out_specs=[row(ODD_IN), _full((1, Q_RANK)), _full((1, KV_RANK))],
        out_shape=[S((t, ODD_IN), BF16), S((1, Q_RANK), F32), S((1, KV_RANK), F32)],
        compiler_params=_cp("arbitrary"), name="rms_bwd")(down, dcq, dckv, dkpe, cos, sin, gq3, gkv3)


def _col_blocks(t, n, bn):
    return pl.BlockSpec((t, bn), lambda i: (0, i))


def _row_blocks(n, bm):
    return pl.BlockSpec((bm, n), lambda i: (i, 0))


def _local_step(x, pos2, tgt, small, weights_of, grads_done, start_dep=None, prefetch=None):
    t = x.shape[0]
    bm = min(512, t)
    inv_freq = (ROPE_THETA ** (-jnp.arange(0, ROPE, 2, dtype=F32) / ROPE)).reshape(1, ROPE // 2)
    cos, sin = _rope_tables(pos2, inv_freq)
    lru_p = {k: small[k] for k in ("conv_w", "conv_b", "w_a", "b_a", "w_x", "b_x", "lam")}

    saved = []
    y, yb = x, x
    for l in range(DEPTH):
        j = l // 2
        big = weights_of(l, 0, y)
        sv = dict(xb=yb, big=big)
        if l % 2 == 0:
            proj = _mm(yb, big["win_t"], mode="nt", grid=(EVEN_IN // 512,), a_spec=_full((t, D)),
                       b_spec=_row_blocks(D, 512), out_shape=S((t, EVEN_IN), F32),
                       out_spec=_col_blocks(t, EVEN_IN, 512), name="even_proj", dep=start_dep if l == 0 else None)
            y_lru, hstate = _lru_fwd(proj, lru_p, j)
            ycat = jnp.concatenate([_pool_fwd(proj, small["pool_w"], small["pool_scale"], j), y_lru], axis=1)
            big.update(weights_of(l, 1, ycat))
            z1, y1, y1b = _proj_resid_ln(y, ycat, big["wout2d"], small["ln_mix_g"], small["ln_mix_b"], l, "even_out")
            sv.update(proj=proj, ycat=ycat, hstate=hstate)
        else:
            down, cq, ckv, kpe = _down_norm(yb, big["wdown"], small["gq"], small["gkv"], cos, sin, j)
            o, mix = _attn_fwd(cq, ckv, kpe, cos, sin, big["wqb"], big["wkvb"], big["wo"])
            z1, y1, y1b = _resid_ln(y, mix, small["ln_mix_g"], small["ln_mix_b"], l, "resid_ln")
            sv.update(down=down, cq=cq, ckv=ckv, kpe=kpe, o=o)
        fetched = prefetch(l + 1, y1) if prefetch is not None and l + 1 < DEPTH else None
        z2, y, yb, act = _mlp_fwd(y1, y1b, big["w1"], big["w2"], small["ln_ffn_g"], small["ln_ffn_b"], l,
                                  dep=fetched)
        sv.update(z1=z1, y1b=y1b, z2=z2, act=act)
        saved.append(sv)

    dy, loss_tile = _loss_grad(y, tgt)

    g = {k: [None] * n for k, n in (("ln_mix_g", 4), ("ln_mix_b", 4), ("ln_ffn_g", 4), ("ln_ffn_b", 4),
                                    ("pool_w", 2), ("pool_scale", 2), ("conv_w", 2), ("conv_b", 2),
                                    ("w_a", 2), ("b_a", 2), ("w_x", 2), ("b_x", 2), ("lam", 2),
                                    ("gq", 2), ("gkv", 2))}
    dep = None
    for l in reversed(range(DEPTH)):
        j = l // 2
        sv = saved[l]
        big = sv["big"]
        dz2, dz2b, g["ln_ffn_g"][l], g["ln_ffn_b"][l] = _ln_bwd(dy, sv["z2"], small["ln_ffn_g"], l, "ln_bwd", dep=dep)
        act = sv["act"]
        dh, dff = _mlp_bwd_dh(act, dz2b, big["w1"], big["w2"])
        dw1 = _mm(sv["y1b"], dh, mode="tn", grid=(N_DEV,), a_spec=_full((t, D)),
                  b_spec=_col_blocks(t, D_FF, FF_BLK), out_shape=S((N_DEV, D, FF_BLK), BF16),
                  out_spec=pl.BlockSpec((None, D, FF_BLK), lambda i: (i, 0, 0)), name="mlp_dw1")
        dw2 = _mm(act, dz2b, mode="tn", grid=(N_DEV,), a_spec=_col_blocks(t, D_FF, FF_BLK),
                  b_spec=_full((t, D)), out_shape=S((N_DEV, FF_BLK, D), BF16),
                  out_spec=pl.BlockSpec((None, FF_BLK, D), lambda i: (i, 0, 0)), name="mlp_dw2")
        dep = grads_done(l, dict(w1=dw1, w2=dw2))
        dz1, dz1b, g["ln_mix_g"][l], g["ln_mix_b"][l] = _ln_bwd(dff, sv["z1"], small["ln_mix_g"], l, "ln_bwd_res",
                                                                 r=dz2, dep=dep)
        if l % 2 == 0:
            wout = big["wout2d"]
            dycat = _mm(dz1b, wout, mode="nt", grid=(EVEN_MIX // 512,), a_spec=_full((t, D)),
                        b_spec=_row_blocks(D, 512), out_shape=S((t, EVEN_MIX), F32),
                        out_spec=_col_blocks(t, EVEN_MIX, 512), name="even_dycat")
            dwout = _mm(sv["ycat"], dz1b, mode="tn", grid=(EVEN_MIX // 512,), a_spec=_col_blocks(t, EVEN_MIX, 512),
                        b_spec=_full((t, D)), out_shape=S((EVEN_MIX, D), BF16), out_spec=_row_blocks(D, 512),
                        name="even_dwout")
            du_pool, g["pool_w"][j], g["pool_scale"][j] = _pool_bwd(sv["proj"], dycat, small["pool_w"],
                                                                   small["pool_scale"], j)
            (du_lru, du_gate, g["conv_w"][j], g["conv_b"][j], g["w_a"][j], g["b_a"][j], g["w_x"][j], g["b_x"][j],
             g["lam"][j]) = _lru_bwd(sv["proj"], dycat, sv["hstate"], lru_p, j)
            dproj = jnp.concatenate([du_pool, du_lru, du_gate], axis=1)
            dep = grads_done(l, dict(win=_even_dwin(sv["xb"], dproj), wout=dwout.reshape(N_DEV, EVEN_MIX // N_DEV, D)))
            dy = _mm(dproj, big["win_t"], mode="nn", grid=(t // bm,), a_spec=_row_blocks(EVEN_IN, bm),
                     b_spec=_full((EVEN_IN, D)), out_shape=S((t, D), F32), out_spec=_row_blocks(D, bm),
                     add=dz1, add_spec=_row_blocks(D, bm), add_scale=ALPHA, name="even_dx", dep=dep)
        else:
            dwo, dwqb, dwkvb, dcq, dckv, dkpe = _attn_bwd(
                sv["cq"], sv["ckv"], sv["kpe"], cos, sin, big["wqb"], big["wkvb"], big["wo"], sv["o"], dz1b)
            ddown, g["gq"][j], g["gkv"][j] = _rms_bwd(sv["down"], dcq, dckv, dkpe, cos, sin, small["gq"],
                                                     small["gkv"], j)
            dwdown = _mm(sv["xb"], ddown, mode="tn", grid=(N_DEV,), a_spec=_col_blocks(t, D, D // N_DEV),
                         b_spec=_full((t, ODD_IN)), out_shape=S((N_DEV, D // N_DEV, ODD_IN), BF16),
                         out_spec=pl.BlockSpec((None, D // N_DEV, ODD_IN), lambda i: (i, 0, 0)),
                         name="odd_dwdown")
            dep = grads_done(l, dict(wdown=dwdown, wqb=dwqb, wkvb=dwkvb, wo=dwo))
            dy = _mm(ddown, big["wdown2d"], mode="nt", grid=(t // bm,), a_spec=_row_blocks(ODD_IN, bm),
                     b_spec=_full((D, ODD_IN)), out_shape=S((t, D), F32), out_spec=_row_blocks(D, bm),
                     add=dz1, add_spec=_row_blocks(D, bm), add_scale=ALPHA, name="odd_dx", dep=dep)
    return loss_tile[0, 0], dy, g


def _mesh_place():
    x, y, c = lax.axis_index("x"), lax.axis_index("y"), lax.axis_index("c")
    return x, y, c


def _peer(place, k):
    x, y, c = place
    return (1 - x if k & 4 else x, 1 - y if k & 2 else y, 1 - c if k & 1 else c)


def _index(place):
    x, y, c = place
    return 4 * x + 2 * y + c


ANY = pl.BlockSpec(memory_space=pl.ANY)


def _make_zones(shards, me, name, dtype=BF16):
    n = len(shards)

    def body(me_ref, *refs):
        for src, dst in zip(refs[:n], refs[n:]):
            dst[...] = src[...].astype(dtype)

    grid_spec = pltpu.PrefetchScalarGridSpec(
        num_scalar_prefetch=1, grid=(1,),
        in_specs=[pl.BlockSpec(s.shape, lambda i, me_ref: (0, 0)) for s in shards],
        out_specs=[pl.BlockSpec((None,) + s.shape, lambda i, me_ref: (me_ref[0], 0, 0)) for s in shards])
    return pl.pallas_call(body, grid_spec=grid_spec, out_shape=[S((N_DEV,) + s.shape, dtype) for s in shards],
                          compiler_params=_cp("arbitrary"), name=name)(me, *shards)


def _shard_rows_tile(a):
    return max(d for d in range(16, 257, 16) if a % d == 0)


HBM = pl.BlockSpec(memory_space=pltpu.HBM)
SEM = pl.BlockSpec(memory_space=pltpu.SEMAPHORE)
DATAFLOW = pltpu.SideEffectType.DATAFLOW_SIDE_EFFECTING


def _in_hbm(a):
    return pltpu.with_memory_space_constraint(a, pltpu.HBM)


def _gather_ici_copies(place, src, land, w):
    me = _index(place)
    return [(_peer(place, k), land.at[me], land.at[me]) for k in (1, 2, 4, 6)]


def _gather_d2d_copies(place, src, land, w):
    blocks = [_index(_peer(place, k)) for k in (2, 4, 6)]
    return [(_peer(place, 1), land.at[b], land.at[b]) for b in blocks]


GATHER_ICI = (4, _gather_ici_copies)
GATHER_D2D = (3, _gather_d2d_copies)


def _scatter_plan(layers):
    def copies(place, src, land, w):
        me = _index(place)
        mine = land.at[me] if layers[w] is None else land.at[me, layers[w]]
        return [(_peer(place, k), src.at[_index(_peer(place, k))], mine) for k in range(1, N_DEV)]
    return (N_DEV - 1, copies)


def _gather_all_copies(place, src, land, w):
    me = _index(place)
    return [(_peer(place, k), land.at[me], land.at[me]) for k in range(1, N_DEV)]


GATHER_ALL = (N_DEV - 1, _gather_all_copies)


def _sum_blocks(zone, part, me):
    r = part.shape[1]

    def body(me_ref, z_ref, p_ref, o_ref):
        acc = None
        for s in range(N_DEV):
            term = jnp.where(me_ref[0] == s, p_ref[...], z_ref[s])
            acc = term if acc is None else acc + term
        o_ref[...] = acc

    grid_spec = pltpu.PrefetchScalarGridSpec(
        num_scalar_prefetch=1, grid=(1,),
        in_specs=[pl.BlockSpec((N_DEV, r, 128), lambda i, me_ref: (0, 0, 0)),
                  pl.BlockSpec((None, r, 128), lambda i, me_ref: (me_ref[0], 0, 0))],
        out_specs=pl.BlockSpec((r, 128), lambda i, me_ref: (0, 0)))
    return pl.pallas_call(body, grid_spec=grid_spec, out_shape=S((r, 128), F32),
                          compiler_params=_cp("arbitrary"), name="sum_small")(me, zone, part)


def _exchange_start(srcs, lands, plan, name, after=()):
    ns, n = len(srcs), len(lands)
    n_in = ns + n + len(after)
    per, copies = plan

    def body(*refs):
        ins, land = refs[:ns], refs[ns:ns + n]
        send, recv = refs[n_in], refs[n_in + 1]
        token = refs[-1]
        place = _mesh_place()
        for i in range(per):
            for w in range(n):
                target, src, dst = copies(place, ins[w] if ns else None, land[w], w)[i]
                pltpu.make_async_remote_copy(src_ref=src, dst_ref=dst, send_sem=send.at[w * per + i],
                                             recv_sem=recv.at[w * per + i], device_id=target, device_id_type=MESH).start()
        token[...] = jnp.zeros_like(token)

    sems = pltpu.SemaphoreType.DMA((n * per,))
    thru = [pltpu.HBM(a.shape, a.dtype) for a in list(srcs) + list(lands)]
    out = pl.pallas_call(
        body, name=name, in_specs=[HBM] * (ns + n) + [ANY] * len(after),
        out_shape=(sems, sems, *thru, S((8, 128), F32)),
        out_specs=(SEM, SEM, *([HBM] * (ns + n)), pl.BlockSpec(memory_space=pltpu.VMEM)),
        input_output_aliases={i: 2 + i for i in range(ns + n)},
        compiler_params=pltpu.CompilerParams(has_side_effects=DATAFLOW),
    )(*[_in_hbm(a) for a in list(srcs) + list(lands)], *after)
    return out[0], out[1], list(out[2:2 + ns]), list(out[2 + ns:2 + ns + n]), out[-1]


def _exchange_wait(send, recv, srcs, lands, plan, after, name):
    ns, n = len(srcs), len(lands)
    per, copies = plan
    afters = tuple(after) if isinstance(after, (tuple, list)) else (after,)

    def body(*refs):
        ins, land = refs[:ns], refs[ns:ns + n]
        send_ref, recv_ref = refs[ns + n], refs[ns + n + 1]
        place = _mesh_place()
        for i in range(per):
            for w in range(n):
                target, src, dst = copies(place, ins[w] if ns else None, land[w], w)[i]
                cp = pltpu.make_async_remote_copy(src_ref=src, dst_ref=dst, send_sem=send_ref.at[w * per + i],
                                                  recv_sem=recv_ref.at[w * per + i], device_id=target,
                                                  device_id_type=MESH)
                cp.wait_send()
                cp.wait_recv()

    thru = [pltpu.HBM(a.shape, a.dtype) for a in list(srcs) + list(lands)]
    out = pl.pallas_call(
        body, name=name, in_specs=[HBM] * (ns + n) + [SEM, SEM] + [ANY] * len(afters),
        out_shape=tuple(thru), out_specs=tuple([HBM] * (ns + n)),
        input_output_aliases={i: i for i in range(ns + n)},
        compiler_params=pltpu.CompilerParams(has_side_effects=DATAFLOW),
    )(*srcs, *lands, send, recv, *afters)
    return list(out[:ns]), list(out[ns:])


def _adamw(w, g, m, v):
    m = ADAM_B1 * m + (1.0 - ADAM_B1) * g
    v = ADAM_B2 * v + (1.0 - ADAM_B2) * (g * g)
    m_hat = m / (1.0 - ADAM_B1 ** ADAM_STEP)
    v_hat = v / (1.0 - ADAM_B2 ** ADAM_STEP)
    return -ADAM_LR * (m_hat / (jnp.sqrt(v_hat) + ADAM_EPS) + ADAM_WD * w), m, v


def _adam_big(parts, own, me, w, m, v, name):
    nl, a, b = w.shape
    ta = _shard_rows_tile(a)

    def body(me_ref, p_ref, *refs):
        own_refs, (w_ref, m_ref, v_ref, g_ref, d_ref, mo_ref, vo_ref) = refs[:nl], refs[nl:]
        layer = pl.program_id(0)
        mine = own_refs[0][...]
        for k in range(1, nl):
            mine = jnp.where(layer == k, own_refs[k][...], mine)
        g = None
        for s in range(N_DEV):
            term = jnp.where(me_ref[0] == s, mine, p_ref[s]).astype(F32)
            g = term if g is None else g + term
        g_ref[...] = g
        d_ref[...], mo_ref[...], vo_ref[...] = _adamw(w_ref[...], g, m_ref[...], v_ref[...])

    blk = pl.BlockSpec((None, ta, b), lambda l, i, me_ref: (l, i, 0))

    def own_spec(k):
        return pl.BlockSpec((None, ta, b), lambda l, i, me_ref: (me_ref[0], jnp.where(l == k, i, 0), 0))

    grid_spec = pltpu.PrefetchScalarGridSpec(
        num_scalar_prefetch=1, grid=(nl, a // ta),
        in_specs=[pl.BlockSpec((N_DEV, None, ta, b), lambda l, i, me_ref: (0, l, i, 0))]
        + [own_spec(k) for k in range(nl)] + [blk, blk, blk],
        out_specs=[blk] * 4)
    return pl.pallas_call(body, grid_spec=grid_spec, out_shape=[S(w.shape, F32)] * 4,
                          compiler_params=_cp("arbitrary", "arbitrary"), name=name)(me, parts, *own, w, m, v)


def _adam_small(gs, ws, ms, vs):
    n = len(gs)

    def body(*refs):
        ins, outs = refs[:4 * n], refs[4 * n:]
        for i in range(n):
            g_ref, w_ref, m_ref, v_ref = (ins[k * n + i] for k in range(4))
            outs[i][...] = g_ref[...]
            outs[n + i][...], outs[2 * n + i][...], outs[3 * n + i][...] = _adamw(w_ref[...], g_ref[...], m_ref[...],
                                                                                 v_ref[...])

    out = pl.pallas_call(body, out_shape=[S(g.shape, F32) for g in gs] * 4, compiler_params=_cp(),
                         name="adam_small")(*gs, *ws, *ms, *vs)
    return out[:n], out[n:2 * n], out[2 * n:3 * n], out[3 * n:]


BIG = ("even_w_in", "even_w_out", "mla_w_down", "mla_w_qb", "mla_w_kvb", "mla_w_o", "mlp_w1", "mlp_w2")
BIG_KEY = dict(even_w_in="win", even_w_out="wout", mla_w_down="wdown", mla_w_qb="wqb", mla_w_kvb="wkvb",
               mla_w_o="wo", mlp_w1="w1", mlp_w2="w2")
SMALL = (("ln_mix_g", "ln_mix_g", None), ("ln_mix_b", "ln_mix_b", None), ("ln_ffn_g", "ln_ffn_g", None),
         ("ln_ffn_b", "ln_ffn_b", None), ("pool_w", "pool_w", None), ("pool_scale", "pool_scale", None),
         ("lru_conv_w", "conv_w", 2), ("lru_conv_b", "conv_b", None), ("lru_w_a", "w_a", None),
         ("lru_b_a", "b_a", None), ("lru_w_x", "w_x", None), ("lru_b_x", "b_x", None), ("lru_lambda", "lam", None),
         ("mla_q_norm_g", "gq", 1), ("mla_kv_norm_g", "gkv", 1))
WEIGHTS = ("ln_mix_g", "ln_mix_b", "ln_ffn_g", "ln_ffn_b", "even_w_in", "pool_w", "pool_scale", "lru_conv_w",
           "lru_conv_b", "lru_w_a", "lru_b_a", "lru_w_x", "lru_b_x", "lru_lambda", "even_w_out", "mla_w_down",
           "mla_q_norm_g", "mla_kv_norm_g", "mla_w_qb", "mla_w_kvb", "mla_w_o", "mlp_w1", "mlp_w2")


def _layer_weights(l):
    j = l // 2
    if l % 2 == 0:
        mixer = [("win", "even_w_in", j), ("wout", "even_w_out", j)]
    else:
        mixer = [("wdown", "mla_w_down", j), ("wqb", "mla_w_qb", j), ("wkvb", "mla_w_kvb", j), ("wo", "mla_w_o", j)]
    return mixer + [("w1", "mlp_w1", l), ("w2", "mlp_w2", l)]


def _pack(arrays, multiple):
    flat = jnp.concatenate([a.reshape(-1) for a in arrays])
    pad = (-flat.shape[0]) % multiple
    return jnp.pad(flat, (0, pad))


def _unpack(flat, shapes):
    out, at = [], 0
    for shp in shapes:
        n = 1
        for s in shp:
            n *= s
        out.append(flat[at:at + n].reshape(shp))
        at += n
    return out


def _global_shape(local_shape, axis):
    if axis is None:
        return tuple(local_shape)
    return tuple(s * N_DEV if i == axis else s for i, s in enumerate(local_shape))


def _step(x, positions, tgt, w, m, v):
    t = x.shape[1]
    me = _index(_mesh_place())

    chunk = N_DEV * 8 * 128
    me_arr = me.astype(jnp.int32).reshape(1)

    lanes = lambda a: jnp.pad(a, ((0, 0), (0, 128 - a.shape[1])))
    mine_packed = jnp.concatenate([w["lru_conv_w"].reshape(8, HEAD), lanes(w["mla_q_norm_g"]),
                                   lanes(w["mla_kv_norm_g"]), jnp.zeros((4, 128), F32)])
    g_send, g_recv, _, g_land, token = _exchange_start([], _make_zones([mine_packed], me_arr, "zones_small", F32),
                                                       GATHER_ALL, "small_params_start")

    def keys_of(l, part):
        keys = [key for key, _, _ in _layer_weights(l)]
        if l == 0:
            return keys[:1] if part == 0 else keys[1:]
        return keys if part == 0 else []

    shard_of = {(l, key): (w[name][i].T if key == "win" else w[name][i])
                for l in range(DEPTH) for key, name, i in _layer_weights(l)}
    flights, after = {}, (token,)
    for l in range(DEPTH):
        for part in (0, 1):
            if keys_of(l, part):
                zones = _make_zones([shard_of[l, key] for key in keys_of(l, part)], me_arr, "zones_%d_%d" % (l, part))
                send, recv, _, lands, token = _exchange_start([], zones, GATHER_ICI, "gather_start_%d_%d" % (l, part),
                                                              after=after)
                flights[l, part] = (send, recv, [], lands)
                after = (token,)

    _, g_land = _exchange_wait(g_send, g_recv, [], g_land, GATHER_ALL, token, "small_params_wait")
    rows_first = g_land[0].transpose(1, 0, 2)
    q_shard, kv_shard = w["mla_q_norm_g"].shape[1], w["mla_kv_norm_g"].shape[1]
    full = dict(lru_conv_w=rows_first[:8].reshape(2, 4, LRU_W),
                mla_q_norm_g=rows_first[8:10, :, :q_shard].reshape(2, Q_RANK),
                mla_kv_norm_g=rows_first[10:12, :, :kv_shard].reshape(2, KV_RANK))

    passing = {}

    def pass_on(l, part, after):
        tag = "%d_%d" % (l, part)
        _, lands = _exchange_wait(*flights[l, part], GATHER_ICI, after, "gather_wait_" + tag)
        send, recv, _, lands, token = _exchange_start([], lands, GATHER_D2D, "gather_pass_" + tag)
        passing[l, part] = (send, recv, [], lands)
        return token

    def early_pass(l, after):
        return pass_on(l, 0, after) if l >= 2 else None

    def weights_of(l, part, after):
        keys = keys_of(l, part)
        if keys:
            if (l, part) not in passing:
                pass_on(l, part, after)
            _, arrays = _exchange_wait(*passing[l, part], GATHER_D2D, after, "gather_pass_wait_%d_%d" % (l, part))
        big = dict(zip(keys, arrays)) if keys else {}
        if "win" in big:
            big["win_t"] = big["win"].reshape(EVEN_IN, D)
        if "wout" in big:
            big["wout2d"] = big["wout"].reshape(EVEN_MIX, D)
        if "wdown" in big:
            big["wdown2d"] = big["wdown"].reshape(D, ODD_IN)
        return big

    zone = {name: lax.empty((N_DEV,) + w[name].shape, BF16) for name in BIG}
    name_of = {key: name for name, key in BIG_KEY.items()}
    sent, last_token = [], [None]

    def grads_done(l, grads):
        keys = list(grads)
        index = {key: i for key, _, i in _layer_weights(l)}
        layers = [index[key] for key in keys]
        send, recv, srcs, lands, tok = _exchange_start([grads[k] for k in keys], [zone[name_of[k]] for k in keys],
                                                       _scatter_plan(layers), "scatter_start_%d_%s" % (l, keys[0]))
        for k, land in zip(keys, lands):
            zone[name_of[k]] = land
        sent.append((send, recv, srcs, keys, layers))
        last_token[0] = tok
        return tok

    row3 = lambda a: a.reshape(a.shape[0], 1, a.shape[1])
    small = dict(ln_mix_g=row3(w["ln_mix_g"]), ln_mix_b=row3(w["ln_mix_b"]), ln_ffn_g=row3(w["ln_ffn_g"]),
                 ln_ffn_b=row3(w["ln_ffn_b"]), pool_w=w["pool_w"], pool_scale=row3(w["pool_scale"]),
                 conv_w=full["lru_conv_w"], conv_b=row3(w["lru_conv_b"]), w_a=w["lru_w_a"], b_a=row3(w["lru_b_a"]),
                 w_x=w["lru_w_x"], b_x=row3(w["lru_b_x"]), lam=row3(w["lru_lambda"]),
                 gq=row3(full["mla_q_norm_g"]), gkv=row3(full["mla_kv_norm_g"]))

    loss_part, grad_x, g = _local_step(x[0], positions.reshape(t, 1), tgt[0], small, weights_of, grads_done,
                                       start_dep=token, prefetch=early_pass)

    own = {name: [None] * w[name].shape[0] for name in BIG}
    me_arr = me.astype(jnp.int32).reshape(1)
    out = {}
    local_g = [jnp.stack(g[key]).reshape(_global_shape(w[name].shape, axis)) for name, key, axis in SMALL]
    local_g.append(loss_part.reshape(1))
    part = _pack(local_g, chunk).reshape(N_DEV, -1, 128)
    small_plan = _scatter_plan([None])
    s_send, s_recv, s_src, s_land, after = _exchange_start([part], [lax.empty(part.shape, F32)], small_plan,
                                                           "small_scatter_start", after=(last_token[0],))
    for n_flight, (send, recv, srcs, keys, layers) in enumerate(sent):
        if n_flight == len(sent) - 1:
            for name in BIG:
                if BIG_KEY[name] not in keys:
                    out[name] = _adam_big(zone[name], own[name], me_arr, w[name], m[name], v[name], "adam_" + name)
            s_src, s_land = _exchange_wait(s_send, s_recv, s_src, s_land, small_plan,
                                           [grad_x] + [o[0] for o in out.values()], "small_scatter_wait")
            chunk_sum = _sum_blocks(s_land[0], s_src[0], me_arr)
            r_zone = lax.dynamic_update_slice_in_dim(lax.empty(part.shape, F32), chunk_sum[None], me, 0)
            r_send, r_recv, _, r_land, after = _exchange_start([], [r_zone], GATHER_ALL, "small_gather_start")
        srcs, lands = _exchange_wait(send, recv, srcs, [zone[name_of[k]] for k in keys], _scatter_plan(layers),
                                     after, "scatter_wait_%d" % n_flight)
        for k, land, src, layer in zip(keys, lands, srcs, layers):
            zone[name_of[k]] = land
            own[name_of[k]][layer] = src
        after = lands[0]
    for name in BIG:
        if name not in out:
            out[name] = _adam_big(zone[name], own[name], me_arr, w[name], m[name], v[name], "adam_" + name)

    _, reduced = _exchange_wait(r_send, r_recv, [], r_land, GATHER_ALL, [out[name][0] for name in BIG],
                                "small_gather_wait")
    reduced = _unpack(reduced[0].reshape(-1), [a.shape for a in local_g])
    loss = reduced[-1][0]
    mine = [a if axis is None else lax.dynamic_slice_in_dim(a, me * w[name].shape[axis], w[name].shape[axis], axis)
            for a, (name, _, axis) in zip(reduced, SMALL)]
    names = [name for name, _, _ in SMALL]
    as_2d = lambda a: a.reshape(-1, a.shape[-1])
    new = _adam_small([as_2d(a) for a in mine], *([as_2d(src[name]) for name in names] for src in (w, m, v)))
    for i, name in enumerate(names):
        out[name] = tuple(part[i].reshape(w[name].shape) for part in new)

    return (loss, grad_x[None]) + tuple(out[name][i] for i in range(4) for name in WEIGHTS)


def kernel(x, positions, ln_mix_g, ln_mix_b, ln_ffn_g, ln_ffn_b, even_w_in, pool_w, pool_scale, lru_conv_w, lru_conv_b, lru_w_a, lru_b_a, lru_w_x, lru_b_x, lru_lambda, even_w_out, mla_w_down, mla_q_norm_g, mla_kv_norm_g, mla_w_qb, mla_w_kvb, mla_w_o, mlp_w1, mlp_w2, loss_target, m_ln_mix_g, m_ln_mix_b, m_ln_ffn_g, m_ln_ffn_b, m_even_w_in, m_pool_w, m_pool_scale, m_lru_conv_w, m_lru_conv_b, m_lru_w_a, m_lru_b_a, m_lru_w_x, m_lru_b_x, m_lru_lambda, m_even_w_out, m_mla_w_down, m_mla_q_norm_g, m_mla_kv_norm_g, m_mla_w_qb, m_mla_w_kvb, m_mla_w_o, m_mlp_w1, m_mlp_w2, v_ln_mix_g, v_ln_mix_b, v_ln_ffn_g, v_ln_ffn_b, v_even_w_in, v_pool_w, v_pool_scale, v_lru_conv_w, v_lru_conv_b, v_lru_w_a, v_lru_b_a, v_lru_w_x, v_lru_b_x, v_lru_lambda, v_even_w_out, v_mla_w_down, v_mla_q_norm_g, v_mla_kv_norm_g, v_mla_w_qb, v_mla_w_kvb, v_mla_w_o, v_mlp_w1, v_mlp_w2):
    w = dict(zip(WEIGHTS, (ln_mix_g, ln_mix_b, ln_ffn_g, ln_ffn_b, even_w_in, pool_w, pool_scale, lru_conv_w,
                           lru_conv_b, lru_w_a, lru_b_a, lru_w_x, lru_b_x, lru_lambda, even_w_out, mla_w_down,
                           mla_q_norm_g, mla_kv_norm_g, mla_w_qb, mla_w_kvb, mla_w_o, mlp_w1, mlp_w2)))
    m = dict(zip(WEIGHTS, (m_ln_mix_g, m_ln_mix_b, m_ln_ffn_g, m_ln_ffn_b, m_even_w_in, m_pool_w, m_pool_scale,
                           m_lru_conv_w, m_lru_conv_b, m_lru_w_a, m_lru_b_a, m_lru_w_x, m_lru_b_x, m_lru_lambda,
                           m_even_w_out, m_mla_w_down, m_mla_q_norm_g, m_mla_kv_norm_g, m_mla_w_qb, m_mla_w_kvb,
                           m_mla_w_o, m_mlp_w1, m_mlp_w2)))
    v = dict(zip(WEIGHTS, (v_ln_mix_g, v_ln_mix_b, v_ln_ffn_g, v_ln_ffn_b, v_even_w_in, v_pool_w, v_pool_scale,
                           v_lru_conv_w, v_lru_conv_b, v_lru_w_a, v_lru_b_a, v_lru_w_x, v_lru_b_x, v_lru_lambda,
                           v_even_w_out, v_mla_w_down, v_mla_q_norm_g, v_mla_kv_norm_g, v_mla_w_qb, v_mla_w_kvb,
                           v_mla_w_o, v_mlp_w1, v_mlp_w2)))
    return _step(x, positions, loss_target, w, m, v)
```

```python
import jax
import jax.numpy as jnp
from jax import lax
from jax.experimental import pallas as pl
from jax.experimental.pallas import tpu as pltpu

F32 = jnp.float32
BF16 = jnp.bfloat16
S = jax.ShapeDtypeStruct

D = 1024
DEPTH = 4
N_DEV = 8
CHUNK_SHIFT = 6
POOL_WINDOWS = (2, 4, 8, 16)
POOL_W = 512
LRU_W = 1024
LRU_HEADS = 8
HEAD = 128
LRU_C = 8.0
EVEN_IN = 2560
EVEN_MIX = 1536
MLA_HEADS = 8
NOPE = 128
ROPE = 64
VDIM = 128
Q_RANK = 384
KV_RANK = 256
ODD_IN = 704
D_FF = 4096
FF_BLK = D_FF // N_DEV
ROPE_THETA = 10000.0
ALPHA = (2 * DEPTH) ** 0.25
LN_EPS = 1e-5
RMS_EPS = 1e-6
ATT_SCALE = (NOPE + ROPE) ** -0.5
NEG = float(jnp.finfo(jnp.float32).min)
ADAM_LR = 0.001
ADAM_B1 = 0.9
ADAM_B2 = 0.999
ADAM_EPS = 1e-08
ADAM_WD = 0.01
ADAM_STEP = 10
V7X_VMEM_BYTES = 64 * 1024 * 1024
VMEM_LIMIT = V7X_VMEM_BYTES - 8 * 1024 * 1024
MESH = pl.DeviceIdType.MESH


def _cp(*sem):
    return pltpu.CompilerParams(dimension_semantics=sem if sem else None, vmem_limit_bytes=VMEM_LIMIT)


def _dot(a, b):
    return jnp.dot(a, b, preferred_element_type=F32)


def _dot_nt(a, b):
    return lax.dot_general(a, b, (((1,), (1,)), ((), ())), preferred_element_type=F32)


def _dot_tn(a, b):
    return lax.dot_general(a, b, (((0,), (0,)), ((), ())), preferred_element_type=F32)


def _full(shape):
    return pl.BlockSpec(shape, lambda *_: (0,) * len(shape))


def _mm(a, b, *, mode, grid, a_spec, b_spec, out_shape, out_spec, name, add=None, add_spec=None, add_scale=1.0,
        dep=None):
    dot = {"nn": _dot, "nt": _dot_nt, "tn": _dot_tn}[mode]

    def body(*refs):
        a_ref, b_ref, o_ref = refs[0], refs[1], refs[-1]
        acc = dot(a_ref[...].astype(BF16), b_ref[...].astype(BF16))
        if add is not None:
            acc = acc + add_scale * refs[2][...]
        o_ref[...] = acc.astype(o_ref.dtype)

    ops = [a, b] if add is None else [a, b, add]
    specs = [a_spec, b_spec] if add is None else [a_spec, b_spec, add_spec]
    if dep is not None:
        ops.append(dep)
        specs.append(pl.BlockSpec(memory_space=pl.ANY))
    return pl.pallas_call(body, grid=grid, in_specs=specs, out_specs=out_spec, out_shape=out_shape,
                          compiler_params=_cp(*(("parallel",) * len(grid))), name=name)(*ops)


def _even_dwin(xb, dproj):
    shard = EVEN_IN // N_DEV

    def body(x_ref, dp_ref, o_ref):
        xv = x_ref[...].astype(BF16)
        for d in range(N_DEV):
            o_ref[d] = _dot_tn(xv, dp_ref[:, d * shard:(d + 1) * shard]).astype(BF16)

    return pl.pallas_call(body, out_shape=S((N_DEV, D, shard), BF16), compiler_params=_cp(), name="even_dwin")(xb, dproj)


def _ln_stats(z):
    mu = jnp.mean(z, axis=-1, keepdims=True)
    zc = z - mu
    var = jnp.mean(zc * zc, axis=-1, keepdims=True)
    rstd = lax.rsqrt(var + LN_EPS)
    return zc * rstd, rstd


def _row_tile(t):
    return min(1024, t)


def _resid_ln(x, mix, g3, b3, l, name):
    t = x.shape[0]
    bm = _row_tile(t)

    def body(x_ref, m_ref, g_ref, b_ref, z_ref, y_ref, yb_ref):
        z = ALPHA * x_ref[...] + m_ref[...]
        xh, _ = _ln_stats(z)
        y = xh * g_ref[...] + b_ref[...]
        z_ref[...] = z
        y_ref[...] = y
        yb_ref[...] = y.astype(BF16)

    row = pl.BlockSpec((bm, D), lambda i: (i, 0))
    vec = pl.BlockSpec((None, 1, D), lambda i: (l, 0, 0))
    return pl.pallas_call(body, grid=(t // bm,), in_specs=[row, row, vec, vec], out_specs=[row, row, row],
                          out_shape=[S((t, D), F32), S((t, D), F32), S((t, D), BF16)],
                          compiler_params=_cp("parallel"), name=name)(x, mix, g3, b3)


def _proj_resid_ln(x, a, wmat, g3, b3, l, name):
    t, k = a.shape
    bm = _row_tile(t)

    def body(x_ref, a_ref, w_ref, g_ref, b_ref, z_ref, y_ref, yb_ref):
        z = ALPHA * x_ref[...] + _dot(a_ref[...], w_ref[...])
        xh, _ = _ln_stats(z)
        y = xh * g_ref[...] + b_ref[...]
        z_ref[...] = z
        y_ref[...] = y
        yb_ref[...] = y.astype(BF16)

    row = pl.BlockSpec((bm, D), lambda i: (i, 0))
    vec = pl.BlockSpec((None, 1, D), lambda i: (l, 0, 0))
    return pl.pallas_call(body, grid=(t // bm,),
                          in_specs=[row, pl.BlockSpec((bm, k), lambda i: (i, 0)), _full((k, D)), vec, vec],
                          out_specs=[row, row, row], out_shape=[S((t, D), F32), S((t, D), F32), S((t, D), BF16)],
                          compiler_params=_cp("parallel"), name=name)(x, a, wmat, g3, b3)


def _ln_bwd(d, z, g3, l, name, r=None, dep=None):
    t = z.shape[0]
    bm = _row_tile(t)

    def body(*refs):
        refs = list(refs)
        d_ref = refs.pop(0)
        dy = d_ref[...]
        if r is not None:
            dy = dy + ALPHA * refs.pop(0)[...]
        z_ref, g_ref = refs.pop(0), refs.pop(0)
        if dep is not None:
            refs.pop(0)
        dz_ref, dzb_ref, dg_ref, db_ref = refs
        xh, rstd = _ln_stats(z_ref[...])
        dyg = dy * g_ref[...]
        m1 = jnp.mean(dyg, axis=-1, keepdims=True)
        m2 = jnp.mean(dyg * xh, axis=-1, keepdims=True)
        dz = rstd * (dyg - m1 - xh * m2)
        dz_ref[...] = dz
        dzb_ref[...] = dz.astype(BF16)

        @pl.when(pl.program_id(0) == 0)
        def _():
            dg_ref[...] = jnp.zeros_like(dg_ref)
            db_ref[...] = jnp.zeros_like(db_ref)

        dg_ref[...] += jnp.sum(dy * xh, axis=0, keepdims=True)
        db_ref[...] += jnp.sum(dy, axis=0, keepdims=True)

    row = pl.BlockSpec((bm, D), lambda i: (i, 0))
    vec = pl.BlockSpec((None, 1, D), lambda i: (l, 0, 0))
    acc = pl.BlockSpec((1, D), lambda i: (0, 0))
    ops = [d, z, g3] if r is None else [d, r, z, g3]
    specs = [row, row, vec] if r is None else [row, row, row, vec]
    if dep is not None:
        ops.append(dep)
        specs.append(_full(dep.shape))
    return pl.pallas_call(body, grid=(t // bm,), in_specs=specs, out_specs=[row, row, acc, acc],
                          out_shape=[S((t, D), F32), S((t, D), BF16), S((1, D), F32), S((1, D), F32)],
                          compiler_params=_cp("arbitrary"), name=name)(*ops)


def _loss_grad(y, tgt):
    t = y.shape[0]
    bm = _row_tile(t)

    def body(y_ref, t_ref, dy_ref, loss_ref, acc_ref):
        i = pl.program_id(0)
        e = y_ref[...] - t_ref[...]
        dy_ref[...] = e * (1.0 / D)

        @pl.when(i == 0)
        def _():
            acc_ref[...] = jnp.zeros_like(acc_ref)

        acc_ref[...] += jnp.sum(e * e, axis=0, keepdims=True)

        @pl.when(i == pl.num_programs(0) - 1)
        def _():
            loss_ref[...] = jnp.full(loss_ref.shape, (0.5 / D) * jnp.sum(acc_ref[...]), F32)

    row = pl.BlockSpec((bm, D), lambda i: (i, 0))
    return pl.pallas_call(body, grid=(t // bm,), in_specs=[row, row],
                          out_specs=[row, pl.BlockSpec((1, 128), lambda i: (0, 0))],
                          out_shape=[S((t, D), F32), S((1, 128), F32)],
                          scratch_shapes=[pltpu.VMEM((1, D), F32)],
                          compiler_params=_cp("arbitrary"), name="loss_grad")(y, tgt)


def _mlp_row_tile(t):
    return min(1024, t)


MLP_ROW_PARTS = 2


def _row_parts(bm):
    step = bm // MLP_ROW_PARTS
    return [slice(k * step, (k + 1) * step) for k in range(MLP_ROW_PARTS)]


def _mlp_fwd(y, yb, w1g, w2g, g3, b3, l, dep=None):
    t = yb.shape[0]
    bm = _mlp_row_tile(t)

    def body(*refs):
        y_ref, yb_ref, w1_ref, w2_ref, g_ref, b_ref = refs[:6]
        z_ref, o_ref, ob_ref, act_ref, acc_ref = refs[-5:]
        j = pl.program_id(1)

        @pl.when(j == 0)
        def _():
            acc_ref[...] = jnp.zeros_like(acc_ref)

        for rows in _row_parts(bm):
            h = jnp.maximum(_dot(yb_ref[rows, :], w1_ref[...]), 0.0)
            act = (h * h).astype(BF16)
            act_ref[rows, :] = act
            acc_ref[rows, :] += _dot(act, w2_ref[...])

        @pl.when(j == N_DEV - 1)
        def _():
            z = ALPHA * y_ref[...] + acc_ref[...]
            xh, _ = _ln_stats(z)
            out = xh * g_ref[...] + b_ref[...]
            z_ref[...] = z
            o_ref[...] = out
            ob_ref[...] = out.astype(BF16)

    row = pl.BlockSpec((bm, D), lambda i, j: (i, 0))
    vec = pl.BlockSpec((None, 1, D), lambda i, j: (l, 0, 0))
    deps = [] if dep is None else [dep]
    return pl.pallas_call(
        body, grid=(t // bm, N_DEV),
        in_specs=[row, row, pl.BlockSpec((None, D, FF_BLK), lambda i, j: (j, 0, 0)),
                  pl.BlockSpec((None, FF_BLK, D), lambda i, j: (j, 0, 0)), vec, vec] + [ANY] * len(deps),
        out_specs=[row, row, row, pl.BlockSpec((bm, FF_BLK), lambda i, j: (i, j))],
        out_shape=[S((t, D), F32), S((t, D), F32), S((t, D), BF16), S((t, D_FF), BF16)],
        scratch_shapes=[pltpu.VMEM((bm, D), F32)],
        compiler_params=_cp("parallel", "arbitrary"), name="mlp_fwd")(y, yb, w1g, w2g, g3, b3, *deps)


def _mlp_bwd_dh(act, dzb, w1g, w2g):
    t = act.shape[0]
    bm = _mlp_row_tile(t)

    def body(a_ref, dz_ref, w1_ref, w2_ref, dh_ref, acc_ref):
        @pl.when(pl.program_id(1) == 0)
        def _():
            acc_ref[...] = jnp.zeros_like(acc_ref)

        for rows in _row_parts(bm):
            r = jnp.sqrt(a_ref[rows, :].astype(F32))
            dh = (_dot_nt(dz_ref[rows, :], w2_ref[...]) * (2.0 * r)).astype(BF16)
            dh_ref[rows, :] = dh
            acc_ref[rows, :] += _dot_nt(dh, w1_ref[...])

    row = pl.BlockSpec((bm, D), lambda i, j: (i, 0))
    hid = pl.BlockSpec((bm, FF_BLK), lambda i, j: (i, j))
    return pl.pallas_call(
        body, grid=(t // bm, N_DEV),
        in_specs=[hid, row,
                  pl.BlockSpec((None, D, FF_BLK), lambda i, j: (j, 0, 0)),
                  pl.BlockSpec((None, FF_BLK, D), lambda i, j: (j, 0, 0))],
        out_specs=[hid, row],
        out_shape=[S((t, D_FF), BF16), S((t, D), F32)],
        compiler_params=_cp("parallel", "arbitrary"), name="mlp_bwd_dh")(act, dzb, w1g, w2g)


F32_SUBLANES = 8


def _shift_dn(x, k, rows, fill=0.0):
    if k % F32_SUBLANES == 0:
        return jnp.concatenate([jnp.full((k,) + x.shape[1:], fill, x.dtype), x[:x.shape[0] - k]], axis=0)
    return jnp.where(rows >= k, pltpu.roll(x, k, 0), fill)


def _shift_up(x, k, rows, fill=0.0):
    t = x.shape[0]
    if k % F32_SUBLANES == 0:
        return jnp.concatenate([x[k:], jnp.full((k,) + x.shape[1:], fill, x.dtype)], axis=0)
    return jnp.where(rows < t - k, pltpu.roll(x, t - k, 0), fill)


def _scan_rows(a, b, shift):
    rows = lax.broadcasted_iota(jnp.int32, a.shape, 0)
    k = 1
    t = a.shape[0]
    while k < t:
        b = a * shift(b, k, rows) + b
        if 2 * k < t:
            a = a * shift(a, k, rows, 1.0)
        k *= 2
    return b


def _scan_dn(a, b):
    return _scan_rows(a, b, _shift_dn)


def _scan_up(a, b):
    return _scan_rows(a, b, _shift_up)


def _window_sum_dn(x, w, rows):
    k = 1
    while k < w:
        x = x + _shift_dn(x, k, rows)
        k *= 2
    return x


def _window_sum_up(x, w, rows):
    k = 1
    while k < w:
        x = x + _shift_up(x, k, rows)
        k *= 2
    return x


def _pool_diff(u, w, rows):
    inv_count = 1.0 / jnp.minimum(rows + 1, w).astype(F32)
    return _window_sum_dn(u, w, rows) * inv_count - u, inv_count


def _pool_fwd(proj, pool_w, pool_scale3, j):
    t = proj.shape[0]

    def body(u_ref, w_ref, s_ref, y_ref):
        rows = lax.broadcasted_iota(jnp.int32, (t, HEAD), 0)
        for g, w in enumerate(POOL_WINDOWS):
            cols = slice(g * HEAD, (g + 1) * HEAD)
            d, _ = _pool_diff(u_ref[:, cols], w, rows)
            y = _dot(d.astype(BF16), w_ref[g].astype(BF16)) * s_ref[:, cols]
            y_ref[:, cols] = y.astype(BF16)

    return pl.pallas_call(
        body, grid=(1,),
        in_specs=[pl.BlockSpec((t, POOL_W), lambda i: (0, 0)),
                  pl.BlockSpec((None, 4, HEAD, HEAD), lambda i: (j, 0, 0, 0)),
                  pl.BlockSpec((None, 1, POOL_W), lambda i: (j, 0, 0))],
        out_specs=pl.BlockSpec((t, POOL_W), lambda i: (0, 0)),
        out_shape=S((t, POOL_W), BF16), compiler_params=_cp("arbitrary"), name="pool_fwd")(proj, pool_w, pool_scale3)


def _pool_bwd(proj, dycat, pool_w, pool_scale3, j):
    t = proj.shape[0]

    def body(u_ref, dy_ref, w_ref, s_ref, du_ref, dw_ref, ds_ref):
        rows = lax.broadcasted_iota(jnp.int32, (t, HEAD), 0)
        for g, w in enumerate(POOL_WINDOWS):
            cols = slice(g * HEAD, (g + 1) * HEAD)
            d, inv_count = _pool_diff(u_ref[:, cols], w, rows)
            db = d.astype(BF16)
            wg = w_ref[g].astype(BF16)
            dy = dy_ref[:, cols]
            ds_ref[:, cols] = jnp.sum(dy * _dot(db, wg), axis=0, keepdims=True)
            dzz = (dy * s_ref[:, cols]).astype(BF16)
            dw_ref[g] = _dot_tn(db, dzz)
            dd = _dot_nt(dzz, wg)
            du_ref[:, cols] = (_window_sum_up(dd * inv_count, w, rows) - dd).astype(BF16)

    return pl.pallas_call(
        body, grid=(1,),
        in_specs=[pl.BlockSpec((t, POOL_W), lambda i: (0, 0)),
                  pl.BlockSpec((t, POOL_W), lambda i: (0, 0)),
                  pl.BlockSpec((None, 4, HEAD, HEAD), lambda i: (j, 0, 0, 0)),
                  pl.BlockSpec((None, 1, POOL_W), lambda i: (j, 0, 0))],
        out_specs=[pl.BlockSpec((t, POOL_W), lambda i: (0, 0)), _full((4, HEAD, HEAD)), _full((1, POOL_W))],
        out_shape=[S((t, POOL_W), BF16), S((4, HEAD, HEAD), F32), S((1, POOL_W), F32)],
        compiler_params=_cp("arbitrary"), name="pool_bwd")(proj, dycat, pool_w, pool_scale3)


GELU_C = 0.7978845608028654
GELU_K = 0.044715


def _gelu(x):
    th = jnp.tanh(GELU_C * (x + GELU_K * x * x * x))
    return 0.5 * x * (1.0 + th), th


def _lru_forward(u, gate, cw, cb, wa, ba, wx, bx, lam, rows, h=None):
    v = cw[3:4] * u + cw[2:3] * _shift_dn(u, 1, rows) + cw[1:2] * _shift_dn(u, 2, rows) \
        + cw[0:1] * _shift_dn(u, 3, rows) + cb
    vb = v.astype(BF16)
    r = jax.nn.sigmoid(_dot(vb, wa) + ba)
    i = jax.nn.sigmoid(_dot(vb, wx) + bx)
    sp = jnp.maximum(-lam, 0.0) + jnp.log1p(jnp.exp(-jnp.abs(lam)))
    log_a = (-LRU_C) * r * sp
    a = jnp.exp(log_a)
    one_m_a2 = -jnp.tanh(log_a) * (a * a + 1.0)
    mult = jnp.sqrt(one_m_a2)
    if h is None:
        h = _scan_dn(a, mult * (i * v))
    gl, th = _gelu(gate)
    return dict(v=v, vb=vb, r=r, i=i, sp=sp, a=a, mult=mult, h=h, gl=gl, th=th)


def _lru_specs(t, j, col0_u, col0_g):
    blk = lambda c0: pl.BlockSpec((t, HEAD), lambda h: (0, c0 + h))
    vec = pl.BlockSpec((None, 1, HEAD), lambda h: (j, 0, h))
    return [blk(col0_u), blk(col0_g),
            pl.BlockSpec((None, 4, HEAD), lambda h: (j, 0, h)), vec,
            pl.BlockSpec((None, None, HEAD, HEAD), lambda h: (j, h, 0, 0)), vec,
            pl.BlockSpec((None, None, HEAD, HEAD), lambda h: (j, h, 0, 0)), vec, vec]


def _lru_fwd(proj, p, j):
    t = proj.shape[0]

    def body(u_ref, g_ref, cw_ref, cb_ref, wa_ref, ba_ref, wx_ref, bx_ref, lam_ref, y_ref, h_ref):
        rows = lax.broadcasted_iota(jnp.int32, (t, HEAD), 0)
        f = _lru_forward(u_ref[...], g_ref[...], cw_ref[...], cb_ref[...], wa_ref[...].astype(BF16), ba_ref[...],
                         wx_ref[...].astype(BF16), bx_ref[...], lam_ref[...], rows)
        y_ref[...] = (f["h"] * f["gl"]).astype(BF16)
        h_ref[...] = f["h"]

    blk = pl.BlockSpec((t, HEAD), lambda h: (0, h))
    return pl.pallas_call(
        body, grid=(LRU_HEADS,), in_specs=_lru_specs(t, j, POOL_W // HEAD, (POOL_W + LRU_W) // HEAD),
        out_specs=[blk, blk], out_shape=[S((t, LRU_W), BF16), S((t, LRU_W), F32)],
        compiler_params=_cp("parallel"), name="lru_fwd")(
            proj, proj, p["conv_w"], p["conv_b"], p["w_a"], p["b_a"], p["w_x"], p["b_x"], p["lam"])


def _lru_bwd(proj, dycat, hstate, p, j):
    t = proj.shape[0]

    def body(u_ref, g_ref, cw_ref, cb_ref, wa_ref, ba_ref, wx_ref, bx_ref, lam_ref, dy_ref, h_ref,
             du_ref, dgate_ref, dcw_ref, dcb_ref, dwa_ref, dba_ref, dwx_ref, dbx_ref, dlam_ref):
        rows = lax.broadcasted_iota(jnp.int32, (t, HEAD), 0)
        u = u_ref[...]
        gate = g_ref[...]
        cw = cw_ref[...]
        wa = wa_ref[...].astype(BF16)
        wx = wx_ref[...].astype(BF16)
        lam = lam_ref[...]
        f = _lru_forward(u, gate, cw, cb_ref[...], wa, ba_ref[...], wx, bx_ref[...], lam, rows, h=h_ref[...])
        v, r, i, a, mult, h, th = f["v"], f["r"], f["i"], f["a"], f["mult"], f["h"], f["th"]
        dy = dy_ref[...]
        dgl = 0.5 * (1.0 + th) + 0.5 * gate * (1.0 - th * th) * GELU_C * (1.0 + 3.0 * GELU_K * gate * gate)
        dgate_ref[...] = (dy * h * dgl).astype(BF16)
        g = _scan_up(_shift_up(a, 1, rows), dy * f["gl"])
        da = g * _shift_dn(h, 1, rows)
        iv = i * v
        dmult = g * iv
        di = g * mult * v
        dv = g * mult * i
        dlog_a = da * a - dmult * (a * a) / mult
        dr = dlog_a * (-LRU_C) * f["sp"]
        dsp = jnp.sum(dlog_a * (-LRU_C) * r, axis=0, keepdims=True)
        dlam_ref[...] = -dsp * jax.nn.sigmoid(-lam)
        dpa = dr * r * (1.0 - r)
        dpx = di * i * (1.0 - i)
        dpab = dpa.astype(BF16)
        dpxb = dpx.astype(BF16)
        dwa_ref[...] = _dot_tn(f["vb"], dpab)
        dwx_ref[...] = _dot_tn(f["vb"], dpxb)
        dba_ref[...] = jnp.sum(dpa, axis=0, keepdims=True)
        dbx_ref[...] = jnp.sum(dpx, axis=0, keepdims=True)
        dv = dv + _dot_nt(dpab, wa) + _dot_nt(dpxb, wx)
        dcb_ref[...] = jnp.sum(dv, axis=0, keepdims=True)
        du = cw[3:4] * dv
        dcw_ref[3:4, :] = jnp.sum(dv * u, axis=0, keepdims=True)
        for k in (1, 2, 3):
            du = du + cw[3 - k:4 - k] * _shift_up(dv, k, rows)
            dcw_ref[3 - k:4 - k, :] = jnp.sum(dv * _shift_dn(u, k, rows), axis=0, keepdims=True)
        du_ref[...] = du.astype(BF16)

    blk = pl.BlockSpec((t, HEAD), lambda h: (0, h))
    vec = pl.BlockSpec((1, HEAD), lambda h: (0, h))
    mat = pl.BlockSpec((None, HEAD, HEAD), lambda h: (h, 0, 0))
    return pl.pallas_call(
        body, grid=(LRU_HEADS,),
        in_specs=_lru_specs(t, j, POOL_W // HEAD, (POOL_W + LRU_W) // HEAD)
        + [pl.BlockSpec((t, HEAD), lambda h: (0, POOL_W // HEAD + h)), blk],
        out_specs=[blk, blk, pl.BlockSpec((4, HEAD), lambda h: (0, h)), vec, mat, vec, mat, vec, vec],
        out_shape=[S((t, LRU_W), BF16), S((t, LRU_W), BF16), S((4, LRU_W), F32), S((1, LRU_W), F32),
                   S((LRU_HEADS, HEAD, HEAD), F32), S((1, LRU_W), F32),
                   S((LRU_HEADS, HEAD, HEAD), F32), S((1, LRU_W), F32), S((1, LRU_W), F32)],
        compiler_params=_cp("parallel"), name="lru_bwd")(
            proj, proj, p["conv_w"], p["conv_b"], p["w_a"], p["b_a"], p["w_x"], p["b_x"], p["lam"], dycat, hstate)


def _rope(x, c, s):
    x1 = x[:, :ROPE // 2]
    x2 = x[:, ROPE // 2:]
    return jnp.concatenate([x1 * c - x2 * s, x1 * s + x2 * c], axis=-1)


def _rope_t(d, c, s):
    d1 = d[:, :ROPE // 2]
    d2 = d[:, ROPE // 2:]
    return jnp.concatenate([d1 * c + d2 * s, d2 * c - d1 * s], axis=-1)


def _rope_tables(pos2, inv_freq):
    t = pos2.shape[0]

    def body(p_ref, f_ref, c_ref, s_ref):
        ang = p_ref[...].astype(F32) * f_ref[...]
        c_ref[...] = jnp.cos(ang)
        s_ref[...] = jnp.sin(ang)

    return pl.pallas_call(body, out_shape=[S((t, ROPE // 2), F32), S((t, ROPE // 2), F32)],
                          name="rope_tables")(pos2, inv_freq)


def _down_norm(xb, wdown_g, gq3, gkv3, cos, sin, j):
    t = xb.shape[0]
    bm = _row_tile(t)

    def body(x_ref, w_ref, gq_ref, gkv_ref, c_ref, s_ref, down_ref, cq_ref, ckv_ref, kpe_ref):
        w = w_ref[...].reshape(D, ODD_IN)
        down = _dot(x_ref[...], w)
        down_ref[...] = down
        q = down[:, :Q_RANK]
        cq_ref[...] = (q * lax.rsqrt(jnp.mean(q * q, axis=-1, keepdims=True) + RMS_EPS) * gq_ref[...]).astype(BF16)
        kv = down[:, Q_RANK:Q_RANK + KV_RANK]
        ckv_ref[...] = (kv * lax.rsqrt(jnp.mean(kv * kv, axis=-1, keepdims=True) + RMS_EPS)
                        * gkv_ref[...]).astype(BF16)
        kpe_ref[...] = _rope(down[:, Q_RANK + KV_RANK:], c_ref[...], s_ref[...])

    row = lambda n: pl.BlockSpec((bm, n), lambda i: (i, 0))
    return pl.pallas_call(
        body, grid=(t // bm,),
        in_specs=[row(D), _full((N_DEV, D // N_DEV, ODD_IN)),
                  pl.BlockSpec((None, 1, Q_RANK), lambda i: (j, 0, 0)),
                  pl.BlockSpec((None, 1, KV_RANK), lambda i: (j, 0, 0)), row(ROPE // 2), row(ROPE // 2)],
        out_specs=[row(ODD_IN), row(Q_RANK), row(KV_RANK), row(ROPE)],
        out_shape=[S((t, ODD_IN), F32), S((t, Q_RANK), BF16), S((t, KV_RANK), BF16), S((t, ROPE), F32)],
        compiler_params=_cp("parallel"), name="down_norm")(xb, wdown_g, gq3, gkv3, cos, sin)


def _q_tile(t, widest):
    return min(widest, t // 2)


def _attn_probs(q, k, qs):
    s = _dot_nt(q, k) * ATT_SCALE
    tq = q.shape[0]
    rows = lax.broadcasted_iota(jnp.int32, (tq, tq), 0)
    cols = lax.broadcasted_iota(jnp.int32, (tq, tq), 1)
    last = jnp.where(jnp.right_shift(cols, CHUNK_SHIFT) <= jnp.right_shift(rows, CHUNK_SHIFT), s[:, qs:], NEG)
    s = last if qs == 0 else jnp.concatenate([s[:, :qs], last], axis=1)
    e = jnp.exp(s - jnp.max(s, axis=-1, keepdims=True))
    return e / jnp.sum(e, axis=-1, keepdims=True)


def _head_qkv(cq, ckv, kpe, c, s, wq_ref, wkv_ref):
    q = jnp.concatenate([_dot(cq, wq_ref[:, :NOPE]), _rope(_dot(cq, wq_ref[:, NOPE:]), c, s)], axis=1).astype(BF16)
    k = jnp.concatenate([_dot(ckv, wkv_ref[:, :NOPE]), kpe], axis=1).astype(BF16)
    vv = _dot(ckv, wkv_ref[:, NOPE:]).astype(BF16)
    return q, k, vv


def _attn_in_specs(t):
    return [_full((t, Q_RANK)), _full((t, KV_RANK)), _full((t, ROPE)), _full((t, ROPE // 2)), _full((t, ROPE // 2)),
            pl.BlockSpec((None, Q_RANK, NOPE + ROPE), lambda h: (h, 0, 0)),
            pl.BlockSpec((None, KV_RANK, NOPE + VDIM), lambda h: (h, 0, 0)),
            pl.BlockSpec((None, VDIM, D), lambda h: (h, 0, 0))]


def _attn_fwd(cq, ckv, kpe, cos, sin, wqb_g, wkvb_g, wo_g):
    t = cq.shape[0]
    tq = _q_tile(t, 256)

    def body(cq_ref, ckv_ref, kpe_ref, c_ref, s_ref, wq_ref, wkv_ref, wo_ref, o_ref, mix_ref):
        q, k, vv = _head_qkv(cq_ref[...], ckv_ref[...], kpe_ref[...], c_ref[...], s_ref[...], wq_ref, wkv_ref)
        for qs in range(0, t, tq):
            ke = qs + tq
            p = _attn_probs(q[qs:ke], k[:ke], qs)
            o_ref[qs:ke, :] = _dot(p.astype(BF16), vv[:ke]).astype(BF16)
        c = _dot(o_ref[...], wo_ref[...])

        @pl.when(pl.program_id(0) == 0)
        def _():
            mix_ref[...] = c

        @pl.when(pl.program_id(0) > 0)
        def _():
            mix_ref[...] += c

    return pl.pallas_call(
        body, grid=(MLA_HEADS,), in_specs=_attn_in_specs(t),
        out_specs=[pl.BlockSpec((None, t, VDIM), lambda h: (h, 0, 0)), _full((t, D))],
        out_shape=[S((MLA_HEADS, t, VDIM), BF16), S((t, D), F32)],
        compiler_params=_cp("arbitrary"), name="attn_fwd")(cq, ckv, kpe, cos, sin, wqb_g, wkvb_g, wo_g)


def _attn_bwd(cq, ckv, kpe, cos, sin, wqb_g, wkvb_g, wo_g, o, dzb):
    t = cq.shape[0]
    tq = _q_tile(t, 512)

    def body(cq_ref, ckv_ref, kpe_ref, c_ref, s_ref, wq_ref, wkv_ref, wo_ref, o_ref, dz_ref,
             dwo_ref, dwq_ref, dwkv_ref, dcq_ref, dckv_ref, dkpe_ref, dkt_s, dvt_s, dq_s):
        cqv = cq_ref[...]
        ckvv = ckv_ref[...]
        c = c_ref[...]
        s = s_ref[...]
        q, k, vv = _head_qkv(cqv, ckvv, kpe_ref[...], c, s, wq_ref, wkv_ref)
        dzv = dz_ref[...]
        dwo_ref[...] = _dot_tn(o_ref[...], dzv).astype(BF16)
        do = _dot_nt(dzv, wo_ref[...]).astype(BF16)
        dkt_s[...] = jnp.zeros_like(dkt_s)
        dvt_s[...] = jnp.zeros_like(dvt_s)
        for qs in range(0, t, tq):
            ke = qs + tq
            p = _attn_probs(q[qs:ke], k[:ke], qs)
            dp = _dot_nt(do[qs:ke], vv[:ke])
            ds = (p * (dp - jnp.sum(p * dp, axis=-1, keepdims=True)) * ATT_SCALE).astype(BF16)
            dq_s[qs:ke, :] = _dot(ds, k[:ke])
            dkt_s[0:NOPE + ROPE, 0:ke] += _dot_tn(q[qs:ke], ds)
            dvt_s[:, 0:ke] += _dot_tn(do[qs:ke], p.astype(BF16))
        dk = dkt_s[...].T
        dqn = dq_s[:, :NOPE].astype(BF16)
        dqp = _rope_t(dq_s[:, NOPE:], c, s).astype(BF16)
        dkn = dk[:, :NOPE].astype(BF16)
        dkp = dk[:, NOPE:NOPE + ROPE]
        dvv = dvt_s[...].T.astype(BF16)
        dwq_ref[:, :NOPE] = _dot_tn(cqv, dqn).astype(BF16)
        dwq_ref[:, NOPE:] = _dot_tn(cqv, dqp).astype(BF16)
        dwkv_ref[:, :NOPE] = _dot_tn(ckvv, dkn).astype(BF16)
        dwkv_ref[:, NOPE:] = _dot_tn(ckvv, dvv).astype(BF16)
        dcq = _dot_nt(dqn, wq_ref[:, :NOPE]) + _dot_nt(dqp, wq_ref[:, NOPE:])
        dckv = _dot_nt(dkn, wkv_ref[:, :NOPE]) + _dot_nt(dvv, wkv_ref[:, NOPE:])

        @pl.when(pl.program_id(0) == 0)
        def _():
            dcq_ref[...] = dcq
            dckv_ref[...] = dckv
            dkpe_ref[...] = dkp

        @pl.when(pl.program_id(0) > 0)
        def _():
            dcq_ref[...] += dcq
            dckv_ref[...] += dckv
            dkpe_ref[...] += dkp

    per_head = lambda a, b: pl.BlockSpec((None, a, b), lambda h: (h, 0, 0))
    return pl.pallas_call(
        body, grid=(MLA_HEADS,),
        in_specs=_attn_in_specs(t) + [per_head(t, VDIM), _full((t, D))],
        out_specs=[per_head(VDIM, D), per_head(Q_RANK, NOPE + ROPE), per_head(KV_RANK, NOPE + VDIM),
                   _full((t, Q_RANK)), _full((t, KV_RANK)), _full((t, ROPE))],
        out_shape=[S((MLA_HEADS, VDIM, D), BF16), S((MLA_HEADS, Q_RANK, NOPE + ROPE), BF16),
                   S((MLA_HEADS, KV_RANK, NOPE + VDIM), BF16),
                   S((t, Q_RANK), F32), S((t, KV_RANK), F32), S((t, ROPE), F32)],
        scratch_shapes=[pltpu.VMEM((2 * NOPE, t), F32), pltpu.VMEM((VDIM, t), F32),
                        pltpu.VMEM((t, NOPE + ROPE), F32)],
        compiler_params=_cp("arbitrary"), name="attn_bwd")(cq, ckv, kpe, cos, sin, wqb_g, wkvb_g, wo_g, o, dzb)


def _rms_bwd(down, dcq, dckv, dkpe, cos, sin, gq3, gkv3, j):
    t = down.shape[0]
    bm = _row_tile(t)

    def body(down_ref, dcq_ref, dckv_ref, dkpe_ref, c_ref, s_ref, gq_ref, gkv_ref, dd_ref, dgq_ref, dgkv_ref):
        @pl.when(pl.program_id(0) == 0)
        def _():
            dgq_ref[...] = jnp.zeros_like(dgq_ref)
            dgkv_ref[...] = jnp.zeros_like(dgkv_ref)

        def rms_b(x, dy, g):
            rstd = lax.rsqrt(jnp.mean(x * x, axis=-1, keepdims=True) + RMS_EPS)
            xh = x * rstd
            dyg = dy * g
            return rstd * (dyg - xh * jnp.mean(dyg * xh, axis=-1, keepdims=True)), jnp.sum(dy * xh, axis=0, keepdims=True)

        dq, dgq = rms_b(down_ref[:, :Q_RANK], dcq_ref[...], gq_ref[...])
        dkv, dgkv = rms_b(down_ref[:, Q_RANK:Q_RANK + KV_RANK], dckv_ref[...], gkv_ref[...])
        dgq_ref[...] += dgq
        dgkv_ref[...] += dgkv
        dd_ref[:, :Q_RANK] = dq.astype(BF16)
        dd_ref[:, Q_RANK:Q_RANK + KV_RANK] = dkv.astype(BF16)
        dd_ref[:, Q_RANK + KV_RANK:] = _rope_t(dkpe_ref[...], c_ref[...], s_ref[...]).astype(BF16)

    row = lambda n: pl.BlockSpec((bm, n), lambda i: (i, 0))
    return pl.pallas_call(
        body, grid=(t // bm,),
        in_specs=[row(ODD_IN), row(Q_RANK), row(KV_RANK), row(ROPE), row(ROPE // 2), row(ROPE // 2),
                  pl.BlockSpec((None, 1, Q_RANK), lambda i: (j, 0, 0)),
                  pl.BlockSpec((None, 1, KV_RANK), lambda i: (j, 0, 0))],
        out_specs=[row(ODD_IN), _full((1, Q_RANK)), _full((1, KV_RANK))],
        out_shape=[S((t, ODD_IN), BF16), S((1, Q_RANK), F32), S((1, KV_RANK), F32)],
        compiler_params=_cp("arbitrary"), name="rms_bwd")(down, dcq, dckv, dkpe, cos, sin, gq3, gkv3)


def _col_blocks(t, n, bn):
    return pl.BlockSpec((t, bn), lambda i: (0, i))


def _row_blocks(n, bm):
    return pl.BlockSpec((bm, n), lambda i: (i, 0))


def _local_step(x, pos2, tgt, small, weights_of, grads_done, start_dep=None, prefetch=None):
    t = x.shape[0]
    bm = min(512, t)
    inv_freq = (ROPE_THETA ** (-jnp.arange(0, ROPE, 2, dtype=F32) / ROPE)).reshape(1, ROPE // 2)
    cos, sin = _rope_tables(pos2, inv_freq)
    lru_p = {k: small[k] for k in ("conv_w", "conv_b", "w_a", "b_a", "w_x", "b_x", "lam")}

    saved = []
    y, yb = x, x
    for l in range(DEPTH):
        j = l // 2
        big = weights_of(l, 0, y)
        sv = dict(xb=yb, big=big)
        if l % 2 == 0:
            proj = _mm(yb, big["win_t"], mode="nt", grid=(EVEN_IN // 512,), a_spec=_full((t, D)),
                       b_spec=_row_blocks(D, 512), out_shape=S((t, EVEN_IN), F32),
                       out_spec=_col_blocks(t, EVEN_IN, 512), name="even_proj", dep=start_dep if l == 0 else None)
            y_lru, hstate = _lru_fwd(proj, lru_p, j)
            ycat = jnp.concatenate([_pool_fwd(proj, small["pool_w"], small["pool_scale"], j), y_lru], axis=1)
            big.update(weights_of(l, 1, ycat))
            z1, y1, y1b = _proj_resid_ln(y, ycat, big["wout2d"], small["ln_mix_g"], small["ln_mix_b"], l, "even_out")
            sv.update(proj=proj, ycat=ycat, hstate=hstate)
        else:
            down, cq, ckv, kpe = _down_norm(yb, big["wdown"], small["gq"], small["gkv"], cos, sin, j)
            o, mix = _attn_fwd(cq, ckv, kpe, cos, sin, big["wqb"], big["wkvb"], big["wo"])
            z1, y1, y1b = _resid_ln(y, mix, small["ln_mix_g"], small["ln_mix_b"], l, "resid_ln")
            sv.update(down=down, cq=cq, ckv=ckv, kpe=kpe, o=o)
        fetched = prefetch(l + 1, y1) if prefetch is not None and l + 1 < DEPTH else None
        z2, y, yb, act = _mlp_fwd(y1, y1b, big["w1"], big["w2"], small["ln_ffn_g"], small["ln_ffn_b"], l,
                                  dep=fetched)
        sv.update(z1=z1, y1b=y1b, z2=z2, act=act)
        saved.append(sv)

    dy, loss_tile = _loss_grad(y, tgt)

    g = {k: [None] * n for k, n in (("ln_mix_g", 4), ("ln_mix_b", 4), ("ln_ffn_g", 4), ("ln_ffn_b", 4),
                                    ("pool_w", 2), ("pool_scale", 2), ("conv_w", 2), ("conv_b", 2),
                                    ("w_a", 2), ("b_a", 2), ("w_x", 2), ("b_x", 2), ("lam", 2),
                                    ("gq", 2), ("gkv", 2))}
    dep = None
    for l in reversed(range(DEPTH)):
        j = l // 2
        sv = saved[l]
        big = sv["big"]
        dz2, dz2b, g["ln_ffn_g"][l], g["ln_ffn_b"][l] = _ln_bwd(dy, sv["z2"], small["ln_ffn_g"], l, "ln_bwd", dep=dep)
        act = sv["act"]
        dh, dff = _mlp_bwd_dh(act, dz2b, big["w1"], big["w2"])
        dw1 = _mm(sv["y1b"], dh, mode="tn", grid=(N_DEV,), a_spec=_full((t, D)),
                  b_spec=_col_blocks(t, D_FF, FF_BLK), out_shape=S((N_DEV, D, FF_BLK), BF16),
                  out_spec=pl.BlockSpec((None, D, FF_BLK), lambda i: (i, 0, 0)), name="mlp_dw1")
        dw2 = _mm(act, dz2b, mode="tn", grid=(N_DEV,), a_spec=_col_blocks(t, D_FF, FF_BLK),
                  b_spec=_full((t, D)), out_shape=S((N_DEV, FF_BLK, D), BF16),
                  out_spec=pl.BlockSpec((None, FF_BLK, D), lambda i: (i, 0, 0)), name="mlp_dw2")
        dep = grads_done(l, dict(w1=dw1, w2=dw2))
        dz1, dz1b, g["ln_mix_g"][l], g["ln_mix_b"][l] = _ln_bwd(dff, sv["z1"], small["ln_mix_g"], l, "ln_bwd_res",
                                                                 r=dz2, dep=dep)
        if l % 2 == 0:
            wout = big["wout2d"]
            dycat = _mm(dz1b, wout, mode="nt", grid=(EVEN_MIX // 512,), a_spec=_full((t, D)),
                        b_spec=_row_blocks(D, 512), out_shape=S((t, EVEN_MIX), F32),
                        out_spec=_col_blocks(t, EVEN_MIX, 512), name="even_dycat")
            dwout = _mm(sv["ycat"], dz1b, mode="tn", grid=(EVEN_MIX // 512,), a_spec=_col_blocks(t, EVEN_MIX, 512),
                        b_spec=_full((t, D)), out_shape=S((EVEN_MIX, D), BF16), out_spec=_row_blocks(D, 512),
                        name="even_dwout")
            grads_done(l, dict(wout=dwout.reshape(N_DEV, EVEN_MIX // N_DEV, D)))
            du_pool, g["pool_w"][j], g["pool_scale"][j] = _pool_bwd(sv["proj"], dycat, small["pool_w"],
                                                                   small["pool_scale"], j)
            (du_lru, du_gate, g["conv_w"][j], g["conv_b"][j], g["w_a"][j], g["b_a"][j], g["w_x"][j], g["b_x"][j],
             g["lam"][j]) = _lru_bwd(sv["proj"], dycat, sv["hstate"], lru_p, j)
            dproj = jnp.concatenate([du_pool, du_lru, du_gate], axis=1)
            dep = grads_done(l, dict(win=_even_dwin(sv["xb"], dproj)))
            dy = _mm(dproj, big["win_t"], mode="nn", grid=(t // bm,), a_spec=_row_blocks(EVEN_IN, bm),
                     b_spec=_full((EVEN_IN, D)), out_shape=S((t, D), F32), out_spec=_row_blocks(D, bm),
                     add=dz1, add_spec=_row_blocks(D, bm), add_scale=ALPHA, name="even_dx", dep=dep)
        else:
            dwo, dwqb, dwkvb, dcq, dckv, dkpe = _attn_bwd(
                sv["cq"], sv["ckv"], sv["kpe"], cos, sin, big["wqb"], big["wkvb"], big["wo"], sv["o"], dz1b)
            ddown, g["gq"][j], g["gkv"][j] = _rms_bwd(sv["down"], dcq, dckv, dkpe, cos, sin, small["gq"],
                                                     small["gkv"], j)
            dwdown = _mm(sv["xb"], ddown, mode="tn", grid=(N_DEV,), a_spec=_col_blocks(t, D, D // N_DEV),
                         b_spec=_full((t, ODD_IN)), out_shape=S((N_DEV, D // N_DEV, ODD_IN), BF16),
                         out_spec=pl.BlockSpec((None, D // N_DEV, ODD_IN), lambda i: (i, 0, 0)),
                         name="odd_dwdown")
            dep = grads_done(l, dict(wdown=dwdown, wqb=dwqb, wkvb=dwkvb, wo=dwo))
            dy = _mm(ddown, big["wdown2d"], mode="nt", grid=(t // bm,), a_spec=_row_blocks(ODD_IN, bm),
                     b_spec=_full((D, ODD_IN)), out_shape=S((t, D), F32), out_spec=_row_blocks(D, bm),
                     add=dz1, add_spec=_row_blocks(D, bm), add_scale=ALPHA, name="odd_dx", dep=dep)
    return loss_tile[0, 0], dy, g


def _mesh_place():
    x, y, c = lax.axis_index("x"), lax.axis_index("y"), lax.axis_index("c")
    return x, y, c


def _peer(place, k):
    x, y, c = place
    return (1 - x if k & 4 else x, 1 - y if k & 2 else y, 1 - c if k & 1 else c)


def _index(place):
    x, y, c = place
    return 4 * x + 2 * y + c


ANY = pl.BlockSpec(memory_space=pl.ANY)


def _make_zones(shards, me, name, dtype=BF16):
    n = len(shards)

    def body(me_ref, *refs):
        for src, dst in zip(refs[:n], refs[n:]):
            dst[...] = src[...].astype(dtype)

    grid_spec = pltpu.PrefetchScalarGridSpec(
        num_scalar_prefetch=1, grid=(1,),
        in_specs=[pl.BlockSpec(s.shape, lambda i, me_ref: (0, 0)) for s in shards],
        out_specs=[pl.BlockSpec((None,) + s.shape, lambda i, me_ref: (me_ref[0], 0, 0)) for s in shards])
    return pl.pallas_call(body, grid_spec=grid_spec, out_shape=[S((N_DEV,) + s.shape, dtype) for s in shards],
                          compiler_params=_cp("arbitrary"), name=name)(me, *shards)


def _shard_rows_tile(a):
    return max(d for d in range(16, 257, 16) if a % d == 0)


HBM = pl.BlockSpec(memory_space=pltpu.HBM)
SEM = pl.BlockSpec(memory_space=pltpu.SEMAPHORE)
DATAFLOW = pltpu.SideEffectType.DATAFLOW_SIDE_EFFECTING


def _in_hbm(a):
    return pltpu.with_memory_space_constraint(a, pltpu.HBM)


def _gather_ici_copies(place, src, land, w):
    me = _index(place)
    return [(_peer(place, k), land.at[me], land.at[me]) for k in (1, 2, 4, 6)]


def _gather_d2d_copies(place, src, land, w):
    blocks = [_index(_peer(place, k)) for k in (2, 4, 6)]
    return [(_peer(place, 1), land.at[b], land.at[b]) for b in blocks]


GATHER_ICI = (4, _gather_ici_copies)
GATHER_D2D = (3, _gather_d2d_copies)


def _scatter_plan(layers):
    def copies(place, src, land, w):
        me = _index(place)
        mine = land.at[me] if layers[w] is None else land.at[me, layers[w]]
        return [(_peer(place, k), src.at[_index(_peer(place, k))], mine) for k in range(1, N_DEV)]
    return (N_DEV - 1, copies)


def _gather_all_copies(place, src, land, w):
    me = _index(place)
    return [(_peer(place, k), land.at[me], land.at[me]) for k in range(1, N_DEV)]


GATHER_ALL = (N_DEV - 1, _gather_all_copies)


def _sum_blocks(zone, part, me):
    r = part.shape[1]

    def body(me_ref, z_ref, p_ref, o_ref):
        acc = None
        for s in range(N_DEV):
            term = jnp.where(me_ref[0] == s, p_ref[...], z_ref[s])
            acc = term if acc is None else acc + term
        o_ref[...] = acc

    grid_spec = pltpu.PrefetchScalarGridSpec(
        num_scalar_prefetch=1, grid=(1,),
        in_specs=[pl.BlockSpec((N_DEV, r, 128), lambda i, me_ref: (0, 0, 0)),
                  pl.BlockSpec((None, r, 128), lambda i, me_ref: (me_ref[0], 0, 0))],
        out_specs=pl.BlockSpec((r, 128), lambda i, me_ref: (0, 0)))
    return pl.pallas_call(body, grid_spec=grid_spec, out_shape=S((r, 128), F32),
                          compiler_params=_cp("arbitrary"), name="sum_small")(me, zone, part)


def _exchange_start(srcs, lands, plan, name, after=()):
    ns, n = len(srcs), len(lands)
    n_in = ns + n + len(after)
    per, copies = plan

    def body(*refs):
        ins, land = refs[:ns], refs[ns:ns + n]
        send, recv = refs[n_in], refs[n_in + 1]
        token = refs[-1]
        place = _mesh_place()
        for i in range(per):
            for w in range(n):
                target, src, dst = copies(place, ins[w] if ns else None, land[w], w)[i]
                pltpu.make_async_remote_copy(src_ref=src, dst_ref=dst, send_sem=send.at[w * per + i],
                                             recv_sem=recv.at[w * per + i], device_id=target, device_id_type=MESH).start()
        token[...] = jnp.zeros_like(token)

    sems = pltpu.SemaphoreType.DMA((n * per,))
    thru = [pltpu.HBM(a.shape, a.dtype) for a in list(srcs) + list(lands)]
    out = pl.pallas_call(
        body, name=name, in_specs=[HBM] * (ns + n) + [ANY] * len(after),
        out_shape=(sems, sems, *thru, S((8, 128), F32)),
        out_specs=(SEM, SEM, *([HBM] * (ns + n)), pl.BlockSpec(memory_space=pltpu.VMEM)),
        input_output_aliases={i: 2 + i for i in range(ns + n)},
        compiler_params=pltpu.CompilerParams(has_side_effects=DATAFLOW),
    )(*[_in_hbm(a) for a in list(srcs) + list(lands)], *after)
    return out[0], out[1], list(out[2:2 + ns]), list(out[2 + ns:2 + ns + n]), out[-1]


def _exchange_wait(send, recv, srcs, lands, plan, after, name):
    ns, n = len(srcs), len(lands)
    per, copies = plan
    afters = tuple(after) if isinstance(after, (tuple, list)) else (after,)

    def body(*refs):
        ins, land = refs[:ns], refs[ns:ns + n]
        send_ref, recv_ref = refs[ns + n], refs[ns + n + 1]
        place = _mesh_place()
        for i in range(per):
            for w in range(n):
                target, src, dst = copies(place, ins[w] if ns else None, land[w], w)[i]
                cp = pltpu.make_async_remote_copy(src_ref=src, dst_ref=dst, send_sem=send_ref.at[w * per + i],
                                                  recv_sem=recv_ref.at[w * per + i], device_id=target,
                                                  device_id_type=MESH)
                cp.wait_send()
                cp.wait_recv()

    thru = [pltpu.HBM(a.shape, a.dtype) for a in list(srcs) + list(lands)]
    out = pl.pallas_call(
        body, name=name, in_specs=[HBM] * (ns + n) + [SEM, SEM] + [ANY] * len(afters),
        out_shape=tuple(thru), out_specs=tuple([HBM] * (ns + n)),
        input_output_aliases={i: i for i in range(ns + n)},
        compiler_params=pltpu.CompilerParams(has_side_effects=DATAFLOW),
    )(*srcs, *lands, send, recv, *afters)
    return list(out[:ns]), list(out[ns:])


def _adamw(w, g, m, v):
    m = ADAM_B1 * m + (1.0 - ADAM_B1) * g
    v = ADAM_B2 * v + (1.0 - ADAM_B2) * (g * g)
    m_hat = m / (1.0 - ADAM_B1 ** ADAM_STEP)
    v_hat = v / (1.0 - ADAM_B2 ** ADAM_STEP)
    return -ADAM_LR * (m_hat / (jnp.sqrt(v_hat) + ADAM_EPS) + ADAM_WD * w), m, v


def _adam_big(parts, own, me, w, m, v, name):
    nl, a, b = w.shape
    ta = _shard_rows_tile(a)

    def body(me_ref, p_ref, *refs):
        own_refs, (w_ref, m_ref, v_ref, g_ref, d_ref, mo_ref, vo_ref) = refs[:nl], refs[nl:]
        layer = pl.program_id(0)
        mine = own_refs[0][...]
        for k in range(1, nl):
            mine = jnp.where(layer == k, own_refs[k][...], mine)
        g = None
        for s in range(N_DEV):
            term = jnp.where(me_ref[0] == s, mine, p_ref[s]).astype(F32)
            g = term if g is None else g + term
        g_ref[...] = g
        d_ref[...], mo_ref[...], vo_ref[...] = _adamw(w_ref[...], g, m_ref[...], v_ref[...])

    blk = pl.BlockSpec((None, ta, b), lambda l, i, me_ref: (l, i, 0))

    def own_spec(k):
        return pl.BlockSpec((None, ta, b), lambda l, i, me_ref: (me_ref[0], jnp.where(l == k, i, 0), 0))

    grid_spec = pltpu.PrefetchScalarGridSpec(
        num_scalar_prefetch=1, grid=(nl, a // ta),
        in_specs=[pl.BlockSpec((N_DEV, None, ta, b), lambda l, i, me_ref: (0, l, i, 0))]
        + [own_spec(k) for k in range(nl)] + [blk, blk, blk],
        out_specs=[blk] * 4)
    return pl.pallas_call(body, grid_spec=grid_spec, out_shape=[S(w.shape, F32)] * 4,
                          compiler_params=_cp("arbitrary", "arbitrary"), name=name)(me, parts, *own, w, m, v)


def _adam_small(gs, ws, ms, vs):
    n = len(gs)

    def body(*refs):
        ins, outs = refs[:4 * n], refs[4 * n:]
        for i in range(n):
            g_ref, w_ref, m_ref, v_ref = (ins[k * n + i] for k in range(4))
            outs[i][...] = g_ref[...]
            outs[n + i][...], outs[2 * n + i][...], outs[3 * n + i][...] = _adamw(w_ref[...], g_ref[...], m_ref[...],
                                                                                 v_ref[...])

    out = pl.pallas_call(body, out_shape=[S(g.shape, F32) for g in gs] * 4, compiler_params=_cp(),
                         name="adam_small")(*gs, *ws, *ms, *vs)
    return out[:n], out[n:2 * n], out[2 * n:3 * n], out[3 * n:]


BIG = ("even_w_in", "even_w_out", "mla_w_down", "mla_w_qb", "mla_w_kvb", "mla_w_o", "mlp_w1", "mlp_w2")
BIG_KEY = dict(even_w_in="win", even_w_out="wout", mla_w_down="wdown", mla_w_qb="wqb", mla_w_kvb="wkvb",
               mla_w_o="wo", mlp_w1="w1", mlp_w2="w2")
SMALL = (("ln_mix_g", "ln_mix_g", None), ("ln_mix_b", "ln_mix_b", None), ("ln_ffn_g", "ln_ffn_g", None),
         ("ln_ffn_b", "ln_ffn_b", None), ("pool_w", "pool_w", None), ("pool_scale", "pool_scale", None),
         ("lru_conv_w", "conv_w", 2), ("lru_conv_b", "conv_b", None), ("lru_w_a", "w_a", None),
         ("lru_b_a", "b_a", None), ("lru_w_x", "w_x", None), ("lru_b_x", "b_x", None), ("lru_lambda", "lam", None),
         ("mla_q_norm_g", "gq", 1), ("mla_kv_norm_g", "gkv", 1))
WEIGHTS = ("ln_mix_g", "ln_mix_b", "ln_ffn_g", "ln_ffn_b", "even_w_in", "pool_w", "pool_scale", "lru_conv_w",
           "lru_conv_b", "lru_w_a", "lru_b_a", "lru_w_x", "lru_b_x", "lru_lambda", "even_w_out", "mla_w_down",
           "mla_q_norm_g", "mla_kv_norm_g", "mla_w_qb", "mla_w_kvb", "mla_w_o", "mlp_w1", "mlp_w2")


def _layer_weights(l):
    j = l // 2
    if l % 2 == 0:
        mixer = [("win", "even_w_in", j), ("wout", "even_w_out", j)]
    else:
        mixer = [("wdown", "mla_w_down", j), ("wqb", "mla_w_qb", j), ("wkvb", "mla_w_kvb", j), ("wo", "mla_w_o", j)]
    return mixer + [("w1", "mlp_w1", l), ("w2", "mlp_w2", l)]


def _pack(arrays, multiple):
    flat = jnp.concatenate([a.reshape(-1) for a in arrays])
    pad = (-flat.shape[0]) % multiple
    return jnp.pad(flat, (0, pad))


def _unpack(flat, shapes):
    out, at = [], 0
    for shp in shapes:
        n = 1
        for s in shp:
            n *= s
        out.append(flat[at:at + n].reshape(shp))
        at += n
    return out


def _global_shape(local_shape, axis):
    if axis is None:
        return tuple(local_shape)
    return tuple(s * N_DEV if i == axis else s for i, s in enumerate(local_shape))


def _step(x, positions, tgt, w, m, v):
    t = x.shape[1]
    me = _index(_mesh_place())

    chunk = N_DEV * 8 * 128
    me_arr = me.astype(jnp.int32).reshape(1)

    lanes = lambda a: jnp.pad(a, ((0, 0), (0, 128 - a.shape[1])))
    mine_packed = jnp.concatenate([w["lru_conv_w"].reshape(8, HEAD), lanes(w["mla_q_norm_g"]),
                                   lanes(w["mla_kv_norm_g"]), jnp.zeros((4, 128), F32)])
    g_send, g_recv, _, g_land, token = _exchange_start([], _make_zones([mine_packed], me_arr, "zones_small", F32),
                                                       GATHER_ALL, "small_params_start")

    def keys_of(l, part):
        keys = [key for key, _, _ in _layer_weights(l)]
        if l == 0:
            return keys[:1] if part == 0 else keys[1:]
        return keys if part == 0 else []

    shard_of = {(l, key): (w[name][i].T if key == "win" else w[name][i])
                for l in range(DEPTH) for key, name, i in _layer_weights(l)}
    flights, after = {}, (token,)
    for l in range(DEPTH):
        for part in (0, 1):
            if keys_of(l, part):
                zones = _make_zones([shard_of[l, key] for key in keys_of(l, part)], me_arr, "zones_%d_%d" % (l, part))
                send, recv, _, lands, token = _exchange_start([], zones, GATHER_ICI, "gather_start_%d_%d" % (l, part),
                                                              after=after)
                flights[l, part] = (send, recv, [], lands)
                after = (token,)

    _, g_land = _exchange_wait(g_send, g_recv, [], g_land, GATHER_ALL, token, "small_params_wait")
    rows_first = g_land[0].transpose(1, 0, 2)
    q_shard, kv_shard = w["mla_q_norm_g"].shape[1], w["mla_kv_norm_g"].shape[1]
    full = dict(lru_conv_w=rows_first[:8].reshape(2, 4, LRU_W),
                mla_q_norm_g=rows_first[8:10, :, :q_shard].reshape(2, Q_RANK),
                mla_kv_norm_g=rows_first[10:12, :, :kv_shard].reshape(2, KV_RANK))

    passing = {}

    def pass_on(l, part, after):
        tag = "%d_%d" % (l, part)
        _, lands = _exchange_wait(*flights[l, part], GATHER_ICI, after, "gather_wait_" + tag)
        send, recv, _, lands, token = _exchange_start([], lands, GATHER_D2D, "gather_pass_" + tag)
        passing[l, part] = (send, recv, [], lands)
        return token

    def early_pass(l, after):
        return pass_on(l, 0, after) if l >= 2 else None

    def weights_of(l, part, after):
        keys = keys_of(l, part)
        if keys:
            if (l, part) not in passing:
                pass_on(l, part, after)
            _, arrays = _exchange_wait(*passing[l, part], GATHER_D2D, after, "gather_pass_wait_%d_%d" % (l, part))
        big = dict(zip(keys, arrays)) if keys else {}
        if "win" in big:
            big["win_t"] = big["win"].reshape(EVEN_IN, D)
        if "wout" in big:
            big["wout2d"] = big["wout"].reshape(EVEN_MIX, D)
        if "wdown" in big:
            big["wdown2d"] = big["wdown"].reshape(D, ODD_IN)
        return big

    zone = {name: lax.empty((N_DEV,) + w[name].shape, BF16) for name in BIG}
    name_of = {key: name for name, key in BIG_KEY.items()}
    sent, last_token = [], [None]

    def grads_done(l, grads):
        keys = list(grads)
        index = {key: i for key, _, i in _layer_weights(l)}
        layers = [index[key] for key in keys]
        send, recv, srcs, lands, tok = _exchange_start([grads[k] for k in keys], [zone[name_of[k]] for k in keys],
                                                       _scatter_plan(layers), "scatter_start_%d_%s" % (l, keys[0]))
        for k, land in zip(keys, lands):
            zone[name_of[k]] = land
        sent.append((send, recv, srcs, keys, layers))
        last_token[0] = tok
        return tok

    row3 = lambda a: a.reshape(a.shape[0], 1, a.shape[1])
    small = dict(ln_mix_g=row3(w["ln_mix_g"]), ln_mix_b=row3(w["ln_mix_b"]), ln_ffn_g=row3(w["ln_ffn_g"]),
                 ln_ffn_b=row3(w["ln_ffn_b"]), pool_w=w["pool_w"], pool_scale=row3(w["pool_scale"]),
                 conv_w=full["lru_conv_w"], conv_b=row3(w["lru_conv_b"]), w_a=w["lru_w_a"], b_a=row3(w["lru_b_a"]),
                 w_x=w["lru_w_x"], b_x=row3(w["lru_b_x"]), lam=row3(w["lru_lambda"]),
                 gq=row3(full["mla_q_norm_g"]), gkv=row3(full["mla_kv_norm_g"]))

    loss_part, grad_x, g = _local_step(x[0], positions.reshape(t, 1), tgt[0], small, weights_of, grads_done,
                                       start_dep=token, prefetch=early_pass)

    own = {name: [None] * w[name].shape[0] for name in BIG}
    me_arr = me.astype(jnp.int32).reshape(1)
    out = {}
    local_g = [jnp.stack(g[key]).reshape(_global_shape(w[name].shape, axis)) for name, key, axis in SMALL]
    local_g.append(loss_part.reshape(1))
    part = _pack(local_g, chunk).reshape(N_DEV, -1, 128)
    small_plan = _scatter_plan([None])
    s_send, s_recv, s_src, s_land, after = _exchange_start([part], [lax.empty(part.shape, F32)], small_plan,
                                                           "small_scatter_start", after=(last_token[0],))
    for n_flight, (send, recv, srcs, keys, layers) in enumerate(sent):
        if n_flight == len(sent) - 1:
            for name in BIG:
                if BIG_KEY[name] not in keys:
                    out[name] = _adam_big(zone[name], own[name], me_arr, w[name], m[name], v[name], "adam_" + name)
            s_src, s_land = _exchange_wait(s_send, s_recv, s_src, s_land, small_plan,
                                           [grad_x] + [o[0] for o in out.values()], "small_scatter_wait")
            chunk_sum = _sum_blocks(s_land[0], s_src[0], me_arr)
            r_zone = lax.dynamic_update_slice_in_dim(lax.empty(part.shape, F32), chunk_sum[None], me, 0)
            r_send, r_recv, _, r_land, after = _exchange_start([], [r_zone], GATHER_ALL, "small_gather_start")
        srcs, lands = _exchange_wait(send, recv, srcs, [zone[name_of[k]] for k in keys], _scatter_plan(layers),
                                     after, "scatter_wait_%d" % n_flight)
        for k, land, src, layer in zip(keys, lands, srcs, layers):
            zone[name_of[k]] = land
            own[name_of[k]][layer] = src
        after = lands[0]
    for name in BIG:
        if name not in out:
            out[name] = _adam_big(zone[name], own[name], me_arr, w[name], m[name], v[name], "adam_" + name)

    _, reduced = _exchange_wait(r_send, r_recv, [], r_land, GATHER_ALL, [out[name][0] for name in BIG],
                                "small_gather_wait")
    reduced = _unpack(reduced[0].reshape(-1), [a.shape for a in local_g])
    loss = reduced[-1][0]
    mine = [a if axis is None else lax.dynamic_slice_in_dim(a, me * w[name].shape[axis], w[name].shape[axis], axis)
            for a, (name, _, axis) in zip(reduced, SMALL)]
    names = [name for name, _, _ in SMALL]
    as_2d = lambda a: a.reshape(-1, a.shape[-1])
    new = _adam_small([as_2d(a) for a in mine], *([as_2d(src[name]) for name in names] for src in (w, m, v)))
    for i, name in enumerate(names):
        out[name] = tuple(part[i].reshape(w[name].shape) for part in new)

    return (loss, grad_x[None]) + tuple(out[name][i] for i in range(4) for name in WEIGHTS)


def kernel(x, positions, ln_mix_g, ln_mix_b, ln_ffn_g, ln_ffn_b, even_w_in, pool_w, pool_scale, lru_conv_w, lru_conv_b, lru_w_a, lru_b_a, lru_w_x, lru_b_x, lru_lambda, even_w_out, mla_w_down, mla_q_norm_g, mla_kv_norm_g, mla_w_qb, mla_w_kvb, mla_w_o, mlp_w1, mlp_w2, loss_target, m_ln_mix_g, m_ln_mix_b, m_ln_ffn_g, m_ln_ffn_b, m_even_w_in, m_pool_w, m_pool_scale, m_lru_conv_w, m_lru_conv_b, m_lru_w_a, m_lru_b_a, m_lru_w_x, m_lru_b_x, m_lru_lambda, m_even_w_out, m_mla_w_down, m_mla_q_norm_g, m_mla_kv_norm_g, m_mla_w_qb, m_mla_w_kvb, m_mla_w_o, m_mlp_w1, m_mlp_w2, v_ln_mix_g, v_ln_mix_b, v_ln_ffn_g, v_ln_ffn_b, v_even_w_in, v_pool_w, v_pool_scale, v_lru_conv_w, v_lru_conv_b, v_lru_w_a, v_lru_b_a, v_lru_w_x, v_lru_b_x, v_lru_lambda, v_even_w_out, v_mla_w_down, v_mla_q_norm_g, v_mla_kv_norm_g, v_mla_w_qb, v_mla_w_kvb, v_mla_w_o, v_mlp_w1, v_mlp_w2):
    w = dict(zip(WEIGHTS, (ln_mix_g, ln_mix_b, ln_ffn_g, ln_ffn_b, even_w_in, pool_w, pool_scale, lru_conv_w,
                           lru_conv_b, lru_w_a, lru_b_a, lru_w_x, lru_b_x, lru_lambda, even_w_out, mla_w_down,
                           mla_q_norm_g, mla_kv_norm_g, mla_w_qb, mla_w_kvb, mla_w_o, mlp_w1, mlp_w2)))
    m = dict(zip(WEIGHTS, (m_ln_mix_g, m_ln_mix_b, m_ln_ffn_g, m_ln_ffn_b, m_even_w_in, m_pool_w, m_pool_scale,
                           m_lru_conv_w, m_lru_conv_b, m_lru_w_a, m_lru_b_a, m_lru_w_x, m_lru_b_x, m_lru_lambda,
                           m_even_w_out, m_mla_w_down, m_mla_q_norm_g, m_mla_kv_norm_g, m_mla_w_qb, m_mla_w_kvb,
                           m_mla_w_o, m_mlp_w1, m_mlp_w2)))
    v = dict(zip(WEIGHTS, (v_ln_mix_g, v_ln_mix_b, v_ln_ffn_g, v_ln_ffn_b, v_even_w_in, v_pool_w, v_pool_scale,
                           v_lru_conv_w, v_lru_conv_b, v_lru_w_a, v_lru_b_a, v_lru_w_x, v_lru_b_x, v_lru_lambda,
                           v_even_w_out, v_mla_w_down, v_mla_q_norm_g, v_mla_kv_norm_g, v_mla_w_qb, v_mla_w_kvb,
                           v_mla_w_o, v_mlp_w1, v_mlp_w2)))
    return _step(x, positions, loss_target, w, m, v)
```
